```python
import jax, jax.numpy as jnp
from jax import lax
import numpy as np

D_MODEL = 1024
BATCH = 8
SEQ = 8192
DEPTH = 2

HEAD_DIM = 64
SWA_Q_HEADS = 8
SWA_KV_HEADS = 2
SWA_WINDOW = 128
FOX_HEADS = 8
BLOCK = 128
D_FF = 2816
CONV_WIDTH = 3
LN_EPS = 1e-5
NEG_INF = -1e30
FOX_GATE_BIAS_INIT = 3.0
DEEPNORM_ALPHA = (2 * DEPTH) ** 0.25
DEEPNORM_BETA = (8 * DEPTH) ** -0.25
SWA_Q = SWA_Q_HEADS * HEAD_DIM
SWA_KV = SWA_KV_HEADS * HEAD_DIM
FOX_W = FOX_HEADS * HEAD_DIM
SPLIT_SIZES = (SWA_Q, SWA_KV, SWA_KV, FOX_W, FOX_W, FOX_W, FOX_HEADS, D_MODEL, D_MODEL)
N_IN = SWA_Q + 2 * SWA_KV + 3 * FOX_W + FOX_HEADS + 2 * D_MODEL

kernel_name = "hybrid_swa_sink_fox_gated_convffn_deepnorm"


def layer_norm(x, g, b):
    xf = x.astype(jnp.float32)
    mu = jnp.mean(xf, axis=-1, keepdims=True)
    var = jnp.mean(jnp.square(xf - mu), axis=-1, keepdims=True)
    y = (xf - mu) * lax.rsqrt(var + LN_EPS) * g.astype(jnp.float32) + b.astype(jnp.float32)
    return y.astype(x.dtype)


def alibi_slopes(n_heads):
    return jnp.asarray(2.0 ** (-8.0 * np.arange(1, n_heads + 1) / n_heads), dtype=jnp.float32)


def swa_sink_attention(q, k, v, sinks):
    B, S, Hq, d = q.shape
    Hkv = k.shape[2]
    G = Hq // Hkv
    nb = S // BLOCK
    qb = q.reshape(B, nb, BLOCK, Hkv, G, d)
    kb = k.reshape(B, nb, BLOCK, Hkv, d)
    vb = v.reshape(B, nb, BLOCK, Hkv, d)

    def with_prev(t):
        prev = jnp.pad(t, ((0, 0), (1, 0), (0, 0), (0, 0), (0, 0)))[:, :-1]
        return jnp.concatenate([prev, t], axis=2)

    kw, vw = with_prev(kb), with_prev(vb)
    scores = jnp.einsum('bnqhgd,bnkhd->bnhgqk', qb, kw).astype(jnp.float32) * (d ** -0.5)
    q_pos = jnp.arange(BLOCK)[:, None] + BLOCK
    k_pos = jnp.arange(2 * BLOCK)[None, :]
    dist = q_pos - k_pos
    valid = (dist >= 0) & (dist < SWA_WINDOW)
    blk = jnp.arange(nb)[:, None, None]
    valid = valid[None] & ((k_pos[None] >= BLOCK) | (blk > 0))
    slopes = alibi_slopes(Hq).reshape(Hkv, G)
    alibi = -slopes[:, :, None, None] * dist.astype(jnp.float32)[None, None]
    scores = scores + alibi[None, None]
    scores = jnp.where(valid[None, :, None, None], scores, NEG_INF)
    sink = jnp.broadcast_to(sinks.astype(jnp.float32).reshape(Hkv, G)[None, None, :, :, None, None],
                            scores.shape[:-1] + (1,))
    probs = jax.nn.softmax(jnp.concatenate([scores, sink], axis=-1), axis=-1)[..., :-1]
    out = jnp.einsum('bnhgqk,bnkhd->bnqhgd', probs.astype(v.dtype), vw)
    return out.reshape(B, S, Hq * d)


def forgetting_attention(q, k, v, log_f):
    B, S, H, d = q.shape
    nb = S // BLOCK
    c = jnp.cumsum(log_f, axis=1)
    c_k = jnp.transpose(c, (0, 2, 1))
    qb = jnp.transpose(q.reshape(B, nb, BLOCK, H, d), (1, 0, 2, 3, 4))
    cq = jnp.transpose(c.reshape(B, nb, BLOCK, H), (1, 0, 3, 2))
    k_pos = jnp.arange(S)
    scale = d ** -0.5

    def one_block(args):
        q_blk, c_blk, i = args
        s = jnp.einsum('bqhd,bkhd->bhqk', q_blk, k).astype(jnp.float32) * scale
        s = s + c_blk[..., None] - c_k[:, :, None, :]
        q_pos = i * BLOCK + jnp.arange(BLOCK)
        mask = k_pos[None, :] <= q_pos[:, None]
        s = jnp.where(mask[None, None], s, NEG_INF)
        p = jax.nn.softmax(s, axis=-1)
        return jnp.einsum('bhqk,bkhd->bqhd', p.astype(v.dtype), v)

    out = lax.map(one_block, (qb, cq, jnp.arange(nb)))
    return jnp.transpose(out, (1, 0, 2, 3, 4)).reshape(B, S, H * d)


def token_mixer(h, w_in, b_in, sinks, w_proj_a, w_proj_b, w_out):
    B, S, _ = h.shape
    z = h @ w_in + b_in
    idx = [int(i) for i in np.cumsum(SPLIT_SIZES)[:-1]]
    q_a, k_a, v_a, q_b, k_b, v_b, f_logit, g_a, g_b = jnp.split(z, idx, axis=-1)
    y_a = swa_sink_attention(q_a.reshape(B, S, SWA_Q_HEADS, HEAD_DIM),
                             k_a.reshape(B, S, SWA_KV_HEADS, HEAD_DIM),
                             v_a.reshape(B, S, SWA_KV_HEADS, HEAD_DIM), sinks) @ w_proj_a
    log_f = jax.nn.log_sigmoid(f_logit.astype(jnp.float32))
    y_b = forgetting_attention(q_b.reshape(B, S, FOX_HEADS, HEAD_DIM),
                               k_b.reshape(B, S, FOX_HEADS, HEAD_DIM),
                               v_b.reshape(B, S, FOX_HEADS, HEAD_DIM), log_f) @ w_proj_b
    merged = jax.nn.sigmoid(g_a) * y_a + jax.nn.sigmoid(g_b) * y_b
    return merged @ w_out


def conv_gated_ffn(h, w_ffn_in, conv_w, conv_b, w_ffn_out):
    S = h.shape[1]
    gate, up = jnp.split(h @ w_ffn_in, 2, axis=-1)
    gp = jnp.pad(gate, ((0, 0), (CONV_WIDTH - 1, 0), (0, 0)))
    conv = conv_b
    for j in range(CONV_WIDTH):
        conv = conv + conv_w[j] * gp[:, j:j + S]
    return (jax.nn.silu(conv) * up) @ w_ffn_out


def _fwd_setup_inputs(seed: int = 0) -> dict:
    key = jax.random.key(seed)
    ks = jax.random.split(key, 16)
    f32 = jnp.float32
    L = DEPTH
    beta = DEEPNORM_BETA
    x = jax.random.normal(ks[0], (BATCH, SEQ, D_MODEL), f32)
    ln_mix_g = 1.0 + 0.02 * jax.random.normal(ks[1], (L, D_MODEL), f32)
    ln_mix_b = 0.02 * jax.random.normal(ks[2], (L, D_MODEL), f32)
    col_scale = np.ones((N_IN,), np.float32)
    va0 = SWA_Q + SWA_KV
    col_scale[va0:va0 + SWA_KV] = beta
    vb0 = SWA_Q + 2 * SWA_KV + 2 * FOX_W
    col_scale[vb0:vb0 + FOX_W] = beta
    w_in = jax.random.normal(ks[3], (L, D_MODEL, N_IN), f32) * (D_MODEL ** -0.5) * jnp.asarray(col_scale)
    bias_off = np.zeros((N_IN,), np.float32)
    f0 = SWA_Q + 2 * SWA_KV + 3 * FOX_W
    bias_off[f0:f0 + FOX_HEADS] = FOX_GATE_BIAS_INIT
    b_in = 0.02 * jax.random.normal(ks[4], (L, N_IN), f32) + jnp.asarray(bias_off)
    attn_sinks = 0.5 * jax.random.normal(ks[5], (L, SWA_Q_HEADS), f32)
    w_proj_a = jax.random.normal(ks[6], (L, SWA_Q, D_MODEL), f32) * (SWA_Q ** -0.5) * beta
    w_proj_b = jax.random.normal(ks[7], (L, FOX_W, D_MODEL), f32) * (FOX_W ** -0.5) * beta
    w_out = jax.random.normal(ks[8], (L, D_MODEL, D_MODEL), f32) * (D_MODEL ** -0.5) * beta
    ln_ffn_g = 1.0 + 0.02 * jax.random.normal(ks[9], (L, D_MODEL), f32)
    ln_ffn_b = 0.02 * jax.random.normal(ks[10], (L, D_MODEL), f32)
    w_ffn_in = jax.random.normal(ks[11], (L, D_MODEL, 2 * D_FF), f32) * (D_MODEL ** -0.5) * beta
    conv_w = jax.random.normal(ks[12], (L, CONV_WIDTH, D_FF), f32) * (CONV_WIDTH ** -0.5)
    conv_b = 0.02 * jax.random.normal(ks[13], (L, D_FF), f32)
    w_ffn_out = jax.random.normal(ks[14], (L, D_FF, D_MODEL), f32) * (D_FF ** -0.5) * beta
    return {"x": x, "ln_mix_g": ln_mix_g, "ln_mix_b": ln_mix_b, "w_in": w_in, "b_in": b_in,
            "attn_sinks": attn_sinks, "w_proj_a": w_proj_a, "w_proj_b": w_proj_b, "w_out": w_out,
            "ln_ffn_g": ln_ffn_g, "ln_ffn_b": ln_ffn_b, "w_ffn_in": w_ffn_in, "conv_w": conv_w,
            "conv_b": conv_b, "w_ffn_out": w_ffn_out}


def _fwd_reference(x, ln_mix_g, ln_mix_b, w_in, b_in, attn_sinks, w_proj_a, w_proj_b, w_out,
              ln_ffn_g, ln_ffn_b, w_ffn_in, conv_w, conv_b, w_ffn_out):
    h = x
    for l in range(DEPTH):
        mix = token_mixer(h, w_in[l], b_in[l], attn_sinks[l], w_proj_a[l], w_proj_b[l], w_out[l])
        h = layer_norm(DEEPNORM_ALPHA * h + mix, ln_mix_g[l], ln_mix_b[l])
        ffn = conv_gated_ffn(h, w_ffn_in[l], conv_w[l], conv_b[l], w_ffn_out[l])
        h = layer_norm(DEEPNORM_ALPHA * h + ffn, ln_ffn_g[l], ln_ffn_b[l])
    return h


import jax as _jax
import jax.numpy as _jnp

TWIN_FORMAT = 'train_step'
FWD_PARAMS = ['x', 'ln_mix_g', 'ln_mix_b', 'w_in', 'b_in', 'attn_sinks', 'w_proj_a', 'w_proj_b', 'w_out', 'ln_ffn_g', 'ln_ffn_b', 'w_ffn_in', 'conv_w', 'conv_b', 'w_ffn_out']
TWIN_WEIGHTS = ['ln_mix_g', 'ln_mix_b', 'w_in', 'b_in', 'attn_sinks', 'w_proj_a', 'w_proj_b', 'w_out', 'ln_ffn_g', 'ln_ffn_b', 'w_ffn_in', 'conv_w', 'conv_b', 'w_ffn_out']
TWIN_DIFF_INPUT = 'x'
TWIN_INPUTS = ['x', 'ln_mix_g', 'ln_mix_b', 'w_in', 'b_in', 'attn_sinks', 'w_proj_a', 'w_proj_b', 'w_out', 'ln_ffn_g', 'ln_ffn_b', 'w_ffn_in', 'conv_w', 'conv_b', 'w_ffn_out', 'loss_target', 'm_ln_mix_g', 'm_ln_mix_b', 'm_w_in', 'm_b_in', 'm_attn_sinks', 'm_w_proj_a', 'm_w_proj_b', 'm_w_out', 'm_ln_ffn_g', 'm_ln_ffn_b', 'm_w_ffn_in', 'm_conv_w', 'm_conv_b', 'm_w_ffn_out', 'v_ln_mix_g', 'v_ln_mix_b', 'v_w_in', 'v_b_in', 'v_attn_sinks', 'v_w_proj_a', 'v_w_proj_b', 'v_w_out', 'v_ln_ffn_g', 'v_ln_ffn_b', 'v_w_ffn_in', 'v_conv_w', 'v_conv_b', 'v_w_ffn_out']
TWIN_OUTPUTS = ['loss', 'grad_x', 'grad_ln_mix_g', 'grad_ln_mix_b', 'grad_w_in', 'grad_b_in', 'grad_attn_sinks', 'grad_w_proj_a', 'grad_w_proj_b', 'grad_w_out', 'grad_ln_ffn_g', 'grad_ln_ffn_b', 'grad_w_ffn_in', 'grad_conv_w', 'grad_conv_b', 'grad_w_ffn_out', 'delta_ln_mix_g', 'delta_ln_mix_b', 'delta_w_in', 'delta_b_in', 'delta_attn_sinks', 'delta_w_proj_a', 'delta_w_proj_b', 'delta_w_out', 'delta_ln_ffn_g', 'delta_ln_ffn_b', 'delta_w_ffn_in', 'delta_conv_w', 'delta_conv_b', 'delta_w_ffn_out', 'new_m_ln_mix_g', 'new_m_ln_mix_b', 'new_m_w_in', 'new_m_b_in', 'new_m_attn_sinks', 'new_m_w_proj_a', 'new_m_w_proj_b', 'new_m_w_out', 'new_m_ln_ffn_g', 'new_m_ln_ffn_b', 'new_m_w_ffn_in', 'new_m_conv_w', 'new_m_conv_b', 'new_m_w_ffn_out', 'new_v_ln_mix_g', 'new_v_ln_mix_b', 'new_v_w_in', 'new_v_b_in', 'new_v_attn_sinks', 'new_v_w_proj_a', 'new_v_w_proj_b', 'new_v_w_out', 'new_v_ln_ffn_g', 'new_v_ln_ffn_b', 'new_v_w_ffn_in', 'new_v_conv_w', 'new_v_conv_b', 'new_v_w_ffn_out']
TWIN_LEAF_KINDS = {'loss': 'loss', 'grad_x': 'grad_x', 'grad_ln_mix_g': 'grad_w', 'grad_ln_mix_b': 'grad_w', 'grad_w_in': 'grad_w', 'grad_b_in': 'grad_w', 'grad_attn_sinks': 'grad_w', 'grad_w_proj_a': 'grad_w', 'grad_w_proj_b': 'grad_w', 'grad_w_out': 'grad_w', 'grad_ln_ffn_g': 'grad_w', 'grad_ln_ffn_b': 'grad_w', 'grad_w_ffn_in': 'grad_w', 'grad_conv_w': 'grad_w', 'grad_conv_b': 'grad_w', 'grad_w_ffn_out': 'grad_w', 'delta_ln_mix_g': 'delta_w', 'delta_ln_mix_b': 'delta_w', 'delta_w_in': 'delta_w', 'delta_b_in': 'delta_w', 'delta_attn_sinks': 'delta_w', 'delta_w_proj_a': 'delta_w', 'delta_w_proj_b': 'delta_w', 'delta_w_out': 'delta_w', 'delta_ln_ffn_g': 'delta_w', 'delta_ln_ffn_b': 'delta_w', 'delta_w_ffn_in': 'delta_w', 'delta_conv_w': 'delta_w', 'delta_conv_b': 'delta_w', 'delta_w_ffn_out': 'delta_w', 'new_m_ln_mix_g': 'new_m', 'new_m_ln_mix_b': 'new_m', 'new_m_w_in': 'new_m', 'new_m_b_in': 'new_m', 'new_m_attn_sinks': 'new_m', 'new_m_w_proj_a': 'new_m', 'new_m_w_proj_b': 'new_m', 'new_m_w_out': 'new_m', 'new_m_ln_ffn_g': 'new_m', 'new_m_ln_ffn_b': 'new_m', 'new_m_w_ffn_in': 'new_m', 'new_m_conv_w': 'new_m', 'new_m_conv_b': 'new_m', 'new_m_w_ffn_out': 'new_m', 'new_v_ln_mix_g': 'new_v', 'new_v_ln_mix_b': 'new_v', 'new_v_w_in': 'new_v', 'new_v_b_in': 'new_v', 'new_v_attn_sinks': 'new_v', 'new_v_w_proj_a': 'new_v', 'new_v_w_proj_b': 'new_v', 'new_v_w_out': 'new_v', 'new_v_ln_ffn_g': 'new_v', 'new_v_ln_ffn_b': 'new_v', 'new_v_w_ffn_in': 'new_v', 'new_v_conv_w': 'new_v', 'new_v_conv_b': 'new_v', 'new_v_w_ffn_out': 'new_v'}


def _forward(args):
    return _fwd_reference(*[args[k] for k in FWD_PARAMS])


def _output_shape():
    out = _jax.eval_shape(lambda: _forward(_fwd_setup_inputs(0)))
    return out.shape, out.dtype

N_MICROBATCH = 1
ADAM_LR = 0.001
ADAM_B1 = 0.9
ADAM_B2 = 0.999
ADAM_EPS = 1e-08
ADAM_WD = 0.01
ADAM_STEP = 10
PER_EXAMPLE_BATCH_AXIS = {'x': 0, 'loss_target': 0}
SHARED_INPUTS = []
_WEIGHT_DTYPES = {'ln_mix_g': _jnp.float32, 'ln_mix_b': _jnp.float32, 'w_in': _jnp.float32, 'b_in': _jnp.float32, 'attn_sinks': _jnp.float32, 'w_proj_a': _jnp.float32, 'w_proj_b': _jnp.float32, 'w_out': _jnp.float32, 'ln_ffn_g': _jnp.float32, 'ln_ffn_b': _jnp.float32, 'w_ffn_in': _jnp.float32, 'conv_w': _jnp.float32, 'conv_b': _jnp.float32, 'w_ffn_out': _jnp.float32}
MOMENT_SCALE = {'ln_mix_g': 1.990167e+00, 'ln_mix_b': 7.005029e-01, 'w_in': 5.868680e-03, 'b_in': 4.563169e-02, 'attn_sinks': 3.401744e-02, 'w_proj_a': 7.220739e-03, 'w_proj_b': 7.633830e-03, 'w_out': 1.053320e-02, 'ln_ffn_g': 4.527205e+01, 'ln_ffn_b': 1.292546e+00, 'w_ffn_in': 1.487930e-02, 'conv_w': 7.517459e-03, 'conv_b': 1.474793e-02, 'w_ffn_out': 2.430731e-02}


def _to_microbatches(a, axis):
    t = _jnp.moveaxis(a, axis, 0)
    t = t.reshape((N_MICROBATCH, t.shape[0] // N_MICROBATCH) + t.shape[1:])
    return _jnp.moveaxis(t, 1, axis + 1)


def setup_inputs(seed: int = 0) -> dict:
    inp = _fwd_setup_inputs(seed)
    key = _jax.random.fold_in(_jax.random.key(seed), 7919)
    shape, _ = _output_shape()
    out = dict(inp)
    out["loss_target"] = _jax.random.normal(_jax.random.fold_in(key, 0), shape, _jnp.float32)
    for i, name in enumerate(TWIN_WEIGHTS):
        w = inp[name].astype(_jnp.float32)
        if MOMENT_SCALE is None:
            s = _jnp.sqrt(_jnp.mean(_jnp.square(w)) + 1e-30)
        else:
            s = MOMENT_SCALE[name]
        km, kv = _jax.random.split(_jax.random.fold_in(key, i + 1))
        out[name] = w
        out["m_" + name] = s * _jax.random.normal(km, w.shape, _jnp.float32)
        out["v_" + name] = (s * s) * _jax.random.uniform(kv, w.shape, _jnp.float32, 0.5, 1.5)
    if N_MICROBATCH > 1:
        for name, axis in PER_EXAMPLE_BATCH_AXIS.items():
            out[name] = _to_microbatches(out[name], axis)
    return {'x': out['x'], 'ln_mix_g': out['ln_mix_g'], 'ln_mix_b': out['ln_mix_b'], 'w_in': out['w_in'], 'b_in': out['b_in'], 'attn_sinks': out['attn_sinks'], 'w_proj_a': out['w_proj_a'], 'w_proj_b': out['w_proj_b'], 'w_out': out['w_out'], 'ln_ffn_g': out['ln_ffn_g'], 'ln_ffn_b': out['ln_ffn_b'], 'w_ffn_in': out['w_ffn_in'], 'conv_w': out['conv_w'], 'conv_b': out['conv_b'], 'w_ffn_out': out['w_ffn_out'], 'loss_target': out['loss_target'], 'm_ln_mix_g': out['m_ln_mix_g'], 'm_ln_mix_b': out['m_ln_mix_b'], 'm_w_in': out['m_w_in'], 'm_b_in': out['m_b_in'], 'm_attn_sinks': out['m_attn_sinks'], 'm_w_proj_a': out['m_w_proj_a'], 'm_w_proj_b': out['m_w_proj_b'], 'm_w_out': out['m_w_out'], 'm_ln_ffn_g': out['m_ln_ffn_g'], 'm_ln_ffn_b': out['m_ln_ffn_b'], 'm_w_ffn_in': out['m_w_ffn_in'], 'm_conv_w': out['m_conv_w'], 'm_conv_b': out['m_conv_b'], 'm_w_ffn_out': out['m_w_ffn_out'], 'v_ln_mix_g': out['v_ln_mix_g'], 'v_ln_mix_b': out['v_ln_mix_b'], 'v_w_in': out['v_w_in'], 'v_b_in': out['v_b_in'], 'v_attn_sinks': out['v_attn_sinks'], 'v_w_proj_a': out['v_w_proj_a'], 'v_w_proj_b': out['v_w_proj_b'], 'v_w_out': out['v_w_out'], 'v_ln_ffn_g': out['v_ln_ffn_g'], 'v_ln_ffn_b': out['v_ln_ffn_b'], 'v_w_ffn_in': out['v_w_ffn_in'], 'v_conv_w': out['v_conv_w'], 'v_conv_b': out['v_conv_b'], 'v_w_ffn_out': out['v_w_ffn_out']}


def _loss(weights, diff, rest, loss_target):
    with _jax.named_scope("forward"):
        args = {**rest, TWIN_DIFF_INPUT: diff, **{k: w.astype(_WEIGHT_DTYPES[k]) for k, w in weights.items()}}
        y = _forward(args)
    with _jax.named_scope("loss_head"):
        err = _jnp.square(y.astype(_jnp.float32) - loss_target)
        return 0.5 * _jnp.sum(_jnp.mean(err, axis=-1)) if err.ndim else 0.5 * err


def _adamw(w, g, m, v):
    m = ADAM_B1 * m + (1.0 - ADAM_B1) * g
    v = ADAM_B2 * v + (1.0 - ADAM_B2) * _jnp.square(g)
    m_hat = m / (1.0 - ADAM_B1 ** ADAM_STEP)
    v_hat = v / (1.0 - ADAM_B2 ** ADAM_STEP)
    delta = -ADAM_LR * (m_hat / (_jnp.sqrt(v_hat) + ADAM_EPS) + ADAM_WD * w)
    return delta, m, v


def reference(x, ln_mix_g, ln_mix_b, w_in, b_in, attn_sinks, w_proj_a, w_proj_b, w_out, ln_ffn_g, ln_ffn_b, w_ffn_in, conv_w, conv_b, w_ffn_out, loss_target, m_ln_mix_g, m_ln_mix_b, m_w_in, m_b_in, m_attn_sinks, m_w_proj_a, m_w_proj_b, m_w_out, m_ln_ffn_g, m_ln_ffn_b, m_w_ffn_in, m_conv_w, m_conv_b, m_w_ffn_out, v_ln_mix_g, v_ln_mix_b, v_w_in, v_b_in, v_attn_sinks, v_w_proj_a, v_w_proj_b, v_w_out, v_ln_ffn_g, v_ln_ffn_b, v_w_ffn_in, v_conv_w, v_conv_b, v_w_ffn_out):
    given = dict(x=x, ln_mix_g=ln_mix_g, ln_mix_b=ln_mix_b, w_in=w_in, b_in=b_in, attn_sinks=attn_sinks, w_proj_a=w_proj_a, w_proj_b=w_proj_b, w_out=w_out, ln_ffn_g=ln_ffn_g, ln_ffn_b=ln_ffn_b, w_ffn_in=w_ffn_in, conv_w=conv_w, conv_b=conv_b, w_ffn_out=w_ffn_out, loss_target=loss_target, m_ln_mix_g=m_ln_mix_g, m_ln_mix_b=m_ln_mix_b, m_w_in=m_w_in, m_b_in=m_b_in, m_attn_sinks=m_attn_sinks, m_w_proj_a=m_w_proj_a, m_w_proj_b=m_w_proj_b, m_w_out=m_w_out, m_ln_ffn_g=m_ln_ffn_g, m_ln_ffn_b=m_ln_ffn_b, m_w_ffn_in=m_w_ffn_in, m_conv_w=m_conv_w, m_conv_b=m_conv_b, m_w_ffn_out=m_w_ffn_out, v_ln_mix_g=v_ln_mix_g, v_ln_mix_b=v_ln_mix_b, v_w_in=v_w_in, v_b_in=v_b_in, v_attn_sinks=v_attn_sinks, v_w_proj_a=v_w_proj_a, v_w_proj_b=v_w_proj_b, v_w_out=v_w_out, v_ln_ffn_g=v_ln_ffn_g, v_ln_ffn_b=v_ln_ffn_b, v_w_ffn_in=v_w_ffn_in, v_conv_w=v_conv_w, v_conv_b=v_conv_b, v_w_ffn_out=v_w_ffn_out)
    weights = {n: given[n] for n in TWIN_WEIGHTS}
    shared = {n: given[n] for n in SHARED_INPUTS}
    per_example = {n: given[n] for n in ['x']}
    grad_fn = _jax.value_and_grad(_loss, argnums=(0, 1))

    def one_microbatch(ex, loss_target):
        ex = dict(ex)
        diff = ex.pop(TWIN_DIFF_INPUT)
        return grad_fn(weights, diff, {**shared, **ex}, loss_target)

    if N_MICROBATCH == 1:
        loss, (grad_w, grad_x) = one_microbatch(per_example, given["loss_target"])
    else:
        def body(carry, xs):
            loss_sum, grad_sum = carry
            l_k, (gw_k, gx_k) = one_microbatch(xs[0], xs[1])
            with _jax.named_scope("update"):
                return (loss_sum + l_k, _jax.tree.map(_jnp.add, grad_sum, gw_k)), gx_k

        init = (_jnp.zeros((), _jnp.float32), _jax.tree.map(_jnp.zeros_like, weights))
        (loss, grad_w), grad_x = _jax.lax.scan(body, init, (per_example, given["loss_target"]))
    with _jax.named_scope("update"):
        delta_w, new_m, new_v = {}, {}, {}
        for n in TWIN_WEIGHTS:
            delta_w[n], new_m[n], new_v[n] = _adamw(weights[n], grad_w[n], given["m_" + n], given["v_" + n])
    return (loss, grad_x, *[grad_w[n] for n in TWIN_WEIGHTS], *[delta_w[n] for n in TWIN_WEIGHTS],
            *[new_m[n] for n in TWIN_WEIGHTS], *[new_v[n] for n in TWIN_WEIGHTS])
```

```python
import functools

import jax
import jax.numpy as jnp
from jax import lax
from jax.experimental import pallas as pl
from jax.experimental.pallas import tpu as pltpu

F32 = jnp.float32
BF16 = jnp.bfloat16
MESH = pl.DeviceIdType.MESH

N_DEV = 8
DEPTH = 2
D_MODEL = 1024
HEAD_DIM = 64
SWA_Q = 512
SWA_KV = 128
FOX_W = 512
FOX_HEADS = 8
SWA_HEADS = 8
D_FF = 2816
N_IN = 4360
N_QKV = SWA_Q + 2 * SWA_KV + 3 * FOX_W
N_GATE = 2 * D_MODEL
F_PAD = 256
N_ZG = N_GATE + F_PAD
N_ZP = N_QKV + N_ZG
LN_EPS = 1e-5
NEG_INF = -1e30
ALPHA = (2 * DEPTH) ** 0.25
SCALE = HEAD_DIM ** -0.5
SLOPES = tuple(2.0 ** (-8.0 * (h + 1) / SWA_HEADS) for h in range(SWA_HEADS))

ADAM_LR = 0.001
ADAM_B1 = 0.9
ADAM_B2 = 0.999
ADAM_EPS = 1e-08
ADAM_WD = 0.01
ADAM_STEP = 10

LANE = 128
VMEM_LIMIT = 56 * 1024 * 1024

BIG = (("w_in", (D_MODEL, N_IN), 1), ("w_proj_a", (SWA_Q, D_MODEL), 1), ("w_proj_b", (FOX_W, D_MODEL), 1),
       ("w_out", (D_MODEL, D_MODEL), 0), ("w_ffn_in", (D_MODEL, 2 * D_FF), 1), ("w_ffn_out", (D_FF, D_MODEL), 0))
SMALL = (("ln_mix_g", D_MODEL), ("ln_mix_b", D_MODEL), ("b_in", N_IN), ("attn_sinks", SWA_HEADS),
         ("ln_ffn_g", D_MODEL), ("ln_ffn_b", D_MODEL), ("conv_w", 3 * D_FF), ("conv_b", D_FF))
BIG_PER_LAYER = sum(s[0] * s[1] // N_DEV for _, s, _ in BIG)
BIG_ELEMS = DEPTH * BIG_PER_LAYER
ROW_BLOCK = 512
BIG_ROWS = -(-BIG_ELEMS // (LANE * ROW_BLOCK)) * ROW_BLOCK
SMALL_ELEMS = DEPTH * sum(n for _, n in SMALL) + 1
SMALL_ROWS = ROW_BLOCK
assert SMALL_ELEMS <= SMALL_ROWS * LANE


def _div(n, cap, unit):
    if n <= cap:
        return n
    best = None
    for t in range(unit, cap + 1, unit):
        if n % t == 0:
            best = t
    assert best is not None, (n, cap, unit)
    return best


def _params(*sem):
    return pltpu.CompilerParams(dimension_semantics=sem, vmem_limit_bytes=VMEM_LIMIT)


def _peer(r):
    x, y, c = lax.axis_index("x"), lax.axis_index("y"), lax.axis_index("c")
    px = 1 - x if (r >> 2) & 1 else x
    py = 1 - y if (r >> 1) & 1 else y
    pc = 1 - c if r & 1 else c
    return (px, py, pc), 4 * px + 2 * py + pc


def _exchange(x, gather, name):
    rows = x.shape[-2]

    def body(x_ref, out_ref, send_sems, recv_sems, local_sem):
        _, me = _peer(0)

        def src(idx):
            return x_ref if gather else x_ref.at[idx]

        mine = pltpu.make_async_copy(src(me), out_ref.at[me], local_sem)
        mine.start()
        sent = []
        for r in range(1, N_DEV):
            peer, pid = _peer(r)
            cp = pltpu.make_async_remote_copy(src_ref=src(pid), dst_ref=out_ref.at[me], send_sem=send_sems.at[r - 1],
                                              recv_sem=recv_sems.at[r - 1], device_id=peer, device_id_type=MESH)
            cp.start()
            sent.append(cp)
        for r in range(1, N_DEV):
            peer, pid = _peer(r)
            pltpu.make_async_remote_copy(src_ref=src(pid), dst_ref=out_ref.at[pid], send_sem=send_sems.at[r - 1],
                                         recv_sem=recv_sems.at[r - 1], device_id=peer, device_id_type=MESH).wait_recv()
        for cp in sent:
            cp.wait_send()
        mine.wait()

    return pl.pallas_call(
        body, name=name,
        out_shape=jax.ShapeDtypeStruct((N_DEV, rows, LANE), x.dtype),
        in_specs=[pl.BlockSpec(memory_space=pl.ANY)],
        out_specs=pl.BlockSpec(memory_space=pl.ANY),
        scratch_shapes=[pltpu.SemaphoreType.DMA((N_DEV - 1,)), pltpu.SemaphoreType.DMA((N_DEV - 1,)),
                        pltpu.SemaphoreType.DMA],
    )(x)


def _linear(a, b, *, name, trans_b=False, bias=None, res=None, res_scale=1.0, out_dtype=F32, tm=512, tn=640):
    m, k = a.shape
    n = b.shape[0] if trans_b else b.shape[1]
    tm = _div(m, tm, 8)
    tn = _div(n, tn, LANE)
    dn = (((1,), (1,)), ((), ())) if trans_b else (((1,), (0,)), ((), ()))

    def body(*refs):
        a_ref, b_ref = refs[0], refs[1]
        rest = list(refs[2:])
        bias_ref = rest.pop(0) if bias is not None else None
        res_ref = rest.pop(0) if res is not None else None
        o_ref = rest.pop(0)
        acc = lax.dot_general(a_ref[...].astype(BF16), b_ref[...].astype(BF16), dn, preferred_element_type=F32)
        if bias_ref is not None:
            acc = acc + bias_ref[...]
        if res_ref is not None:
            acc = acc + res_scale * res_ref[...].astype(F32)
        o_ref[...] = acc.astype(out_dtype)

    in_specs = [pl.BlockSpec((tm, k), lambda i, j: (i, 0)),
                pl.BlockSpec((tn, k), lambda i, j: (j, 0)) if trans_b else pl.BlockSpec((k, tn), lambda i, j: (0, j))]
    args = [a, b]
    if bias is not None:
        in_specs.append(pl.BlockSpec((1, tn), lambda i, j: (0, j)))
        args.append(bias)
    if res is not None:
        in_specs.append(pl.BlockSpec((tm, tn), lambda i, j: (i, j)))
        args.append(res)
    return pl.pallas_call(
        body, name=name, grid=(m // tm, n // tn), in_specs=in_specs,
        out_specs=pl.BlockSpec((tm, tn), lambda i, j: (i, j)),
        out_shape=jax.ShapeDtypeStruct((m, n), out_dtype),
        compiler_params=_params("parallel", "arbitrary"),
    )(*args)


def _linear_tn(a, g, *, name, tk=1024, tn=640, tm=512):
    m, k = a.shape
    n = g.shape[1]
    tk = _div(k, tk, LANE)
    tn = _div(n, tn, LANE)
    tm = _div(m, tm, 8)
    steps = m // tm

    def body(a_ref, g_ref, o_ref, acc_ref):
        s = pl.program_id(2)

        @pl.when(s == 0)
        def _():
            acc_ref[...] = jnp.zeros_like(acc_ref)

        acc_ref[...] += lax.dot_general(a_ref[...].astype(BF16), g_ref[...].astype(BF16), (((0,), (0,)), ((), ())),
                                        preferred_element_type=F32)

        @pl.when(s == steps - 1)
        def _():
            o_ref[...] = acc_ref[...]

    return pl.pallas_call(
        body, name=name, grid=(k // tk, n // tn, steps),
        in_specs=[pl.BlockSpec((tm, tk), lambda i, j, s: (s, i)), pl.BlockSpec((tm, tn), lambda i, j, s: (s, j))],
        out_specs=pl.BlockSpec((tk, tn), lambda i, j, s: (i, j)),
        out_shape=jax.ShapeDtypeStruct((k, n), F32),
        scratch_shapes=[pltpu.VMEM((tk, tn), F32)],
        compiler_params=_params("parallel", "parallel", "arbitrary"),
    )(a, g)


def _colsum(g, *, name, tm=512):
    m, n = g.shape
    tm = _div(m, tm, 8)

    def body(g_ref, o_ref):
        @pl.when(pl.program_id(0) == 0)
        def _():
            o_ref[...] = jnp.zeros_like(o_ref)

        o_ref[...] += jnp.sum(g_ref[...].astype(F32), axis=0, keepdims=True)

    return pl.pallas_call(
        body, name=name, grid=(m // tm,),
        in_specs=[pl.BlockSpec((tm, n), lambda i: (i, 0))],
        out_specs=pl.BlockSpec((1, n), lambda i: (0, 0)),
        out_shape=jax.ShapeDtypeStruct((1, n), F32),
        compiler_params=_params("arbitrary"),
    )(g)


def _ln(u, g, b):
    mu = jnp.mean(u, axis=-1, keepdims=True)
    d = u - mu
    var = jnp.mean(d * d, axis=-1, keepdims=True)
    return d * lax.rsqrt(var + LN_EPS) * g + b


def _ln_bwd(dy, u, g, *, name, tm=256):
    m, d = u.shape
    tm = _div(m, tm, 8)

    def body(dy_ref, u_ref, g_ref, du_ref, dg_ref, db_ref):
        @pl.when(pl.program_id(0) == 0)
        def _():
            dg_ref[...] = jnp.zeros_like(dg_ref)
            db_ref[...] = jnp.zeros_like(db_ref)

        dy = dy_ref[...]
        uu = u_ref[...]
        mu = jnp.mean(uu, axis=-1, keepdims=True)
        dd = uu - mu
        rstd = lax.rsqrt(jnp.mean(dd * dd, axis=-1, keepdims=True) + LN_EPS)
        xhat = dd * rstd
        dxh = dy * g_ref[...]
        m1 = jnp.mean(dxh, axis=-1, keepdims=True)
        m2 = jnp.mean(dxh * xhat, axis=-1, keepdims=True)
        du_ref[...] = rstd * (dxh - m1 - xhat * m2)
        dg_ref[...] += jnp.sum(dy * xhat, axis=0, keepdims=True)
        db_ref[...] += jnp.sum(dy, axis=0, keepdims=True)

    row = pl.BlockSpec((tm, d), lambda i: (i, 0))
    vec = pl.BlockSpec((1, d), lambda i: (0, 0))
    return pl.pallas_call(
        body, name=name, grid=(m // tm,), in_specs=[row, row, vec], out_specs=[row, vec, vec],
        out_shape=[jax.ShapeDtypeStruct((m, d), F32), jax.ShapeDtypeStruct((1, d), F32),
                   jax.ShapeDtypeStruct((1, d), F32)],
        compiler_params=_params("arbitrary"),
    )(dy, u, g)


def _linear_res_ln(a, w, res, g, b, *, name, tm=256):
    m, k = a.shape
    d = w.shape[1]
    tm = _div(m, tm, 8)

    def body(a_ref, w_ref, res_ref, g_ref, b_ref, u_ref, y_ref):
        u = ALPHA * res_ref[...] + jnp.dot(a_ref[...].astype(BF16), w_ref[...], preferred_element_type=F32)
        u_ref[...] = u
        y_ref[...] = _ln(u, g_ref[...], b_ref[...])

    row = pl.BlockSpec((tm, d), lambda i: (i, 0))
    vec = pl.BlockSpec((1, d), lambda i: (0, 0))
    return pl.pallas_call(
        body, name=name, grid=(m // tm,),
        in_specs=[pl.BlockSpec((tm, k), lambda i: (i, 0)), pl.BlockSpec((k, d), lambda i: (0, 0)), row, vec, vec],
        out_specs=[row, row],
        out_shape=[jax.ShapeDtypeStruct((m, d), F32), jax.ShapeDtypeStruct((m, d), F32)],
        compiler_params=_params("parallel"),
    )(a, w, res, g, b)


def _tri(n, upper):
    r = lax.broadcasted_iota(jnp.int32, (n, n), 0)
    c = lax.broadcasted_iota(jnp.int32, (n, n), 1)
    return jnp.where((c >= r) if upper else (c <= r), 1.0, 0.0).astype(F32)


def _cumsum_logf(zg):
    s = zg.shape[0]
    nb = s // LANE
    fcol = N_GATE // LANE

    def body(f_ref, c_ref, carry_ref):
        @pl.when(pl.program_id(0) == 0)
        def _():
            carry_ref[...] = jnp.zeros_like(carry_ref)

        f = f_ref[...]
        logf = jnp.minimum(f, 0.0) - jnp.log(1.0 + jnp.exp(-jnp.abs(f)))
        c = jnp.dot(_tri(LANE, False), logf, precision=lax.Precision.HIGHEST, preferred_element_type=F32)
        c = c + carry_ref[0:1, :]
        c_ref[...] = c
        carry_ref[...] = jnp.broadcast_to(c[LANE - 1:LANE, :], carry_ref.shape)

    return pl.pallas_call(
        body, name="cumsum_logf", grid=(nb,),
        in_specs=[pl.BlockSpec((LANE, LANE), lambda i: (i, fcol))],
        out_specs=pl.BlockSpec((LANE, LANE), lambda i: (i, 0)),
        out_shape=jax.ShapeDtypeStruct((s, LANE), F32),
        scratch_shapes=[pltpu.VMEM((8, LANE), F32)],
        compiler_params=_params("arbitrary"),
    )(zg)


def _forget_bwd(dc, zg):
    s = zg.shape[0]
    nb = s // LANE
    fcol = N_GATE // LANE

    def body(dc_ref, f_ref, o_ref, carry_ref):
        @pl.when(pl.program_id(0) == 0)
        def _():
            carry_ref[...] = jnp.zeros_like(carry_ref)

        dc = dc_ref[...]
        dlogf = jnp.dot(_tri(LANE, True), dc, precision=lax.Precision.HIGHEST, preferred_element_type=F32)
        dlogf = dlogf + carry_ref[0:1, :]
        o_ref[...] = (dlogf * jax.nn.sigmoid(-f_ref[...])).astype(BF16)
        carry_ref[...] = jnp.broadcast_to(dlogf[0:1, :], carry_ref.shape)

    return pl.pallas_call(
        body, name="forget_bwd", grid=(nb,),
        in_specs=[pl.BlockSpec((LANE, LANE), lambda i: (nb - 1 - i, 0)),
                  pl.BlockSpec((LANE, LANE), lambda i: (nb - 1 - i, fcol))],
        out_specs=pl.BlockSpec((LANE, LANE), lambda i: (nb - 1 - i, 0)),
        out_shape=jax.ShapeDtypeStruct((s, LANE), BF16),
        scratch_shapes=[pltpu.VMEM((8, LANE), F32)],
        compiler_params=_params("arbitrary"),
    )(dc, zg)


KA_COL = SWA_Q // LANE
VA_COL = KA_COL + 1


def _half_masks():
    lane = lax.broadcasted_iota(jnp.int32, (1, LANE), 1)
    hi = lane >= HEAD_DIM
    return (jnp.logical_not(hi), hi)


def _both_halves(x, sel):
    xs = jnp.where(sel, x, 0.0)
    return xs + pltpu.roll(xs, HEAD_DIM, 1)


def _swa_geometry(i):
    r = lax.broadcasted_iota(jnp.int32, (LANE, 2 * LANE), 0)
    c = lax.broadcasted_iota(jnp.int32, (LANE, 2 * LANE), 1)
    dist = r + LANE - c
    valid = (dist >= 0) & (dist < LANE) & ((c >= LANE) | (i > 0))
    return valid, -dist.astype(F32)


def _swa_specs(nb):
    prev = lambda i: jnp.maximum(i - 1, 0)
    return [pl.BlockSpec((LANE, SWA_Q), lambda i: (i, 0)),
            pl.BlockSpec((LANE, LANE), lambda i: (i, KA_COL)), pl.BlockSpec((LANE, LANE), lambda i: (i, VA_COL)),
            pl.BlockSpec((LANE, LANE), lambda i: (prev(i), KA_COL)),
            pl.BlockSpec((LANE, LANE), lambda i: (prev(i), VA_COL))]


def _swa_fwd(zq, sinks):
    s_len = zq.shape[0]
    nb = s_len // LANE

    def body(q_ref, kc_ref, vc_ref, kp_ref, vp_ref, sink_ref, o_ref, lse_ref):
        i = pl.program_id(0)
        halves = _half_masks()
        lane = lax.broadcasted_iota(jnp.int32, (1, LANE), 1)
        valid, negdist = _swa_geometry(i)
        kcat = jnp.concatenate([kp_ref[...], kc_ref[...]], axis=0).astype(F32)
        vcat = jnp.concatenate([vp_ref[...], vc_ref[...]], axis=0).astype(F32)
        lse_acc = jnp.zeros((LANE, LANE), F32)
        for hk in range(2):
            kb = _both_halves(kcat, halves[hk]).astype(BF16)
            vb = _both_halves(vcat, halves[hk])
            v_e = [jnp.where(halves[e], vb, 0.0).astype(BF16) for e in range(2)]
            for pp in range(2):
                p = 2 * hk + pp
                qp = q_ref[:, p * LANE:(p + 1) * LANE]
                acc = jnp.zeros((LANE, LANE), F32)
                for e in range(2):
                    hq = 2 * p + e
                    qh = jnp.where(halves[e], qp, jnp.zeros_like(qp))
                    s = lax.dot_general(qh, kb, (((1,), (1,)), ((), ())), preferred_element_type=F32) * SCALE
                    s = jnp.where(valid, s + SLOPES[hq] * negdist, NEG_INF)
                    sink = sink_ref[0:1, hq:hq + 1]
                    m = jnp.maximum(jnp.max(s, axis=1, keepdims=True), sink)
                    pe = jnp.exp(s - m)
                    den = jnp.sum(pe, axis=1, keepdims=True) + jnp.exp(sink - m)
                    acc = acc + jnp.dot((pe / den).astype(BF16), v_e[e], preferred_element_type=F32)
                    lse_acc = jnp.where(lane == hq, m + jnp.log(den), lse_acc)
                o_ref[:, p * LANE:(p + 1) * LANE] = acc.astype(BF16)
        lse_ref[...] = lse_acc

    return pl.pallas_call(
        body, name="swa_fwd", grid=(nb,),
        in_specs=_swa_specs(nb) + [pl.BlockSpec((1, LANE), lambda i: (0, 0))],
        out_specs=[pl.BlockSpec((LANE, SWA_Q), lambda i: (i, 0)), pl.BlockSpec((LANE, LANE), lambda i: (i, 0))],
        out_shape=[jax.ShapeDtypeStruct((s_len, SWA_Q), BF16), jax.ShapeDtypeStruct((s_len, LANE), F32)],
        compiler_params=_params("parallel"),
    )(zq, zq, zq, zq, zq, sinks)


def _swa_bwd(zq, sinks, o, do, lse):
    s_len = zq.shape[0]
    nb = s_len // LANE

    def body(q_ref, kc_ref, vc_ref, kp_ref, vp_ref, sink_ref, o_ref, do_ref, lse_ref, dq_ref, dk_ref, dv_ref, ds_ref):
        i = pl.program_id(0)
        halves = _half_masks()
        lane = lax.broadcasted_iota(jnp.int32, (1, LANE), 1)
        valid, negdist = _swa_geometry(i)
        kcat = jnp.concatenate([kp_ref[...], kc_ref[...]], axis=0).astype(F32)
        vcat = jnp.concatenate([vp_ref[...], vc_ref[...]], axis=0).astype(F32)
        lse_all = lse_ref[...]
        dk_tot = jnp.zeros((2 * LANE, LANE), F32)
        dv_tot = jnp.zeros((2 * LANE, LANE), F32)
        dsink = jnp.zeros((1, LANE), F32)
        for hk in range(2):
            kb = _both_halves(kcat, halves[hk])
            vb = _both_halves(vcat, halves[hk])
            k_e = [jnp.where(halves[e], kb, 0.0).astype(BF16) for e in range(2)]
            v_e = [jnp.where(halves[e], vb, 0.0).astype(BF16) for e in range(2)]
            kb = kb.astype(BF16)
            dk_acc = jnp.zeros((2 * LANE, LANE), F32)
            dv_acc = jnp.zeros((2 * LANE, LANE), F32)
            for pp in range(2):
                p = 2 * hk + pp
                cols = slice(p * LANE, (p + 1) * LANE)
                qp = q_ref[:, cols]
                dop = do_ref[:, cols]
                prod = dop.astype(F32) * o_ref[:, cols].astype(F32)
                dq_acc = jnp.zeros((LANE, LANE), F32)
                for e in range(2):
                    hq = 2 * p + e
                    qh = jnp.where(halves[e], qp, jnp.zeros_like(qp))
                    doh = jnp.where(halves[e], dop, jnp.zeros_like(dop))
                    delta = jnp.sum(jnp.where(halves[e], prod, 0.0), axis=1, keepdims=True)
                    lse_h = lse_all[:, hq:hq + 1]
                    s = lax.dot_general(qh, kb, (((1,), (1,)), ((), ())), preferred_element_type=F32) * SCALE
                    s = jnp.where(valid, s + SLOPES[hq] * negdist, NEG_INF)
                    pr = jnp.exp(s - lse_h)
                    dp = lax.dot_general(doh, v_e[e], (((1,), (1,)), ((), ())), preferred_element_type=F32)
                    ds = pr * (dp - delta)
                    sink = sink_ref[0:1, hq:hq + 1]
                    dsink_h = -jnp.sum(jnp.exp(sink - lse_h) * delta, axis=0, keepdims=True)
                    dsink = dsink + jnp.where(lane == hq, dsink_h, 0.0)
                    dsb = (ds * SCALE).astype(BF16)
                    dq_acc = dq_acc + jnp.dot(dsb, k_e[e], preferred_element_type=F32)
                    dk_acc = dk_acc + lax.dot_general(dsb, qh, (((0,), (0,)), ((), ())), preferred_element_type=F32)
                    dv_acc = dv_acc + lax.dot_general(pr.astype(BF16), doh, (((0,), (0,)), ((), ())),
                                                      preferred_element_type=F32)
                dq_ref[:, cols] = dq_acc.astype(BF16)
            dk_tot = dk_tot + jnp.where(halves[hk], dk_acc + pltpu.roll(dk_acc, HEAD_DIM, 1), 0.0)
            dv_tot = dv_tot + jnp.where(halves[hk], dv_acc + pltpu.roll(dv_acc, HEAD_DIM, 1), 0.0)

        @pl.when(i == 0)
        def _():
            ds_ref[...] = jnp.zeros_like(ds_ref)

        ds_ref[...] += dsink
        cur = pl.ds(pl.multiple_of(i * LANE, LANE), LANE)
        dk_ref[cur, :] = dk_tot[LANE:, :]
        dv_ref[cur, :] = dv_tot[LANE:, :]

        @pl.when(i > 0)
        def _():
            prv = pl.ds(pl.multiple_of((i - 1) * LANE, LANE), LANE)
            dk_ref[prv, :] += dk_tot[:LANE, :]
            dv_ref[prv, :] += dv_tot[:LANE, :]

    blk512 = pl.BlockSpec((LANE, SWA_Q), lambda i: (i, 0))
    full = pl.BlockSpec((s_len, LANE), lambda i: (0, 0))
    vec = pl.BlockSpec((1, LANE), lambda i: (0, 0))
    return pl.pallas_call(
        body, name="swa_bwd", grid=(nb,),
        in_specs=_swa_specs(nb) + [vec, blk512, blk512, pl.BlockSpec((LANE, LANE), lambda i: (i, 0))],
        out_specs=[blk512, full, full, vec],
        out_shape=[jax.ShapeDtypeStruct((s_len, SWA_Q), BF16), jax.ShapeDtypeStruct((s_len, LANE), F32),
                   jax.ShapeDtypeStruct((s_len, LANE), F32), jax.ShapeDtypeStruct((1, LANE), F32)],
        compiler_params=_params("arbitrary"),
    )(zq, zq, zq, zq, zq, sinks, o, do, lse)


QB_COL = (SWA_Q + 2 * SWA_KV) // LANE
KB_COL = QB_COL + FOX_W // LANE
VB_COL = KB_COL + FOX_W // LANE
N_PAIR = FOX_HEADS // 2


def _causal(t):
    r = lax.broadcasted_iota(jnp.int32, (t, t), 0)
    c = lax.broadcasted_iota(jnp.int32, (t, t), 1)
    return r >= c


def _fox_fwd(zq, cq, ck, t_cap=256):
    s_len = zq.shape[0]
    t = _div(s_len, t_cap, LANE)
    nq = s_len // t

    def body(q_ref, k_ref, v_ref, cq_ref, ck_ref, o_ref, o32_ref, lse_ref):
        i = pl.program_id(1)
        halves = _half_masks()
        q = q_ref[...]
        q_e = [jnp.where(halves[e], q, jnp.zeros_like(q)) for e in range(2)]
        cq_e = [cq_ref[:, e * HEAD_DIM:e * HEAD_DIM + 1] for e in range(2)]
        tri = _causal(t)

        def step(j, carry, diag):
            m, l, acc = carry
            rows = pl.ds(pl.multiple_of(j * t, t), t)
            ks = k_ref[rows, :]
            vs = v_ref[rows, :]
            m_new, l_new, a_e, pv = [], [], [], []
            for e in range(2):
                ck_row = ck_ref[0, e:e + 1, rows]
                s = lax.dot_general(q_e[e], ks, (((1,), (1,)), ((), ())), preferred_element_type=F32) * SCALE
                s = s + (cq_e[e] - ck_row)
                if diag:
                    s = jnp.where(tri, s, NEG_INF)
                mn = jnp.maximum(m[e], jnp.max(s, axis=1, keepdims=True))
                a = jnp.exp(m[e] - mn)
                pe = jnp.exp(s - mn)
                m_new.append(mn)
                l_new.append(a * l[e] + jnp.sum(pe, axis=1, keepdims=True))
                a_e.append(a)
                ve = jnp.where(halves[e], vs, jnp.zeros_like(vs))
                pv.append(jnp.dot(pe.astype(BF16), ve, preferred_element_type=F32))
            acc = acc * jnp.where(halves[1], a_e[1], a_e[0]) + pv[0] + pv[1]
            return (tuple(m_new), tuple(l_new), acc)

        col = jnp.full((t, 1), NEG_INF, F32)
        zero = jnp.zeros((t, 1), F32)
        carry = ((col, col), (zero, zero), jnp.zeros((t, LANE), F32))
        carry = lax.fori_loop(0, i, lambda j, c: step(j, c, False), carry)
        m, l, acc = step(i, carry, True)
        out = acc / jnp.where(halves[1], l[1], l[0])
        o_ref[...] = out.astype(BF16)
        o32_ref[...] = out
        lse_ref[...] = jnp.where(halves[1], m[1] + jnp.log(l[1]), m[0] + jnp.log(l[0]))

    return pl.pallas_call(
        body, name="fox_fwd", grid=(N_PAIR, nq),
        in_specs=[pl.BlockSpec((t, LANE), lambda p, i: (i, QB_COL + p)),
                  pl.BlockSpec((s_len, LANE), lambda p, i: (0, KB_COL + p)),
                  pl.BlockSpec((s_len, LANE), lambda p, i: (0, VB_COL + p)),
                  pl.BlockSpec((t, LANE), lambda p, i: (i, p)),
                  pl.BlockSpec((1, 8, s_len), lambda p, i: (p, 0, 0))],
        out_specs=[pl.BlockSpec((t, LANE), lambda p, i: (i, p)), pl.BlockSpec((t, LANE), lambda p, i: (i, p)),
                   pl.BlockSpec((t, LANE), lambda p, i: (i, p))],
        out_shape=[jax.ShapeDtypeStruct((s_len, FOX_W), BF16), jax.ShapeDtypeStruct((s_len, FOX_W), F32),
                   jax.ShapeDtypeStruct((s_len, FOX_W), F32)],
        compiler_params=_params("parallel", "parallel"),
    )(zq, zq, zq, cq, ck)


def _fox_stats(o, do, cq, lse, tm=256):
    s_len = o.shape[0]
    tm = _div(s_len, tm, 8)

    def body(o_ref, do_ref, cq_ref, lse_ref, st_ref):
        lane = lax.broadcasted_iota(jnp.int32, (1, LANE), 1)
        prod = o_ref[...].astype(F32) * do_ref[...].astype(F32)
        diff = cq_ref[...] - lse_ref[...]
        for p in range(N_PAIR):
            st = jnp.zeros((tm, LANE), F32)
            for e in range(2):
                lo = (2 * p + e) * HEAD_DIM
                st = jnp.where(lane == e, diff[:, lo:lo + 1], st)
                delta = jnp.sum(prod[:, lo:lo + HEAD_DIM], axis=1, keepdims=True)
                st = jnp.where(lane == 2 + e, delta, st)
            st_ref[:, p * LANE:(p + 1) * LANE] = st

    row = pl.BlockSpec((tm, FOX_W), lambda i: (i, 0))
    return pl.pallas_call(
        body, name="fox_stats", grid=(s_len // tm,), in_specs=[row, row, row, row], out_specs=row,
        out_shape=jax.ShapeDtypeStruct((s_len, FOX_W), F32),
        compiler_params=_params("parallel"),
    )(o, do, cq, lse)


def _fox_bwd(zq, do, stats, ck, t_cap=256):
    s_len = zq.shape[0]
    t = _div(s_len, t_cap, LANE)
    nq = s_len // t

    def body(q_ref, do_ref, st_ref, k_ref, v_ref, ck_ref, dq_ref, dk_ref, dv_ref, dck_ref):
        j = pl.program_id(1)
        halves = _half_masks()

        @pl.when(j == 0)
        def _():
            dq_ref[...] = jnp.zeros_like(dq_ref)

        ks = k_ref[...]
        vs = v_ref[...]
        k_e = [jnp.where(halves[e], ks, jnp.zeros_like(ks)) for e in range(2)]
        v_e = [jnp.where(halves[e], vs, jnp.zeros_like(vs)) for e in range(2)]
        ck_e = [ck_ref[0, e:e + 1, :] for e in range(2)]
        tri = _causal(t)

        def step(i, carry, diag):
            dk, dv, dc = carry
            rows = pl.ds(pl.multiple_of(i * t, t), t)
            q = q_ref[rows, :]
            dout = do_ref[rows, :]
            st = st_ref[rows, :]
            dq = jnp.zeros((t, LANE), F32)
            dc_new = []
            for e in range(2):
                qe = jnp.where(halves[e], q, jnp.zeros_like(q))
                doe = jnp.where(halves[e], dout, jnp.zeros_like(dout))
                s = lax.dot_general(qe, ks, (((1,), (1,)), ((), ())), preferred_element_type=F32) * SCALE
                s = s + (st[:, e:e + 1] - ck_e[e])
                if diag:
                    s = jnp.where(tri, s, NEG_INF)
                pr = jnp.exp(s)
                dp = lax.dot_general(doe, v_e[e], (((1,), (1,)), ((), ())), preferred_element_type=F32)
                ds = pr * (dp - st[:, 2 + e:3 + e])
                dc_new.append(dc[e] + jnp.sum(ds, axis=0, keepdims=True))
                dsb = (ds * SCALE).astype(BF16)
                dv = dv + lax.dot_general(pr.astype(BF16), doe, (((0,), (0,)), ((), ())), preferred_element_type=F32)
                dk = dk + lax.dot_general(dsb, qe, (((0,), (0,)), ((), ())), preferred_element_type=F32)
                dq = dq + jnp.dot(dsb, k_e[e], preferred_element_type=F32)
            dq_ref[rows, :] += dq
            return (dk, dv, tuple(dc_new))

        zrow = jnp.zeros((1, t), F32)
        carry = (jnp.zeros((t, LANE), F32), jnp.zeros((t, LANE), F32), (zrow, zrow))
        carry = step(j, carry, True)
        dk, dv, dc = lax.fori_loop(j + 1, nq, lambda i, c: step(i, c, False), carry)
        dk_ref[...] = dk.astype(BF16)
        dv_ref[...] = dv.astype(BF16)
        row = lax.broadcasted_iota(jnp.int32, (8, t), 0)
        dck_ref[0] = jnp.where(row == 0, -dc[0], jnp.where(row == 1, -dc[1], 0.0))

    full = lambda col: pl.BlockSpec((s_len, LANE), lambda p, j: (0, col(p)))
    return pl.pallas_call(
        body, name="fox_bwd", grid=(N_PAIR, nq),
        in_specs=[full(lambda p: QB_COL + p), full(lambda p: p), full(lambda p: p),
                  pl.BlockSpec((t, LANE), lambda p, j: (j, KB_COL + p)),
                  pl.BlockSpec((t, LANE), lambda p, j: (j, VB_COL + p)),
                  pl.BlockSpec((1, 8, t), lambda p, j: (p, 0, j))],
        out_specs=[full(lambda p: p), pl.BlockSpec((t, LANE), lambda p, j: (j, p)),
                   pl.BlockSpec((t, LANE), lambda p, j: (j, p)), pl.BlockSpec((1, 8, t), lambda p, j: (p, 0, j))],
        out_shape=[jax.ShapeDtypeStruct((s_len, FOX_W), F32), jax.ShapeDtypeStruct((s_len, FOX_W), BF16),
                   jax.ShapeDtypeStruct((s_len, FOX_W), BF16), jax.ShapeDtypeStruct((N_PAIR, 8, s_len), F32)],
        compiler_params=_params("arbitrary", "arbitrary"),
    )(zq, do, stats, zq, zq, ck)


def _mixer_out(attn_a, attn_b, zg, h, wpa, wpb, wout, g, b, tm=256):
    m = h.shape[0]
    tm = _div(m, tm, 8)

    def body(a_ref, b_ref, ga_ref, gb_ref, h_ref, wpa_ref, wpb_ref, wout_ref, g_ref, bb_ref,
             h1_ref, u_ref, mg_ref, ya_ref, yb_ref):
        ya = jnp.dot(a_ref[...], wpa_ref[...], preferred_element_type=F32)
        yb = jnp.dot(b_ref[...], wpb_ref[...], preferred_element_type=F32)
        merged = (jax.nn.sigmoid(ga_ref[...]) * ya + jax.nn.sigmoid(gb_ref[...]) * yb).astype(BF16)
        u = ALPHA * h_ref[...] + jnp.dot(merged, wout_ref[...], preferred_element_type=F32)
        u_ref[...] = u
        h1_ref[...] = _ln(u, g_ref[...], bb_ref[...])
        mg_ref[...] = merged
        ya_ref[...] = ya.astype(BF16)
        yb_ref[...] = yb.astype(BF16)

    row = pl.BlockSpec((tm, D_MODEL), lambda i: (i, 0))
    att = pl.BlockSpec((tm, SWA_Q), lambda i: (i, 0))
    vec = pl.BlockSpec((1, D_MODEL), lambda i: (0, 0))
    wsm = pl.BlockSpec((SWA_Q, D_MODEL), lambda i: (0, 0))
    return pl.pallas_call(
        body, name="mixer_out", grid=(m // tm,),
        in_specs=[att, att, row, pl.BlockSpec((tm, D_MODEL), lambda i: (i, 1)), row, wsm, wsm,
                  pl.BlockSpec((D_MODEL, D_MODEL), lambda i: (0, 0)), vec, vec],
        out_specs=[row, row, row, row, row],
        out_shape=[jax.ShapeDtypeStruct((m, D_MODEL), F32), jax.ShapeDtypeStruct((m, D_MODEL), F32),
                   jax.ShapeDtypeStruct((m, D_MODEL), BF16), jax.ShapeDtypeStruct((m, D_MODEL), BF16),
                   jax.ShapeDtypeStruct((m, D_MODEL), BF16)],
        compiler_params=_params("parallel"),
    )(attn_a, attn_b, zg, zg, h, wpa, wpb, wout, g, b)


def _gate_bwd(dmerged, ya, yb, zg, tm=256):
    m = dmerged.shape[0]
    tm = _div(m, tm, 8)

    def body(dm_ref, ya_ref, yb_ref, ga_ref, gb_ref, dya_ref, dyb_ref, dga_ref, dgb_ref):
        dm = dm_ref[...]
        for y_ref, g_ref, dy_ref, dg_ref in ((ya_ref, ga_ref, dya_ref, dga_ref), (yb_ref, gb_ref, dyb_ref, dgb_ref)):
            sg = jax.nn.sigmoid(g_ref[...])
            dy_ref[...] = (dm * sg).astype(BF16)
            dg_ref[...] = (dm * y_ref[...].astype(F32) * sg * (1.0 - sg)).astype(BF16)

    row = pl.BlockSpec((tm, D_MODEL), lambda i: (i, 0))
    out = jax.ShapeDtypeStruct((m, D_MODEL), BF16)
    return pl.pallas_call(
        body, name="gate_bwd", grid=(m // tm,),
        in_specs=[row, row, row, row, pl.BlockSpec((tm, D_MODEL), lambda i: (i, 1))],
        out_specs=[row, row, row, row], out_shape=[out, out, out, out],
        compiler_params=_params("parallel"),
    )(dmerged, ya, yb, zg, zg)


def _shift_down(x, k, halo, first):
    rows = lax.broadcasted_iota(jnp.int32, (x.shape[0], 1), 0)
    y = pltpu.roll(x, k, 0)
    for r in range(k):
        fill = jnp.where(first, 0.0, halo[8 - k + r:8 - k + r + 1, :])
        y = jnp.where(rows == r, fill, y)
    return y


def _shift_up(x, k, halo, last):
    n = x.shape[0]
    rows = lax.broadcasted_iota(jnp.int32, (n, 1), 0)
    y = pltpu.roll(x, n - k, 0)
    for r in range(k):
        fill = jnp.where(last, 0.0, halo[r:r + 1, :])
        y = jnp.where(rows == n - k + r, fill, y)
    return y


def _conv_act(gate, gate_m1, gate_m2, cw, cb):
    return cb + cw[0:1, :] * gate_m2 + cw[1:2, :] * gate_m1 + cw[2:3, :] * gate


def _conv_specs(tm, tn, nrow, ncol_off):
    hb = tm // 8
    return [pl.BlockSpec((tm, tn), lambda j, i: (i, j)),
            pl.BlockSpec((8, tn), lambda j, i: (jnp.maximum(i * hb - 1, 0), j)),
            pl.BlockSpec((tm, tn), lambda j, i: (i, j + ncol_off))]


def _conv_fwd(gu, cw, cb, tm=256, tn=1408):
    s_len = gu.shape[0]
    tm = _div(s_len, tm, 8)
    tn = _div(D_FF, tn, LANE)

    def body(g_ref, gp_ref, up_ref, cw_ref, cb_ref, o_ref):
        first = pl.program_id(1) == 0
        gate = g_ref[...]
        halo = gp_ref[...]
        conv = _conv_act(gate, _shift_down(gate, 1, halo, first), _shift_down(gate, 2, halo, first),
                         cw_ref[...], cb_ref[...])
        o_ref[...] = (conv * jax.nn.sigmoid(conv) * up_ref[...]).astype(BF16)

    return pl.pallas_call(
        body, name="conv_fwd", grid=(D_FF // tn, s_len // tm),
        in_specs=_conv_specs(tm, tn, s_len // tm, D_FF // tn) + [pl.BlockSpec((8, tn), lambda j, i: (0, j)),
                                                                 pl.BlockSpec((1, tn), lambda j, i: (0, j))],
        out_specs=pl.BlockSpec((tm, tn), lambda j, i: (i, j)),
        out_shape=jax.ShapeDtypeStruct((s_len, D_FF), BF16),
        compiler_params=_params("parallel", "parallel"),
    )(gu, gu, gu, cw, cb)


def _conv_bwd(gu, dact, cw, cb, tm=256, tn=1408):
    s_len = gu.shape[0]
    tm = _div(s_len, tm, 8)
    tn = _div(D_FF, tn, LANE)
    nrow = s_len // tm
    ncol = D_FF // tn
    hb = tm // 8

    def dconv_of(conv, up, da):
        sg = jax.nn.sigmoid(conv)
        return da * up * (sg * (1.0 + conv * (1.0 - sg)))

    def body(g_ref, gp_ref, up_ref, gn_ref, upn_ref, da_ref, dan_ref, cw_ref, cb_ref, dg_ref, dup_ref, dcw_ref):
        i = pl.program_id(1)
        first = i == 0
        last = i == nrow - 1
        cw = cw_ref[...]
        cb = cb_ref[...]
        gate = g_ref[...]
        halo = gp_ref[...]
        g_m1 = _shift_down(gate, 1, halo, first)
        g_m2 = _shift_down(gate, 2, halo, first)
        conv = _conv_act(gate, g_m1, g_m2, cw, cb)
        da = da_ref[...]
        sg = jax.nn.sigmoid(conv)
        dup_ref[...] = (da * conv * sg).astype(BF16)
        dconv = dconv_of(conv, up_ref[...], da)
        gate_n = gn_ref[...]
        tail = gate[tm - 8:, :]
        conv_n = _conv_act(gate_n, _shift_down(gate_n, 1, tail, False), _shift_down(gate_n, 2, tail, False), cw, cb)
        dconv_n = dconv_of(conv_n, upn_ref[...], dan_ref[...])
        dgate = (cw[2:3, :] * dconv + cw[1:2, :] * _shift_up(dconv, 1, dconv_n, last)
                 + cw[0:1, :] * _shift_up(dconv, 2, dconv_n, last))
        dg_ref[...] = dgate.astype(BF16)

        @pl.when(first)
        def _():
            dcw_ref[...] = jnp.zeros_like(dcw_ref)

        row = lax.broadcasted_iota(jnp.int32, (8, 1), 0)
        part = jnp.zeros((8, tn), F32)
        for r, term in enumerate((dconv * g_m2, dconv * g_m1, dconv * gate, dconv)):
            part = jnp.where(row == r, jnp.sum(term, axis=0, keepdims=True), part)
        dcw_ref[...] += part

    nxt = lambda i: jnp.minimum((i + 1) * hb, s_len // 8 - 1)
    main = pl.BlockSpec((tm, tn), lambda j, i: (i, j))
    return pl.pallas_call(
        body, name="conv_bwd", grid=(ncol, nrow),
        in_specs=_conv_specs(tm, tn, nrow, ncol) + [
            pl.BlockSpec((8, tn), lambda j, i: (nxt(i), j)), pl.BlockSpec((8, tn), lambda j, i: (nxt(i), j + ncol)),
            main, pl.BlockSpec((8, tn), lambda j, i: (nxt(i), j)),
            pl.BlockSpec((8, tn), lambda j, i: (0, j)), pl.BlockSpec((1, tn), lambda j, i: (0, j))],
        out_specs=[main, main, pl.BlockSpec((8, tn), lambda j, i: (0, j))],
        out_shape=[jax.ShapeDtypeStruct((s_len, D_FF), BF16), jax.ShapeDtypeStruct((s_len, D_FF), BF16),
                   jax.ShapeDtypeStruct((8, D_FF), F32)],
        compiler_params=_params("parallel", "arbitrary"),
    )(gu, gu, gu, gu, gu, dact, dact, cw, cb)


def _loss_head(y, target, tm=256):
    m, d = y.shape
    tm = _div(m, tm, 8)

    def body(y_ref, t_ref, dy_ref, loss_ref):
        @pl.when(pl.program_id(0) == 0)
        def _():
            loss_ref[...] = jnp.zeros_like(loss_ref)

        err = y_ref[...] - t_ref[...]
        dy_ref[...] = err / d
        loss_ref[...] += 0.5 * jnp.sum(jnp.sum(err * err, axis=1, keepdims=True) / d, axis=0, keepdims=True)

    row = pl.BlockSpec((tm, d), lambda i: (i, 0))
    return pl.pallas_call(
        body, name="loss_head", grid=(m // tm,), in_specs=[row, row],
        out_specs=[row, pl.BlockSpec((8, LANE), lambda i: (0, 0))],
        out_shape=[jax.ShapeDtypeStruct((m, d), F32), jax.ShapeDtypeStruct((8, LANE), F32)],
        compiler_params=_params("arbitrary"),
    )(y, target)


def _sum8(recv):
    rows = recv.shape[1]

    def body(r_ref, o_ref):
        acc = r_ref[0]
        for d in range(1, N_DEV):
            acc = acc + r_ref[d]
        o_ref[...] = acc

    return pl.pallas_call(
        body, name="sum8", grid=(rows // ROW_BLOCK,),
        in_specs=[pl.BlockSpec((N_DEV, ROW_BLOCK, LANE), lambda i: (0, i, 0))],
        out_specs=pl.BlockSpec((ROW_BLOCK, LANE), lambda i: (i, 0)),
        out_shape=jax.ShapeDtypeStruct((rows, LANE), F32),
        compiler_params=_params("parallel"),
    )(recv)


def _adamw(w, g, m, v, name):
    rows = w.shape[0]
    tr = _div(rows, ROW_BLOCK, 8)

    def body(w_ref, g_ref, m_ref, v_ref, d_ref, mo_ref, vo_ref):
        g = g_ref[...]
        m = ADAM_B1 * m_ref[...] + (1.0 - ADAM_B1) * g
        v = ADAM_B2 * v_ref[...] + (1.0 - ADAM_B2) * (g * g)
        m_hat = m / (1.0 - ADAM_B1 ** ADAM_STEP)
        v_hat = v / (1.0 - ADAM_B2 ** ADAM_STEP)
        d_ref[...] = -ADAM_LR * (m_hat / (jnp.sqrt(v_hat) + ADAM_EPS) + ADAM_WD * w_ref[...])
        mo_ref[...] = m
        vo_ref[...] = v

    blk = pl.BlockSpec((tr, LANE), lambda i: (i, 0))
    out = jax.ShapeDtypeStruct((rows, LANE), F32)
    return pl.pallas_call(
        body, name=name, grid=(rows // tr,), in_specs=[blk, blk, blk, blk], out_specs=[blk, blk, blk],
        out_shape=[out, out, out], compiler_params=_params("parallel"),
    )(w, g, m, v)


def _to_rows(flat, rows):
    flat = flat.reshape(-1)
    return jnp.pad(flat, (0, rows * LANE - flat.shape[0])).reshape(rows, LANE)


def _pad_cols_z(a):
    f0 = N_QKV
    g0 = N_QKV + FOX_HEADS
    pad = jnp.zeros(a.shape[:-1] + (F_PAD - FOX_HEADS,), a.dtype)
    return jnp.concatenate([a[..., :f0], a[..., g0:], a[..., f0:g0], pad], axis=-1)


def _unpad_cols_z(a):
    f0 = N_QKV + N_GATE
    return jnp.concatenate([a[..., :N_QKV], a[..., f0:f0 + FOX_HEADS], a[..., N_QKV:f0]], axis=-1)


def _unpack_gathered(gathered):
    flat = gathered.reshape(N_DEV, -1)
    layers = []
    off = 0
    for _ in range(DEPTH):
        w = {}
        for name, (k, n), axis in BIG:
            size = k * n // N_DEV
            part = flat[:, off:off + size]
            off += size
            if axis == 1:
                w[name] = part.reshape(N_DEV, k, n // N_DEV).transpose(1, 0, 2).reshape(k, n)
            else:
                w[name] = part.reshape(k, n)
        layers.append(w)
    return layers, flat[:, off:off + 2 * DEPTH * 3 * (D_FF // N_DEV)]


def _shard_slices(full, axis):
    k, n = full.shape
    if axis == 1:
        return full.reshape(k, N_DEV, n // N_DEV).transpose(1, 0, 2).reshape(N_DEV, -1)
    return full.reshape(N_DEV, -1)


def _c_layouts(c):
    s_len = c.shape[0]
    c8 = c[:, :FOX_HEADS]
    cq = jnp.repeat(c8, HEAD_DIM, axis=1)
    ck = jnp.pad(c8.T.reshape(N_PAIR, 2, s_len), ((0, 0), (0, 6), (0, 0)))
    return cq, ck


def _layer_fwd(h, w, p):
    zq = _linear(h, w["wq"], bias=p["bq"], out_dtype=BF16, name="z_qkv", tn=768)
    zg = _linear(h, w["wg"], bias=p["bg"], name="z_gate", tn=768)
    c = _cumsum_logf(zg)
    cq, ck = _c_layouts(c)
    attn_a, lse_a = _swa_fwd(zq, p["sinks"])
    attn_b, attn_b32, lse_b = _fox_fwd(zq, cq, ck)
    h1, u1, merged, ya, yb = _mixer_out(attn_a, attn_b, zg, h, w["w_proj_a"], w["w_proj_b"], w["w_out"],
                                        p["ln_mix_g"], p["ln_mix_b"])
    gu = _linear(h1, w["w_ffn_in"], name="ffn_in", tn=512)
    act = _conv_fwd(gu, p["conv_w"], p["conv_b"])
    u2, h2 = _linear_res_ln(act, w["w_ffn_out"], h1, p["ln_ffn_g"], p["ln_ffn_b"], name="ffn_out_ln")
    saved = dict(h=h, zq=zq, zg=zg, cq=cq, ck=ck, attn_a=attn_a, lse_a=lse_a, attn_b=attn_b, attn_b32=attn_b32, lse_b=lse_b, h1=h1,
                 u1=u1, merged=merged, ya=ya, yb=yb, gu=gu, act=act, u2=u2)
    return h2, saved


def _layer_bwd(dh2, sv, w, p):
    s_len = dh2.shape[0]
    du2, d_ffn_g, d_ffn_b = _ln_bwd(dh2, sv["u2"], p["ln_ffn_g"], name="ln_ffn_bwd")
    dact = _linear(du2, w["w_ffn_out"], trans_b=True, name="d_act", tn=1408)
    g_ffn_out = _linear_tn(sv["act"], du2, name="g_w_ffn_out", tk=1408, tn=1024)
    dgate, dup, dcw = _conv_bwd(sv["gu"], dact, p["conv_w"], p["conv_b"])
    dgu = jnp.concatenate([dgate, dup], axis=1)
    dh1 = _linear(dgu, w["w_ffn_in"], trans_b=True, res=du2, res_scale=ALPHA, name="d_h1", tn=512)
    g_ffn_in = _linear_tn(sv["h1"], dgu, name="g_w_ffn_in", tn=512)
    du1, d_mix_g, d_mix_b = _ln_bwd(dh1, sv["u1"], p["ln_mix_g"], name="ln_mix_bwd")
    dmerged = _linear(du1, w["w_out"], trans_b=True, name="d_merged", tn=1024)
    g_out = _linear_tn(sv["merged"], du1, name="g_w_out", tn=1024)
    dya, dyb, dga, dgb = _gate_bwd(dmerged, sv["ya"], sv["yb"], sv["zg"])
    dattn_a = _linear(dya, w["w_proj_a"], trans_b=True, out_dtype=BF16, name="d_attn_a", tn=512)
    dattn_b = _linear(dyb, w["w_proj_b"], trans_b=True, out_dtype=BF16, name="d_attn_b", tn=512)
    g_proj_a = _linear_tn(sv["attn_a"], dya, name="g_w_proj_a", tk=512, tn=1024)
    g_proj_b = _linear_tn(sv["attn_b"], dyb, name="g_w_proj_b", tk=512, tn=1024)
    dq_a, dk_a, dv_a, dsinks = _swa_bwd(sv["zq"], p["sinks"], sv["attn_a"], dattn_a, sv["lse_a"])
    stats = _fox_stats(sv["attn_b32"], dattn_b, sv["cq"], sv["lse_b"])
    dq_b, dk_b, dv_b, dck = _fox_bwd(sv["zq"], dattn_b, stats, sv["ck"])
    dc = jnp.pad(dck[:, :2, :].reshape(FOX_HEADS, s_len).T, ((0, 0), (0, LANE - FOX_HEADS)))
    df = _forget_bwd(dc, sv["zg"])
    dz = jnp.concatenate([dq_a, dk_a.astype(BF16), dv_a.astype(BF16), dq_b.astype(BF16), dk_b, dv_b, dga, dgb, df,
                          jnp.zeros((s_len, F_PAD - LANE), BF16)], axis=1)
    dh = _linear(dz, w["w_in_p"], trans_b=True, res=du1, res_scale=ALPHA, name="d_h", tn=512)
    g_in = _unpad_cols_z(_linear_tn(sv["h"], dz, name="g_w_in", tn=768))
    g_b_in = _unpad_cols_z(_colsum(dz, name="g_b_in"))
    big = dict(w_in=g_in, w_proj_a=g_proj_a, w_proj_b=g_proj_b, w_out=g_out, w_ffn_in=g_ffn_in, w_ffn_out=g_ffn_out)
    small = dict(ln_mix_g=d_mix_g, ln_mix_b=d_mix_b, b_in=g_b_in, attn_sinks=dsinks[:, :SWA_HEADS],
                 ln_ffn_g=d_ffn_g, ln_ffn_b=d_ffn_b, conv_w=dcw[:3], conv_b=dcw[3:4])
    return dh, big, small


def _local_step(x, target, layers, reps):
    ws, ps = [], []
    for w, r in zip(layers, reps):
        w_in_p = _pad_cols_z(w["w_in"])
        ws.append(dict(w, w_in_p=w_in_p, wq=w_in_p[:, :N_QKV], wg=w_in_p[:, N_QKV:]))
        b_p = _pad_cols_z(r["b_in"].reshape(1, N_IN))
        ps.append(dict(
            bq=b_p[:, :N_QKV], bg=b_p[:, N_QKV:],
            sinks=jnp.pad(r["attn_sinks"].reshape(1, SWA_HEADS), ((0, 0), (0, LANE - SWA_HEADS))),
            ln_mix_g=r["ln_mix_g"].reshape(1, D_MODEL), ln_mix_b=r["ln_mix_b"].reshape(1, D_MODEL),
            ln_ffn_g=r["ln_ffn_g"].reshape(1, D_MODEL), ln_ffn_b=r["ln_ffn_b"].reshape(1, D_MODEL),
            conv_w=jnp.pad(r["conv_w"], ((0, 5), (0, 0))), conv_b=r["conv_b"].reshape(1, D_FF)))
    h = x
    saved = []
    for w, p in zip(ws, ps):
        h, sv = _layer_fwd(h, w, p)
        saved.append(sv)
    dh, loss = _loss_head(h, target)
    bigs, smalls = [None] * DEPTH, [None] * DEPTH
    for l in reversed(range(DEPTH)):
        dh, bigs[l], smalls[l] = _layer_bwd(dh, saved[l], ws[l], ps[l])
    return loss[0, 0], dh, bigs, smalls


def kernel(x, ln_mix_g, ln_mix_b, w_in, b_in, attn_sinks, w_proj_a, w_proj_b, w_out, ln_ffn_g, ln_ffn_b, w_ffn_in, conv_w, conv_b, w_ffn_out, loss_target, m_ln_mix_g, m_ln_mix_b, m_w_in, m_b_in, m_attn_sinks, m_w_proj_a, m_w_proj_b, m_w_out, m_ln_ffn_g, m_ln_ffn_b, m_w_ffn_in, m_conv_w, m_conv_b, m_w_ffn_out, v_ln_mix_g, v_ln_mix_b, v_w_in, v_b_in, v_attn_sinks, v_w_proj_a, v_w_proj_b, v_w_out, v_ln_ffn_g, v_ln_ffn_b, v_w_ffn_in, v_conv_w, v_conv_b, v_w_ffn_out):
    wts = dict(ln_mix_g=ln_mix_g, ln_mix_b=ln_mix_b, w_in=w_in, b_in=b_in, attn_sinks=attn_sinks, w_proj_a=w_proj_a,
               w_proj_b=w_proj_b, w_out=w_out, ln_ffn_g=ln_ffn_g, ln_ffn_b=ln_ffn_b, w_ffn_in=w_ffn_in,
               conv_w=conv_w, conv_b=conv_b, w_ffn_out=w_ffn_out)
    mom = dict(ln_mix_g=m_ln_mix_g, ln_mix_b=m_ln_mix_b, w_in=m_w_in, b_in=m_b_in, attn_sinks=m_attn_sinks,
               w_proj_a=m_w_proj_a, w_proj_b=m_w_proj_b, w_out=m_w_out, ln_ffn_g=m_ln_ffn_g, ln_ffn_b=m_ln_ffn_b,
               w_ffn_in=m_w_ffn_in, conv_w=m_conv_w, conv_b=m_conv_b, w_ffn_out=m_w_ffn_out)
    vel = dict(ln_mix_g=v_ln_mix_g, ln_mix_b=v_ln_mix_b, w_in=v_w_in, b_in=v_b_in, attn_sinks=v_attn_sinks,
               w_proj_a=v_w_proj_a, w_proj_b=v_w_proj_b, w_out=v_w_out, ln_ffn_g=v_ln_ffn_g, ln_ffn_b=v_ln_ffn_b,
               w_ffn_in=v_w_ffn_in, conv_w=v_conv_w, conv_b=v_conv_b, w_ffn_out=v_w_ffn_out)
    names = list(wts)
    big_names = [n for n, _, _ in BIG]
    me = 4 * lax.axis_index("x") + 2 * lax.axis_index("y") + lax.axis_index("c")
    cw_shard = D_FF // N_DEV

    def pack_big(tree):
        return jnp.concatenate([tree[n][l].reshape(-1) for l in range(DEPTH) for n in big_names])

    conv_bits = lax.bitcast_convert_type(conv_w.reshape(-1), BF16).reshape(-1)
    shard = jnp.concatenate([pack_big(wts).astype(BF16), conv_bits])
    gather_rows = -(-shard.shape[0] // (16 * LANE)) * 16
    gathered = _exchange(_to_rows(shard, gather_rows), True, "gather_weights")
    layers, conv_bits_all = _unpack_gathered(gathered)
    conv_all = lax.bitcast_convert_type(conv_bits_all.reshape(N_DEV, DEPTH * 3 * cw_shard, 2), F32)
    conv_full = conv_all.reshape(N_DEV, DEPTH, 3, cw_shard).transpose(1, 2, 0, 3).reshape(DEPTH, 3, D_FF)
    reps = [dict(b_in=b_in[l], attn_sinks=attn_sinks[l], ln_mix_g=ln_mix_g[l], ln_mix_b=ln_mix_b[l],
                 ln_ffn_g=ln_ffn_g[l], ln_ffn_b=ln_ffn_b[l], conv_w=conv_full[l], conv_b=conv_b[l])
            for l in range(DEPTH)]

    loss_part, grad_x, bigs, smalls = _local_step(x[0], loss_target[0], layers, reps)

    big_send = jnp.concatenate([_shard_slices(bigs[l][n], axis) for l in range(DEPTH) for n, _, axis in BIG], axis=1)
    big_send = jnp.pad(big_send, ((0, 0), (0, BIG_ROWS * LANE - BIG_ELEMS))).reshape(N_DEV, BIG_ROWS, LANE)
    small_vec = jnp.concatenate([smalls[l][n].reshape(-1) for l in range(DEPTH) for n, _ in SMALL]
                                + [loss_part.reshape(1)])
    small_send = jnp.broadcast_to(_to_rows(small_vec, SMALL_ROWS)[None], (N_DEV, SMALL_ROWS, LANE))
    recv = _exchange(jnp.concatenate([big_send, small_send], axis=1), False, "exchange_grads")
    gsum = _sum8(recv)

    d_big, m_big, v_big = _adamw(_to_rows(pack_big(wts), BIG_ROWS), gsum, _to_rows(pack_big(mom), BIG_ROWS),
                                 _to_rows(pack_big(vel), BIG_ROWS), "adamw_big")
    small_flat = gsum[BIG_ROWS:].reshape(-1)
    g_small = {}
    off = 0
    for l in range(DEPTH):
        for n, size in SMALL:
            g_small.setdefault(n, []).append(small_flat[off:off + size])
            off += size
    loss = small_flat[off]
    g_small = {n: jnp.stack(v).reshape((DEPTH, 3, D_FF) if n == "conv_w" else wts[n].shape)
               for n, v in g_small.items()}
    g_small["conv_w"] = lax.dynamic_slice_in_dim(g_small["conv_w"], me * cw_shard, cw_shard, axis=2)
    small_names = [n for n, _ in SMALL]

    def pack_small(tree):
        return _to_rows(jnp.concatenate([tree[n].reshape(-1) for n in small_names]), SMALL_ROWS)

    d_small, m_small, v_small = _adamw(pack_small(wts), pack_small(g_small), pack_small(mom), pack_small(vel),
                                       "adamw_small")

    def unpack(big_rows, small_rows):
        out = {}
        flat = big_rows.reshape(-1)
        off = 0
        per = {n: [] for n in big_names}
        for l in range(DEPTH):
            for n in big_names:
                size = wts[n][l].size
                per[n].append(flat[off:off + size].reshape(wts[n][l].shape))
                off += size
        out.update({n: jnp.stack(v) for n, v in per.items()})
        flat = small_rows.reshape(-1)
        off = 0
        for n in small_names:
            out[n] = flat[off:off + wts[n].size].reshape(wts[n].shape)
            off += wts[n].size
        return out

    grads = unpack(gsum, pack_small(g_small))
    deltas = unpack(d_big, d_small)
    new_m = unpack(m_big, m_small)
    new_v = unpack(v_big, v_small)
    return (loss, grad_x[None], *[grads[n] for n in names], *[deltas[n] for n in names],
            *[new_m[n] for n in names], *[new_v[n] for n in names])
```

```python
import functools

import jax
import jax.numpy as jnp
from jax import lax
from jax.experimental import pallas as pl
from jax.experimental.pallas import tpu as pltpu

F32 = jnp.float32
BF16 = jnp.bfloat16
MESH = pl.DeviceIdType.MESH

N_DEV = 8
DEPTH = 2
D_MODEL = 1024
HEAD_DIM = 64
SWA_Q = 512
SWA_KV = 128
FOX_W = 512
FOX_HEADS = 8
SWA_HEADS = 8
D_FF = 2816
N_IN = 4360
N_QKV = SWA_Q + 2 * SWA_KV + 3 * FOX_W
N_GATE = 2 * D_MODEL
F_PAD = 256
N_ZG = N_GATE + F_PAD
N_ZP = N_QKV + N_ZG
LN_EPS = 1e-5
NEG_INF = -1e30
ALPHA = (2 * DEPTH) ** 0.25
SCALE = HEAD_DIM ** -0.5
SLOPES = tuple(2.0 ** (-8.0 * (h + 1) / SWA_HEADS) for h in range(SWA_HEADS))

ADAM_LR = 0.001
ADAM_B1 = 0.9
ADAM_B2 = 0.999
ADAM_EPS = 1e-08
ADAM_WD = 0.01
ADAM_STEP = 10

LANE = 128
VMEM_LIMIT = 56 * 1024 * 1024

BIG = (("w_in", (D_MODEL, N_IN), 1), ("w_proj_a", (SWA_Q, D_MODEL), 1), ("w_proj_b", (FOX_W, D_MODEL), 1),
       ("w_out", (D_MODEL, D_MODEL), 0), ("w_ffn_in", (D_MODEL, 2 * D_FF), 1), ("w_ffn_out", (D_FF, D_MODEL), 0))
SMALL = (("ln_mix_g", D_MODEL), ("ln_mix_b", D_MODEL), ("b_in", N_IN), ("attn_sinks", SWA_HEADS),
         ("ln_ffn_g", D_MODEL), ("ln_ffn_b", D_MODEL), ("conv_w", 3 * D_FF), ("conv_b", D_FF))
ROW_BLOCK = 512
SMALL_LAYER_ROWS = -(-(sum(n for _, n in SMALL) + 1) // (8 * LANE)) * 8
SMALL_ROWS = ROW_BLOCK
FF_CHUNK = 2 * D_FF // N_DEV
N_CHUNK = D_FF // FF_CHUNK


def _div(n, cap, unit):
    if n <= cap:
        return n
    best = None
    for t in range(unit, cap + 1, unit):
        if n % t == 0:
            best = t
    assert best is not None, (n, cap, unit)
    return best


def _params(*sem):
    return pltpu.CompilerParams(dimension_semantics=sem, vmem_limit_bytes=VMEM_LIMIT)


def _peer(r):
    x, y, c = lax.axis_index("x"), lax.axis_index("y"), lax.axis_index("c")
    px = 1 - x if (r >> 2) & 1 else x
    py = 1 - y if (r >> 1) & 1 else y
    pc = 1 - c if r & 1 else c
    return (px, py, pc), 4 * px + 2 * py + pc


def _exchange(tensors, name):
    n = len(tensors)
    gathers = [g for _, g in tensors]

    def body(*refs):
        x_refs, out_refs = refs[:n], refs[n:2 * n]
        send_sems, recv_sems, local_sems = refs[2 * n:]
        _, me = _peer(0)

        def src(t, idx):
            return x_refs[t] if gathers[t] else x_refs[t].at[idx]

        def remote(r, t, slab):
            peer, pid = _peer(r)
            return pltpu.make_async_remote_copy(src_ref=src(t, pid), dst_ref=out_refs[t].at[me if slab is None else pid],
                                                send_sem=send_sems.at[r - 1, t], recv_sem=recv_sems.at[r - 1, t],
                                                device_id=peer, device_id_type=MESH)

        local = [pltpu.make_async_copy(src(t, me), out_refs[t].at[me], local_sems.at[t]) for t in range(n)]
        for cp in local:
            cp.start()
        sent = [remote(r, t, None) for r in range(1, N_DEV) for t in range(n)]
        for cp in sent:
            cp.start()
        for r in range(1, N_DEV):
            for t in range(n):
                remote(r, t, "theirs").wait_recv()
        for cp in sent:
            cp.wait_send()
        for cp in local:
            cp.wait()

    any_spec = pl.BlockSpec(memory_space=pl.ANY)
    return pl.pallas_call(
        body, name=name,
        out_shape=[jax.ShapeDtypeStruct((N_DEV,) + (x.shape if g else x.shape[1:]), x.dtype) for x, g in tensors],
        in_specs=[any_spec] * n, out_specs=[any_spec] * n,
        scratch_shapes=[pltpu.SemaphoreType.DMA((N_DEV - 1, n)), pltpu.SemaphoreType.DMA((N_DEV - 1, n)),
                        pltpu.SemaphoreType.DMA((n,))],
    )(*[x for x, _ in tensors])


def _linear(a, b, *, name, trans_b=False, bias=None, res=None, res_scale=1.0, out_dtype=F32, tm=512, tn=640):
    m, k = a.shape
    n = b.shape[0] if trans_b else b.shape[1]
    tm = _div(m, tm, 8)
    tn = _div(n, tn, LANE)
    dn = (((1,), (1,)), ((), ())) if trans_b else (((1,), (0,)), ((), ()))

    def body(*refs):
        a_ref, b_ref = refs[0], refs[1]
        rest = list(refs[2:])
        bias_ref = rest.pop(0) if bias is not None else None
        res_ref = rest.pop(0) if res is not None else None
        o_ref = rest.pop(0)
        acc = lax.dot_general(a_ref[...].astype(BF16), b_ref[...].astype(BF16), dn, preferred_element_type=F32)
        if bias_ref is not None:
            acc = acc + bias_ref[...]
        if res_ref is not None:
            acc = acc + res_scale * res_ref[...].astype(F32)
        o_ref[...] = acc.astype(out_dtype)

    in_specs = [pl.BlockSpec((tm, k), lambda i, j: (i, 0)),
                pl.BlockSpec((tn, k), lambda i, j: (j, 0)) if trans_b else pl.BlockSpec((k, tn), lambda i, j: (0, j))]
    args = [a, b]
    if bias is not None:
        in_specs.append(pl.BlockSpec((1, tn), lambda i, j: (0, j)))
        args.append(bias)
    if res is not None:
        in_specs.append(pl.BlockSpec((tm, tn), lambda i, j: (i, j)))
        args.append(res)
    return pl.pallas_call(
        body, name=name, grid=(m // tm, n // tn), in_specs=in_specs,
        out_specs=pl.BlockSpec((tm, tn), lambda i, j: (i, j)),
        out_shape=jax.ShapeDtypeStruct((m, n), out_dtype),
        compiler_params=_params("parallel", "arbitrary"),
    )(*args)


def _linear_tn(a, g, *, name, tk=1024, tn=640, tm=512):
    m, k = a.shape
    n = g.shape[1]
    tk = _div(k, tk, LANE)
    tn = _div(n, tn, LANE)
    tm = _div(m, tm, 8)
    steps = m // tm

    def body(a_ref, g_ref, o_ref, acc_ref):
        s = pl.program_id(2)

        @pl.when(s == 0)
        def _():
            acc_ref[...] = jnp.zeros_like(acc_ref)

        acc_ref[...] += lax.dot_general(a_ref[...].astype(BF16), g_ref[...].astype(BF16), (((0,), (0,)), ((), ())),
                                        preferred_element_type=F32)

        @pl.when(s == steps - 1)
        def _():
            o_ref[...] = acc_ref[...]

    return pl.pallas_call(
        body, name=name, grid=(k // tk, n // tn, steps),
        in_specs=[pl.BlockSpec((tm, tk), lambda i, j, s: (s, i)), pl.BlockSpec((tm, tn), lambda i, j, s: (s, j))],
        out_specs=pl.BlockSpec((tk, tn), lambda i, j, s: (i, j)),
        out_shape=jax.ShapeDtypeStruct((k, n), F32),
        scratch_shapes=[pltpu.VMEM((tk, tn), F32)],
        compiler_params=_params("parallel", "parallel", "arbitrary"),
    )(a, g)


def _colsum(g, *, name, tm=512):
    m, n = g.shape
    tm = _div(m, tm, 8)

    def body(g_ref, o_ref):
        @pl.when(pl.program_id(0) == 0)
        def _():
            o_ref[...] = jnp.zeros_like(o_ref)

        o_ref[...] += jnp.sum(g_ref[...].astype(F32), axis=0, keepdims=True)

    return pl.pallas_call(
        body, name=name, grid=(m // tm,),
        in_specs=[pl.BlockSpec((tm, n), lambda i: (i, 0))],
        out_specs=pl.BlockSpec((1, n), lambda i: (0, 0)),
        out_shape=jax.ShapeDtypeStruct((1, n), F32),
        compiler_params=_params("arbitrary"),
    )(g)


def _ln(u, g, b):
    mu = jnp.mean(u, axis=-1, keepdims=True)
    d = u - mu
    var = jnp.mean(d * d, axis=-1, keepdims=True)
    return d * lax.rsqrt(var + LN_EPS) * g + b


def _ln_bwd(dy, u, g, *, name, tm=256):
    m, d = u.shape
    tm = _div(m, tm, 8)

    def body(dy_ref, u_ref, g_ref, du_ref, dg_ref, db_ref):
        @pl.when(pl.program_id(0) == 0)
        def _():
            dg_ref[...] = jnp.zeros_like(dg_ref)
            db_ref[...] = jnp.zeros_like(db_ref)

        dy = dy_ref[...]
        uu = u_ref[...]
        mu = jnp.mean(uu, axis=-1, keepdims=True)
        dd = uu - mu
        rstd = lax.rsqrt(jnp.mean(dd * dd, axis=-1, keepdims=True) + LN_EPS)
        xhat = dd * rstd
        dxh = dy * g_ref[...]
        m1 = jnp.mean(dxh, axis=-1, keepdims=True)
        m2 = jnp.mean(dxh * xhat, axis=-1, keepdims=True)
        du_ref[...] = rstd * (dxh - m1 - xhat * m2)
        dg_ref[...] += jnp.sum(dy * xhat, axis=0, keepdims=True)
        db_ref[...] += jnp.sum(dy, axis=0, keepdims=True)

    row = pl.BlockSpec((tm, d), lambda i: (i, 0))
    vec = pl.BlockSpec((1, d), lambda i: (0, 0))
    return pl.pallas_call(
        body, name=name, grid=(m // tm,), in_specs=[row, row, vec], out_specs=[row, vec, vec],
        out_shape=[jax.ShapeDtypeStruct((m, d), F32), jax.ShapeDtypeStruct((1, d), F32),
                   jax.ShapeDtypeStruct((1, d), F32)],
        compiler_params=_params("arbitrary"),
    )(dy, u, g)


def _linear_res_ln(a, w, res, g, b, *, name, tm=256):
    m, k = a.shape
    d = w.shape[1]
    tm = _div(m, tm, 8)

    def body(a_ref, w_ref, res_ref, g_ref, b_ref, u_ref, y_ref):
        u = ALPHA * res_ref[...] + jnp.dot(a_ref[...].astype(BF16), w_ref[...], preferred_element_type=F32)
        u_ref[...] = u
        y_ref[...] = _ln(u, g_ref[...], b_ref[...])

    row = pl.BlockSpec((tm, d), lambda i: (i, 0))
    vec = pl.BlockSpec((1, d), lambda i: (0, 0))
    return pl.pallas_call(
        body, name=name, grid=(m // tm,),
        in_specs=[pl.BlockSpec((tm, k), lambda i: (i, 0)), pl.BlockSpec((k, d), lambda i: (0, 0)), row, vec, vec],
        out_specs=[row, row],
        out_shape=[jax.ShapeDtypeStruct((m, d), F32), jax.ShapeDtypeStruct((m, d), F32)],
        compiler_params=_params("parallel"),
    )(a, w, res, g, b)


def _tri(n, upper):
    r = lax.broadcasted_iota(jnp.int32, (n, n), 0)
    c = lax.broadcasted_iota(jnp.int32, (n, n), 1)
    return jnp.where((c >= r) if upper else (c <= r), 1.0, 0.0).astype(F32)


def _cumsum_logf(zg):
    s = zg.shape[0]
    nb = s // LANE
    fcol = N_GATE // LANE

    def body(f_ref, c_ref, carry_ref):
        @pl.when(pl.program_id(0) == 0)
        def _():
            carry_ref[...] = jnp.zeros_like(carry_ref)

        f = f_ref[...]
        logf = jnp.minimum(f, 0.0) - jnp.log(1.0 + jnp.exp(-jnp.abs(f)))
        c = jnp.dot(_tri(LANE, False), logf, precision=lax.Precision.HIGHEST, preferred_element_type=F32)
        c = c + carry_ref[0:1, :]
        c_ref[...] = c
        carry_ref[...] = jnp.broadcast_to(c[LANE - 1:LANE, :], carry_ref.shape)

    return pl.pallas_call(
        body, name="cumsum_logf", grid=(nb,),
        in_specs=[pl.BlockSpec((LANE, LANE), lambda i: (i, fcol))],
        out_specs=pl.BlockSpec((LANE, LANE), lambda i: (i, 0)),
        out_shape=jax.ShapeDtypeStruct((s, LANE), F32),
        scratch_shapes=[pltpu.VMEM((8, LANE), F32)],
        compiler_params=_params("arbitrary"),
    )(zg)


def _forget_bwd(dc, zg):
    s = zg.shape[0]
    nb = s // LANE
    fcol = N_GATE // LANE

    def body(dc_ref, f_ref, o_ref, carry_ref):
        @pl.when(pl.program_id(0) == 0)
        def _():
            carry_ref[...] = jnp.zeros_like(carry_ref)

        dc = dc_ref[...]
        dlogf = jnp.dot(_tri(LANE, True), dc, precision=lax.Precision.HIGHEST, preferred_element_type=F32)
        dlogf = dlogf + carry_ref[0:1, :]
        o_ref[...] = (dlogf * jax.nn.sigmoid(-f_ref[...])).astype(BF16)
        carry_ref[...] = jnp.broadcast_to(dlogf[0:1, :], carry_ref.shape)

    return pl.pallas_call(
        body, name="forget_bwd", grid=(nb,),
        in_specs=[pl.BlockSpec((LANE, LANE), lambda i: (nb - 1 - i, 0)),
                  pl.BlockSpec((LANE, LANE), lambda i: (nb - 1 - i, fcol))],
        out_specs=pl.BlockSpec((LANE, LANE), lambda i: (nb - 1 - i, 0)),
        out_shape=jax.ShapeDtypeStruct((s, LANE), BF16),
        scratch_shapes=[pltpu.VMEM((8, LANE), F32)],
        compiler_params=_params("arbitrary"),
    )(dc, zg)


KA_COL = SWA_Q // LANE
VA_COL = KA_COL + 1


def _half_masks():
    lane = lax.broadcasted_iota(jnp.int32, (1, LANE), 1)
    hi = lane >= HEAD_DIM
    return (jnp.logical_not(hi), hi)


def _both_halves(x, sel):
    xs = jnp.where(sel, x, 0.0)
    return xs + pltpu.roll(xs, HEAD_DIM, 1)


def _swa_geometry(i):
    r = lax.broadcasted_iota(jnp.int32, (LANE, 2 * LANE), 0)
    c = lax.broadcasted_iota(jnp.int32, (LANE, 2 * LANE), 1)
    dist = r + LANE - c
    valid = (dist >= 0) & (dist < LANE) & ((c >= LANE) | (i > 0))
    return valid, -dist.astype(F32)


def _swa_specs(nb):
    prev = lambda i: jnp.maximum(i - 1, 0)
    return [pl.BlockSpec((LANE, SWA_Q), lambda i: (i, 0)),
            pl.BlockSpec((LANE, LANE), lambda i: (i, KA_COL)), pl.BlockSpec((LANE, LANE), lambda i: (i, VA_COL)),
            pl.BlockSpec((LANE, LANE), lambda i: (prev(i), KA_COL)),
            pl.BlockSpec((LANE, LANE), lambda i: (prev(i), VA_COL))]


def _swa_fwd(zq, sinks):
    s_len = zq.shape[0]
    nb = s_len // LANE

    def body(q_ref, kc_ref, vc_ref, kp_ref, vp_ref, sink_ref, o_ref, lse_ref):
        i = pl.program_id(0)
        halves = _half_masks()
        lane = lax.broadcasted_iota(jnp.int32, (1, LANE), 1)
        valid, negdist = _swa_geometry(i)
        kcat = jnp.concatenate([kp_ref[...], kc_ref[...]], axis=0).astype(F32)
        vcat = jnp.concatenate([vp_ref[...], vc_ref[...]], axis=0).astype(F32)
        lse_acc = jnp.zeros((LANE, LANE), F32)
        for hk in range(2):
            kb = _both_halves(kcat, halves[hk]).astype(BF16)
            vb = _both_halves(vcat, halves[hk])
            v_e = [jnp.where(halves[e], vb, 0.0).astype(BF16) for e in range(2)]
            for pp in range(2):
                p = 2 * hk + pp
                qp = q_ref[:, p * LANE:(p + 1) * LANE]
                acc = jnp.zeros((LANE, LANE), F32)
                for e in range(2):
                    hq = 2 * p + e
                    qh = jnp.where(halves[e], qp, jnp.zeros_like(qp))
                    s = lax.dot_general(qh, kb, (((1,), (1,)), ((), ())), preferred_element_type=F32) * SCALE
                    s = jnp.where(valid, s + SLOPES[hq] * negdist, NEG_INF)
                    sink = sink_ref[0:1, hq:hq + 1]
                    m = jnp.maximum(jnp.max(s, axis=1, keepdims=True), sink)
                    pe = jnp.exp(s - m)
                    den = jnp.sum(pe, axis=1, keepdims=True) + jnp.exp(sink - m)
                    acc = acc + jnp.dot((pe / den).astype(BF16), v_e[e], preferred_element_type=F32)
                    lse_acc = jnp.where(lane == hq, m + jnp.log(den), lse_acc)
                o_ref[:, p * LANE:(p + 1) * LANE] = acc.astype(BF16)
        lse_ref[...] = lse_acc

    return pl.pallas_call(
        body, name="swa_fwd", grid=(nb,),
        in_specs=_swa_specs(nb) + [pl.BlockSpec((1, LANE), lambda i: (0, 0))],
        out_specs=[pl.BlockSpec((LANE, SWA_Q), lambda i: (i, 0)), pl.BlockSpec((LANE, LANE), lambda i: (i, 0))],
        out_shape=[jax.ShapeDtypeStruct((s_len, SWA_Q), BF16), jax.ShapeDtypeStruct((s_len, LANE), F32)],
        compiler_params=_params("parallel"),
    )(zq, zq, zq, zq, zq, sinks)


def _swa_bwd(zq, sinks, o, do, lse):
    s_len = zq.shape[0]
    nb = s_len // LANE

    def body(q_ref, kc_ref, vc_ref, kp_ref, vp_ref, sink_ref, o_ref, do_ref, lse_ref, dq_ref, dk_ref, dv_ref, ds_ref):
        i = pl.program_id(0)
        halves = _half_masks()
        lane = lax.broadcasted_iota(jnp.int32, (1, LANE), 1)
        valid, negdist = _swa_geometry(i)
        kcat = jnp.concatenate([kp_ref[...], kc_ref[...]], axis=0).astype(F32)
        vcat = jnp.concatenate([vp_ref[...], vc_ref[...]], axis=0).astype(F32)
        lse_all = lse_ref[...]
        dk_tot = jnp.zeros((2 * LANE, LANE), F32)
        dv_tot = jnp.zeros((2 * LANE, LANE), F32)
        dsink = jnp.zeros((1, LANE), F32)
        for hk in range(2):
            kb = _both_halves(kcat, halves[hk])
            vb = _both_halves(vcat, halves[hk])
            k_e = [jnp.where(halves[e], kb, 0.0).astype(BF16) for e in range(2)]
            v_e = [jnp.where(halves[e], vb, 0.0).astype(BF16) for e in range(2)]
            kb = kb.astype(BF16)
            dk_acc = jnp.zeros((2 * LANE, LANE), F32)
            dv_acc = jnp.zeros((2 * LANE, LANE), F32)
            for pp in range(2):
                p = 2 * hk + pp
                cols = slice(p * LANE, (p + 1) * LANE)
                qp = q_ref[:, cols]
                dop = do_ref[:, cols]
                prod = dop.astype(F32) * o_ref[:, cols].astype(F32)
                dq_acc = jnp.zeros((LANE, LANE), F32)
                for e in range(2):
                    hq = 2 * p + e
                    qh = jnp.where(halves[e], qp, jnp.zeros_like(qp))
                    doh = jnp.where(halves[e], dop, jnp.zeros_like(dop))
                    delta = jnp.sum(jnp.where(halves[e], prod, 0.0), axis=1, keepdims=True)
                    lse_h = lse_all[:, hq:hq + 1]
                    s = lax.dot_general(qh, kb, (((1,), (1,)), ((), ())), preferred_element_type=F32) * SCALE
                    s = jnp.where(valid, s + SLOPES[hq] * negdist, NEG_INF)
                    pr = jnp.exp(s - lse_h)
                    dp = lax.dot_general(doh, v_e[e], (((1,), (1,)), ((), ())), preferred_element_type=F32)
                    ds = pr * (dp - delta)
                    sink = sink_ref[0:1, hq:hq + 1]
                    dsink_h = -jnp.sum(jnp.exp(sink - lse_h) * delta, axis=0, keepdims=True)
                    dsink = dsink + jnp.where(lane == hq, dsink_h, 0.0)
                    dsb = (ds * SCALE).astype(BF16)
                    dq_acc = dq_acc + jnp.dot(dsb, k_e[e], preferred_element_type=F32)
                    dk_acc = dk_acc + lax.dot_general(dsb, qh, (((0,), (0,)), ((), ())), preferred_element_type=F32)
                    dv_acc = dv_acc + lax.dot_general(pr.astype(BF16), doh, (((0,), (0,)), ((), ())),
                                                      preferred_element_type=F32)
                dq_ref[:, cols] = dq_acc.astype(BF16)
            dk_tot = dk_tot + jnp.where(halves[hk], dk_acc + pltpu.roll(dk_acc, HEAD_DIM, 1), 0.0)
            dv_tot = dv_tot + jnp.where(halves[hk], dv_acc + pltpu.roll(dv_acc, HEAD_DIM, 1), 0.0)

        @pl.when(i == 0)
        def _():
            ds_ref[...] = jnp.zeros_like(ds_ref)

        ds_ref[...] += dsink
        cur = pl.ds(pl.multiple_of(i * LANE, LANE), LANE)
        dk_ref[cur, :] = dk_tot[LANE:, :]
        dv_ref[cur, :] = dv_tot[LANE:, :]

        @pl.when(i > 0)
        def _():
            prv = pl.ds(pl.multiple_of((i - 1) * LANE, LANE), LANE)
            dk_ref[prv, :] += dk_tot[:LANE, :]
            dv_ref[prv, :] += dv_tot[:LANE, :]

    blk512 = pl.BlockSpec((LANE, SWA_Q), lambda i: (i, 0))
    full = pl.BlockSpec((s_len, LANE), lambda i: (0, 0))
    vec = pl.BlockSpec((1, LANE), lambda i: (0, 0))
    return pl.pallas_call(
        body, name="swa_bwd", grid=(nb,),
        in_specs=_swa_specs(nb) + [vec, blk512, blk512, pl.BlockSpec((LANE, LANE), lambda i: (i, 0))],
        out_specs=[blk512, full, full, vec],
        out_shape=[jax.ShapeDtypeStruct((s_len, SWA_Q), BF16), jax.ShapeDtypeStruct((s_len, LANE), F32),
                   jax.ShapeDtypeStruct((s_len, LANE), F32), jax.ShapeDtypeStruct((1, LANE), F32)],
        compiler_params=_params("arbitrary"),
    )(zq, zq, zq, zq, zq, sinks, o, do, lse)


QB_COL = (SWA_Q + 2 * SWA_KV) // LANE
KB_COL = QB_COL + FOX_W // LANE
VB_COL = KB_COL + FOX_W // LANE
N_PAIR = FOX_HEADS // 2


def _causal(t):
    r = lax.broadcasted_iota(jnp.int32, (t, t), 0)
    c = lax.broadcasted_iota(jnp.int32, (t, t), 1)
    return r >= c


def _fox_fwd(zq, cq, ck, t_cap=256):
    s_len = zq.shape[0]
    t = _div(s_len, t_cap, LANE)
    nq = s_len // t

    def body(q_ref, k_ref, v_ref, cq_ref, ck_ref, o_ref, o32_ref, lse_ref):
        i = pl.program_id(1)
        halves = _half_masks()
        q = q_ref[...]
        q_e = [jnp.where(halves[e], q, jnp.zeros_like(q)) for e in range(2)]
        cq_e = [cq_ref[:, e * HEAD_DIM:e * HEAD_DIM + 1] for e in range(2)]
        tri = _causal(t)

        def step(j, carry, diag):
            m, l, acc = carry
            rows = pl.ds(pl.multiple_of(j * t, t), t)
            ks = k_ref[rows, :]
            vs = v_ref[rows, :]
            m_new, l_new, a_e, pv = [], [], [], []
            for e in range(2):
                ck_row = ck_ref[0, e:e + 1, rows]
                s = lax.dot_general(q_e[e], ks, (((1,), (1,)), ((), ())), preferred_element_type=F32) * SCALE
                s = s + (cq_e[e] - ck_row)
                if diag:
                    s = jnp.where(tri, s, NEG_INF)
                mn = jnp.maximum(m[e], jnp.max(s, axis=1, keepdims=True))
                a = jnp.exp(m[e] - mn)
                pe = jnp.exp(s - mn)
                m_new.append(mn)
                l_new.append(a * l[e] + jnp.sum(pe, axis=1, keepdims=True))
                a_e.append(a)
                ve = jnp.where(halves[e], vs, jnp.zeros_like(vs))
                pv.append(jnp.dot(pe.astype(BF16), ve, preferred_element_type=F32))
            acc = acc * jnp.where(halves[1], a_e[1], a_e[0]) + pv[0] + pv[1]
            return (tuple(m_new), tuple(l_new), acc)

        col = jnp.full((t, 1), NEG_INF, F32)
        zero = jnp.zeros((t, 1), F32)
        carry = ((col, col), (zero, zero), jnp.zeros((t, LANE), F32))
        carry = lax.fori_loop(0, i, lambda j, c: step(j, c, False), carry)
        m, l, acc = step(i, carry, True)
        out = acc / jnp.where(halves[1], l[1], l[0])
        o_ref[...] = out.astype(BF16)
        o32_ref[...] = out
        lse_ref[...] = jnp.where(halves[1], m[1] + jnp.log(l[1]), m[0] + jnp.log(l[0]))

    return pl.pallas_call(
        body, name="fox_fwd", grid=(N_PAIR, nq),
        in_specs=[pl.BlockSpec((t, LANE), lambda p, i: (i, QB_COL + p)),
                  pl.BlockSpec((s_len, LANE), lambda p, i: (0, KB_COL + p)),
                  pl.BlockSpec((s_len, LANE), lambda p, i: (0, VB_COL + p)),
                  pl.BlockSpec((t, LANE), lambda p, i: (i, p)),
                  pl.BlockSpec((1, 8, s_len), lambda p, i: (p, 0, 0))],
        out_specs=[pl.BlockSpec((t, LANE), lambda p, i: (i, p)), pl.BlockSpec((t, LANE), lambda p, i: (i, p)),
                   pl.BlockSpec((t, LANE), lambda p, i: (i, p))],
        out_shape=[jax.ShapeDtypeStruct((s_len, FOX_W), BF16), jax.ShapeDtypeStruct((s_len, FOX_W), F32),
                   jax.ShapeDtypeStruct((s_len, FOX_W), F32)],
        compiler_params=_params("parallel", "parallel"),
    )(zq, zq, zq, cq, ck)


def _fox_stats(o, do, cq, lse, tm=256):
    s_len = o.shape[0]
    tm = _div(s_len, tm, 8)

    def body(o_ref, do_ref, cq_ref, lse_ref, st_ref):
        lane = lax.broadcasted_iota(jnp.int32, (1, LANE), 1)
        prod = o_ref[...].astype(F32) * do_ref[...].astype(F32)
        diff = cq_ref[...] - lse_ref[...]
        for p in range(N_PAIR):
            st = jnp.zeros((tm, LANE), F32)
            for e in range(2):
                lo = (2 * p + e) * HEAD_DIM
                st = jnp.where(lane == e, diff[:, lo:lo + 1], st)
                delta = jnp.sum(prod[:, lo:lo + HEAD_DIM], axis=1, keepdims=True)
                st = jnp.where(lane == 2 + e, delta, st)
            st_ref[:, p * LANE:(p + 1) * LANE] = st

    row = pl.BlockSpec((tm, FOX_W), lambda i: (i, 0))
    return pl.pallas_call(
        body, name="fox_stats", grid=(s_len // tm,), in_specs=[row, row, row, row], out_specs=row,
        out_shape=jax.ShapeDtypeStruct((s_len, FOX_W), F32),
        compiler_params=_params("parallel"),
    )(o, do, cq, lse)


def _fox_bwd(zq, do, stats, ck, t_cap=256):
    s_len = zq.shape[0]
    t = _div(s_len, t_cap, LANE)
    nq = s_len // t

    def body(q_ref, do_ref, st_ref, k_ref, v_ref, ck_ref, dq_ref, dk_ref, dv_ref, dck_ref):
        j = pl.program_id(1)
        halves = _half_masks()

        @pl.when(j == 0)
        def _():
            dq_ref[...] = jnp.zeros_like(dq_ref)

        ks = k_ref[...]
        vs = v_ref[...]
        k_e = [jnp.where(halves[e], ks, jnp.zeros_like(ks)) for e in range(2)]
        v_e = [jnp.where(halves[e], vs, jnp.zeros_like(vs)) for e in range(2)]
        ck_e = [ck_ref[0, e:e + 1, :] for e in range(2)]
        tri = _causal(t)

        def step(i, carry, diag):
            dk, dv, dc = carry
            rows = pl.ds(pl.multiple_of(i * t, t), t)
            q = q_ref[rows, :]
            dout = do_ref[rows, :]
            st = st_ref[rows, :]
            dq = jnp.zeros((t, LANE), F32)
            dc_new = []
            for e in range(2):
                qe = jnp.where(halves[e], q, jnp.zeros_like(q))
                doe = jnp.where(halves[e], dout, jnp.zeros_like(dout))
                s = lax.dot_general(qe, ks, (((1,), (1,)), ((), ())), preferred_element_type=F32) * SCALE
                s = s + (st[:, e:e + 1] - ck_e[e])
                if diag:
                    s = jnp.where(tri, s, NEG_INF)
                pr = jnp.exp(s)
                dp = lax.dot_general(doe, v_e[e], (((1,), (1,)), ((), ())), preferred_element_type=F32)
                ds = pr * (dp - st[:, 2 + e:3 + e])
                dc_new.append(dc[e] + jnp.sum(ds, axis=0, keepdims=True))
                dsb = (ds * SCALE).astype(BF16)
                dv = dv + lax.dot_general(pr.astype(BF16), doe, (((0,), (0,)), ((), ())), preferred_element_type=F32)
                dk = dk + lax.dot_general(dsb, qe, (((0,), (0,)), ((), ())), preferred_element_type=F32)
                dq = dq + jnp.dot(dsb, k_e[e], preferred_element_type=F32)
            dq_ref[rows, :] += dq
            return (dk, dv, tuple(dc_new))

        zrow = jnp.zeros((1, t), F32)
        carry = (jnp.zeros((t, LANE), F32), jnp.zeros((t, LANE), F32), (zrow, zrow))
        carry = step(j, carry, True)
        dk, dv, dc = lax.fori_loop(j + 1, nq, lambda i, c: step(i, c, False), carry)
        dk_ref[...] = dk.astype(BF16)
        dv_ref[...] = dv.astype(BF16)
        row = lax.broadcasted_iota(jnp.int32, (8, t), 0)
        dck_ref[0] = jnp.where(row == 0, -dc[0], jnp.where(row == 1, -dc[1], 0.0))

    full = lambda col: pl.BlockSpec((s_len, LANE), lambda p, j: (0, col(p)))
    return pl.pallas_call(
        body, name="fox_bwd", grid=(N_PAIR, nq),
        in_specs=[full(lambda p: QB_COL + p), full(lambda p: p), full(lambda p: p),
                  pl.BlockSpec((t, LANE), lambda p, j: (j, KB_COL + p)),
                  pl.BlockSpec((t, LANE), lambda p, j: (j, VB_COL + p)),
                  pl.BlockSpec((1, 8, t), lambda p, j: (p, 0, j))],
        out_specs=[full(lambda p: p), pl.BlockSpec((t, LANE), lambda p, j: (j, p)),
                   pl.BlockSpec((t, LANE), lambda p, j: (j, p)), pl.BlockSpec((1, 8, t), lambda p, j: (p, 0, j))],
        out_shape=[jax.ShapeDtypeStruct((s_len, FOX_W), F32), jax.ShapeDtypeStruct((s_len, FOX_W), BF16),
                   jax.ShapeDtypeStruct((s_len, FOX_W), BF16), jax.ShapeDtypeStruct((N_PAIR, 8, s_len), F32)],
        compiler_params=_params("arbitrary", "arbitrary"),
    )(zq, do, stats, zq, zq, ck)


def _mixer_out(attn_a, attn_b, zg, h, wpa, wpb, wout, g, b, tm=256):
    m = h.shape[0]
    tm = _div(m, tm, 8)

    def body(a_ref, b_ref, ga_ref, gb_ref, h_ref, wpa_ref, wpb_ref, wout_ref, g_ref, bb_ref,
             h1_ref, u_ref, mg_ref, ya_ref, yb_ref):
        ya = jnp.dot(a_ref[...], wpa_ref[...], preferred_element_type=F32)
        yb = jnp.dot(b_ref[...], wpb_ref[...], preferred_element_type=F32)
        merged = (jax.nn.sigmoid(ga_ref[...]) * ya + jax.nn.sigmoid(gb_ref[...]) * yb).astype(BF16)
        u = ALPHA * h_ref[...] + jnp.dot(merged, wout_ref[...], preferred_element_type=F32)
        u_ref[...] = u
        h1_ref[...] = _ln(u, g_ref[...], bb_ref[...])
        mg_ref[...] = merged
        ya_ref[...] = ya.astype(BF16)
        yb_ref[...] = yb.astype(BF16)

    row = pl.BlockSpec((tm, D_MODEL), lambda i: (i, 0))
    att = pl.BlockSpec((tm, SWA_Q), lambda i: (i, 0))
    vec = pl.BlockSpec((1, D_MODEL), lambda i: (0, 0))
    wsm = pl.BlockSpec((SWA_Q, D_MODEL), lambda i: (0, 0))
    return pl.pallas_call(
        body, name="mixer_out", grid=(m // tm,),
        in_specs=[att, att, row, pl.BlockSpec((tm, D_MODEL), lambda i: (i, 1)), row, wsm, wsm,
                  pl.BlockSpec((D_MODEL, D_MODEL), lambda i: (0, 0)), vec, vec],
        out_specs=[row, row, row, row, row],
        out_shape=[jax.ShapeDtypeStruct((m, D_MODEL), F32), jax.ShapeDtypeStruct((m, D_MODEL), F32),
                   jax.ShapeDtypeStruct((m, D_MODEL), BF16), jax.ShapeDtypeStruct((m, D_MODEL), BF16),
                   jax.ShapeDtypeStruct((m, D_MODEL), BF16)],
        compiler_params=_params("parallel"),
    )(attn_a, attn_b, zg, zg, h, wpa, wpb, wout, g, b)


def _gate_bwd(dmerged, ya, yb, zg, tm=256):
    m = dmerged.shape[0]
    tm = _div(m, tm, 8)

    def body(dm_ref, ya_ref, yb_ref, ga_ref, gb_ref, dya_ref, dyb_ref, dga_ref, dgb_ref):
        dm = dm_ref[...]
        for y_ref, g_ref, dy_ref, dg_ref in ((ya_ref, ga_ref, dya_ref, dga_ref), (yb_ref, gb_ref, dyb_ref, dgb_ref)):
            sg = jax.nn.sigmoid(g_ref[...])
            dy_ref[...] = (dm * sg).astype(BF16)
            dg_ref[...] = (dm * y_ref[...].astype(F32) * sg * (1.0 - sg)).astype(BF16)

    row = pl.BlockSpec((tm, D_MODEL), lambda i: (i, 0))
    out = jax.ShapeDtypeStruct((m, D_MODEL), BF16)
    return pl.pallas_call(
        body, name="gate_bwd", grid=(m // tm,),
        in_specs=[row, row, row, row, pl.BlockSpec((tm, D_MODEL), lambda i: (i, 1))],
        out_specs=[row, row, row, row], out_shape=[out, out, out, out],
        compiler_params=_params("parallel"),
    )(dmerged, ya, yb, zg, zg)


def _shift_down(x, k, halo, first):
    rows = lax.broadcasted_iota(jnp.int32, (x.shape[0], 1), 0)
    y = pltpu.roll(x, k, 0)
    for r in range(k):
        fill = jnp.where(first, 0.0, halo[8 - k + r:8 - k + r + 1, :])
        y = jnp.where(rows == r, fill, y)
    return y


def _shift_up(x, k, halo, last):
    n = x.shape[0]
    rows = lax.broadcasted_iota(jnp.int32, (n, 1), 0)
    y = pltpu.roll(x, n - k, 0)
    for r in range(k):
        fill = jnp.where(last, 0.0, halo[r:r + 1, :])
        y = jnp.where(rows == n - k + r, fill, y)
    return y


def _conv_act(gate, gate_m1, gate_m2, cw, cb):
    return cb + cw[0:1, :] * gate_m2 + cw[1:2, :] * gate_m1 + cw[2:3, :] * gate


def _ffn_in(h1, wfi, tm=256):
    s_len = h1.shape[0]
    tm = _div(s_len, tm, 8)

    def body(a_ref, w_ref, o_ref):
        a = a_ref[...].astype(BF16)
        for g in range(2):
            for c in range(N_CHUNK):
                o_ref[c, g] = jnp.dot(a, w_ref[N_CHUNK * g + c], preferred_element_type=F32)

    return pl.pallas_call(
        body, name="ffn_in", grid=(s_len // tm,),
        in_specs=[pl.BlockSpec((tm, D_MODEL), lambda i: (i, 0)),
                  pl.BlockSpec((N_DEV, D_MODEL, FF_CHUNK), lambda i: (0, 0, 0))],
        out_specs=pl.BlockSpec((N_CHUNK, 2, tm, FF_CHUNK), lambda i: (0, 0, i, 0)),
        out_shape=jax.ShapeDtypeStruct((N_CHUNK, 2, s_len, FF_CHUNK), F32),
        compiler_params=_params("parallel"),
    )(h1, wfi)


def _conv_fwd(gu, cw, cb, tm=256):
    s_len = gu.shape[2]
    tm = _div(s_len, tm, 8)
    hb = tm // 8

    def body(gu_ref, gp_ref, cw_ref, cb_ref, o_ref):
        first = pl.program_id(1) == 0
        gate = gu_ref[0, 0]
        halo = gp_ref[0, 0]
        conv = _conv_act(gate, _shift_down(gate, 1, halo, first), _shift_down(gate, 2, halo, first),
                         cw_ref[0], cb_ref[0])
        o_ref[0] = (conv * jax.nn.sigmoid(conv) * gu_ref[0, 1]).astype(BF16)

    return pl.pallas_call(
        body, name="conv_fwd", grid=(N_CHUNK, s_len // tm),
        in_specs=[pl.BlockSpec((1, 2, tm, FF_CHUNK), lambda c, i: (c, 0, i, 0)),
                  pl.BlockSpec((1, 1, 8, FF_CHUNK), lambda c, i: (c, 0, jnp.maximum(i * hb - 1, 0), 0)),
                  pl.BlockSpec((1, 8, FF_CHUNK), lambda c, i: (c, 0, 0)),
                  pl.BlockSpec((1, 1, FF_CHUNK), lambda c, i: (c, 0, 0))],
        out_specs=pl.BlockSpec((1, tm, FF_CHUNK), lambda c, i: (c, i, 0)),
        out_shape=jax.ShapeDtypeStruct((N_CHUNK, s_len, FF_CHUNK), BF16),
        compiler_params=_params("parallel", "parallel"),
    )(gu, gu, cw, cb)


def _ffn_out_ln(act, wfo, res, g, b, tm=256):
    s_len = res.shape[0]
    tm = _div(s_len, tm, 8)

    def body(a_ref, w_ref, res_ref, g_ref, b_ref, u_ref, y_ref):
        u = ALPHA * res_ref[...]
        for c in range(N_CHUNK):
            u = u + jnp.dot(a_ref[c], w_ref[c], preferred_element_type=F32)
        u_ref[...] = u
        y_ref[...] = _ln(u, g_ref[...], b_ref[...])

    row = pl.BlockSpec((tm, D_MODEL), lambda i: (i, 0))
    vec = pl.BlockSpec((1, D_MODEL), lambda i: (0, 0))
    return pl.pallas_call(
        body, name="ffn_out_ln", grid=(s_len // tm,),
        in_specs=[pl.BlockSpec((N_CHUNK, tm, FF_CHUNK), lambda i: (0, i, 0)),
                  pl.BlockSpec((N_CHUNK, FF_CHUNK, D_MODEL), lambda i: (0, 0, 0)), row, vec, vec],
        out_specs=[row, row],
        out_shape=[jax.ShapeDtypeStruct((s_len, D_MODEL), F32), jax.ShapeDtypeStruct((s_len, D_MODEL), F32)],
        compiler_params=_params("parallel"),
    )(act, wfo, res, g, b)


def _d_act(du, wfo, tm=256):
    s_len = du.shape[0]
    tm = _div(s_len, tm, 8)

    def body(du_ref, w_ref, o_ref):
        du_b = du_ref[...].astype(BF16)
        for c in range(N_CHUNK):
            o_ref[c] = lax.dot_general(du_b, w_ref[c], (((1,), (1,)), ((), ())), preferred_element_type=F32)

    return pl.pallas_call(
        body, name="d_act", grid=(s_len // tm,),
        in_specs=[pl.BlockSpec((tm, D_MODEL), lambda i: (i, 0)),
                  pl.BlockSpec((N_CHUNK, FF_CHUNK, D_MODEL), lambda i: (0, 0, 0))],
        out_specs=pl.BlockSpec((N_CHUNK, tm, FF_CHUNK), lambda i: (0, i, 0)),
        out_shape=jax.ShapeDtypeStruct((N_CHUNK, s_len, FF_CHUNK), F32),
        compiler_params=_params("parallel"),
    )(du, wfo)


def _g_w_ffn_out(act, du, tm=512):
    s_len = du.shape[0]
    tm = _div(s_len, tm, 8)
    steps = s_len // tm

    def body(a_ref, g_ref, o_ref, acc_ref):
        s = pl.program_id(1)

        @pl.when(s == 0)
        def _():
            acc_ref[...] = jnp.zeros_like(acc_ref)

        acc_ref[...] += lax.dot_general(a_ref[0], g_ref[...].astype(BF16), (((0,), (0,)), ((), ())),
                                        preferred_element_type=F32)

        @pl.when(s == steps - 1)
        def _():
            o_ref[0] = acc_ref[...]

    return pl.pallas_call(
        body, name="g_w_ffn_out", grid=(N_CHUNK, steps),
        in_specs=[pl.BlockSpec((1, tm, FF_CHUNK), lambda c, s: (c, s, 0)),
                  pl.BlockSpec((tm, D_MODEL), lambda c, s: (s, 0))],
        out_specs=pl.BlockSpec((1, FF_CHUNK, D_MODEL), lambda c, s: (c, 0, 0)),
        out_shape=jax.ShapeDtypeStruct((N_CHUNK, FF_CHUNK, D_MODEL), F32),
        scratch_shapes=[pltpu.VMEM((FF_CHUNK, D_MODEL), F32)],
        compiler_params=_params("parallel", "arbitrary"),
    )(act, du)


def _d_h1(dgu, wfi, res, tm=256):
    s_len = res.shape[0]
    tm = _div(s_len, tm, 8)

    def body(a_ref, w_ref, res_ref, o_ref):
        acc = ALPHA * res_ref[...]
        for g in range(2):
            for c in range(N_CHUNK):
                acc = acc + lax.dot_general(a_ref[c, g], w_ref[N_CHUNK * g + c], (((1,), (1,)), ((), ())),
                                            preferred_element_type=F32)
        o_ref[...] = acc

    row = pl.BlockSpec((tm, D_MODEL), lambda i: (i, 0))
    return pl.pallas_call(
        body, name="d_h1", grid=(s_len // tm,),
        in_specs=[pl.BlockSpec((N_CHUNK, 2, tm, FF_CHUNK), lambda i: (0, 0, i, 0)),
                  pl.BlockSpec((N_DEV, D_MODEL, FF_CHUNK), lambda i: (0, 0, 0)), row],
        out_specs=row, out_shape=jax.ShapeDtypeStruct((s_len, D_MODEL), F32),
        compiler_params=_params("parallel"),
    )(dgu, wfi, res)


def _g_w_ffn_in(h1, dgu, tm=512):
    s_len = h1.shape[0]
    tm = _div(s_len, tm, 8)
    steps = s_len // tm

    def body(a_ref, g_ref, o_ref, acc_ref):
        s = pl.program_id(1)

        @pl.when(s == 0)
        def _():
            acc_ref[...] = jnp.zeros_like(acc_ref)

        acc_ref[...] += lax.dot_general(a_ref[...].astype(BF16), g_ref[0, 0], (((0,), (0,)), ((), ())),
                                        preferred_element_type=F32)

        @pl.when(s == steps - 1)
        def _():
            o_ref[0] = acc_ref[...]

    return pl.pallas_call(
        body, name="g_w_ffn_in", grid=(N_DEV, steps),
        in_specs=[pl.BlockSpec((tm, D_MODEL), lambda d, s: (s, 0)),
                  pl.BlockSpec((1, 1, tm, FF_CHUNK), lambda d, s: (d % N_CHUNK, d // N_CHUNK, s, 0))],
        out_specs=pl.BlockSpec((1, D_MODEL, FF_CHUNK), lambda d, s: (d, 0, 0)),
        out_shape=jax.ShapeDtypeStruct((N_DEV, D_MODEL, FF_CHUNK), F32),
        scratch_shapes=[pltpu.VMEM((D_MODEL, FF_CHUNK), F32)],
        compiler_params=_params("parallel", "arbitrary"),
    )(h1, dgu)


def _conv_bwd(gu, dact, cw, cb, tm=256):
    s_len = gu.shape[2]
    tm = _div(s_len, tm, 8)
    nrow = s_len // tm
    hb = tm // 8

    def dconv_of(conv, up, da):
        sg = jax.nn.sigmoid(conv)
        return da * up * (sg * (1.0 + conv * (1.0 - sg)))

    def body(gu_ref, gp_ref, gun_ref, da_ref, dan_ref, cw_ref, cb_ref, dgu_ref, dcw_ref):
        i = pl.program_id(1)
        first = i == 0
        last = i == nrow - 1
        cw = cw_ref[0]
        cb = cb_ref[0]
        gate = gu_ref[0, 0]
        halo = gp_ref[0, 0]
        g_m1 = _shift_down(gate, 1, halo, first)
        g_m2 = _shift_down(gate, 2, halo, first)
        conv = _conv_act(gate, g_m1, g_m2, cw, cb)
        da = da_ref[0]
        sg = jax.nn.sigmoid(conv)
        dgu_ref[0, 1] = (da * conv * sg).astype(BF16)
        dconv = dconv_of(conv, gu_ref[0, 1], da)
        gate_n = gun_ref[0, 0]
        tail = gate[tm - 8:, :]
        conv_n = _conv_act(gate_n, _shift_down(gate_n, 1, tail, False), _shift_down(gate_n, 2, tail, False), cw, cb)
        dconv_n = dconv_of(conv_n, gun_ref[0, 1], dan_ref[0])
        dgate = (cw[2:3, :] * dconv + cw[1:2, :] * _shift_up(dconv, 1, dconv_n, last)
                 + cw[0:1, :] * _shift_up(dconv, 2, dconv_n, last))
        dgu_ref[0, 0] = dgate.astype(BF16)

        @pl.when(first)
        def _():
            dcw_ref[...] = jnp.zeros_like(dcw_ref)

        row = lax.broadcasted_iota(jnp.int32, (8, 1), 0)
        part = jnp.zeros((8, FF_CHUNK), F32)
        for r, term in enumerate((dconv * g_m2, dconv * g_m1, dconv * gate, dconv)):
            part = jnp.where(row == r, jnp.sum(term, axis=0, keepdims=True), part)
        dcw_ref[0] += part

    nxt = lambda i: jnp.minimum((i + 1) * hb, s_len // 8 - 1)
    main = pl.BlockSpec((1, 2, tm, FF_CHUNK), lambda c, i: (c, 0, i, 0))
    return pl.pallas_call(
        body, name="conv_bwd", grid=(N_CHUNK, nrow),
        in_specs=[main,
                  pl.BlockSpec((1, 1, 8, FF_CHUNK), lambda c, i: (c, 0, jnp.maximum(i * hb - 1, 0), 0)),
                  pl.BlockSpec((1, 2, 8, FF_CHUNK), lambda c, i: (c, 0, nxt(i), 0)),
                  pl.BlockSpec((1, tm, FF_CHUNK), lambda c, i: (c, i, 0)),
                  pl.BlockSpec((1, 8, FF_CHUNK), lambda c, i: (c, nxt(i), 0)),
                  pl.BlockSpec((1, 8, FF_CHUNK), lambda c, i: (c, 0, 0)),
                  pl.BlockSpec((1, 1, FF_CHUNK), lambda c, i: (c, 0, 0))],
        out_specs=[main, pl.BlockSpec((1, 8, FF_CHUNK), lambda c, i: (c, 0, 0))],
        out_shape=[jax.ShapeDtypeStruct((N_CHUNK, 2, s_len, FF_CHUNK), BF16),
                   jax.ShapeDtypeStruct((N_CHUNK, 8, FF_CHUNK), F32)],
        compiler_params=_params("parallel", "arbitrary"),
    )(gu, gu, gu, dact, dact, cw, cb)


def _loss_head(y, target, tm=256):
    m, d = y.shape
    tm = _div(m, tm, 8)

    def body(y_ref, t_ref, dy_ref, loss_ref):
        @pl.when(pl.program_id(0) == 0)
        def _():
            loss_ref[...] = jnp.zeros_like(loss_ref)

        err = y_ref[...] - t_ref[...]
        dy_ref[...] = err / d
        loss_ref[...] += 0.5 * jnp.sum(jnp.sum(err * err, axis=1, keepdims=True) / d, axis=0, keepdims=True)

    row = pl.BlockSpec((tm, d), lambda i: (i, 0))
    return pl.pallas_call(
        body, name="loss_head", grid=(m // tm,), in_specs=[row, row],
        out_specs=[row, pl.BlockSpec((8, LANE), lambda i: (0, 0))],
        out_shape=[jax.ShapeDtypeStruct((m, d), F32), jax.ShapeDtypeStruct((8, LANE), F32)],
        compiler_params=_params("arbitrary"),
    )(y, target)


def _sum_devices(r_ref):
    acc = r_ref[0]
    for d in range(1, N_DEV):
        acc = acc + r_ref[d]
    return acc


def _sum8(recv):
    rows = recv.shape[1]
    tr = _div(rows, ROW_BLOCK, 8)

    def body(r_ref, o_ref):
        o_ref[...] = _sum_devices(r_ref)

    return pl.pallas_call(
        body, name="sum8", grid=(rows // tr,),
        in_specs=[pl.BlockSpec((N_DEV, tr, LANE), lambda i: (0, i, 0))],
        out_specs=pl.BlockSpec((tr, LANE), lambda i: (i, 0)),
        out_shape=jax.ShapeDtypeStruct((rows, LANE), F32),
        compiler_params=_params("parallel"),
    )(recv)


def _adamw_math(w, g, m, v):
    m = ADAM_B1 * m + (1.0 - ADAM_B1) * g
    v = ADAM_B2 * v + (1.0 - ADAM_B2) * (g * g)
    m_hat = m / (1.0 - ADAM_B1 ** ADAM_STEP)
    v_hat = v / (1.0 - ADAM_B2 ** ADAM_STEP)
    return -ADAM_LR * (m_hat / (jnp.sqrt(v_hat) + ADAM_EPS) + ADAM_WD * w), m, v


def _adamw_rows(w, g, m, v, name):
    rows = w.shape[0]
    tr = _div(rows, ROW_BLOCK, 8)

    def body(w_ref, g_ref, m_ref, v_ref, d_ref, mo_ref, vo_ref):
        d_ref[...], mo_ref[...], vo_ref[...] = _adamw_math(w_ref[...], g_ref[...], m_ref[...], v_ref[...])

    blk = pl.BlockSpec((tr, LANE), lambda i: (i, 0))
    out = jax.ShapeDtypeStruct((rows, LANE), F32)
    return pl.pallas_call(
        body, name=name, grid=(rows // tr,), in_specs=[blk, blk, blk, blk], out_specs=[blk, blk, blk],
        out_shape=[out, out, out], compiler_params=_params("parallel"),
    )(w, g, m, v)


def _adamw_shard(recv, w, m, v, layer, prev, name):
    _, k, n = recv.shape
    tk = _div(k, 128, 8)

    def body(r_ref, w_ref, m_ref, v_ref, *rest):
        g_ref, d_ref, mo_ref, vo_ref = rest[-4:]
        g = _sum_devices(r_ref)
        g_ref[0] = g
        d_ref[0], mo_ref[0], vo_ref[0] = _adamw_math(w_ref[0], g, m_ref[0], v_ref[0])

    blk = pl.BlockSpec((1, tk, n), lambda i: (layer, i, 0))
    out = jax.ShapeDtypeStruct((DEPTH, k, n), F32)
    carried = [] if prev is None else list(prev)
    return pl.pallas_call(
        body, name=name, grid=(k // tk,),
        in_specs=[pl.BlockSpec((N_DEV, tk, n), lambda i: (0, i, 0)), blk, blk, blk]
        + [pl.BlockSpec(memory_space=pl.ANY)] * len(carried),
        out_specs=[blk, blk, blk, blk], out_shape=[out, out, out, out],
        input_output_aliases={4 + j: j for j in range(len(carried))},
        compiler_params=_params("parallel"),
    )(recv, w, m, v, *carried)


def _to_rows(flat, rows):
    flat = flat.reshape(-1)
    return jnp.pad(flat, (0, rows * LANE - flat.shape[0])).reshape(rows, LANE)


def _pad_cols_z(a):
    f0 = N_QKV
    g0 = N_QKV + FOX_HEADS
    pad = jnp.zeros(a.shape[:-1] + (F_PAD - FOX_HEADS,), a.dtype)
    return jnp.concatenate([a[..., :f0], a[..., g0:], a[..., f0:g0], pad], axis=-1)


def _unpad_cols_z(a):
    f0 = N_QKV + N_GATE
    return jnp.concatenate([a[..., :N_QKV], a[..., f0:f0 + FOX_HEADS], a[..., N_QKV:f0]], axis=-1)


def _shards_to_cols(g):
    _, k, n = g.shape
    return g.transpose(1, 0, 2).reshape(k, N_DEV * n)


def _cols_to_shards(full):
    k, n = full.shape
    return full.reshape(k, N_DEV, n // N_DEV).transpose(1, 0, 2)


def _c_layouts(c):
    s_len = c.shape[0]
    c8 = c[:, :FOX_HEADS]
    cq = jnp.repeat(c8, HEAD_DIM, axis=1)
    ck = jnp.pad(c8.T.reshape(N_PAIR, 2, s_len), ((0, 0), (0, 6), (0, 0)))
    return cq, ck


def _layer_fwd(h, w, p):
    zq = _linear(h, w["wq"], bias=p["bq"], out_dtype=BF16, name="z_qkv", tn=768)
    zg = _linear(h, w["wg"], bias=p["bg"], name="z_gate", tn=768)
    c = _cumsum_logf(zg)
    cq, ck = _c_layouts(c)
    attn_a, lse_a = _swa_fwd(zq, p["sinks"])
    attn_b, attn_b32, lse_b = _fox_fwd(zq, cq, ck)
    h1, u1, merged, ya, yb = _mixer_out(attn_a, attn_b, zg, h, w["w_proj_a"], w["w_proj_b"], w["w_out"],
                                        p["ln_mix_g"], p["ln_mix_b"])
    gu = _ffn_in(h1, w["w_ffn_in"])
    act = _conv_fwd(gu, p["conv_w"], p["conv_b"])
    u2, h2 = _ffn_out_ln(act, w["w_ffn_out"], h1, p["ln_ffn_g"], p["ln_ffn_b"])
    saved = dict(h=h, zq=zq, zg=zg, cq=cq, ck=ck, attn_a=attn_a, lse_a=lse_a, attn_b=attn_b, attn_b32=attn_b32, lse_b=lse_b, h1=h1,
                 u1=u1, merged=merged, ya=ya, yb=yb, gu=gu, act=act, u2=u2)
    return h2, saved


def _layer_bwd(dh2, sv, w, p):
    s_len = dh2.shape[0]
    du2, d_ffn_g, d_ffn_b = _ln_bwd(dh2, sv["u2"], p["ln_ffn_g"], name="ln_ffn_bwd")
    dact = _d_act(du2, w["w_ffn_out"])
    g_ffn_out = _g_w_ffn_out(sv["act"], du2)
    dgu, dcw = _conv_bwd(sv["gu"], dact, p["conv_w"], p["conv_b"])
    dcw = dcw.transpose(1, 0, 2).reshape(8, D_FF)
    dh1 = _d_h1(dgu, w["w_ffn_in"], du2)
    g_ffn_in = _g_w_ffn_in(sv["h1"], dgu)
    du1, d_mix_g, d_mix_b = _ln_bwd(dh1, sv["u1"], p["ln_mix_g"], name="ln_mix_bwd")
    dmerged = _linear(du1, w["w_out"], trans_b=True, name="d_merged", tn=1024)
    g_out = _linear_tn(sv["merged"], du1, name="g_w_out", tn=1024)
    dya, dyb, dga, dgb = _gate_bwd(dmerged, sv["ya"], sv["yb"], sv["zg"])
    dattn_a = _linear(dya, w["w_proj_a"], trans_b=True, out_dtype=BF16, name="d_attn_a", tn=512)
    dattn_b = _linear(dyb, w["w_proj_b"], trans_b=True, out_dtype=BF16, name="d_attn_b", tn=512)
    g_proj_a = _linear_tn(sv["attn_a"], dya, name="g_w_proj_a", tk=512, tn=1024)
    g_proj_b = _linear_tn(sv["attn_b"], dyb, name="g_w_proj_b", tk=512, tn=1024)
    dq_a, dk_a, dv_a, dsinks = _swa_bwd(sv["zq"], p["sinks"], sv["attn_a"], dattn_a, sv["lse_a"])
    stats = _fox_stats(sv["attn_b32"], dattn_b, sv["cq"], sv["lse_b"])
    dq_b, dk_b, dv_b, dck = _fox_bwd(sv["zq"], dattn_b, stats, sv["ck"])
    dc = jnp.pad(dck[:, :2, :].reshape(FOX_HEADS, s_len).T, ((0, 0), (0, LANE - FOX_HEADS)))
    df = _forget_bwd(dc, sv["zg"])
    dz = jnp.concatenate([dq_a, dk_a.astype(BF16), dv_a.astype(BF16), dq_b.astype(BF16), dk_b, dv_b, dga, dgb, df,
                          jnp.zeros((s_len, F_PAD - LANE), BF16)], axis=1)
    dh = _linear(dz, w["w_in_p"], trans_b=True, res=du1, res_scale=ALPHA, name="d_h", tn=512)
    g_in = _unpad_cols_z(_linear_tn(sv["h"], dz, name="g_w_in", tn=768))
    g_b_in = _unpad_cols_z(_colsum(dz, name="g_b_in"))
    big = dict(w_in=_cols_to_shards(g_in), w_proj_a=_cols_to_shards(g_proj_a), w_proj_b=_cols_to_shards(g_proj_b),
               w_out=g_out.reshape(N_DEV, D_MODEL // N_DEV, D_MODEL), w_ffn_in=g_ffn_in,
               w_ffn_out=g_ffn_out.reshape(N_DEV, D_FF // N_DEV, D_MODEL))
    small = dict(ln_mix_g=d_mix_g, ln_mix_b=d_mix_b, b_in=g_b_in, attn_sinks=dsinks[:, :SWA_HEADS],
                 ln_ffn_g=d_ffn_g, ln_ffn_b=d_ffn_b, conv_w=dcw[:3], conv_b=dcw[3:4])
    return dh, big, small


def _layer_weights(w_in, w_proj_a, w_proj_b, w_out, w_ffn_in, w_ffn_out):
    w_in_p = _pad_cols_z(_shards_to_cols(w_in))
    return dict(w_in_p=w_in_p, wq=w_in_p[:, :N_QKV], wg=w_in_p[:, N_QKV:], w_proj_a=_shards_to_cols(w_proj_a),
                w_proj_b=_shards_to_cols(w_proj_b), w_out=w_out.reshape(D_MODEL, D_MODEL), w_ffn_in=w_ffn_in,
                w_ffn_out=w_ffn_out.reshape(N_CHUNK, FF_CHUNK, D_MODEL))


def _layer_params(r):
    b_p = _pad_cols_z(r["b_in"].reshape(1, N_IN))
    return dict(
        bq=b_p[:, :N_QKV], bg=b_p[:, N_QKV:],
        sinks=jnp.pad(r["attn_sinks"].reshape(1, SWA_HEADS), ((0, 0), (0, LANE - SWA_HEADS))),
        ln_mix_g=r["ln_mix_g"].reshape(1, D_MODEL), ln_mix_b=r["ln_mix_b"].reshape(1, D_MODEL),
        ln_ffn_g=r["ln_ffn_g"].reshape(1, D_MODEL), ln_ffn_b=r["ln_ffn_b"].reshape(1, D_MODEL),
        conv_w=jnp.pad(r["conv_w"], ((0, 5), (0, 0))).reshape(8, N_CHUNK, FF_CHUNK).transpose(1, 0, 2),
        conv_b=r["conv_b"].reshape(N_CHUNK, 1, FF_CHUNK))


def _local_step(x, target, ws, ps, after_layer):
    h = x
    saved = []
    for w, p in zip(ws, ps):
        h, sv = _layer_fwd(h, w, p)
        saved.append(sv)
    dh, loss = _loss_head(h, target)
    outs = [None] * DEPTH
    for l in reversed(range(DEPTH)):
        dh, big, small = _layer_bwd(dh, saved[l], ws[l], ps[l])
        outs[l] = after_layer(l, big, small, loss[0, 0])
    return dh, outs


def kernel(x, ln_mix_g, ln_mix_b, w_in, b_in, attn_sinks, w_proj_a, w_proj_b, w_out, ln_ffn_g, ln_ffn_b, w_ffn_in, conv_w, conv_b, w_ffn_out, loss_target, m_ln_mix_g, m_ln_mix_b, m_w_in, m_b_in, m_attn_sinks, m_w_proj_a, m_w_proj_b, m_w_out, m_ln_ffn_g, m_ln_ffn_b, m_w_ffn_in, m_conv_w, m_conv_b, m_w_ffn_out, v_ln_mix_g, v_ln_mix_b, v_w_in, v_b_in, v_attn_sinks, v_w_proj_a, v_w_proj_b, v_w_out, v_ln_ffn_g, v_ln_ffn_b, v_w_ffn_in, v_conv_w, v_conv_b, v_w_ffn_out):
    wts = dict(ln_mix_g=ln_mix_g, ln_mix_b=ln_mix_b, w_in=w_in, b_in=b_in, attn_sinks=attn_sinks, w_proj_a=w_proj_a,
               w_proj_b=w_proj_b, w_out=w_out, ln_ffn_g=ln_ffn_g, ln_ffn_b=ln_ffn_b, w_ffn_in=w_ffn_in,
               conv_w=conv_w, conv_b=conv_b, w_ffn_out=w_ffn_out)
    mom = dict(ln_mix_g=m_ln_mix_g, ln_mix_b=m_ln_mix_b, w_in=m_w_in, b_in=m_b_in, attn_sinks=m_attn_sinks,
               w_proj_a=m_w_proj_a, w_proj_b=m_w_proj_b, w_out=m_w_out, ln_ffn_g=m_ln_ffn_g, ln_ffn_b=m_ln_ffn_b,
               w_ffn_in=m_w_ffn_in, conv_w=m_conv_w, conv_b=m_conv_b, w_ffn_out=m_w_ffn_out)
    vel = dict(ln_mix_g=v_ln_mix_g, ln_mix_b=v_ln_mix_b, w_in=v_w_in, b_in=v_b_in, attn_sinks=v_attn_sinks,
               w_proj_a=v_w_proj_a, w_proj_b=v_w_proj_b, w_out=v_w_out, ln_ffn_g=v_ln_ffn_g, ln_ffn_b=v_ln_ffn_b,
               w_ffn_in=v_w_ffn_in, conv_w=v_conv_w, conv_b=v_conv_b, w_ffn_out=v_w_ffn_out)
    names = list(wts)
    big_names = [n for n, _, _ in BIG]
    small_names = [n for n, _ in SMALL]
    me = 4 * lax.axis_index("x") + 2 * lax.axis_index("y") + lax.axis_index("c")
    cw_shard = D_FF // N_DEV

    gathered = _exchange([(wts[n].astype(BF16), True) for n in big_names] + [(conv_w, True)], "gather_weights")
    conv_full = gathered[-1].transpose(1, 2, 0, 3).reshape(DEPTH, 3, D_FF)
    ws = [_layer_weights(*[g[:, l] for g in gathered[:-1]]) for l in range(DEPTH)]
    ps = [_layer_params(dict(b_in=b_in[l], attn_sinks=attn_sinks[l], ln_mix_g=ln_mix_g[l], ln_mix_b=ln_mix_b[l],
                             ln_ffn_g=ln_ffn_g[l], ln_ffn_b=ln_ffn_b[l], conv_w=conv_full[l], conv_b=conv_b[l]))
          for l in range(DEPTH)]

    def exchange_layer(l, big, small, loss_part):
        vec = jnp.concatenate([small[n].reshape(-1) for n in small_names] + [loss_part.reshape(1)])
        return _exchange([(big[n], False) for n in big_names] + [(_to_rows(vec, SMALL_LAYER_ROWS), True)],
                         "exchange_grads_%d" % l)

    grad_x, recv = _local_step(x[0], loss_target[0], ws, ps, exchange_layer)

    big_out = {}
    for t, n in enumerate(big_names):
        outs = None
        for l in reversed(range(DEPTH)):
            outs = _adamw_shard(recv[l][t], wts[n], mom[n], vel[n], l, outs, "adamw_%s_%d" % (n, l))
        big_out[n] = outs
    small_sum = [_sum8(recv[l][-1]).reshape(-1) for l in range(DEPTH)]
    g_small = {}
    off = 0
    for n, size in SMALL:
        g_small[n] = jnp.stack([small_sum[l][off:off + size] for l in range(DEPTH)])
        off += size
    loss = small_sum[0][off]
    g_small["conv_w"] = lax.dynamic_slice_in_dim(g_small["conv_w"].reshape(DEPTH, 3, D_FF), me * cw_shard, cw_shard,
                                                 axis=2)
    g_small = {n: g_small[n].reshape(wts[n].shape) for n in small_names}

    def pack_small(tree):
        return _to_rows(jnp.concatenate([tree[n].reshape(-1) for n in small_names]), SMALL_ROWS)

    small_out = (pack_small(g_small),) + tuple(_adamw_rows(pack_small(wts), pack_small(g_small), pack_small(mom),
                                                           pack_small(vel), "adamw_small"))

    def result(j):
        out = {n: big_out[n][j] for n in big_names}
        flat = small_out[j].reshape(-1)
        off = 0
        for n in small_names:
            out[n] = flat[off:off + wts[n].size].reshape(wts[n].shape)
            off += wts[n].size
        return [out[n] for n in names]

    return (loss, grad_x[None], *result(0), *result(1), *result(2), *result(3))
```

```python
import functools

import jax
import jax.numpy as jnp
from jax import lax
from jax.experimental import pallas as pl
from jax.experimental.pallas import tpu as pltpu

F32 = jnp.float32
BF16 = jnp.bfloat16
MESH = pl.DeviceIdType.MESH

N_DEV = 8
DEPTH = 2
D_MODEL = 1024
HEAD_DIM = 64
SWA_Q = 512
SWA_KV = 128
FOX_W = 512
FOX_HEADS = 8
SWA_HEADS = 8
D_FF = 2816
N_IN = 4360
N_QKV = SWA_Q + 2 * SWA_KV + 3 * FOX_W
N_GATE = 2 * D_MODEL
F_PAD = 256
N_ZG = N_GATE + F_PAD
N_ZP = N_QKV + N_ZG
LN_EPS = 1e-5
NEG_INF = -1e30
ALPHA = (2 * DEPTH) ** 0.25
SCALE = HEAD_DIM ** -0.5
SLOPES = tuple(2.0 ** (-8.0 * (h + 1) / SWA_HEADS) for h in range(SWA_HEADS))

ADAM_LR = 0.001
ADAM_B1 = 0.9
ADAM_B2 = 0.999
ADAM_EPS = 1e-08
ADAM_WD = 0.01
ADAM_STEP = 10

LANE = 128
VMEM_LIMIT = 56 * 1024 * 1024

BIG = (("w_in", (D_MODEL, N_IN), 1), ("w_proj_a", (SWA_Q, D_MODEL), 1), ("w_proj_b", (FOX_W, D_MODEL), 1),
       ("w_out", (D_MODEL, D_MODEL), 0), ("w_ffn_in", (D_MODEL, 2 * D_FF), 1), ("w_ffn_out", (D_FF, D_MODEL), 0))
SMALL = (("ln_mix_g", D_MODEL), ("ln_mix_b", D_MODEL), ("b_in", N_IN), ("attn_sinks", SWA_HEADS),
         ("ln_ffn_g", D_MODEL), ("ln_ffn_b", D_MODEL), ("conv_w", 3 * D_FF), ("conv_b", D_FF))
ROW_BLOCK = 512
SMALL_LAYER_ROWS = -(-(sum(n for _, n in SMALL) + 1) // (8 * LANE)) * 8
SMALL_ROWS = ROW_BLOCK
FF_CHUNK = 2 * D_FF // N_DEV
N_CHUNK = D_FF // FF_CHUNK


def _div(n, cap, unit):
    if n <= cap:
        return n
    best = None
    for t in range(unit, cap + 1, unit):
        if n % t == 0:
            best = t
    assert best is not None, (n, cap, unit)
    return best


def _params(*sem):
    return pltpu.CompilerParams(dimension_semantics=sem, vmem_limit_bytes=VMEM_LIMIT)


def _peer(r):
    x, y, c = lax.axis_index("x"), lax.axis_index("y"), lax.axis_index("c")
    px = 1 - x if (r >> 2) & 1 else x
    py = 1 - y if (r >> 1) & 1 else y
    pc = 1 - c if r & 1 else c
    return (px, py, pc), 4 * px + 2 * py + pc


def _exchange(tensors, name):
    n = len(tensors)
    gathers = [g for _, g in tensors]

    def body(*refs):
        x_refs, out_refs = refs[:n], refs[n:2 * n]
        send_sems, recv_sems, local_sems = refs[2 * n:]
        _, me = _peer(0)

        def src(t, idx):
            return x_refs[t] if gathers[t] else x_refs[t].at[idx]

        def remote(r, t, slab):
            peer, pid = _peer(r)
            return pltpu.make_async_remote_copy(src_ref=src(t, pid), dst_ref=out_refs[t].at[me if slab is None else pid],
                                                send_sem=send_sems.at[r - 1, t], recv_sem=recv_sems.at[r - 1, t],
                                                device_id=peer, device_id_type=MESH)

        local = [pltpu.make_async_copy(src(t, me), out_refs[t].at[me], local_sems.at[t]) for t in range(n)]
        for cp in local:
            cp.start()
        sent = [remote(r, t, None) for r in range(1, N_DEV) for t in range(n)]
        for cp in sent:
            cp.start()
        for r in range(1, N_DEV):
            for t in range(n):
                remote(r, t, "theirs").wait_recv()
        for cp in sent:
            cp.wait_send()
        for cp in local:
            cp.wait()

    any_spec = pl.BlockSpec(memory_space=pl.ANY)
    return pl.pallas_call(
        body, name=name,
        out_shape=[jax.ShapeDtypeStruct((N_DEV,) + (x.shape if g else x.shape[1:]), x.dtype) for x, g in tensors],
        in_specs=[any_spec] * n, out_specs=[any_spec] * n,
        scratch_shapes=[pltpu.SemaphoreType.DMA((N_DEV - 1, n)), pltpu.SemaphoreType.DMA((N_DEV - 1, n)),
                        pltpu.SemaphoreType.DMA((n,))],
    )(*[x for x, _ in tensors])


def _linear(a, b, *, name, trans_b=False, bias=None, res=None, res_scale=1.0, out_dtype=F32, tm=512, tn=640):
    m, k = a.shape
    n = b.shape[0] if trans_b else b.shape[1]
    tm = _div(m, tm, 8)
    tn = _div(n, tn, LANE)
    dn = (((1,), (1,)), ((), ())) if trans_b else (((1,), (0,)), ((), ()))

    def body(*refs):
        a_ref, b_ref = refs[0], refs[1]
        rest = list(refs[2:])
        bias_ref = rest.pop(0) if bias is not None else None
        res_ref = rest.pop(0) if res is not None else None
        o_ref = rest.pop(0)
        acc = lax.dot_general(a_ref[...].astype(BF16), b_ref[...].astype(BF16), dn, preferred_element_type=F32)
        if bias_ref is not None:
            acc = acc + bias_ref[...]
        if res_ref is not None:
            acc = acc + res_scale * res_ref[...].astype(F32)
        o_ref[...] = acc.astype(out_dtype)

    in_specs = [pl.BlockSpec((tm, k), lambda i, j: (i, 0)),
                pl.BlockSpec((tn, k), lambda i, j: (j, 0)) if trans_b else pl.BlockSpec((k, tn), lambda i, j: (0, j))]
    args = [a, b]
    if bias is not None:
        in_specs.append(pl.BlockSpec((1, tn), lambda i, j: (0, j)))
        args.append(bias)
    if res is not None:
        in_specs.append(pl.BlockSpec((tm, tn), lambda i, j: (i, j)))
        args.append(res)
    return pl.pallas_call(
        body, name=name, grid=(m // tm, n // tn), in_specs=in_specs,
        out_specs=pl.BlockSpec((tm, tn), lambda i, j: (i, j)),
        out_shape=jax.ShapeDtypeStruct((m, n), out_dtype),
        compiler_params=_params("parallel", "arbitrary"),
    )(*args)


def _linear_tn(a, g, *, name, tk=1024, tn=640, tm=512):
    m, k = a.shape
    n = g.shape[1]
    tk = _div(k, tk, LANE)
    tn = _div(n, tn, LANE)
    tm = _div(m, tm, 8)
    steps = m // tm

    def body(a_ref, g_ref, o_ref, acc_ref):
        s = pl.program_id(2)

        @pl.when(s == 0)
        def _():
            acc_ref[...] = jnp.zeros_like(acc_ref)

        acc_ref[...] += lax.dot_general(a_ref[...].astype(BF16), g_ref[...].astype(BF16), (((0,), (0,)), ((), ())),
                                        preferred_element_type=F32)

        @pl.when(s == steps - 1)
        def _():
            o_ref[...] = acc_ref[...]

    return pl.pallas_call(
        body, name=name, grid=(k // tk, n // tn, steps),
        in_specs=[pl.BlockSpec((tm, tk), lambda i, j, s: (s, i)), pl.BlockSpec((tm, tn), lambda i, j, s: (s, j))],
        out_specs=pl.BlockSpec((tk, tn), lambda i, j, s: (i, j)),
        out_shape=jax.ShapeDtypeStruct((k, n), F32),
        scratch_shapes=[pltpu.VMEM((tk, tn), F32)],
        compiler_params=_params("parallel", "parallel", "arbitrary"),
    )(a, g)


def _colsum(g, *, name, tm=512):
    m, n = g.shape
    tm = _div(m, tm, 8)

    def body(g_ref, o_ref):
        @pl.when(pl.program_id(0) == 0)
        def _():
            o_ref[...] = jnp.zeros_like(o_ref)

        o_ref[...] += jnp.sum(g_ref[...].astype(F32), axis=0, keepdims=True)

    return pl.pallas_call(
        body, name=name, grid=(m // tm,),
        in_specs=[pl.BlockSpec((tm, n), lambda i: (i, 0))],
        out_specs=pl.BlockSpec((1, n), lambda i: (0, 0)),
        out_shape=jax.ShapeDtypeStruct((1, n), F32),
        compiler_params=_params("arbitrary"),
    )(g)


def _ln(u, g, b):
    mu = jnp.mean(u, axis=-1, keepdims=True)
    d = u - mu
    var = jnp.mean(d * d, axis=-1, keepdims=True)
    return d * lax.rsqrt(var + LN_EPS) * g + b


def _ln_bwd(dy, u, g, *, name, tm=256):
    m, d = u.shape
    tm = _div(m, tm, 8)

    def body(dy_ref, u_ref, g_ref, du_ref, dg_ref, db_ref):
        @pl.when(pl.program_id(0) == 0)
        def _():
            dg_ref[...] = jnp.zeros_like(dg_ref)
            db_ref[...] = jnp.zeros_like(db_ref)

        dy = dy_ref[...]
        uu = u_ref[...]
        mu = jnp.mean(uu, axis=-1, keepdims=True)
        dd = uu - mu
        rstd = lax.rsqrt(jnp.mean(dd * dd, axis=-1, keepdims=True) + LN_EPS)
        xhat = dd * rstd
        dxh = dy * g_ref[...]
        m1 = jnp.mean(dxh, axis=-1, keepdims=True)
        m2 = jnp.mean(dxh * xhat, axis=-1, keepdims=True)
        du_ref[...] = rstd * (dxh - m1 - xhat * m2)
        dg_ref[...] += jnp.sum(dy * xhat, axis=0, keepdims=True)
        db_ref[...] += jnp.sum(dy, axis=0, keepdims=True)

    row = pl.BlockSpec((tm, d), lambda i: (i, 0))
    vec = pl.BlockSpec((1, d), lambda i: (0, 0))
    return pl.pallas_call(
        body, name=name, grid=(m // tm,), in_specs=[row, row, vec], out_specs=[row, vec, vec],
        out_shape=[jax.ShapeDtypeStruct((m, d), F32), jax.ShapeDtypeStruct((1, d), F32),
                   jax.ShapeDtypeStruct((1, d), F32)],
        compiler_params=_params("arbitrary"),
    )(dy, u, g)


def _linear_res_ln(a, w, res, g, b, *, name, tm=256):
    m, k = a.shape
    d = w.shape[1]
    tm = _div(m, tm, 8)

    def body(a_ref, w_ref, res_ref, g_ref, b_ref, u_ref, y_ref):
        u = ALPHA * res_ref[...] + jnp.dot(a_ref[...].astype(BF16), w_ref[...], preferred_element_type=F32)
        u_ref[...] = u
        y_ref[...] = _ln(u, g_ref[...], b_ref[...])

    row = pl.BlockSpec((tm, d), lambda i: (i, 0))
    vec = pl.BlockSpec((1, d), lambda i: (0, 0))
    return pl.pallas_call(
        body, name=name, grid=(m // tm,),
        in_specs=[pl.BlockSpec((tm, k), lambda i: (i, 0)), pl.BlockSpec((k, d), lambda i: (0, 0)), row, vec, vec],
        out_specs=[row, row],
        out_shape=[jax.ShapeDtypeStruct((m, d), F32), jax.ShapeDtypeStruct((m, d), F32)],
        compiler_params=_params("parallel"),
    )(a, w, res, g, b)


def _tri(n, upper):
    r = lax.broadcasted_iota(jnp.int32, (n, n), 0)
    c = lax.broadcasted_iota(jnp.int32, (n, n), 1)
    return jnp.where((c >= r) if upper else (c <= r), 1.0, 0.0).astype(F32)


def _cumsum_logf(zg):
    s = zg.shape[0]
    nb = s // LANE
    fcol = N_GATE // LANE

    def body(f_ref, c_ref, carry_ref):
        @pl.when(pl.program_id(0) == 0)
        def _():
            carry_ref[...] = jnp.zeros_like(carry_ref)

        f = f_ref[...]
        logf = jnp.minimum(f, 0.0) - jnp.log(1.0 + jnp.exp(-jnp.abs(f)))
        c = jnp.dot(_tri(LANE, False), logf, precision=lax.Precision.HIGHEST, preferred_element_type=F32)
        c = c + carry_ref[0:1, :]
        c_ref[...] = c
        carry_ref[...] = jnp.broadcast_to(c[LANE - 1:LANE, :], carry_ref.shape)

    return pl.pallas_call(
        body, name="cumsum_logf", grid=(nb,),
        in_specs=[pl.BlockSpec((LANE, LANE), lambda i: (i, fcol))],
        out_specs=pl.BlockSpec((LANE, LANE), lambda i: (i, 0)),
        out_shape=jax.ShapeDtypeStruct((s, LANE), F32),
        scratch_shapes=[pltpu.VMEM((8, LANE), F32)],
        compiler_params=_params("arbitrary"),
    )(zg)


def _forget_bwd(dc, zg):
    s = zg.shape[0]
    nb = s // LANE
    fcol = N_GATE // LANE

    def body(dc_ref, f_ref, o_ref, carry_ref):
        @pl.when(pl.program_id(0) == 0)
        def _():
            carry_ref[...] = jnp.zeros_like(carry_ref)

        dc = dc_ref[...]
        dlogf = jnp.dot(_tri(LANE, True), dc, precision=lax.Precision.HIGHEST, preferred_element_type=F32)
        dlogf = dlogf + carry_ref[0:1, :]
        o_ref[...] = (dlogf * jax.nn.sigmoid(-f_ref[...])).astype(BF16)
        carry_ref[...] = jnp.broadcast_to(dlogf[0:1, :], carry_ref.shape)

    return pl.pallas_call(
        body, name="forget_bwd", grid=(nb,),
        in_specs=[pl.BlockSpec((LANE, LANE), lambda i: (nb - 1 - i, 0)),
                  pl.BlockSpec((LANE, LANE), lambda i: (nb - 1 - i, fcol))],
        out_specs=pl.BlockSpec((LANE, LANE), lambda i: (nb - 1 - i, 0)),
        out_shape=jax.ShapeDtypeStruct((s, LANE), BF16),
        scratch_shapes=[pltpu.VMEM((8, LANE), F32)],
        compiler_params=_params("arbitrary"),
    )(dc, zg)


KA_COL = SWA_Q // LANE
VA_COL = KA_COL + 1


def _half_masks():
    lane = lax.broadcasted_iota(jnp.int32, (1, LANE), 1)
    hi = lane >= HEAD_DIM
    return (jnp.logical_not(hi), hi)


def _both_halves(x, sel):
    xs = jnp.where(sel, x, 0.0)
    return xs + pltpu.roll(xs, HEAD_DIM, 1)


def _swa_geometry(i):
    r = lax.broadcasted_iota(jnp.int32, (LANE, 2 * LANE), 0)
    c = lax.broadcasted_iota(jnp.int32, (LANE, 2 * LANE), 1)
    dist = r + LANE - c
    valid = (dist >= 0) & (dist < LANE) & ((c >= LANE) | (i > 0))
    return valid, -dist.astype(F32)


def _swa_specs(nb):
    prev = lambda i: jnp.maximum(i - 1, 0)
    return [pl.BlockSpec((LANE, SWA_Q), lambda i: (i, 0)),
            pl.BlockSpec((LANE, LANE), lambda i: (i, KA_COL)), pl.BlockSpec((LANE, LANE), lambda i: (i, VA_COL)),
            pl.BlockSpec((LANE, LANE), lambda i: (prev(i), KA_COL)),
            pl.BlockSpec((LANE, LANE), lambda i: (prev(i), VA_COL))]


def _swa_fwd(zq, sinks):
    s_len = zq.shape[0]
    nb = s_len // LANE

    def body(q_ref, kc_ref, vc_ref, kp_ref, vp_ref, sink_ref, o_ref, lse_ref):
        i = pl.program_id(0)
        halves = _half_masks()
        lane = lax.broadcasted_iota(jnp.int32, (1, LANE), 1)
        valid, negdist = _swa_geometry(i)
        kcat = jnp.concatenate([kp_ref[...], kc_ref[...]], axis=0).astype(F32)
        vcat = jnp.concatenate([vp_ref[...], vc_ref[...]], axis=0).astype(F32)
        lse_acc = jnp.zeros((LANE, LANE), F32)
        for hk in range(2):
            kb = _both_halves(kcat, halves[hk]).astype(BF16)
            vb = _both_halves(vcat, halves[hk])
            v_e = [jnp.where(halves[e], vb, 0.0).astype(BF16) for e in range(2)]
            for pp in range(2):
                p = 2 * hk + pp
                qp = q_ref[:, p * LANE:(p + 1) * LANE]
                acc = jnp.zeros((LANE, LANE), F32)
                for e in range(2):
                    hq = 2 * p + e
                    qh = jnp.where(halves[e], qp, jnp.zeros_like(qp))
                    s = lax.dot_general(qh, kb, (((1,), (1,)), ((), ())), preferred_element_type=F32) * SCALE
                    s = jnp.where(valid, s + SLOPES[hq] * negdist, NEG_INF)
                    sink = sink_ref[0:1, hq:hq + 1]
                    m = jnp.maximum(jnp.max(s, axis=1, keepdims=True), sink)
                    pe = jnp.exp(s - m)
                    den = jnp.sum(pe, axis=1, keepdims=True) + jnp.exp(sink - m)
                    acc = acc + jnp.dot((pe / den).astype(BF16), v_e[e], preferred_element_type=F32)
                    lse_acc = jnp.where(lane == hq, m + jnp.log(den), lse_acc)
                o_ref[:, p * LANE:(p + 1) * LANE] = acc.astype(BF16)
        lse_ref[...] = lse_acc

    return pl.pallas_call(
        body, name="swa_fwd", grid=(nb,),
        in_specs=_swa_specs(nb) + [pl.BlockSpec((1, LANE), lambda i: (0, 0))],
        out_specs=[pl.BlockSpec((LANE, SWA_Q), lambda i: (i, 0)), pl.BlockSpec((LANE, LANE), lambda i: (i, 0))],
        out_shape=[jax.ShapeDtypeStruct((s_len, SWA_Q), BF16), jax.ShapeDtypeStruct((s_len, LANE), F32)],
        compiler_params=_params("parallel"),
    )(zq, zq, zq, zq, zq, sinks)


def _swa_bwd(zq, sinks, o, do, lse):
    s_len = zq.shape[0]
    nb = s_len // LANE

    def body(q_ref, kc_ref, vc_ref, kp_ref, vp_ref, sink_ref, o_ref, do_ref, lse_ref, dq_ref, dk_ref, dv_ref, ds_ref):
        i = pl.program_id(0)
        halves = _half_masks()
        lane = lax.broadcasted_iota(jnp.int32, (1, LANE), 1)
        valid, negdist = _swa_geometry(i)
        kcat = jnp.concatenate([kp_ref[...], kc_ref[...]], axis=0).astype(F32)
        vcat = jnp.concatenate([vp_ref[...], vc_ref[...]], axis=0).astype(F32)
        lse_all = lse_ref[...]
        dk_tot = jnp.zeros((2 * LANE, LANE), F32)
        dv_tot = jnp.zeros((2 * LANE, LANE), F32)
        dsink = jnp.zeros((1, LANE), F32)
        for hk in range(2):
            kb = _both_halves(kcat, halves[hk])
            vb = _both_halves(vcat, halves[hk])
            k_e = [jnp.where(halves[e], kb, 0.0).astype(BF16) for e in range(2)]
            v_e = [jnp.where(halves[e], vb, 0.0).astype(BF16) for e in range(2)]
            kb = kb.astype(BF16)
            dk_acc = jnp.zeros((2 * LANE, LANE), F32)
            dv_acc = jnp.zeros((2 * LANE, LANE), F32)
            for pp in range(2):
                p = 2 * hk + pp
                cols = slice(p * LANE, (p + 1) * LANE)
                qp = q_ref[:, cols]
                dop = do_ref[:, cols]
                prod = dop.astype(F32) * o_ref[:, cols].astype(F32)
                dq_acc = jnp.zeros((LANE, LANE), F32)
                for e in range(2):
                    hq = 2 * p + e
                    qh = jnp.where(halves[e], qp, jnp.zeros_like(qp))
                    doh = jnp.where(halves[e], dop, jnp.zeros_like(dop))
                    delta = jnp.sum(jnp.where(halves[e], prod, 0.0), axis=1, keepdims=True)
                    lse_h = lse_all[:, hq:hq + 1]
                    s = lax.dot_general(qh, kb, (((1,), (1,)), ((), ())), preferred_element_type=F32) * SCALE
                    s = jnp.where(valid, s + SLOPES[hq] * negdist, NEG_INF)
                    pr = jnp.exp(s - lse_h)
                    dp = lax.dot_general(doh, v_e[e], (((1,), (1,)), ((), ())), preferred_element_type=F32)
                    ds = pr * (dp - delta)
                    sink = sink_ref[0:1, hq:hq + 1]
                    dsink_h = -jnp.sum(jnp.exp(sink - lse_h) * delta, axis=0, keepdims=True)
                    dsink = dsink + jnp.where(lane == hq, dsink_h, 0.0)
                    dsb = (ds * SCALE).astype(BF16)
                    dq_acc = dq_acc + jnp.dot(dsb, k_e[e], preferred_element_type=F32)
                    dk_acc = dk_acc + lax.dot_general(dsb, qh, (((0,), (0,)), ((), ())), preferred_element_type=F32)
                    dv_acc = dv_acc + lax.dot_general(pr.astype(BF16), doh, (((0,), (0,)), ((), ())),
                                                      preferred_element_type=F32)
                dq_ref[:, cols] = dq_acc.astype(BF16)
            dk_tot = dk_tot + jnp.where(halves[hk], dk_acc + pltpu.roll(dk_acc, HEAD_DIM, 1), 0.0)
            dv_tot = dv_tot + jnp.where(halves[hk], dv_acc + pltpu.roll(dv_acc, HEAD_DIM, 1), 0.0)

        @pl.when(i == 0)
        def _():
            ds_ref[...] = jnp.zeros_like(ds_ref)

        ds_ref[...] += dsink
        cur = pl.ds(pl.multiple_of(i * LANE, LANE), LANE)
        dk_ref[cur, :] = dk_tot[LANE:, :]
        dv_ref[cur, :] = dv_tot[LANE:, :]

        @pl.when(i > 0)
        def _():
            prv = pl.ds(pl.multiple_of((i - 1) * LANE, LANE), LANE)
            dk_ref[prv, :] += dk_tot[:LANE, :]
            dv_ref[prv, :] += dv_tot[:LANE, :]

    blk512 = pl.BlockSpec((LANE, SWA_Q), lambda i: (i, 0))
    full = pl.BlockSpec((s_len, LANE), lambda i: (0, 0))
    vec = pl.BlockSpec((1, LANE), lambda i: (0, 0))
    return pl.pallas_call(
        body, name="swa_bwd", grid=(nb,),
        in_specs=_swa_specs(nb) + [vec, blk512, blk512, pl.BlockSpec((LANE, LANE), lambda i: (i, 0))],
        out_specs=[blk512, full, full, vec],
        out_shape=[jax.ShapeDtypeStruct((s_len, SWA_Q), BF16), jax.ShapeDtypeStruct((s_len, LANE), F32),
                   jax.ShapeDtypeStruct((s_len, LANE), F32), jax.ShapeDtypeStruct((1, LANE), F32)],
        compiler_params=_params("arbitrary"),
    )(zq, zq, zq, zq, zq, sinks, o, do, lse)


QB_COL = (SWA_Q + 2 * SWA_KV) // LANE
KB_COL = QB_COL + FOX_W // LANE
VB_COL = KB_COL + FOX_W // LANE
N_PAIR = FOX_HEADS // 2


def _causal(t, keys_first=False):
    r = lax.broadcasted_iota(jnp.int32, (t, t), 0)
    c = lax.broadcasted_iota(jnp.int32, (t, t), 1)
    return c >= r if keys_first else r >= c


N_SPLIT = 3


def _own_half(e):
    hi = lax.broadcasted_iota(jnp.int32, (1, LANE), 1) >= HEAD_DIM
    return hi if e else jnp.logical_not(hi)


def _feature_lane(e, t):
    return HEAD_DIM * (1 - e) + t


def _fox_prep(zq, c, tm=256):
    s_len = zq.shape[0]
    tm = _div(s_len, tm, 8)

    def body(z_ref, c_ref, qx_ref, kx_ref, vx_ref):
        lane = lax.broadcasted_iota(jnp.int32, (1, LANE), 1)
        for h in range(FOX_HEADS):
            p, e = divmod(h, 2)
            own = _own_half(e)
            tile = lambda col: z_ref[:, (col + p) * LANE:(col + p + 1) * LANE].astype(F32)
            rest = c_ref[:, h:h + 1]
            qf = jnp.zeros((tm, LANE), F32)
            kf = jnp.zeros((tm, LANE), F32)
            for t in range(N_SPLIT):
                part = rest.astype(BF16).astype(F32)
                rest = rest - part
                qf = jnp.where(lane == _feature_lane(e, t), part, qf)
                qf = jnp.where(lane == _feature_lane(e, N_SPLIT + t), 1.0, qf)
                kf = jnp.where(lane == _feature_lane(e, t), 1.0, kf)
                kf = jnp.where(lane == _feature_lane(e, N_SPLIT + t), -part, kf)
            vf = jnp.where(lane == _feature_lane(e, 0), 1.0, 0.0)
            cols = slice(h * LANE, (h + 1) * LANE)
            qx_ref[:, cols] = jnp.where(own, tile(QB_COL) * SCALE, qf).astype(BF16)
            kx_ref[:, cols] = jnp.where(own, tile(KB_COL), kf).astype(BF16)
            vx_ref[:, cols] = jnp.where(own, tile(VB_COL), vf).astype(BF16)

    out = jax.ShapeDtypeStruct((s_len, FOX_HEADS * LANE), BF16)
    blk = pl.BlockSpec((tm, FOX_HEADS * LANE), lambda i: (i, 0))
    return pl.pallas_call(
        body, name="fox_prep", grid=(s_len // tm,),
        in_specs=[pl.BlockSpec((tm, N_QKV), lambda i: (i, 0)), pl.BlockSpec((tm, LANE), lambda i: (i, 0))],
        out_specs=[blk, blk, blk], out_shape=[out, out, out],
        compiler_params=_params("parallel"),
    )(zq, c)


def _fox_fwd(qx, kx, vx, t_cap=1024):
    s_len = qx.shape[0]
    t = _div(s_len, t_cap, LANE)
    nq = s_len // t

    def body(q_ref, k_ref, v_ref, o_ref, o32_ref, lse_ref):
        i = pl.program_id(1)
        qs = [q_ref[:, e * LANE:(e + 1) * LANE] for e in range(2)]

        def step(j, carry, diag):
            rows = pl.ds(pl.multiple_of(j * t, t), t)
            new = []
            for e in range(2):
                m, acc = carry[e]
                s = lax.dot_general(qs[e], k_ref[rows, e * LANE:(e + 1) * LANE], (((1,), (1,)), ((), ())),
                                    preferred_element_type=F32)
                if diag:
                    s = jnp.where(_causal(t), s, NEG_INF)
                mn = jnp.maximum(m, jnp.max(s, axis=1, keepdims=True))
                pe = jnp.exp(s - mn).astype(BF16)
                acc = acc * jnp.exp(m - mn) + jnp.dot(pe, v_ref[rows, e * LANE:(e + 1) * LANE],
                                                      preferred_element_type=F32)
                new.append((mn, acc))
            return tuple(new)

        init = (jnp.full((t, 1), NEG_INF, F32), jnp.zeros((t, LANE), F32))
        carry = lax.fori_loop(0, i, lambda j, c: step(j, c, False), (init, init))
        carry = step(i, carry, True)
        outs, lses = [], []
        for e in range(2):
            m, acc = carry[e]
            l = acc[:, _feature_lane(e, 0):_feature_lane(e, 0) + 1]
            outs.append(acc / l)
            lses.append(m + jnp.log(l))
        out = jnp.where(_own_half(1), outs[1], outs[0])
        o_ref[...] = out.astype(BF16)
        o32_ref[...] = out
        lse_ref[...] = jnp.where(_own_half(1), lses[1], lses[0])

    pair = pl.BlockSpec((s_len, 2 * LANE), lambda p, i: (0, p))
    tile = pl.BlockSpec((t, LANE), lambda p, i: (i, p))
    return pl.pallas_call(
        body, name="fox_fwd", grid=(N_PAIR, nq),
        in_specs=[pl.BlockSpec((t, 2 * LANE), lambda p, i: (i, p)), pair, pair],
        out_specs=[tile, tile, tile],
        out_shape=[jax.ShapeDtypeStruct((s_len, FOX_W), BF16), jax.ShapeDtypeStruct((s_len, FOX_W), F32),
                   jax.ShapeDtypeStruct((s_len, FOX_W), F32)],
        compiler_params=_params("parallel", "parallel"),
    )(qx, kx, vx)


def _fox_stats(o, do, lse, tm=256):
    s_len = o.shape[0]
    tm = _div(s_len, tm, LANE)

    def body(o_ref, do_ref, lse_ref, dox_ref, st_ref):
        lane = lax.broadcasted_iota(jnp.int32, (1, LANE), 1)
        for p in range(N_PAIR):
            cols = slice(p * LANE, (p + 1) * LANE)
            dout = do_ref[:, cols]
            prod = o_ref[:, cols] * dout.astype(F32)
            lse = lse_ref[:, cols]
            st = jnp.zeros((tm, LANE), F32)
            for e in range(2):
                h = 2 * p + e
                dox_ref[:, h * LANE:(h + 1) * LANE] = jnp.where(_own_half(e), dout, jnp.zeros_like(dout))
                st = jnp.where(lane == e, lse[:, e * HEAD_DIM:e * HEAD_DIM + 1], st)
                delta = jnp.sum(jnp.where(_own_half(e), prod, 0.0), axis=1, keepdims=True)
                st = jnp.where(lane == 2 + e, delta, st)
            st_ref[p] = st.T[:8, :]

    row = pl.BlockSpec((tm, FOX_W), lambda i: (i, 0))
    return pl.pallas_call(
        body, name="fox_stats", grid=(s_len // tm,), in_specs=[row, row, row],
        out_specs=[pl.BlockSpec((tm, FOX_HEADS * LANE), lambda i: (i, 0)),
                   pl.BlockSpec((N_PAIR, 8, tm), lambda i: (0, 0, i))],
        out_shape=[jax.ShapeDtypeStruct((s_len, FOX_HEADS * LANE), BF16),
                   jax.ShapeDtypeStruct((N_PAIR, 8, s_len), F32)],
        compiler_params=_params("parallel"),
    )(o, do, lse)


def _fox_bwd(qx, kx, vx, dox, stats, t_cap=512):
    s_len = qx.shape[0]
    t = _div(s_len, t_cap, LANE)
    n = s_len // t

    def body(q_ref, do_ref, st_ref, k_ref, v_ref, dq_ref, dk_ref, dv_ref, dc_ref):
        j = pl.program_id(1)
        lane = lax.broadcasted_iota(jnp.int32, (1, LANE), 1)

        @pl.when(j == 0)
        def _():
            dq_ref[...] = jnp.zeros_like(dq_ref)

        ks = [k_ref[:, e * LANE:(e + 1) * LANE] for e in range(2)]
        vs = [v_ref[:, e * LANE:(e + 1) * LANE] for e in range(2)]

        def step(i, carry, diag):
            rows = pl.ds(pl.multiple_of(i * t, t), t)
            new = []
            dq = jnp.zeros((t, LANE), F32)
            for e in range(2):
                dk, dv = carry[e]
                q = q_ref[rows, e * LANE:(e + 1) * LANE]
                dout = do_ref[rows, e * LANE:(e + 1) * LANE]
                s_t = lax.dot_general(ks[e], q, (((1,), (1,)), ((), ())), preferred_element_type=F32)
                if diag:
                    s_t = jnp.where(_causal(t, keys_first=True), s_t, NEG_INF)
                p_t = jnp.exp(s_t - st_ref[0, e:e + 1, rows])
                dp_t = lax.dot_general(vs[e], dout, (((1,), (1,)), ((), ())), preferred_element_type=F32)
                ds_t = (p_t * (dp_t - st_ref[0, 2 + e:3 + e, rows])).astype(BF16)
                dv = dv + jnp.dot(p_t.astype(BF16), dout, preferred_element_type=F32)
                dk = dk + jnp.dot(ds_t, q, preferred_element_type=F32)
                dq_e = lax.dot_general(ds_t, ks[e], (((0,), (0,)), ((), ())), preferred_element_type=F32)
                dq = dq + jnp.where(_own_half(e), dq_e, 0.0)
                new.append((dk, dv))
            dq_ref[rows, :] += dq * SCALE
            return tuple(new)

        zero = jnp.zeros((t, LANE), F32)
        carry = step(j, ((zero, zero), (zero, zero)), True)
        (dk0, dv0), (dk1, dv1) = lax.fori_loop(j + 1, n, lambda i, c: step(i, c, False), carry)
        dk_ref[...] = jnp.where(_own_half(1), dk1, dk0).astype(BF16)
        dv_ref[...] = jnp.where(_own_half(1), dv1, dv0).astype(BF16)
        ones0 = _feature_lane(0, N_SPLIT)
        ones1 = _feature_lane(1, N_SPLIT)
        dc_ref[...] = jnp.where(lane == 0, -dk0[:, ones0:ones0 + 1], jnp.where(lane == 1, -dk1[:, ones1:ones1 + 1], 0.0))

    pair = pl.BlockSpec((s_len, 2 * LANE), lambda p, j: (0, p))
    blk = pl.BlockSpec((t, 2 * LANE), lambda p, j: (j, p))
    tile = pl.BlockSpec((t, LANE), lambda p, j: (j, p))
    return pl.pallas_call(
        body, name="fox_bwd", grid=(N_PAIR, n),
        in_specs=[pair, pair, pl.BlockSpec((1, 8, s_len), lambda p, j: (p, 0, 0)), blk, blk],
        out_specs=[pl.BlockSpec((s_len, LANE), lambda p, j: (0, p)), tile, tile, tile],
        out_shape=[jax.ShapeDtypeStruct((s_len, FOX_W), F32), jax.ShapeDtypeStruct((s_len, FOX_W), BF16),
                   jax.ShapeDtypeStruct((s_len, FOX_W), BF16), jax.ShapeDtypeStruct((s_len, FOX_W), F32)],
        compiler_params=_params("arbitrary", "arbitrary"),
    )(qx, dox, stats, kx, vx)


def _mixer_out(attn_a, attn_b, zg, h, wpa, wpb, wout, g, b, tm=256):
    m = h.shape[0]
    tm = _div(m, tm, 8)

    def body(a_ref, b_ref, ga_ref, gb_ref, h_ref, wpa_ref, wpb_ref, wout_ref, g_ref, bb_ref,
             h1_ref, u_ref, mg_ref, ya_ref, yb_ref):
        ya = jnp.dot(a_ref[...], wpa_ref[...], preferred_element_type=F32)
        yb = jnp.dot(b_ref[...], wpb_ref[...], preferred_element_type=F32)
        merged = (jax.nn.sigmoid(ga_ref[...]) * ya + jax.nn.sigmoid(gb_ref[...]) * yb).astype(BF16)
        u = ALPHA * h_ref[...] + jnp.dot(merged, wout_ref[...], preferred_element_type=F32)
        u_ref[...] = u
        h1_ref[...] = _ln(u, g_ref[...], bb_ref[...])
        mg_ref[...] = merged
        ya_ref[...] = ya.astype(BF16)
        yb_ref[...] = yb.astype(BF16)

    row = pl.BlockSpec((tm, D_MODEL), lambda i: (i, 0))
    att = pl.BlockSpec((tm, SWA_Q), lambda i: (i, 0))
    vec = pl.BlockSpec((1, D_MODEL), lambda i: (0, 0))
    wsm = pl.BlockSpec((SWA_Q, D_MODEL), lambda i: (0, 0))
    return pl.pallas_call(
        body, name="mixer_out", grid=(m // tm,),
        in_specs=[att, att, row, pl.BlockSpec((tm, D_MODEL), lambda i: (i, 1)), row, wsm, wsm,
                  pl.BlockSpec((D_MODEL, D_MODEL), lambda i: (0, 0)), vec, vec],
        out_specs=[row, row, row, row, row],
        out_shape=[jax.ShapeDtypeStruct((m, D_MODEL), F32), jax.ShapeDtypeStruct((m, D_MODEL), F32),
                   jax.ShapeDtypeStruct((m, D_MODEL), BF16), jax.ShapeDtypeStruct((m, D_MODEL), BF16),
                   jax.ShapeDtypeStruct((m, D_MODEL), BF16)],
        compiler_params=_params("parallel"),
    )(attn_a, attn_b, zg, zg, h, wpa, wpb, wout, g, b)


def _gate_bwd(dmerged, ya, yb, zg, tm=256):
    m = dmerged.shape[0]
    tm = _div(m, tm, 8)

    def body(dm_ref, ya_ref, yb_ref, ga_ref, gb_ref, dya_ref, dyb_ref, dga_ref, dgb_ref):
        dm = dm_ref[...]
        for y_ref, g_ref, dy_ref, dg_ref in ((ya_ref, ga_ref, dya_ref, dga_ref), (yb_ref, gb_ref, dyb_ref, dgb_ref)):
            sg = jax.nn.sigmoid(g_ref[...])
            dy_ref[...] = (dm * sg).astype(BF16)
            dg_ref[...] = (dm * y_ref[...].astype(F32) * sg * (1.0 - sg)).astype(BF16)

    row = pl.BlockSpec((tm, D_MODEL), lambda i: (i, 0))
    out = jax.ShapeDtypeStruct((m, D_MODEL), BF16)
    return pl.pallas_call(
        body, name="gate_bwd", grid=(m // tm,),
        in_specs=[row, row, row, row, pl.BlockSpec((tm, D_MODEL), lambda i: (i, 1))],
        out_specs=[row, row, row, row], out_shape=[out, out, out, out],
        compiler_params=_params("parallel"),
    )(dmerged, ya, yb, zg, zg)


def _shift_down(x, k, halo, first):
    rows = lax.broadcasted_iota(jnp.int32, (x.shape[0], 1), 0)
    y = pltpu.roll(x, k, 0)
    for r in range(k):
        fill = jnp.where(first, 0.0, halo[8 - k + r:8 - k + r + 1, :])
        y = jnp.where(rows == r, fill, y)
    return y


def _shift_up(x, k, halo, last):
    n = x.shape[0]
    rows = lax.broadcasted_iota(jnp.int32, (n, 1), 0)
    y = pltpu.roll(x, n - k, 0)
    for r in range(k):
        fill = jnp.where(last, 0.0, halo[r:r + 1, :])
        y = jnp.where(rows == n - k + r, fill, y)
    return y


def _conv_act(gate, gate_m1, gate_m2, cw, cb):
    return cb + cw[0:1, :] * gate_m2 + cw[1:2, :] * gate_m1 + cw[2:3, :] * gate


def _ffn_in(h1, wfi, tm=256):
    s_len = h1.shape[0]
    tm = _div(s_len, tm, 8)

    def body(a_ref, w_ref, o_ref):
        a = a_ref[...].astype(BF16)
        for g in range(2):
            for c in range(N_CHUNK):
                o_ref[c, g] = jnp.dot(a, w_ref[N_CHUNK * g + c], preferred_element_type=F32)

    return pl.pallas_call(
        body, name="ffn_in", grid=(s_len // tm,),
        in_specs=[pl.BlockSpec((tm, D_MODEL), lambda i: (i, 0)),
                  pl.BlockSpec((N_DEV, D_MODEL, FF_CHUNK), lambda i: (0, 0, 0))],
        out_specs=pl.BlockSpec((N_CHUNK, 2, tm, FF_CHUNK), lambda i: (0, 0, i, 0)),
        out_shape=jax.ShapeDtypeStruct((N_CHUNK, 2, s_len, FF_CHUNK), F32),
        compiler_params=_params("parallel"),
    )(h1, wfi)


def _conv_fwd(gu, cw, cb, tm=256):
    s_len = gu.shape[2]
    tm = _div(s_len, tm, 8)
    hb = tm // 8

    def body(gu_ref, gp_ref, cw_ref, cb_ref, o_ref):
        first = pl.program_id(1) == 0
        gate = gu_ref[0, 0]
        halo = gp_ref[0, 0]
        conv = _conv_act(gate, _shift_down(gate, 1, halo, first), _shift_down(gate, 2, halo, first),
                         cw_ref[0], cb_ref[0])
        o_ref[0] = (conv * jax.nn.sigmoid(conv) * gu_ref[0, 1]).astype(BF16)

    return pl.pallas_call(
        body, name="conv_fwd", grid=(N_CHUNK, s_len // tm),
        in_specs=[pl.BlockSpec((1, 2, tm, FF_CHUNK), lambda c, i: (c, 0, i, 0)),
                  pl.BlockSpec((1, 1, 8, FF_CHUNK), lambda c, i: (c, 0, jnp.maximum(i * hb - 1, 0), 0)),
                  pl.BlockSpec((1, 8, FF_CHUNK), lambda c, i: (c, 0, 0)),
                  pl.BlockSpec((1, 1, FF_CHUNK), lambda c, i: (c, 0, 0))],
        out_specs=pl.BlockSpec((1, tm, FF_CHUNK), lambda c, i: (c, i, 0)),
        out_shape=jax.ShapeDtypeStruct((N_CHUNK, s_len, FF_CHUNK), BF16),
        compiler_params=_params("parallel", "parallel"),
    )(gu, gu, cw, cb)


def _ffn_out_ln(act, wfo, res, g, b, tm=256):
    s_len = res.shape[0]
    tm = _div(s_len, tm, 8)

    def body(a_ref, w_ref, res_ref, g_ref, b_ref, u_ref, y_ref):
        u = ALPHA * res_ref[...]
        for c in range(N_CHUNK):
            u = u + jnp.dot(a_ref[c], w_ref[c], preferred_element_type=F32)
        u_ref[...] = u
        y_ref[...] = _ln(u, g_ref[...], b_ref[...])

    row = pl.BlockSpec((tm, D_MODEL), lambda i: (i, 0))
    vec = pl.BlockSpec((1, D_MODEL), lambda i: (0, 0))
    return pl.pallas_call(
        body, name="ffn_out_ln", grid=(s_len // tm,),
        in_specs=[pl.BlockSpec((N_CHUNK, tm, FF_CHUNK), lambda i: (0, i, 0)),
                  pl.BlockSpec((N_CHUNK, FF_CHUNK, D_MODEL), lambda i: (0, 0, 0)), row, vec, vec],
        out_specs=[row, row],
        out_shape=[jax.ShapeDtypeStruct((s_len, D_MODEL), F32), jax.ShapeDtypeStruct((s_len, D_MODEL), F32)],
        compiler_params=_params("parallel"),
    )(act, wfo, res, g, b)


def _d_act(du, wfo, tm=256):
    s_len = du.shape[0]
    tm = _div(s_len, tm, 8)

    def body(du_ref, w_ref, o_ref):
        du_b = du_ref[...].astype(BF16)
        for c in range(N_CHUNK):
            o_ref[c] = lax.dot_general(du_b, w_ref[c], (((1,), (1,)), ((), ())), preferred_element_type=F32)

    return pl.pallas_call(
        body, name="d_act", grid=(s_len // tm,),
        in_specs=[pl.BlockSpec((tm, D_MODEL), lambda i: (i, 0)),
                  pl.BlockSpec((N_CHUNK, FF_CHUNK, D_MODEL), lambda i: (0, 0, 0))],
        out_specs=pl.BlockSpec((N_CHUNK, tm, FF_CHUNK), lambda i: (0, i, 0)),
        out_shape=jax.ShapeDtypeStruct((N_CHUNK, s_len, FF_CHUNK), F32),
        compiler_params=_params("parallel"),
    )(du, wfo)


def _g_w_ffn_out(act, du, tm=512):
    s_len = du.shape[0]
    tm = _div(s_len, tm, 8)
    steps = s_len // tm

    def body(a_ref, g_ref, o_ref, acc_ref):
        s = pl.program_id(1)

        @pl.when(s == 0)
        def _():
            acc_ref[...] = jnp.zeros_like(acc_ref)

        acc_ref[...] += lax.dot_general(a_ref[0], g_ref[...].astype(BF16), (((0,), (0,)), ((), ())),
                                        preferred_element_type=F32)

        @pl.when(s == steps - 1)
        def _():
            o_ref[0] = acc_ref[...]

    return pl.pallas_call(
        body, name="g_w_ffn_out", grid=(N_CHUNK, steps),
        in_specs=[pl.BlockSpec((1, tm, FF_CHUNK), lambda c, s: (c, s, 0)),
                  pl.BlockSpec((tm, D_MODEL), lambda c, s: (s, 0))],
        out_specs=pl.BlockSpec((1, FF_CHUNK, D_MODEL), lambda c, s: (c, 0, 0)),
        out_shape=jax.ShapeDtypeStruct((N_CHUNK, FF_CHUNK, D_MODEL), F32),
        scratch_shapes=[pltpu.VMEM((FF_CHUNK, D_MODEL), F32)],
        compiler_params=_params("parallel", "arbitrary"),
    )(act, du)


def _d_h1(dgu, wfi, res, tm=256):
    s_len = res.shape[0]
    tm = _div(s_len, tm, 8)

    def body(a_ref, w_ref, res_ref, o_ref):
        acc = ALPHA * res_ref[...]
        for g in range(2):
            for c in range(N_CHUNK):
                acc = acc + lax.dot_general(a_ref[c, g], w_ref[N_CHUNK * g + c], (((1,), (1,)), ((), ())),
                                            preferred_element_type=F32)
        o_ref[...] = acc

    row = pl.BlockSpec((tm, D_MODEL), lambda i: (i, 0))
    return pl.pallas_call(
        body, name="d_h1", grid=(s_len // tm,),
        in_specs=[pl.BlockSpec((N_CHUNK, 2, tm, FF_CHUNK), lambda i: (0, 0, i, 0)),
                  pl.BlockSpec((N_DEV, D_MODEL, FF_CHUNK), lambda i: (0, 0, 0)), row],
        out_specs=row, out_shape=jax.ShapeDtypeStruct((s_len, D_MODEL), F32),
        compiler_params=_params("parallel"),
    )(dgu, wfi, res)


def _g_w_ffn_in(h1, dgu, tm=512):
    s_len = h1.shape[0]
    tm = _div(s_len, tm, 8)
    steps = s_len // tm

    def body(a_ref, g_ref, o_ref, acc_ref):
        s = pl.program_id(1)

        @pl.when(s == 0)
        def _():
            acc_ref[...] = jnp.zeros_like(acc_ref)

        acc_ref[...] += lax.dot_general(a_ref[...].astype(BF16), g_ref[0, 0], (((0,), (0,)), ((), ())),
                                        preferred_element_type=F32)

        @pl.when(s == steps - 1)
        def _():
            o_ref[0] = acc_ref[...]

    return pl.pallas_call(
        body, name="g_w_ffn_in", grid=(N_DEV, steps),
        in_specs=[pl.BlockSpec((tm, D_MODEL), lambda d, s: (s, 0)),
                  pl.BlockSpec((1, 1, tm, FF_CHUNK), lambda d, s: (d % N_CHUNK, d // N_CHUNK, s, 0))],
        out_specs=pl.BlockSpec((1, D_MODEL, FF_CHUNK), lambda d, s: (d, 0, 0)),
        out_shape=jax.ShapeDtypeStruct((N_DEV, D_MODEL, FF_CHUNK), F32),
        scratch_shapes=[pltpu.VMEM((D_MODEL, FF_CHUNK), F32)],
        compiler_params=_params("parallel", "arbitrary"),
    )(h1, dgu)


def _conv_bwd(gu, dact, cw, cb, tm=256):
    s_len = gu.shape[2]
    tm = _div(s_len, tm, 8)
    nrow = s_len // tm
    hb = tm // 8

    def dconv_of(conv, up, da):
        sg = jax.nn.sigmoid(conv)
        return da * up * (sg * (1.0 + conv * (1.0 - sg)))

    def body(gu_ref, gp_ref, gun_ref, da_ref, dan_ref, cw_ref, cb_ref, dgu_ref, dcw_ref):
        i = pl.program_id(1)
        first = i == 0
        last = i == nrow - 1
        cw = cw_ref[0]
        cb = cb_ref[0]
        gate = gu_ref[0, 0]
        halo = gp_ref[0, 0]
        g_m1 = _shift_down(gate, 1, halo, first)
        g_m2 = _shift_down(gate, 2, halo, first)
        conv = _conv_act(gate, g_m1, g_m2, cw, cb)
        da = da_ref[0]
        sg = jax.nn.sigmoid(conv)
        dgu_ref[0, 1] = (da * conv * sg).astype(BF16)
        dconv = dconv_of(conv, gu_ref[0, 1], da)
        gate_n = gun_ref[0, 0]
        tail = gate[tm - 8:, :]
        conv_n = _conv_act(gate_n, _shift_down(gate_n, 1, tail, False), _shift_down(gate_n, 2, tail, False), cw, cb)
        dconv_n = dconv_of(conv_n, gun_ref[0, 1], dan_ref[0])
        dgate = (cw[2:3, :] * dconv + cw[1:2, :] * _shift_up(dconv, 1, dconv_n, last)
                 + cw[0:1, :] * _shift_up(dconv, 2, dconv_n, last))
        dgu_ref[0, 0] = dgate.astype(BF16)

        @pl.when(first)
        def _():
            dcw_ref[...] = jnp.zeros_like(dcw_ref)

        row = lax.broadcasted_iota(jnp.int32, (8, 1), 0)
        part = jnp.zeros((8, FF_CHUNK), F32)
        for r, term in enumerate((dconv * g_m2, dconv * g_m1, dconv * gate, dconv)):
            part = jnp.where(row == r, jnp.sum(term, axis=0, keepdims=True), part)
        dcw_ref[0] += part

    nxt = lambda i: jnp.minimum((i + 1) * hb, s_len // 8 - 1)
    main = pl.BlockSpec((1, 2, tm, FF_CHUNK), lambda c, i: (c, 0, i, 0))
    return pl.pallas_call(
        body, name="conv_bwd", grid=(N_CHUNK, nrow),
        in_specs=[main,
                  pl.BlockSpec((1, 1, 8, FF_CHUNK), lambda c, i: (c, 0, jnp.maximum(i * hb - 1, 0), 0)),
                  pl.BlockSpec((1, 2, 8, FF_CHUNK), lambda c, i: (c, 0, nxt(i), 0)),
                  pl.BlockSpec((1, tm, FF_CHUNK), lambda c, i: (c, i, 0)),
                  pl.BlockSpec((1, 8, FF_CHUNK), lambda c, i: (c, nxt(i), 0)),
                  pl.BlockSpec((1, 8, FF_CHUNK), lambda c, i: (c, 0, 0)),
                  pl.BlockSpec((1, 1, FF_CHUNK), lambda c, i: (c, 0, 0))],
        out_specs=[main, pl.BlockSpec((1, 8, FF_CHUNK), lambda c, i: (c, 0, 0))],
        out_shape=[jax.ShapeDtypeStruct((N_CHUNK, 2, s_len, FF_CHUNK), BF16),
                   jax.ShapeDtypeStruct((N_CHUNK, 8, FF_CHUNK), F32)],
        compiler_params=_params("parallel", "arbitrary"),
    )(gu, gu, gu, dact, dact, cw, cb)


def _loss_head(y, target, tm=256):
    m, d = y.shape
    tm = _div(m, tm, 8)

    def body(y_ref, t_ref, dy_ref, loss_ref):
        @pl.when(pl.program_id(0) == 0)
        def _():
            loss_ref[...] = jnp.zeros_like(loss_ref)

        err = y_ref[...] - t_ref[...]
        dy_ref[...] = err / d
        loss_ref[...] += 0.5 * jnp.sum(jnp.sum(err * err, axis=1, keepdims=True) / d, axis=0, keepdims=True)

    row = pl.BlockSpec((tm, d), lambda i: (i, 0))
    return pl.pallas_call(
        body, name="loss_head", grid=(m // tm,), in_specs=[row, row],
        out_specs=[row, pl.BlockSpec((8, LANE), lambda i: (0, 0))],
        out_shape=[jax.ShapeDtypeStruct((m, d), F32), jax.ShapeDtypeStruct((8, LANE), F32)],
        compiler_params=_params("arbitrary"),
    )(y, target)


def _sum_devices(r_ref):
    acc = r_ref[0]
    for d in range(1, N_DEV):
        acc = acc + r_ref[d]
    return acc


def _sum8(recv):
    rows = recv.shape[1]
    tr = _div(rows, ROW_BLOCK, 8)

    def body(r_ref, o_ref):
        o_ref[...] = _sum_devices(r_ref)

    return pl.pallas_call(
        body, name="sum8", grid=(rows // tr,),
        in_specs=[pl.BlockSpec((N_DEV, tr, LANE), lambda i: (0, i, 0))],
        out_specs=pl.BlockSpec((tr, LANE), lambda i: (i, 0)),
        out_shape=jax.ShapeDtypeStruct((rows, LANE), F32),
        compiler_params=_params("parallel"),
    )(recv)


def _adamw_math(w, g, m, v):
    m = ADAM_B1 * m + (1.0 - ADAM_B1) * g
    v = ADAM_B2 * v + (1.0 - ADAM_B2) * (g * g)
    m_hat = m / (1.0 - ADAM_B1 ** ADAM_STEP)
    v_hat = v / (1.0 - ADAM_B2 ** ADAM_STEP)
    return -ADAM_LR * (m_hat / (jnp.sqrt(v_hat) + ADAM_EPS) + ADAM_WD * w), m, v


def _adamw_rows(w, g, m, v, name):
    rows = w.shape[0]
    tr = _div(rows, ROW_BLOCK, 8)

    def body(w_ref, g_ref, m_ref, v_ref, d_ref, mo_ref, vo_ref):
        d_ref[...], mo_ref[...], vo_ref[...] = _adamw_math(w_ref[...], g_ref[...], m_ref[...], v_ref[...])

    blk = pl.BlockSpec((tr, LANE), lambda i: (i, 0))
    out = jax.ShapeDtypeStruct((rows, LANE), F32)
    return pl.pallas_call(
        body, name=name, grid=(rows // tr,), in_specs=[blk, blk, blk, blk], out_specs=[blk, blk, blk],
        out_shape=[out, out, out], compiler_params=_params("parallel"),
    )(w, g, m, v)


def _adamw_shard(recv, w, m, v, layer, prev, name):
    _, k, n = recv.shape
    tk = _div(k, 128, 8)

    def body(r_ref, w_ref, m_ref, v_ref, *rest):
        g_ref, d_ref, mo_ref, vo_ref = rest[-4:]
        g = _sum_devices(r_ref)
        g_ref[0] = g
        d_ref[0], mo_ref[0], vo_ref[0] = _adamw_math(w_ref[0], g, m_ref[0], v_ref[0])

    blk = pl.BlockSpec((1, tk, n), lambda i: (layer, i, 0))
    out = jax.ShapeDtypeStruct((DEPTH, k, n), F32)
    carried = [] if prev is None else list(prev)
    return pl.pallas_call(
        body, name=name, grid=(k // tk,),
        in_specs=[pl.BlockSpec((N_DEV, tk, n), lambda i: (0, i, 0)), blk, blk, blk]
        + [pl.BlockSpec(memory_space=pl.ANY)] * len(carried),
        out_specs=[blk, blk, blk, blk], out_shape=[out, out, out, out],
        input_output_aliases={4 + j: j for j in range(len(carried))},
        compiler_params=_params("parallel"),
    )(recv, w, m, v, *carried)


def _to_rows(flat, rows):
    flat = flat.reshape(-1)
    return jnp.pad(flat, (0, rows * LANE - flat.shape[0])).reshape(rows, LANE)


def _pad_cols_z(a):
    f0 = N_QKV
    g0 = N_QKV + FOX_HEADS
    pad = jnp.zeros(a.shape[:-1] + (F_PAD - FOX_HEADS,), a.dtype)
    return jnp.concatenate([a[..., :f0], a[..., g0:], a[..., f0:g0], pad], axis=-1)


def _unpad_cols_z(a):
    f0 = N_QKV + N_GATE
    return jnp.concatenate([a[..., :N_QKV], a[..., f0:f0 + FOX_HEADS], a[..., N_QKV:f0]], axis=-1)


def _shards_to_cols(g):
    _, k, n = g.shape
    return g.transpose(1, 0, 2).reshape(k, N_DEV * n)


def _cols_to_shards(full):
    k, n = full.shape
    return full.reshape(k, N_DEV, n // N_DEV).transpose(1, 0, 2)


def _layer_fwd(h, w, p):
    zq = _linear(h, w["wq"], bias=p["bq"], out_dtype=BF16, name="z_qkv", tn=768)
    zg = _linear(h, w["wg"], bias=p["bg"], name="z_gate", tn=768)
    qx, kx, vx = _fox_prep(zq, _cumsum_logf(zg))
    attn_a, lse_a = _swa_fwd(zq, p["sinks"])
    attn_b, attn_b32, lse_b = _fox_fwd(qx, kx, vx)
    h1, u1, merged, ya, yb = _mixer_out(attn_a, attn_b, zg, h, w["w_proj_a"], w["w_proj_b"], w["w_out"],
                                        p["ln_mix_g"], p["ln_mix_b"])
    gu = _ffn_in(h1, w["w_ffn_in"])
    act = _conv_fwd(gu, p["conv_w"], p["conv_b"])
    u2, h2 = _ffn_out_ln(act, w["w_ffn_out"], h1, p["ln_ffn_g"], p["ln_ffn_b"])
    saved = dict(h=h, zq=zq, zg=zg, qx=qx, kx=kx, vx=vx, attn_a=attn_a, lse_a=lse_a, attn_b=attn_b, attn_b32=attn_b32, lse_b=lse_b, h1=h1,
                 u1=u1, merged=merged, ya=ya, yb=yb, gu=gu, act=act, u2=u2)
    return h2, saved


def _layer_bwd(dh2, sv, w, p):
    s_len = dh2.shape[0]
    du2, d_ffn_g, d_ffn_b = _ln_bwd(dh2, sv["u2"], p["ln_ffn_g"], name="ln_ffn_bwd")
    dact = _d_act(du2, w["w_ffn_out"])
    g_ffn_out = _g_w_ffn_out(sv["act"], du2)
    dgu, dcw = _conv_bwd(sv["gu"], dact, p["conv_w"], p["conv_b"])
    dcw = dcw.transpose(1, 0, 2).reshape(8, D_FF)
    dh1 = _d_h1(dgu, w["w_ffn_in"], du2)
    g_ffn_in = _g_w_ffn_in(sv["h1"], dgu)
    du1, d_mix_g, d_mix_b = _ln_bwd(dh1, sv["u1"], p["ln_mix_g"], name="ln_mix_bwd")
    dmerged = _linear(du1, w["w_out"], trans_b=True, name="d_merged", tn=1024)
    g_out = _linear_tn(sv["merged"], du1, name="g_w_out", tn=1024)
    dya, dyb, dga, dgb = _gate_bwd(dmerged, sv["ya"], sv["yb"], sv["zg"])
    dattn_a = _linear(dya, w["w_proj_a"], trans_b=True, out_dtype=BF16, name="d_attn_a", tn=512)
    dattn_b = _linear(dyb, w["w_proj_b"], trans_b=True, out_dtype=BF16, name="d_attn_b", tn=512)
    g_proj_a = _linear_tn(sv["attn_a"], dya, name="g_w_proj_a", tk=512, tn=1024)
    g_proj_b = _linear_tn(sv["attn_b"], dyb, name="g_w_proj_b", tk=512, tn=1024)
    dq_a, dk_a, dv_a, dsinks = _swa_bwd(sv["zq"], p["sinks"], sv["attn_a"], dattn_a, sv["lse_a"])
    dox, stats = _fox_stats(sv["attn_b32"], dattn_b, sv["lse_b"])
    dq_b, dk_b, dv_b, dcc = _fox_bwd(sv["qx"], sv["kx"], sv["vx"], dox, stats)
    dc = jnp.pad(dcc.reshape(s_len, N_PAIR, LANE)[:, :, :2].reshape(s_len, FOX_HEADS), ((0, 0), (0, LANE - FOX_HEADS)))
    df = _forget_bwd(dc, sv["zg"])
    dz = jnp.concatenate([dq_a, dk_a.astype(BF16), dv_a.astype(BF16), dq_b.astype(BF16), dk_b, dv_b, dga, dgb, df,
                          jnp.zeros((s_len, F_PAD - LANE), BF16)], axis=1)
    dh = _linear(dz, w["w_in_p"], trans_b=True, res=du1, res_scale=ALPHA, name="d_h", tn=512)
    g_in = _unpad_cols_z(_linear_tn(sv["h"], dz, name="g_w_in", tn=768))
    g_b_in = _unpad_cols_z(_colsum(dz, name="g_b_in"))
    big = dict(w_in=_cols_to_shards(g_in), w_proj_a=_cols_to_shards(g_proj_a), w_proj_b=_cols_to_shards(g_proj_b),
               w_out=g_out.reshape(N_DEV, D_MODEL // N_DEV, D_MODEL), w_ffn_in=g_ffn_in,
               w_ffn_out=g_ffn_out.reshape(N_DEV, D_FF // N_DEV, D_MODEL))
    small = dict(ln_mix_g=d_mix_g, ln_mix_b=d_mix_b, b_in=g_b_in, attn_sinks=dsinks[:, :SWA_HEADS],
                 ln_ffn_g=d_ffn_g, ln_ffn_b=d_ffn_b, conv_w=dcw[:3], conv_b=dcw[3:4])
    return dh, big, small


def _layer_weights(w_in, w_proj_a, w_proj_b, w_out, w_ffn_in, w_ffn_out):
    w_in_p = _pad_cols_z(_shards_to_cols(w_in))
    return dict(w_in_p=w_in_p, wq=w_in_p[:, :N_QKV], wg=w_in_p[:, N_QKV:], w_proj_a=_shards_to_cols(w_proj_a),
                w_proj_b=_shards_to_cols(w_proj_b), w_out=w_out.reshape(D_MODEL, D_MODEL), w_ffn_in=w_ffn_in,
                w_ffn_out=w_ffn_out.reshape(N_CHUNK, FF_CHUNK, D_MODEL))


def _layer_params(r):
    b_p = _pad_cols_z(r["b_in"].reshape(1, N_IN))
    return dict(
        bq=b_p[:, :N_QKV], bg=b_p[:, N_QKV:],
        sinks=jnp.pad(r["attn_sinks"].reshape(1, SWA_HEADS), ((0, 0), (0, LANE - SWA_HEADS))),
        ln_mix_g=r["ln_mix_g"].reshape(1, D_MODEL), ln_mix_b=r["ln_mix_b"].reshape(1, D_MODEL),
        ln_ffn_g=r["ln_ffn_g"].reshape(1, D_MODEL), ln_ffn_b=r["ln_ffn_b"].reshape(1, D_MODEL),
        conv_w=jnp.pad(r["conv_w"], ((0, 5), (0, 0))).reshape(8, N_CHUNK, FF_CHUNK).transpose(1, 0, 2),
        conv_b=r["conv_b"].reshape(N_CHUNK, 1, FF_CHUNK))


def _local_step(x, target, ws, ps, after_layer):
    h = x
    saved = []
    for w, p in zip(ws, ps):
        h, sv = _layer_fwd(h, w, p)
        saved.append(sv)
    dh, loss = _loss_head(h, target)
    outs = [None] * DEPTH
    for l in reversed(range(DEPTH)):
        dh, big, small = _layer_bwd(dh, saved[l], ws[l], ps[l])
        outs[l] = after_layer(l, big, small, loss[0, 0])
    return dh, outs


def kernel(x, ln_mix_g, ln_mix_b, w_in, b_in, attn_sinks, w_proj_a, w_proj_b, w_out, ln_ffn_g, ln_ffn_b, w_ffn_in, conv_w, conv_b, w_ffn_out, loss_target, m_ln_mix_g, m_ln_mix_b, m_w_in, m_b_in, m_attn_sinks, m_w_proj_a, m_w_proj_b, m_w_out, m_ln_ffn_g, m_ln_ffn_b, m_w_ffn_in, m_conv_w, m_conv_b, m_w_ffn_out, v_ln_mix_g, v_ln_mix_b, v_w_in, v_b_in, v_attn_sinks, v_w_proj_a, v_w_proj_b, v_w_out, v_ln_ffn_g, v_ln_ffn_b, v_w_ffn_in, v_conv_w, v_conv_b, v_w_ffn_out):
    wts = dict(ln_mix_g=ln_mix_g, ln_mix_b=ln_mix_b, w_in=w_in, b_in=b_in, attn_sinks=attn_sinks, w_proj_a=w_proj_a,
               w_proj_b=w_proj_b, w_out=w_out, ln_ffn_g=ln_ffn_g, ln_ffn_b=ln_ffn_b, w_ffn_in=w_ffn_in,
               conv_w=conv_w, conv_b=conv_b, w_ffn_out=w_ffn_out)
    mom = dict(ln_mix_g=m_ln_mix_g, ln_mix_b=m_ln_mix_b, w_in=m_w_in, b_in=m_b_in, attn_sinks=m_attn_sinks,
               w_proj_a=m_w_proj_a, w_proj_b=m_w_proj_b, w_out=m_w_out, ln_ffn_g=m_ln_ffn_g, ln_ffn_b=m_ln_ffn_b,
               w_ffn_in=m_w_ffn_in, conv_w=m_conv_w, conv_b=m_conv_b, w_ffn_out=m_w_ffn_out)
    vel = dict(ln_mix_g=v_ln_mix_g, ln_mix_b=v_ln_mix_b, w_in=v_w_in, b_in=v_b_in, attn_sinks=v_attn_sinks,
               w_proj_a=v_w_proj_a, w_proj_b=v_w_proj_b, w_out=v_w_out, ln_ffn_g=v_ln_ffn_g, ln_ffn_b=v_ln_ffn_b,
               w_ffn_in=v_w_ffn_in, conv_w=v_conv_w, conv_b=v_conv_b, w_ffn_out=v_w_ffn_out)
    names = list(wts)
    big_names = [n for n, _, _ in BIG]
    small_names = [n for n, _ in SMALL]
    me = 4 * lax.axis_index("x") + 2 * lax.axis_index("y") + lax.axis_index("c")
    cw_shard = D_FF // N_DEV

    gathered = _exchange([(wts[n].astype(BF16), True) for n in big_names] + [(conv_w, True)], "gather_weights")
    conv_full = gathered[-1].transpose(1, 2, 0, 3).reshape(DEPTH, 3, D_FF)
    ws = [_layer_weights(*[g[:, l] for g in gathered[:-1]]) for l in range(DEPTH)]
    ps = [_layer_params(dict(b_in=b_in[l], attn_sinks=attn_sinks[l], ln_mix_g=ln_mix_g[l], ln_mix_b=ln_mix_b[l],
                             ln_ffn_g=ln_ffn_g[l], ln_ffn_b=ln_ffn_b[l], conv_w=conv_full[l], conv_b=conv_b[l]))
          for l in range(DEPTH)]

    def exchange_layer(l, big, small, loss_part):
        vec = jnp.concatenate([small[n].reshape(-1) for n in small_names] + [loss_part.reshape(1)])
        return _exchange([(big[n], False) for n in big_names] + [(_to_rows(vec, SMALL_LAYER_ROWS), True)],
                         "exchange_grads_%d" % l)

    grad_x, recv = _local_step(x[0], loss_target[0], ws, ps, exchange_layer)

    big_out = {}
    for t, n in enumerate(big_names):
        outs = None
        for l in reversed(range(DEPTH)):
            outs = _adamw_shard(recv[l][t], wts[n], mom[n], vel[n], l, outs, "adamw_%s_%d" % (n, l))
        big_out[n] = outs
    small_sum = [_sum8(recv[l][-1]).reshape(-1) for l in range(DEPTH)]
    g_small = {}
    off = 0
    for n, size in SMALL:
        g_small[n] = jnp.stack([small_sum[l][off:off + size] for l in range(DEPTH)])
        off += size
    loss = small_sum[0][off]
    g_small["conv_w"] = lax.dynamic_slice_in_dim(g_small["conv_w"].reshape(DEPTH, 3, D_FF), me * cw_shard, cw_shard,
                                                 axis=2)
    g_small = {n: g_small[n].reshape(wts[n].shape) for n in small_names}

    def pack_small(tree):
        return _to_rows(jnp.concatenate([tree[n].reshape(-1) for n in small_names]), SMALL_ROWS)

    small_out = (pack_small(g_small),) + tuple(_adamw_rows(pack_small(wts), pack_small(g_small), pack_small(mom),
                                                           pack_small(vel), "adamw_small"))

    def result(j):
        out = {n: big_out[n][j] for n in big_names}
        flat = small_out[j].reshape(-1)
        off = 0
        for n in small_names:
            out[n] = flat[off:off + wts[n].size].reshape(wts[n].shape)
            off += wts[n].size
        return [out[n] for n in names]

    return (loss, grad_x[None], *result(0), *result(1), *result(2), *result(3))
```

```python
import functools

import jax
import jax.numpy as jnp
from jax import lax
from jax.experimental import pallas as pl
from jax.experimental.pallas import tpu as pltpu

F32 = jnp.float32
BF16 = jnp.bfloat16
MESH = pl.DeviceIdType.MESH

N_DEV = 8
DEPTH = 2
D_MODEL = 1024
HEAD_DIM = 64
SWA_Q = 512
SWA_KV = 128
FOX_W = 512
FOX_HEADS = 8
SWA_HEADS = 8
D_FF = 2816
N_IN = 4360
N_QKV = SWA_Q + 2 * SWA_KV + 3 * FOX_W
N_GATE = 2 * D_MODEL
F_PAD = 256
N_ZG = N_GATE + F_PAD
N_ZP = N_QKV + N_ZG
LN_EPS = 1e-5
NEG_INF = -1e30
ALPHA = (2 * DEPTH) ** 0.25
SCALE = HEAD_DIM ** -0.5
SLOPES = tuple(2.0 ** (-8.0 * (h + 1) / SWA_HEADS) for h in range(SWA_HEADS))

ADAM_LR = 0.001
ADAM_B1 = 0.9
ADAM_B2 = 0.999
ADAM_EPS = 1e-08
ADAM_WD = 0.01
ADAM_STEP = 10

LANE = 128
VMEM_LIMIT = 56 * 1024 * 1024

BIG = (("w_in", (D_MODEL, N_IN), 1), ("w_proj_a", (SWA_Q, D_MODEL), 1), ("w_proj_b", (FOX_W, D_MODEL), 1),
       ("w_out", (D_MODEL, D_MODEL), 0), ("w_ffn_in", (D_MODEL, 2 * D_FF), 1), ("w_ffn_out", (D_FF, D_MODEL), 0))
SMALL = (("ln_mix_g", D_MODEL), ("ln_mix_b", D_MODEL), ("b_in", N_IN), ("attn_sinks", SWA_HEADS),
         ("ln_ffn_g", D_MODEL), ("ln_ffn_b", D_MODEL), ("conv_w", 3 * D_FF), ("conv_b", D_FF))
ROW_BLOCK = 512
SMALL_LAYER_ROWS = -(-(sum(n for _, n in SMALL) + 1) // (8 * LANE)) * 8
SMALL_ROWS = ROW_BLOCK
FF_CHUNK = 2 * D_FF // N_DEV
N_CHUNK = D_FF // FF_CHUNK


def _div(n, cap, unit):
    if n <= cap:
        return n
    best = None
    for t in range(unit, cap + 1, unit):
        if n % t == 0:
            best = t
    assert best is not None, (n, cap, unit)
    return best


def _params(*sem):
    return pltpu.CompilerParams(dimension_semantics=sem, vmem_limit_bytes=VMEM_LIMIT)


def _peer(r):
    x, y, c = lax.axis_index("x"), lax.axis_index("y"), lax.axis_index("c")
    px = 1 - x if (r >> 2) & 1 else x
    py = 1 - y if (r >> 1) & 1 else y
    pc = 1 - c if r & 1 else c
    return (px, py, pc), 4 * px + 2 * py + pc


def _exchange(tensors, name):
    n = len(tensors)
    gathers = [g for _, g in tensors]

    def body(*refs):
        x_refs, out_refs = refs[:n], refs[n:2 * n]
        send_sems, recv_sems, local_sems = refs[2 * n:]
        _, me = _peer(0)

        def src(t, idx):
            return x_refs[t] if gathers[t] else x_refs[t].at[idx]

        def remote(r, t, slab):
            peer, pid = _peer(r)
            return pltpu.make_async_remote_copy(src_ref=src(t, pid), dst_ref=out_refs[t].at[me if slab is None else pid],
                                                send_sem=send_sems.at[r - 1, t], recv_sem=recv_sems.at[r - 1, t],
                                                device_id=peer, device_id_type=MESH)

        local = [pltpu.make_async_copy(src(t, me), out_refs[t].at[me], local_sems.at[t]) for t in range(n)]
        for cp in local:
            cp.start()
        sent = [remote(r, t, None) for r in range(1, N_DEV) for t in range(n)]
        for cp in sent:
            cp.start()
        for r in range(1, N_DEV):
            for t in range(n):
                remote(r, t, "theirs").wait_recv()
        for cp in sent:
            cp.wait_send()
        for cp in local:
            cp.wait()

    any_spec = pl.BlockSpec(memory_space=pl.ANY)
    return pl.pallas_call(
        body, name=name,
        out_shape=[jax.ShapeDtypeStruct((N_DEV,) + (x.shape if g else x.shape[1:]), x.dtype) for x, g in tensors],
        in_specs=[any_spec] * n, out_specs=[any_spec] * n,
        scratch_shapes=[pltpu.SemaphoreType.DMA((N_DEV - 1, n)), pltpu.SemaphoreType.DMA((N_DEV - 1, n)),
                        pltpu.SemaphoreType.DMA((n,))],
    )(*[x for x, _ in tensors])


def _linear(a, b, *, name, trans_b=False, bias=None, res=None, res_scale=1.0, out_dtype=F32, tm=512, tn=640):
    m, k = a.shape
    n = b.shape[0] if trans_b else b.shape[1]
    tm = _div(m, tm, 8)
    tn = _div(n, tn, LANE)
    dn = (((1,), (1,)), ((), ())) if trans_b else (((1,), (0,)), ((), ()))

    def body(*refs):
        a_ref, b_ref = refs[0], refs[1]
        rest = list(refs[2:])
        bias_ref = rest.pop(0) if bias is not None else None
        res_ref = rest.pop(0) if res is not None else None
        o_ref = rest.pop(0)
        acc = lax.dot_general(a_ref[...].astype(BF16), b_ref[...].astype(BF16), dn, preferred_element_type=F32)
        if bias_ref is not None:
            acc = acc + bias_ref[...]
        if res_ref is not None:
            acc = acc + res_scale * res_ref[...].astype(F32)
        o_ref[...] = acc.astype(out_dtype)

    in_specs = [pl.BlockSpec((tm, k), lambda i, j: (i, 0)),
                pl.BlockSpec((tn, k), lambda i, j: (j, 0)) if trans_b else pl.BlockSpec((k, tn), lambda i, j: (0, j))]
    args = [a, b]
    if bias is not None:
        in_specs.append(pl.BlockSpec((1, tn), lambda i, j: (0, j)))
        args.append(bias)
    if res is not None:
        in_specs.append(pl.BlockSpec((tm, tn), lambda i, j: (i, j)))
        args.append(res)
    return pl.pallas_call(
        body, name=name, grid=(m // tm, n // tn), in_specs=in_specs,
        out_specs=pl.BlockSpec((tm, tn), lambda i, j: (i, j)),
        out_shape=jax.ShapeDtypeStruct((m, n), out_dtype),
        compiler_params=_params("parallel", "arbitrary"),
    )(*args)


def _linear_tn(a, g, *, name, tk=1024, tn=640, tm=512):
    m, k = a.shape
    n = g.shape[1]
    tk = _div(k, tk, LANE)
    tn = _div(n, tn, LANE)
    tm = _div(m, tm, 8)
    steps = m // tm

    def body(a_ref, g_ref, o_ref, acc_ref):
        s = pl.program_id(2)

        @pl.when(s == 0)
        def _():
            acc_ref[...] = jnp.zeros_like(acc_ref)

        acc_ref[...] += lax.dot_general(a_ref[...].astype(BF16), g_ref[...].astype(BF16), (((0,), (0,)), ((), ())),
                                        preferred_element_type=F32)

        @pl.when(s == steps - 1)
        def _():
            o_ref[...] = acc_ref[...]

    return pl.pallas_call(
        body, name=name, grid=(k // tk, n // tn, steps),
        in_specs=[pl.BlockSpec((tm, tk), lambda i, j, s: (s, i)), pl.BlockSpec((tm, tn), lambda i, j, s: (s, j))],
        out_specs=pl.BlockSpec((tk, tn), lambda i, j, s: (i, j)),
        out_shape=jax.ShapeDtypeStruct((k, n), F32),
        scratch_shapes=[pltpu.VMEM((tk, tn), F32)],
        compiler_params=_params("parallel", "parallel", "arbitrary"),
    )(a, g)


def _colsum(g, *, name, tm=512):
    m, n = g.shape
    tm = _div(m, tm, 8)

    def body(g_ref, o_ref):
        @pl.when(pl.program_id(0) == 0)
        def _():
            o_ref[...] = jnp.zeros_like(o_ref)

        o_ref[...] += jnp.sum(g_ref[...].astype(F32), axis=0, keepdims=True)

    return pl.pallas_call(
        body, name=name, grid=(m // tm,),
        in_specs=[pl.BlockSpec((tm, n), lambda i: (i, 0))],
        out_specs=pl.BlockSpec((1, n), lambda i: (0, 0)),
        out_shape=jax.ShapeDtypeStruct((1, n), F32),
        compiler_params=_params("arbitrary"),
    )(g)


def _ln(u, g, b):
    mu = jnp.mean(u, axis=-1, keepdims=True)
    d = u - mu
    var = jnp.mean(d * d, axis=-1, keepdims=True)
    return d * lax.rsqrt(var + LN_EPS) * g + b


def _ln_bwd(dy, u, g, *, name, tm=256):
    m, d = u.shape
    tm = _div(m, tm, 8)

    def body(dy_ref, u_ref, g_ref, du_ref, dg_ref, db_ref):
        @pl.when(pl.program_id(0) == 0)
        def _():
            dg_ref[...] = jnp.zeros_like(dg_ref)
            db_ref[...] = jnp.zeros_like(db_ref)

        dy = dy_ref[...]
        uu = u_ref[...]
        mu = jnp.mean(uu, axis=-1, keepdims=True)
        dd = uu - mu
        rstd = lax.rsqrt(jnp.mean(dd * dd, axis=-1, keepdims=True) + LN_EPS)
        xhat = dd * rstd
        dxh = dy * g_ref[...]
        m1 = jnp.mean(dxh, axis=-1, keepdims=True)
        m2 = jnp.mean(dxh * xhat, axis=-1, keepdims=True)
        du_ref[...] = rstd * (dxh - m1 - xhat * m2)
        dg_ref[...] += jnp.sum(dy * xhat, axis=0, keepdims=True)
        db_ref[...] += jnp.sum(dy, axis=0, keepdims=True)

    row = pl.BlockSpec((tm, d), lambda i: (i, 0))
    vec = pl.BlockSpec((1, d), lambda i: (0, 0))
    return pl.pallas_call(
        body, name=name, grid=(m // tm,), in_specs=[row, row, vec], out_specs=[row, vec, vec],
        out_shape=[jax.ShapeDtypeStruct((m, d), F32), jax.ShapeDtypeStruct((1, d), F32),
                   jax.ShapeDtypeStruct((1, d), F32)],
        compiler_params=_params("arbitrary"),
    )(dy, u, g)


def _linear_res_ln(a, w, res, g, b, *, name, tm=256):
    m, k = a.shape
    d = w.shape[1]
    tm = _div(m, tm, 8)

    def body(a_ref, w_ref, res_ref, g_ref, b_ref, u_ref, y_ref):
        u = ALPHA * res_ref[...] + jnp.dot(a_ref[...].astype(BF16), w_ref[...], preferred_element_type=F32)
        u_ref[...] = u
        y_ref[...] = _ln(u, g_ref[...], b_ref[...])

    row = pl.BlockSpec((tm, d), lambda i: (i, 0))
    vec = pl.BlockSpec((1, d), lambda i: (0, 0))
    return pl.pallas_call(
        body, name=name, grid=(m // tm,),
        in_specs=[pl.BlockSpec((tm, k), lambda i: (i, 0)), pl.BlockSpec((k, d), lambda i: (0, 0)), row, vec, vec],
        out_specs=[row, row],
        out_shape=[jax.ShapeDtypeStruct((m, d), F32), jax.ShapeDtypeStruct((m, d), F32)],
        compiler_params=_params("parallel"),
    )(a, w, res, g, b)


def _tri(n, upper):
    r = lax.broadcasted_iota(jnp.int32, (n, n), 0)
    c = lax.broadcasted_iota(jnp.int32, (n, n), 1)
    return jnp.where((c >= r) if upper else (c <= r), 1.0, 0.0).astype(F32)


def _cumsum_logf(zg):
    s = zg.shape[0]
    nb = s // LANE
    fcol = N_GATE // LANE

    def body(f_ref, c_ref, carry_ref):
        @pl.when(pl.program_id(0) == 0)
        def _():
            carry_ref[...] = jnp.zeros_like(carry_ref)

        f = f_ref[...]
        logf = jnp.minimum(f, 0.0) - jnp.log(1.0 + jnp.exp(-jnp.abs(f)))
        c = jnp.dot(_tri(LANE, False), logf, precision=lax.Precision.HIGHEST, preferred_element_type=F32)
        c = c + carry_ref[0:1, :]
        c_ref[...] = c
        carry_ref[...] = jnp.broadcast_to(c[LANE - 1:LANE, :], carry_ref.shape)

    return pl.pallas_call(
        body, name="cumsum_logf", grid=(nb,),
        in_specs=[pl.BlockSpec((LANE, LANE), lambda i: (i, fcol))],
        out_specs=pl.BlockSpec((LANE, LANE), lambda i: (i, 0)),
        out_shape=jax.ShapeDtypeStruct((s, LANE), F32),
        scratch_shapes=[pltpu.VMEM((8, LANE), F32)],
        compiler_params=_params("arbitrary"),
    )(zg)


def _forget_bwd(dc, zg):
    s = zg.shape[0]
    nb = s // LANE
    fcol = N_GATE // LANE

    def body(dc_ref, f_ref, o_ref, carry_ref):
        @pl.when(pl.program_id(0) == 0)
        def _():
            carry_ref[...] = jnp.zeros_like(carry_ref)

        dc = dc_ref[...]
        dlogf = jnp.dot(_tri(LANE, True), dc, precision=lax.Precision.HIGHEST, preferred_element_type=F32)
        dlogf = dlogf + carry_ref[0:1, :]
        o_ref[...] = (dlogf * jax.nn.sigmoid(-f_ref[...])).astype(BF16)
        carry_ref[...] = jnp.broadcast_to(dlogf[0:1, :], carry_ref.shape)

    return pl.pallas_call(
        body, name="forget_bwd", grid=(nb,),
        in_specs=[pl.BlockSpec((LANE, LANE), lambda i: (nb - 1 - i, 0)),
                  pl.BlockSpec((LANE, LANE), lambda i: (nb - 1 - i, fcol))],
        out_specs=pl.BlockSpec((LANE, LANE), lambda i: (nb - 1 - i, 0)),
        out_shape=jax.ShapeDtypeStruct((s, LANE), BF16),
        scratch_shapes=[pltpu.VMEM((8, LANE), F32)],
        compiler_params=_params("arbitrary"),
    )(dc, zg)


KA_COL = SWA_Q // LANE
VA_COL = KA_COL + 1


def _half_masks():
    lane = lax.broadcasted_iota(jnp.int32, (1, LANE), 1)
    hi = lane >= HEAD_DIM
    return (jnp.logical_not(hi), hi)


def _both_halves(x, sel):
    xs = jnp.where(sel, x, 0.0)
    return xs + pltpu.roll(xs, HEAD_DIM, 1)


def _swa_geometry(i):
    r = lax.broadcasted_iota(jnp.int32, (LANE, 2 * LANE), 0)
    c = lax.broadcasted_iota(jnp.int32, (LANE, 2 * LANE), 1)
    dist = r + LANE - c
    valid = (dist >= 0) & (dist < LANE) & ((c >= LANE) | (i > 0))
    return valid, -dist.astype(F32)


def _swa_specs(nb):
    prev = lambda i: jnp.maximum(i - 1, 0)
    return [pl.BlockSpec((LANE, SWA_Q), lambda i: (i, 0)),
            pl.BlockSpec((LANE, LANE), lambda i: (i, KA_COL)), pl.BlockSpec((LANE, LANE), lambda i: (i, VA_COL)),
            pl.BlockSpec((LANE, LANE), lambda i: (prev(i), KA_COL)),
            pl.BlockSpec((LANE, LANE), lambda i: (prev(i), VA_COL))]


def _swa_fwd(zq, sinks):
    s_len = zq.shape[0]
    nb = s_len // LANE

    def body(q_ref, kc_ref, vc_ref, kp_ref, vp_ref, sink_ref, o_ref, lse_ref):
        i = pl.program_id(0)
        halves = _half_masks()
        lane = lax.broadcasted_iota(jnp.int32, (1, LANE), 1)
        valid, negdist = _swa_geometry(i)
        kcat = jnp.concatenate([kp_ref[...], kc_ref[...]], axis=0).astype(F32)
        vcat = jnp.concatenate([vp_ref[...], vc_ref[...]], axis=0).astype(F32)
        lse_acc = jnp.zeros((LANE, LANE), F32)
        for hk in range(2):
            kb = _both_halves(kcat, halves[hk]).astype(BF16)
            vb = _both_halves(vcat, halves[hk])
            v_e = [jnp.where(halves[e], vb, 0.0).astype(BF16) for e in range(2)]
            for pp in range(2):
                p = 2 * hk + pp
                qp = q_ref[:, p * LANE:(p + 1) * LANE]
                acc = jnp.zeros((LANE, LANE), F32)
                for e in range(2):
                    hq = 2 * p + e
                    qh = jnp.where(halves[e], qp, jnp.zeros_like(qp))
                    s = lax.dot_general(qh, kb, (((1,), (1,)), ((), ())), preferred_element_type=F32) * SCALE
                    s = jnp.where(valid, s + SLOPES[hq] * negdist, NEG_INF)
                    sink = sink_ref[0:1, hq:hq + 1]
                    m = jnp.maximum(jnp.max(s, axis=1, keepdims=True), sink)
                    pe = jnp.exp(s - m)
                    den = jnp.sum(pe, axis=1, keepdims=True) + jnp.exp(sink - m)
                    acc = acc + jnp.dot((pe / den).astype(BF16), v_e[e], preferred_element_type=F32)
                    lse_acc = jnp.where(lane == hq, m + jnp.log(den), lse_acc)
                o_ref[:, p * LANE:(p + 1) * LANE] = acc.astype(BF16)
        lse_ref[...] = lse_acc

    return pl.pallas_call(
        body, name="swa_fwd", grid=(nb,),
        in_specs=_swa_specs(nb) + [pl.BlockSpec((1, LANE), lambda i: (0, 0))],
        out_specs=[pl.BlockSpec((LANE, SWA_Q), lambda i: (i, 0)), pl.BlockSpec((LANE, LANE), lambda i: (i, 0))],
        out_shape=[jax.ShapeDtypeStruct((s_len, SWA_Q), BF16), jax.ShapeDtypeStruct((s_len, LANE), F32)],
        compiler_params=_params("parallel"),
    )(zq, zq, zq, zq, zq, sinks)


def _swa_bwd(zq, sinks, o, do, lse):
    s_len = zq.shape[0]
    nb = s_len // LANE

    def body(q_ref, kc_ref, vc_ref, kp_ref, vp_ref, sink_ref, o_ref, do_ref, lse_ref, dq_ref, dk_ref, dv_ref, ds_ref):
        i = pl.program_id(0)
        halves = _half_masks()
        lane = lax.broadcasted_iota(jnp.int32, (1, LANE), 1)
        valid, negdist = _swa_geometry(i)
        kcat = jnp.concatenate([kp_ref[...], kc_ref[...]], axis=0).astype(F32)
        vcat = jnp.concatenate([vp_ref[...], vc_ref[...]], axis=0).astype(F32)
        lse_all = lse_ref[...]
        dk_tot = jnp.zeros((2 * LANE, LANE), F32)
        dv_tot = jnp.zeros((2 * LANE, LANE), F32)
        dsink = jnp.zeros((1, LANE), F32)
        for hk in range(2):
            kb = _both_halves(kcat, halves[hk])
            vb = _both_halves(vcat, halves[hk])
            k_e = [jnp.where(halves[e], kb, 0.0).astype(BF16) for e in range(2)]
            v_e = [jnp.where(halves[e], vb, 0.0).astype(BF16) for e in range(2)]
            kb = kb.astype(BF16)
            dk_acc = jnp.zeros((2 * LANE, LANE), F32)
            dv_acc = jnp.zeros((2 * LANE, LANE), F32)
            for pp in range(2):
                p = 2 * hk + pp
                cols = slice(p * LANE, (p + 1) * LANE)
                qp = q_ref[:, cols]
                dop = do_ref[:, cols]
                prod = dop.astype(F32) * o_ref[:, cols].astype(F32)
                dq_acc = jnp.zeros((LANE, LANE), F32)
                for e in range(2):
                    hq = 2 * p + e
                    qh = jnp.where(halves[e], qp, jnp.zeros_like(qp))
                    doh = jnp.where(halves[e], dop, jnp.zeros_like(dop))
                    delta = jnp.sum(jnp.where(halves[e], prod, 0.0), axis=1, keepdims=True)
                    lse_h = lse_all[:, hq:hq + 1]
                    s = lax.dot_general(qh, kb, (((1,), (1,)), ((), ())), preferred_element_type=F32) * SCALE
                    s = jnp.where(valid, s + SLOPES[hq] * negdist, NEG_INF)
                    pr = jnp.exp(s - lse_h)
                    dp = lax.dot_general(doh, v_e[e], (((1,), (1,)), ((), ())), preferred_element_type=F32)
                    ds = pr * (dp - delta)
                    sink = sink_ref[0:1, hq:hq + 1]
                    dsink_h = -jnp.sum(jnp.exp(sink - lse_h) * delta, axis=0, keepdims=True)
                    dsink = dsink + jnp.where(lane == hq, dsink_h, 0.0)
                    dsb = (ds * SCALE).astype(BF16)
                    dq_acc = dq_acc + jnp.dot(dsb, k_e[e], preferred_element_type=F32)
                    dk_acc = dk_acc + lax.dot_general(dsb, qh, (((0,), (0,)), ((), ())), preferred_element_type=F32)
                    dv_acc = dv_acc + lax.dot_general(pr.astype(BF16), doh, (((0,), (0,)), ((), ())),
                                                      preferred_element_type=F32)
                dq_ref[:, cols] = dq_acc.astype(BF16)
            dk_tot = dk_tot + jnp.where(halves[hk], dk_acc + pltpu.roll(dk_acc, HEAD_DIM, 1), 0.0)
            dv_tot = dv_tot + jnp.where(halves[hk], dv_acc + pltpu.roll(dv_acc, HEAD_DIM, 1), 0.0)

        @pl.when(i == 0)
        def _():
            ds_ref[...] = jnp.zeros_like(ds_ref)

        ds_ref[...] += dsink
        cur = pl.ds(pl.multiple_of(i * LANE, LANE), LANE)
        dk_ref[cur, :] = dk_tot[LANE:, :]
        dv_ref[cur, :] = dv_tot[LANE:, :]

        @pl.when(i > 0)
        def _():
            prv = pl.ds(pl.multiple_of((i - 1) * LANE, LANE), LANE)
            dk_ref[prv, :] += dk_tot[:LANE, :]
            dv_ref[prv, :] += dv_tot[:LANE, :]

    blk512 = pl.BlockSpec((LANE, SWA_Q), lambda i: (i, 0))
    full = pl.BlockSpec((s_len, LANE), lambda i: (0, 0))
    vec = pl.BlockSpec((1, LANE), lambda i: (0, 0))
    return pl.pallas_call(
        body, name="swa_bwd", grid=(nb,),
        in_specs=_swa_specs(nb) + [vec, blk512, blk512, pl.BlockSpec((LANE, LANE), lambda i: (i, 0))],
        out_specs=[blk512, full, full, vec],
        out_shape=[jax.ShapeDtypeStruct((s_len, SWA_Q), BF16), jax.ShapeDtypeStruct((s_len, LANE), F32),
                   jax.ShapeDtypeStruct((s_len, LANE), F32), jax.ShapeDtypeStruct((1, LANE), F32)],
        compiler_params=_params("arbitrary"),
    )(zq, zq, zq, zq, zq, sinks, o, do, lse)


QB_COL = (SWA_Q + 2 * SWA_KV) // LANE
KB_COL = QB_COL + FOX_W // LANE
VB_COL = KB_COL + FOX_W // LANE
N_PAIR = FOX_HEADS // 2


def _causal(t, keys_first=False):
    r = lax.broadcasted_iota(jnp.int32, (t, t), 0)
    c = lax.broadcasted_iota(jnp.int32, (t, t), 1)
    return c >= r if keys_first else r >= c


N_SPLIT = 3


def _own_half(e):
    hi = lax.broadcasted_iota(jnp.int32, (1, LANE), 1) >= HEAD_DIM
    return hi if e else jnp.logical_not(hi)


def _feature_lane(e, t):
    return HEAD_DIM * (1 - e) + t


def _fox_prep(zq, c, tm=256):
    s_len = zq.shape[0]
    tm = _div(s_len, tm, 8)

    def body(z_ref, c_ref, qx_ref, kx_ref, vx_ref):
        lane = lax.broadcasted_iota(jnp.int32, (1, LANE), 1)
        for h in range(FOX_HEADS):
            p, e = divmod(h, 2)
            own = _own_half(e)
            tile = lambda col: z_ref[:, (col + p) * LANE:(col + p + 1) * LANE].astype(F32)
            rest = c_ref[:, h:h + 1]
            qf = jnp.zeros((tm, LANE), F32)
            kf = jnp.zeros((tm, LANE), F32)
            for t in range(N_SPLIT):
                part = rest.astype(BF16).astype(F32)
                rest = rest - part
                qf = jnp.where(lane == _feature_lane(e, t), part, qf)
                qf = jnp.where(lane == _feature_lane(e, N_SPLIT + t), 1.0, qf)
                kf = jnp.where(lane == _feature_lane(e, t), 1.0, kf)
                kf = jnp.where(lane == _feature_lane(e, N_SPLIT + t), -part, kf)
            vf = jnp.where(lane == _feature_lane(e, 0), 1.0, 0.0)
            cols = slice(h * LANE, (h + 1) * LANE)
            qx_ref[:, cols] = jnp.where(own, tile(QB_COL) * SCALE, qf).astype(BF16)
            kx_ref[:, cols] = jnp.where(own, tile(KB_COL), kf).astype(BF16)
            vx_ref[:, cols] = jnp.where(own, tile(VB_COL), vf).astype(BF16)

    out = jax.ShapeDtypeStruct((s_len, FOX_HEADS * LANE), BF16)
    blk = pl.BlockSpec((tm, FOX_HEADS * LANE), lambda i: (i, 0))
    return pl.pallas_call(
        body, name="fox_prep", grid=(s_len // tm,),
        in_specs=[pl.BlockSpec((tm, N_QKV), lambda i: (i, 0)), pl.BlockSpec((tm, LANE), lambda i: (i, 0))],
        out_specs=[blk, blk, blk], out_shape=[out, out, out],
        compiler_params=_params("parallel"),
    )(zq, c)


def _fox_fwd(qx, kx, vx, t_cap=1024):
    s_len = qx.shape[0]
    t = _div(s_len, t_cap, LANE)
    nq = s_len // t

    def body(q_ref, k_ref, v_ref, o_ref, o32_ref, lse_ref):
        i = pl.program_id(1)
        qs = [q_ref[:, e * LANE:(e + 1) * LANE] for e in range(2)]

        def step(j, carry, diag):
            rows = pl.ds(pl.multiple_of(j * t, t), t)
            new = []
            for e in range(2):
                m, acc = carry[e]
                s = lax.dot_general(qs[e], k_ref[rows, e * LANE:(e + 1) * LANE], (((1,), (1,)), ((), ())),
                                    preferred_element_type=F32)
                if diag:
                    s = jnp.where(_causal(t), s, NEG_INF)
                mn = jnp.maximum(m, jnp.max(s, axis=1, keepdims=True))
                pe = jnp.exp(s - mn)
                p_hi = pe.astype(BF16)
                p_lo = (pe - p_hi.astype(F32)).astype(BF16)
                vs = v_ref[rows, e * LANE:(e + 1) * LANE]
                acc = (acc * jnp.exp(m - mn) + jnp.dot(p_hi, vs, preferred_element_type=F32)
                       + jnp.dot(p_lo, vs, preferred_element_type=F32))
                new.append((mn, acc))
            return tuple(new)

        init = (jnp.full((t, 1), NEG_INF, F32), jnp.zeros((t, LANE), F32))
        carry = lax.fori_loop(0, i, lambda j, c: step(j, c, False), (init, init))
        carry = step(i, carry, True)
        outs, lses = [], []
        for e in range(2):
            m, acc = carry[e]
            l = acc[:, _feature_lane(e, 0):_feature_lane(e, 0) + 1]
            outs.append(acc / l)
            lses.append(m + jnp.log(l))
        out = jnp.where(_own_half(1), outs[1], outs[0])
        o_ref[...] = out.astype(BF16)
        o32_ref[...] = out
        lse_ref[...] = jnp.where(_own_half(1), lses[1], lses[0])

    pair = pl.BlockSpec((s_len, 2 * LANE), lambda p, i: (0, p))
    tile = pl.BlockSpec((t, LANE), lambda p, i: (i, p))
    return pl.pallas_call(
        body, name="fox_fwd", grid=(N_PAIR, nq),
        in_specs=[pl.BlockSpec((t, 2 * LANE), lambda p, i: (i, p)), pair, pair],
        out_specs=[tile, tile, tile],
        out_shape=[jax.ShapeDtypeStruct((s_len, FOX_W), BF16), jax.ShapeDtypeStruct((s_len, FOX_W), F32),
                   jax.ShapeDtypeStruct((s_len, FOX_W), F32)],
        compiler_params=_params("parallel", "parallel"),
    )(qx, kx, vx)


def _fox_stats(o, do, lse, tm=256):
    s_len = o.shape[0]
    tm = _div(s_len, tm, LANE)

    def body(o_ref, do_ref, lse_ref, dox_ref, st_ref):
        lane = lax.broadcasted_iota(jnp.int32, (1, LANE), 1)
        for p in range(N_PAIR):
            cols = slice(p * LANE, (p + 1) * LANE)
            dout = do_ref[:, cols]
            prod = o_ref[:, cols] * dout.astype(F32)
            lse = lse_ref[:, cols]
            st = jnp.zeros((tm, LANE), F32)
            for e in range(2):
                h = 2 * p + e
                dox_ref[:, h * LANE:(h + 1) * LANE] = jnp.where(_own_half(e), dout, jnp.zeros_like(dout))
                st = jnp.where(lane == e, lse[:, e * HEAD_DIM:e * HEAD_DIM + 1], st)
                delta = jnp.sum(jnp.where(_own_half(e), prod, 0.0), axis=1, keepdims=True)
                st = jnp.where(lane == 2 + e, delta, st)
            st_ref[p] = st.T[:8, :]

    row = pl.BlockSpec((tm, FOX_W), lambda i: (i, 0))
    return pl.pallas_call(
        body, name="fox_stats", grid=(s_len // tm,), in_specs=[row, row, row],
        out_specs=[pl.BlockSpec((tm, FOX_HEADS * LANE), lambda i: (i, 0)),
                   pl.BlockSpec((N_PAIR, 8, tm), lambda i: (0, 0, i))],
        out_shape=[jax.ShapeDtypeStruct((s_len, FOX_HEADS * LANE), BF16),
                   jax.ShapeDtypeStruct((N_PAIR, 8, s_len), F32)],
        compiler_params=_params("parallel"),
    )(o, do, lse)


def _fox_bwd(qx, kx, vx, dox, stats, t_cap=512):
    s_len = qx.shape[0]
    t = _div(s_len, t_cap, LANE)
    n = s_len // t

    def body(q_ref, do_ref, st_ref, k_ref, v_ref, dq_ref, dk_ref, dv_ref, dc_ref):
        j = pl.program_id(1)
        lane = lax.broadcasted_iota(jnp.int32, (1, LANE), 1)

        @pl.when(j == 0)
        def _():
            dq_ref[...] = jnp.zeros_like(dq_ref)

        ks = [k_ref[:, e * LANE:(e + 1) * LANE] for e in range(2)]
        vs = [v_ref[:, e * LANE:(e + 1) * LANE] for e in range(2)]

        def step(i, carry, diag):
            rows = pl.ds(pl.multiple_of(i * t, t), t)
            new = []
            dq = jnp.zeros((t, LANE), F32)
            for e in range(2):
                dk, dv, dc = carry[e]
                q = q_ref[rows, e * LANE:(e + 1) * LANE]
                dout = do_ref[rows, e * LANE:(e + 1) * LANE]
                s_t = lax.dot_general(ks[e], q, (((1,), (1,)), ((), ())), preferred_element_type=F32)
                if diag:
                    s_t = jnp.where(_causal(t, keys_first=True), s_t, NEG_INF)
                p_t = jnp.exp(s_t - st_ref[0, e:e + 1, rows])
                dp_t = lax.dot_general(vs[e], dout, (((1,), (1,)), ((), ())), preferred_element_type=F32)
                ds_f = p_t * (dp_t - st_ref[0, 2 + e:3 + e, rows])
                ds_t = ds_f.astype(BF16)
                dc = dc + jnp.sum(ds_f, axis=1, keepdims=True)
                dv = dv + jnp.dot(p_t.astype(BF16), dout, preferred_element_type=F32)
                dk = dk + jnp.dot(ds_t, q, preferred_element_type=F32)
                dq_e = lax.dot_general(ds_t, ks[e], (((0,), (0,)), ((), ())), preferred_element_type=F32)
                dq = dq + jnp.where(_own_half(e), dq_e, 0.0)
                new.append((dk, dv, dc))
            dq_ref[rows, :] += dq * SCALE
            return tuple(new)

        zero = jnp.zeros((t, LANE), F32)
        init = (zero, zero, jnp.zeros((t, 1), F32))
        carry = step(j, (init, init), True)
        (dk0, dv0, dc0), (dk1, dv1, dc1) = lax.fori_loop(j + 1, n, lambda i, c: step(i, c, False), carry)
        dk_ref[...] = jnp.where(_own_half(1), dk1, dk0).astype(BF16)
        dv_ref[...] = jnp.where(_own_half(1), dv1, dv0).astype(BF16)
        dc_ref[...] = jnp.where(lane == 0, -dc0, jnp.where(lane == 1, -dc1, 0.0))

    pair = pl.BlockSpec((s_len, 2 * LANE), lambda p, j: (0, p))
    blk = pl.BlockSpec((t, 2 * LANE), lambda p, j: (j, p))
    tile = pl.BlockSpec((t, LANE), lambda p, j: (j, p))
    return pl.pallas_call(
        body, name="fox_bwd", grid=(N_PAIR, n),
        in_specs=[pair, pair, pl.BlockSpec((1, 8, s_len), lambda p, j: (p, 0, 0)), blk, blk],
        out_specs=[pl.BlockSpec((s_len, LANE), lambda p, j: (0, p)), tile, tile, tile],
        out_shape=[jax.ShapeDtypeStruct((s_len, FOX_W), F32), jax.ShapeDtypeStruct((s_len, FOX_W), BF16),
                   jax.ShapeDtypeStruct((s_len, FOX_W), BF16), jax.ShapeDtypeStruct((s_len, FOX_W), F32)],
        compiler_params=_params("arbitrary", "arbitrary"),
    )(qx, dox, stats, kx, vx)


def _mixer_out(attn_a, attn_b, zg, h, wpa, wpb, wout, g, b, tm=256):
    m = h.shape[0]
    tm = _div(m, tm, 8)

    def body(a_ref, b_ref, ga_ref, gb_ref, h_ref, wpa_ref, wpb_ref, wout_ref, g_ref, bb_ref,
             h1_ref, u_ref, mg_ref, ya_ref, yb_ref):
        ya = jnp.dot(a_ref[...], wpa_ref[...], preferred_element_type=F32)
        yb = jnp.dot(b_ref[...], wpb_ref[...], preferred_element_type=F32)
        merged = (jax.nn.sigmoid(ga_ref[...]) * ya + jax.nn.sigmoid(gb_ref[...]) * yb).astype(BF16)
        u = ALPHA * h_ref[...] + jnp.dot(merged, wout_ref[...], preferred_element_type=F32)
        u_ref[...] = u
        h1_ref[...] = _ln(u, g_ref[...], bb_ref[...])
        mg_ref[...] = merged
        ya_ref[...] = ya.astype(BF16)
        yb_ref[...] = yb.astype(BF16)

    row = pl.BlockSpec((tm, D_MODEL), lambda i: (i, 0))
    att = pl.BlockSpec((tm, SWA_Q), lambda i: (i, 0))
    vec = pl.BlockSpec((1, D_MODEL), lambda i: (0, 0))
    wsm = pl.BlockSpec((SWA_Q, D_MODEL), lambda i: (0, 0))
    return pl.pallas_call(
        body, name="mixer_out", grid=(m // tm,),
        in_specs=[att, att, row, pl.BlockSpec((tm, D_MODEL), lambda i: (i, 1)), row, wsm, wsm,
                  pl.BlockSpec((D_MODEL, D_MODEL), lambda i: (0, 0)), vec, vec],
        out_specs=[row, row, row, row, row],
        out_shape=[jax.ShapeDtypeStruct((m, D_MODEL), F32), jax.ShapeDtypeStruct((m, D_MODEL), F32),
                   jax.ShapeDtypeStruct((m, D_MODEL), BF16), jax.ShapeDtypeStruct((m, D_MODEL), BF16),
                   jax.ShapeDtypeStruct((m, D_MODEL), BF16)],
        compiler_params=_params("parallel"),
    )(attn_a, attn_b, zg, zg, h, wpa, wpb, wout, g, b)


def _gate_bwd(dmerged, ya, yb, zg, tm=256):
    m = dmerged.shape[0]
    tm = _div(m, tm, 8)

    def body(dm_ref, ya_ref, yb_ref, ga_ref, gb_ref, dya_ref, dyb_ref, dga_ref, dgb_ref):
        dm = dm_ref[...]
        for y_ref, g_ref, dy_ref, dg_ref in ((ya_ref, ga_ref, dya_ref, dga_ref), (yb_ref, gb_ref, dyb_ref, dgb_ref)):
            sg = jax.nn.sigmoid(g_ref[...])
            dy_ref[...] = (dm * sg).astype(BF16)
            dg_ref[...] = (dm * y_ref[...].astype(F32) * sg * (1.0 - sg)).astype(BF16)

    row = pl.BlockSpec((tm, D_MODEL), lambda i: (i, 0))
    out = jax.ShapeDtypeStruct((m, D_MODEL), BF16)
    return pl.pallas_call(
        body, name="gate_bwd", grid=(m // tm,),
        in_specs=[row, row, row, row, pl.BlockSpec((tm, D_MODEL), lambda i: (i, 1))],
        out_specs=[row, row, row, row], out_shape=[out, out, out, out],
        compiler_params=_params("parallel"),
    )(dmerged, ya, yb, zg, zg)


def _shift_down(x, k, halo, first):
    rows = lax.broadcasted_iota(jnp.int32, (x.shape[0], 1), 0)
    y = pltpu.roll(x, k, 0)
    for r in range(k):
        fill = jnp.where(first, 0.0, halo[8 - k + r:8 - k + r + 1, :])
        y = jnp.where(rows == r, fill, y)
    return y


def _shift_up(x, k, halo, last):
    n = x.shape[0]
    rows = lax.broadcasted_iota(jnp.int32, (n, 1), 0)
    y = pltpu.roll(x, n - k, 0)
    for r in range(k):
        fill = jnp.where(last, 0.0, halo[r:r + 1, :])
        y = jnp.where(rows == n - k + r, fill, y)
    return y


def _conv_act(gate, gate_m1, gate_m2, cw, cb):
    return cb + cw[0:1, :] * gate_m2 + cw[1:2, :] * gate_m1 + cw[2:3, :] * gate


def _ffn_in(h1, wfi, tm=256):
    s_len = h1.shape[0]
    tm = _div(s_len, tm, 8)

    def body(a_ref, w_ref, o_ref):
        a = a_ref[...].astype(BF16)
        for g in range(2):
            for c in range(N_CHUNK):
                o_ref[c, g] = jnp.dot(a, w_ref[N_CHUNK * g + c], preferred_element_type=F32)

    return pl.pallas_call(
        body, name="ffn_in", grid=(s_len // tm,),
        in_specs=[pl.BlockSpec((tm, D_MODEL), lambda i: (i, 0)),
                  pl.BlockSpec((N_DEV, D_MODEL, FF_CHUNK), lambda i: (0, 0, 0))],
        out_specs=pl.BlockSpec((N_CHUNK, 2, tm, FF_CHUNK), lambda i: (0, 0, i, 0)),
        out_shape=jax.ShapeDtypeStruct((N_CHUNK, 2, s_len, FF_CHUNK), F32),
        compiler_params=_params("parallel"),
    )(h1, wfi)


def _conv_fwd(gu, cw, cb, tm=256):
    s_len = gu.shape[2]
    tm = _div(s_len, tm, 8)
    hb = tm // 8

    def body(gu_ref, gp_ref, cw_ref, cb_ref, o_ref):
        first = pl.program_id(1) == 0
        gate = gu_ref[0, 0]
        halo = gp_ref[0, 0]
        conv = _conv_act(gate, _shift_down(gate, 1, halo, first), _shift_down(gate, 2, halo, first),
                         cw_ref[0], cb_ref[0])
        o_ref[0] = (conv * jax.nn.sigmoid(conv) * gu_ref[0, 1]).astype(BF16)

    return pl.pallas_call(
        body, name="conv_fwd", grid=(N_CHUNK, s_len // tm),
        in_specs=[pl.BlockSpec((1, 2, tm, FF_CHUNK), lambda c, i: (c, 0, i, 0)),
                  pl.BlockSpec((1, 1, 8, FF_CHUNK), lambda c, i: (c, 0, jnp.maximum(i * hb - 1, 0), 0)),
                  pl.BlockSpec((1, 8, FF_CHUNK), lambda c, i: (c, 0, 0)),
                  pl.BlockSpec((1, 1, FF_CHUNK), lambda c, i: (c, 0, 0))],
        out_specs=pl.BlockSpec((1, tm, FF_CHUNK), lambda c, i: (c, i, 0)),
        out_shape=jax.ShapeDtypeStruct((N_CHUNK, s_len, FF_CHUNK), BF16),
        compiler_params=_params("parallel", "parallel"),
    )(gu, gu, cw, cb)


def _ffn_out_ln(act, wfo, res, g, b, tm=256):
    s_len = res.shape[0]
    tm = _div(s_len, tm, 8)

    def body(a_ref, w_ref, res_ref, g_ref, b_ref, u_ref, y_ref):
        u = ALPHA * res_ref[...]
        for c in range(N_CHUNK):
            u = u + jnp.dot(a_ref[c], w_ref[c], preferred_element_type=F32)
        u_ref[...] = u
        y_ref[...] = _ln(u, g_ref[...], b_ref[...])

    row = pl.BlockSpec((tm, D_MODEL), lambda i: (i, 0))
    vec = pl.BlockSpec((1, D_MODEL), lambda i: (0, 0))
    return pl.pallas_call(
        body, name="ffn_out_ln", grid=(s_len // tm,),
        in_specs=[pl.BlockSpec((N_CHUNK, tm, FF_CHUNK), lambda i: (0, i, 0)),
                  pl.BlockSpec((N_CHUNK, FF_CHUNK, D_MODEL), lambda i: (0, 0, 0)), row, vec, vec],
        out_specs=[row, row],
        out_shape=[jax.ShapeDtypeStruct((s_len, D_MODEL), F32), jax.ShapeDtypeStruct((s_len, D_MODEL), F32)],
        compiler_params=_params("parallel"),
    )(act, wfo, res, g, b)


def _d_act(du, wfo, tm=256):
    s_len = du.shape[0]
    tm = _div(s_len, tm, 8)

    def body(du_ref, w_ref, o_ref):
        du_b = du_ref[...].astype(BF16)
        for c in range(N_CHUNK):
            o_ref[c] = lax.dot_general(du_b, w_ref[c], (((1,), (1,)), ((), ())), preferred_element_type=F32)

    return pl.pallas_call(
        body, name="d_act", grid=(s_len // tm,),
        in_specs=[pl.BlockSpec((tm, D_MODEL), lambda i: (i, 0)),
                  pl.BlockSpec((N_CHUNK, FF_CHUNK, D_MODEL), lambda i: (0, 0, 0))],
        out_specs=pl.BlockSpec((N_CHUNK, tm, FF_CHUNK), lambda i: (0, i, 0)),
        out_shape=jax.ShapeDtypeStruct((N_CHUNK, s_len, FF_CHUNK), F32),
        compiler_params=_params("parallel"),
    )(du, wfo)


def _g_w_ffn_out(act, du, tm=512):
    s_len = du.shape[0]
    tm = _div(s_len, tm, 8)
    steps = s_len // tm

    def body(a_ref, g_ref, o_ref, acc_ref):
        s = pl.program_id(1)

        @pl.when(s == 0)
        def _():
            acc_ref[...] = jnp.zeros_like(acc_ref)

        acc_ref[...] += lax.dot_general(a_ref[0], g_ref[...].astype(BF16), (((0,), (0,)), ((), ())),
                                        preferred_element_type=F32)

        @pl.when(s == steps - 1)
        def _():
            o_ref[0] = acc_ref[...]

    return pl.pallas_call(
        body, name="g_w_ffn_out", grid=(N_CHUNK, steps),
        in_specs=[pl.BlockSpec((1, tm, FF_CHUNK), lambda c, s: (c, s, 0)),
                  pl.BlockSpec((tm, D_MODEL), lambda c, s: (s, 0))],
        out_specs=pl.BlockSpec((1, FF_CHUNK, D_MODEL), lambda c, s: (c, 0, 0)),
        out_shape=jax.ShapeDtypeStruct((N_CHUNK, FF_CHUNK, D_MODEL), F32),
        scratch_shapes=[pltpu.VMEM((FF_CHUNK, D_MODEL), F32)],
        compiler_params=_params("parallel", "arbitrary"),
    )(act, du)


def _d_h1(dgu, wfi, res, tm=256):
    s_len = res.shape[0]
    tm = _div(s_len, tm, 8)

    def body(a_ref, w_ref, res_ref, o_ref):
        acc = ALPHA * res_ref[...]
        for g in range(2):
            for c in range(N_CHUNK):
                acc = acc + lax.dot_general(a_ref[c, g], w_ref[N_CHUNK * g + c], (((1,), (1,)), ((), ())),
                                            preferred_element_type=F32)
        o_ref[...] = acc

    row = pl.BlockSpec((tm, D_MODEL), lambda i: (i, 0))
    return pl.pallas_call(
        body, name="d_h1", grid=(s_len // tm,),
        in_specs=[pl.BlockSpec((N_CHUNK, 2, tm, FF_CHUNK), lambda i: (0, 0, i, 0)),
                  pl.BlockSpec((N_DEV, D_MODEL, FF_CHUNK), lambda i: (0, 0, 0)), row],
        out_specs=row, out_shape=jax.ShapeDtypeStruct((s_len, D_MODEL), F32),
        compiler_params=_params("parallel"),
    )(dgu, wfi, res)


def _g_w_ffn_in(h1, dgu, tm=512):
    s_len = h1.shape[0]
    tm = _div(s_len, tm, 8)
    steps = s_len // tm

    def body(a_ref, g_ref, o_ref, acc_ref):
        s = pl.program_id(1)

        @pl.when(s == 0)
        def _():
            acc_ref[...] = jnp.zeros_like(acc_ref)

        acc_ref[...] += lax.dot_general(a_ref[...].astype(BF16), g_ref[0, 0], (((0,), (0,)), ((), ())),
                                        preferred_element_type=F32)

        @pl.when(s == steps - 1)
        def _():
            o_ref[0] = acc_ref[...]

    return pl.pallas_call(
        body, name="g_w_ffn_in", grid=(N_DEV, steps),
        in_specs=[pl.BlockSpec((tm, D_MODEL), lambda d, s: (s, 0)),
                  pl.BlockSpec((1, 1, tm, FF_CHUNK), lambda d, s: (d % N_CHUNK, d // N_CHUNK, s, 0))],
        out_specs=pl.BlockSpec((1, D_MODEL, FF_CHUNK), lambda d, s: (d, 0, 0)),
        out_shape=jax.ShapeDtypeStruct((N_DEV, D_MODEL, FF_CHUNK), F32),
        scratch_shapes=[pltpu.VMEM((D_MODEL, FF_CHUNK), F32)],
        compiler_params=_params("parallel", "arbitrary"),
    )(h1, dgu)


def _conv_bwd(gu, dact, cw, cb, tm=256):
    s_len = gu.shape[2]
    tm = _div(s_len, tm, 8)
    nrow = s_len // tm
    hb = tm // 8

    def dconv_of(conv, up, da):
        sg = jax.nn.sigmoid(conv)
        return da * up * (sg * (1.0 + conv * (1.0 - sg)))

    def body(gu_ref, gp_ref, gun_ref, da_ref, dan_ref, cw_ref, cb_ref, dgu_ref, dcw_ref):
        i = pl.program_id(1)
        first = i == 0
        last = i == nrow - 1
        cw = cw_ref[0]
        cb = cb_ref[0]
        gate = gu_ref[0, 0]
        halo = gp_ref[0, 0]
        g_m1 = _shift_down(gate, 1, halo, first)
        g_m2 = _shift_down(gate, 2, halo, first)
        conv = _conv_act(gate, g_m1, g_m2, cw, cb)
        da = da_ref[0]
        sg = jax.nn.sigmoid(conv)
        dgu_ref[0, 1] = (da * conv * sg).astype(BF16)
        dconv = dconv_of(conv, gu_ref[0, 1], da)
        gate_n = gun_ref[0, 0]
        tail = gate[tm - 8:, :]
        conv_n = _conv_act(gate_n, _shift_down(gate_n, 1, tail, False), _shift_down(gate_n, 2, tail, False), cw, cb)
        dconv_n = dconv_of(conv_n, gun_ref[0, 1], dan_ref[0])
        dgate = (cw[2:3, :] * dconv + cw[1:2, :] * _shift_up(dconv, 1, dconv_n, last)
                 + cw[0:1, :] * _shift_up(dconv, 2, dconv_n, last))
        dgu_ref[0, 0] = dgate.astype(BF16)

        @pl.when(first)
        def _():
            dcw_ref[...] = jnp.zeros_like(dcw_ref)

        row = lax.broadcasted_iota(jnp.int32, (8, 1), 0)
        part = jnp.zeros((8, FF_CHUNK), F32)
        for r, term in enumerate((dconv * g_m2, dconv * g_m1, dconv * gate, dconv)):
            part = jnp.where(row == r, jnp.sum(term, axis=0, keepdims=True), part)
        dcw_ref[0] += part

    nxt = lambda i: jnp.minimum((i + 1) * hb, s_len // 8 - 1)
    main = pl.BlockSpec((1, 2, tm, FF_CHUNK), lambda c, i: (c, 0, i, 0))
    return pl.pallas_call(
        body, name="conv_bwd", grid=(N_CHUNK, nrow),
        in_specs=[main,
                  pl.BlockSpec((1, 1, 8, FF_CHUNK), lambda c, i: (c, 0, jnp.maximum(i * hb - 1, 0), 0)),
                  pl.BlockSpec((1, 2, 8, FF_CHUNK), lambda c, i: (c, 0, nxt(i), 0)),
                  pl.BlockSpec((1, tm, FF_CHUNK), lambda c, i: (c, i, 0)),
                  pl.BlockSpec((1, 8, FF_CHUNK), lambda c, i: (c, nxt(i), 0)),
                  pl.BlockSpec((1, 8, FF_CHUNK), lambda c, i: (c, 0, 0)),
                  pl.BlockSpec((1, 1, FF_CHUNK), lambda c, i: (c, 0, 0))],
        out_specs=[main, pl.BlockSpec((1, 8, FF_CHUNK), lambda c, i: (c, 0, 0))],
        out_shape=[jax.ShapeDtypeStruct((N_CHUNK, 2, s_len, FF_CHUNK), BF16),
                   jax.ShapeDtypeStruct((N_CHUNK, 8, FF_CHUNK), F32)],
        compiler_params=_params("parallel", "arbitrary"),
    )(gu, gu, gu, dact, dact, cw, cb)


def _loss_head(y, target, tm=256):
    m, d = y.shape
    tm = _div(m, tm, 8)

    def body(y_ref, t_ref, dy_ref, loss_ref):
        @pl.when(pl.program_id(0) == 0)
        def _():
            loss_ref[...] = jnp.zeros_like(loss_ref)

        err = y_ref[...] - t_ref[...]
        dy_ref[...] = err / d
        loss_ref[...] += 0.5 * jnp.sum(jnp.sum(err * err, axis=1, keepdims=True) / d, axis=0, keepdims=True)

    row = pl.BlockSpec((tm, d), lambda i: (i, 0))
    return pl.pallas_call(
        body, name="loss_head", grid=(m // tm,), in_specs=[row, row],
        out_specs=[row, pl.BlockSpec((8, LANE), lambda i: (0, 0))],
        out_shape=[jax.ShapeDtypeStruct((m, d), F32), jax.ShapeDtypeStruct((8, LANE), F32)],
        compiler_params=_params("arbitrary"),
    )(y, target)


def _sum_devices(r_ref):
    acc = r_ref[0].astype(F32)
    for d in range(1, N_DEV):
        acc = acc + r_ref[d].astype(F32)
    return acc


def _sum8(recv):
    rows = recv.shape[1]
    tr = _div(rows, ROW_BLOCK, 8)

    def body(r_ref, o_ref):
        o_ref[...] = _sum_devices(r_ref)

    return pl.pallas_call(
        body, name="sum8", grid=(rows // tr,),
        in_specs=[pl.BlockSpec((N_DEV, tr, LANE), lambda i: (0, i, 0))],
        out_specs=pl.BlockSpec((tr, LANE), lambda i: (i, 0)),
        out_shape=jax.ShapeDtypeStruct((rows, LANE), F32),
        compiler_params=_params("parallel"),
    )(recv)


def _adamw_math(w, g, m, v):
    m = ADAM_B1 * m + (1.0 - ADAM_B1) * g
    v = ADAM_B2 * v + (1.0 - ADAM_B2) * (g * g)
    m_hat = m / (1.0 - ADAM_B1 ** ADAM_STEP)
    v_hat = v / (1.0 - ADAM_B2 ** ADAM_STEP)
    return -ADAM_LR * (m_hat / (jnp.sqrt(v_hat) + ADAM_EPS) + ADAM_WD * w), m, v


def _adamw_rows(w, g, m, v, name):
    rows = w.shape[0]
    tr = _div(rows, ROW_BLOCK, 8)

    def body(w_ref, g_ref, m_ref, v_ref, d_ref, mo_ref, vo_ref):
        d_ref[...], mo_ref[...], vo_ref[...] = _adamw_math(w_ref[...], g_ref[...], m_ref[...], v_ref[...])

    blk = pl.BlockSpec((tr, LANE), lambda i: (i, 0))
    out = jax.ShapeDtypeStruct((rows, LANE), F32)
    return pl.pallas_call(
        body, name=name, grid=(rows // tr,), in_specs=[blk, blk, blk, blk], out_specs=[blk, blk, blk],
        out_shape=[out, out, out], compiler_params=_params("parallel"),
    )(w, g, m, v)


def _adamw_shard(recv, w, m, v, layer, prev, name):
    _, k, n = recv.shape
    tk = _div(k, 128, 16)

    def body(r_ref, w_ref, m_ref, v_ref, *rest):
        g_ref, d_ref, mo_ref, vo_ref = rest[-4:]
        g = _sum_devices(r_ref)
        g_ref[0] = g
        d_ref[0], mo_ref[0], vo_ref[0] = _adamw_math(w_ref[0], g, m_ref[0], v_ref[0])

    blk = pl.BlockSpec((1, tk, n), lambda i: (layer, i, 0))
    out = jax.ShapeDtypeStruct((DEPTH, k, n), F32)
    carried = [] if prev is None else list(prev)
    return pl.pallas_call(
        body, name=name, grid=(k // tk,),
        in_specs=[pl.BlockSpec((N_DEV, tk, n), lambda i: (0, i, 0)), blk, blk, blk]
        + [pl.BlockSpec(memory_space=pl.ANY)] * len(carried),
        out_specs=[blk, blk, blk, blk], out_shape=[out, out, out, out],
        input_output_aliases={4 + j: j for j in range(len(carried))},
        compiler_params=_params("parallel"),
    )(recv, w, m, v, *carried)


def _to_rows(flat, rows):
    flat = flat.reshape(-1)
    return jnp.pad(flat, (0, rows * LANE - flat.shape[0])).reshape(rows, LANE)


def _pad_cols_z(a):
    f0 = N_QKV
    g0 = N_QKV + FOX_HEADS
    pad = jnp.zeros(a.shape[:-1] + (F_PAD - FOX_HEADS,), a.dtype)
    return jnp.concatenate([a[..., :f0], a[..., g0:], a[..., f0:g0], pad], axis=-1)


def _unpad_cols_z(a):
    f0 = N_QKV + N_GATE
    return jnp.concatenate([a[..., :N_QKV], a[..., f0:f0 + FOX_HEADS], a[..., N_QKV:f0]], axis=-1)


def _shards_to_cols(g):
    _, k, n = g.shape
    return g.transpose(1, 0, 2).reshape(k, N_DEV * n)


def _cols_to_shards(full):
    k, n = full.shape
    return full.reshape(k, N_DEV, n // N_DEV).transpose(1, 0, 2)


def _layer_fwd(h, w, p):
    zq = _linear(h, w["wq"], bias=p["bq"], out_dtype=BF16, name="z_qkv", tn=768)
    zg = _linear(h, w["wg"], bias=p["bg"], name="z_gate", tn=768)
    qx, kx, vx = _fox_prep(zq, _cumsum_logf(zg))
    attn_a, lse_a = _swa_fwd(zq, p["sinks"])
    attn_b, attn_b32, lse_b = _fox_fwd(qx, kx, vx)
    h1, u1, merged, ya, yb = _mixer_out(attn_a, attn_b, zg, h, w["w_proj_a"], w["w_proj_b"], w["w_out"],
                                        p["ln_mix_g"], p["ln_mix_b"])
    gu = _ffn_in(h1, w["w_ffn_in"])
    act = _conv_fwd(gu, p["conv_w"], p["conv_b"])
    u2, h2 = _ffn_out_ln(act, w["w_ffn_out"], h1, p["ln_ffn_g"], p["ln_ffn_b"])
    saved = dict(h=h, zq=zq, zg=zg, qx=qx, kx=kx, vx=vx, attn_a=attn_a, lse_a=lse_a, attn_b=attn_b, attn_b32=attn_b32, lse_b=lse_b, h1=h1,
                 u1=u1, merged=merged, ya=ya, yb=yb, gu=gu, act=act, u2=u2)
    return h2, saved


def _layer_bwd(dh2, sv, w, p):
    s_len = dh2.shape[0]
    du2, d_ffn_g, d_ffn_b = _ln_bwd(dh2, sv["u2"], p["ln_ffn_g"], name="ln_ffn_bwd")
    dact = _d_act(du2, w["w_ffn_out"])
    g_ffn_out = _g_w_ffn_out(sv["act"], du2)
    dgu, dcw = _conv_bwd(sv["gu"], dact, p["conv_w"], p["conv_b"])
    dcw = dcw.transpose(1, 0, 2).reshape(8, D_FF)
    dh1 = _d_h1(dgu, w["w_ffn_in"], du2)
    g_ffn_in = _g_w_ffn_in(sv["h1"], dgu)
    du1, d_mix_g, d_mix_b = _ln_bwd(dh1, sv["u1"], p["ln_mix_g"], name="ln_mix_bwd")
    dmerged = _linear(du1, w["w_out"], trans_b=True, name="d_merged", tn=1024)
    g_out = _linear_tn(sv["merged"], du1, name="g_w_out", tn=1024)
    dya, dyb, dga, dgb = _gate_bwd(dmerged, sv["ya"], sv["yb"], sv["zg"])
    dattn_a = _linear(dya, w["w_proj_a"], trans_b=True, out_dtype=BF16, name="d_attn_a", tn=512)
    dattn_b = _linear(dyb, w["w_proj_b"], trans_b=True, out_dtype=BF16, name="d_attn_b", tn=512)
    g_proj_a = _linear_tn(sv["attn_a"], dya, name="g_w_proj_a", tk=512, tn=1024)
    g_proj_b = _linear_tn(sv["attn_b"], dyb, name="g_w_proj_b", tk=512, tn=1024)
    dq_a, dk_a, dv_a, dsinks = _swa_bwd(sv["zq"], p["sinks"], sv["attn_a"], dattn_a, sv["lse_a"])
    dox, stats = _fox_stats(sv["attn_b32"], dattn_b, sv["lse_b"])
    dq_b, dk_b, dv_b, dcc = _fox_bwd(sv["qx"], sv["kx"], sv["vx"], dox, stats)
    dc = jnp.pad(dcc.reshape(s_len, N_PAIR, LANE)[:, :, :2].reshape(s_len, FOX_HEADS), ((0, 0), (0, LANE - FOX_HEADS)))
    df = _forget_bwd(dc, sv["zg"])
    dz = jnp.concatenate([dq_a, dk_a.astype(BF16), dv_a.astype(BF16), dq_b.astype(BF16), dk_b, dv_b, dga, dgb, df,
                          jnp.zeros((s_len, F_PAD - LANE), BF16)], axis=1)
    dh = _linear(dz, w["w_in_p"], trans_b=True, res=du1, res_scale=ALPHA, name="d_h", tn=512)
    g_in = _unpad_cols_z(_linear_tn(sv["h"], dz, name="g_w_in", tn=768))
    g_b_in = _unpad_cols_z(_colsum(dz, name="g_b_in"))
    big = dict(w_in=_cols_to_shards(g_in), w_proj_a=_cols_to_shards(g_proj_a), w_proj_b=_cols_to_shards(g_proj_b),
               w_out=g_out.reshape(N_DEV, D_MODEL // N_DEV, D_MODEL), w_ffn_in=g_ffn_in,
               w_ffn_out=g_ffn_out.reshape(N_DEV, D_FF // N_DEV, D_MODEL))
    small = dict(ln_mix_g=d_mix_g, ln_mix_b=d_mix_b, b_in=g_b_in, attn_sinks=dsinks[:, :SWA_HEADS],
                 ln_ffn_g=d_ffn_g, ln_ffn_b=d_ffn_b, conv_w=dcw[:3], conv_b=dcw[3:4])
    return dh, big, small


def _layer_weights(w_in, w_proj_a, w_proj_b, w_out, w_ffn_in, w_ffn_out):
    w_in_p = _pad_cols_z(_shards_to_cols(w_in))
    return dict(w_in_p=w_in_p, wq=w_in_p[:, :N_QKV], wg=w_in_p[:, N_QKV:], w_proj_a=_shards_to_cols(w_proj_a),
                w_proj_b=_shards_to_cols(w_proj_b), w_out=w_out.reshape(D_MODEL, D_MODEL), w_ffn_in=w_ffn_in,
                w_ffn_out=w_ffn_out.reshape(N_CHUNK, FF_CHUNK, D_MODEL))


def _layer_params(r):
    b_p = _pad_cols_z(r["b_in"].reshape(1, N_IN))
    return dict(
        bq=b_p[:, :N_QKV], bg=b_p[:, N_QKV:],
        sinks=jnp.pad(r["attn_sinks"].reshape(1, SWA_HEADS), ((0, 0), (0, LANE - SWA_HEADS))),
        ln_mix_g=r["ln_mix_g"].reshape(1, D_MODEL), ln_mix_b=r["ln_mix_b"].reshape(1, D_MODEL),
        ln_ffn_g=r["ln_ffn_g"].reshape(1, D_MODEL), ln_ffn_b=r["ln_ffn_b"].reshape(1, D_MODEL),
        conv_w=jnp.pad(r["conv_w"], ((0, 5), (0, 0))).reshape(8, N_CHUNK, FF_CHUNK).transpose(1, 0, 2),
        conv_b=r["conv_b"].reshape(N_CHUNK, 1, FF_CHUNK))


def _local_step(x, target, ws, ps, after_layer):
    h = x
    saved = []
    for w, p in zip(ws, ps):
        h, sv = _layer_fwd(h, w, p)
        saved.append(sv)
    dh, loss = _loss_head(h, target)
    outs = [None] * DEPTH
    for l in reversed(range(DEPTH)):
        dh, big, small = _layer_bwd(dh, saved[l], ws[l], ps[l])
        outs[l] = after_layer(l, big, small, loss[0, 0])
    return dh, outs


def kernel(x, ln_mix_g, ln_mix_b, w_in, b_in, attn_sinks, w_proj_a, w_proj_b, w_out, ln_ffn_g, ln_ffn_b, w_ffn_in, conv_w, conv_b, w_ffn_out, loss_target, m_ln_mix_g, m_ln_mix_b, m_w_in, m_b_in, m_attn_sinks, m_w_proj_a, m_w_proj_b, m_w_out, m_ln_ffn_g, m_ln_ffn_b, m_w_ffn_in, m_conv_w, m_conv_b, m_w_ffn_out, v_ln_mix_g, v_ln_mix_b, v_w_in, v_b_in, v_attn_sinks, v_w_proj_a, v_w_proj_b, v_w_out, v_ln_ffn_g, v_ln_ffn_b, v_w_ffn_in, v_conv_w, v_conv_b, v_w_ffn_out):
    wts = dict(ln_mix_g=ln_mix_g, ln_mix_b=ln_mix_b, w_in=w_in, b_in=b_in, attn_sinks=attn_sinks, w_proj_a=w_proj_a,
               w_proj_b=w_proj_b, w_out=w_out, ln_ffn_g=ln_ffn_g, ln_ffn_b=ln_ffn_b, w_ffn_in=w_ffn_in,
               conv_w=conv_w, conv_b=conv_b, w_ffn_out=w_ffn_out)
    mom = dict(ln_mix_g=m_ln_mix_g, ln_mix_b=m_ln_mix_b, w_in=m_w_in, b_in=m_b_in, attn_sinks=m_attn_sinks,
               w_proj_a=m_w_proj_a, w_proj_b=m_w_proj_b, w_out=m_w_out, ln_ffn_g=m_ln_ffn_g, ln_ffn_b=m_ln_ffn_b,
               w_ffn_in=m_w_ffn_in, conv_w=m_conv_w, conv_b=m_conv_b, w_ffn_out=m_w_ffn_out)
    vel = dict(ln_mix_g=v_ln_mix_g, ln_mix_b=v_ln_mix_b, w_in=v_w_in, b_in=v_b_in, attn_sinks=v_attn_sinks,
               w_proj_a=v_w_proj_a, w_proj_b=v_w_proj_b, w_out=v_w_out, ln_ffn_g=v_ln_ffn_g, ln_ffn_b=v_ln_ffn_b,
               w_ffn_in=v_w_ffn_in, conv_w=v_conv_w, conv_b=v_conv_b, w_ffn_out=v_w_ffn_out)
    names = list(wts)
    big_names = [n for n, _, _ in BIG]
    small_names = [n for n, _ in SMALL]
    me = 4 * lax.axis_index("x") + 2 * lax.axis_index("y") + lax.axis_index("c")
    cw_shard = D_FF // N_DEV

    gathered = _exchange([(wts[n].astype(BF16), True) for n in big_names] + [(conv_w, True)], "gather_weights")
    conv_full = gathered[-1].transpose(1, 2, 0, 3).reshape(DEPTH, 3, D_FF)
    ws = [_layer_weights(*[g[:, l] for g in gathered[:-1]]) for l in range(DEPTH)]
    ps = [_layer_params(dict(b_in=b_in[l], attn_sinks=attn_sinks[l], ln_mix_g=ln_mix_g[l], ln_mix_b=ln_mix_b[l],
                             ln_ffn_g=ln_ffn_g[l], ln_ffn_b=ln_ffn_b[l], conv_w=conv_full[l], conv_b=conv_b[l]))
          for l in range(DEPTH)]

    def exchange_layer(l, big, small, loss_part):
        vec = jnp.concatenate([small[n].reshape(-1) for n in small_names] + [loss_part.reshape(1)])
        return _exchange([(big[n].astype(BF16), False) for n in big_names] + [(_to_rows(vec, SMALL_LAYER_ROWS), True)],
                         "exchange_grads_%d" % l)

    grad_x, recv = _local_step(x[0], loss_target[0], ws, ps, exchange_layer)

    big_out = {}
    for t, n in enumerate(big_names):
        outs = None
        for l in reversed(range(DEPTH)):
            outs = _adamw_shard(recv[l][t], wts[n], mom[n], vel[n], l, outs, "adamw_%s_%d" % (n, l))
        big_out[n] = outs
    small_sum = [_sum8(recv[l][-1]).reshape(-1) for l in range(DEPTH)]
    g_small = {}
    off = 0
    for n, size in SMALL:
        g_small[n] = jnp.stack([small_sum[l][off:off + size] for l in range(DEPTH)])
        off += size
    loss = small_sum[0][off]
    g_small["conv_w"] = lax.dynamic_slice_in_dim(g_small["conv_w"].reshape(DEPTH, 3, D_FF), me * cw_shard, cw_shard,
                                                 axis=2)
    g_small = {n: g_small[n].reshape(wts[n].shape) for n in small_names}

    def pack_small(tree):
        return _to_rows(jnp.concatenate([tree[n].reshape(-1) for n in small_names]), SMALL_ROWS)

    small_out = (pack_small(g_small),) + tuple(_adamw_rows(pack_small(wts), pack_small(g_small), pack_small(mom),
                                                           pack_small(vel), "adamw_small"))

    def result(j):
        out = {n: big_out[n][j] for n in big_names}
        flat = small_out[j].reshape(-1)
        off = 0
        for n in small_names:
            out[n] = flat[off:off + wts[n].size].reshape(wts[n].shape)
            off += wts[n].size
        return [out[n] for n in names]

    return (loss, grad_x[None], *result(0), *result(1), *result(2), *result(3))
```

```python
import functools

import jax
import jax.numpy as jnp
from jax import lax
from jax.experimental import pallas as pl
from jax.experimental.pallas import tpu as pltpu

F32 = jnp.float32
BF16 = jnp.bfloat16
MESH = pl.DeviceIdType.MESH

N_DEV = 8
DEPTH = 2
D_MODEL = 1024
HEAD_DIM = 64
SWA_Q = 512
SWA_KV = 128
FOX_W = 512
FOX_HEADS = 8
SWA_HEADS = 8
D_FF = 2816
N_IN = 4360
N_QKV = SWA_Q + 2 * SWA_KV + 3 * FOX_W
N_GATE = 2 * D_MODEL
F_PAD = 256
N_ZG = N_GATE + F_PAD
N_ZP = N_QKV + N_ZG
LN_EPS = 1e-5
NEG_INF = -1e30
ALPHA = (2 * DEPTH) ** 0.25
SCALE = HEAD_DIM ** -0.5
SLOPES = tuple(2.0 ** (-8.0 * (h + 1) / SWA_HEADS) for h in range(SWA_HEADS))

ADAM_LR = 0.001
ADAM_B1 = 0.9
ADAM_B2 = 0.999
ADAM_EPS = 1e-08
ADAM_WD = 0.01
ADAM_STEP = 10

LANE = 128
VMEM_LIMIT = 56 * 1024 * 1024

BIG = (("w_in", (D_MODEL, N_IN), 1), ("w_proj_a", (SWA_Q, D_MODEL), 1), ("w_proj_b", (FOX_W, D_MODEL), 1),
       ("w_out", (D_MODEL, D_MODEL), 0), ("w_ffn_in", (D_MODEL, 2 * D_FF), 1), ("w_ffn_out", (D_FF, D_MODEL), 0))
SMALL = (("ln_mix_g", D_MODEL), ("ln_mix_b", D_MODEL), ("b_in", N_IN), ("attn_sinks", SWA_HEADS),
         ("ln_ffn_g", D_MODEL), ("ln_ffn_b", D_MODEL), ("conv_w", 3 * D_FF), ("conv_b", D_FF))
ROW_BLOCK = 512
SMALL_LAYER_ROWS = -(-(sum(n for _, n in SMALL) + 1) // (8 * LANE)) * 8
SMALL_ROWS = ROW_BLOCK
FF_CHUNK = 2 * D_FF // N_DEV
N_CHUNK = D_FF // FF_CHUNK


def _div(n, cap, unit):
    if n <= cap:
        return n
    best = None
    for t in range(unit, cap + 1, unit):
        if n % t == 0:
            best = t
    assert best is not None, (n, cap, unit)
    return best


def _params(*sem):
    return pltpu.CompilerParams(dimension_semantics=sem, vmem_limit_bytes=VMEM_LIMIT)


def _peer(r):
    x, y, c = lax.axis_index("x"), lax.axis_index("y"), lax.axis_index("c")
    px = 1 - x if (r >> 2) & 1 else x
    py = 1 - y if (r >> 1) & 1 else y
    pc = 1 - c if r & 1 else c
    return (px, py, pc), 4 * px + 2 * py + pc


class _Comm:
    def __init__(self, tensors):
        self.arrays = [x for x, _ in tensors]
        self.gathers = [g for _, g in tensors]
        self.n = len(tensors)
        self.out_shape = [jax.ShapeDtypeStruct((N_DEV,) + (x.shape if g else x.shape[1:]), x.dtype)
                          for x, g in tensors]
        self.specs = [pl.BlockSpec(memory_space=pl.ANY)] * self.n
        self.scratch = [pltpu.SemaphoreType.DMA((N_DEV - 1, self.n)), pltpu.SemaphoreType.DMA((N_DEV - 1, self.n)),
                        pltpu.SemaphoreType.DMA((self.n,))]

    def _copies(self, x_refs, out_refs, sems):
        send_sems, recv_sems, local_sems = sems
        _, me = _peer(0)

        def src(t, idx):
            return x_refs[t] if self.gathers[t] else x_refs[t].at[idx]

        def remote(r, t, mine):
            peer, pid = _peer(r)
            return pltpu.make_async_remote_copy(src_ref=src(t, pid), dst_ref=out_refs[t].at[me if mine else pid],
                                                send_sem=send_sems.at[r - 1, t], recv_sem=recv_sems.at[r - 1, t],
                                                device_id=peer, device_id_type=MESH)

        pairs = [(r, t) for r in range(1, N_DEV) for t in range(self.n)]
        local = [pltpu.make_async_copy(src(t, me), out_refs[t].at[me], local_sems.at[t]) for t in range(self.n)]
        return local, [remote(r, t, True) for r, t in pairs], lambda: [remote(r, t, False) for r, t in pairs]

    def start(self, x_refs, out_refs, sems):
        local, sent, _ = self._copies(x_refs, out_refs, sems)
        for cp in local + sent:
            cp.start()

    def wait(self, x_refs, out_refs, sems):
        local, sent, landing = self._copies(x_refs, out_refs, sems)
        for cp in landing():
            cp.wait_recv()
        for cp in sent:
            cp.wait_send()
        for cp in local:
            cp.wait()


def _exchange(tensors, name):
    comm = _Comm(tensors)
    n = comm.n

    def body(*refs):
        comm.start(refs[:n], refs[n:2 * n], refs[2 * n:])
        comm.wait(refs[:n], refs[n:2 * n], refs[2 * n:])

    return pl.pallas_call(body, name=name, out_shape=comm.out_shape, in_specs=comm.specs, out_specs=comm.specs,
                          scratch_shapes=comm.scratch)(*comm.arrays)


def _with_comm(comm, n_in, n_out, first, last, compute):
    nc = comm.n if comm is not None else 0

    def body(*refs):
        ins, x_refs = refs[:n_in], refs[n_in:n_in + nc]
        outs = refs[n_in + nc:n_in + nc + n_out]
        out_refs = refs[n_in + nc + n_out:n_in + 2 * nc + n_out]
        sems = refs[n_in + 2 * nc + n_out:]
        if nc:
            @pl.when(first())
            def _():
                comm.start(x_refs, out_refs, sems)

        compute(*ins, *outs)
        if nc:
            @pl.when(last())
            def _():
                comm.wait(x_refs, out_refs, sems)

    return body


def _linear(a, b, *, name, trans_b=False, bias=None, res=None, res_scale=1.0, out_dtype=F32, tm=512, tn=640):
    m, k = a.shape
    n = b.shape[0] if trans_b else b.shape[1]
    tm = _div(m, tm, 8)
    tn = _div(n, tn, LANE)
    dn = (((1,), (1,)), ((), ())) if trans_b else (((1,), (0,)), ((), ()))

    def body(*refs):
        a_ref, b_ref = refs[0], refs[1]
        rest = list(refs[2:])
        bias_ref = rest.pop(0) if bias is not None else None
        res_ref = rest.pop(0) if res is not None else None
        o_ref = rest.pop(0)
        acc = lax.dot_general(a_ref[...].astype(BF16), b_ref[...].astype(BF16), dn, preferred_element_type=F32)
        if bias_ref is not None:
            acc = acc + bias_ref[...]
        if res_ref is not None:
            acc = acc + res_scale * res_ref[...].astype(F32)
        o_ref[...] = acc.astype(out_dtype)

    in_specs = [pl.BlockSpec((tm, k), lambda i, j: (i, 0)),
                pl.BlockSpec((tn, k), lambda i, j: (j, 0)) if trans_b else pl.BlockSpec((k, tn), lambda i, j: (0, j))]
    args = [a, b]
    if bias is not None:
        in_specs.append(pl.BlockSpec((1, tn), lambda i, j: (0, j)))
        args.append(bias)
    if res is not None:
        in_specs.append(pl.BlockSpec((tm, tn), lambda i, j: (i, j)))
        args.append(res)
    return pl.pallas_call(
        body, name=name, grid=(m // tm, n // tn), in_specs=in_specs,
        out_specs=pl.BlockSpec((tm, tn), lambda i, j: (i, j)),
        out_shape=jax.ShapeDtypeStruct((m, n), out_dtype),
        compiler_params=_params("parallel", "arbitrary"),
    )(*args)


def _linear_tn(a, g, *, name, tk=1024, tn=640, tm=512):
    m, k = a.shape
    n = g.shape[1]
    tk = _div(k, tk, LANE)
    tn = _div(n, tn, LANE)
    tm = _div(m, tm, 8)
    steps = m // tm

    def body(a_ref, g_ref, o_ref, acc_ref):
        s = pl.program_id(2)

        @pl.when(s == 0)
        def _():
            acc_ref[...] = jnp.zeros_like(acc_ref)

        acc_ref[...] += lax.dot_general(a_ref[...].astype(BF16), g_ref[...].astype(BF16), (((0,), (0,)), ((), ())),
                                        preferred_element_type=F32)

        @pl.when(s == steps - 1)
        def _():
            o_ref[...] = acc_ref[...]

    return pl.pallas_call(
        body, name=name, grid=(k // tk, n // tn, steps),
        in_specs=[pl.BlockSpec((tm, tk), lambda i, j, s: (s, i)), pl.BlockSpec((tm, tn), lambda i, j, s: (s, j))],
        out_specs=pl.BlockSpec((tk, tn), lambda i, j, s: (i, j)),
        out_shape=jax.ShapeDtypeStruct((k, n), F32),
        scratch_shapes=[pltpu.VMEM((tk, tn), F32)],
        compiler_params=_params("parallel", "parallel", "arbitrary"),
    )(a, g)


def _colsum(g, *, name, tm=512):
    m, n = g.shape
    tm = _div(m, tm, 8)

    def body(g_ref, o_ref):
        @pl.when(pl.program_id(0) == 0)
        def _():
            o_ref[...] = jnp.zeros_like(o_ref)

        o_ref[...] += jnp.sum(g_ref[...].astype(F32), axis=0, keepdims=True)

    return pl.pallas_call(
        body, name=name, grid=(m // tm,),
        in_specs=[pl.BlockSpec((tm, n), lambda i: (i, 0))],
        out_specs=pl.BlockSpec((1, n), lambda i: (0, 0)),
        out_shape=jax.ShapeDtypeStruct((1, n), F32),
        compiler_params=_params("arbitrary"),
    )(g)


def _ln(u, g, b):
    mu = jnp.mean(u, axis=-1, keepdims=True)
    d = u - mu
    var = jnp.mean(d * d, axis=-1, keepdims=True)
    return d * lax.rsqrt(var + LN_EPS) * g + b


def _ln_bwd(dy, u, g, *, name, tm=256):
    m, d = u.shape
    tm = _div(m, tm, 8)

    def body(dy_ref, u_ref, g_ref, du_ref, dg_ref, db_ref):
        @pl.when(pl.program_id(0) == 0)
        def _():
            dg_ref[...] = jnp.zeros_like(dg_ref)
            db_ref[...] = jnp.zeros_like(db_ref)

        dy = dy_ref[...]
        uu = u_ref[...]
        mu = jnp.mean(uu, axis=-1, keepdims=True)
        dd = uu - mu
        rstd = lax.rsqrt(jnp.mean(dd * dd, axis=-1, keepdims=True) + LN_EPS)
        xhat = dd * rstd
        dxh = dy * g_ref[...]
        m1 = jnp.mean(dxh, axis=-1, keepdims=True)
        m2 = jnp.mean(dxh * xhat, axis=-1, keepdims=True)
        du_ref[...] = rstd * (dxh - m1 - xhat * m2)
        dg_ref[...] += jnp.sum(dy * xhat, axis=0, keepdims=True)
        db_ref[...] += jnp.sum(dy, axis=0, keepdims=True)

    row = pl.BlockSpec((tm, d), lambda i: (i, 0))
    vec = pl.BlockSpec((1, d), lambda i: (0, 0))
    return pl.pallas_call(
        body, name=name, grid=(m // tm,), in_specs=[row, row, vec], out_specs=[row, vec, vec],
        out_shape=[jax.ShapeDtypeStruct((m, d), F32), jax.ShapeDtypeStruct((1, d), F32),
                   jax.ShapeDtypeStruct((1, d), F32)],
        compiler_params=_params("arbitrary"),
    )(dy, u, g)


def _linear_res_ln(a, w, res, g, b, *, name, tm=256):
    m, k = a.shape
    d = w.shape[1]
    tm = _div(m, tm, 8)

    def body(a_ref, w_ref, res_ref, g_ref, b_ref, u_ref, y_ref):
        u = ALPHA * res_ref[...] + jnp.dot(a_ref[...].astype(BF16), w_ref[...], preferred_element_type=F32)
        u_ref[...] = u
        y_ref[...] = _ln(u, g_ref[...], b_ref[...])

    row = pl.BlockSpec((tm, d), lambda i: (i, 0))
    vec = pl.BlockSpec((1, d), lambda i: (0, 0))
    return pl.pallas_call(
        body, name=name, grid=(m // tm,),
        in_specs=[pl.BlockSpec((tm, k), lambda i: (i, 0)), pl.BlockSpec((k, d), lambda i: (0, 0)), row, vec, vec],
        out_specs=[row, row],
        out_shape=[jax.ShapeDtypeStruct((m, d), F32), jax.ShapeDtypeStruct((m, d), F32)],
        compiler_params=_params("parallel"),
    )(a, w, res, g, b)


def _tri(n, upper):
    r = lax.broadcasted_iota(jnp.int32, (n, n), 0)
    c = lax.broadcasted_iota(jnp.int32, (n, n), 1)
    return jnp.where((c >= r) if upper else (c <= r), 1.0, 0.0).astype(F32)


def _cumsum_logf(zg):
    s = zg.shape[0]
    nb = s // LANE
    fcol = N_GATE // LANE

    def body(f_ref, c_ref, carry_ref):
        @pl.when(pl.program_id(0) == 0)
        def _():
            carry_ref[...] = jnp.zeros_like(carry_ref)

        f = f_ref[...]
        logf = jnp.minimum(f, 0.0) - jnp.log(1.0 + jnp.exp(-jnp.abs(f)))
        c = jnp.dot(_tri(LANE, False), logf, precision=lax.Precision.HIGHEST, preferred_element_type=F32)
        c = c + carry_ref[0:1, :]
        c_ref[...] = c
        carry_ref[...] = jnp.broadcast_to(c[LANE - 1:LANE, :], carry_ref.shape)

    return pl.pallas_call(
        body, name="cumsum_logf", grid=(nb,),
        in_specs=[pl.BlockSpec((LANE, LANE), lambda i: (i, fcol))],
        out_specs=pl.BlockSpec((LANE, LANE), lambda i: (i, 0)),
        out_shape=jax.ShapeDtypeStruct((s, LANE), F32),
        scratch_shapes=[pltpu.VMEM((8, LANE), F32)],
        compiler_params=_params("arbitrary"),
    )(zg)


def _forget_bwd(dc, zg):
    s = zg.shape[0]
    nb = s // LANE
    fcol = N_GATE // LANE

    def body(dc_ref, f_ref, o_ref, carry_ref):
        @pl.when(pl.program_id(0) == 0)
        def _():
            carry_ref[...] = jnp.zeros_like(carry_ref)

        dc = dc_ref[...]
        dlogf = jnp.dot(_tri(LANE, True), dc, precision=lax.Precision.HIGHEST, preferred_element_type=F32)
        dlogf = dlogf + carry_ref[0:1, :]
        o_ref[...] = (dlogf * jax.nn.sigmoid(-f_ref[...])).astype(BF16)
        carry_ref[...] = jnp.broadcast_to(dlogf[0:1, :], carry_ref.shape)

    return pl.pallas_call(
        body, name="forget_bwd", grid=(nb,),
        in_specs=[pl.BlockSpec((LANE, LANE), lambda i: (nb - 1 - i, 0)),
                  pl.BlockSpec((LANE, LANE), lambda i: (nb - 1 - i, fcol))],
        out_specs=pl.BlockSpec((LANE, LANE), lambda i: (nb - 1 - i, 0)),
        out_shape=jax.ShapeDtypeStruct((s, LANE), BF16),
        scratch_shapes=[pltpu.VMEM((8, LANE), F32)],
        compiler_params=_params("arbitrary"),
    )(dc, zg)


KA_COL = SWA_Q // LANE
VA_COL = KA_COL + 1


def _half_masks():
    lane = lax.broadcasted_iota(jnp.int32, (1, LANE), 1)
    hi = lane >= HEAD_DIM
    return (jnp.logical_not(hi), hi)


def _both_halves(x, sel):
    xs = jnp.where(sel, x, 0.0)
    return xs + pltpu.roll(xs, HEAD_DIM, 1)


def _swa_geometry(i):
    r = lax.broadcasted_iota(jnp.int32, (LANE, 2 * LANE), 0)
    c = lax.broadcasted_iota(jnp.int32, (LANE, 2 * LANE), 1)
    dist = r + LANE - c
    valid = (dist >= 0) & (dist < LANE) & ((c >= LANE) | (i > 0))
    return valid, -dist.astype(F32)


def _swa_specs(nb):
    prev = lambda i: jnp.maximum(i - 1, 0)
    return [pl.BlockSpec((LANE, SWA_Q), lambda i: (i, 0)),
            pl.BlockSpec((LANE, LANE), lambda i: (i, KA_COL)), pl.BlockSpec((LANE, LANE), lambda i: (i, VA_COL)),
            pl.BlockSpec((LANE, LANE), lambda i: (prev(i), KA_COL)),
            pl.BlockSpec((LANE, LANE), lambda i: (prev(i), VA_COL))]


def _swa_fwd(zq, sinks):
    s_len = zq.shape[0]
    nb = s_len // LANE

    def body(q_ref, kc_ref, vc_ref, kp_ref, vp_ref, sink_ref, o_ref, lse_ref):
        i = pl.program_id(0)
        halves = _half_masks()
        lane = lax.broadcasted_iota(jnp.int32, (1, LANE), 1)
        valid, negdist = _swa_geometry(i)
        kcat = jnp.concatenate([kp_ref[...], kc_ref[...]], axis=0).astype(F32)
        vcat = jnp.concatenate([vp_ref[...], vc_ref[...]], axis=0).astype(F32)
        lse_acc = jnp.zeros((LANE, LANE), F32)
        for hk in range(2):
            kb = _both_halves(kcat, halves[hk]).astype(BF16)
            vb = _both_halves(vcat, halves[hk])
            v_e = [jnp.where(halves[e], vb, 0.0).astype(BF16) for e in range(2)]
            for pp in range(2):
                p = 2 * hk + pp
                qp = q_ref[:, p * LANE:(p + 1) * LANE]
                acc = jnp.zeros((LANE, LANE), F32)
                for e in range(2):
                    hq = 2 * p + e
                    qh = jnp.where(halves[e], qp, jnp.zeros_like(qp))
                    s = lax.dot_general(qh, kb, (((1,), (1,)), ((), ())), preferred_element_type=F32) * SCALE
                    s = jnp.where(valid, s + SLOPES[hq] * negdist, NEG_INF)
                    sink = sink_ref[0:1, hq:hq + 1]
                    m = jnp.maximum(jnp.max(s, axis=1, keepdims=True), sink)
                    pe = jnp.exp(s - m)
                    den = jnp.sum(pe, axis=1, keepdims=True) + jnp.exp(sink - m)
                    acc = acc + jnp.dot((pe / den).astype(BF16), v_e[e], preferred_element_type=F32)
                    lse_acc = jnp.where(lane == hq, m + jnp.log(den), lse_acc)
                o_ref[:, p * LANE:(p + 1) * LANE] = acc.astype(BF16)
        lse_ref[...] = lse_acc

    return pl.pallas_call(
        body, name="swa_fwd", grid=(nb,),
        in_specs=_swa_specs(nb) + [pl.BlockSpec((1, LANE), lambda i: (0, 0))],
        out_specs=[pl.BlockSpec((LANE, SWA_Q), lambda i: (i, 0)), pl.BlockSpec((LANE, LANE), lambda i: (i, 0))],
        out_shape=[jax.ShapeDtypeStruct((s_len, SWA_Q), BF16), jax.ShapeDtypeStruct((s_len, LANE), F32)],
        compiler_params=_params("parallel"),
    )(zq, zq, zq, zq, zq, sinks)


def _swa_bwd(zq, sinks, o, do, lse):
    s_len = zq.shape[0]
    nb = s_len // LANE

    def body(q_ref, kc_ref, vc_ref, kp_ref, vp_ref, sink_ref, o_ref, do_ref, lse_ref, dq_ref, dk_ref, dv_ref, ds_ref):
        i = pl.program_id(0)
        halves = _half_masks()
        lane = lax.broadcasted_iota(jnp.int32, (1, LANE), 1)
        valid, negdist = _swa_geometry(i)
        kcat = jnp.concatenate([kp_ref[...], kc_ref[...]], axis=0).astype(F32)
        vcat = jnp.concatenate([vp_ref[...], vc_ref[...]], axis=0).astype(F32)
        lse_all = lse_ref[...]
        dk_tot = jnp.zeros((2 * LANE, LANE), F32)
        dv_tot = jnp.zeros((2 * LANE, LANE), F32)
        dsink = jnp.zeros((1, LANE), F32)
        for hk in range(2):
            kb = _both_halves(kcat, halves[hk])
            vb = _both_halves(vcat, halves[hk])
            k_e = [jnp.where(halves[e], kb, 0.0).astype(BF16) for e in range(2)]
            v_e = [jnp.where(halves[e], vb, 0.0).astype(BF16) for e in range(2)]
            kb = kb.astype(BF16)
            dk_acc = jnp.zeros((2 * LANE, LANE), F32)
            dv_acc = jnp.zeros((2 * LANE, LANE), F32)
            for pp in range(2):
                p = 2 * hk + pp
                cols = slice(p * LANE, (p + 1) * LANE)
                qp = q_ref[:, cols]
                dop = do_ref[:, cols]
                prod = dop.astype(F32) * o_ref[:, cols].astype(F32)
                dq_acc = jnp.zeros((LANE, LANE), F32)
                for e in range(2):
                    hq = 2 * p + e
                    qh = jnp.where(halves[e], qp, jnp.zeros_like(qp))
                    doh = jnp.where(halves[e], dop, jnp.zeros_like(dop))
                    delta = jnp.sum(jnp.where(halves[e], prod, 0.0), axis=1, keepdims=True)
                    lse_h = lse_all[:, hq:hq + 1]
                    s = lax.dot_general(qh, kb, (((1,), (1,)), ((), ())), preferred_element_type=F32) * SCALE
                    s = jnp.where(valid, s + SLOPES[hq] * negdist, NEG_INF)
                    pr = jnp.exp(s - lse_h)
                    dp = lax.dot_general(doh, v_e[e], (((1,), (1,)), ((), ())), preferred_element_type=F32)
                    ds = pr * (dp - delta)
                    sink = sink_ref[0:1, hq:hq + 1]
                    dsink_h = -jnp.sum(jnp.exp(sink - lse_h) * delta, axis=0, keepdims=True)
                    dsink = dsink + jnp.where(lane == hq, dsink_h, 0.0)
                    dsb = (ds * SCALE).astype(BF16)
                    dq_acc = dq_acc + jnp.dot(dsb, k_e[e], preferred_element_type=F32)
                    dk_acc = dk_acc + lax.dot_general(dsb, qh, (((0,), (0,)), ((), ())), preferred_element_type=F32)
                    dv_acc = dv_acc + lax.dot_general(pr.astype(BF16), doh, (((0,), (0,)), ((), ())),
                                                      preferred_element_type=F32)
                dq_ref[:, cols] = dq_acc.astype(BF16)
            dk_tot = dk_tot + jnp.where(halves[hk], dk_acc + pltpu.roll(dk_acc, HEAD_DIM, 1), 0.0)
            dv_tot = dv_tot + jnp.where(halves[hk], dv_acc + pltpu.roll(dv_acc, HEAD_DIM, 1), 0.0)

        @pl.when(i == 0)
        def _():
            ds_ref[...] = jnp.zeros_like(ds_ref)

        ds_ref[...] += dsink
        cur = pl.ds(pl.multiple_of(i * LANE, LANE), LANE)
        dk_ref[cur, :] = dk_tot[LANE:, :]
        dv_ref[cur, :] = dv_tot[LANE:, :]

        @pl.when(i > 0)
        def _():
            prv = pl.ds(pl.multiple_of((i - 1) * LANE, LANE), LANE)
            dk_ref[prv, :] += dk_tot[:LANE, :]
            dv_ref[prv, :] += dv_tot[:LANE, :]

    blk512 = pl.BlockSpec((LANE, SWA_Q), lambda i: (i, 0))
    full = pl.BlockSpec((s_len, LANE), lambda i: (0, 0))
    vec = pl.BlockSpec((1, LANE), lambda i: (0, 0))
    return pl.pallas_call(
        body, name="swa_bwd", grid=(nb,),
        in_specs=_swa_specs(nb) + [vec, blk512, blk512, pl.BlockSpec((LANE, LANE), lambda i: (i, 0))],
        out_specs=[blk512, full, full, vec],
        out_shape=[jax.ShapeDtypeStruct((s_len, SWA_Q), BF16), jax.ShapeDtypeStruct((s_len, LANE), F32),
                   jax.ShapeDtypeStruct((s_len, LANE), F32), jax.ShapeDtypeStruct((1, LANE), F32)],
        compiler_params=_params("arbitrary"),
    )(zq, zq, zq, zq, zq, sinks, o, do, lse)


QB_COL = (SWA_Q + 2 * SWA_KV) // LANE
KB_COL = QB_COL + FOX_W // LANE
VB_COL = KB_COL + FOX_W // LANE
N_PAIR = FOX_HEADS // 2


def _causal(t, keys_first=False):
    r = lax.broadcasted_iota(jnp.int32, (t, t), 0)
    c = lax.broadcasted_iota(jnp.int32, (t, t), 1)
    return c >= r if keys_first else r >= c


N_SPLIT = 3


def _own_half(e):
    hi = lax.broadcasted_iota(jnp.int32, (1, LANE), 1) >= HEAD_DIM
    return hi if e else jnp.logical_not(hi)


def _feature_lane(e, t):
    return HEAD_DIM * (1 - e) + t


def _fox_prep(zq, c, tm=256):
    s_len = zq.shape[0]
    tm = _div(s_len, tm, 8)

    def body(z_ref, c_ref, qx_ref, kx_ref, vx_ref):
        lane = lax.broadcasted_iota(jnp.int32, (1, LANE), 1)
        for h in range(FOX_HEADS):
            p, e = divmod(h, 2)
            own = _own_half(e)
            tile = lambda col: z_ref[:, (col + p) * LANE:(col + p + 1) * LANE].astype(F32)
            rest = c_ref[:, h:h + 1]
            qf = jnp.zeros((tm, LANE), F32)
            kf = jnp.zeros((tm, LANE), F32)
            for t in range(N_SPLIT):
                part = rest.astype(BF16).astype(F32)
                rest = rest - part
                qf = jnp.where(lane == _feature_lane(e, t), part, qf)
                qf = jnp.where(lane == _feature_lane(e, N_SPLIT + t), 1.0, qf)
                kf = jnp.where(lane == _feature_lane(e, t), 1.0, kf)
                kf = jnp.where(lane == _feature_lane(e, N_SPLIT + t), -part, kf)
            vf = jnp.where(lane == _feature_lane(e, 0), 1.0, 0.0)
            cols = slice(h * LANE, (h + 1) * LANE)
            qx_ref[:, cols] = jnp.where(own, tile(QB_COL) * SCALE, qf).astype(BF16)
            kx_ref[:, cols] = jnp.where(own, tile(KB_COL), kf).astype(BF16)
            vx_ref[:, cols] = jnp.where(own, tile(VB_COL), vf).astype(BF16)

    out = jax.ShapeDtypeStruct((s_len, FOX_HEADS * LANE), BF16)
    blk = pl.BlockSpec((tm, FOX_HEADS * LANE), lambda i: (i, 0))
    return pl.pallas_call(
        body, name="fox_prep", grid=(s_len // tm,),
        in_specs=[pl.BlockSpec((tm, N_QKV), lambda i: (i, 0)), pl.BlockSpec((tm, LANE), lambda i: (i, 0))],
        out_specs=[blk, blk, blk], out_shape=[out, out, out],
        compiler_params=_params("parallel"),
    )(zq, c)


def _comm_parts(comm):
    return ([], [], [], []) if comm is None else (comm.specs, comm.out_shape, comm.scratch, comm.arrays)


def _fox_fwd(qx, kx, vx, comm=None, t_cap=1024):
    s_len = qx.shape[0]
    t = _div(s_len, t_cap, LANE)
    nq = s_len // t
    c_specs, c_shapes, c_scratch, c_arrays = _comm_parts(comm)

    def compute(q_ref, k_ref, v_ref, o_ref, o32_ref, lse_ref):
        i = pl.program_id(1)
        qs = [q_ref[:, e * LANE:(e + 1) * LANE] for e in range(2)]

        def step(j, carry, diag):
            rows = pl.ds(pl.multiple_of(j * t, t), t)
            new = []
            for e in range(2):
                m, acc = carry[e]
                s = lax.dot_general(qs[e], k_ref[rows, e * LANE:(e + 1) * LANE], (((1,), (1,)), ((), ())),
                                    preferred_element_type=F32)
                if diag:
                    s = jnp.where(_causal(t), s, NEG_INF)
                mn = jnp.maximum(m, jnp.max(s, axis=1, keepdims=True))
                pe = jnp.exp(s - mn)
                p_hi = pe.astype(BF16)
                p_lo = (pe - p_hi.astype(F32)).astype(BF16)
                vs = v_ref[rows, e * LANE:(e + 1) * LANE]
                acc = (acc * jnp.exp(m - mn) + jnp.dot(p_hi, vs, preferred_element_type=F32)
                       + jnp.dot(p_lo, vs, preferred_element_type=F32))
                new.append((mn, acc))
            return tuple(new)

        init = (jnp.full((t, 1), NEG_INF, F32), jnp.zeros((t, LANE), F32))
        carry = lax.fori_loop(0, i, lambda j, c: step(j, c, False), (init, init))
        carry = step(i, carry, True)
        outs, lses = [], []
        for e in range(2):
            m, acc = carry[e]
            l = acc[:, _feature_lane(e, 0):_feature_lane(e, 0) + 1]
            outs.append(acc / l)
            lses.append(m + jnp.log(l))
        out = jnp.where(_own_half(1), outs[1], outs[0])
        o_ref[...] = out.astype(BF16)
        o32_ref[...] = out
        lse_ref[...] = jnp.where(_own_half(1), lses[1], lses[0])

    body = _with_comm(comm, 3, 3, lambda: (pl.program_id(0) == 0) & (pl.program_id(1) == 0),
                      lambda: (pl.program_id(0) == N_PAIR - 1) & (pl.program_id(1) == nq - 1), compute)
    pair = pl.BlockSpec((s_len, 2 * LANE), lambda p, i: (0, p))
    tile = pl.BlockSpec((t, LANE), lambda p, i: (i, p))
    outs = pl.pallas_call(
        body, name="fox_fwd" if comm is None else "fox_fwd_comm", grid=(N_PAIR, nq),
        in_specs=[pl.BlockSpec((t, 2 * LANE), lambda p, i: (i, p)), pair, pair] + c_specs,
        out_specs=[tile, tile, tile] + c_specs,
        out_shape=[jax.ShapeDtypeStruct((s_len, FOX_W), BF16), jax.ShapeDtypeStruct((s_len, FOX_W), F32),
                   jax.ShapeDtypeStruct((s_len, FOX_W), F32)] + c_shapes,
        scratch_shapes=c_scratch,
        compiler_params=_params("arbitrary", "arbitrary"),
    )(qx, kx, vx, *c_arrays)
    return outs[0], outs[1], outs[2], outs[3:]


def _fox_stats(o, do, lse, tm=256):
    s_len = o.shape[0]
    tm = _div(s_len, tm, LANE)

    def body(o_ref, do_ref, lse_ref, dox_ref, st_ref):
        lane = lax.broadcasted_iota(jnp.int32, (1, LANE), 1)
        for p in range(N_PAIR):
            cols = slice(p * LANE, (p + 1) * LANE)
            dout = do_ref[:, cols]
            prod = o_ref[:, cols] * dout.astype(F32)
            lse = lse_ref[:, cols]
            st = jnp.zeros((tm, LANE), F32)
            for e in range(2):
                h = 2 * p + e
                dox_ref[:, h * LANE:(h + 1) * LANE] = jnp.where(_own_half(e), dout, jnp.zeros_like(dout))
                st = jnp.where(lane == e, lse[:, e * HEAD_DIM:e * HEAD_DIM + 1], st)
                delta = jnp.sum(jnp.where(_own_half(e), prod, 0.0), axis=1, keepdims=True)
                st = jnp.where(lane == 2 + e, delta, st)
            st_ref[p] = st.T[:8, :]

    row = pl.BlockSpec((tm, FOX_W), lambda i: (i, 0))
    return pl.pallas_call(
        body, name="fox_stats", grid=(s_len // tm,), in_specs=[row, row, row],
        out_specs=[pl.BlockSpec((tm, FOX_HEADS * LANE), lambda i: (i, 0)),
                   pl.BlockSpec((N_PAIR, 8, tm), lambda i: (0, 0, i))],
        out_shape=[jax.ShapeDtypeStruct((s_len, FOX_HEADS * LANE), BF16),
                   jax.ShapeDtypeStruct((N_PAIR, 8, s_len), F32)],
        compiler_params=_params("parallel"),
    )(o, do, lse)


def _fox_bwd(qx, kx, vx, dox, stats, comm=None, t_cap=512):
    s_len = qx.shape[0]
    t = _div(s_len, t_cap, LANE)
    n = s_len // t
    c_specs, c_shapes, c_scratch, c_arrays = _comm_parts(comm)

    def compute(q_ref, do_ref, st_ref, k_ref, v_ref, dq_ref, dk_ref, dv_ref, dc_ref):
        j = pl.program_id(1)
        lane = lax.broadcasted_iota(jnp.int32, (1, LANE), 1)

        @pl.when(j == 0)
        def _():
            dq_ref[...] = jnp.zeros_like(dq_ref)

        ks = [k_ref[:, e * LANE:(e + 1) * LANE] for e in range(2)]
        vs = [v_ref[:, e * LANE:(e + 1) * LANE] for e in range(2)]

        def step(i, carry, diag):
            rows = pl.ds(pl.multiple_of(i * t, t), t)
            new = []
            dq = jnp.zeros((t, LANE), F32)
            for e in range(2):
                dk, dv, dc = carry[e]
                q = q_ref[rows, e * LANE:(e + 1) * LANE]
                dout = do_ref[rows, e * LANE:(e + 1) * LANE]
                s_t = lax.dot_general(ks[e], q, (((1,), (1,)), ((), ())), preferred_element_type=F32)
                if diag:
                    s_t = jnp.where(_causal(t, keys_first=True), s_t, NEG_INF)
                p_t = jnp.exp(s_t - st_ref[0, e:e + 1, rows])
                dp_t = lax.dot_general(vs[e], dout, (((1,), (1,)), ((), ())), preferred_element_type=F32)
                ds_f = p_t * (dp_t - st_ref[0, 2 + e:3 + e, rows])
                ds_t = ds_f.astype(BF16)
                dc = dc + jnp.sum(ds_f, axis=1, keepdims=True)
                dv = dv + jnp.dot(p_t.astype(BF16), dout, preferred_element_type=F32)
                dk = dk + jnp.dot(ds_t, q, preferred_element_type=F32)
                dq_e = lax.dot_general(ds_t, ks[e], (((0,), (0,)), ((), ())), preferred_element_type=F32)
                dq = dq + jnp.where(_own_half(e), dq_e, 0.0)
                new.append((dk, dv, dc))
            dq_ref[rows, :] += dq * SCALE
            return tuple(new)

        zero = jnp.zeros((t, LANE), F32)
        init = (zero, zero, jnp.zeros((t, 1), F32))
        carry = step(j, (init, init), True)
        (dk0, dv0, dc0), (dk1, dv1, dc1) = lax.fori_loop(j + 1, n, lambda i, c: step(i, c, False), carry)
        dk_ref[...] = jnp.where(_own_half(1), dk1, dk0).astype(BF16)
        dv_ref[...] = jnp.where(_own_half(1), dv1, dv0).astype(BF16)
        dc_ref[...] = jnp.where(lane == 0, -dc0, jnp.where(lane == 1, -dc1, 0.0))

    body = _with_comm(comm, 5, 4, lambda: (pl.program_id(0) == 0) & (pl.program_id(1) == 0),
                      lambda: (pl.program_id(0) == N_PAIR - 1) & (pl.program_id(1) == n - 1), compute)
    pair = pl.BlockSpec((s_len, 2 * LANE), lambda p, j: (0, p))
    blk = pl.BlockSpec((t, 2 * LANE), lambda p, j: (j, p))
    tile = pl.BlockSpec((t, LANE), lambda p, j: (j, p))
    outs = pl.pallas_call(
        body, name="fox_bwd" if comm is None else "fox_bwd_comm", grid=(N_PAIR, n),
        in_specs=[pair, pair, pl.BlockSpec((1, 8, s_len), lambda p, j: (p, 0, 0)), blk, blk] + c_specs,
        out_specs=[pl.BlockSpec((s_len, LANE), lambda p, j: (0, p)), tile, tile, tile] + c_specs,
        out_shape=[jax.ShapeDtypeStruct((s_len, FOX_W), F32), jax.ShapeDtypeStruct((s_len, FOX_W), BF16),
                   jax.ShapeDtypeStruct((s_len, FOX_W), BF16), jax.ShapeDtypeStruct((s_len, FOX_W), F32)] + c_shapes,
        scratch_shapes=c_scratch,
        compiler_params=_params("arbitrary", "arbitrary"),
    )(qx, dox, stats, kx, vx, *c_arrays)
    return outs[0], outs[1], outs[2], outs[3], outs[4:]


def _mixer_out(attn_a, attn_b, zg, h, wpa, wpb, wout, g, b, tm=256):
    m = h.shape[0]
    tm = _div(m, tm, 8)

    def body(a_ref, b_ref, ga_ref, gb_ref, h_ref, wpa_ref, wpb_ref, wout_ref, g_ref, bb_ref,
             h1_ref, u_ref, mg_ref, ya_ref, yb_ref):
        ya = jnp.dot(a_ref[...], wpa_ref[...], preferred_element_type=F32)
        yb = jnp.dot(b_ref[...], wpb_ref[...], preferred_element_type=F32)
        merged = (jax.nn.sigmoid(ga_ref[...]) * ya + jax.nn.sigmoid(gb_ref[...]) * yb).astype(BF16)
        u = ALPHA * h_ref[...] + jnp.dot(merged, wout_ref[...], preferred_element_type=F32)
        u_ref[...] = u
        h1_ref[...] = _ln(u, g_ref[...], bb_ref[...])
        mg_ref[...] = merged
        ya_ref[...] = ya.astype(BF16)
        yb_ref[...] = yb.astype(BF16)

    row = pl.BlockSpec((tm, D_MODEL), lambda i: (i, 0))
    att = pl.BlockSpec((tm, SWA_Q), lambda i: (i, 0))
    vec = pl.BlockSpec((1, D_MODEL), lambda i: (0, 0))
    wsm = pl.BlockSpec((SWA_Q, D_MODEL), lambda i: (0, 0))
    return pl.pallas_call(
        body, name="mixer_out", grid=(m // tm,),
        in_specs=[att, att, row, pl.BlockSpec((tm, D_MODEL), lambda i: (i, 1)), row, wsm, wsm,
                  pl.BlockSpec((D_MODEL, D_MODEL), lambda i: (0, 0)), vec, vec],
        out_specs=[row, row, row, row, row],
        out_shape=[jax.ShapeDtypeStruct((m, D_MODEL), F32), jax.ShapeDtypeStruct((m, D_MODEL), F32),
                   jax.ShapeDtypeStruct((m, D_MODEL), BF16), jax.ShapeDtypeStruct((m, D_MODEL), BF16),
                   jax.ShapeDtypeStruct((m, D_MODEL), BF16)],
        compiler_params=_params("parallel"),
    )(attn_a, attn_b, zg, zg, h, wpa, wpb, wout, g, b)


def _gate_bwd(dmerged, ya, yb, zg, tm=256):
    m = dmerged.shape[0]
    tm = _div(m, tm, 8)

    def body(dm_ref, ya_ref, yb_ref, ga_ref, gb_ref, dya_ref, dyb_ref, dga_ref, dgb_ref):
        dm = dm_ref[...]
        for y_ref, g_ref, dy_ref, dg_ref in ((ya_ref, ga_ref, dya_ref, dga_ref), (yb_ref, gb_ref, dyb_ref, dgb_ref)):
            sg = jax.nn.sigmoid(g_ref[...])
            dy_ref[...] = (dm * sg).astype(BF16)
            dg_ref[...] = (dm * y_ref[...].astype(F32) * sg * (1.0 - sg)).astype(BF16)

    row = pl.BlockSpec((tm, D_MODEL), lambda i: (i, 0))
    out = jax.ShapeDtypeStruct((m, D_MODEL), BF16)
    return pl.pallas_call(
        body, name="gate_bwd", grid=(m // tm,),
        in_specs=[row, row, row, row, pl.BlockSpec((tm, D_MODEL), lambda i: (i, 1))],
        out_specs=[row, row, row, row], out_shape=[out, out, out, out],
        compiler_params=_params("parallel"),
    )(dmerged, ya, yb, zg, zg)


def _shift_down(x, k, halo, first):
    rows = lax.broadcasted_iota(jnp.int32, (x.shape[0], 1), 0)
    y = pltpu.roll(x, k, 0)
    for r in range(k):
        fill = jnp.where(first, 0.0, halo[8 - k + r:8 - k + r + 1, :])
        y = jnp.where(rows == r, fill, y)
    return y


def _shift_up(x, k, halo, last):
    n = x.shape[0]
    rows = lax.broadcasted_iota(jnp.int32, (n, 1), 0)
    y = pltpu.roll(x, n - k, 0)
    for r in range(k):
        fill = jnp.where(last, 0.0, halo[r:r + 1, :])
        y = jnp.where(rows == n - k + r, fill, y)
    return y


def _conv_act(gate, gate_m1, gate_m2, cw, cb):
    return cb + cw[0:1, :] * gate_m2 + cw[1:2, :] * gate_m1 + cw[2:3, :] * gate


def _ffn_in(h1, wfi, tm=256):
    s_len = h1.shape[0]
    tm = _div(s_len, tm, 8)

    def body(a_ref, w_ref, o_ref):
        a = a_ref[...].astype(BF16)
        for g in range(2):
            for c in range(N_CHUNK):
                o_ref[c, g] = jnp.dot(a, w_ref[N_CHUNK * g + c], preferred_element_type=F32)

    return pl.pallas_call(
        body, name="ffn_in", grid=(s_len // tm,),
        in_specs=[pl.BlockSpec((tm, D_MODEL), lambda i: (i, 0)),
                  pl.BlockSpec((N_DEV, D_MODEL, FF_CHUNK), lambda i: (0, 0, 0))],
        out_specs=pl.BlockSpec((N_CHUNK, 2, tm, FF_CHUNK), lambda i: (0, 0, i, 0)),
        out_shape=jax.ShapeDtypeStruct((N_CHUNK, 2, s_len, FF_CHUNK), F32),
        compiler_params=_params("parallel"),
    )(h1, wfi)


def _conv_fwd(gu, cw, cb, tm=256):
    s_len = gu.shape[2]
    tm = _div(s_len, tm, 8)
    hb = tm // 8

    def body(gu_ref, gp_ref, cw_ref, cb_ref, o_ref):
        first = pl.program_id(1) == 0
        gate = gu_ref[0, 0]
        halo = gp_ref[0, 0]
        conv = _conv_act(gate, _shift_down(gate, 1, halo, first), _shift_down(gate, 2, halo, first),
                         cw_ref[0], cb_ref[0])
        o_ref[0] = (conv * jax.nn.sigmoid(conv) * gu_ref[0, 1]).astype(BF16)

    return pl.pallas_call(
        body, name="conv_fwd", grid=(N_CHUNK, s_len // tm),
        in_specs=[pl.BlockSpec((1, 2, tm, FF_CHUNK), lambda c, i: (c, 0, i, 0)),
                  pl.BlockSpec((1, 1, 8, FF_CHUNK), lambda c, i: (c, 0, jnp.maximum(i * hb - 1, 0), 0)),
                  pl.BlockSpec((1, 8, FF_CHUNK), lambda c, i: (c, 0, 0)),
                  pl.BlockSpec((1, 1, FF_CHUNK), lambda c, i: (c, 0, 0))],
        out_specs=pl.BlockSpec((1, tm, FF_CHUNK), lambda c, i: (c, i, 0)),
        out_shape=jax.ShapeDtypeStruct((N_CHUNK, s_len, FF_CHUNK), BF16),
        compiler_params=_params("parallel", "parallel"),
    )(gu, gu, cw, cb)


def _ffn_out_ln(act, wfo, res, g, b, tm=256):
    s_len = res.shape[0]
    tm = _div(s_len, tm, 8)

    def body(a_ref, w_ref, res_ref, g_ref, b_ref, u_ref, y_ref):
        u = ALPHA * res_ref[...]
        for c in range(N_CHUNK):
            u = u + jnp.dot(a_ref[c], w_ref[c], preferred_element_type=F32)
        u_ref[...] = u
        y_ref[...] = _ln(u, g_ref[...], b_ref[...])

    row = pl.BlockSpec((tm, D_MODEL), lambda i: (i, 0))
    vec = pl.BlockSpec((1, D_MODEL), lambda i: (0, 0))
    return pl.pallas_call(
        body, name="ffn_out_ln", grid=(s_len // tm,),
        in_specs=[pl.BlockSpec((N_CHUNK, tm, FF_CHUNK), lambda i: (0, i, 0)),
                  pl.BlockSpec((N_CHUNK, FF_CHUNK, D_MODEL), lambda i: (0, 0, 0)), row, vec, vec],
        out_specs=[row, row],
        out_shape=[jax.ShapeDtypeStruct((s_len, D_MODEL), F32), jax.ShapeDtypeStruct((s_len, D_MODEL), F32)],
        compiler_params=_params("parallel"),
    )(act, wfo, res, g, b)


def _d_act(du, wfo, tm=256):
    s_len = du.shape[0]
    tm = _div(s_len, tm, 8)

    def body(du_ref, w_ref, o_ref):
        du_b = du_ref[...].astype(BF16)
        for c in range(N_CHUNK):
            o_ref[c] = lax.dot_general(du_b, w_ref[c], (((1,), (1,)), ((), ())), preferred_element_type=F32)

    return pl.pallas_call(
        body, name="d_act", grid=(s_len // tm,),
        in_specs=[pl.BlockSpec((tm, D_MODEL), lambda i: (i, 0)),
                  pl.BlockSpec((N_CHUNK, FF_CHUNK, D_MODEL), lambda i: (0, 0, 0))],
        out_specs=pl.BlockSpec((N_CHUNK, tm, FF_CHUNK), lambda i: (0, i, 0)),
        out_shape=jax.ShapeDtypeStruct((N_CHUNK, s_len, FF_CHUNK), F32),
        compiler_params=_params("parallel"),
    )(du, wfo)


def _g_w_ffn_out(act, du, tm=512):
    s_len = du.shape[0]
    tm = _div(s_len, tm, 8)
    steps = s_len // tm

    def body(a_ref, g_ref, o_ref, acc_ref):
        s = pl.program_id(1)

        @pl.when(s == 0)
        def _():
            acc_ref[...] = jnp.zeros_like(acc_ref)

        acc_ref[...] += lax.dot_general(a_ref[0], g_ref[...].astype(BF16), (((0,), (0,)), ((), ())),
                                        preferred_element_type=F32)

        @pl.when(s == steps - 1)
        def _():
            o_ref[0] = acc_ref[...]

    return pl.pallas_call(
        body, name="g_w_ffn_out", grid=(N_CHUNK, steps),
        in_specs=[pl.BlockSpec((1, tm, FF_CHUNK), lambda c, s: (c, s, 0)),
                  pl.BlockSpec((tm, D_MODEL), lambda c, s: (s, 0))],
        out_specs=pl.BlockSpec((1, FF_CHUNK, D_MODEL), lambda c, s: (c, 0, 0)),
        out_shape=jax.ShapeDtypeStruct((N_CHUNK, FF_CHUNK, D_MODEL), F32),
        scratch_shapes=[pltpu.VMEM((FF_CHUNK, D_MODEL), F32)],
        compiler_params=_params("parallel", "arbitrary"),
    )(act, du)


def _d_h1(dgu, wfi, res, tm=256):
    s_len = res.shape[0]
    tm = _div(s_len, tm, 8)

    def body(a_ref, w_ref, res_ref, o_ref):
        acc = ALPHA * res_ref[...]
        for g in range(2):
            for c in range(N_CHUNK):
                acc = acc + lax.dot_general(a_ref[c, g], w_ref[N_CHUNK * g + c], (((1,), (1,)), ((), ())),
                                            preferred_element_type=F32)
        o_ref[...] = acc

    row = pl.BlockSpec((tm, D_MODEL), lambda i: (i, 0))
    return pl.pallas_call(
        body, name="d_h1", grid=(s_len // tm,),
        in_specs=[pl.BlockSpec((N_CHUNK, 2, tm, FF_CHUNK), lambda i: (0, 0, i, 0)),
                  pl.BlockSpec((N_DEV, D_MODEL, FF_CHUNK), lambda i: (0, 0, 0)), row],
        out_specs=row, out_shape=jax.ShapeDtypeStruct((s_len, D_MODEL), F32),
        compiler_params=_params("parallel"),
    )(dgu, wfi, res)


def _g_w_ffn_in(h1, dgu, tm=512):
    s_len = h1.shape[0]
    tm = _div(s_len, tm, 8)
    steps = s_len // tm

    def body(a_ref, g_ref, o_ref, acc_ref):
        s = pl.program_id(1)

        @pl.when(s == 0)
        def _():
            acc_ref[...] = jnp.zeros_like(acc_ref)

        acc_ref[...] += lax.dot_general(a_ref[...].astype(BF16), g_ref[0, 0], (((0,), (0,)), ((), ())),
                                        preferred_element_type=F32)

        @pl.when(s == steps - 1)
        def _():
            o_ref[0] = acc_ref[...]

    return pl.pallas_call(
        body, name="g_w_ffn_in", grid=(N_DEV, steps),
        in_specs=[pl.BlockSpec((tm, D_MODEL), lambda d, s: (s, 0)),
                  pl.BlockSpec((1, 1, tm, FF_CHUNK), lambda d, s: (d % N_CHUNK, d // N_CHUNK, s, 0))],
        out_specs=pl.BlockSpec((1, D_MODEL, FF_CHUNK), lambda d, s: (d, 0, 0)),
        out_shape=jax.ShapeDtypeStruct((N_DEV, D_MODEL, FF_CHUNK), F32),
        scratch_shapes=[pltpu.VMEM((D_MODEL, FF_CHUNK), F32)],
        compiler_params=_params("parallel", "arbitrary"),
    )(h1, dgu)


def _conv_bwd(gu, dact, cw, cb, tm=256):
    s_len = gu.shape[2]
    tm = _div(s_len, tm, 8)
    nrow = s_len // tm
    hb = tm // 8

    def dconv_of(conv, up, da):
        sg = jax.nn.sigmoid(conv)
        return da * up * (sg * (1.0 + conv * (1.0 - sg)))

    def body(gu_ref, gp_ref, gun_ref, da_ref, dan_ref, cw_ref, cb_ref, dgu_ref, dcw_ref):
        i = pl.program_id(1)
        first = i == 0
        last = i == nrow - 1
        cw = cw_ref[0]
        cb = cb_ref[0]
        gate = gu_ref[0, 0]
        halo = gp_ref[0, 0]
        g_m1 = _shift_down(gate, 1, halo, first)
        g_m2 = _shift_down(gate, 2, halo, first)
        conv = _conv_act(gate, g_m1, g_m2, cw, cb)
        da = da_ref[0]
        sg = jax.nn.sigmoid(conv)
        dgu_ref[0, 1] = (da * conv * sg).astype(BF16)
        dconv = dconv_of(conv, gu_ref[0, 1], da)
        gate_n = gun_ref[0, 0]
        tail = gate[tm - 8:, :]
        conv_n = _conv_act(gate_n, _shift_down(gate_n, 1, tail, False), _shift_down(gate_n, 2, tail, False), cw, cb)
        dconv_n = dconv_of(conv_n, gun_ref[0, 1], dan_ref[0])
        dgate = (cw[2:3, :] * dconv + cw[1:2, :] * _shift_up(dconv, 1, dconv_n, last)
                 + cw[0:1, :] * _shift_up(dconv, 2, dconv_n, last))
        dgu_ref[0, 0] = dgate.astype(BF16)

        @pl.when(first)
        def _():
            dcw_ref[...] = jnp.zeros_like(dcw_ref)

        row = lax.broadcasted_iota(jnp.int32, (8, 1), 0)
        part = jnp.zeros((8, FF_CHUNK), F32)
        for r, term in enumerate((dconv * g_m2, dconv * g_m1, dconv * gate, dconv)):
            part = jnp.where(row == r, jnp.sum(term, axis=0, keepdims=True), part)
        dcw_ref[0] += part

    nxt = lambda i: jnp.minimum((i + 1) * hb, s_len // 8 - 1)
    main = pl.BlockSpec((1, 2, tm, FF_CHUNK), lambda c, i: (c, 0, i, 0))
    return pl.pallas_call(
        body, name="conv_bwd", grid=(N_CHUNK, nrow),
        in_specs=[main,
                  pl.BlockSpec((1, 1, 8, FF_CHUNK), lambda c, i: (c, 0, jnp.maximum(i * hb - 1, 0), 0)),
                  pl.BlockSpec((1, 2, 8, FF_CHUNK), lambda c, i: (c, 0, nxt(i), 0)),
                  pl.BlockSpec((1, tm, FF_CHUNK), lambda c, i: (c, i, 0)),
                  pl.BlockSpec((1, 8, FF_CHUNK), lambda c, i: (c, nxt(i), 0)),
                  pl.BlockSpec((1, 8, FF_CHUNK), lambda c, i: (c, 0, 0)),
                  pl.BlockSpec((1, 1, FF_CHUNK), lambda c, i: (c, 0, 0))],
        out_specs=[main, pl.BlockSpec((1, 8, FF_CHUNK), lambda c, i: (c, 0, 0))],
        out_shape=[jax.ShapeDtypeStruct((N_CHUNK, 2, s_len, FF_CHUNK), BF16),
                   jax.ShapeDtypeStruct((N_CHUNK, 8, FF_CHUNK), F32)],
        compiler_params=_params("parallel", "arbitrary"),
    )(gu, gu, gu, dact, dact, cw, cb)


def _loss_head(y, target, tm=256):
    m, d = y.shape
    tm = _div(m, tm, 8)

    def body(y_ref, t_ref, dy_ref, loss_ref):
        @pl.when(pl.program_id(0) == 0)
        def _():
            loss_ref[...] = jnp.zeros_like(loss_ref)

        err = y_ref[...] - t_ref[...]
        dy_ref[...] = err / d
        loss_ref[...] += 0.5 * jnp.sum(jnp.sum(err * err, axis=1, keepdims=True) / d, axis=0, keepdims=True)

    row = pl.BlockSpec((tm, d), lambda i: (i, 0))
    return pl.pallas_call(
        body, name="loss_head", grid=(m // tm,), in_specs=[row, row],
        out_specs=[row, pl.BlockSpec((8, LANE), lambda i: (0, 0))],
        out_shape=[jax.ShapeDtypeStruct((m, d), F32), jax.ShapeDtypeStruct((8, LANE), F32)],
        compiler_params=_params("arbitrary"),
    )(y, target)


def _sum_devices(r_ref):
    acc = r_ref[0].astype(F32)
    for d in range(1, N_DEV):
        acc = acc + r_ref[d].astype(F32)
    return acc


def _sum8(recv):
    rows = recv.shape[1]
    tr = _div(rows, ROW_BLOCK, 8)

    def body(r_ref, o_ref):
        o_ref[...] = _sum_devices(r_ref)

    return pl.pallas_call(
        body, name="sum8", grid=(rows // tr,),
        in_specs=[pl.BlockSpec((N_DEV, tr, LANE), lambda i: (0, i, 0))],
        out_specs=pl.BlockSpec((tr, LANE), lambda i: (i, 0)),
        out_shape=jax.ShapeDtypeStruct((rows, LANE), F32),
        compiler_params=_params("parallel"),
    )(recv)


def _adamw_math(w, g, m, v):
    m = ADAM_B1 * m + (1.0 - ADAM_B1) * g
    v = ADAM_B2 * v + (1.0 - ADAM_B2) * (g * g)
    m_hat = m / (1.0 - ADAM_B1 ** ADAM_STEP)
    v_hat = v / (1.0 - ADAM_B2 ** ADAM_STEP)
    return -ADAM_LR * (m_hat / (jnp.sqrt(v_hat) + ADAM_EPS) + ADAM_WD * w), m, v


def _adamw_rows(w, g, m, v, name):
    rows = w.shape[0]
    tr = _div(rows, ROW_BLOCK, 8)

    def body(w_ref, g_ref, m_ref, v_ref, d_ref, mo_ref, vo_ref):
        d_ref[...], mo_ref[...], vo_ref[...] = _adamw_math(w_ref[...], g_ref[...], m_ref[...], v_ref[...])

    blk = pl.BlockSpec((tr, LANE), lambda i: (i, 0))
    out = jax.ShapeDtypeStruct((rows, LANE), F32)
    return pl.pallas_call(
        body, name=name, grid=(rows // tr,), in_specs=[blk, blk, blk, blk], out_specs=[blk, blk, blk],
        out_shape=[out, out, out], compiler_params=_params("parallel"),
    )(w, g, m, v)


def _adamw_shard(recv, w, m, v, layer, prev, name):
    _, k, n = recv.shape
    tk = _div(k, 128, 16)

    def body(r_ref, w_ref, m_ref, v_ref, *rest):
        g_ref, d_ref, mo_ref, vo_ref = rest[-4:]
        g = _sum_devices(r_ref)
        g_ref[0] = g
        d_ref[0], mo_ref[0], vo_ref[0] = _adamw_math(w_ref[0], g, m_ref[0], v_ref[0])

    blk = pl.BlockSpec((1, tk, n), lambda i: (layer, i, 0))
    out = jax.ShapeDtypeStruct((DEPTH, k, n), F32)
    carried = [] if prev is None else list(prev)
    return pl.pallas_call(
        body, name=name, grid=(k // tk,),
        in_specs=[pl.BlockSpec((N_DEV, tk, n), lambda i: (0, i, 0)), blk, blk, blk]
        + [pl.BlockSpec(memory_space=pl.ANY)] * len(carried),
        out_specs=[blk, blk, blk, blk], out_shape=[out, out, out, out],
        input_output_aliases={4 + j: j for j in range(len(carried))},
        compiler_params=_params("parallel"),
    )(recv, w, m, v, *carried)


def _to_rows(flat, rows):
    flat = flat.reshape(-1)
    return jnp.pad(flat, (0, rows * LANE - flat.shape[0])).reshape(rows, LANE)


def _pad_cols_z(a):
    f0 = N_QKV
    g0 = N_QKV + FOX_HEADS
    pad = jnp.zeros(a.shape[:-1] + (F_PAD - FOX_HEADS,), a.dtype)
    return jnp.concatenate([a[..., :f0], a[..., g0:], a[..., f0:g0], pad], axis=-1)


def _unpad_cols_z(a):
    f0 = N_QKV + N_GATE
    return jnp.concatenate([a[..., :N_QKV], a[..., f0:f0 + FOX_HEADS], a[..., N_QKV:f0]], axis=-1)


def _shards_to_cols(g):
    _, k, n = g.shape
    return g.transpose(1, 0, 2).reshape(k, N_DEV * n)


def _cols_to_shards(full):
    k, n = full.shape
    return full.reshape(k, N_DEV, n // N_DEV).transpose(1, 0, 2)


def _layer_fwd(h, w, p, comm=None, late=None):
    zq = _linear(h, w["wq"], bias=p["bq"], out_dtype=BF16, name="z_qkv", tn=768)
    zg = _linear(h, w["wg"], bias=p["bg"], name="z_gate", tn=768)
    qx, kx, vx = _fox_prep(zq, _cumsum_logf(zg))
    attn_a, lse_a = _swa_fwd(zq, p["sinks"])
    attn_b, attn_b32, lse_b, arrived = _fox_fwd(qx, kx, vx, comm)
    if late is not None:
        w, p = late(w, p, arrived)
    h1, u1, merged, ya, yb = _mixer_out(attn_a, attn_b, zg, h, w["w_proj_a"], w["w_proj_b"], w["w_out"],
                                        p["ln_mix_g"], p["ln_mix_b"])
    gu = _ffn_in(h1, w["w_ffn_in"])
    act = _conv_fwd(gu, p["conv_w"], p["conv_b"])
    u2, h2 = _ffn_out_ln(act, w["w_ffn_out"], h1, p["ln_ffn_g"], p["ln_ffn_b"])
    saved = dict(h=h, zq=zq, zg=zg, qx=qx, kx=kx, vx=vx, attn_a=attn_a, lse_a=lse_a, attn_b=attn_b,
                 attn_b32=attn_b32, lse_b=lse_b, h1=h1, u1=u1, merged=merged, ya=ya, yb=yb, gu=gu, act=act, u2=u2)
    return h2, saved, w, p


def _layer_bwd(dh2, sv, w, p, make_comm=None):
    s_len = dh2.shape[0]
    du2, d_ffn_g, d_ffn_b = _ln_bwd(dh2, sv["u2"], p["ln_ffn_g"], name="ln_ffn_bwd")
    dact = _d_act(du2, w["w_ffn_out"])
    g_ffn_out = _g_w_ffn_out(sv["act"], du2)
    dgu, dcw = _conv_bwd(sv["gu"], dact, p["conv_w"], p["conv_b"])
    dcw = dcw.transpose(1, 0, 2).reshape(8, D_FF)
    dh1 = _d_h1(dgu, w["w_ffn_in"], du2)
    g_ffn_in = _g_w_ffn_in(sv["h1"], dgu)
    du1, d_mix_g, d_mix_b = _ln_bwd(dh1, sv["u1"], p["ln_mix_g"], name="ln_mix_bwd")
    dmerged = _linear(du1, w["w_out"], trans_b=True, name="d_merged", tn=1024)
    g_out = _linear_tn(sv["merged"], du1, name="g_w_out", tn=1024)
    dya, dyb, dga, dgb = _gate_bwd(dmerged, sv["ya"], sv["yb"], sv["zg"])
    dattn_a = _linear(dya, w["w_proj_a"], trans_b=True, out_dtype=BF16, name="d_attn_a", tn=512)
    dattn_b = _linear(dyb, w["w_proj_b"], trans_b=True, out_dtype=BF16, name="d_attn_b", tn=512)
    g_proj_a = _linear_tn(sv["attn_a"], dya, name="g_w_proj_a", tk=512, tn=1024)
    g_proj_b = _linear_tn(sv["attn_b"], dyb, name="g_w_proj_b", tk=512, tn=1024)
    dq_a, dk_a, dv_a, dsinks = _swa_bwd(sv["zq"], p["sinks"], sv["attn_a"], dattn_a, sv["lse_a"])
    big = dict(w_proj_a=_cols_to_shards(g_proj_a), w_proj_b=_cols_to_shards(g_proj_b),
               w_out=g_out.reshape(N_DEV, D_MODEL // N_DEV, D_MODEL), w_ffn_in=g_ffn_in,
               w_ffn_out=g_ffn_out.reshape(N_DEV, D_FF // N_DEV, D_MODEL))
    dox, stats = _fox_stats(sv["attn_b32"], dattn_b, sv["lse_b"])
    dq_b, dk_b, dv_b, dcc, arrived = _fox_bwd(sv["qx"], sv["kx"], sv["vx"], dox, stats,
                                              None if make_comm is None else make_comm(big))
    dc = jnp.pad(dcc.reshape(s_len, N_PAIR, LANE)[:, :, :2].reshape(s_len, FOX_HEADS), ((0, 0), (0, LANE - FOX_HEADS)))
    df = _forget_bwd(dc, sv["zg"])
    dz = jnp.concatenate([dq_a, dk_a.astype(BF16), dv_a.astype(BF16), dq_b.astype(BF16), dk_b, dv_b, dga, dgb, df,
                          jnp.zeros((s_len, F_PAD - LANE), BF16)], axis=1)
    dh = _linear(dz, w["w_in_p"], trans_b=True, res=du1, res_scale=ALPHA, name="d_h", tn=512)
    g_in = _unpad_cols_z(_linear_tn(sv["h"], dz, name="g_w_in", tn=768))
    g_b_in = _unpad_cols_z(_colsum(dz, name="g_b_in"))
    big["w_in"] = _cols_to_shards(g_in)
    small = dict(ln_mix_g=d_mix_g, ln_mix_b=d_mix_b, b_in=g_b_in, attn_sinks=dsinks[:, :SWA_HEADS],
                 ln_ffn_g=d_ffn_g, ln_ffn_b=d_ffn_b, conv_w=dcw[:3], conv_b=dcw[3:4])
    return dh, big, small, arrived


def _w_in_layouts(w_in):
    w_in_p = _pad_cols_z(_shards_to_cols(w_in))
    return dict(w_in_p=w_in_p, wq=w_in_p[:, :N_QKV], wg=w_in_p[:, N_QKV:])


def _other_layouts(w_proj_a, w_proj_b, w_out, w_ffn_in, w_ffn_out):
    return dict(w_proj_a=_shards_to_cols(w_proj_a), w_proj_b=_shards_to_cols(w_proj_b),
                w_out=w_out.reshape(D_MODEL, D_MODEL), w_ffn_in=w_ffn_in,
                w_ffn_out=w_ffn_out.reshape(N_CHUNK, FF_CHUNK, D_MODEL))


def _layer_params(r):
    b_p = _pad_cols_z(r["b_in"].reshape(1, N_IN))
    return dict(
        bq=b_p[:, :N_QKV], bg=b_p[:, N_QKV:],
        sinks=jnp.pad(r["attn_sinks"].reshape(1, SWA_HEADS), ((0, 0), (0, LANE - SWA_HEADS))),
        ln_mix_g=r["ln_mix_g"].reshape(1, D_MODEL), ln_mix_b=r["ln_mix_b"].reshape(1, D_MODEL),
        ln_ffn_g=r["ln_ffn_g"].reshape(1, D_MODEL), ln_ffn_b=r["ln_ffn_b"].reshape(1, D_MODEL),
        conv_b=r["conv_b"].reshape(N_CHUNK, 1, FF_CHUNK))


def _conv_w_layout(conv_w):
    return jnp.pad(conv_w, ((0, 5), (0, 0))).reshape(8, N_CHUNK, FF_CHUNK).transpose(1, 0, 2)


def kernel(x, ln_mix_g, ln_mix_b, w_in, b_in, attn_sinks, w_proj_a, w_proj_b, w_out, ln_ffn_g, ln_ffn_b, w_ffn_in, conv_w, conv_b, w_ffn_out, loss_target, m_ln_mix_g, m_ln_mix_b, m_w_in, m_b_in, m_attn_sinks, m_w_proj_a, m_w_proj_b, m_w_out, m_ln_ffn_g, m_ln_ffn_b, m_w_ffn_in, m_conv_w, m_conv_b, m_w_ffn_out, v_ln_mix_g, v_ln_mix_b, v_w_in, v_b_in, v_attn_sinks, v_w_proj_a, v_w_proj_b, v_w_out, v_ln_ffn_g, v_ln_ffn_b, v_w_ffn_in, v_conv_w, v_conv_b, v_w_ffn_out):
    wts = dict(ln_mix_g=ln_mix_g, ln_mix_b=ln_mix_b, w_in=w_in, b_in=b_in, attn_sinks=attn_sinks, w_proj_a=w_proj_a,
               w_proj_b=w_proj_b, w_out=w_out, ln_ffn_g=ln_ffn_g, ln_ffn_b=ln_ffn_b, w_ffn_in=w_ffn_in,
               conv_w=conv_w, conv_b=conv_b, w_ffn_out=w_ffn_out)
    mom = dict(ln_mix_g=m_ln_mix_g, ln_mix_b=m_ln_mix_b, w_in=m_w_in, b_in=m_b_in, attn_sinks=m_attn_sinks,
               w_proj_a=m_w_proj_a, w_proj_b=m_w_proj_b, w_out=m_w_out, ln_ffn_g=m_ln_ffn_g, ln_ffn_b=m_ln_ffn_b,
               w_ffn_in=m_w_ffn_in, conv_w=m_conv_w, conv_b=m_conv_b, w_ffn_out=m_w_ffn_out)
    vel = dict(ln_mix_g=v_ln_mix_g, ln_mix_b=v_ln_mix_b, w_in=v_w_in, b_in=v_b_in, attn_sinks=v_attn_sinks,
               w_proj_a=v_w_proj_a, w_proj_b=v_w_proj_b, w_out=v_w_out, ln_ffn_g=v_ln_ffn_g, ln_ffn_b=v_ln_ffn_b,
               w_ffn_in=v_w_ffn_in, conv_w=v_conv_w, conv_b=v_conv_b, w_ffn_out=v_w_ffn_out)
    names = list(wts)
    big_names = [n for n, _, _ in BIG]
    small_names = [n for n, _ in SMALL]
    me = 4 * lax.axis_index("x") + 2 * lax.axis_index("y") + lax.axis_index("c")
    cw_shard = D_FF // N_DEV

    wb = {n: wts[n].astype(BF16) for n in big_names}
    ps = [_layer_params(dict(b_in=b_in[l], attn_sinks=attn_sinks[l], ln_mix_g=ln_mix_g[l], ln_mix_b=ln_mix_b[l],
                             ln_ffn_g=ln_ffn_g[l], ln_ffn_b=ln_ffn_b[l], conv_b=conv_b[l])) for l in range(DEPTH)]
    w_in_0, = _exchange([(wb["w_in"][0], True)], "gather_w_in_0")
    others = big_names[1:]
    gather_rest = _Comm([(wb[n][0], True) for n in others] + [(wb[n][1], True) for n in big_names] + [(conv_w, True)])
    next_layer = {}

    def late(w, p, arrived):
        conv_full = arrived[-1].transpose(1, 2, 0, 3).reshape(DEPTH, 3, D_FF)
        layer_1 = arrived[len(others):-1]
        next_layer["w"] = dict(_w_in_layouts(layer_1[0]), **_other_layouts(*layer_1[1:]))
        next_layer["p"] = dict(ps[1], conv_w=_conv_w_layout(conv_full[1]))
        return dict(w, **_other_layouts(*arrived[:len(others)])), dict(p, conv_w=_conv_w_layout(conv_full[0]))

    saved, ws = [None] * DEPTH, [None] * DEPTH
    h, saved[0], ws[0], ps[0] = _layer_fwd(x[0], _w_in_layouts(w_in_0), ps[0], gather_rest, late)
    h, saved[1], ws[1], ps[1] = _layer_fwd(h, next_layer["w"], next_layer["p"])
    dh, loss_part = _loss_head(h, loss_target[0])

    def small_rows(small):
        vec = jnp.concatenate([small[n].reshape(-1) for n in small_names] + [loss_part[0, 0].reshape(1)])
        return _to_rows(vec, SMALL_LAYER_ROWS)

    dh, big_1, small_1, _ = _layer_bwd(dh, saved[1], ws[1], ps[1])

    def exchange_early(big_0):
        return _Comm([(big_1[n].astype(BF16), False) for n in big_names] + [(small_rows(small_1), True)]
                     + [(big_0[n].astype(BF16), False) for n in others])

    grad_x, big_0, small_0, arrived = _layer_bwd(dh, saved[0], ws[0], ps[0], exchange_early)
    g_in_0, g_small_0 = _exchange([(big_0["w_in"].astype(BF16), False), (small_rows(small_0), True)],
                                  "exchange_grads_0")
    n_big = len(big_names)
    recv = [[g_in_0] + list(arrived[n_big + 1:]) + [g_small_0], list(arrived[:n_big + 1])]

    big_out = {}
    for t, n in enumerate(big_names):
        outs = None
        for l in reversed(range(DEPTH)):
            outs = _adamw_shard(recv[l][t], wts[n], mom[n], vel[n], l, outs, "adamw_%s_%d" % (n, l))
        big_out[n] = outs
    small_sum = [_sum8(recv[l][-1]).reshape(-1) for l in range(DEPTH)]
    g_small = {}
    off = 0
    for n, size in SMALL:
        g_small[n] = jnp.stack([small_sum[l][off:off + size] for l in range(DEPTH)])
        off += size
    loss = small_sum[0][off]
    g_small["conv_w"] = lax.dynamic_slice_in_dim(g_small["conv_w"].reshape(DEPTH, 3, D_FF), me * cw_shard, cw_shard,
                                                 axis=2)
    g_small = {n: g_small[n].reshape(wts[n].shape) for n in small_names}

    def pack_small(tree):
        return _to_rows(jnp.concatenate([tree[n].reshape(-1) for n in small_names]), SMALL_ROWS)

    small_out = (pack_small(g_small),) + tuple(_adamw_rows(pack_small(wts), pack_small(g_small), pack_small(mom),
                                                           pack_small(vel), "adamw_small"))

    def result(j):
        out = {n: big_out[n][j] for n in big_names}
        flat = small_out[j].reshape(-1)
        off = 0
        for n in small_names:
            out[n] = flat[off:off + wts[n].size].reshape(wts[n].shape)
            off += wts[n].size
        return [out[n] for n in names]

    return (loss, grad_x[None], *result(0), *result(1), *result(2), *result(3))
```

```python
import functools

import jax
import jax.numpy as jnp
from jax import lax
from jax.experimental import pallas as pl
from jax.experimental.pallas import tpu as pltpu

F32 = jnp.float32
BF16 = jnp.bfloat16
MESH = pl.DeviceIdType.MESH

N_DEV = 8
DEPTH = 2
D_MODEL = 1024
HEAD_DIM = 64
SWA_Q = 512
SWA_KV = 128
FOX_W = 512
FOX_HEADS = 8
SWA_HEADS = 8
D_FF = 2816
N_IN = 4360
N_QKV = SWA_Q + 2 * SWA_KV + 3 * FOX_W
N_GATE = 2 * D_MODEL
F_PAD = 256
N_ZG = N_GATE + F_PAD
N_ZP = N_QKV + N_ZG
LN_EPS = 1e-5
NEG_INF = -1e30
ALPHA = (2 * DEPTH) ** 0.25
SCALE = HEAD_DIM ** -0.5
SLOPES = tuple(2.0 ** (-8.0 * (h + 1) / SWA_HEADS) for h in range(SWA_HEADS))

ADAM_LR = 0.001
ADAM_B1 = 0.9
ADAM_B2 = 0.999
ADAM_EPS = 1e-08
ADAM_WD = 0.01
ADAM_STEP = 10

LANE = 128
VMEM_LIMIT = 56 * 1024 * 1024

BIG = (("w_in", (D_MODEL, N_IN), 1), ("w_proj_a", (SWA_Q, D_MODEL), 1), ("w_proj_b", (FOX_W, D_MODEL), 1),
       ("w_out", (D_MODEL, D_MODEL), 0), ("w_ffn_in", (D_MODEL, 2 * D_FF), 1), ("w_ffn_out", (D_FF, D_MODEL), 0))
SMALL = (("ln_mix_g", D_MODEL), ("ln_mix_b", D_MODEL), ("b_in", N_IN), ("attn_sinks", SWA_HEADS),
         ("ln_ffn_g", D_MODEL), ("ln_ffn_b", D_MODEL), ("conv_w", 3 * D_FF), ("conv_b", D_FF))
ROW_BLOCK = 512
SMALL_LAYER_ROWS = -(-(sum(n for _, n in SMALL) + 1) // (8 * LANE)) * 8
SMALL_ROWS = ROW_BLOCK
FF_CHUNK = 2 * D_FF // N_DEV
N_CHUNK = D_FF // FF_CHUNK


def _div(n, cap, unit):
    if n <= cap:
        return n
    best = None
    for t in range(unit, cap + 1, unit):
        if n % t == 0:
            best = t
    assert best is not None, (n, cap, unit)
    return best


def _params(*sem):
    return pltpu.CompilerParams(dimension_semantics=sem, vmem_limit_bytes=VMEM_LIMIT)


def _peer(r):
    x, y, c = lax.axis_index("x"), lax.axis_index("y"), lax.axis_index("c")
    px = 1 - x if (r >> 2) & 1 else x
    py = 1 - y if (r >> 1) & 1 else y
    pc = 1 - c if r & 1 else c
    return (px, py, pc), 4 * px + 2 * py + pc


class _Comm:
    def __init__(self, tensors):
        self.arrays = [x for x, _ in tensors]
        self.gathers = [g for _, g in tensors]
        self.n = len(tensors)
        self.out_shape = [jax.ShapeDtypeStruct((N_DEV,) + (x.shape if g else x.shape[1:]), x.dtype)
                          for x, g in tensors]
        self.specs = [pl.BlockSpec(memory_space=pl.ANY)] * self.n
        self.scratch = [pltpu.SemaphoreType.DMA((N_DEV - 1, self.n)), pltpu.SemaphoreType.DMA((N_DEV - 1, self.n)),
                        pltpu.SemaphoreType.DMA((self.n,))]

    def _copies(self, x_refs, out_refs, sems):
        send_sems, recv_sems, local_sems = sems
        _, me = _peer(0)

        def src(t, idx):
            return x_refs[t] if self.gathers[t] else x_refs[t].at[idx]

        def remote(r, t, mine):
            peer, pid = _peer(r)
            return pltpu.make_async_remote_copy(src_ref=src(t, pid), dst_ref=out_refs[t].at[me if mine else pid],
                                                send_sem=send_sems.at[r - 1, t], recv_sem=recv_sems.at[r - 1, t],
                                                device_id=peer, device_id_type=MESH)

        pairs = [(r, t) for r in range(1, N_DEV) for t in range(self.n)]
        local = [pltpu.make_async_copy(src(t, me), out_refs[t].at[me], local_sems.at[t]) for t in range(self.n)]
        return local, [remote(r, t, True) for r, t in pairs], lambda: [remote(r, t, False) for r, t in pairs]

    def start(self, x_refs, out_refs, sems):
        local, sent, _ = self._copies(x_refs, out_refs, sems)
        for cp in local + sent:
            cp.start()

    def wait(self, x_refs, out_refs, sems):
        local, sent, landing = self._copies(x_refs, out_refs, sems)
        for cp in landing():
            cp.wait_recv()
        for cp in sent:
            cp.wait_send()
        for cp in local:
            cp.wait()


def _exchange(tensors, name):
    comm = _Comm(tensors)
    n = comm.n

    def body(*refs):
        comm.start(refs[:n], refs[n:2 * n], refs[2 * n:])
        comm.wait(refs[:n], refs[n:2 * n], refs[2 * n:])

    return pl.pallas_call(body, name=name, out_shape=comm.out_shape, in_specs=comm.specs, out_specs=comm.specs,
                          scratch_shapes=comm.scratch)(*comm.arrays)


def _with_comm(comm, n_in, n_out, first, last, compute):
    nc = comm.n if comm is not None else 0

    def body(*refs):
        ins, x_refs = refs[:n_in], refs[n_in:n_in + nc]
        outs = refs[n_in + nc:n_in + nc + n_out]
        out_refs = refs[n_in + nc + n_out:n_in + 2 * nc + n_out]
        sems = refs[n_in + 2 * nc + n_out:]
        if nc:
            @pl.when(first())
            def _():
                comm.start(x_refs, out_refs, sems)

        compute(*ins, *outs)
        if nc:
            @pl.when(last())
            def _():
                comm.wait(x_refs, out_refs, sems)

    return body


def _linear(a, b, *, name, trans_b=False, bias=None, res=None, res_scale=1.0, out_dtype=F32, tm=512, tn=640):
    m, k = a.shape
    n = b.shape[0] if trans_b else b.shape[1]
    tm = _div(m, tm, 8)
    tn = _div(n, tn, LANE)
    dn = (((1,), (1,)), ((), ())) if trans_b else (((1,), (0,)), ((), ()))

    def body(*refs):
        a_ref, b_ref = refs[0], refs[1]
        rest = list(refs[2:])
        bias_ref = rest.pop(0) if bias is not None else None
        res_ref = rest.pop(0) if res is not None else None
        o_ref = rest.pop(0)
        acc = lax.dot_general(a_ref[...].astype(BF16), b_ref[...].astype(BF16), dn, preferred_element_type=F32)
        if bias_ref is not None:
            acc = acc + bias_ref[...]
        if res_ref is not None:
            acc = acc + res_scale * res_ref[...].astype(F32)
        o_ref[...] = acc.astype(out_dtype)

    in_specs = [pl.BlockSpec((tm, k), lambda i, j: (i, 0)),
                pl.BlockSpec((tn, k), lambda i, j: (j, 0)) if trans_b else pl.BlockSpec((k, tn), lambda i, j: (0, j))]
    args = [a, b]
    if bias is not None:
        in_specs.append(pl.BlockSpec((1, tn), lambda i, j: (0, j)))
        args.append(bias)
    if res is not None:
        in_specs.append(pl.BlockSpec((tm, tn), lambda i, j: (i, j)))
        args.append(res)
    return pl.pallas_call(
        body, name=name, grid=(m // tm, n // tn), in_specs=in_specs,
        out_specs=pl.BlockSpec((tm, tn), lambda i, j: (i, j)),
        out_shape=jax.ShapeDtypeStruct((m, n), out_dtype),
        compiler_params=_params("parallel", "arbitrary"),
    )(*args)


def _z_proj(h, w_in_p, b_p, tm=512):
    m, k = h.shape
    tm = _div(m, tm, 8)

    def body(h_ref, w_ref, b_ref, zq_ref, zg_ref):
        a = h_ref[...].astype(BF16)
        zq_ref[...] = (jnp.dot(a, w_ref[:, :N_QKV], preferred_element_type=F32) + b_ref[:, :N_QKV]).astype(BF16)
        zg_ref[...] = jnp.dot(a, w_ref[:, N_QKV:], preferred_element_type=F32) + b_ref[:, N_QKV:]

    return pl.pallas_call(
        body, name="z_proj", grid=(m // tm,),
        in_specs=[pl.BlockSpec((tm, k), lambda i: (i, 0)), pl.BlockSpec((k, N_ZP), lambda i: (0, 0)),
                  pl.BlockSpec((1, N_ZP), lambda i: (0, 0))],
        out_specs=[pl.BlockSpec((tm, N_QKV), lambda i: (i, 0)), pl.BlockSpec((tm, N_ZG), lambda i: (i, 0))],
        out_shape=[jax.ShapeDtypeStruct((m, N_QKV), BF16), jax.ShapeDtypeStruct((m, N_ZG), F32)],
        compiler_params=_params("parallel"),
    )(h, w_in_p, b_p)


def _linear_tn(a, g, *, name, tk=1024, tn=640, tm=2048):
    m, k = a.shape
    n = g.shape[1]
    tk = _div(k, tk, LANE)
    tn = _div(n, tn, LANE)
    tm = _div(m, tm, 8)
    steps = m // tm

    def body(a_ref, g_ref, o_ref, acc_ref):
        s = pl.program_id(2)

        @pl.when(s == 0)
        def _():
            acc_ref[...] = jnp.zeros_like(acc_ref)

        acc_ref[...] += lax.dot_general(a_ref[...].astype(BF16), g_ref[...].astype(BF16), (((0,), (0,)), ((), ())),
                                        preferred_element_type=F32)

        @pl.when(s == steps - 1)
        def _():
            o_ref[...] = acc_ref[...]

    return pl.pallas_call(
        body, name=name, grid=(k // tk, n // tn, steps),
        in_specs=[pl.BlockSpec((tm, tk), lambda i, j, s: (s, i)), pl.BlockSpec((tm, tn), lambda i, j, s: (s, j))],
        out_specs=pl.BlockSpec((tk, tn), lambda i, j, s: (i, j)),
        out_shape=jax.ShapeDtypeStruct((k, n), F32),
        scratch_shapes=[pltpu.VMEM((tk, tn), F32)],
        compiler_params=_params("parallel", "parallel", "arbitrary"),
    )(a, g)


def _colsum(g, *, name, tm=512):
    m, n = g.shape
    tm = _div(m, tm, 8)

    def body(g_ref, o_ref):
        @pl.when(pl.program_id(0) == 0)
        def _():
            o_ref[...] = jnp.zeros_like(o_ref)

        o_ref[...] += jnp.sum(g_ref[...].astype(F32), axis=0, keepdims=True)

    return pl.pallas_call(
        body, name=name, grid=(m // tm,),
        in_specs=[pl.BlockSpec((tm, n), lambda i: (i, 0))],
        out_specs=pl.BlockSpec((1, n), lambda i: (0, 0)),
        out_shape=jax.ShapeDtypeStruct((1, n), F32),
        compiler_params=_params("arbitrary"),
    )(g)


def _ln(u, g, b):
    mu = jnp.mean(u, axis=-1, keepdims=True)
    d = u - mu
    var = jnp.mean(d * d, axis=-1, keepdims=True)
    return d * lax.rsqrt(var + LN_EPS) * g + b


def _ln_bwd_block(dy, u, g):
    mu = jnp.mean(u, axis=-1, keepdims=True)
    dd = u - mu
    rstd = lax.rsqrt(jnp.mean(dd * dd, axis=-1, keepdims=True) + LN_EPS)
    xhat = dd * rstd
    dxh = dy * g
    m1 = jnp.mean(dxh, axis=-1, keepdims=True)
    m2 = jnp.mean(dxh * xhat, axis=-1, keepdims=True)
    return (rstd * (dxh - m1 - xhat * m2), jnp.sum(dy * xhat, axis=0, keepdims=True),
            jnp.sum(dy, axis=0, keepdims=True))


def _tri(n, upper):
    r = lax.broadcasted_iota(jnp.int32, (n, n), 0)
    c = lax.broadcasted_iota(jnp.int32, (n, n), 1)
    return jnp.where((c >= r) if upper else (c <= r), 1.0, 0.0).astype(F32)


def _cumsum_logf(zg):
    s = zg.shape[0]
    nb = s // LANE
    fcol = N_GATE // LANE

    def body(f_ref, c_ref, carry_ref):
        @pl.when(pl.program_id(0) == 0)
        def _():
            carry_ref[...] = jnp.zeros_like(carry_ref)

        f = f_ref[...]
        logf = jnp.minimum(f, 0.0) - jnp.log(1.0 + jnp.exp(-jnp.abs(f)))
        c = jnp.dot(_tri(LANE, False), logf, precision=lax.Precision.HIGHEST, preferred_element_type=F32)
        c = c + carry_ref[0:1, :]
        c_ref[...] = c
        carry_ref[...] = jnp.broadcast_to(c[LANE - 1:LANE, :], carry_ref.shape)

    return pl.pallas_call(
        body, name="cumsum_logf", grid=(nb,),
        in_specs=[pl.BlockSpec((LANE, LANE), lambda i: (i, fcol))],
        out_specs=pl.BlockSpec((LANE, LANE), lambda i: (i, 0)),
        out_shape=jax.ShapeDtypeStruct((s, LANE), F32),
        scratch_shapes=[pltpu.VMEM((8, LANE), F32)],
        compiler_params=_params("arbitrary"),
    )(zg)


def _forget_bwd(dc, zg):
    s = zg.shape[0]
    nb = s // LANE
    fcol = N_GATE // LANE

    def body(dc_ref, f_ref, o_ref, carry_ref):
        @pl.when(pl.program_id(0) == 0)
        def _():
            carry_ref[...] = jnp.zeros_like(carry_ref)

        dc = dc_ref[...]
        dlogf = jnp.dot(_tri(LANE, True), dc, precision=lax.Precision.HIGHEST, preferred_element_type=F32)
        dlogf = dlogf + carry_ref[0:1, :]
        o_ref[...] = (dlogf * jax.nn.sigmoid(-f_ref[...])).astype(BF16)
        carry_ref[...] = jnp.broadcast_to(dlogf[0:1, :], carry_ref.shape)

    return pl.pallas_call(
        body, name="forget_bwd", grid=(nb,),
        in_specs=[pl.BlockSpec((LANE, LANE), lambda i: (nb - 1 - i, 0)),
                  pl.BlockSpec((LANE, LANE), lambda i: (nb - 1 - i, fcol))],
        out_specs=pl.BlockSpec((LANE, LANE), lambda i: (nb - 1 - i, 0)),
        out_shape=jax.ShapeDtypeStruct((s, LANE), BF16),
        scratch_shapes=[pltpu.VMEM((8, LANE), F32)],
        compiler_params=_params("arbitrary"),
    )(dc, zg)


KA_COL = SWA_Q // LANE
VA_COL = KA_COL + 1


def _half_masks():
    lane = lax.broadcasted_iota(jnp.int32, (1, LANE), 1)
    hi = lane >= HEAD_DIM
    return (jnp.logical_not(hi), hi)


def _both_halves(x, sel):
    xs = jnp.where(sel, x, 0.0)
    return xs + pltpu.roll(xs, HEAD_DIM, 1)


def _swa_geometry(i):
    r = lax.broadcasted_iota(jnp.int32, (LANE, 2 * LANE), 0)
    c = lax.broadcasted_iota(jnp.int32, (LANE, 2 * LANE), 1)
    dist = r + LANE - c
    valid = (dist >= 0) & (dist < LANE) & ((c >= LANE) | (i > 0))
    return valid, -dist.astype(F32)


def _swa_specs(nb):
    prev = lambda i: jnp.maximum(i - 1, 0)
    return [pl.BlockSpec((LANE, SWA_Q), lambda i: (i, 0)),
            pl.BlockSpec((LANE, LANE), lambda i: (i, KA_COL)), pl.BlockSpec((LANE, LANE), lambda i: (i, VA_COL)),
            pl.BlockSpec((LANE, LANE), lambda i: (prev(i), KA_COL)),
            pl.BlockSpec((LANE, LANE), lambda i: (prev(i), VA_COL))]


def _swa_fwd(zq, sinks):
    s_len = zq.shape[0]
    nb = s_len // LANE

    def body(q_ref, kc_ref, vc_ref, kp_ref, vp_ref, sink_ref, o_ref, lse_ref):
        i = pl.program_id(0)
        halves = _half_masks()
        lane = lax.broadcasted_iota(jnp.int32, (1, LANE), 1)
        valid, negdist = _swa_geometry(i)
        kcat = jnp.concatenate([kp_ref[...], kc_ref[...]], axis=0).astype(F32)
        vcat = jnp.concatenate([vp_ref[...], vc_ref[...]], axis=0).astype(F32)
        lse_acc = jnp.zeros((LANE, LANE), F32)
        for hk in range(2):
            kb = _both_halves(kcat, halves[hk]).astype(BF16)
            vb = _both_halves(vcat, halves[hk])
            v_e = [jnp.where(halves[e], vb, 0.0).astype(BF16) for e in range(2)]
            for pp in range(2):
                p = 2 * hk + pp
                qp = q_ref[:, p * LANE:(p + 1) * LANE]
                acc = jnp.zeros((LANE, LANE), F32)
                for e in range(2):
                    hq = 2 * p + e
                    qh = jnp.where(halves[e], qp, jnp.zeros_like(qp))
                    s = lax.dot_general(qh, kb, (((1,), (1,)), ((), ())), preferred_element_type=F32) * SCALE
                    s = jnp.where(valid, s + SLOPES[hq] * negdist, NEG_INF)
                    sink = sink_ref[0:1, hq:hq + 1]
                    m = jnp.maximum(jnp.max(s, axis=1, keepdims=True), sink)
                    pe = jnp.exp(s - m)
                    den = jnp.sum(pe, axis=1, keepdims=True) + jnp.exp(sink - m)
                    acc = acc + jnp.dot((pe / den).astype(BF16), v_e[e], preferred_element_type=F32)
                    lse_acc = jnp.where(lane == hq, m + jnp.log(den), lse_acc)
                o_ref[:, p * LANE:(p + 1) * LANE] = acc.astype(BF16)
        lse_ref[...] = lse_acc

    return pl.pallas_call(
        body, name="swa_fwd", grid=(nb,),
        in_specs=_swa_specs(nb) + [pl.BlockSpec((1, LANE), lambda i: (0, 0))],
        out_specs=[pl.BlockSpec((LANE, SWA_Q), lambda i: (i, 0)), pl.BlockSpec((LANE, LANE), lambda i: (i, 0))],
        out_shape=[jax.ShapeDtypeStruct((s_len, SWA_Q), BF16), jax.ShapeDtypeStruct((s_len, LANE), F32)],
        compiler_params=_params("parallel"),
    )(zq, zq, zq, zq, zq, sinks)


def _swa_bwd(zq, sinks, o, do, lse):
    s_len = zq.shape[0]
    nb = s_len // LANE

    def body(q_ref, kc_ref, vc_ref, kp_ref, vp_ref, sink_ref, o_ref, do_ref, lse_ref, dq_ref, dk_ref, dv_ref, ds_ref):
        i = pl.program_id(0)
        halves = _half_masks()
        lane = lax.broadcasted_iota(jnp.int32, (1, LANE), 1)
        valid, negdist = _swa_geometry(i)
        kcat = jnp.concatenate([kp_ref[...], kc_ref[...]], axis=0).astype(F32)
        vcat = jnp.concatenate([vp_ref[...], vc_ref[...]], axis=0).astype(F32)
        lse_all = lse_ref[...]
        dk_tot = jnp.zeros((2 * LANE, LANE), F32)
        dv_tot = jnp.zeros((2 * LANE, LANE), F32)
        dsink = jnp.zeros((1, LANE), F32)
        for hk in range(2):
            kb = _both_halves(kcat, halves[hk])
            vb = _both_halves(vcat, halves[hk])
            k_e = [jnp.where(halves[e], kb, 0.0).astype(BF16) for e in range(2)]
            v_e = [jnp.where(halves[e], vb, 0.0).astype(BF16) for e in range(2)]
            kb = kb.astype(BF16)
            dk_acc = jnp.zeros((2 * LANE, LANE), F32)
            dv_acc = jnp.zeros((2 * LANE, LANE), F32)
            for pp in range(2):
                p = 2 * hk + pp
                cols = slice(p * LANE, (p + 1) * LANE)
                qp = q_ref[:, cols]
                dop = do_ref[:, cols]
                prod = dop.astype(F32) * o_ref[:, cols].astype(F32)
                dq_acc = jnp.zeros((LANE, LANE), F32)
                for e in range(2):
                    hq = 2 * p + e
                    qh = jnp.where(halves[e], qp, jnp.zeros_like(qp))
                    doh = jnp.where(halves[e], dop, jnp.zeros_like(dop))
                    delta = jnp.sum(jnp.where(halves[e], prod, 0.0), axis=1, keepdims=True)
                    lse_h = lse_all[:, hq:hq + 1]
                    s = lax.dot_general(qh, kb, (((1,), (1,)), ((), ())), preferred_element_type=F32) * SCALE
                    s = jnp.where(valid, s + SLOPES[hq] * negdist, NEG_INF)
                    pr = jnp.exp(s - lse_h)
                    dp = lax.dot_general(doh, v_e[e], (((1,), (1,)), ((), ())), preferred_element_type=F32)
                    ds = pr * (dp - delta)
                    sink = sink_ref[0:1, hq:hq + 1]
                    dsink_h = -jnp.sum(jnp.exp(sink - lse_h) * delta, axis=0, keepdims=True)
                    dsink = dsink + jnp.where(lane == hq, dsink_h, 0.0)
                    dsb = (ds * SCALE).astype(BF16)
                    dq_acc = dq_acc + jnp.dot(dsb, k_e[e], preferred_element_type=F32)
                    dk_acc = dk_acc + lax.dot_general(dsb, qh, (((0,), (0,)), ((), ())), preferred_element_type=F32)
                    dv_acc = dv_acc + lax.dot_general(pr.astype(BF16), doh, (((0,), (0,)), ((), ())),
                                                      preferred_element_type=F32)
                dq_ref[:, cols] = dq_acc.astype(BF16)
            dk_tot = dk_tot + jnp.where(halves[hk], dk_acc + pltpu.roll(dk_acc, HEAD_DIM, 1), 0.0)
            dv_tot = dv_tot + jnp.where(halves[hk], dv_acc + pltpu.roll(dv_acc, HEAD_DIM, 1), 0.0)

        @pl.when(i == 0)
        def _():
            ds_ref[...] = jnp.zeros_like(ds_ref)

        ds_ref[...] += dsink
        cur = pl.ds(pl.multiple_of(i * LANE, LANE), LANE)
        dk_ref[cur, :] = dk_tot[LANE:, :]
        dv_ref[cur, :] = dv_tot[LANE:, :]

        @pl.when(i > 0)
        def _():
            prv = pl.ds(pl.multiple_of((i - 1) * LANE, LANE), LANE)
            dk_ref[prv, :] += dk_tot[:LANE, :]
            dv_ref[prv, :] += dv_tot[:LANE, :]

    blk512 = pl.BlockSpec((LANE, SWA_Q), lambda i: (i, 0))
    full = pl.BlockSpec((s_len, LANE), lambda i: (0, 0))
    vec = pl.BlockSpec((1, LANE), lambda i: (0, 0))
    return pl.pallas_call(
        body, name="swa_bwd", grid=(nb,),
        in_specs=_swa_specs(nb) + [vec, blk512, blk512, pl.BlockSpec((LANE, LANE), lambda i: (i, 0))],
        out_specs=[blk512, full, full, vec],
        out_shape=[jax.ShapeDtypeStruct((s_len, SWA_Q), BF16), jax.ShapeDtypeStruct((s_len, LANE), F32),
                   jax.ShapeDtypeStruct((s_len, LANE), F32), jax.ShapeDtypeStruct((1, LANE), F32)],
        compiler_params=_params("arbitrary"),
    )(zq, zq, zq, zq, zq, sinks, o, do, lse)


QB_COL = (SWA_Q + 2 * SWA_KV) // LANE
KB_COL = QB_COL + FOX_W // LANE
VB_COL = KB_COL + FOX_W // LANE
N_PAIR = FOX_HEADS // 2


def _causal(t, keys_first=False):
    r = lax.broadcasted_iota(jnp.int32, (t, t), 0)
    c = lax.broadcasted_iota(jnp.int32, (t, t), 1)
    return c >= r if keys_first else r >= c


N_SPLIT = 3


def _own_half(e):
    hi = lax.broadcasted_iota(jnp.int32, (1, LANE), 1) >= HEAD_DIM
    return hi if e else jnp.logical_not(hi)


def _feature_lane(e, t):
    return HEAD_DIM * (1 - e) + t


def _fox_prep(zq, c, tm=256):
    s_len = zq.shape[0]
    tm = _div(s_len, tm, 8)

    def body(z_ref, c_ref, qx_ref, kx_ref, vx_ref):
        lane = lax.broadcasted_iota(jnp.int32, (1, LANE), 1)
        for h in range(FOX_HEADS):
            p, e = divmod(h, 2)
            own = _own_half(e)
            tile = lambda col: z_ref[:, (col + p) * LANE:(col + p + 1) * LANE].astype(F32)
            rest = c_ref[:, h:h + 1]
            qf = jnp.zeros((tm, LANE), F32)
            kf = jnp.zeros((tm, LANE), F32)
            for t in range(N_SPLIT):
                part = rest.astype(BF16).astype(F32)
                rest = rest - part
                qf = jnp.where(lane == _feature_lane(e, t), part, qf)
                qf = jnp.where(lane == _feature_lane(e, N_SPLIT + t), 1.0, qf)
                kf = jnp.where(lane == _feature_lane(e, t), 1.0, kf)
                kf = jnp.where(lane == _feature_lane(e, N_SPLIT + t), -part, kf)
            vf = jnp.where(lane == _feature_lane(e, 0), 1.0, 0.0)
            cols = slice(h * LANE, (h + 1) * LANE)
            qx_ref[:, cols] = jnp.where(own, tile(QB_COL) * SCALE, qf).astype(BF16)
            kx_ref[:, cols] = jnp.where(own, tile(KB_COL), kf).astype(BF16)
            vx_ref[:, cols] = jnp.where(own, tile(VB_COL), vf).astype(BF16)

    out = jax.ShapeDtypeStruct((s_len, FOX_HEADS * LANE), BF16)
    blk = pl.BlockSpec((tm, FOX_HEADS * LANE), lambda i: (i, 0))
    return pl.pallas_call(
        body, name="fox_prep", grid=(s_len // tm,),
        in_specs=[pl.BlockSpec((tm, N_QKV), lambda i: (i, 0)), pl.BlockSpec((tm, LANE), lambda i: (i, 0))],
        out_specs=[blk, blk, blk], out_shape=[out, out, out],
        compiler_params=_params("parallel"),
    )(zq, c)


def _comm_parts(comm):
    return ([], [], [], []) if comm is None else (comm.specs, comm.out_shape, comm.scratch, comm.arrays)


def _fox_fwd(qx, kx, vx, comm=None, t_cap=1024):
    s_len = qx.shape[0]
    t = _div(s_len, t_cap, LANE)
    nq = s_len // t
    c_specs, c_shapes, c_scratch, c_arrays = _comm_parts(comm)

    def compute(q_ref, k_ref, v_ref, o_ref, o32_ref, lse_ref):
        i = pl.program_id(1)
        qs = [q_ref[:, e * LANE:(e + 1) * LANE] for e in range(2)]

        def step(j, carry, diag):
            rows = pl.ds(pl.multiple_of(j * t, t), t)
            new = []
            for e in range(2):
                m, acc = carry[e]
                s = lax.dot_general(qs[e], k_ref[rows, e * LANE:(e + 1) * LANE], (((1,), (1,)), ((), ())),
                                    preferred_element_type=F32)
                if diag:
                    s = jnp.where(_causal(t), s, NEG_INF)
                mn = jnp.maximum(m, jnp.max(s, axis=1, keepdims=True))
                pe = jnp.exp(s - mn)
                p_hi = pe.astype(BF16)
                p_lo = (pe - p_hi.astype(F32)).astype(BF16)
                vs = v_ref[rows, e * LANE:(e + 1) * LANE]
                acc = (acc * jnp.exp(m - mn) + jnp.dot(p_hi, vs, preferred_element_type=F32)
                       + jnp.dot(p_lo, vs, preferred_element_type=F32))
                new.append((mn, acc))
            return tuple(new)

        init = (jnp.full((t, 1), NEG_INF, F32), jnp.zeros((t, LANE), F32))
        carry = lax.fori_loop(0, i, lambda j, c: step(j, c, False), (init, init))
        carry = step(i, carry, True)
        outs, lses = [], []
        for e in range(2):
            m, acc = carry[e]
            l = acc[:, _feature_lane(e, 0):_feature_lane(e, 0) + 1]
            outs.append(acc / l)
            lses.append(m + jnp.log(l))
        out = jnp.where(_own_half(1), outs[1], outs[0])
        o_ref[...] = out.astype(BF16)
        o32_ref[...] = out
        lse_ref[...] = jnp.where(_own_half(1), lses[1], lses[0])

    body = _with_comm(comm, 3, 3, lambda: (pl.program_id(0) == 0) & (pl.program_id(1) == 0),
                      lambda: (pl.program_id(0) == N_PAIR - 1) & (pl.program_id(1) == nq - 1), compute)
    pair = pl.BlockSpec((s_len, 2 * LANE), lambda p, i: (0, p))
    tile = pl.BlockSpec((t, LANE), lambda p, i: (i, p))
    outs = pl.pallas_call(
        body, name="fox_fwd" if comm is None else "fox_fwd_comm", grid=(N_PAIR, nq),
        in_specs=[pl.BlockSpec((t, 2 * LANE), lambda p, i: (i, p)), pair, pair] + c_specs,
        out_specs=[tile, tile, tile] + c_specs,
        out_shape=[jax.ShapeDtypeStruct((s_len, FOX_W), BF16), jax.ShapeDtypeStruct((s_len, FOX_W), F32),
                   jax.ShapeDtypeStruct((s_len, FOX_W), F32)] + c_shapes,
        scratch_shapes=c_scratch,
        compiler_params=_params("arbitrary", "arbitrary"),
    )(qx, kx, vx, *c_arrays)
    return outs[0], outs[1], outs[2], outs[3:]


def _fox_stats(o, do, lse, tm=256):
    s_len = o.shape[0]
    tm = _div(s_len, tm, LANE)

    def body(o_ref, do_ref, lse_ref, dox_ref, st_ref):
        lane = lax.broadcasted_iota(jnp.int32, (1, LANE), 1)
        for p in range(N_PAIR):
            cols = slice(p * LANE, (p + 1) * LANE)
            dout = do_ref[:, cols]
            prod = o_ref[:, cols] * dout.astype(F32)
            lse = lse_ref[:, cols]
            st = jnp.zeros((tm, LANE), F32)
            for e in range(2):
                h = 2 * p + e
                dox_ref[:, h * LANE:(h + 1) * LANE] = jnp.where(_own_half(e), dout, jnp.zeros_like(dout))
                st = jnp.where(lane == e, lse[:, e * HEAD_DIM:e * HEAD_DIM + 1], st)
                delta = jnp.sum(jnp.where(_own_half(e), prod, 0.0), axis=1, keepdims=True)
                st = jnp.where(lane == 2 + e, delta, st)
            st_ref[p] = st.T[:8, :]

    row = pl.BlockSpec((tm, FOX_W), lambda i: (i, 0))
    return pl.pallas_call(
        body, name="fox_stats", grid=(s_len // tm,), in_specs=[row, row, row],
        out_specs=[pl.BlockSpec((tm, FOX_HEADS * LANE), lambda i: (i, 0)),
                   pl.BlockSpec((N_PAIR, 8, tm), lambda i: (0, 0, i))],
        out_shape=[jax.ShapeDtypeStruct((s_len, FOX_HEADS * LANE), BF16),
                   jax.ShapeDtypeStruct((N_PAIR, 8, s_len), F32)],
        compiler_params=_params("parallel"),
    )(o, do, lse)


def _fox_bwd(qx, kx, vx, dox, stats, comm=None, t_cap=512):
    s_len = qx.shape[0]
    t = _div(s_len, t_cap, LANE)
    n = s_len // t
    c_specs, c_shapes, c_scratch, c_arrays = _comm_parts(comm)

    def compute(q_ref, do_ref, st_ref, k_ref, v_ref, dq_ref, dk_ref, dv_ref, dc_ref):
        j = pl.program_id(1)
        lane = lax.broadcasted_iota(jnp.int32, (1, LANE), 1)

        @pl.when(j == 0)
        def _():
            dq_ref[...] = jnp.zeros_like(dq_ref)

        ks = [k_ref[:, e * LANE:(e + 1) * LANE] for e in range(2)]
        vs = [v_ref[:, e * LANE:(e + 1) * LANE] for e in range(2)]

        def step(i, carry, diag):
            rows = pl.ds(pl.multiple_of(i * t, t), t)
            new = []
            dq = jnp.zeros((t, LANE), F32)
            for e in range(2):
                dk, dv, dc = carry[e]
                q = q_ref[rows, e * LANE:(e + 1) * LANE]
                dout = do_ref[rows, e * LANE:(e + 1) * LANE]
                s_t = lax.dot_general(ks[e], q, (((1,), (1,)), ((), ())), preferred_element_type=F32)
                if diag:
                    s_t = jnp.where(_causal(t, keys_first=True), s_t, NEG_INF)
                p_t = jnp.exp(s_t - st_ref[0, e:e + 1, rows])
                dp_t = lax.dot_general(vs[e], dout, (((1,), (1,)), ((), ())), preferred_element_type=F32)
                ds_f = p_t * (dp_t - st_ref[0, 2 + e:3 + e, rows])
                ds_t = ds_f.astype(BF16)
                dc = dc + jnp.sum(ds_f, axis=1, keepdims=True)
                dv = dv + jnp.dot(p_t.astype(BF16), dout, preferred_element_type=F32)
                dk = dk + jnp.dot(ds_t, q, preferred_element_type=F32)
                dq_e = lax.dot_general(ds_t, ks[e], (((0,), (0,)), ((), ())), preferred_element_type=F32)
                dq = dq + jnp.where(_own_half(e), dq_e, 0.0)
                new.append((dk, dv, dc))
            dq_ref[rows, :] += dq * SCALE
            return tuple(new)

        zero = jnp.zeros((t, LANE), F32)
        init = (zero, zero, jnp.zeros((t, 1), F32))
        carry = step(j, (init, init), True)
        (dk0, dv0, dc0), (dk1, dv1, dc1) = lax.fori_loop(j + 1, n, lambda i, c: step(i, c, False), carry)
        dk_ref[...] = jnp.where(_own_half(1), dk1, dk0).astype(BF16)
        dv_ref[...] = jnp.where(_own_half(1), dv1, dv0).astype(BF16)
        dc_ref[...] = jnp.where(lane == 0, -dc0, jnp.where(lane == 1, -dc1, 0.0))

    body = _with_comm(comm, 5, 4, lambda: (pl.program_id(0) == 0) & (pl.program_id(1) == 0),
                      lambda: (pl.program_id(0) == N_PAIR - 1) & (pl.program_id(1) == n - 1), compute)
    pair = pl.BlockSpec((s_len, 2 * LANE), lambda p, j: (0, p))
    blk = pl.BlockSpec((t, 2 * LANE), lambda p, j: (j, p))
    tile = pl.BlockSpec((t, LANE), lambda p, j: (j, p))
    outs = pl.pallas_call(
        body, name="fox_bwd" if comm is None else "fox_bwd_comm", grid=(N_PAIR, n),
        in_specs=[pair, pair, pl.BlockSpec((1, 8, s_len), lambda p, j: (p, 0, 0)), blk, blk] + c_specs,
        out_specs=[pl.BlockSpec((s_len, LANE), lambda p, j: (0, p)), tile, tile, tile] + c_specs,
        out_shape=[jax.ShapeDtypeStruct((s_len, FOX_W), F32), jax.ShapeDtypeStruct((s_len, FOX_W), BF16),
                   jax.ShapeDtypeStruct((s_len, FOX_W), BF16), jax.ShapeDtypeStruct((s_len, FOX_W), F32)] + c_shapes,
        scratch_shapes=c_scratch,
        compiler_params=_params("arbitrary", "arbitrary"),
    )(qx, dox, stats, kx, vx, *c_arrays)
    return outs[0], outs[1], outs[2], outs[3], outs[4:]


def _mixer_out(attn_a, attn_b, zg, h, wpa, wpb, wout, g, b, tm=256):
    m = h.shape[0]
    tm = _div(m, tm, 8)

    def body(a_ref, b_ref, ga_ref, gb_ref, h_ref, wpa_ref, wpb_ref, wout_ref, g_ref, bb_ref,
             h1_ref, u_ref, mg_ref, ya_ref, yb_ref):
        ya = jnp.dot(a_ref[...], wpa_ref[...], preferred_element_type=F32)
        yb = jnp.dot(b_ref[...], wpb_ref[...], preferred_element_type=F32)
        merged = (jax.nn.sigmoid(ga_ref[...]) * ya + jax.nn.sigmoid(gb_ref[...]) * yb).astype(BF16)
        u = ALPHA * h_ref[...] + jnp.dot(merged, wout_ref[...], preferred_element_type=F32)
        u_ref[...] = u
        h1_ref[...] = _ln(u, g_ref[...], bb_ref[...])
        mg_ref[...] = merged
        ya_ref[...] = ya.astype(BF16)
        yb_ref[...] = yb.astype(BF16)

    row = pl.BlockSpec((tm, D_MODEL), lambda i: (i, 0))
    att = pl.BlockSpec((tm, SWA_Q), lambda i: (i, 0))
    vec = pl.BlockSpec((1, D_MODEL), lambda i: (0, 0))
    wsm = pl.BlockSpec((SWA_Q, D_MODEL), lambda i: (0, 0))
    return pl.pallas_call(
        body, name="mixer_out", grid=(m // tm,),
        in_specs=[att, att, row, pl.BlockSpec((tm, D_MODEL), lambda i: (i, 1)), row, wsm, wsm,
                  pl.BlockSpec((D_MODEL, D_MODEL), lambda i: (0, 0)), vec, vec],
        out_specs=[row, row, row, row, row],
        out_shape=[jax.ShapeDtypeStruct((m, D_MODEL), F32), jax.ShapeDtypeStruct((m, D_MODEL), F32),
                   jax.ShapeDtypeStruct((m, D_MODEL), BF16), jax.ShapeDtypeStruct((m, D_MODEL), BF16),
                   jax.ShapeDtypeStruct((m, D_MODEL), BF16)],
        compiler_params=_params("parallel"),
    )(attn_a, attn_b, zg, zg, h, wpa, wpb, wout, g, b)


def _mixer_bwd(dh1, u1, g, wout, ya, yb, zg, wpa, wpb, tm=256):
    m = dh1.shape[0]
    tm = _div(m, tm, 8)

    def body(dh_ref, u_ref, g_ref, wout_ref, ya_ref, yb_ref, ga_ref, gb_ref, wpa_ref, wpb_ref,
             du_ref, dg_ref, db_ref, dya_ref, dyb_ref, dga_ref, dgb_ref, da_ref, dbb_ref):
        @pl.when(pl.program_id(0) == 0)
        def _():
            dg_ref[...] = jnp.zeros_like(dg_ref)
            db_ref[...] = jnp.zeros_like(db_ref)

        du, dg, db = _ln_bwd_block(dh_ref[...], u_ref[...], g_ref[...])
        du_ref[...] = du
        dg_ref[...] += dg
        db_ref[...] += db
        dm = lax.dot_general(du.astype(BF16), wout_ref[...], (((1,), (1,)), ((), ())), preferred_element_type=F32)
        for y_ref, gate_ref, w_ref, dy_ref, dgate_ref, dattn_ref in (
                (ya_ref, ga_ref, wpa_ref, dya_ref, dga_ref, da_ref), (yb_ref, gb_ref, wpb_ref, dyb_ref, dgb_ref, dbb_ref)):
            sg = jax.nn.sigmoid(gate_ref[...])
            dy = (dm * sg).astype(BF16)
            dy_ref[...] = dy
            dgate_ref[...] = (dm * y_ref[...].astype(F32) * sg * (1.0 - sg)).astype(BF16)
            dattn_ref[...] = lax.dot_general(dy, w_ref[...], (((1,), (1,)), ((), ())),
                                             preferred_element_type=F32).astype(BF16)

    row = pl.BlockSpec((tm, D_MODEL), lambda i: (i, 0))
    att = pl.BlockSpec((tm, SWA_Q), lambda i: (i, 0))
    vec = pl.BlockSpec((1, D_MODEL), lambda i: (0, 0))
    wsm = pl.BlockSpec((SWA_Q, D_MODEL), lambda i: (0, 0))
    wide = jax.ShapeDtypeStruct((m, D_MODEL), BF16)
    narrow = jax.ShapeDtypeStruct((m, SWA_Q), BF16)
    sums = jax.ShapeDtypeStruct((1, D_MODEL), F32)
    return pl.pallas_call(
        body, name="mixer_bwd", grid=(m // tm,),
        in_specs=[row, row, vec, pl.BlockSpec((D_MODEL, D_MODEL), lambda i: (0, 0)), row, row, row,
                  pl.BlockSpec((tm, D_MODEL), lambda i: (i, 1)), wsm, wsm],
        out_specs=[row, vec, vec, row, row, row, row, att, att],
        out_shape=[jax.ShapeDtypeStruct((m, D_MODEL), F32), sums, sums, wide, wide, wide, wide, narrow, narrow],
        compiler_params=_params("arbitrary"),
    )(dh1, u1, g, wout, ya, yb, zg, zg, wpa, wpb)


def _shift_down(x, k, halo, first):
    rows = lax.broadcasted_iota(jnp.int32, (x.shape[0], 1), 0)
    y = pltpu.roll(x, k, 0)
    for r in range(k):
        fill = jnp.where(first, 0.0, halo[8 - k + r:8 - k + r + 1, :])
        y = jnp.where(rows == r, fill, y)
    return y


def _shift_up(x, k, halo, last):
    n = x.shape[0]
    rows = lax.broadcasted_iota(jnp.int32, (n, 1), 0)
    y = pltpu.roll(x, n - k, 0)
    for r in range(k):
        fill = jnp.where(last, 0.0, halo[r:r + 1, :])
        y = jnp.where(rows == n - k + r, fill, y)
    return y


def _conv_act(gate, gate_m1, gate_m2, cw, cb):
    return cb + cw[0:1, :] * gate_m2 + cw[1:2, :] * gate_m1 + cw[2:3, :] * gate


def _ffn_in(h1, wfi, tm=256):
    s_len = h1.shape[0]
    tm = _div(s_len, tm, 8)

    def body(a_ref, w_ref, o_ref):
        a = a_ref[...].astype(BF16)
        for g in range(2):
            for c in range(N_CHUNK):
                o_ref[c, g] = jnp.dot(a, w_ref[N_CHUNK * g + c], preferred_element_type=F32)

    return pl.pallas_call(
        body, name="ffn_in", grid=(s_len // tm,),
        in_specs=[pl.BlockSpec((tm, D_MODEL), lambda i: (i, 0)),
                  pl.BlockSpec((N_DEV, D_MODEL, FF_CHUNK), lambda i: (0, 0, 0))],
        out_specs=pl.BlockSpec((N_CHUNK, 2, tm, FF_CHUNK), lambda i: (0, 0, i, 0)),
        out_shape=jax.ShapeDtypeStruct((N_CHUNK, 2, s_len, FF_CHUNK), F32),
        compiler_params=_params("parallel"),
    )(h1, wfi)


def _conv_fwd(gu, cw, cb, tm=256):
    s_len = gu.shape[2]
    tm = _div(s_len, tm, 8)
    hb = tm // 8

    def body(gu_ref, gp_ref, cw_ref, cb_ref, o_ref):
        first = pl.program_id(1) == 0
        gate = gu_ref[0, 0]
        halo = gp_ref[0, 0]
        conv = _conv_act(gate, _shift_down(gate, 1, halo, first), _shift_down(gate, 2, halo, first),
                         cw_ref[0], cb_ref[0])
        o_ref[0] = (conv * jax.nn.sigmoid(conv) * gu_ref[0, 1]).astype(BF16)

    return pl.pallas_call(
        body, name="conv_fwd", grid=(N_CHUNK, s_len // tm),
        in_specs=[pl.BlockSpec((1, 2, tm, FF_CHUNK), lambda c, i: (c, 0, i, 0)),
                  pl.BlockSpec((1, 1, 8, FF_CHUNK), lambda c, i: (c, 0, jnp.maximum(i * hb - 1, 0), 0)),
                  pl.BlockSpec((1, 8, FF_CHUNK), lambda c, i: (c, 0, 0)),
                  pl.BlockSpec((1, 1, FF_CHUNK), lambda c, i: (c, 0, 0))],
        out_specs=pl.BlockSpec((1, tm, FF_CHUNK), lambda c, i: (c, i, 0)),
        out_shape=jax.ShapeDtypeStruct((N_CHUNK, s_len, FF_CHUNK), BF16),
        compiler_params=_params("parallel", "parallel"),
    )(gu, gu, cw, cb)


def _ffn_out_ln(act, wfo, res, g, b, tm=256):
    s_len = res.shape[0]
    tm = _div(s_len, tm, 8)

    def body(a_ref, w_ref, res_ref, g_ref, b_ref, u_ref, y_ref):
        u = ALPHA * res_ref[...]
        for c in range(N_CHUNK):
            u = u + jnp.dot(a_ref[c], w_ref[c], preferred_element_type=F32)
        u_ref[...] = u
        y_ref[...] = _ln(u, g_ref[...], b_ref[...])

    row = pl.BlockSpec((tm, D_MODEL), lambda i: (i, 0))
    vec = pl.BlockSpec((1, D_MODEL), lambda i: (0, 0))
    return pl.pallas_call(
        body, name="ffn_out_ln", grid=(s_len // tm,),
        in_specs=[pl.BlockSpec((N_CHUNK, tm, FF_CHUNK), lambda i: (0, i, 0)),
                  pl.BlockSpec((N_CHUNK, FF_CHUNK, D_MODEL), lambda i: (0, 0, 0)), row, vec, vec],
        out_specs=[row, row],
        out_shape=[jax.ShapeDtypeStruct((s_len, D_MODEL), F32), jax.ShapeDtypeStruct((s_len, D_MODEL), F32)],
        compiler_params=_params("parallel"),
    )(act, wfo, res, g, b)


def _ffn_out_bwd(dh2, u2, g, wfo, tm=256):
    s_len = dh2.shape[0]
    tm = _div(s_len, tm, 8)

    def body(dh_ref, u_ref, g_ref, w_ref, du_ref, dg_ref, db_ref, o_ref):
        @pl.when(pl.program_id(0) == 0)
        def _():
            dg_ref[...] = jnp.zeros_like(dg_ref)
            db_ref[...] = jnp.zeros_like(db_ref)

        du, dg, db = _ln_bwd_block(dh_ref[...], u_ref[...], g_ref[...])
        du_ref[...] = du
        dg_ref[...] += dg
        db_ref[...] += db
        du_b = du.astype(BF16)
        for c in range(N_CHUNK):
            o_ref[c] = lax.dot_general(du_b, w_ref[c], (((1,), (1,)), ((), ())), preferred_element_type=F32)

    row = pl.BlockSpec((tm, D_MODEL), lambda i: (i, 0))
    vec = pl.BlockSpec((1, D_MODEL), lambda i: (0, 0))
    sums = jax.ShapeDtypeStruct((1, D_MODEL), F32)
    return pl.pallas_call(
        body, name="ffn_out_bwd", grid=(s_len // tm,),
        in_specs=[row, row, vec, pl.BlockSpec((N_CHUNK, FF_CHUNK, D_MODEL), lambda i: (0, 0, 0))],
        out_specs=[row, vec, vec, pl.BlockSpec((N_CHUNK, tm, FF_CHUNK), lambda i: (0, i, 0))],
        out_shape=[jax.ShapeDtypeStruct((s_len, D_MODEL), F32), sums, sums,
                   jax.ShapeDtypeStruct((N_CHUNK, s_len, FF_CHUNK), F32)],
        compiler_params=_params("arbitrary"),
    )(dh2, u2, g, wfo)


def _g_w_ffn_out(act, du, tm=2048):
    s_len = du.shape[0]
    tm = _div(s_len, tm, 8)
    steps = s_len // tm

    def body(a_ref, g_ref, o_ref, acc_ref):
        s = pl.program_id(1)

        @pl.when(s == 0)
        def _():
            acc_ref[...] = jnp.zeros_like(acc_ref)

        acc_ref[...] += lax.dot_general(a_ref[0], g_ref[...].astype(BF16), (((0,), (0,)), ((), ())),
                                        preferred_element_type=F32)

        @pl.when(s == steps - 1)
        def _():
            o_ref[0] = acc_ref[...]

    return pl.pallas_call(
        body, name="g_w_ffn_out", grid=(N_CHUNK, steps),
        in_specs=[pl.BlockSpec((1, tm, FF_CHUNK), lambda c, s: (c, s, 0)),
                  pl.BlockSpec((tm, D_MODEL), lambda c, s: (s, 0))],
        out_specs=pl.BlockSpec((1, FF_CHUNK, D_MODEL), lambda c, s: (c, 0, 0)),
        out_shape=jax.ShapeDtypeStruct((N_CHUNK, FF_CHUNK, D_MODEL), F32),
        scratch_shapes=[pltpu.VMEM((FF_CHUNK, D_MODEL), F32)],
        compiler_params=_params("parallel", "arbitrary"),
    )(act, du)


def _d_h1(dgu, wfi, res, tm=256):
    s_len = res.shape[0]
    tm = _div(s_len, tm, 8)

    def body(a_ref, w_ref, res_ref, o_ref):
        acc = ALPHA * res_ref[...]
        for g in range(2):
            for c in range(N_CHUNK):
                acc = acc + lax.dot_general(a_ref[c, g], w_ref[N_CHUNK * g + c], (((1,), (1,)), ((), ())),
                                            preferred_element_type=F32)
        o_ref[...] = acc

    row = pl.BlockSpec((tm, D_MODEL), lambda i: (i, 0))
    return pl.pallas_call(
        body, name="d_h1", grid=(s_len // tm,),
        in_specs=[pl.BlockSpec((N_CHUNK, 2, tm, FF_CHUNK), lambda i: (0, 0, i, 0)),
                  pl.BlockSpec((N_DEV, D_MODEL, FF_CHUNK), lambda i: (0, 0, 0)), row],
        out_specs=row, out_shape=jax.ShapeDtypeStruct((s_len, D_MODEL), F32),
        compiler_params=_params("parallel"),
    )(dgu, wfi, res)


def _g_w_ffn_in(h1, dgu, tm=2048):
    s_len = h1.shape[0]
    tm = _div(s_len, tm, 8)
    steps = s_len // tm

    def body(a_ref, g_ref, o_ref, acc_ref):
        s = pl.program_id(1)

        @pl.when(s == 0)
        def _():
            acc_ref[...] = jnp.zeros_like(acc_ref)

        acc_ref[...] += lax.dot_general(a_ref[...].astype(BF16), g_ref[0, 0], (((0,), (0,)), ((), ())),
                                        preferred_element_type=F32)

        @pl.when(s == steps - 1)
        def _():
            o_ref[0] = acc_ref[...]

    return pl.pallas_call(
        body, name="g_w_ffn_in", grid=(N_DEV, steps),
        in_specs=[pl.BlockSpec((tm, D_MODEL), lambda d, s: (s, 0)),
                  pl.BlockSpec((1, 1, tm, FF_CHUNK), lambda d, s: (d % N_CHUNK, d // N_CHUNK, s, 0))],
        out_specs=pl.BlockSpec((1, D_MODEL, FF_CHUNK), lambda d, s: (d, 0, 0)),
        out_shape=jax.ShapeDtypeStruct((N_DEV, D_MODEL, FF_CHUNK), F32),
        scratch_shapes=[pltpu.VMEM((D_MODEL, FF_CHUNK), F32)],
        compiler_params=_params("parallel", "arbitrary"),
    )(h1, dgu)


def _conv_bwd(gu, dact, cw, cb, tm=256):
    s_len = gu.shape[2]
    tm = _div(s_len, tm, 8)
    nrow = s_len // tm
    hb = tm // 8

    def dconv_of(conv, up, da):
        sg = jax.nn.sigmoid(conv)
        return da * up * (sg * (1.0 + conv * (1.0 - sg)))

    def body(gu_ref, gp_ref, gun_ref, da_ref, dan_ref, cw_ref, cb_ref, dgu_ref, dcw_ref):
        i = pl.program_id(1)
        first = i == 0
        last = i == nrow - 1
        cw = cw_ref[0]
        cb = cb_ref[0]
        gate = gu_ref[0, 0]
        halo = gp_ref[0, 0]
        g_m1 = _shift_down(gate, 1, halo, first)
        g_m2 = _shift_down(gate, 2, halo, first)
        conv = _conv_act(gate, g_m1, g_m2, cw, cb)
        da = da_ref[0]
        sg = jax.nn.sigmoid(conv)
        dgu_ref[0, 1] = (da * conv * sg).astype(BF16)
        dconv = dconv_of(conv, gu_ref[0, 1], da)
        gate_n = gun_ref[0, 0]
        tail = gate[tm - 8:, :]
        conv_n = _conv_act(gate_n, _shift_down(gate_n, 1, tail, False), _shift_down(gate_n, 2, tail, False), cw, cb)
        dconv_n = dconv_of(conv_n, gun_ref[0, 1], dan_ref[0])
        dgate = (cw[2:3, :] * dconv + cw[1:2, :] * _shift_up(dconv, 1, dconv_n, last)
                 + cw[0:1, :] * _shift_up(dconv, 2, dconv_n, last))
        dgu_ref[0, 0] = dgate.astype(BF16)

        @pl.when(first)
        def _():
            dcw_ref[...] = jnp.zeros_like(dcw_ref)

        row = lax.broadcasted_iota(jnp.int32, (8, 1), 0)
        part = jnp.zeros((8, FF_CHUNK), F32)
        for r, term in enumerate((dconv * g_m2, dconv * g_m1, dconv * gate, dconv)):
            part = jnp.where(row == r, jnp.sum(term, axis=0, keepdims=True), part)
        dcw_ref[0] += part

    nxt = lambda i: jnp.minimum((i + 1) * hb, s_len // 8 - 1)
    main = pl.BlockSpec((1, 2, tm, FF_CHUNK), lambda c, i: (c, 0, i, 0))
    return pl.pallas_call(
        body, name="conv_bwd", grid=(N_CHUNK, nrow),
        in_specs=[main,
                  pl.BlockSpec((1, 1, 8, FF_CHUNK), lambda c, i: (c, 0, jnp.maximum(i * hb - 1, 0), 0)),
                  pl.BlockSpec((1, 2, 8, FF_CHUNK), lambda c, i: (c, 0, nxt(i), 0)),
                  pl.BlockSpec((1, tm, FF_CHUNK), lambda c, i: (c, i, 0)),
                  pl.BlockSpec((1, 8, FF_CHUNK), lambda c, i: (c, nxt(i), 0)),
                  pl.BlockSpec((1, 8, FF_CHUNK), lambda c, i: (c, 0, 0)),
                  pl.BlockSpec((1, 1, FF_CHUNK), lambda c, i: (c, 0, 0))],
        out_specs=[main, pl.BlockSpec((1, 8, FF_CHUNK), lambda c, i: (c, 0, 0))],
        out_shape=[jax.ShapeDtypeStruct((N_CHUNK, 2, s_len, FF_CHUNK), BF16),
                   jax.ShapeDtypeStruct((N_CHUNK, 8, FF_CHUNK), F32)],
        compiler_params=_params("parallel", "arbitrary"),
    )(gu, gu, gu, dact, dact, cw, cb)


def _loss_head(y, target, tm=256):
    m, d = y.shape
    tm = _div(m, tm, 8)

    def body(y_ref, t_ref, dy_ref, loss_ref):
        @pl.when(pl.program_id(0) == 0)
        def _():
            loss_ref[...] = jnp.zeros_like(loss_ref)

        err = y_ref[...] - t_ref[...]
        dy_ref[...] = err / d
        loss_ref[...] += 0.5 * jnp.sum(jnp.sum(err * err, axis=1, keepdims=True) / d, axis=0, keepdims=True)

    row = pl.BlockSpec((tm, d), lambda i: (i, 0))
    return pl.pallas_call(
        body, name="loss_head", grid=(m // tm,), in_specs=[row, row],
        out_specs=[row, pl.BlockSpec((8, LANE), lambda i: (0, 0))],
        out_shape=[jax.ShapeDtypeStruct((m, d), F32), jax.ShapeDtypeStruct((8, LANE), F32)],
        compiler_params=_params("arbitrary"),
    )(y, target)


def _sum_devices(r_ref):
    acc = r_ref[0].astype(F32)
    for d in range(1, N_DEV):
        acc = acc + r_ref[d].astype(F32)
    return acc


def _sum8(recv):
    rows = recv.shape[1]
    tr = _div(rows, ROW_BLOCK, 8)

    def body(r_ref, o_ref):
        o_ref[...] = _sum_devices(r_ref)

    return pl.pallas_call(
        body, name="sum8", grid=(rows // tr,),
        in_specs=[pl.BlockSpec((N_DEV, tr, LANE), lambda i: (0, i, 0))],
        out_specs=pl.BlockSpec((tr, LANE), lambda i: (i, 0)),
        out_shape=jax.ShapeDtypeStruct((rows, LANE), F32),
        compiler_params=_params("parallel"),
    )(recv)


def _adamw_math(w, g, m, v):
    m = ADAM_B1 * m + (1.0 - ADAM_B1) * g
    v = ADAM_B2 * v + (1.0 - ADAM_B2) * (g * g)
    m_hat = m / (1.0 - ADAM_B1 ** ADAM_STEP)
    v_hat = v / (1.0 - ADAM_B2 ** ADAM_STEP)
    return -ADAM_LR * (m_hat / (jnp.sqrt(v_hat) + ADAM_EPS) + ADAM_WD * w), m, v


def _adamw_rows(w, g, m, v, name):
    rows = w.shape[0]
    tr = _div(rows, ROW_BLOCK, 8)

    def body(w_ref, g_ref, m_ref, v_ref, d_ref, mo_ref, vo_ref):
        d_ref[...], mo_ref[...], vo_ref[...] = _adamw_math(w_ref[...], g_ref[...], m_ref[...], v_ref[...])

    blk = pl.BlockSpec((tr, LANE), lambda i: (i, 0))
    out = jax.ShapeDtypeStruct((rows, LANE), F32)
    return pl.pallas_call(
        body, name=name, grid=(rows // tr,), in_specs=[blk, blk, blk, blk], out_specs=[blk, blk, blk],
        out_shape=[out, out, out], compiler_params=_params("parallel"),
    )(w, g, m, v)


def _adamw_shard(recv, w, m, v, layer, prev, name):
    _, k, n = recv.shape
    tk = _div(k, 128, 16)

    def body(r_ref, w_ref, m_ref, v_ref, *rest):
        g_ref, d_ref, mo_ref, vo_ref = rest[-4:]
        g = _sum_devices(r_ref)
        g_ref[0] = g
        d_ref[0], mo_ref[0], vo_ref[0] = _adamw_math(w_ref[0], g, m_ref[0], v_ref[0])

    blk = pl.BlockSpec((1, tk, n), lambda i: (layer, i, 0))
    out = jax.ShapeDtypeStruct((DEPTH, k, n), F32)
    carried = [] if prev is None else list(prev)
    return pl.pallas_call(
        body, name=name, grid=(k // tk,),
        in_specs=[pl.BlockSpec((N_DEV, tk, n), lambda i: (0, i, 0)), blk, blk, blk]
        + [pl.BlockSpec(memory_space=pl.ANY)] * len(carried),
        out_specs=[blk, blk, blk, blk], out_shape=[out, out, out, out],
        input_output_aliases={4 + j: j for j in range(len(carried))},
        compiler_params=_params("parallel"),
    )(recv, w, m, v, *carried)


def _to_rows(flat, rows):
    flat = flat.reshape(-1)
    return jnp.pad(flat, (0, rows * LANE - flat.shape[0])).reshape(rows, LANE)


def _pad_cols_z(a):
    f0 = N_QKV
    g0 = N_QKV + FOX_HEADS
    pad = jnp.zeros(a.shape[:-1] + (F_PAD - FOX_HEADS,), a.dtype)
    return jnp.concatenate([a[..., :f0], a[..., g0:], a[..., f0:g0], pad], axis=-1)


def _unpad_cols_z(a):
    f0 = N_QKV + N_GATE
    return jnp.concatenate([a[..., :N_QKV], a[..., f0:f0 + FOX_HEADS], a[..., N_QKV:f0]], axis=-1)


def _shards_to_cols(g):
    _, k, n = g.shape
    return g.transpose(1, 0, 2).reshape(k, N_DEV * n)


def _cols_to_shards(full):
    k, n = full.shape
    return full.reshape(k, N_DEV, n // N_DEV).transpose(1, 0, 2)


def _layer_fwd(h, w, p, comm=None, late=None):
    zq, zg = _z_proj(h, w["w_in_p"], p["b_in_p"])
    qx, kx, vx = _fox_prep(zq, _cumsum_logf(zg))
    attn_a, lse_a = _swa_fwd(zq, p["sinks"])
    attn_b, attn_b32, lse_b, arrived = _fox_fwd(qx, kx, vx, comm)
    if late is not None:
        w, p = late(w, p, arrived)
    h1, u1, merged, ya, yb = _mixer_out(attn_a, attn_b, zg, h, w["w_proj_a"], w["w_proj_b"], w["w_out"],
                                        p["ln_mix_g"], p["ln_mix_b"])
    gu = _ffn_in(h1, w["w_ffn_in"])
    act = _conv_fwd(gu, p["conv_w"], p["conv_b"])
    u2, h2 = _ffn_out_ln(act, w["w_ffn_out"], h1, p["ln_ffn_g"], p["ln_ffn_b"])
    saved = dict(h=h, zq=zq, zg=zg, qx=qx, kx=kx, vx=vx, attn_a=attn_a, lse_a=lse_a, attn_b=attn_b,
                 attn_b32=attn_b32, lse_b=lse_b, h1=h1, u1=u1, merged=merged, ya=ya, yb=yb, gu=gu, act=act, u2=u2)
    return h2, saved, w, p


def _layer_bwd(dh2, sv, w, p, make_comm=None):
    s_len = dh2.shape[0]
    du2, d_ffn_g, d_ffn_b, dact = _ffn_out_bwd(dh2, sv["u2"], p["ln_ffn_g"], w["w_ffn_out"])
    g_ffn_out = _g_w_ffn_out(sv["act"], du2)
    dgu, dcw = _conv_bwd(sv["gu"], dact, p["conv_w"], p["conv_b"])
    dcw = dcw.transpose(1, 0, 2).reshape(8, D_FF)
    dh1 = _d_h1(dgu, w["w_ffn_in"], du2)
    g_ffn_in = _g_w_ffn_in(sv["h1"], dgu)
    du1, d_mix_g, d_mix_b, dya, dyb, dga, dgb, dattn_a, dattn_b = _mixer_bwd(
        dh1, sv["u1"], p["ln_mix_g"], w["w_out"], sv["ya"], sv["yb"], sv["zg"], w["w_proj_a"], w["w_proj_b"])
    g_out = _linear_tn(sv["merged"], du1, name="g_w_out", tn=1024)
    g_proj_a = _linear_tn(sv["attn_a"], dya, name="g_w_proj_a", tk=512, tn=1024)
    g_proj_b = _linear_tn(sv["attn_b"], dyb, name="g_w_proj_b", tk=512, tn=1024)
    dq_a, dk_a, dv_a, dsinks = _swa_bwd(sv["zq"], p["sinks"], sv["attn_a"], dattn_a, sv["lse_a"])
    big = dict(w_proj_a=_cols_to_shards(g_proj_a), w_proj_b=_cols_to_shards(g_proj_b),
               w_out=g_out.reshape(N_DEV, D_MODEL // N_DEV, D_MODEL), w_ffn_in=g_ffn_in,
               w_ffn_out=g_ffn_out.reshape(N_DEV, D_FF // N_DEV, D_MODEL))
    dox, stats = _fox_stats(sv["attn_b32"], dattn_b, sv["lse_b"])
    dq_b, dk_b, dv_b, dcc, arrived = _fox_bwd(sv["qx"], sv["kx"], sv["vx"], dox, stats,
                                              None if make_comm is None else make_comm(big))
    dc = jnp.pad(dcc.reshape(s_len, N_PAIR, LANE)[:, :, :2].reshape(s_len, FOX_HEADS), ((0, 0), (0, LANE - FOX_HEADS)))
    df = _forget_bwd(dc, sv["zg"])
    dz = jnp.concatenate([dq_a, dk_a.astype(BF16), dv_a.astype(BF16), dq_b.astype(BF16), dk_b, dv_b, dga, dgb, df,
                          jnp.zeros((s_len, F_PAD - LANE), BF16)], axis=1)
    dh = _linear(dz, w["w_in_p"], trans_b=True, res=du1, res_scale=ALPHA, name="d_h", tn=D_MODEL)
    g_in = _unpad_cols_z(_linear_tn(sv["h"], dz, name="g_w_in", tn=768))
    g_b_in = _unpad_cols_z(_colsum(dz, name="g_b_in"))
    big["w_in"] = _cols_to_shards(g_in)
    small = dict(ln_mix_g=d_mix_g, ln_mix_b=d_mix_b, b_in=g_b_in, attn_sinks=dsinks[:, :SWA_HEADS],
                 ln_ffn_g=d_ffn_g, ln_ffn_b=d_ffn_b, conv_w=dcw[:3], conv_b=dcw[3:4])
    return dh, big, small, arrived


def _w_in_layouts(w_in):
    return dict(w_in_p=_pad_cols_z(_shards_to_cols(w_in)))


def _other_layouts(w_proj_a, w_proj_b, w_out, w_ffn_in, w_ffn_out):
    return dict(w_proj_a=_shards_to_cols(w_proj_a), w_proj_b=_shards_to_cols(w_proj_b),
                w_out=w_out.reshape(D_MODEL, D_MODEL), w_ffn_in=w_ffn_in,
                w_ffn_out=w_ffn_out.reshape(N_CHUNK, FF_CHUNK, D_MODEL))


def _layer_params(r):
    return dict(
        b_in_p=_pad_cols_z(r["b_in"].reshape(1, N_IN)),
        sinks=jnp.pad(r["attn_sinks"].reshape(1, SWA_HEADS), ((0, 0), (0, LANE - SWA_HEADS))),
        ln_mix_g=r["ln_mix_g"].reshape(1, D_MODEL), ln_mix_b=r["ln_mix_b"].reshape(1, D_MODEL),
        ln_ffn_g=r["ln_ffn_g"].reshape(1, D_MODEL), ln_ffn_b=r["ln_ffn_b"].reshape(1, D_MODEL),
        conv_b=r["conv_b"].reshape(N_CHUNK, 1, FF_CHUNK))


def _conv_w_layout(conv_w):
    return jnp.pad(conv_w, ((0, 5), (0, 0))).reshape(8, N_CHUNK, FF_CHUNK).transpose(1, 0, 2)


def kernel(x, ln_mix_g, ln_mix_b, w_in, b_in, attn_sinks, w_proj_a, w_proj_b, w_out, ln_ffn_g, ln_ffn_b, w_ffn_in, conv_w, conv_b, w_ffn_out, loss_target, m_ln_mix_g, m_ln_mix_b, m_w_in, m_b_in, m_attn_sinks, m_w_proj_a, m_w_proj_b, m_w_out, m_ln_ffn_g, m_ln_ffn_b, m_w_ffn_in, m_conv_w, m_conv_b, m_w_ffn_out, v_ln_mix_g, v_ln_mix_b, v_w_in, v_b_in, v_attn_sinks, v_w_proj_a, v_w_proj_b, v_w_out, v_ln_ffn_g, v_ln_ffn_b, v_w_ffn_in, v_conv_w, v_conv_b, v_w_ffn_out):
    wts = dict(ln_mix_g=ln_mix_g, ln_mix_b=ln_mix_b, w_in=w_in, b_in=b_in, attn_sinks=attn_sinks, w_proj_a=w_proj_a,
               w_proj_b=w_proj_b, w_out=w_out, ln_ffn_g=ln_ffn_g, ln_ffn_b=ln_ffn_b, w_ffn_in=w_ffn_in,
               conv_w=conv_w, conv_b=conv_b, w_ffn_out=w_ffn_out)
    mom = dict(ln_mix_g=m_ln_mix_g, ln_mix_b=m_ln_mix_b, w_in=m_w_in, b_in=m_b_in, attn_sinks=m_attn_sinks,
               w_proj_a=m_w_proj_a, w_proj_b=m_w_proj_b, w_out=m_w_out, ln_ffn_g=m_ln_ffn_g, ln_ffn_b=m_ln_ffn_b,
               w_ffn_in=m_w_ffn_in, conv_w=m_conv_w, conv_b=m_conv_b, w_ffn_out=m_w_ffn_out)
    vel = dict(ln_mix_g=v_ln_mix_g, ln_mix_b=v_ln_mix_b, w_in=v_w_in, b_in=v_b_in, attn_sinks=v_attn_sinks,
               w_proj_a=v_w_proj_a, w_proj_b=v_w_proj_b, w_out=v_w_out, ln_ffn_g=v_ln_ffn_g, ln_ffn_b=v_ln_ffn_b,
               w_ffn_in=v_w_ffn_in, conv_w=v_conv_w, conv_b=v_conv_b, w_ffn_out=v_w_ffn_out)
    names = list(wts)
    big_names = [n for n, _, _ in BIG]
    small_names = [n for n, _ in SMALL]
    me = 4 * lax.axis_index("x") + 2 * lax.axis_index("y") + lax.axis_index("c")
    cw_shard = D_FF // N_DEV

    wb = {n: wts[n].astype(BF16) for n in big_names}
    ps = [_layer_params(dict(b_in=b_in[l], attn_sinks=attn_sinks[l], ln_mix_g=ln_mix_g[l], ln_mix_b=ln_mix_b[l],
                             ln_ffn_g=ln_ffn_g[l], ln_ffn_b=ln_ffn_b[l], conv_b=conv_b[l])) for l in range(DEPTH)]
    w_in_0, = _exchange([(wb["w_in"][0], True)], "gather_w_in_0")
    others = big_names[1:]
    gather_rest = _Comm([(wb[n][0], True) for n in others] + [(wb[n][1], True) for n in big_names] + [(conv_w, True)])
    next_layer = {}

    def late(w, p, arrived):
        conv_full = arrived[-1].transpose(1, 2, 0, 3).reshape(DEPTH, 3, D_FF)
        layer_1 = arrived[len(others):-1]
        next_layer["w"] = dict(_w_in_layouts(layer_1[0]), **_other_layouts(*layer_1[1:]))
        next_layer["p"] = dict(ps[1], conv_w=_conv_w_layout(conv_full[1]))
        return dict(w, **_other_layouts(*arrived[:len(others)])), dict(p, conv_w=_conv_w_layout(conv_full[0]))

    saved, ws = [None] * DEPTH, [None] * DEPTH
    h, saved[0], ws[0], ps[0] = _layer_fwd(x[0], _w_in_layouts(w_in_0), ps[0], gather_rest, late)
    h, saved[1], ws[1], ps[1] = _layer_fwd(h, next_layer["w"], next_layer["p"])
    dh, loss_part = _loss_head(h, loss_target[0])

    def small_rows(small):
        vec = jnp.concatenate([small[n].reshape(-1) for n in small_names] + [loss_part[0, 0].reshape(1)])
        return _to_rows(vec, SMALL_LAYER_ROWS)

    dh, big_1, small_1, _ = _layer_bwd(dh, saved[1], ws[1], ps[1])

    def exchange_early(big_0):
        return _Comm([(big_1[n].astype(BF16), False) for n in big_names] + [(small_rows(small_1), True)]
                     + [(big_0[n].astype(BF16), False) for n in others])

    grad_x, big_0, small_0, arrived = _layer_bwd(dh, saved[0], ws[0], ps[0], exchange_early)
    g_in_0, g_small_0 = _exchange([(big_0["w_in"].astype(BF16), False), (small_rows(small_0), True)],
                                  "exchange_grads_0")
    n_big = len(big_names)
    recv = [[g_in_0] + list(arrived[n_big + 1:]) + [g_small_0], list(arrived[:n_big + 1])]

    big_out = {}
    for t, n in enumerate(big_names):
        outs = None
        for l in reversed(range(DEPTH)):
            outs = _adamw_shard(recv[l][t], wts[n], mom[n], vel[n], l, outs, "adamw_%s_%d" % (n, l))
        big_out[n] = outs
    small_sum = [_sum8(recv[l][-1]).reshape(-1) for l in range(DEPTH)]
    g_small = {}
    off = 0
    for n, size in SMALL:
        g_small[n] = jnp.stack([small_sum[l][off:off + size] for l in range(DEPTH)])
        off += size
    loss = small_sum[0][off]
    g_small["conv_w"] = lax.dynamic_slice_in_dim(g_small["conv_w"].reshape(DEPTH, 3, D_FF), me * cw_shard, cw_shard,
                                                 axis=2)
    g_small = {n: g_small[n].reshape(wts[n].shape) for n in small_names}

    def pack_small(tree):
        return _to_rows(jnp.concatenate([tree[n].reshape(-1) for n in small_names]), SMALL_ROWS)

    small_out = (pack_small(g_small),) + tuple(_adamw_rows(pack_small(wts), pack_small(g_small), pack_small(mom),
                                                           pack_small(vel), "adamw_small"))

    def result(j):
        out = {n: big_out[n][j] for n in big_names}
        flat = small_out[j].reshape(-1)
        off = 0
        for n in small_names:
            out[n] = flat[off:off + wts[n].size].reshape(wts[n].shape)
            off += wts[n].size
        return [out[n] for n in names]

    return (loss, grad_x[None], *result(0), *result(1), *result(2), *result(3))
```

```python
import functools

import jax
import jax.numpy as jnp
import numpy as np
from jax import lax
from jax.experimental import pallas as pl
from jax.experimental.pallas import tpu as pltpu

F32 = jnp.float32
BF16 = jnp.bfloat16
MESH = pl.DeviceIdType.MESH

N_DEV = 8
DEPTH = 2
D_MODEL = 1024
HEAD_DIM = 64
SWA_Q = 512
SWA_KV = 128
FOX_W = 512
FOX_HEADS = 8
SWA_HEADS = 8
D_FF = 2816
N_IN = 4360
N_QKV = SWA_Q + 2 * SWA_KV + 3 * FOX_W
N_GATE = 2 * D_MODEL
F_PAD = 256
N_ZG = N_GATE + F_PAD
N_ZP = N_QKV + N_ZG
LN_EPS = 1e-5
NEG_INF = -1e30
ALPHA = (2 * DEPTH) ** 0.25
SCALE = HEAD_DIM ** -0.5
SLOPES = tuple(2.0 ** (-8.0 * (h + 1) / SWA_HEADS) for h in range(SWA_HEADS))

ADAM_LR = 0.001
ADAM_B1 = 0.9
ADAM_B2 = 0.999
ADAM_EPS = 1e-08
ADAM_WD = 0.01
ADAM_STEP = 10

LANE = 128
VMEM_LIMIT = 56 * 1024 * 1024

BIG = (("w_in", (D_MODEL, N_IN), 1), ("w_proj_a", (SWA_Q, D_MODEL), 1), ("w_proj_b", (FOX_W, D_MODEL), 1),
       ("w_out", (D_MODEL, D_MODEL), 0), ("w_ffn_in", (D_MODEL, 2 * D_FF), 1), ("w_ffn_out", (D_FF, D_MODEL), 0))
SMALL = (("ln_mix_g", D_MODEL), ("ln_mix_b", D_MODEL), ("b_in", N_IN), ("attn_sinks", SWA_HEADS),
         ("ln_ffn_g", D_MODEL), ("ln_ffn_b", D_MODEL), ("conv_w", 3 * D_FF), ("conv_b", D_FF))
ROW_BLOCK = 512
SMALL_LAYER_ROWS = -(-(sum(n for _, n in SMALL) + 1) // (8 * LANE)) * 8
SMALL_ROWS = ROW_BLOCK
FF_CHUNK = 2 * D_FF // N_DEV
N_CHUNK = D_FF // FF_CHUNK


def _div(n, cap, unit):
    if n <= cap:
        return n
    best = None
    for t in range(unit, cap + 1, unit):
        if n % t == 0:
            best = t
    assert best is not None, (n, cap, unit)
    return best


def _params(*sem):
    return pltpu.CompilerParams(dimension_semantics=sem, vmem_limit_bytes=VMEM_LIMIT)


def _peer(r):
    x, y, c = lax.axis_index("x"), lax.axis_index("y"), lax.axis_index("c")
    px = 1 - x if (r >> 2) & 1 else x
    py = 1 - y if (r >> 1) & 1 else y
    pc = 1 - c if r & 1 else c
    return (px, py, pc), 4 * px + 2 * py + pc


class _Comm:
    def __init__(self, tensors):
        self.arrays = [x for x, _ in tensors]
        self.gathers = [g for _, g in tensors]
        self.n = len(tensors)
        self.out_shape = [jax.ShapeDtypeStruct((N_DEV,) + (x.shape if g else x.shape[1:]), x.dtype)
                          for x, g in tensors]
        self.specs = [pl.BlockSpec(memory_space=pl.ANY)] * self.n
        self.scratch = [pltpu.SemaphoreType.DMA((N_DEV - 1, self.n)), pltpu.SemaphoreType.DMA((N_DEV - 1, self.n)),
                        pltpu.SemaphoreType.DMA((self.n,))]

    def _copies(self, x_refs, out_refs, sems):
        send_sems, recv_sems, local_sems = sems
        _, me = _peer(0)

        def src(t, idx):
            return x_refs[t] if self.gathers[t] else x_refs[t].at[idx]

        def remote(r, t, mine):
            peer, pid = _peer(r)
            return pltpu.make_async_remote_copy(src_ref=src(t, pid), dst_ref=out_refs[t].at[me if mine else pid],
                                                send_sem=send_sems.at[r - 1, t], recv_sem=recv_sems.at[r - 1, t],
                                                device_id=peer, device_id_type=MESH)

        pairs = [(r, t) for r in range(1, N_DEV) for t in range(self.n)]
        local = [pltpu.make_async_copy(src(t, me), out_refs[t].at[me], local_sems.at[t]) for t in range(self.n)]
        return local, [remote(r, t, True) for r, t in pairs], lambda: [remote(r, t, False) for r, t in pairs]

    def start(self, x_refs, out_refs, sems):
        local, sent, _ = self._copies(x_refs, out_refs, sems)
        for cp in local + sent:
            cp.start()

    def wait(self, x_refs, out_refs, sems):
        local, sent, landing = self._copies(x_refs, out_refs, sems)
        for cp in landing():
            cp.wait_recv()
        for cp in sent:
            cp.wait_send()
        for cp in local:
            cp.wait()


def _exchange(tensors, name):
    comm = _Comm(tensors)
    n = comm.n

    def body(*refs):
        comm.start(refs[:n], refs[n:2 * n], refs[2 * n:])
        comm.wait(refs[:n], refs[n:2 * n], refs[2 * n:])

    return pl.pallas_call(body, name=name, out_shape=comm.out_shape, in_specs=comm.specs, out_specs=comm.specs,
                          scratch_shapes=comm.scratch)(*comm.arrays)


def _with_comm(comm, n_in, n_out, first, last, compute):
    nc = comm.n if comm is not None else 0

    def body(*refs):
        ins, x_refs = refs[:n_in], refs[n_in:n_in + nc]
        outs = refs[n_in + nc:n_in + nc + n_out]
        out_refs = refs[n_in + nc + n_out:n_in + 2 * nc + n_out]
        sems = refs[n_in + 2 * nc + n_out:]
        if nc:
            @pl.when(first())
            def _():
                comm.start(x_refs, out_refs, sems)

        compute(*ins, *outs)
        if nc:
            @pl.when(last())
            def _():
                comm.wait(x_refs, out_refs, sems)

    return body


def _linear(a, b, *, name, trans_b=False, bias=None, res=None, res_scale=1.0, out_dtype=F32, tm=512, tn=640):
    m, k = a.shape
    n = b.shape[0] if trans_b else b.shape[1]
    tm = _div(m, tm, 8)
    tn = _div(n, tn, LANE)
    dn = (((1,), (1,)), ((), ())) if trans_b else (((1,), (0,)), ((), ()))

    def body(*refs):
        a_ref, b_ref = refs[0], refs[1]
        rest = list(refs[2:])
        bias_ref = rest.pop(0) if bias is not None else None
        res_ref = rest.pop(0) if res is not None else None
        o_ref = rest.pop(0)
        acc = lax.dot_general(a_ref[...].astype(BF16), b_ref[...].astype(BF16), dn, preferred_element_type=F32)
        if bias_ref is not None:
            acc = acc + bias_ref[...]
        if res_ref is not None:
            acc = acc + res_scale * res_ref[...].astype(F32)
        o_ref[...] = acc.astype(out_dtype)

    in_specs = [pl.BlockSpec((tm, k), lambda i, j: (i, 0)),
                pl.BlockSpec((tn, k), lambda i, j: (j, 0)) if trans_b else pl.BlockSpec((k, tn), lambda i, j: (0, j))]
    args = [a, b]
    if bias is not None:
        in_specs.append(pl.BlockSpec((1, tn), lambda i, j: (0, j)))
        args.append(bias)
    if res is not None:
        in_specs.append(pl.BlockSpec((tm, tn), lambda i, j: (i, j)))
        args.append(res)
    return pl.pallas_call(
        body, name=name, grid=(m // tm, n // tn), in_specs=in_specs,
        out_specs=pl.BlockSpec((tm, tn), lambda i, j: (i, j)),
        out_shape=jax.ShapeDtypeStruct((m, n), out_dtype),
        compiler_params=_params("parallel", "arbitrary"),
    )(*args)


def _z_proj(h, w_in_p, b_p, tm=512):
    m, k = h.shape
    tm = _div(m, tm, 8)

    def body(h_ref, w_ref, b_ref, zq_ref, zg_ref):
        a = h_ref[...].astype(BF16)
        zq_ref[...] = (jnp.dot(a, w_ref[:, :N_QKV], preferred_element_type=F32) + b_ref[:, :N_QKV]).astype(BF16)
        zg_ref[...] = jnp.dot(a, w_ref[:, N_QKV:], preferred_element_type=F32) + b_ref[:, N_QKV:]

    return pl.pallas_call(
        body, name="z_proj", grid=(m // tm,),
        in_specs=[pl.BlockSpec((tm, k), lambda i: (i, 0)), pl.BlockSpec((k, N_ZP), lambda i: (0, 0)),
                  pl.BlockSpec((1, N_ZP), lambda i: (0, 0))],
        out_specs=[pl.BlockSpec((tm, N_QKV), lambda i: (i, 0)), pl.BlockSpec((tm, N_ZG), lambda i: (i, 0))],
        out_shape=[jax.ShapeDtypeStruct((m, N_QKV), BF16), jax.ShapeDtypeStruct((m, N_ZG), F32)],
        compiler_params=_params("parallel"),
    )(h, w_in_p, b_p)


def _linear_tn(a, g, *, name, tk=1024, tn=640, tm=2048):
    m, k = a.shape
    n = g.shape[1]
    tk = _div(k, tk, LANE)
    tn = _div(n, tn, LANE)
    tm = _div(m, tm, 8)
    steps = m // tm

    def body(a_ref, g_ref, o_ref, acc_ref):
        s = pl.program_id(2)

        @pl.when(s == 0)
        def _():
            acc_ref[...] = jnp.zeros_like(acc_ref)

        acc_ref[...] += lax.dot_general(a_ref[...].astype(BF16), g_ref[...].astype(BF16), (((0,), (0,)), ((), ())),
                                        preferred_element_type=F32)

        @pl.when(s == steps - 1)
        def _():
            o_ref[...] = acc_ref[...]

    return pl.pallas_call(
        body, name=name, grid=(k // tk, n // tn, steps),
        in_specs=[pl.BlockSpec((tm, tk), lambda i, j, s: (s, i)), pl.BlockSpec((tm, tn), lambda i, j, s: (s, j))],
        out_specs=pl.BlockSpec((tk, tn), lambda i, j, s: (i, j)),
        out_shape=jax.ShapeDtypeStruct((k, n), F32),
        scratch_shapes=[pltpu.VMEM((tk, tn), F32)],
        compiler_params=_params("parallel", "parallel", "arbitrary"),
    )(a, g)


def _colsum(g, *, name, tm=512):
    m, n = g.shape
    tm = _div(m, tm, 8)

    def body(g_ref, o_ref):
        @pl.when(pl.program_id(0) == 0)
        def _():
            o_ref[...] = jnp.zeros_like(o_ref)

        o_ref[...] += jnp.sum(g_ref[...].astype(F32), axis=0, keepdims=True)

    return pl.pallas_call(
        body, name=name, grid=(m // tm,),
        in_specs=[pl.BlockSpec((tm, n), lambda i: (i, 0))],
        out_specs=pl.BlockSpec((1, n), lambda i: (0, 0)),
        out_shape=jax.ShapeDtypeStruct((1, n), F32),
        compiler_params=_params("arbitrary"),
    )(g)


def _ln(u, g, b):
    mu = jnp.mean(u, axis=-1, keepdims=True)
    d = u - mu
    var = jnp.mean(d * d, axis=-1, keepdims=True)
    return d * lax.rsqrt(var + LN_EPS) * g + b


def _ln_bwd_block(dy, u, g):
    mu = jnp.mean(u, axis=-1, keepdims=True)
    dd = u - mu
    rstd = lax.rsqrt(jnp.mean(dd * dd, axis=-1, keepdims=True) + LN_EPS)
    xhat = dd * rstd
    dxh = dy * g
    m1 = jnp.mean(dxh, axis=-1, keepdims=True)
    m2 = jnp.mean(dxh * xhat, axis=-1, keepdims=True)
    return (rstd * (dxh - m1 - xhat * m2), jnp.sum(dy * xhat, axis=0, keepdims=True),
            jnp.sum(dy, axis=0, keepdims=True))


def _tri(n, upper):
    r = lax.broadcasted_iota(jnp.int32, (n, n), 0)
    c = lax.broadcasted_iota(jnp.int32, (n, n), 1)
    return jnp.where((c >= r) if upper else (c <= r), 1.0, 0.0).astype(F32)


def _cumsum_logf(zg):
    s = zg.shape[0]
    nb = s // LANE
    fcol = N_GATE // LANE

    def body(f_ref, c_ref, carry_ref):
        @pl.when(pl.program_id(0) == 0)
        def _():
            carry_ref[...] = jnp.zeros_like(carry_ref)

        f = f_ref[...]
        logf = jnp.minimum(f, 0.0) - jnp.log(1.0 + jnp.exp(-jnp.abs(f)))
        c = jnp.dot(_tri(LANE, False), logf, precision=lax.Precision.HIGHEST, preferred_element_type=F32)
        c = c + carry_ref[0:1, :]
        c_ref[...] = c
        carry_ref[...] = jnp.broadcast_to(c[LANE - 1:LANE, :], carry_ref.shape)

    return pl.pallas_call(
        body, name="cumsum_logf", grid=(nb,),
        in_specs=[pl.BlockSpec((LANE, LANE), lambda i: (i, fcol))],
        out_specs=pl.BlockSpec((LANE, LANE), lambda i: (i, 0)),
        out_shape=jax.ShapeDtypeStruct((s, LANE), F32),
        scratch_shapes=[pltpu.VMEM((8, LANE), F32)],
        compiler_params=_params("arbitrary"),
    )(zg)


def _forget_bwd(dc, zg):
    s = zg.shape[0]
    nb = s // LANE
    fcol = N_GATE // LANE

    def body(dc_ref, f_ref, o_ref, carry_ref):
        @pl.when(pl.program_id(0) == 0)
        def _():
            carry_ref[...] = jnp.zeros_like(carry_ref)

        dc = dc_ref[...]
        dlogf = jnp.dot(_tri(LANE, True), dc, precision=lax.Precision.HIGHEST, preferred_element_type=F32)
        dlogf = dlogf + carry_ref[0:1, :]
        o_ref[...] = (dlogf * jax.nn.sigmoid(-f_ref[...])).astype(BF16)
        carry_ref[...] = jnp.broadcast_to(dlogf[0:1, :], carry_ref.shape)

    return pl.pallas_call(
        body, name="forget_bwd", grid=(nb,),
        in_specs=[pl.BlockSpec((LANE, LANE), lambda i: (nb - 1 - i, 0)),
                  pl.BlockSpec((LANE, LANE), lambda i: (nb - 1 - i, fcol))],
        out_specs=pl.BlockSpec((LANE, LANE), lambda i: (nb - 1 - i, 0)),
        out_shape=jax.ShapeDtypeStruct((s, LANE), BF16),
        scratch_shapes=[pltpu.VMEM((8, LANE), F32)],
        compiler_params=_params("arbitrary"),
    )(dc, zg)


KA_COL = SWA_Q // LANE
VA_COL = KA_COL + 1


def _half_masks():
    lane = lax.broadcasted_iota(jnp.int32, (1, LANE), 1)
    hi = lane >= HEAD_DIM
    return (jnp.logical_not(hi), hi)


def _both_halves(x, sel):
    xs = jnp.where(sel, x, 0.0)
    return xs + pltpu.roll(xs, HEAD_DIM, 1)


SWA_PER_KV = 4
WIDE = SWA_PER_KV * LANE


def _swa_bias():
    k = np.arange(2 * LANE)[:, None]
    q = np.arange(LANE)[None, :]
    dist = (q + LANE - k).astype(np.float32)
    valid = (dist >= 0) & (dist < LANE)
    per_head = [np.where(valid, np.float32(-s) * dist, np.float32(NEG_INF)) for s in SLOPES]
    return jnp.asarray(np.stack([np.concatenate(per_head[SWA_PER_KV * hk:SWA_PER_KV * (hk + 1)], axis=1)
                                 for hk in range(2)]), F32)


def _no_previous_block(i_blk):
    k = lax.broadcasted_iota(jnp.int32, (2 * LANE, WIDE), 0)
    return jnp.where((i_blk == 0) & (k < LANE), NEG_INF, 0.0)


def _stack_heads(ref, blk, hk, halves, scale):
    tiles = []
    for j in range(SWA_PER_KV):
        p = 2 * hk + j // 2
        t = ref[blk, p * LANE:(p + 1) * LANE]
        if scale:
            t = _scaled(t)
        tiles.append(jnp.where(halves[j % 2], t, jnp.zeros_like(t)))
    return jnp.concatenate(tiles, axis=0)


def _pair_tile(wide, pp, row_halves):
    a = wide[:, (2 * pp) * LANE:(2 * pp + 1) * LANE]
    b = wide[:, (2 * pp + 1) * LANE:(2 * pp + 2) * LANE]
    return jnp.where(row_halves[0], a, b).T


def _lane_blocks(rows8, hk):
    return jnp.concatenate([rows8[SWA_PER_KV * hk + j:SWA_PER_KV * hk + j + 1, :] for j in range(SWA_PER_KV)], axis=1)


def _row_halves():
    hi = lax.broadcasted_iota(jnp.int32, (LANE, 1), 0) >= HEAD_DIM
    return (jnp.logical_not(hi), hi)


NT = (((1,), (1,)), ((), ()))


def _scaled(q):
    return (q.astype(F32) * SCALE).astype(BF16)


SWA_GROUP = 4


def _swa_group(s_len):
    return SWA_GROUP if (s_len // LANE) % SWA_GROUP == 0 else 1


def _swa_specs(group):
    rows = group * LANE
    prev = lambda i: jnp.maximum(i * group - 1, 0)
    return [pl.BlockSpec((rows, SWA_Q), lambda i: (i, 0)),
            pl.BlockSpec((rows, LANE), lambda i: (i, KA_COL)), pl.BlockSpec((rows, LANE), lambda i: (i, VA_COL)),
            pl.BlockSpec((LANE, LANE), lambda i: (prev(i), KA_COL)),
            pl.BlockSpec((LANE, LANE), lambda i: (prev(i), VA_COL))]


def _swa_window(g, cur_ref, prev_ref):
    before = prev_ref[...] if g == 0 else cur_ref[(g - 1) * LANE:g * LANE, :]
    return jnp.concatenate([before, cur_ref[g * LANE:(g + 1) * LANE, :]], axis=0).astype(F32)


def _swa_fwd(zq, sinks):
    s_len = zq.shape[0]
    group = _swa_group(s_len)
    rows = group * LANE
    sink_lanes = jnp.repeat(sinks[:, :SWA_HEADS], LANE, axis=1)

    def body(q_ref, kc_ref, vc_ref, kp_ref, vp_ref, sink_ref, bias_ref, o_ref, lse_ref):
        halves = _half_masks()
        row_halves = _row_halves()
        for g in range(group):
            blk = slice(g * LANE, (g + 1) * LANE)
            kcat = _swa_window(g, kc_ref, kp_ref)
            vcat = _swa_window(g, vc_ref, vp_ref)
            lse_rows = []
            for hk in range(2):
                kb = _both_halves(kcat, halves[hk]).astype(BF16)
                v_t = _both_halves(vcat, halves[hk]).T.astype(BF16)
                q4 = _stack_heads(q_ref, blk, hk, halves, True)
                s_t = lax.dot_general(kb, q4, NT, preferred_element_type=F32) + bias_ref[hk]
                if g == 0:
                    s_t = s_t + _no_previous_block(pl.program_id(0))
                sink = sink_ref[:, hk * WIDE:(hk + 1) * WIDE]
                m = jnp.maximum(jnp.max(s_t, axis=0, keepdims=True), sink)
                pe = jnp.exp(s_t - m)
                den = jnp.sum(pe, axis=0, keepdims=True) + jnp.exp(sink - m)
                out_t = jnp.dot(v_t, (pe * (1.0 / den)).astype(BF16), preferred_element_type=F32)
                for pp in range(2):
                    p = 2 * hk + pp
                    o_ref[blk, p * LANE:(p + 1) * LANE] = _pair_tile(out_t, pp, row_halves).astype(BF16)
                lse4 = m + jnp.log(den)
                lse_rows += [lse4[:, j * LANE:(j + 1) * LANE] for j in range(SWA_PER_KV)]
            lse_ref[:, blk] = jnp.concatenate(lse_rows, axis=0)

    return pl.pallas_call(
        body, name="swa_fwd", grid=(s_len // rows,),
        in_specs=_swa_specs(group) + [pl.BlockSpec((1, SWA_HEADS * LANE), lambda i: (0, 0)),
                                      pl.BlockSpec((2, 2 * LANE, WIDE), lambda i: (0, 0, 0))],
        out_specs=[pl.BlockSpec((rows, SWA_Q), lambda i: (i, 0)), pl.BlockSpec((SWA_HEADS, rows), lambda i: (0, i))],
        out_shape=[jax.ShapeDtypeStruct((s_len, SWA_Q), BF16), jax.ShapeDtypeStruct((SWA_HEADS, s_len), F32)],
        compiler_params=_params("parallel"),
    )(zq, zq, zq, zq, zq, sink_lanes, _swa_bias())


def _swa_bwd(zq, sinks, o, do, lse):
    s_len = zq.shape[0]
    group = _swa_group(s_len)
    rows = group * LANE

    def body(q_ref, kc_ref, vc_ref, kp_ref, vp_ref, sink_ref, bias_ref, o_ref, do_ref, lse_ref,
             dq_ref, dk_ref, dv_ref, ds_ref):
        halves = _half_masks()
        row_halves = _row_halves()
        lane = lax.broadcasted_iota(jnp.int32, (1, LANE), 1)
        dsink = jnp.zeros((1, LANE), F32)
        for g in range(group):
            blk = slice(g * LANE, (g + 1) * LANE)
            i_blk = pl.program_id(0) * group + g
            kcat = _swa_window(g, kc_ref, kp_ref)
            vcat = _swa_window(g, vc_ref, vp_ref)
            lse_rows = lse_ref[:, blk]
            prod = do_ref[blk, :].astype(F32) * o_ref[blk, :].astype(F32)
            select = (lax.broadcasted_iota(jnp.int32, (SWA_HEADS, SWA_Q), 1) // HEAD_DIM
                      == lax.broadcasted_iota(jnp.int32, (SWA_HEADS, SWA_Q), 0))
            delta_rows = lax.dot_general(jnp.where(select, 1.0, 0.0), prod, NT, precision=lax.Precision.HIGHEST,
                                         preferred_element_type=F32)
            dk_tot = jnp.zeros((2 * LANE, LANE), F32)
            dv_tot = jnp.zeros((2 * LANE, LANE), F32)
            for hk in range(2):
                kb = _both_halves(kcat, halves[hk])
                k_t = kb.T.astype(BF16)
                kb = kb.astype(BF16)
                vb = _both_halves(vcat, halves[hk]).astype(BF16)
                q4 = _stack_heads(q_ref, blk, hk, halves, True)
                do4 = _stack_heads(do_ref, blk, hk, halves, False)
                lse4 = _lane_blocks(lse_rows, hk)
                delta4 = _lane_blocks(delta_rows, hk)
                s_t = lax.dot_general(kb, q4, NT, preferred_element_type=F32) + bias_ref[hk]
                if g == 0:
                    s_t = s_t + _no_previous_block(pl.program_id(0))
                p_t = jnp.exp(s_t - lse4)
                dp_t = lax.dot_general(vb, do4, NT, preferred_element_type=F32)
                ds_t = (p_t * (dp_t - delta4)).astype(BF16)
                sink_part = jnp.exp(sink_ref[:, hk * WIDE:(hk + 1) * WIDE] - lse4) * delta4
                for j in range(SWA_PER_KV):
                    dsink_h = -jnp.sum(sink_part[:, j * LANE:(j + 1) * LANE], axis=1, keepdims=True)
                    dsink = dsink + jnp.where(lane == SWA_PER_KV * hk + j, dsink_h, 0.0)
                dq_t = jnp.dot(k_t, ds_t, preferred_element_type=F32)
                for pp in range(2):
                    p = 2 * hk + pp
                    dq_ref[blk, p * LANE:(p + 1) * LANE] = (_pair_tile(dq_t, pp, row_halves) * SCALE).astype(BF16)
                dk_acc = jnp.dot(ds_t, q4, preferred_element_type=F32)
                dv_acc = jnp.dot(p_t.astype(BF16), do4, preferred_element_type=F32)
                dk_tot = dk_tot + jnp.where(halves[hk], dk_acc + pltpu.roll(dk_acc, HEAD_DIM, 1), 0.0)
                dv_tot = dv_tot + jnp.where(halves[hk], dv_acc + pltpu.roll(dv_acc, HEAD_DIM, 1), 0.0)
            cur = pl.ds(pl.multiple_of(i_blk * LANE, LANE), LANE)
            dk_ref[cur, :] = dk_tot[LANE:, :]
            dv_ref[cur, :] = dv_tot[LANE:, :]

            def add_previous(i_blk=i_blk, dk_tot=dk_tot, dv_tot=dv_tot):
                prv = pl.ds(pl.multiple_of((i_blk - 1) * LANE, LANE), LANE)
                dk_ref[prv, :] += dk_tot[:LANE, :]
                dv_ref[prv, :] += dv_tot[:LANE, :]

            if g == 0:
                pl.when(i_blk > 0)(add_previous)
            else:
                add_previous()

        @pl.when(pl.program_id(0) == 0)
        def _():
            ds_ref[...] = jnp.zeros_like(ds_ref)

        ds_ref[...] += dsink

    blk512 = pl.BlockSpec((rows, SWA_Q), lambda i: (i, 0))
    full = pl.BlockSpec((s_len, LANE), lambda i: (0, 0))
    vec = pl.BlockSpec((1, LANE), lambda i: (0, 0))
    return pl.pallas_call(
        body, name="swa_bwd", grid=(s_len // rows,),
        in_specs=_swa_specs(group) + [pl.BlockSpec((1, SWA_HEADS * LANE), lambda i: (0, 0)),
                                      pl.BlockSpec((2, 2 * LANE, WIDE), lambda i: (0, 0, 0)), blk512, blk512,
                                      pl.BlockSpec((SWA_HEADS, rows), lambda i: (0, i))],
        out_specs=[blk512, full, full, vec],
        out_shape=[jax.ShapeDtypeStruct((s_len, SWA_Q), BF16), jax.ShapeDtypeStruct((s_len, LANE), F32),
                   jax.ShapeDtypeStruct((s_len, LANE), F32), jax.ShapeDtypeStruct((1, LANE), F32)],
        compiler_params=_params("arbitrary"),
    )(zq, zq, zq, zq, zq, jnp.repeat(sinks[:, :SWA_HEADS], LANE, axis=1), _swa_bias(), o, do, lse)


QB_COL = (SWA_Q + 2 * SWA_KV) // LANE
KB_COL = QB_COL + FOX_W // LANE
VB_COL = KB_COL + FOX_W // LANE
N_PAIR = FOX_HEADS // 2


def _causal(t, keys_first=False):
    r = lax.broadcasted_iota(jnp.int32, (t, t), 0)
    c = lax.broadcasted_iota(jnp.int32, (t, t), 1)
    return c >= r if keys_first else r >= c


N_SPLIT = 3


def _own_half(e):
    hi = lax.broadcasted_iota(jnp.int32, (1, LANE), 1) >= HEAD_DIM
    return hi if e else jnp.logical_not(hi)


def _feature_lane(e, t):
    return HEAD_DIM * (1 - e) + t


def _fox_prep(zq, c, tm=256):
    s_len = zq.shape[0]
    tm = _div(s_len, tm, 8)

    def body(z_ref, c_ref, qx_ref, kx_ref, vx_ref):
        lane = lax.broadcasted_iota(jnp.int32, (1, LANE), 1)
        for h in range(FOX_HEADS):
            p, e = divmod(h, 2)
            own = _own_half(e)
            tile = lambda col: z_ref[:, (col + p) * LANE:(col + p + 1) * LANE].astype(F32)
            rest = c_ref[:, h:h + 1]
            qf = jnp.zeros((tm, LANE), F32)
            kf = jnp.zeros((tm, LANE), F32)
            for t in range(N_SPLIT):
                part = rest.astype(BF16).astype(F32)
                rest = rest - part
                qf = jnp.where(lane == _feature_lane(e, t), part, qf)
                qf = jnp.where(lane == _feature_lane(e, N_SPLIT + t), 1.0, qf)
                kf = jnp.where(lane == _feature_lane(e, t), 1.0, kf)
                kf = jnp.where(lane == _feature_lane(e, N_SPLIT + t), -part, kf)
            vf = jnp.where(lane == _feature_lane(e, 0), 1.0, 0.0)
            cols = slice(h * LANE, (h + 1) * LANE)
            qx_ref[:, cols] = jnp.where(own, tile(QB_COL) * SCALE, qf).astype(BF16)
            kx_ref[:, cols] = jnp.where(own, tile(KB_COL), kf).astype(BF16)
            vx_ref[:, cols] = jnp.where(own, tile(VB_COL), vf).astype(BF16)

    out = jax.ShapeDtypeStruct((s_len, FOX_HEADS * LANE), BF16)
    blk = pl.BlockSpec((tm, FOX_HEADS * LANE), lambda i: (i, 0))
    return pl.pallas_call(
        body, name="fox_prep", grid=(s_len // tm,),
        in_specs=[pl.BlockSpec((tm, N_QKV), lambda i: (i, 0)), pl.BlockSpec((tm, LANE), lambda i: (i, 0))],
        out_specs=[blk, blk, blk], out_shape=[out, out, out],
        compiler_params=_params("parallel"),
    )(zq, c)


def _comm_parts(comm):
    return ([], [], [], []) if comm is None else (comm.specs, comm.out_shape, comm.scratch, comm.arrays)


def _fox_fwd(qx, kx, vx, comm=None, t_cap=1024):
    s_len = qx.shape[0]
    t = _div(s_len, t_cap, LANE)
    nq = s_len // t
    c_specs, c_shapes, c_scratch, c_arrays = _comm_parts(comm)

    def compute(q_ref, k_ref, v_ref, o_ref, o32_ref, lse_ref):
        i = pl.program_id(1)
        qs = [q_ref[:, e * LANE:(e + 1) * LANE] for e in range(2)]

        def step(j, carry, diag):
            rows = pl.ds(pl.multiple_of(j * t, t), t)
            new = []
            for e in range(2):
                m, acc = carry[e]
                s = lax.dot_general(qs[e], k_ref[rows, e * LANE:(e + 1) * LANE], (((1,), (1,)), ((), ())),
                                    preferred_element_type=F32)
                if diag:
                    s = jnp.where(_causal(t), s, NEG_INF)
                mn = jnp.maximum(m, jnp.max(s, axis=1, keepdims=True))
                pe = jnp.exp(s - mn)
                p_hi = pe.astype(BF16)
                p_lo = (pe - p_hi.astype(F32)).astype(BF16)
                vs = v_ref[rows, e * LANE:(e + 1) * LANE]
                acc = (acc * jnp.exp(m - mn) + jnp.dot(p_hi, vs, preferred_element_type=F32)
                       + jnp.dot(p_lo, vs, preferred_element_type=F32))
                new.append((mn, acc))
            return tuple(new)

        init = (jnp.full((t, 1), NEG_INF, F32), jnp.zeros((t, LANE), F32))
        carry = lax.fori_loop(0, i, lambda j, c: step(j, c, False), (init, init))
        carry = step(i, carry, True)
        outs, lses = [], []
        for e in range(2):
            m, acc = carry[e]
            l = acc[:, _feature_lane(e, 0):_feature_lane(e, 0) + 1]
            outs.append(acc / l)
            lses.append(m + jnp.log(l))
        out = jnp.where(_own_half(1), outs[1], outs[0])
        o_ref[...] = out.astype(BF16)
        o32_ref[...] = out
        lse_ref[...] = jnp.where(_own_half(1), lses[1], lses[0])

    body = _with_comm(comm, 3, 3, lambda: (pl.program_id(0) == 0) & (pl.program_id(1) == 0),
                      lambda: (pl.program_id(0) == N_PAIR - 1) & (pl.program_id(1) == nq - 1), compute)
    pair = pl.BlockSpec((s_len, 2 * LANE), lambda p, i: (0, p))
    tile = pl.BlockSpec((t, LANE), lambda p, i: (i, p))
    outs = pl.pallas_call(
        body, name="fox_fwd" if comm is None else "fox_fwd_comm", grid=(N_PAIR, nq),
        in_specs=[pl.BlockSpec((t, 2 * LANE), lambda p, i: (i, p)), pair, pair] + c_specs,
        out_specs=[tile, tile, tile] + c_specs,
        out_shape=[jax.ShapeDtypeStruct((s_len, FOX_W), BF16), jax.ShapeDtypeStruct((s_len, FOX_W), F32),
                   jax.ShapeDtypeStruct((s_len, FOX_W), F32)] + c_shapes,
        scratch_shapes=c_scratch,
        compiler_params=_params("arbitrary", "arbitrary"),
    )(qx, kx, vx, *c_arrays)
    return outs[0], outs[1], outs[2], outs[3:]


def _fox_stats(o, do, lse, tm=256):
    s_len = o.shape[0]
    tm = _div(s_len, tm, LANE)

    def body(o_ref, do_ref, lse_ref, dox_ref, st_ref):
        lane = lax.broadcasted_iota(jnp.int32, (1, LANE), 1)
        for p in range(N_PAIR):
            cols = slice(p * LANE, (p + 1) * LANE)
            dout = do_ref[:, cols]
            prod = o_ref[:, cols] * dout.astype(F32)
            lse = lse_ref[:, cols]
            st = jnp.zeros((tm, LANE), F32)
            for e in range(2):
                h = 2 * p + e
                dox_ref[:, h * LANE:(h + 1) * LANE] = jnp.where(_own_half(e), dout, jnp.zeros_like(dout))
                st = jnp.where(lane == e, lse[:, e * HEAD_DIM:e * HEAD_DIM + 1], st)
                delta = jnp.sum(jnp.where(_own_half(e), prod, 0.0), axis=1, keepdims=True)
                st = jnp.where(lane == 2 + e, delta, st)
            st_ref[p] = st.T[:8, :]

    row = pl.BlockSpec((tm, FOX_W), lambda i: (i, 0))
    return pl.pallas_call(
        body, name="fox_stats", grid=(s_len // tm,), in_specs=[row, row, row],
        out_specs=[pl.BlockSpec((tm, FOX_HEADS * LANE), lambda i: (i, 0)),
                   pl.BlockSpec((N_PAIR, 8, tm), lambda i: (0, 0, i))],
        out_shape=[jax.ShapeDtypeStruct((s_len, FOX_HEADS * LANE), BF16),
                   jax.ShapeDtypeStruct((N_PAIR, 8, s_len), F32)],
        compiler_params=_params("parallel"),
    )(o, do, lse)


def _fox_bwd(qx, kx, vx, dox, stats, comm=None, t_cap=512):
    s_len = qx.shape[0]
    t = _div(s_len, t_cap, LANE)
    n = s_len // t
    c_specs, c_shapes, c_scratch, c_arrays = _comm_parts(comm)

    def compute(q_ref, do_ref, st_ref, k_ref, v_ref, dq_ref, dk_ref, dv_ref, dc_ref):
        j = pl.program_id(1)
        lane = lax.broadcasted_iota(jnp.int32, (1, LANE), 1)

        @pl.when(j == 0)
        def _():
            dq_ref[...] = jnp.zeros_like(dq_ref)

        ks = [k_ref[:, e * LANE:(e + 1) * LANE] for e in range(2)]
        vs = [v_ref[:, e * LANE:(e + 1) * LANE] for e in range(2)]

        def step(i, carry, diag):
            rows = pl.ds(pl.multiple_of(i * t, t), t)
            new = []
            dq = jnp.zeros((t, LANE), F32)
            for e in range(2):
                dk, dv, dc = carry[e]
                q = q_ref[rows, e * LANE:(e + 1) * LANE]
                dout = do_ref[rows, e * LANE:(e + 1) * LANE]
                s_t = lax.dot_general(ks[e], q, (((1,), (1,)), ((), ())), preferred_element_type=F32)
                if diag:
                    s_t = jnp.where(_causal(t, keys_first=True), s_t, NEG_INF)
                p_t = jnp.exp(s_t - st_ref[0, e:e + 1, rows])
                dp_t = lax.dot_general(vs[e], dout, (((1,), (1,)), ((), ())), preferred_element_type=F32)
                ds_f = p_t * (dp_t - st_ref[0, 2 + e:3 + e, rows])
                ds_t = ds_f.astype(BF16)
                dc = dc + jnp.sum(ds_f, axis=1, keepdims=True)
                dv = dv + jnp.dot(p_t.astype(BF16), dout, preferred_element_type=F32)
                dk = dk + jnp.dot(ds_t, q, preferred_element_type=F32)
                dq_e = lax.dot_general(ds_t, ks[e], (((0,), (0,)), ((), ())), preferred_element_type=F32)
                dq = dq + jnp.where(_own_half(e), dq_e, 0.0)
                new.append((dk, dv, dc))
            dq_ref[rows, :] += dq * SCALE
            return tuple(new)

        zero = jnp.zeros((t, LANE), F32)
        init = (zero, zero, jnp.zeros((t, 1), F32))
        carry = step(j, (init, init), True)
        (dk0, dv0, dc0), (dk1, dv1, dc1) = lax.fori_loop(j + 1, n, lambda i, c: step(i, c, False), carry)
        dk_ref[...] = jnp.where(_own_half(1), dk1, dk0).astype(BF16)
        dv_ref[...] = jnp.where(_own_half(1), dv1, dv0).astype(BF16)
        dc_ref[...] = jnp.where(lane == 0, -dc0, jnp.where(lane == 1, -dc1, 0.0))

    body = _with_comm(comm, 5, 4, lambda: (pl.program_id(0) == 0) & (pl.program_id(1) == 0),
                      lambda: (pl.program_id(0) == N_PAIR - 1) & (pl.program_id(1) == n - 1), compute)
    pair = pl.BlockSpec((s_len, 2 * LANE), lambda p, j: (0, p))
    blk = pl.BlockSpec((t, 2 * LANE), lambda p, j: (j, p))
    tile = pl.BlockSpec((t, LANE), lambda p, j: (j, p))
    outs = pl.pallas_call(
        body, name="fox_bwd" if comm is None else "fox_bwd_comm", grid=(N_PAIR, n),
        in_specs=[pair, pair, pl.BlockSpec((1, 8, s_len), lambda p, j: (p, 0, 0)), blk, blk] + c_specs,
        out_specs=[pl.BlockSpec((s_len, LANE), lambda p, j: (0, p)), tile, tile, tile] + c_specs,
        out_shape=[jax.ShapeDtypeStruct((s_len, FOX_W), F32), jax.ShapeDtypeStruct((s_len, FOX_W), BF16),
                   jax.ShapeDtypeStruct((s_len, FOX_W), BF16), jax.ShapeDtypeStruct((s_len, FOX_W), F32)] + c_shapes,
        scratch_shapes=c_scratch,
        compiler_params=_params("arbitrary", "arbitrary"),
    )(qx, dox, stats, kx, vx, *c_arrays)
    return outs[0], outs[1], outs[2], outs[3], outs[4:]


def _mixer_out(attn_a, attn_b, zg, h, wpa, wpb, wout, g, b, tm=256):
    m = h.shape[0]
    tm = _div(m, tm, 8)

    def body(a_ref, b_ref, ga_ref, gb_ref, h_ref, wpa_ref, wpb_ref, wout_ref, g_ref, bb_ref,
             h1_ref, u_ref, mg_ref, ya_ref, yb_ref):
        ya = jnp.dot(a_ref[...], wpa_ref[...], preferred_element_type=F32)
        yb = jnp.dot(b_ref[...], wpb_ref[...], preferred_element_type=F32)
        merged = (jax.nn.sigmoid(ga_ref[...]) * ya + jax.nn.sigmoid(gb_ref[...]) * yb).astype(BF16)
        u = ALPHA * h_ref[...] + jnp.dot(merged, wout_ref[...], preferred_element_type=F32)
        u_ref[...] = u
        h1_ref[...] = _ln(u, g_ref[...], bb_ref[...])
        mg_ref[...] = merged
        ya_ref[...] = ya.astype(BF16)
        yb_ref[...] = yb.astype(BF16)

    row = pl.BlockSpec((tm, D_MODEL), lambda i: (i, 0))
    att = pl.BlockSpec((tm, SWA_Q), lambda i: (i, 0))
    vec = pl.BlockSpec((1, D_MODEL), lambda i: (0, 0))
    wsm = pl.BlockSpec((SWA_Q, D_MODEL), lambda i: (0, 0))
    return pl.pallas_call(
        body, name="mixer_out", grid=(m // tm,),
        in_specs=[att, att, row, pl.BlockSpec((tm, D_MODEL), lambda i: (i, 1)), row, wsm, wsm,
                  pl.BlockSpec((D_MODEL, D_MODEL), lambda i: (0, 0)), vec, vec],
        out_specs=[row, row, row, row, row],
        out_shape=[jax.ShapeDtypeStruct((m, D_MODEL), F32), jax.ShapeDtypeStruct((m, D_MODEL), F32),
                   jax.ShapeDtypeStruct((m, D_MODEL), BF16), jax.ShapeDtypeStruct((m, D_MODEL), BF16),
                   jax.ShapeDtypeStruct((m, D_MODEL), BF16)],
        compiler_params=_params("parallel"),
    )(attn_a, attn_b, zg, zg, h, wpa, wpb, wout, g, b)


def _mixer_bwd(dh1, u1, g, wout, ya, yb, zg, wpa, wpb, tm=256):
    m = dh1.shape[0]
    tm = _div(m, tm, 8)

    def body(dh_ref, u_ref, g_ref, wout_ref, ya_ref, yb_ref, ga_ref, gb_ref, wpa_ref, wpb_ref,
             du_ref, dg_ref, db_ref, dya_ref, dyb_ref, dga_ref, dgb_ref, da_ref, dbb_ref):
        @pl.when(pl.program_id(0) == 0)
        def _():
            dg_ref[...] = jnp.zeros_like(dg_ref)
            db_ref[...] = jnp.zeros_like(db_ref)

        du, dg, db = _ln_bwd_block(dh_ref[...], u_ref[...], g_ref[...])
        du_ref[...] = du
        dg_ref[...] += dg
        db_ref[...] += db
        dm = lax.dot_general(du.astype(BF16), wout_ref[...], (((1,), (1,)), ((), ())), preferred_element_type=F32)
        for y_ref, gate_ref, w_ref, dy_ref, dgate_ref, dattn_ref in (
                (ya_ref, ga_ref, wpa_ref, dya_ref, dga_ref, da_ref), (yb_ref, gb_ref, wpb_ref, dyb_ref, dgb_ref, dbb_ref)):
            sg = jax.nn.sigmoid(gate_ref[...])
            dy = (dm * sg).astype(BF16)
            dy_ref[...] = dy
            dgate_ref[...] = (dm * y_ref[...].astype(F32) * sg * (1.0 - sg)).astype(BF16)
            dattn_ref[...] = lax.dot_general(dy, w_ref[...], (((1,), (1,)), ((), ())),
                                             preferred_element_type=F32).astype(BF16)

    row = pl.BlockSpec((tm, D_MODEL), lambda i: (i, 0))
    att = pl.BlockSpec((tm, SWA_Q), lambda i: (i, 0))
    vec = pl.BlockSpec((1, D_MODEL), lambda i: (0, 0))
    wsm = pl.BlockSpec((SWA_Q, D_MODEL), lambda i: (0, 0))
    wide = jax.ShapeDtypeStruct((m, D_MODEL), BF16)
    narrow = jax.ShapeDtypeStruct((m, SWA_Q), BF16)
    sums = jax.ShapeDtypeStruct((1, D_MODEL), F32)
    return pl.pallas_call(
        body, name="mixer_bwd", grid=(m // tm,),
        in_specs=[row, row, vec, pl.BlockSpec((D_MODEL, D_MODEL), lambda i: (0, 0)), row, row, row,
                  pl.BlockSpec((tm, D_MODEL), lambda i: (i, 1)), wsm, wsm],
        out_specs=[row, vec, vec, row, row, row, row, att, att],
        out_shape=[jax.ShapeDtypeStruct((m, D_MODEL), F32), sums, sums, wide, wide, wide, wide, narrow, narrow],
        compiler_params=_params("arbitrary"),
    )(dh1, u1, g, wout, ya, yb, zg, zg, wpa, wpb)


def _shift_down(x, k, halo, first):
    rows = lax.broadcasted_iota(jnp.int32, (x.shape[0], 1), 0)
    y = pltpu.roll(x, k, 0)
    for r in range(k):
        fill = jnp.where(first, 0.0, halo[8 - k + r:8 - k + r + 1, :])
        y = jnp.where(rows == r, fill, y)
    return y


def _shift_up(x, k, halo, last):
    n = x.shape[0]
    rows = lax.broadcasted_iota(jnp.int32, (n, 1), 0)
    y = pltpu.roll(x, n - k, 0)
    for r in range(k):
        fill = jnp.where(last, 0.0, halo[r:r + 1, :])
        y = jnp.where(rows == n - k + r, fill, y)
    return y


def _conv_act(gate, gate_m1, gate_m2, cw, cb):
    return cb + cw[0:1, :] * gate_m2 + cw[1:2, :] * gate_m1 + cw[2:3, :] * gate


def _ffn_in(h1, wfi, tm=256):
    s_len = h1.shape[0]
    tm = _div(s_len, tm, 8)

    def body(a_ref, w_ref, o_ref):
        a = a_ref[...].astype(BF16)
        for g in range(2):
            for c in range(N_CHUNK):
                o_ref[c, g] = jnp.dot(a, w_ref[N_CHUNK * g + c], preferred_element_type=F32)

    return pl.pallas_call(
        body, name="ffn_in", grid=(s_len // tm,),
        in_specs=[pl.BlockSpec((tm, D_MODEL), lambda i: (i, 0)),
                  pl.BlockSpec((N_DEV, D_MODEL, FF_CHUNK), lambda i: (0, 0, 0))],
        out_specs=pl.BlockSpec((N_CHUNK, 2, tm, FF_CHUNK), lambda i: (0, 0, i, 0)),
        out_shape=jax.ShapeDtypeStruct((N_CHUNK, 2, s_len, FF_CHUNK), F32),
        compiler_params=_params("parallel"),
    )(h1, wfi)


def _conv_fwd(gu, cw, cb, tm=256):
    s_len = gu.shape[2]
    tm = _div(s_len, tm, 8)
    hb = tm // 8

    def body(gu_ref, gp_ref, cw_ref, cb_ref, o_ref):
        first = pl.program_id(1) == 0
        gate = gu_ref[0, 0]
        halo = gp_ref[0, 0]
        conv = _conv_act(gate, _shift_down(gate, 1, halo, first), _shift_down(gate, 2, halo, first),
                         cw_ref[0], cb_ref[0])
        o_ref[0] = (conv * jax.nn.sigmoid(conv) * gu_ref[0, 1]).astype(BF16)

    return pl.pallas_call(
        body, name="conv_fwd", grid=(N_CHUNK, s_len // tm),
        in_specs=[pl.BlockSpec((1, 2, tm, FF_CHUNK), lambda c, i: (c, 0, i, 0)),
                  pl.BlockSpec((1, 1, 8, FF_CHUNK), lambda c, i: (c, 0, jnp.maximum(i * hb - 1, 0), 0)),
                  pl.BlockSpec((1, 8, FF_CHUNK), lambda c, i: (c, 0, 0)),
                  pl.BlockSpec((1, 1, FF_CHUNK), lambda c, i: (c, 0, 0))],
        out_specs=pl.BlockSpec((1, tm, FF_CHUNK), lambda c, i: (c, i, 0)),
        out_shape=jax.ShapeDtypeStruct((N_CHUNK, s_len, FF_CHUNK), BF16),
        compiler_params=_params("parallel", "parallel"),
    )(gu, gu, cw, cb)


def _ffn_out_ln(act, wfo, res, g, b, tm=256):
    s_len = res.shape[0]
    tm = _div(s_len, tm, 8)

    def body(a_ref, w_ref, res_ref, g_ref, b_ref, u_ref, y_ref):
        u = ALPHA * res_ref[...]
        for c in range(N_CHUNK):
            u = u + jnp.dot(a_ref[c], w_ref[c], preferred_element_type=F32)
        u_ref[...] = u
        y_ref[...] = _ln(u, g_ref[...], b_ref[...])

    row = pl.BlockSpec((tm, D_MODEL), lambda i: (i, 0))
    vec = pl.BlockSpec((1, D_MODEL), lambda i: (0, 0))
    return pl.pallas_call(
        body, name="ffn_out_ln", grid=(s_len // tm,),
        in_specs=[pl.BlockSpec((N_CHUNK, tm, FF_CHUNK), lambda i: (0, i, 0)),
                  pl.BlockSpec((N_CHUNK, FF_CHUNK, D_MODEL), lambda i: (0, 0, 0)), row, vec, vec],
        out_specs=[row, row],
        out_shape=[jax.ShapeDtypeStruct((s_len, D_MODEL), F32), jax.ShapeDtypeStruct((s_len, D_MODEL), F32)],
        compiler_params=_params("parallel"),
    )(act, wfo, res, g, b)


def _ffn_out_bwd(dh2, u2, g, wfo, tm=256):
    s_len = dh2.shape[0]
    tm = _div(s_len, tm, 8)

    def body(dh_ref, u_ref, g_ref, w_ref, du_ref, dg_ref, db_ref, o_ref):
        @pl.when(pl.program_id(0) == 0)
        def _():
            dg_ref[...] = jnp.zeros_like(dg_ref)
            db_ref[...] = jnp.zeros_like(db_ref)

        du, dg, db = _ln_bwd_block(dh_ref[...], u_ref[...], g_ref[...])
        du_ref[...] = du
        dg_ref[...] += dg
        db_ref[...] += db
        du_b = du.astype(BF16)
        for c in range(N_CHUNK):
            o_ref[c] = lax.dot_general(du_b, w_ref[c], (((1,), (1,)), ((), ())), preferred_element_type=F32)

    row = pl.BlockSpec((tm, D_MODEL), lambda i: (i, 0))
    vec = pl.BlockSpec((1, D_MODEL), lambda i: (0, 0))
    sums = jax.ShapeDtypeStruct((1, D_MODEL), F32)
    return pl.pallas_call(
        body, name="ffn_out_bwd", grid=(s_len // tm,),
        in_specs=[row, row, vec, pl.BlockSpec((N_CHUNK, FF_CHUNK, D_MODEL), lambda i: (0, 0, 0))],
        out_specs=[row, vec, vec, pl.BlockSpec((N_CHUNK, tm, FF_CHUNK), lambda i: (0, i, 0))],
        out_shape=[jax.ShapeDtypeStruct((s_len, D_MODEL), F32), sums, sums,
                   jax.ShapeDtypeStruct((N_CHUNK, s_len, FF_CHUNK), F32)],
        compiler_params=_params("arbitrary"),
    )(dh2, u2, g, wfo)


def _g_w_ffn_out(act, du, tm=2048):
    s_len = du.shape[0]
    tm = _div(s_len, tm, 8)
    steps = s_len // tm

    def body(a_ref, g_ref, o_ref, acc_ref):
        s = pl.program_id(1)

        @pl.when(s == 0)
        def _():
            acc_ref[...] = jnp.zeros_like(acc_ref)

        acc_ref[...] += lax.dot_general(a_ref[0], g_ref[...].astype(BF16), (((0,), (0,)), ((), ())),
                                        preferred_element_type=F32)

        @pl.when(s == steps - 1)
        def _():
            o_ref[0] = acc_ref[...]

    return pl.pallas_call(
        body, name="g_w_ffn_out", grid=(N_CHUNK, steps),
        in_specs=[pl.BlockSpec((1, tm, FF_CHUNK), lambda c, s: (c, s, 0)),
                  pl.BlockSpec((tm, D_MODEL), lambda c, s: (s, 0))],
        out_specs=pl.BlockSpec((1, FF_CHUNK, D_MODEL), lambda c, s: (c, 0, 0)),
        out_shape=jax.ShapeDtypeStruct((N_CHUNK, FF_CHUNK, D_MODEL), F32),
        scratch_shapes=[pltpu.VMEM((FF_CHUNK, D_MODEL), F32)],
        compiler_params=_params("parallel", "arbitrary"),
    )(act, du)


def _d_h1(dgu, wfi, res, tm=256):
    s_len = res.shape[0]
    tm = _div(s_len, tm, 8)

    def body(a_ref, w_ref, res_ref, o_ref):
        acc = ALPHA * res_ref[...]
        for g in range(2):
            for c in range(N_CHUNK):
                acc = acc + lax.dot_general(a_ref[c, g], w_ref[N_CHUNK * g + c], (((1,), (1,)), ((), ())),
                                            preferred_element_type=F32)
        o_ref[...] = acc

    row = pl.BlockSpec((tm, D_MODEL), lambda i: (i, 0))
    return pl.pallas_call(
        body, name="d_h1", grid=(s_len // tm,),
        in_specs=[pl.BlockSpec((N_CHUNK, 2, tm, FF_CHUNK), lambda i: (0, 0, i, 0)),
                  pl.BlockSpec((N_DEV, D_MODEL, FF_CHUNK), lambda i: (0, 0, 0)), row],
        out_specs=row, out_shape=jax.ShapeDtypeStruct((s_len, D_MODEL), F32),
        compiler_params=_params("parallel"),
    )(dgu, wfi, res)


def _g_w_ffn_in(h1, dgu, tm=2048):
    s_len = h1.shape[0]
    tm = _div(s_len, tm, 8)
    steps = s_len // tm

    def body(a_ref, g_ref, o_ref, acc_ref):
        s = pl.program_id(1)

        @pl.when(s == 0)
        def _():
            acc_ref[...] = jnp.zeros_like(acc_ref)

        acc_ref[...] += lax.dot_general(a_ref[...].astype(BF16), g_ref[0, 0], (((0,), (0,)), ((), ())),
                                        preferred_element_type=F32)

        @pl.when(s == steps - 1)
        def _():
            o_ref[0] = acc_ref[...]

    return pl.pallas_call(
        body, name="g_w_ffn_in", grid=(N_DEV, steps),
        in_specs=[pl.BlockSpec((tm, D_MODEL), lambda d, s: (s, 0)),
                  pl.BlockSpec((1, 1, tm, FF_CHUNK), lambda d, s: (d % N_CHUNK, d // N_CHUNK, s, 0))],
        out_specs=pl.BlockSpec((1, D_MODEL, FF_CHUNK), lambda d, s: (d, 0, 0)),
        out_shape=jax.ShapeDtypeStruct((N_DEV, D_MODEL, FF_CHUNK), F32),
        scratch_shapes=[pltpu.VMEM((D_MODEL, FF_CHUNK), F32)],
        compiler_params=_params("parallel", "arbitrary"),
    )(h1, dgu)


def _conv_bwd(gu, dact, cw, cb, tm=256):
    s_len = gu.shape[2]
    tm = _div(s_len, tm, 8)
    nrow = s_len // tm
    hb = tm // 8

    def dconv_of(conv, up, da):
        sg = jax.nn.sigmoid(conv)
        return da * up * (sg * (1.0 + conv * (1.0 - sg)))

    def body(gu_ref, gp_ref, gun_ref, da_ref, dan_ref, cw_ref, cb_ref, dgu_ref, dcw_ref):
        i = pl.program_id(1)
        first = i == 0
        last = i == nrow - 1
        cw = cw_ref[0]
        cb = cb_ref[0]
        gate = gu_ref[0, 0]
        halo = gp_ref[0, 0]
        g_m1 = _shift_down(gate, 1, halo, first)
        g_m2 = _shift_down(gate, 2, halo, first)
        conv = _conv_act(gate, g_m1, g_m2, cw, cb)
        da = da_ref[0]
        sg = jax.nn.sigmoid(conv)
        dgu_ref[0, 1] = (da * conv * sg).astype(BF16)
        dconv = dconv_of(conv, gu_ref[0, 1], da)
        gate_n = gun_ref[0, 0]
        tail = gate[tm - 8:, :]
        conv_n = _conv_act(gate_n, _shift_down(gate_n, 1, tail, False), _shift_down(gate_n, 2, tail, False), cw, cb)
        dconv_n = dconv_of(conv_n, gun_ref[0, 1], dan_ref[0])
        dgate = (cw[2:3, :] * dconv + cw[1:2, :] * _shift_up(dconv, 1, dconv_n, last)
                 + cw[0:1, :] * _shift_up(dconv, 2, dconv_n, last))
        dgu_ref[0, 0] = dgate.astype(BF16)

        @pl.when(first)
        def _():
            dcw_ref[...] = jnp.zeros_like(dcw_ref)

        row = lax.broadcasted_iota(jnp.int32, (8, 1), 0)
        part = jnp.zeros((8, FF_CHUNK), F32)
        for r, term in enumerate((dconv * g_m2, dconv * g_m1, dconv * gate, dconv)):
            part = jnp.where(row == r, jnp.sum(term, axis=0, keepdims=True), part)
        dcw_ref[0] += part

    nxt = lambda i: jnp.minimum((i + 1) * hb, s_len // 8 - 1)
    main = pl.BlockSpec((1, 2, tm, FF_CHUNK), lambda c, i: (c, 0, i, 0))
    return pl.pallas_call(
        body, name="conv_bwd", grid=(N_CHUNK, nrow),
        in_specs=[main,
                  pl.BlockSpec((1, 1, 8, FF_CHUNK), lambda c, i: (c, 0, jnp.maximum(i * hb - 1, 0), 0)),
                  pl.BlockSpec((1, 2, 8, FF_CHUNK), lambda c, i: (c, 0, nxt(i), 0)),
                  pl.BlockSpec((1, tm, FF_CHUNK), lambda c, i: (c, i, 0)),
                  pl.BlockSpec((1, 8, FF_CHUNK), lambda c, i: (c, nxt(i), 0)),
                  pl.BlockSpec((1, 8, FF_CHUNK), lambda c, i: (c, 0, 0)),
                  pl.BlockSpec((1, 1, FF_CHUNK), lambda c, i: (c, 0, 0))],
        out_specs=[main, pl.BlockSpec((1, 8, FF_CHUNK), lambda c, i: (c, 0, 0))],
        out_shape=[jax.ShapeDtypeStruct((N_CHUNK, 2, s_len, FF_CHUNK), BF16),
                   jax.ShapeDtypeStruct((N_CHUNK, 8, FF_CHUNK), F32)],
        compiler_params=_params("parallel", "arbitrary"),
    )(gu, gu, gu, dact, dact, cw, cb)


def _loss_head(y, target, tm=256):
    m, d = y.shape
    tm = _div(m, tm, 8)

    def body(y_ref, t_ref, dy_ref, loss_ref):
        @pl.when(pl.program_id(0) == 0)
        def _():
            loss_ref[...] = jnp.zeros_like(loss_ref)

        err = y_ref[...] - t_ref[...]
        dy_ref[...] = err / d
        loss_ref[...] += 0.5 * jnp.sum(jnp.sum(err * err, axis=1, keepdims=True) / d, axis=0, keepdims=True)

    row = pl.BlockSpec((tm, d), lambda i: (i, 0))
    return pl.pallas_call(
        body, name="loss_head", grid=(m // tm,), in_specs=[row, row],
        out_specs=[row, pl.BlockSpec((8, LANE), lambda i: (0, 0))],
        out_shape=[jax.ShapeDtypeStruct((m, d), F32), jax.ShapeDtypeStruct((8, LANE), F32)],
        compiler_params=_params("arbitrary"),
    )(y, target)


def _sum_devices(r_ref):
    acc = r_ref[0].astype(F32)
    for d in range(1, N_DEV):
        acc = acc + r_ref[d].astype(F32)
    return acc


def _sum8(recv):
    rows = recv.shape[1]
    tr = _div(rows, ROW_BLOCK, 8)

    def body(r_ref, o_ref):
        o_ref[...] = _sum_devices(r_ref)

    return pl.pallas_call(
        body, name="sum8", grid=(rows // tr,),
        in_specs=[pl.BlockSpec((N_DEV, tr, LANE), lambda i: (0, i, 0))],
        out_specs=pl.BlockSpec((tr, LANE), lambda i: (i, 0)),
        out_shape=jax.ShapeDtypeStruct((rows, LANE), F32),
        compiler_params=_params("parallel"),
    )(recv)


def _adamw_math(w, g, m, v):
    m = ADAM_B1 * m + (1.0 - ADAM_B1) * g
    v = ADAM_B2 * v + (1.0 - ADAM_B2) * (g * g)
    m_hat = m / (1.0 - ADAM_B1 ** ADAM_STEP)
    v_hat = v / (1.0 - ADAM_B2 ** ADAM_STEP)
    return -ADAM_LR * (m_hat / (jnp.sqrt(v_hat) + ADAM_EPS) + ADAM_WD * w), m, v


def _adamw_rows(w, g, m, v, name):
    rows = w.shape[0]
    tr = _div(rows, ROW_BLOCK, 8)

    def body(w_ref, g_ref, m_ref, v_ref, d_ref, mo_ref, vo_ref):
        d_ref[...], mo_ref[...], vo_ref[...] = _adamw_math(w_ref[...], g_ref[...], m_ref[...], v_ref[...])

    blk = pl.BlockSpec((tr, LANE), lambda i: (i, 0))
    out = jax.ShapeDtypeStruct((rows, LANE), F32)
    return pl.pallas_call(
        body, name=name, grid=(rows // tr,), in_specs=[blk, blk, blk, blk], out_specs=[blk, blk, blk],
        out_shape=[out, out, out], compiler_params=_params("parallel"),
    )(w, g, m, v)


def _adamw_shard(recv, w, m, v, layer, prev, name):
    _, k, n = recv.shape
    tk = _div(k, 128, 16)

    def body(r_ref, w_ref, m_ref, v_ref, *rest):
        g_ref, d_ref, mo_ref, vo_ref = rest[-4:]
        g = _sum_devices(r_ref)
        g_ref[0] = g
        d_ref[0], mo_ref[0], vo_ref[0] = _adamw_math(w_ref[0], g, m_ref[0], v_ref[0])

    blk = pl.BlockSpec((1, tk, n), lambda i: (layer, i, 0))
    out = jax.ShapeDtypeStruct((DEPTH, k, n), F32)
    carried = [] if prev is None else list(prev)
    return pl.pallas_call(
        body, name=name, grid=(k // tk,),
        in_specs=[pl.BlockSpec((N_DEV, tk, n), lambda i: (0, i, 0)), blk, blk, blk]
        + [pl.BlockSpec(memory_space=pl.ANY)] * len(carried),
        out_specs=[blk, blk, blk, blk], out_shape=[out, out, out, out],
        input_output_aliases={4 + j: j for j in range(len(carried))},
        compiler_params=_params("parallel"),
    )(recv, w, m, v, *carried)


def _to_rows(flat, rows):
    flat = flat.reshape(-1)
    return jnp.pad(flat, (0, rows * LANE - flat.shape[0])).reshape(rows, LANE)


def _pad_cols_z(a):
    f0 = N_QKV
    g0 = N_QKV + FOX_HEADS
    pad = jnp.zeros(a.shape[:-1] + (F_PAD - FOX_HEADS,), a.dtype)
    return jnp.concatenate([a[..., :f0], a[..., g0:], a[..., f0:g0], pad], axis=-1)


def _unpad_cols_z(a):
    f0 = N_QKV + N_GATE
    return jnp.concatenate([a[..., :N_QKV], a[..., f0:f0 + FOX_HEADS], a[..., N_QKV:f0]], axis=-1)


def _shards_to_cols(g):
    _, k, n = g.shape
    return g.transpose(1, 0, 2).reshape(k, N_DEV * n)


def _cols_to_shards(full):
    k, n = full.shape
    return full.reshape(k, N_DEV, n // N_DEV).transpose(1, 0, 2)


def _layer_fwd(h, w, p, comm=None, late=None):
    zq, zg = _z_proj(h, w["w_in_p"], p["b_in_p"])
    qx, kx, vx = _fox_prep(zq, _cumsum_logf(zg))
    attn_a, lse_a = _swa_fwd(zq, p["sinks"])
    attn_b, attn_b32, lse_b, arrived = _fox_fwd(qx, kx, vx, comm)
    if late is not None:
        w, p = late(w, p, arrived)
    h1, u1, merged, ya, yb = _mixer_out(attn_a, attn_b, zg, h, w["w_proj_a"], w["w_proj_b"], w["w_out"],
                                        p["ln_mix_g"], p["ln_mix_b"])
    gu = _ffn_in(h1, w["w_ffn_in"])
    act = _conv_fwd(gu, p["conv_w"], p["conv_b"])
    u2, h2 = _ffn_out_ln(act, w["w_ffn_out"], h1, p["ln_ffn_g"], p["ln_ffn_b"])
    saved = dict(h=h, zq=zq, zg=zg, qx=qx, kx=kx, vx=vx, attn_a=attn_a, lse_a=lse_a, attn_b=attn_b,
                 attn_b32=attn_b32, lse_b=lse_b, h1=h1, u1=u1, merged=merged, ya=ya, yb=yb, gu=gu, act=act, u2=u2)
    return h2, saved, w, p


def _layer_bwd(dh2, sv, w, p, make_comm=None):
    s_len = dh2.shape[0]
    du2, d_ffn_g, d_ffn_b, dact = _ffn_out_bwd(dh2, sv["u2"], p["ln_ffn_g"], w["w_ffn_out"])
    g_ffn_out = _g_w_ffn_out(sv["act"], du2)
    dgu, dcw = _conv_bwd(sv["gu"], dact, p["conv_w"], p["conv_b"])
    dcw = dcw.transpose(1, 0, 2).reshape(8, D_FF)
    dh1 = _d_h1(dgu, w["w_ffn_in"], du2)
    g_ffn_in = _g_w_ffn_in(sv["h1"], dgu)
    du1, d_mix_g, d_mix_b, dya, dyb, dga, dgb, dattn_a, dattn_b = _mixer_bwd(
        dh1, sv["u1"], p["ln_mix_g"], w["w_out"], sv["ya"], sv["yb"], sv["zg"], w["w_proj_a"], w["w_proj_b"])
    g_out = _linear_tn(sv["merged"], du1, name="g_w_out", tn=1024)
    g_proj_a = _linear_tn(sv["attn_a"], dya, name="g_w_proj_a", tk=512, tn=1024)
    g_proj_b = _linear_tn(sv["attn_b"], dyb, name="g_w_proj_b", tk=512, tn=1024)
    dq_a, dk_a, dv_a, dsinks = _swa_bwd(sv["zq"], p["sinks"], sv["attn_a"], dattn_a, sv["lse_a"])
    big = dict(w_proj_a=_cols_to_shards(g_proj_a), w_proj_b=_cols_to_shards(g_proj_b),
               w_out=g_out.reshape(N_DEV, D_MODEL // N_DEV, D_MODEL), w_ffn_in=g_ffn_in,
               w_ffn_out=g_ffn_out.reshape(N_DEV, D_FF // N_DEV, D_MODEL))
    dox, stats = _fox_stats(sv["attn_b32"], dattn_b, sv["lse_b"])
    dq_b, dk_b, dv_b, dcc, arrived = _fox_bwd(sv["qx"], sv["kx"], sv["vx"], dox, stats,
                                              None if make_comm is None else make_comm(big))
    dc = jnp.pad(dcc.reshape(s_len, N_PAIR, LANE)[:, :, :2].reshape(s_len, FOX_HEADS), ((0, 0), (0, LANE - FOX_HEADS)))
    df = _forget_bwd(dc, sv["zg"])
    dz = jnp.concatenate([dq_a, dk_a.astype(BF16), dv_a.astype(BF16), dq_b.astype(BF16), dk_b, dv_b, dga, dgb, df,
                          jnp.zeros((s_len, F_PAD - LANE), BF16)], axis=1)
    dh = _linear(dz, w["w_in_p"], trans_b=True, res=du1, res_scale=ALPHA, name="d_h", tn=D_MODEL)
    g_in = _unpad_cols_z(_linear_tn(sv["h"], dz, name="g_w_in", tn=768))
    g_b_in = _unpad_cols_z(_colsum(dz, name="g_b_in"))
    big["w_in"] = _cols_to_shards(g_in)
    small = dict(ln_mix_g=d_mix_g, ln_mix_b=d_mix_b, b_in=g_b_in, attn_sinks=dsinks[:, :SWA_HEADS],
                 ln_ffn_g=d_ffn_g, ln_ffn_b=d_ffn_b, conv_w=dcw[:3], conv_b=dcw[3:4])
    return dh, big, small, arrived


def _w_in_layouts(w_in):
    return dict(w_in_p=_pad_cols_z(_shards_to_cols(w_in)))


def _other_layouts(w_proj_a, w_proj_b, w_out, w_ffn_in, w_ffn_out):
    return dict(w_proj_a=_shards_to_cols(w_proj_a), w_proj_b=_shards_to_cols(w_proj_b),
                w_out=w_out.reshape(D_MODEL, D_MODEL), w_ffn_in=w_ffn_in,
                w_ffn_out=w_ffn_out.reshape(N_CHUNK, FF_CHUNK, D_MODEL))


def _layer_params(r):
    return dict(
        b_in_p=_pad_cols_z(r["b_in"].reshape(1, N_IN)),
        sinks=jnp.pad(r["attn_sinks"].reshape(1, SWA_HEADS), ((0, 0), (0, LANE - SWA_HEADS))),
        ln_mix_g=r["ln_mix_g"].reshape(1, D_MODEL), ln_mix_b=r["ln_mix_b"].reshape(1, D_MODEL),
        ln_ffn_g=r["ln_ffn_g"].reshape(1, D_MODEL), ln_ffn_b=r["ln_ffn_b"].reshape(1, D_MODEL),
        conv_b=r["conv_b"].reshape(N_CHUNK, 1, FF_CHUNK))


def _conv_w_layout(conv_w):
    return jnp.pad(conv_w, ((0, 5), (0, 0))).reshape(8, N_CHUNK, FF_CHUNK).transpose(1, 0, 2)


def kernel(x, ln_mix_g, ln_mix_b, w_in, b_in, attn_sinks, w_proj_a, w_proj_b, w_out, ln_ffn_g, ln_ffn_b, w_ffn_in, conv_w, conv_b, w_ffn_out, loss_target, m_ln_mix_g, m_ln_mix_b, m_w_in, m_b_in, m_attn_sinks, m_w_proj_a, m_w_proj_b, m_w_out, m_ln_ffn_g, m_ln_ffn_b, m_w_ffn_in, m_conv_w, m_conv_b, m_w_ffn_out, v_ln_mix_g, v_ln_mix_b, v_w_in, v_b_in, v_attn_sinks, v_w_proj_a, v_w_proj_b, v_w_out, v_ln_ffn_g, v_ln_ffn_b, v_w_ffn_in, v_conv_w, v_conv_b, v_w_ffn_out):
    wts = dict(ln_mix_g=ln_mix_g, ln_mix_b=ln_mix_b, w_in=w_in, b_in=b_in, attn_sinks=attn_sinks, w_proj_a=w_proj_a,
               w_proj_b=w_proj_b, w_out=w_out, ln_ffn_g=ln_ffn_g, ln_ffn_b=ln_ffn_b, w_ffn_in=w_ffn_in,
               conv_w=conv_w, conv_b=conv_b, w_ffn_out=w_ffn_out)
    mom = dict(ln_mix_g=m_ln_mix_g, ln_mix_b=m_ln_mix_b, w_in=m_w_in, b_in=m_b_in, attn_sinks=m_attn_sinks,
               w_proj_a=m_w_proj_a, w_proj_b=m_w_proj_b, w_out=m_w_out, ln_ffn_g=m_ln_ffn_g, ln_ffn_b=m_ln_ffn_b,
               w_ffn_in=m_w_ffn_in, conv_w=m_conv_w, conv_b=m_conv_b, w_ffn_out=m_w_ffn_out)
    vel = dict(ln_mix_g=v_ln_mix_g, ln_mix_b=v_ln_mix_b, w_in=v_w_in, b_in=v_b_in, attn_sinks=v_attn_sinks,
               w_proj_a=v_w_proj_a, w_proj_b=v_w_proj_b, w_out=v_w_out, ln_ffn_g=v_ln_ffn_g, ln_ffn_b=v_ln_ffn_b,
               w_ffn_in=v_w_ffn_in, conv_w=v_conv_w, conv_b=v_conv_b, w_ffn_out=v_w_ffn_out)
    names = list(wts)
    big_names = [n for n, _, _ in BIG]
    small_names = [n for n, _ in SMALL]
    me = 4 * lax.axis_index("x") + 2 * lax.axis_index("y") + lax.axis_index("c")
    cw_shard = D_FF // N_DEV

    wb = {n: wts[n].astype(BF16) for n in big_names}
    ps = [_layer_params(dict(b_in=b_in[l], attn_sinks=attn_sinks[l], ln_mix_g=ln_mix_g[l], ln_mix_b=ln_mix_b[l],
                             ln_ffn_g=ln_ffn_g[l], ln_ffn_b=ln_ffn_b[l], conv_b=conv_b[l])) for l in range(DEPTH)]
    w_in_0, = _exchange([(wb["w_in"][0], True)], "gather_w_in_0")
    others = big_names[1:]
    gather_rest = _Comm([(wb[n][0], True) for n in others] + [(wb[n][1], True) for n in big_names] + [(conv_w, True)])
    next_layer = {}

    def late(w, p, arrived):
        conv_full = arrived[-1].transpose(1, 2, 0, 3).reshape(DEPTH, 3, D_FF)
        layer_1 = arrived[len(others):-1]
        next_layer["w"] = dict(_w_in_layouts(layer_1[0]), **_other_layouts(*layer_1[1:]))
        next_layer["p"] = dict(ps[1], conv_w=_conv_w_layout(conv_full[1]))
        return dict(w, **_other_layouts(*arrived[:len(others)])), dict(p, conv_w=_conv_w_layout(conv_full[0]))

    saved, ws = [None] * DEPTH, [None] * DEPTH
    h, saved[0], ws[0], ps[0] = _layer_fwd(x[0], _w_in_layouts(w_in_0), ps[0], gather_rest, late)
    h, saved[1], ws[1], ps[1] = _layer_fwd(h, next_layer["w"], next_layer["p"])
    dh, loss_part = _loss_head(h, loss_target[0])

    def small_rows(small):
        vec = jnp.concatenate([small[n].reshape(-1) for n in small_names] + [loss_part[0, 0].reshape(1)])
        return _to_rows(vec, SMALL_LAYER_ROWS)

    dh, big_1, small_1, _ = _layer_bwd(dh, saved[1], ws[1], ps[1])

    def exchange_early(big_0):
        return _Comm([(big_1[n].astype(BF16), False) for n in big_names] + [(small_rows(small_1), True)]
                     + [(big_0[n].astype(BF16), False) for n in others])

    grad_x, big_0, small_0, arrived = _layer_bwd(dh, saved[0], ws[0], ps[0], exchange_early)
    g_in_0, g_small_0 = _exchange([(big_0["w_in"].astype(BF16), False), (small_rows(small_0), True)],
                                  "exchange_grads_0")
    n_big = len(big_names)
    recv = [[g_in_0] + list(arrived[n_big + 1:]) + [g_small_0], list(arrived[:n_big + 1])]

    big_out = {}
    for t, n in enumerate(big_names):
        outs = None
        for l in reversed(range(DEPTH)):
            outs = _adamw_shard(recv[l][t], wts[n], mom[n], vel[n], l, outs, "adamw_%s_%d" % (n, l))
        big_out[n] = outs
    small_sum = [_sum8(recv[l][-1]).reshape(-1) for l in range(DEPTH)]
    g_small = {}
    off = 0
    for n, size in SMALL:
        g_small[n] = jnp.stack([small_sum[l][off:off + size] for l in range(DEPTH)])
        off += size
    loss = small_sum[0][off]
    g_small["conv_w"] = lax.dynamic_slice_in_dim(g_small["conv_w"].reshape(DEPTH, 3, D_FF), me * cw_shard, cw_shard,
                                                 axis=2)
    g_small = {n: g_small[n].reshape(wts[n].shape) for n in small_names}

    def pack_small(tree):
        return _to_rows(jnp.concatenate([tree[n].reshape(-1) for n in small_names]), SMALL_ROWS)

    small_out = (pack_small(g_small),) + tuple(_adamw_rows(pack_small(wts), pack_small(g_small), pack_small(mom),
                                                           pack_small(vel), "adamw_small"))

    def result(j):
        out = {n: big_out[n][j] for n in big_names}
        flat = small_out[j].reshape(-1)
        off = 0
        for n in small_names:
            out[n] = flat[off:off + wts[n].size].reshape(wts[n].shape)
            off += wts[n].size
        return [out[n] for n in names]

    return (loss, grad_x[None], *result(0), *result(1), *result(2), *result(3))
```

```python
import functools

import jax
import jax.numpy as jnp
import numpy as np
from jax import lax
from jax.experimental import pallas as pl
from jax.experimental.pallas import tpu as pltpu

F32 = jnp.float32
BF16 = jnp.bfloat16
MESH = pl.DeviceIdType.MESH

N_DEV = 8
DEPTH = 2
D_MODEL = 1024
HEAD_DIM = 64
SWA_Q = 512
SWA_KV = 128
FOX_W = 512
FOX_HEADS = 8
SWA_HEADS = 8
D_FF = 2816
N_IN = 4360
N_QKV = SWA_Q + 2 * SWA_KV + 3 * FOX_W
N_GATE = 2 * D_MODEL
F_PAD = 256
N_ZG = N_GATE + F_PAD
N_ZP = N_QKV + N_ZG
LN_EPS = 1e-5
NEG_INF = -1e30
ALPHA = (2 * DEPTH) ** 0.25
SCALE = HEAD_DIM ** -0.5
SLOPES = tuple(2.0 ** (-8.0 * (h + 1) / SWA_HEADS) for h in range(SWA_HEADS))

ADAM_LR = 0.001
ADAM_B1 = 0.9
ADAM_B2 = 0.999
ADAM_EPS = 1e-08
ADAM_WD = 0.01
ADAM_STEP = 10

LANE = 128
VMEM_LIMIT = 56 * 1024 * 1024

BIG = (("w_in", (D_MODEL, N_IN), 1), ("w_proj_a", (SWA_Q, D_MODEL), 1), ("w_proj_b", (FOX_W, D_MODEL), 1),
       ("w_out", (D_MODEL, D_MODEL), 0), ("w_ffn_in", (D_MODEL, 2 * D_FF), 1), ("w_ffn_out", (D_FF, D_MODEL), 0))
SMALL = (("ln_mix_g", D_MODEL), ("ln_mix_b", D_MODEL), ("b_in", N_IN), ("attn_sinks", SWA_HEADS),
         ("ln_ffn_g", D_MODEL), ("ln_ffn_b", D_MODEL), ("conv_w", 3 * D_FF), ("conv_b", D_FF))
ROW_BLOCK = 512
SMALL_LAYER_ROWS = -(-(sum(n for _, n in SMALL) + 1) // (8 * LANE)) * 8
SMALL_ROWS = ROW_BLOCK
FF_CHUNK = 2 * D_FF // N_DEV
N_CHUNK = D_FF // FF_CHUNK


def _div(n, cap, unit):
    if n <= cap:
        return n
    best = None
    for t in range(unit, cap + 1, unit):
        if n % t == 0:
            best = t
    assert best is not None, (n, cap, unit)
    return best


def _params(*sem):
    return pltpu.CompilerParams(dimension_semantics=sem, vmem_limit_bytes=VMEM_LIMIT)


def _peer(r):
    x, y, c = lax.axis_index("x"), lax.axis_index("y"), lax.axis_index("c")
    px = 1 - x if (r >> 2) & 1 else x
    py = 1 - y if (r >> 1) & 1 else y
    pc = 1 - c if r & 1 else c
    return (px, py, pc), 4 * px + 2 * py + pc


class _Comm:
    def __init__(self, tensors):
        self.arrays = [x for x, _ in tensors]
        self.gathers = [g for _, g in tensors]
        self.n = len(tensors)
        self.out_shape = [jax.ShapeDtypeStruct((N_DEV,) + (x.shape if g else x.shape[1:]), x.dtype)
                          for x, g in tensors]
        self.specs = [pl.BlockSpec(memory_space=pl.ANY)] * self.n
        self.scratch = [pltpu.SemaphoreType.DMA((N_DEV - 1, self.n)), pltpu.SemaphoreType.DMA((N_DEV - 1, self.n)),
                        pltpu.SemaphoreType.DMA((self.n,))]

    def _copies(self, x_refs, out_refs, sems):
        send_sems, recv_sems, local_sems = sems
        _, me = _peer(0)

        def src(t, idx):
            return x_refs[t] if self.gathers[t] else x_refs[t].at[idx]

        def remote(r, t, mine):
            peer, pid = _peer(r)
            return pltpu.make_async_remote_copy(src_ref=src(t, pid), dst_ref=out_refs[t].at[me if mine else pid],
                                                send_sem=send_sems.at[r - 1, t], recv_sem=recv_sems.at[r - 1, t],
                                                device_id=peer, device_id_type=MESH)

        pairs = [(r, t) for r in range(1, N_DEV) for t in range(self.n)]
        local = [pltpu.make_async_copy(src(t, me), out_refs[t].at[me], local_sems.at[t]) for t in range(self.n)]
        return local, [remote(r, t, True) for r, t in pairs], lambda: [remote(r, t, False) for r, t in pairs]

    def start(self, x_refs, out_refs, sems):
        local, sent, _ = self._copies(x_refs, out_refs, sems)
        for cp in local + sent:
            cp.start()

    def wait(self, x_refs, out_refs, sems):
        local, sent, landing = self._copies(x_refs, out_refs, sems)
        for cp in landing():
            cp.wait_recv()
        for cp in sent:
            cp.wait_send()
        for cp in local:
            cp.wait()


def _exchange(tensors, name):
    comm = _Comm(tensors)
    n = comm.n

    def body(*refs):
        comm.start(refs[:n], refs[n:2 * n], refs[2 * n:])
        comm.wait(refs[:n], refs[n:2 * n], refs[2 * n:])

    return pl.pallas_call(body, name=name, out_shape=comm.out_shape, in_specs=comm.specs, out_specs=comm.specs,
                          scratch_shapes=comm.scratch)(*comm.arrays)


def _with_comm(comm, n_in, n_out, first, last, compute):
    nc = comm.n if comm is not None else 0

    def body(*refs):
        ins, x_refs = refs[:n_in], refs[n_in:n_in + nc]
        outs = refs[n_in + nc:n_in + nc + n_out]
        out_refs = refs[n_in + nc + n_out:n_in + 2 * nc + n_out]
        sems = refs[n_in + 2 * nc + n_out:]
        if nc:
            @pl.when(first())
            def _():
                comm.start(x_refs, out_refs, sems)

        compute(*ins, *outs)
        if nc:
            @pl.when(last())
            def _():
                comm.wait(x_refs, out_refs, sems)

    return body


def _linear(a, b, *, name, trans_b=False, bias=None, res=None, res_scale=1.0, out_dtype=F32, tm=512, tn=640):
    m, k = a.shape
    n = b.shape[0] if trans_b else b.shape[1]
    tm = _div(m, tm, 8)
    tn = _div(n, tn, LANE)
    dn = (((1,), (1,)), ((), ())) if trans_b else (((1,), (0,)), ((), ()))

    def body(*refs):
        a_ref, b_ref = refs[0], refs[1]
        rest = list(refs[2:])
        bias_ref = rest.pop(0) if bias is not None else None
        res_ref = rest.pop(0) if res is not None else None
        o_ref = rest.pop(0)
        acc = lax.dot_general(a_ref[...].astype(BF16), b_ref[...].astype(BF16), dn, preferred_element_type=F32)
        if bias_ref is not None:
            acc = acc + bias_ref[...]
        if res_ref is not None:
            acc = acc + res_scale * res_ref[...].astype(F32)
        o_ref[...] = acc.astype(out_dtype)

    in_specs = [pl.BlockSpec((tm, k), lambda i, j: (i, 0)),
                pl.BlockSpec((tn, k), lambda i, j: (j, 0)) if trans_b else pl.BlockSpec((k, tn), lambda i, j: (0, j))]
    args = [a, b]
    if bias is not None:
        in_specs.append(pl.BlockSpec((1, tn), lambda i, j: (0, j)))
        args.append(bias)
    if res is not None:
        in_specs.append(pl.BlockSpec((tm, tn), lambda i, j: (i, j)))
        args.append(res)
    return pl.pallas_call(
        body, name=name, grid=(m // tm, n // tn), in_specs=in_specs,
        out_specs=pl.BlockSpec((tm, tn), lambda i, j: (i, j)),
        out_shape=jax.ShapeDtypeStruct((m, n), out_dtype),
        compiler_params=_params("parallel", "arbitrary"),
    )(*args)


def _z_proj(h, w_in_p, b_p, tm=512):
    m, k = h.shape
    tm = _div(m, tm, 8)

    def body(h_ref, w_ref, b_ref, zq_ref, zg_ref):
        a = h_ref[...].astype(BF16)
        zq_ref[...] = (jnp.dot(a, w_ref[:, :N_QKV], preferred_element_type=F32) + b_ref[:, :N_QKV]).astype(BF16)
        zg_ref[...] = jnp.dot(a, w_ref[:, N_QKV:], preferred_element_type=F32) + b_ref[:, N_QKV:]

    return pl.pallas_call(
        body, name="z_proj", grid=(m // tm,),
        in_specs=[pl.BlockSpec((tm, k), lambda i: (i, 0)), pl.BlockSpec((k, N_ZP), lambda i: (0, 0)),
                  pl.BlockSpec((1, N_ZP), lambda i: (0, 0))],
        out_specs=[pl.BlockSpec((tm, N_QKV), lambda i: (i, 0)), pl.BlockSpec((tm, N_ZG), lambda i: (i, 0))],
        out_shape=[jax.ShapeDtypeStruct((m, N_QKV), BF16), jax.ShapeDtypeStruct((m, N_ZG), F32)],
        compiler_params=_params("parallel"),
    )(h, w_in_p, b_p)


def _linear_tn(a, g, *, name, tk=1024, tn=640, tm=2048):
    m, k = a.shape
    n = g.shape[1]
    tk = _div(k, tk, LANE)
    tn = _div(n, tn, LANE)
    tm = _div(m, tm, 8)
    steps = m // tm

    def body(a_ref, g_ref, o_ref, acc_ref):
        s = pl.program_id(2)

        @pl.when(s == 0)
        def _():
            acc_ref[...] = jnp.zeros_like(acc_ref)

        acc_ref[...] += lax.dot_general(a_ref[...].astype(BF16), g_ref[...].astype(BF16), (((0,), (0,)), ((), ())),
                                        preferred_element_type=F32)

        @pl.when(s == steps - 1)
        def _():
            o_ref[...] = acc_ref[...]

    return pl.pallas_call(
        body, name=name, grid=(k // tk, n // tn, steps),
        in_specs=[pl.BlockSpec((tm, tk), lambda i, j, s: (s, i)), pl.BlockSpec((tm, tn), lambda i, j, s: (s, j))],
        out_specs=pl.BlockSpec((tk, tn), lambda i, j, s: (i, j)),
        out_shape=jax.ShapeDtypeStruct((k, n), F32),
        scratch_shapes=[pltpu.VMEM((tk, tn), F32)],
        compiler_params=_params("parallel", "parallel", "arbitrary"),
    )(a, g)


def _colsum(g, *, name, tm=512):
    m, n = g.shape
    tm = _div(m, tm, 8)

    def body(g_ref, o_ref):
        @pl.when(pl.program_id(0) == 0)
        def _():
            o_ref[...] = jnp.zeros_like(o_ref)

        o_ref[...] += jnp.sum(g_ref[...].astype(F32), axis=0, keepdims=True)

    return pl.pallas_call(
        body, name=name, grid=(m // tm,),
        in_specs=[pl.BlockSpec((tm, n), lambda i: (i, 0))],
        out_specs=pl.BlockSpec((1, n), lambda i: (0, 0)),
        out_shape=jax.ShapeDtypeStruct((1, n), F32),
        compiler_params=_params("arbitrary"),
    )(g)


def _ln(u, g, b):
    mu = jnp.mean(u, axis=-1, keepdims=True)
    d = u - mu
    var = jnp.mean(d * d, axis=-1, keepdims=True)
    return d * lax.rsqrt(var + LN_EPS) * g + b


def _ln_bwd_block(dy, u, g):
    mu = jnp.mean(u, axis=-1, keepdims=True)
    dd = u - mu
    rstd = lax.rsqrt(jnp.mean(dd * dd, axis=-1, keepdims=True) + LN_EPS)
    xhat = dd * rstd
    dxh = dy * g
    m1 = jnp.mean(dxh, axis=-1, keepdims=True)
    m2 = jnp.mean(dxh * xhat, axis=-1, keepdims=True)
    return (rstd * (dxh - m1 - xhat * m2), jnp.sum(dy * xhat, axis=0, keepdims=True),
            jnp.sum(dy, axis=0, keepdims=True))


SCAN_ROWS = 512


def _tri(n, upper):
    r = lax.broadcasted_iota(jnp.int32, (n, n), 0)
    c = lax.broadcasted_iota(jnp.int32, (n, n), 1)
    return jnp.where((c >= r) if upper else (c <= r), 1.0, 0.0).astype(F32)


def _cumsum_logf(zg):
    s = zg.shape[0]
    t = _div(s, SCAN_ROWS, LANE)
    nb = s // t
    fcol = N_GATE // LANE

    def body(f_ref, c_ref, carry_ref):
        @pl.when(pl.program_id(0) == 0)
        def _():
            carry_ref[...] = jnp.zeros_like(carry_ref)

        f = f_ref[...]
        logf = jnp.minimum(f, 0.0) - jnp.log(1.0 + jnp.exp(-jnp.abs(f)))
        c = jnp.dot(_tri(t, False), logf, precision=lax.Precision.HIGHEST, preferred_element_type=F32)
        c = c + carry_ref[0:1, :]
        c_ref[...] = c
        carry_ref[...] = jnp.broadcast_to(c[t - 1:t, :], carry_ref.shape)

    return pl.pallas_call(
        body, name="cumsum_logf", grid=(nb,),
        in_specs=[pl.BlockSpec((t, LANE), lambda i: (i, fcol))],
        out_specs=pl.BlockSpec((t, LANE), lambda i: (i, 0)),
        out_shape=jax.ShapeDtypeStruct((s, LANE), F32),
        scratch_shapes=[pltpu.VMEM((8, LANE), F32)],
        compiler_params=_params("arbitrary"),
    )(zg)


def _forget_bwd(dc, zg):
    s = zg.shape[0]
    t = _div(s, SCAN_ROWS, LANE)
    nb = s // t
    fcol = N_GATE // LANE

    def body(dc_ref, f_ref, o_ref, carry_ref):
        @pl.when(pl.program_id(0) == 0)
        def _():
            carry_ref[...] = jnp.zeros_like(carry_ref)

        dc = dc_ref[...]
        dlogf = jnp.dot(_tri(t, True), dc, precision=lax.Precision.HIGHEST, preferred_element_type=F32)
        dlogf = dlogf + carry_ref[0:1, :]
        o_ref[...] = (dlogf * jax.nn.sigmoid(-f_ref[...])).astype(BF16)
        carry_ref[...] = jnp.broadcast_to(dlogf[0:1, :], carry_ref.shape)

    return pl.pallas_call(
        body, name="forget_bwd", grid=(nb,),
        in_specs=[pl.BlockSpec((t, LANE), lambda i: (nb - 1 - i, 0)),
                  pl.BlockSpec((t, LANE), lambda i: (nb - 1 - i, fcol))],
        out_specs=pl.BlockSpec((t, LANE), lambda i: (nb - 1 - i, 0)),
        out_shape=jax.ShapeDtypeStruct((s, LANE), BF16),
        scratch_shapes=[pltpu.VMEM((8, LANE), F32)],
        compiler_params=_params("arbitrary"),
    )(dc, zg)


KA_COL = SWA_Q // LANE
VA_COL = KA_COL + 1


def _half_masks():
    lane = lax.broadcasted_iota(jnp.int32, (1, LANE), 1)
    hi = lane >= HEAD_DIM
    return (jnp.logical_not(hi), hi)


def _both_halves(x, sel):
    xs = jnp.where(sel, x, 0.0)
    return xs + pltpu.roll(xs, HEAD_DIM, 1)


SWA_PER_KV = 4
WIDE = SWA_PER_KV * LANE


def _swa_bias():
    k = np.arange(2 * LANE)[:, None]
    q = np.arange(LANE)[None, :]
    dist = (q + LANE - k).astype(np.float32)
    valid = (dist >= 0) & (dist < LANE)
    per_head = [np.where(valid, np.float32(-s) * dist, np.float32(NEG_INF)) for s in SLOPES]
    return jnp.asarray(np.stack([np.concatenate(per_head[SWA_PER_KV * hk:SWA_PER_KV * (hk + 1)], axis=1)
                                 for hk in range(2)]), F32)


def _no_previous_block(i_blk):
    k = lax.broadcasted_iota(jnp.int32, (2 * LANE, WIDE), 0)
    return jnp.where((i_blk == 0) & (k < LANE), NEG_INF, 0.0)


def _stack_heads(ref, blk, hk, halves, scale):
    tiles = []
    for j in range(SWA_PER_KV):
        p = 2 * hk + j // 2
        t = ref[blk, p * LANE:(p + 1) * LANE]
        if scale:
            t = _scaled(t)
        tiles.append(jnp.where(halves[j % 2], t, jnp.zeros_like(t)))
    return jnp.concatenate(tiles, axis=0)


def _pair_tile(wide, pp, row_halves):
    a = wide[:, (2 * pp) * LANE:(2 * pp + 1) * LANE]
    b = wide[:, (2 * pp + 1) * LANE:(2 * pp + 2) * LANE]
    return jnp.where(row_halves[0], a, b).T


def _lane_blocks(rows8, hk):
    return jnp.concatenate([rows8[SWA_PER_KV * hk + j:SWA_PER_KV * hk + j + 1, :] for j in range(SWA_PER_KV)], axis=1)


def _row_halves():
    hi = lax.broadcasted_iota(jnp.int32, (LANE, 1), 0) >= HEAD_DIM
    return (jnp.logical_not(hi), hi)


NT = (((1,), (1,)), ((), ()))


def _scaled(q):
    return (q.astype(F32) * SCALE).astype(BF16)


SWA_GROUP = 4


def _swa_group(s_len):
    return SWA_GROUP if (s_len // LANE) % SWA_GROUP == 0 else 1


def _swa_specs(group):
    rows = group * LANE
    prev = lambda i: jnp.maximum(i * group - 1, 0)
    return [pl.BlockSpec((rows, SWA_Q), lambda i: (i, 0)),
            pl.BlockSpec((rows, LANE), lambda i: (i, KA_COL)), pl.BlockSpec((rows, LANE), lambda i: (i, VA_COL)),
            pl.BlockSpec((LANE, LANE), lambda i: (prev(i), KA_COL)),
            pl.BlockSpec((LANE, LANE), lambda i: (prev(i), VA_COL))]


def _swa_window(g, cur_ref, prev_ref):
    before = prev_ref[...] if g == 0 else cur_ref[(g - 1) * LANE:g * LANE, :]
    return jnp.concatenate([before, cur_ref[g * LANE:(g + 1) * LANE, :]], axis=0).astype(F32)


def _swa_fwd(zq, sinks):
    s_len = zq.shape[0]
    group = _swa_group(s_len)
    rows = group * LANE
    sink_lanes = jnp.repeat(sinks[:, :SWA_HEADS], LANE, axis=1)

    def body(q_ref, kc_ref, vc_ref, kp_ref, vp_ref, sink_ref, bias_ref, o_ref, lse_ref):
        halves = _half_masks()
        row_halves = _row_halves()
        for g in range(group):
            blk = slice(g * LANE, (g + 1) * LANE)
            kcat = _swa_window(g, kc_ref, kp_ref)
            vcat = _swa_window(g, vc_ref, vp_ref)
            lse_rows = []
            for hk in range(2):
                kb = _both_halves(kcat, halves[hk]).astype(BF16)
                v_t = _both_halves(vcat, halves[hk]).T.astype(BF16)
                q4 = _stack_heads(q_ref, blk, hk, halves, True)
                s_t = lax.dot_general(kb, q4, NT, preferred_element_type=F32) + bias_ref[hk]
                if g == 0:
                    s_t = s_t + _no_previous_block(pl.program_id(0))
                sink = sink_ref[:, hk * WIDE:(hk + 1) * WIDE]
                m = jnp.maximum(jnp.max(s_t, axis=0, keepdims=True), sink)
                pe = jnp.exp(s_t - m)
                den = jnp.sum(pe, axis=0, keepdims=True) + jnp.exp(sink - m)
                out_t = jnp.dot(v_t, (pe * (1.0 / den)).astype(BF16), preferred_element_type=F32)
                for pp in range(2):
                    p = 2 * hk + pp
                    o_ref[blk, p * LANE:(p + 1) * LANE] = _pair_tile(out_t, pp, row_halves).astype(BF16)
                lse4 = m + jnp.log(den)
                lse_rows += [lse4[:, j * LANE:(j + 1) * LANE] for j in range(SWA_PER_KV)]
            lse_ref[:, blk] = jnp.concatenate(lse_rows, axis=0)

    return pl.pallas_call(
        body, name="swa_fwd", grid=(s_len // rows,),
        in_specs=_swa_specs(group) + [pl.BlockSpec((1, SWA_HEADS * LANE), lambda i: (0, 0)),
                                      pl.BlockSpec((2, 2 * LANE, WIDE), lambda i: (0, 0, 0))],
        out_specs=[pl.BlockSpec((rows, SWA_Q), lambda i: (i, 0)), pl.BlockSpec((SWA_HEADS, rows), lambda i: (0, i))],
        out_shape=[jax.ShapeDtypeStruct((s_len, SWA_Q), BF16), jax.ShapeDtypeStruct((SWA_HEADS, s_len), F32)],
        compiler_params=_params("parallel"),
    )(zq, zq, zq, zq, zq, sink_lanes, _swa_bias())


def _swa_bwd(zq, sinks, o, do, lse):
    s_len = zq.shape[0]
    group = _swa_group(s_len)
    rows = group * LANE

    def body(q_ref, kc_ref, vc_ref, kp_ref, vp_ref, sink_ref, bias_ref, o_ref, do_ref, lse_ref,
             dq_ref, dk_ref, dv_ref, ds_ref):
        halves = _half_masks()
        row_halves = _row_halves()
        lane = lax.broadcasted_iota(jnp.int32, (1, LANE), 1)
        dsink = jnp.zeros((1, LANE), F32)
        for g in range(group):
            blk = slice(g * LANE, (g + 1) * LANE)
            i_blk = pl.program_id(0) * group + g
            kcat = _swa_window(g, kc_ref, kp_ref)
            vcat = _swa_window(g, vc_ref, vp_ref)
            lse_rows = lse_ref[:, blk]
            prod = do_ref[blk, :].astype(F32) * o_ref[blk, :].astype(F32)
            select = (lax.broadcasted_iota(jnp.int32, (SWA_HEADS, SWA_Q), 1) // HEAD_DIM
                      == lax.broadcasted_iota(jnp.int32, (SWA_HEADS, SWA_Q), 0))
            delta_rows = lax.dot_general(jnp.where(select, 1.0, 0.0), prod, NT, precision=lax.Precision.HIGHEST,
                                         preferred_element_type=F32)
            dk_tot = jnp.zeros((2 * LANE, LANE), F32)
            dv_tot = jnp.zeros((2 * LANE, LANE), F32)
            for hk in range(2):
                kb = _both_halves(kcat, halves[hk])
                k_t = kb.T.astype(BF16)
                kb = kb.astype(BF16)
                vb = _both_halves(vcat, halves[hk]).astype(BF16)
                q4 = _stack_heads(q_ref, blk, hk, halves, True)
                do4 = _stack_heads(do_ref, blk, hk, halves, False)
                lse4 = _lane_blocks(lse_rows, hk)
                delta4 = _lane_blocks(delta_rows, hk)
                s_t = lax.dot_general(kb, q4, NT, preferred_element_type=F32) + bias_ref[hk]
                if g == 0:
                    s_t = s_t + _no_previous_block(pl.program_id(0))
                p_t = jnp.exp(s_t - lse4)
                dp_t = lax.dot_general(vb, do4, NT, preferred_element_type=F32)
                ds_t = (p_t * (dp_t - delta4)).astype(BF16)
                sink_part = jnp.exp(sink_ref[:, hk * WIDE:(hk + 1) * WIDE] - lse4) * delta4
                for j in range(SWA_PER_KV):
                    dsink_h = -jnp.sum(sink_part[:, j * LANE:(j + 1) * LANE], axis=1, keepdims=True)
                    dsink = dsink + jnp.where(lane == SWA_PER_KV * hk + j, dsink_h, 0.0)
                dq_t = jnp.dot(k_t, ds_t, preferred_element_type=F32)
                for pp in range(2):
                    p = 2 * hk + pp
                    dq_ref[blk, p * LANE:(p + 1) * LANE] = (_pair_tile(dq_t, pp, row_halves) * SCALE).astype(BF16)
                dk_acc = jnp.dot(ds_t, q4, preferred_element_type=F32)
                dv_acc = jnp.dot(p_t.astype(BF16), do4, preferred_element_type=F32)
                dk_tot = dk_tot + jnp.where(halves[hk], dk_acc + pltpu.roll(dk_acc, HEAD_DIM, 1), 0.0)
                dv_tot = dv_tot + jnp.where(halves[hk], dv_acc + pltpu.roll(dv_acc, HEAD_DIM, 1), 0.0)
            cur = pl.ds(pl.multiple_of(i_blk * LANE, LANE), LANE)
            dk_ref[cur, :] = dk_tot[LANE:, :]
            dv_ref[cur, :] = dv_tot[LANE:, :]

            def add_previous(i_blk=i_blk, dk_tot=dk_tot, dv_tot=dv_tot):
                prv = pl.ds(pl.multiple_of((i_blk - 1) * LANE, LANE), LANE)
                dk_ref[prv, :] += dk_tot[:LANE, :]
                dv_ref[prv, :] += dv_tot[:LANE, :]

            if g == 0:
                pl.when(i_blk > 0)(add_previous)
            else:
                add_previous()

        @pl.when(pl.program_id(0) == 0)
        def _():
            ds_ref[...] = jnp.zeros_like(ds_ref)

        ds_ref[...] += dsink

    blk512 = pl.BlockSpec((rows, SWA_Q), lambda i: (i, 0))
    full = pl.BlockSpec((s_len, LANE), lambda i: (0, 0))
    vec = pl.BlockSpec((1, LANE), lambda i: (0, 0))
    return pl.pallas_call(
        body, name="swa_bwd", grid=(s_len // rows,),
        in_specs=_swa_specs(group) + [pl.BlockSpec((1, SWA_HEADS * LANE), lambda i: (0, 0)),
                                      pl.BlockSpec((2, 2 * LANE, WIDE), lambda i: (0, 0, 0)), blk512, blk512,
                                      pl.BlockSpec((SWA_HEADS, rows), lambda i: (0, i))],
        out_specs=[blk512, full, full, vec],
        out_shape=[jax.ShapeDtypeStruct((s_len, SWA_Q), BF16), jax.ShapeDtypeStruct((s_len, LANE), F32),
                   jax.ShapeDtypeStruct((s_len, LANE), F32), jax.ShapeDtypeStruct((1, LANE), F32)],
        compiler_params=_params("arbitrary"),
    )(zq, zq, zq, zq, zq, jnp.repeat(sinks[:, :SWA_HEADS], LANE, axis=1), _swa_bias(), o, do, lse)


QB_COL = (SWA_Q + 2 * SWA_KV) // LANE
KB_COL = QB_COL + FOX_W // LANE
VB_COL = KB_COL + FOX_W // LANE
N_PAIR = FOX_HEADS // 2


def _causal(t, keys_first=False):
    r = lax.broadcasted_iota(jnp.int32, (t, t), 0)
    c = lax.broadcasted_iota(jnp.int32, (t, t), 1)
    return c >= r if keys_first else r >= c


N_SPLIT = 3


def _own_half(e):
    hi = lax.broadcasted_iota(jnp.int32, (1, LANE), 1) >= HEAD_DIM
    return hi if e else jnp.logical_not(hi)


def _feature_lane(e, t):
    return HEAD_DIM * (1 - e) + t


def _fox_prep(zq, c, tm=256):
    s_len = zq.shape[0]
    tm = _div(s_len, tm, 8)

    def body(z_ref, c_ref, qx_ref, kx_ref, vx_ref):
        lane = lax.broadcasted_iota(jnp.int32, (1, LANE), 1)
        for h in range(FOX_HEADS):
            p, e = divmod(h, 2)
            own = _own_half(e)
            tile = lambda col: z_ref[:, (col + p) * LANE:(col + p + 1) * LANE].astype(F32)
            rest = c_ref[:, h:h + 1]
            qf = jnp.zeros((tm, LANE), F32)
            kf = jnp.zeros((tm, LANE), F32)
            for t in range(N_SPLIT):
                part = rest.astype(BF16).astype(F32)
                rest = rest - part
                qf = jnp.where(lane == _feature_lane(e, t), part, qf)
                qf = jnp.where(lane == _feature_lane(e, N_SPLIT + t), 1.0, qf)
                kf = jnp.where(lane == _feature_lane(e, t), 1.0, kf)
                kf = jnp.where(lane == _feature_lane(e, N_SPLIT + t), -part, kf)
            vf = jnp.where(lane == _feature_lane(e, 0), 1.0, 0.0)
            cols = slice(h * LANE, (h + 1) * LANE)
            qx_ref[:, cols] = jnp.where(own, tile(QB_COL) * SCALE, qf).astype(BF16)
            kx_ref[:, cols] = jnp.where(own, tile(KB_COL), kf).astype(BF16)
            vx_ref[:, cols] = jnp.where(own, tile(VB_COL), vf).astype(BF16)

    out = jax.ShapeDtypeStruct((s_len, FOX_HEADS * LANE), BF16)
    blk = pl.BlockSpec((tm, FOX_HEADS * LANE), lambda i: (i, 0))
    return pl.pallas_call(
        body, name="fox_prep", grid=(s_len // tm,),
        in_specs=[pl.BlockSpec((tm, N_QKV), lambda i: (i, 0)), pl.BlockSpec((tm, LANE), lambda i: (i, 0))],
        out_specs=[blk, blk, blk], out_shape=[out, out, out],
        compiler_params=_params("parallel"),
    )(zq, c)


def _comm_parts(comm):
    return ([], [], [], []) if comm is None else (comm.specs, comm.out_shape, comm.scratch, comm.arrays)


def _fox_fwd(qx, kx, vx, comm=None, t_cap=1024):
    s_len = qx.shape[0]
    t = _div(s_len, t_cap, LANE)
    nq = s_len // t
    c_specs, c_shapes, c_scratch, c_arrays = _comm_parts(comm)

    def compute(q_ref, k_ref, v_ref, o_ref, o32_ref, lse_ref):
        i = pl.program_id(1)
        qs = [q_ref[:, e * LANE:(e + 1) * LANE] for e in range(2)]

        def step(j, carry, diag):
            rows = pl.ds(pl.multiple_of(j * t, t), t)
            new = []
            for e in range(2):
                m, acc = carry[e]
                s = lax.dot_general(qs[e], k_ref[rows, e * LANE:(e + 1) * LANE], (((1,), (1,)), ((), ())),
                                    preferred_element_type=F32)
                if diag:
                    s = jnp.where(_causal(t), s, NEG_INF)
                mn = jnp.maximum(m, jnp.max(s, axis=1, keepdims=True))
                pe = jnp.exp(s - mn)
                p_hi = pe.astype(BF16)
                p_lo = (pe - p_hi.astype(F32)).astype(BF16)
                vs = v_ref[rows, e * LANE:(e + 1) * LANE]
                acc = (acc * jnp.exp(m - mn) + jnp.dot(p_hi, vs, preferred_element_type=F32)
                       + jnp.dot(p_lo, vs, preferred_element_type=F32))
                new.append((mn, acc))
            return tuple(new)

        init = (jnp.full((t, 1), NEG_INF, F32), jnp.zeros((t, LANE), F32))
        carry = lax.fori_loop(0, i, lambda j, c: step(j, c, False), (init, init))
        carry = step(i, carry, True)
        outs, lses = [], []
        for e in range(2):
            m, acc = carry[e]
            l = acc[:, _feature_lane(e, 0):_feature_lane(e, 0) + 1]
            outs.append(acc / l)
            lses.append(m + jnp.log(l))
        out = jnp.where(_own_half(1), outs[1], outs[0])
        o_ref[...] = out.astype(BF16)
        o32_ref[...] = out
        lse_ref[...] = jnp.where(_own_half(1), lses[1], lses[0])

    body = _with_comm(comm, 3, 3, lambda: (pl.program_id(0) == 0) & (pl.program_id(1) == 0),
                      lambda: (pl.program_id(0) == N_PAIR - 1) & (pl.program_id(1) == nq - 1), compute)
    pair = pl.BlockSpec((s_len, 2 * LANE), lambda p, i: (0, p))
    tile = pl.BlockSpec((t, LANE), lambda p, i: (i, p))
    outs = pl.pallas_call(
        body, name="fox_fwd" if comm is None else "fox_fwd_comm", grid=(N_PAIR, nq),
        in_specs=[pl.BlockSpec((t, 2 * LANE), lambda p, i: (i, p)), pair, pair] + c_specs,
        out_specs=[tile, tile, tile] + c_specs,
        out_shape=[jax.ShapeDtypeStruct((s_len, FOX_W), BF16), jax.ShapeDtypeStruct((s_len, FOX_W), F32),
                   jax.ShapeDtypeStruct((s_len, FOX_W), F32)] + c_shapes,
        scratch_shapes=c_scratch,
        compiler_params=_params("arbitrary", "arbitrary"),
    )(qx, kx, vx, *c_arrays)
    return outs[0], outs[1], outs[2], outs[3:]


def _fox_stats(o, do, lse, tm=256):
    s_len = o.shape[0]
    tm = _div(s_len, tm, LANE)

    def body(o_ref, do_ref, lse_ref, dox_ref, st_ref):
        lane = lax.broadcasted_iota(jnp.int32, (1, LANE), 1)
        for p in range(N_PAIR):
            cols = slice(p * LANE, (p + 1) * LANE)
            dout = do_ref[:, cols]
            prod = o_ref[:, cols] * dout.astype(F32)
            lse = lse_ref[:, cols]
            st = jnp.zeros((tm, LANE), F32)
            for e in range(2):
                h = 2 * p + e
                dox_ref[:, h * LANE:(h + 1) * LANE] = jnp.where(_own_half(e), dout, jnp.zeros_like(dout))
                st = jnp.where(lane == e, lse[:, e * HEAD_DIM:e * HEAD_DIM + 1], st)
                delta = jnp.sum(jnp.where(_own_half(e), prod, 0.0), axis=1, keepdims=True)
                st = jnp.where(lane == 2 + e, delta, st)
            st_ref[p] = st.T[:8, :]

    row = pl.BlockSpec((tm, FOX_W), lambda i: (i, 0))
    return pl.pallas_call(
        body, name="fox_stats", grid=(s_len // tm,), in_specs=[row, row, row],
        out_specs=[pl.BlockSpec((tm, FOX_HEADS * LANE), lambda i: (i, 0)),
                   pl.BlockSpec((N_PAIR, 8, tm), lambda i: (0, 0, i))],
        out_shape=[jax.ShapeDtypeStruct((s_len, FOX_HEADS * LANE), BF16),
                   jax.ShapeDtypeStruct((N_PAIR, 8, s_len), F32)],
        compiler_params=_params("parallel"),
    )(o, do, lse)


def _fox_bwd(qx, kx, vx, dox, stats, comm=None, t_cap=512):
    s_len = qx.shape[0]
    t = _div(s_len, t_cap, LANE)
    n = s_len // t
    c_specs, c_shapes, c_scratch, c_arrays = _comm_parts(comm)

    def compute(q_ref, do_ref, st_ref, k_ref, v_ref, dq_ref, dk_ref, dv_ref, dc_ref):
        j = pl.program_id(1)
        lane = lax.broadcasted_iota(jnp.int32, (1, LANE), 1)

        @pl.when(j == 0)
        def _():
            dq_ref[...] = jnp.zeros_like(dq_ref)

        ks = [k_ref[:, e * LANE:(e + 1) * LANE] for e in range(2)]
        vs = [v_ref[:, e * LANE:(e + 1) * LANE] for e in range(2)]

        def step(i, carry, diag):
            rows = pl.ds(pl.multiple_of(i * t, t), t)
            new = []
            dq = jnp.zeros((t, LANE), F32)
            for e in range(2):
                dk, dv, dc = carry[e]
                q = q_ref[rows, e * LANE:(e + 1) * LANE]
                dout = do_ref[rows, e * LANE:(e + 1) * LANE]
                s_t = lax.dot_general(ks[e], q, (((1,), (1,)), ((), ())), preferred_element_type=F32)
                if diag:
                    s_t = jnp.where(_causal(t, keys_first=True), s_t, NEG_INF)
                p_t = jnp.exp(s_t - st_ref[0, e:e + 1, rows])
                dp_t = lax.dot_general(vs[e], dout, (((1,), (1,)), ((), ())), preferred_element_type=F32)
                ds_f = p_t * (dp_t - st_ref[0, 2 + e:3 + e, rows])
                ds_t = ds_f.astype(BF16)
                dc = dc + jnp.sum(ds_f, axis=1, keepdims=True)
                dv = dv + jnp.dot(p_t.astype(BF16), dout, preferred_element_type=F32)
                dk = dk + jnp.dot(ds_t, q, preferred_element_type=F32)
                dq_e = lax.dot_general(ds_t, ks[e], (((0,), (0,)), ((), ())), preferred_element_type=F32)
                dq = dq + jnp.where(_own_half(e), dq_e, 0.0)
                new.append((dk, dv, dc))
            dq_ref[rows, :] += dq * SCALE
            return tuple(new)

        zero = jnp.zeros((t, LANE), F32)
        init = (zero, zero, jnp.zeros((t, 1), F32))
        carry = step(j, (init, init), True)
        (dk0, dv0, dc0), (dk1, dv1, dc1) = lax.fori_loop(j + 1, n, lambda i, c: step(i, c, False), carry)
        dk_ref[...] = jnp.where(_own_half(1), dk1, dk0).astype(BF16)
        dv_ref[...] = jnp.where(_own_half(1), dv1, dv0).astype(BF16)
        dc_ref[...] = jnp.where(lane == 0, -dc0, jnp.where(lane == 1, -dc1, 0.0))

    body = _with_comm(comm, 5, 4, lambda: (pl.program_id(0) == 0) & (pl.program_id(1) == 0),
                      lambda: (pl.program_id(0) == N_PAIR - 1) & (pl.program_id(1) == n - 1), compute)
    pair = pl.BlockSpec((s_len, 2 * LANE), lambda p, j: (0, p))
    blk = pl.BlockSpec((t, 2 * LANE), lambda p, j: (j, p))
    tile = pl.BlockSpec((t, LANE), lambda p, j: (j, p))
    outs = pl.pallas_call(
        body, name="fox_bwd" if comm is None else "fox_bwd_comm", grid=(N_PAIR, n),
        in_specs=[pair, pair, pl.BlockSpec((1, 8, s_len), lambda p, j: (p, 0, 0)), blk, blk] + c_specs,
        out_specs=[pl.BlockSpec((s_len, LANE), lambda p, j: (0, p)), tile, tile, tile] + c_specs,
        out_shape=[jax.ShapeDtypeStruct((s_len, FOX_W), F32), jax.ShapeDtypeStruct((s_len, FOX_W), BF16),
                   jax.ShapeDtypeStruct((s_len, FOX_W), BF16), jax.ShapeDtypeStruct((s_len, FOX_W), F32)] + c_shapes,
        scratch_shapes=c_scratch,
        compiler_params=_params("arbitrary", "arbitrary"),
    )(qx, dox, stats, kx, vx, *c_arrays)
    return outs[0], outs[1], outs[2], outs[3], outs[4:]


def _mixer_out(attn_a, attn_b, zg, h, wpa, wpb, wout, g, b, tm=256):
    m = h.shape[0]
    tm = _div(m, tm, 8)

    def body(a_ref, b_ref, ga_ref, gb_ref, h_ref, wpa_ref, wpb_ref, wout_ref, g_ref, bb_ref,
             h1_ref, u_ref, mg_ref, ya_ref, yb_ref):
        ya = jnp.dot(a_ref[...], wpa_ref[...], preferred_element_type=F32)
        yb = jnp.dot(b_ref[...], wpb_ref[...], preferred_element_type=F32)
        merged = (jax.nn.sigmoid(ga_ref[...]) * ya + jax.nn.sigmoid(gb_ref[...]) * yb).astype(BF16)
        u = ALPHA * h_ref[...] + jnp.dot(merged, wout_ref[...], preferred_element_type=F32)
        u_ref[...] = u
        h1_ref[...] = _ln(u, g_ref[...], bb_ref[...])
        mg_ref[...] = merged
        ya_ref[...] = ya.astype(BF16)
        yb_ref[...] = yb.astype(BF16)

    row = pl.BlockSpec((tm, D_MODEL), lambda i: (i, 0))
    att = pl.BlockSpec((tm, SWA_Q), lambda i: (i, 0))
    vec = pl.BlockSpec((1, D_MODEL), lambda i: (0, 0))
    wsm = pl.BlockSpec((SWA_Q, D_MODEL), lambda i: (0, 0))
    return pl.pallas_call(
        body, name="mixer_out", grid=(m // tm,),
        in_specs=[att, att, row, pl.BlockSpec((tm, D_MODEL), lambda i: (i, 1)), row, wsm, wsm,
                  pl.BlockSpec((D_MODEL, D_MODEL), lambda i: (0, 0)), vec, vec],
        out_specs=[row, row, row, row, row],
        out_shape=[jax.ShapeDtypeStruct((m, D_MODEL), F32), jax.ShapeDtypeStruct((m, D_MODEL), F32),
                   jax.ShapeDtypeStruct((m, D_MODEL), BF16), jax.ShapeDtypeStruct((m, D_MODEL), BF16),
                   jax.ShapeDtypeStruct((m, D_MODEL), BF16)],
        compiler_params=_params("parallel"),
    )(attn_a, attn_b, zg, zg, h, wpa, wpb, wout, g, b)


def _mixer_bwd(dh1, u1, g, wout, ya, yb, zg, wpa, wpb, tm=256):
    m = dh1.shape[0]
    tm = _div(m, tm, 8)

    def body(dh_ref, u_ref, g_ref, wout_ref, ya_ref, yb_ref, ga_ref, gb_ref, wpa_ref, wpb_ref,
             du_ref, dg_ref, db_ref, dya_ref, dyb_ref, dga_ref, dgb_ref, da_ref, dbb_ref):
        @pl.when(pl.program_id(0) == 0)
        def _():
            dg_ref[...] = jnp.zeros_like(dg_ref)
            db_ref[...] = jnp.zeros_like(db_ref)

        du, dg, db = _ln_bwd_block(dh_ref[...], u_ref[...], g_ref[...])
        du_ref[...] = du
        dg_ref[...] += dg
        db_ref[...] += db
        dm = lax.dot_general(du.astype(BF16), wout_ref[...], (((1,), (1,)), ((), ())), preferred_element_type=F32)
        for y_ref, gate_ref, w_ref, dy_ref, dgate_ref, dattn_ref in (
                (ya_ref, ga_ref, wpa_ref, dya_ref, dga_ref, da_ref), (yb_ref, gb_ref, wpb_ref, dyb_ref, dgb_ref, dbb_ref)):
            sg = jax.nn.sigmoid(gate_ref[...])
            dy = (dm * sg).astype(BF16)
            dy_ref[...] = dy
            dgate_ref[...] = (dm * y_ref[...].astype(F32) * sg * (1.0 - sg)).astype(BF16)
            dattn_ref[...] = lax.dot_general(dy, w_ref[...], (((1,), (1,)), ((), ())),
                                             preferred_element_type=F32).astype(BF16)

    row = pl.BlockSpec((tm, D_MODEL), lambda i: (i, 0))
    att = pl.BlockSpec((tm, SWA_Q), lambda i: (i, 0))
    vec = pl.BlockSpec((1, D_MODEL), lambda i: (0, 0))
    wsm = pl.BlockSpec((SWA_Q, D_MODEL), lambda i: (0, 0))
    wide = jax.ShapeDtypeStruct((m, D_MODEL), BF16)
    narrow = jax.ShapeDtypeStruct((m, SWA_Q), BF16)
    sums = jax.ShapeDtypeStruct((1, D_MODEL), F32)
    return pl.pallas_call(
        body, name="mixer_bwd", grid=(m // tm,),
        in_specs=[row, row, vec, pl.BlockSpec((D_MODEL, D_MODEL), lambda i: (0, 0)), row, row, row,
                  pl.BlockSpec((tm, D_MODEL), lambda i: (i, 1)), wsm, wsm],
        out_specs=[row, vec, vec, row, row, row, row, att, att],
        out_shape=[jax.ShapeDtypeStruct((m, D_MODEL), F32), sums, sums, wide, wide, wide, wide, narrow, narrow],
        compiler_params=_params("arbitrary"),
    )(dh1, u1, g, wout, ya, yb, zg, zg, wpa, wpb)


def _shift_down(x, k, halo, first):
    rows = lax.broadcasted_iota(jnp.int32, (x.shape[0], 1), 0)
    y = pltpu.roll(x, k, 0)
    for r in range(k):
        fill = jnp.where(first, 0.0, halo[8 - k + r:8 - k + r + 1, :])
        y = jnp.where(rows == r, fill, y)
    return y


def _shift_up(x, k, halo, last):
    n = x.shape[0]
    rows = lax.broadcasted_iota(jnp.int32, (n, 1), 0)
    y = pltpu.roll(x, n - k, 0)
    for r in range(k):
        fill = jnp.where(last, 0.0, halo[r:r + 1, :])
        y = jnp.where(rows == n - k + r, fill, y)
    return y


def _conv_act(gate, gate_m1, gate_m2, cw, cb):
    return cb + cw[0:1, :] * gate_m2 + cw[1:2, :] * gate_m1 + cw[2:3, :] * gate


def _ffn_in_conv(h1, wfi, cw, cb, tm=256):
    s_len = h1.shape[0]
    tm = _div(s_len, tm, 8)
    hb = tm // 8

    def body(a_ref, ap_ref, w_ref, cw_ref, cb_ref, gu_ref, act_ref):
        first = pl.program_id(0) == 0
        a = a_ref[...].astype(BF16)
        before = ap_ref[...].astype(BF16)
        for c in range(N_CHUNK):
            gate = jnp.dot(a, w_ref[c], preferred_element_type=F32)
            up = jnp.dot(a, w_ref[N_CHUNK + c], preferred_element_type=F32)
            halo = jnp.dot(before, w_ref[c], preferred_element_type=F32)
            gu_ref[c, 0] = gate
            gu_ref[c, 1] = up
            conv = _conv_act(gate, _shift_down(gate, 1, halo, first), _shift_down(gate, 2, halo, first),
                             cw_ref[c], cb_ref[c])
            act_ref[c] = (conv * jax.nn.sigmoid(conv) * up).astype(BF16)

    return pl.pallas_call(
        body, name="ffn_in_conv", grid=(s_len // tm,),
        in_specs=[pl.BlockSpec((tm, D_MODEL), lambda i: (i, 0)),
                  pl.BlockSpec((8, D_MODEL), lambda i: (jnp.maximum(i * hb - 1, 0), 0)),
                  pl.BlockSpec((N_DEV, D_MODEL, FF_CHUNK), lambda i: (0, 0, 0)),
                  pl.BlockSpec((N_CHUNK, 8, FF_CHUNK), lambda i: (0, 0, 0)),
                  pl.BlockSpec((N_CHUNK, 1, FF_CHUNK), lambda i: (0, 0, 0))],
        out_specs=[pl.BlockSpec((N_CHUNK, 2, tm, FF_CHUNK), lambda i: (0, 0, i, 0)),
                   pl.BlockSpec((N_CHUNK, tm, FF_CHUNK), lambda i: (0, i, 0))],
        out_shape=[jax.ShapeDtypeStruct((N_CHUNK, 2, s_len, FF_CHUNK), F32),
                   jax.ShapeDtypeStruct((N_CHUNK, s_len, FF_CHUNK), BF16)],
        compiler_params=_params("parallel"),
    )(h1, h1, wfi, cw, cb)


def _ffn_out_ln(act, wfo, res, g, b, tm=256):
    s_len = res.shape[0]
    tm = _div(s_len, tm, 8)

    def body(a_ref, w_ref, res_ref, g_ref, b_ref, u_ref, y_ref):
        u = ALPHA * res_ref[...]
        for c in range(N_CHUNK):
            u = u + jnp.dot(a_ref[c], w_ref[c], preferred_element_type=F32)
        u_ref[...] = u
        y_ref[...] = _ln(u, g_ref[...], b_ref[...])

    row = pl.BlockSpec((tm, D_MODEL), lambda i: (i, 0))
    vec = pl.BlockSpec((1, D_MODEL), lambda i: (0, 0))
    return pl.pallas_call(
        body, name="ffn_out_ln", grid=(s_len // tm,),
        in_specs=[pl.BlockSpec((N_CHUNK, tm, FF_CHUNK), lambda i: (0, i, 0)),
                  pl.BlockSpec((N_CHUNK, FF_CHUNK, D_MODEL), lambda i: (0, 0, 0)), row, vec, vec],
        out_specs=[row, row],
        out_shape=[jax.ShapeDtypeStruct((s_len, D_MODEL), F32), jax.ShapeDtypeStruct((s_len, D_MODEL), F32)],
        compiler_params=_params("parallel"),
    )(act, wfo, res, g, b)


def _ffn_out_bwd(dh2, u2, g, wfo, tm=256):
    s_len = dh2.shape[0]
    tm = _div(s_len, tm, 8)

    def body(dh_ref, u_ref, g_ref, w_ref, du_ref, dg_ref, db_ref, o_ref):
        @pl.when(pl.program_id(0) == 0)
        def _():
            dg_ref[...] = jnp.zeros_like(dg_ref)
            db_ref[...] = jnp.zeros_like(db_ref)

        du, dg, db = _ln_bwd_block(dh_ref[...], u_ref[...], g_ref[...])
        du_ref[...] = du
        dg_ref[...] += dg
        db_ref[...] += db
        du_b = du.astype(BF16)
        for c in range(N_CHUNK):
            o_ref[c] = lax.dot_general(du_b, w_ref[c], (((1,), (1,)), ((), ())), preferred_element_type=F32)

    row = pl.BlockSpec((tm, D_MODEL), lambda i: (i, 0))
    vec = pl.BlockSpec((1, D_MODEL), lambda i: (0, 0))
    sums = jax.ShapeDtypeStruct((1, D_MODEL), F32)
    return pl.pallas_call(
        body, name="ffn_out_bwd", grid=(s_len // tm,),
        in_specs=[row, row, vec, pl.BlockSpec((N_CHUNK, FF_CHUNK, D_MODEL), lambda i: (0, 0, 0))],
        out_specs=[row, vec, vec, pl.BlockSpec((N_CHUNK, tm, FF_CHUNK), lambda i: (0, i, 0))],
        out_shape=[jax.ShapeDtypeStruct((s_len, D_MODEL), F32), sums, sums,
                   jax.ShapeDtypeStruct((N_CHUNK, s_len, FF_CHUNK), F32)],
        compiler_params=_params("arbitrary"),
    )(dh2, u2, g, wfo)


def _g_w_ffn_out(act, du, tm=2048):
    s_len = du.shape[0]
    tm = _div(s_len, tm, 8)
    steps = s_len // tm

    def body(a_ref, g_ref, o_ref, acc_ref):
        s = pl.program_id(1)

        @pl.when(s == 0)
        def _():
            acc_ref[...] = jnp.zeros_like(acc_ref)

        acc_ref[...] += lax.dot_general(a_ref[0], g_ref[...].astype(BF16), (((0,), (0,)), ((), ())),
                                        preferred_element_type=F32)

        @pl.when(s == steps - 1)
        def _():
            o_ref[0] = acc_ref[...]

    return pl.pallas_call(
        body, name="g_w_ffn_out", grid=(N_CHUNK, steps),
        in_specs=[pl.BlockSpec((1, tm, FF_CHUNK), lambda c, s: (c, s, 0)),
                  pl.BlockSpec((tm, D_MODEL), lambda c, s: (s, 0))],
        out_specs=pl.BlockSpec((1, FF_CHUNK, D_MODEL), lambda c, s: (c, 0, 0)),
        out_shape=jax.ShapeDtypeStruct((N_CHUNK, FF_CHUNK, D_MODEL), F32),
        scratch_shapes=[pltpu.VMEM((FF_CHUNK, D_MODEL), F32)],
        compiler_params=_params("parallel", "arbitrary"),
    )(act, du)


def _g_w_ffn_in(h1, dgu, tm=2048):
    s_len = h1.shape[0]
    tm = _div(s_len, tm, 8)
    steps = s_len // tm

    def body(a_ref, g_ref, o_ref, acc_ref):
        s = pl.program_id(1)

        @pl.when(s == 0)
        def _():
            acc_ref[...] = jnp.zeros_like(acc_ref)

        acc_ref[...] += lax.dot_general(a_ref[...].astype(BF16), g_ref[0, 0], (((0,), (0,)), ((), ())),
                                        preferred_element_type=F32)

        @pl.when(s == steps - 1)
        def _():
            o_ref[0] = acc_ref[...]

    return pl.pallas_call(
        body, name="g_w_ffn_in", grid=(N_DEV, steps),
        in_specs=[pl.BlockSpec((tm, D_MODEL), lambda d, s: (s, 0)),
                  pl.BlockSpec((1, 1, tm, FF_CHUNK), lambda d, s: (d % N_CHUNK, d // N_CHUNK, s, 0))],
        out_specs=pl.BlockSpec((1, D_MODEL, FF_CHUNK), lambda d, s: (d, 0, 0)),
        out_shape=jax.ShapeDtypeStruct((N_DEV, D_MODEL, FF_CHUNK), F32),
        scratch_shapes=[pltpu.VMEM((D_MODEL, FF_CHUNK), F32)],
        compiler_params=_params("parallel", "arbitrary"),
    )(h1, dgu)


def _conv_bwd_dh1(gu, dact, cw, cb, wfi, res, tm=256):
    s_len = gu.shape[2]
    tm = _div(s_len, tm, 8)
    nrow = s_len // tm
    hb = tm // 8

    def dconv_of(conv, up, da):
        sg = jax.nn.sigmoid(conv)
        return da * up * (sg * (1.0 + conv * (1.0 - sg)))

    def body(gu_ref, gp_ref, gun_ref, da_ref, dan_ref, cw_ref, cb_ref, w_ref, res_ref, dgu_ref, dcw_ref, dh_ref):
        i = pl.program_id(0)
        first = i == 0
        last = i == nrow - 1

        @pl.when(first)
        def _():
            dcw_ref[...] = jnp.zeros_like(dcw_ref)

        row = lax.broadcasted_iota(jnp.int32, (8, 1), 0)
        acc = ALPHA * res_ref[...]
        for c in range(N_CHUNK):
            cw = cw_ref[c]
            cb = cb_ref[c]
            gate = gu_ref[c, 0]
            halo = gp_ref[c, 0]
            g_m1 = _shift_down(gate, 1, halo, first)
            g_m2 = _shift_down(gate, 2, halo, first)
            conv = _conv_act(gate, g_m1, g_m2, cw, cb)
            da = da_ref[c]
            dup = (da * conv * jax.nn.sigmoid(conv)).astype(BF16)
            dconv = dconv_of(conv, gu_ref[c, 1], da)
            gate_n = gun_ref[c, 0]
            tail = gate[tm - 8:, :]
            conv_n = _conv_act(gate_n, _shift_down(gate_n, 1, tail, False), _shift_down(gate_n, 2, tail, False),
                               cw, cb)
            dconv_n = dconv_of(conv_n, gun_ref[c, 1], dan_ref[c])
            dgate = (cw[2:3, :] * dconv + cw[1:2, :] * _shift_up(dconv, 1, dconv_n, last)
                     + cw[0:1, :] * _shift_up(dconv, 2, dconv_n, last)).astype(BF16)
            dgu_ref[c, 0] = dgate
            dgu_ref[c, 1] = dup
            acc = acc + lax.dot_general(dgate, w_ref[c], NT, preferred_element_type=F32)
            acc = acc + lax.dot_general(dup, w_ref[N_CHUNK + c], NT, preferred_element_type=F32)
            part = jnp.zeros((8, FF_CHUNK), F32)
            for r, term in enumerate((dconv * g_m2, dconv * g_m1, dconv * gate, dconv)):
                part = jnp.where(row == r, jnp.sum(term, axis=0, keepdims=True), part)
            dcw_ref[c] += part
        dh_ref[...] = acc

    nxt = lambda i: jnp.minimum((i + 1) * hb, s_len // 8 - 1)
    main = pl.BlockSpec((N_CHUNK, 2, tm, FF_CHUNK), lambda i: (0, 0, i, 0))
    row_d = pl.BlockSpec((tm, D_MODEL), lambda i: (i, 0))
    return pl.pallas_call(
        body, name="conv_bwd_dh1", grid=(nrow,),
        in_specs=[main,
                  pl.BlockSpec((N_CHUNK, 1, 8, FF_CHUNK), lambda i: (0, 0, jnp.maximum(i * hb - 1, 0), 0)),
                  pl.BlockSpec((N_CHUNK, 2, 8, FF_CHUNK), lambda i: (0, 0, nxt(i), 0)),
                  pl.BlockSpec((N_CHUNK, tm, FF_CHUNK), lambda i: (0, i, 0)),
                  pl.BlockSpec((N_CHUNK, 8, FF_CHUNK), lambda i: (0, nxt(i), 0)),
                  pl.BlockSpec((N_CHUNK, 8, FF_CHUNK), lambda i: (0, 0, 0)),
                  pl.BlockSpec((N_CHUNK, 1, FF_CHUNK), lambda i: (0, 0, 0)),
                  pl.BlockSpec((N_DEV, D_MODEL, FF_CHUNK), lambda i: (0, 0, 0)), row_d],
        out_specs=[main, pl.BlockSpec((N_CHUNK, 8, FF_CHUNK), lambda i: (0, 0, 0)), row_d],
        out_shape=[jax.ShapeDtypeStruct((N_CHUNK, 2, s_len, FF_CHUNK), BF16),
                   jax.ShapeDtypeStruct((N_CHUNK, 8, FF_CHUNK), F32),
                   jax.ShapeDtypeStruct((s_len, D_MODEL), F32)],
        compiler_params=_params("arbitrary"),
    )(gu, gu, gu, dact, dact, cw, cb, wfi, res)


def _loss_head(y, target, tm=256):
    m, d = y.shape
    tm = _div(m, tm, 8)

    def body(y_ref, t_ref, dy_ref, loss_ref):
        @pl.when(pl.program_id(0) == 0)
        def _():
            loss_ref[...] = jnp.zeros_like(loss_ref)

        err = y_ref[...] - t_ref[...]
        dy_ref[...] = err / d
        loss_ref[...] += 0.5 * jnp.sum(jnp.sum(err * err, axis=1, keepdims=True) / d, axis=0, keepdims=True)

    row = pl.BlockSpec((tm, d), lambda i: (i, 0))
    return pl.pallas_call(
        body, name="loss_head", grid=(m // tm,), in_specs=[row, row],
        out_specs=[row, pl.BlockSpec((8, LANE), lambda i: (0, 0))],
        out_shape=[jax.ShapeDtypeStruct((m, d), F32), jax.ShapeDtypeStruct((8, LANE), F32)],
        compiler_params=_params("arbitrary"),
    )(y, target)


def _sum_devices(r_ref):
    acc = r_ref[0].astype(F32)
    for d in range(1, N_DEV):
        acc = acc + r_ref[d].astype(F32)
    return acc


def _sum8(recv):
    rows = recv.shape[1]
    tr = _div(rows, ROW_BLOCK, 8)

    def body(r_ref, o_ref):
        o_ref[...] = _sum_devices(r_ref)

    return pl.pallas_call(
        body, name="sum8", grid=(rows // tr,),
        in_specs=[pl.BlockSpec((N_DEV, tr, LANE), lambda i: (0, i, 0))],
        out_specs=pl.BlockSpec((tr, LANE), lambda i: (i, 0)),
        out_shape=jax.ShapeDtypeStruct((rows, LANE), F32),
        compiler_params=_params("parallel"),
    )(recv)


def _adamw_math(w, g, m, v):
    m = ADAM_B1 * m + (1.0 - ADAM_B1) * g
    v = ADAM_B2 * v + (1.0 - ADAM_B2) * (g * g)
    m_hat = m / (1.0 - ADAM_B1 ** ADAM_STEP)
    v_hat = v / (1.0 - ADAM_B2 ** ADAM_STEP)
    return -ADAM_LR * (m_hat / (jnp.sqrt(v_hat) + ADAM_EPS) + ADAM_WD * w), m, v


def _adamw_rows(w, g, m, v, name):
    rows = w.shape[0]
    tr = _div(rows, ROW_BLOCK, 8)

    def body(w_ref, g_ref, m_ref, v_ref, d_ref, mo_ref, vo_ref):
        d_ref[...], mo_ref[...], vo_ref[...] = _adamw_math(w_ref[...], g_ref[...], m_ref[...], v_ref[...])

    blk = pl.BlockSpec((tr, LANE), lambda i: (i, 0))
    out = jax.ShapeDtypeStruct((rows, LANE), F32)
    return pl.pallas_call(
        body, name=name, grid=(rows // tr,), in_specs=[blk, blk, blk, blk], out_specs=[blk, blk, blk],
        out_shape=[out, out, out], compiler_params=_params("parallel"),
    )(w, g, m, v)


def _adamw_shard(recv, w, m, v, layer, prev, name):
    _, k, n = recv.shape
    tk = _div(k, 128, 16)

    def body(r_ref, w_ref, m_ref, v_ref, *rest):
        g_ref, d_ref, mo_ref, vo_ref = rest[-4:]
        g = _sum_devices(r_ref)
        g_ref[0] = g
        d_ref[0], mo_ref[0], vo_ref[0] = _adamw_math(w_ref[0], g, m_ref[0], v_ref[0])

    blk = pl.BlockSpec((1, tk, n), lambda i: (layer, i, 0))
    out = jax.ShapeDtypeStruct((DEPTH, k, n), F32)
    carried = [] if prev is None else list(prev)
    return pl.pallas_call(
        body, name=name, grid=(k // tk,),
        in_specs=[pl.BlockSpec((N_DEV, tk, n), lambda i: (0, i, 0)), blk, blk, blk]
        + [pl.BlockSpec(memory_space=pl.ANY)] * len(carried),
        out_specs=[blk, blk, blk, blk], out_shape=[out, out, out, out],
        input_output_aliases={4 + j: j for j in range(len(carried))},
        compiler_params=_params("parallel"),
    )(recv, w, m, v, *carried)


def _to_rows(flat, rows):
    flat = flat.reshape(-1)
    return jnp.pad(flat, (0, rows * LANE - flat.shape[0])).reshape(rows, LANE)


def _pad_cols_z(a):
    f0 = N_QKV
    g0 = N_QKV + FOX_HEADS
    pad = jnp.zeros(a.shape[:-1] + (F_PAD - FOX_HEADS,), a.dtype)
    return jnp.concatenate([a[..., :f0], a[..., g0:], a[..., f0:g0], pad], axis=-1)


def _unpad_cols_z(a):
    f0 = N_QKV + N_GATE
    return jnp.concatenate([a[..., :N_QKV], a[..., f0:f0 + FOX_HEADS], a[..., N_QKV:f0]], axis=-1)


def _shards_to_cols(g):
    _, k, n = g.shape
    return g.transpose(1, 0, 2).reshape(k, N_DEV * n)


def _cols_to_shards(full):
    k, n = full.shape
    return full.reshape(k, N_DEV, n // N_DEV).transpose(1, 0, 2)


def _layer_fwd(h, w, p, comm=None, late=None):
    zq, zg = _z_proj(h, w["w_in_p"], p["b_in_p"])
    qx, kx, vx = _fox_prep(zq, _cumsum_logf(zg))
    attn_a, lse_a = _swa_fwd(zq, p["sinks"])
    attn_b, attn_b32, lse_b, arrived = _fox_fwd(qx, kx, vx, comm)
    if late is not None:
        w, p = late(w, p, arrived)
    h1, u1, merged, ya, yb = _mixer_out(attn_a, attn_b, zg, h, w["w_proj_a"], w["w_proj_b"], w["w_out"],
                                        p["ln_mix_g"], p["ln_mix_b"])
    gu, act = _ffn_in_conv(h1, w["w_ffn_in"], p["conv_w"], p["conv_b"])
    u2, h2 = _ffn_out_ln(act, w["w_ffn_out"], h1, p["ln_ffn_g"], p["ln_ffn_b"])
    saved = dict(h=h, zq=zq, zg=zg, qx=qx, kx=kx, vx=vx, attn_a=attn_a, lse_a=lse_a, attn_b=attn_b,
                 attn_b32=attn_b32, lse_b=lse_b, h1=h1, u1=u1, merged=merged, ya=ya, yb=yb, gu=gu, act=act, u2=u2)
    return h2, saved, w, p


def _layer_bwd(dh2, sv, w, p, make_comm=None):
    s_len = dh2.shape[0]
    du2, d_ffn_g, d_ffn_b, dact = _ffn_out_bwd(dh2, sv["u2"], p["ln_ffn_g"], w["w_ffn_out"])
    g_ffn_out = _g_w_ffn_out(sv["act"], du2)
    dgu, dcw, dh1 = _conv_bwd_dh1(sv["gu"], dact, p["conv_w"], p["conv_b"], w["w_ffn_in"], du2)
    dcw = dcw.transpose(1, 0, 2).reshape(8, D_FF)
    g_ffn_in = _g_w_ffn_in(sv["h1"], dgu)
    du1, d_mix_g, d_mix_b, dya, dyb, dga, dgb, dattn_a, dattn_b = _mixer_bwd(
        dh1, sv["u1"], p["ln_mix_g"], w["w_out"], sv["ya"], sv["yb"], sv["zg"], w["w_proj_a"], w["w_proj_b"])
    g_out = _linear_tn(sv["merged"], du1, name="g_w_out", tn=1024)
    g_proj_a = _linear_tn(sv["attn_a"], dya, name="g_w_proj_a", tk=512, tn=1024)
    g_proj_b = _linear_tn(sv["attn_b"], dyb, name="g_w_proj_b", tk=512, tn=1024)
    dq_a, dk_a, dv_a, dsinks = _swa_bwd(sv["zq"], p["sinks"], sv["attn_a"], dattn_a, sv["lse_a"])
    big = dict(w_proj_a=_cols_to_shards(g_proj_a), w_proj_b=_cols_to_shards(g_proj_b),
               w_out=g_out.reshape(N_DEV, D_MODEL // N_DEV, D_MODEL), w_ffn_in=g_ffn_in,
               w_ffn_out=g_ffn_out.reshape(N_DEV, D_FF // N_DEV, D_MODEL))
    dox, stats = _fox_stats(sv["attn_b32"], dattn_b, sv["lse_b"])
    dq_b, dk_b, dv_b, dcc, arrived = _fox_bwd(sv["qx"], sv["kx"], sv["vx"], dox, stats,
                                              None if make_comm is None else make_comm(big))
    dc = jnp.pad(dcc.reshape(s_len, N_PAIR, LANE)[:, :, :2].reshape(s_len, FOX_HEADS), ((0, 0), (0, LANE - FOX_HEADS)))
    df = _forget_bwd(dc, sv["zg"])
    dz = jnp.concatenate([dq_a, dk_a.astype(BF16), dv_a.astype(BF16), dq_b.astype(BF16), dk_b, dv_b, dga, dgb, df,
                          jnp.zeros((s_len, F_PAD - LANE), BF16)], axis=1)
    dh = _linear(dz, w["w_in_p"], trans_b=True, res=du1, res_scale=ALPHA, name="d_h", tn=D_MODEL)
    g_in = _unpad_cols_z(_linear_tn(sv["h"], dz, name="g_w_in", tn=768))
    g_b_in = _unpad_cols_z(_colsum(dz, name="g_b_in"))
    big["w_in"] = _cols_to_shards(g_in)
    small = dict(ln_mix_g=d_mix_g, ln_mix_b=d_mix_b, b_in=g_b_in, attn_sinks=dsinks[:, :SWA_HEADS],
                 ln_ffn_g=d_ffn_g, ln_ffn_b=d_ffn_b, conv_w=dcw[:3], conv_b=dcw[3:4])
    return dh, big, small, arrived


def _w_in_layouts(w_in):
    return dict(w_in_p=_pad_cols_z(_shards_to_cols(w_in)))


def _other_layouts(w_proj_a, w_proj_b, w_out, w_ffn_in, w_ffn_out):
    return dict(w_proj_a=_shards_to_cols(w_proj_a), w_proj_b=_shards_to_cols(w_proj_b),
                w_out=w_out.reshape(D_MODEL, D_MODEL), w_ffn_in=w_ffn_in,
                w_ffn_out=w_ffn_out.reshape(N_CHUNK, FF_CHUNK, D_MODEL))


def _layer_params(r):
    return dict(
        b_in_p=_pad_cols_z(r["b_in"].reshape(1, N_IN)),
        sinks=jnp.pad(r["attn_sinks"].reshape(1, SWA_HEADS), ((0, 0), (0, LANE - SWA_HEADS))),
        ln_mix_g=r["ln_mix_g"].reshape(1, D_MODEL), ln_mix_b=r["ln_mix_b"].reshape(1, D_MODEL),
        ln_ffn_g=r["ln_ffn_g"].reshape(1, D_MODEL), ln_ffn_b=r["ln_ffn_b"].reshape(1, D_MODEL),
        conv_b=r["conv_b"].reshape(N_CHUNK, 1, FF_CHUNK))


def _conv_w_layout(conv_w):
    return jnp.pad(conv_w, ((0, 5), (0, 0))).reshape(8, N_CHUNK, FF_CHUNK).transpose(1, 0, 2)


def kernel(x, ln_mix_g, ln_mix_b, w_in, b_in, attn_sinks, w_proj_a, w_proj_b, w_out, ln_ffn_g, ln_ffn_b, w_ffn_in, conv_w, conv_b, w_ffn_out, loss_target, m_ln_mix_g, m_ln_mix_b, m_w_in, m_b_in, m_attn_sinks, m_w_proj_a, m_w_proj_b, m_w_out, m_ln_ffn_g, m_ln_ffn_b, m_w_ffn_in, m_conv_w, m_conv_b, m_w_ffn_out, v_ln_mix_g, v_ln_mix_b, v_w_in, v_b_in, v_attn_sinks, v_w_proj_a, v_w_proj_b, v_w_out, v_ln_ffn_g, v_ln_ffn_b, v_w_ffn_in, v_conv_w, v_conv_b, v_w_ffn_out):
    wts = dict(ln_mix_g=ln_mix_g, ln_mix_b=ln_mix_b, w_in=w_in, b_in=b_in, attn_sinks=attn_sinks, w_proj_a=w_proj_a,
               w_proj_b=w_proj_b, w_out=w_out, ln_ffn_g=ln_ffn_g, ln_ffn_b=ln_ffn_b, w_ffn_in=w_ffn_in,
               conv_w=conv_w, conv_b=conv_b, w_ffn_out=w_ffn_out)
    mom = dict(ln_mix_g=m_ln_mix_g, ln_mix_b=m_ln_mix_b, w_in=m_w_in, b_in=m_b_in, attn_sinks=m_attn_sinks,
               w_proj_a=m_w_proj_a, w_proj_b=m_w_proj_b, w_out=m_w_out, ln_ffn_g=m_ln_ffn_g, ln_ffn_b=m_ln_ffn_b,
               w_ffn_in=m_w_ffn_in, conv_w=m_conv_w, conv_b=m_conv_b, w_ffn_out=m_w_ffn_out)
    vel = dict(ln_mix_g=v_ln_mix_g, ln_mix_b=v_ln_mix_b, w_in=v_w_in, b_in=v_b_in, attn_sinks=v_attn_sinks,
               w_proj_a=v_w_proj_a, w_proj_b=v_w_proj_b, w_out=v_w_out, ln_ffn_g=v_ln_ffn_g, ln_ffn_b=v_ln_ffn_b,
               w_ffn_in=v_w_ffn_in, conv_w=v_conv_w, conv_b=v_conv_b, w_ffn_out=v_w_ffn_out)
    names = list(wts)
    big_names = [n for n, _, _ in BIG]
    small_names = [n for n, _ in SMALL]
    me = 4 * lax.axis_index("x") + 2 * lax.axis_index("y") + lax.axis_index("c")
    cw_shard = D_FF // N_DEV

    wb = {n: wts[n].astype(BF16) for n in big_names}
    ps = [_layer_params(dict(b_in=b_in[l], attn_sinks=attn_sinks[l], ln_mix_g=ln_mix_g[l], ln_mix_b=ln_mix_b[l],
                             ln_ffn_g=ln_ffn_g[l], ln_ffn_b=ln_ffn_b[l], conv_b=conv_b[l])) for l in range(DEPTH)]
    w_in_0, = _exchange([(wb["w_in"][0], True)], "gather_w_in_0")
    others = big_names[1:]
    gather_rest = _Comm([(wb[n][0], True) for n in others] + [(wb[n][1], True) for n in big_names] + [(conv_w, True)])
    next_layer = {}

    def late(w, p, arrived):
        conv_full = arrived[-1].transpose(1, 2, 0, 3).reshape(DEPTH, 3, D_FF)
        layer_1 = arrived[len(others):-1]
        next_layer["w"] = dict(_w_in_layouts(layer_1[0]), **_other_layouts(*layer_1[1:]))
        next_layer["p"] = dict(ps[1], conv_w=_conv_w_layout(conv_full[1]))
        return dict(w, **_other_layouts(*arrived[:len(others)])), dict(p, conv_w=_conv_w_layout(conv_full[0]))

    saved, ws = [None] * DEPTH, [None] * DEPTH
    h, saved[0], ws[0], ps[0] = _layer_fwd(x[0], _w_in_layouts(w_in_0), ps[0], gather_rest, late)
    h, saved[1], ws[1], ps[1] = _layer_fwd(h, next_layer["w"], next_layer["p"])
    dh, loss_part = _loss_head(h, loss_target[0])

    def small_rows(small):
        vec = jnp.concatenate([small[n].reshape(-1) for n in small_names] + [loss_part[0, 0].reshape(1)])
        return _to_rows(vec, SMALL_LAYER_ROWS)

    dh, big_1, small_1, _ = _layer_bwd(dh, saved[1], ws[1], ps[1])

    def exchange_early(big_0):
        return _Comm([(big_1[n].astype(BF16), False) for n in big_names] + [(small_rows(small_1), True)]
                     + [(big_0[n].astype(BF16), False) for n in others])

    grad_x, big_0, small_0, arrived = _layer_bwd(dh, saved[0], ws[0], ps[0], exchange_early)
    g_in_0, g_small_0 = _exchange([(big_0["w_in"].astype(BF16), False), (small_rows(small_0), True)],
                                  "exchange_grads_0")
    n_big = len(big_names)
    recv = [[g_in_0] + list(arrived[n_big + 1:]) + [g_small_0], list(arrived[:n_big + 1])]

    big_out = {}
    for t, n in enumerate(big_names):
        outs = None
        for l in reversed(range(DEPTH)):
            outs = _adamw_shard(recv[l][t], wts[n], mom[n], vel[n], l, outs, "adamw_%s_%d" % (n, l))
        big_out[n] = outs
    small_sum = [_sum8(recv[l][-1]).reshape(-1) for l in range(DEPTH)]
    g_small = {}
    off = 0
    for n, size in SMALL:
        g_small[n] = jnp.stack([small_sum[l][off:off + size] for l in range(DEPTH)])
        off += size
    loss = small_sum[0][off]
    g_small["conv_w"] = lax.dynamic_slice_in_dim(g_small["conv_w"].reshape(DEPTH, 3, D_FF), me * cw_shard, cw_shard,
                                                 axis=2)
    g_small = {n: g_small[n].reshape(wts[n].shape) for n in small_names}

    def pack_small(tree):
        return _to_rows(jnp.concatenate([tree[n].reshape(-1) for n in small_names]), SMALL_ROWS)

    small_out = (pack_small(g_small),) + tuple(_adamw_rows(pack_small(wts), pack_small(g_small), pack_small(mom),
                                                           pack_small(vel), "adamw_small"))

    def result(j):
        out = {n: big_out[n][j] for n in big_names}
        flat = small_out[j].reshape(-1)
        off = 0
        for n in small_names:
            out[n] = flat[off:off + wts[n].size].reshape(wts[n].shape)
            off += wts[n].size
        return [out[n] for n in names]

    return (loss, grad_x[None], *result(0), *result(1), *result(2), *result(3))
```

```python
import functools

import jax
import jax.numpy as jnp
import numpy as np
from jax import lax
from jax.experimental import pallas as pl
from jax.experimental.pallas import tpu as pltpu

F32 = jnp.float32
BF16 = jnp.bfloat16
MESH = pl.DeviceIdType.MESH

N_DEV = 8
DEPTH = 2
D_MODEL = 1024
HEAD_DIM = 64
SWA_Q = 512
SWA_KV = 128
FOX_W = 512
FOX_HEADS = 8
SWA_HEADS = 8
D_FF = 2816
N_IN = 4360
N_QKV = SWA_Q + 2 * SWA_KV + 3 * FOX_W
N_GATE = 2 * D_MODEL
F_PAD = 256
N_ZG = N_GATE + F_PAD
N_ZP = N_QKV + N_ZG
LN_EPS = 1e-5
NEG_INF = -1e30
ALPHA = (2 * DEPTH) ** 0.25
SCALE = HEAD_DIM ** -0.5
SLOPES = tuple(2.0 ** (-8.0 * (h + 1) / SWA_HEADS) for h in range(SWA_HEADS))

ADAM_LR = 0.001
ADAM_B1 = 0.9
ADAM_B2 = 0.999
ADAM_EPS = 1e-08
ADAM_WD = 0.01
ADAM_STEP = 10

LANE = 128
VMEM_LIMIT = 56 * 1024 * 1024

BIG = (("w_in", (D_MODEL, N_IN), 1), ("w_proj_a", (SWA_Q, D_MODEL), 1), ("w_proj_b", (FOX_W, D_MODEL), 1),
       ("w_out", (D_MODEL, D_MODEL), 0), ("w_ffn_in", (D_MODEL, 2 * D_FF), 1), ("w_ffn_out", (D_FF, D_MODEL), 0))
SMALL = (("ln_mix_g", D_MODEL), ("ln_mix_b", D_MODEL), ("b_in", N_IN), ("attn_sinks", SWA_HEADS),
         ("ln_ffn_g", D_MODEL), ("ln_ffn_b", D_MODEL), ("conv_w", 3 * D_FF), ("conv_b", D_FF))
ROW_BLOCK = 512
SMALL_LAYER_ROWS = -(-(sum(n for _, n in SMALL) + 1) // (8 * LANE)) * 8
SMALL_ROWS = ROW_BLOCK
FF_CHUNK = 2 * D_FF // N_DEV
N_CHUNK = D_FF // FF_CHUNK


def _div(n, cap, unit):
    if n <= cap:
        return n
    best = None
    for t in range(unit, cap + 1, unit):
        if n % t == 0:
            best = t
    assert best is not None, (n, cap, unit)
    return best


def _params(*sem):
    return pltpu.CompilerParams(dimension_semantics=sem, vmem_limit_bytes=VMEM_LIMIT)


def _peer(r):
    x, y, c = lax.axis_index("x"), lax.axis_index("y"), lax.axis_index("c")
    px = 1 - x if (r >> 2) & 1 else x
    py = 1 - y if (r >> 1) & 1 else y
    pc = 1 - c if r & 1 else c
    return (px, py, pc), 4 * px + 2 * py + pc


class _Comm:
    def __init__(self, tensors):
        self.arrays = [x for x, _ in tensors]
        self.gathers = [g for _, g in tensors]
        self.n = len(tensors)
        self.out_shape = [jax.ShapeDtypeStruct((N_DEV,) + (x.shape if g else x.shape[1:]), x.dtype)
                          for x, g in tensors]
        self.specs = [pl.BlockSpec(memory_space=pl.ANY)] * self.n
        self.scratch = [pltpu.SemaphoreType.DMA((N_DEV - 1, self.n)), pltpu.SemaphoreType.DMA((N_DEV - 1, self.n)),
                        pltpu.SemaphoreType.DMA((self.n,))]

    def _copies(self, x_refs, out_refs, sems):
        send_sems, recv_sems, local_sems = sems
        _, me = _peer(0)

        def src(t, idx):
            return x_refs[t] if self.gathers[t] else x_refs[t].at[idx]

        def remote(r, t, mine):
            peer, pid = _peer(r)
            return pltpu.make_async_remote_copy(src_ref=src(t, pid), dst_ref=out_refs[t].at[me if mine else pid],
                                                send_sem=send_sems.at[r - 1, t], recv_sem=recv_sems.at[r - 1, t],
                                                device_id=peer, device_id_type=MESH)

        pairs = [(r, t) for r in range(1, N_DEV) for t in range(self.n)]
        local = [pltpu.make_async_copy(src(t, me), out_refs[t].at[me], local_sems.at[t]) for t in range(self.n)]
        return local, [remote(r, t, True) for r, t in pairs], lambda: [remote(r, t, False) for r, t in pairs]

    def start(self, x_refs, out_refs, sems):
        local, sent, _ = self._copies(x_refs, out_refs, sems)
        for cp in local + sent:
            cp.start()

    def wait(self, x_refs, out_refs, sems):
        local, sent, landing = self._copies(x_refs, out_refs, sems)
        for cp in landing():
            cp.wait_recv()
        for cp in sent:
            cp.wait_send()
        for cp in local:
            cp.wait()


def _exchange(tensors, name):
    comm = _Comm(tensors)
    n = comm.n

    def body(*refs):
        comm.start(refs[:n], refs[n:2 * n], refs[2 * n:])
        comm.wait(refs[:n], refs[n:2 * n], refs[2 * n:])

    return pl.pallas_call(body, name=name, out_shape=comm.out_shape, in_specs=comm.specs, out_specs=comm.specs,
                          scratch_shapes=comm.scratch)(*comm.arrays)


def _with_comm(comm, n_in, n_out, first, last, compute):
    nc = comm.n if comm is not None else 0

    def body(*refs):
        ins, x_refs = refs[:n_in], refs[n_in:n_in + nc]
        outs = refs[n_in + nc:n_in + nc + n_out]
        out_refs = refs[n_in + nc + n_out:n_in + 2 * nc + n_out]
        sems = refs[n_in + 2 * nc + n_out:]
        if nc:
            @pl.when(first())
            def _():
                comm.start(x_refs, out_refs, sems)

        compute(*ins, *outs)
        if nc:
            @pl.when(last())
            def _():
                comm.wait(x_refs, out_refs, sems)

    return body


def _d_h(dz, w_in_p, res, comm=None, tm=512):
    m, k = dz.shape
    d = w_in_p.shape[0]
    tm = _div(m, tm, 8)
    steps = m // tm
    c_specs, c_shapes, c_scratch, c_arrays = _comm_parts(comm)

    def compute(dz_ref, w_ref, res_ref, o_ref):
        o_ref[...] = ALPHA * res_ref[...] + lax.dot_general(dz_ref[...], w_ref[...], (((1,), (1,)), ((), ())),
                                                            preferred_element_type=F32)

    body = _with_comm(comm, 3, 1, lambda: pl.program_id(0) == 0, lambda: pl.program_id(0) == steps - 1, compute)
    row = pl.BlockSpec((tm, d), lambda i: (i, 0))
    outs = pl.pallas_call(
        body, name="d_h" if comm is None else "d_h_comm", grid=(steps,),
        in_specs=[pl.BlockSpec((tm, k), lambda i: (i, 0)), pl.BlockSpec((d, k), lambda i: (0, 0)), row] + c_specs,
        out_specs=[row] + c_specs, out_shape=[jax.ShapeDtypeStruct((m, d), F32)] + c_shapes,
        scratch_shapes=c_scratch,
        compiler_params=_params("arbitrary"),
    )(dz, w_in_p, res, *c_arrays)
    return outs[0], outs[1:]


def _z_proj(h, w_in_p, b_p, tm=512):
    m, k = h.shape
    tm = _div(m, tm, 8)

    def body(h_ref, w_ref, b_ref, zq_ref, zg_ref):
        a = h_ref[...].astype(BF16)
        zq_ref[...] = (jnp.dot(a, w_ref[:, :N_QKV], preferred_element_type=F32) + b_ref[:, :N_QKV]).astype(BF16)
        zg_ref[...] = jnp.dot(a, w_ref[:, N_QKV:], preferred_element_type=F32) + b_ref[:, N_QKV:]

    return pl.pallas_call(
        body, name="z_proj", grid=(m // tm,),
        in_specs=[pl.BlockSpec((tm, k), lambda i: (i, 0)), pl.BlockSpec((k, N_ZP), lambda i: (0, 0)),
                  pl.BlockSpec((1, N_ZP), lambda i: (0, 0))],
        out_specs=[pl.BlockSpec((tm, N_QKV), lambda i: (i, 0)), pl.BlockSpec((tm, N_ZG), lambda i: (i, 0))],
        out_shape=[jax.ShapeDtypeStruct((m, N_QKV), BF16), jax.ShapeDtypeStruct((m, N_ZG), F32)],
        compiler_params=_params("parallel"),
    )(h, w_in_p, b_p)


def _linear_tn(a, g, *, name, tk=1024, tn=640, tm=2048):
    m, k = a.shape
    n = g.shape[1]
    tk = _div(k, tk, LANE)
    tn = _div(n, tn, LANE)
    tm = _div(m, tm, 8)
    steps = m // tm

    def body(a_ref, g_ref, o_ref, acc_ref):
        s = pl.program_id(2)

        @pl.when(s == 0)
        def _():
            acc_ref[...] = jnp.zeros_like(acc_ref)

        acc_ref[...] += lax.dot_general(a_ref[...].astype(BF16), g_ref[...].astype(BF16), (((0,), (0,)), ((), ())),
                                        preferred_element_type=F32)

        @pl.when(s == steps - 1)
        def _():
            o_ref[...] = acc_ref[...].astype(BF16)

    return pl.pallas_call(
        body, name=name, grid=(k // tk, n // tn, steps),
        in_specs=[pl.BlockSpec((tm, tk), lambda i, j, s: (s, i)), pl.BlockSpec((tm, tn), lambda i, j, s: (s, j))],
        out_specs=pl.BlockSpec((tk, tn), lambda i, j, s: (i, j)),
        out_shape=jax.ShapeDtypeStruct((k, n), BF16),
        scratch_shapes=[pltpu.VMEM((tk, tn), F32)],
        compiler_params=_params("parallel", "parallel", "arbitrary"),
    )(a, g)


def _colsum(g, *, name, tm=512):
    m, n = g.shape
    tm = _div(m, tm, 8)

    def body(g_ref, o_ref):
        @pl.when(pl.program_id(0) == 0)
        def _():
            o_ref[...] = jnp.zeros_like(o_ref)

        o_ref[...] += jnp.sum(g_ref[...].astype(F32), axis=0, keepdims=True)

    return pl.pallas_call(
        body, name=name, grid=(m // tm,),
        in_specs=[pl.BlockSpec((tm, n), lambda i: (i, 0))],
        out_specs=pl.BlockSpec((1, n), lambda i: (0, 0)),
        out_shape=jax.ShapeDtypeStruct((1, n), F32),
        compiler_params=_params("arbitrary"),
    )(g)


def _ln(u, g, b):
    mu = jnp.mean(u, axis=-1, keepdims=True)
    d = u - mu
    var = jnp.mean(d * d, axis=-1, keepdims=True)
    return d * lax.rsqrt(var + LN_EPS) * g + b


def _ln_bwd_block(dy, u, g):
    mu = jnp.mean(u, axis=-1, keepdims=True)
    dd = u - mu
    rstd = lax.rsqrt(jnp.mean(dd * dd, axis=-1, keepdims=True) + LN_EPS)
    xhat = dd * rstd
    dxh = dy * g
    m1 = jnp.mean(dxh, axis=-1, keepdims=True)
    m2 = jnp.mean(dxh * xhat, axis=-1, keepdims=True)
    return (rstd * (dxh - m1 - xhat * m2), jnp.sum(dy * xhat, axis=0, keepdims=True),
            jnp.sum(dy, axis=0, keepdims=True))


SCAN_ROWS = 512


def _tri(n, upper):
    r = lax.broadcasted_iota(jnp.int32, (n, n), 0)
    c = lax.broadcasted_iota(jnp.int32, (n, n), 1)
    return jnp.where((c >= r) if upper else (c <= r), 1.0, 0.0).astype(F32)


def _cumsum_logf(zg):
    s = zg.shape[0]
    t = _div(s, SCAN_ROWS, LANE)
    nb = s // t
    fcol = N_GATE // LANE

    def body(f_ref, c_ref, carry_ref):
        @pl.when(pl.program_id(0) == 0)
        def _():
            carry_ref[...] = jnp.zeros_like(carry_ref)

        f = f_ref[...]
        logf = jnp.minimum(f, 0.0) - jnp.log(1.0 + jnp.exp(-jnp.abs(f)))
        c = jnp.dot(_tri(t, False), logf, precision=lax.Precision.HIGHEST, preferred_element_type=F32)
        c = c + carry_ref[0:1, :]
        c_ref[...] = c
        carry_ref[...] = jnp.broadcast_to(c[t - 1:t, :], carry_ref.shape)

    return pl.pallas_call(
        body, name="cumsum_logf", grid=(nb,),
        in_specs=[pl.BlockSpec((t, LANE), lambda i: (i, fcol))],
        out_specs=pl.BlockSpec((t, LANE), lambda i: (i, 0)),
        out_shape=jax.ShapeDtypeStruct((s, LANE), F32),
        scratch_shapes=[pltpu.VMEM((8, LANE), F32)],
        compiler_params=_params("arbitrary"),
    )(zg)


def _forget_bwd(dc, zg):
    s = zg.shape[0]
    t = _div(s, SCAN_ROWS, LANE)
    nb = s // t
    fcol = N_GATE // LANE

    def body(dc_ref, f_ref, o_ref, carry_ref):
        @pl.when(pl.program_id(0) == 0)
        def _():
            carry_ref[...] = jnp.zeros_like(carry_ref)

        dc = dc_ref[...]
        dlogf = jnp.dot(_tri(t, True), dc, precision=lax.Precision.HIGHEST, preferred_element_type=F32)
        dlogf = dlogf + carry_ref[0:1, :]
        o_ref[...] = (dlogf * jax.nn.sigmoid(-f_ref[...])).astype(BF16)
        carry_ref[...] = jnp.broadcast_to(dlogf[0:1, :], carry_ref.shape)

    return pl.pallas_call(
        body, name="forget_bwd", grid=(nb,),
        in_specs=[pl.BlockSpec((t, LANE), lambda i: (nb - 1 - i, 0)),
                  pl.BlockSpec((t, LANE), lambda i: (nb - 1 - i, fcol))],
        out_specs=pl.BlockSpec((t, LANE), lambda i: (nb - 1 - i, 0)),
        out_shape=jax.ShapeDtypeStruct((s, LANE), BF16),
        scratch_shapes=[pltpu.VMEM((8, LANE), F32)],
        compiler_params=_params("arbitrary"),
    )(dc, zg)


KA_COL = SWA_Q // LANE
VA_COL = KA_COL + 1


def _half_masks():
    lane = lax.broadcasted_iota(jnp.int32, (1, LANE), 1)
    hi = lane >= HEAD_DIM
    return (jnp.logical_not(hi), hi)


def _both_halves(x, sel):
    xs = jnp.where(sel, x, 0.0)
    return xs + pltpu.roll(xs, HEAD_DIM, 1)


SWA_PER_KV = 4
WIDE = SWA_PER_KV * LANE


def _swa_bias():
    k = np.arange(2 * LANE)[:, None]
    q = np.arange(LANE)[None, :]
    dist = (q + LANE - k).astype(np.float32)
    valid = (dist >= 0) & (dist < LANE)
    per_head = [np.where(valid, np.float32(-s) * dist, np.float32(NEG_INF)) for s in SLOPES]
    return jnp.asarray(np.stack([np.concatenate(per_head[SWA_PER_KV * hk:SWA_PER_KV * (hk + 1)], axis=1)
                                 for hk in range(2)]), F32)


def _no_previous_block(i_blk):
    k = lax.broadcasted_iota(jnp.int32, (2 * LANE, WIDE), 0)
    return jnp.where((i_blk == 0) & (k < LANE), NEG_INF, 0.0)


def _stack_heads(ref, blk, hk, halves, scale):
    tiles = []
    for j in range(SWA_PER_KV):
        p = 2 * hk + j // 2
        t = ref[blk, p * LANE:(p + 1) * LANE]
        if scale:
            t = _scaled(t)
        tiles.append(jnp.where(halves[j % 2], t, jnp.zeros_like(t)))
    return jnp.concatenate(tiles, axis=0)


def _pair_tile(wide, pp, row_halves):
    a = wide[:, (2 * pp) * LANE:(2 * pp + 1) * LANE]
    b = wide[:, (2 * pp + 1) * LANE:(2 * pp + 2) * LANE]
    return jnp.where(row_halves[0], a, b).T


def _lane_blocks(rows8, hk):
    return jnp.concatenate([rows8[SWA_PER_KV * hk + j:SWA_PER_KV * hk + j + 1, :] for j in range(SWA_PER_KV)], axis=1)


def _row_halves():
    hi = lax.broadcasted_iota(jnp.int32, (LANE, 1), 0) >= HEAD_DIM
    return (jnp.logical_not(hi), hi)


NT = (((1,), (1,)), ((), ()))


def _scaled(q):
    return (q.astype(F32) * SCALE).astype(BF16)


SWA_GROUP = 4


def _swa_group(s_len):
    return SWA_GROUP if (s_len // LANE) % SWA_GROUP == 0 else 1


def _swa_specs(group):
    rows = group * LANE
    prev = lambda i: jnp.maximum(i * group - 1, 0)
    return [pl.BlockSpec((rows, SWA_Q), lambda i: (i, 0)),
            pl.BlockSpec((rows, LANE), lambda i: (i, KA_COL)), pl.BlockSpec((rows, LANE), lambda i: (i, VA_COL)),
            pl.BlockSpec((LANE, LANE), lambda i: (prev(i), KA_COL)),
            pl.BlockSpec((LANE, LANE), lambda i: (prev(i), VA_COL))]


def _swa_window(g, cur_ref, prev_ref):
    before = prev_ref[...] if g == 0 else cur_ref[(g - 1) * LANE:g * LANE, :]
    return jnp.concatenate([before, cur_ref[g * LANE:(g + 1) * LANE, :]], axis=0).astype(F32)


def _swa_fwd(zq, sinks):
    s_len = zq.shape[0]
    group = _swa_group(s_len)
    rows = group * LANE
    sink_lanes = jnp.repeat(sinks[:, :SWA_HEADS], LANE, axis=1)

    def body(q_ref, kc_ref, vc_ref, kp_ref, vp_ref, sink_ref, bias_ref, o_ref, lse_ref):
        halves = _half_masks()
        row_halves = _row_halves()
        for g in range(group):
            blk = slice(g * LANE, (g + 1) * LANE)
            kcat = _swa_window(g, kc_ref, kp_ref)
            vcat = _swa_window(g, vc_ref, vp_ref)
            lse_rows = []
            for hk in range(2):
                kb = _both_halves(kcat, halves[hk]).astype(BF16)
                v_t = _both_halves(vcat, halves[hk]).T.astype(BF16)
                q4 = _stack_heads(q_ref, blk, hk, halves, True)
                s_t = lax.dot_general(kb, q4, NT, preferred_element_type=F32) + bias_ref[hk]
                if g == 0:
                    s_t = s_t + _no_previous_block(pl.program_id(0))
                sink = sink_ref[:, hk * WIDE:(hk + 1) * WIDE]
                m = jnp.maximum(jnp.max(s_t, axis=0, keepdims=True), sink)
                pe = jnp.exp(s_t - m)
                den = jnp.sum(pe, axis=0, keepdims=True) + jnp.exp(sink - m)
                out_t = jnp.dot(v_t, (pe * (1.0 / den)).astype(BF16), preferred_element_type=F32)
                for pp in range(2):
                    p = 2 * hk + pp
                    o_ref[blk, p * LANE:(p + 1) * LANE] = _pair_tile(out_t, pp, row_halves).astype(BF16)
                lse4 = m + jnp.log(den)
                lse_rows += [lse4[:, j * LANE:(j + 1) * LANE] for j in range(SWA_PER_KV)]
            lse_ref[:, blk] = jnp.concatenate(lse_rows, axis=0)

    return pl.pallas_call(
        body, name="swa_fwd", grid=(s_len // rows,),
        in_specs=_swa_specs(group) + [pl.BlockSpec((1, SWA_HEADS * LANE), lambda i: (0, 0)),
                                      pl.BlockSpec((2, 2 * LANE, WIDE), lambda i: (0, 0, 0))],
        out_specs=[pl.BlockSpec((rows, SWA_Q), lambda i: (i, 0)), pl.BlockSpec((SWA_HEADS, rows), lambda i: (0, i))],
        out_shape=[jax.ShapeDtypeStruct((s_len, SWA_Q), BF16), jax.ShapeDtypeStruct((SWA_HEADS, s_len), F32)],
        compiler_params=_params("parallel"),
    )(zq, zq, zq, zq, zq, sink_lanes, _swa_bias())


def _swa_bwd(zq, sinks, o, do, lse):
    s_len = zq.shape[0]
    group = _swa_group(s_len)
    rows = group * LANE

    def body(q_ref, kc_ref, vc_ref, kp_ref, vp_ref, sink_ref, bias_ref, o_ref, do_ref, lse_ref,
             dq_ref, dk_ref, dv_ref, ds_ref):
        halves = _half_masks()
        row_halves = _row_halves()
        lane = lax.broadcasted_iota(jnp.int32, (1, LANE), 1)
        dsink = jnp.zeros((1, LANE), F32)
        for g in range(group):
            blk = slice(g * LANE, (g + 1) * LANE)
            i_blk = pl.program_id(0) * group + g
            kcat = _swa_window(g, kc_ref, kp_ref)
            vcat = _swa_window(g, vc_ref, vp_ref)
            lse_rows = lse_ref[:, blk]
            prod = do_ref[blk, :].astype(F32) * o_ref[blk, :].astype(F32)
            select = (lax.broadcasted_iota(jnp.int32, (SWA_HEADS, SWA_Q), 1) // HEAD_DIM
                      == lax.broadcasted_iota(jnp.int32, (SWA_HEADS, SWA_Q), 0))
            delta_rows = lax.dot_general(jnp.where(select, 1.0, 0.0), prod, NT, precision=lax.Precision.HIGHEST,
                                         preferred_element_type=F32)
            dk_tot = jnp.zeros((2 * LANE, LANE), F32)
            dv_tot = jnp.zeros((2 * LANE, LANE), F32)
            for hk in range(2):
                kb = _both_halves(kcat, halves[hk])
                k_t = kb.T.astype(BF16)
                kb = kb.astype(BF16)
                vb = _both_halves(vcat, halves[hk]).astype(BF16)
                q4 = _stack_heads(q_ref, blk, hk, halves, True)
                do4 = _stack_heads(do_ref, blk, hk, halves, False)
                lse4 = _lane_blocks(lse_rows, hk)
                delta4 = _lane_blocks(delta_rows, hk)
                s_t = lax.dot_general(kb, q4, NT, preferred_element_type=F32) + bias_ref[hk]
                if g == 0:
                    s_t = s_t + _no_previous_block(pl.program_id(0))
                p_t = jnp.exp(s_t - lse4)
                dp_t = lax.dot_general(vb, do4, NT, preferred_element_type=F32)
                ds_t = (p_t * (dp_t - delta4)).astype(BF16)
                sink_part = jnp.exp(sink_ref[:, hk * WIDE:(hk + 1) * WIDE] - lse4) * delta4
                for j in range(SWA_PER_KV):
                    dsink_h = -jnp.sum(sink_part[:, j * LANE:(j + 1) * LANE], axis=1, keepdims=True)
                    dsink = dsink + jnp.where(lane == SWA_PER_KV * hk + j, dsink_h, 0.0)
                dq_t = jnp.dot(k_t, ds_t, preferred_element_type=F32)
                for pp in range(2):
                    p = 2 * hk + pp
                    dq_ref[blk, p * LANE:(p + 1) * LANE] = (_pair_tile(dq_t, pp, row_halves) * SCALE).astype(BF16)
                dk_acc = jnp.dot(ds_t, q4, preferred_element_type=F32)
                dv_acc = jnp.dot(p_t.astype(BF16), do4, preferred_element_type=F32)
                dk_tot = dk_tot + jnp.where(halves[hk], dk_acc + pltpu.roll(dk_acc, HEAD_DIM, 1), 0.0)
                dv_tot = dv_tot + jnp.where(halves[hk], dv_acc + pltpu.roll(dv_acc, HEAD_DIM, 1), 0.0)
            cur = pl.ds(pl.multiple_of(i_blk * LANE, LANE), LANE)
            dk_ref[cur, :] = dk_tot[LANE:, :]
            dv_ref[cur, :] = dv_tot[LANE:, :]

            def add_previous(i_blk=i_blk, dk_tot=dk_tot, dv_tot=dv_tot):
                prv = pl.ds(pl.multiple_of((i_blk - 1) * LANE, LANE), LANE)
                dk_ref[prv, :] += dk_tot[:LANE, :]
                dv_ref[prv, :] += dv_tot[:LANE, :]

            if g == 0:
                pl.when(i_blk > 0)(add_previous)
            else:
                add_previous()

        @pl.when(pl.program_id(0) == 0)
        def _():
            ds_ref[...] = jnp.zeros_like(ds_ref)

        ds_ref[...] += dsink

    blk512 = pl.BlockSpec((rows, SWA_Q), lambda i: (i, 0))
    full = pl.BlockSpec((s_len, LANE), lambda i: (0, 0))
    vec = pl.BlockSpec((1, LANE), lambda i: (0, 0))
    return pl.pallas_call(
        body, name="swa_bwd", grid=(s_len // rows,),
        in_specs=_swa_specs(group) + [pl.BlockSpec((1, SWA_HEADS * LANE), lambda i: (0, 0)),
                                      pl.BlockSpec((2, 2 * LANE, WIDE), lambda i: (0, 0, 0)), blk512, blk512,
                                      pl.BlockSpec((SWA_HEADS, rows), lambda i: (0, i))],
        out_specs=[blk512, full, full, vec],
        out_shape=[jax.ShapeDtypeStruct((s_len, SWA_Q), BF16), jax.ShapeDtypeStruct((s_len, LANE), F32),
                   jax.ShapeDtypeStruct((s_len, LANE), F32), jax.ShapeDtypeStruct((1, LANE), F32)],
        compiler_params=_params("arbitrary"),
    )(zq, zq, zq, zq, zq, jnp.repeat(sinks[:, :SWA_HEADS], LANE, axis=1), _swa_bias(), o, do, lse)


QB_COL = (SWA_Q + 2 * SWA_KV) // LANE
KB_COL = QB_COL + FOX_W // LANE
VB_COL = KB_COL + FOX_W // LANE
N_PAIR = FOX_HEADS // 2


def _causal(t, keys_first=False):
    r = lax.broadcasted_iota(jnp.int32, (t, t), 0)
    c = lax.broadcasted_iota(jnp.int32, (t, t), 1)
    return c >= r if keys_first else r >= c


N_SPLIT = 3


def _own_half(e):
    hi = lax.broadcasted_iota(jnp.int32, (1, LANE), 1) >= HEAD_DIM
    return hi if e else jnp.logical_not(hi)


def _feature_lane(e, t):
    return HEAD_DIM * (1 - e) + t


def _fox_prep(zq, c, tm=256):
    s_len = zq.shape[0]
    tm = _div(s_len, tm, 8)

    def body(z_ref, c_ref, qx_ref, kx_ref, vx_ref):
        lane = lax.broadcasted_iota(jnp.int32, (1, LANE), 1)
        for h in range(FOX_HEADS):
            p, e = divmod(h, 2)
            own = _own_half(e)
            tile = lambda col: z_ref[:, (col + p) * LANE:(col + p + 1) * LANE].astype(F32)
            rest = c_ref[:, h:h + 1]
            qf = jnp.zeros((tm, LANE), F32)
            kf = jnp.zeros((tm, LANE), F32)
            for t in range(N_SPLIT):
                part = rest.astype(BF16).astype(F32)
                rest = rest - part
                qf = jnp.where(lane == _feature_lane(e, t), part, qf)
                qf = jnp.where(lane == _feature_lane(e, N_SPLIT + t), 1.0, qf)
                kf = jnp.where(lane == _feature_lane(e, t), 1.0, kf)
                kf = jnp.where(lane == _feature_lane(e, N_SPLIT + t), -part, kf)
            vf = jnp.where(lane == _feature_lane(e, 0), 1.0, 0.0)
            cols = slice(h * LANE, (h + 1) * LANE)
            qx_ref[:, cols] = jnp.where(own, tile(QB_COL) * SCALE, qf).astype(BF16)
            kx_ref[:, cols] = jnp.where(own, tile(KB_COL), kf).astype(BF16)
            vx_ref[:, cols] = jnp.where(own, tile(VB_COL), vf).astype(BF16)

    out = jax.ShapeDtypeStruct((s_len, FOX_HEADS * LANE), BF16)
    blk = pl.BlockSpec((tm, FOX_HEADS * LANE), lambda i: (i, 0))
    return pl.pallas_call(
        body, name="fox_prep", grid=(s_len // tm,),
        in_specs=[pl.BlockSpec((tm, N_QKV), lambda i: (i, 0)), pl.BlockSpec((tm, LANE), lambda i: (i, 0))],
        out_specs=[blk, blk, blk], out_shape=[out, out, out],
        compiler_params=_params("parallel"),
    )(zq, c)


def _comm_parts(comm):
    return ([], [], [], []) if comm is None else (comm.specs, comm.out_shape, comm.scratch, comm.arrays)


def _fox_fwd(qx, kx, vx, comm=None, t_cap=1024):
    s_len = qx.shape[0]
    t = _div(s_len, t_cap, LANE)
    nq = s_len // t
    c_specs, c_shapes, c_scratch, c_arrays = _comm_parts(comm)

    def compute(q_ref, k_ref, v_ref, o_ref, o32_ref, lse_ref):
        i = pl.program_id(1)
        qs = [q_ref[:, e * LANE:(e + 1) * LANE] for e in range(2)]

        def step(j, carry, diag):
            rows = pl.ds(pl.multiple_of(j * t, t), t)
            new = []
            for e in range(2):
                m, acc = carry[e]
                s = lax.dot_general(qs[e], k_ref[rows, e * LANE:(e + 1) * LANE], (((1,), (1,)), ((), ())),
                                    preferred_element_type=F32)
                if diag:
                    s = jnp.where(_causal(t), s, NEG_INF)
                mn = jnp.maximum(m, jnp.max(s, axis=1, keepdims=True))
                pe = jnp.exp(s - mn)
                p_hi = pe.astype(BF16)
                p_lo = (pe - p_hi.astype(F32)).astype(BF16)
                vs = v_ref[rows, e * LANE:(e + 1) * LANE]
                acc = (acc * jnp.exp(m - mn) + jnp.dot(p_hi, vs, preferred_element_type=F32)
                       + jnp.dot(p_lo, vs, preferred_element_type=F32))
                new.append((mn, acc))
            return tuple(new)

        init = (jnp.full((t, 1), NEG_INF, F32), jnp.zeros((t, LANE), F32))
        carry = lax.fori_loop(0, i, lambda j, c: step(j, c, False), (init, init))
        carry = step(i, carry, True)
        outs, lses = [], []
        for e in range(2):
            m, acc = carry[e]
            l = acc[:, _feature_lane(e, 0):_feature_lane(e, 0) + 1]
            outs.append(acc / l)
            lses.append(m + jnp.log(l))
        out = jnp.where(_own_half(1), outs[1], outs[0])
        o_ref[...] = out.astype(BF16)
        o32_ref[...] = out
        lse_ref[...] = jnp.where(_own_half(1), lses[1], lses[0])

    body = _with_comm(comm, 3, 3, lambda: (pl.program_id(0) == 0) & (pl.program_id(1) == 0),
                      lambda: (pl.program_id(0) == N_PAIR - 1) & (pl.program_id(1) == nq - 1), compute)
    pair = pl.BlockSpec((s_len, 2 * LANE), lambda p, i: (0, p))
    tile = pl.BlockSpec((t, LANE), lambda p, i: (i, p))
    outs = pl.pallas_call(
        body, name="fox_fwd" if comm is None else "fox_fwd_comm", grid=(N_PAIR, nq),
        in_specs=[pl.BlockSpec((t, 2 * LANE), lambda p, i: (i, p)), pair, pair] + c_specs,
        out_specs=[tile, tile, tile] + c_specs,
        out_shape=[jax.ShapeDtypeStruct((s_len, FOX_W), BF16), jax.ShapeDtypeStruct((s_len, FOX_W), F32),
                   jax.ShapeDtypeStruct((s_len, FOX_W), F32)] + c_shapes,
        scratch_shapes=c_scratch,
        compiler_params=_params("arbitrary", "arbitrary"),
    )(qx, kx, vx, *c_arrays)
    return outs[0], outs[1], outs[2], outs[3:]


def _fox_stats(o, do, lse, tm=256):
    s_len = o.shape[0]
    tm = _div(s_len, tm, LANE)

    def body(o_ref, do_ref, lse_ref, dox_ref, st_ref):
        lane = lax.broadcasted_iota(jnp.int32, (1, LANE), 1)
        for p in range(N_PAIR):
            cols = slice(p * LANE, (p + 1) * LANE)
            dout = do_ref[:, cols]
            prod = o_ref[:, cols] * dout.astype(F32)
            lse = lse_ref[:, cols]
            st = jnp.zeros((tm, LANE), F32)
            for e in range(2):
                h = 2 * p + e
                dox_ref[:, h * LANE:(h + 1) * LANE] = jnp.where(_own_half(e), dout, jnp.zeros_like(dout))
                st = jnp.where(lane == e, lse[:, e * HEAD_DIM:e * HEAD_DIM + 1], st)
                delta = jnp.sum(jnp.where(_own_half(e), prod, 0.0), axis=1, keepdims=True)
                st = jnp.where(lane == 2 + e, delta, st)
            st_ref[p] = st.T[:8, :]

    row = pl.BlockSpec((tm, FOX_W), lambda i: (i, 0))
    return pl.pallas_call(
        body, name="fox_stats", grid=(s_len // tm,), in_specs=[row, row, row],
        out_specs=[pl.BlockSpec((tm, FOX_HEADS * LANE), lambda i: (i, 0)),
                   pl.BlockSpec((N_PAIR, 8, tm), lambda i: (0, 0, i))],
        out_shape=[jax.ShapeDtypeStruct((s_len, FOX_HEADS * LANE), BF16),
                   jax.ShapeDtypeStruct((N_PAIR, 8, s_len), F32)],
        compiler_params=_params("parallel"),
    )(o, do, lse)


def _fox_bwd(qx, kx, vx, dox, stats, comm=None, t_cap=512):
    s_len = qx.shape[0]
    t = _div(s_len, t_cap, LANE)
    n = s_len // t
    c_specs, c_shapes, c_scratch, c_arrays = _comm_parts(comm)

    def compute(q_ref, do_ref, st_ref, k_ref, v_ref, dq_ref, dk_ref, dv_ref, dc_ref):
        j = pl.program_id(1)
        lane = lax.broadcasted_iota(jnp.int32, (1, LANE), 1)

        @pl.when(j == 0)
        def _():
            dq_ref[...] = jnp.zeros_like(dq_ref)

        ks = [k_ref[:, e * LANE:(e + 1) * LANE] for e in range(2)]
        vs = [v_ref[:, e * LANE:(e + 1) * LANE] for e in range(2)]
        ks_t = [k.astype(F32).T.astype(BF16) for k in ks]

        def step(i, carry, diag):
            rows = pl.ds(pl.multiple_of(i * t, t), t)
            new = []
            dq = jnp.zeros((LANE, t), F32)
            for e in range(2):
                dk, dv, dc = carry[e]
                q = q_ref[rows, e * LANE:(e + 1) * LANE]
                dout = do_ref[rows, e * LANE:(e + 1) * LANE]
                s_t = lax.dot_general(ks[e], q, (((1,), (1,)), ((), ())), preferred_element_type=F32)
                if diag:
                    s_t = jnp.where(_causal(t, keys_first=True), s_t, NEG_INF)
                p_t = jnp.exp(s_t - st_ref[0, e:e + 1, rows])
                dp_t = lax.dot_general(vs[e], dout, (((1,), (1,)), ((), ())), preferred_element_type=F32)
                ds_f = p_t * (dp_t - st_ref[0, 2 + e:3 + e, rows])
                ds_t = ds_f.astype(BF16)
                dc = dc + jnp.sum(ds_f, axis=1, keepdims=True)
                dv = dv + jnp.dot(p_t.astype(BF16), dout, preferred_element_type=F32)
                dk = dk + jnp.dot(ds_t, q, preferred_element_type=F32)
                dq_e = jnp.dot(ks_t[e], ds_t, preferred_element_type=F32)
                dq = dq + jnp.where(_row_halves()[e], dq_e, 0.0)
                new.append((dk, dv, dc))
            dq_ref[rows, :] += dq.T * SCALE
            return tuple(new)

        zero = jnp.zeros((t, LANE), F32)
        init = (zero, zero, jnp.zeros((t, 1), F32))
        carry = step(j, (init, init), True)
        (dk0, dv0, dc0), (dk1, dv1, dc1) = lax.fori_loop(j + 1, n, lambda i, c: step(i, c, False), carry)
        dk_ref[...] = jnp.where(_own_half(1), dk1, dk0).astype(BF16)
        dv_ref[...] = jnp.where(_own_half(1), dv1, dv0).astype(BF16)
        dc_ref[...] = jnp.where(lane == 0, -dc0, jnp.where(lane == 1, -dc1, 0.0))

    body = _with_comm(comm, 5, 4, lambda: (pl.program_id(0) == 0) & (pl.program_id(1) == 0),
                      lambda: (pl.program_id(0) == N_PAIR - 1) & (pl.program_id(1) == n - 1), compute)
    pair = pl.BlockSpec((s_len, 2 * LANE), lambda p, j: (0, p))
    blk = pl.BlockSpec((t, 2 * LANE), lambda p, j: (j, p))
    tile = pl.BlockSpec((t, LANE), lambda p, j: (j, p))
    outs = pl.pallas_call(
        body, name="fox_bwd" if comm is None else "fox_bwd_comm", grid=(N_PAIR, n),
        in_specs=[pair, pair, pl.BlockSpec((1, 8, s_len), lambda p, j: (p, 0, 0)), blk, blk] + c_specs,
        out_specs=[pl.BlockSpec((s_len, LANE), lambda p, j: (0, p)), tile, tile, tile] + c_specs,
        out_shape=[jax.ShapeDtypeStruct((s_len, FOX_W), F32), jax.ShapeDtypeStruct((s_len, FOX_W), BF16),
                   jax.ShapeDtypeStruct((s_len, FOX_W), BF16), jax.ShapeDtypeStruct((s_len, FOX_W), F32)] + c_shapes,
        scratch_shapes=c_scratch,
        compiler_params=_params("arbitrary", "arbitrary"),
    )(qx, dox, stats, kx, vx, *c_arrays)
    return outs[0], outs[1], outs[2], outs[3], outs[4:]


def _mixer_out(attn_a, attn_b, zg, h, wpa, wpb, wout, g, b, tm=256):
    m = h.shape[0]
    tm = _div(m, tm, 8)

    def body(a_ref, b_ref, ga_ref, gb_ref, h_ref, wpa_ref, wpb_ref, wout_ref, g_ref, bb_ref,
             h1_ref, u_ref, mg_ref, ya_ref, yb_ref):
        ya = jnp.dot(a_ref[...], wpa_ref[...], preferred_element_type=F32)
        yb = jnp.dot(b_ref[...], wpb_ref[...], preferred_element_type=F32)
        merged = (jax.nn.sigmoid(ga_ref[...]) * ya + jax.nn.sigmoid(gb_ref[...]) * yb).astype(BF16)
        u = ALPHA * h_ref[...] + jnp.dot(merged, wout_ref[...], preferred_element_type=F32)
        u_ref[...] = u
        h1_ref[...] = _ln(u, g_ref[...], bb_ref[...])
        mg_ref[...] = merged
        ya_ref[...] = ya.astype(BF16)
        yb_ref[...] = yb.astype(BF16)

    row = pl.BlockSpec((tm, D_MODEL), lambda i: (i, 0))
    att = pl.BlockSpec((tm, SWA_Q), lambda i: (i, 0))
    vec = pl.BlockSpec((1, D_MODEL), lambda i: (0, 0))
    wsm = pl.BlockSpec((SWA_Q, D_MODEL), lambda i: (0, 0))
    return pl.pallas_call(
        body, name="mixer_out", grid=(m // tm,),
        in_specs=[att, att, row, pl.BlockSpec((tm, D_MODEL), lambda i: (i, 1)), row, wsm, wsm,
                  pl.BlockSpec((D_MODEL, D_MODEL), lambda i: (0, 0)), vec, vec],
        out_specs=[row, row, row, row, row],
        out_shape=[jax.ShapeDtypeStruct((m, D_MODEL), F32), jax.ShapeDtypeStruct((m, D_MODEL), F32),
                   jax.ShapeDtypeStruct((m, D_MODEL), BF16), jax.ShapeDtypeStruct((m, D_MODEL), BF16),
                   jax.ShapeDtypeStruct((m, D_MODEL), BF16)],
        compiler_params=_params("parallel"),
    )(attn_a, attn_b, zg, zg, h, wpa, wpb, wout, g, b)


def _mixer_bwd(dh1, u1, g, wout, ya, yb, zg, wpa, wpb, tm=256):
    m = dh1.shape[0]
    tm = _div(m, tm, 8)

    def body(dh_ref, u_ref, g_ref, wout_ref, ya_ref, yb_ref, ga_ref, gb_ref, wpa_ref, wpb_ref,
             du_ref, dg_ref, db_ref, dya_ref, dyb_ref, dga_ref, dgb_ref, da_ref, dbb_ref):
        @pl.when(pl.program_id(0) == 0)
        def _():
            dg_ref[...] = jnp.zeros_like(dg_ref)
            db_ref[...] = jnp.zeros_like(db_ref)

        du, dg, db = _ln_bwd_block(dh_ref[...], u_ref[...], g_ref[...])
        du_ref[...] = du
        dg_ref[...] += dg
        db_ref[...] += db
        dm = lax.dot_general(du.astype(BF16), wout_ref[...], (((1,), (1,)), ((), ())), preferred_element_type=F32)
        for y_ref, gate_ref, w_ref, dy_ref, dgate_ref, dattn_ref in (
                (ya_ref, ga_ref, wpa_ref, dya_ref, dga_ref, da_ref), (yb_ref, gb_ref, wpb_ref, dyb_ref, dgb_ref, dbb_ref)):
            sg = jax.nn.sigmoid(gate_ref[...])
            dy = (dm * sg).astype(BF16)
            dy_ref[...] = dy
            dgate_ref[...] = (dm * y_ref[...].astype(F32) * sg * (1.0 - sg)).astype(BF16)
            dattn_ref[...] = lax.dot_general(dy, w_ref[...], (((1,), (1,)), ((), ())),
                                             preferred_element_type=F32).astype(BF16)

    row = pl.BlockSpec((tm, D_MODEL), lambda i: (i, 0))
    att = pl.BlockSpec((tm, SWA_Q), lambda i: (i, 0))
    vec = pl.BlockSpec((1, D_MODEL), lambda i: (0, 0))
    wsm = pl.BlockSpec((SWA_Q, D_MODEL), lambda i: (0, 0))
    wide = jax.ShapeDtypeStruct((m, D_MODEL), BF16)
    narrow = jax.ShapeDtypeStruct((m, SWA_Q), BF16)
    sums = jax.ShapeDtypeStruct((1, D_MODEL), F32)
    return pl.pallas_call(
        body, name="mixer_bwd", grid=(m // tm,),
        in_specs=[row, row, vec, pl.BlockSpec((D_MODEL, D_MODEL), lambda i: (0, 0)), row, row, row,
                  pl.BlockSpec((tm, D_MODEL), lambda i: (i, 1)), wsm, wsm],
        out_specs=[row, vec, vec, row, row, row, row, att, att],
        out_shape=[jax.ShapeDtypeStruct((m, D_MODEL), F32), sums, sums, wide, wide, wide, wide, narrow, narrow],
        compiler_params=_params("arbitrary"),
    )(dh1, u1, g, wout, ya, yb, zg, zg, wpa, wpb)


def _shift_down(x, k, halo, first):
    rows = lax.broadcasted_iota(jnp.int32, (x.shape[0], 1), 0)
    y = pltpu.roll(x, k, 0)
    for r in range(k):
        fill = jnp.where(first, 0.0, halo[8 - k + r:8 - k + r + 1, :])
        y = jnp.where(rows == r, fill, y)
    return y


def _shift_up(x, k, halo, last):
    n = x.shape[0]
    rows = lax.broadcasted_iota(jnp.int32, (n, 1), 0)
    y = pltpu.roll(x, n - k, 0)
    for r in range(k):
        fill = jnp.where(last, 0.0, halo[r:r + 1, :])
        y = jnp.where(rows == n - k + r, fill, y)
    return y


def _conv_act(gate, gate_m1, gate_m2, cw, cb):
    return cb + cw[0:1, :] * gate_m2 + cw[1:2, :] * gate_m1 + cw[2:3, :] * gate


def _ffn_in_conv(h1, wfi, cw, cb, tm=256):
    s_len = h1.shape[0]
    tm = _div(s_len, tm, 8)
    hb = tm // 8

    def body(a_ref, ap_ref, w_ref, cw_ref, cb_ref, gu_ref, act_ref):
        first = pl.program_id(0) == 0
        a = a_ref[...].astype(BF16)
        before = ap_ref[...].astype(BF16)
        for c in range(N_CHUNK):
            gate = jnp.dot(a, w_ref[c], preferred_element_type=F32)
            up = jnp.dot(a, w_ref[N_CHUNK + c], preferred_element_type=F32)
            halo = jnp.dot(before, w_ref[c], preferred_element_type=F32)
            gu_ref[c, 0] = gate
            gu_ref[c, 1] = up
            conv = _conv_act(gate, _shift_down(gate, 1, halo, first), _shift_down(gate, 2, halo, first),
                             cw_ref[c], cb_ref[c])
            act_ref[c] = (conv * jax.nn.sigmoid(conv) * up).astype(BF16)

    return pl.pallas_call(
        body, name="ffn_in_conv", grid=(s_len // tm,),
        in_specs=[pl.BlockSpec((tm, D_MODEL), lambda i: (i, 0)),
                  pl.BlockSpec((8, D_MODEL), lambda i: (jnp.maximum(i * hb - 1, 0), 0)),
                  pl.BlockSpec((N_DEV, D_MODEL, FF_CHUNK), lambda i: (0, 0, 0)),
                  pl.BlockSpec((N_CHUNK, 8, FF_CHUNK), lambda i: (0, 0, 0)),
                  pl.BlockSpec((N_CHUNK, 1, FF_CHUNK), lambda i: (0, 0, 0))],
        out_specs=[pl.BlockSpec((N_CHUNK, 2, tm, FF_CHUNK), lambda i: (0, 0, i, 0)),
                   pl.BlockSpec((N_CHUNK, tm, FF_CHUNK), lambda i: (0, i, 0))],
        out_shape=[jax.ShapeDtypeStruct((N_CHUNK, 2, s_len, FF_CHUNK), F32),
                   jax.ShapeDtypeStruct((N_CHUNK, s_len, FF_CHUNK), BF16)],
        compiler_params=_params("parallel"),
    )(h1, h1, wfi, cw, cb)


def _ffn_out_ln(act, wfo, res, g, b, tm=256):
    s_len = res.shape[0]
    tm = _div(s_len, tm, 8)

    def body(a_ref, w_ref, res_ref, g_ref, b_ref, u_ref, y_ref):
        u = ALPHA * res_ref[...]
        for c in range(N_CHUNK):
            u = u + jnp.dot(a_ref[c], w_ref[c], preferred_element_type=F32)
        u_ref[...] = u
        y_ref[...] = _ln(u, g_ref[...], b_ref[...])

    row = pl.BlockSpec((tm, D_MODEL), lambda i: (i, 0))
    vec = pl.BlockSpec((1, D_MODEL), lambda i: (0, 0))
    return pl.pallas_call(
        body, name="ffn_out_ln", grid=(s_len // tm,),
        in_specs=[pl.BlockSpec((N_CHUNK, tm, FF_CHUNK), lambda i: (0, i, 0)),
                  pl.BlockSpec((N_CHUNK, FF_CHUNK, D_MODEL), lambda i: (0, 0, 0)), row, vec, vec],
        out_specs=[row, row],
        out_shape=[jax.ShapeDtypeStruct((s_len, D_MODEL), F32), jax.ShapeDtypeStruct((s_len, D_MODEL), F32)],
        compiler_params=_params("parallel"),
    )(act, wfo, res, g, b)


def _ffn_out_bwd(dh2, u2, g, wfo, tm=256):
    s_len = dh2.shape[0]
    tm = _div(s_len, tm, 8)

    def body(dh_ref, u_ref, g_ref, w_ref, du_ref, dg_ref, db_ref, o_ref):
        @pl.when(pl.program_id(0) == 0)
        def _():
            dg_ref[...] = jnp.zeros_like(dg_ref)
            db_ref[...] = jnp.zeros_like(db_ref)

        du, dg, db = _ln_bwd_block(dh_ref[...], u_ref[...], g_ref[...])
        du_ref[...] = du
        dg_ref[...] += dg
        db_ref[...] += db
        du_b = du.astype(BF16)
        for c in range(N_CHUNK):
            o_ref[c] = lax.dot_general(du_b, w_ref[c], (((1,), (1,)), ((), ())), preferred_element_type=F32)

    row = pl.BlockSpec((tm, D_MODEL), lambda i: (i, 0))
    vec = pl.BlockSpec((1, D_MODEL), lambda i: (0, 0))
    sums = jax.ShapeDtypeStruct((1, D_MODEL), F32)
    return pl.pallas_call(
        body, name="ffn_out_bwd", grid=(s_len // tm,),
        in_specs=[row, row, vec, pl.BlockSpec((N_CHUNK, FF_CHUNK, D_MODEL), lambda i: (0, 0, 0))],
        out_specs=[row, vec, vec, pl.BlockSpec((N_CHUNK, tm, FF_CHUNK), lambda i: (0, i, 0))],
        out_shape=[jax.ShapeDtypeStruct((s_len, D_MODEL), F32), sums, sums,
                   jax.ShapeDtypeStruct((N_CHUNK, s_len, FF_CHUNK), F32)],
        compiler_params=_params("arbitrary"),
    )(dh2, u2, g, wfo)


def _g_w_ffn_out(act, du, tm=2048):
    s_len = du.shape[0]
    tm = _div(s_len, tm, 8)
    steps = s_len // tm

    def body(a_ref, g_ref, o_ref, acc_ref):
        s = pl.program_id(1)

        @pl.when(s == 0)
        def _():
            acc_ref[...] = jnp.zeros_like(acc_ref)

        acc_ref[...] += lax.dot_general(a_ref[0], g_ref[...].astype(BF16), (((0,), (0,)), ((), ())),
                                        preferred_element_type=F32)

        @pl.when(s == steps - 1)
        def _():
            o_ref[0] = acc_ref[...].astype(BF16)

    return pl.pallas_call(
        body, name="g_w_ffn_out", grid=(N_CHUNK, steps),
        in_specs=[pl.BlockSpec((1, tm, FF_CHUNK), lambda c, s: (c, s, 0)),
                  pl.BlockSpec((tm, D_MODEL), lambda c, s: (s, 0))],
        out_specs=pl.BlockSpec((1, FF_CHUNK, D_MODEL), lambda c, s: (c, 0, 0)),
        out_shape=jax.ShapeDtypeStruct((N_CHUNK, FF_CHUNK, D_MODEL), BF16),
        scratch_shapes=[pltpu.VMEM((FF_CHUNK, D_MODEL), F32)],
        compiler_params=_params("parallel", "arbitrary"),
    )(act, du)


def _g_w_ffn_in(h1, dgu, tm=2048):
    s_len = h1.shape[0]
    tm = _div(s_len, tm, 8)
    steps = s_len // tm

    def body(a_ref, g_ref, o_ref, acc_ref):
        s = pl.program_id(1)

        @pl.when(s == 0)
        def _():
            acc_ref[...] = jnp.zeros_like(acc_ref)

        acc_ref[...] += lax.dot_general(a_ref[...].astype(BF16), g_ref[0, 0], (((0,), (0,)), ((), ())),
                                        preferred_element_type=F32)

        @pl.when(s == steps - 1)
        def _():
            o_ref[0] = acc_ref[...].astype(BF16)

    return pl.pallas_call(
        body, name="g_w_ffn_in", grid=(N_DEV, steps),
        in_specs=[pl.BlockSpec((tm, D_MODEL), lambda d, s: (s, 0)),
                  pl.BlockSpec((1, 1, tm, FF_CHUNK), lambda d, s: (d % N_CHUNK, d // N_CHUNK, s, 0))],
        out_specs=pl.BlockSpec((1, D_MODEL, FF_CHUNK), lambda d, s: (d, 0, 0)),
        out_shape=jax.ShapeDtypeStruct((N_DEV, D_MODEL, FF_CHUNK), BF16),
        scratch_shapes=[pltpu.VMEM((D_MODEL, FF_CHUNK), F32)],
        compiler_params=_params("parallel", "arbitrary"),
    )(h1, dgu)


def _conv_bwd_dh1(gu, dact, cw, cb, wfi, res, tm=256):
    s_len = gu.shape[2]
    tm = _div(s_len, tm, 8)
    nrow = s_len // tm
    hb = tm // 8

    def dconv_of(conv, up, da):
        sg = jax.nn.sigmoid(conv)
        return da * up * (sg * (1.0 + conv * (1.0 - sg)))

    def body(gu_ref, gp_ref, gun_ref, da_ref, dan_ref, cw_ref, cb_ref, w_ref, res_ref, dgu_ref, dcw_ref, dh_ref):
        i = pl.program_id(0)
        first = i == 0
        last = i == nrow - 1

        @pl.when(first)
        def _():
            dcw_ref[...] = jnp.zeros_like(dcw_ref)

        row = lax.broadcasted_iota(jnp.int32, (8, 1), 0)
        acc = ALPHA * res_ref[...]
        for c in range(N_CHUNK):
            cw = cw_ref[c]
            cb = cb_ref[c]
            gate = gu_ref[c, 0]
            halo = gp_ref[c, 0]
            g_m1 = _shift_down(gate, 1, halo, first)
            g_m2 = _shift_down(gate, 2, halo, first)
            conv = _conv_act(gate, g_m1, g_m2, cw, cb)
            da = da_ref[c]
            dup = (da * conv * jax.nn.sigmoid(conv)).astype(BF16)
            dconv = dconv_of(conv, gu_ref[c, 1], da)
            gate_n = gun_ref[c, 0]
            tail = gate[tm - 8:, :]
            conv_n = _conv_act(gate_n, _shift_down(gate_n, 1, tail, False), _shift_down(gate_n, 2, tail, False),
                               cw, cb)
            dconv_n = dconv_of(conv_n, gun_ref[c, 1], dan_ref[c])
            dgate = (cw[2:3, :] * dconv + cw[1:2, :] * _shift_up(dconv, 1, dconv_n, last)
                     + cw[0:1, :] * _shift_up(dconv, 2, dconv_n, last)).astype(BF16)
            dgu_ref[c, 0] = dgate
            dgu_ref[c, 1] = dup
            acc = acc + lax.dot_general(dgate, w_ref[c], NT, preferred_element_type=F32)
            acc = acc + lax.dot_general(dup, w_ref[N_CHUNK + c], NT, preferred_element_type=F32)
            part = jnp.zeros((8, FF_CHUNK), F32)
            for r, term in enumerate((dconv * g_m2, dconv * g_m1, dconv * gate, dconv)):
                part = jnp.where(row == r, jnp.sum(term, axis=0, keepdims=True), part)
            dcw_ref[c] += part
        dh_ref[...] = acc

    nxt = lambda i: jnp.minimum((i + 1) * hb, s_len // 8 - 1)
    main = pl.BlockSpec((N_CHUNK, 2, tm, FF_CHUNK), lambda i: (0, 0, i, 0))
    row_d = pl.BlockSpec((tm, D_MODEL), lambda i: (i, 0))
    return pl.pallas_call(
        body, name="conv_bwd_dh1", grid=(nrow,),
        in_specs=[main,
                  pl.BlockSpec((N_CHUNK, 1, 8, FF_CHUNK), lambda i: (0, 0, jnp.maximum(i * hb - 1, 0), 0)),
                  pl.BlockSpec((N_CHUNK, 2, 8, FF_CHUNK), lambda i: (0, 0, nxt(i), 0)),
                  pl.BlockSpec((N_CHUNK, tm, FF_CHUNK), lambda i: (0, i, 0)),
                  pl.BlockSpec((N_CHUNK, 8, FF_CHUNK), lambda i: (0, nxt(i), 0)),
                  pl.BlockSpec((N_CHUNK, 8, FF_CHUNK), lambda i: (0, 0, 0)),
                  pl.BlockSpec((N_CHUNK, 1, FF_CHUNK), lambda i: (0, 0, 0)),
                  pl.BlockSpec((N_DEV, D_MODEL, FF_CHUNK), lambda i: (0, 0, 0)), row_d],
        out_specs=[main, pl.BlockSpec((N_CHUNK, 8, FF_CHUNK), lambda i: (0, 0, 0)), row_d],
        out_shape=[jax.ShapeDtypeStruct((N_CHUNK, 2, s_len, FF_CHUNK), BF16),
                   jax.ShapeDtypeStruct((N_CHUNK, 8, FF_CHUNK), F32),
                   jax.ShapeDtypeStruct((s_len, D_MODEL), F32)],
        compiler_params=_params("arbitrary"),
    )(gu, gu, gu, dact, dact, cw, cb, wfi, res)


def _loss_head(y, target, tm=256):
    m, d = y.shape
    tm = _div(m, tm, 8)

    def body(y_ref, t_ref, dy_ref, loss_ref):
        @pl.when(pl.program_id(0) == 0)
        def _():
            loss_ref[...] = jnp.zeros_like(loss_ref)

        err = y_ref[...] - t_ref[...]
        dy_ref[...] = err / d
        loss_ref[...] += 0.5 * jnp.sum(jnp.sum(err * err, axis=1, keepdims=True) / d, axis=0, keepdims=True)

    row = pl.BlockSpec((tm, d), lambda i: (i, 0))
    return pl.pallas_call(
        body, name="loss_head", grid=(m // tm,), in_specs=[row, row],
        out_specs=[row, pl.BlockSpec((8, LANE), lambda i: (0, 0))],
        out_shape=[jax.ShapeDtypeStruct((m, d), F32), jax.ShapeDtypeStruct((8, LANE), F32)],
        compiler_params=_params("arbitrary"),
    )(y, target)


def _sum_devices(r_ref):
    acc = r_ref[0].astype(F32)
    for d in range(1, N_DEV):
        acc = acc + r_ref[d].astype(F32)
    return acc


def _sum8(recv):
    rows = recv.shape[1]
    tr = _div(rows, ROW_BLOCK, 8)

    def body(r_ref, o_ref):
        o_ref[...] = _sum_devices(r_ref)

    return pl.pallas_call(
        body, name="sum8", grid=(rows // tr,),
        in_specs=[pl.BlockSpec((N_DEV, tr, LANE), lambda i: (0, i, 0))],
        out_specs=pl.BlockSpec((tr, LANE), lambda i: (i, 0)),
        out_shape=jax.ShapeDtypeStruct((rows, LANE), F32),
        compiler_params=_params("parallel"),
    )(recv)


def _adamw_math(w, g, m, v):
    m = ADAM_B1 * m + (1.0 - ADAM_B1) * g
    v = ADAM_B2 * v + (1.0 - ADAM_B2) * (g * g)
    m_hat = m / (1.0 - ADAM_B1 ** ADAM_STEP)
    v_hat = v / (1.0 - ADAM_B2 ** ADAM_STEP)
    return -ADAM_LR * (m_hat / (jnp.sqrt(v_hat) + ADAM_EPS) + ADAM_WD * w), m, v


def _adamw_rows(w, g, m, v, name):
    rows = w.shape[0]
    tr = _div(rows, ROW_BLOCK, 8)

    def body(w_ref, g_ref, m_ref, v_ref, d_ref, mo_ref, vo_ref):
        d_ref[...], mo_ref[...], vo_ref[...] = _adamw_math(w_ref[...], g_ref[...], m_ref[...], v_ref[...])

    blk = pl.BlockSpec((tr, LANE), lambda i: (i, 0))
    out = jax.ShapeDtypeStruct((rows, LANE), F32)
    return pl.pallas_call(
        body, name=name, grid=(rows // tr,), in_specs=[blk, blk, blk, blk], out_specs=[blk, blk, blk],
        out_shape=[out, out, out], compiler_params=_params("parallel"),
    )(w, g, m, v)


def _adamw_shard(recv, w, m, v, layer, prev, name):
    _, k, n = recv.shape
    tk = _div(k, 128, 16)

    def body(r_ref, w_ref, m_ref, v_ref, *rest):
        g_ref, d_ref, mo_ref, vo_ref = rest[-4:]
        g = _sum_devices(r_ref)
        g_ref[0] = g
        d_ref[0], mo_ref[0], vo_ref[0] = _adamw_math(w_ref[0], g, m_ref[0], v_ref[0])

    blk = pl.BlockSpec((1, tk, n), lambda i: (layer, i, 0))
    out = jax.ShapeDtypeStruct((DEPTH, k, n), F32)
    carried = [] if prev is None else list(prev)
    return pl.pallas_call(
        body, name=name, grid=(k // tk,),
        in_specs=[pl.BlockSpec((N_DEV, tk, n), lambda i: (0, i, 0)), blk, blk, blk]
        + [pl.BlockSpec(memory_space=pl.ANY)] * len(carried),
        out_specs=[blk, blk, blk, blk], out_shape=[out, out, out, out],
        input_output_aliases={4 + j: j for j in range(len(carried))},
        compiler_params=_params("parallel"),
    )(recv, w, m, v, *carried)


def _to_rows(flat, rows):
    flat = flat.reshape(-1)
    return jnp.pad(flat, (0, rows * LANE - flat.shape[0])).reshape(rows, LANE)


def _pad_cols_z(a):
    f0 = N_QKV
    g0 = N_QKV + FOX_HEADS
    pad = jnp.zeros(a.shape[:-1] + (F_PAD - FOX_HEADS,), a.dtype)
    return jnp.concatenate([a[..., :f0], a[..., g0:], a[..., f0:g0], pad], axis=-1)


def _unpad_cols_z(a):
    f0 = N_QKV + N_GATE
    return jnp.concatenate([a[..., :N_QKV], a[..., f0:f0 + FOX_HEADS], a[..., N_QKV:f0]], axis=-1)


def _shards_to_cols(g):
    _, k, n = g.shape
    return g.transpose(1, 0, 2).reshape(k, N_DEV * n)


def _cols_to_shards(full):
    k, n = full.shape
    return full.reshape(k, N_DEV, n // N_DEV).transpose(1, 0, 2)


def _layer_fwd(h, w, p, comm=None, late=None):
    zq, zg = _z_proj(h, w["w_in_p"], p["b_in_p"])
    qx, kx, vx = _fox_prep(zq, _cumsum_logf(zg))
    attn_a, lse_a = _swa_fwd(zq, p["sinks"])
    attn_b, attn_b32, lse_b, arrived = _fox_fwd(qx, kx, vx, comm)
    if late is not None:
        w, p = late(w, p, arrived)
    h1, u1, merged, ya, yb = _mixer_out(attn_a, attn_b, zg, h, w["w_proj_a"], w["w_proj_b"], w["w_out"],
                                        p["ln_mix_g"], p["ln_mix_b"])
    gu, act = _ffn_in_conv(h1, w["w_ffn_in"], p["conv_w"], p["conv_b"])
    u2, h2 = _ffn_out_ln(act, w["w_ffn_out"], h1, p["ln_ffn_g"], p["ln_ffn_b"])
    saved = dict(h=h, zq=zq, zg=zg, qx=qx, kx=kx, vx=vx, attn_a=attn_a, lse_a=lse_a, attn_b=attn_b,
                 attn_b32=attn_b32, lse_b=lse_b, h1=h1, u1=u1, merged=merged, ya=ya, yb=yb, gu=gu, act=act, u2=u2)
    return h2, saved, w, p


def _layer_bwd(dh2, sv, w, p, make_comm=None, make_last_comm=None):
    s_len = dh2.shape[0]
    du2, d_ffn_g, d_ffn_b, dact = _ffn_out_bwd(dh2, sv["u2"], p["ln_ffn_g"], w["w_ffn_out"])
    g_ffn_out = _g_w_ffn_out(sv["act"], du2)
    dgu, dcw, dh1 = _conv_bwd_dh1(sv["gu"], dact, p["conv_w"], p["conv_b"], w["w_ffn_in"], du2)
    dcw = dcw.transpose(1, 0, 2).reshape(8, D_FF)
    g_ffn_in = _g_w_ffn_in(sv["h1"], dgu)
    du1, d_mix_g, d_mix_b, dya, dyb, dga, dgb, dattn_a, dattn_b = _mixer_bwd(
        dh1, sv["u1"], p["ln_mix_g"], w["w_out"], sv["ya"], sv["yb"], sv["zg"], w["w_proj_a"], w["w_proj_b"])
    g_out = _linear_tn(sv["merged"], du1, name="g_w_out", tn=1024)
    g_proj_a = _linear_tn(sv["attn_a"], dya, name="g_w_proj_a", tk=512, tn=1024)
    g_proj_b = _linear_tn(sv["attn_b"], dyb, name="g_w_proj_b", tk=512, tn=1024)
    dq_a, dk_a, dv_a, dsinks = _swa_bwd(sv["zq"], p["sinks"], sv["attn_a"], dattn_a, sv["lse_a"])
    big = dict(w_proj_a=_cols_to_shards(g_proj_a), w_proj_b=_cols_to_shards(g_proj_b),
               w_out=g_out.reshape(N_DEV, D_MODEL // N_DEV, D_MODEL), w_ffn_in=g_ffn_in,
               w_ffn_out=g_ffn_out.reshape(N_DEV, D_FF // N_DEV, D_MODEL))
    dox, stats = _fox_stats(sv["attn_b32"], dattn_b, sv["lse_b"])
    dq_b, dk_b, dv_b, dcc, arrived = _fox_bwd(sv["qx"], sv["kx"], sv["vx"], dox, stats,
                                              None if make_comm is None else make_comm(big))
    dc = jnp.pad(dcc.reshape(s_len, N_PAIR, LANE)[:, :, :2].reshape(s_len, FOX_HEADS), ((0, 0), (0, LANE - FOX_HEADS)))
    df = _forget_bwd(dc, sv["zg"])
    dz = jnp.concatenate([dq_a, dk_a.astype(BF16), dv_a.astype(BF16), dq_b.astype(BF16), dk_b, dv_b, dga, dgb, df,
                          jnp.zeros((s_len, F_PAD - LANE), BF16)], axis=1)
    g_in = _unpad_cols_z(_linear_tn(sv["h"], dz, name="g_w_in", tn=768))
    g_b_in = _unpad_cols_z(_colsum(dz, name="g_b_in"))
    big["w_in"] = _cols_to_shards(g_in)
    small = dict(ln_mix_g=d_mix_g, ln_mix_b=d_mix_b, b_in=g_b_in, attn_sinks=dsinks[:, :SWA_HEADS],
                 ln_ffn_g=d_ffn_g, ln_ffn_b=d_ffn_b, conv_w=dcw[:3], conv_b=dcw[3:4])
    dh, arrived_last = _d_h(dz, w["w_in_p"], du1, None if make_last_comm is None else make_last_comm(big, small))
    return dh, big, small, arrived, arrived_last


def _w_in_layouts(w_in):
    return dict(w_in_p=_pad_cols_z(_shards_to_cols(w_in)))


def _other_layouts(w_proj_a, w_proj_b, w_out, w_ffn_in, w_ffn_out):
    return dict(w_proj_a=_shards_to_cols(w_proj_a), w_proj_b=_shards_to_cols(w_proj_b),
                w_out=w_out.reshape(D_MODEL, D_MODEL), w_ffn_in=w_ffn_in,
                w_ffn_out=w_ffn_out.reshape(N_CHUNK, FF_CHUNK, D_MODEL))


def _layer_params(r):
    return dict(
        b_in_p=_pad_cols_z(r["b_in"].reshape(1, N_IN)),
        sinks=jnp.pad(r["attn_sinks"].reshape(1, SWA_HEADS), ((0, 0), (0, LANE - SWA_HEADS))),
        ln_mix_g=r["ln_mix_g"].reshape(1, D_MODEL), ln_mix_b=r["ln_mix_b"].reshape(1, D_MODEL),
        ln_ffn_g=r["ln_ffn_g"].reshape(1, D_MODEL), ln_ffn_b=r["ln_ffn_b"].reshape(1, D_MODEL),
        conv_b=r["conv_b"].reshape(N_CHUNK, 1, FF_CHUNK))


def _conv_w_layout(conv_w):
    return jnp.pad(conv_w, ((0, 5), (0, 0))).reshape(8, N_CHUNK, FF_CHUNK).transpose(1, 0, 2)


def kernel(x, ln_mix_g, ln_mix_b, w_in, b_in, attn_sinks, w_proj_a, w_proj_b, w_out, ln_ffn_g, ln_ffn_b, w_ffn_in, conv_w, conv_b, w_ffn_out, loss_target, m_ln_mix_g, m_ln_mix_b, m_w_in, m_b_in, m_attn_sinks, m_w_proj_a, m_w_proj_b, m_w_out, m_ln_ffn_g, m_ln_ffn_b, m_w_ffn_in, m_conv_w, m_conv_b, m_w_ffn_out, v_ln_mix_g, v_ln_mix_b, v_w_in, v_b_in, v_attn_sinks, v_w_proj_a, v_w_proj_b, v_w_out, v_ln_ffn_g, v_ln_ffn_b, v_w_ffn_in, v_conv_w, v_conv_b, v_w_ffn_out):
    wts = dict(ln_mix_g=ln_mix_g, ln_mix_b=ln_mix_b, w_in=w_in, b_in=b_in, attn_sinks=attn_sinks, w_proj_a=w_proj_a,
               w_proj_b=w_proj_b, w_out=w_out, ln_ffn_g=ln_ffn_g, ln_ffn_b=ln_ffn_b, w_ffn_in=w_ffn_in,
               conv_w=conv_w, conv_b=conv_b, w_ffn_out=w_ffn_out)
    mom = dict(ln_mix_g=m_ln_mix_g, ln_mix_b=m_ln_mix_b, w_in=m_w_in, b_in=m_b_in, attn_sinks=m_attn_sinks,
               w_proj_a=m_w_proj_a, w_proj_b=m_w_proj_b, w_out=m_w_out, ln_ffn_g=m_ln_ffn_g, ln_ffn_b=m_ln_ffn_b,
               w_ffn_in=m_w_ffn_in, conv_w=m_conv_w, conv_b=m_conv_b, w_ffn_out=m_w_ffn_out)
    vel = dict(ln_mix_g=v_ln_mix_g, ln_mix_b=v_ln_mix_b, w_in=v_w_in, b_in=v_b_in, attn_sinks=v_attn_sinks,
               w_proj_a=v_w_proj_a, w_proj_b=v_w_proj_b, w_out=v_w_out, ln_ffn_g=v_ln_ffn_g, ln_ffn_b=v_ln_ffn_b,
               w_ffn_in=v_w_ffn_in, conv_w=v_conv_w, conv_b=v_conv_b, w_ffn_out=v_w_ffn_out)
    names = list(wts)
    big_names = [n for n, _, _ in BIG]
    small_names = [n for n, _ in SMALL]
    me = 4 * lax.axis_index("x") + 2 * lax.axis_index("y") + lax.axis_index("c")
    cw_shard = D_FF // N_DEV

    wb = {n: wts[n].astype(BF16) for n in big_names}
    ps = [_layer_params(dict(b_in=b_in[l], attn_sinks=attn_sinks[l], ln_mix_g=ln_mix_g[l], ln_mix_b=ln_mix_b[l],
                             ln_ffn_g=ln_ffn_g[l], ln_ffn_b=ln_ffn_b[l], conv_b=conv_b[l])) for l in range(DEPTH)]
    w_in_0, = _exchange([(wb["w_in"][0], True)], "gather_w_in_0")
    others = big_names[1:]
    gather_rest = _Comm([(wb[n][0], True) for n in others] + [(wb[n][1], True) for n in big_names] + [(conv_w, True)])
    next_layer = {}

    def late(w, p, arrived):
        conv_full = arrived[-1].transpose(1, 2, 0, 3).reshape(DEPTH, 3, D_FF)
        layer_1 = arrived[len(others):-1]
        next_layer["w"] = dict(_w_in_layouts(layer_1[0]), **_other_layouts(*layer_1[1:]))
        next_layer["p"] = dict(ps[1], conv_w=_conv_w_layout(conv_full[1]))
        return dict(w, **_other_layouts(*arrived[:len(others)])), dict(p, conv_w=_conv_w_layout(conv_full[0]))

    saved, ws = [None] * DEPTH, [None] * DEPTH
    h, saved[0], ws[0], ps[0] = _layer_fwd(x[0], _w_in_layouts(w_in_0), ps[0], gather_rest, late)
    h, saved[1], ws[1], ps[1] = _layer_fwd(h, next_layer["w"], next_layer["p"])
    dh, loss_part = _loss_head(h, loss_target[0])

    def small_rows(small):
        vec = jnp.concatenate([small[n].reshape(-1) for n in small_names] + [loss_part[0, 0].reshape(1)])
        return _to_rows(vec, SMALL_LAYER_ROWS)

    dh, big_1, small_1, _, _ = _layer_bwd(dh, saved[1], ws[1], ps[1])

    def exchange_early(big_0):
        return _Comm([(big_1[n].astype(BF16), False) for n in big_names] + [(small_rows(small_1), True)]
                     + [(big_0[n].astype(BF16), False) for n in others])

    def exchange_last(big_0, small_0):
        return _Comm([(big_0["w_in"].astype(BF16), False), (small_rows(small_0), True)])

    grad_x, _, _, arrived, (g_in_0, g_small_0) = _layer_bwd(dh, saved[0], ws[0], ps[0], exchange_early, exchange_last)
    n_big = len(big_names)
    recv = [[g_in_0] + list(arrived[n_big + 1:]) + [g_small_0], list(arrived[:n_big + 1])]

    big_out = {}
    for t, n in enumerate(big_names):
        outs = None
        for l in reversed(range(DEPTH)):
            outs = _adamw_shard(recv[l][t], wts[n], mom[n], vel[n], l, outs, "adamw_%s_%d" % (n, l))
        big_out[n] = outs
    small_sum = [_sum8(recv[l][-1]).reshape(-1) for l in range(DEPTH)]
    g_small = {}
    off = 0
    for n, size in SMALL:
        g_small[n] = jnp.stack([small_sum[l][off:off + size] for l in range(DEPTH)])
        off += size
    loss = small_sum[0][off]
    g_small["conv_w"] = lax.dynamic_slice_in_dim(g_small["conv_w"].reshape(DEPTH, 3, D_FF), me * cw_shard, cw_shard,
                                                 axis=2)
    g_small = {n: g_small[n].reshape(wts[n].shape) for n in small_names}

    def pack_small(tree):
        return _to_rows(jnp.concatenate([tree[n].reshape(-1) for n in small_names]), SMALL_ROWS)

    small_out = (pack_small(g_small),) + tuple(_adamw_rows(pack_small(wts), pack_small(g_small), pack_small(mom),
                                                           pack_small(vel), "adamw_small"))

    def result(j):
        out = {n: big_out[n][j] for n in big_names}
        flat = small_out[j].reshape(-1)
        off = 0
        for n in small_names:
            out[n] = flat[off:off + wts[n].size].reshape(wts[n].shape)
            off += wts[n].size
        return [out[n] for n in names]

    return (loss, grad_x[None], *result(0), *result(1), *result(2), *result(3))
```

```python
import functools

import jax
import jax.numpy as jnp
import numpy as np
from jax import lax
from jax.experimental import pallas as pl
from jax.experimental.pallas import tpu as pltpu

F32 = jnp.float32
BF16 = jnp.bfloat16
MESH = pl.DeviceIdType.MESH

N_DEV = 8
DEPTH = 2
D_MODEL = 1024
HEAD_DIM = 64
SWA_Q = 512
SWA_KV = 128
FOX_W = 512
FOX_HEADS = 8
SWA_HEADS = 8
D_FF = 2816
N_IN = 4360
N_QKV = SWA_Q + 2 * SWA_KV + 3 * FOX_W
N_GATE = 2 * D_MODEL
F_PAD = 256
N_ZG = N_GATE + F_PAD
N_ZP = N_QKV + N_ZG
LN_EPS = 1e-5
NEG_INF = -1e30
ALPHA = (2 * DEPTH) ** 0.25
SCALE = HEAD_DIM ** -0.5
LOG2E = 1.4426950408889634
SLOPES = tuple(2.0 ** (-8.0 * (h + 1) / SWA_HEADS) for h in range(SWA_HEADS))

ADAM_LR = 0.001
ADAM_B1 = 0.9
ADAM_B2 = 0.999
ADAM_EPS = 1e-08
ADAM_WD = 0.01
ADAM_STEP = 10

LANE = 128
VMEM_LIMIT = 56 * 1024 * 1024

BIG = (("w_in", (D_MODEL, N_IN), 1), ("w_proj_a", (SWA_Q, D_MODEL), 1), ("w_proj_b", (FOX_W, D_MODEL), 1),
       ("w_out", (D_MODEL, D_MODEL), 0), ("w_ffn_in", (D_MODEL, 2 * D_FF), 1), ("w_ffn_out", (D_FF, D_MODEL), 0))
SMALL = (("ln_mix_g", D_MODEL), ("ln_mix_b", D_MODEL), ("b_in", N_IN), ("attn_sinks", SWA_HEADS),
         ("ln_ffn_g", D_MODEL), ("ln_ffn_b", D_MODEL), ("conv_w", 3 * D_FF), ("conv_b", D_FF))
ROW_BLOCK = 512
SMALL_LAYER_ROWS = -(-(sum(n for _, n in SMALL) + 1) // (8 * LANE)) * 8
SMALL_ROWS = ROW_BLOCK
FF_CHUNK = 2 * D_FF // N_DEV
N_CHUNK = D_FF // FF_CHUNK


def _div(n, cap, unit):
    if n <= cap:
        return n
    best = None
    for t in range(unit, cap + 1, unit):
        if n % t == 0:
            best = t
    assert best is not None, (n, cap, unit)
    return best


def _params(*sem):
    return pltpu.CompilerParams(dimension_semantics=sem, vmem_limit_bytes=VMEM_LIMIT)


def _peer(r):
    x, y, c = lax.axis_index("x"), lax.axis_index("y"), lax.axis_index("c")
    px = 1 - x if (r >> 2) & 1 else x
    py = 1 - y if (r >> 1) & 1 else y
    pc = 1 - c if r & 1 else c
    return (px, py, pc), 4 * px + 2 * py + pc


class _Comm:
    def __init__(self, tensors):
        self.arrays = [x for x, _ in tensors]
        self.gathers = [g for _, g in tensors]
        self.n = len(tensors)
        self.out_shape = [jax.ShapeDtypeStruct((N_DEV,) + (x.shape if g else x.shape[1:]), x.dtype)
                          for x, g in tensors]
        self.specs = [pl.BlockSpec(memory_space=pl.ANY)] * self.n
        self.scratch = [pltpu.SemaphoreType.DMA((N_DEV - 1, self.n)), pltpu.SemaphoreType.DMA((N_DEV - 1, self.n)),
                        pltpu.SemaphoreType.DMA((self.n,))]

    def _copies(self, x_refs, out_refs, sems):
        send_sems, recv_sems, local_sems = sems
        _, me = _peer(0)

        def src(t, idx):
            return x_refs[t] if self.gathers[t] else x_refs[t].at[idx]

        def remote(r, t, mine):
            peer, pid = _peer(r)
            return pltpu.make_async_remote_copy(src_ref=src(t, pid), dst_ref=out_refs[t].at[me if mine else pid],
                                                send_sem=send_sems.at[r - 1, t], recv_sem=recv_sems.at[r - 1, t],
                                                device_id=peer, device_id_type=MESH)

        pairs = [(r, t) for r in range(1, N_DEV) for t in range(self.n)]
        local = [pltpu.make_async_copy(src(t, me), out_refs[t].at[me], local_sems.at[t]) for t in range(self.n)]
        return local, [remote(r, t, True) for r, t in pairs], lambda: [remote(r, t, False) for r, t in pairs]

    def start(self, x_refs, out_refs, sems):
        local, sent, _ = self._copies(x_refs, out_refs, sems)
        for cp in local + sent:
            cp.start()

    def wait(self, x_refs, out_refs, sems):
        local, sent, landing = self._copies(x_refs, out_refs, sems)
        for cp in landing():
            cp.wait_recv()
        for cp in sent:
            cp.wait_send()
        for cp in local:
            cp.wait()


def _exchange(tensors, name):
    comm = _Comm(tensors)
    n = comm.n

    def body(*refs):
        comm.start(refs[:n], refs[n:2 * n], refs[2 * n:])
        comm.wait(refs[:n], refs[n:2 * n], refs[2 * n:])

    return pl.pallas_call(body, name=name, out_shape=comm.out_shape, in_specs=comm.specs, out_specs=comm.specs,
                          scratch_shapes=comm.scratch)(*comm.arrays)


def _with_comm(comm, n_in, n_out, first, last, compute):
    nc = comm.n if comm is not None else 0

    def body(*refs):
        ins, x_refs = refs[:n_in], refs[n_in:n_in + nc]
        outs = refs[n_in + nc:n_in + nc + n_out]
        out_refs = refs[n_in + nc + n_out:n_in + 2 * nc + n_out]
        sems = refs[n_in + 2 * nc + n_out:]
        if nc:
            @pl.when(first())
            def _():
                comm.start(x_refs, out_refs, sems)

        compute(*ins, *outs)
        if nc:
            @pl.when(last())
            def _():
                comm.wait(x_refs, out_refs, sems)

    return body


def _d_h(dz, w_in_p, res, comm=None, tm=512):
    m, k = dz.shape
    d = w_in_p.shape[0]
    tm = _div(m, tm, 8)
    steps = m // tm
    c_specs, c_shapes, c_scratch, c_arrays = _comm_parts(comm)

    def compute(dz_ref, w_ref, res_ref, o_ref):
        o_ref[...] = ALPHA * res_ref[...] + lax.dot_general(dz_ref[...], w_ref[...], (((1,), (1,)), ((), ())),
                                                            preferred_element_type=F32)

    body = _with_comm(comm, 3, 1, lambda: pl.program_id(0) == 0, lambda: pl.program_id(0) == steps - 1, compute)
    row = pl.BlockSpec((tm, d), lambda i: (i, 0))
    outs = pl.pallas_call(
        body, name="d_h" if comm is None else "d_h_comm", grid=(steps,),
        in_specs=[pl.BlockSpec((tm, k), lambda i: (i, 0)), pl.BlockSpec((d, k), lambda i: (0, 0)), row] + c_specs,
        out_specs=[row] + c_specs, out_shape=[jax.ShapeDtypeStruct((m, d), F32)] + c_shapes,
        scratch_shapes=c_scratch,
        compiler_params=_params("arbitrary"),
    )(dz, w_in_p, res, *c_arrays)
    return outs[0], outs[1:]


def _z_proj(h, w_in_p, b_p, tm=512):
    m, k = h.shape
    tm = _div(m, tm, 8)

    def body(h_ref, w_ref, b_ref, zq_ref, zg_ref):
        a = h_ref[...].astype(BF16)
        zq_ref[...] = (jnp.dot(a, w_ref[:, :N_QKV], preferred_element_type=F32) + b_ref[:, :N_QKV]).astype(BF16)
        zg_ref[...] = jnp.dot(a, w_ref[:, N_QKV:], preferred_element_type=F32) + b_ref[:, N_QKV:]

    return pl.pallas_call(
        body, name="z_proj", grid=(m // tm,),
        in_specs=[pl.BlockSpec((tm, k), lambda i: (i, 0)), pl.BlockSpec((k, N_ZP), lambda i: (0, 0)),
                  pl.BlockSpec((1, N_ZP), lambda i: (0, 0))],
        out_specs=[pl.BlockSpec((tm, N_QKV), lambda i: (i, 0)), pl.BlockSpec((tm, N_ZG), lambda i: (i, 0))],
        out_shape=[jax.ShapeDtypeStruct((m, N_QKV), BF16), jax.ShapeDtypeStruct((m, N_ZG), F32)],
        compiler_params=_params("parallel"),
    )(h, w_in_p, b_p)


def _linear_tn(a, g, *, name, tk=1024, tn=640, tm=2048):
    m, k = a.shape
    n = g.shape[1]
    tk = _div(k, tk, LANE)
    tn = _div(n, tn, LANE)
    tm = _div(m, tm, 8)
    steps = m // tm

    def body(a_ref, g_ref, o_ref, acc_ref):
        s = pl.program_id(2)

        @pl.when(s == 0)
        def _():
            acc_ref[...] = jnp.zeros_like(acc_ref)

        acc_ref[...] += lax.dot_general(a_ref[...].astype(BF16), g_ref[...].astype(BF16), (((0,), (0,)), ((), ())),
                                        preferred_element_type=F32)

        @pl.when(s == steps - 1)
        def _():
            o_ref[...] = acc_ref[...].astype(BF16)

    return pl.pallas_call(
        body, name=name, grid=(k // tk, n // tn, steps),
        in_specs=[pl.BlockSpec((tm, tk), lambda i, j, s: (s, i)), pl.BlockSpec((tm, tn), lambda i, j, s: (s, j))],
        out_specs=pl.BlockSpec((tk, tn), lambda i, j, s: (i, j)),
        out_shape=jax.ShapeDtypeStruct((k, n), BF16),
        scratch_shapes=[pltpu.VMEM((tk, tn), F32)],
        compiler_params=_params("parallel", "parallel", "arbitrary"),
    )(a, g)


def _colsum(g, *, name, tm=512):
    m, n = g.shape
    tm = _div(m, tm, 8)

    def body(g_ref, o_ref):
        @pl.when(pl.program_id(0) == 0)
        def _():
            o_ref[...] = jnp.zeros_like(o_ref)

        o_ref[...] += jnp.sum(g_ref[...].astype(F32), axis=0, keepdims=True)

    return pl.pallas_call(
        body, name=name, grid=(m // tm,),
        in_specs=[pl.BlockSpec((tm, n), lambda i: (i, 0))],
        out_specs=pl.BlockSpec((1, n), lambda i: (0, 0)),
        out_shape=jax.ShapeDtypeStruct((1, n), F32),
        compiler_params=_params("arbitrary"),
    )(g)


def _ln(u, g, b):
    mu = jnp.mean(u, axis=-1, keepdims=True)
    d = u - mu
    var = jnp.mean(d * d, axis=-1, keepdims=True)
    return d * lax.rsqrt(var + LN_EPS) * g + b


def _ln_bwd_block(dy, u, g):
    mu = jnp.mean(u, axis=-1, keepdims=True)
    dd = u - mu
    rstd = lax.rsqrt(jnp.mean(dd * dd, axis=-1, keepdims=True) + LN_EPS)
    xhat = dd * rstd
    dxh = dy * g
    m1 = jnp.mean(dxh, axis=-1, keepdims=True)
    m2 = jnp.mean(dxh * xhat, axis=-1, keepdims=True)
    return (rstd * (dxh - m1 - xhat * m2), jnp.sum(dy * xhat, axis=0, keepdims=True),
            jnp.sum(dy, axis=0, keepdims=True))


SCAN_ROWS = 512


def _tri(n, upper):
    r = lax.broadcasted_iota(jnp.int32, (n, n), 0)
    c = lax.broadcasted_iota(jnp.int32, (n, n), 1)
    return jnp.where((c >= r) if upper else (c <= r), 1.0, 0.0).astype(F32)


def _cumsum_logf(zg):
    s = zg.shape[0]
    t = _div(s, SCAN_ROWS, LANE)
    nb = s // t
    fcol = N_GATE // LANE

    def body(f_ref, c_ref, carry_ref):
        @pl.when(pl.program_id(0) == 0)
        def _():
            carry_ref[...] = jnp.zeros_like(carry_ref)

        f = f_ref[...]
        logf = jnp.minimum(f, 0.0) - jnp.log(1.0 + jnp.exp(-jnp.abs(f)))
        c = jnp.dot(_tri(t, False), logf, precision=lax.Precision.HIGHEST, preferred_element_type=F32)
        c = c + carry_ref[0:1, :]
        c_ref[...] = c
        carry_ref[...] = jnp.broadcast_to(c[t - 1:t, :], carry_ref.shape)

    return pl.pallas_call(
        body, name="cumsum_logf", grid=(nb,),
        in_specs=[pl.BlockSpec((t, LANE), lambda i: (i, fcol))],
        out_specs=pl.BlockSpec((t, LANE), lambda i: (i, 0)),
        out_shape=jax.ShapeDtypeStruct((s, LANE), F32),
        scratch_shapes=[pltpu.VMEM((8, LANE), F32)],
        compiler_params=_params("arbitrary"),
    )(zg)


def _forget_bwd(dcc, zg):
    s = zg.shape[0]
    t = _div(s, SCAN_ROWS, LANE)
    nb = s // t
    fcol = N_GATE // LANE

    def body(dc_ref, f_ref, o_ref, carry_ref):
        @pl.when(pl.program_id(0) == 0)
        def _():
            carry_ref[...] = jnp.zeros_like(carry_ref)

        lane = lax.broadcasted_iota(jnp.int32, (1, LANE), 1)
        dc = jnp.zeros((t, LANE), F32)
        for p in range(FOX_HEADS // 2):
            tile = dc_ref[:, p * LANE:(p + 1) * LANE]
            moved = pltpu.roll(tile, 2 * p, 1) if p else tile
            dc = jnp.where((lane == 2 * p) | (lane == 2 * p + 1), moved, dc)
        dlogf = jnp.dot(_tri(t, True), dc, precision=lax.Precision.HIGHEST, preferred_element_type=F32)
        dlogf = dlogf + carry_ref[0:1, :]
        o_ref[...] = (dlogf * jax.nn.sigmoid(-f_ref[...])).astype(BF16)
        carry_ref[...] = jnp.broadcast_to(dlogf[0:1, :], carry_ref.shape)

    return pl.pallas_call(
        body, name="forget_bwd", grid=(nb,),
        in_specs=[pl.BlockSpec((t, FOX_W), lambda i: (nb - 1 - i, 0)),
                  pl.BlockSpec((t, LANE), lambda i: (nb - 1 - i, fcol))],
        out_specs=pl.BlockSpec((t, LANE), lambda i: (nb - 1 - i, 0)),
        out_shape=jax.ShapeDtypeStruct((s, LANE), BF16),
        scratch_shapes=[pltpu.VMEM((8, LANE), F32)],
        compiler_params=_params("arbitrary"),
    )(dcc, zg)


KA_COL = SWA_Q // LANE
VA_COL = KA_COL + 1


def _half_masks():
    lane = lax.broadcasted_iota(jnp.int32, (1, LANE), 1)
    hi = lane >= HEAD_DIM
    return (jnp.logical_not(hi), hi)


def _both_halves(x, sel):
    xs = jnp.where(sel, x, 0.0)
    return xs + pltpu.roll(xs, HEAD_DIM, 1)


SWA_PER_KV = 4
WIDE = SWA_PER_KV * LANE


def _swa_bias():
    k = np.arange(2 * LANE)[:, None]
    q = np.arange(LANE)[None, :]
    dist = (q + LANE - k).astype(np.float32)
    valid = (dist >= 0) & (dist < LANE)
    per_head = [np.where(valid, np.float32(-s) * dist, np.float32(NEG_INF)) for s in SLOPES]
    return jnp.asarray(np.stack([np.concatenate(per_head[SWA_PER_KV * hk:SWA_PER_KV * (hk + 1)], axis=1)
                                 for hk in range(2)]), F32)


def _no_previous_block(i_blk):
    k = lax.broadcasted_iota(jnp.int32, (2 * LANE, WIDE), 0)
    return jnp.where((i_blk == 0) & (k < LANE), NEG_INF, 0.0)


def _stack_heads(ref, blk, hk, halves, scale):
    tiles = []
    for j in range(SWA_PER_KV):
        p = 2 * hk + j // 2
        t = ref[blk, p * LANE:(p + 1) * LANE]
        if scale:
            t = _scaled(t)
        tiles.append(jnp.where(halves[j % 2], t, jnp.zeros_like(t)))
    return jnp.concatenate(tiles, axis=0)


def _pair_tile(wide, pp, row_halves):
    a = wide[:, (2 * pp) * LANE:(2 * pp + 1) * LANE]
    b = wide[:, (2 * pp + 1) * LANE:(2 * pp + 2) * LANE]
    return jnp.where(row_halves[0], a, b).T


def _lane_blocks(rows8, hk):
    return jnp.concatenate([rows8[SWA_PER_KV * hk + j:SWA_PER_KV * hk + j + 1, :] for j in range(SWA_PER_KV)], axis=1)


def _row_halves():
    hi = lax.broadcasted_iota(jnp.int32, (LANE, 1), 0) >= HEAD_DIM
    return (jnp.logical_not(hi), hi)


NT = (((1,), (1,)), ((), ()))


def _scaled(q):
    return (q.astype(F32) * SCALE).astype(BF16)


SWA_GROUP = 4


def _swa_group(s_len):
    return SWA_GROUP if (s_len // LANE) % SWA_GROUP == 0 else 1


def _swa_specs(group):
    rows = group * LANE
    prev = lambda i: jnp.maximum(i * group - 1, 0)
    return [pl.BlockSpec((rows, SWA_Q), lambda i: (i, 0)),
            pl.BlockSpec((rows, LANE), lambda i: (i, KA_COL)), pl.BlockSpec((rows, LANE), lambda i: (i, VA_COL)),
            pl.BlockSpec((LANE, LANE), lambda i: (prev(i), KA_COL)),
            pl.BlockSpec((LANE, LANE), lambda i: (prev(i), VA_COL))]


def _swa_window(g, cur_ref, prev_ref):
    before = prev_ref[...] if g == 0 else cur_ref[(g - 1) * LANE:g * LANE, :]
    return jnp.concatenate([before, cur_ref[g * LANE:(g + 1) * LANE, :]], axis=0).astype(F32)


def _swa_fwd(zq, sinks):
    s_len = zq.shape[0]
    group = _swa_group(s_len)
    rows = group * LANE
    sink_lanes = jnp.repeat(sinks[:, :SWA_HEADS], LANE, axis=1)

    def body(q_ref, kc_ref, vc_ref, kp_ref, vp_ref, sink_ref, bias_ref, o_ref, lse_ref):
        halves = _half_masks()
        row_halves = _row_halves()
        for g in range(group):
            blk = slice(g * LANE, (g + 1) * LANE)
            kcat = _swa_window(g, kc_ref, kp_ref)
            vcat = _swa_window(g, vc_ref, vp_ref)
            lse_rows = []
            for hk in range(2):
                kb = _both_halves(kcat, halves[hk]).astype(BF16)
                v_t = _both_halves(vcat, halves[hk]).T.astype(BF16)
                q4 = _stack_heads(q_ref, blk, hk, halves, True)
                s_t = lax.dot_general(kb, q4, NT, preferred_element_type=F32) + bias_ref[hk]
                if g == 0:
                    s_t = s_t + _no_previous_block(pl.program_id(0))
                sink = sink_ref[:, hk * WIDE:(hk + 1) * WIDE]
                m = jnp.maximum(jnp.max(s_t, axis=0, keepdims=True), sink)
                pe = jnp.exp(s_t - m)
                den = jnp.sum(pe, axis=0, keepdims=True) + jnp.exp(sink - m)
                out_t = jnp.dot(v_t, (pe * (1.0 / den)).astype(BF16), preferred_element_type=F32)
                for pp in range(2):
                    p = 2 * hk + pp
                    o_ref[blk, p * LANE:(p + 1) * LANE] = _pair_tile(out_t, pp, row_halves).astype(BF16)
                lse4 = m + jnp.log(den)
                lse_rows += [lse4[:, j * LANE:(j + 1) * LANE] for j in range(SWA_PER_KV)]
            lse_ref[:, blk] = jnp.concatenate(lse_rows, axis=0)

    return pl.pallas_call(
        body, name="swa_fwd", grid=(s_len // rows,),
        in_specs=_swa_specs(group) + [pl.BlockSpec((1, SWA_HEADS * LANE), lambda i: (0, 0)),
                                      pl.BlockSpec((2, 2 * LANE, WIDE), lambda i: (0, 0, 0))],
        out_specs=[pl.BlockSpec((rows, SWA_Q), lambda i: (i, 0)), pl.BlockSpec((SWA_HEADS, rows), lambda i: (0, i))],
        out_shape=[jax.ShapeDtypeStruct((s_len, SWA_Q), BF16), jax.ShapeDtypeStruct((SWA_HEADS, s_len), F32)],
        compiler_params=_params("parallel"),
    )(zq, zq, zq, zq, zq, sink_lanes, _swa_bias())


def _swa_bwd(zq, sinks, o, do, lse):
    s_len = zq.shape[0]
    group = _swa_group(s_len)
    rows = group * LANE

    def body(q_ref, kc_ref, vc_ref, kp_ref, vp_ref, sink_ref, bias_ref, o_ref, do_ref, lse_ref,
             dq_ref, dk_ref, dv_ref, ds_ref):
        halves = _half_masks()
        row_halves = _row_halves()
        lane = lax.broadcasted_iota(jnp.int32, (1, LANE), 1)
        dsink = jnp.zeros((1, LANE), F32)
        for g in range(group):
            blk = slice(g * LANE, (g + 1) * LANE)
            i_blk = pl.program_id(0) * group + g
            kcat = _swa_window(g, kc_ref, kp_ref)
            vcat = _swa_window(g, vc_ref, vp_ref)
            lse_rows = lse_ref[:, blk]
            prod = do_ref[blk, :].astype(F32) * o_ref[blk, :].astype(F32)
            select = (lax.broadcasted_iota(jnp.int32, (SWA_HEADS, SWA_Q), 1) // HEAD_DIM
                      == lax.broadcasted_iota(jnp.int32, (SWA_HEADS, SWA_Q), 0))
            delta_rows = lax.dot_general(jnp.where(select, 1.0, 0.0), prod, NT, precision=lax.Precision.HIGHEST,
                                         preferred_element_type=F32)
            dk_tot = jnp.zeros((2 * LANE, LANE), F32)
            dv_tot = jnp.zeros((2 * LANE, LANE), F32)
            for hk in range(2):
                kb = _both_halves(kcat, halves[hk])
                k_t = kb.T.astype(BF16)
                kb = kb.astype(BF16)
                vb = _both_halves(vcat, halves[hk]).astype(BF16)
                q4 = _stack_heads(q_ref, blk, hk, halves, True)
                do4 = _stack_heads(do_ref, blk, hk, halves, False)
                lse4 = _lane_blocks(lse_rows, hk)
                delta4 = _lane_blocks(delta_rows, hk)
                s_t = lax.dot_general(kb, q4, NT, preferred_element_type=F32) + bias_ref[hk]
                if g == 0:
                    s_t = s_t + _no_previous_block(pl.program_id(0))
                p_t = jnp.exp(s_t - lse4)
                dp_t = lax.dot_general(vb, do4, NT, preferred_element_type=F32)
                ds_t = (p_t * (dp_t - delta4)).astype(BF16)
                sink_part = jnp.exp(sink_ref[:, hk * WIDE:(hk + 1) * WIDE] - lse4) * delta4
                for j in range(SWA_PER_KV):
                    dsink_h = -jnp.sum(sink_part[:, j * LANE:(j + 1) * LANE], axis=1, keepdims=True)
                    dsink = dsink + jnp.where(lane == SWA_PER_KV * hk + j, dsink_h, 0.0)
                dq_t = jnp.dot(k_t, ds_t, preferred_element_type=F32)
                for pp in range(2):
                    p = 2 * hk + pp
                    dq_ref[blk, p * LANE:(p + 1) * LANE] = (_pair_tile(dq_t, pp, row_halves) * SCALE).astype(BF16)
                dk_acc = jnp.dot(ds_t, q4, preferred_element_type=F32)
                dv_acc = jnp.dot(p_t.astype(BF16), do4, preferred_element_type=F32)
                dk_tot = dk_tot + jnp.where(halves[hk], dk_acc + pltpu.roll(dk_acc, HEAD_DIM, 1), 0.0)
                dv_tot = dv_tot + jnp.where(halves[hk], dv_acc + pltpu.roll(dv_acc, HEAD_DIM, 1), 0.0)
            cur = pl.ds(pl.multiple_of(i_blk * LANE, LANE), LANE)
            dk_ref[cur, :] = dk_tot[LANE:, :]
            dv_ref[cur, :] = dv_tot[LANE:, :]

            def add_previous(i_blk=i_blk, dk_tot=dk_tot, dv_tot=dv_tot):
                prv = pl.ds(pl.multiple_of((i_blk - 1) * LANE, LANE), LANE)
                dk_ref[prv, :] += dk_tot[:LANE, :]
                dv_ref[prv, :] += dv_tot[:LANE, :]

            if g == 0:
                pl.when(i_blk > 0)(add_previous)
            else:
                add_previous()

        @pl.when(pl.program_id(0) == 0)
        def _():
            ds_ref[...] = jnp.zeros_like(ds_ref)

        ds_ref[...] += dsink

    blk512 = pl.BlockSpec((rows, SWA_Q), lambda i: (i, 0))
    full = pl.BlockSpec((s_len, LANE), lambda i: (0, 0))
    vec = pl.BlockSpec((1, LANE), lambda i: (0, 0))
    return pl.pallas_call(
        body, name="swa_bwd", grid=(s_len // rows,),
        in_specs=_swa_specs(group) + [pl.BlockSpec((1, SWA_HEADS * LANE), lambda i: (0, 0)),
                                      pl.BlockSpec((2, 2 * LANE, WIDE), lambda i: (0, 0, 0)), blk512, blk512,
                                      pl.BlockSpec((SWA_HEADS, rows), lambda i: (0, i))],
        out_specs=[blk512, full, full, vec],
        out_shape=[jax.ShapeDtypeStruct((s_len, SWA_Q), BF16), jax.ShapeDtypeStruct((s_len, LANE), F32),
                   jax.ShapeDtypeStruct((s_len, LANE), F32), jax.ShapeDtypeStruct((1, LANE), F32)],
        compiler_params=_params("arbitrary"),
    )(zq, zq, zq, zq, zq, jnp.repeat(sinks[:, :SWA_HEADS], LANE, axis=1), _swa_bias(), o, do, lse)


QB_COL = (SWA_Q + 2 * SWA_KV) // LANE
KB_COL = QB_COL + FOX_W // LANE
VB_COL = KB_COL + FOX_W // LANE
N_PAIR = FOX_HEADS // 2


def _causal(t, keys_first=False):
    r = lax.broadcasted_iota(jnp.int32, (t, t), 0)
    c = lax.broadcasted_iota(jnp.int32, (t, t), 1)
    return c >= r if keys_first else r >= c


N_SPLIT = 3


def _own_half(e):
    hi = lax.broadcasted_iota(jnp.int32, (1, LANE), 1) >= HEAD_DIM
    return hi if e else jnp.logical_not(hi)


def _feature_lane(e, t):
    return HEAD_DIM * (1 - e) + t


def _fox_prep(zq, c, tm=256):
    s_len = zq.shape[0]
    tm = _div(s_len, tm, 8)

    def body(z_ref, c_ref, qx_ref, kx_ref, vx_ref):
        lane = lax.broadcasted_iota(jnp.int32, (1, LANE), 1)
        for h in range(FOX_HEADS):
            p, e = divmod(h, 2)
            own = _own_half(e)
            tile = lambda col: z_ref[:, (col + p) * LANE:(col + p + 1) * LANE].astype(F32)
            rest = c_ref[:, h:h + 1]
            qf = jnp.zeros((tm, LANE), F32)
            kf = jnp.zeros((tm, LANE), F32)
            for t in range(N_SPLIT):
                part = rest.astype(BF16).astype(F32)
                rest = rest - part
                qf = jnp.where(lane == _feature_lane(e, t), part, qf)
                qf = jnp.where(lane == _feature_lane(e, N_SPLIT + t), 1.0, qf)
                kf = jnp.where(lane == _feature_lane(e, t), 1.0, kf)
                kf = jnp.where(lane == _feature_lane(e, N_SPLIT + t), -part, kf)
            vf = jnp.where(lane == _feature_lane(e, 0), 1.0, 0.0)
            cols = slice(h * LANE, (h + 1) * LANE)
            qx_ref[:, cols] = jnp.where(own, tile(QB_COL) * SCALE, qf).astype(BF16)
            kx_ref[:, cols] = jnp.where(own, tile(KB_COL), kf).astype(BF16)
            vx_ref[:, cols] = jnp.where(own, tile(VB_COL), vf).astype(BF16)

    out = jax.ShapeDtypeStruct((s_len, FOX_HEADS * LANE), BF16)
    blk = pl.BlockSpec((tm, FOX_HEADS * LANE), lambda i: (i, 0))
    return pl.pallas_call(
        body, name="fox_prep", grid=(s_len // tm,),
        in_specs=[pl.BlockSpec((tm, N_QKV), lambda i: (i, 0)), pl.BlockSpec((tm, LANE), lambda i: (i, 0))],
        out_specs=[blk, blk, blk], out_shape=[out, out, out],
        compiler_params=_params("parallel"),
    )(zq, c)


def _comm_parts(comm):
    return ([], [], [], []) if comm is None else (comm.specs, comm.out_shape, comm.scratch, comm.arrays)


def _fox_fwd(qx, kx, vx, comm=None, t_cap=1024):
    s_len = qx.shape[0]
    t = _div(s_len, t_cap, LANE)
    nq = s_len // t
    c_specs, c_shapes, c_scratch, c_arrays = _comm_parts(comm)

    def compute(q_ref, k_ref, v_ref, o_ref, o32_ref, m_ref, l_ref):
        i = pl.program_id(1)
        qs = [q_ref[:, e * LANE:(e + 1) * LANE] for e in range(2)]

        def step(j, carry, diag):
            rows = pl.ds(pl.multiple_of(j * t, t), t)
            new = []
            for e in range(2):
                m, acc = carry[e]
                s2 = lax.dot_general(qs[e], k_ref[rows, e * LANE:(e + 1) * LANE], NT,
                                     preferred_element_type=F32) * LOG2E
                if diag:
                    s2 = jnp.where(_causal(t), s2, NEG_INF)
                mn = jnp.maximum(m, jnp.ceil(jnp.max(s2, axis=1, keepdims=True)))
                pe = jnp.exp2(s2 - mn).astype(BF16)
                acc = acc * jnp.exp2(m - mn) + jnp.dot(pe, v_ref[rows, e * LANE:(e + 1) * LANE],
                                                       preferred_element_type=F32)
                new.append((mn, acc))
            return tuple(new)

        init = (jnp.full((t, 1), NEG_INF, F32), jnp.zeros((t, LANE), F32))
        carry = lax.fori_loop(0, i, lambda j, c: step(j, c, False), (init, init))
        carry = step(i, carry, True)
        outs, ls = [], []
        for e in range(2):
            m, acc = carry[e]
            l = acc[:, _feature_lane(e, 0):_feature_lane(e, 0) + 1]
            outs.append(acc / l)
            ls.append(l)
        out = jnp.where(_own_half(1), outs[1], outs[0])
        o_ref[...] = out.astype(BF16)
        o32_ref[...] = out
        m_ref[...] = jnp.where(_own_half(1), carry[1][0], carry[0][0])
        l_ref[...] = jnp.where(_own_half(1), ls[1], ls[0])

    body = _with_comm(comm, 3, 4, lambda: (pl.program_id(0) == 0) & (pl.program_id(1) == 0),
                      lambda: (pl.program_id(0) == N_PAIR - 1) & (pl.program_id(1) == nq - 1), compute)
    pair = pl.BlockSpec((s_len, 2 * LANE), lambda p, i: (0, p))
    tile = pl.BlockSpec((t, LANE), lambda p, i: (i, p))
    wide = jax.ShapeDtypeStruct((s_len, FOX_W), F32)
    outs = pl.pallas_call(
        body, name="fox_fwd" if comm is None else "fox_fwd_comm", grid=(N_PAIR, nq),
        in_specs=[pl.BlockSpec((t, 2 * LANE), lambda p, i: (i, p)), pair, pair] + c_specs,
        out_specs=[tile, tile, tile, tile] + c_specs,
        out_shape=[jax.ShapeDtypeStruct((s_len, FOX_W), BF16), wide, wide, wide] + c_shapes,
        scratch_shapes=c_scratch,
        compiler_params=_params("arbitrary", "arbitrary"),
    )(qx, kx, vx, *c_arrays)
    return outs[0], outs[1], outs[2], outs[3], outs[4:]


def _fox_stats(o, do, m, l, tm=256):
    s_len = o.shape[0]
    tm = _div(s_len, tm, LANE)

    def body(o_ref, do_ref, m_ref, l_ref, dox_ref, st_ref):
        lane = lax.broadcasted_iota(jnp.int32, (1, LANE), 1)
        for p in range(N_PAIR):
            cols = slice(p * LANE, (p + 1) * LANE)
            dout = do_ref[:, cols]
            prod = o_ref[:, cols] * dout.astype(F32)
            shift = m_ref[:, cols]
            inv_l = 1.0 / l_ref[:, cols]
            st = jnp.zeros((tm, LANE), F32)
            for e in range(2):
                h = 2 * p + e
                dox_ref[:, h * LANE:(h + 1) * LANE] = jnp.where(_own_half(e), dout, jnp.zeros_like(dout))
                st = jnp.where(lane == e, shift[:, e * HEAD_DIM:e * HEAD_DIM + 1], st)
                delta = jnp.sum(jnp.where(_own_half(e), prod, 0.0), axis=1, keepdims=True)
                st = jnp.where(lane == 2 + e, delta, st)
                st = jnp.where(lane == 4 + e, inv_l[:, e * HEAD_DIM:e * HEAD_DIM + 1], st)
            st_ref[p] = st.T[:8, :]

    row = pl.BlockSpec((tm, FOX_W), lambda i: (i, 0))
    return pl.pallas_call(
        body, name="fox_stats", grid=(s_len // tm,), in_specs=[row, row, row, row],
        out_specs=[pl.BlockSpec((tm, FOX_HEADS * LANE), lambda i: (i, 0)),
                   pl.BlockSpec((N_PAIR, 8, tm), lambda i: (0, 0, i))],
        out_shape=[jax.ShapeDtypeStruct((s_len, FOX_HEADS * LANE), BF16),
                   jax.ShapeDtypeStruct((N_PAIR, 8, s_len), F32)],
        compiler_params=_params("parallel"),
    )(o, do, m, l)


def _fox_bwd(qx, kx, vx, dox, stats, comm=None, t_cap=512):
    s_len = qx.shape[0]
    t = _div(s_len, t_cap, LANE)
    n = s_len // t
    c_specs, c_shapes, c_scratch, c_arrays = _comm_parts(comm)

    def compute(q_ref, do_ref, st_ref, k_ref, v_ref, dq_ref, dk_ref, dv_ref, dc_ref):
        j = pl.program_id(1)
        lane = lax.broadcasted_iota(jnp.int32, (1, LANE), 1)

        @pl.when(j == 0)
        def _():
            dq_ref[...] = jnp.zeros_like(dq_ref)

        ks = [k_ref[:, e * LANE:(e + 1) * LANE] for e in range(2)]
        vs = [v_ref[:, e * LANE:(e + 1) * LANE] for e in range(2)]
        ks_t = [k.astype(F32).T.astype(BF16) for k in ks]

        def step(i, carry, diag):
            rows = pl.ds(pl.multiple_of(i * t, t), t)
            new = []
            dq = jnp.zeros((LANE, t), F32)
            for e in range(2):
                dk, dv, dc = carry[e]
                q = q_ref[rows, e * LANE:(e + 1) * LANE]
                dout = do_ref[rows, e * LANE:(e + 1) * LANE]
                s_t = lax.dot_general(ks[e], q, NT, preferred_element_type=F32) * LOG2E
                if diag:
                    s_t = jnp.where(_causal(t, keys_first=True), s_t, NEG_INF)
                p_t = jnp.exp2(s_t - st_ref[0, e:e + 1, rows]).astype(BF16).astype(F32) * st_ref[0, 4 + e:5 + e, rows]
                dp_t = lax.dot_general(vs[e], dout, NT, preferred_element_type=F32)
                ds_f = p_t * (dp_t - st_ref[0, 2 + e:3 + e, rows])
                ds_t = ds_f.astype(BF16)
                dc = dc + jnp.sum(ds_f, axis=1, keepdims=True)
                dv = dv + jnp.dot(p_t.astype(BF16), dout, preferred_element_type=F32)
                dk = dk + jnp.dot(ds_t, q, preferred_element_type=F32)
                dq_e = jnp.dot(ks_t[e], ds_t, preferred_element_type=F32)
                dq = dq + jnp.where(_row_halves()[e], dq_e, 0.0)
                new.append((dk, dv, dc))
            dq_ref[rows, :] += dq.T * SCALE
            return tuple(new)

        zero = jnp.zeros((t, LANE), F32)
        init = (zero, zero, jnp.zeros((t, 1), F32))
        carry = step(j, (init, init), True)
        (dk0, dv0, dc0), (dk1, dv1, dc1) = lax.fori_loop(j + 1, n, lambda i, c: step(i, c, False), carry)
        dk_ref[...] = jnp.where(_own_half(1), dk1, dk0).astype(BF16)
        dv_ref[...] = jnp.where(_own_half(1), dv1, dv0).astype(BF16)
        dc_ref[...] = jnp.where(lane == 0, -dc0, jnp.where(lane == 1, -dc1, 0.0))

    body = _with_comm(comm, 5, 4, lambda: (pl.program_id(0) == 0) & (pl.program_id(1) == 0),
                      lambda: (pl.program_id(0) == N_PAIR - 1) & (pl.program_id(1) == n - 1), compute)
    pair = pl.BlockSpec((s_len, 2 * LANE), lambda p, j: (0, p))
    blk = pl.BlockSpec((t, 2 * LANE), lambda p, j: (j, p))
    tile = pl.BlockSpec((t, LANE), lambda p, j: (j, p))
    outs = pl.pallas_call(
        body, name="fox_bwd" if comm is None else "fox_bwd_comm", grid=(N_PAIR, n),
        in_specs=[pair, pair, pl.BlockSpec((1, 8, s_len), lambda p, j: (p, 0, 0)), blk, blk] + c_specs,
        out_specs=[pl.BlockSpec((s_len, LANE), lambda p, j: (0, p)), tile, tile, tile] + c_specs,
        out_shape=[jax.ShapeDtypeStruct((s_len, FOX_W), F32), jax.ShapeDtypeStruct((s_len, FOX_W), BF16),
                   jax.ShapeDtypeStruct((s_len, FOX_W), BF16), jax.ShapeDtypeStruct((s_len, FOX_W), F32)] + c_shapes,
        scratch_shapes=c_scratch,
        compiler_params=_params("arbitrary", "arbitrary"),
    )(qx, dox, stats, kx, vx, *c_arrays)
    return outs[0], outs[1], outs[2], outs[3], outs[4:]


def _mixer_out(attn_a, attn_b, zg, h, wpa, wpb, wout, g, b, tm=256):
    m = h.shape[0]
    tm = _div(m, tm, 8)

    def body(a_ref, b_ref, ga_ref, gb_ref, h_ref, wpa_ref, wpb_ref, wout_ref, g_ref, bb_ref,
             h1_ref, u_ref, mg_ref, ya_ref, yb_ref):
        ya = jnp.dot(a_ref[...], wpa_ref[...], preferred_element_type=F32)
        yb = jnp.dot(b_ref[...], wpb_ref[...], preferred_element_type=F32)
        merged = (jax.nn.sigmoid(ga_ref[...]) * ya + jax.nn.sigmoid(gb_ref[...]) * yb).astype(BF16)
        u = ALPHA * h_ref[...] + jnp.dot(merged, wout_ref[...], preferred_element_type=F32)
        u_ref[...] = u
        h1_ref[...] = _ln(u, g_ref[...], bb_ref[...])
        mg_ref[...] = merged
        ya_ref[...] = ya.astype(BF16)
        yb_ref[...] = yb.astype(BF16)

    row = pl.BlockSpec((tm, D_MODEL), lambda i: (i, 0))
    att = pl.BlockSpec((tm, SWA_Q), lambda i: (i, 0))
    vec = pl.BlockSpec((1, D_MODEL), lambda i: (0, 0))
    wsm = pl.BlockSpec((SWA_Q, D_MODEL), lambda i: (0, 0))
    return pl.pallas_call(
        body, name="mixer_out", grid=(m // tm,),
        in_specs=[att, att, row, pl.BlockSpec((tm, D_MODEL), lambda i: (i, 1)), row, wsm, wsm,
                  pl.BlockSpec((D_MODEL, D_MODEL), lambda i: (0, 0)), vec, vec],
        out_specs=[row, row, row, row, row],
        out_shape=[jax.ShapeDtypeStruct((m, D_MODEL), F32), jax.ShapeDtypeStruct((m, D_MODEL), F32),
                   jax.ShapeDtypeStruct((m, D_MODEL), BF16), jax.ShapeDtypeStruct((m, D_MODEL), BF16),
                   jax.ShapeDtypeStruct((m, D_MODEL), BF16)],
        compiler_params=_params("parallel"),
    )(attn_a, attn_b, zg, zg, h, wpa, wpb, wout, g, b)


def _mixer_bwd(dh1, u1, g, wout, ya, yb, zg, wpa, wpb, tm=256):
    m = dh1.shape[0]
    tm = _div(m, tm, 8)

    def body(dh_ref, u_ref, g_ref, wout_ref, ya_ref, yb_ref, ga_ref, gb_ref, wpa_ref, wpb_ref,
             du_ref, dg_ref, db_ref, dya_ref, dyb_ref, dga_ref, dgb_ref, da_ref, dbb_ref):
        @pl.when(pl.program_id(0) == 0)
        def _():
            dg_ref[...] = jnp.zeros_like(dg_ref)
            db_ref[...] = jnp.zeros_like(db_ref)

        du, dg, db = _ln_bwd_block(dh_ref[...], u_ref[...], g_ref[...])
        du_ref[...] = du
        dg_ref[...] += dg
        db_ref[...] += db
        dm = lax.dot_general(du.astype(BF16), wout_ref[...], (((1,), (1,)), ((), ())), preferred_element_type=F32)
        for y_ref, gate_ref, w_ref, dy_ref, dgate_ref, dattn_ref in (
                (ya_ref, ga_ref, wpa_ref, dya_ref, dga_ref, da_ref), (yb_ref, gb_ref, wpb_ref, dyb_ref, dgb_ref, dbb_ref)):
            sg = jax.nn.sigmoid(gate_ref[...])
            dy = (dm * sg).astype(BF16)
            dy_ref[...] = dy
            dgate_ref[...] = (dm * y_ref[...].astype(F32) * sg * (1.0 - sg)).astype(BF16)
            dattn_ref[...] = lax.dot_general(dy, w_ref[...], (((1,), (1,)), ((), ())),
                                             preferred_element_type=F32).astype(BF16)

    row = pl.BlockSpec((tm, D_MODEL), lambda i: (i, 0))
    att = pl.BlockSpec((tm, SWA_Q), lambda i: (i, 0))
    vec = pl.BlockSpec((1, D_MODEL), lambda i: (0, 0))
    wsm = pl.BlockSpec((SWA_Q, D_MODEL), lambda i: (0, 0))
    wide = jax.ShapeDtypeStruct((m, D_MODEL), BF16)
    narrow = jax.ShapeDtypeStruct((m, SWA_Q), BF16)
    sums = jax.ShapeDtypeStruct((1, D_MODEL), F32)
    return pl.pallas_call(
        body, name="mixer_bwd", grid=(m // tm,),
        in_specs=[row, row, vec, pl.BlockSpec((D_MODEL, D_MODEL), lambda i: (0, 0)), row, row, row,
                  pl.BlockSpec((tm, D_MODEL), lambda i: (i, 1)), wsm, wsm],
        out_specs=[row, vec, vec, row, row, row, row, att, att],
        out_shape=[jax.ShapeDtypeStruct((m, D_MODEL), F32), sums, sums, wide, wide, wide, wide, narrow, narrow],
        compiler_params=_params("arbitrary"),
    )(dh1, u1, g, wout, ya, yb, zg, zg, wpa, wpb)


def _shift_down(x, k, halo, first):
    rows = lax.broadcasted_iota(jnp.int32, (x.shape[0], 1), 0)
    y = pltpu.roll(x, k, 0)
    for r in range(k):
        fill = jnp.where(first, 0.0, halo[8 - k + r:8 - k + r + 1, :])
        y = jnp.where(rows == r, fill, y)
    return y


def _shift_up(x, k, halo, last):
    n = x.shape[0]
    rows = lax.broadcasted_iota(jnp.int32, (n, 1), 0)
    y = pltpu.roll(x, n - k, 0)
    for r in range(k):
        fill = jnp.where(last, 0.0, halo[r:r + 1, :])
        y = jnp.where(rows == n - k + r, fill, y)
    return y


def _conv_act(gate, gate_m1, gate_m2, cw, cb):
    return cb + cw[0:1, :] * gate_m2 + cw[1:2, :] * gate_m1 + cw[2:3, :] * gate


def _ffn_in_conv(h1, wfi, cw, cb, tm=256):
    s_len = h1.shape[0]
    tm = _div(s_len, tm, 8)
    hb = tm // 8

    def body(a_ref, ap_ref, w_ref, cw_ref, cb_ref, gu_ref, act_ref):
        first = pl.program_id(0) == 0
        a = a_ref[...].astype(BF16)
        before = ap_ref[...].astype(BF16)
        for c in range(N_CHUNK):
            gate = jnp.dot(a, w_ref[c], preferred_element_type=F32)
            up = jnp.dot(a, w_ref[N_CHUNK + c], preferred_element_type=F32)
            halo = jnp.dot(before, w_ref[c], preferred_element_type=F32)
            gu_ref[c, 0] = gate
            gu_ref[c, 1] = up
            conv = _conv_act(gate, _shift_down(gate, 1, halo, first), _shift_down(gate, 2, halo, first),
                             cw_ref[c], cb_ref[c])
            act_ref[c] = (conv * jax.nn.sigmoid(conv) * up).astype(BF16)

    return pl.pallas_call(
        body, name="ffn_in_conv", grid=(s_len // tm,),
        in_specs=[pl.BlockSpec((tm, D_MODEL), lambda i: (i, 0)),
                  pl.BlockSpec((8, D_MODEL), lambda i: (jnp.maximum(i * hb - 1, 0), 0)),
                  pl.BlockSpec((N_DEV, D_MODEL, FF_CHUNK), lambda i: (0, 0, 0)),
                  pl.BlockSpec((N_CHUNK, 8, FF_CHUNK), lambda i: (0, 0, 0)),
                  pl.BlockSpec((N_CHUNK, 1, FF_CHUNK), lambda i: (0, 0, 0))],
        out_specs=[pl.BlockSpec((N_CHUNK, 2, tm, FF_CHUNK), lambda i: (0, 0, i, 0)),
                   pl.BlockSpec((N_CHUNK, tm, FF_CHUNK), lambda i: (0, i, 0))],
        out_shape=[jax.ShapeDtypeStruct((N_CHUNK, 2, s_len, FF_CHUNK), F32),
                   jax.ShapeDtypeStruct((N_CHUNK, s_len, FF_CHUNK), BF16)],
        compiler_params=_params("parallel"),
    )(h1, h1, wfi, cw, cb)


def _ffn_out_ln(act, wfo, res, g, b, tm=256):
    s_len = res.shape[0]
    tm = _div(s_len, tm, 8)

    def body(a_ref, w_ref, res_ref, g_ref, b_ref, u_ref, y_ref):
        u = ALPHA * res_ref[...]
        for c in range(N_CHUNK):
            u = u + jnp.dot(a_ref[c], w_ref[c], preferred_element_type=F32)
        u_ref[...] = u
        y_ref[...] = _ln(u, g_ref[...], b_ref[...])

    row = pl.BlockSpec((tm, D_MODEL), lambda i: (i, 0))
    vec = pl.BlockSpec((1, D_MODEL), lambda i: (0, 0))
    return pl.pallas_call(
        body, name="ffn_out_ln", grid=(s_len // tm,),
        in_specs=[pl.BlockSpec((N_CHUNK, tm, FF_CHUNK), lambda i: (0, i, 0)),
                  pl.BlockSpec((N_CHUNK, FF_CHUNK, D_MODEL), lambda i: (0, 0, 0)), row, vec, vec],
        out_specs=[row, row],
        out_shape=[jax.ShapeDtypeStruct((s_len, D_MODEL), F32), jax.ShapeDtypeStruct((s_len, D_MODEL), F32)],
        compiler_params=_params("parallel"),
    )(act, wfo, res, g, b)


def _ffn_out_bwd(dh2, u2, g, wfo, tm=256):
    s_len = dh2.shape[0]
    tm = _div(s_len, tm, 8)

    def body(dh_ref, u_ref, g_ref, w_ref, du_ref, dg_ref, db_ref, o_ref):
        @pl.when(pl.program_id(0) == 0)
        def _():
            dg_ref[...] = jnp.zeros_like(dg_ref)
            db_ref[...] = jnp.zeros_like(db_ref)

        du, dg, db = _ln_bwd_block(dh_ref[...], u_ref[...], g_ref[...])
        du_ref[...] = du
        dg_ref[...] += dg
        db_ref[...] += db
        du_b = du.astype(BF16)
        for c in range(N_CHUNK):
            o_ref[c] = lax.dot_general(du_b, w_ref[c], (((1,), (1,)), ((), ())), preferred_element_type=F32)

    row = pl.BlockSpec((tm, D_MODEL), lambda i: (i, 0))
    vec = pl.BlockSpec((1, D_MODEL), lambda i: (0, 0))
    sums = jax.ShapeDtypeStruct((1, D_MODEL), F32)
    return pl.pallas_call(
        body, name="ffn_out_bwd", grid=(s_len // tm,),
        in_specs=[row, row, vec, pl.BlockSpec((N_CHUNK, FF_CHUNK, D_MODEL), lambda i: (0, 0, 0))],
        out_specs=[row, vec, vec, pl.BlockSpec((N_CHUNK, tm, FF_CHUNK), lambda i: (0, i, 0))],
        out_shape=[jax.ShapeDtypeStruct((s_len, D_MODEL), F32), sums, sums,
                   jax.ShapeDtypeStruct((N_CHUNK, s_len, FF_CHUNK), F32)],
        compiler_params=_params("arbitrary"),
    )(dh2, u2, g, wfo)


def _g_w_ffn_out(act, du, tm=2048):
    s_len = du.shape[0]
    tm = _div(s_len, tm, 8)
    steps = s_len // tm

    def body(a_ref, g_ref, o_ref, acc_ref):
        s = pl.program_id(1)

        @pl.when(s == 0)
        def _():
            acc_ref[...] = jnp.zeros_like(acc_ref)

        acc_ref[...] += lax.dot_general(a_ref[0], g_ref[...].astype(BF16), (((0,), (0,)), ((), ())),
                                        preferred_element_type=F32)

        @pl.when(s == steps - 1)
        def _():
            o_ref[0] = acc_ref[...].astype(BF16)

    return pl.pallas_call(
        body, name="g_w_ffn_out", grid=(N_CHUNK, steps),
        in_specs=[pl.BlockSpec((1, tm, FF_CHUNK), lambda c, s: (c, s, 0)),
                  pl.BlockSpec((tm, D_MODEL), lambda c, s: (s, 0))],
        out_specs=pl.BlockSpec((1, FF_CHUNK, D_MODEL), lambda c, s: (c, 0, 0)),
        out_shape=jax.ShapeDtypeStruct((N_CHUNK, FF_CHUNK, D_MODEL), BF16),
        scratch_shapes=[pltpu.VMEM((FF_CHUNK, D_MODEL), F32)],
        compiler_params=_params("parallel", "arbitrary"),
    )(act, du)


def _g_w_ffn_in(h1, dgu, tm=2048):
    s_len = h1.shape[0]
    tm = _div(s_len, tm, 8)
    steps = s_len // tm

    def body(a_ref, g_ref, o_ref, acc_ref):
        s = pl.program_id(1)

        @pl.when(s == 0)
        def _():
            acc_ref[...] = jnp.zeros_like(acc_ref)

        acc_ref[...] += lax.dot_general(a_ref[...].astype(BF16), g_ref[0, 0], (((0,), (0,)), ((), ())),
                                        preferred_element_type=F32)

        @pl.when(s == steps - 1)
        def _():
            o_ref[0] = acc_ref[...].astype(BF16)

    return pl.pallas_call(
        body, name="g_w_ffn_in", grid=(N_DEV, steps),
        in_specs=[pl.BlockSpec((tm, D_MODEL), lambda d, s: (s, 0)),
                  pl.BlockSpec((1, 1, tm, FF_CHUNK), lambda d, s: (d % N_CHUNK, d // N_CHUNK, s, 0))],
        out_specs=pl.BlockSpec((1, D_MODEL, FF_CHUNK), lambda d, s: (d, 0, 0)),
        out_shape=jax.ShapeDtypeStruct((N_DEV, D_MODEL, FF_CHUNK), BF16),
        scratch_shapes=[pltpu.VMEM((D_MODEL, FF_CHUNK), F32)],
        compiler_params=_params("parallel", "arbitrary"),
    )(h1, dgu)


def _conv_bwd_dh1(gu, dact, cw, cb, wfi, res, tm=256):
    s_len = gu.shape[2]
    tm = _div(s_len, tm, 8)
    nrow = s_len // tm
    hb = tm // 8

    def dconv_of(conv, up, da):
        sg = jax.nn.sigmoid(conv)
        return da * up * (sg * (1.0 + conv * (1.0 - sg)))

    def body(gu_ref, gp_ref, gun_ref, da_ref, dan_ref, cw_ref, cb_ref, w_ref, res_ref, dgu_ref, dcw_ref, dh_ref):
        i = pl.program_id(0)
        first = i == 0
        last = i == nrow - 1

        @pl.when(first)
        def _():
            dcw_ref[...] = jnp.zeros_like(dcw_ref)

        row = lax.broadcasted_iota(jnp.int32, (8, 1), 0)
        acc = ALPHA * res_ref[...]
        for c in range(N_CHUNK):
            cw = cw_ref[c]
            cb = cb_ref[c]
            gate = gu_ref[c, 0]
            halo = gp_ref[c, 0]
            g_m1 = _shift_down(gate, 1, halo, first)
            g_m2 = _shift_down(gate, 2, halo, first)
            conv = _conv_act(gate, g_m1, g_m2, cw, cb)
            da = da_ref[c]
            dup = (da * conv * jax.nn.sigmoid(conv)).astype(BF16)
            dconv = dconv_of(conv, gu_ref[c, 1], da)
            gate_n = gun_ref[c, 0]
            tail = gate[tm - 8:, :]
            conv_n = _conv_act(gate_n, _shift_down(gate_n, 1, tail, False), _shift_down(gate_n, 2, tail, False),
                               cw, cb)
            dconv_n = dconv_of(conv_n, gun_ref[c, 1], dan_ref[c])
            dgate = (cw[2:3, :] * dconv + cw[1:2, :] * _shift_up(dconv, 1, dconv_n, last)
                     + cw[0:1, :] * _shift_up(dconv, 2, dconv_n, last)).astype(BF16)
            dgu_ref[c, 0] = dgate
            dgu_ref[c, 1] = dup
            acc = acc + lax.dot_general(dgate, w_ref[c], NT, preferred_element_type=F32)
            acc = acc + lax.dot_general(dup, w_ref[N_CHUNK + c], NT, preferred_element_type=F32)
            part = jnp.zeros((8, FF_CHUNK), F32)
            for r, term in enumerate((dconv * g_m2, dconv * g_m1, dconv * gate, dconv)):
                part = jnp.where(row == r, jnp.sum(term, axis=0, keepdims=True), part)
            dcw_ref[c] += part
        dh_ref[...] = acc

    nxt = lambda i: jnp.minimum((i + 1) * hb, s_len // 8 - 1)
    main = pl.BlockSpec((N_CHUNK, 2, tm, FF_CHUNK), lambda i: (0, 0, i, 0))
    row_d = pl.BlockSpec((tm, D_MODEL), lambda i: (i, 0))
    return pl.pallas_call(
        body, name="conv_bwd_dh1", grid=(nrow,),
        in_specs=[main,
                  pl.BlockSpec((N_CHUNK, 1, 8, FF_CHUNK), lambda i: (0, 0, jnp.maximum(i * hb - 1, 0), 0)),
                  pl.BlockSpec((N_CHUNK, 2, 8, FF_CHUNK), lambda i: (0, 0, nxt(i), 0)),
                  pl.BlockSpec((N_CHUNK, tm, FF_CHUNK), lambda i: (0, i, 0)),
                  pl.BlockSpec((N_CHUNK, 8, FF_CHUNK), lambda i: (0, nxt(i), 0)),
                  pl.BlockSpec((N_CHUNK, 8, FF_CHUNK), lambda i: (0, 0, 0)),
                  pl.BlockSpec((N_CHUNK, 1, FF_CHUNK), lambda i: (0, 0, 0)),
                  pl.BlockSpec((N_DEV, D_MODEL, FF_CHUNK), lambda i: (0, 0, 0)), row_d],
        out_specs=[main, pl.BlockSpec((N_CHUNK, 8, FF_CHUNK), lambda i: (0, 0, 0)), row_d],
        out_shape=[jax.ShapeDtypeStruct((N_CHUNK, 2, s_len, FF_CHUNK), BF16),
                   jax.ShapeDtypeStruct((N_CHUNK, 8, FF_CHUNK), F32),
                   jax.ShapeDtypeStruct((s_len, D_MODEL), F32)],
        compiler_params=_params("arbitrary"),
    )(gu, gu, gu, dact, dact, cw, cb, wfi, res)


def _loss_head(y, target, tm=256):
    m, d = y.shape
    tm = _div(m, tm, 8)

    def body(y_ref, t_ref, dy_ref, loss_ref):
        @pl.when(pl.program_id(0) == 0)
        def _():
            loss_ref[...] = jnp.zeros_like(loss_ref)

        err = y_ref[...] - t_ref[...]
        dy_ref[...] = err / d
        loss_ref[...] += 0.5 * jnp.sum(jnp.sum(err * err, axis=1, keepdims=True) / d, axis=0, keepdims=True)

    row = pl.BlockSpec((tm, d), lambda i: (i, 0))
    return pl.pallas_call(
        body, name="loss_head", grid=(m // tm,), in_specs=[row, row],
        out_specs=[row, pl.BlockSpec((8, LANE), lambda i: (0, 0))],
        out_shape=[jax.ShapeDtypeStruct((m, d), F32), jax.ShapeDtypeStruct((8, LANE), F32)],
        compiler_params=_params("arbitrary"),
    )(y, target)


def _sum_devices(r_ref):
    acc = r_ref[0].astype(F32)
    for d in range(1, N_DEV):
        acc = acc + r_ref[d].astype(F32)
    return acc


def _sum8(recv):
    rows = recv.shape[1]
    tr = _div(rows, ROW_BLOCK, 8)

    def body(r_ref, o_ref):
        o_ref[...] = _sum_devices(r_ref)

    return pl.pallas_call(
        body, name="sum8", grid=(rows // tr,),
        in_specs=[pl.BlockSpec((N_DEV, tr, LANE), lambda i: (0, i, 0))],
        out_specs=pl.BlockSpec((tr, LANE), lambda i: (i, 0)),
        out_shape=jax.ShapeDtypeStruct((rows, LANE), F32),
        compiler_params=_params("parallel"),
    )(recv)


def _adamw_math(w, g, m, v):
    m = ADAM_B1 * m + (1.0 - ADAM_B1) * g
    v = ADAM_B2 * v + (1.0 - ADAM_B2) * (g * g)
    m_hat = m / (1.0 - ADAM_B1 ** ADAM_STEP)
    v_hat = v / (1.0 - ADAM_B2 ** ADAM_STEP)
    return -ADAM_LR * (m_hat / (jnp.sqrt(v_hat) + ADAM_EPS) + ADAM_WD * w), m, v


def _adamw_rows(w, g, m, v, name):
    rows = w.shape[0]
    tr = _div(rows, ROW_BLOCK, 8)

    def body(w_ref, g_ref, m_ref, v_ref, d_ref, mo_ref, vo_ref):
        d_ref[...], mo_ref[...], vo_ref[...] = _adamw_math(w_ref[...], g_ref[...], m_ref[...], v_ref[...])

    blk = pl.BlockSpec((tr, LANE), lambda i: (i, 0))
    out = jax.ShapeDtypeStruct((rows, LANE), F32)
    return pl.pallas_call(
        body, name=name, grid=(rows // tr,), in_specs=[blk, blk, blk, blk], out_specs=[blk, blk, blk],
        out_shape=[out, out, out], compiler_params=_params("parallel"),
    )(w, g, m, v)


def _adamw_shard(recv, w, m, v, layer, prev, name):
    _, k, n = recv.shape
    tk = _div(k, 128, 16)

    def body(r_ref, w_ref, m_ref, v_ref, *rest):
        g_ref, d_ref, mo_ref, vo_ref = rest[-4:]
        g = _sum_devices(r_ref)
        g_ref[0] = g
        d_ref[0], mo_ref[0], vo_ref[0] = _adamw_math(w_ref[0], g, m_ref[0], v_ref[0])

    blk = pl.BlockSpec((1, tk, n), lambda i: (layer, i, 0))
    out = jax.ShapeDtypeStruct((DEPTH, k, n), F32)
    carried = [] if prev is None else list(prev)
    return pl.pallas_call(
        body, name=name, grid=(k // tk,),
        in_specs=[pl.BlockSpec((N_DEV, tk, n), lambda i: (0, i, 0)), blk, blk, blk]
        + [pl.BlockSpec(memory_space=pl.ANY)] * len(carried),
        out_specs=[blk, blk, blk, blk], out_shape=[out, out, out, out],
        input_output_aliases={4 + j: j for j in range(len(carried))},
        compiler_params=_params("parallel"),
    )(recv, w, m, v, *carried)


def _to_rows(flat, rows):
    flat = flat.reshape(-1)
    return jnp.pad(flat, (0, rows * LANE - flat.shape[0])).reshape(rows, LANE)


def _pad_cols_z(a):
    f0 = N_QKV
    g0 = N_QKV + FOX_HEADS
    pad = jnp.zeros(a.shape[:-1] + (F_PAD - FOX_HEADS,), a.dtype)
    return jnp.concatenate([a[..., :f0], a[..., g0:], a[..., f0:g0], pad], axis=-1)


def _unpad_cols_z(a):
    f0 = N_QKV + N_GATE
    return jnp.concatenate([a[..., :N_QKV], a[..., f0:f0 + FOX_HEADS], a[..., N_QKV:f0]], axis=-1)


def _shards_to_cols(g):
    _, k, n = g.shape
    return g.transpose(1, 0, 2).reshape(k, N_DEV * n)


def _cols_to_shards(full):
    k, n = full.shape
    return full.reshape(k, N_DEV, n // N_DEV).transpose(1, 0, 2)


def _layer_fwd(h, w, p, comm=None, late=None):
    zq, zg = _z_proj(h, w["w_in_p"], p["b_in_p"])
    qx, kx, vx = _fox_prep(zq, _cumsum_logf(zg))
    attn_a, lse_a = _swa_fwd(zq, p["sinks"])
    attn_b, attn_b32, m_b, l_b, arrived = _fox_fwd(qx, kx, vx, comm)
    if late is not None:
        w, p = late(w, p, arrived)
    h1, u1, merged, ya, yb = _mixer_out(attn_a, attn_b, zg, h, w["w_proj_a"], w["w_proj_b"], w["w_out"],
                                        p["ln_mix_g"], p["ln_mix_b"])
    gu, act = _ffn_in_conv(h1, w["w_ffn_in"], p["conv_w"], p["conv_b"])
    u2, h2 = _ffn_out_ln(act, w["w_ffn_out"], h1, p["ln_ffn_g"], p["ln_ffn_b"])
    saved = dict(h=h, zq=zq, zg=zg, qx=qx, kx=kx, vx=vx, attn_a=attn_a, lse_a=lse_a, attn_b=attn_b,
                 attn_b32=attn_b32, m_b=m_b, l_b=l_b, h1=h1, u1=u1, merged=merged, ya=ya, yb=yb, gu=gu, act=act, u2=u2)
    return h2, saved, w, p


def _layer_bwd(dh2, sv, w, p, make_comm=None, make_last_comm=None):
    s_len = dh2.shape[0]
    du2, d_ffn_g, d_ffn_b, dact = _ffn_out_bwd(dh2, sv["u2"], p["ln_ffn_g"], w["w_ffn_out"])
    g_ffn_out = _g_w_ffn_out(sv["act"], du2)
    dgu, dcw, dh1 = _conv_bwd_dh1(sv["gu"], dact, p["conv_w"], p["conv_b"], w["w_ffn_in"], du2)
    dcw = dcw.transpose(1, 0, 2).reshape(8, D_FF)
    g_ffn_in = _g_w_ffn_in(sv["h1"], dgu)
    du1, d_mix_g, d_mix_b, dya, dyb, dga, dgb, dattn_a, dattn_b = _mixer_bwd(
        dh1, sv["u1"], p["ln_mix_g"], w["w_out"], sv["ya"], sv["yb"], sv["zg"], w["w_proj_a"], w["w_proj_b"])
    g_out = _linear_tn(sv["merged"], du1, name="g_w_out", tn=1024)
    g_proj_a = _linear_tn(sv["attn_a"], dya, name="g_w_proj_a", tk=512, tn=1024)
    g_proj_b = _linear_tn(sv["attn_b"], dyb, name="g_w_proj_b", tk=512, tn=1024)
    dq_a, dk_a, dv_a, dsinks = _swa_bwd(sv["zq"], p["sinks"], sv["attn_a"], dattn_a, sv["lse_a"])
    big = dict(w_proj_a=_cols_to_shards(g_proj_a), w_proj_b=_cols_to_shards(g_proj_b),
               w_out=g_out.reshape(N_DEV, D_MODEL // N_DEV, D_MODEL), w_ffn_in=g_ffn_in,
               w_ffn_out=g_ffn_out.reshape(N_DEV, D_FF // N_DEV, D_MODEL))
    dox, stats = _fox_stats(sv["attn_b32"], dattn_b, sv["m_b"], sv["l_b"])
    dq_b, dk_b, dv_b, dcc, arrived = _fox_bwd(sv["qx"], sv["kx"], sv["vx"], dox, stats,
                                              None if make_comm is None else make_comm(big))
    df = _forget_bwd(dcc, sv["zg"])
    dz = jnp.concatenate([dq_a, dk_a.astype(BF16), dv_a.astype(BF16), dq_b.astype(BF16), dk_b, dv_b, dga, dgb, df,
                          jnp.zeros((s_len, F_PAD - LANE), BF16)], axis=1)
    g_in = _unpad_cols_z(_linear_tn(sv["h"], dz, name="g_w_in", tn=768))
    g_b_in = _unpad_cols_z(_colsum(dz, name="g_b_in"))
    big["w_in"] = _cols_to_shards(g_in)
    small = dict(ln_mix_g=d_mix_g, ln_mix_b=d_mix_b, b_in=g_b_in, attn_sinks=dsinks[:, :SWA_HEADS],
                 ln_ffn_g=d_ffn_g, ln_ffn_b=d_ffn_b, conv_w=dcw[:3], conv_b=dcw[3:4])
    dh, arrived_last = _d_h(dz, w["w_in_p"], du1, None if make_last_comm is None else make_last_comm(big, small))
    return dh, big, small, arrived, arrived_last


def _w_in_layouts(w_in):
    return dict(w_in_p=_pad_cols_z(_shards_to_cols(w_in)))


def _other_layouts(w_proj_a, w_proj_b, w_out, w_ffn_in, w_ffn_out):
    return dict(w_proj_a=_shards_to_cols(w_proj_a), w_proj_b=_shards_to_cols(w_proj_b),
                w_out=w_out.reshape(D_MODEL, D_MODEL), w_ffn_in=w_ffn_in,
                w_ffn_out=w_ffn_out.reshape(N_CHUNK, FF_CHUNK, D_MODEL))


def _layer_params(r):
    return dict(
        b_in_p=_pad_cols_z(r["b_in"].reshape(1, N_IN)),
        sinks=jnp.pad(r["attn_sinks"].reshape(1, SWA_HEADS), ((0, 0), (0, LANE - SWA_HEADS))),
        ln_mix_g=r["ln_mix_g"].reshape(1, D_MODEL), ln_mix_b=r["ln_mix_b"].reshape(1, D_MODEL),
        ln_ffn_g=r["ln_ffn_g"].reshape(1, D_MODEL), ln_ffn_b=r["ln_ffn_b"].reshape(1, D_MODEL),
        conv_b=r["conv_b"].reshape(N_CHUNK, 1, FF_CHUNK))


def _conv_w_layout(conv_w):
    return jnp.pad(conv_w, ((0, 5), (0, 0))).reshape(8, N_CHUNK, FF_CHUNK).transpose(1, 0, 2)


def kernel(x, ln_mix_g, ln_mix_b, w_in, b_in, attn_sinks, w_proj_a, w_proj_b, w_out, ln_ffn_g, ln_ffn_b, w_ffn_in, conv_w, conv_b, w_ffn_out, loss_target, m_ln_mix_g, m_ln_mix_b, m_w_in, m_b_in, m_attn_sinks, m_w_proj_a, m_w_proj_b, m_w_out, m_ln_ffn_g, m_ln_ffn_b, m_w_ffn_in, m_conv_w, m_conv_b, m_w_ffn_out, v_ln_mix_g, v_ln_mix_b, v_w_in, v_b_in, v_attn_sinks, v_w_proj_a, v_w_proj_b, v_w_out, v_ln_ffn_g, v_ln_ffn_b, v_w_ffn_in, v_conv_w, v_conv_b, v_w_ffn_out):
    wts = dict(ln_mix_g=ln_mix_g, ln_mix_b=ln_mix_b, w_in=w_in, b_in=b_in, attn_sinks=attn_sinks, w_proj_a=w_proj_a,
               w_proj_b=w_proj_b, w_out=w_out, ln_ffn_g=ln_ffn_g, ln_ffn_b=ln_ffn_b, w_ffn_in=w_ffn_in,
               conv_w=conv_w, conv_b=conv_b, w_ffn_out=w_ffn_out)
    mom = dict(ln_mix_g=m_ln_mix_g, ln_mix_b=m_ln_mix_b, w_in=m_w_in, b_in=m_b_in, attn_sinks=m_attn_sinks,
               w_proj_a=m_w_proj_a, w_proj_b=m_w_proj_b, w_out=m_w_out, ln_ffn_g=m_ln_ffn_g, ln_ffn_b=m_ln_ffn_b,
               w_ffn_in=m_w_ffn_in, conv_w=m_conv_w, conv_b=m_conv_b, w_ffn_out=m_w_ffn_out)
    vel = dict(ln_mix_g=v_ln_mix_g, ln_mix_b=v_ln_mix_b, w_in=v_w_in, b_in=v_b_in, attn_sinks=v_attn_sinks,
               w_proj_a=v_w_proj_a, w_proj_b=v_w_proj_b, w_out=v_w_out, ln_ffn_g=v_ln_ffn_g, ln_ffn_b=v_ln_ffn_b,
               w_ffn_in=v_w_ffn_in, conv_w=v_conv_w, conv_b=v_conv_b, w_ffn_out=v_w_ffn_out)
    names = list(wts)
    big_names = [n for n, _, _ in BIG]
    small_names = [n for n, _ in SMALL]
    me = 4 * lax.axis_index("x") + 2 * lax.axis_index("y") + lax.axis_index("c")
    cw_shard = D_FF // N_DEV

    wb = {n: wts[n].astype(BF16) for n in big_names}
    ps = [_layer_params(dict(b_in=b_in[l], attn_sinks=attn_sinks[l], ln_mix_g=ln_mix_g[l], ln_mix_b=ln_mix_b[l],
                             ln_ffn_g=ln_ffn_g[l], ln_ffn_b=ln_ffn_b[l], conv_b=conv_b[l])) for l in range(DEPTH)]
    w_in_0, = _exchange([(wb["w_in"][0], True)], "gather_w_in_0")
    others = big_names[1:]
    gather_rest = _Comm([(wb[n][0], True) for n in others] + [(wb[n][1], True) for n in big_names] + [(conv_w, True)])
    next_layer = {}

    def late(w, p, arrived):
        conv_full = arrived[-1].transpose(1, 2, 0, 3).reshape(DEPTH, 3, D_FF)
        layer_1 = arrived[len(others):-1]
        next_layer["w"] = dict(_w_in_layouts(layer_1[0]), **_other_layouts(*layer_1[1:]))
        next_layer["p"] = dict(ps[1], conv_w=_conv_w_layout(conv_full[1]))
        return dict(w, **_other_layouts(*arrived[:len(others)])), dict(p, conv_w=_conv_w_layout(conv_full[0]))

    saved, ws = [None] * DEPTH, [None] * DEPTH
    h, saved[0], ws[0], ps[0] = _layer_fwd(x[0], _w_in_layouts(w_in_0), ps[0], gather_rest, late)
    h, saved[1], ws[1], ps[1] = _layer_fwd(h, next_layer["w"], next_layer["p"])
    dh, loss_part = _loss_head(h, loss_target[0])

    def small_rows(small):
        vec = jnp.concatenate([small[n].reshape(-1) for n in small_names] + [loss_part[0, 0].reshape(1)])
        return _to_rows(vec, SMALL_LAYER_ROWS)

    dh, big_1, small_1, _, _ = _layer_bwd(dh, saved[1], ws[1], ps[1])

    def exchange_early(big_0):
        return _Comm([(big_1[n].astype(BF16), False) for n in big_names] + [(small_rows(small_1), True)]
                     + [(big_0[n].astype(BF16), False) for n in others])

    def exchange_last(big_0, small_0):
        return _Comm([(big_0["w_in"].astype(BF16), False), (small_rows(small_0), True)])

    grad_x, _, _, arrived, (g_in_0, g_small_0) = _layer_bwd(dh, saved[0], ws[0], ps[0], exchange_early, exchange_last)
    n_big = len(big_names)
    recv = [[g_in_0] + list(arrived[n_big + 1:]) + [g_small_0], list(arrived[:n_big + 1])]

    big_out = {}
    for t, n in enumerate(big_names):
        outs = None
        for l in reversed(range(DEPTH)):
            outs = _adamw_shard(recv[l][t], wts[n], mom[n], vel[n], l, outs, "adamw_%s_%d" % (n, l))
        big_out[n] = outs
    small_sum = [_sum8(recv[l][-1]).reshape(-1) for l in range(DEPTH)]
    g_small = {}
    off = 0
    for n, size in SMALL:
        g_small[n] = jnp.stack([small_sum[l][off:off + size] for l in range(DEPTH)])
        off += size
    loss = small_sum[0][off]
    g_small["conv_w"] = lax.dynamic_slice_in_dim(g_small["conv_w"].reshape(DEPTH, 3, D_FF), me * cw_shard, cw_shard,
                                                 axis=2)
    g_small = {n: g_small[n].reshape(wts[n].shape) for n in small_names}

    def pack_small(tree):
        return _to_rows(jnp.concatenate([tree[n].reshape(-1) for n in small_names]), SMALL_ROWS)

    small_out = (pack_small(g_small),) + tuple(_adamw_rows(pack_small(wts), pack_small(g_small), pack_small(mom),
                                                           pack_small(vel), "adamw_small"))

    def result(j):
        out = {n: big_out[n][j] for n in big_names}
        flat = small_out[j].reshape(-1)
        off = 0
        for n in small_names:
            out[n] = flat[off:off + wts[n].size].reshape(wts[n].shape)
            off += wts[n].size
        return [out[n] for n in names]

    return (loss, grad_x[None], *result(0), *result(1), *result(2), *result(3))
```

```python
import functools

import jax
import jax.numpy as jnp
import numpy as np
from jax import lax
from jax.experimental import pallas as pl
from jax.experimental.pallas import tpu as pltpu

F32 = jnp.float32
BF16 = jnp.bfloat16
MESH = pl.DeviceIdType.MESH

N_DEV = 8
DEPTH = 2
D_MODEL = 1024
HEAD_DIM = 64
SWA_Q = 512
SWA_KV = 128
FOX_W = 512
FOX_HEADS = 8
SWA_HEADS = 8
D_FF = 2816
N_IN = 4360
N_QKV = SWA_Q + 2 * SWA_KV + 3 * FOX_W
N_GATE = 2 * D_MODEL
F_PAD = 256
N_ZG = N_GATE + F_PAD
N_ZP = N_QKV + N_ZG
LN_EPS = 1e-5
NEG_INF = -1e30
ALPHA = (2 * DEPTH) ** 0.25
SCALE = HEAD_DIM ** -0.5
LOG2E = 1.4426950408889634
SLOPES = tuple(2.0 ** (-8.0 * (h + 1) / SWA_HEADS) for h in range(SWA_HEADS))

ADAM_LR = 0.001
ADAM_B1 = 0.9
ADAM_B2 = 0.999
ADAM_EPS = 1e-08
ADAM_WD = 0.01
ADAM_STEP = 10

LANE = 128
VMEM_LIMIT = 56 * 1024 * 1024

BIG = (("w_in", (D_MODEL, N_IN), 1), ("w_proj_a", (SWA_Q, D_MODEL), 1), ("w_proj_b", (FOX_W, D_MODEL), 1),
       ("w_out", (D_MODEL, D_MODEL), 0), ("w_ffn_in", (D_MODEL, 2 * D_FF), 1), ("w_ffn_out", (D_FF, D_MODEL), 0))
SMALL = (("ln_mix_g", D_MODEL), ("ln_mix_b", D_MODEL), ("b_in", N_IN), ("attn_sinks", SWA_HEADS),
         ("ln_ffn_g", D_MODEL), ("ln_ffn_b", D_MODEL), ("conv_w", 3 * D_FF), ("conv_b", D_FF))
ROW_BLOCK = 512
SMALL_LAYER_ROWS = -(-(sum(n for _, n in SMALL) + 1) // (8 * LANE)) * 8
SMALL_ROWS = ROW_BLOCK
FF_CHUNK = 2 * D_FF // N_DEV
N_CHUNK = D_FF // FF_CHUNK


def _div(n, cap, unit):
    if n <= cap:
        return n
    best = None
    for t in range(unit, cap + 1, unit):
        if n % t == 0:
            best = t
    assert best is not None, (n, cap, unit)
    return best


def _params(*sem):
    return pltpu.CompilerParams(dimension_semantics=sem, vmem_limit_bytes=VMEM_LIMIT)


def _peer(r):
    x, y, c = lax.axis_index("x"), lax.axis_index("y"), lax.axis_index("c")
    px = 1 - x if (r >> 2) & 1 else x
    py = 1 - y if (r >> 1) & 1 else y
    pc = 1 - c if r & 1 else c
    return (px, py, pc), 4 * px + 2 * py + pc


class _Comm:
    def __init__(self, tensors):
        self.arrays = [x for x, _ in tensors]
        self.gathers = [g for _, g in tensors]
        self.n = len(tensors)
        self.out_shape = [jax.ShapeDtypeStruct((N_DEV,) + (x.shape if g else x.shape[1:]), x.dtype)
                          for x, g in tensors]
        self.specs = [pl.BlockSpec(memory_space=pl.ANY)] * self.n
        self.scratch = [pltpu.SemaphoreType.DMA((N_DEV - 1, self.n)), pltpu.SemaphoreType.DMA((N_DEV - 1, self.n)),
                        pltpu.SemaphoreType.DMA((self.n,))]

    def _copies(self, x_refs, out_refs, sems):
        send_sems, recv_sems, local_sems = sems
        _, me = _peer(0)

        def src(t, idx):
            return x_refs[t] if self.gathers[t] else x_refs[t].at[idx]

        def remote(r, t, mine):
            peer, pid = _peer(r)
            return pltpu.make_async_remote_copy(src_ref=src(t, pid), dst_ref=out_refs[t].at[me if mine else pid],
                                                send_sem=send_sems.at[r - 1, t], recv_sem=recv_sems.at[r - 1, t],
                                                device_id=peer, device_id_type=MESH)

        pairs = [(r, t) for r in range(1, N_DEV) for t in range(self.n)]
        local = [pltpu.make_async_copy(src(t, me), out_refs[t].at[me], local_sems.at[t]) for t in range(self.n)]
        return local, [remote(r, t, True) for r, t in pairs], lambda: [remote(r, t, False) for r, t in pairs]

    def start(self, x_refs, out_refs, sems):
        local, sent, _ = self._copies(x_refs, out_refs, sems)
        for cp in local + sent:
            cp.start()

    def wait(self, x_refs, out_refs, sems):
        local, sent, landing = self._copies(x_refs, out_refs, sems)
        for cp in landing():
            cp.wait_recv()
        for cp in sent:
            cp.wait_send()
        for cp in local:
            cp.wait()


def _exchange(tensors, name):
    comm = _Comm(tensors)
    n = comm.n

    def body(*refs):
        comm.start(refs[:n], refs[n:2 * n], refs[2 * n:])
        comm.wait(refs[:n], refs[n:2 * n], refs[2 * n:])

    return pl.pallas_call(body, name=name, out_shape=comm.out_shape, in_specs=comm.specs, out_specs=comm.specs,
                          scratch_shapes=comm.scratch)(*comm.arrays)


def _with_comm(comm, n_in, n_out, first, last, compute):
    nc = comm.n if comm is not None else 0

    def body(*refs):
        ins, x_refs = refs[:n_in], refs[n_in:n_in + nc]
        outs = refs[n_in + nc:n_in + nc + n_out]
        out_refs = refs[n_in + nc + n_out:n_in + 2 * nc + n_out]
        sems = refs[n_in + 2 * nc + n_out:]
        if nc:
            @pl.when(first())
            def _():
                comm.start(x_refs, out_refs, sems)

        compute(*ins, *outs)
        if nc:
            @pl.when(last())
            def _():
                comm.wait(x_refs, out_refs, sems)

    return body


def _d_h(dz, w_in_p, res, comm=None, tm=512):
    m, k = dz.shape
    d = w_in_p.shape[0]
    tm = _div(m, tm, 8)
    steps = m // tm
    c_specs, c_shapes, c_scratch, c_arrays = _comm_parts(comm)

    def compute(dz_ref, w_ref, res_ref, o_ref):
        o_ref[...] = ALPHA * res_ref[...] + lax.dot_general(dz_ref[...], w_ref[...], (((1,), (1,)), ((), ())),
                                                            preferred_element_type=F32)

    body = _with_comm(comm, 3, 1, lambda: pl.program_id(0) == 0, lambda: pl.program_id(0) == steps - 1, compute)
    row = pl.BlockSpec((tm, d), lambda i: (i, 0))
    outs = pl.pallas_call(
        body, name="d_h" if comm is None else "d_h_comm", grid=(steps,),
        in_specs=[pl.BlockSpec((tm, k), lambda i: (i, 0)), pl.BlockSpec((d, k), lambda i: (0, 0)), row] + c_specs,
        out_specs=[row] + c_specs, out_shape=[jax.ShapeDtypeStruct((m, d), F32)] + c_shapes,
        scratch_shapes=c_scratch,
        compiler_params=_params("arbitrary"),
    )(dz, w_in_p, res, *c_arrays)
    return outs[0], outs[1:]


def _z_proj(h, w_in_p, b_p, tm=512):
    m, k = h.shape
    tm = _div(m, tm, 8)

    def body(h_ref, w_ref, b_ref, zq_ref, zg_ref):
        a = h_ref[...].astype(BF16)
        zq_ref[...] = (jnp.dot(a, w_ref[:, :N_QKV], preferred_element_type=F32) + b_ref[:, :N_QKV]).astype(BF16)
        zg_ref[...] = jnp.dot(a, w_ref[:, N_QKV:], preferred_element_type=F32) + b_ref[:, N_QKV:]

    return pl.pallas_call(
        body, name="z_proj", grid=(m // tm,),
        in_specs=[pl.BlockSpec((tm, k), lambda i: (i, 0)), pl.BlockSpec((k, N_ZP), lambda i: (0, 0)),
                  pl.BlockSpec((1, N_ZP), lambda i: (0, 0))],
        out_specs=[pl.BlockSpec((tm, N_QKV), lambda i: (i, 0)), pl.BlockSpec((tm, N_ZG), lambda i: (i, 0))],
        out_shape=[jax.ShapeDtypeStruct((m, N_QKV), BF16), jax.ShapeDtypeStruct((m, N_ZG), F32)],
        compiler_params=_params("parallel"),
    )(h, w_in_p, b_p)


def _linear_tn(a, g, *, name, tk=1024, tn=640, tm=2048):
    m, k = a.shape
    n = g.shape[1]
    tk = _div(k, tk, LANE)
    tn = _div(n, tn, LANE)
    tm = _div(m, tm, 8)
    steps = m // tm

    def body(a_ref, g_ref, o_ref, acc_ref):
        s = pl.program_id(2)

        @pl.when(s == 0)
        def _():
            acc_ref[...] = jnp.zeros_like(acc_ref)

        acc_ref[...] += lax.dot_general(a_ref[...].astype(BF16), g_ref[...].astype(BF16), (((0,), (0,)), ((), ())),
                                        preferred_element_type=F32)

        @pl.when(s == steps - 1)
        def _():
            o_ref[...] = acc_ref[...].astype(BF16)

    return pl.pallas_call(
        body, name=name, grid=(k // tk, n // tn, steps),
        in_specs=[pl.BlockSpec((tm, tk), lambda i, j, s: (s, i)), pl.BlockSpec((tm, tn), lambda i, j, s: (s, j))],
        out_specs=pl.BlockSpec((tk, tn), lambda i, j, s: (i, j)),
        out_shape=jax.ShapeDtypeStruct((k, n), BF16),
        scratch_shapes=[pltpu.VMEM((tk, tn), F32)],
        compiler_params=_params("parallel", "parallel", "arbitrary"),
    )(a, g)


def _colsum(g, *, name, tm=512):
    m, n = g.shape
    tm = _div(m, tm, 8)

    def body(g_ref, o_ref):
        @pl.when(pl.program_id(0) == 0)
        def _():
            o_ref[...] = jnp.zeros_like(o_ref)

        o_ref[...] += jnp.sum(g_ref[...].astype(F32), axis=0, keepdims=True)

    return pl.pallas_call(
        body, name=name, grid=(m // tm,),
        in_specs=[pl.BlockSpec((tm, n), lambda i: (i, 0))],
        out_specs=pl.BlockSpec((1, n), lambda i: (0, 0)),
        out_shape=jax.ShapeDtypeStruct((1, n), F32),
        compiler_params=_params("arbitrary"),
    )(g)


def _ln(u, g, b):
    mu = jnp.mean(u, axis=-1, keepdims=True)
    d = u - mu
    var = jnp.mean(d * d, axis=-1, keepdims=True)
    return d * lax.rsqrt(var + LN_EPS) * g + b


def _ln_bwd_block(dy, u, g):
    mu = jnp.mean(u, axis=-1, keepdims=True)
    dd = u - mu
    rstd = lax.rsqrt(jnp.mean(dd * dd, axis=-1, keepdims=True) + LN_EPS)
    xhat = dd * rstd
    dxh = dy * g
    m1 = jnp.mean(dxh, axis=-1, keepdims=True)
    m2 = jnp.mean(dxh * xhat, axis=-1, keepdims=True)
    return (rstd * (dxh - m1 - xhat * m2), jnp.sum(dy * xhat, axis=0, keepdims=True),
            jnp.sum(dy, axis=0, keepdims=True))


SCAN_ROWS = 512


def _tri(n, upper):
    r = lax.broadcasted_iota(jnp.int32, (n, n), 0)
    c = lax.broadcasted_iota(jnp.int32, (n, n), 1)
    return jnp.where((c >= r) if upper else (c <= r), 1.0, 0.0).astype(F32)


def _cumsum_logf(zg):
    s = zg.shape[0]
    t = _div(s, SCAN_ROWS, LANE)
    nb = s // t
    fcol = N_GATE // LANE

    def body(f_ref, c_ref, carry_ref):
        @pl.when(pl.program_id(0) == 0)
        def _():
            carry_ref[...] = jnp.zeros_like(carry_ref)

        f = f_ref[...]
        logf = jnp.minimum(f, 0.0) - jnp.log(1.0 + jnp.exp(-jnp.abs(f)))
        c = jnp.dot(_tri(t, False), logf, precision=lax.Precision.HIGHEST, preferred_element_type=F32)
        c = c + carry_ref[0:1, :]
        c_ref[...] = c
        carry_ref[...] = jnp.broadcast_to(c[t - 1:t, :], carry_ref.shape)

    return pl.pallas_call(
        body, name="cumsum_logf", grid=(nb,),
        in_specs=[pl.BlockSpec((t, LANE), lambda i: (i, fcol))],
        out_specs=pl.BlockSpec((t, LANE), lambda i: (i, 0)),
        out_shape=jax.ShapeDtypeStruct((s, LANE), F32),
        scratch_shapes=[pltpu.VMEM((8, LANE), F32)],
        compiler_params=_params("arbitrary"),
    )(zg)


def _forget_bwd(dcc, zg):
    s = zg.shape[0]
    t = _div(s, SCAN_ROWS, LANE)
    nb = s // t
    fcol = N_GATE // LANE

    def body(dc_ref, f_ref, o_ref, carry_ref):
        @pl.when(pl.program_id(0) == 0)
        def _():
            carry_ref[...] = jnp.zeros_like(carry_ref)

        lane = lax.broadcasted_iota(jnp.int32, (1, LANE), 1)
        dc = jnp.zeros((t, LANE), F32)
        for p in range(FOX_HEADS // 2):
            tile = dc_ref[:, p * LANE:(p + 1) * LANE]
            moved = pltpu.roll(tile, 2 * p, 1) if p else tile
            dc = jnp.where((lane == 2 * p) | (lane == 2 * p + 1), moved, dc)
        dlogf = jnp.dot(_tri(t, True), dc, precision=lax.Precision.HIGHEST, preferred_element_type=F32)
        dlogf = dlogf + carry_ref[0:1, :]
        o_ref[...] = (dlogf * jax.nn.sigmoid(-f_ref[...])).astype(BF16)
        carry_ref[...] = jnp.broadcast_to(dlogf[0:1, :], carry_ref.shape)

    return pl.pallas_call(
        body, name="forget_bwd", grid=(nb,),
        in_specs=[pl.BlockSpec((t, FOX_W), lambda i: (nb - 1 - i, 0)),
                  pl.BlockSpec((t, LANE), lambda i: (nb - 1 - i, fcol))],
        out_specs=pl.BlockSpec((t, LANE), lambda i: (nb - 1 - i, 0)),
        out_shape=jax.ShapeDtypeStruct((s, LANE), BF16),
        scratch_shapes=[pltpu.VMEM((8, LANE), F32)],
        compiler_params=_params("arbitrary"),
    )(dcc, zg)


KA_COL = SWA_Q // LANE
VA_COL = KA_COL + 1


def _half_masks():
    lane = lax.broadcasted_iota(jnp.int32, (1, LANE), 1)
    hi = lane >= HEAD_DIM
    return (jnp.logical_not(hi), hi)


def _both_halves(x, sel):
    xs = jnp.where(sel, x, 0.0)
    return xs + pltpu.roll(xs, HEAD_DIM, 1)


SWA_PER_KV = 4
WIDE = SWA_PER_KV * LANE


def _swa_bias():
    k = np.arange(2 * LANE)[:, None]
    q = np.arange(LANE)[None, :]
    dist = (q + LANE - k).astype(np.float32)
    valid = (dist >= 0) & (dist < LANE)
    per_head = [np.where(valid, np.float32(-s) * dist, np.float32(NEG_INF)) for s in SLOPES]
    return jnp.asarray(np.stack([np.concatenate(per_head[SWA_PER_KV * hk:SWA_PER_KV * (hk + 1)], axis=1)
                                 for hk in range(2)]), F32)


def _no_previous_block(i_blk):
    k = lax.broadcasted_iota(jnp.int32, (2 * LANE, WIDE), 0)
    return jnp.where((i_blk == 0) & (k < LANE), NEG_INF, 0.0)


def _stack_heads(ref, blk, hk, halves, scale):
    tiles = []
    for j in range(SWA_PER_KV):
        p = 2 * hk + j // 2
        t = ref[blk, p * LANE:(p + 1) * LANE]
        if scale:
            t = _scaled(t)
        tiles.append(jnp.where(halves[j % 2], t, jnp.zeros_like(t)))
    return jnp.concatenate(tiles, axis=0)


def _pair_tile(wide, pp, row_halves):
    a = wide[:, (2 * pp) * LANE:(2 * pp + 1) * LANE]
    b = wide[:, (2 * pp + 1) * LANE:(2 * pp + 2) * LANE]
    return jnp.where(row_halves[0], a, b).T


def _lane_blocks(rows8, hk):
    return jnp.concatenate([rows8[SWA_PER_KV * hk + j:SWA_PER_KV * hk + j + 1, :] for j in range(SWA_PER_KV)], axis=1)


def _row_halves():
    hi = lax.broadcasted_iota(jnp.int32, (LANE, 1), 0) >= HEAD_DIM
    return (jnp.logical_not(hi), hi)


NT = (((1,), (1,)), ((), ()))


def _scaled(q):
    return (q.astype(F32) * SCALE).astype(BF16)


SWA_GROUP = 4


def _swa_group(s_len):
    return SWA_GROUP if (s_len // LANE) % SWA_GROUP == 0 else 1


def _swa_specs(group):
    rows = group * LANE
    prev = lambda i: jnp.maximum(i * group - 1, 0)
    return [pl.BlockSpec((rows, SWA_Q), lambda i: (i, 0)),
            pl.BlockSpec((rows, LANE), lambda i: (i, KA_COL)), pl.BlockSpec((rows, LANE), lambda i: (i, VA_COL)),
            pl.BlockSpec((LANE, LANE), lambda i: (prev(i), KA_COL)),
            pl.BlockSpec((LANE, LANE), lambda i: (prev(i), VA_COL))]


def _swa_window(g, cur_ref, prev_ref):
    before = prev_ref[...] if g == 0 else cur_ref[(g - 1) * LANE:g * LANE, :]
    return jnp.concatenate([before, cur_ref[g * LANE:(g + 1) * LANE, :]], axis=0).astype(F32)


def _swa_fwd(zq, sinks):
    s_len = zq.shape[0]
    group = _swa_group(s_len)
    rows = group * LANE
    sink_lanes = jnp.repeat(sinks[:, :SWA_HEADS], LANE, axis=1)

    def body(q_ref, kc_ref, vc_ref, kp_ref, vp_ref, sink_ref, bias_ref, o_ref, lse_ref):
        halves = _half_masks()
        row_halves = _row_halves()
        for g in range(group):
            blk = slice(g * LANE, (g + 1) * LANE)
            kcat = _swa_window(g, kc_ref, kp_ref)
            vcat = _swa_window(g, vc_ref, vp_ref)
            lse_rows = []
            for hk in range(2):
                kb = _both_halves(kcat, halves[hk]).astype(BF16)
                v_t = _both_halves(vcat, halves[hk]).T.astype(BF16)
                q4 = _stack_heads(q_ref, blk, hk, halves, True)
                s_t = lax.dot_general(kb, q4, NT, preferred_element_type=F32) + bias_ref[hk]
                if g == 0:
                    s_t = s_t + _no_previous_block(pl.program_id(0))
                sink = sink_ref[:, hk * WIDE:(hk + 1) * WIDE]
                m = jnp.maximum(jnp.max(s_t, axis=0, keepdims=True), sink)
                pe = jnp.exp(s_t - m)
                den = jnp.sum(pe, axis=0, keepdims=True) + jnp.exp(sink - m)
                out_t = jnp.dot(v_t, (pe * (1.0 / den)).astype(BF16), preferred_element_type=F32)
                for pp in range(2):
                    p = 2 * hk + pp
                    o_ref[blk, p * LANE:(p + 1) * LANE] = _pair_tile(out_t, pp, row_halves).astype(BF16)
                lse4 = m + jnp.log(den)
                lse_rows += [lse4[:, j * LANE:(j + 1) * LANE] for j in range(SWA_PER_KV)]
            lse_ref[:, blk] = jnp.concatenate(lse_rows, axis=0)

    return pl.pallas_call(
        body, name="swa_fwd", grid=(s_len // rows,),
        in_specs=_swa_specs(group) + [pl.BlockSpec((1, SWA_HEADS * LANE), lambda i: (0, 0)),
                                      pl.BlockSpec((2, 2 * LANE, WIDE), lambda i: (0, 0, 0))],
        out_specs=[pl.BlockSpec((rows, SWA_Q), lambda i: (i, 0)), pl.BlockSpec((SWA_HEADS, rows), lambda i: (0, i))],
        out_shape=[jax.ShapeDtypeStruct((s_len, SWA_Q), BF16), jax.ShapeDtypeStruct((SWA_HEADS, s_len), F32)],
        compiler_params=_params("parallel"),
    )(zq, zq, zq, zq, zq, sink_lanes, _swa_bias())


def _swa_bwd(zq, sinks, o, do, lse):
    s_len = zq.shape[0]
    group = _swa_group(s_len)
    rows = group * LANE

    def body(q_ref, kc_ref, vc_ref, kp_ref, vp_ref, sink_ref, bias_ref, o_ref, do_ref, lse_ref,
             dq_ref, dk_ref, dv_ref, ds_ref):
        halves = _half_masks()
        row_halves = _row_halves()
        lane = lax.broadcasted_iota(jnp.int32, (1, LANE), 1)
        dsink = jnp.zeros((1, LANE), F32)
        for g in range(group):
            blk = slice(g * LANE, (g + 1) * LANE)
            i_blk = pl.program_id(0) * group + g
            kcat = _swa_window(g, kc_ref, kp_ref)
            vcat = _swa_window(g, vc_ref, vp_ref)
            lse_rows = lse_ref[:, blk]
            prod = do_ref[blk, :].astype(F32) * o_ref[blk, :].astype(F32)
            select = (lax.broadcasted_iota(jnp.int32, (SWA_HEADS, SWA_Q), 1) // HEAD_DIM
                      == lax.broadcasted_iota(jnp.int32, (SWA_HEADS, SWA_Q), 0))
            delta_rows = lax.dot_general(jnp.where(select, 1.0, 0.0), prod, NT, precision=lax.Precision.HIGHEST,
                                         preferred_element_type=F32)
            dk_tot = jnp.zeros((2 * LANE, LANE), F32)
            dv_tot = jnp.zeros((2 * LANE, LANE), F32)
            for hk in range(2):
                kb = _both_halves(kcat, halves[hk])
                k_t = kb.T.astype(BF16)
                kb = kb.astype(BF16)
                vb = _both_halves(vcat, halves[hk]).astype(BF16)
                q4 = _stack_heads(q_ref, blk, hk, halves, True)
                do4 = _stack_heads(do_ref, blk, hk, halves, False)
                lse4 = _lane_blocks(lse_rows, hk)
                delta4 = _lane_blocks(delta_rows, hk)
                s_t = lax.dot_general(kb, q4, NT, preferred_element_type=F32) + bias_ref[hk]
                if g == 0:
                    s_t = s_t + _no_previous_block(pl.program_id(0))
                p_t = jnp.exp(s_t - lse4)
                dp_t = lax.dot_general(vb, do4, NT, preferred_element_type=F32)
                ds_t = (p_t * (dp_t - delta4)).astype(BF16)
                sink_part = jnp.exp(sink_ref[:, hk * WIDE:(hk + 1) * WIDE] - lse4) * delta4
                for j in range(SWA_PER_KV):
                    dsink_h = -jnp.sum(sink_part[:, j * LANE:(j + 1) * LANE], axis=1, keepdims=True)
                    dsink = dsink + jnp.where(lane == SWA_PER_KV * hk + j, dsink_h, 0.0)
                dq_t = jnp.dot(k_t, ds_t, preferred_element_type=F32)
                for pp in range(2):
                    p = 2 * hk + pp
                    dq_ref[blk, p * LANE:(p + 1) * LANE] = (_pair_tile(dq_t, pp, row_halves) * SCALE).astype(BF16)
                dk_acc = jnp.dot(ds_t, q4, preferred_element_type=F32)
                dv_acc = jnp.dot(p_t.astype(BF16), do4, preferred_element_type=F32)
                dk_tot = dk_tot + jnp.where(halves[hk], dk_acc + pltpu.roll(dk_acc, HEAD_DIM, 1), 0.0)
                dv_tot = dv_tot + jnp.where(halves[hk], dv_acc + pltpu.roll(dv_acc, HEAD_DIM, 1), 0.0)
            cur = pl.ds(pl.multiple_of(i_blk * LANE, LANE), LANE)
            dk_ref[cur, :] = dk_tot[LANE:, :]
            dv_ref[cur, :] = dv_tot[LANE:, :]

            def add_previous(i_blk=i_blk, dk_tot=dk_tot, dv_tot=dv_tot):
                prv = pl.ds(pl.multiple_of((i_blk - 1) * LANE, LANE), LANE)
                dk_ref[prv, :] += dk_tot[:LANE, :]
                dv_ref[prv, :] += dv_tot[:LANE, :]

            if g == 0:
                pl.when(i_blk > 0)(add_previous)
            else:
                add_previous()

        @pl.when(pl.program_id(0) == 0)
        def _():
            ds_ref[...] = jnp.zeros_like(ds_ref)

        ds_ref[...] += dsink

    blk512 = pl.BlockSpec((rows, SWA_Q), lambda i: (i, 0))
    full = pl.BlockSpec((s_len, LANE), lambda i: (0, 0))
    vec = pl.BlockSpec((1, LANE), lambda i: (0, 0))
    return pl.pallas_call(
        body, name="swa_bwd", grid=(s_len // rows,),
        in_specs=_swa_specs(group) + [pl.BlockSpec((1, SWA_HEADS * LANE), lambda i: (0, 0)),
                                      pl.BlockSpec((2, 2 * LANE, WIDE), lambda i: (0, 0, 0)), blk512, blk512,
                                      pl.BlockSpec((SWA_HEADS, rows), lambda i: (0, i))],
        out_specs=[blk512, full, full, vec],
        out_shape=[jax.ShapeDtypeStruct((s_len, SWA_Q), BF16), jax.ShapeDtypeStruct((s_len, LANE), F32),
                   jax.ShapeDtypeStruct((s_len, LANE), F32), jax.ShapeDtypeStruct((1, LANE), F32)],
        compiler_params=_params("arbitrary"),
    )(zq, zq, zq, zq, zq, jnp.repeat(sinks[:, :SWA_HEADS], LANE, axis=1), _swa_bias(), o, do, lse)


QB_COL = (SWA_Q + 2 * SWA_KV) // LANE
KB_COL = QB_COL + FOX_W // LANE
VB_COL = KB_COL + FOX_W // LANE
N_PAIR = FOX_HEADS // 2


def _causal(t, keys_first=False):
    r = lax.broadcasted_iota(jnp.int32, (t, t), 0)
    c = lax.broadcasted_iota(jnp.int32, (t, t), 1)
    return c >= r if keys_first else r >= c


N_SPLIT = 3


def _own_half(e):
    hi = lax.broadcasted_iota(jnp.int32, (1, LANE), 1) >= HEAD_DIM
    return hi if e else jnp.logical_not(hi)


def _feature_lane(e, t):
    return HEAD_DIM * (1 - e) + t


def _fox_prep(zq, c, tm=256):
    s_len = zq.shape[0]
    tm = _div(s_len, tm, 8)

    def body(z_ref, c_ref, qx_ref, kx_ref, vx_ref):
        lane = lax.broadcasted_iota(jnp.int32, (1, LANE), 1)
        for h in range(FOX_HEADS):
            p, e = divmod(h, 2)
            own = _own_half(e)
            tile = lambda col: z_ref[:, (col + p) * LANE:(col + p + 1) * LANE].astype(F32)
            rest = c_ref[:, h:h + 1]
            qf = jnp.zeros((tm, LANE), F32)
            kf = jnp.zeros((tm, LANE), F32)
            for t in range(N_SPLIT):
                part = rest.astype(BF16).astype(F32)
                rest = rest - part
                qf = jnp.where(lane == _feature_lane(e, t), part, qf)
                qf = jnp.where(lane == _feature_lane(e, N_SPLIT + t), 1.0, qf)
                kf = jnp.where(lane == _feature_lane(e, t), 1.0, kf)
                kf = jnp.where(lane == _feature_lane(e, N_SPLIT + t), -part, kf)
            vf = jnp.where(lane == _feature_lane(e, 0), 1.0, 0.0)
            cols = slice(h * LANE, (h + 1) * LANE)
            qx_ref[:, cols] = jnp.where(own, tile(QB_COL) * SCALE, qf).astype(BF16)
            kx_ref[:, cols] = jnp.where(own, tile(KB_COL), kf).astype(BF16)
            vx_ref[:, cols] = jnp.where(own, tile(VB_COL), vf).astype(BF16)

    out = jax.ShapeDtypeStruct((s_len, FOX_HEADS * LANE), BF16)
    blk = pl.BlockSpec((tm, FOX_HEADS * LANE), lambda i: (i, 0))
    return pl.pallas_call(
        body, name="fox_prep", grid=(s_len // tm,),
        in_specs=[pl.BlockSpec((tm, N_QKV), lambda i: (i, 0)), pl.BlockSpec((tm, LANE), lambda i: (i, 0))],
        out_specs=[blk, blk, blk], out_shape=[out, out, out],
        compiler_params=_params("parallel"),
    )(zq, c)


def _comm_parts(comm):
    return ([], [], [], []) if comm is None else (comm.specs, comm.out_shape, comm.scratch, comm.arrays)


def _fox_fwd(qx, kx, vx, comm=None, t_cap=1024):
    s_len = qx.shape[0]
    t = _div(s_len, t_cap, LANE)
    nq = s_len // t
    c_specs, c_shapes, c_scratch, c_arrays = _comm_parts(comm)

    def compute(q_ref, k_ref, v_ref, o_ref, o32_ref, m_ref, l_ref):
        i = pl.program_id(1)
        qs = [q_ref[:, e * LANE:(e + 1) * LANE] for e in range(2)]

        def step(j, carry, diag):
            rows = pl.ds(pl.multiple_of(j * t, t), t)
            new = []
            for e in range(2):
                m, acc = carry[e]
                s2 = lax.dot_general(qs[e], k_ref[rows, e * LANE:(e + 1) * LANE], NT,
                                     preferred_element_type=F32) * LOG2E
                if diag:
                    s2 = jnp.where(_causal(t), s2, NEG_INF)
                mn = jnp.maximum(m, jnp.ceil(jnp.max(s2, axis=1, keepdims=True)))
                pe = jnp.exp2(s2 - mn).astype(BF16)
                acc = acc * jnp.exp2(m - mn) + jnp.dot(pe, v_ref[rows, e * LANE:(e + 1) * LANE],
                                                       preferred_element_type=F32)
                new.append((mn, acc))
            return tuple(new)

        init = (jnp.full((t, 1), NEG_INF, F32), jnp.zeros((t, LANE), F32))
        carry = lax.fori_loop(0, i, lambda j, c: step(j, c, False), (init, init))
        carry = step(i, carry, True)
        outs, ls = [], []
        for e in range(2):
            m, acc = carry[e]
            l = acc[:, _feature_lane(e, 0):_feature_lane(e, 0) + 1]
            outs.append(acc / l)
            ls.append(l)
        out = jnp.where(_own_half(1), outs[1], outs[0])
        o_ref[...] = out.astype(BF16)
        o32_ref[...] = out
        m_ref[...] = jnp.where(_own_half(1), carry[1][0], carry[0][0])
        l_ref[...] = jnp.where(_own_half(1), ls[1], ls[0])

    body = _with_comm(comm, 3, 4, lambda: (pl.program_id(0) == 0) & (pl.program_id(1) == 0),
                      lambda: (pl.program_id(0) == N_PAIR - 1) & (pl.program_id(1) == nq - 1), compute)
    pair = pl.BlockSpec((s_len, 2 * LANE), lambda p, i: (0, p))
    tile = pl.BlockSpec((t, LANE), lambda p, i: (i, p))
    wide = jax.ShapeDtypeStruct((s_len, FOX_W), F32)
    outs = pl.pallas_call(
        body, name="fox_fwd" if comm is None else "fox_fwd_comm%d" % comm.n, grid=(N_PAIR, nq),
        in_specs=[pl.BlockSpec((t, 2 * LANE), lambda p, i: (i, p)), pair, pair] + c_specs,
        out_specs=[tile, tile, tile, tile] + c_specs,
        out_shape=[jax.ShapeDtypeStruct((s_len, FOX_W), BF16), wide, wide, wide] + c_shapes,
        scratch_shapes=c_scratch,
        compiler_params=_params("arbitrary", "arbitrary"),
    )(qx, kx, vx, *c_arrays)
    return outs[0], outs[1], outs[2], outs[3], outs[4:]


def _fox_stats(o, do, m, l, tm=256):
    s_len = o.shape[0]
    tm = _div(s_len, tm, LANE)

    def body(o_ref, do_ref, m_ref, l_ref, dox_ref, st_ref):
        lane = lax.broadcasted_iota(jnp.int32, (1, LANE), 1)
        for p in range(N_PAIR):
            cols = slice(p * LANE, (p + 1) * LANE)
            dout = do_ref[:, cols]
            prod = o_ref[:, cols] * dout.astype(F32)
            shift = m_ref[:, cols]
            inv_l = 1.0 / l_ref[:, cols]
            st = jnp.zeros((tm, LANE), F32)
            for e in range(2):
                h = 2 * p + e
                dox_ref[:, h * LANE:(h + 1) * LANE] = jnp.where(_own_half(e), dout, jnp.zeros_like(dout))
                st = jnp.where(lane == e, shift[:, e * HEAD_DIM:e * HEAD_DIM + 1], st)
                delta = jnp.sum(jnp.where(_own_half(e), prod, 0.0), axis=1, keepdims=True)
                st = jnp.where(lane == 2 + e, delta, st)
                st = jnp.where(lane == 4 + e, inv_l[:, e * HEAD_DIM:e * HEAD_DIM + 1], st)
            st_ref[p] = st.T[:8, :]

    row = pl.BlockSpec((tm, FOX_W), lambda i: (i, 0))
    return pl.pallas_call(
        body, name="fox_stats", grid=(s_len // tm,), in_specs=[row, row, row, row],
        out_specs=[pl.BlockSpec((tm, FOX_HEADS * LANE), lambda i: (i, 0)),
                   pl.BlockSpec((N_PAIR, 8, tm), lambda i: (0, 0, i))],
        out_shape=[jax.ShapeDtypeStruct((s_len, FOX_HEADS * LANE), BF16),
                   jax.ShapeDtypeStruct((N_PAIR, 8, s_len), F32)],
        compiler_params=_params("parallel"),
    )(o, do, m, l)


def _fox_bwd(qx, kx, vx, dox, stats, comm=None, t_cap=1024):
    s_len = qx.shape[0]
    t = _div(s_len, t_cap, LANE)
    n = s_len // t
    c_specs, c_shapes, c_scratch, c_arrays = _comm_parts(comm)

    def compute(q_ref, do_ref, st_ref, k_ref, v_ref, dq_ref, dk_ref, dv_ref, dc_ref):
        j = pl.program_id(1)
        lane = lax.broadcasted_iota(jnp.int32, (1, LANE), 1)

        @pl.when(j == 0)
        def _():
            dq_ref[...] = jnp.zeros_like(dq_ref)

        ks = [k_ref[:, e * LANE:(e + 1) * LANE] for e in range(2)]
        vs = [v_ref[:, e * LANE:(e + 1) * LANE] for e in range(2)]
        ks_t = [k.astype(F32).T.astype(BF16) for k in ks]

        def step(i, carry, diag):
            rows = pl.ds(pl.multiple_of(i * t, t), t)
            new = []
            dq = jnp.zeros((LANE, t), F32)
            for e in range(2):
                dk, dv, dc = carry[e]
                q = q_ref[rows, e * LANE:(e + 1) * LANE]
                dout = do_ref[rows, e * LANE:(e + 1) * LANE]
                s_t = lax.dot_general(ks[e], q, NT, preferred_element_type=F32) * LOG2E
                if diag:
                    s_t = jnp.where(_causal(t, keys_first=True), s_t, NEG_INF)
                p_t = jnp.exp2(s_t - st_ref[0, e:e + 1, rows]).astype(BF16).astype(F32) * st_ref[0, 4 + e:5 + e, rows]
                dp_t = lax.dot_general(vs[e], dout, NT, preferred_element_type=F32)
                ds_f = p_t * (dp_t - st_ref[0, 2 + e:3 + e, rows])
                ds_t = ds_f.astype(BF16)
                dc = dc + jnp.sum(ds_f, axis=1, keepdims=True)
                dv = dv + jnp.dot(p_t.astype(BF16), dout, preferred_element_type=F32)
                dk = dk + jnp.dot(ds_t, q, preferred_element_type=F32)
                dq_e = jnp.dot(ks_t[e], ds_t, preferred_element_type=F32)
                dq = dq + jnp.where(_row_halves()[e], dq_e, 0.0)
                new.append((dk, dv, dc))
            dq_ref[rows, :] += dq.T * SCALE
            return tuple(new)

        zero = jnp.zeros((t, LANE), F32)
        init = (zero, zero, jnp.zeros((t, 1), F32))
        carry = step(j, (init, init), True)
        (dk0, dv0, dc0), (dk1, dv1, dc1) = lax.fori_loop(j + 1, n, lambda i, c: step(i, c, False), carry)
        dk_ref[...] = jnp.where(_own_half(1), dk1, dk0).astype(BF16)
        dv_ref[...] = jnp.where(_own_half(1), dv1, dv0).astype(BF16)
        dc_ref[...] = jnp.where(lane == 0, -dc0, jnp.where(lane == 1, -dc1, 0.0))

    body = _with_comm(comm, 5, 4, lambda: (pl.program_id(0) == 0) & (pl.program_id(1) == 0),
                      lambda: (pl.program_id(0) == N_PAIR - 1) & (pl.program_id(1) == n - 1), compute)
    pair = pl.BlockSpec((s_len, 2 * LANE), lambda p, j: (0, p))
    blk = pl.BlockSpec((t, 2 * LANE), lambda p, j: (j, p))
    tile = pl.BlockSpec((t, LANE), lambda p, j: (j, p))
    outs = pl.pallas_call(
        body, name="fox_bwd" if comm is None else "fox_bwd_comm", grid=(N_PAIR, n),
        in_specs=[pair, pair, pl.BlockSpec((1, 8, s_len), lambda p, j: (p, 0, 0)), blk, blk] + c_specs,
        out_specs=[pl.BlockSpec((s_len, LANE), lambda p, j: (0, p)), tile, tile, tile] + c_specs,
        out_shape=[jax.ShapeDtypeStruct((s_len, FOX_W), F32), jax.ShapeDtypeStruct((s_len, FOX_W), BF16),
                   jax.ShapeDtypeStruct((s_len, FOX_W), BF16), jax.ShapeDtypeStruct((s_len, FOX_W), F32)] + c_shapes,
        scratch_shapes=c_scratch,
        compiler_params=_params("arbitrary", "arbitrary"),
    )(qx, dox, stats, kx, vx, *c_arrays)
    return outs[0], outs[1], outs[2], outs[3], outs[4:]


def _mixer_out(attn_a, attn_b, zg, h, wpa, wpb, wout, g, b, tm=256):
    m = h.shape[0]
    tm = _div(m, tm, 8)

    def body(a_ref, b_ref, ga_ref, gb_ref, h_ref, wpa_ref, wpb_ref, wout_ref, g_ref, bb_ref,
             h1_ref, u_ref, mg_ref, ya_ref, yb_ref):
        ya = jnp.dot(a_ref[...], wpa_ref[...], preferred_element_type=F32)
        yb = jnp.dot(b_ref[...], wpb_ref[...], preferred_element_type=F32)
        merged = (jax.nn.sigmoid(ga_ref[...]) * ya + jax.nn.sigmoid(gb_ref[...]) * yb).astype(BF16)
        u = ALPHA * h_ref[...] + jnp.dot(merged, wout_ref[...], preferred_element_type=F32)
        u_ref[...] = u
        h1_ref[...] = _ln(u, g_ref[...], bb_ref[...])
        mg_ref[...] = merged
        ya_ref[...] = ya.astype(BF16)
        yb_ref[...] = yb.astype(BF16)

    row = pl.BlockSpec((tm, D_MODEL), lambda i: (i, 0))
    att = pl.BlockSpec((tm, SWA_Q), lambda i: (i, 0))
    vec = pl.BlockSpec((1, D_MODEL), lambda i: (0, 0))
    wsm = pl.BlockSpec((SWA_Q, D_MODEL), lambda i: (0, 0))
    return pl.pallas_call(
        body, name="mixer_out", grid=(m // tm,),
        in_specs=[att, att, row, pl.BlockSpec((tm, D_MODEL), lambda i: (i, 1)), row, wsm, wsm,
                  pl.BlockSpec((D_MODEL, D_MODEL), lambda i: (0, 0)), vec, vec],
        out_specs=[row, row, row, row, row],
        out_shape=[jax.ShapeDtypeStruct((m, D_MODEL), F32), jax.ShapeDtypeStruct((m, D_MODEL), F32),
                   jax.ShapeDtypeStruct((m, D_MODEL), BF16), jax.ShapeDtypeStruct((m, D_MODEL), BF16),
                   jax.ShapeDtypeStruct((m, D_MODEL), BF16)],
        compiler_params=_params("parallel"),
    )(attn_a, attn_b, zg, zg, h, wpa, wpb, wout, g, b)


def _mixer_bwd(dh1, u1, g, wout, ya, yb, zg, wpa, wpb, tm=256):
    m = dh1.shape[0]
    tm = _div(m, tm, 8)

    def body(dh_ref, u_ref, g_ref, wout_ref, ya_ref, yb_ref, ga_ref, gb_ref, wpa_ref, wpb_ref,
             du_ref, dg_ref, db_ref, dya_ref, dyb_ref, dga_ref, dgb_ref, da_ref, dbb_ref):
        @pl.when(pl.program_id(0) == 0)
        def _():
            dg_ref[...] = jnp.zeros_like(dg_ref)
            db_ref[...] = jnp.zeros_like(db_ref)

        du, dg, db = _ln_bwd_block(dh_ref[...], u_ref[...], g_ref[...])
        du_ref[...] = du
        dg_ref[...] += dg
        db_ref[...] += db
        dm = lax.dot_general(du.astype(BF16), wout_ref[...], (((1,), (1,)), ((), ())), preferred_element_type=F32)
        for y_ref, gate_ref, w_ref, dy_ref, dgate_ref, dattn_ref in (
                (ya_ref, ga_ref, wpa_ref, dya_ref, dga_ref, da_ref), (yb_ref, gb_ref, wpb_ref, dyb_ref, dgb_ref, dbb_ref)):
            sg = jax.nn.sigmoid(gate_ref[...])
            dy = (dm * sg).astype(BF16)
            dy_ref[...] = dy
            dgate_ref[...] = (dm * y_ref[...].astype(F32) * sg * (1.0 - sg)).astype(BF16)
            dattn_ref[...] = lax.dot_general(dy, w_ref[...], (((1,), (1,)), ((), ())),
                                             preferred_element_type=F32).astype(BF16)

    row = pl.BlockSpec((tm, D_MODEL), lambda i: (i, 0))
    att = pl.BlockSpec((tm, SWA_Q), lambda i: (i, 0))
    vec = pl.BlockSpec((1, D_MODEL), lambda i: (0, 0))
    wsm = pl.BlockSpec((SWA_Q, D_MODEL), lambda i: (0, 0))
    wide = jax.ShapeDtypeStruct((m, D_MODEL), BF16)
    narrow = jax.ShapeDtypeStruct((m, SWA_Q), BF16)
    sums = jax.ShapeDtypeStruct((1, D_MODEL), F32)
    return pl.pallas_call(
        body, name="mixer_bwd", grid=(m // tm,),
        in_specs=[row, row, vec, pl.BlockSpec((D_MODEL, D_MODEL), lambda i: (0, 0)), row, row, row,
                  pl.BlockSpec((tm, D_MODEL), lambda i: (i, 1)), wsm, wsm],
        out_specs=[row, vec, vec, row, row, row, row, att, att],
        out_shape=[jax.ShapeDtypeStruct((m, D_MODEL), F32), sums, sums, wide, wide, wide, wide, narrow, narrow],
        compiler_params=_params("arbitrary"),
    )(dh1, u1, g, wout, ya, yb, zg, zg, wpa, wpb)


def _shift_down(x, k, halo, first):
    rows = lax.broadcasted_iota(jnp.int32, (x.shape[0], 1), 0)
    y = pltpu.roll(x, k, 0)
    for r in range(k):
        fill = jnp.where(first, 0.0, halo[8 - k + r:8 - k + r + 1, :])
        y = jnp.where(rows == r, fill, y)
    return y


def _shift_up(x, k, halo, last):
    n = x.shape[0]
    rows = lax.broadcasted_iota(jnp.int32, (n, 1), 0)
    y = pltpu.roll(x, n - k, 0)
    for r in range(k):
        fill = jnp.where(last, 0.0, halo[r:r + 1, :])
        y = jnp.where(rows == n - k + r, fill, y)
    return y


def _conv_act(gate, gate_m1, gate_m2, cw, cb):
    return cb + cw[0:1, :] * gate_m2 + cw[1:2, :] * gate_m1 + cw[2:3, :] * gate


def _ffn_in_conv(h1, wfi, cw, cb, tm=256):
    s_len = h1.shape[0]
    tm = _div(s_len, tm, 8)
    hb = tm // 8

    def body(a_ref, ap_ref, w_ref, cw_ref, cb_ref, gu_ref, act_ref):
        first = pl.program_id(0) == 0
        a = a_ref[...].astype(BF16)
        before = ap_ref[...].astype(BF16)
        for c in range(N_CHUNK):
            gate = jnp.dot(a, w_ref[c], preferred_element_type=F32)
            up = jnp.dot(a, w_ref[N_CHUNK + c], preferred_element_type=F32)
            halo = jnp.dot(before, w_ref[c], preferred_element_type=F32)
            gu_ref[c, 0] = gate
            gu_ref[c, 1] = up
            conv = _conv_act(gate, _shift_down(gate, 1, halo, first), _shift_down(gate, 2, halo, first),
                             cw_ref[c], cb_ref[c])
            act_ref[c] = (conv * jax.nn.sigmoid(conv) * up).astype(BF16)

    return pl.pallas_call(
        body, name="ffn_in_conv", grid=(s_len // tm,),
        in_specs=[pl.BlockSpec((tm, D_MODEL), lambda i: (i, 0)),
                  pl.BlockSpec((8, D_MODEL), lambda i: (jnp.maximum(i * hb - 1, 0), 0)),
                  pl.BlockSpec((N_DEV, D_MODEL, FF_CHUNK), lambda i: (0, 0, 0)),
                  pl.BlockSpec((N_CHUNK, 8, FF_CHUNK), lambda i: (0, 0, 0)),
                  pl.BlockSpec((N_CHUNK, 1, FF_CHUNK), lambda i: (0, 0, 0))],
        out_specs=[pl.BlockSpec((N_CHUNK, 2, tm, FF_CHUNK), lambda i: (0, 0, i, 0)),
                   pl.BlockSpec((N_CHUNK, tm, FF_CHUNK), lambda i: (0, i, 0))],
        out_shape=[jax.ShapeDtypeStruct((N_CHUNK, 2, s_len, FF_CHUNK), F32),
                   jax.ShapeDtypeStruct((N_CHUNK, s_len, FF_CHUNK), BF16)],
        compiler_params=_params("parallel"),
    )(h1, h1, wfi, cw, cb)


def _ffn_out_ln(act, wfo, res, g, b, tm=256):
    s_len = res.shape[0]
    tm = _div(s_len, tm, 8)

    def body(a_ref, w_ref, res_ref, g_ref, b_ref, u_ref, y_ref):
        u = ALPHA * res_ref[...]
        for c in range(N_CHUNK):
            u = u + jnp.dot(a_ref[c], w_ref[c], preferred_element_type=F32)
        u_ref[...] = u
        y_ref[...] = _ln(u, g_ref[...], b_ref[...])

    row = pl.BlockSpec((tm, D_MODEL), lambda i: (i, 0))
    vec = pl.BlockSpec((1, D_MODEL), lambda i: (0, 0))
    return pl.pallas_call(
        body, name="ffn_out_ln", grid=(s_len // tm,),
        in_specs=[pl.BlockSpec((N_CHUNK, tm, FF_CHUNK), lambda i: (0, i, 0)),
                  pl.BlockSpec((N_CHUNK, FF_CHUNK, D_MODEL), lambda i: (0, 0, 0)), row, vec, vec],
        out_specs=[row, row],
        out_shape=[jax.ShapeDtypeStruct((s_len, D_MODEL), F32), jax.ShapeDtypeStruct((s_len, D_MODEL), F32)],
        compiler_params=_params("parallel"),
    )(act, wfo, res, g, b)


def _ffn_out_bwd(dh2, u2, g, wfo, tm=256):
    s_len = dh2.shape[0]
    tm = _div(s_len, tm, 8)

    def body(dh_ref, u_ref, g_ref, w_ref, du_ref, dg_ref, db_ref, o_ref):
        @pl.when(pl.program_id(0) == 0)
        def _():
            dg_ref[...] = jnp.zeros_like(dg_ref)
            db_ref[...] = jnp.zeros_like(db_ref)

        du, dg, db = _ln_bwd_block(dh_ref[...], u_ref[...], g_ref[...])
        du_ref[...] = du
        dg_ref[...] += dg
        db_ref[...] += db
        du_b = du.astype(BF16)
        for c in range(N_CHUNK):
            o_ref[c] = lax.dot_general(du_b, w_ref[c], (((1,), (1,)), ((), ())), preferred_element_type=F32)

    row = pl.BlockSpec((tm, D_MODEL), lambda i: (i, 0))
    vec = pl.BlockSpec((1, D_MODEL), lambda i: (0, 0))
    sums = jax.ShapeDtypeStruct((1, D_MODEL), F32)
    return pl.pallas_call(
        body, name="ffn_out_bwd", grid=(s_len // tm,),
        in_specs=[row, row, vec, pl.BlockSpec((N_CHUNK, FF_CHUNK, D_MODEL), lambda i: (0, 0, 0))],
        out_specs=[row, vec, vec, pl.BlockSpec((N_CHUNK, tm, FF_CHUNK), lambda i: (0, i, 0))],
        out_shape=[jax.ShapeDtypeStruct((s_len, D_MODEL), F32), sums, sums,
                   jax.ShapeDtypeStruct((N_CHUNK, s_len, FF_CHUNK), F32)],
        compiler_params=_params("arbitrary"),
    )(dh2, u2, g, wfo)


def _g_w_ffn_out(act, du, tm=2048):
    s_len = du.shape[0]
    tm = _div(s_len, tm, 8)
    steps = s_len // tm

    def body(a_ref, g_ref, o_ref, acc_ref):
        s = pl.program_id(1)

        @pl.when(s == 0)
        def _():
            acc_ref[...] = jnp.zeros_like(acc_ref)

        acc_ref[...] += lax.dot_general(a_ref[0], g_ref[...].astype(BF16), (((0,), (0,)), ((), ())),
                                        preferred_element_type=F32)

        @pl.when(s == steps - 1)
        def _():
            o_ref[0] = acc_ref[...].astype(BF16)

    return pl.pallas_call(
        body, name="g_w_ffn_out", grid=(N_CHUNK, steps),
        in_specs=[pl.BlockSpec((1, tm, FF_CHUNK), lambda c, s: (c, s, 0)),
                  pl.BlockSpec((tm, D_MODEL), lambda c, s: (s, 0))],
        out_specs=pl.BlockSpec((1, FF_CHUNK, D_MODEL), lambda c, s: (c, 0, 0)),
        out_shape=jax.ShapeDtypeStruct((N_CHUNK, FF_CHUNK, D_MODEL), BF16),
        scratch_shapes=[pltpu.VMEM((FF_CHUNK, D_MODEL), F32)],
        compiler_params=_params("parallel", "arbitrary"),
    )(act, du)


def _g_w_ffn_in(h1, dgu, tm=2048):
    s_len = h1.shape[0]
    tm = _div(s_len, tm, 8)
    steps = s_len // tm

    def body(a_ref, g_ref, o_ref, acc_ref):
        s = pl.program_id(1)

        @pl.when(s == 0)
        def _():
            acc_ref[...] = jnp.zeros_like(acc_ref)

        acc_ref[...] += lax.dot_general(a_ref[...].astype(BF16), g_ref[0, 0], (((0,), (0,)), ((), ())),
                                        preferred_element_type=F32)

        @pl.when(s == steps - 1)
        def _():
            o_ref[0] = acc_ref[...].astype(BF16)

    return pl.pallas_call(
        body, name="g_w_ffn_in", grid=(N_DEV, steps),
        in_specs=[pl.BlockSpec((tm, D_MODEL), lambda d, s: (s, 0)),
                  pl.BlockSpec((1, 1, tm, FF_CHUNK), lambda d, s: (d % N_CHUNK, d // N_CHUNK, s, 0))],
        out_specs=pl.BlockSpec((1, D_MODEL, FF_CHUNK), lambda d, s: (d, 0, 0)),
        out_shape=jax.ShapeDtypeStruct((N_DEV, D_MODEL, FF_CHUNK), BF16),
        scratch_shapes=[pltpu.VMEM((D_MODEL, FF_CHUNK), F32)],
        compiler_params=_params("parallel", "arbitrary"),
    )(h1, dgu)


def _conv_bwd_dh1(gu, dact, cw, cb, wfi, res, tm=256):
    s_len = gu.shape[2]
    tm = _div(s_len, tm, 8)
    nrow = s_len // tm
    hb = tm // 8

    def dconv_of(conv, up, da):
        sg = jax.nn.sigmoid(conv)
        return da * up * (sg * (1.0 + conv * (1.0 - sg)))

    def body(gu_ref, gp_ref, gun_ref, da_ref, dan_ref, cw_ref, cb_ref, w_ref, res_ref, dgu_ref, dcw_ref, dh_ref):
        i = pl.program_id(0)
        first = i == 0
        last = i == nrow - 1

        @pl.when(first)
        def _():
            dcw_ref[...] = jnp.zeros_like(dcw_ref)

        row = lax.broadcasted_iota(jnp.int32, (8, 1), 0)
        acc = ALPHA * res_ref[...]
        for c in range(N_CHUNK):
            cw = cw_ref[c]
            cb = cb_ref[c]
            gate = gu_ref[c, 0]
            halo = gp_ref[c, 0]
            g_m1 = _shift_down(gate, 1, halo, first)
            g_m2 = _shift_down(gate, 2, halo, first)
            conv = _conv_act(gate, g_m1, g_m2, cw, cb)
            da = da_ref[c]
            dup = (da * conv * jax.nn.sigmoid(conv)).astype(BF16)
            dconv = dconv_of(conv, gu_ref[c, 1], da)
            gate_n = gun_ref[c, 0]
            tail = gate[tm - 8:, :]
            conv_n = _conv_act(gate_n, _shift_down(gate_n, 1, tail, False), _shift_down(gate_n, 2, tail, False),
                               cw, cb)
            dconv_n = dconv_of(conv_n, gun_ref[c, 1], dan_ref[c])
            dgate = (cw[2:3, :] * dconv + cw[1:2, :] * _shift_up(dconv, 1, dconv_n, last)
                     + cw[0:1, :] * _shift_up(dconv, 2, dconv_n, last)).astype(BF16)
            dgu_ref[c, 0] = dgate
            dgu_ref[c, 1] = dup
            acc = acc + lax.dot_general(dgate, w_ref[c], NT, preferred_element_type=F32)
            acc = acc + lax.dot_general(dup, w_ref[N_CHUNK + c], NT, preferred_element_type=F32)
            part = jnp.zeros((8, FF_CHUNK), F32)
            for r, term in enumerate((dconv * g_m2, dconv * g_m1, dconv * gate, dconv)):
                part = jnp.where(row == r, jnp.sum(term, axis=0, keepdims=True), part)
            dcw_ref[c] += part
        dh_ref[...] = acc

    nxt = lambda i: jnp.minimum((i + 1) * hb, s_len // 8 - 1)
    main = pl.BlockSpec((N_CHUNK, 2, tm, FF_CHUNK), lambda i: (0, 0, i, 0))
    row_d = pl.BlockSpec((tm, D_MODEL), lambda i: (i, 0))
    return pl.pallas_call(
        body, name="conv_bwd_dh1", grid=(nrow,),
        in_specs=[main,
                  pl.BlockSpec((N_CHUNK, 1, 8, FF_CHUNK), lambda i: (0, 0, jnp.maximum(i * hb - 1, 0), 0)),
                  pl.BlockSpec((N_CHUNK, 2, 8, FF_CHUNK), lambda i: (0, 0, nxt(i), 0)),
                  pl.BlockSpec((N_CHUNK, tm, FF_CHUNK), lambda i: (0, i, 0)),
                  pl.BlockSpec((N_CHUNK, 8, FF_CHUNK), lambda i: (0, nxt(i), 0)),
                  pl.BlockSpec((N_CHUNK, 8, FF_CHUNK), lambda i: (0, 0, 0)),
                  pl.BlockSpec((N_CHUNK, 1, FF_CHUNK), lambda i: (0, 0, 0)),
                  pl.BlockSpec((N_DEV, D_MODEL, FF_CHUNK), lambda i: (0, 0, 0)), row_d],
        out_specs=[main, pl.BlockSpec((N_CHUNK, 8, FF_CHUNK), lambda i: (0, 0, 0)), row_d],
        out_shape=[jax.ShapeDtypeStruct((N_CHUNK, 2, s_len, FF_CHUNK), BF16),
                   jax.ShapeDtypeStruct((N_CHUNK, 8, FF_CHUNK), F32),
                   jax.ShapeDtypeStruct((s_len, D_MODEL), F32)],
        compiler_params=_params("arbitrary"),
    )(gu, gu, gu, dact, dact, cw, cb, wfi, res)


def _loss_head(y, target, tm=256):
    m, d = y.shape
    tm = _div(m, tm, 8)

    def body(y_ref, t_ref, dy_ref, loss_ref):
        @pl.when(pl.program_id(0) == 0)
        def _():
            loss_ref[...] = jnp.zeros_like(loss_ref)

        err = y_ref[...] - t_ref[...]
        dy_ref[...] = err / d
        loss_ref[...] += 0.5 * jnp.sum(jnp.sum(err * err, axis=1, keepdims=True) / d, axis=0, keepdims=True)

    row = pl.BlockSpec((tm, d), lambda i: (i, 0))
    return pl.pallas_call(
        body, name="loss_head", grid=(m // tm,), in_specs=[row, row],
        out_specs=[row, pl.BlockSpec((8, LANE), lambda i: (0, 0))],
        out_shape=[jax.ShapeDtypeStruct((m, d), F32), jax.ShapeDtypeStruct((8, LANE), F32)],
        compiler_params=_params("arbitrary"),
    )(y, target)


def _sum_devices(r_ref):
    acc = r_ref[0].astype(F32)
    for d in range(1, N_DEV):
        acc = acc + r_ref[d].astype(F32)
    return acc


def _sum8(recv):
    rows = recv.shape[1]
    tr = _div(rows, ROW_BLOCK, 8)

    def body(r_ref, o_ref):
        o_ref[...] = _sum_devices(r_ref)

    return pl.pallas_call(
        body, name="sum8", grid=(rows // tr,),
        in_specs=[pl.BlockSpec((N_DEV, tr, LANE), lambda i: (0, i, 0))],
        out_specs=pl.BlockSpec((tr, LANE), lambda i: (i, 0)),
        out_shape=jax.ShapeDtypeStruct((rows, LANE), F32),
        compiler_params=_params("parallel"),
    )(recv)


def _adamw_math(w, g, m, v):
    m = ADAM_B1 * m + (1.0 - ADAM_B1) * g
    v = ADAM_B2 * v + (1.0 - ADAM_B2) * (g * g)
    m_hat = m / (1.0 - ADAM_B1 ** ADAM_STEP)
    v_hat = v / (1.0 - ADAM_B2 ** ADAM_STEP)
    return -ADAM_LR * (m_hat / (jnp.sqrt(v_hat) + ADAM_EPS) + ADAM_WD * w), m, v


def _adamw_rows(w, g, m, v, name):
    rows = w.shape[0]
    tr = _div(rows, ROW_BLOCK, 8)

    def body(w_ref, g_ref, m_ref, v_ref, d_ref, mo_ref, vo_ref):
        d_ref[...], mo_ref[...], vo_ref[...] = _adamw_math(w_ref[...], g_ref[...], m_ref[...], v_ref[...])

    blk = pl.BlockSpec((tr, LANE), lambda i: (i, 0))
    out = jax.ShapeDtypeStruct((rows, LANE), F32)
    return pl.pallas_call(
        body, name=name, grid=(rows // tr,), in_specs=[blk, blk, blk, blk], out_specs=[blk, blk, blk],
        out_shape=[out, out, out], compiler_params=_params("parallel"),
    )(w, g, m, v)


def _adamw_shard(recv, w, m, v, layer, prev, name):
    _, k, n = recv.shape
    tk = _div(k, 128, 16)

    def body(r_ref, w_ref, m_ref, v_ref, *rest):
        g_ref, d_ref, mo_ref, vo_ref = rest[-4:]
        g = _sum_devices(r_ref)
        g_ref[0] = g
        d_ref[0], mo_ref[0], vo_ref[0] = _adamw_math(w_ref[0], g, m_ref[0], v_ref[0])

    blk = pl.BlockSpec((1, tk, n), lambda i: (layer, i, 0))
    out = jax.ShapeDtypeStruct((DEPTH, k, n), F32)
    carried = [] if prev is None else list(prev)
    return pl.pallas_call(
        body, name=name, grid=(k // tk,),
        in_specs=[pl.BlockSpec((N_DEV, tk, n), lambda i: (0, i, 0)), blk, blk, blk]
        + [pl.BlockSpec(memory_space=pl.ANY)] * len(carried),
        out_specs=[blk, blk, blk, blk], out_shape=[out, out, out, out],
        input_output_aliases={4 + j: j for j in range(len(carried))},
        compiler_params=_params("parallel"),
    )(recv, w, m, v, *carried)


def _to_rows(flat, rows):
    flat = flat.reshape(-1)
    return jnp.pad(flat, (0, rows * LANE - flat.shape[0])).reshape(rows, LANE)


def _pad_cols_z(a):
    f0 = N_QKV
    g0 = N_QKV + FOX_HEADS
    pad = jnp.zeros(a.shape[:-1] + (F_PAD - FOX_HEADS,), a.dtype)
    return jnp.concatenate([a[..., :f0], a[..., g0:], a[..., f0:g0], pad], axis=-1)


def _unpad_cols_z(a):
    f0 = N_QKV + N_GATE
    return jnp.concatenate([a[..., :N_QKV], a[..., f0:f0 + FOX_HEADS], a[..., N_QKV:f0]], axis=-1)


def _shards_to_cols(g):
    _, k, n = g.shape
    return g.transpose(1, 0, 2).reshape(k, N_DEV * n)


def _cols_to_shards(full):
    k, n = full.shape
    return full.reshape(k, N_DEV, n // N_DEV).transpose(1, 0, 2)


def _layer_fwd(h, w, p, comm=None, late=None):
    zq, zg = _z_proj(h, w["w_in_p"], p["b_in_p"])
    qx, kx, vx = _fox_prep(zq, _cumsum_logf(zg))
    attn_a, lse_a = _swa_fwd(zq, p["sinks"])
    attn_b, attn_b32, m_b, l_b, arrived = _fox_fwd(qx, kx, vx, comm)
    if late is not None:
        w, p = late(w, p, arrived)
    h1, u1, merged, ya, yb = _mixer_out(attn_a, attn_b, zg, h, w["w_proj_a"], w["w_proj_b"], w["w_out"],
                                        p["ln_mix_g"], p["ln_mix_b"])
    gu, act = _ffn_in_conv(h1, w["w_ffn_in"], p["conv_w"], p["conv_b"])
    u2, h2 = _ffn_out_ln(act, w["w_ffn_out"], h1, p["ln_ffn_g"], p["ln_ffn_b"])
    saved = dict(h=h, zq=zq, zg=zg, qx=qx, kx=kx, vx=vx, attn_a=attn_a, lse_a=lse_a, attn_b=attn_b,
                 attn_b32=attn_b32, m_b=m_b, l_b=l_b, h1=h1, u1=u1, merged=merged, ya=ya, yb=yb, gu=gu, act=act, u2=u2)
    return h2, saved, w, p


def _layer_bwd(dh2, sv, w, p, make_comm=None, make_last_comm=None):
    s_len = dh2.shape[0]
    du2, d_ffn_g, d_ffn_b, dact = _ffn_out_bwd(dh2, sv["u2"], p["ln_ffn_g"], w["w_ffn_out"])
    g_ffn_out = _g_w_ffn_out(sv["act"], du2)
    dgu, dcw, dh1 = _conv_bwd_dh1(sv["gu"], dact, p["conv_w"], p["conv_b"], w["w_ffn_in"], du2)
    dcw = dcw.transpose(1, 0, 2).reshape(8, D_FF)
    g_ffn_in = _g_w_ffn_in(sv["h1"], dgu)
    du1, d_mix_g, d_mix_b, dya, dyb, dga, dgb, dattn_a, dattn_b = _mixer_bwd(
        dh1, sv["u1"], p["ln_mix_g"], w["w_out"], sv["ya"], sv["yb"], sv["zg"], w["w_proj_a"], w["w_proj_b"])
    g_out = _linear_tn(sv["merged"], du1, name="g_w_out", tn=1024)
    g_proj_a = _linear_tn(sv["attn_a"], dya, name="g_w_proj_a", tk=512, tn=1024)
    g_proj_b = _linear_tn(sv["attn_b"], dyb, name="g_w_proj_b", tk=512, tn=1024)
    dq_a, dk_a, dv_a, dsinks = _swa_bwd(sv["zq"], p["sinks"], sv["attn_a"], dattn_a, sv["lse_a"])
    big = dict(w_proj_a=_cols_to_shards(g_proj_a), w_proj_b=_cols_to_shards(g_proj_b),
               w_out=g_out.reshape(N_DEV, D_MODEL // N_DEV, D_MODEL), w_ffn_in=g_ffn_in,
               w_ffn_out=g_ffn_out.reshape(N_DEV, D_FF // N_DEV, D_MODEL))
    dox, stats = _fox_stats(sv["attn_b32"], dattn_b, sv["m_b"], sv["l_b"])
    dq_b, dk_b, dv_b, dcc, arrived = _fox_bwd(sv["qx"], sv["kx"], sv["vx"], dox, stats,
                                              None if make_comm is None else make_comm(big))
    df = _forget_bwd(dcc, sv["zg"])
    dz = jnp.concatenate([dq_a, dk_a.astype(BF16), dv_a.astype(BF16), dq_b.astype(BF16), dk_b, dv_b, dga, dgb, df,
                          jnp.zeros((s_len, F_PAD - LANE), BF16)], axis=1)
    g_in = _unpad_cols_z(_linear_tn(sv["h"], dz, name="g_w_in", tn=768))
    g_b_in = _unpad_cols_z(_colsum(dz, name="g_b_in"))
    big["w_in"] = _cols_to_shards(g_in)
    small = dict(ln_mix_g=d_mix_g, ln_mix_b=d_mix_b, b_in=g_b_in, attn_sinks=dsinks[:, :SWA_HEADS],
                 ln_ffn_g=d_ffn_g, ln_ffn_b=d_ffn_b, conv_w=dcw[:3], conv_b=dcw[3:4])
    dh, arrived_last = _d_h(dz, w["w_in_p"], du1, None if make_last_comm is None else make_last_comm(big, small))
    return dh, big, small, arrived, arrived_last


def _w_in_layouts(w_in):
    return dict(w_in_p=_pad_cols_z(_shards_to_cols(w_in)))


def _other_layouts(w_proj_a, w_proj_b, w_out, w_ffn_in, w_ffn_out):
    return dict(w_proj_a=_shards_to_cols(w_proj_a), w_proj_b=_shards_to_cols(w_proj_b),
                w_out=w_out.reshape(D_MODEL, D_MODEL), w_ffn_in=w_ffn_in,
                w_ffn_out=w_ffn_out.reshape(N_CHUNK, FF_CHUNK, D_MODEL))


def _layer_params(r):
    return dict(
        b_in_p=_pad_cols_z(r["b_in"].reshape(1, N_IN)),
        sinks=jnp.pad(r["attn_sinks"].reshape(1, SWA_HEADS), ((0, 0), (0, LANE - SWA_HEADS))),
        ln_mix_g=r["ln_mix_g"].reshape(1, D_MODEL), ln_mix_b=r["ln_mix_b"].reshape(1, D_MODEL),
        ln_ffn_g=r["ln_ffn_g"].reshape(1, D_MODEL), ln_ffn_b=r["ln_ffn_b"].reshape(1, D_MODEL),
        conv_b=r["conv_b"].reshape(N_CHUNK, 1, FF_CHUNK))


def _conv_w_layout(conv_w):
    return jnp.pad(conv_w, ((0, 5), (0, 0))).reshape(8, N_CHUNK, FF_CHUNK).transpose(1, 0, 2)


def kernel(x, ln_mix_g, ln_mix_b, w_in, b_in, attn_sinks, w_proj_a, w_proj_b, w_out, ln_ffn_g, ln_ffn_b, w_ffn_in, conv_w, conv_b, w_ffn_out, loss_target, m_ln_mix_g, m_ln_mix_b, m_w_in, m_b_in, m_attn_sinks, m_w_proj_a, m_w_proj_b, m_w_out, m_ln_ffn_g, m_ln_ffn_b, m_w_ffn_in, m_conv_w, m_conv_b, m_w_ffn_out, v_ln_mix_g, v_ln_mix_b, v_w_in, v_b_in, v_attn_sinks, v_w_proj_a, v_w_proj_b, v_w_out, v_ln_ffn_g, v_ln_ffn_b, v_w_ffn_in, v_conv_w, v_conv_b, v_w_ffn_out):
    wts = dict(ln_mix_g=ln_mix_g, ln_mix_b=ln_mix_b, w_in=w_in, b_in=b_in, attn_sinks=attn_sinks, w_proj_a=w_proj_a,
               w_proj_b=w_proj_b, w_out=w_out, ln_ffn_g=ln_ffn_g, ln_ffn_b=ln_ffn_b, w_ffn_in=w_ffn_in,
               conv_w=conv_w, conv_b=conv_b, w_ffn_out=w_ffn_out)
    mom = dict(ln_mix_g=m_ln_mix_g, ln_mix_b=m_ln_mix_b, w_in=m_w_in, b_in=m_b_in, attn_sinks=m_attn_sinks,
               w_proj_a=m_w_proj_a, w_proj_b=m_w_proj_b, w_out=m_w_out, ln_ffn_g=m_ln_ffn_g, ln_ffn_b=m_ln_ffn_b,
               w_ffn_in=m_w_ffn_in, conv_w=m_conv_w, conv_b=m_conv_b, w_ffn_out=m_w_ffn_out)
    vel = dict(ln_mix_g=v_ln_mix_g, ln_mix_b=v_ln_mix_b, w_in=v_w_in, b_in=v_b_in, attn_sinks=v_attn_sinks,
               w_proj_a=v_w_proj_a, w_proj_b=v_w_proj_b, w_out=v_w_out, ln_ffn_g=v_ln_ffn_g, ln_ffn_b=v_ln_ffn_b,
               w_ffn_in=v_w_ffn_in, conv_w=v_conv_w, conv_b=v_conv_b, w_ffn_out=v_w_ffn_out)
    names = list(wts)
    big_names = [n for n, _, _ in BIG]
    small_names = [n for n, _ in SMALL]
    me = 4 * lax.axis_index("x") + 2 * lax.axis_index("y") + lax.axis_index("c")
    cw_shard = D_FF // N_DEV

    wb = {n: wts[n].astype(BF16) for n in big_names}
    ps = [_layer_params(dict(b_in=b_in[l], attn_sinks=attn_sinks[l], ln_mix_g=ln_mix_g[l], ln_mix_b=ln_mix_b[l],
                             ln_ffn_g=ln_ffn_g[l], ln_ffn_b=ln_ffn_b[l], conv_b=conv_b[l])) for l in range(DEPTH)]
    w_in_0, = _exchange([(wb["w_in"][0], True)], "gather_w_in_0")
    others = big_names[1:]
    next_layer = {}

    def late_0(w, p, arrived):
        conv_full = arrived[-1].transpose(1, 2, 0, 3).reshape(DEPTH, 3, D_FF)
        next_layer["w"] = _w_in_layouts(arrived[len(others)])
        next_layer["p"] = dict(ps[1], conv_w=_conv_w_layout(conv_full[1]))
        return dict(w, **_other_layouts(*arrived[:len(others)])), dict(p, conv_w=_conv_w_layout(conv_full[0]))

    def late_1(w, p, arrived):
        return dict(w, **_other_layouts(*arrived)), p

    saved, ws = [None] * DEPTH, [None] * DEPTH
    gather_0 = _Comm([(wb[n][0], True) for n in others] + [(wb["w_in"][1], True), (conv_w, True)])
    h, saved[0], ws[0], ps[0] = _layer_fwd(x[0], _w_in_layouts(w_in_0), ps[0], gather_0, late_0)
    gather_1 = _Comm([(wb[n][1], True) for n in others])
    h, saved[1], ws[1], ps[1] = _layer_fwd(h, next_layer["w"], next_layer["p"], gather_1, late_1)
    dh, loss_part = _loss_head(h, loss_target[0])

    def small_rows(small):
        vec = jnp.concatenate([small[n].reshape(-1) for n in small_names] + [loss_part[0, 0].reshape(1)])
        return _to_rows(vec, SMALL_LAYER_ROWS)

    dh, big_1, small_1, _, _ = _layer_bwd(dh, saved[1], ws[1], ps[1])

    def exchange_early(big_0):
        return _Comm([(big_1[n].astype(BF16), False) for n in big_names] + [(small_rows(small_1), True)]
                     + [(big_0[n].astype(BF16), False) for n in others])

    def exchange_last(big_0, small_0):
        return _Comm([(big_0["w_in"].astype(BF16), False), (small_rows(small_0), True)])

    grad_x, _, _, arrived, (g_in_0, g_small_0) = _layer_bwd(dh, saved[0], ws[0], ps[0], exchange_early, exchange_last)
    n_big = len(big_names)
    recv = [[g_in_0] + list(arrived[n_big + 1:]) + [g_small_0], list(arrived[:n_big + 1])]

    big_out = {}
    for t, n in enumerate(big_names):
        outs = None
        for l in reversed(range(DEPTH)):
            outs = _adamw_shard(recv[l][t], wts[n], mom[n], vel[n], l, outs, "adamw_%s_%d" % (n, l))
        big_out[n] = outs
    small_sum = [_sum8(recv[l][-1]).reshape(-1) for l in range(DEPTH)]
    g_small = {}
    off = 0
    for n, size in SMALL:
        g_small[n] = jnp.stack([small_sum[l][off:off + size] for l in range(DEPTH)])
        off += size
    loss = small_sum[0][off]
    g_small["conv_w"] = lax.dynamic_slice_in_dim(g_small["conv_w"].reshape(DEPTH, 3, D_FF), me * cw_shard, cw_shard,
                                                 axis=2)
    g_small = {n: g_small[n].reshape(wts[n].shape) for n in small_names}

    def pack_small(tree):
        return _to_rows(jnp.concatenate([tree[n].reshape(-1) for n in small_names]), SMALL_ROWS)

    small_out = (pack_small(g_small),) + tuple(_adamw_rows(pack_small(wts), pack_small(g_small), pack_small(mom),
                                                           pack_small(vel), "adamw_small"))

    def result(j):
        out = {n: big_out[n][j] for n in big_names}
        flat = small_out[j].reshape(-1)
        off = 0
        for n in small_names:
            out[n] = flat[off:off + wts[n].size].reshape(wts[n].shape)
            off += wts[n].size
        return [out[n] for n in names]

    return (loss, grad_x[None], *result(0), *result(1), *result(2), *result(3))
```

```python
import functools

import jax
import jax.numpy as jnp
import numpy as np
from jax import lax
from jax.experimental import pallas as pl
from jax.experimental.pallas import tpu as pltpu

F32 = jnp.float32
BF16 = jnp.bfloat16
MESH = pl.DeviceIdType.MESH

N_DEV = 8
DEPTH = 2
D_MODEL = 1024
HEAD_DIM = 64
SWA_Q = 512
SWA_KV = 128
FOX_W = 512
FOX_HEADS = 8
SWA_HEADS = 8
D_FF = 2816
N_IN = 4360
N_QKV = SWA_Q + 2 * SWA_KV + 3 * FOX_W
N_GATE = 2 * D_MODEL
F_PAD = 256
N_ZG = N_GATE + F_PAD
N_ZP = N_QKV + N_ZG
LN_EPS = 1e-5
NEG_INF = -1e30
ALPHA = (2 * DEPTH) ** 0.25
SCALE = HEAD_DIM ** -0.5
LOG2E = 1.4426950408889634
SLOPES = tuple(2.0 ** (-8.0 * (h + 1) / SWA_HEADS) for h in range(SWA_HEADS))

ADAM_LR = 0.001
ADAM_B1 = 0.9
ADAM_B2 = 0.999
ADAM_EPS = 1e-08
ADAM_WD = 0.01
ADAM_STEP = 10

LANE = 128
VMEM_LIMIT = 56 * 1024 * 1024

BIG = (("w_in", (D_MODEL, N_IN), 1), ("w_proj_a", (SWA_Q, D_MODEL), 1), ("w_proj_b", (FOX_W, D_MODEL), 1),
       ("w_out", (D_MODEL, D_MODEL), 0), ("w_ffn_in", (D_MODEL, 2 * D_FF), 1), ("w_ffn_out", (D_FF, D_MODEL), 0))
SMALL = (("ln_mix_g", D_MODEL), ("ln_mix_b", D_MODEL), ("b_in", N_IN), ("attn_sinks", SWA_HEADS),
         ("ln_ffn_g", D_MODEL), ("ln_ffn_b", D_MODEL), ("conv_w", 3 * D_FF), ("conv_b", D_FF))
ROW_BLOCK = 512
SMALL_LAYER_ROWS = -(-(sum(n for _, n in SMALL) + 1) // (8 * LANE)) * 8
SMALL_ROWS = ROW_BLOCK
FF_CHUNK = 2 * D_FF // N_DEV
N_CHUNK = D_FF // FF_CHUNK


def _div(n, cap, unit):
    if n <= cap:
        return n
    best = None
    for t in range(unit, cap + 1, unit):
        if n % t == 0:
            best = t
    assert best is not None, (n, cap, unit)
    return best


def _params(*sem):
    return pltpu.CompilerParams(dimension_semantics=sem, vmem_limit_bytes=VMEM_LIMIT)


def _peer(r):
    x, y, c = lax.axis_index("x"), lax.axis_index("y"), lax.axis_index("c")
    px = 1 - x if (r >> 2) & 1 else x
    py = 1 - y if (r >> 1) & 1 else y
    pc = 1 - c if r & 1 else c
    return (px, py, pc), 4 * px + 2 * py + pc


class _Comm:
    def __init__(self, tensors):
        self.arrays = [x for x, _ in tensors]
        self.gathers = [g for _, g in tensors]
        self.n = len(tensors)
        self.out_shape = [jax.ShapeDtypeStruct((N_DEV,) + (x.shape if g else x.shape[1:]), x.dtype)
                          for x, g in tensors]
        self.specs = [pl.BlockSpec(memory_space=pl.ANY)] * self.n
        self.scratch = [pltpu.SemaphoreType.DMA((N_DEV - 1, self.n)), pltpu.SemaphoreType.DMA((N_DEV - 1, self.n)),
                        pltpu.SemaphoreType.DMA((self.n,))]

    def _copies(self, x_refs, out_refs, sems):
        send_sems, recv_sems, local_sems = sems
        _, me = _peer(0)

        def src(t, idx):
            return x_refs[t] if self.gathers[t] else x_refs[t].at[idx]

        def remote(r, t, mine):
            peer, pid = _peer(r)
            return pltpu.make_async_remote_copy(src_ref=src(t, pid), dst_ref=out_refs[t].at[me if mine else pid],
                                                send_sem=send_sems.at[r - 1, t], recv_sem=recv_sems.at[r - 1, t],
                                                device_id=peer, device_id_type=MESH)

        pairs = [(r, t) for r in range(1, N_DEV) for t in range(self.n)]
        local = [pltpu.make_async_copy(src(t, me), out_refs[t].at[me], local_sems.at[t]) for t in range(self.n)]
        return local, [remote(r, t, True) for r, t in pairs], lambda: [remote(r, t, False) for r, t in pairs]

    def start(self, x_refs, out_refs, sems):
        local, sent, _ = self._copies(x_refs, out_refs, sems)
        for cp in local + sent:
            cp.start()

    def wait(self, x_refs, out_refs, sems):
        local, sent, landing = self._copies(x_refs, out_refs, sems)
        for cp in landing():
            cp.wait_recv()
        for cp in sent:
            cp.wait_send()
        for cp in local:
            cp.wait()


def _exchange(tensors, name):
    comm = _Comm(tensors)
    n = comm.n

    def body(*refs):
        comm.start(refs[:n], refs[n:2 * n], refs[2 * n:])
        comm.wait(refs[:n], refs[n:2 * n], refs[2 * n:])

    return pl.pallas_call(body, name=name, out_shape=comm.out_shape, in_specs=comm.specs, out_specs=comm.specs,
                          scratch_shapes=comm.scratch)(*comm.arrays)


def _with_comm(comm, n_in, n_out, first, last, compute):
    nc = comm.n if comm is not None else 0

    def body(*refs):
        ins, x_refs = refs[:n_in], refs[n_in:n_in + nc]
        outs = refs[n_in + nc:n_in + nc + n_out]
        out_refs = refs[n_in + nc + n_out:n_in + 2 * nc + n_out]
        sems = refs[n_in + 2 * nc + n_out:]
        if nc:
            @pl.when(first())
            def _():
                comm.start(x_refs, out_refs, sems)

        compute(*ins, *outs)
        if nc:
            @pl.when(last())
            def _():
                comm.wait(x_refs, out_refs, sems)

    return body


def _d_h(dz, w_in_t, res, comm=None, tm=512):
    m, k = dz.shape
    d = w_in_t.shape[1]
    tm = _div(m, tm, 8)
    steps = m // tm
    c_specs, c_shapes, c_scratch, c_arrays = _comm_parts(comm)

    def compute(dz_ref, w_ref, res_ref, o_ref):
        o_ref[...] = ALPHA * res_ref[...] + jnp.dot(dz_ref[...], w_ref[...], preferred_element_type=F32)

    body = _with_comm(comm, 3, 1, lambda: pl.program_id(0) == 0, lambda: pl.program_id(0) == steps - 1, compute)
    row = pl.BlockSpec((tm, d), lambda i: (i, 0))
    outs = pl.pallas_call(
        body, name="d_h" if comm is None else "d_h_comm", grid=(steps,),
        in_specs=[pl.BlockSpec((tm, k), lambda i: (i, 0)), pl.BlockSpec((k, d), lambda i: (0, 0)), row] + c_specs,
        out_specs=[row] + c_specs, out_shape=[jax.ShapeDtypeStruct((m, d), F32)] + c_shapes,
        scratch_shapes=c_scratch,
        compiler_params=_params("arbitrary"),
    )(dz, w_in_t, res, *c_arrays)
    return outs[0], outs[1:]


def _z_proj(h, w_in_t, b_p, tm=512):
    m, k = h.shape
    tm = _div(m, tm, 8)
    nt = (((1,), (1,)), ((), ()))

    def body(h_ref, w_ref, b_ref, zq_ref, zg_ref):
        a = h_ref[...].astype(BF16)
        zq = lax.dot_general(a, w_ref[:N_QKV, :], nt, preferred_element_type=F32)
        zq_ref[...] = (zq + b_ref[:, :N_QKV]).astype(BF16)
        zg_ref[...] = lax.dot_general(a, w_ref[N_QKV:, :], nt, preferred_element_type=F32) + b_ref[:, N_QKV:]

    return pl.pallas_call(
        body, name="z_proj", grid=(m // tm,),
        in_specs=[pl.BlockSpec((tm, k), lambda i: (i, 0)), pl.BlockSpec((N_ZP, k), lambda i: (0, 0)),
                  pl.BlockSpec((1, N_ZP), lambda i: (0, 0))],
        out_specs=[pl.BlockSpec((tm, N_QKV), lambda i: (i, 0)), pl.BlockSpec((tm, N_ZG), lambda i: (i, 0))],
        out_shape=[jax.ShapeDtypeStruct((m, N_QKV), BF16), jax.ShapeDtypeStruct((m, N_ZG), F32)],
        compiler_params=_params("parallel"),
    )(h, w_in_t, b_p)


def _linear_tn(a, g, *, name, tk=1024, tn=640, tm=2048):
    m, k = a.shape
    n = g.shape[1]
    tk = _div(k, tk, LANE)
    tn = _div(n, tn, LANE)
    tm = _div(m, tm, 8)
    steps = m // tm

    def body(a_ref, g_ref, o_ref, acc_ref):
        s = pl.program_id(2)

        @pl.when(s == 0)
        def _():
            acc_ref[...] = jnp.zeros_like(acc_ref)

        acc_ref[...] += lax.dot_general(a_ref[...].astype(BF16), g_ref[...].astype(BF16), (((0,), (0,)), ((), ())),
                                        preferred_element_type=F32)

        @pl.when(s == steps - 1)
        def _():
            o_ref[...] = acc_ref[...].astype(BF16)

    return pl.pallas_call(
        body, name=name, grid=(k // tk, n // tn, steps),
        in_specs=[pl.BlockSpec((tm, tk), lambda i, j, s: (s, i)), pl.BlockSpec((tm, tn), lambda i, j, s: (s, j))],
        out_specs=pl.BlockSpec((tk, tn), lambda i, j, s: (i, j)),
        out_shape=jax.ShapeDtypeStruct((k, n), BF16),
        scratch_shapes=[pltpu.VMEM((tk, tn), F32)],
        compiler_params=_params("parallel", "parallel", "arbitrary"),
    )(a, g)


def _colsum(g, *, name, tm=512):
    m, n = g.shape
    tm = _div(m, tm, 8)

    def body(g_ref, o_ref):
        @pl.when(pl.program_id(0) == 0)
        def _():
            o_ref[...] = jnp.zeros_like(o_ref)

        o_ref[...] += jnp.sum(g_ref[...].astype(F32), axis=0, keepdims=True)

    return pl.pallas_call(
        body, name=name, grid=(m // tm,),
        in_specs=[pl.BlockSpec((tm, n), lambda i: (i, 0))],
        out_specs=pl.BlockSpec((1, n), lambda i: (0, 0)),
        out_shape=jax.ShapeDtypeStruct((1, n), F32),
        compiler_params=_params("arbitrary"),
    )(g)


def _ln(u, g, b):
    mu = jnp.mean(u, axis=-1, keepdims=True)
    d = u - mu
    var = jnp.mean(d * d, axis=-1, keepdims=True)
    return d * lax.rsqrt(var + LN_EPS) * g + b


def _ln_bwd_block(dy, u, g):
    mu = jnp.mean(u, axis=-1, keepdims=True)
    dd = u - mu
    rstd = lax.rsqrt(jnp.mean(dd * dd, axis=-1, keepdims=True) + LN_EPS)
    xhat = dd * rstd
    dxh = dy * g
    m1 = jnp.mean(dxh, axis=-1, keepdims=True)
    m2 = jnp.mean(dxh * xhat, axis=-1, keepdims=True)
    return (rstd * (dxh - m1 - xhat * m2), jnp.sum(dy * xhat, axis=0, keepdims=True),
            jnp.sum(dy, axis=0, keepdims=True))


SCAN_ROWS = 512


def _tri(n, upper):
    r = lax.broadcasted_iota(jnp.int32, (n, n), 0)
    c = lax.broadcasted_iota(jnp.int32, (n, n), 1)
    return jnp.where((c >= r) if upper else (c <= r), 1.0, 0.0).astype(F32)


def _cumsum_logf(zg):
    s = zg.shape[0]
    t = _div(s, SCAN_ROWS, LANE)
    nb = s // t
    fcol = N_GATE // LANE

    def body(f_ref, c_ref, carry_ref):
        @pl.when(pl.program_id(0) == 0)
        def _():
            carry_ref[...] = jnp.zeros_like(carry_ref)

        f = f_ref[...]
        logf = jnp.minimum(f, 0.0) - jnp.log(1.0 + jnp.exp(-jnp.abs(f)))
        c = jnp.dot(_tri(t, False), logf, precision=lax.Precision.HIGHEST, preferred_element_type=F32)
        c = c + carry_ref[0:1, :]
        c_ref[...] = c
        carry_ref[...] = jnp.broadcast_to(c[t - 1:t, :], carry_ref.shape)

    return pl.pallas_call(
        body, name="cumsum_logf", grid=(nb,),
        in_specs=[pl.BlockSpec((t, LANE), lambda i: (i, fcol))],
        out_specs=pl.BlockSpec((t, LANE), lambda i: (i, 0)),
        out_shape=jax.ShapeDtypeStruct((s, LANE), F32),
        scratch_shapes=[pltpu.VMEM((8, LANE), F32)],
        compiler_params=_params("arbitrary"),
    )(zg)


def _forget_bwd(dcc, zg):
    s = zg.shape[0]
    t = _div(s, SCAN_ROWS, LANE)
    nb = s // t
    fcol = N_GATE // LANE

    def body(dc_ref, f_ref, o_ref, carry_ref):
        @pl.when(pl.program_id(0) == 0)
        def _():
            carry_ref[...] = jnp.zeros_like(carry_ref)

        lane = lax.broadcasted_iota(jnp.int32, (1, LANE), 1)
        dc = jnp.zeros((t, LANE), F32)
        for p in range(FOX_HEADS // 2):
            tile = dc_ref[:, p * LANE:(p + 1) * LANE]
            moved = pltpu.roll(tile, 2 * p, 1) if p else tile
            dc = jnp.where((lane == 2 * p) | (lane == 2 * p + 1), moved, dc)
        dlogf = jnp.dot(_tri(t, True), dc, precision=lax.Precision.HIGHEST, preferred_element_type=F32)
        dlogf = dlogf + carry_ref[0:1, :]
        o_ref[...] = (dlogf * jax.nn.sigmoid(-f_ref[...])).astype(BF16)
        carry_ref[...] = jnp.broadcast_to(dlogf[0:1, :], carry_ref.shape)

    return pl.pallas_call(
        body, name="forget_bwd", grid=(nb,),
        in_specs=[pl.BlockSpec((t, FOX_W), lambda i: (nb - 1 - i, 0)),
                  pl.BlockSpec((t, LANE), lambda i: (nb - 1 - i, fcol))],
        out_specs=pl.BlockSpec((t, LANE), lambda i: (nb - 1 - i, 0)),
        out_shape=jax.ShapeDtypeStruct((s, LANE), BF16),
        scratch_shapes=[pltpu.VMEM((8, LANE), F32)],
        compiler_params=_params("arbitrary"),
    )(dcc, zg)


KA_COL = SWA_Q // LANE
VA_COL = KA_COL + 1


def _half_masks():
    lane = lax.broadcasted_iota(jnp.int32, (1, LANE), 1)
    hi = lane >= HEAD_DIM
    return (jnp.logical_not(hi), hi)


def _both_halves(x, sel):
    xs = jnp.where(sel, x, 0.0)
    return xs + pltpu.roll(xs, HEAD_DIM, 1)


SWA_PER_KV = 4
WIDE = SWA_PER_KV * LANE


def _swa_bias():
    k = np.arange(2 * LANE)[:, None]
    q = np.arange(LANE)[None, :]
    dist = (q + LANE - k).astype(np.float32)
    valid = (dist >= 0) & (dist < LANE)
    per_head = [np.where(valid, np.float32(-s) * dist, np.float32(NEG_INF)) for s in SLOPES]
    return jnp.asarray(np.stack([np.concatenate(per_head[SWA_PER_KV * hk:SWA_PER_KV * (hk + 1)], axis=1)
                                 for hk in range(2)]), F32)


def _no_previous_block(i_blk):
    k = lax.broadcasted_iota(jnp.int32, (2 * LANE, WIDE), 0)
    return jnp.where((i_blk == 0) & (k < LANE), NEG_INF, 0.0)


def _stack_heads(ref, blk, hk, halves, scale):
    tiles = []
    for j in range(SWA_PER_KV):
        p = 2 * hk + j // 2
        t = ref[blk, p * LANE:(p + 1) * LANE]
        if scale:
            t = _scaled(t)
        tiles.append(jnp.where(halves[j % 2], t, jnp.zeros_like(t)))
    return jnp.concatenate(tiles, axis=0)


def _pair_tile(wide, pp, row_halves):
    a = wide[:, (2 * pp) * LANE:(2 * pp + 1) * LANE]
    b = wide[:, (2 * pp + 1) * LANE:(2 * pp + 2) * LANE]
    return jnp.where(row_halves[0], a, b).T


def _lane_blocks(rows8, hk):
    return jnp.concatenate([rows8[SWA_PER_KV * hk + j:SWA_PER_KV * hk + j + 1, :] for j in range(SWA_PER_KV)], axis=1)


def _row_halves():
    hi = lax.broadcasted_iota(jnp.int32, (LANE, 1), 0) >= HEAD_DIM
    return (jnp.logical_not(hi), hi)


NT = (((1,), (1,)), ((), ()))


def _scaled(q):
    return (q.astype(F32) * SCALE).astype(BF16)


SWA_GROUP = 4


def _swa_group(s_len):
    return SWA_GROUP if (s_len // LANE) % SWA_GROUP == 0 else 1


def _swa_specs(group):
    rows = group * LANE
    prev = lambda i: jnp.maximum(i * group - 1, 0)
    return [pl.BlockSpec((rows, SWA_Q), lambda i: (i, 0)),
            pl.BlockSpec((rows, LANE), lambda i: (i, KA_COL)), pl.BlockSpec((rows, LANE), lambda i: (i, VA_COL)),
            pl.BlockSpec((LANE, LANE), lambda i: (prev(i), KA_COL)),
            pl.BlockSpec((LANE, LANE), lambda i: (prev(i), VA_COL))]


def _swa_window(g, cur_ref, prev_ref):
    before = prev_ref[...] if g == 0 else cur_ref[(g - 1) * LANE:g * LANE, :]
    return jnp.concatenate([before, cur_ref[g * LANE:(g + 1) * LANE, :]], axis=0).astype(F32)


def _swa_fwd(zq, sinks):
    s_len = zq.shape[0]
    group = _swa_group(s_len)
    rows = group * LANE
    sink_lanes = jnp.repeat(sinks[:, :SWA_HEADS], LANE, axis=1)

    def body(q_ref, kc_ref, vc_ref, kp_ref, vp_ref, sink_ref, bias_ref, o_ref, lse_ref):
        halves = _half_masks()
        row_halves = _row_halves()
        for g in range(group):
            blk = slice(g * LANE, (g + 1) * LANE)
            kcat = _swa_window(g, kc_ref, kp_ref)
            vcat = _swa_window(g, vc_ref, vp_ref)
            lse_rows = []
            for hk in range(2):
                kb = _both_halves(kcat, halves[hk]).astype(BF16)
                v_t = _both_halves(vcat, halves[hk]).T.astype(BF16)
                q4 = _stack_heads(q_ref, blk, hk, halves, True)
                s_t = lax.dot_general(kb, q4, NT, preferred_element_type=F32) + bias_ref[hk]
                if g == 0:
                    s_t = s_t + _no_previous_block(pl.program_id(0))
                sink = sink_ref[:, hk * WIDE:(hk + 1) * WIDE]
                m = jnp.maximum(jnp.max(s_t, axis=0, keepdims=True), sink)
                pe = jnp.exp(s_t - m)
                den = jnp.sum(pe, axis=0, keepdims=True) + jnp.exp(sink - m)
                out_t = jnp.dot(v_t, (pe * (1.0 / den)).astype(BF16), preferred_element_type=F32)
                for pp in range(2):
                    p = 2 * hk + pp
                    o_ref[blk, p * LANE:(p + 1) * LANE] = _pair_tile(out_t, pp, row_halves).astype(BF16)
                lse4 = m + jnp.log(den)
                lse_rows += [lse4[:, j * LANE:(j + 1) * LANE] for j in range(SWA_PER_KV)]
            lse_ref[:, blk] = jnp.concatenate(lse_rows, axis=0)

    return pl.pallas_call(
        body, name="swa_fwd", grid=(s_len // rows,),
        in_specs=_swa_specs(group) + [pl.BlockSpec((1, SWA_HEADS * LANE), lambda i: (0, 0)),
                                      pl.BlockSpec((2, 2 * LANE, WIDE), lambda i: (0, 0, 0))],
        out_specs=[pl.BlockSpec((rows, SWA_Q), lambda i: (i, 0)), pl.BlockSpec((SWA_HEADS, rows), lambda i: (0, i))],
        out_shape=[jax.ShapeDtypeStruct((s_len, SWA_Q), BF16), jax.ShapeDtypeStruct((SWA_HEADS, s_len), F32)],
        compiler_params=_params("parallel"),
    )(zq, zq, zq, zq, zq, sink_lanes, _swa_bias())


def _swa_bwd(zq, sinks, o, do, lse):
    s_len = zq.shape[0]
    group = _swa_group(s_len)
    rows = group * LANE

    def body(q_ref, kc_ref, vc_ref, kp_ref, vp_ref, sink_ref, bias_ref, o_ref, do_ref, lse_ref,
             dq_ref, dk_ref, dv_ref, ds_ref):
        halves = _half_masks()
        row_halves = _row_halves()
        lane = lax.broadcasted_iota(jnp.int32, (1, LANE), 1)
        dsink = jnp.zeros((1, LANE), F32)
        for g in range(group):
            blk = slice(g * LANE, (g + 1) * LANE)
            i_blk = pl.program_id(0) * group + g
            kcat = _swa_window(g, kc_ref, kp_ref)
            vcat = _swa_window(g, vc_ref, vp_ref)
            lse_rows = lse_ref[:, blk]
            prod = do_ref[blk, :].astype(F32) * o_ref[blk, :].astype(F32)
            select = (lax.broadcasted_iota(jnp.int32, (SWA_HEADS, SWA_Q), 1) // HEAD_DIM
                      == lax.broadcasted_iota(jnp.int32, (SWA_HEADS, SWA_Q), 0))
            delta_rows = lax.dot_general(jnp.where(select, 1.0, 0.0), prod, NT, precision=lax.Precision.HIGHEST,
                                         preferred_element_type=F32)
            dk_tot = jnp.zeros((2 * LANE, LANE), F32)
            dv_tot = jnp.zeros((2 * LANE, LANE), F32)
            for hk in range(2):
                kb = _both_halves(kcat, halves[hk])
                k_t = kb.T.astype(BF16)
                kb = kb.astype(BF16)
                vb = _both_halves(vcat, halves[hk]).astype(BF16)
                q4 = _stack_heads(q_ref, blk, hk, halves, True)
                do4 = _stack_heads(do_ref, blk, hk, halves, False)
                lse4 = _lane_blocks(lse_rows, hk)
                delta4 = _lane_blocks(delta_rows, hk)
                s_t = lax.dot_general(kb, q4, NT, preferred_element_type=F32) + bias_ref[hk]
                if g == 0:
                    s_t = s_t + _no_previous_block(pl.program_id(0))
                p_t = jnp.exp(s_t - lse4)
                dp_t = lax.dot_general(vb, do4, NT, preferred_element_type=F32)
                ds_t = (p_t * (dp_t - delta4)).astype(BF16)
                sink_part = jnp.exp(sink_ref[:, hk * WIDE:(hk + 1) * WIDE] - lse4) * delta4
                for j in range(SWA_PER_KV):
                    dsink_h = -jnp.sum(sink_part[:, j * LANE:(j + 1) * LANE], axis=1, keepdims=True)
                    dsink = dsink + jnp.where(lane == SWA_PER_KV * hk + j, dsink_h, 0.0)
                dq_t = jnp.dot(k_t, ds_t, preferred_element_type=F32)
                for pp in range(2):
                    p = 2 * hk + pp
                    dq_ref[blk, p * LANE:(p + 1) * LANE] = (_pair_tile(dq_t, pp, row_halves) * SCALE).astype(BF16)
                dk_acc = jnp.dot(ds_t, q4, preferred_element_type=F32)
                dv_acc = jnp.dot(p_t.astype(BF16), do4, preferred_element_type=F32)
                dk_tot = dk_tot + jnp.where(halves[hk], dk_acc + pltpu.roll(dk_acc, HEAD_DIM, 1), 0.0)
                dv_tot = dv_tot + jnp.where(halves[hk], dv_acc + pltpu.roll(dv_acc, HEAD_DIM, 1), 0.0)
            cur = pl.ds(pl.multiple_of(i_blk * LANE, LANE), LANE)
            dk_ref[cur, :] = dk_tot[LANE:, :]
            dv_ref[cur, :] = dv_tot[LANE:, :]

            def add_previous(i_blk=i_blk, dk_tot=dk_tot, dv_tot=dv_tot):
                prv = pl.ds(pl.multiple_of((i_blk - 1) * LANE, LANE), LANE)
                dk_ref[prv, :] += dk_tot[:LANE, :]
                dv_ref[prv, :] += dv_tot[:LANE, :]

            if g == 0:
                pl.when(i_blk > 0)(add_previous)
            else:
                add_previous()

        @pl.when(pl.program_id(0) == 0)
        def _():
            ds_ref[...] = jnp.zeros_like(ds_ref)

        ds_ref[...] += dsink

    blk512 = pl.BlockSpec((rows, SWA_Q), lambda i: (i, 0))
    full = pl.BlockSpec((s_len, LANE), lambda i: (0, 0))
    vec = pl.BlockSpec((1, LANE), lambda i: (0, 0))
    return pl.pallas_call(
        body, name="swa_bwd", grid=(s_len // rows,),
        in_specs=_swa_specs(group) + [pl.BlockSpec((1, SWA_HEADS * LANE), lambda i: (0, 0)),
                                      pl.BlockSpec((2, 2 * LANE, WIDE), lambda i: (0, 0, 0)), blk512, blk512,
                                      pl.BlockSpec((SWA_HEADS, rows), lambda i: (0, i))],
        out_specs=[blk512, full, full, vec],
        out_shape=[jax.ShapeDtypeStruct((s_len, SWA_Q), BF16), jax.ShapeDtypeStruct((s_len, LANE), F32),
                   jax.ShapeDtypeStruct((s_len, LANE), F32), jax.ShapeDtypeStruct((1, LANE), F32)],
        compiler_params=_params("arbitrary"),
    )(zq, zq, zq, zq, zq, jnp.repeat(sinks[:, :SWA_HEADS], LANE, axis=1), _swa_bias(), o, do, lse)


QB_COL = (SWA_Q + 2 * SWA_KV) // LANE
KB_COL = QB_COL + FOX_W // LANE
VB_COL = KB_COL + FOX_W // LANE
N_PAIR = FOX_HEADS // 2


def _causal(t, keys_first=False):
    r = lax.broadcasted_iota(jnp.int32, (t, t), 0)
    c = lax.broadcasted_iota(jnp.int32, (t, t), 1)
    return c >= r if keys_first else r >= c


N_SPLIT = 3


def _own_half(e):
    hi = lax.broadcasted_iota(jnp.int32, (1, LANE), 1) >= HEAD_DIM
    return hi if e else jnp.logical_not(hi)


def _feature_lane(e, t):
    return HEAD_DIM * (1 - e) + t


def _fox_prep(zq, c, tm=256):
    s_len = zq.shape[0]
    tm = _div(s_len, tm, 8)

    def body(z_ref, c_ref, qx_ref, kx_ref, vx_ref):
        lane = lax.broadcasted_iota(jnp.int32, (1, LANE), 1)
        for h in range(FOX_HEADS):
            p, e = divmod(h, 2)
            own = _own_half(e)
            tile = lambda col: z_ref[:, (col + p) * LANE:(col + p + 1) * LANE].astype(F32)
            rest = c_ref[:, h:h + 1]
            qf = jnp.zeros((tm, LANE), F32)
            kf = jnp.zeros((tm, LANE), F32)
            for t in range(N_SPLIT):
                part = rest.astype(BF16).astype(F32)
                rest = rest - part
                qf = jnp.where(lane == _feature_lane(e, t), part, qf)
                qf = jnp.where(lane == _feature_lane(e, N_SPLIT + t), 1.0, qf)
                kf = jnp.where(lane == _feature_lane(e, t), 1.0, kf)
                kf = jnp.where(lane == _feature_lane(e, N_SPLIT + t), -part, kf)
            vf = jnp.where(lane == _feature_lane(e, 0), 1.0, 0.0)
            cols = slice(h * LANE, (h + 1) * LANE)
            qx_ref[:, cols] = jnp.where(own, tile(QB_COL) * SCALE, qf).astype(BF16)
            kx_ref[:, cols] = jnp.where(own, tile(KB_COL), kf).astype(BF16)
            vx_ref[:, cols] = jnp.where(own, tile(VB_COL), vf).astype(BF16)

    out = jax.ShapeDtypeStruct((s_len, FOX_HEADS * LANE), BF16)
    blk = pl.BlockSpec((tm, FOX_HEADS * LANE), lambda i: (i, 0))
    return pl.pallas_call(
        body, name="fox_prep", grid=(s_len // tm,),
        in_specs=[pl.BlockSpec((tm, N_QKV), lambda i: (i, 0)), pl.BlockSpec((tm, LANE), lambda i: (i, 0))],
        out_specs=[blk, blk, blk], out_shape=[out, out, out],
        compiler_params=_params("parallel"),
    )(zq, c)


def _comm_parts(comm):
    return ([], [], [], []) if comm is None else (comm.specs, comm.out_shape, comm.scratch, comm.arrays)


def _fox_fwd(qx, kx, vx, comm=None, t_cap=1024):
    s_len = qx.shape[0]
    t = _div(s_len, t_cap, LANE)
    nq = s_len // t
    c_specs, c_shapes, c_scratch, c_arrays = _comm_parts(comm)

    def compute(q_ref, k_ref, v_ref, o_ref, o32_ref, m_ref, l_ref):
        i = pl.program_id(1)
        qs = [q_ref[:, e * LANE:(e + 1) * LANE] for e in range(2)]

        def step(j, carry, diag):
            rows = pl.ds(pl.multiple_of(j * t, t), t)
            new = []
            for e in range(2):
                m, acc = carry[e]
                s2 = lax.dot_general(qs[e], k_ref[rows, e * LANE:(e + 1) * LANE], NT,
                                     preferred_element_type=F32) * LOG2E
                if diag:
                    s2 = jnp.where(_causal(t), s2, NEG_INF)
                mn = jnp.maximum(m, jnp.ceil(jnp.max(s2, axis=1, keepdims=True)))
                pe = jnp.exp2(s2 - mn).astype(BF16)
                acc = acc * jnp.exp2(m - mn) + jnp.dot(pe, v_ref[rows, e * LANE:(e + 1) * LANE],
                                                       preferred_element_type=F32)
                new.append((mn, acc))
            return tuple(new)

        init = (jnp.full((t, 1), NEG_INF, F32), jnp.zeros((t, LANE), F32))
        carry = lax.fori_loop(0, i, lambda j, c: step(j, c, False), (init, init))
        carry = step(i, carry, True)
        outs, ls = [], []
        for e in range(2):
            m, acc = carry[e]
            l = acc[:, _feature_lane(e, 0):_feature_lane(e, 0) + 1]
            outs.append(acc / l)
            ls.append(l)
        out = jnp.where(_own_half(1), outs[1], outs[0])
        o_ref[...] = out.astype(BF16)
        o32_ref[...] = out
        m_ref[...] = jnp.where(_own_half(1), carry[1][0], carry[0][0])
        l_ref[...] = jnp.where(_own_half(1), ls[1], ls[0])

    body = _with_comm(comm, 3, 4, lambda: (pl.program_id(0) == 0) & (pl.program_id(1) == 0),
                      lambda: (pl.program_id(0) == N_PAIR - 1) & (pl.program_id(1) == nq - 1), compute)
    pair = pl.BlockSpec((s_len, 2 * LANE), lambda p, i: (0, p))
    tile = pl.BlockSpec((t, LANE), lambda p, i: (i, p))
    wide = jax.ShapeDtypeStruct((s_len, FOX_W), F32)
    outs = pl.pallas_call(
        body, name="fox_fwd" if comm is None else "fox_fwd_comm%d" % comm.n, grid=(N_PAIR, nq),
        in_specs=[pl.BlockSpec((t, 2 * LANE), lambda p, i: (i, p)), pair, pair] + c_specs,
        out_specs=[tile, tile, tile, tile] + c_specs,
        out_shape=[jax.ShapeDtypeStruct((s_len, FOX_W), BF16), wide, wide, wide] + c_shapes,
        scratch_shapes=c_scratch,
        compiler_params=_params("arbitrary", "arbitrary"),
    )(qx, kx, vx, *c_arrays)
    return outs[0], outs[1], outs[2], outs[3], outs[4:]


def _fox_stats(o, do, m, l, tm=256):
    s_len = o.shape[0]
    tm = _div(s_len, tm, LANE)

    def body(o_ref, do_ref, m_ref, l_ref, dox_ref, st_ref):
        lane = lax.broadcasted_iota(jnp.int32, (1, LANE), 1)
        for p in range(N_PAIR):
            cols = slice(p * LANE, (p + 1) * LANE)
            dout = do_ref[:, cols]
            prod = o_ref[:, cols] * dout.astype(F32)
            shift = m_ref[:, cols]
            inv_l = 1.0 / l_ref[:, cols]
            st = jnp.zeros((tm, LANE), F32)
            for e in range(2):
                h = 2 * p + e
                dox_ref[:, h * LANE:(h + 1) * LANE] = jnp.where(_own_half(e), dout, jnp.zeros_like(dout))
                st = jnp.where(lane == e, shift[:, e * HEAD_DIM:e * HEAD_DIM + 1], st)
                delta = jnp.sum(jnp.where(_own_half(e), prod, 0.0), axis=1, keepdims=True)
                st = jnp.where(lane == 2 + e, delta, st)
                st = jnp.where(lane == 4 + e, inv_l[:, e * HEAD_DIM:e * HEAD_DIM + 1], st)
            st_ref[p] = st.T[:8, :]

    row = pl.BlockSpec((tm, FOX_W), lambda i: (i, 0))
    return pl.pallas_call(
        body, name="fox_stats", grid=(s_len // tm,), in_specs=[row, row, row, row],
        out_specs=[pl.BlockSpec((tm, FOX_HEADS * LANE), lambda i: (i, 0)),
                   pl.BlockSpec((N_PAIR, 8, tm), lambda i: (0, 0, i))],
        out_shape=[jax.ShapeDtypeStruct((s_len, FOX_HEADS * LANE), BF16),
                   jax.ShapeDtypeStruct((N_PAIR, 8, s_len), F32)],
        compiler_params=_params("parallel"),
    )(o, do, m, l)


def _fox_bwd(qx, kx, vx, dox, stats, comm=None, t_cap=1024):
    s_len = qx.shape[0]
    t = _div(s_len, t_cap, LANE)
    n = s_len // t
    c_specs, c_shapes, c_scratch, c_arrays = _comm_parts(comm)

    def compute(q_ref, do_ref, st_ref, k_ref, v_ref, dq_ref, dk_ref, dv_ref, dc_ref):
        j = pl.program_id(1)
        lane = lax.broadcasted_iota(jnp.int32, (1, LANE), 1)

        @pl.when(j == 0)
        def _():
            dq_ref[...] = jnp.zeros_like(dq_ref)

        ks = [k_ref[:, e * LANE:(e + 1) * LANE] for e in range(2)]
        vs = [v_ref[:, e * LANE:(e + 1) * LANE] for e in range(2)]
        ks_t = [k.astype(F32).T.astype(BF16) for k in ks]

        def step(i, carry, diag):
            rows = pl.ds(pl.multiple_of(i * t, t), t)
            new = []
            dq = jnp.zeros((LANE, t), F32)
            for e in range(2):
                dk, dv, dc = carry[e]
                q = q_ref[rows, e * LANE:(e + 1) * LANE]
                dout = do_ref[rows, e * LANE:(e + 1) * LANE]
                s_t = lax.dot_general(ks[e], q, NT, preferred_element_type=F32) * LOG2E
                if diag:
                    s_t = jnp.where(_causal(t, keys_first=True), s_t, NEG_INF)
                p_t = jnp.exp2(s_t - st_ref[0, e:e + 1, rows]).astype(BF16).astype(F32) * st_ref[0, 4 + e:5 + e, rows]
                dp_t = lax.dot_general(vs[e], dout, NT, preferred_element_type=F32)
                ds_f = p_t * (dp_t - st_ref[0, 2 + e:3 + e, rows])
                ds_t = ds_f.astype(BF16)
                dc = dc + jnp.sum(ds_f, axis=1, keepdims=True)
                dv = dv + jnp.dot(p_t.astype(BF16), dout, preferred_element_type=F32)
                dk = dk + jnp.dot(ds_t, q, preferred_element_type=F32)
                dq_e = jnp.dot(ks_t[e], ds_t, preferred_element_type=F32)
                dq = dq + jnp.where(_row_halves()[e], dq_e, 0.0)
                new.append((dk, dv, dc))
            dq_ref[rows, :] += dq.T * SCALE
            return tuple(new)

        zero = jnp.zeros((t, LANE), F32)
        init = (zero, zero, jnp.zeros((t, 1), F32))
        carry = step(j, (init, init), True)
        (dk0, dv0, dc0), (dk1, dv1, dc1) = lax.fori_loop(j + 1, n, lambda i, c: step(i, c, False), carry)
        dk_ref[...] = jnp.where(_own_half(1), dk1, dk0).astype(BF16)
        dv_ref[...] = jnp.where(_own_half(1), dv1, dv0).astype(BF16)
        dc_ref[...] = jnp.where(lane == 0, -dc0, jnp.where(lane == 1, -dc1, 0.0))

    body = _with_comm(comm, 5, 4, lambda: (pl.program_id(0) == 0) & (pl.program_id(1) == 0),
                      lambda: (pl.program_id(0) == N_PAIR - 1) & (pl.program_id(1) == n - 1), compute)
    pair = pl.BlockSpec((s_len, 2 * LANE), lambda p, j: (0, p))
    blk = pl.BlockSpec((t, 2 * LANE), lambda p, j: (j, p))
    tile = pl.BlockSpec((t, LANE), lambda p, j: (j, p))
    outs = pl.pallas_call(
        body, name="fox_bwd" if comm is None else "fox_bwd_comm", grid=(N_PAIR, n),
        in_specs=[pair, pair, pl.BlockSpec((1, 8, s_len), lambda p, j: (p, 0, 0)), blk, blk] + c_specs,
        out_specs=[pl.BlockSpec((s_len, LANE), lambda p, j: (0, p)), tile, tile, tile] + c_specs,
        out_shape=[jax.ShapeDtypeStruct((s_len, FOX_W), F32), jax.ShapeDtypeStruct((s_len, FOX_W), BF16),
                   jax.ShapeDtypeStruct((s_len, FOX_W), BF16), jax.ShapeDtypeStruct((s_len, FOX_W), F32)] + c_shapes,
        scratch_shapes=c_scratch,
        compiler_params=_params("arbitrary", "arbitrary"),
    )(qx, dox, stats, kx, vx, *c_arrays)
    return outs[0], outs[1], outs[2], outs[3], outs[4:]


def _mixer_out(attn_a, attn_b, zg, h, wpa, wpb, wout, g, b, tm=256):
    m = h.shape[0]
    tm = _div(m, tm, 8)

    def body(a_ref, b_ref, ga_ref, gb_ref, h_ref, wpa_ref, wpb_ref, wout_ref, g_ref, bb_ref,
             h1_ref, u_ref, mg_ref, ya_ref, yb_ref):
        ya = jnp.dot(a_ref[...], wpa_ref[...], preferred_element_type=F32)
        yb = jnp.dot(b_ref[...], wpb_ref[...], preferred_element_type=F32)
        merged = (jax.nn.sigmoid(ga_ref[...]) * ya + jax.nn.sigmoid(gb_ref[...]) * yb).astype(BF16)
        u = ALPHA * h_ref[...] + jnp.dot(merged, wout_ref[...], preferred_element_type=F32)
        u_ref[...] = u
        h1_ref[...] = _ln(u, g_ref[...], bb_ref[...])
        mg_ref[...] = merged
        ya_ref[...] = ya.astype(BF16)
        yb_ref[...] = yb.astype(BF16)

    row = pl.BlockSpec((tm, D_MODEL), lambda i: (i, 0))
    att = pl.BlockSpec((tm, SWA_Q), lambda i: (i, 0))
    vec = pl.BlockSpec((1, D_MODEL), lambda i: (0, 0))
    wsm = pl.BlockSpec((SWA_Q, D_MODEL), lambda i: (0, 0))
    return pl.pallas_call(
        body, name="mixer_out", grid=(m // tm,),
        in_specs=[att, att, row, pl.BlockSpec((tm, D_MODEL), lambda i: (i, 1)), row, wsm, wsm,
                  pl.BlockSpec((D_MODEL, D_MODEL), lambda i: (0, 0)), vec, vec],
        out_specs=[row, row, row, row, row],
        out_shape=[jax.ShapeDtypeStruct((m, D_MODEL), F32), jax.ShapeDtypeStruct((m, D_MODEL), F32),
                   jax.ShapeDtypeStruct((m, D_MODEL), BF16), jax.ShapeDtypeStruct((m, D_MODEL), BF16),
                   jax.ShapeDtypeStruct((m, D_MODEL), BF16)],
        compiler_params=_params("parallel"),
    )(attn_a, attn_b, zg, zg, h, wpa, wpb, wout, g, b)


def _mixer_bwd(dh1, u1, g, wout, ya, yb, zg, wpa, wpb, tm=256):
    m = dh1.shape[0]
    tm = _div(m, tm, 8)

    def body(dh_ref, u_ref, g_ref, wout_ref, ya_ref, yb_ref, ga_ref, gb_ref, wpa_ref, wpb_ref,
             du_ref, dg_ref, db_ref, dya_ref, dyb_ref, dga_ref, dgb_ref, da_ref, dbb_ref):
        @pl.when(pl.program_id(0) == 0)
        def _():
            dg_ref[...] = jnp.zeros_like(dg_ref)
            db_ref[...] = jnp.zeros_like(db_ref)

        du, dg, db = _ln_bwd_block(dh_ref[...], u_ref[...], g_ref[...])
        du_ref[...] = du
        dg_ref[...] += dg
        db_ref[...] += db
        dm = lax.dot_general(du.astype(BF16), wout_ref[...], (((1,), (1,)), ((), ())), preferred_element_type=F32)
        for y_ref, gate_ref, w_ref, dy_ref, dgate_ref, dattn_ref in (
                (ya_ref, ga_ref, wpa_ref, dya_ref, dga_ref, da_ref), (yb_ref, gb_ref, wpb_ref, dyb_ref, dgb_ref, dbb_ref)):
            sg = jax.nn.sigmoid(gate_ref[...])
            dy = (dm * sg).astype(BF16)
            dy_ref[...] = dy
            dgate_ref[...] = (dm * y_ref[...].astype(F32) * sg * (1.0 - sg)).astype(BF16)
            dattn_ref[...] = lax.dot_general(dy, w_ref[...], (((1,), (1,)), ((), ())),
                                             preferred_element_type=F32).astype(BF16)

    row = pl.BlockSpec((tm, D_MODEL), lambda i: (i, 0))
    att = pl.BlockSpec((tm, SWA_Q), lambda i: (i, 0))
    vec = pl.BlockSpec((1, D_MODEL), lambda i: (0, 0))
    wsm = pl.BlockSpec((SWA_Q, D_MODEL), lambda i: (0, 0))
    wide = jax.ShapeDtypeStruct((m, D_MODEL), BF16)
    narrow = jax.ShapeDtypeStruct((m, SWA_Q), BF16)
    sums = jax.ShapeDtypeStruct((1, D_MODEL), F32)
    return pl.pallas_call(
        body, name="mixer_bwd", grid=(m // tm,),
        in_specs=[row, row, vec, pl.BlockSpec((D_MODEL, D_MODEL), lambda i: (0, 0)), row, row, row,
                  pl.BlockSpec((tm, D_MODEL), lambda i: (i, 1)), wsm, wsm],
        out_specs=[row, vec, vec, row, row, row, row, att, att],
        out_shape=[jax.ShapeDtypeStruct((m, D_MODEL), F32), sums, sums, wide, wide, wide, wide, narrow, narrow],
        compiler_params=_params("arbitrary"),
    )(dh1, u1, g, wout, ya, yb, zg, zg, wpa, wpb)


def _shift_down(x, k, halo, first):
    rows = lax.broadcasted_iota(jnp.int32, (x.shape[0], 1), 0)
    y = pltpu.roll(x, k, 0)
    for r in range(k):
        fill = jnp.where(first, 0.0, halo[8 - k + r:8 - k + r + 1, :])
        y = jnp.where(rows == r, fill, y)
    return y


def _shift_up(x, k, halo, last):
    n = x.shape[0]
    rows = lax.broadcasted_iota(jnp.int32, (n, 1), 0)
    y = pltpu.roll(x, n - k, 0)
    for r in range(k):
        fill = jnp.where(last, 0.0, halo[r:r + 1, :])
        y = jnp.where(rows == n - k + r, fill, y)
    return y


def _conv_act(gate, gate_m1, gate_m2, cw, cb):
    return cb + cw[0:1, :] * gate_m2 + cw[1:2, :] * gate_m1 + cw[2:3, :] * gate


def _ffn_in_conv(h1, wfi, cw, cb, tm=256):
    s_len = h1.shape[0]
    tm = _div(s_len, tm, 8)
    hb = tm // 8

    def body(a_ref, ap_ref, w_ref, cw_ref, cb_ref, gu_ref, act_ref):
        first = pl.program_id(0) == 0
        a = a_ref[...].astype(BF16)
        before = ap_ref[...].astype(BF16)
        for c in range(N_CHUNK):
            gate = lax.dot_general(a, w_ref[c], NT, preferred_element_type=F32)
            up = lax.dot_general(a, w_ref[N_CHUNK + c], NT, preferred_element_type=F32)
            halo = lax.dot_general(before, w_ref[c], NT, preferred_element_type=F32)
            gu_ref[c, 0] = gate
            gu_ref[c, 1] = up
            conv = _conv_act(gate, _shift_down(gate, 1, halo, first), _shift_down(gate, 2, halo, first),
                             cw_ref[c], cb_ref[c])
            act_ref[c] = (conv * jax.nn.sigmoid(conv) * up).astype(BF16)

    return pl.pallas_call(
        body, name="ffn_in_conv", grid=(s_len // tm,),
        in_specs=[pl.BlockSpec((tm, D_MODEL), lambda i: (i, 0)),
                  pl.BlockSpec((8, D_MODEL), lambda i: (jnp.maximum(i * hb - 1, 0), 0)),
                  pl.BlockSpec((N_DEV, FF_CHUNK, D_MODEL), lambda i: (0, 0, 0)),
                  pl.BlockSpec((N_CHUNK, 8, FF_CHUNK), lambda i: (0, 0, 0)),
                  pl.BlockSpec((N_CHUNK, 1, FF_CHUNK), lambda i: (0, 0, 0))],
        out_specs=[pl.BlockSpec((N_CHUNK, 2, tm, FF_CHUNK), lambda i: (0, 0, i, 0)),
                   pl.BlockSpec((N_CHUNK, tm, FF_CHUNK), lambda i: (0, i, 0))],
        out_shape=[jax.ShapeDtypeStruct((N_CHUNK, 2, s_len, FF_CHUNK), F32),
                   jax.ShapeDtypeStruct((N_CHUNK, s_len, FF_CHUNK), BF16)],
        compiler_params=_params("parallel"),
    )(h1, h1, wfi, cw, cb)


def _ffn_out_ln(act, wfo, res, g, b, tm=256):
    s_len = res.shape[0]
    tm = _div(s_len, tm, 8)

    def body(a_ref, w_ref, res_ref, g_ref, b_ref, u_ref, y_ref):
        u = ALPHA * res_ref[...]
        for c in range(N_CHUNK):
            u = u + jnp.dot(a_ref[c], w_ref[c], preferred_element_type=F32)
        u_ref[...] = u
        y_ref[...] = _ln(u, g_ref[...], b_ref[...])

    row = pl.BlockSpec((tm, D_MODEL), lambda i: (i, 0))
    vec = pl.BlockSpec((1, D_MODEL), lambda i: (0, 0))
    return pl.pallas_call(
        body, name="ffn_out_ln", grid=(s_len // tm,),
        in_specs=[pl.BlockSpec((N_CHUNK, tm, FF_CHUNK), lambda i: (0, i, 0)),
                  pl.BlockSpec((N_CHUNK, FF_CHUNK, D_MODEL), lambda i: (0, 0, 0)), row, vec, vec],
        out_specs=[row, row],
        out_shape=[jax.ShapeDtypeStruct((s_len, D_MODEL), F32), jax.ShapeDtypeStruct((s_len, D_MODEL), F32)],
        compiler_params=_params("parallel"),
    )(act, wfo, res, g, b)


def _ffn_out_bwd(dh2, u2, g, wfo, tm=256):
    s_len = dh2.shape[0]
    tm = _div(s_len, tm, 8)

    def body(dh_ref, u_ref, g_ref, w_ref, du_ref, dg_ref, db_ref, o_ref):
        @pl.when(pl.program_id(0) == 0)
        def _():
            dg_ref[...] = jnp.zeros_like(dg_ref)
            db_ref[...] = jnp.zeros_like(db_ref)

        du, dg, db = _ln_bwd_block(dh_ref[...], u_ref[...], g_ref[...])
        du_ref[...] = du
        dg_ref[...] += dg
        db_ref[...] += db
        du_b = du.astype(BF16)
        for c in range(N_CHUNK):
            o_ref[c] = lax.dot_general(du_b, w_ref[c], (((1,), (1,)), ((), ())), preferred_element_type=F32)

    row = pl.BlockSpec((tm, D_MODEL), lambda i: (i, 0))
    vec = pl.BlockSpec((1, D_MODEL), lambda i: (0, 0))
    sums = jax.ShapeDtypeStruct((1, D_MODEL), F32)
    return pl.pallas_call(
        body, name="ffn_out_bwd", grid=(s_len // tm,),
        in_specs=[row, row, vec, pl.BlockSpec((N_CHUNK, FF_CHUNK, D_MODEL), lambda i: (0, 0, 0))],
        out_specs=[row, vec, vec, pl.BlockSpec((N_CHUNK, tm, FF_CHUNK), lambda i: (0, i, 0))],
        out_shape=[jax.ShapeDtypeStruct((s_len, D_MODEL), F32), sums, sums,
                   jax.ShapeDtypeStruct((N_CHUNK, s_len, FF_CHUNK), F32)],
        compiler_params=_params("arbitrary"),
    )(dh2, u2, g, wfo)


def _g_w_ffn_out(act, du, tm=2048):
    s_len = du.shape[0]
    tm = _div(s_len, tm, 8)
    steps = s_len // tm

    def body(a_ref, g_ref, o_ref, acc_ref):
        s = pl.program_id(1)

        @pl.when(s == 0)
        def _():
            acc_ref[...] = jnp.zeros_like(acc_ref)

        acc_ref[...] += lax.dot_general(a_ref[0], g_ref[...].astype(BF16), (((0,), (0,)), ((), ())),
                                        preferred_element_type=F32)

        @pl.when(s == steps - 1)
        def _():
            o_ref[0] = acc_ref[...].astype(BF16)

    return pl.pallas_call(
        body, name="g_w_ffn_out", grid=(N_CHUNK, steps),
        in_specs=[pl.BlockSpec((1, tm, FF_CHUNK), lambda c, s: (c, s, 0)),
                  pl.BlockSpec((tm, D_MODEL), lambda c, s: (s, 0))],
        out_specs=pl.BlockSpec((1, FF_CHUNK, D_MODEL), lambda c, s: (c, 0, 0)),
        out_shape=jax.ShapeDtypeStruct((N_CHUNK, FF_CHUNK, D_MODEL), BF16),
        scratch_shapes=[pltpu.VMEM((FF_CHUNK, D_MODEL), F32)],
        compiler_params=_params("parallel", "arbitrary"),
    )(act, du)


def _g_w_ffn_in(h1, dgu, tm=2048):
    s_len = h1.shape[0]
    tm = _div(s_len, tm, 8)
    steps = s_len // tm

    def body(a_ref, g_ref, o_ref, acc_ref):
        s = pl.program_id(1)

        @pl.when(s == 0)
        def _():
            acc_ref[...] = jnp.zeros_like(acc_ref)

        acc_ref[...] += lax.dot_general(g_ref[0, 0], a_ref[...].astype(BF16), (((0,), (0,)), ((), ())),
                                        preferred_element_type=F32)

        @pl.when(s == steps - 1)
        def _():
            o_ref[0] = acc_ref[...].astype(BF16)

    return pl.pallas_call(
        body, name="g_w_ffn_in", grid=(N_DEV, steps),
        in_specs=[pl.BlockSpec((tm, D_MODEL), lambda d, s: (s, 0)),
                  pl.BlockSpec((1, 1, tm, FF_CHUNK), lambda d, s: (d % N_CHUNK, d // N_CHUNK, s, 0))],
        out_specs=pl.BlockSpec((1, FF_CHUNK, D_MODEL), lambda d, s: (d, 0, 0)),
        out_shape=jax.ShapeDtypeStruct((N_DEV, FF_CHUNK, D_MODEL), BF16),
        scratch_shapes=[pltpu.VMEM((FF_CHUNK, D_MODEL), F32)],
        compiler_params=_params("parallel", "arbitrary"),
    )(h1, dgu)


def _conv_bwd_dh1(gu, dact, cw, cb, wfi, res, tm=256):
    s_len = gu.shape[2]
    tm = _div(s_len, tm, 8)
    nrow = s_len // tm
    hb = tm // 8

    def dconv_of(conv, up, da):
        sg = jax.nn.sigmoid(conv)
        return da * up * (sg * (1.0 + conv * (1.0 - sg)))

    def body(gu_ref, gp_ref, gun_ref, da_ref, dan_ref, cw_ref, cb_ref, w_ref, res_ref, dgu_ref, dcw_ref, dh_ref):
        i = pl.program_id(0)
        first = i == 0
        last = i == nrow - 1

        @pl.when(first)
        def _():
            dcw_ref[...] = jnp.zeros_like(dcw_ref)

        row = lax.broadcasted_iota(jnp.int32, (8, 1), 0)
        acc = ALPHA * res_ref[...]
        for c in range(N_CHUNK):
            cw = cw_ref[c]
            cb = cb_ref[c]
            gate = gu_ref[c, 0]
            halo = gp_ref[c, 0]
            g_m1 = _shift_down(gate, 1, halo, first)
            g_m2 = _shift_down(gate, 2, halo, first)
            conv = _conv_act(gate, g_m1, g_m2, cw, cb)
            da = da_ref[c]
            dup = (da * conv * jax.nn.sigmoid(conv)).astype(BF16)
            dconv = dconv_of(conv, gu_ref[c, 1], da)
            gate_n = gun_ref[c, 0]
            tail = gate[tm - 8:, :]
            conv_n = _conv_act(gate_n, _shift_down(gate_n, 1, tail, False), _shift_down(gate_n, 2, tail, False),
                               cw, cb)
            dconv_n = dconv_of(conv_n, gun_ref[c, 1], dan_ref[c])
            dgate = (cw[2:3, :] * dconv + cw[1:2, :] * _shift_up(dconv, 1, dconv_n, last)
                     + cw[0:1, :] * _shift_up(dconv, 2, dconv_n, last)).astype(BF16)
            dgu_ref[c, 0] = dgate
            dgu_ref[c, 1] = dup
            acc = acc + jnp.dot(dgate, w_ref[c], preferred_element_type=F32)
            acc = acc + jnp.dot(dup, w_ref[N_CHUNK + c], preferred_element_type=F32)
            part = jnp.zeros((8, FF_CHUNK), F32)
            for r, term in enumerate((dconv * g_m2, dconv * g_m1, dconv * gate, dconv)):
                part = jnp.where(row == r, jnp.sum(term, axis=0, keepdims=True), part)
            dcw_ref[c] += part
        dh_ref[...] = acc

    nxt = lambda i: jnp.minimum((i + 1) * hb, s_len // 8 - 1)
    main = pl.BlockSpec((N_CHUNK, 2, tm, FF_CHUNK), lambda i: (0, 0, i, 0))
    row_d = pl.BlockSpec((tm, D_MODEL), lambda i: (i, 0))
    return pl.pallas_call(
        body, name="conv_bwd_dh1", grid=(nrow,),
        in_specs=[main,
                  pl.BlockSpec((N_CHUNK, 1, 8, FF_CHUNK), lambda i: (0, 0, jnp.maximum(i * hb - 1, 0), 0)),
                  pl.BlockSpec((N_CHUNK, 2, 8, FF_CHUNK), lambda i: (0, 0, nxt(i), 0)),
                  pl.BlockSpec((N_CHUNK, tm, FF_CHUNK), lambda i: (0, i, 0)),
                  pl.BlockSpec((N_CHUNK, 8, FF_CHUNK), lambda i: (0, nxt(i), 0)),
                  pl.BlockSpec((N_CHUNK, 8, FF_CHUNK), lambda i: (0, 0, 0)),
                  pl.BlockSpec((N_CHUNK, 1, FF_CHUNK), lambda i: (0, 0, 0)),
                  pl.BlockSpec((N_DEV, FF_CHUNK, D_MODEL), lambda i: (0, 0, 0)), row_d],
        out_specs=[main, pl.BlockSpec((N_CHUNK, 8, FF_CHUNK), lambda i: (0, 0, 0)), row_d],
        out_shape=[jax.ShapeDtypeStruct((N_CHUNK, 2, s_len, FF_CHUNK), BF16),
                   jax.ShapeDtypeStruct((N_CHUNK, 8, FF_CHUNK), F32),
                   jax.ShapeDtypeStruct((s_len, D_MODEL), F32)],
        compiler_params=_params("arbitrary"),
    )(gu, gu, gu, dact, dact, cw, cb, wfi, res)


def _loss_head(y, target, tm=256):
    m, d = y.shape
    tm = _div(m, tm, 8)

    def body(y_ref, t_ref, dy_ref, loss_ref):
        @pl.when(pl.program_id(0) == 0)
        def _():
            loss_ref[...] = jnp.zeros_like(loss_ref)

        err = y_ref[...] - t_ref[...]
        dy_ref[...] = err / d
        loss_ref[...] += 0.5 * jnp.sum(jnp.sum(err * err, axis=1, keepdims=True) / d, axis=0, keepdims=True)

    row = pl.BlockSpec((tm, d), lambda i: (i, 0))
    return pl.pallas_call(
        body, name="loss_head", grid=(m // tm,), in_specs=[row, row],
        out_specs=[row, pl.BlockSpec((8, LANE), lambda i: (0, 0))],
        out_shape=[jax.ShapeDtypeStruct((m, d), F32), jax.ShapeDtypeStruct((8, LANE), F32)],
        compiler_params=_params("arbitrary"),
    )(y, target)


def _sum_devices(r_ref):
    acc = r_ref[0].astype(F32)
    for d in range(1, N_DEV):
        acc = acc + r_ref[d].astype(F32)
    return acc


def _sum8(recv):
    rows = recv.shape[1]
    tr = _div(rows, ROW_BLOCK, 8)

    def body(r_ref, o_ref):
        o_ref[...] = _sum_devices(r_ref)

    return pl.pallas_call(
        body, name="sum8", grid=(rows // tr,),
        in_specs=[pl.BlockSpec((N_DEV, tr, LANE), lambda i: (0, i, 0))],
        out_specs=pl.BlockSpec((tr, LANE), lambda i: (i, 0)),
        out_shape=jax.ShapeDtypeStruct((rows, LANE), F32),
        compiler_params=_params("parallel"),
    )(recv)


def _adamw_math(w, g, m, v):
    m = ADAM_B1 * m + (1.0 - ADAM_B1) * g
    v = ADAM_B2 * v + (1.0 - ADAM_B2) * (g * g)
    m_hat = m / (1.0 - ADAM_B1 ** ADAM_STEP)
    v_hat = v / (1.0 - ADAM_B2 ** ADAM_STEP)
    return -ADAM_LR * (m_hat / (jnp.sqrt(v_hat) + ADAM_EPS) + ADAM_WD * w), m, v


def _adamw_rows(w, g, m, v, name):
    rows = w.shape[0]
    tr = _div(rows, ROW_BLOCK, 8)

    def body(w_ref, g_ref, m_ref, v_ref, d_ref, mo_ref, vo_ref):
        d_ref[...], mo_ref[...], vo_ref[...] = _adamw_math(w_ref[...], g_ref[...], m_ref[...], v_ref[...])

    blk = pl.BlockSpec((tr, LANE), lambda i: (i, 0))
    out = jax.ShapeDtypeStruct((rows, LANE), F32)
    return pl.pallas_call(
        body, name=name, grid=(rows // tr,), in_specs=[blk, blk, blk, blk], out_specs=[blk, blk, blk],
        out_shape=[out, out, out], compiler_params=_params("parallel"),
    )(w, g, m, v)


def _adamw_w_in(recv, w, m, v, tl=128):
    n, depth, d = w.shape

    def body(*refs):
        r_refs, (w_ref, m_ref, v_ref), (g_ref, d_ref, mo_ref, vo_ref) = refs[:depth], refs[depth:depth + 3], refs[-4:]
        for l in range(depth):
            g = _sum_devices(r_refs[l])
            g_ref[:, l, :] = g
            d_ref[:, l, :], mo_ref[:, l, :], vo_ref[:, l, :] = _adamw_math(w_ref[:, l, :], g, m_ref[:, l, :],
                                                                            v_ref[:, l, :])

    blk = pl.BlockSpec((n, depth, tl), lambda j: (0, 0, j))
    out = jax.ShapeDtypeStruct((n, depth, d), F32)
    return pl.pallas_call(
        body, name="adamw_w_in", grid=(d // tl,),
        in_specs=[pl.BlockSpec((N_DEV, n, tl), lambda j: (0, 0, j))] * depth + [blk, blk, blk],
        out_specs=[blk, blk, blk, blk], out_shape=[out, out, out, out],
        compiler_params=_params("parallel"),
    )(*recv, w, m, v)


def _adamw_shard(recv, w, m, v, layer, prev, name):
    _, k, n = recv.shape
    tk = _div(k, 128, 16)

    def body(r_ref, w_ref, m_ref, v_ref, *rest):
        g_ref, d_ref, mo_ref, vo_ref = rest[-4:]
        g = _sum_devices(r_ref)
        g_ref[0] = g
        d_ref[0], mo_ref[0], vo_ref[0] = _adamw_math(w_ref[0], g, m_ref[0], v_ref[0])

    blk = pl.BlockSpec((1, tk, n), lambda i: (layer, i, 0))
    out = jax.ShapeDtypeStruct((DEPTH, k, n), F32)
    carried = [] if prev is None else list(prev)
    return pl.pallas_call(
        body, name=name, grid=(k // tk,),
        in_specs=[pl.BlockSpec((N_DEV, tk, n), lambda i: (0, i, 0)), blk, blk, blk]
        + [pl.BlockSpec(memory_space=pl.ANY)] * len(carried),
        out_specs=[blk, blk, blk, blk], out_shape=[out, out, out, out],
        input_output_aliases={4 + j: j for j in range(len(carried))},
        compiler_params=_params("parallel"),
    )(recv, w, m, v, *carried)


def _to_rows(flat, rows):
    flat = flat.reshape(-1)
    return jnp.pad(flat, (0, rows * LANE - flat.shape[0])).reshape(rows, LANE)


def _pad_z(a, axis):
    f0 = N_QKV
    g0 = N_QKV + FOX_HEADS
    take = lambda lo, hi: lax.slice_in_dim(a, lo, hi, axis=axis)
    shape = list(a.shape)
    shape[axis] = F_PAD - FOX_HEADS
    return jnp.concatenate([take(0, f0), take(g0, N_IN), take(f0, g0), jnp.zeros(shape, a.dtype)], axis=axis)


def _unpad_z(a, axis):
    f0 = N_QKV + N_GATE
    take = lambda lo, hi: lax.slice_in_dim(a, lo, hi, axis=axis)
    return jnp.concatenate([take(0, N_QKV), take(f0, f0 + FOX_HEADS), take(N_QKV, f0)], axis=axis)


def _shards_to_cols(g):
    _, k, n = g.shape
    return g.transpose(1, 0, 2).reshape(k, N_DEV * n)


def _cols_to_shards(full):
    k, n = full.shape
    return full.reshape(k, N_DEV, n // N_DEV).transpose(1, 0, 2)


def _layer_fwd(h, w, p, comm=None, late=None):
    zq, zg = _z_proj(h, w["w_in_p"], p["b_in_p"])
    qx, kx, vx = _fox_prep(zq, _cumsum_logf(zg))
    attn_a, lse_a = _swa_fwd(zq, p["sinks"])
    attn_b, attn_b32, m_b, l_b, arrived = _fox_fwd(qx, kx, vx, comm)
    if late is not None:
        w, p = late(w, p, arrived)
    h1, u1, merged, ya, yb = _mixer_out(attn_a, attn_b, zg, h, w["w_proj_a"], w["w_proj_b"], w["w_out"],
                                        p["ln_mix_g"], p["ln_mix_b"])
    gu, act = _ffn_in_conv(h1, w["w_ffn_in"], p["conv_w"], p["conv_b"])
    u2, h2 = _ffn_out_ln(act, w["w_ffn_out"], h1, p["ln_ffn_g"], p["ln_ffn_b"])
    saved = dict(h=h, zq=zq, zg=zg, qx=qx, kx=kx, vx=vx, attn_a=attn_a, lse_a=lse_a, attn_b=attn_b,
                 attn_b32=attn_b32, m_b=m_b, l_b=l_b, h1=h1, u1=u1, merged=merged, ya=ya, yb=yb, gu=gu, act=act, u2=u2)
    return h2, saved, w, p


def _layer_bwd(dh2, sv, w, p, make_comm=None, make_last_comm=None):
    s_len = dh2.shape[0]
    du2, d_ffn_g, d_ffn_b, dact = _ffn_out_bwd(dh2, sv["u2"], p["ln_ffn_g"], w["w_ffn_out"])
    g_ffn_out = _g_w_ffn_out(sv["act"], du2)
    dgu, dcw, dh1 = _conv_bwd_dh1(sv["gu"], dact, p["conv_w"], p["conv_b"], w["w_ffn_in"], du2)
    dcw = dcw.transpose(1, 0, 2).reshape(8, D_FF)
    g_ffn_in = _g_w_ffn_in(sv["h1"], dgu)
    du1, d_mix_g, d_mix_b, dya, dyb, dga, dgb, dattn_a, dattn_b = _mixer_bwd(
        dh1, sv["u1"], p["ln_mix_g"], w["w_out"], sv["ya"], sv["yb"], sv["zg"], w["w_proj_a"], w["w_proj_b"])
    g_out = _linear_tn(sv["merged"], du1, name="g_w_out", tn=1024)
    g_proj_a = _linear_tn(sv["attn_a"], dya, name="g_w_proj_a", tk=512, tn=1024)
    g_proj_b = _linear_tn(sv["attn_b"], dyb, name="g_w_proj_b", tk=512, tn=1024)
    dq_a, dk_a, dv_a, dsinks = _swa_bwd(sv["zq"], p["sinks"], sv["attn_a"], dattn_a, sv["lse_a"])
    big = dict(w_proj_a=_cols_to_shards(g_proj_a), w_proj_b=_cols_to_shards(g_proj_b),
               w_out=g_out.reshape(N_DEV, D_MODEL // N_DEV, D_MODEL), w_ffn_in=g_ffn_in,
               w_ffn_out=g_ffn_out.reshape(N_DEV, D_FF // N_DEV, D_MODEL))
    dox, stats = _fox_stats(sv["attn_b32"], dattn_b, sv["m_b"], sv["l_b"])
    dq_b, dk_b, dv_b, dcc, arrived = _fox_bwd(sv["qx"], sv["kx"], sv["vx"], dox, stats,
                                              None if make_comm is None else make_comm(big))
    df = _forget_bwd(dcc, sv["zg"])
    dz = jnp.concatenate([dq_a, dk_a.astype(BF16), dv_a.astype(BF16), dq_b.astype(BF16), dk_b, dv_b, dga, dgb, df,
                          jnp.zeros((s_len, F_PAD - LANE), BF16)], axis=1)
    g_in_t = _unpad_z(_linear_tn(dz, sv["h"], name="g_w_in", tk=768, tn=1024), 0)
    g_b_in = _unpad_z(_colsum(dz, name="g_b_in"), 1)
    big["w_in"] = g_in_t.reshape(N_DEV, N_IN // N_DEV, D_MODEL)
    small = dict(ln_mix_g=d_mix_g, ln_mix_b=d_mix_b, b_in=g_b_in, attn_sinks=dsinks[:, :SWA_HEADS],
                 ln_ffn_g=d_ffn_g, ln_ffn_b=d_ffn_b, conv_w=dcw[:3], conv_b=dcw[3:4])
    dh, arrived_last = _d_h(dz, w["w_in_p"], du1, None if make_last_comm is None else make_last_comm(big, small))
    return dh, big, small, arrived, arrived_last


def _w_in_layouts(w_in):
    return dict(w_in_p=_pad_z(w_in.reshape(N_IN, D_MODEL), 0))


def _other_layouts(w_proj_a, w_proj_b, w_out, w_ffn_in, w_ffn_out):
    return dict(w_proj_a=_shards_to_cols(w_proj_a), w_proj_b=_shards_to_cols(w_proj_b),
                w_out=w_out.reshape(D_MODEL, D_MODEL), w_ffn_in=w_ffn_in,
                w_ffn_out=w_ffn_out.reshape(N_CHUNK, FF_CHUNK, D_MODEL))


def _layer_params(r):
    return dict(
        b_in_p=_pad_z(r["b_in"].reshape(1, N_IN), 1),
        sinks=jnp.pad(r["attn_sinks"].reshape(1, SWA_HEADS), ((0, 0), (0, LANE - SWA_HEADS))),
        ln_mix_g=r["ln_mix_g"].reshape(1, D_MODEL), ln_mix_b=r["ln_mix_b"].reshape(1, D_MODEL),
        ln_ffn_g=r["ln_ffn_g"].reshape(1, D_MODEL), ln_ffn_b=r["ln_ffn_b"].reshape(1, D_MODEL),
        conv_b=r["conv_b"].reshape(N_CHUNK, 1, FF_CHUNK))


def _conv_w_layout(conv_w):
    return jnp.pad(conv_w, ((0, 5), (0, 0))).reshape(8, N_CHUNK, FF_CHUNK).transpose(1, 0, 2)


def kernel(x, ln_mix_g, ln_mix_b, w_in, b_in, attn_sinks, w_proj_a, w_proj_b, w_out, ln_ffn_g, ln_ffn_b, w_ffn_in, conv_w, conv_b, w_ffn_out, loss_target, m_ln_mix_g, m_ln_mix_b, m_w_in, m_b_in, m_attn_sinks, m_w_proj_a, m_w_proj_b, m_w_out, m_ln_ffn_g, m_ln_ffn_b, m_w_ffn_in, m_conv_w, m_conv_b, m_w_ffn_out, v_ln_mix_g, v_ln_mix_b, v_w_in, v_b_in, v_attn_sinks, v_w_proj_a, v_w_proj_b, v_w_out, v_ln_ffn_g, v_ln_ffn_b, v_w_ffn_in, v_conv_w, v_conv_b, v_w_ffn_out):
    wts = dict(ln_mix_g=ln_mix_g, ln_mix_b=ln_mix_b, w_in=w_in, b_in=b_in, attn_sinks=attn_sinks, w_proj_a=w_proj_a,
               w_proj_b=w_proj_b, w_out=w_out, ln_ffn_g=ln_ffn_g, ln_ffn_b=ln_ffn_b, w_ffn_in=w_ffn_in,
               conv_w=conv_w, conv_b=conv_b, w_ffn_out=w_ffn_out)
    mom = dict(ln_mix_g=m_ln_mix_g, ln_mix_b=m_ln_mix_b, w_in=m_w_in, b_in=m_b_in, attn_sinks=m_attn_sinks,
               w_proj_a=m_w_proj_a, w_proj_b=m_w_proj_b, w_out=m_w_out, ln_ffn_g=m_ln_ffn_g, ln_ffn_b=m_ln_ffn_b,
               w_ffn_in=m_w_ffn_in, conv_w=m_conv_w, conv_b=m_conv_b, w_ffn_out=m_w_ffn_out)
    vel = dict(ln_mix_g=v_ln_mix_g, ln_mix_b=v_ln_mix_b, w_in=v_w_in, b_in=v_b_in, attn_sinks=v_attn_sinks,
               w_proj_a=v_w_proj_a, w_proj_b=v_w_proj_b, w_out=v_w_out, ln_ffn_g=v_ln_ffn_g, ln_ffn_b=v_ln_ffn_b,
               w_ffn_in=v_w_ffn_in, conv_w=v_conv_w, conv_b=v_conv_b, w_ffn_out=v_w_ffn_out)
    names = list(wts)
    big_names = [n for n, _, _ in BIG]
    small_names = [n for n, _ in SMALL]
    me = 4 * lax.axis_index("x") + 2 * lax.axis_index("y") + lax.axis_index("c")
    cw_shard = D_FF // N_DEV

    stored = {"w_in": ((2, 0, 1), (1, 2, 0)), "w_ffn_in": ((0, 2, 1), (0, 2, 1))}
    as_stored = lambda tree: {n: jnp.transpose(tree[n], stored[n][0]) if n in stored else tree[n] for n in big_names}
    w_st, m_st, v_st = as_stored(wts), as_stored(mom), as_stored(vel)
    wb = {n: [(w_st[n][:, l] if n == "w_in" else w_st[n][l]).astype(BF16) for l in range(DEPTH)] for n in big_names}
    ps = [_layer_params(dict(b_in=b_in[l], attn_sinks=attn_sinks[l], ln_mix_g=ln_mix_g[l], ln_mix_b=ln_mix_b[l],
                             ln_ffn_g=ln_ffn_g[l], ln_ffn_b=ln_ffn_b[l], conv_b=conv_b[l])) for l in range(DEPTH)]
    w_in_0, = _exchange([(wb["w_in"][0], True)], "gather_w_in_0")
    others = big_names[1:]
    next_layer = {}

    def late_0(w, p, arrived):
        conv_full = arrived[-1].transpose(1, 2, 0, 3).reshape(DEPTH, 3, D_FF)
        next_layer["w"] = _w_in_layouts(arrived[len(others)])
        next_layer["p"] = dict(ps[1], conv_w=_conv_w_layout(conv_full[1]))
        return dict(w, **_other_layouts(*arrived[:len(others)])), dict(p, conv_w=_conv_w_layout(conv_full[0]))

    def late_1(w, p, arrived):
        return dict(w, **_other_layouts(*arrived)), p

    saved, ws = [None] * DEPTH, [None] * DEPTH
    gather_0 = _Comm([(wb[n][0], True) for n in others] + [(wb["w_in"][1], True), (conv_w, True)])
    h, saved[0], ws[0], ps[0] = _layer_fwd(x[0], _w_in_layouts(w_in_0), ps[0], gather_0, late_0)
    gather_1 = _Comm([(wb[n][1], True) for n in others])
    h, saved[1], ws[1], ps[1] = _layer_fwd(h, next_layer["w"], next_layer["p"], gather_1, late_1)
    dh, loss_part = _loss_head(h, loss_target[0])

    def small_rows(small):
        vec = jnp.concatenate([small[n].reshape(-1) for n in small_names] + [loss_part[0, 0].reshape(1)])
        return _to_rows(vec, SMALL_LAYER_ROWS)

    dh, big_1, small_1, _, _ = _layer_bwd(dh, saved[1], ws[1], ps[1])

    def exchange_early(big_0):
        return _Comm([(big_1[n].astype(BF16), False) for n in big_names] + [(small_rows(small_1), True)]
                     + [(big_0[n].astype(BF16), False) for n in others])

    def exchange_last(big_0, small_0):
        return _Comm([(big_0["w_in"].astype(BF16), False), (small_rows(small_0), True)])

    grad_x, _, _, arrived, (g_in_0, g_small_0) = _layer_bwd(dh, saved[0], ws[0], ps[0], exchange_early, exchange_last)
    n_big = len(big_names)
    recv = [[g_in_0] + list(arrived[n_big + 1:]) + [g_small_0], list(arrived[:n_big + 1])]

    big_out = {"w_in": _adamw_w_in([recv[l][0] for l in range(DEPTH)], w_st["w_in"], m_st["w_in"], v_st["w_in"])}
    for t, n in enumerate(big_names):
        if n == "w_in":
            continue
        outs = None
        for l in reversed(range(DEPTH)):
            outs = _adamw_shard(recv[l][t], w_st[n], m_st[n], v_st[n], l, outs, "adamw_%s_%d" % (n, l))
        big_out[n] = outs
    for n, (_, back) in stored.items():
        big_out[n] = [jnp.transpose(a, back) for a in big_out[n]]
    small_sum = [_sum8(recv[l][-1]).reshape(-1) for l in range(DEPTH)]
    g_small = {}
    off = 0
    for n, size in SMALL:
        g_small[n] = jnp.stack([small_sum[l][off:off + size] for l in range(DEPTH)])
        off += size
    loss = small_sum[0][off]
    g_small["conv_w"] = lax.dynamic_slice_in_dim(g_small["conv_w"].reshape(DEPTH, 3, D_FF), me * cw_shard, cw_shard,
                                                 axis=2)
    g_small = {n: g_small[n].reshape(wts[n].shape) for n in small_names}

    def pack_small(tree):
        return _to_rows(jnp.concatenate([tree[n].reshape(-1) for n in small_names]), SMALL_ROWS)

    small_out = (pack_small(g_small),) + tuple(_adamw_rows(pack_small(wts), pack_small(g_small), pack_small(mom),
                                                           pack_small(vel), "adamw_small"))

    def result(j):
        out = {n: big_out[n][j] for n in big_names}
        flat = small_out[j].reshape(-1)
        off = 0
        for n in small_names:
            out[n] = flat[off:off + wts[n].size].reshape(wts[n].shape)
            off += wts[n].size
        return [out[n] for n in names]

    return (loss, grad_x[None], *result(0), *result(1), *result(2), *result(3))
```

```python
import functools

import jax
import jax.numpy as jnp
import numpy as np
from jax import lax
from jax.experimental import pallas as pl
from jax.experimental.pallas import tpu as pltpu

F32 = jnp.float32
BF16 = jnp.bfloat16
MESH = pl.DeviceIdType.MESH

N_DEV = 8
DEPTH = 2
D_MODEL = 1024
HEAD_DIM = 64
SWA_Q = 512
SWA_KV = 128
FOX_W = 512
FOX_HEADS = 8
SWA_HEADS = 8
D_FF = 2816
N_IN = 4360
N_QKV = SWA_Q + 2 * SWA_KV + 3 * FOX_W
N_GATE = 2 * D_MODEL
F_PAD = 256
N_ZG = N_GATE + F_PAD
N_ZP = N_QKV + N_ZG
LN_EPS = 1e-5
NEG_INF = -1e30
ALPHA = (2 * DEPTH) ** 0.25
SCALE = HEAD_DIM ** -0.5
LOG2E = 1.4426950408889634
SLOPES = tuple(2.0 ** (-8.0 * (h + 1) / SWA_HEADS) for h in range(SWA_HEADS))

ADAM_LR = 0.001
ADAM_B1 = 0.9
ADAM_B2 = 0.999
ADAM_EPS = 1e-08
ADAM_WD = 0.01
ADAM_STEP = 10

LANE = 128
VMEM_LIMIT = 56 * 1024 * 1024

BIG = (("w_in", (D_MODEL, N_IN), 1), ("w_proj_a", (SWA_Q, D_MODEL), 1), ("w_proj_b", (FOX_W, D_MODEL), 1),
       ("w_out", (D_MODEL, D_MODEL), 0), ("w_ffn_in", (D_MODEL, 2 * D_FF), 1), ("w_ffn_out", (D_FF, D_MODEL), 0))
SMALL = (("ln_mix_g", D_MODEL), ("ln_mix_b", D_MODEL), ("b_in", N_IN), ("attn_sinks", SWA_HEADS),
         ("ln_ffn_g", D_MODEL), ("ln_ffn_b", D_MODEL), ("conv_w", 3 * D_FF), ("conv_b", D_FF))
ROW_BLOCK = 512
SMALL_LAYER_ROWS = -(-(sum(n for _, n in SMALL) + 1) // (8 * LANE)) * 8
SMALL_ROWS = ROW_BLOCK
FF_CHUNK = 2 * D_FF // N_DEV
N_CHUNK = D_FF // FF_CHUNK


def _div(n, cap, unit):
    if n <= cap:
        return n
    best = None
    for t in range(unit, cap + 1, unit):
        if n % t == 0:
            best = t
    assert best is not None, (n, cap, unit)
    return best


def _params(*sem):
    return pltpu.CompilerParams(dimension_semantics=sem, vmem_limit_bytes=VMEM_LIMIT)


def _peer(r):
    x, y, c = lax.axis_index("x"), lax.axis_index("y"), lax.axis_index("c")
    px = 1 - x if (r >> 2) & 1 else x
    py = 1 - y if (r >> 1) & 1 else y
    pc = 1 - c if r & 1 else c
    return (px, py, pc), 4 * px + 2 * py + pc


class _Comm:
    def __init__(self, tensors):
        self.arrays = [x for x, _ in tensors]
        self.gathers = [g for _, g in tensors]
        self.n = len(tensors)
        self.out_shape = [jax.ShapeDtypeStruct((N_DEV,) + (x.shape if g else x.shape[1:]), x.dtype)
                          for x, g in tensors]
        self.specs = [pl.BlockSpec(memory_space=pl.ANY)] * self.n
        self.scratch = [pltpu.SemaphoreType.DMA((N_DEV - 1, self.n)), pltpu.SemaphoreType.DMA((N_DEV - 1, self.n)),
                        pltpu.SemaphoreType.DMA((self.n,))]

    def _copies(self, x_refs, out_refs, sems):
        send_sems, recv_sems, local_sems = sems
        _, me = _peer(0)

        def src(t, idx):
            return x_refs[t] if self.gathers[t] else x_refs[t].at[idx]

        def remote(r, t, mine):
            peer, pid = _peer(r)
            return pltpu.make_async_remote_copy(src_ref=src(t, pid), dst_ref=out_refs[t].at[me if mine else pid],
                                                send_sem=send_sems.at[r - 1, t], recv_sem=recv_sems.at[r - 1, t],
                                                device_id=peer, device_id_type=MESH)

        pairs = [(r, t) for r in range(1, N_DEV) for t in range(self.n)]
        local = [pltpu.make_async_copy(src(t, me), out_refs[t].at[me], local_sems.at[t]) for t in range(self.n)]
        return local, [remote(r, t, True) for r, t in pairs], lambda: [remote(r, t, False) for r, t in pairs]

    def start(self, x_refs, out_refs, sems):
        local, sent, _ = self._copies(x_refs, out_refs, sems)
        for cp in local + sent:
            cp.start()

    def wait(self, x_refs, out_refs, sems):
        local, sent, landing = self._copies(x_refs, out_refs, sems)
        for cp in landing():
            cp.wait_recv()
        for cp in sent:
            cp.wait_send()
        for cp in local:
            cp.wait()


def _exchange(tensors, name):
    comm = _Comm(tensors)
    n = comm.n

    def body(*refs):
        comm.start(refs[:n], refs[n:2 * n], refs[2 * n:])
        comm.wait(refs[:n], refs[n:2 * n], refs[2 * n:])

    return pl.pallas_call(body, name=name, out_shape=comm.out_shape, in_specs=comm.specs, out_specs=comm.specs,
                          scratch_shapes=comm.scratch)(*comm.arrays)


def _with_comm(comm, n_in, n_out, first, last, compute):
    nc = comm.n if comm is not None else 0

    def body(*refs):
        ins, x_refs = refs[:n_in], refs[n_in:n_in + nc]
        outs = refs[n_in + nc:n_in + nc + n_out]
        out_refs = refs[n_in + nc + n_out:n_in + 2 * nc + n_out]
        sems = refs[n_in + 2 * nc + n_out:]
        if nc:
            @pl.when(first())
            def _():
                comm.start(x_refs, out_refs, sems)

        compute(*ins, *outs)
        if nc:
            @pl.when(last())
            def _():
                comm.wait(x_refs, out_refs, sems)

    return body


def _d_h(dz, w_in_t, res, comm=None, tm=512):
    m, k = dz.shape
    d = w_in_t.shape[1]
    tm = _div(m, tm, 8)
    steps = m // tm
    c_specs, c_shapes, c_scratch, c_arrays = _comm_parts(comm)

    def compute(dz_ref, w_ref, res_ref, o_ref):
        o_ref[...] = ALPHA * res_ref[...] + jnp.dot(dz_ref[...], w_ref[...], preferred_element_type=F32)

    body = _with_comm(comm, 3, 1, lambda: pl.program_id(0) == 0, lambda: pl.program_id(0) == steps - 1, compute)
    row = pl.BlockSpec((tm, d), lambda i: (i, 0))
    outs = pl.pallas_call(
        body, name="d_h" if comm is None else "d_h_comm", grid=(steps,),
        in_specs=[pl.BlockSpec((tm, k), lambda i: (i, 0)), pl.BlockSpec((k, d), lambda i: (0, 0)), row] + c_specs,
        out_specs=[row] + c_specs, out_shape=[jax.ShapeDtypeStruct((m, d), F32)] + c_shapes,
        scratch_shapes=c_scratch,
        compiler_params=_params("arbitrary"),
    )(dz, w_in_t, res, *c_arrays)
    return outs[0], outs[1:]


def _z_proj(h, w_in_t, b_p, tm=512):
    m, k = h.shape
    tm = _div(m, tm, 8)
    nt = (((1,), (1,)), ((), ()))

    def body(h_ref, w_ref, b_ref, zq_ref, zg_ref):
        a = h_ref[...].astype(BF16)
        zq = lax.dot_general(a, w_ref[:N_QKV, :], nt, preferred_element_type=F32)
        zq_ref[...] = (zq + b_ref[:, :N_QKV]).astype(BF16)
        zg_ref[...] = lax.dot_general(a, w_ref[N_QKV:, :], nt, preferred_element_type=F32) + b_ref[:, N_QKV:]

    return pl.pallas_call(
        body, name="z_proj", grid=(m // tm,),
        in_specs=[pl.BlockSpec((tm, k), lambda i: (i, 0)), pl.BlockSpec((N_ZP, k), lambda i: (0, 0)),
                  pl.BlockSpec((1, N_ZP), lambda i: (0, 0))],
        out_specs=[pl.BlockSpec((tm, N_QKV), lambda i: (i, 0)), pl.BlockSpec((tm, N_ZG), lambda i: (i, 0))],
        out_shape=[jax.ShapeDtypeStruct((m, N_QKV), BF16), jax.ShapeDtypeStruct((m, N_ZG), F32)],
        compiler_params=_params("parallel"),
    )(h, w_in_t, b_p)


def _linear_tn(a, g, *, name, tk=1024, tn=640, tm=2048, colsum=False):
    m, k = a.shape
    n = g.shape[1]
    tk = _div(k, tk, LANE)
    tn = _div(n, tn, LANE)
    tm = _div(m, tm, 8)
    steps = m // tm
    assert not colsum or tn == n

    def body(a_ref, g_ref, o_ref, *rest):
        acc_ref = rest[-1]
        s = pl.program_id(2)

        @pl.when(s == 0)
        def _():
            acc_ref[...] = jnp.zeros_like(acc_ref)
            if colsum:
                rest[0][...] = jnp.zeros_like(rest[0])

        a_blk = a_ref[...]
        acc_ref[...] += lax.dot_general(a_blk.astype(BF16), g_ref[...].astype(BF16), (((0,), (0,)), ((), ())),
                                        preferred_element_type=F32)
        if colsum:
            rest[0][...] += jnp.sum(a_blk.astype(F32), axis=0, keepdims=True)

        @pl.when(s == steps - 1)
        def _():
            o_ref[...] = acc_ref[...].astype(BF16)

    out_specs = [pl.BlockSpec((tk, tn), lambda i, j, s: (i, j))]
    out_shape = [jax.ShapeDtypeStruct((k, n), BF16)]
    if colsum:
        out_specs.append(pl.BlockSpec((1, tk), lambda i, j, s: (0, i)))
        out_shape.append(jax.ShapeDtypeStruct((1, k), F32))
    outs = pl.pallas_call(
        body, name=name, grid=(k // tk, n // tn, steps),
        in_specs=[pl.BlockSpec((tm, tk), lambda i, j, s: (s, i)), pl.BlockSpec((tm, tn), lambda i, j, s: (s, j))],
        out_specs=out_specs, out_shape=out_shape,
        scratch_shapes=[pltpu.VMEM((tk, tn), F32)],
        compiler_params=_params("parallel", "parallel", "arbitrary"),
    )(a, g)
    return outs if colsum else outs[0]


def _ln(u, g, b):
    mu = jnp.mean(u, axis=-1, keepdims=True)
    d = u - mu
    var = jnp.mean(d * d, axis=-1, keepdims=True)
    return d * lax.rsqrt(var + LN_EPS) * g + b


def _ln_bwd_block(dy, u, g):
    mu = jnp.mean(u, axis=-1, keepdims=True)
    dd = u - mu
    rstd = lax.rsqrt(jnp.mean(dd * dd, axis=-1, keepdims=True) + LN_EPS)
    xhat = dd * rstd
    dxh = dy * g
    m1 = jnp.mean(dxh, axis=-1, keepdims=True)
    m2 = jnp.mean(dxh * xhat, axis=-1, keepdims=True)
    return (rstd * (dxh - m1 - xhat * m2), jnp.sum(dy * xhat, axis=0, keepdims=True),
            jnp.sum(dy, axis=0, keepdims=True))


SCAN_ROWS = 512


def _tri(n, upper):
    r = lax.broadcasted_iota(jnp.int32, (n, n), 0)
    c = lax.broadcasted_iota(jnp.int32, (n, n), 1)
    return jnp.where((c >= r) if upper else (c <= r), 1.0, 0.0).astype(F32)


def _cumsum_logf(zg):
    s = zg.shape[0]
    t = _div(s, SCAN_ROWS, LANE)
    nb = s // t
    fcol = N_GATE // LANE

    def body(f_ref, c_ref, carry_ref):
        @pl.when(pl.program_id(0) == 0)
        def _():
            carry_ref[...] = jnp.zeros_like(carry_ref)

        f = f_ref[...]
        logf = jnp.minimum(f, 0.0) - jnp.log(1.0 + jnp.exp(-jnp.abs(f)))
        c = jnp.dot(_tri(t, False), logf, precision=lax.Precision.HIGHEST, preferred_element_type=F32)
        c = c + carry_ref[0:1, :]
        c_ref[...] = c
        carry_ref[...] = jnp.broadcast_to(c[t - 1:t, :], carry_ref.shape)

    return pl.pallas_call(
        body, name="cumsum_logf", grid=(nb,),
        in_specs=[pl.BlockSpec((t, LANE), lambda i: (i, fcol))],
        out_specs=pl.BlockSpec((t, LANE), lambda i: (i, 0)),
        out_shape=jax.ShapeDtypeStruct((s, LANE), F32),
        scratch_shapes=[pltpu.VMEM((8, LANE), F32)],
        compiler_params=_params("arbitrary"),
    )(zg)


def _forget_bwd(dcc, zg):
    s = zg.shape[0]
    t = _div(s, SCAN_ROWS, LANE)
    nb = s // t
    fcol = N_GATE // LANE

    def body(dc_ref, f_ref, o_ref, carry_ref):
        @pl.when(pl.program_id(0) == 0)
        def _():
            carry_ref[...] = jnp.zeros_like(carry_ref)

        lane = lax.broadcasted_iota(jnp.int32, (1, LANE), 1)
        dc = jnp.zeros((t, LANE), F32)
        for p in range(FOX_HEADS // 2):
            tile = dc_ref[:, p * LANE:(p + 1) * LANE]
            moved = pltpu.roll(tile, 2 * p, 1) if p else tile
            dc = jnp.where((lane == 2 * p) | (lane == 2 * p + 1), moved, dc)
        dlogf = jnp.dot(_tri(t, True), dc, precision=lax.Precision.HIGHEST, preferred_element_type=F32)
        dlogf = dlogf + carry_ref[0:1, :]
        o_ref[...] = (dlogf * jax.nn.sigmoid(-f_ref[...])).astype(BF16)
        carry_ref[...] = jnp.broadcast_to(dlogf[0:1, :], carry_ref.shape)

    return pl.pallas_call(
        body, name="forget_bwd", grid=(nb,),
        in_specs=[pl.BlockSpec((t, FOX_W), lambda i: (nb - 1 - i, 0)),
                  pl.BlockSpec((t, LANE), lambda i: (nb - 1 - i, fcol))],
        out_specs=pl.BlockSpec((t, LANE), lambda i: (nb - 1 - i, 0)),
        out_shape=jax.ShapeDtypeStruct((s, LANE), BF16),
        scratch_shapes=[pltpu.VMEM((8, LANE), F32)],
        compiler_params=_params("arbitrary"),
    )(dcc, zg)


KA_COL = SWA_Q // LANE
VA_COL = KA_COL + 1


def _half_masks():
    lane = lax.broadcasted_iota(jnp.int32, (1, LANE), 1)
    hi = lane >= HEAD_DIM
    return (jnp.logical_not(hi), hi)


def _both_halves(x, sel):
    xs = jnp.where(sel, x, 0.0)
    return xs + pltpu.roll(xs, HEAD_DIM, 1)


SWA_PER_KV = 4
WIDE = SWA_PER_KV * LANE


def _swa_bias():
    k = np.arange(2 * LANE)[:, None]
    q = np.arange(LANE)[None, :]
    dist = (q + LANE - k).astype(np.float32)
    valid = (dist >= 0) & (dist < LANE)
    per_head = [np.where(valid, np.float32(-s) * dist, np.float32(NEG_INF)) for s in SLOPES]
    return jnp.asarray(np.stack([np.concatenate(per_head[SWA_PER_KV * hk:SWA_PER_KV * (hk + 1)], axis=1)
                                 for hk in range(2)]), F32)


def _no_previous_block(i_blk):
    k = lax.broadcasted_iota(jnp.int32, (2 * LANE, WIDE), 0)
    return jnp.where((i_blk == 0) & (k < LANE), NEG_INF, 0.0)


def _stack_heads(ref, blk, hk, halves, scale):
    tiles = []
    for j in range(SWA_PER_KV):
        p = 2 * hk + j // 2
        t = ref[blk, p * LANE:(p + 1) * LANE]
        if scale:
            t = _scaled(t)
        tiles.append(jnp.where(halves[j % 2], t, jnp.zeros_like(t)))
    return jnp.concatenate(tiles, axis=0)


def _pair_tile(wide, pp, row_halves):
    a = wide[:, (2 * pp) * LANE:(2 * pp + 1) * LANE]
    b = wide[:, (2 * pp + 1) * LANE:(2 * pp + 2) * LANE]
    return jnp.where(row_halves[0], a, b).T


def _lane_blocks(rows8, hk):
    return jnp.concatenate([rows8[SWA_PER_KV * hk + j:SWA_PER_KV * hk + j + 1, :] for j in range(SWA_PER_KV)], axis=1)


def _row_halves():
    hi = lax.broadcasted_iota(jnp.int32, (LANE, 1), 0) >= HEAD_DIM
    return (jnp.logical_not(hi), hi)


NT = (((1,), (1,)), ((), ()))


def _scaled(q):
    return (q.astype(F32) * SCALE).astype(BF16)


SWA_GROUP = 4


def _swa_group(s_len):
    return SWA_GROUP if (s_len // LANE) % SWA_GROUP == 0 else 1


def _swa_specs(group):
    rows = group * LANE
    prev = lambda i: jnp.maximum(i * group - 1, 0)
    return [pl.BlockSpec((rows, SWA_Q), lambda i: (i, 0)),
            pl.BlockSpec((rows, LANE), lambda i: (i, KA_COL)), pl.BlockSpec((rows, LANE), lambda i: (i, VA_COL)),
            pl.BlockSpec((LANE, LANE), lambda i: (prev(i), KA_COL)),
            pl.BlockSpec((LANE, LANE), lambda i: (prev(i), VA_COL))]


def _swa_window(g, cur_ref, prev_ref):
    before = prev_ref[...] if g == 0 else cur_ref[(g - 1) * LANE:g * LANE, :]
    return jnp.concatenate([before, cur_ref[g * LANE:(g + 1) * LANE, :]], axis=0).astype(F32)


def _swa_fwd(zq, sinks):
    s_len = zq.shape[0]
    group = _swa_group(s_len)
    rows = group * LANE
    sink_lanes = jnp.repeat(sinks[:, :SWA_HEADS], LANE, axis=1)

    def body(q_ref, kc_ref, vc_ref, kp_ref, vp_ref, sink_ref, bias_ref, o_ref, lse_ref):
        halves = _half_masks()
        row_halves = _row_halves()
        for g in range(group):
            blk = slice(g * LANE, (g + 1) * LANE)
            kcat = _swa_window(g, kc_ref, kp_ref)
            vcat = _swa_window(g, vc_ref, vp_ref)
            lse_rows = []
            for hk in range(2):
                kb = _both_halves(kcat, halves[hk]).astype(BF16)
                v_t = _both_halves(vcat, halves[hk]).T.astype(BF16)
                q4 = _stack_heads(q_ref, blk, hk, halves, True)
                s_t = lax.dot_general(kb, q4, NT, preferred_element_type=F32) + bias_ref[hk]
                if g == 0:
                    s_t = s_t + _no_previous_block(pl.program_id(0))
                sink = sink_ref[:, hk * WIDE:(hk + 1) * WIDE]
                m = jnp.maximum(jnp.max(s_t, axis=0, keepdims=True), sink)
                pe = jnp.exp(s_t - m)
                den = jnp.sum(pe, axis=0, keepdims=True) + jnp.exp(sink - m)
                out_t = jnp.dot(v_t, (pe * (1.0 / den)).astype(BF16), preferred_element_type=F32)
                for pp in range(2):
                    p = 2 * hk + pp
                    o_ref[blk, p * LANE:(p + 1) * LANE] = _pair_tile(out_t, pp, row_halves).astype(BF16)
                lse4 = m + jnp.log(den)
                lse_rows += [lse4[:, j * LANE:(j + 1) * LANE] for j in range(SWA_PER_KV)]
            lse_ref[:, blk] = jnp.concatenate(lse_rows, axis=0)

    return pl.pallas_call(
        body, name="swa_fwd", grid=(s_len // rows,),
        in_specs=_swa_specs(group) + [pl.BlockSpec((1, SWA_HEADS * LANE), lambda i: (0, 0)),
                                      pl.BlockSpec((2, 2 * LANE, WIDE), lambda i: (0, 0, 0))],
        out_specs=[pl.BlockSpec((rows, SWA_Q), lambda i: (i, 0)), pl.BlockSpec((SWA_HEADS, rows), lambda i: (0, i))],
        out_shape=[jax.ShapeDtypeStruct((s_len, SWA_Q), BF16), jax.ShapeDtypeStruct((SWA_HEADS, s_len), F32)],
        compiler_params=_params("parallel"),
    )(zq, zq, zq, zq, zq, sink_lanes, _swa_bias())


def _swa_bwd(zq, sinks, o, do, lse):
    s_len = zq.shape[0]
    group = _swa_group(s_len)
    rows = group * LANE

    def body(q_ref, kc_ref, vc_ref, kp_ref, vp_ref, sink_ref, bias_ref, o_ref, do_ref, lse_ref,
             dq_ref, dk_ref, dv_ref, ds_ref):
        halves = _half_masks()
        row_halves = _row_halves()
        lane = lax.broadcasted_iota(jnp.int32, (1, LANE), 1)
        dsink = jnp.zeros((1, LANE), F32)
        for g in range(group):
            blk = slice(g * LANE, (g + 1) * LANE)
            i_blk = pl.program_id(0) * group + g
            kcat = _swa_window(g, kc_ref, kp_ref)
            vcat = _swa_window(g, vc_ref, vp_ref)
            lse_rows = lse_ref[:, blk]
            prod = do_ref[blk, :].astype(F32) * o_ref[blk, :].astype(F32)
            select = (lax.broadcasted_iota(jnp.int32, (SWA_HEADS, SWA_Q), 1) // HEAD_DIM
                      == lax.broadcasted_iota(jnp.int32, (SWA_HEADS, SWA_Q), 0))
            delta_rows = lax.dot_general(jnp.where(select, 1.0, 0.0), prod, NT, precision=lax.Precision.HIGHEST,
                                         preferred_element_type=F32)
            dk_tot = jnp.zeros((2 * LANE, LANE), F32)
            dv_tot = jnp.zeros((2 * LANE, LANE), F32)
            for hk in range(2):
                kb = _both_halves(kcat, halves[hk])
                k_t = kb.T.astype(BF16)
                kb = kb.astype(BF16)
                vb = _both_halves(vcat, halves[hk]).astype(BF16)
                q4 = _stack_heads(q_ref, blk, hk, halves, True)
                do4 = _stack_heads(do_ref, blk, hk, halves, False)
                lse4 = _lane_blocks(lse_rows, hk)
                delta4 = _lane_blocks(delta_rows, hk)
                s_t = lax.dot_general(kb, q4, NT, preferred_element_type=F32) + bias_ref[hk]
                if g == 0:
                    s_t = s_t + _no_previous_block(pl.program_id(0))
                p_t = jnp.exp(s_t - lse4)
                dp_t = lax.dot_general(vb, do4, NT, preferred_element_type=F32)
                ds_t = (p_t * (dp_t - delta4)).astype(BF16)
                sink_part = jnp.exp(sink_ref[:, hk * WIDE:(hk + 1) * WIDE] - lse4) * delta4
                for j in range(SWA_PER_KV):
                    dsink_h = -jnp.sum(sink_part[:, j * LANE:(j + 1) * LANE], axis=1, keepdims=True)
                    dsink = dsink + jnp.where(lane == SWA_PER_KV * hk + j, dsink_h, 0.0)
                dq_t = jnp.dot(k_t, ds_t, preferred_element_type=F32)
                for pp in range(2):
                    p = 2 * hk + pp
                    dq_ref[blk, p * LANE:(p + 1) * LANE] = (_pair_tile(dq_t, pp, row_halves) * SCALE).astype(BF16)
                dk_acc = jnp.dot(ds_t, q4, preferred_element_type=F32)
                dv_acc = jnp.dot(p_t.astype(BF16), do4, preferred_element_type=F32)
                dk_tot = dk_tot + jnp.where(halves[hk], dk_acc + pltpu.roll(dk_acc, HEAD_DIM, 1), 0.0)
                dv_tot = dv_tot + jnp.where(halves[hk], dv_acc + pltpu.roll(dv_acc, HEAD_DIM, 1), 0.0)
            cur = pl.ds(pl.multiple_of(i_blk * LANE, LANE), LANE)
            dk_ref[cur, :] = dk_tot[LANE:, :]
            dv_ref[cur, :] = dv_tot[LANE:, :]

            def add_previous(i_blk=i_blk, dk_tot=dk_tot, dv_tot=dv_tot):
                prv = pl.ds(pl.multiple_of((i_blk - 1) * LANE, LANE), LANE)
                dk_ref[prv, :] += dk_tot[:LANE, :]
                dv_ref[prv, :] += dv_tot[:LANE, :]

            if g == 0:
                pl.when(i_blk > 0)(add_previous)
            else:
                add_previous()

        @pl.when(pl.program_id(0) == 0)
        def _():
            ds_ref[...] = jnp.zeros_like(ds_ref)

        ds_ref[...] += dsink

    blk512 = pl.BlockSpec((rows, SWA_Q), lambda i: (i, 0))
    full = pl.BlockSpec((s_len, LANE), lambda i: (0, 0))
    vec = pl.BlockSpec((1, LANE), lambda i: (0, 0))
    return pl.pallas_call(
        body, name="swa_bwd", grid=(s_len // rows,),
        in_specs=_swa_specs(group) + [pl.BlockSpec((1, SWA_HEADS * LANE), lambda i: (0, 0)),
                                      pl.BlockSpec((2, 2 * LANE, WIDE), lambda i: (0, 0, 0)), blk512, blk512,
                                      pl.BlockSpec((SWA_HEADS, rows), lambda i: (0, i))],
        out_specs=[blk512, full, full, vec],
        out_shape=[jax.ShapeDtypeStruct((s_len, SWA_Q), BF16), jax.ShapeDtypeStruct((s_len, LANE), F32),
                   jax.ShapeDtypeStruct((s_len, LANE), F32), jax.ShapeDtypeStruct((1, LANE), F32)],
        compiler_params=_params("arbitrary"),
    )(zq, zq, zq, zq, zq, jnp.repeat(sinks[:, :SWA_HEADS], LANE, axis=1), _swa_bias(), o, do, lse)


QB_COL = (SWA_Q + 2 * SWA_KV) // LANE
KB_COL = QB_COL + FOX_W // LANE
VB_COL = KB_COL + FOX_W // LANE
N_PAIR = FOX_HEADS // 2


def _causal(t, keys_first=False):
    r = lax.broadcasted_iota(jnp.int32, (t, t), 0)
    c = lax.broadcasted_iota(jnp.int32, (t, t), 1)
    return c >= r if keys_first else r >= c


N_SPLIT = 3


def _own_half(e):
    hi = lax.broadcasted_iota(jnp.int32, (1, LANE), 1) >= HEAD_DIM
    return hi if e else jnp.logical_not(hi)


def _feature_lane(e, t):
    return HEAD_DIM * (1 - e) + t


def _fox_prep(zq, c, tm=256):
    s_len = zq.shape[0]
    tm = _div(s_len, tm, 8)

    def body(z_ref, c_ref, qx_ref, kx_ref, vx_ref):
        lane = lax.broadcasted_iota(jnp.int32, (1, LANE), 1)
        for h in range(FOX_HEADS):
            p, e = divmod(h, 2)
            own = _own_half(e)
            tile = lambda col: z_ref[:, (col + p) * LANE:(col + p + 1) * LANE].astype(F32)
            rest = c_ref[:, h:h + 1]
            qf = jnp.zeros((tm, LANE), F32)
            kf = jnp.zeros((tm, LANE), F32)
            for t in range(N_SPLIT):
                part = rest.astype(BF16).astype(F32)
                rest = rest - part
                qf = jnp.where(lane == _feature_lane(e, t), part, qf)
                qf = jnp.where(lane == _feature_lane(e, N_SPLIT + t), 1.0, qf)
                kf = jnp.where(lane == _feature_lane(e, t), 1.0, kf)
                kf = jnp.where(lane == _feature_lane(e, N_SPLIT + t), -part, kf)
            vf = jnp.where(lane == _feature_lane(e, 0), 1.0, 0.0)
            cols = slice(h * LANE, (h + 1) * LANE)
            qx_ref[:, cols] = jnp.where(own, tile(QB_COL) * SCALE, qf).astype(BF16)
            kx_ref[:, cols] = jnp.where(own, tile(KB_COL), kf).astype(BF16)
            vx_ref[:, cols] = jnp.where(own, tile(VB_COL), vf).astype(BF16)

    out = jax.ShapeDtypeStruct((s_len, FOX_HEADS * LANE), BF16)
    blk = pl.BlockSpec((tm, FOX_HEADS * LANE), lambda i: (i, 0))
    return pl.pallas_call(
        body, name="fox_prep", grid=(s_len // tm,),
        in_specs=[pl.BlockSpec((tm, N_QKV), lambda i: (i, 0)), pl.BlockSpec((tm, LANE), lambda i: (i, 0))],
        out_specs=[blk, blk, blk], out_shape=[out, out, out],
        compiler_params=_params("parallel"),
    )(zq, c)


def _comm_parts(comm):
    return ([], [], [], []) if comm is None else (comm.specs, comm.out_shape, comm.scratch, comm.arrays)


def _fox_fwd(qx, kx, vx, comm=None, t_cap=1024):
    s_len = qx.shape[0]
    t = _div(s_len, t_cap, LANE)
    nq = s_len // t
    c_specs, c_shapes, c_scratch, c_arrays = _comm_parts(comm)

    def compute(q_ref, k_ref, v_ref, o_ref, o32_ref, m_ref, l_ref):
        i = pl.program_id(1)
        qs = [q_ref[:, e * LANE:(e + 1) * LANE] for e in range(2)]

        def step(j, carry, diag):
            rows = pl.ds(pl.multiple_of(j * t, t), t)
            new = []
            for e in range(2):
                m, acc = carry[e]
                s2 = lax.dot_general(qs[e], k_ref[rows, e * LANE:(e + 1) * LANE], NT,
                                     preferred_element_type=F32) * LOG2E
                if diag:
                    s2 = jnp.where(_causal(t), s2, NEG_INF)
                mn = jnp.maximum(m, jnp.ceil(jnp.max(s2, axis=1, keepdims=True)))
                pe = jnp.exp2(s2 - mn).astype(BF16)
                acc = acc * jnp.exp2(m - mn) + jnp.dot(pe, v_ref[rows, e * LANE:(e + 1) * LANE],
                                                       preferred_element_type=F32)
                new.append((mn, acc))
            return tuple(new)

        init = (jnp.full((t, 1), NEG_INF, F32), jnp.zeros((t, LANE), F32))
        carry = lax.fori_loop(0, i, lambda j, c: step(j, c, False), (init, init))
        carry = step(i, carry, True)
        outs, ls = [], []
        for e in range(2):
            m, acc = carry[e]
            l = acc[:, _feature_lane(e, 0):_feature_lane(e, 0) + 1]
            outs.append(acc / l)
            ls.append(l)
        out = jnp.where(_own_half(1), outs[1], outs[0])
        o_ref[...] = out.astype(BF16)
        o32_ref[...] = out
        m_ref[...] = jnp.where(_own_half(1), carry[1][0], carry[0][0])
        l_ref[...] = jnp.where(_own_half(1), ls[1], ls[0])

    body = _with_comm(comm, 3, 4, lambda: (pl.program_id(0) == 0) & (pl.program_id(1) == 0),
                      lambda: (pl.program_id(0) == N_PAIR - 1) & (pl.program_id(1) == nq - 1), compute)
    pair = pl.BlockSpec((s_len, 2 * LANE), lambda p, i: (0, p))
    tile = pl.BlockSpec((t, LANE), lambda p, i: (i, p))
    wide = jax.ShapeDtypeStruct((s_len, FOX_W), F32)
    outs = pl.pallas_call(
        body, name="fox_fwd" if comm is None else "fox_fwd_comm%d" % comm.n, grid=(N_PAIR, nq),
        in_specs=[pl.BlockSpec((t, 2 * LANE), lambda p, i: (i, p)), pair, pair] + c_specs,
        out_specs=[tile, tile, tile, tile] + c_specs,
        out_shape=[jax.ShapeDtypeStruct((s_len, FOX_W), BF16), wide, wide, wide] + c_shapes,
        scratch_shapes=c_scratch,
        compiler_params=_params("arbitrary", "arbitrary"),
    )(qx, kx, vx, *c_arrays)
    return outs[0], outs[1], outs[2], outs[3], outs[4:]


def _fox_stats(o, do, m, l, tm=256):
    s_len = o.shape[0]
    tm = _div(s_len, tm, LANE)

    def body(o_ref, do_ref, m_ref, l_ref, dox_ref, st_ref):
        lane = lax.broadcasted_iota(jnp.int32, (1, LANE), 1)
        for p in range(N_PAIR):
            cols = slice(p * LANE, (p + 1) * LANE)
            dout = do_ref[:, cols]
            prod = o_ref[:, cols] * dout.astype(F32)
            shift = m_ref[:, cols]
            inv_l = 1.0 / l_ref[:, cols]
            st = jnp.zeros((tm, LANE), F32)
            for e in range(2):
                h = 2 * p + e
                dox_ref[:, h * LANE:(h + 1) * LANE] = jnp.where(_own_half(e), dout, jnp.zeros_like(dout))
                st = jnp.where(lane == e, shift[:, e * HEAD_DIM:e * HEAD_DIM + 1], st)
                delta = jnp.sum(jnp.where(_own_half(e), prod, 0.0), axis=1, keepdims=True)
                st = jnp.where(lane == 2 + e, delta, st)
                st = jnp.where(lane == 4 + e, inv_l[:, e * HEAD_DIM:e * HEAD_DIM + 1], st)
            st_ref[p] = st.T[:8, :]

    row = pl.BlockSpec((tm, FOX_W), lambda i: (i, 0))
    return pl.pallas_call(
        body, name="fox_stats", grid=(s_len // tm,), in_specs=[row, row, row, row],
        out_specs=[pl.BlockSpec((tm, FOX_HEADS * LANE), lambda i: (i, 0)),
                   pl.BlockSpec((N_PAIR, 8, tm), lambda i: (0, 0, i))],
        out_shape=[jax.ShapeDtypeStruct((s_len, FOX_HEADS * LANE), BF16),
                   jax.ShapeDtypeStruct((N_PAIR, 8, s_len), F32)],
        compiler_params=_params("parallel"),
    )(o, do, m, l)


def _fox_bwd(qx, kx, vx, dox, stats, comm=None, t_cap=1024):
    s_len = qx.shape[0]
    t = _div(s_len, t_cap, LANE)
    n = s_len // t
    c_specs, c_shapes, c_scratch, c_arrays = _comm_parts(comm)

    def compute(q_ref, do_ref, st_ref, k_ref, v_ref, dq_ref, dk_ref, dv_ref, dc_ref):
        j = pl.program_id(1)
        lane = lax.broadcasted_iota(jnp.int32, (1, LANE), 1)

        @pl.when(j == 0)
        def _():
            dq_ref[...] = jnp.zeros_like(dq_ref)

        ks = [k_ref[:, e * LANE:(e + 1) * LANE] for e in range(2)]
        vs = [v_ref[:, e * LANE:(e + 1) * LANE] for e in range(2)]
        ks_t = [k.astype(F32).T.astype(BF16) for k in ks]

        def step(i, carry, diag):
            rows = pl.ds(pl.multiple_of(i * t, t), t)
            new = []
            dq = jnp.zeros((LANE, t), F32)
            for e in range(2):
                dk, dv, dc = carry[e]
                q = q_ref[rows, e * LANE:(e + 1) * LANE]
                dout = do_ref[rows, e * LANE:(e + 1) * LANE]
                s_t = lax.dot_general(ks[e], q, NT, preferred_element_type=F32) * LOG2E
                if diag:
                    s_t = jnp.where(_causal(t, keys_first=True), s_t, NEG_INF)
                p_t = jnp.exp2(s_t - st_ref[0, e:e + 1, rows]).astype(BF16).astype(F32) * st_ref[0, 4 + e:5 + e, rows]
                dp_t = lax.dot_general(vs[e], dout, NT, preferred_element_type=F32)
                ds_f = p_t * (dp_t - st_ref[0, 2 + e:3 + e, rows])
                ds_t = ds_f.astype(BF16)
                dc = dc + jnp.sum(ds_f, axis=1, keepdims=True)
                dv = dv + jnp.dot(p_t.astype(BF16), dout, preferred_element_type=F32)
                dk = dk + jnp.dot(ds_t, q, preferred_element_type=F32)
                dq_e = jnp.dot(ks_t[e], ds_t, preferred_element_type=F32)
                dq = dq + jnp.where(_row_halves()[e], dq_e, 0.0)
                new.append((dk, dv, dc))
            dq_ref[rows, :] += dq.T * SCALE
            return tuple(new)

        zero = jnp.zeros((t, LANE), F32)
        init = (zero, zero, jnp.zeros((t, 1), F32))
        carry = step(j, (init, init), True)
        (dk0, dv0, dc0), (dk1, dv1, dc1) = lax.fori_loop(j + 1, n, lambda i, c: step(i, c, False), carry)
        dk_ref[...] = jnp.where(_own_half(1), dk1, dk0).astype(BF16)
        dv_ref[...] = jnp.where(_own_half(1), dv1, dv0).astype(BF16)
        dc_ref[...] = jnp.where(lane == 0, -dc0, jnp.where(lane == 1, -dc1, 0.0))

    body = _with_comm(comm, 5, 4, lambda: (pl.program_id(0) == 0) & (pl.program_id(1) == 0),
                      lambda: (pl.program_id(0) == N_PAIR - 1) & (pl.program_id(1) == n - 1), compute)
    pair = pl.BlockSpec((s_len, 2 * LANE), lambda p, j: (0, p))
    blk = pl.BlockSpec((t, 2 * LANE), lambda p, j: (j, p))
    tile = pl.BlockSpec((t, LANE), lambda p, j: (j, p))
    outs = pl.pallas_call(
        body, name="fox_bwd" if comm is None else "fox_bwd_comm", grid=(N_PAIR, n),
        in_specs=[pair, pair, pl.BlockSpec((1, 8, s_len), lambda p, j: (p, 0, 0)), blk, blk] + c_specs,
        out_specs=[pl.BlockSpec((s_len, LANE), lambda p, j: (0, p)), tile, tile, tile] + c_specs,
        out_shape=[jax.ShapeDtypeStruct((s_len, FOX_W), F32), jax.ShapeDtypeStruct((s_len, FOX_W), BF16),
                   jax.ShapeDtypeStruct((s_len, FOX_W), BF16), jax.ShapeDtypeStruct((s_len, FOX_W), F32)] + c_shapes,
        scratch_shapes=c_scratch,
        compiler_params=_params("arbitrary", "arbitrary"),
    )(qx, dox, stats, kx, vx, *c_arrays)
    return outs[0], outs[1], outs[2], outs[3], outs[4:]


def _mixer_out(attn_a, attn_b, zg, h, wpa, wpb, wout, g, b, tm=256):
    m = h.shape[0]
    tm = _div(m, tm, 8)

    def body(a_ref, b_ref, ga_ref, gb_ref, h_ref, wpa_ref, wpb_ref, wout_ref, g_ref, bb_ref, h1_ref, u_ref, mg_ref):
        ya = jnp.dot(a_ref[...], wpa_ref[...], preferred_element_type=F32)
        yb = jnp.dot(b_ref[...], wpb_ref[...], preferred_element_type=F32)
        merged = (jax.nn.sigmoid(ga_ref[...]) * ya + jax.nn.sigmoid(gb_ref[...]) * yb).astype(BF16)
        u = ALPHA * h_ref[...] + jnp.dot(merged, wout_ref[...], preferred_element_type=F32)
        u_ref[...] = u
        h1_ref[...] = _ln(u, g_ref[...], bb_ref[...])
        mg_ref[...] = merged

    row = pl.BlockSpec((tm, D_MODEL), lambda i: (i, 0))
    att = pl.BlockSpec((tm, SWA_Q), lambda i: (i, 0))
    vec = pl.BlockSpec((1, D_MODEL), lambda i: (0, 0))
    wsm = pl.BlockSpec((SWA_Q, D_MODEL), lambda i: (0, 0))
    return pl.pallas_call(
        body, name="mixer_out", grid=(m // tm,),
        in_specs=[att, att, row, pl.BlockSpec((tm, D_MODEL), lambda i: (i, 1)), row, wsm, wsm,
                  pl.BlockSpec((D_MODEL, D_MODEL), lambda i: (0, 0)), vec, vec],
        out_specs=[row, row, row],
        out_shape=[jax.ShapeDtypeStruct((m, D_MODEL), F32), jax.ShapeDtypeStruct((m, D_MODEL), F32),
                   jax.ShapeDtypeStruct((m, D_MODEL), BF16)],
        compiler_params=_params("parallel"),
    )(attn_a, attn_b, zg, zg, h, wpa, wpb, wout, g, b)


def _mixer_bwd(dh1, u1, g, wout, attn_a, attn_b, zg, wpa, wpb, tm=256):
    m = dh1.shape[0]
    tm = _div(m, tm, 8)

    def body(dh_ref, u_ref, g_ref, wout_ref, a_ref, b_ref, ga_ref, gb_ref, wpa_ref, wpb_ref,
             du_ref, dg_ref, db_ref, dya_ref, dyb_ref, dga_ref, dgb_ref, da_ref, dbb_ref):
        @pl.when(pl.program_id(0) == 0)
        def _():
            dg_ref[...] = jnp.zeros_like(dg_ref)
            db_ref[...] = jnp.zeros_like(db_ref)

        du, dg, db = _ln_bwd_block(dh_ref[...], u_ref[...], g_ref[...])
        du_ref[...] = du
        dg_ref[...] += dg
        db_ref[...] += db
        dm = lax.dot_general(du.astype(BF16), wout_ref[...], (((1,), (1,)), ((), ())), preferred_element_type=F32)
        for x_ref, gate_ref, w_ref, dy_ref, dgate_ref, dattn_ref in (
                (a_ref, ga_ref, wpa_ref, dya_ref, dga_ref, da_ref), (b_ref, gb_ref, wpb_ref, dyb_ref, dgb_ref, dbb_ref)):
            sg = jax.nn.sigmoid(gate_ref[...])
            dy = (dm * sg).astype(BF16)
            dy_ref[...] = dy
            y = jnp.dot(x_ref[...], w_ref[...], preferred_element_type=F32)
            dgate_ref[...] = (dm * y * sg * (1.0 - sg)).astype(BF16)
            dattn_ref[...] = lax.dot_general(dy, w_ref[...], (((1,), (1,)), ((), ())),
                                             preferred_element_type=F32).astype(BF16)

    row = pl.BlockSpec((tm, D_MODEL), lambda i: (i, 0))
    att = pl.BlockSpec((tm, SWA_Q), lambda i: (i, 0))
    vec = pl.BlockSpec((1, D_MODEL), lambda i: (0, 0))
    wsm = pl.BlockSpec((SWA_Q, D_MODEL), lambda i: (0, 0))
    wide = jax.ShapeDtypeStruct((m, D_MODEL), BF16)
    narrow = jax.ShapeDtypeStruct((m, SWA_Q), BF16)
    sums = jax.ShapeDtypeStruct((1, D_MODEL), F32)
    return pl.pallas_call(
        body, name="mixer_bwd", grid=(m // tm,),
        in_specs=[row, row, vec, pl.BlockSpec((D_MODEL, D_MODEL), lambda i: (0, 0)), att, att, row,
                  pl.BlockSpec((tm, D_MODEL), lambda i: (i, 1)), wsm, wsm],
        out_specs=[row, vec, vec, row, row, row, row, att, att],
        out_shape=[jax.ShapeDtypeStruct((m, D_MODEL), F32), sums, sums, wide, wide, wide, wide, narrow, narrow],
        compiler_params=_params("arbitrary"),
    )(dh1, u1, g, wout, attn_a, attn_b, zg, zg, wpa, wpb)


def _shift_down(x, k, halo, first):
    rows = lax.broadcasted_iota(jnp.int32, (x.shape[0], 1), 0)
    y = pltpu.roll(x, k, 0)
    for r in range(k):
        fill = jnp.where(first, 0.0, halo[8 - k + r:8 - k + r + 1, :])
        y = jnp.where(rows == r, fill, y)
    return y


def _shift_up(x, k, halo, last):
    n = x.shape[0]
    rows = lax.broadcasted_iota(jnp.int32, (n, 1), 0)
    y = pltpu.roll(x, n - k, 0)
    for r in range(k):
        fill = jnp.where(last, 0.0, halo[r:r + 1, :])
        y = jnp.where(rows == n - k + r, fill, y)
    return y


def _conv_act(gate, gate_m1, gate_m2, cw, cb):
    return cb + cw[0:1, :] * gate_m2 + cw[1:2, :] * gate_m1 + cw[2:3, :] * gate


def _ffn_in_conv(h1, wfi, cw, cb, tm=256):
    s_len = h1.shape[0]
    tm = _div(s_len, tm, 8)
    hb = tm // 8

    def body(a_ref, ap_ref, w_ref, cw_ref, cb_ref, gu_ref, act_ref):
        first = pl.program_id(0) == 0
        a = a_ref[...].astype(BF16)
        before = ap_ref[...].astype(BF16)
        for c in range(N_CHUNK):
            gate = lax.dot_general(a, w_ref[c], NT, preferred_element_type=F32)
            up = lax.dot_general(a, w_ref[N_CHUNK + c], NT, preferred_element_type=F32)
            halo = lax.dot_general(before, w_ref[c], NT, preferred_element_type=F32)
            gu_ref[c, 0] = gate
            gu_ref[c, 1] = up
            conv = _conv_act(gate, _shift_down(gate, 1, halo, first), _shift_down(gate, 2, halo, first),
                             cw_ref[c], cb_ref[c])
            act_ref[c] = (conv * jax.nn.sigmoid(conv) * up).astype(BF16)

    return pl.pallas_call(
        body, name="ffn_in_conv", grid=(s_len // tm,),
        in_specs=[pl.BlockSpec((tm, D_MODEL), lambda i: (i, 0)),
                  pl.BlockSpec((8, D_MODEL), lambda i: (jnp.maximum(i * hb - 1, 0), 0)),
                  pl.BlockSpec((N_DEV, FF_CHUNK, D_MODEL), lambda i: (0, 0, 0)),
                  pl.BlockSpec((N_CHUNK, 8, FF_CHUNK), lambda i: (0, 0, 0)),
                  pl.BlockSpec((N_CHUNK, 1, FF_CHUNK), lambda i: (0, 0, 0))],
        out_specs=[pl.BlockSpec((N_CHUNK, 2, tm, FF_CHUNK), lambda i: (0, 0, i, 0)),
                   pl.BlockSpec((N_CHUNK, tm, FF_CHUNK), lambda i: (0, i, 0))],
        out_shape=[jax.ShapeDtypeStruct((N_CHUNK, 2, s_len, FF_CHUNK), F32),
                   jax.ShapeDtypeStruct((N_CHUNK, s_len, FF_CHUNK), BF16)],
        compiler_params=_params("parallel"),
    )(h1, h1, wfi, cw, cb)


def _ffn_out_ln(act, wfo, res, g, b, target=None, tm=512):
    s_len = res.shape[0]
    tm = _div(s_len, tm, 8)
    last = target is not None

    def body(a_ref, w_ref, res_ref, g_ref, b_ref, *rest):
        u = ALPHA * res_ref[...]
        for c in range(N_CHUNK):
            u = u + jnp.dot(a_ref[c], w_ref[c], preferred_element_type=F32)
        y = _ln(u, g_ref[...], b_ref[...])
        if not last:
            u_ref, y_ref = rest
            u_ref[...] = u
            y_ref[...] = y
            return
        t_ref, u_ref, dy_ref, loss_ref = rest
        u_ref[...] = u

        @pl.when(pl.program_id(0) == 0)
        def _():
            loss_ref[...] = jnp.zeros_like(loss_ref)

        err = y - t_ref[...]
        dy_ref[...] = err / D_MODEL
        loss_ref[...] += 0.5 * jnp.sum(jnp.sum(err * err, axis=1, keepdims=True) / D_MODEL, axis=0, keepdims=True)

    row = pl.BlockSpec((tm, D_MODEL), lambda i: (i, 0))
    vec = pl.BlockSpec((1, D_MODEL), lambda i: (0, 0))
    wide = jax.ShapeDtypeStruct((s_len, D_MODEL), F32)
    return pl.pallas_call(
        body, name="ffn_out_ln_loss" if last else "ffn_out_ln", grid=(s_len // tm,),
        in_specs=[pl.BlockSpec((N_CHUNK, tm, FF_CHUNK), lambda i: (0, i, 0)),
                  pl.BlockSpec((N_CHUNK, FF_CHUNK, D_MODEL), lambda i: (0, 0, 0)), row, vec, vec] + [row] * last,
        out_specs=[row, row] + [pl.BlockSpec((8, LANE), lambda i: (0, 0))] * last,
        out_shape=[wide, wide] + [jax.ShapeDtypeStruct((8, LANE), F32)] * last,
        compiler_params=_params("arbitrary" if last else "parallel"),
    )(act, wfo, res, g, b, *([target] if last else []))


def _ffn_out_bwd(dh2, u2, g, wfo, tm=512):
    s_len = dh2.shape[0]
    tm = _div(s_len, tm, 8)

    def body(dh_ref, u_ref, g_ref, w_ref, du_ref, dg_ref, db_ref, o_ref):
        @pl.when(pl.program_id(0) == 0)
        def _():
            dg_ref[...] = jnp.zeros_like(dg_ref)
            db_ref[...] = jnp.zeros_like(db_ref)

        du, dg, db = _ln_bwd_block(dh_ref[...], u_ref[...], g_ref[...])
        du_ref[...] = du
        dg_ref[...] += dg
        db_ref[...] += db
        du_b = du.astype(BF16)
        for c in range(N_CHUNK):
            o_ref[c] = lax.dot_general(du_b, w_ref[c], (((1,), (1,)), ((), ())), preferred_element_type=F32)

    row = pl.BlockSpec((tm, D_MODEL), lambda i: (i, 0))
    vec = pl.BlockSpec((1, D_MODEL), lambda i: (0, 0))
    sums = jax.ShapeDtypeStruct((1, D_MODEL), F32)
    return pl.pallas_call(
        body, name="ffn_out_bwd", grid=(s_len // tm,),
        in_specs=[row, row, vec, pl.BlockSpec((N_CHUNK, FF_CHUNK, D_MODEL), lambda i: (0, 0, 0))],
        out_specs=[row, vec, vec, pl.BlockSpec((N_CHUNK, tm, FF_CHUNK), lambda i: (0, i, 0))],
        out_shape=[jax.ShapeDtypeStruct((s_len, D_MODEL), F32), sums, sums,
                   jax.ShapeDtypeStruct((N_CHUNK, s_len, FF_CHUNK), F32)],
        compiler_params=_params("arbitrary"),
    )(dh2, u2, g, wfo)


def _g_w_ffn_out(act, du, tm=2048):
    s_len = du.shape[0]
    tm = _div(s_len, tm, 8)
    steps = s_len // tm

    def body(a_ref, g_ref, o_ref, acc_ref):
        s = pl.program_id(1)

        @pl.when(s == 0)
        def _():
            acc_ref[...] = jnp.zeros_like(acc_ref)

        acc_ref[...] += lax.dot_general(a_ref[0], g_ref[...].astype(BF16), (((0,), (0,)), ((), ())),
                                        preferred_element_type=F32)

        @pl.when(s == steps - 1)
        def _():
            o_ref[0] = acc_ref[...].astype(BF16)

    return pl.pallas_call(
        body, name="g_w_ffn_out", grid=(N_CHUNK, steps),
        in_specs=[pl.BlockSpec((1, tm, FF_CHUNK), lambda c, s: (c, s, 0)),
                  pl.BlockSpec((tm, D_MODEL), lambda c, s: (s, 0))],
        out_specs=pl.BlockSpec((1, FF_CHUNK, D_MODEL), lambda c, s: (c, 0, 0)),
        out_shape=jax.ShapeDtypeStruct((N_CHUNK, FF_CHUNK, D_MODEL), BF16),
        scratch_shapes=[pltpu.VMEM((FF_CHUNK, D_MODEL), F32)],
        compiler_params=_params("parallel", "arbitrary"),
    )(act, du)


def _g_w_ffn_in(h1, dgu, tm=2048):
    s_len = h1.shape[0]
    tm = _div(s_len, tm, 8)
    steps = s_len // tm

    def body(a_ref, g_ref, o_ref, acc_ref):
        s = pl.program_id(1)

        @pl.when(s == 0)
        def _():
            acc_ref[...] = jnp.zeros_like(acc_ref)

        acc_ref[...] += lax.dot_general(g_ref[0, 0], a_ref[...].astype(BF16), (((0,), (0,)), ((), ())),
                                        preferred_element_type=F32)

        @pl.when(s == steps - 1)
        def _():
            o_ref[0] = acc_ref[...].astype(BF16)

    return pl.pallas_call(
        body, name="g_w_ffn_in", grid=(N_DEV, steps),
        in_specs=[pl.BlockSpec((tm, D_MODEL), lambda d, s: (s, 0)),
                  pl.BlockSpec((1, 1, tm, FF_CHUNK), lambda d, s: (d % N_CHUNK, d // N_CHUNK, s, 0))],
        out_specs=pl.BlockSpec((1, FF_CHUNK, D_MODEL), lambda d, s: (d, 0, 0)),
        out_shape=jax.ShapeDtypeStruct((N_DEV, FF_CHUNK, D_MODEL), BF16),
        scratch_shapes=[pltpu.VMEM((FF_CHUNK, D_MODEL), F32)],
        compiler_params=_params("parallel", "arbitrary"),
    )(h1, dgu)


def _conv_bwd_dh1(gu, dact, cw, cb, wfi, res, tm=256):
    s_len = gu.shape[2]
    tm = _div(s_len, tm, 8)
    nrow = s_len // tm
    hb = tm // 8

    def dconv_of(conv, up, da):
        sg = jax.nn.sigmoid(conv)
        return da * up * (sg * (1.0 + conv * (1.0 - sg)))

    def body(gu_ref, gp_ref, gun_ref, da_ref, dan_ref, cw_ref, cb_ref, w_ref, res_ref, dgu_ref, dcw_ref, dh_ref):
        i = pl.program_id(0)
        first = i == 0
        last = i == nrow - 1

        @pl.when(first)
        def _():
            dcw_ref[...] = jnp.zeros_like(dcw_ref)

        row = lax.broadcasted_iota(jnp.int32, (8, 1), 0)
        acc = ALPHA * res_ref[...]
        for c in range(N_CHUNK):
            cw = cw_ref[c]
            cb = cb_ref[c]
            gate = gu_ref[c, 0]
            halo = gp_ref[c, 0]
            g_m1 = _shift_down(gate, 1, halo, first)
            g_m2 = _shift_down(gate, 2, halo, first)
            conv = _conv_act(gate, g_m1, g_m2, cw, cb)
            da = da_ref[c]
            dup = (da * conv * jax.nn.sigmoid(conv)).astype(BF16)
            dconv = dconv_of(conv, gu_ref[c, 1], da)
            gate_n = gun_ref[c, 0]
            tail = gate[tm - 8:, :]
            conv_n = _conv_act(gate_n, _shift_down(gate_n, 1, tail, False), _shift_down(gate_n, 2, tail, False),
                               cw, cb)
            dconv_n = dconv_of(conv_n, gun_ref[c, 1], dan_ref[c])
            dgate = (cw[2:3, :] * dconv + cw[1:2, :] * _shift_up(dconv, 1, dconv_n, last)
                     + cw[0:1, :] * _shift_up(dconv, 2, dconv_n, last)).astype(BF16)
            dgu_ref[c, 0] = dgate
            dgu_ref[c, 1] = dup
            acc = acc + jnp.dot(dgate, w_ref[c], preferred_element_type=F32)
            acc = acc + jnp.dot(dup, w_ref[N_CHUNK + c], preferred_element_type=F32)
            part = jnp.zeros((8, FF_CHUNK), F32)
            for r, term in enumerate((dconv * g_m2, dconv * g_m1, dconv * gate, dconv)):
                part = jnp.where(row == r, jnp.sum(term, axis=0, keepdims=True), part)
            dcw_ref[c] += part
        dh_ref[...] = acc

    nxt = lambda i: jnp.minimum((i + 1) * hb, s_len // 8 - 1)
    main = pl.BlockSpec((N_CHUNK, 2, tm, FF_CHUNK), lambda i: (0, 0, i, 0))
    row_d = pl.BlockSpec((tm, D_MODEL), lambda i: (i, 0))
    return pl.pallas_call(
        body, name="conv_bwd_dh1", grid=(nrow,),
        in_specs=[main,
                  pl.BlockSpec((N_CHUNK, 1, 8, FF_CHUNK), lambda i: (0, 0, jnp.maximum(i * hb - 1, 0), 0)),
                  pl.BlockSpec((N_CHUNK, 2, 8, FF_CHUNK), lambda i: (0, 0, nxt(i), 0)),
                  pl.BlockSpec((N_CHUNK, tm, FF_CHUNK), lambda i: (0, i, 0)),
                  pl.BlockSpec((N_CHUNK, 8, FF_CHUNK), lambda i: (0, nxt(i), 0)),
                  pl.BlockSpec((N_CHUNK, 8, FF_CHUNK), lambda i: (0, 0, 0)),
                  pl.BlockSpec((N_CHUNK, 1, FF_CHUNK), lambda i: (0, 0, 0)),
                  pl.BlockSpec((N_DEV, FF_CHUNK, D_MODEL), lambda i: (0, 0, 0)), row_d],
        out_specs=[main, pl.BlockSpec((N_CHUNK, 8, FF_CHUNK), lambda i: (0, 0, 0)), row_d],
        out_shape=[jax.ShapeDtypeStruct((N_CHUNK, 2, s_len, FF_CHUNK), BF16),
                   jax.ShapeDtypeStruct((N_CHUNK, 8, FF_CHUNK), F32),
                   jax.ShapeDtypeStruct((s_len, D_MODEL), F32)],
        compiler_params=_params("arbitrary"),
    )(gu, gu, gu, dact, dact, cw, cb, wfi, res)


def _sum_devices(r_ref):
    acc = r_ref[0].astype(F32)
    for d in range(1, N_DEV):
        acc = acc + r_ref[d].astype(F32)
    return acc


def _sum8(recv):
    rows = recv.shape[1]
    tr = _div(rows, ROW_BLOCK, 8)

    def body(r_ref, o_ref):
        o_ref[...] = _sum_devices(r_ref)

    return pl.pallas_call(
        body, name="sum8", grid=(rows // tr,),
        in_specs=[pl.BlockSpec((N_DEV, tr, LANE), lambda i: (0, i, 0))],
        out_specs=pl.BlockSpec((tr, LANE), lambda i: (i, 0)),
        out_shape=jax.ShapeDtypeStruct((rows, LANE), F32),
        compiler_params=_params("parallel"),
    )(recv)


def _adamw_math(w, g, m, v):
    m = ADAM_B1 * m + (1.0 - ADAM_B1) * g
    v = ADAM_B2 * v + (1.0 - ADAM_B2) * (g * g)
    m_hat = m / (1.0 - ADAM_B1 ** ADAM_STEP)
    v_hat = v / (1.0 - ADAM_B2 ** ADAM_STEP)
    return -ADAM_LR * (m_hat / (jnp.sqrt(v_hat) + ADAM_EPS) + ADAM_WD * w), m, v


def _adamw_rows(w, g, m, v, name):
    rows = w.shape[0]
    tr = _div(rows, ROW_BLOCK, 8)

    def body(w_ref, g_ref, m_ref, v_ref, d_ref, mo_ref, vo_ref):
        d_ref[...], mo_ref[...], vo_ref[...] = _adamw_math(w_ref[...], g_ref[...], m_ref[...], v_ref[...])

    blk = pl.BlockSpec((tr, LANE), lambda i: (i, 0))
    out = jax.ShapeDtypeStruct((rows, LANE), F32)
    return pl.pallas_call(
        body, name=name, grid=(rows // tr,), in_specs=[blk, blk, blk, blk], out_specs=[blk, blk, blk],
        out_shape=[out, out, out], compiler_params=_params("parallel"),
    )(w, g, m, v)


def _adamw_w_in(recv, w, m, v, tl=128):
    n, depth, d = w.shape

    def body(*refs):
        r_refs, (w_ref, m_ref, v_ref), (g_ref, d_ref, mo_ref, vo_ref) = refs[:depth], refs[depth:depth + 3], refs[-4:]
        for l in range(depth):
            g = _sum_devices(r_refs[l])
            g_ref[:, l, :] = g
            d_ref[:, l, :], mo_ref[:, l, :], vo_ref[:, l, :] = _adamw_math(w_ref[:, l, :], g, m_ref[:, l, :],
                                                                            v_ref[:, l, :])

    blk = pl.BlockSpec((n, depth, tl), lambda j: (0, 0, j))
    out = jax.ShapeDtypeStruct((n, depth, d), F32)
    return pl.pallas_call(
        body, name="adamw_w_in", grid=(d // tl,),
        in_specs=[pl.BlockSpec((N_DEV, n, tl), lambda j: (0, 0, j))] * depth + [blk, blk, blk],
        out_specs=[blk, blk, blk, blk], out_shape=[out, out, out, out],
        compiler_params=_params("parallel"),
    )(*recv, w, m, v)


def _adamw_shard(recv, w, m, v, layer, prev, name):
    _, k, n = recv.shape
    tk = _div(k, 128, 16)

    def body(r_ref, w_ref, m_ref, v_ref, *rest):
        g_ref, d_ref, mo_ref, vo_ref = rest[-4:]
        g = _sum_devices(r_ref)
        g_ref[0] = g
        d_ref[0], mo_ref[0], vo_ref[0] = _adamw_math(w_ref[0], g, m_ref[0], v_ref[0])

    blk = pl.BlockSpec((1, tk, n), lambda i: (layer, i, 0))
    out = jax.ShapeDtypeStruct((DEPTH, k, n), F32)
    carried = [] if prev is None else list(prev)
    return pl.pallas_call(
        body, name=name, grid=(k // tk,),
        in_specs=[pl.BlockSpec((N_DEV, tk, n), lambda i: (0, i, 0)), blk, blk, blk]
        + [pl.BlockSpec(memory_space=pl.ANY)] * len(carried),
        out_specs=[blk, blk, blk, blk], out_shape=[out, out, out, out],
        input_output_aliases={4 + j: j for j in range(len(carried))},
        compiler_params=_params("parallel"),
    )(recv, w, m, v, *carried)


def _to_rows(flat, rows):
    flat = flat.reshape(-1)
    return jnp.pad(flat, (0, rows * LANE - flat.shape[0])).reshape(rows, LANE)


def _pad_z(a, axis):
    f0 = N_QKV
    g0 = N_QKV + FOX_HEADS
    take = lambda lo, hi: lax.slice_in_dim(a, lo, hi, axis=axis)
    shape = list(a.shape)
    shape[axis] = F_PAD - FOX_HEADS
    return jnp.concatenate([take(0, f0), take(g0, N_IN), take(f0, g0), jnp.zeros(shape, a.dtype)], axis=axis)


def _unpad_z(a, axis):
    f0 = N_QKV + N_GATE
    take = lambda lo, hi: lax.slice_in_dim(a, lo, hi, axis=axis)
    return jnp.concatenate([take(0, N_QKV), take(f0, f0 + FOX_HEADS), take(N_QKV, f0)], axis=axis)


def _shards_to_cols(g):
    _, k, n = g.shape
    return g.transpose(1, 0, 2).reshape(k, N_DEV * n)


def _cols_to_shards(full):
    k, n = full.shape
    return full.reshape(k, N_DEV, n // N_DEV).transpose(1, 0, 2)


def _layer_fwd(h, w, p, comm=None, late=None, target=None):
    zq, zg = _z_proj(h, w["w_in_p"], p["b_in_p"])
    qx, kx, vx = _fox_prep(zq, _cumsum_logf(zg))
    attn_a, lse_a = _swa_fwd(zq, p["sinks"])
    attn_b, attn_b32, m_b, l_b, arrived = _fox_fwd(qx, kx, vx, comm)
    if late is not None:
        w, p = late(w, p, arrived)
    h1, u1, merged = _mixer_out(attn_a, attn_b, zg, h, w["w_proj_a"], w["w_proj_b"], w["w_out"],
                                        p["ln_mix_g"], p["ln_mix_b"])
    gu, act = _ffn_in_conv(h1, w["w_ffn_in"], p["conv_w"], p["conv_b"])
    u2, *h2 = _ffn_out_ln(act, w["w_ffn_out"], h1, p["ln_ffn_g"], p["ln_ffn_b"], target)
    saved = dict(h=h, zq=zq, zg=zg, qx=qx, kx=kx, vx=vx, attn_a=attn_a, lse_a=lse_a, attn_b=attn_b,
                 attn_b32=attn_b32, m_b=m_b, l_b=l_b, h1=h1, u1=u1, merged=merged, gu=gu, act=act, u2=u2)
    return h2, saved, w, p


def _layer_bwd(dh2, sv, w, p, make_comm=None, make_last_comm=None):
    s_len = dh2.shape[0]
    du2, d_ffn_g, d_ffn_b, dact = _ffn_out_bwd(dh2, sv["u2"], p["ln_ffn_g"], w["w_ffn_out"])
    g_ffn_out = _g_w_ffn_out(sv["act"], du2)
    dgu, dcw, dh1 = _conv_bwd_dh1(sv["gu"], dact, p["conv_w"], p["conv_b"], w["w_ffn_in"], du2)
    dcw = dcw.transpose(1, 0, 2).reshape(8, D_FF)
    g_ffn_in = _g_w_ffn_in(sv["h1"], dgu)
    du1, d_mix_g, d_mix_b, dya, dyb, dga, dgb, dattn_a, dattn_b = _mixer_bwd(
        dh1, sv["u1"], p["ln_mix_g"], w["w_out"], sv["attn_a"], sv["attn_b"], sv["zg"], w["w_proj_a"], w["w_proj_b"])
    g_out = _linear_tn(sv["merged"], du1, name="g_w_out", tn=1024)
    g_proj_a = _linear_tn(sv["attn_a"], dya, name="g_w_proj_a", tk=512, tn=1024)
    g_proj_b = _linear_tn(sv["attn_b"], dyb, name="g_w_proj_b", tk=512, tn=1024)
    dq_a, dk_a, dv_a, dsinks = _swa_bwd(sv["zq"], p["sinks"], sv["attn_a"], dattn_a, sv["lse_a"])
    big = dict(w_proj_a=_cols_to_shards(g_proj_a), w_proj_b=_cols_to_shards(g_proj_b),
               w_out=g_out.reshape(N_DEV, D_MODEL // N_DEV, D_MODEL), w_ffn_in=g_ffn_in,
               w_ffn_out=g_ffn_out.reshape(N_DEV, D_FF // N_DEV, D_MODEL))
    dox, stats = _fox_stats(sv["attn_b32"], dattn_b, sv["m_b"], sv["l_b"])
    dq_b, dk_b, dv_b, dcc, arrived = _fox_bwd(sv["qx"], sv["kx"], sv["vx"], dox, stats,
                                              None if make_comm is None else make_comm(big))
    df = _forget_bwd(dcc, sv["zg"])
    dz = jnp.concatenate([dq_a, dk_a.astype(BF16), dv_a.astype(BF16), dq_b.astype(BF16), dk_b, dv_b, dga, dgb, df,
                          jnp.zeros((s_len, F_PAD - LANE), BF16)], axis=1)
    g_in_t, g_b_in = _linear_tn(dz, sv["h"], name="g_w_in", tk=768, tn=1024, colsum=True)
    g_in_t, g_b_in = _unpad_z(g_in_t, 0), _unpad_z(g_b_in, 1)
    big["w_in"] = g_in_t.reshape(N_DEV, N_IN // N_DEV, D_MODEL)
    small = dict(ln_mix_g=d_mix_g, ln_mix_b=d_mix_b, b_in=g_b_in, attn_sinks=dsinks[:, :SWA_HEADS],
                 ln_ffn_g=d_ffn_g, ln_ffn_b=d_ffn_b, conv_w=dcw[:3], conv_b=dcw[3:4])
    dh, arrived_last = _d_h(dz, w["w_in_p"], du1, None if make_last_comm is None else make_last_comm(big, small))
    return dh, big, small, arrived, arrived_last


def _w_in_layouts(w_in):
    return dict(w_in_p=_pad_z(w_in.reshape(N_IN, D_MODEL), 0))


def _other_layouts(w_proj_a, w_proj_b, w_out, w_ffn_in, w_ffn_out):
    return dict(w_proj_a=_shards_to_cols(w_proj_a), w_proj_b=_shards_to_cols(w_proj_b),
                w_out=w_out.reshape(D_MODEL, D_MODEL), w_ffn_in=w_ffn_in,
                w_ffn_out=w_ffn_out.reshape(N_CHUNK, FF_CHUNK, D_MODEL))


def _layer_params(r):
    return dict(
        b_in_p=_pad_z(r["b_in"].reshape(1, N_IN), 1),
        sinks=jnp.pad(r["attn_sinks"].reshape(1, SWA_HEADS), ((0, 0), (0, LANE - SWA_HEADS))),
        ln_mix_g=r["ln_mix_g"].reshape(1, D_MODEL), ln_mix_b=r["ln_mix_b"].reshape(1, D_MODEL),
        ln_ffn_g=r["ln_ffn_g"].reshape(1, D_MODEL), ln_ffn_b=r["ln_ffn_b"].reshape(1, D_MODEL),
        conv_b=r["conv_b"].reshape(N_CHUNK, 1, FF_CHUNK))


def _conv_w_layout(conv_w):
    return jnp.pad(conv_w, ((0, 5), (0, 0))).reshape(8, N_CHUNK, FF_CHUNK).transpose(1, 0, 2)


def kernel(x, ln_mix_g, ln_mix_b, w_in, b_in, attn_sinks, w_proj_a, w_proj_b, w_out, ln_ffn_g, ln_ffn_b, w_ffn_in, conv_w, conv_b, w_ffn_out, loss_target, m_ln_mix_g, m_ln_mix_b, m_w_in, m_b_in, m_attn_sinks, m_w_proj_a, m_w_proj_b, m_w_out, m_ln_ffn_g, m_ln_ffn_b, m_w_ffn_in, m_conv_w, m_conv_b, m_w_ffn_out, v_ln_mix_g, v_ln_mix_b, v_w_in, v_b_in, v_attn_sinks, v_w_proj_a, v_w_proj_b, v_w_out, v_ln_ffn_g, v_ln_ffn_b, v_w_ffn_in, v_conv_w, v_conv_b, v_w_ffn_out):
    wts = dict(ln_mix_g=ln_mix_g, ln_mix_b=ln_mix_b, w_in=w_in, b_in=b_in, attn_sinks=attn_sinks, w_proj_a=w_proj_a,
               w_proj_b=w_proj_b, w_out=w_out, ln_ffn_g=ln_ffn_g, ln_ffn_b=ln_ffn_b, w_ffn_in=w_ffn_in,
               conv_w=conv_w, conv_b=conv_b, w_ffn_out=w_ffn_out)
    mom = dict(ln_mix_g=m_ln_mix_g, ln_mix_b=m_ln_mix_b, w_in=m_w_in, b_in=m_b_in, attn_sinks=m_attn_sinks,
               w_proj_a=m_w_proj_a, w_proj_b=m_w_proj_b, w_out=m_w_out, ln_ffn_g=m_ln_ffn_g, ln_ffn_b=m_ln_ffn_b,
               w_ffn_in=m_w_ffn_in, conv_w=m_conv_w, conv_b=m_conv_b, w_ffn_out=m_w_ffn_out)
    vel = dict(ln_mix_g=v_ln_mix_g, ln_mix_b=v_ln_mix_b, w_in=v_w_in, b_in=v_b_in, attn_sinks=v_attn_sinks,
               w_proj_a=v_w_proj_a, w_proj_b=v_w_proj_b, w_out=v_w_out, ln_ffn_g=v_ln_ffn_g, ln_ffn_b=v_ln_ffn_b,
               w_ffn_in=v_w_ffn_in, conv_w=v_conv_w, conv_b=v_conv_b, w_ffn_out=v_w_ffn_out)
    names = list(wts)
    big_names = [n for n, _, _ in BIG]
    small_names = [n for n, _ in SMALL]
    me = 4 * lax.axis_index("x") + 2 * lax.axis_index("y") + lax.axis_index("c")
    cw_shard = D_FF // N_DEV

    stored = {"w_in": ((2, 0, 1), (1, 2, 0)), "w_ffn_in": ((0, 2, 1), (0, 2, 1))}
    as_stored = lambda tree: {n: jnp.transpose(tree[n], stored[n][0]) if n in stored else tree[n] for n in big_names}
    w_st, m_st, v_st = as_stored(wts), as_stored(mom), as_stored(vel)
    wb = {n: [(w_st[n][:, l] if n == "w_in" else w_st[n][l]).astype(BF16) for l in range(DEPTH)] for n in big_names}
    ps = [_layer_params(dict(b_in=b_in[l], attn_sinks=attn_sinks[l], ln_mix_g=ln_mix_g[l], ln_mix_b=ln_mix_b[l],
                             ln_ffn_g=ln_ffn_g[l], ln_ffn_b=ln_ffn_b[l], conv_b=conv_b[l])) for l in range(DEPTH)]
    w_in_0, = _exchange([(wb["w_in"][0], True)], "gather_w_in_0")
    others = big_names[1:]
    next_layer = {}

    def late_0(w, p, arrived):
        conv_full = arrived[-1].transpose(1, 2, 0, 3).reshape(DEPTH, 3, D_FF)
        next_layer["w"] = _w_in_layouts(arrived[len(others)])
        next_layer["p"] = dict(ps[1], conv_w=_conv_w_layout(conv_full[1]))
        return dict(w, **_other_layouts(*arrived[:len(others)])), dict(p, conv_w=_conv_w_layout(conv_full[0]))

    def late_1(w, p, arrived):
        return dict(w, **_other_layouts(*arrived)), p

    saved, ws = [None] * DEPTH, [None] * DEPTH
    gather_0 = _Comm([(wb[n][0], True) for n in others] + [(wb["w_in"][1], True), (conv_w, True)])
    (h,), saved[0], ws[0], ps[0] = _layer_fwd(x[0], _w_in_layouts(w_in_0), ps[0], gather_0, late_0)
    gather_1 = _Comm([(wb[n][1], True) for n in others])
    (dh, loss_part), saved[1], ws[1], ps[1] = _layer_fwd(h, next_layer["w"], next_layer["p"], gather_1, late_1,
                                                         loss_target[0])

    def small_rows(small):
        vec = jnp.concatenate([small[n].reshape(-1) for n in small_names] + [loss_part[0, 0].reshape(1)])
        return _to_rows(vec, SMALL_LAYER_ROWS)

    dh, big_1, small_1, _, _ = _layer_bwd(dh, saved[1], ws[1], ps[1])

    def exchange_early(big_0):
        return _Comm([(big_1[n].astype(BF16), False) for n in big_names] + [(small_rows(small_1), True)]
                     + [(big_0[n].astype(BF16), False) for n in others])

    def exchange_last(big_0, small_0):
        return _Comm([(big_0["w_in"].astype(BF16), False), (small_rows(small_0), True)])

    grad_x, _, _, arrived, (g_in_0, g_small_0) = _layer_bwd(dh, saved[0], ws[0], ps[0], exchange_early, exchange_last)
    n_big = len(big_names)
    recv = [[g_in_0] + list(arrived[n_big + 1:]) + [g_small_0], list(arrived[:n_big + 1])]

    big_out = {"w_in": _adamw_w_in([recv[l][0] for l in range(DEPTH)], w_st["w_in"], m_st["w_in"], v_st["w_in"])}
    for t, n in enumerate(big_names):
        if n == "w_in":
            continue
        outs = None
        for l in reversed(range(DEPTH)):
            outs = _adamw_shard(recv[l][t], w_st[n], m_st[n], v_st[n], l, outs, "adamw_%s_%d" % (n, l))
        big_out[n] = outs
    for n, (_, back) in stored.items():
        big_out[n] = [jnp.transpose(a, back) for a in big_out[n]]
    small_sum = [_sum8(recv[l][-1]).reshape(-1) for l in range(DEPTH)]
    g_small = {}
    off = 0
    for n, size in SMALL:
        g_small[n] = jnp.stack([small_sum[l][off:off + size] for l in range(DEPTH)])
        off += size
    loss = small_sum[0][off]
    g_small["conv_w"] = lax.dynamic_slice_in_dim(g_small["conv_w"].reshape(DEPTH, 3, D_FF), me * cw_shard, cw_shard,
                                                 axis=2)
    g_small = {n: g_small[n].reshape(wts[n].shape) for n in small_names}

    def pack_small(tree):
        return _to_rows(jnp.concatenate([tree[n].reshape(-1) for n in small_names]), SMALL_ROWS)

    small_out = (pack_small(g_small),) + tuple(_adamw_rows(pack_small(wts), pack_small(g_small), pack_small(mom),
                                                           pack_small(vel), "adamw_small"))

    def result(j):
        out = {n: big_out[n][j] for n in big_names}
        flat = small_out[j].reshape(-1)
        off = 0
        for n in small_names:
            out[n] = flat[off:off + wts[n].size].reshape(wts[n].shape)
            off += wts[n].size
        return [out[n] for n in names]

    return (loss, grad_x[None], *result(0), *result(1), *result(2), *result(3))
```

```python
import functools

import jax
import jax.numpy as jnp
import numpy as np
from jax import lax
from jax.experimental import pallas as pl
from jax.experimental.pallas import tpu as pltpu

F32 = jnp.float32
BF16 = jnp.bfloat16
MESH = pl.DeviceIdType.MESH

N_DEV = 8
DEPTH = 2
D_MODEL = 1024
HEAD_DIM = 64
SWA_Q = 512
SWA_KV = 128
FOX_W = 512
FOX_HEADS = 8
SWA_HEADS = 8
D_FF = 2816
N_IN = 4360
N_QKV = SWA_Q + 2 * SWA_KV + 3 * FOX_W
N_GATE = 2 * D_MODEL
F_PAD = 256
N_ZG = N_GATE + F_PAD
N_ZP = N_QKV + N_ZG
LN_EPS = 1e-5
NEG_INF = -1e30
ALPHA = (2 * DEPTH) ** 0.25
SCALE = HEAD_DIM ** -0.5
LOG2E = 1.4426950408889634
SLOPES = tuple(2.0 ** (-8.0 * (h + 1) / SWA_HEADS) for h in range(SWA_HEADS))

ADAM_LR = 0.001
ADAM_B1 = 0.9
ADAM_B2 = 0.999
ADAM_EPS = 1e-08
ADAM_WD = 0.01
ADAM_STEP = 10

LANE = 128
VMEM_LIMIT = 56 * 1024 * 1024

BIG = (("w_in", (D_MODEL, N_IN), 1), ("w_proj_a", (SWA_Q, D_MODEL), 1), ("w_proj_b", (FOX_W, D_MODEL), 1),
       ("w_out", (D_MODEL, D_MODEL), 0), ("w_ffn_in", (D_MODEL, 2 * D_FF), 1), ("w_ffn_out", (D_FF, D_MODEL), 0))
SMALL = (("ln_mix_g", D_MODEL), ("ln_mix_b", D_MODEL), ("b_in", N_IN), ("attn_sinks", SWA_HEADS),
         ("ln_ffn_g", D_MODEL), ("ln_ffn_b", D_MODEL), ("conv_w", 3 * D_FF), ("conv_b", D_FF))
ROW_BLOCK = 512
SMALL_LAYER_ROWS = -(-(sum(n for _, n in SMALL) + 1) // (8 * LANE)) * 8
SMALL_ROWS = ROW_BLOCK
FF_CHUNK = 2 * D_FF // N_DEV
N_CHUNK = D_FF // FF_CHUNK


def _div(n, cap, unit):
    if n <= cap:
        return n
    best = None
    for t in range(unit, cap + 1, unit):
        if n % t == 0:
            best = t
    assert best is not None, (n, cap, unit)
    return best


def _params(*sem):
    return pltpu.CompilerParams(dimension_semantics=sem, vmem_limit_bytes=VMEM_LIMIT)


def _peer(r):
    x, y, c = lax.axis_index("x"), lax.axis_index("y"), lax.axis_index("c")
    px = 1 - x if (r >> 2) & 1 else x
    py = 1 - y if (r >> 1) & 1 else y
    pc = 1 - c if r & 1 else c
    return (px, py, pc), 4 * px + 2 * py + pc


class _Comm:
    def __init__(self, tensors):
        self.arrays = [x for x, _ in tensors]
        self.gathers = [g for _, g in tensors]
        self.n = len(tensors)
        self.out_shape = [jax.ShapeDtypeStruct((N_DEV,) + (x.shape if g else x.shape[1:]), x.dtype)
                          for x, g in tensors]
        self.specs = [pl.BlockSpec(memory_space=pl.ANY)] * self.n
        self.scratch = [pltpu.SemaphoreType.DMA((N_DEV - 1, self.n)), pltpu.SemaphoreType.DMA((N_DEV - 1, self.n)),
                        pltpu.SemaphoreType.DMA((self.n,))]

    def _copies(self, x_refs, out_refs, sems):
        send_sems, recv_sems, local_sems = sems
        _, me = _peer(0)

        def src(t, idx):
            return x_refs[t] if self.gathers[t] else x_refs[t].at[idx]

        def remote(r, t, mine):
            peer, pid = _peer(r)
            return pltpu.make_async_remote_copy(src_ref=src(t, pid), dst_ref=out_refs[t].at[me if mine else pid],
                                                send_sem=send_sems.at[r - 1, t], recv_sem=recv_sems.at[r - 1, t],
                                                device_id=peer, device_id_type=MESH)

        pairs = [(r, t) for r in range(1, N_DEV) for t in range(self.n)]
        local = [pltpu.make_async_copy(src(t, me), out_refs[t].at[me], local_sems.at[t]) for t in range(self.n)]
        return local, [remote(r, t, True) for r, t in pairs], lambda: [remote(r, t, False) for r, t in pairs]

    def start(self, x_refs, out_refs, sems):
        local, sent, _ = self._copies(x_refs, out_refs, sems)
        for cp in local + sent:
            cp.start()

    def wait(self, x_refs, out_refs, sems):
        local, sent, landing = self._copies(x_refs, out_refs, sems)
        for cp in landing():
            cp.wait_recv()
        for cp in sent:
            cp.wait_send()
        for cp in local:
            cp.wait()


def _exchange(tensors, name):
    comm = _Comm(tensors)
    n = comm.n

    def body(*refs):
        comm.start(refs[:n], refs[n:2 * n], refs[2 * n:])
        comm.wait(refs[:n], refs[n:2 * n], refs[2 * n:])

    return pl.pallas_call(body, name=name, out_shape=comm.out_shape, in_specs=comm.specs, out_specs=comm.specs,
                          scratch_shapes=comm.scratch)(*comm.arrays)


def _with_comm(comm, n_in, n_out, first, last, compute):
    nc = comm.n if comm is not None else 0

    def body(*refs):
        ins, x_refs = refs[:n_in], refs[n_in:n_in + nc]
        outs = refs[n_in + nc:n_in + nc + n_out]
        out_refs = refs[n_in + nc + n_out:n_in + 2 * nc + n_out]
        sems = refs[n_in + 2 * nc + n_out:]
        if nc:
            @pl.when(first())
            def _():
                comm.start(x_refs, out_refs, sems)

        compute(*ins, *outs)
        if nc:
            @pl.when(last())
            def _():
                comm.wait(x_refs, out_refs, sems)

    return body


def _d_h(dz, w_in_t, res, comm=None, tm=512):
    m, k = dz.shape
    d = w_in_t.shape[1]
    tm = _div(m, tm, 8)
    steps = m // tm
    c_specs, c_shapes, c_scratch, c_arrays = _comm_parts(comm)

    def compute(dz_ref, w_ref, res_ref, o_ref):
        o_ref[...] = ALPHA * res_ref[...] + jnp.dot(dz_ref[...], w_ref[...], preferred_element_type=F32)

    body = _with_comm(comm, 3, 1, lambda: pl.program_id(0) == 0, lambda: pl.program_id(0) == steps - 1, compute)
    row = pl.BlockSpec((tm, d), lambda i: (i, 0))
    outs = pl.pallas_call(
        body, name="d_h" if comm is None else "d_h_comm", grid=(steps,),
        in_specs=[pl.BlockSpec((tm, k), lambda i: (i, 0)), pl.BlockSpec((k, d), lambda i: (0, 0)), row] + c_specs,
        out_specs=[row] + c_specs, out_shape=[jax.ShapeDtypeStruct((m, d), F32)] + c_shapes,
        scratch_shapes=c_scratch,
        compiler_params=_params("arbitrary"),
    )(dz, w_in_t, res, *c_arrays)
    return outs[0], outs[1:]


def _z_proj(h, w_in_t, b_p, tm=512):
    m, k = h.shape
    tm = _div(m, tm, 8)
    nt = (((1,), (1,)), ((), ()))

    def body(h_ref, w_ref, b_ref, zq_ref, zg_ref):
        a = h_ref[...].astype(BF16)
        zq = lax.dot_general(a, w_ref[:N_QKV, :], nt, preferred_element_type=F32)
        zq_ref[...] = (zq + b_ref[:, :N_QKV]).astype(BF16)
        zg_ref[...] = lax.dot_general(a, w_ref[N_QKV:, :], nt, preferred_element_type=F32) + b_ref[:, N_QKV:]

    return pl.pallas_call(
        body, name="z_proj", grid=(m // tm,),
        in_specs=[pl.BlockSpec((tm, k), lambda i: (i, 0)), pl.BlockSpec((N_ZP, k), lambda i: (0, 0)),
                  pl.BlockSpec((1, N_ZP), lambda i: (0, 0))],
        out_specs=[pl.BlockSpec((tm, N_QKV), lambda i: (i, 0)), pl.BlockSpec((tm, N_ZG), lambda i: (i, 0))],
        out_shape=[jax.ShapeDtypeStruct((m, N_QKV), BF16), jax.ShapeDtypeStruct((m, N_ZG), F32)],
        compiler_params=_params("parallel"),
    )(h, w_in_t, b_p)


def _linear_tn(a, g, *, name, tk=1024, tn=640, tm=2048, colsum=False):
    m, k = a.shape
    n = g.shape[1]
    tk = _div(k, tk, LANE)
    tn = _div(n, tn, LANE)
    tm = _div(m, tm, 8)
    steps = m // tm
    assert not colsum or tn == n

    def body(a_ref, g_ref, o_ref, *rest):
        acc_ref = rest[-1]
        s = pl.program_id(2)

        @pl.when(s == 0)
        def _():
            acc_ref[...] = jnp.zeros_like(acc_ref)
            if colsum:
                rest[0][...] = jnp.zeros_like(rest[0])

        a_blk = a_ref[...]
        acc_ref[...] += lax.dot_general(a_blk.astype(BF16), g_ref[...].astype(BF16), (((0,), (0,)), ((), ())),
                                        preferred_element_type=F32)
        if colsum:
            rest[0][...] += jnp.sum(a_blk.astype(F32), axis=0, keepdims=True)

        @pl.when(s == steps - 1)
        def _():
            o_ref[...] = acc_ref[...].astype(BF16)

    out_specs = [pl.BlockSpec((tk, tn), lambda i, j, s: (i, j))]
    out_shape = [jax.ShapeDtypeStruct((k, n), BF16)]
    if colsum:
        out_specs.append(pl.BlockSpec((1, tk), lambda i, j, s: (0, i)))
        out_shape.append(jax.ShapeDtypeStruct((1, k), F32))
    outs = pl.pallas_call(
        body, name=name, grid=(k // tk, n // tn, steps),
        in_specs=[pl.BlockSpec((tm, tk), lambda i, j, s: (s, i)), pl.BlockSpec((tm, tn), lambda i, j, s: (s, j))],
        out_specs=out_specs, out_shape=out_shape,
        scratch_shapes=[pltpu.VMEM((tk, tn), F32)],
        compiler_params=_params("parallel", "parallel", "arbitrary"),
    )(a, g)
    return outs if colsum else outs[0]


def _ln(u, g, b):
    mu = jnp.mean(u, axis=-1, keepdims=True)
    d = u - mu
    var = jnp.mean(d * d, axis=-1, keepdims=True)
    return d * lax.rsqrt(var + LN_EPS) * g + b


def _ln_bwd_block(dy, u, g):
    mu = jnp.mean(u, axis=-1, keepdims=True)
    dd = u - mu
    rstd = lax.rsqrt(jnp.mean(dd * dd, axis=-1, keepdims=True) + LN_EPS)
    xhat = dd * rstd
    dxh = dy * g
    m1 = jnp.mean(dxh, axis=-1, keepdims=True)
    m2 = jnp.mean(dxh * xhat, axis=-1, keepdims=True)
    return (rstd * (dxh - m1 - xhat * m2), jnp.sum(dy * xhat, axis=0, keepdims=True),
            jnp.sum(dy, axis=0, keepdims=True))


SCAN_ROWS = 512


def _tri(n, upper):
    r = lax.broadcasted_iota(jnp.int32, (n, n), 0)
    c = lax.broadcasted_iota(jnp.int32, (n, n), 1)
    return jnp.where((c >= r) if upper else (c <= r), 1.0, 0.0).astype(F32)


def _cumsum_logf(zg):
    s = zg.shape[0]
    t = _div(s, SCAN_ROWS, LANE)
    nb = s // t
    fcol = N_GATE // LANE

    def body(f_ref, c_ref, carry_ref):
        @pl.when(pl.program_id(0) == 0)
        def _():
            carry_ref[...] = jnp.zeros_like(carry_ref)

        f = f_ref[...]
        logf = jnp.minimum(f, 0.0) - jnp.log(1.0 + jnp.exp(-jnp.abs(f)))
        c = jnp.dot(_tri(t, False), logf, precision=lax.Precision.HIGHEST, preferred_element_type=F32)
        c = c + carry_ref[0:1, :]
        c_ref[...] = c
        carry_ref[...] = jnp.broadcast_to(c[t - 1:t, :], carry_ref.shape)

    return pl.pallas_call(
        body, name="cumsum_logf", grid=(nb,),
        in_specs=[pl.BlockSpec((t, LANE), lambda i: (i, fcol))],
        out_specs=pl.BlockSpec((t, LANE), lambda i: (i, 0)),
        out_shape=jax.ShapeDtypeStruct((s, LANE), F32),
        scratch_shapes=[pltpu.VMEM((8, LANE), F32)],
        compiler_params=_params("arbitrary"),
    )(zg)


def _forget_bwd(dcc, zg):
    s = zg.shape[0]
    t = _div(s, SCAN_ROWS, LANE)
    nb = s // t
    fcol = N_GATE // LANE

    def body(dc_ref, f_ref, o_ref, carry_ref):
        @pl.when(pl.program_id(0) == 0)
        def _():
            carry_ref[...] = jnp.zeros_like(carry_ref)

        lane = lax.broadcasted_iota(jnp.int32, (1, LANE), 1)
        dc = jnp.zeros((t, LANE), F32)
        for p in range(FOX_HEADS // 2):
            tile = dc_ref[:, p * LANE:(p + 1) * LANE]
            moved = pltpu.roll(tile, 2 * p, 1) if p else tile
            dc = jnp.where((lane == 2 * p) | (lane == 2 * p + 1), moved, dc)
        dlogf = jnp.dot(_tri(t, True), dc, precision=lax.Precision.HIGHEST, preferred_element_type=F32)
        dlogf = dlogf + carry_ref[0:1, :]
        o_ref[...] = (dlogf * jax.nn.sigmoid(-f_ref[...])).astype(BF16)
        carry_ref[...] = jnp.broadcast_to(dlogf[0:1, :], carry_ref.shape)

    return pl.pallas_call(
        body, name="forget_bwd", grid=(nb,),
        in_specs=[pl.BlockSpec((t, FOX_W), lambda i: (nb - 1 - i, 0)),
                  pl.BlockSpec((t, LANE), lambda i: (nb - 1 - i, fcol))],
        out_specs=pl.BlockSpec((t, LANE), lambda i: (nb - 1 - i, 0)),
        out_shape=jax.ShapeDtypeStruct((s, LANE), BF16),
        scratch_shapes=[pltpu.VMEM((8, LANE), F32)],
        compiler_params=_params("arbitrary"),
    )(dcc, zg)


KA_COL = SWA_Q // LANE
VA_COL = KA_COL + 1


def _half_masks():
    lane = lax.broadcasted_iota(jnp.int32, (1, LANE), 1)
    hi = lane >= HEAD_DIM
    return (jnp.logical_not(hi), hi)


def _both_halves(x, sel):
    xs = jnp.where(sel, x, 0.0)
    return xs + pltpu.roll(xs, HEAD_DIM, 1)


SWA_PER_KV = 4
WIDE = SWA_PER_KV * LANE


def _swa_bias():
    k = np.arange(2 * LANE)[:, None]
    q = np.arange(LANE)[None, :]
    dist = (q + LANE - k).astype(np.float32)
    valid = (dist >= 0) & (dist < LANE)
    per_head = [np.where(valid, np.float32(-s) * dist, np.float32(NEG_INF)) for s in SLOPES]
    return jnp.asarray(np.stack([np.concatenate(per_head[SWA_PER_KV * hk:SWA_PER_KV * (hk + 1)], axis=1)
                                 for hk in range(2)]), F32)


def _no_previous_block(i_blk):
    k = lax.broadcasted_iota(jnp.int32, (2 * LANE, WIDE), 0)
    return jnp.where((i_blk == 0) & (k < LANE), NEG_INF, 0.0)


def _stack_heads(ref, blk, hk, halves, scale):
    tiles = []
    for j in range(SWA_PER_KV):
        p = 2 * hk + j // 2
        t = ref[blk, p * LANE:(p + 1) * LANE]
        if scale:
            t = _scaled(t)
        tiles.append(jnp.where(halves[j % 2], t, jnp.zeros_like(t)))
    return jnp.concatenate(tiles, axis=0)


def _pair_tile(wide, pp, row_halves):
    a = wide[:, (2 * pp) * LANE:(2 * pp + 1) * LANE]
    b = wide[:, (2 * pp + 1) * LANE:(2 * pp + 2) * LANE]
    return jnp.where(row_halves[0], a, b).T


def _lane_blocks(rows8, hk):
    return jnp.concatenate([rows8[SWA_PER_KV * hk + j:SWA_PER_KV * hk + j + 1, :] for j in range(SWA_PER_KV)], axis=1)


def _row_halves():
    hi = lax.broadcasted_iota(jnp.int32, (LANE, 1), 0) >= HEAD_DIM
    return (jnp.logical_not(hi), hi)


NT = (((1,), (1,)), ((), ()))


def _scaled(q):
    return (q.astype(F32) * SCALE).astype(BF16)


SWA_GROUP = 4


def _swa_group(s_len):
    return SWA_GROUP if (s_len // LANE) % SWA_GROUP == 0 else 1


def _swa_specs(group):
    rows = group * LANE
    prev = lambda i: jnp.maximum(i * group - 1, 0)
    return [pl.BlockSpec((rows, SWA_Q), lambda i: (i, 0)),
            pl.BlockSpec((rows, LANE), lambda i: (i, KA_COL)), pl.BlockSpec((rows, LANE), lambda i: (i, VA_COL)),
            pl.BlockSpec((LANE, LANE), lambda i: (prev(i), KA_COL)),
            pl.BlockSpec((LANE, LANE), lambda i: (prev(i), VA_COL))]


def _swa_window(g, cur_ref, prev_ref):
    before = prev_ref[...] if g == 0 else cur_ref[(g - 1) * LANE:g * LANE, :]
    return jnp.concatenate([before, cur_ref[g * LANE:(g + 1) * LANE, :]], axis=0).astype(F32)


def _swa_fwd(zq, sinks):
    s_len = zq.shape[0]
    group = _swa_group(s_len)
    rows = group * LANE
    sink_lanes = jnp.repeat(sinks[:, :SWA_HEADS], LANE, axis=1)

    def body(q_ref, kc_ref, vc_ref, kp_ref, vp_ref, sink_ref, bias_ref, o_ref, lse_ref):
        halves = _half_masks()
        row_halves = _row_halves()
        for g in range(group):
            blk = slice(g * LANE, (g + 1) * LANE)
            kcat = _swa_window(g, kc_ref, kp_ref)
            vcat = _swa_window(g, vc_ref, vp_ref)
            lse_rows = []
            for hk in range(2):
                kb = _both_halves(kcat, halves[hk]).astype(BF16)
                v_t = _both_halves(vcat, halves[hk]).T.astype(BF16)
                q4 = _stack_heads(q_ref, blk, hk, halves, True)
                s_t = lax.dot_general(kb, q4, NT, preferred_element_type=F32) + bias_ref[hk]
                if g == 0:
                    s_t = s_t + _no_previous_block(pl.program_id(0))
                sink = sink_ref[:, hk * WIDE:(hk + 1) * WIDE]
                m = jnp.maximum(jnp.max(s_t, axis=0, keepdims=True), sink)
                pe = jnp.exp(s_t - m)
                den = jnp.sum(pe, axis=0, keepdims=True) + jnp.exp(sink - m)
                out_t = jnp.dot(v_t, (pe * (1.0 / den)).astype(BF16), preferred_element_type=F32)
                for pp in range(2):
                    p = 2 * hk + pp
                    o_ref[blk, p * LANE:(p + 1) * LANE] = _pair_tile(out_t, pp, row_halves).astype(BF16)
                lse4 = m + jnp.log(den)
                lse_rows += [lse4[:, j * LANE:(j + 1) * LANE] for j in range(SWA_PER_KV)]
            lse_ref[:, blk] = jnp.concatenate(lse_rows, axis=0)

    return pl.pallas_call(
        body, name="swa_fwd", grid=(s_len // rows,),
        in_specs=_swa_specs(group) + [pl.BlockSpec((1, SWA_HEADS * LANE), lambda i: (0, 0)),
                                      pl.BlockSpec((2, 2 * LANE, WIDE), lambda i: (0, 0, 0))],
        out_specs=[pl.BlockSpec((rows, SWA_Q), lambda i: (i, 0)), pl.BlockSpec((SWA_HEADS, rows), lambda i: (0, i))],
        out_shape=[jax.ShapeDtypeStruct((s_len, SWA_Q), BF16), jax.ShapeDtypeStruct((SWA_HEADS, s_len), F32)],
        compiler_params=_params("parallel"),
    )(zq, zq, zq, zq, zq, sink_lanes, _swa_bias())


def _swa_bwd(zq, sinks, o, do, lse):
    s_len = zq.shape[0]
    group = _swa_group(s_len)
    rows = group * LANE

    def body(q_ref, kc_ref, vc_ref, kp_ref, vp_ref, sink_ref, bias_ref, o_ref, do_ref, lse_ref,
             dq_ref, dk_ref, dv_ref, ds_ref):
        halves = _half_masks()
        row_halves = _row_halves()
        lane = lax.broadcasted_iota(jnp.int32, (1, LANE), 1)
        dsink = jnp.zeros((1, LANE), F32)
        for g in range(group):
            blk = slice(g * LANE, (g + 1) * LANE)
            i_blk = pl.program_id(0) * group + g
            kcat = _swa_window(g, kc_ref, kp_ref)
            vcat = _swa_window(g, vc_ref, vp_ref)
            lse_rows = lse_ref[:, blk]
            prod = do_ref[blk, :].astype(F32) * o_ref[blk, :].astype(F32)
            select = (lax.broadcasted_iota(jnp.int32, (SWA_HEADS, SWA_Q), 1) // HEAD_DIM
                      == lax.broadcasted_iota(jnp.int32, (SWA_HEADS, SWA_Q), 0))
            delta_rows = lax.dot_general(jnp.where(select, 1.0, 0.0), prod, NT, precision=lax.Precision.HIGHEST,
                                         preferred_element_type=F32)
            dk_tot = jnp.zeros((2 * LANE, LANE), F32)
            dv_tot = jnp.zeros((2 * LANE, LANE), F32)
            for hk in range(2):
                kb = _both_halves(kcat, halves[hk])
                k_t = kb.T.astype(BF16)
                kb = kb.astype(BF16)
                vb = _both_halves(vcat, halves[hk]).astype(BF16)
                q4 = _stack_heads(q_ref, blk, hk, halves, True)
                do4 = _stack_heads(do_ref, blk, hk, halves, False)
                lse4 = _lane_blocks(lse_rows, hk)
                delta4 = _lane_blocks(delta_rows, hk)
                s_t = lax.dot_general(kb, q4, NT, preferred_element_type=F32) + bias_ref[hk]
                if g == 0:
                    s_t = s_t + _no_previous_block(pl.program_id(0))
                p_t = jnp.exp(s_t - lse4)
                dp_t = lax.dot_general(vb, do4, NT, preferred_element_type=F32)
                ds_t = (p_t * (dp_t - delta4)).astype(BF16)
                sink_part = jnp.exp(sink_ref[:, hk * WIDE:(hk + 1) * WIDE] - lse4) * delta4
                for j in range(SWA_PER_KV):
                    dsink_h = -jnp.sum(sink_part[:, j * LANE:(j + 1) * LANE], axis=1, keepdims=True)
                    dsink = dsink + jnp.where(lane == SWA_PER_KV * hk + j, dsink_h, 0.0)
                dq_t = jnp.dot(k_t, ds_t, preferred_element_type=F32)
                for pp in range(2):
                    p = 2 * hk + pp
                    dq_ref[blk, p * LANE:(p + 1) * LANE] = (_pair_tile(dq_t, pp, row_halves) * SCALE).astype(BF16)
                dk_acc = jnp.dot(ds_t, q4, preferred_element_type=F32)
                dv_acc = jnp.dot(p_t.astype(BF16), do4, preferred_element_type=F32)
                dk_tot = dk_tot + jnp.where(halves[hk], dk_acc + pltpu.roll(dk_acc, HEAD_DIM, 1), 0.0)
                dv_tot = dv_tot + jnp.where(halves[hk], dv_acc + pltpu.roll(dv_acc, HEAD_DIM, 1), 0.0)
            cur = pl.ds(pl.multiple_of(i_blk * LANE, LANE), LANE)
            dk_ref[cur, :] = dk_tot[LANE:, :]
            dv_ref[cur, :] = dv_tot[LANE:, :]

            def add_previous(i_blk=i_blk, dk_tot=dk_tot, dv_tot=dv_tot):
                prv = pl.ds(pl.multiple_of((i_blk - 1) * LANE, LANE), LANE)
                dk_ref[prv, :] += dk_tot[:LANE, :]
                dv_ref[prv, :] += dv_tot[:LANE, :]

            if g == 0:
                pl.when(i_blk > 0)(add_previous)
            else:
                add_previous()

        @pl.when(pl.program_id(0) == 0)
        def _():
            ds_ref[...] = jnp.zeros_like(ds_ref)

        ds_ref[...] += dsink

    blk512 = pl.BlockSpec((rows, SWA_Q), lambda i: (i, 0))
    full = pl.BlockSpec((s_len, LANE), lambda i: (0, 0))
    vec = pl.BlockSpec((1, LANE), lambda i: (0, 0))
    return pl.pallas_call(
        body, name="swa_bwd", grid=(s_len // rows,),
        in_specs=_swa_specs(group) + [pl.BlockSpec((1, SWA_HEADS * LANE), lambda i: (0, 0)),
                                      pl.BlockSpec((2, 2 * LANE, WIDE), lambda i: (0, 0, 0)), blk512, blk512,
                                      pl.BlockSpec((SWA_HEADS, rows), lambda i: (0, i))],
        out_specs=[blk512, full, full, vec],
        out_shape=[jax.ShapeDtypeStruct((s_len, SWA_Q), BF16), jax.ShapeDtypeStruct((s_len, LANE), F32),
                   jax.ShapeDtypeStruct((s_len, LANE), F32), jax.ShapeDtypeStruct((1, LANE), F32)],
        compiler_params=_params("arbitrary"),
    )(zq, zq, zq, zq, zq, jnp.repeat(sinks[:, :SWA_HEADS], LANE, axis=1), _swa_bias(), o, do, lse)


QB_COL = (SWA_Q + 2 * SWA_KV) // LANE
KB_COL = QB_COL + FOX_W // LANE
VB_COL = KB_COL + FOX_W // LANE
N_PAIR = FOX_HEADS // 2


def _causal(t, keys_first=False):
    r = lax.broadcasted_iota(jnp.int32, (t, t), 0)
    c = lax.broadcasted_iota(jnp.int32, (t, t), 1)
    return c >= r if keys_first else r >= c


N_SPLIT = 3


def _own_half(e):
    hi = lax.broadcasted_iota(jnp.int32, (1, LANE), 1) >= HEAD_DIM
    return hi if e else jnp.logical_not(hi)


def _feature_lane(e, t):
    return HEAD_DIM * (1 - e) + t


def _feature_tables():
    wide = FOX_HEADS * LANE
    place_q, place_k = np.zeros((N_SPLIT * LANE, wide), np.float32), np.zeros((N_SPLIT * LANE, wide), np.float32)
    ones_q, ones_k, ones_v, own = (np.zeros((1, wide), np.float32) for _ in range(4))
    for h in range(FOX_HEADS):
        e = h % 2
        own[0, h * LANE + HEAD_DIM * e:h * LANE + HEAD_DIM * (e + 1)] = 1.0
        ones_v[0, h * LANE + _feature_lane(e, 0)] = 1.0
        for t in range(N_SPLIT):
            place_q[t * LANE + h, h * LANE + _feature_lane(e, t)] = 1.0
            ones_q[0, h * LANE + _feature_lane(e, N_SPLIT + t)] = 1.0
            ones_k[0, h * LANE + _feature_lane(e, t)] = 1.0
            place_k[t * LANE + h, h * LANE + _feature_lane(e, N_SPLIT + t)] = -1.0
    return tuple(jnp.asarray(a) for a in (place_q, place_k, ones_q, ones_k, ones_v, own))


def _fox_prep(zq, c, tm=256):
    s_len = zq.shape[0]
    tm = _div(s_len, tm, 8)
    wide = FOX_HEADS * LANE

    def body(z_ref, c_ref, pq_ref, pk_ref, oq_ref, ok_ref, ov_ref, own_ref, qx_ref, kx_ref, vx_ref):
        rest = c_ref[...]
        parts = []
        for _ in range(N_SPLIT):
            part = rest.astype(BF16).astype(F32)
            rest = rest - part
            parts.append(part)
        parts = jnp.concatenate(parts, axis=1)
        qf = jnp.dot(parts, pq_ref[...], preferred_element_type=F32) + oq_ref[...]
        kf = jnp.dot(parts, pk_ref[...], preferred_element_type=F32) + ok_ref[...]
        own = own_ref[...] > 0.5
        for p in range(N_PAIR):
            cols = slice(2 * p * LANE, (2 * p + 2) * LANE)
            pair = lambda col: jnp.tile(z_ref[:, (col + p) * LANE:(col + p + 1) * LANE].astype(F32), (1, 2))
            qx_ref[:, cols] = jnp.where(own[:, cols], pair(QB_COL) * SCALE, qf[:, cols]).astype(BF16)
            kx_ref[:, cols] = jnp.where(own[:, cols], pair(KB_COL), kf[:, cols]).astype(BF16)
            vx_ref[:, cols] = jnp.where(own[:, cols], pair(VB_COL), ov_ref[:, cols]).astype(BF16)

    out = jax.ShapeDtypeStruct((s_len, wide), BF16)
    blk = pl.BlockSpec((tm, wide), lambda i: (i, 0))
    table = pl.BlockSpec((N_SPLIT * LANE, wide), lambda i: (0, 0))
    vec = pl.BlockSpec((1, wide), lambda i: (0, 0))
    return pl.pallas_call(
        body, name="fox_prep", grid=(s_len // tm,),
        in_specs=[pl.BlockSpec((tm, N_QKV), lambda i: (i, 0)), pl.BlockSpec((tm, LANE), lambda i: (i, 0)),
                  table, table, vec, vec, vec, vec],
        out_specs=[blk, blk, blk], out_shape=[out, out, out],
        compiler_params=_params("parallel"),
    )(zq, c, *_feature_tables())


def _comm_parts(comm):
    return ([], [], [], []) if comm is None else (comm.specs, comm.out_shape, comm.scratch, comm.arrays)


def _fox_fwd(qx, kx, vx, comm=None, t_cap=1024):
    s_len = qx.shape[0]
    t = _div(s_len, t_cap, LANE)
    nq = s_len // t
    c_specs, c_shapes, c_scratch, c_arrays = _comm_parts(comm)

    def compute(q_ref, k_ref, v_ref, o_ref, o32_ref, m_ref, l_ref):
        i = pl.program_id(1)
        qs = [q_ref[:, e * LANE:(e + 1) * LANE] for e in range(2)]

        def step(j, carry, diag):
            rows = pl.ds(pl.multiple_of(j * t, t), t)
            new = []
            for e in range(2):
                m, acc = carry[e]
                s2 = lax.dot_general(qs[e], k_ref[rows, e * LANE:(e + 1) * LANE], NT,
                                     preferred_element_type=F32) * LOG2E
                if diag:
                    s2 = jnp.where(_causal(t), s2, NEG_INF)
                mn = jnp.maximum(m, jnp.ceil(jnp.max(s2, axis=1, keepdims=True)))
                pe = jnp.exp2(s2 - mn).astype(BF16)
                acc = acc * jnp.exp2(m - mn) + jnp.dot(pe, v_ref[rows, e * LANE:(e + 1) * LANE],
                                                       preferred_element_type=F32)
                new.append((mn, acc))
            return tuple(new)

        init = (jnp.full((t, 1), NEG_INF, F32), jnp.zeros((t, LANE), F32))
        carry = lax.fori_loop(0, i, lambda j, c: step(j, c, False), (init, init))
        carry = step(i, carry, True)
        outs, ls = [], []
        for e in range(2):
            m, acc = carry[e]
            l = acc[:, _feature_lane(e, 0):_feature_lane(e, 0) + 1]
            outs.append(acc / l)
            ls.append(l)
        out = jnp.where(_own_half(1), outs[1], outs[0])
        o_ref[...] = out.astype(BF16)
        o32_ref[...] = out
        m_ref[...] = jnp.where(_own_half(1), carry[1][0], carry[0][0])
        l_ref[...] = jnp.where(_own_half(1), ls[1], ls[0])

    body = _with_comm(comm, 3, 4, lambda: (pl.program_id(0) == 0) & (pl.program_id(1) == 0),
                      lambda: (pl.program_id(0) == N_PAIR - 1) & (pl.program_id(1) == nq - 1), compute)
    pair = pl.BlockSpec((s_len, 2 * LANE), lambda p, i: (0, p))
    tile = pl.BlockSpec((t, LANE), lambda p, i: (i, p))
    wide = jax.ShapeDtypeStruct((s_len, FOX_W), F32)
    outs = pl.pallas_call(
        body, name="fox_fwd" if comm is None else "fox_fwd_comm%d" % comm.n, grid=(N_PAIR, nq),
        in_specs=[pl.BlockSpec((t, 2 * LANE), lambda p, i: (i, p)), pair, pair] + c_specs,
        out_specs=[tile, tile, tile, tile] + c_specs,
        out_shape=[jax.ShapeDtypeStruct((s_len, FOX_W), BF16), wide, wide, wide] + c_shapes,
        scratch_shapes=c_scratch,
        compiler_params=_params("arbitrary", "arbitrary"),
    )(qx, kx, vx, *c_arrays)
    return outs[0], outs[1], outs[2], outs[3], outs[4:]


def _fox_stats(o, do, m, l, tm=256):
    s_len = o.shape[0]
    tm = _div(s_len, tm, LANE)

    def body(o_ref, do_ref, m_ref, l_ref, dox_ref, st_ref):
        lane = lax.broadcasted_iota(jnp.int32, (1, LANE), 1)
        for p in range(N_PAIR):
            cols = slice(p * LANE, (p + 1) * LANE)
            dout = do_ref[:, cols]
            prod = o_ref[:, cols] * dout.astype(F32)
            shift = m_ref[:, cols]
            inv_l = 1.0 / l_ref[:, cols]
            st = jnp.zeros((tm, LANE), F32)
            for e in range(2):
                h = 2 * p + e
                dox_ref[:, h * LANE:(h + 1) * LANE] = jnp.where(_own_half(e), dout, jnp.zeros_like(dout))
                st = jnp.where(lane == e, shift[:, e * HEAD_DIM:e * HEAD_DIM + 1], st)
                delta = jnp.sum(jnp.where(_own_half(e), prod, 0.0), axis=1, keepdims=True)
                st = jnp.where(lane == 2 + e, delta, st)
                st = jnp.where(lane == 4 + e, inv_l[:, e * HEAD_DIM:e * HEAD_DIM + 1], st)
            st_ref[p] = st.T[:8, :]

    row = pl.BlockSpec((tm, FOX_W), lambda i: (i, 0))
    return pl.pallas_call(
        body, name="fox_stats", grid=(s_len // tm,), in_specs=[row, row, row, row],
        out_specs=[pl.BlockSpec((tm, FOX_HEADS * LANE), lambda i: (i, 0)),
                   pl.BlockSpec((N_PAIR, 8, tm), lambda i: (0, 0, i))],
        out_shape=[jax.ShapeDtypeStruct((s_len, FOX_HEADS * LANE), BF16),
                   jax.ShapeDtypeStruct((N_PAIR, 8, s_len), F32)],
        compiler_params=_params("parallel"),
    )(o, do, m, l)


def _fox_bwd(qx, kx, vx, dox, stats, comm=None, t_cap=1024):
    s_len = qx.shape[0]
    t = _div(s_len, t_cap, LANE)
    n = s_len // t
    c_specs, c_shapes, c_scratch, c_arrays = _comm_parts(comm)

    def compute(q_ref, do_ref, st_ref, k_ref, v_ref, dq_ref, dk_ref, dv_ref, dc_ref):
        j = pl.program_id(1)
        lane = lax.broadcasted_iota(jnp.int32, (1, LANE), 1)

        @pl.when(j == 0)
        def _():
            dq_ref[...] = jnp.zeros_like(dq_ref)

        ks = [k_ref[:, e * LANE:(e + 1) * LANE] for e in range(2)]
        vs = [v_ref[:, e * LANE:(e + 1) * LANE] for e in range(2)]
        ks_t = [k.astype(F32).T.astype(BF16) for k in ks]

        def step(i, carry, diag):
            rows = pl.ds(pl.multiple_of(i * t, t), t)
            new = []
            dq = jnp.zeros((LANE, t), F32)
            for e in range(2):
                dk, dv, dc = carry[e]
                q = q_ref[rows, e * LANE:(e + 1) * LANE]
                dout = do_ref[rows, e * LANE:(e + 1) * LANE]
                s_t = lax.dot_general(ks[e], q, NT, preferred_element_type=F32) * LOG2E
                if diag:
                    s_t = jnp.where(_causal(t, keys_first=True), s_t, NEG_INF)
                p_t = jnp.exp2(s_t - st_ref[0, e:e + 1, rows]).astype(BF16).astype(F32) * st_ref[0, 4 + e:5 + e, rows]
                dp_t = lax.dot_general(vs[e], dout, NT, preferred_element_type=F32)
                ds_f = p_t * (dp_t - st_ref[0, 2 + e:3 + e, rows])
                ds_t = ds_f.astype(BF16)
                dc = dc + jnp.sum(ds_f, axis=1, keepdims=True)
                dv = dv + jnp.dot(p_t.astype(BF16), dout, preferred_element_type=F32)
                dk = dk + jnp.dot(ds_t, q, preferred_element_type=F32)
                dq_e = jnp.dot(ks_t[e], ds_t, preferred_element_type=F32)
                dq = dq + jnp.where(_row_halves()[e], dq_e, 0.0)
                new.append((dk, dv, dc))
            dq_ref[rows, :] += dq.T * SCALE
            return tuple(new)

        zero = jnp.zeros((t, LANE), F32)
        init = (zero, zero, jnp.zeros((t, 1), F32))
        carry = step(j, (init, init), True)
        (dk0, dv0, dc0), (dk1, dv1, dc1) = lax.fori_loop(j + 1, n, lambda i, c: step(i, c, False), carry)
        dk_ref[...] = jnp.where(_own_half(1), dk1, dk0).astype(BF16)
        dv_ref[...] = jnp.where(_own_half(1), dv1, dv0).astype(BF16)
        dc_ref[...] = jnp.where(lane == 0, -dc0, jnp.where(lane == 1, -dc1, 0.0))

    body = _with_comm(comm, 5, 4, lambda: (pl.program_id(0) == 0) & (pl.program_id(1) == 0),
                      lambda: (pl.program_id(0) == N_PAIR - 1) & (pl.program_id(1) == n - 1), compute)
    pair = pl.BlockSpec((s_len, 2 * LANE), lambda p, j: (0, p))
    blk = pl.BlockSpec((t, 2 * LANE), lambda p, j: (j, p))
    tile = pl.BlockSpec((t, LANE), lambda p, j: (j, p))
    outs = pl.pallas_call(
        body, name="fox_bwd" if comm is None else "fox_bwd_comm", grid=(N_PAIR, n),
        in_specs=[pair, pair, pl.BlockSpec((1, 8, s_len), lambda p, j: (p, 0, 0)), blk, blk] + c_specs,
        out_specs=[pl.BlockSpec((s_len, LANE), lambda p, j: (0, p)), tile, tile, tile] + c_specs,
        out_shape=[jax.ShapeDtypeStruct((s_len, FOX_W), F32), jax.ShapeDtypeStruct((s_len, FOX_W), BF16),
                   jax.ShapeDtypeStruct((s_len, FOX_W), BF16), jax.ShapeDtypeStruct((s_len, FOX_W), F32)] + c_shapes,
        scratch_shapes=c_scratch,
        compiler_params=_params("arbitrary", "arbitrary"),
    )(qx, dox, stats, kx, vx, *c_arrays)
    return outs[0], outs[1], outs[2], outs[3], outs[4:]


def _mixer_out(attn_a, attn_b, zg, h, wpa, wpb, wout, g, b, tm=256):
    m = h.shape[0]
    tm = _div(m, tm, 8)

    def body(a_ref, b_ref, ga_ref, gb_ref, h_ref, wpa_ref, wpb_ref, wout_ref, g_ref, bb_ref, h1_ref, u_ref, mg_ref):
        ya = jnp.dot(a_ref[...], wpa_ref[...], preferred_element_type=F32)
        yb = jnp.dot(b_ref[...], wpb_ref[...], preferred_element_type=F32)
        merged = (jax.nn.sigmoid(ga_ref[...]) * ya + jax.nn.sigmoid(gb_ref[...]) * yb).astype(BF16)
        u = ALPHA * h_ref[...] + jnp.dot(merged, wout_ref[...], preferred_element_type=F32)
        u_ref[...] = u
        h1_ref[...] = _ln(u, g_ref[...], bb_ref[...])
        mg_ref[...] = merged

    row = pl.BlockSpec((tm, D_MODEL), lambda i: (i, 0))
    att = pl.BlockSpec((tm, SWA_Q), lambda i: (i, 0))
    vec = pl.BlockSpec((1, D_MODEL), lambda i: (0, 0))
    wsm = pl.BlockSpec((SWA_Q, D_MODEL), lambda i: (0, 0))
    return pl.pallas_call(
        body, name="mixer_out", grid=(m // tm,),
        in_specs=[att, att, row, pl.BlockSpec((tm, D_MODEL), lambda i: (i, 1)), row, wsm, wsm,
                  pl.BlockSpec((D_MODEL, D_MODEL), lambda i: (0, 0)), vec, vec],
        out_specs=[row, row, row],
        out_shape=[jax.ShapeDtypeStruct((m, D_MODEL), F32), jax.ShapeDtypeStruct((m, D_MODEL), F32),
                   jax.ShapeDtypeStruct((m, D_MODEL), BF16)],
        compiler_params=_params("parallel"),
    )(attn_a, attn_b, zg, zg, h, wpa, wpb, wout, g, b)


def _mixer_bwd(dh1, u1, g, wout, attn_a, attn_b, zg, wpa, wpb, tm=256):
    m = dh1.shape[0]
    tm = _div(m, tm, 8)

    def body(dh_ref, u_ref, g_ref, wout_ref, a_ref, b_ref, ga_ref, gb_ref, wpa_ref, wpb_ref,
             du_ref, dg_ref, db_ref, dya_ref, dyb_ref, dga_ref, dgb_ref, da_ref, dbb_ref):
        @pl.when(pl.program_id(0) == 0)
        def _():
            dg_ref[...] = jnp.zeros_like(dg_ref)
            db_ref[...] = jnp.zeros_like(db_ref)

        du, dg, db = _ln_bwd_block(dh_ref[...], u_ref[...], g_ref[...])
        du_ref[...] = du
        dg_ref[...] += dg
        db_ref[...] += db
        dm = lax.dot_general(du.astype(BF16), wout_ref[...], (((1,), (1,)), ((), ())), preferred_element_type=F32)
        for x_ref, gate_ref, w_ref, dy_ref, dgate_ref, dattn_ref in (
                (a_ref, ga_ref, wpa_ref, dya_ref, dga_ref, da_ref), (b_ref, gb_ref, wpb_ref, dyb_ref, dgb_ref, dbb_ref)):
            sg = jax.nn.sigmoid(gate_ref[...])
            dy = (dm * sg).astype(BF16)
            dy_ref[...] = dy
            y = jnp.dot(x_ref[...], w_ref[...], preferred_element_type=F32)
            dgate_ref[...] = (dm * y * sg * (1.0 - sg)).astype(BF16)
            dattn_ref[...] = lax.dot_general(dy, w_ref[...], (((1,), (1,)), ((), ())),
                                             preferred_element_type=F32).astype(BF16)

    row = pl.BlockSpec((tm, D_MODEL), lambda i: (i, 0))
    att = pl.BlockSpec((tm, SWA_Q), lambda i: (i, 0))
    vec = pl.BlockSpec((1, D_MODEL), lambda i: (0, 0))
    wsm = pl.BlockSpec((SWA_Q, D_MODEL), lambda i: (0, 0))
    wide = jax.ShapeDtypeStruct((m, D_MODEL), BF16)
    narrow = jax.ShapeDtypeStruct((m, SWA_Q), BF16)
    sums = jax.ShapeDtypeStruct((1, D_MODEL), F32)
    return pl.pallas_call(
        body, name="mixer_bwd", grid=(m // tm,),
        in_specs=[row, row, vec, pl.BlockSpec((D_MODEL, D_MODEL), lambda i: (0, 0)), att, att, row,
                  pl.BlockSpec((tm, D_MODEL), lambda i: (i, 1)), wsm, wsm],
        out_specs=[row, vec, vec, row, row, row, row, att, att],
        out_shape=[jax.ShapeDtypeStruct((m, D_MODEL), F32), sums, sums, wide, wide, wide, wide, narrow, narrow],
        compiler_params=_params("arbitrary"),
    )(dh1, u1, g, wout, attn_a, attn_b, zg, zg, wpa, wpb)


def _shift_down(x, k, halo, first):
    rows = lax.broadcasted_iota(jnp.int32, (x.shape[0], 1), 0)
    y = pltpu.roll(x, k, 0)
    for r in range(k):
        fill = jnp.where(first, 0.0, halo[8 - k + r:8 - k + r + 1, :])
        y = jnp.where(rows == r, fill, y)
    return y


def _shift_up(x, k, halo, last):
    n = x.shape[0]
    rows = lax.broadcasted_iota(jnp.int32, (n, 1), 0)
    y = pltpu.roll(x, n - k, 0)
    for r in range(k):
        fill = jnp.where(last, 0.0, halo[r:r + 1, :])
        y = jnp.where(rows == n - k + r, fill, y)
    return y


def _conv_act(gate, gate_m1, gate_m2, cw, cb):
    return cb + cw[0:1, :] * gate_m2 + cw[1:2, :] * gate_m1 + cw[2:3, :] * gate


def _ffn_in_conv(h1, wfi, cw, cb, tm=256):
    s_len = h1.shape[0]
    tm = _div(s_len, tm, 8)
    hb = tm // 8

    def body(a_ref, ap_ref, w_ref, cw_ref, cb_ref, gu_ref, act_ref):
        first = pl.program_id(0) == 0
        a = a_ref[...].astype(BF16)
        before = ap_ref[...].astype(BF16)
        for c in range(N_CHUNK):
            gate = jnp.dot(a, w_ref[c], preferred_element_type=F32)
            up = jnp.dot(a, w_ref[N_CHUNK + c], preferred_element_type=F32)
            halo = jnp.dot(before, w_ref[c], preferred_element_type=F32)
            gu_ref[c, 0] = gate
            gu_ref[c, 1] = up
            conv = _conv_act(gate, _shift_down(gate, 1, halo, first), _shift_down(gate, 2, halo, first),
                             cw_ref[c], cb_ref[c])
            act_ref[c] = (conv * jax.nn.sigmoid(conv) * up).astype(BF16)

    return pl.pallas_call(
        body, name="ffn_in_conv", grid=(s_len // tm,),
        in_specs=[pl.BlockSpec((tm, D_MODEL), lambda i: (i, 0)),
                  pl.BlockSpec((8, D_MODEL), lambda i: (jnp.maximum(i * hb - 1, 0), 0)),
                  pl.BlockSpec((N_DEV, D_MODEL, FF_CHUNK), lambda i: (0, 0, 0)),
                  pl.BlockSpec((N_CHUNK, 8, FF_CHUNK), lambda i: (0, 0, 0)),
                  pl.BlockSpec((N_CHUNK, 1, FF_CHUNK), lambda i: (0, 0, 0))],
        out_specs=[pl.BlockSpec((N_CHUNK, 2, tm, FF_CHUNK), lambda i: (0, 0, i, 0)),
                   pl.BlockSpec((N_CHUNK, tm, FF_CHUNK), lambda i: (0, i, 0))],
        out_shape=[jax.ShapeDtypeStruct((N_CHUNK, 2, s_len, FF_CHUNK), F32),
                   jax.ShapeDtypeStruct((N_CHUNK, s_len, FF_CHUNK), BF16)],
        compiler_params=_params("parallel"),
    )(h1, h1, wfi, cw, cb)


def _ffn_out_ln(act, wfo, res, g, b, target=None, tm=512):
    s_len = res.shape[0]
    tm = _div(s_len, tm, 8)
    last = target is not None

    def body(a_ref, w_ref, res_ref, g_ref, b_ref, *rest):
        u = ALPHA * res_ref[...]
        for c in range(N_CHUNK):
            u = u + jnp.dot(a_ref[c], w_ref[c], preferred_element_type=F32)
        y = _ln(u, g_ref[...], b_ref[...])
        if not last:
            u_ref, y_ref = rest
            u_ref[...] = u
            y_ref[...] = y
            return
        t_ref, u_ref, dy_ref, loss_ref = rest
        u_ref[...] = u

        @pl.when(pl.program_id(0) == 0)
        def _():
            loss_ref[...] = jnp.zeros_like(loss_ref)

        err = y - t_ref[...]
        dy_ref[...] = err / D_MODEL
        loss_ref[...] += 0.5 * jnp.sum(jnp.sum(err * err, axis=1, keepdims=True) / D_MODEL, axis=0, keepdims=True)

    row = pl.BlockSpec((tm, D_MODEL), lambda i: (i, 0))
    vec = pl.BlockSpec((1, D_MODEL), lambda i: (0, 0))
    wide = jax.ShapeDtypeStruct((s_len, D_MODEL), F32)
    return pl.pallas_call(
        body, name="ffn_out_ln_loss" if last else "ffn_out_ln", grid=(s_len // tm,),
        in_specs=[pl.BlockSpec((N_CHUNK, tm, FF_CHUNK), lambda i: (0, i, 0)),
                  pl.BlockSpec((N_CHUNK, FF_CHUNK, D_MODEL), lambda i: (0, 0, 0)), row, vec, vec] + [row] * last,
        out_specs=[row, row] + [pl.BlockSpec((8, LANE), lambda i: (0, 0))] * last,
        out_shape=[wide, wide] + [jax.ShapeDtypeStruct((8, LANE), F32)] * last,
        compiler_params=_params("arbitrary" if last else "parallel"),
    )(act, wfo, res, g, b, *([target] if last else []))


def _ffn_out_bwd(dh2, u2, g, wfo, tm=512):
    s_len = dh2.shape[0]
    tm = _div(s_len, tm, 8)

    def body(dh_ref, u_ref, g_ref, w_ref, du_ref, dg_ref, db_ref, o_ref):
        @pl.when(pl.program_id(0) == 0)
        def _():
            dg_ref[...] = jnp.zeros_like(dg_ref)
            db_ref[...] = jnp.zeros_like(db_ref)

        du, dg, db = _ln_bwd_block(dh_ref[...], u_ref[...], g_ref[...])
        du_ref[...] = du
        dg_ref[...] += dg
        db_ref[...] += db
        du_b = du.astype(BF16)
        for c in range(N_CHUNK):
            o_ref[c] = lax.dot_general(du_b, w_ref[c], (((1,), (1,)), ((), ())), preferred_element_type=F32)

    row = pl.BlockSpec((tm, D_MODEL), lambda i: (i, 0))
    vec = pl.BlockSpec((1, D_MODEL), lambda i: (0, 0))
    sums = jax.ShapeDtypeStruct((1, D_MODEL), F32)
    return pl.pallas_call(
        body, name="ffn_out_bwd", grid=(s_len // tm,),
        in_specs=[row, row, vec, pl.BlockSpec((N_CHUNK, FF_CHUNK, D_MODEL), lambda i: (0, 0, 0))],
        out_specs=[row, vec, vec, pl.BlockSpec((N_CHUNK, tm, FF_CHUNK), lambda i: (0, i, 0))],
        out_shape=[jax.ShapeDtypeStruct((s_len, D_MODEL), F32), sums, sums,
                   jax.ShapeDtypeStruct((N_CHUNK, s_len, FF_CHUNK), F32)],
        compiler_params=_params("arbitrary"),
    )(dh2, u2, g, wfo)


def _g_w_ffn_out(act, du, tm=2048):
    s_len = du.shape[0]
    tm = _div(s_len, tm, 8)
    steps = s_len // tm

    def body(a_ref, g_ref, o_ref, acc_ref):
        s = pl.program_id(1)

        @pl.when(s == 0)
        def _():
            acc_ref[...] = jnp.zeros_like(acc_ref)

        acc_ref[...] += lax.dot_general(a_ref[0], g_ref[...].astype(BF16), (((0,), (0,)), ((), ())),
                                        preferred_element_type=F32)

        @pl.when(s == steps - 1)
        def _():
            o_ref[0] = acc_ref[...].astype(BF16)

    return pl.pallas_call(
        body, name="g_w_ffn_out", grid=(N_CHUNK, steps),
        in_specs=[pl.BlockSpec((1, tm, FF_CHUNK), lambda c, s: (c, s, 0)),
                  pl.BlockSpec((tm, D_MODEL), lambda c, s: (s, 0))],
        out_specs=pl.BlockSpec((1, FF_CHUNK, D_MODEL), lambda c, s: (c, 0, 0)),
        out_shape=jax.ShapeDtypeStruct((N_CHUNK, FF_CHUNK, D_MODEL), BF16),
        scratch_shapes=[pltpu.VMEM((FF_CHUNK, D_MODEL), F32)],
        compiler_params=_params("parallel", "arbitrary"),
    )(act, du)


def _g_w_ffn_in(h1, dgu, tm=2048):
    s_len = h1.shape[0]
    tm = _div(s_len, tm, 8)
    steps = s_len // tm

    def body(a_ref, g_ref, o_ref, acc_ref):
        s = pl.program_id(1)

        @pl.when(s == 0)
        def _():
            acc_ref[...] = jnp.zeros_like(acc_ref)

        acc_ref[...] += lax.dot_general(g_ref[0, 0], a_ref[...].astype(BF16), (((0,), (0,)), ((), ())),
                                        preferred_element_type=F32)

        @pl.when(s == steps - 1)
        def _():
            o_ref[0] = acc_ref[...].astype(BF16)

    return pl.pallas_call(
        body, name="g_w_ffn_in", grid=(N_DEV, steps),
        in_specs=[pl.BlockSpec((tm, D_MODEL), lambda d, s: (s, 0)),
                  pl.BlockSpec((1, 1, tm, FF_CHUNK), lambda d, s: (d % N_CHUNK, d // N_CHUNK, s, 0))],
        out_specs=pl.BlockSpec((1, FF_CHUNK, D_MODEL), lambda d, s: (d, 0, 0)),
        out_shape=jax.ShapeDtypeStruct((N_DEV, FF_CHUNK, D_MODEL), BF16),
        scratch_shapes=[pltpu.VMEM((FF_CHUNK, D_MODEL), F32)],
        compiler_params=_params("parallel", "arbitrary"),
    )(h1, dgu)


def _conv_bwd_dh1(gu, dact, cw, cb, wfi, res, tm=256):
    s_len = gu.shape[2]
    tm = _div(s_len, tm, 8)
    nrow = s_len // tm
    hb = tm // 8

    def dconv_of(conv, up, da):
        sg = jax.nn.sigmoid(conv)
        return da * up * (sg * (1.0 + conv * (1.0 - sg)))

    def body(gu_ref, gp_ref, gun_ref, da_ref, dan_ref, cw_ref, cb_ref, w_ref, res_ref, dgu_ref, dcw_ref, dh_ref):
        i = pl.program_id(0)
        first = i == 0
        last = i == nrow - 1

        @pl.when(first)
        def _():
            dcw_ref[...] = jnp.zeros_like(dcw_ref)

        row = lax.broadcasted_iota(jnp.int32, (8, 1), 0)
        acc = ALPHA * res_ref[...]
        for c in range(N_CHUNK):
            cw = cw_ref[c]
            cb = cb_ref[c]
            gate = gu_ref[c, 0]
            halo = gp_ref[c, 0]
            g_m1 = _shift_down(gate, 1, halo, first)
            g_m2 = _shift_down(gate, 2, halo, first)
            conv = _conv_act(gate, g_m1, g_m2, cw, cb)
            da = da_ref[c]
            dup = (da * conv * jax.nn.sigmoid(conv)).astype(BF16)
            dconv = dconv_of(conv, gu_ref[c, 1], da)
            gate_n = gun_ref[c, 0]
            tail = gate[tm - 8:, :]
            conv_n = _conv_act(gate_n, _shift_down(gate_n, 1, tail, False), _shift_down(gate_n, 2, tail, False),
                               cw, cb)
            dconv_n = dconv_of(conv_n, gun_ref[c, 1], dan_ref[c])
            dgate = (cw[2:3, :] * dconv + cw[1:2, :] * _shift_up(dconv, 1, dconv_n, last)
                     + cw[0:1, :] * _shift_up(dconv, 2, dconv_n, last)).astype(BF16)
            dgu_ref[c, 0] = dgate
            dgu_ref[c, 1] = dup
            acc = acc + jnp.dot(dgate, w_ref[c], preferred_element_type=F32)
            acc = acc + jnp.dot(dup, w_ref[N_CHUNK + c], preferred_element_type=F32)
            part = jnp.zeros((8, FF_CHUNK), F32)
            for r, term in enumerate((dconv * g_m2, dconv * g_m1, dconv * gate, dconv)):
                part = jnp.where(row == r, jnp.sum(term, axis=0, keepdims=True), part)
            dcw_ref[c] += part
        dh_ref[...] = acc

    nxt = lambda i: jnp.minimum((i + 1) * hb, s_len // 8 - 1)
    main = pl.BlockSpec((N_CHUNK, 2, tm, FF_CHUNK), lambda i: (0, 0, i, 0))
    row_d = pl.BlockSpec((tm, D_MODEL), lambda i: (i, 0))
    return pl.pallas_call(
        body, name="conv_bwd_dh1", grid=(nrow,),
        in_specs=[main,
                  pl.BlockSpec((N_CHUNK, 1, 8, FF_CHUNK), lambda i: (0, 0, jnp.maximum(i * hb - 1, 0), 0)),
                  pl.BlockSpec((N_CHUNK, 2, 8, FF_CHUNK), lambda i: (0, 0, nxt(i), 0)),
                  pl.BlockSpec((N_CHUNK, tm, FF_CHUNK), lambda i: (0, i, 0)),
                  pl.BlockSpec((N_CHUNK, 8, FF_CHUNK), lambda i: (0, nxt(i), 0)),
                  pl.BlockSpec((N_CHUNK, 8, FF_CHUNK), lambda i: (0, 0, 0)),
                  pl.BlockSpec((N_CHUNK, 1, FF_CHUNK), lambda i: (0, 0, 0)),
                  pl.BlockSpec((N_DEV, FF_CHUNK, D_MODEL), lambda i: (0, 0, 0)), row_d],
        out_specs=[main, pl.BlockSpec((N_CHUNK, 8, FF_CHUNK), lambda i: (0, 0, 0)), row_d],
        out_shape=[jax.ShapeDtypeStruct((N_CHUNK, 2, s_len, FF_CHUNK), BF16),
                   jax.ShapeDtypeStruct((N_CHUNK, 8, FF_CHUNK), F32),
                   jax.ShapeDtypeStruct((s_len, D_MODEL), F32)],
        compiler_params=_params("arbitrary"),
    )(gu, gu, gu, dact, dact, cw, cb, wfi, res)


def _sum_devices(r_ref):
    acc = r_ref[0].astype(F32)
    for d in range(1, N_DEV):
        acc = acc + r_ref[d].astype(F32)
    return acc


def _sum8(recv):
    rows = recv.shape[1]
    tr = _div(rows, ROW_BLOCK, 8)

    def body(r_ref, o_ref):
        o_ref[...] = _sum_devices(r_ref)

    return pl.pallas_call(
        body, name="sum8", grid=(rows // tr,),
        in_specs=[pl.BlockSpec((N_DEV, tr, LANE), lambda i: (0, i, 0))],
        out_specs=pl.BlockSpec((tr, LANE), lambda i: (i, 0)),
        out_shape=jax.ShapeDtypeStruct((rows, LANE), F32),
        compiler_params=_params("parallel"),
    )(recv)


def _adamw_math(w, g, m, v):
    m = ADAM_B1 * m + (1.0 - ADAM_B1) * g
    v = ADAM_B2 * v + (1.0 - ADAM_B2) * (g * g)
    m_hat = m / (1.0 - ADAM_B1 ** ADAM_STEP)
    v_hat = v / (1.0 - ADAM_B2 ** ADAM_STEP)
    return -ADAM_LR * (m_hat / (jnp.sqrt(v_hat) + ADAM_EPS) + ADAM_WD * w), m, v


def _adamw_rows(w, g, m, v, name):
    rows = w.shape[0]
    tr = _div(rows, ROW_BLOCK, 8)

    def body(w_ref, g_ref, m_ref, v_ref, d_ref, mo_ref, vo_ref):
        d_ref[...], mo_ref[...], vo_ref[...] = _adamw_math(w_ref[...], g_ref[...], m_ref[...], v_ref[...])

    blk = pl.BlockSpec((tr, LANE), lambda i: (i, 0))
    out = jax.ShapeDtypeStruct((rows, LANE), F32)
    return pl.pallas_call(
        body, name=name, grid=(rows // tr,), in_specs=[blk, blk, blk, blk], out_specs=[blk, blk, blk],
        out_shape=[out, out, out], compiler_params=_params("parallel"),
    )(w, g, m, v)


def _adamw_w_in(recv, w, m, v, tl=128):
    n, depth, d = w.shape

    def body(*refs):
        r_refs, (w_ref, m_ref, v_ref), (g_ref, d_ref, mo_ref, vo_ref) = refs[:depth], refs[depth:depth + 3], refs[-4:]
        for l in range(depth):
            g = _sum_devices(r_refs[l])
            g_ref[:, l, :] = g
            d_ref[:, l, :], mo_ref[:, l, :], vo_ref[:, l, :] = _adamw_math(w_ref[:, l, :], g, m_ref[:, l, :],
                                                                            v_ref[:, l, :])

    blk = pl.BlockSpec((n, depth, tl), lambda j: (0, 0, j))
    out = jax.ShapeDtypeStruct((n, depth, d), F32)
    return pl.pallas_call(
        body, name="adamw_w_in", grid=(d // tl,),
        in_specs=[pl.BlockSpec((N_DEV, n, tl), lambda j: (0, 0, j))] * depth + [blk, blk, blk],
        out_specs=[blk, blk, blk, blk], out_shape=[out, out, out, out],
        compiler_params=_params("parallel"),
    )(*recv, w, m, v)


def _adamw_shard(recv, w, m, v, layer, prev, name):
    _, k, n = recv.shape
    tk = _div(k, 128, 16)

    def body(r_ref, w_ref, m_ref, v_ref, *rest):
        g_ref, d_ref, mo_ref, vo_ref = rest[-4:]
        g = _sum_devices(r_ref)
        g_ref[0] = g
        d_ref[0], mo_ref[0], vo_ref[0] = _adamw_math(w_ref[0], g, m_ref[0], v_ref[0])

    blk = pl.BlockSpec((1, tk, n), lambda i: (layer, i, 0))
    out = jax.ShapeDtypeStruct((DEPTH, k, n), F32)
    carried = [] if prev is None else list(prev)
    return pl.pallas_call(
        body, name=name, grid=(k // tk,),
        in_specs=[pl.BlockSpec((N_DEV, tk, n), lambda i: (0, i, 0)), blk, blk, blk]
        + [pl.BlockSpec(memory_space=pl.ANY)] * len(carried),
        out_specs=[blk, blk, blk, blk], out_shape=[out, out, out, out],
        input_output_aliases={4 + j: j for j in range(len(carried))},
        compiler_params=_params("parallel"),
    )(recv, w, m, v, *carried)


def _to_rows(flat, rows):
    flat = flat.reshape(-1)
    return jnp.pad(flat, (0, rows * LANE - flat.shape[0])).reshape(rows, LANE)


def _pad_z(a, axis):
    f0 = N_QKV
    g0 = N_QKV + FOX_HEADS
    take = lambda lo, hi: lax.slice_in_dim(a, lo, hi, axis=axis)
    shape = list(a.shape)
    shape[axis] = F_PAD - FOX_HEADS
    return jnp.concatenate([take(0, f0), take(g0, N_IN), take(f0, g0), jnp.zeros(shape, a.dtype)], axis=axis)


def _unpad_z(a, axis):
    f0 = N_QKV + N_GATE
    take = lambda lo, hi: lax.slice_in_dim(a, lo, hi, axis=axis)
    return jnp.concatenate([take(0, N_QKV), take(f0, f0 + FOX_HEADS), take(N_QKV, f0)], axis=axis)


def _shards_to_cols(g):
    _, k, n = g.shape
    return g.transpose(1, 0, 2).reshape(k, N_DEV * n)


def _cols_to_shards(full):
    k, n = full.shape
    return full.reshape(k, N_DEV, n // N_DEV).transpose(1, 0, 2)


def _layer_fwd(h, w, p, comm=None, late=None, target=None):
    zq, zg = _z_proj(h, w["w_in_p"], p["b_in_p"])
    qx, kx, vx = _fox_prep(zq, _cumsum_logf(zg))
    attn_a, lse_a = _swa_fwd(zq, p["sinks"])
    attn_b, attn_b32, m_b, l_b, arrived = _fox_fwd(qx, kx, vx, comm)
    if late is not None:
        w, p = late(w, p, arrived)
    h1, u1, merged = _mixer_out(attn_a, attn_b, zg, h, w["w_proj_a"], w["w_proj_b"], w["w_out"],
                                        p["ln_mix_g"], p["ln_mix_b"])
    gu, act = _ffn_in_conv(h1, w["w_ffn_in_fwd"], p["conv_w"], p["conv_b"])
    u2, *h2 = _ffn_out_ln(act, w["w_ffn_out"], h1, p["ln_ffn_g"], p["ln_ffn_b"], target)
    saved = dict(h=h, zq=zq, zg=zg, qx=qx, kx=kx, vx=vx, attn_a=attn_a, lse_a=lse_a, attn_b=attn_b,
                 attn_b32=attn_b32, m_b=m_b, l_b=l_b, h1=h1, u1=u1, merged=merged, gu=gu, act=act, u2=u2)
    return h2, saved, w, p


def _layer_bwd(dh2, sv, w, p, make_comm=None, make_last_comm=None):
    s_len = dh2.shape[0]
    du2, d_ffn_g, d_ffn_b, dact = _ffn_out_bwd(dh2, sv["u2"], p["ln_ffn_g"], w["w_ffn_out"])
    g_ffn_out = _g_w_ffn_out(sv["act"], du2)
    dgu, dcw, dh1 = _conv_bwd_dh1(sv["gu"], dact, p["conv_w"], p["conv_b"], w["w_ffn_in"], du2)
    dcw = dcw.transpose(1, 0, 2).reshape(8, D_FF)
    g_ffn_in = _g_w_ffn_in(sv["h1"], dgu)
    du1, d_mix_g, d_mix_b, dya, dyb, dga, dgb, dattn_a, dattn_b = _mixer_bwd(
        dh1, sv["u1"], p["ln_mix_g"], w["w_out"], sv["attn_a"], sv["attn_b"], sv["zg"], w["w_proj_a"], w["w_proj_b"])
    g_out = _linear_tn(sv["merged"], du1, name="g_w_out", tn=1024)
    g_proj_a = _linear_tn(sv["attn_a"], dya, name="g_w_proj_a", tk=512, tn=1024)
    g_proj_b = _linear_tn(sv["attn_b"], dyb, name="g_w_proj_b", tk=512, tn=1024)
    dq_a, dk_a, dv_a, dsinks = _swa_bwd(sv["zq"], p["sinks"], sv["attn_a"], dattn_a, sv["lse_a"])
    big = dict(w_proj_a=_cols_to_shards(g_proj_a), w_proj_b=_cols_to_shards(g_proj_b),
               w_out=g_out.reshape(N_DEV, D_MODEL // N_DEV, D_MODEL), w_ffn_in=g_ffn_in,
               w_ffn_out=g_ffn_out.reshape(N_DEV, D_FF // N_DEV, D_MODEL))
    dox, stats = _fox_stats(sv["attn_b32"], dattn_b, sv["m_b"], sv["l_b"])
    dq_b, dk_b, dv_b, dcc, arrived = _fox_bwd(sv["qx"], sv["kx"], sv["vx"], dox, stats,
                                              None if make_comm is None else make_comm(big))
    df = _forget_bwd(dcc, sv["zg"])
    dz = jnp.concatenate([dq_a, dk_a.astype(BF16), dv_a.astype(BF16), dq_b.astype(BF16), dk_b, dv_b, dga, dgb, df,
                          jnp.zeros((s_len, F_PAD - LANE), BF16)], axis=1)
    g_in_t, g_b_in = _linear_tn(dz, sv["h"], name="g_w_in", tk=768, tn=1024, colsum=True)
    g_in_t, g_b_in = _unpad_z(g_in_t, 0), _unpad_z(g_b_in, 1)
    big["w_in"] = g_in_t.reshape(N_DEV, N_IN // N_DEV, D_MODEL)
    small = dict(ln_mix_g=d_mix_g, ln_mix_b=d_mix_b, b_in=g_b_in, attn_sinks=dsinks[:, :SWA_HEADS],
                 ln_ffn_g=d_ffn_g, ln_ffn_b=d_ffn_b, conv_w=dcw[:3], conv_b=dcw[3:4])
    dh, arrived_last = _d_h(dz, w["w_in_p"], du1, None if make_last_comm is None else make_last_comm(big, small))
    return dh, big, small, arrived, arrived_last


def _w_in_layouts(w_in):
    return dict(w_in_p=_pad_z(w_in.reshape(N_IN, D_MODEL), 0))


def _other_layouts(w_proj_a, w_proj_b, w_out, w_ffn_in, w_ffn_out):
    return dict(w_proj_a=_shards_to_cols(w_proj_a), w_proj_b=_shards_to_cols(w_proj_b),
                w_out=w_out.reshape(D_MODEL, D_MODEL), w_ffn_in=w_ffn_in,
                w_ffn_in_fwd=w_ffn_in.transpose(0, 2, 1),
                w_ffn_out=w_ffn_out.reshape(N_CHUNK, FF_CHUNK, D_MODEL))


def _layer_params(r):
    return dict(
        b_in_p=_pad_z(r["b_in"].reshape(1, N_IN), 1),
        sinks=jnp.pad(r["attn_sinks"].reshape(1, SWA_HEADS), ((0, 0), (0, LANE - SWA_HEADS))),
        ln_mix_g=r["ln_mix_g"].reshape(1, D_MODEL), ln_mix_b=r["ln_mix_b"].reshape(1, D_MODEL),
        ln_ffn_g=r["ln_ffn_g"].reshape(1, D_MODEL), ln_ffn_b=r["ln_ffn_b"].reshape(1, D_MODEL),
        conv_b=r["conv_b"].reshape(N_CHUNK, 1, FF_CHUNK))


def _conv_w_layout(conv_w):
    return jnp.pad(conv_w, ((0, 5), (0, 0))).reshape(8, N_CHUNK, FF_CHUNK).transpose(1, 0, 2)


def kernel(x, ln_mix_g, ln_mix_b, w_in, b_in, attn_sinks, w_proj_a, w_proj_b, w_out, ln_ffn_g, ln_ffn_b, w_ffn_in, conv_w, conv_b, w_ffn_out, loss_target, m_ln_mix_g, m_ln_mix_b, m_w_in, m_b_in, m_attn_sinks, m_w_proj_a, m_w_proj_b, m_w_out, m_ln_ffn_g, m_ln_ffn_b, m_w_ffn_in, m_conv_w, m_conv_b, m_w_ffn_out, v_ln_mix_g, v_ln_mix_b, v_w_in, v_b_in, v_attn_sinks, v_w_proj_a, v_w_proj_b, v_w_out, v_ln_ffn_g, v_ln_ffn_b, v_w_ffn_in, v_conv_w, v_conv_b, v_w_ffn_out):
    wts = dict(ln_mix_g=ln_mix_g, ln_mix_b=ln_mix_b, w_in=w_in, b_in=b_in, attn_sinks=attn_sinks, w_proj_a=w_proj_a,
               w_proj_b=w_proj_b, w_out=w_out, ln_ffn_g=ln_ffn_g, ln_ffn_b=ln_ffn_b, w_ffn_in=w_ffn_in,
               conv_w=conv_w, conv_b=conv_b, w_ffn_out=w_ffn_out)
    mom = dict(ln_mix_g=m_ln_mix_g, ln_mix_b=m_ln_mix_b, w_in=m_w_in, b_in=m_b_in, attn_sinks=m_attn_sinks,
               w_proj_a=m_w_proj_a, w_proj_b=m_w_proj_b, w_out=m_w_out, ln_ffn_g=m_ln_ffn_g, ln_ffn_b=m_ln_ffn_b,
               w_ffn_in=m_w_ffn_in, conv_w=m_conv_w, conv_b=m_conv_b, w_ffn_out=m_w_ffn_out)
    vel = dict(ln_mix_g=v_ln_mix_g, ln_mix_b=v_ln_mix_b, w_in=v_w_in, b_in=v_b_in, attn_sinks=v_attn_sinks,
               w_proj_a=v_w_proj_a, w_proj_b=v_w_proj_b, w_out=v_w_out, ln_ffn_g=v_ln_ffn_g, ln_ffn_b=v_ln_ffn_b,
               w_ffn_in=v_w_ffn_in, conv_w=v_conv_w, conv_b=v_conv_b, w_ffn_out=v_w_ffn_out)
    names = list(wts)
    big_names = [n for n, _, _ in BIG]
    small_names = [n for n, _ in SMALL]
    me = 4 * lax.axis_index("x") + 2 * lax.axis_index("y") + lax.axis_index("c")
    cw_shard = D_FF // N_DEV

    stored = {"w_in": ((2, 0, 1), (1, 2, 0)), "w_ffn_in": ((0, 2, 1), (0, 2, 1))}
    as_stored = lambda tree: {n: jnp.transpose(tree[n], stored[n][0]) if n in stored else tree[n] for n in big_names}
    w_st, m_st, v_st = as_stored(wts), as_stored(mom), as_stored(vel)
    wb = {n: [(w_st[n][:, l] if n == "w_in" else w_st[n][l]).astype(BF16) for l in range(DEPTH)] for n in big_names}
    ps = [_layer_params(dict(b_in=b_in[l], attn_sinks=attn_sinks[l], ln_mix_g=ln_mix_g[l], ln_mix_b=ln_mix_b[l],
                             ln_ffn_g=ln_ffn_g[l], ln_ffn_b=ln_ffn_b[l], conv_b=conv_b[l])) for l in range(DEPTH)]
    w_in_0, = _exchange([(wb["w_in"][0], True)], "gather_w_in_0")
    others = big_names[1:]
    next_layer = {}

    def late_0(w, p, arrived):
        conv_full = arrived[-1].transpose(1, 2, 0, 3).reshape(DEPTH, 3, D_FF)
        next_layer["w"] = _w_in_layouts(arrived[len(others)])
        next_layer["p"] = dict(ps[1], conv_w=_conv_w_layout(conv_full[1]))
        return dict(w, **_other_layouts(*arrived[:len(others)])), dict(p, conv_w=_conv_w_layout(conv_full[0]))

    def late_1(w, p, arrived):
        return dict(w, **_other_layouts(*arrived)), p

    saved, ws = [None] * DEPTH, [None] * DEPTH
    gather_0 = _Comm([(wb[n][0], True) for n in others] + [(wb["w_in"][1], True), (conv_w, True)])
    (h,), saved[0], ws[0], ps[0] = _layer_fwd(x[0], _w_in_layouts(w_in_0), ps[0], gather_0, late_0)
    gather_1 = _Comm([(wb[n][1], True) for n in others])
    (dh, loss_part), saved[1], ws[1], ps[1] = _layer_fwd(h, next_layer["w"], next_layer["p"], gather_1, late_1,
                                                         loss_target[0])

    def small_rows(small):
        vec = jnp.concatenate([small[n].reshape(-1) for n in small_names] + [loss_part[0, 0].reshape(1)])
        return _to_rows(vec, SMALL_LAYER_ROWS)

    dh, big_1, small_1, _, _ = _layer_bwd(dh, saved[1], ws[1], ps[1])

    def exchange_early(big_0):
        return _Comm([(big_1[n].astype(BF16), False) for n in big_names] + [(small_rows(small_1), True)]
                     + [(big_0[n].astype(BF16), False) for n in others])

    def exchange_last(big_0, small_0):
        return _Comm([(big_0["w_in"].astype(BF16), False), (small_rows(small_0), True)])

    grad_x, _, _, arrived, (g_in_0, g_small_0) = _layer_bwd(dh, saved[0], ws[0], ps[0], exchange_early, exchange_last)
    n_big = len(big_names)
    recv = [[g_in_0] + list(arrived[n_big + 1:]) + [g_small_0], list(arrived[:n_big + 1])]

    big_out = {"w_in": _adamw_w_in([recv[l][0] for l in range(DEPTH)], w_st["w_in"], m_st["w_in"], v_st["w_in"])}
    for t, n in enumerate(big_names):
        if n == "w_in":
            continue
        outs = None
        for l in reversed(range(DEPTH)):
            outs = _adamw_shard(recv[l][t], w_st[n], m_st[n], v_st[n], l, outs, "adamw_%s_%d" % (n, l))
        big_out[n] = outs
    for n, (_, back) in stored.items():
        big_out[n] = [jnp.transpose(a, back) for a in big_out[n]]
    small_sum = [_sum8(recv[l][-1]).reshape(-1) for l in range(DEPTH)]
    g_small = {}
    off = 0
    for n, size in SMALL:
        g_small[n] = jnp.stack([small_sum[l][off:off + size] for l in range(DEPTH)])
        off += size
    loss = small_sum[0][off]
    g_small["conv_w"] = lax.dynamic_slice_in_dim(g_small["conv_w"].reshape(DEPTH, 3, D_FF), me * cw_shard, cw_shard,
                                                 axis=2)
    g_small = {n: g_small[n].reshape(wts[n].shape) for n in small_names}

    def pack_small(tree):
        return _to_rows(jnp.concatenate([tree[n].reshape(-1) for n in small_names]), SMALL_ROWS)

    small_out = (pack_small(g_small),) + tuple(_adamw_rows(pack_small(wts), pack_small(g_small), pack_small(mom),
                                                           pack_small(vel), "adamw_small"))

    def result(j):
        out = {n: big_out[n][j] for n in big_names}
        flat = small_out[j].reshape(-1)
        off = 0
        for n in small_names:
            out[n] = flat[off:off + wts[n].size].reshape(wts[n].shape)
            off += wts[n].size
        return [out[n] for n in names]

    return (loss, grad_x[None], *result(0), *result(1), *result(2), *result(3))
```

```python
import functools

import jax
import jax.numpy as jnp
import numpy as np
from jax import lax
from jax.experimental import pallas as pl
from jax.experimental.pallas import tpu as pltpu

F32 = jnp.float32
BF16 = jnp.bfloat16
MESH = pl.DeviceIdType.MESH

N_DEV = 8
DEPTH = 2
D_MODEL = 1024
HEAD_DIM = 64
SWA_Q = 512
SWA_KV = 128
FOX_W = 512
FOX_HEADS = 8
SWA_HEADS = 8
D_FF = 2816
N_IN = 4360
N_QKV = SWA_Q + 2 * SWA_KV + 3 * FOX_W
N_GATE = 2 * D_MODEL
F_PAD = 256
N_ZG = N_GATE + F_PAD
N_ZP = N_QKV + N_ZG
LN_EPS = 1e-5
NEG_INF = -1e30
ALPHA = (2 * DEPTH) ** 0.25
SCALE = HEAD_DIM ** -0.5
LOG2E = 1.4426950408889634
SLOPES = tuple(2.0 ** (-8.0 * (h + 1) / SWA_HEADS) for h in range(SWA_HEADS))

ADAM_LR = 0.001
ADAM_B1 = 0.9
ADAM_B2 = 0.999
ADAM_EPS = 1e-08
ADAM_WD = 0.01
ADAM_STEP = 10

LANE = 128
VMEM_LIMIT = 56 * 1024 * 1024

BIG = (("w_in", (D_MODEL, N_IN), 1), ("w_proj_a", (SWA_Q, D_MODEL), 1), ("w_proj_b", (FOX_W, D_MODEL), 1),
       ("w_out", (D_MODEL, D_MODEL), 0), ("w_ffn_in", (D_MODEL, 2 * D_FF), 1), ("w_ffn_out", (D_FF, D_MODEL), 0))
SMALL = (("ln_mix_g", D_MODEL), ("ln_mix_b", D_MODEL), ("b_in", N_IN), ("attn_sinks", SWA_HEADS),
         ("ln_ffn_g", D_MODEL), ("ln_ffn_b", D_MODEL), ("conv_w", 3 * D_FF), ("conv_b", D_FF))
ROW_BLOCK = 512
SMALL_LAYER_ROWS = -(-(sum(n for _, n in SMALL) + 1) // (8 * LANE)) * 8
SMALL_ROWS = ROW_BLOCK
FF_CHUNK = 2 * D_FF // N_DEV
N_CHUNK = D_FF // FF_CHUNK


def _div(n, cap, unit):
    if n <= cap:
        return n
    best = None
    for t in range(unit, cap + 1, unit):
        if n % t == 0:
            best = t
    assert best is not None, (n, cap, unit)
    return best


def _params(*sem):
    return pltpu.CompilerParams(dimension_semantics=sem, vmem_limit_bytes=VMEM_LIMIT)


def _peer(r):
    x, y, c = lax.axis_index("x"), lax.axis_index("y"), lax.axis_index("c")
    px = 1 - x if (r >> 2) & 1 else x
    py = 1 - y if (r >> 1) & 1 else y
    pc = 1 - c if r & 1 else c
    return (px, py, pc), 4 * px + 2 * py + pc


class _Comm:
    def __init__(self, tensors):
        self.arrays = [x for x, _ in tensors]
        self.gathers = [g for _, g in tensors]
        self.n = len(tensors)
        self.out_shape = [jax.ShapeDtypeStruct((N_DEV,) + (x.shape if g else x.shape[1:]), x.dtype)
                          for x, g in tensors]
        self.specs = [pl.BlockSpec(memory_space=pl.ANY)] * self.n
        self.scratch = [pltpu.SemaphoreType.DMA((N_DEV - 1, self.n)), pltpu.SemaphoreType.DMA((N_DEV - 1, self.n)),
                        pltpu.SemaphoreType.DMA((self.n,))]

    def _copies(self, x_refs, out_refs, sems):
        send_sems, recv_sems, local_sems = sems
        _, me = _peer(0)

        def src(t, idx):
            return x_refs[t] if self.gathers[t] else x_refs[t].at[idx]

        def remote(r, t, mine):
            peer, pid = _peer(r)
            return pltpu.make_async_remote_copy(src_ref=src(t, pid), dst_ref=out_refs[t].at[me if mine else pid],
                                                send_sem=send_sems.at[r - 1, t], recv_sem=recv_sems.at[r - 1, t],
                                                device_id=peer, device_id_type=MESH)

        pairs = [(r, t) for r in range(1, N_DEV) for t in range(self.n)]
        local = [pltpu.make_async_copy(src(t, me), out_refs[t].at[me], local_sems.at[t]) for t in range(self.n)]
        return local, [remote(r, t, True) for r, t in pairs], lambda: [remote(r, t, False) for r, t in pairs]

    def start(self, x_refs, out_refs, sems):
        local, sent, _ = self._copies(x_refs, out_refs, sems)
        for cp in local + sent:
            cp.start()

    def wait(self, x_refs, out_refs, sems):
        local, sent, landing = self._copies(x_refs, out_refs, sems)
        for cp in landing():
            cp.wait_recv()
        for cp in sent:
            cp.wait_send()
        for cp in local:
            cp.wait()


def _gather_two_level(x, name):
    def body(x_ref, out_ref, send_sems, recv_sems, local_sem):
        x_, y_, c_ = lax.axis_index("x"), lax.axis_index("y"), lax.axis_index("c")
        me, sibling = (x_, y_, c_), (x_, y_, 1 - c_)
        chips = [(1 - x_, y_), (x_, 1 - y_), (1 - x_, 1 - y_)]

        def slab(px, py, pc):
            return out_ref.at[4 * px + 2 * py + pc]

        def copy(k, block, to, src=None):
            return pltpu.make_async_remote_copy(src_ref=slab(*block) if src is None else src, dst_ref=slab(*block),
                                                send_sem=send_sems.at[k], recv_sem=recv_sems.at[k], device_id=to,
                                                device_id_type=MESH)

        mine = pltpu.make_async_copy(x_ref, slab(*me), local_sem)
        mine.start()
        first = [copy(0, me, sibling, src=x_ref)] + [copy(1 + j, me, (*chip, c_), src=x_ref)
                                                     for j, chip in enumerate(chips)]
        for cp in first:
            cp.start()
        passed = [copy(4 + j, (*chip, c_), sibling) for j, chip in enumerate(chips)]
        for j, chip in enumerate(chips):
            copy(1 + j, (*chip, c_), me).wait_recv()
            passed[j].start()
        copy(0, sibling, me).wait_recv()
        for j, chip in enumerate(chips):
            copy(4 + j, (*chip, 1 - c_), me).wait_recv()
        for cp in first + passed:
            cp.wait_send()
        mine.wait()

    spec = pl.BlockSpec(memory_space=pl.ANY)
    return pl.pallas_call(
        body, name=name, out_shape=jax.ShapeDtypeStruct((N_DEV,) + x.shape, x.dtype), in_specs=[spec], out_specs=spec,
        scratch_shapes=[pltpu.SemaphoreType.DMA((N_DEV - 1,)), pltpu.SemaphoreType.DMA((N_DEV - 1,)),
                        pltpu.SemaphoreType.DMA],
    )(x)


def _with_comm(comm, n_in, n_out, first, last, compute):
    nc = comm.n if comm is not None else 0

    def body(*refs):
        ins, x_refs = refs[:n_in], refs[n_in:n_in + nc]
        outs = refs[n_in + nc:n_in + nc + n_out]
        out_refs = refs[n_in + nc + n_out:n_in + 2 * nc + n_out]
        sems = refs[n_in + 2 * nc + n_out:]
        if nc:
            @pl.when(first())
            def _():
                comm.start(x_refs, out_refs, sems)

        compute(*ins, *outs)
        if nc:
            @pl.when(last())
            def _():
                comm.wait(x_refs, out_refs, sems)

    return body


def _d_h(dz, w_in_t, res, comm=None, tm=512):
    m, k = dz.shape
    d = w_in_t.shape[1]
    tm = _div(m, tm, 8)
    steps = m // tm
    c_specs, c_shapes, c_scratch, c_arrays = _comm_parts(comm)

    def compute(dz_ref, w_ref, res_ref, o_ref):
        o_ref[...] = ALPHA * res_ref[...] + jnp.dot(dz_ref[...], w_ref[...], preferred_element_type=F32)

    body = _with_comm(comm, 3, 1, lambda: pl.program_id(0) == 0, lambda: pl.program_id(0) == steps - 1, compute)
    row = pl.BlockSpec((tm, d), lambda i: (i, 0))
    outs = pl.pallas_call(
        body, name="d_h" if comm is None else "d_h_comm", grid=(steps,),
        in_specs=[pl.BlockSpec((tm, k), lambda i: (i, 0)), pl.BlockSpec((k, d), lambda i: (0, 0)), row] + c_specs,
        out_specs=[row] + c_specs, out_shape=[jax.ShapeDtypeStruct((m, d), F32)] + c_shapes,
        scratch_shapes=c_scratch,
        compiler_params=_params("arbitrary"),
    )(dz, w_in_t, res, *c_arrays)
    return outs[0], outs[1:]


def _z_proj(h, w_in_t, b_p, tm=512):
    m, k = h.shape
    tm = _div(m, tm, 8)
    nt = (((1,), (1,)), ((), ()))

    def body(h_ref, w_ref, b_ref, zq_ref, zg_ref):
        a = h_ref[...].astype(BF16)
        zq = lax.dot_general(a, w_ref[:N_QKV, :], nt, preferred_element_type=F32)
        zq_ref[...] = (zq + b_ref[:, :N_QKV]).astype(BF16)
        zg_ref[...] = lax.dot_general(a, w_ref[N_QKV:, :], nt, preferred_element_type=F32) + b_ref[:, N_QKV:]

    return pl.pallas_call(
        body, name="z_proj", grid=(m // tm,),
        in_specs=[pl.BlockSpec((tm, k), lambda i: (i, 0)), pl.BlockSpec((N_ZP, k), lambda i: (0, 0)),
                  pl.BlockSpec((1, N_ZP), lambda i: (0, 0))],
        out_specs=[pl.BlockSpec((tm, N_QKV), lambda i: (i, 0)), pl.BlockSpec((tm, N_ZG), lambda i: (i, 0))],
        out_shape=[jax.ShapeDtypeStruct((m, N_QKV), BF16), jax.ShapeDtypeStruct((m, N_ZG), F32)],
        compiler_params=_params("parallel"),
    )(h, w_in_t, b_p)


def _linear_tn(a, g, *, name, tk=1024, tn=640, tm=2048, colsum=False):
    m, k = a.shape
    n = g.shape[1]
    tk = _div(k, tk, LANE)
    tn = _div(n, tn, LANE)
    tm = _div(m, tm, 8)
    steps = m // tm
    assert not colsum or tn == n

    def body(a_ref, g_ref, o_ref, *rest):
        acc_ref = rest[-1]
        s = pl.program_id(2)

        @pl.when(s == 0)
        def _():
            acc_ref[...] = jnp.zeros_like(acc_ref)
            if colsum:
                rest[0][...] = jnp.zeros_like(rest[0])

        a_blk = a_ref[...]
        acc_ref[...] += lax.dot_general(a_blk.astype(BF16), g_ref[...].astype(BF16), (((0,), (0,)), ((), ())),
                                        preferred_element_type=F32)
        if colsum:
            rest[0][...] += jnp.sum(a_blk.astype(F32), axis=0, keepdims=True)

        @pl.when(s == steps - 1)
        def _():
            o_ref[...] = acc_ref[...].astype(BF16)

    out_specs = [pl.BlockSpec((tk, tn), lambda i, j, s: (i, j))]
    out_shape = [jax.ShapeDtypeStruct((k, n), BF16)]
    if colsum:
        out_specs.append(pl.BlockSpec((1, tk), lambda i, j, s: (0, i)))
        out_shape.append(jax.ShapeDtypeStruct((1, k), F32))
    outs = pl.pallas_call(
        body, name=name, grid=(k // tk, n // tn, steps),
        in_specs=[pl.BlockSpec((tm, tk), lambda i, j, s: (s, i)), pl.BlockSpec((tm, tn), lambda i, j, s: (s, j))],
        out_specs=out_specs, out_shape=out_shape,
        scratch_shapes=[pltpu.VMEM((tk, tn), F32)],
        compiler_params=_params("parallel", "parallel", "arbitrary"),
    )(a, g)
    return outs if colsum else outs[0]


def _ln(u, g, b):
    mu = jnp.mean(u, axis=-1, keepdims=True)
    d = u - mu
    var = jnp.mean(d * d, axis=-1, keepdims=True)
    return d * lax.rsqrt(var + LN_EPS) * g + b


def _ln_bwd_block(dy, u, g):
    mu = jnp.mean(u, axis=-1, keepdims=True)
    dd = u - mu
    rstd = lax.rsqrt(jnp.mean(dd * dd, axis=-1, keepdims=True) + LN_EPS)
    xhat = dd * rstd
    dxh = dy * g
    m1 = jnp.mean(dxh, axis=-1, keepdims=True)
    m2 = jnp.mean(dxh * xhat, axis=-1, keepdims=True)
    return (rstd * (dxh - m1 - xhat * m2), jnp.sum(dy * xhat, axis=0, keepdims=True),
            jnp.sum(dy, axis=0, keepdims=True))


SCAN_ROWS = 512


def _tri(n, upper):
    r = lax.broadcasted_iota(jnp.int32, (n, n), 0)
    c = lax.broadcasted_iota(jnp.int32, (n, n), 1)
    return jnp.where((c >= r) if upper else (c <= r), 1.0, 0.0).astype(F32)


def _cumsum_logf(zg):
    s = zg.shape[0]
    t = _div(s, SCAN_ROWS, LANE)
    nb = s // t
    fcol = N_GATE // LANE

    def body(f_ref, c_ref, carry_ref):
        @pl.when(pl.program_id(0) == 0)
        def _():
            carry_ref[...] = jnp.zeros_like(carry_ref)

        f = f_ref[...]
        logf = jnp.minimum(f, 0.0) - jnp.log(1.0 + jnp.exp(-jnp.abs(f)))
        c = jnp.dot(_tri(t, False), logf, precision=lax.Precision.HIGHEST, preferred_element_type=F32)
        c = c + carry_ref[0:1, :]
        c_ref[...] = c
        carry_ref[...] = jnp.broadcast_to(c[t - 1:t, :], carry_ref.shape)

    return pl.pallas_call(
        body, name="cumsum_logf", grid=(nb,),
        in_specs=[pl.BlockSpec((t, LANE), lambda i: (i, fcol))],
        out_specs=pl.BlockSpec((t, LANE), lambda i: (i, 0)),
        out_shape=jax.ShapeDtypeStruct((s, LANE), F32),
        scratch_shapes=[pltpu.VMEM((8, LANE), F32)],
        compiler_params=_params("arbitrary"),
    )(zg)


def _forget_bwd(dcc, zg):
    s = zg.shape[0]
    t = _div(s, SCAN_ROWS, LANE)
    nb = s // t
    fcol = N_GATE // LANE

    def body(dc_ref, f_ref, o_ref, carry_ref):
        @pl.when(pl.program_id(0) == 0)
        def _():
            carry_ref[...] = jnp.zeros_like(carry_ref)

        lane = lax.broadcasted_iota(jnp.int32, (1, LANE), 1)
        dc = jnp.zeros((t, LANE), F32)
        for p in range(FOX_HEADS // 2):
            tile = dc_ref[:, p * LANE:(p + 1) * LANE]
            moved = pltpu.roll(tile, 2 * p, 1) if p else tile
            dc = jnp.where((lane == 2 * p) | (lane == 2 * p + 1), moved, dc)
        dlogf = jnp.dot(_tri(t, True), dc, precision=lax.Precision.HIGHEST, preferred_element_type=F32)
        dlogf = dlogf + carry_ref[0:1, :]
        o_ref[...] = (dlogf * jax.nn.sigmoid(-f_ref[...])).astype(BF16)
        carry_ref[...] = jnp.broadcast_to(dlogf[0:1, :], carry_ref.shape)

    return pl.pallas_call(
        body, name="forget_bwd", grid=(nb,),
        in_specs=[pl.BlockSpec((t, FOX_W), lambda i: (nb - 1 - i, 0)),
                  pl.BlockSpec((t, LANE), lambda i: (nb - 1 - i, fcol))],
        out_specs=pl.BlockSpec((t, LANE), lambda i: (nb - 1 - i, 0)),
        out_shape=jax.ShapeDtypeStruct((s, LANE), BF16),
        scratch_shapes=[pltpu.VMEM((8, LANE), F32)],
        compiler_params=_params("arbitrary"),
    )(dcc, zg)


KA_COL = SWA_Q // LANE
VA_COL = KA_COL + 1


def _half_masks():
    lane = lax.broadcasted_iota(jnp.int32, (1, LANE), 1)
    hi = lane >= HEAD_DIM
    return (jnp.logical_not(hi), hi)


def _both_halves(x, sel):
    xs = jnp.where(sel, x, 0.0)
    return xs + pltpu.roll(xs, HEAD_DIM, 1)


SWA_PER_KV = 4
WIDE = SWA_PER_KV * LANE


def _swa_bias():
    k = np.arange(2 * LANE)[:, None]
    q = np.arange(LANE)[None, :]
    dist = (q + LANE - k).astype(np.float32)
    valid = (dist >= 0) & (dist < LANE)
    per_head = [np.where(valid, np.float32(-s) * dist, np.float32(NEG_INF)) for s in SLOPES]
    return jnp.asarray(np.stack([np.concatenate(per_head[SWA_PER_KV * hk:SWA_PER_KV * (hk + 1)], axis=1)
                                 for hk in range(2)]), F32)


def _no_previous_block(i_blk):
    k = lax.broadcasted_iota(jnp.int32, (2 * LANE, WIDE), 0)
    return jnp.where((i_blk == 0) & (k < LANE), NEG_INF, 0.0)


def _stack_heads(ref, blk, hk, halves, scale):
    tiles = []
    for j in range(SWA_PER_KV):
        p = 2 * hk + j // 2
        t = ref[blk, p * LANE:(p + 1) * LANE]
        if scale:
            t = _scaled(t)
        tiles.append(jnp.where(halves[j % 2], t, jnp.zeros_like(t)))
    return jnp.concatenate(tiles, axis=0)


def _pair_tile(wide, pp, row_halves):
    a = wide[:, (2 * pp) * LANE:(2 * pp + 1) * LANE]
    b = wide[:, (2 * pp + 1) * LANE:(2 * pp + 2) * LANE]
    return jnp.where(row_halves[0], a, b).T


def _lane_blocks(rows8, hk):
    return jnp.concatenate([rows8[SWA_PER_KV * hk + j:SWA_PER_KV * hk + j + 1, :] for j in range(SWA_PER_KV)], axis=1)


def _row_halves():
    hi = lax.broadcasted_iota(jnp.int32, (LANE, 1), 0) >= HEAD_DIM
    return (jnp.logical_not(hi), hi)


NT = (((1,), (1,)), ((), ()))


def _scaled(q):
    return (q.astype(F32) * SCALE).astype(BF16)


SWA_GROUP = 4


def _swa_group(s_len):
    return SWA_GROUP if (s_len // LANE) % SWA_GROUP == 0 else 1


def _swa_specs(group):
    rows = group * LANE
    prev = lambda i: jnp.maximum(i * group - 1, 0)
    return [pl.BlockSpec((rows, SWA_Q), lambda i: (i, 0)),
            pl.BlockSpec((rows, LANE), lambda i: (i, KA_COL)), pl.BlockSpec((rows, LANE), lambda i: (i, VA_COL)),
            pl.BlockSpec((LANE, LANE), lambda i: (prev(i), KA_COL)),
            pl.BlockSpec((LANE, LANE), lambda i: (prev(i), VA_COL))]


def _swa_window(g, cur_ref, prev_ref):
    before = prev_ref[...] if g == 0 else cur_ref[(g - 1) * LANE:g * LANE, :]
    return jnp.concatenate([before, cur_ref[g * LANE:(g + 1) * LANE, :]], axis=0).astype(F32)


def _swa_fwd(zq, sinks):
    s_len = zq.shape[0]
    group = _swa_group(s_len)
    rows = group * LANE
    sink_lanes = jnp.repeat(sinks[:, :SWA_HEADS], LANE, axis=1)

    def body(q_ref, kc_ref, vc_ref, kp_ref, vp_ref, sink_ref, bias_ref, o_ref, lse_ref):
        halves = _half_masks()
        row_halves = _row_halves()
        for g in range(group):
            blk = slice(g * LANE, (g + 1) * LANE)
            kcat = _swa_window(g, kc_ref, kp_ref)
            vcat = _swa_window(g, vc_ref, vp_ref)
            lse_rows = []
            for hk in range(2):
                kb = _both_halves(kcat, halves[hk]).astype(BF16)
                v_t = _both_halves(vcat, halves[hk]).T.astype(BF16)
                q4 = _stack_heads(q_ref, blk, hk, halves, True)
                s_t = lax.dot_general(kb, q4, NT, preferred_element_type=F32) + bias_ref[hk]
                if g == 0:
                    s_t = s_t + _no_previous_block(pl.program_id(0))
                sink = sink_ref[:, hk * WIDE:(hk + 1) * WIDE]
                m = jnp.maximum(jnp.max(s_t, axis=0, keepdims=True), sink)
                pe = jnp.exp(s_t - m)
                den = jnp.sum(pe, axis=0, keepdims=True) + jnp.exp(sink - m)
                out_t = jnp.dot(v_t, (pe * (1.0 / den)).astype(BF16), preferred_element_type=F32)
                for pp in range(2):
                    p = 2 * hk + pp
                    o_ref[blk, p * LANE:(p + 1) * LANE] = _pair_tile(out_t, pp, row_halves).astype(BF16)
                lse4 = m + jnp.log(den)
                lse_rows += [lse4[:, j * LANE:(j + 1) * LANE] for j in range(SWA_PER_KV)]
            lse_ref[:, blk] = jnp.concatenate(lse_rows, axis=0)

    return pl.pallas_call(
        body, name="swa_fwd", grid=(s_len // rows,),
        in_specs=_swa_specs(group) + [pl.BlockSpec((1, SWA_HEADS * LANE), lambda i: (0, 0)),
                                      pl.BlockSpec((2, 2 * LANE, WIDE), lambda i: (0, 0, 0))],
        out_specs=[pl.BlockSpec((rows, SWA_Q), lambda i: (i, 0)), pl.BlockSpec((SWA_HEADS, rows), lambda i: (0, i))],
        out_shape=[jax.ShapeDtypeStruct((s_len, SWA_Q), BF16), jax.ShapeDtypeStruct((SWA_HEADS, s_len), F32)],
        compiler_params=_params("parallel"),
    )(zq, zq, zq, zq, zq, sink_lanes, _swa_bias())


def _swa_bwd(zq, sinks, o, do, lse):
    s_len = zq.shape[0]
    group = _swa_group(s_len)
    rows = group * LANE

    def body(q_ref, kc_ref, vc_ref, kp_ref, vp_ref, sink_ref, bias_ref, o_ref, do_ref, lse_ref,
             dq_ref, dk_ref, dv_ref, ds_ref):
        halves = _half_masks()
        row_halves = _row_halves()
        lane = lax.broadcasted_iota(jnp.int32, (1, LANE), 1)
        dsink = jnp.zeros((1, LANE), F32)
        for g in range(group):
            blk = slice(g * LANE, (g + 1) * LANE)
            i_blk = pl.program_id(0) * group + g
            kcat = _swa_window(g, kc_ref, kp_ref)
            vcat = _swa_window(g, vc_ref, vp_ref)
            lse_rows = lse_ref[:, blk]
            prod = do_ref[blk, :].astype(F32) * o_ref[blk, :].astype(F32)
            select = (lax.broadcasted_iota(jnp.int32, (SWA_HEADS, SWA_Q), 1) // HEAD_DIM
                      == lax.broadcasted_iota(jnp.int32, (SWA_HEADS, SWA_Q), 0))
            delta_rows = lax.dot_general(jnp.where(select, 1.0, 0.0), prod, NT, precision=lax.Precision.HIGHEST,
                                         preferred_element_type=F32)
            dk_tot = jnp.zeros((2 * LANE, LANE), F32)
            dv_tot = jnp.zeros((2 * LANE, LANE), F32)
            for hk in range(2):
                kb = _both_halves(kcat, halves[hk])
                k_t = kb.T.astype(BF16)
                kb = kb.astype(BF16)
                vb = _both_halves(vcat, halves[hk]).astype(BF16)
                q4 = _stack_heads(q_ref, blk, hk, halves, True)
                do4 = _stack_heads(do_ref, blk, hk, halves, False)
                lse4 = _lane_blocks(lse_rows, hk)
                delta4 = _lane_blocks(delta_rows, hk)
                s_t = lax.dot_general(kb, q4, NT, preferred_element_type=F32) + bias_ref[hk]
                if g == 0:
                    s_t = s_t + _no_previous_block(pl.program_id(0))
                p_t = jnp.exp(s_t - lse4)
                dp_t = lax.dot_general(vb, do4, NT, preferred_element_type=F32)
                ds_t = (p_t * (dp_t - delta4)).astype(BF16)
                sink_part = jnp.exp(sink_ref[:, hk * WIDE:(hk + 1) * WIDE] - lse4) * delta4
                for j in range(SWA_PER_KV):
                    dsink_h = -jnp.sum(sink_part[:, j * LANE:(j + 1) * LANE], axis=1, keepdims=True)
                    dsink = dsink + jnp.where(lane == SWA_PER_KV * hk + j, dsink_h, 0.0)
                dq_t = jnp.dot(k_t, ds_t, preferred_element_type=F32)
                for pp in range(2):
                    p = 2 * hk + pp
                    dq_ref[blk, p * LANE:(p + 1) * LANE] = (_pair_tile(dq_t, pp, row_halves) * SCALE).astype(BF16)
                dk_acc = jnp.dot(ds_t, q4, preferred_element_type=F32)
                dv_acc = jnp.dot(p_t.astype(BF16), do4, preferred_element_type=F32)
                dk_tot = dk_tot + jnp.where(halves[hk], dk_acc + pltpu.roll(dk_acc, HEAD_DIM, 1), 0.0)
                dv_tot = dv_tot + jnp.where(halves[hk], dv_acc + pltpu.roll(dv_acc, HEAD_DIM, 1), 0.0)
            cur = pl.ds(pl.multiple_of(i_blk * LANE, LANE), LANE)
            dk_ref[cur, :] = dk_tot[LANE:, :]
            dv_ref[cur, :] = dv_tot[LANE:, :]

            def add_previous(i_blk=i_blk, dk_tot=dk_tot, dv_tot=dv_tot):
                prv = pl.ds(pl.multiple_of((i_blk - 1) * LANE, LANE), LANE)
                dk_ref[prv, :] += dk_tot[:LANE, :]
                dv_ref[prv, :] += dv_tot[:LANE, :]

            if g == 0:
                pl.when(i_blk > 0)(add_previous)
            else:
                add_previous()

        @pl.when(pl.program_id(0) == 0)
        def _():
            ds_ref[...] = jnp.zeros_like(ds_ref)

        ds_ref[...] += dsink

    blk512 = pl.BlockSpec((rows, SWA_Q), lambda i: (i, 0))
    full = pl.BlockSpec((s_len, LANE), lambda i: (0, 0))
    vec = pl.BlockSpec((1, LANE), lambda i: (0, 0))
    return pl.pallas_call(
        body, name="swa_bwd", grid=(s_len // rows,),
        in_specs=_swa_specs(group) + [pl.BlockSpec((1, SWA_HEADS * LANE), lambda i: (0, 0)),
                                      pl.BlockSpec((2, 2 * LANE, WIDE), lambda i: (0, 0, 0)), blk512, blk512,
                                      pl.BlockSpec((SWA_HEADS, rows), lambda i: (0, i))],
        out_specs=[blk512, full, full, vec],
        out_shape=[jax.ShapeDtypeStruct((s_len, SWA_Q), BF16), jax.ShapeDtypeStruct((s_len, LANE), F32),
                   jax.ShapeDtypeStruct((s_len, LANE), F32), jax.ShapeDtypeStruct((1, LANE), F32)],
        compiler_params=_params("arbitrary"),
    )(zq, zq, zq, zq, zq, jnp.repeat(sinks[:, :SWA_HEADS], LANE, axis=1), _swa_bias(), o, do, lse)


QB_COL = (SWA_Q + 2 * SWA_KV) // LANE
KB_COL = QB_COL + FOX_W // LANE
VB_COL = KB_COL + FOX_W // LANE
N_PAIR = FOX_HEADS // 2


def _causal(t, keys_first=False):
    r = lax.broadcasted_iota(jnp.int32, (t, t), 0)
    c = lax.broadcasted_iota(jnp.int32, (t, t), 1)
    return c >= r if keys_first else r >= c


N_SPLIT = 3


def _own_half(e):
    hi = lax.broadcasted_iota(jnp.int32, (1, LANE), 1) >= HEAD_DIM
    return hi if e else jnp.logical_not(hi)


def _feature_lane(e, t):
    return HEAD_DIM * (1 - e) + t


def _feature_tables():
    wide = FOX_HEADS * LANE
    place_q, place_k = np.zeros((N_SPLIT * LANE, wide), np.float32), np.zeros((N_SPLIT * LANE, wide), np.float32)
    ones_q, ones_k, ones_v, own = (np.zeros((1, wide), np.float32) for _ in range(4))
    for h in range(FOX_HEADS):
        e = h % 2
        own[0, h * LANE + HEAD_DIM * e:h * LANE + HEAD_DIM * (e + 1)] = 1.0
        ones_v[0, h * LANE + _feature_lane(e, 0)] = 1.0
        for t in range(N_SPLIT):
            place_q[t * LANE + h, h * LANE + _feature_lane(e, t)] = 1.0
            ones_q[0, h * LANE + _feature_lane(e, N_SPLIT + t)] = 1.0
            ones_k[0, h * LANE + _feature_lane(e, t)] = 1.0
            place_k[t * LANE + h, h * LANE + _feature_lane(e, N_SPLIT + t)] = -1.0
    return tuple(jnp.asarray(a) for a in (place_q, place_k, ones_q, ones_k, ones_v, own))


def _fox_prep(zq, c, tm=256):
    s_len = zq.shape[0]
    tm = _div(s_len, tm, 8)
    wide = FOX_HEADS * LANE

    def body(z_ref, c_ref, pq_ref, pk_ref, oq_ref, ok_ref, ov_ref, own_ref, qx_ref, kx_ref, vx_ref):
        rest = c_ref[...]
        parts = []
        for _ in range(N_SPLIT):
            part = rest.astype(BF16).astype(F32)
            rest = rest - part
            parts.append(part)
        parts = jnp.concatenate(parts, axis=1)
        qf = jnp.dot(parts, pq_ref[...], preferred_element_type=F32) + oq_ref[...]
        kf = jnp.dot(parts, pk_ref[...], preferred_element_type=F32) + ok_ref[...]
        own = own_ref[...] > 0.5
        for p in range(N_PAIR):
            cols = slice(2 * p * LANE, (2 * p + 2) * LANE)
            pair = lambda col: jnp.tile(z_ref[:, (col + p) * LANE:(col + p + 1) * LANE].astype(F32), (1, 2))
            qx_ref[:, cols] = jnp.where(own[:, cols], pair(QB_COL) * SCALE, qf[:, cols]).astype(BF16)
            kx_ref[:, cols] = jnp.where(own[:, cols], pair(KB_COL), kf[:, cols]).astype(BF16)
            vx_ref[:, cols] = jnp.where(own[:, cols], pair(VB_COL), ov_ref[:, cols]).astype(BF16)

    out = jax.ShapeDtypeStruct((s_len, wide), BF16)
    blk = pl.BlockSpec((tm, wide), lambda i: (i, 0))
    table = pl.BlockSpec((N_SPLIT * LANE, wide), lambda i: (0, 0))
    vec = pl.BlockSpec((1, wide), lambda i: (0, 0))
    return pl.pallas_call(
        body, name="fox_prep", grid=(s_len // tm,),
        in_specs=[pl.BlockSpec((tm, N_QKV), lambda i: (i, 0)), pl.BlockSpec((tm, LANE), lambda i: (i, 0)),
                  table, table, vec, vec, vec, vec],
        out_specs=[blk, blk, blk], out_shape=[out, out, out],
        compiler_params=_params("parallel"),
    )(zq, c, *_feature_tables())


def _comm_parts(comm):
    return ([], [], [], []) if comm is None else (comm.specs, comm.out_shape, comm.scratch, comm.arrays)


def _fox_fwd(qx, kx, vx, comm=None, t_cap=1024):
    s_len = qx.shape[0]
    t = _div(s_len, t_cap, LANE)
    nq = s_len // t
    c_specs, c_shapes, c_scratch, c_arrays = _comm_parts(comm)

    def compute(q_ref, k_ref, v_ref, o_ref, o32_ref, m_ref, l_ref):
        i = pl.program_id(1)
        qs = [q_ref[:, e * LANE:(e + 1) * LANE] for e in range(2)]

        def step(j, carry, diag):
            rows = pl.ds(pl.multiple_of(j * t, t), t)
            new = []
            for e in range(2):
                m, acc = carry[e]
                s2 = lax.dot_general(qs[e], k_ref[rows, e * LANE:(e + 1) * LANE], NT,
                                     preferred_element_type=F32) * LOG2E
                if diag:
                    s2 = jnp.where(_causal(t), s2, NEG_INF)
                mn = jnp.maximum(m, jnp.ceil(jnp.max(s2, axis=1, keepdims=True)))
                pe = jnp.exp2(s2 - mn).astype(BF16)
                acc = acc * jnp.exp2(m - mn) + jnp.dot(pe, v_ref[rows, e * LANE:(e + 1) * LANE],
                                                       preferred_element_type=F32)
                new.append((mn, acc))
            return tuple(new)

        init = (jnp.full((t, 1), NEG_INF, F32), jnp.zeros((t, LANE), F32))
        carry = lax.fori_loop(0, i, lambda j, c: step(j, c, False), (init, init))
        carry = step(i, carry, True)
        outs, ls = [], []
        for e in range(2):
            m, acc = carry[e]
            l = acc[:, _feature_lane(e, 0):_feature_lane(e, 0) + 1]
            outs.append(acc / l)
            ls.append(l)
        out = jnp.where(_own_half(1), outs[1], outs[0])
        o_ref[...] = out.astype(BF16)
        o32_ref[...] = out
        m_ref[...] = jnp.where(_own_half(1), carry[1][0], carry[0][0])
        l_ref[...] = jnp.where(_own_half(1), ls[1], ls[0])

    body = _with_comm(comm, 3, 4, lambda: (pl.program_id(0) == 0) & (pl.program_id(1) == 0),
                      lambda: (pl.program_id(0) == N_PAIR - 1) & (pl.program_id(1) == nq - 1), compute)
    pair = pl.BlockSpec((s_len, 2 * LANE), lambda p, i: (0, p))
    tile = pl.BlockSpec((t, LANE), lambda p, i: (i, p))
    wide = jax.ShapeDtypeStruct((s_len, FOX_W), F32)
    outs = pl.pallas_call(
        body, name="fox_fwd" if comm is None else "fox_fwd_comm%d" % comm.n, grid=(N_PAIR, nq),
        in_specs=[pl.BlockSpec((t, 2 * LANE), lambda p, i: (i, p)), pair, pair] + c_specs,
        out_specs=[tile, tile, tile, tile] + c_specs,
        out_shape=[jax.ShapeDtypeStruct((s_len, FOX_W), BF16), wide, wide, wide] + c_shapes,
        scratch_shapes=c_scratch,
        compiler_params=_params("arbitrary", "arbitrary"),
    )(qx, kx, vx, *c_arrays)
    return outs[0], outs[1], outs[2], outs[3], outs[4:]


def _fox_stats(o, do, m, l, tm=256):
    s_len = o.shape[0]
    tm = _div(s_len, tm, LANE)

    def body(o_ref, do_ref, m_ref, l_ref, dox_ref, st_ref):
        lane = lax.broadcasted_iota(jnp.int32, (1, LANE), 1)
        for p in range(N_PAIR):
            cols = slice(p * LANE, (p + 1) * LANE)
            dout = do_ref[:, cols]
            prod = o_ref[:, cols] * dout.astype(F32)
            shift = m_ref[:, cols]
            inv_l = 1.0 / l_ref[:, cols]
            st = jnp.zeros((tm, LANE), F32)
            for e in range(2):
                h = 2 * p + e
                dox_ref[:, h * LANE:(h + 1) * LANE] = jnp.where(_own_half(e), dout, jnp.zeros_like(dout))
                st = jnp.where(lane == e, shift[:, e * HEAD_DIM:e * HEAD_DIM + 1], st)
                delta = jnp.sum(jnp.where(_own_half(e), prod, 0.0), axis=1, keepdims=True)
                st = jnp.where(lane == 2 + e, delta, st)
                st = jnp.where(lane == 4 + e, inv_l[:, e * HEAD_DIM:e * HEAD_DIM + 1], st)
            st_ref[p] = st.T[:8, :]

    row = pl.BlockSpec((tm, FOX_W), lambda i: (i, 0))
    return pl.pallas_call(
        body, name="fox_stats", grid=(s_len // tm,), in_specs=[row, row, row, row],
        out_specs=[pl.BlockSpec((tm, FOX_HEADS * LANE), lambda i: (i, 0)),
                   pl.BlockSpec((N_PAIR, 8, tm), lambda i: (0, 0, i))],
        out_shape=[jax.ShapeDtypeStruct((s_len, FOX_HEADS * LANE), BF16),
                   jax.ShapeDtypeStruct((N_PAIR, 8, s_len), F32)],
        compiler_params=_params("parallel"),
    )(o, do, m, l)


def _fox_bwd(qx, kx, vx, dox, stats, comm=None, t_cap=1024):
    s_len = qx.shape[0]
    t = _div(s_len, t_cap, LANE)
    n = s_len // t
    c_specs, c_shapes, c_scratch, c_arrays = _comm_parts(comm)

    def compute(q_ref, do_ref, st_ref, k_ref, v_ref, dq_ref, dk_ref, dv_ref, dc_ref):
        j = pl.program_id(1)
        lane = lax.broadcasted_iota(jnp.int32, (1, LANE), 1)

        @pl.when(j == 0)
        def _():
            dq_ref[...] = jnp.zeros_like(dq_ref)

        ks = [k_ref[:, e * LANE:(e + 1) * LANE] for e in range(2)]
        vs = [v_ref[:, e * LANE:(e + 1) * LANE] for e in range(2)]
        ks_t = [k.astype(F32).T.astype(BF16) for k in ks]

        def step(i, carry, diag):
            rows = pl.ds(pl.multiple_of(i * t, t), t)
            new = []
            dq = jnp.zeros((LANE, t), F32)
            for e in range(2):
                dk, dv, dc = carry[e]
                q = q_ref[rows, e * LANE:(e + 1) * LANE]
                dout = do_ref[rows, e * LANE:(e + 1) * LANE]
                s_t = lax.dot_general(ks[e], q, NT, preferred_element_type=F32) * LOG2E
                if diag:
                    s_t = jnp.where(_causal(t, keys_first=True), s_t, NEG_INF)
                p_t = jnp.exp2(s_t - st_ref[0, e:e + 1, rows]).astype(BF16).astype(F32) * st_ref[0, 4 + e:5 + e, rows]
                dp_t = lax.dot_general(vs[e], dout, NT, preferred_element_type=F32)
                ds_f = p_t * (dp_t - st_ref[0, 2 + e:3 + e, rows])
                ds_t = ds_f.astype(BF16)
                dc = dc + jnp.sum(ds_f, axis=1, keepdims=True)
                dv = dv + jnp.dot(p_t.astype(BF16), dout, preferred_element_type=F32)
                dk = dk + jnp.dot(ds_t, q, preferred_element_type=F32)
                dq_e = jnp.dot(ks_t[e], ds_t, preferred_element_type=F32)
                dq = dq + jnp.where(_row_halves()[e], dq_e, 0.0)
                new.append((dk, dv, dc))
            dq_ref[rows, :] += dq.T * SCALE
            return tuple(new)

        zero = jnp.zeros((t, LANE), F32)
        init = (zero, zero, jnp.zeros((t, 1), F32))
        carry = step(j, (init, init), True)
        (dk0, dv0, dc0), (dk1, dv1, dc1) = lax.fori_loop(j + 1, n, lambda i, c: step(i, c, False), carry)
        dk_ref[...] = jnp.where(_own_half(1), dk1, dk0).astype(BF16)
        dv_ref[...] = jnp.where(_own_half(1), dv1, dv0).astype(BF16)
        dc_ref[...] = jnp.where(lane == 0, -dc0, jnp.where(lane == 1, -dc1, 0.0))

    body = _with_comm(comm, 5, 4, lambda: (pl.program_id(0) == 0) & (pl.program_id(1) == 0),
                      lambda: (pl.program_id(0) == N_PAIR - 1) & (pl.program_id(1) == n - 1), compute)
    pair = pl.BlockSpec((s_len, 2 * LANE), lambda p, j: (0, p))
    blk = pl.BlockSpec((t, 2 * LANE), lambda p, j: (j, p))
    tile = pl.BlockSpec((t, LANE), lambda p, j: (j, p))
    outs = pl.pallas_call(
        body, name="fox_bwd" if comm is None else "fox_bwd_comm", grid=(N_PAIR, n),
        in_specs=[pair, pair, pl.BlockSpec((1, 8, s_len), lambda p, j: (p, 0, 0)), blk, blk] + c_specs,
        out_specs=[pl.BlockSpec((s_len, LANE), lambda p, j: (0, p)), tile, tile, tile] + c_specs,
        out_shape=[jax.ShapeDtypeStruct((s_len, FOX_W), F32), jax.ShapeDtypeStruct((s_len, FOX_W), BF16),
                   jax.ShapeDtypeStruct((s_len, FOX_W), BF16), jax.ShapeDtypeStruct((s_len, FOX_W), F32)] + c_shapes,
        scratch_shapes=c_scratch,
        compiler_params=_params("arbitrary", "arbitrary"),
    )(qx, dox, stats, kx, vx, *c_arrays)
    return outs[0], outs[1], outs[2], outs[3], outs[4:]


def _mixer_out(attn_a, attn_b, zg, h, wpa, wpb, wout, g, b, tm=256):
    m = h.shape[0]
    tm = _div(m, tm, 8)

    def body(a_ref, b_ref, ga_ref, gb_ref, h_ref, wpa_ref, wpb_ref, wout_ref, g_ref, bb_ref, h1_ref, u_ref, mg_ref):
        ya = jnp.dot(a_ref[...], wpa_ref[...], preferred_element_type=F32)
        yb = jnp.dot(b_ref[...], wpb_ref[...], preferred_element_type=F32)
        merged = (jax.nn.sigmoid(ga_ref[...]) * ya + jax.nn.sigmoid(gb_ref[...]) * yb).astype(BF16)
        u = ALPHA * h_ref[...] + jnp.dot(merged, wout_ref[...], preferred_element_type=F32)
        u_ref[...] = u
        h1_ref[...] = _ln(u, g_ref[...], bb_ref[...])
        mg_ref[...] = merged

    row = pl.BlockSpec((tm, D_MODEL), lambda i: (i, 0))
    att = pl.BlockSpec((tm, SWA_Q), lambda i: (i, 0))
    vec = pl.BlockSpec((1, D_MODEL), lambda i: (0, 0))
    wsm = pl.BlockSpec((SWA_Q, D_MODEL), lambda i: (0, 0))
    return pl.pallas_call(
        body, name="mixer_out", grid=(m // tm,),
        in_specs=[att, att, row, pl.BlockSpec((tm, D_MODEL), lambda i: (i, 1)), row, wsm, wsm,
                  pl.BlockSpec((D_MODEL, D_MODEL), lambda i: (0, 0)), vec, vec],
        out_specs=[row, row, row],
        out_shape=[jax.ShapeDtypeStruct((m, D_MODEL), F32), jax.ShapeDtypeStruct((m, D_MODEL), F32),
                   jax.ShapeDtypeStruct((m, D_MODEL), BF16)],
        compiler_params=_params("parallel"),
    )(attn_a, attn_b, zg, zg, h, wpa, wpb, wout, g, b)


def _mixer_bwd(dh1, u1, g, wout, attn_a, attn_b, zg, wpa, wpb, tm=256):
    m = dh1.shape[0]
    tm = _div(m, tm, 8)

    def body(dh_ref, u_ref, g_ref, wout_ref, a_ref, b_ref, ga_ref, gb_ref, wpa_ref, wpb_ref,
             du_ref, dg_ref, db_ref, dya_ref, dyb_ref, dga_ref, dgb_ref, da_ref, dbb_ref):
        @pl.when(pl.program_id(0) == 0)
        def _():
            dg_ref[...] = jnp.zeros_like(dg_ref)
            db_ref[...] = jnp.zeros_like(db_ref)

        du, dg, db = _ln_bwd_block(dh_ref[...], u_ref[...], g_ref[...])
        du_ref[...] = du
        dg_ref[...] += dg
        db_ref[...] += db
        dm = lax.dot_general(du.astype(BF16), wout_ref[...], (((1,), (1,)), ((), ())), preferred_element_type=F32)
        for x_ref, gate_ref, w_ref, dy_ref, dgate_ref, dattn_ref in (
                (a_ref, ga_ref, wpa_ref, dya_ref, dga_ref, da_ref), (b_ref, gb_ref, wpb_ref, dyb_ref, dgb_ref, dbb_ref)):
            sg = jax.nn.sigmoid(gate_ref[...])
            dy = (dm * sg).astype(BF16)
            dy_ref[...] = dy
            y = jnp.dot(x_ref[...], w_ref[...], preferred_element_type=F32)
            dgate_ref[...] = (dm * y * sg * (1.0 - sg)).astype(BF16)
            dattn_ref[...] = lax.dot_general(dy, w_ref[...], (((1,), (1,)), ((), ())),
                                             preferred_element_type=F32).astype(BF16)

    row = pl.BlockSpec((tm, D_MODEL), lambda i: (i, 0))
    att = pl.BlockSpec((tm, SWA_Q), lambda i: (i, 0))
    vec = pl.BlockSpec((1, D_MODEL), lambda i: (0, 0))
    wsm = pl.BlockSpec((SWA_Q, D_MODEL), lambda i: (0, 0))
    wide = jax.ShapeDtypeStruct((m, D_MODEL), BF16)
    narrow = jax.ShapeDtypeStruct((m, SWA_Q), BF16)
    sums = jax.ShapeDtypeStruct((1, D_MODEL), F32)
    return pl.pallas_call(
        body, name="mixer_bwd", grid=(m // tm,),
        in_specs=[row, row, vec, pl.BlockSpec((D_MODEL, D_MODEL), lambda i: (0, 0)), att, att, row,
                  pl.BlockSpec((tm, D_MODEL), lambda i: (i, 1)), wsm, wsm],
        out_specs=[row, vec, vec, row, row, row, row, att, att],
        out_shape=[jax.ShapeDtypeStruct((m, D_MODEL), F32), sums, sums, wide, wide, wide, wide, narrow, narrow],
        compiler_params=_params("arbitrary"),
    )(dh1, u1, g, wout, attn_a, attn_b, zg, zg, wpa, wpb)


def _shift_down(x, k, halo, first):
    rows = lax.broadcasted_iota(jnp.int32, (x.shape[0], 1), 0)
    y = pltpu.roll(x, k, 0)
    for r in range(k):
        fill = jnp.where(first, 0.0, halo[8 - k + r:8 - k + r + 1, :])
        y = jnp.where(rows == r, fill, y)
    return y


def _shift_up(x, k, halo, last):
    n = x.shape[0]
    rows = lax.broadcasted_iota(jnp.int32, (n, 1), 0)
    y = pltpu.roll(x, n - k, 0)
    for r in range(k):
        fill = jnp.where(last, 0.0, halo[r:r + 1, :])
        y = jnp.where(rows == n - k + r, fill, y)
    return y


def _conv_act(gate, gate_m1, gate_m2, cw, cb):
    return cb + cw[0:1, :] * gate_m2 + cw[1:2, :] * gate_m1 + cw[2:3, :] * gate


def _ffn_in_conv(h1, wfi, cw, cb, tm=256):
    s_len = h1.shape[0]
    tm = _div(s_len, tm, 8)
    hb = tm // 8

    def body(a_ref, ap_ref, w_ref, cw_ref, cb_ref, gu_ref, act_ref):
        first = pl.program_id(0) == 0
        a = a_ref[...].astype(BF16)
        before = ap_ref[...].astype(BF16)
        for c in range(N_CHUNK):
            gate = jnp.dot(a, w_ref[c], preferred_element_type=F32)
            up = jnp.dot(a, w_ref[N_CHUNK + c], preferred_element_type=F32)
            halo = jnp.dot(before, w_ref[c], preferred_element_type=F32)
            gu_ref[c, 0] = gate
            gu_ref[c, 1] = up
            conv = _conv_act(gate, _shift_down(gate, 1, halo, first), _shift_down(gate, 2, halo, first),
                             cw_ref[c], cb_ref[c])
            act_ref[c] = (conv * jax.nn.sigmoid(conv) * up).astype(BF16)

    return pl.pallas_call(
        body, name="ffn_in_conv", grid=(s_len // tm,),
        in_specs=[pl.BlockSpec((tm, D_MODEL), lambda i: (i, 0)),
                  pl.BlockSpec((8, D_MODEL), lambda i: (jnp.maximum(i * hb - 1, 0), 0)),
                  pl.BlockSpec((N_DEV, D_MODEL, FF_CHUNK), lambda i: (0, 0, 0)),
                  pl.BlockSpec((N_CHUNK, 8, FF_CHUNK), lambda i: (0, 0, 0)),
                  pl.BlockSpec((N_CHUNK, 1, FF_CHUNK), lambda i: (0, 0, 0))],
        out_specs=[pl.BlockSpec((N_CHUNK, 2, tm, FF_CHUNK), lambda i: (0, 0, i, 0)),
                   pl.BlockSpec((N_CHUNK, tm, FF_CHUNK), lambda i: (0, i, 0))],
        out_shape=[jax.ShapeDtypeStruct((N_CHUNK, 2, s_len, FF_CHUNK), F32),
                   jax.ShapeDtypeStruct((N_CHUNK, s_len, FF_CHUNK), BF16)],
        compiler_params=_params("parallel"),
    )(h1, h1, wfi, cw, cb)


def _ffn_out_ln(act, wfo, res, g, b, target=None, tm=512):
    s_len = res.shape[0]
    tm = _div(s_len, tm, 8)
    last = target is not None

    def body(a_ref, w_ref, res_ref, g_ref, b_ref, *rest):
        u = ALPHA * res_ref[...]
        for c in range(N_CHUNK):
            u = u + jnp.dot(a_ref[c], w_ref[c], preferred_element_type=F32)
        y = _ln(u, g_ref[...], b_ref[...])
        if not last:
            u_ref, y_ref = rest
            u_ref[...] = u
            y_ref[...] = y
            return
        t_ref, u_ref, dy_ref, loss_ref = rest
        u_ref[...] = u

        @pl.when(pl.program_id(0) == 0)
        def _():
            loss_ref[...] = jnp.zeros_like(loss_ref)

        err = y - t_ref[...]
        dy_ref[...] = err / D_MODEL
        loss_ref[...] += 0.5 * jnp.sum(jnp.sum(err * err, axis=1, keepdims=True) / D_MODEL, axis=0, keepdims=True)

    row = pl.BlockSpec((tm, D_MODEL), lambda i: (i, 0))
    vec = pl.BlockSpec((1, D_MODEL), lambda i: (0, 0))
    wide = jax.ShapeDtypeStruct((s_len, D_MODEL), F32)
    return pl.pallas_call(
        body, name="ffn_out_ln_loss" if last else "ffn_out_ln", grid=(s_len // tm,),
        in_specs=[pl.BlockSpec((N_CHUNK, tm, FF_CHUNK), lambda i: (0, i, 0)),
                  pl.BlockSpec((N_CHUNK, FF_CHUNK, D_MODEL), lambda i: (0, 0, 0)), row, vec, vec] + [row] * last,
        out_specs=[row, row] + [pl.BlockSpec((8, LANE), lambda i: (0, 0))] * last,
        out_shape=[wide, wide] + [jax.ShapeDtypeStruct((8, LANE), F32)] * last,
        compiler_params=_params("arbitrary" if last else "parallel"),
    )(act, wfo, res, g, b, *([target] if last else []))


def _ffn_out_bwd(dh2, u2, g, wfo, tm=512):
    s_len = dh2.shape[0]
    tm = _div(s_len, tm, 8)

    def body(dh_ref, u_ref, g_ref, w_ref, du_ref, dg_ref, db_ref, o_ref):
        @pl.when(pl.program_id(0) == 0)
        def _():
            dg_ref[...] = jnp.zeros_like(dg_ref)
            db_ref[...] = jnp.zeros_like(db_ref)

        du, dg, db = _ln_bwd_block(dh_ref[...], u_ref[...], g_ref[...])
        du_ref[...] = du
        dg_ref[...] += dg
        db_ref[...] += db
        du_b = du.astype(BF16)
        for c in range(N_CHUNK):
            o_ref[c] = lax.dot_general(du_b, w_ref[c], (((1,), (1,)), ((), ())), preferred_element_type=F32)

    row = pl.BlockSpec((tm, D_MODEL), lambda i: (i, 0))
    vec = pl.BlockSpec((1, D_MODEL), lambda i: (0, 0))
    sums = jax.ShapeDtypeStruct((1, D_MODEL), F32)
    return pl.pallas_call(
        body, name="ffn_out_bwd", grid=(s_len // tm,),
        in_specs=[row, row, vec, pl.BlockSpec((N_CHUNK, FF_CHUNK, D_MODEL), lambda i: (0, 0, 0))],
        out_specs=[row, vec, vec, pl.BlockSpec((N_CHUNK, tm, FF_CHUNK), lambda i: (0, i, 0))],
        out_shape=[jax.ShapeDtypeStruct((s_len, D_MODEL), F32), sums, sums,
                   jax.ShapeDtypeStruct((N_CHUNK, s_len, FF_CHUNK), F32)],
        compiler_params=_params("arbitrary"),
    )(dh2, u2, g, wfo)


def _g_w_ffn_out(act, du, tm=2048):
    s_len = du.shape[0]
    tm = _div(s_len, tm, 8)
    steps = s_len // tm

    def body(a_ref, g_ref, o_ref, acc_ref):
        s = pl.program_id(1)

        @pl.when(s == 0)
        def _():
            acc_ref[...] = jnp.zeros_like(acc_ref)

        acc_ref[...] += lax.dot_general(a_ref[0], g_ref[...].astype(BF16), (((0,), (0,)), ((), ())),
                                        preferred_element_type=F32)

        @pl.when(s == steps - 1)
        def _():
            o_ref[0] = acc_ref[...].astype(BF16)

    return pl.pallas_call(
        body, name="g_w_ffn_out", grid=(N_CHUNK, steps),
        in_specs=[pl.BlockSpec((1, tm, FF_CHUNK), lambda c, s: (c, s, 0)),
                  pl.BlockSpec((tm, D_MODEL), lambda c, s: (s, 0))],
        out_specs=pl.BlockSpec((1, FF_CHUNK, D_MODEL), lambda c, s: (c, 0, 0)),
        out_shape=jax.ShapeDtypeStruct((N_CHUNK, FF_CHUNK, D_MODEL), BF16),
        scratch_shapes=[pltpu.VMEM((FF_CHUNK, D_MODEL), F32)],
        compiler_params=_params("parallel", "arbitrary"),
    )(act, du)


def _g_w_ffn_in(h1, dgu, tm=2048):
    s_len = h1.shape[0]
    tm = _div(s_len, tm, 8)
    steps = s_len // tm

    def body(a_ref, g_ref, o_ref, acc_ref):
        s = pl.program_id(1)

        @pl.when(s == 0)
        def _():
            acc_ref[...] = jnp.zeros_like(acc_ref)

        acc_ref[...] += lax.dot_general(g_ref[0, 0], a_ref[...].astype(BF16), (((0,), (0,)), ((), ())),
                                        preferred_element_type=F32)

        @pl.when(s == steps - 1)
        def _():
            o_ref[0] = acc_ref[...].astype(BF16)

    return pl.pallas_call(
        body, name="g_w_ffn_in", grid=(N_DEV, steps),
        in_specs=[pl.BlockSpec((tm, D_MODEL), lambda d, s: (s, 0)),
                  pl.BlockSpec((1, 1, tm, FF_CHUNK), lambda d, s: (d % N_CHUNK, d // N_CHUNK, s, 0))],
        out_specs=pl.BlockSpec((1, FF_CHUNK, D_MODEL), lambda d, s: (d, 0, 0)),
        out_shape=jax.ShapeDtypeStruct((N_DEV, FF_CHUNK, D_MODEL), BF16),
        scratch_shapes=[pltpu.VMEM((FF_CHUNK, D_MODEL), F32)],
        compiler_params=_params("parallel", "arbitrary"),
    )(h1, dgu)


def _conv_bwd_dh1(gu, dact, cw, cb, wfi, res, tm=256):
    s_len = gu.shape[2]
    tm = _div(s_len, tm, 8)
    nrow = s_len // tm
    hb = tm // 8

    def dconv_of(conv, up, da):
        sg = jax.nn.sigmoid(conv)
        return da * up * (sg * (1.0 + conv * (1.0 - sg)))

    def body(gu_ref, gp_ref, gun_ref, da_ref, dan_ref, cw_ref, cb_ref, w_ref, res_ref, dgu_ref, dcw_ref, dh_ref):
        i = pl.program_id(0)
        first = i == 0
        last = i == nrow - 1

        @pl.when(first)
        def _():
            dcw_ref[...] = jnp.zeros_like(dcw_ref)

        row = lax.broadcasted_iota(jnp.int32, (8, 1), 0)
        acc = ALPHA * res_ref[...]
        for c in range(N_CHUNK):
            cw = cw_ref[c]
            cb = cb_ref[c]
            gate = gu_ref[c, 0]
            halo = gp_ref[c, 0]
            g_m1 = _shift_down(gate, 1, halo, first)
            g_m2 = _shift_down(gate, 2, halo, first)
            conv = _conv_act(gate, g_m1, g_m2, cw, cb)
            da = da_ref[c]
            dup = (da * conv * jax.nn.sigmoid(conv)).astype(BF16)
            dconv = dconv_of(conv, gu_ref[c, 1], da)
            gate_n = gun_ref[c, 0]
            tail = gate[tm - 8:, :]
            conv_n = _conv_act(gate_n, _shift_down(gate_n, 1, tail, False), _shift_down(gate_n, 2, tail, False),
                               cw, cb)
            dconv_n = dconv_of(conv_n, gun_ref[c, 1], dan_ref[c])
            dgate = (cw[2:3, :] * dconv + cw[1:2, :] * _shift_up(dconv, 1, dconv_n, last)
                     + cw[0:1, :] * _shift_up(dconv, 2, dconv_n, last)).astype(BF16)
            dgu_ref[c, 0] = dgate
            dgu_ref[c, 1] = dup
            acc = acc + jnp.dot(dgate, w_ref[c], preferred_element_type=F32)
            acc = acc + jnp.dot(dup, w_ref[N_CHUNK + c], preferred_element_type=F32)
            part = jnp.zeros((8, FF_CHUNK), F32)
            for r, term in enumerate((dconv * g_m2, dconv * g_m1, dconv * gate, dconv)):
                part = jnp.where(row == r, jnp.sum(term, axis=0, keepdims=True), part)
            dcw_ref[c] += part
        dh_ref[...] = acc

    nxt = lambda i: jnp.minimum((i + 1) * hb, s_len // 8 - 1)
    main = pl.BlockSpec((N_CHUNK, 2, tm, FF_CHUNK), lambda i: (0, 0, i, 0))
    row_d = pl.BlockSpec((tm, D_MODEL), lambda i: (i, 0))
    return pl.pallas_call(
        body, name="conv_bwd_dh1", grid=(nrow,),
        in_specs=[main,
                  pl.BlockSpec((N_CHUNK, 1, 8, FF_CHUNK), lambda i: (0, 0, jnp.maximum(i * hb - 1, 0), 0)),
                  pl.BlockSpec((N_CHUNK, 2, 8, FF_CHUNK), lambda i: (0, 0, nxt(i), 0)),
                  pl.BlockSpec((N_CHUNK, tm, FF_CHUNK), lambda i: (0, i, 0)),
                  pl.BlockSpec((N_CHUNK, 8, FF_CHUNK), lambda i: (0, nxt(i), 0)),
                  pl.BlockSpec((N_CHUNK, 8, FF_CHUNK), lambda i: (0, 0, 0)),
                  pl.BlockSpec((N_CHUNK, 1, FF_CHUNK), lambda i: (0, 0, 0)),
                  pl.BlockSpec((N_DEV, FF_CHUNK, D_MODEL), lambda i: (0, 0, 0)), row_d],
        out_specs=[main, pl.BlockSpec((N_CHUNK, 8, FF_CHUNK), lambda i: (0, 0, 0)), row_d],
        out_shape=[jax.ShapeDtypeStruct((N_CHUNK, 2, s_len, FF_CHUNK), BF16),
                   jax.ShapeDtypeStruct((N_CHUNK, 8, FF_CHUNK), F32),
                   jax.ShapeDtypeStruct((s_len, D_MODEL), F32)],
        compiler_params=_params("arbitrary"),
    )(gu, gu, gu, dact, dact, cw, cb, wfi, res)


def _sum_devices(r_ref):
    acc = r_ref[0].astype(F32)
    for d in range(1, N_DEV):
        acc = acc + r_ref[d].astype(F32)
    return acc


def _sum8(recv):
    rows = recv.shape[1]
    tr = _div(rows, ROW_BLOCK, 8)

    def body(r_ref, o_ref):
        o_ref[...] = _sum_devices(r_ref)

    return pl.pallas_call(
        body, name="sum8", grid=(rows // tr,),
        in_specs=[pl.BlockSpec((N_DEV, tr, LANE), lambda i: (0, i, 0))],
        out_specs=pl.BlockSpec((tr, LANE), lambda i: (i, 0)),
        out_shape=jax.ShapeDtypeStruct((rows, LANE), F32),
        compiler_params=_params("parallel"),
    )(recv)


def _adamw_math(w, g, m, v):
    m = ADAM_B1 * m + (1.0 - ADAM_B1) * g
    v = ADAM_B2 * v + (1.0 - ADAM_B2) * (g * g)
    m_hat = m / (1.0 - ADAM_B1 ** ADAM_STEP)
    v_hat = v / (1.0 - ADAM_B2 ** ADAM_STEP)
    return -ADAM_LR * (m_hat / (jnp.sqrt(v_hat) + ADAM_EPS) + ADAM_WD * w), m, v


def _adamw_rows(w, g, m, v, name):
    rows = w.shape[0]
    tr = _div(rows, ROW_BLOCK, 8)

    def body(w_ref, g_ref, m_ref, v_ref, d_ref, mo_ref, vo_ref):
        d_ref[...], mo_ref[...], vo_ref[...] = _adamw_math(w_ref[...], g_ref[...], m_ref[...], v_ref[...])

    blk = pl.BlockSpec((tr, LANE), lambda i: (i, 0))
    out = jax.ShapeDtypeStruct((rows, LANE), F32)
    return pl.pallas_call(
        body, name=name, grid=(rows // tr,), in_specs=[blk, blk, blk, blk], out_specs=[blk, blk, blk],
        out_shape=[out, out, out], compiler_params=_params("parallel"),
    )(w, g, m, v)


def _adamw_w_in(recv, w, m, v, tl=128):
    n, depth, d = w.shape

    def body(*refs):
        r_refs, (w_ref, m_ref, v_ref), (g_ref, d_ref, mo_ref, vo_ref) = refs[:depth], refs[depth:depth + 3], refs[-4:]
        for l in range(depth):
            g = _sum_devices(r_refs[l])
            g_ref[:, l, :] = g
            d_ref[:, l, :], mo_ref[:, l, :], vo_ref[:, l, :] = _adamw_math(w_ref[:, l, :], g, m_ref[:, l, :],
                                                                            v_ref[:, l, :])

    blk = pl.BlockSpec((n, depth, tl), lambda j: (0, 0, j))
    out = jax.ShapeDtypeStruct((n, depth, d), F32)
    return pl.pallas_call(
        body, name="adamw_w_in", grid=(d // tl,),
        in_specs=[pl.BlockSpec((N_DEV, n, tl), lambda j: (0, 0, j))] * depth + [blk, blk, blk],
        out_specs=[blk, blk, blk, blk], out_shape=[out, out, out, out],
        compiler_params=_params("parallel"),
    )(*recv, w, m, v)


def _adamw_shard(recv, w, m, v, layer, prev, name):
    _, k, n = recv.shape
    tk = _div(k, 128, 16)

    def body(r_ref, w_ref, m_ref, v_ref, *rest):
        g_ref, d_ref, mo_ref, vo_ref = rest[-4:]
        g = _sum_devices(r_ref)
        g_ref[0] = g
        d_ref[0], mo_ref[0], vo_ref[0] = _adamw_math(w_ref[0], g, m_ref[0], v_ref[0])

    blk = pl.BlockSpec((1, tk, n), lambda i: (layer, i, 0))
    out = jax.ShapeDtypeStruct((DEPTH, k, n), F32)
    carried = [] if prev is None else list(prev)
    return pl.pallas_call(
        body, name=name, grid=(k // tk,),
        in_specs=[pl.BlockSpec((N_DEV, tk, n), lambda i: (0, i, 0)), blk, blk, blk]
        + [pl.BlockSpec(memory_space=pl.ANY)] * len(carried),
        out_specs=[blk, blk, blk, blk], out_shape=[out, out, out, out],
        input_output_aliases={4 + j: j for j in range(len(carried))},
        compiler_params=_params("parallel"),
    )(recv, w, m, v, *carried)


def _to_rows(flat, rows):
    flat = flat.reshape(-1)
    return jnp.pad(flat, (0, rows * LANE - flat.shape[0])).reshape(rows, LANE)


def _pad_z(a, axis):
    f0 = N_QKV
    g0 = N_QKV + FOX_HEADS
    take = lambda lo, hi: lax.slice_in_dim(a, lo, hi, axis=axis)
    shape = list(a.shape)
    shape[axis] = F_PAD - FOX_HEADS
    return jnp.concatenate([take(0, f0), take(g0, N_IN), take(f0, g0), jnp.zeros(shape, a.dtype)], axis=axis)


def _unpad_z(a, axis):
    f0 = N_QKV + N_GATE
    take = lambda lo, hi: lax.slice_in_dim(a, lo, hi, axis=axis)
    return jnp.concatenate([take(0, N_QKV), take(f0, f0 + FOX_HEADS), take(N_QKV, f0)], axis=axis)


def _shards_to_cols(g):
    _, k, n = g.shape
    return g.transpose(1, 0, 2).reshape(k, N_DEV * n)


def _cols_to_shards(full):
    k, n = full.shape
    return full.reshape(k, N_DEV, n // N_DEV).transpose(1, 0, 2)


def _layer_fwd(h, w, p, comm=None, late=None, target=None):
    zq, zg = _z_proj(h, w["w_in_p"], p["b_in_p"])
    qx, kx, vx = _fox_prep(zq, _cumsum_logf(zg))
    attn_a, lse_a = _swa_fwd(zq, p["sinks"])
    attn_b, attn_b32, m_b, l_b, arrived = _fox_fwd(qx, kx, vx, comm)
    if late is not None:
        w, p = late(w, p, arrived)
    h1, u1, merged = _mixer_out(attn_a, attn_b, zg, h, w["w_proj_a"], w["w_proj_b"], w["w_out"],
                                        p["ln_mix_g"], p["ln_mix_b"])
    gu, act = _ffn_in_conv(h1, w["w_ffn_in_fwd"], p["conv_w"], p["conv_b"])
    u2, *h2 = _ffn_out_ln(act, w["w_ffn_out"], h1, p["ln_ffn_g"], p["ln_ffn_b"], target)
    saved = dict(h=h, zq=zq, zg=zg, qx=qx, kx=kx, vx=vx, attn_a=attn_a, lse_a=lse_a, attn_b=attn_b,
                 attn_b32=attn_b32, m_b=m_b, l_b=l_b, h1=h1, u1=u1, merged=merged, gu=gu, act=act, u2=u2)
    return h2, saved, w, p


def _layer_bwd(dh2, sv, w, p, make_comm=None, make_last_comm=None):
    s_len = dh2.shape[0]
    du2, d_ffn_g, d_ffn_b, dact = _ffn_out_bwd(dh2, sv["u2"], p["ln_ffn_g"], w["w_ffn_out"])
    g_ffn_out = _g_w_ffn_out(sv["act"], du2)
    dgu, dcw, dh1 = _conv_bwd_dh1(sv["gu"], dact, p["conv_w"], p["conv_b"], w["w_ffn_in"], du2)
    dcw = dcw.transpose(1, 0, 2).reshape(8, D_FF)
    g_ffn_in = _g_w_ffn_in(sv["h1"], dgu)
    du1, d_mix_g, d_mix_b, dya, dyb, dga, dgb, dattn_a, dattn_b = _mixer_bwd(
        dh1, sv["u1"], p["ln_mix_g"], w["w_out"], sv["attn_a"], sv["attn_b"], sv["zg"], w["w_proj_a"], w["w_proj_b"])
    g_out = _linear_tn(sv["merged"], du1, name="g_w_out", tn=1024)
    g_proj_a = _linear_tn(sv["attn_a"], dya, name="g_w_proj_a", tk=512, tn=1024)
    g_proj_b = _linear_tn(sv["attn_b"], dyb, name="g_w_proj_b", tk=512, tn=1024)
    dq_a, dk_a, dv_a, dsinks = _swa_bwd(sv["zq"], p["sinks"], sv["attn_a"], dattn_a, sv["lse_a"])
    big = dict(w_proj_a=_cols_to_shards(g_proj_a), w_proj_b=_cols_to_shards(g_proj_b),
               w_out=g_out.reshape(N_DEV, D_MODEL // N_DEV, D_MODEL), w_ffn_in=g_ffn_in,
               w_ffn_out=g_ffn_out.reshape(N_DEV, D_FF // N_DEV, D_MODEL))
    dox, stats = _fox_stats(sv["attn_b32"], dattn_b, sv["m_b"], sv["l_b"])
    dq_b, dk_b, dv_b, dcc, arrived = _fox_bwd(sv["qx"], sv["kx"], sv["vx"], dox, stats,
                                              None if make_comm is None else make_comm(big))
    df = _forget_bwd(dcc, sv["zg"])
    dz = jnp.concatenate([dq_a, dk_a.astype(BF16), dv_a.astype(BF16), dq_b.astype(BF16), dk_b, dv_b, dga, dgb, df,
                          jnp.zeros((s_len, F_PAD - LANE), BF16)], axis=1)
    g_in_t, g_b_in = _linear_tn(dz, sv["h"], name="g_w_in", tk=768, tn=1024, colsum=True)
    g_in_t, g_b_in = _unpad_z(g_in_t, 0), _unpad_z(g_b_in, 1)
    big["w_in"] = g_in_t.reshape(N_DEV, N_IN // N_DEV, D_MODEL)
    small = dict(ln_mix_g=d_mix_g, ln_mix_b=d_mix_b, b_in=g_b_in, attn_sinks=dsinks[:, :SWA_HEADS],
                 ln_ffn_g=d_ffn_g, ln_ffn_b=d_ffn_b, conv_w=dcw[:3], conv_b=dcw[3:4])
    dh, arrived_last = _d_h(dz, w["w_in_p"], du1, None if make_last_comm is None else make_last_comm(big, small))
    return dh, big, small, arrived, arrived_last


def _w_in_layouts(w_in):
    return dict(w_in_p=_pad_z(w_in.reshape(N_IN, D_MODEL), 0))


def _other_layouts(w_proj_a, w_proj_b, w_out, w_ffn_in, w_ffn_out):
    return dict(w_proj_a=_shards_to_cols(w_proj_a), w_proj_b=_shards_to_cols(w_proj_b),
                w_out=w_out.reshape(D_MODEL, D_MODEL), w_ffn_in=w_ffn_in,
                w_ffn_in_fwd=w_ffn_in.transpose(0, 2, 1),
                w_ffn_out=w_ffn_out.reshape(N_CHUNK, FF_CHUNK, D_MODEL))


def _layer_params(r):
    return dict(
        b_in_p=_pad_z(r["b_in"].reshape(1, N_IN), 1),
        sinks=jnp.pad(r["attn_sinks"].reshape(1, SWA_HEADS), ((0, 0), (0, LANE - SWA_HEADS))),
        ln_mix_g=r["ln_mix_g"].reshape(1, D_MODEL), ln_mix_b=r["ln_mix_b"].reshape(1, D_MODEL),
        ln_ffn_g=r["ln_ffn_g"].reshape(1, D_MODEL), ln_ffn_b=r["ln_ffn_b"].reshape(1, D_MODEL),
        conv_b=r["conv_b"].reshape(N_CHUNK, 1, FF_CHUNK))


def _conv_w_layout(conv_w):
    return jnp.pad(conv_w, ((0, 5), (0, 0))).reshape(8, N_CHUNK, FF_CHUNK).transpose(1, 0, 2)


def kernel(x, ln_mix_g, ln_mix_b, w_in, b_in, attn_sinks, w_proj_a, w_proj_b, w_out, ln_ffn_g, ln_ffn_b, w_ffn_in, conv_w, conv_b, w_ffn_out, loss_target, m_ln_mix_g, m_ln_mix_b, m_w_in, m_b_in, m_attn_sinks, m_w_proj_a, m_w_proj_b, m_w_out, m_ln_ffn_g, m_ln_ffn_b, m_w_ffn_in, m_conv_w, m_conv_b, m_w_ffn_out, v_ln_mix_g, v_ln_mix_b, v_w_in, v_b_in, v_attn_sinks, v_w_proj_a, v_w_proj_b, v_w_out, v_ln_ffn_g, v_ln_ffn_b, v_w_ffn_in, v_conv_w, v_conv_b, v_w_ffn_out):
    wts = dict(ln_mix_g=ln_mix_g, ln_mix_b=ln_mix_b, w_in=w_in, b_in=b_in, attn_sinks=attn_sinks, w_proj_a=w_proj_a,
               w_proj_b=w_proj_b, w_out=w_out, ln_ffn_g=ln_ffn_g, ln_ffn_b=ln_ffn_b, w_ffn_in=w_ffn_in,
               conv_w=conv_w, conv_b=conv_b, w_ffn_out=w_ffn_out)
    mom = dict(ln_mix_g=m_ln_mix_g, ln_mix_b=m_ln_mix_b, w_in=m_w_in, b_in=m_b_in, attn_sinks=m_attn_sinks,
               w_proj_a=m_w_proj_a, w_proj_b=m_w_proj_b, w_out=m_w_out, ln_ffn_g=m_ln_ffn_g, ln_ffn_b=m_ln_ffn_b,
               w_ffn_in=m_w_ffn_in, conv_w=m_conv_w, conv_b=m_conv_b, w_ffn_out=m_w_ffn_out)
    vel = dict(ln_mix_g=v_ln_mix_g, ln_mix_b=v_ln_mix_b, w_in=v_w_in, b_in=v_b_in, attn_sinks=v_attn_sinks,
               w_proj_a=v_w_proj_a, w_proj_b=v_w_proj_b, w_out=v_w_out, ln_ffn_g=v_ln_ffn_g, ln_ffn_b=v_ln_ffn_b,
               w_ffn_in=v_w_ffn_in, conv_w=v_conv_w, conv_b=v_conv_b, w_ffn_out=v_w_ffn_out)
    names = list(wts)
    big_names = [n for n, _, _ in BIG]
    small_names = [n for n, _ in SMALL]
    me = 4 * lax.axis_index("x") + 2 * lax.axis_index("y") + lax.axis_index("c")
    cw_shard = D_FF // N_DEV

    stored = {"w_in": ((2, 0, 1), (1, 2, 0)), "w_ffn_in": ((0, 2, 1), (0, 2, 1))}
    as_stored = lambda tree: {n: jnp.transpose(tree[n], stored[n][0]) if n in stored else tree[n] for n in big_names}
    w_st, m_st, v_st = as_stored(wts), as_stored(mom), as_stored(vel)
    wb = {n: [(w_st[n][:, l] if n == "w_in" else w_st[n][l]).astype(BF16) for l in range(DEPTH)] for n in big_names}
    ps = [_layer_params(dict(b_in=b_in[l], attn_sinks=attn_sinks[l], ln_mix_g=ln_mix_g[l], ln_mix_b=ln_mix_b[l],
                             ln_ffn_g=ln_ffn_g[l], ln_ffn_b=ln_ffn_b[l], conv_b=conv_b[l])) for l in range(DEPTH)]
    w_in_0 = _gather_two_level(wb["w_in"][0], "gather_w_in_0")
    others = big_names[1:]
    next_layer = {}

    def late_0(w, p, arrived):
        conv_full = arrived[-1].transpose(1, 2, 0, 3).reshape(DEPTH, 3, D_FF)
        next_layer["w"] = _w_in_layouts(arrived[len(others)])
        next_layer["p"] = dict(ps[1], conv_w=_conv_w_layout(conv_full[1]))
        return dict(w, **_other_layouts(*arrived[:len(others)])), dict(p, conv_w=_conv_w_layout(conv_full[0]))

    def late_1(w, p, arrived):
        return dict(w, **_other_layouts(*arrived)), p

    saved, ws = [None] * DEPTH, [None] * DEPTH
    gather_0 = _Comm([(wb[n][0], True) for n in others] + [(wb["w_in"][1], True), (conv_w, True)])
    (h,), saved[0], ws[0], ps[0] = _layer_fwd(x[0], _w_in_layouts(w_in_0), ps[0], gather_0, late_0)
    gather_1 = _Comm([(wb[n][1], True) for n in others])
    (dh, loss_part), saved[1], ws[1], ps[1] = _layer_fwd(h, next_layer["w"], next_layer["p"], gather_1, late_1,
                                                         loss_target[0])

    def small_rows(small):
        vec = jnp.concatenate([small[n].reshape(-1) for n in small_names] + [loss_part[0, 0].reshape(1)])
        return _to_rows(vec, SMALL_LAYER_ROWS)

    dh, big_1, small_1, _, _ = _layer_bwd(dh, saved[1], ws[1], ps[1])

    def exchange_early(big_0):
        return _Comm([(big_1[n].astype(BF16), False) for n in big_names] + [(small_rows(small_1), True)]
                     + [(big_0[n].astype(BF16), False) for n in others])

    def exchange_last(big_0, small_0):
        return _Comm([(big_0["w_in"].astype(BF16), False), (small_rows(small_0), True)])

    grad_x, _, _, arrived, (g_in_0, g_small_0) = _layer_bwd(dh, saved[0], ws[0], ps[0], exchange_early, exchange_last)
    n_big = len(big_names)
    recv = [[g_in_0] + list(arrived[n_big + 1:]) + [g_small_0], list(arrived[:n_big + 1])]

    big_out = {"w_in": _adamw_w_in([recv[l][0] for l in range(DEPTH)], w_st["w_in"], m_st["w_in"], v_st["w_in"])}
    for t, n in enumerate(big_names):
        if n == "w_in":
            continue
        outs = None
        for l in reversed(range(DEPTH)):
            outs = _adamw_shard(recv[l][t], w_st[n], m_st[n], v_st[n], l, outs, "adamw_%s_%d" % (n, l))
        big_out[n] = outs
    for n, (_, back) in stored.items():
        big_out[n] = [jnp.transpose(a, back) for a in big_out[n]]
    small_sum = [_sum8(recv[l][-1]).reshape(-1) for l in range(DEPTH)]
    g_small = {}
    off = 0
    for n, size in SMALL:
        g_small[n] = jnp.stack([small_sum[l][off:off + size] for l in range(DEPTH)])
        off += size
    loss = small_sum[0][off]
    g_small["conv_w"] = lax.dynamic_slice_in_dim(g_small["conv_w"].reshape(DEPTH, 3, D_FF), me * cw_shard, cw_shard,
                                                 axis=2)
    g_small = {n: g_small[n].reshape(wts[n].shape) for n in small_names}

    def pack_small(tree):
        return _to_rows(jnp.concatenate([tree[n].reshape(-1) for n in small_names]), SMALL_ROWS)

    small_out = (pack_small(g_small),) + tuple(_adamw_rows(pack_small(wts), pack_small(g_small), pack_small(mom),
                                                           pack_small(vel), "adamw_small"))

    def result(j):
        out = {n: big_out[n][j] for n in big_names}
        flat = small_out[j].reshape(-1)
        off = 0
        for n in small_names:
            out[n] = flat[off:off + wts[n].size].reshape(wts[n].shape)
            off += wts[n].size
        return [out[n] for n in names]

    return (loss, grad_x[None], *result(0), *result(1), *result(2), *result(3))
```

```python
import jax
import jax.numpy as jnp
import numpy as np
from jax import lax
from jax.experimental import pallas as pl
from jax.experimental.pallas import tpu as pltpu

F32 = jnp.float32
BF16 = jnp.bfloat16
MESH = pl.DeviceIdType.MESH

N_DEV = 8
DEPTH = 2
D_MODEL = 1024
HEAD_DIM = 64
SWA_Q = 512
SWA_KV = 128
FOX_W = 512
FOX_HEADS = 8
SWA_HEADS = 8
D_FF = 2816
N_IN = 4360
N_QKV = SWA_Q + 2 * SWA_KV + 3 * FOX_W
N_GATE = 2 * D_MODEL
F_PAD = 256
N_ZG = N_GATE + F_PAD
N_ZP = N_QKV + N_ZG
LN_EPS = 1e-5
NEG_INF = -1e30
ALPHA = (2 * DEPTH) ** 0.25
SCALE = HEAD_DIM ** -0.5
LOG2E = 1.4426950408889634
SLOPES = tuple(2.0 ** (-8.0 * (h + 1) / SWA_HEADS) for h in range(SWA_HEADS))

ADAM_LR = 0.001
ADAM_B1 = 0.9
ADAM_B2 = 0.999
ADAM_EPS = 1e-08
ADAM_WD = 0.01
ADAM_STEP = 10

LANE = 128
VMEM_LIMIT = 56 * 1024 * 1024

BIG = (("w_in", (D_MODEL, N_IN), 1), ("w_proj_a", (SWA_Q, D_MODEL), 1), ("w_proj_b", (FOX_W, D_MODEL), 1),
       ("w_out", (D_MODEL, D_MODEL), 0), ("w_ffn_in", (D_MODEL, 2 * D_FF), 1), ("w_ffn_out", (D_FF, D_MODEL), 0))
SMALL = (("ln_mix_g", D_MODEL), ("ln_mix_b", D_MODEL), ("b_in", N_IN), ("attn_sinks", SWA_HEADS),
         ("ln_ffn_g", D_MODEL), ("ln_ffn_b", D_MODEL), ("conv_w", 3 * D_FF), ("conv_b", D_FF))
ROW_BLOCK = 512
SMALL_LAYER_ROWS = -(-(sum(n for _, n in SMALL) + 1) // (8 * LANE)) * 8
SMALL_ROWS = ROW_BLOCK
FF_CHUNK = 2 * D_FF // N_DEV
N_CHUNK = D_FF // FF_CHUNK


def _div(n, cap, unit):
    if n <= cap:
        return n
    best = None
    for t in range(unit, cap + 1, unit):
        if n % t == 0:
            best = t
    assert best is not None, (n, cap, unit)
    return best


def _params(*sem):
    return pltpu.CompilerParams(dimension_semantics=sem, vmem_limit_bytes=VMEM_LIMIT)


def _peer(r):
    x, y, c = lax.axis_index("x"), lax.axis_index("y"), lax.axis_index("c")
    px = 1 - x if (r >> 2) & 1 else x
    py = 1 - y if (r >> 1) & 1 else y
    pc = 1 - c if r & 1 else c
    return (px, py, pc), 4 * px + 2 * py + pc


class _Comm:
    def __init__(self, tensors):
        self.arrays = [x for x, _ in tensors]
        self.gathers = [g for _, g in tensors]
        self.n = len(tensors)
        self.out_shape = [jax.ShapeDtypeStruct((N_DEV,) + (x.shape if g else x.shape[1:]), x.dtype)
                          for x, g in tensors]
        self.specs = [pl.BlockSpec(memory_space=pl.ANY)] * self.n
        self.scratch = [pltpu.SemaphoreType.DMA((N_DEV - 1, self.n)), pltpu.SemaphoreType.DMA((N_DEV - 1, self.n)),
                        pltpu.SemaphoreType.DMA((self.n,))]

    def _copies(self, x_refs, out_refs, sems):
        send_sems, recv_sems, local_sems = sems
        _, me = _peer(0)

        def src(t, idx):
            return x_refs[t] if self.gathers[t] else x_refs[t].at[idx]

        def remote(r, t, mine):
            peer, pid = _peer(r)
            return pltpu.make_async_remote_copy(src_ref=src(t, pid), dst_ref=out_refs[t].at[me if mine else pid],
                                                send_sem=send_sems.at[r - 1, t], recv_sem=recv_sems.at[r - 1, t],
                                                device_id=peer, device_id_type=MESH)

        pairs = [(r, t) for r in range(1, N_DEV) for t in range(self.n)]
        local = [pltpu.make_async_copy(src(t, me), out_refs[t].at[me], local_sems.at[t]) for t in range(self.n)]
        return local, [remote(r, t, True) for r, t in pairs], lambda: [remote(r, t, False) for r, t in pairs]

    def start(self, x_refs, out_refs, sems):
        local, sent, _ = self._copies(x_refs, out_refs, sems)
        for cp in local + sent:
            cp.start()

    def wait(self, x_refs, out_refs, sems):
        local, sent, landing = self._copies(x_refs, out_refs, sems)
        for cp in landing():
            cp.wait_recv()
        for cp in sent:
            cp.wait_send()
        for cp in local:
            cp.wait()


def _gather_two_level(x, name):
    def body(x_ref, out_ref, send_sems, recv_sems, local_sem):
        x_, y_, c_ = lax.axis_index("x"), lax.axis_index("y"), lax.axis_index("c")
        me, sibling = (x_, y_, c_), (x_, y_, 1 - c_)
        chips = [(1 - x_, y_), (x_, 1 - y_), (1 - x_, 1 - y_)]

        def slab(px, py, pc):
            return out_ref.at[4 * px + 2 * py + pc]

        def copy(k, block, to, src=None):
            return pltpu.make_async_remote_copy(src_ref=slab(*block) if src is None else src, dst_ref=slab(*block),
                                                send_sem=send_sems.at[k], recv_sem=recv_sems.at[k], device_id=to,
                                                device_id_type=MESH)

        mine = pltpu.make_async_copy(x_ref, slab(*me), local_sem)
        mine.start()
        first = [copy(0, me, sibling, src=x_ref)] + [copy(1 + j, me, (*chip, c_), src=x_ref)
                                                     for j, chip in enumerate(chips)]
        for cp in first:
            cp.start()
        passed = [copy(4 + j, (*chip, c_), sibling) for j, chip in enumerate(chips)]
        for j, chip in enumerate(chips):
            copy(1 + j, (*chip, c_), me).wait_recv()
            passed[j].start()
        copy(0, sibling, me).wait_recv()
        for j, chip in enumerate(chips):
            copy(4 + j, (*chip, 1 - c_), me).wait_recv()
        for cp in first + passed:
            cp.wait_send()
        mine.wait()

    spec = pl.BlockSpec(memory_space=pl.ANY)
    return pl.pallas_call(
        body, name=name, out_shape=jax.ShapeDtypeStruct((N_DEV,) + x.shape, x.dtype), in_specs=[spec], out_specs=spec,
        scratch_shapes=[pltpu.SemaphoreType.DMA((N_DEV - 1,)), pltpu.SemaphoreType.DMA((N_DEV - 1,)),
                        pltpu.SemaphoreType.DMA],
    )(x)


def _with_comm(comm, n_in, n_out, first, last, compute):
    nc = comm.n if comm is not None else 0

    def body(*refs):
        ins, x_refs = refs[:n_in], refs[n_in:n_in + nc]
        outs = refs[n_in + nc:n_in + nc + n_out]
        out_refs = refs[n_in + nc + n_out:n_in + 2 * nc + n_out]
        sems = refs[n_in + 2 * nc + n_out:]
        if nc:
            @pl.when(first())
            def _():
                comm.start(x_refs, out_refs, sems)

        compute(*ins, *outs)
        if nc:
            @pl.when(last())
            def _():
                comm.wait(x_refs, out_refs, sems)

    return body


def _d_h(dz, w_in_t, res, comm=None, tm=512):
    m, k = dz.shape
    d = w_in_t.shape[1]
    tm = _div(m, tm, 8)
    steps = m // tm
    c_specs, c_shapes, c_scratch, c_arrays = _comm_parts(comm)

    def compute(dz_ref, w_ref, res_ref, o_ref):
        o_ref[...] = ALPHA * res_ref[...] + jnp.dot(dz_ref[...], w_ref[...], preferred_element_type=F32)

    body = _with_comm(comm, 3, 1, lambda: pl.program_id(0) == 0, lambda: pl.program_id(0) == steps - 1, compute)
    row = pl.BlockSpec((tm, d), lambda i: (i, 0))
    outs = pl.pallas_call(
        body, name="d_h" if comm is None else "d_h_comm", grid=(steps,),
        in_specs=[pl.BlockSpec((tm, k), lambda i: (i, 0)), pl.BlockSpec((k, d), lambda i: (0, 0)), row] + c_specs,
        out_specs=[row] + c_specs, out_shape=[jax.ShapeDtypeStruct((m, d), F32)] + c_shapes,
        scratch_shapes=c_scratch,
        compiler_params=_params("arbitrary"),
    )(dz, w_in_t, res, *c_arrays)
    return outs[0], outs[1:]


def _z_proj(h, w_in_t, b_p, tm=512):
    m, k = h.shape
    tm = _div(m, tm, 8)
    nt = (((1,), (1,)), ((), ()))

    def body(h_ref, w_ref, b_ref, zq_ref, zg_ref):
        a = h_ref[...].astype(BF16)
        zq = lax.dot_general(a, w_ref[:N_QKV, :], nt, preferred_element_type=F32)
        zq_ref[...] = (zq + b_ref[:, :N_QKV]).astype(BF16)
        zg_ref[...] = lax.dot_general(a, w_ref[N_QKV:, :], nt, preferred_element_type=F32) + b_ref[:, N_QKV:]

    return pl.pallas_call(
        body, name="z_proj", grid=(m // tm,),
        in_specs=[pl.BlockSpec((tm, k), lambda i: (i, 0)), pl.BlockSpec((N_ZP, k), lambda i: (0, 0)),
                  pl.BlockSpec((1, N_ZP), lambda i: (0, 0))],
        out_specs=[pl.BlockSpec((tm, N_QKV), lambda i: (i, 0)), pl.BlockSpec((tm, N_ZG), lambda i: (i, 0))],
        out_shape=[jax.ShapeDtypeStruct((m, N_QKV), BF16), jax.ShapeDtypeStruct((m, N_ZG), F32)],
        compiler_params=_params("parallel"),
    )(h, w_in_t, b_p)


def _linear_tn(a, g, *, name, tk=1024, tn=640, tm=2048, colsum=False):
    m, k = a.shape
    n = g.shape[1]
    tk = _div(k, tk, LANE)
    tn = _div(n, tn, LANE)
    tm = _div(m, tm, 8)
    steps = m // tm
    assert not colsum or tn == n

    def body(a_ref, g_ref, o_ref, *rest):
        acc_ref = rest[-1]
        s = pl.program_id(2)

        @pl.when(s == 0)
        def _():
            acc_ref[...] = jnp.zeros_like(acc_ref)
            if colsum:
                rest[0][...] = jnp.zeros_like(rest[0])

        a_blk = a_ref[...]
        acc_ref[...] += lax.dot_general(a_blk.astype(BF16), g_ref[...].astype(BF16), (((0,), (0,)), ((), ())),
                                        preferred_element_type=F32)
        if colsum:
            rest[0][...] += jnp.sum(a_blk.astype(F32), axis=0, keepdims=True)

        @pl.when(s == steps - 1)
        def _():
            o_ref[...] = acc_ref[...].astype(BF16)

    out_specs = [pl.BlockSpec((tk, tn), lambda i, j, s: (i, j))]
    out_shape = [jax.ShapeDtypeStruct((k, n), BF16)]
    if colsum:
        out_specs.append(pl.BlockSpec((1, tk), lambda i, j, s: (0, i)))
        out_shape.append(jax.ShapeDtypeStruct((1, k), F32))
    outs = pl.pallas_call(
        body, name=name, grid=(k // tk, n // tn, steps),
        in_specs=[pl.BlockSpec((tm, tk), lambda i, j, s: (s, i)), pl.BlockSpec((tm, tn), lambda i, j, s: (s, j))],
        out_specs=out_specs, out_shape=out_shape,
        scratch_shapes=[pltpu.VMEM((tk, tn), F32)],
        compiler_params=_params("parallel", "parallel", "arbitrary"),
    )(a, g)
    return outs if colsum else outs[0]


def _ln(u, g, b):
    mu = jnp.mean(u, axis=-1, keepdims=True)
    d = u - mu
    var = jnp.mean(d * d, axis=-1, keepdims=True)
    return d * lax.rsqrt(var + LN_EPS) * g + b


def _ln_bwd_block(dy, u, g):
    mu = jnp.mean(u, axis=-1, keepdims=True)
    dd = u - mu
    rstd = lax.rsqrt(jnp.mean(dd * dd, axis=-1, keepdims=True) + LN_EPS)
    xhat = dd * rstd
    dxh = dy * g
    m1 = jnp.mean(dxh, axis=-1, keepdims=True)
    m2 = jnp.mean(dxh * xhat, axis=-1, keepdims=True)
    return (rstd * (dxh - m1 - xhat * m2), jnp.sum(dy * xhat, axis=0, keepdims=True),
            jnp.sum(dy, axis=0, keepdims=True))


SCAN_ROWS = 512


def _tri(n, upper):
    r = lax.broadcasted_iota(jnp.int32, (n, n), 0)
    c = lax.broadcasted_iota(jnp.int32, (n, n), 1)
    return jnp.where((c >= r) if upper else (c <= r), 1.0, 0.0).astype(F32)


def _cumsum_logf(zg):
    s = zg.shape[0]
    t = _div(s, SCAN_ROWS, LANE)
    nb = s // t
    fcol = N_GATE // LANE

    def body(f_ref, c_ref, carry_ref):
        @pl.when(pl.program_id(0) == 0)
        def _():
            carry_ref[...] = jnp.zeros_like(carry_ref)

        f = f_ref[...]
        logf = jnp.minimum(f, 0.0) - jnp.log(1.0 + jnp.exp(-jnp.abs(f)))
        c = jnp.dot(_tri(t, False), logf, precision=lax.Precision.HIGHEST, preferred_element_type=F32)
        c = c + carry_ref[0:1, :]
        c_ref[...] = c
        carry_ref[...] = jnp.broadcast_to(c[t - 1:t, :], carry_ref.shape)

    return pl.pallas_call(
        body, name="cumsum_logf", grid=(nb,),
        in_specs=[pl.BlockSpec((t, LANE), lambda i: (i, fcol))],
        out_specs=pl.BlockSpec((t, LANE), lambda i: (i, 0)),
        out_shape=jax.ShapeDtypeStruct((s, LANE), F32),
        scratch_shapes=[pltpu.VMEM((8, LANE), F32)],
        compiler_params=_params("arbitrary"),
    )(zg)


def _forget_bwd(dcc, zg):
    s = zg.shape[0]
    t = _div(s, SCAN_ROWS, LANE)
    nb = s // t
    fcol = N_GATE // LANE

    def body(dc_ref, f_ref, o_ref, carry_ref):
        @pl.when(pl.program_id(0) == 0)
        def _():
            carry_ref[...] = jnp.zeros_like(carry_ref)

        lane = lax.broadcasted_iota(jnp.int32, (1, LANE), 1)
        dc = jnp.zeros((t, LANE), F32)
        for p in range(FOX_HEADS // 2):
            tile = dc_ref[:, p * LANE:(p + 1) * LANE]
            moved = pltpu.roll(tile, 2 * p, 1) if p else tile
            dc = jnp.where((lane == 2 * p) | (lane == 2 * p + 1), moved, dc)
        dlogf = jnp.dot(_tri(t, True), dc, precision=lax.Precision.HIGHEST, preferred_element_type=F32)
        dlogf = dlogf + carry_ref[0:1, :]
        o_ref[...] = (dlogf * jax.nn.sigmoid(-f_ref[...])).astype(BF16)
        carry_ref[...] = jnp.broadcast_to(dlogf[0:1, :], carry_ref.shape)

    return pl.pallas_call(
        body, name="forget_bwd", grid=(nb,),
        in_specs=[pl.BlockSpec((t, FOX_W), lambda i: (nb - 1 - i, 0)),
                  pl.BlockSpec((t, LANE), lambda i: (nb - 1 - i, fcol))],
        out_specs=pl.BlockSpec((t, LANE), lambda i: (nb - 1 - i, 0)),
        out_shape=jax.ShapeDtypeStruct((s, LANE), BF16),
        scratch_shapes=[pltpu.VMEM((8, LANE), F32)],
        compiler_params=_params("arbitrary"),
    )(dcc, zg)


KA_COL = SWA_Q // LANE
VA_COL = KA_COL + 1


def _half_masks():
    lane = lax.broadcasted_iota(jnp.int32, (1, LANE), 1)
    hi = lane >= HEAD_DIM
    return (jnp.logical_not(hi), hi)


def _both_halves(x, sel):
    xs = jnp.where(sel, x, 0.0)
    return xs + pltpu.roll(xs, HEAD_DIM, 1)


SWA_PER_KV = 4
WIDE = SWA_PER_KV * LANE


def _swa_bias():
    k = np.arange(2 * LANE)[:, None]
    q = np.arange(LANE)[None, :]
    dist = (q + LANE - k).astype(np.float32)
    valid = (dist >= 0) & (dist < LANE)
    per_head = [np.where(valid, np.float32(-s) * dist, np.float32(NEG_INF)) for s in SLOPES]
    return jnp.asarray(np.stack([np.concatenate(per_head[SWA_PER_KV * hk:SWA_PER_KV * (hk + 1)], axis=1)
                                 for hk in range(2)]), F32)


def _no_previous_block(i_blk):
    k = lax.broadcasted_iota(jnp.int32, (2 * LANE, WIDE), 0)
    return jnp.where((i_blk == 0) & (k < LANE), NEG_INF, 0.0)


def _stack_heads(ref, blk, hk, halves, scale):
    tiles = []
    for j in range(SWA_PER_KV):
        p = 2 * hk + j // 2
        t = ref[blk, p * LANE:(p + 1) * LANE]
        if scale:
            t = _scaled(t)
        tiles.append(jnp.where(halves[j % 2], t, jnp.zeros_like(t)))
    return jnp.concatenate(tiles, axis=0)


def _pair_tile(wide, pp, row_halves):
    a = wide[:, (2 * pp) * LANE:(2 * pp + 1) * LANE]
    b = wide[:, (2 * pp + 1) * LANE:(2 * pp + 2) * LANE]
    return jnp.where(row_halves[0], a, b).T


def _lane_blocks(rows8, hk):
    return jnp.concatenate([rows8[SWA_PER_KV * hk + j:SWA_PER_KV * hk + j + 1, :] for j in range(SWA_PER_KV)], axis=1)


def _row_halves():
    hi = lax.broadcasted_iota(jnp.int32, (LANE, 1), 0) >= HEAD_DIM
    return (jnp.logical_not(hi), hi)


NT = (((1,), (1,)), ((), ()))


def _scaled(q):
    return (q.astype(F32) * SCALE).astype(BF16)


SWA_GROUP = 4


def _swa_group(s_len):
    return SWA_GROUP if (s_len // LANE) % SWA_GROUP == 0 else 1


def _swa_specs(group):
    rows = group * LANE
    prev = lambda i: jnp.maximum(i * group - 1, 0)
    return [pl.BlockSpec((rows, SWA_Q), lambda i: (i, 0)),
            pl.BlockSpec((rows, LANE), lambda i: (i, KA_COL)), pl.BlockSpec((rows, LANE), lambda i: (i, VA_COL)),
            pl.BlockSpec((LANE, LANE), lambda i: (prev(i), KA_COL)),
            pl.BlockSpec((LANE, LANE), lambda i: (prev(i), VA_COL))]


def _swa_window(g, cur_ref, prev_ref):
    before = prev_ref[...] if g == 0 else cur_ref[(g - 1) * LANE:g * LANE, :]
    return jnp.concatenate([before, cur_ref[g * LANE:(g + 1) * LANE, :]], axis=0).astype(F32)


def _swa_fwd(zq, sinks):
    s_len = zq.shape[0]
    group = _swa_group(s_len)
    rows = group * LANE
    sink_lanes = jnp.repeat(sinks[:, :SWA_HEADS], LANE, axis=1)

    def body(q_ref, kc_ref, vc_ref, kp_ref, vp_ref, sink_ref, bias_ref, o_ref, lse_ref):
        halves = _half_masks()
        row_halves = _row_halves()
        for g in range(group):
            blk = slice(g * LANE, (g + 1) * LANE)
            kcat = _swa_window(g, kc_ref, kp_ref)
            vcat = _swa_window(g, vc_ref, vp_ref)
            lse_rows = []
            for hk in range(2):
                kb = _both_halves(kcat, halves[hk]).astype(BF16)
                v_t = _both_halves(vcat, halves[hk]).T.astype(BF16)
                q4 = _stack_heads(q_ref, blk, hk, halves, True)
                s_t = lax.dot_general(kb, q4, NT, preferred_element_type=F32) + bias_ref[hk]
                if g == 0:
                    s_t = s_t + _no_previous_block(pl.program_id(0))
                sink = sink_ref[:, hk * WIDE:(hk + 1) * WIDE]
                m = jnp.maximum(jnp.max(s_t, axis=0, keepdims=True), sink)
                pe = jnp.exp(s_t - m)
                den = jnp.sum(pe, axis=0, keepdims=True) + jnp.exp(sink - m)
                out_t = jnp.dot(v_t, (pe * (1.0 / den)).astype(BF16), preferred_element_type=F32)
                for pp in range(2):
                    p = 2 * hk + pp
                    o_ref[blk, p * LANE:(p + 1) * LANE] = _pair_tile(out_t, pp, row_halves).astype(BF16)
                lse4 = m + jnp.log(den)
                lse_rows += [lse4[:, j * LANE:(j + 1) * LANE] for j in range(SWA_PER_KV)]
            lse_ref[:, blk] = jnp.concatenate(lse_rows, axis=0)

    return pl.pallas_call(
        body, name="swa_fwd", grid=(s_len // rows,),
        in_specs=_swa_specs(group) + [pl.BlockSpec((1, SWA_HEADS * LANE), lambda i: (0, 0)),
                                      pl.BlockSpec((2, 2 * LANE, WIDE), lambda i: (0, 0, 0))],
        out_specs=[pl.BlockSpec((rows, SWA_Q), lambda i: (i, 0)), pl.BlockSpec((SWA_HEADS, rows), lambda i: (0, i))],
        out_shape=[jax.ShapeDtypeStruct((s_len, SWA_Q), BF16), jax.ShapeDtypeStruct((SWA_HEADS, s_len), F32)],
        compiler_params=_params("parallel"),
    )(zq, zq, zq, zq, zq, sink_lanes, _swa_bias())


def _swa_bwd(zq, sinks, o, do, lse):
    s_len = zq.shape[0]
    group = _swa_group(s_len)
    rows = group * LANE

    def body(q_ref, kc_ref, vc_ref, kp_ref, vp_ref, sink_ref, bias_ref, o_ref, do_ref, lse_ref,
             dq_ref, dk_ref, dv_ref, ds_ref):
        halves = _half_masks()
        row_halves = _row_halves()
        lane = lax.broadcasted_iota(jnp.int32, (1, LANE), 1)
        dsink = jnp.zeros((1, LANE), F32)
        for g in range(group):
            blk = slice(g * LANE, (g + 1) * LANE)
            i_blk = pl.program_id(0) * group + g
            kcat = _swa_window(g, kc_ref, kp_ref)
            vcat = _swa_window(g, vc_ref, vp_ref)
            lse_rows = lse_ref[:, blk]
            prod = do_ref[blk, :].astype(F32) * o_ref[blk, :].astype(F32)
            select = (lax.broadcasted_iota(jnp.int32, (SWA_HEADS, SWA_Q), 1) // HEAD_DIM
                      == lax.broadcasted_iota(jnp.int32, (SWA_HEADS, SWA_Q), 0))
            picks = jnp.where(select, 1.0, 0.0).astype(BF16)
            prod_hi = prod.astype(BF16)
            prod_lo = (prod - prod_hi.astype(F32)).astype(BF16)
            delta_rows = (lax.dot_general(picks, prod_hi, NT, preferred_element_type=F32)
                          + lax.dot_general(picks, prod_lo, NT, preferred_element_type=F32))
            dk_tot = jnp.zeros((2 * LANE, LANE), F32)
            dv_tot = jnp.zeros((2 * LANE, LANE), F32)
            for hk in range(2):
                kb = _both_halves(kcat, halves[hk])
                k_t = kb.T.astype(BF16)
                kb = kb.astype(BF16)
                vb = _both_halves(vcat, halves[hk]).astype(BF16)
                q4 = _stack_heads(q_ref, blk, hk, halves, True)
                do4 = _stack_heads(do_ref, blk, hk, halves, False)
                lse4 = _lane_blocks(lse_rows, hk)
                delta4 = _lane_blocks(delta_rows, hk)
                s_t = lax.dot_general(kb, q4, NT, preferred_element_type=F32) + bias_ref[hk]
                if g == 0:
                    s_t = s_t + _no_previous_block(pl.program_id(0))
                p_t = jnp.exp(s_t - lse4)
                dp_t = lax.dot_general(vb, do4, NT, preferred_element_type=F32)
                ds_t = (p_t * (dp_t - delta4)).astype(BF16)
                sink_part = jnp.exp(sink_ref[:, hk * WIDE:(hk + 1) * WIDE] - lse4) * delta4
                for j in range(SWA_PER_KV):
                    dsink_h = -jnp.sum(sink_part[:, j * LANE:(j + 1) * LANE], axis=1, keepdims=True)
                    dsink = dsink + jnp.where(lane == SWA_PER_KV * hk + j, dsink_h, 0.0)
                dq_t = jnp.dot(k_t, ds_t, preferred_element_type=F32)
                for pp in range(2):
                    p = 2 * hk + pp
                    dq_ref[blk, p * LANE:(p + 1) * LANE] = (_pair_tile(dq_t, pp, row_halves) * SCALE).astype(BF16)
                dk_acc = jnp.dot(ds_t, q4, preferred_element_type=F32)
                dv_acc = jnp.dot(p_t.astype(BF16), do4, preferred_element_type=F32)
                dk_tot = dk_tot + jnp.where(halves[hk], dk_acc + pltpu.roll(dk_acc, HEAD_DIM, 1), 0.0)
                dv_tot = dv_tot + jnp.where(halves[hk], dv_acc + pltpu.roll(dv_acc, HEAD_DIM, 1), 0.0)
            cur = pl.ds(pl.multiple_of(i_blk * LANE, LANE), LANE)
            dk_ref[cur, :] = dk_tot[LANE:, :]
            dv_ref[cur, :] = dv_tot[LANE:, :]

            def add_previous(i_blk=i_blk, dk_tot=dk_tot, dv_tot=dv_tot):
                prv = pl.ds(pl.multiple_of((i_blk - 1) * LANE, LANE), LANE)
                dk_ref[prv, :] += dk_tot[:LANE, :]
                dv_ref[prv, :] += dv_tot[:LANE, :]

            if g == 0:
                pl.when(i_blk > 0)(add_previous)
            else:
                add_previous()

        @pl.when(pl.program_id(0) == 0)
        def _():
            ds_ref[...] = jnp.zeros_like(ds_ref)

        ds_ref[...] += dsink

    blk512 = pl.BlockSpec((rows, SWA_Q), lambda i: (i, 0))
    full = pl.BlockSpec((s_len, LANE), lambda i: (0, 0))
    vec = pl.BlockSpec((1, LANE), lambda i: (0, 0))
    return pl.pallas_call(
        body, name="swa_bwd", grid=(s_len // rows,),
        in_specs=_swa_specs(group) + [pl.BlockSpec((1, SWA_HEADS * LANE), lambda i: (0, 0)),
                                      pl.BlockSpec((2, 2 * LANE, WIDE), lambda i: (0, 0, 0)), blk512, blk512,
                                      pl.BlockSpec((SWA_HEADS, rows), lambda i: (0, i))],
        out_specs=[blk512, full, full, vec],
        out_shape=[jax.ShapeDtypeStruct((s_len, SWA_Q), BF16), jax.ShapeDtypeStruct((s_len, LANE), F32),
                   jax.ShapeDtypeStruct((s_len, LANE), F32), jax.ShapeDtypeStruct((1, LANE), F32)],
        compiler_params=_params("arbitrary"),
    )(zq, zq, zq, zq, zq, jnp.repeat(sinks[:, :SWA_HEADS], LANE, axis=1), _swa_bias(), o, do, lse)


QB_COL = (SWA_Q + 2 * SWA_KV) // LANE
KB_COL = QB_COL + FOX_W // LANE
VB_COL = KB_COL + FOX_W // LANE
N_PAIR = FOX_HEADS // 2


def _causal(t, keys_first=False):
    r = lax.broadcasted_iota(jnp.int32, (t, t), 0)
    c = lax.broadcasted_iota(jnp.int32, (t, t), 1)
    return c >= r if keys_first else r >= c


N_SPLIT = 3


def _own_half(e):
    hi = lax.broadcasted_iota(jnp.int32, (1, LANE), 1) >= HEAD_DIM
    return hi if e else jnp.logical_not(hi)


def _feature_lane(e, t):
    return HEAD_DIM * (1 - e) + t


def _feature_tables():
    wide = FOX_HEADS * LANE
    place_q, place_k = np.zeros((N_SPLIT * LANE, wide), np.float32), np.zeros((N_SPLIT * LANE, wide), np.float32)
    ones_q, ones_k, ones_v, own = (np.zeros((1, wide), np.float32) for _ in range(4))
    for h in range(FOX_HEADS):
        e = h % 2
        own[0, h * LANE + HEAD_DIM * e:h * LANE + HEAD_DIM * (e + 1)] = 1.0
        ones_v[0, h * LANE + _feature_lane(e, 0)] = 1.0
        for t in range(N_SPLIT):
            place_q[t * LANE + h, h * LANE + _feature_lane(e, t)] = 1.0
            ones_q[0, h * LANE + _feature_lane(e, N_SPLIT + t)] = 1.0
            ones_k[0, h * LANE + _feature_lane(e, t)] = 1.0
            place_k[t * LANE + h, h * LANE + _feature_lane(e, N_SPLIT + t)] = -1.0
    return tuple(jnp.asarray(a) for a in (place_q, place_k, ones_q, ones_k, ones_v, own))


def _fox_prep(zq, c, tm=256):
    s_len = zq.shape[0]
    tm = _div(s_len, tm, 8)
    wide = FOX_HEADS * LANE

    def body(z_ref, c_ref, pq_ref, pk_ref, oq_ref, ok_ref, ov_ref, own_ref, qx_ref, kx_ref, vx_ref):
        rest = c_ref[...]
        parts = []
        for _ in range(N_SPLIT):
            part = rest.astype(BF16).astype(F32)
            rest = rest - part
            parts.append(part)
        parts = jnp.concatenate(parts, axis=1)
        qf = jnp.dot(parts, pq_ref[...], preferred_element_type=F32) + oq_ref[...]
        kf = jnp.dot(parts, pk_ref[...], preferred_element_type=F32) + ok_ref[...]
        own = own_ref[...] > 0.5
        for p in range(N_PAIR):
            cols = slice(2 * p * LANE, (2 * p + 2) * LANE)
            pair = lambda col: jnp.tile(z_ref[:, (col + p) * LANE:(col + p + 1) * LANE].astype(F32), (1, 2))
            qx_ref[:, cols] = jnp.where(own[:, cols], pair(QB_COL) * SCALE, qf[:, cols]).astype(BF16)
            kx_ref[:, cols] = jnp.where(own[:, cols], pair(KB_COL), kf[:, cols]).astype(BF16)
            vx_ref[:, cols] = jnp.where(own[:, cols], pair(VB_COL), ov_ref[:, cols]).astype(BF16)

    out = jax.ShapeDtypeStruct((s_len, wide), BF16)
    blk = pl.BlockSpec((tm, wide), lambda i: (i, 0))
    table = pl.BlockSpec((N_SPLIT * LANE, wide), lambda i: (0, 0))
    vec = pl.BlockSpec((1, wide), lambda i: (0, 0))
    return pl.pallas_call(
        body, name="fox_prep", grid=(s_len // tm,),
        in_specs=[pl.BlockSpec((tm, N_QKV), lambda i: (i, 0)), pl.BlockSpec((tm, LANE), lambda i: (i, 0)),
                  table, table, vec, vec, vec, vec],
        out_specs=[blk, blk, blk], out_shape=[out, out, out],
        compiler_params=_params("parallel"),
    )(zq, c, *_feature_tables())


def _comm_parts(comm):
    return ([], [], [], []) if comm is None else (comm.specs, comm.out_shape, comm.scratch, comm.arrays)


def _fox_fwd(qx, kx, vx, comm=None, t_cap=1024):
    s_len = qx.shape[0]
    t = _div(s_len, t_cap, LANE)
    nq = s_len // t
    c_specs, c_shapes, c_scratch, c_arrays = _comm_parts(comm)

    def compute(q_ref, k_ref, v_ref, o_ref, o32_ref, m_ref, l_ref):
        i = pl.program_id(1)
        qs = [q_ref[:, e * LANE:(e + 1) * LANE] for e in range(2)]

        def step(j, carry, diag):
            rows = pl.ds(pl.multiple_of(j * t, t), t)
            new = []
            for e in range(2):
                m, acc = carry[e]
                s2 = lax.dot_general(qs[e], k_ref[rows, e * LANE:(e + 1) * LANE], NT,
                                     preferred_element_type=F32) * LOG2E
                if diag:
                    s2 = jnp.where(_causal(t), s2, NEG_INF)
                mn = jnp.maximum(m, jnp.ceil(jnp.max(s2, axis=1, keepdims=True)))
                pe = jnp.exp2(s2 - mn).astype(BF16)
                acc = acc * jnp.exp2(m - mn) + jnp.dot(pe, v_ref[rows, e * LANE:(e + 1) * LANE],
                                                       preferred_element_type=F32)
                new.append((mn, acc))
            return tuple(new)

        init = (jnp.full((t, 1), NEG_INF, F32), jnp.zeros((t, LANE), F32))
        carry = lax.fori_loop(0, i, lambda j, c: step(j, c, False), (init, init))
        carry = step(i, carry, True)
        outs, ls = [], []
        for e in range(2):
            m, acc = carry[e]
            l = acc[:, _feature_lane(e, 0):_feature_lane(e, 0) + 1]
            outs.append(acc / l)
            ls.append(l)
        out = jnp.where(_own_half(1), outs[1], outs[0])
        o_ref[...] = out.astype(BF16)
        o32_ref[...] = out
        m_ref[...] = jnp.where(_own_half(1), carry[1][0], carry[0][0])
        l_ref[...] = jnp.where(_own_half(1), ls[1], ls[0])

    body = _with_comm(comm, 3, 4, lambda: (pl.program_id(0) == 0) & (pl.program_id(1) == 0),
                      lambda: (pl.program_id(0) == N_PAIR - 1) & (pl.program_id(1) == nq - 1), compute)
    pair = pl.BlockSpec((s_len, 2 * LANE), lambda p, i: (0, p))
    tile = pl.BlockSpec((t, LANE), lambda p, i: (i, p))
    wide = jax.ShapeDtypeStruct((s_len, FOX_W), F32)
    outs = pl.pallas_call(
        body, name="fox_fwd" if comm is None else "fox_fwd_comm%d" % comm.n, grid=(N_PAIR, nq),
        in_specs=[pl.BlockSpec((t, 2 * LANE), lambda p, i: (i, p)), pair, pair] + c_specs,
        out_specs=[tile, tile, tile, tile] + c_specs,
        out_shape=[jax.ShapeDtypeStruct((s_len, FOX_W), BF16), wide, wide, wide] + c_shapes,
        scratch_shapes=c_scratch,
        compiler_params=_params("arbitrary", "arbitrary"),
    )(qx, kx, vx, *c_arrays)
    return outs[0], outs[1], outs[2], outs[3], outs[4:]


def _fox_stats(o, do, m, l, tm=256):
    s_len = o.shape[0]
    tm = _div(s_len, tm, LANE)

    def body(o_ref, do_ref, m_ref, l_ref, dox_ref, st_ref):
        lane = lax.broadcasted_iota(jnp.int32, (1, LANE), 1)
        for p in range(N_PAIR):
            cols = slice(p * LANE, (p + 1) * LANE)
            dout = do_ref[:, cols]
            prod = o_ref[:, cols] * dout.astype(F32)
            shift = m_ref[:, cols]
            inv_l = 1.0 / l_ref[:, cols]
            st = jnp.zeros((tm, LANE), F32)
            for e in range(2):
                h = 2 * p + e
                dox_ref[:, h * LANE:(h + 1) * LANE] = jnp.where(_own_half(e), dout, jnp.zeros_like(dout))
                st = jnp.where(lane == e, shift[:, e * HEAD_DIM:e * HEAD_DIM + 1], st)
                delta = jnp.sum(jnp.where(_own_half(e), prod, 0.0), axis=1, keepdims=True)
                st = jnp.where(lane == 2 + e, delta, st)
                st = jnp.where(lane == 4 + e, inv_l[:, e * HEAD_DIM:e * HEAD_DIM + 1], st)
            st_ref[p] = st.T[:8, :]

    row = pl.BlockSpec((tm, FOX_W), lambda i: (i, 0))
    return pl.pallas_call(
        body, name="fox_stats", grid=(s_len // tm,), in_specs=[row, row, row, row],
        out_specs=[pl.BlockSpec((tm, FOX_HEADS * LANE), lambda i: (i, 0)),
                   pl.BlockSpec((N_PAIR, 8, tm), lambda i: (0, 0, i))],
        out_shape=[jax.ShapeDtypeStruct((s_len, FOX_HEADS * LANE), BF16),
                   jax.ShapeDtypeStruct((N_PAIR, 8, s_len), F32)],
        compiler_params=_params("parallel"),
    )(o, do, m, l)


def _fox_bwd(qx, kx, vx, dox, stats, comm=None, t_cap=1024):
    s_len = qx.shape[0]
    t = _div(s_len, t_cap, LANE)
    n = s_len // t
    c_specs, c_shapes, c_scratch, c_arrays = _comm_parts(comm)

    def compute(q_ref, do_ref, st_ref, k_ref, v_ref, dq_ref, dk_ref, dv_ref, dc_ref):
        j = pl.program_id(1)
        lane = lax.broadcasted_iota(jnp.int32, (1, LANE), 1)

        @pl.when(j == 0)
        def _():
            dq_ref[...] = jnp.zeros_like(dq_ref)

        ks = [k_ref[:, e * LANE:(e + 1) * LANE] for e in range(2)]
        vs = [v_ref[:, e * LANE:(e + 1) * LANE] for e in range(2)]
        ks_t = [k.astype(F32).T.astype(BF16) for k in ks]

        def step(i, carry, diag):
            rows = pl.ds(pl.multiple_of(i * t, t), t)
            new = []
            dq = jnp.zeros((LANE, t), F32)
            for e in range(2):
                dk, dv, dc = carry[e]
                q = q_ref[rows, e * LANE:(e + 1) * LANE]
                dout = do_ref[rows, e * LANE:(e + 1) * LANE]
                s_t = lax.dot_general(ks[e], q, NT, preferred_element_type=F32) * LOG2E
                if diag:
                    s_t = jnp.where(_causal(t, keys_first=True), s_t, NEG_INF)
                p_t = jnp.exp2(s_t - st_ref[0, e:e + 1, rows]).astype(BF16).astype(F32) * st_ref[0, 4 + e:5 + e, rows]
                dp_t = lax.dot_general(vs[e], dout, NT, preferred_element_type=F32)
                ds_f = p_t * (dp_t - st_ref[0, 2 + e:3 + e, rows])
                ds_t = ds_f.astype(BF16)
                dc = dc + jnp.sum(ds_f, axis=1, keepdims=True)
                dv = dv + jnp.dot(p_t.astype(BF16), dout, preferred_element_type=F32)
                dk = dk + jnp.dot(ds_t, q, preferred_element_type=F32)
                dq_e = jnp.dot(ks_t[e], ds_t, preferred_element_type=F32)
                dq = dq + jnp.where(_row_halves()[e], dq_e, 0.0)
                new.append((dk, dv, dc))
            dq_ref[rows, :] += dq.T * SCALE
            return tuple(new)

        zero = jnp.zeros((t, LANE), F32)
        init = (zero, zero, jnp.zeros((t, 1), F32))
        carry = step(j, (init, init), True)
        (dk0, dv0, dc0), (dk1, dv1, dc1) = lax.fori_loop(j + 1, n, lambda i, c: step(i, c, False), carry)
        dk_ref[...] = jnp.where(_own_half(1), dk1, dk0).astype(BF16)
        dv_ref[...] = jnp.where(_own_half(1), dv1, dv0).astype(BF16)
        dc_ref[...] = jnp.where(lane == 0, -dc0, jnp.where(lane == 1, -dc1, 0.0))

    body = _with_comm(comm, 5, 4, lambda: (pl.program_id(0) == 0) & (pl.program_id(1) == 0),
                      lambda: (pl.program_id(0) == N_PAIR - 1) & (pl.program_id(1) == n - 1), compute)
    pair = pl.BlockSpec((s_len, 2 * LANE), lambda p, j: (0, p))
    blk = pl.BlockSpec((t, 2 * LANE), lambda p, j: (j, p))
    tile = pl.BlockSpec((t, LANE), lambda p, j: (j, p))
    outs = pl.pallas_call(
        body, name="fox_bwd" if comm is None else "fox_bwd_comm", grid=(N_PAIR, n),
        in_specs=[pair, pair, pl.BlockSpec((1, 8, s_len), lambda p, j: (p, 0, 0)), blk, blk] + c_specs,
        out_specs=[pl.BlockSpec((s_len, LANE), lambda p, j: (0, p)), tile, tile, tile] + c_specs,
        out_shape=[jax.ShapeDtypeStruct((s_len, FOX_W), F32), jax.ShapeDtypeStruct((s_len, FOX_W), BF16),
                   jax.ShapeDtypeStruct((s_len, FOX_W), BF16), jax.ShapeDtypeStruct((s_len, FOX_W), F32)] + c_shapes,
        scratch_shapes=c_scratch,
        compiler_params=_params("arbitrary", "arbitrary"),
    )(qx, dox, stats, kx, vx, *c_arrays)
    return outs[0], outs[1], outs[2], outs[3], outs[4:]


def _mixer_out(attn_a, attn_b, zg, h, wpa, wpb, wout, g, b, tm=256):
    m = h.shape[0]
    tm = _div(m, tm, 8)

    def body(a_ref, b_ref, ga_ref, gb_ref, h_ref, wpa_ref, wpb_ref, wout_ref, g_ref, bb_ref, h1_ref, u_ref, mg_ref):
        ya = jnp.dot(a_ref[...], wpa_ref[...], preferred_element_type=F32)
        yb = jnp.dot(b_ref[...], wpb_ref[...], preferred_element_type=F32)
        merged = (jax.nn.sigmoid(ga_ref[...]) * ya + jax.nn.sigmoid(gb_ref[...]) * yb).astype(BF16)
        u = ALPHA * h_ref[...] + jnp.dot(merged, wout_ref[...], preferred_element_type=F32)
        u_ref[...] = u
        h1_ref[...] = _ln(u, g_ref[...], bb_ref[...])
        mg_ref[...] = merged

    row = pl.BlockSpec((tm, D_MODEL), lambda i: (i, 0))
    att = pl.BlockSpec((tm, SWA_Q), lambda i: (i, 0))
    vec = pl.BlockSpec((1, D_MODEL), lambda i: (0, 0))
    wsm = pl.BlockSpec((SWA_Q, D_MODEL), lambda i: (0, 0))
    return pl.pallas_call(
        body, name="mixer_out", grid=(m // tm,),
        in_specs=[att, att, row, pl.BlockSpec((tm, D_MODEL), lambda i: (i, 1)), row, wsm, wsm,
                  pl.BlockSpec((D_MODEL, D_MODEL), lambda i: (0, 0)), vec, vec],
        out_specs=[row, row, row],
        out_shape=[jax.ShapeDtypeStruct((m, D_MODEL), F32), jax.ShapeDtypeStruct((m, D_MODEL), F32),
                   jax.ShapeDtypeStruct((m, D_MODEL), BF16)],
        compiler_params=_params("parallel"),
    )(attn_a, attn_b, zg, zg, h, wpa, wpb, wout, g, b)


def _mixer_bwd(dh1, u1, g, wout, attn_a, attn_b, zg, wpa, wpb, tm=256):
    m = dh1.shape[0]
    tm = _div(m, tm, 8)

    def body(dh_ref, u_ref, g_ref, wout_ref, a_ref, b_ref, ga_ref, gb_ref, wpa_ref, wpb_ref,
             du_ref, dg_ref, db_ref, dya_ref, dyb_ref, dga_ref, dgb_ref, da_ref, dbb_ref):
        @pl.when(pl.program_id(0) == 0)
        def _():
            dg_ref[...] = jnp.zeros_like(dg_ref)
            db_ref[...] = jnp.zeros_like(db_ref)

        du, dg, db = _ln_bwd_block(dh_ref[...], u_ref[...], g_ref[...])
        du_ref[...] = du
        dg_ref[...] += dg
        db_ref[...] += db
        dm = lax.dot_general(du.astype(BF16), wout_ref[...], (((1,), (1,)), ((), ())), preferred_element_type=F32)
        for x_ref, gate_ref, w_ref, dy_ref, dgate_ref, dattn_ref in (
                (a_ref, ga_ref, wpa_ref, dya_ref, dga_ref, da_ref), (b_ref, gb_ref, wpb_ref, dyb_ref, dgb_ref, dbb_ref)):
            sg = jax.nn.sigmoid(gate_ref[...])
            dy = (dm * sg).astype(BF16)
            dy_ref[...] = dy
            y = jnp.dot(x_ref[...], w_ref[...], preferred_element_type=F32)
            dgate_ref[...] = (dm * y * sg * (1.0 - sg)).astype(BF16)
            dattn_ref[...] = lax.dot_general(dy, w_ref[...], (((1,), (1,)), ((), ())),
                                             preferred_element_type=F32).astype(BF16)

    row = pl.BlockSpec((tm, D_MODEL), lambda i: (i, 0))
    att = pl.BlockSpec((tm, SWA_Q), lambda i: (i, 0))
    vec = pl.BlockSpec((1, D_MODEL), lambda i: (0, 0))
    wsm = pl.BlockSpec((SWA_Q, D_MODEL), lambda i: (0, 0))
    wide = jax.ShapeDtypeStruct((m, D_MODEL), BF16)
    narrow = jax.ShapeDtypeStruct((m, SWA_Q), BF16)
    sums = jax.ShapeDtypeStruct((1, D_MODEL), F32)
    return pl.pallas_call(
        body, name="mixer_bwd", grid=(m // tm,),
        in_specs=[row, row, vec, pl.BlockSpec((D_MODEL, D_MODEL), lambda i: (0, 0)), att, att, row,
                  pl.BlockSpec((tm, D_MODEL), lambda i: (i, 1)), wsm, wsm],
        out_specs=[row, vec, vec, row, row, row, row, att, att],
        out_shape=[jax.ShapeDtypeStruct((m, D_MODEL), F32), sums, sums, wide, wide, wide, wide, narrow, narrow],
        compiler_params=_params("arbitrary"),
    )(dh1, u1, g, wout, attn_a, attn_b, zg, zg, wpa, wpb)


def _shift_down(x, k, halo, first):
    rows = lax.broadcasted_iota(jnp.int32, (x.shape[0], 1), 0)
    y = pltpu.roll(x, k, 0)
    for r in range(k):
        fill = jnp.where(first, 0.0, halo[8 - k + r:8 - k + r + 1, :])
        y = jnp.where(rows == r, fill, y)
    return y


def _shift_up(x, k, halo, last):
    n = x.shape[0]
    rows = lax.broadcasted_iota(jnp.int32, (n, 1), 0)
    y = pltpu.roll(x, n - k, 0)
    for r in range(k):
        fill = jnp.where(last, 0.0, halo[r:r + 1, :])
        y = jnp.where(rows == n - k + r, fill, y)
    return y


def _conv_act(gate, gate_m1, gate_m2, cw, cb):
    return cb + cw[0:1, :] * gate_m2 + cw[1:2, :] * gate_m1 + cw[2:3, :] * gate


def _ffn_in_conv(h1, wfi, cw, cb, tm=256):
    s_len = h1.shape[0]
    tm = _div(s_len, tm, 8)
    hb = tm // 8

    def body(a_ref, ap_ref, w_ref, cw_ref, cb_ref, gu_ref, act_ref):
        first = pl.program_id(0) == 0
        a = a_ref[...].astype(BF16)
        before = ap_ref[...].astype(BF16)
        for c in range(N_CHUNK):
            gate = jnp.dot(a, w_ref[c], preferred_element_type=F32)
            up = jnp.dot(a, w_ref[N_CHUNK + c], preferred_element_type=F32)
            halo = jnp.dot(before, w_ref[c], preferred_element_type=F32)
            gu_ref[c, 0] = gate
            gu_ref[c, 1] = up
            conv = _conv_act(gate, _shift_down(gate, 1, halo, first), _shift_down(gate, 2, halo, first),
                             cw_ref[c], cb_ref[c])
            act_ref[c] = (conv * jax.nn.sigmoid(conv) * up).astype(BF16)

    return pl.pallas_call(
        body, name="ffn_in_conv", grid=(s_len // tm,),
        in_specs=[pl.BlockSpec((tm, D_MODEL), lambda i: (i, 0)),
                  pl.BlockSpec((8, D_MODEL), lambda i: (jnp.maximum(i * hb - 1, 0), 0)),
                  pl.BlockSpec((N_DEV, D_MODEL, FF_CHUNK), lambda i: (0, 0, 0)),
                  pl.BlockSpec((N_CHUNK, 8, FF_CHUNK), lambda i: (0, 0, 0)),
                  pl.BlockSpec((N_CHUNK, 1, FF_CHUNK), lambda i: (0, 0, 0))],
        out_specs=[pl.BlockSpec((N_CHUNK, 2, tm, FF_CHUNK), lambda i: (0, 0, i, 0)),
                   pl.BlockSpec((N_CHUNK, tm, FF_CHUNK), lambda i: (0, i, 0))],
        out_shape=[jax.ShapeDtypeStruct((N_CHUNK, 2, s_len, FF_CHUNK), F32),
                   jax.ShapeDtypeStruct((N_CHUNK, s_len, FF_CHUNK), BF16)],
        compiler_params=_params("parallel"),
    )(h1, h1, wfi, cw, cb)


def _ffn_out_ln(act, wfo, res, g, b, target=None, tm=512):
    s_len = res.shape[0]
    tm = _div(s_len, tm, 8)
    last = target is not None

    def body(a_ref, w_ref, res_ref, g_ref, b_ref, *rest):
        u = ALPHA * res_ref[...]
        for c in range(N_CHUNK):
            u = u + jnp.dot(a_ref[c], w_ref[c], preferred_element_type=F32)
        y = _ln(u, g_ref[...], b_ref[...])
        if not last:
            u_ref, y_ref = rest
            u_ref[...] = u
            y_ref[...] = y
            return
        t_ref, u_ref, dy_ref, loss_ref = rest
        u_ref[...] = u

        @pl.when(pl.program_id(0) == 0)
        def _():
            loss_ref[...] = jnp.zeros_like(loss_ref)

        err = y - t_ref[...]
        dy_ref[...] = err / D_MODEL
        loss_ref[...] += 0.5 * jnp.sum(jnp.sum(err * err, axis=1, keepdims=True) / D_MODEL, axis=0, keepdims=True)

    row = pl.BlockSpec((tm, D_MODEL), lambda i: (i, 0))
    vec = pl.BlockSpec((1, D_MODEL), lambda i: (0, 0))
    wide = jax.ShapeDtypeStruct((s_len, D_MODEL), F32)
    return pl.pallas_call(
        body, name="ffn_out_ln_loss" if last else "ffn_out_ln", grid=(s_len // tm,),
        in_specs=[pl.BlockSpec((N_CHUNK, tm, FF_CHUNK), lambda i: (0, i, 0)),
                  pl.BlockSpec((N_CHUNK, FF_CHUNK, D_MODEL), lambda i: (0, 0, 0)), row, vec, vec] + [row] * last,
        out_specs=[row, row] + [pl.BlockSpec((8, LANE), lambda i: (0, 0))] * last,
        out_shape=[wide, wide] + [jax.ShapeDtypeStruct((8, LANE), F32)] * last,
        compiler_params=_params("arbitrary" if last else "parallel"),
    )(act, wfo, res, g, b, *([target] if last else []))


def _ffn_out_bwd(dh2, u2, g, wfo, tm=512):
    s_len = dh2.shape[0]
    tm = _div(s_len, tm, 8)

    def body(dh_ref, u_ref, g_ref, w_ref, du_ref, dg_ref, db_ref, o_ref):
        @pl.when(pl.program_id(0) == 0)
        def _():
            dg_ref[...] = jnp.zeros_like(dg_ref)
            db_ref[...] = jnp.zeros_like(db_ref)

        du, dg, db = _ln_bwd_block(dh_ref[...], u_ref[...], g_ref[...])
        du_ref[...] = du
        dg_ref[...] += dg
        db_ref[...] += db
        du_b = du.astype(BF16)
        for c in range(N_CHUNK):
            o_ref[c] = lax.dot_general(du_b, w_ref[c], (((1,), (1,)), ((), ())), preferred_element_type=F32)

    row = pl.BlockSpec((tm, D_MODEL), lambda i: (i, 0))
    vec = pl.BlockSpec((1, D_MODEL), lambda i: (0, 0))
    sums = jax.ShapeDtypeStruct((1, D_MODEL), F32)
    return pl.pallas_call(
        body, name="ffn_out_bwd", grid=(s_len // tm,),
        in_specs=[row, row, vec, pl.BlockSpec((N_CHUNK, FF_CHUNK, D_MODEL), lambda i: (0, 0, 0))],
        out_specs=[row, vec, vec, pl.BlockSpec((N_CHUNK, tm, FF_CHUNK), lambda i: (0, i, 0))],
        out_shape=[jax.ShapeDtypeStruct((s_len, D_MODEL), F32), sums, sums,
                   jax.ShapeDtypeStruct((N_CHUNK, s_len, FF_CHUNK), F32)],
        compiler_params=_params("arbitrary"),
    )(dh2, u2, g, wfo)


def _g_w_ffn_out(act, du, tm=2048):
    s_len = du.shape[0]
    tm = _div(s_len, tm, 8)
    steps = s_len // tm

    def body(a_ref, g_ref, o_ref, acc_ref):
        s = pl.program_id(1)

        @pl.when(s == 0)
        def _():
            acc_ref[...] = jnp.zeros_like(acc_ref)

        acc_ref[...] += lax.dot_general(a_ref[0], g_ref[...].astype(BF16), (((0,), (0,)), ((), ())),
                                        preferred_element_type=F32)

        @pl.when(s == steps - 1)
        def _():
            o_ref[0] = acc_ref[...].astype(BF16)

    return pl.pallas_call(
        body, name="g_w_ffn_out", grid=(N_CHUNK, steps),
        in_specs=[pl.BlockSpec((1, tm, FF_CHUNK), lambda c, s: (c, s, 0)),
                  pl.BlockSpec((tm, D_MODEL), lambda c, s: (s, 0))],
        out_specs=pl.BlockSpec((1, FF_CHUNK, D_MODEL), lambda c, s: (c, 0, 0)),
        out_shape=jax.ShapeDtypeStruct((N_CHUNK, FF_CHUNK, D_MODEL), BF16),
        scratch_shapes=[pltpu.VMEM((FF_CHUNK, D_MODEL), F32)],
        compiler_params=_params("parallel", "arbitrary"),
    )(act, du)


def _g_w_ffn_in(h1, dgu, tm=2048):
    s_len = h1.shape[0]
    tm = _div(s_len, tm, 8)
    steps = s_len // tm

    def body(a_ref, g_ref, o_ref, acc_ref):
        s = pl.program_id(1)

        @pl.when(s == 0)
        def _():
            acc_ref[...] = jnp.zeros_like(acc_ref)

        acc_ref[...] += lax.dot_general(g_ref[0, 0], a_ref[...].astype(BF16), (((0,), (0,)), ((), ())),
                                        preferred_element_type=F32)

        @pl.when(s == steps - 1)
        def _():
            o_ref[0] = acc_ref[...].astype(BF16)

    return pl.pallas_call(
        body, name="g_w_ffn_in", grid=(N_DEV, steps),
        in_specs=[pl.BlockSpec((tm, D_MODEL), lambda d, s: (s, 0)),
                  pl.BlockSpec((1, 1, tm, FF_CHUNK), lambda d, s: (d % N_CHUNK, d // N_CHUNK, s, 0))],
        out_specs=pl.BlockSpec((1, FF_CHUNK, D_MODEL), lambda d, s: (d, 0, 0)),
        out_shape=jax.ShapeDtypeStruct((N_DEV, FF_CHUNK, D_MODEL), BF16),
        scratch_shapes=[pltpu.VMEM((FF_CHUNK, D_MODEL), F32)],
        compiler_params=_params("parallel", "arbitrary"),
    )(h1, dgu)


def _conv_bwd_dh1(gu, dact, cw, cb, wfi, res, tm=256):
    s_len = gu.shape[2]
    tm = _div(s_len, tm, 8)
    nrow = s_len // tm
    hb = tm // 8

    def dconv_of(conv, sg, up, da):
        return da * up * (sg * (1.0 + conv * (1.0 - sg)))

    def body(gu_ref, gp_ref, gun_ref, da_ref, dan_ref, cw_ref, cb_ref, w_ref, res_ref, dgu_ref, dcw_ref, dh_ref):
        i = pl.program_id(0)
        first = i == 0
        last = i == nrow - 1

        @pl.when(first)
        def _():
            dcw_ref[...] = jnp.zeros_like(dcw_ref)

        row = lax.broadcasted_iota(jnp.int32, (8, 1), 0)
        acc = ALPHA * res_ref[...]
        for c in range(N_CHUNK):
            cw = cw_ref[c]
            cb = cb_ref[c]
            gate = gu_ref[c, 0]
            halo = gp_ref[c, 0]
            g_m1 = _shift_down(gate, 1, halo, first)
            g_m2 = _shift_down(gate, 2, halo, first)
            conv = _conv_act(gate, g_m1, g_m2, cw, cb)
            da = da_ref[c]
            sg = jax.nn.sigmoid(conv)
            dup = (da * conv * sg).astype(BF16)
            dconv = dconv_of(conv, sg, gu_ref[c, 1], da)
            gate_n = gun_ref[c, 0]
            tail = gate[tm - 8:, :]
            conv_n = _conv_act(gate_n, _shift_down(gate_n, 1, tail, False), _shift_down(gate_n, 2, tail, False),
                               cw, cb)
            dconv_n = dconv_of(conv_n, jax.nn.sigmoid(conv_n), gun_ref[c, 1], dan_ref[c])
            dgate = (cw[2:3, :] * dconv + cw[1:2, :] * _shift_up(dconv, 1, dconv_n, last)
                     + cw[0:1, :] * _shift_up(dconv, 2, dconv_n, last)).astype(BF16)
            dgu_ref[c, 0] = dgate
            dgu_ref[c, 1] = dup
            acc = acc + jnp.dot(dgate, w_ref[c], preferred_element_type=F32)
            acc = acc + jnp.dot(dup, w_ref[N_CHUNK + c], preferred_element_type=F32)
            part = jnp.zeros((8, FF_CHUNK), F32)
            for r, term in enumerate((dconv * g_m2, dconv * g_m1, dconv * gate, dconv)):
                part = jnp.where(row == r, jnp.sum(term, axis=0, keepdims=True), part)
            dcw_ref[c] += part
        dh_ref[...] = acc

    nxt = lambda i: jnp.minimum((i + 1) * hb, s_len // 8 - 1)
    main = pl.BlockSpec((N_CHUNK, 2, tm, FF_CHUNK), lambda i: (0, 0, i, 0))
    row_d = pl.BlockSpec((tm, D_MODEL), lambda i: (i, 0))
    return pl.pallas_call(
        body, name="conv_bwd_dh1", grid=(nrow,),
        in_specs=[main,
                  pl.BlockSpec((N_CHUNK, 1, 8, FF_CHUNK), lambda i: (0, 0, jnp.maximum(i * hb - 1, 0), 0)),
                  pl.BlockSpec((N_CHUNK, 2, 8, FF_CHUNK), lambda i: (0, 0, nxt(i), 0)),
                  pl.BlockSpec((N_CHUNK, tm, FF_CHUNK), lambda i: (0, i, 0)),
                  pl.BlockSpec((N_CHUNK, 8, FF_CHUNK), lambda i: (0, nxt(i), 0)),
                  pl.BlockSpec((N_CHUNK, 8, FF_CHUNK), lambda i: (0, 0, 0)),
                  pl.BlockSpec((N_CHUNK, 1, FF_CHUNK), lambda i: (0, 0, 0)),
                  pl.BlockSpec((N_DEV, FF_CHUNK, D_MODEL), lambda i: (0, 0, 0)), row_d],
        out_specs=[main, pl.BlockSpec((N_CHUNK, 8, FF_CHUNK), lambda i: (0, 0, 0)), row_d],
        out_shape=[jax.ShapeDtypeStruct((N_CHUNK, 2, s_len, FF_CHUNK), BF16),
                   jax.ShapeDtypeStruct((N_CHUNK, 8, FF_CHUNK), F32),
                   jax.ShapeDtypeStruct((s_len, D_MODEL), F32)],
        compiler_params=_params("arbitrary"),
    )(gu, gu, gu, dact, dact, cw, cb, wfi, res)


def _sum_devices(r_ref):
    acc = r_ref[0].astype(F32)
    for d in range(1, N_DEV):
        acc = acc + r_ref[d].astype(F32)
    return acc


def _sum8(recv):
    rows = recv.shape[1]
    tr = _div(rows, ROW_BLOCK, 8)

    def body(r_ref, o_ref):
        o_ref[...] = _sum_devices(r_ref)

    return pl.pallas_call(
        body, name="sum8", grid=(rows // tr,),
        in_specs=[pl.BlockSpec((N_DEV, tr, LANE), lambda i: (0, i, 0))],
        out_specs=pl.BlockSpec((tr, LANE), lambda i: (i, 0)),
        out_shape=jax.ShapeDtypeStruct((rows, LANE), F32),
        compiler_params=_params("parallel"),
    )(recv)


def _adamw_math(w, g, m, v):
    m = ADAM_B1 * m + (1.0 - ADAM_B1) * g
    v = ADAM_B2 * v + (1.0 - ADAM_B2) * (g * g)
    m_hat = m / (1.0 - ADAM_B1 ** ADAM_STEP)
    v_hat = v / (1.0 - ADAM_B2 ** ADAM_STEP)
    return -ADAM_LR * (m_hat / (jnp.sqrt(v_hat) + ADAM_EPS) + ADAM_WD * w), m, v


def _adamw_rows(w, g, m, v, name):
    rows = w.shape[0]
    tr = _div(rows, ROW_BLOCK, 8)

    def body(w_ref, g_ref, m_ref, v_ref, d_ref, mo_ref, vo_ref):
        d_ref[...], mo_ref[...], vo_ref[...] = _adamw_math(w_ref[...], g_ref[...], m_ref[...], v_ref[...])

    blk = pl.BlockSpec((tr, LANE), lambda i: (i, 0))
    out = jax.ShapeDtypeStruct((rows, LANE), F32)
    return pl.pallas_call(
        body, name=name, grid=(rows // tr,), in_specs=[blk, blk, blk, blk], out_specs=[blk, blk, blk],
        out_shape=[out, out, out], compiler_params=_params("parallel"),
    )(w, g, m, v)


def _adamw_w_in(recv, w, m, v, tl=128):
    n, depth, d = w.shape

    def body(*refs):
        r_refs, (w_ref, m_ref, v_ref), (g_ref, d_ref, mo_ref, vo_ref) = refs[:depth], refs[depth:depth + 3], refs[-4:]
        for l in range(depth):
            g = _sum_devices(r_refs[l])
            g_ref[:, l, :] = g
            d_ref[:, l, :], mo_ref[:, l, :], vo_ref[:, l, :] = _adamw_math(w_ref[:, l, :], g, m_ref[:, l, :],
                                                                            v_ref[:, l, :])

    blk = pl.BlockSpec((n, depth, tl), lambda j: (0, 0, j))
    out = jax.ShapeDtypeStruct((n, depth, d), F32)
    return pl.pallas_call(
        body, name="adamw_w_in", grid=(d // tl,),
        in_specs=[pl.BlockSpec((N_DEV, n, tl), lambda j: (0, 0, j))] * depth + [blk, blk, blk],
        out_specs=[blk, blk, blk, blk], out_shape=[out, out, out, out],
        compiler_params=_params("parallel"),
    )(*recv, w, m, v)


def _adamw_shard(recv, w, m, v, layer, prev, name):
    _, k, n = recv.shape
    tk = _div(k, 128, 16)

    def body(r_ref, w_ref, m_ref, v_ref, *rest):
        g_ref, d_ref, mo_ref, vo_ref = rest[-4:]
        g = _sum_devices(r_ref)
        g_ref[0] = g
        d_ref[0], mo_ref[0], vo_ref[0] = _adamw_math(w_ref[0], g, m_ref[0], v_ref[0])

    blk = pl.BlockSpec((1, tk, n), lambda i: (layer, i, 0))
    out = jax.ShapeDtypeStruct((DEPTH, k, n), F32)
    carried = [] if prev is None else list(prev)
    return pl.pallas_call(
        body, name=name, grid=(k // tk,),
        in_specs=[pl.BlockSpec((N_DEV, tk, n), lambda i: (0, i, 0)), blk, blk, blk]
        + [pl.BlockSpec(memory_space=pl.ANY)] * len(carried),
        out_specs=[blk, blk, blk, blk], out_shape=[out, out, out, out],
        input_output_aliases={4 + j: j for j in range(len(carried))},
        compiler_params=_params("parallel"),
    )(recv, w, m, v, *carried)


def _to_rows(flat, rows):
    flat = flat.reshape(-1)
    return jnp.pad(flat, (0, rows * LANE - flat.shape[0])).reshape(rows, LANE)


def _pad_z(a, axis):
    f0 = N_QKV
    g0 = N_QKV + FOX_HEADS
    take = lambda lo, hi: lax.slice_in_dim(a, lo, hi, axis=axis)
    shape = list(a.shape)
    shape[axis] = F_PAD - FOX_HEADS
    return jnp.concatenate([take(0, f0), take(g0, N_IN), take(f0, g0), jnp.zeros(shape, a.dtype)], axis=axis)


def _unpad_z(a, axis):
    f0 = N_QKV + N_GATE
    take = lambda lo, hi: lax.slice_in_dim(a, lo, hi, axis=axis)
    return jnp.concatenate([take(0, N_QKV), take(f0, f0 + FOX_HEADS), take(N_QKV, f0)], axis=axis)


def _shards_to_cols(g):
    _, k, n = g.shape
    return g.transpose(1, 0, 2).reshape(k, N_DEV * n)


def _cols_to_shards(full):
    k, n = full.shape
    return full.reshape(k, N_DEV, n // N_DEV).transpose(1, 0, 2)


def _layer_fwd(h, w, p, comm=None, late=None, target=None):
    zq, zg = _z_proj(h, w["w_in_p"], p["b_in_p"])
    qx, kx, vx = _fox_prep(zq, _cumsum_logf(zg))
    attn_a, lse_a = _swa_fwd(zq, p["sinks"])
    attn_b, attn_b32, m_b, l_b, arrived = _fox_fwd(qx, kx, vx, comm)
    if late is not None:
        w, p = late(w, p, arrived)
    h1, u1, merged = _mixer_out(attn_a, attn_b, zg, h, w["w_proj_a"], w["w_proj_b"], w["w_out"],
                                        p["ln_mix_g"], p["ln_mix_b"])
    gu, act = _ffn_in_conv(h1, w["w_ffn_in_fwd"], p["conv_w"], p["conv_b"])
    u2, *h2 = _ffn_out_ln(act, w["w_ffn_out"], h1, p["ln_ffn_g"], p["ln_ffn_b"], target)
    saved = dict(h=h, zq=zq, zg=zg, qx=qx, kx=kx, vx=vx, attn_a=attn_a, lse_a=lse_a, attn_b=attn_b,
                 attn_b32=attn_b32, m_b=m_b, l_b=l_b, h1=h1, u1=u1, merged=merged, gu=gu, act=act, u2=u2)
    return h2, saved, w, p


def _layer_bwd(dh2, sv, w, p, make_comm=None, make_last_comm=None):
    s_len = dh2.shape[0]
    du2, d_ffn_g, d_ffn_b, dact = _ffn_out_bwd(dh2, sv["u2"], p["ln_ffn_g"], w["w_ffn_out"])
    g_ffn_out = _g_w_ffn_out(sv["act"], du2)
    dgu, dcw, dh1 = _conv_bwd_dh1(sv["gu"], dact, p["conv_w"], p["conv_b"], w["w_ffn_in"], du2)
    dcw = dcw.transpose(1, 0, 2).reshape(8, D_FF)
    g_ffn_in = _g_w_ffn_in(sv["h1"], dgu)
    du1, d_mix_g, d_mix_b, dya, dyb, dga, dgb, dattn_a, dattn_b = _mixer_bwd(
        dh1, sv["u1"], p["ln_mix_g"], w["w_out"], sv["attn_a"], sv["attn_b"], sv["zg"], w["w_proj_a"], w["w_proj_b"])
    g_out = _linear_tn(sv["merged"], du1, name="g_w_out", tn=1024)
    g_proj_a = _linear_tn(sv["attn_a"], dya, name="g_w_proj_a", tk=512, tn=1024)
    g_proj_b = _linear_tn(sv["attn_b"], dyb, name="g_w_proj_b", tk=512, tn=1024)
    dq_a, dk_a, dv_a, dsinks = _swa_bwd(sv["zq"], p["sinks"], sv["attn_a"], dattn_a, sv["lse_a"])
    big = dict(w_proj_a=_cols_to_shards(g_proj_a), w_proj_b=_cols_to_shards(g_proj_b),
               w_out=g_out.reshape(N_DEV, D_MODEL // N_DEV, D_MODEL), w_ffn_in=g_ffn_in,
               w_ffn_out=g_ffn_out.reshape(N_DEV, D_FF // N_DEV, D_MODEL))
    dox, stats = _fox_stats(sv["attn_b32"], dattn_b, sv["m_b"], sv["l_b"])
    dq_b, dk_b, dv_b, dcc, arrived = _fox_bwd(sv["qx"], sv["kx"], sv["vx"], dox, stats,
                                              None if make_comm is None else make_comm(big))
    df = _forget_bwd(dcc, sv["zg"])
    dz = jnp.concatenate([dq_a, dk_a.astype(BF16), dv_a.astype(BF16), dq_b.astype(BF16), dk_b, dv_b, dga, dgb, df,
                          jnp.zeros((s_len, F_PAD - LANE), BF16)], axis=1)
    g_in_t, g_b_in = _linear_tn(dz, sv["h"], name="g_w_in", tk=768, tn=1024, colsum=True)
    g_in_t, g_b_in = _unpad_z(g_in_t, 0), _unpad_z(g_b_in, 1)
    big["w_in"] = g_in_t.reshape(N_DEV, N_IN // N_DEV, D_MODEL)
    small = dict(ln_mix_g=d_mix_g, ln_mix_b=d_mix_b, b_in=g_b_in, attn_sinks=dsinks[:, :SWA_HEADS],
                 ln_ffn_g=d_ffn_g, ln_ffn_b=d_ffn_b, conv_w=dcw[:3], conv_b=dcw[3:4])
    dh, arrived_last = _d_h(dz, w["w_in_p"], du1, None if make_last_comm is None else make_last_comm(big, small))
    return dh, big, small, arrived, arrived_last


def _w_in_layouts(w_in):
    return dict(w_in_p=_pad_z(w_in.reshape(N_IN, D_MODEL), 0))


def _other_layouts(w_proj_a, w_proj_b, w_out, w_ffn_in, w_ffn_out):
    return dict(w_proj_a=_shards_to_cols(w_proj_a), w_proj_b=_shards_to_cols(w_proj_b),
                w_out=w_out.reshape(D_MODEL, D_MODEL), w_ffn_in=w_ffn_in,
                w_ffn_in_fwd=w_ffn_in.transpose(0, 2, 1),
                w_ffn_out=w_ffn_out.reshape(N_CHUNK, FF_CHUNK, D_MODEL))


def _layer_params(r):
    return dict(
        b_in_p=_pad_z(r["b_in"].reshape(1, N_IN), 1),
        sinks=jnp.pad(r["attn_sinks"].reshape(1, SWA_HEADS), ((0, 0), (0, LANE - SWA_HEADS))),
        ln_mix_g=r["ln_mix_g"].reshape(1, D_MODEL), ln_mix_b=r["ln_mix_b"].reshape(1, D_MODEL),
        ln_ffn_g=r["ln_ffn_g"].reshape(1, D_MODEL), ln_ffn_b=r["ln_ffn_b"].reshape(1, D_MODEL),
        conv_b=r["conv_b"].reshape(N_CHUNK, 1, FF_CHUNK))


def _conv_w_layout(conv_w):
    return jnp.pad(conv_w, ((0, 5), (0, 0))).reshape(8, N_CHUNK, FF_CHUNK).transpose(1, 0, 2)


def kernel(x, ln_mix_g, ln_mix_b, w_in, b_in, attn_sinks, w_proj_a, w_proj_b, w_out, ln_ffn_g, ln_ffn_b, w_ffn_in, conv_w, conv_b, w_ffn_out, loss_target, m_ln_mix_g, m_ln_mix_b, m_w_in, m_b_in, m_attn_sinks, m_w_proj_a, m_w_proj_b, m_w_out, m_ln_ffn_g, m_ln_ffn_b, m_w_ffn_in, m_conv_w, m_conv_b, m_w_ffn_out, v_ln_mix_g, v_ln_mix_b, v_w_in, v_b_in, v_attn_sinks, v_w_proj_a, v_w_proj_b, v_w_out, v_ln_ffn_g, v_ln_ffn_b, v_w_ffn_in, v_conv_w, v_conv_b, v_w_ffn_out):
    wts = dict(ln_mix_g=ln_mix_g, ln_mix_b=ln_mix_b, w_in=w_in, b_in=b_in, attn_sinks=attn_sinks, w_proj_a=w_proj_a,
               w_proj_b=w_proj_b, w_out=w_out, ln_ffn_g=ln_ffn_g, ln_ffn_b=ln_ffn_b, w_ffn_in=w_ffn_in,
               conv_w=conv_w, conv_b=conv_b, w_ffn_out=w_ffn_out)
    mom = dict(ln_mix_g=m_ln_mix_g, ln_mix_b=m_ln_mix_b, w_in=m_w_in, b_in=m_b_in, attn_sinks=m_attn_sinks,
               w_proj_a=m_w_proj_a, w_proj_b=m_w_proj_b, w_out=m_w_out, ln_ffn_g=m_ln_ffn_g, ln_ffn_b=m_ln_ffn_b,
               w_ffn_in=m_w_ffn_in, conv_w=m_conv_w, conv_b=m_conv_b, w_ffn_out=m_w_ffn_out)
    vel = dict(ln_mix_g=v_ln_mix_g, ln_mix_b=v_ln_mix_b, w_in=v_w_in, b_in=v_b_in, attn_sinks=v_attn_sinks,
               w_proj_a=v_w_proj_a, w_proj_b=v_w_proj_b, w_out=v_w_out, ln_ffn_g=v_ln_ffn_g, ln_ffn_b=v_ln_ffn_b,
               w_ffn_in=v_w_ffn_in, conv_w=v_conv_w, conv_b=v_conv_b, w_ffn_out=v_w_ffn_out)
    names = list(wts)
    big_names = [n for n, _, _ in BIG]
    small_names = [n for n, _ in SMALL]
    me = 4 * lax.axis_index("x") + 2 * lax.axis_index("y") + lax.axis_index("c")
    cw_shard = D_FF // N_DEV

    stored = {"w_in": ((2, 0, 1), (1, 2, 0)), "w_ffn_in": ((0, 2, 1), (0, 2, 1))}
    as_stored = lambda tree: {n: jnp.transpose(tree[n], stored[n][0]) if n in stored else tree[n] for n in big_names}
    w_st, m_st, v_st = as_stored(wts), as_stored(mom), as_stored(vel)
    wb = {n: [(w_st[n][:, l] if n == "w_in" else w_st[n][l]).astype(BF16) for l in range(DEPTH)] for n in big_names}
    ps = [_layer_params(dict(b_in=b_in[l], attn_sinks=attn_sinks[l], ln_mix_g=ln_mix_g[l], ln_mix_b=ln_mix_b[l],
                             ln_ffn_g=ln_ffn_g[l], ln_ffn_b=ln_ffn_b[l], conv_b=conv_b[l])) for l in range(DEPTH)]
    w_in_0 = _gather_two_level(wb["w_in"][0], "gather_w_in_0")
    others = big_names[1:]
    next_layer = {}

    def late_0(w, p, arrived):
        conv_full = arrived[-1].transpose(1, 2, 0, 3).reshape(DEPTH, 3, D_FF)
        next_layer["w"] = _w_in_layouts(arrived[len(others)])
        next_layer["p"] = dict(ps[1], conv_w=_conv_w_layout(conv_full[1]))
        return dict(w, **_other_layouts(*arrived[:len(others)])), dict(p, conv_w=_conv_w_layout(conv_full[0]))

    def late_1(w, p, arrived):
        return dict(w, **_other_layouts(*arrived)), p

    saved, ws = [None] * DEPTH, [None] * DEPTH
    gather_0 = _Comm([(wb[n][0], True) for n in others] + [(wb["w_in"][1], True), (conv_w, True)])
    (h,), saved[0], ws[0], ps[0] = _layer_fwd(x[0], _w_in_layouts(w_in_0), ps[0], gather_0, late_0)
    gather_1 = _Comm([(wb[n][1], True) for n in others])
    (dh, loss_part), saved[1], ws[1], ps[1] = _layer_fwd(h, next_layer["w"], next_layer["p"], gather_1, late_1,
                                                         loss_target[0])

    def small_rows(small):
        vec = jnp.concatenate([small[n].reshape(-1) for n in small_names] + [loss_part[0, 0].reshape(1)])
        return _to_rows(vec, SMALL_LAYER_ROWS)

    dh, big_1, small_1, _, _ = _layer_bwd(dh, saved[1], ws[1], ps[1])

    def exchange_early(big_0):
        return _Comm([(big_1[n].astype(BF16), False) for n in big_names] + [(small_rows(small_1), True)]
                     + [(big_0[n].astype(BF16), False) for n in others])

    def exchange_last(big_0, small_0):
        return _Comm([(big_0["w_in"].astype(BF16), False), (small_rows(small_0), True)])

    grad_x, _, _, arrived, (g_in_0, g_small_0) = _layer_bwd(dh, saved[0], ws[0], ps[0], exchange_early, exchange_last)
    n_big = len(big_names)
    recv = [[g_in_0] + list(arrived[n_big + 1:]) + [g_small_0], list(arrived[:n_big + 1])]

    big_out = {"w_in": _adamw_w_in([recv[l][0] for l in range(DEPTH)], w_st["w_in"], m_st["w_in"], v_st["w_in"])}
    for t, n in enumerate(big_names):
        if n == "w_in":
            continue
        outs = None
        for l in reversed(range(DEPTH)):
            outs = _adamw_shard(recv[l][t], w_st[n], m_st[n], v_st[n], l, outs, "adamw_%s_%d" % (n, l))
        big_out[n] = outs
    for n, (_, back) in stored.items():
        big_out[n] = [jnp.transpose(a, back) for a in big_out[n]]
    small_sum = [_sum8(recv[l][-1]).reshape(-1) for l in range(DEPTH)]
    g_small = {}
    off = 0
    for n, size in SMALL:
        g_small[n] = jnp.stack([small_sum[l][off:off + size] for l in range(DEPTH)])
        off += size
    loss = small_sum[0][off]
    g_small["conv_w"] = lax.dynamic_slice_in_dim(g_small["conv_w"].reshape(DEPTH, 3, D_FF), me * cw_shard, cw_shard,
                                                 axis=2)
    g_small = {n: g_small[n].reshape(wts[n].shape) for n in small_names}

    def pack_small(tree):
        return _to_rows(jnp.concatenate([tree[n].reshape(-1) for n in small_names]), SMALL_ROWS)

    small_out = (pack_small(g_small),) + tuple(_adamw_rows(pack_small(wts), pack_small(g_small), pack_small(mom),
                                                           pack_small(vel), "adamw_small"))

    def result(j):
        out = {n: big_out[n][j] for n in big_names}
        flat = small_out[j].reshape(-1)
        off = 0
        for n in small_names:
            out[n] = flat[off:off + wts[n].size].reshape(wts[n].shape)
            off += wts[n].size
        return [out[n] for n in names]

    return (loss, grad_x[None], *result(0), *result(1), *result(2), *result(3))
```

```python
import jax
import jax.numpy as jnp
import numpy as np
from jax import lax
from jax.experimental import pallas as pl
from jax.experimental.pallas import tpu as pltpu

F32 = jnp.float32
BF16 = jnp.bfloat16
MESH = pl.DeviceIdType.MESH

N_DEV = 8
DEPTH = 2
D_MODEL = 1024
HEAD_DIM = 64
SWA_Q = 512
SWA_KV = 128
FOX_W = 512
FOX_HEADS = 8
SWA_HEADS = 8
D_FF = 2816
N_IN = 4360
N_QKV = SWA_Q + 2 * SWA_KV + 3 * FOX_W
N_GATE = 2 * D_MODEL
F_PAD = 256
N_ZG = N_GATE + F_PAD
N_ZP = N_QKV + N_ZG
LN_EPS = 1e-5
NEG_INF = -1e30
ALPHA = (2 * DEPTH) ** 0.25
SCALE = HEAD_DIM ** -0.5
LOG2E = 1.4426950408889634
SLOPES = tuple(2.0 ** (-8.0 * (h + 1) / SWA_HEADS) for h in range(SWA_HEADS))

ADAM_LR = 0.001
ADAM_B1 = 0.9
ADAM_B2 = 0.999
ADAM_EPS = 1e-08
ADAM_WD = 0.01
ADAM_STEP = 10

LANE = 128
VMEM_LIMIT = 56 * 1024 * 1024

BIG = (("w_in", (D_MODEL, N_IN), 1), ("w_proj_a", (SWA_Q, D_MODEL), 1), ("w_proj_b", (FOX_W, D_MODEL), 1),
       ("w_out", (D_MODEL, D_MODEL), 0), ("w_ffn_in", (D_MODEL, 2 * D_FF), 1), ("w_ffn_out", (D_FF, D_MODEL), 0))
SMALL = (("ln_mix_g", D_MODEL), ("ln_mix_b", D_MODEL), ("b_in", N_IN), ("attn_sinks", SWA_HEADS),
         ("ln_ffn_g", D_MODEL), ("ln_ffn_b", D_MODEL), ("conv_w", 3 * D_FF), ("conv_b", D_FF))
ROW_BLOCK = 512
SMALL_LAYER_ROWS = -(-(sum(n for _, n in SMALL) + 1) // (8 * LANE)) * 8
SMALL_ROWS = ROW_BLOCK
FF_CHUNK = 2 * D_FF // N_DEV
N_CHUNK = D_FF // FF_CHUNK


def _div(n, cap, unit):
    if n <= cap:
        return n
    best = None
    for t in range(unit, cap + 1, unit):
        if n % t == 0:
            best = t
    assert best is not None, (n, cap, unit)
    return best


def _params(*sem):
    return pltpu.CompilerParams(dimension_semantics=sem, vmem_limit_bytes=VMEM_LIMIT)


def _peer(r):
    x, y, c = lax.axis_index("x"), lax.axis_index("y"), lax.axis_index("c")
    px = 1 - x if (r >> 2) & 1 else x
    py = 1 - y if (r >> 1) & 1 else y
    pc = 1 - c if r & 1 else c
    return (px, py, pc), 4 * px + 2 * py + pc


class _Comm:
    def __init__(self, tensors):
        self.arrays = [x for x, _ in tensors]
        self.gathers = [g for _, g in tensors]
        self.n = len(tensors)
        self.out_shape = [jax.ShapeDtypeStruct((N_DEV,) + (x.shape if g else x.shape[1:]), x.dtype)
                          for x, g in tensors]
        self.specs = [pl.BlockSpec(memory_space=pl.ANY)] * self.n
        self.scratch = [pltpu.SemaphoreType.DMA((N_DEV - 1, self.n)), pltpu.SemaphoreType.DMA((N_DEV - 1, self.n)),
                        pltpu.SemaphoreType.DMA((self.n,))]

    def _copies(self, x_refs, out_refs, sems):
        send_sems, recv_sems, local_sems = sems
        _, me = _peer(0)

        def src(t, idx):
            return x_refs[t] if self.gathers[t] else x_refs[t].at[idx]

        def remote(r, t, mine):
            peer, pid = _peer(r)
            return pltpu.make_async_remote_copy(src_ref=src(t, pid), dst_ref=out_refs[t].at[me if mine else pid],
                                                send_sem=send_sems.at[r - 1, t], recv_sem=recv_sems.at[r - 1, t],
                                                device_id=peer, device_id_type=MESH)

        pairs = [(r, t) for r in range(1, N_DEV) for t in range(self.n)]
        local = [pltpu.make_async_copy(src(t, me), out_refs[t].at[me], local_sems.at[t]) for t in range(self.n)]
        return local, [remote(r, t, True) for r, t in pairs], lambda: [remote(r, t, False) for r, t in pairs]

    def start(self, x_refs, out_refs, sems):
        local, sent, _ = self._copies(x_refs, out_refs, sems)
        for cp in local + sent:
            cp.start()

    def wait(self, x_refs, out_refs, sems):
        local, sent, landing = self._copies(x_refs, out_refs, sems)
        for cp in landing():
            cp.wait_recv()
        for cp in sent:
            cp.wait_send()
        for cp in local:
            cp.wait()


def _gather_two_level(x, name):
    def body(x_ref, out_ref, send_sems, recv_sems, local_sem):
        x_, y_, c_ = lax.axis_index("x"), lax.axis_index("y"), lax.axis_index("c")
        me, sibling = (x_, y_, c_), (x_, y_, 1 - c_)
        chips = [(1 - x_, y_), (x_, 1 - y_), (1 - x_, 1 - y_)]

        def slab(px, py, pc):
            return out_ref.at[4 * px + 2 * py + pc]

        def copy(k, block, to, src=None):
            return pltpu.make_async_remote_copy(src_ref=slab(*block) if src is None else src, dst_ref=slab(*block),
                                                send_sem=send_sems.at[k], recv_sem=recv_sems.at[k], device_id=to,
                                                device_id_type=MESH)

        mine = pltpu.make_async_copy(x_ref, slab(*me), local_sem)
        mine.start()
        first = [copy(0, me, sibling, src=x_ref)] + [copy(1 + j, me, (*chip, c_), src=x_ref)
                                                     for j, chip in enumerate(chips)]
        for cp in first:
            cp.start()
        passed = [copy(4 + j, (*chip, c_), sibling) for j, chip in enumerate(chips)]
        for j, chip in enumerate(chips):
            copy(1 + j, (*chip, c_), me).wait_recv()
            passed[j].start()
        copy(0, sibling, me).wait_recv()
        for j, chip in enumerate(chips):
            copy(4 + j, (*chip, 1 - c_), me).wait_recv()
        for cp in first + passed:
            cp.wait_send()
        mine.wait()

    spec = pl.BlockSpec(memory_space=pl.ANY)
    return pl.pallas_call(
        body, name=name, out_shape=jax.ShapeDtypeStruct((N_DEV,) + x.shape, x.dtype), in_specs=[spec], out_specs=spec,
        scratch_shapes=[pltpu.SemaphoreType.DMA((N_DEV - 1,)), pltpu.SemaphoreType.DMA((N_DEV - 1,)),
                        pltpu.SemaphoreType.DMA],
    )(x)


def _with_comm(comm, n_in, n_out, first, last, compute):
    nc = comm.n if comm is not None else 0

    def body(*refs):
        ins, x_refs = refs[:n_in], refs[n_in:n_in + nc]
        outs = refs[n_in + nc:n_in + nc + n_out]
        out_refs = refs[n_in + nc + n_out:n_in + 2 * nc + n_out]
        sems = refs[n_in + 2 * nc + n_out:]
        if nc:
            @pl.when(first())
            def _():
                comm.start(x_refs, out_refs, sems)

        compute(*ins, *outs)
        if nc:
            @pl.when(last())
            def _():
                comm.wait(x_refs, out_refs, sems)

    return body


def _d_h(dz, w_in_t, res, comm=None, tm=512):
    m, k = dz.shape
    d = w_in_t.shape[1]
    tm = _div(m, tm, 8)
    steps = m // tm
    c_specs, c_shapes, c_scratch, c_arrays = _comm_parts(comm)

    def compute(dz_ref, w_ref, res_ref, o_ref):
        o_ref[...] = ALPHA * res_ref[...] + jnp.dot(dz_ref[...], w_ref[...], preferred_element_type=F32)

    body = _with_comm(comm, 3, 1, lambda: pl.program_id(0) == 0, lambda: pl.program_id(0) == steps - 1, compute)
    row = pl.BlockSpec((tm, d), lambda i: (i, 0))
    outs = pl.pallas_call(
        body, name="d_h" if comm is None else "d_h_comm", grid=(steps,),
        in_specs=[pl.BlockSpec((tm, k), lambda i: (i, 0)), pl.BlockSpec((k, d), lambda i: (0, 0)), row] + c_specs,
        out_specs=[row] + c_specs, out_shape=[jax.ShapeDtypeStruct((m, d), F32)] + c_shapes,
        scratch_shapes=c_scratch,
        compiler_params=_params("arbitrary"),
    )(dz, w_in_t, res, *c_arrays)
    return outs[0], outs[1:]


def _z_proj(h, w_in_t, b_p, tm=512):
    m, k = h.shape
    tm = _div(m, tm, 8)
    nt = (((1,), (1,)), ((), ()))

    def body(h_ref, w_ref, b_ref, zq_ref, zg_ref):
        a = h_ref[...].astype(BF16)
        zq = lax.dot_general(a, w_ref[:N_QKV, :], nt, preferred_element_type=F32)
        zq_ref[...] = (zq + b_ref[:, :N_QKV]).astype(BF16)
        zg_ref[...] = lax.dot_general(a, w_ref[N_QKV:, :], nt, preferred_element_type=F32) + b_ref[:, N_QKV:]

    return pl.pallas_call(
        body, name="z_proj", grid=(m // tm,),
        in_specs=[pl.BlockSpec((tm, k), lambda i: (i, 0)), pl.BlockSpec((N_ZP, k), lambda i: (0, 0)),
                  pl.BlockSpec((1, N_ZP), lambda i: (0, 0))],
        out_specs=[pl.BlockSpec((tm, N_QKV), lambda i: (i, 0)), pl.BlockSpec((tm, N_ZG), lambda i: (i, 0))],
        out_shape=[jax.ShapeDtypeStruct((m, N_QKV), BF16), jax.ShapeDtypeStruct((m, N_ZG), F32)],
        compiler_params=_params("parallel"),
    )(h, w_in_t, b_p)


def _linear_tn(a, g, *, name, tk=1024, tn=640, tm=2048, colsum=False):
    m, k = a.shape
    n = g.shape[1]
    tk = _div(k, tk, LANE)
    tn = _div(n, tn, LANE)
    tm = _div(m, tm, 8)
    steps = m // tm
    assert not colsum or tn == n

    def body(a_ref, g_ref, o_ref, *rest):
        acc_ref = rest[-1]
        s = pl.program_id(2)

        @pl.when(s == 0)
        def _():
            acc_ref[...] = jnp.zeros_like(acc_ref)
            if colsum:
                rest[0][...] = jnp.zeros_like(rest[0])

        a_blk = a_ref[...]
        acc_ref[...] += lax.dot_general(a_blk.astype(BF16), g_ref[...].astype(BF16), (((0,), (0,)), ((), ())),
                                        preferred_element_type=F32)
        if colsum:
            rest[0][...] += jnp.sum(a_blk.astype(F32), axis=0, keepdims=True)

        @pl.when(s == steps - 1)
        def _():
            o_ref[...] = acc_ref[...].astype(BF16)

    out_specs = [pl.BlockSpec((tk, tn), lambda i, j, s: (i, j))]
    out_shape = [jax.ShapeDtypeStruct((k, n), BF16)]
    if colsum:
        out_specs.append(pl.BlockSpec((1, tk), lambda i, j, s: (0, i)))
        out_shape.append(jax.ShapeDtypeStruct((1, k), F32))
    outs = pl.pallas_call(
        body, name=name, grid=(k // tk, n // tn, steps),
        in_specs=[pl.BlockSpec((tm, tk), lambda i, j, s: (s, i)), pl.BlockSpec((tm, tn), lambda i, j, s: (s, j))],
        out_specs=out_specs, out_shape=out_shape,
        scratch_shapes=[pltpu.VMEM((tk, tn), F32)],
        compiler_params=_params("parallel", "parallel", "arbitrary"),
    )(a, g)
    return outs if colsum else outs[0]


def _ln(u, g, b):
    mu = jnp.mean(u, axis=-1, keepdims=True)
    d = u - mu
    var = jnp.mean(d * d, axis=-1, keepdims=True)
    return d * lax.rsqrt(var + LN_EPS) * g + b


def _ln_bwd_block(dy, u, g):
    mu = jnp.mean(u, axis=-1, keepdims=True)
    dd = u - mu
    rstd = lax.rsqrt(jnp.mean(dd * dd, axis=-1, keepdims=True) + LN_EPS)
    xhat = dd * rstd
    dxh = dy * g
    m1 = jnp.mean(dxh, axis=-1, keepdims=True)
    m2 = jnp.mean(dxh * xhat, axis=-1, keepdims=True)
    return (rstd * (dxh - m1 - xhat * m2), jnp.sum(dy * xhat, axis=0, keepdims=True),
            jnp.sum(dy, axis=0, keepdims=True))


SCAN_ROWS = 512


def _tri(n, upper):
    r = lax.broadcasted_iota(jnp.int32, (n, n), 0)
    c = lax.broadcasted_iota(jnp.int32, (n, n), 1)
    return jnp.where((c >= r) if upper else (c <= r), 1.0, 0.0).astype(F32)


def _cumsum_logf(zg):
    s = zg.shape[0]
    t = _div(s, SCAN_ROWS, LANE)
    nb = s // t
    fcol = N_GATE // LANE

    def body(f_ref, c_ref, carry_ref):
        @pl.when(pl.program_id(0) == 0)
        def _():
            carry_ref[...] = jnp.zeros_like(carry_ref)

        f = f_ref[...]
        logf = jnp.minimum(f, 0.0) - jnp.log(1.0 + jnp.exp(-jnp.abs(f)))
        c = jnp.dot(_tri(t, False), logf, precision=lax.Precision.HIGHEST, preferred_element_type=F32)
        c = c + carry_ref[0:1, :]
        c_ref[...] = c
        carry_ref[...] = jnp.broadcast_to(c[t - 1:t, :], carry_ref.shape)

    return pl.pallas_call(
        body, name="cumsum_logf", grid=(nb,),
        in_specs=[pl.BlockSpec((t, LANE), lambda i: (i, fcol))],
        out_specs=pl.BlockSpec((t, LANE), lambda i: (i, 0)),
        out_shape=jax.ShapeDtypeStruct((s, LANE), F32),
        scratch_shapes=[pltpu.VMEM((8, LANE), F32)],
        compiler_params=_params("arbitrary"),
    )(zg)


def _forget_bwd(dcc, zg):
    s = zg.shape[0]
    t = _div(s, SCAN_ROWS, LANE)
    nb = s // t
    fcol = N_GATE // LANE

    def body(dc_ref, f_ref, o_ref, carry_ref):
        @pl.when(pl.program_id(0) == 0)
        def _():
            carry_ref[...] = jnp.zeros_like(carry_ref)

        lane = lax.broadcasted_iota(jnp.int32, (1, LANE), 1)
        dc = jnp.zeros((t, LANE), F32)
        for p in range(FOX_HEADS // 2):
            tile = dc_ref[:, p * LANE:(p + 1) * LANE]
            moved = pltpu.roll(tile, 2 * p, 1) if p else tile
            dc = jnp.where((lane == 2 * p) | (lane == 2 * p + 1), moved, dc)
        dlogf = jnp.dot(_tri(t, True), dc, precision=lax.Precision.HIGHEST, preferred_element_type=F32)
        dlogf = dlogf + carry_ref[0:1, :]
        o_ref[...] = (dlogf * jax.nn.sigmoid(-f_ref[...])).astype(BF16)
        carry_ref[...] = jnp.broadcast_to(dlogf[0:1, :], carry_ref.shape)

    return pl.pallas_call(
        body, name="forget_bwd", grid=(nb,),
        in_specs=[pl.BlockSpec((t, FOX_W), lambda i: (nb - 1 - i, 0)),
                  pl.BlockSpec((t, LANE), lambda i: (nb - 1 - i, fcol))],
        out_specs=pl.BlockSpec((t, LANE), lambda i: (nb - 1 - i, 0)),
        out_shape=jax.ShapeDtypeStruct((s, LANE), BF16),
        scratch_shapes=[pltpu.VMEM((8, LANE), F32)],
        compiler_params=_params("arbitrary"),
    )(dcc, zg)


KA_COL = SWA_Q // LANE
VA_COL = KA_COL + 1


def _half_masks():
    lane = lax.broadcasted_iota(jnp.int32, (1, LANE), 1)
    hi = lane >= HEAD_DIM
    return (jnp.logical_not(hi), hi)


def _both_halves(x, sel):
    xs = jnp.where(sel, x, 0.0)
    return xs + pltpu.roll(xs, HEAD_DIM, 1)


SWA_PER_KV = 4
WIDE = SWA_PER_KV * LANE


def _swa_bias():
    k = np.arange(2 * LANE)[:, None]
    q = np.arange(LANE)[None, :]
    dist = (q + LANE - k).astype(np.float32)
    valid = (dist >= 0) & (dist < LANE)
    per_head = [np.where(valid, np.float32(-s) * dist, np.float32(NEG_INF)) for s in SLOPES]
    return jnp.asarray(np.stack([np.concatenate(per_head[SWA_PER_KV * hk:SWA_PER_KV * (hk + 1)], axis=1)
                                 for hk in range(2)]), F32)


def _no_previous_block(i_blk):
    k = lax.broadcasted_iota(jnp.int32, (2 * LANE, WIDE), 0)
    return jnp.where((i_blk == 0) & (k < LANE), NEG_INF, 0.0)


def _stack_heads(ref, blk, hk, halves, scale):
    tiles = []
    for j in range(SWA_PER_KV):
        p = 2 * hk + j // 2
        t = ref[blk, p * LANE:(p + 1) * LANE]
        if scale:
            t = _scaled(t)
        tiles.append(jnp.where(halves[j % 2], t, jnp.zeros_like(t)))
    return jnp.concatenate(tiles, axis=0)


def _pair_tile(wide, pp, row_halves):
    a = wide[:, (2 * pp) * LANE:(2 * pp + 1) * LANE]
    b = wide[:, (2 * pp + 1) * LANE:(2 * pp + 2) * LANE]
    return jnp.where(row_halves[0], a, b).T


def _lane_blocks(rows8, hk):
    return jnp.concatenate([rows8[SWA_PER_KV * hk + j:SWA_PER_KV * hk + j + 1, :] for j in range(SWA_PER_KV)], axis=1)


def _row_halves():
    hi = lax.broadcasted_iota(jnp.int32, (LANE, 1), 0) >= HEAD_DIM
    return (jnp.logical_not(hi), hi)


NT = (((1,), (1,)), ((), ()))


def _scaled(q):
    return (q.astype(F32) * SCALE).astype(BF16)


SWA_GROUP = 4


def _swa_group(s_len):
    return SWA_GROUP if (s_len // LANE) % SWA_GROUP == 0 else 1


def _swa_specs(group):
    rows = group * LANE
    prev = lambda i: jnp.maximum(i * group - 1, 0)
    return [pl.BlockSpec((rows, SWA_Q), lambda i: (i, 0)),
            pl.BlockSpec((rows, LANE), lambda i: (i, KA_COL)), pl.BlockSpec((rows, LANE), lambda i: (i, VA_COL)),
            pl.BlockSpec((LANE, LANE), lambda i: (prev(i), KA_COL)),
            pl.BlockSpec((LANE, LANE), lambda i: (prev(i), VA_COL))]


def _swa_window(g, cur_ref, prev_ref):
    before = prev_ref[...] if g == 0 else cur_ref[(g - 1) * LANE:g * LANE, :]
    return jnp.concatenate([before, cur_ref[g * LANE:(g + 1) * LANE, :]], axis=0).astype(F32)


def _swa_fwd(zq, sinks):
    s_len = zq.shape[0]
    group = _swa_group(s_len)
    rows = group * LANE
    sink_lanes = jnp.repeat(sinks[:, :SWA_HEADS], LANE, axis=1)

    def body(q_ref, kc_ref, vc_ref, kp_ref, vp_ref, sink_ref, bias_ref, o_ref, lse_ref):
        halves = _half_masks()
        row_halves = _row_halves()
        for g in range(group):
            blk = slice(g * LANE, (g + 1) * LANE)
            kcat = _swa_window(g, kc_ref, kp_ref)
            vcat = _swa_window(g, vc_ref, vp_ref)
            lse_rows = []
            for hk in range(2):
                kb = _both_halves(kcat, halves[hk]).astype(BF16)
                v_t = _both_halves(vcat, halves[hk]).T.astype(BF16)
                q4 = _stack_heads(q_ref, blk, hk, halves, True)
                s_t = lax.dot_general(kb, q4, NT, preferred_element_type=F32) + bias_ref[hk]
                if g == 0:
                    s_t = s_t + _no_previous_block(pl.program_id(0))
                sink = sink_ref[:, hk * WIDE:(hk + 1) * WIDE]
                m = jnp.maximum(jnp.max(s_t, axis=0, keepdims=True), sink)
                pe = jnp.exp(s_t - m)
                den = jnp.sum(pe, axis=0, keepdims=True) + jnp.exp(sink - m)
                out_t = jnp.dot(v_t, (pe * (1.0 / den)).astype(BF16), preferred_element_type=F32)
                for pp in range(2):
                    p = 2 * hk + pp
                    o_ref[blk, p * LANE:(p + 1) * LANE] = _pair_tile(out_t, pp, row_halves).astype(BF16)
                lse4 = m + jnp.log(den)
                lse_rows += [lse4[:, j * LANE:(j + 1) * LANE] for j in range(SWA_PER_KV)]
            lse_ref[:, blk] = jnp.concatenate(lse_rows, axis=0)

    return pl.pallas_call(
        body, name="swa_fwd", grid=(s_len // rows,),
        in_specs=_swa_specs(group) + [pl.BlockSpec((1, SWA_HEADS * LANE), lambda i: (0, 0)),
                                      pl.BlockSpec((2, 2 * LANE, WIDE), lambda i: (0, 0, 0))],
        out_specs=[pl.BlockSpec((rows, SWA_Q), lambda i: (i, 0)), pl.BlockSpec((SWA_HEADS, rows), lambda i: (0, i))],
        out_shape=[jax.ShapeDtypeStruct((s_len, SWA_Q), BF16), jax.ShapeDtypeStruct((SWA_HEADS, s_len), F32)],
        compiler_params=_params("parallel"),
    )(zq, zq, zq, zq, zq, sink_lanes, _swa_bias())


def _swa_bwd(zq, sinks, o, do, lse):
    s_len = zq.shape[0]
    group = _swa_group(s_len)
    rows = group * LANE

    def body(q_ref, kc_ref, vc_ref, kp_ref, vp_ref, sink_ref, bias_ref, o_ref, do_ref, lse_ref,
             dq_ref, dk_ref, dv_ref, ds_ref):
        halves = _half_masks()
        row_halves = _row_halves()
        lane = lax.broadcasted_iota(jnp.int32, (1, LANE), 1)
        dsink = jnp.zeros((1, LANE), F32)
        for g in range(group):
            blk = slice(g * LANE, (g + 1) * LANE)
            i_blk = pl.program_id(0) * group + g
            kcat = _swa_window(g, kc_ref, kp_ref)
            vcat = _swa_window(g, vc_ref, vp_ref)
            lse_rows = lse_ref[:, blk]
            prod = do_ref[blk, :].astype(F32) * o_ref[blk, :].astype(F32)
            select = (lax.broadcasted_iota(jnp.int32, (SWA_HEADS, SWA_Q), 1) // HEAD_DIM
                      == lax.broadcasted_iota(jnp.int32, (SWA_HEADS, SWA_Q), 0))
            picks = jnp.where(select, 1.0, 0.0).astype(BF16)
            prod_hi = prod.astype(BF16)
            prod_lo = (prod - prod_hi.astype(F32)).astype(BF16)
            delta_rows = (lax.dot_general(picks, prod_hi, NT, preferred_element_type=F32)
                          + lax.dot_general(picks, prod_lo, NT, preferred_element_type=F32))
            dk_tot = jnp.zeros((2 * LANE, LANE), F32)
            dv_tot = jnp.zeros((2 * LANE, LANE), F32)
            for hk in range(2):
                kb = _both_halves(kcat, halves[hk])
                k_t = kb.T.astype(BF16)
                kb = kb.astype(BF16)
                vb = _both_halves(vcat, halves[hk]).astype(BF16)
                q4 = _stack_heads(q_ref, blk, hk, halves, True)
                do4 = _stack_heads(do_ref, blk, hk, halves, False)
                lse4 = _lane_blocks(lse_rows, hk)
                delta4 = _lane_blocks(delta_rows, hk)
                s_t = lax.dot_general(kb, q4, NT, preferred_element_type=F32) + bias_ref[hk]
                if g == 0:
                    s_t = s_t + _no_previous_block(pl.program_id(0))
                p_t = jnp.exp(s_t - lse4)
                dp_t = lax.dot_general(vb, do4, NT, preferred_element_type=F32)
                ds_t = (p_t * (dp_t - delta4)).astype(BF16)
                sink_part = jnp.exp(sink_ref[:, hk * WIDE:(hk + 1) * WIDE] - lse4) * delta4
                for j in range(SWA_PER_KV):
                    dsink_h = -jnp.sum(sink_part[:, j * LANE:(j + 1) * LANE], axis=1, keepdims=True)
                    dsink = dsink + jnp.where(lane == SWA_PER_KV * hk + j, dsink_h, 0.0)
                dq_t = jnp.dot(k_t, ds_t, preferred_element_type=F32)
                for pp in range(2):
                    p = 2 * hk + pp
                    dq_ref[blk, p * LANE:(p + 1) * LANE] = (_pair_tile(dq_t, pp, row_halves) * SCALE).astype(BF16)
                dk_acc = jnp.dot(ds_t, q4, preferred_element_type=F32)
                dv_acc = jnp.dot(p_t.astype(BF16), do4, preferred_element_type=F32)
                dk_tot = dk_tot + jnp.where(halves[hk], dk_acc + pltpu.roll(dk_acc, HEAD_DIM, 1), 0.0)
                dv_tot = dv_tot + jnp.where(halves[hk], dv_acc + pltpu.roll(dv_acc, HEAD_DIM, 1), 0.0)
            cur = pl.ds(pl.multiple_of(i_blk * LANE, LANE), LANE)
            dk_ref[cur, :] = dk_tot[LANE:, :]
            dv_ref[cur, :] = dv_tot[LANE:, :]

            def add_previous(i_blk=i_blk, dk_tot=dk_tot, dv_tot=dv_tot):
                prv = pl.ds(pl.multiple_of((i_blk - 1) * LANE, LANE), LANE)
                dk_ref[prv, :] += dk_tot[:LANE, :]
                dv_ref[prv, :] += dv_tot[:LANE, :]

            if g == 0:
                pl.when(i_blk > 0)(add_previous)
            else:
                add_previous()

        @pl.when(pl.program_id(0) == 0)
        def _():
            ds_ref[...] = jnp.zeros_like(ds_ref)

        ds_ref[...] += dsink

    blk512 = pl.BlockSpec((rows, SWA_Q), lambda i: (i, 0))
    full = pl.BlockSpec((s_len, LANE), lambda i: (0, 0))
    vec = pl.BlockSpec((1, LANE), lambda i: (0, 0))
    return pl.pallas_call(
        body, name="swa_bwd", grid=(s_len // rows,),
        in_specs=_swa_specs(group) + [pl.BlockSpec((1, SWA_HEADS * LANE), lambda i: (0, 0)),
                                      pl.BlockSpec((2, 2 * LANE, WIDE), lambda i: (0, 0, 0)), blk512, blk512,
                                      pl.BlockSpec((SWA_HEADS, rows), lambda i: (0, i))],
        out_specs=[blk512, full, full, vec],
        out_shape=[jax.ShapeDtypeStruct((s_len, SWA_Q), BF16), jax.ShapeDtypeStruct((s_len, LANE), F32),
                   jax.ShapeDtypeStruct((s_len, LANE), F32), jax.ShapeDtypeStruct((1, LANE), F32)],
        compiler_params=_params("arbitrary"),
    )(zq, zq, zq, zq, zq, jnp.repeat(sinks[:, :SWA_HEADS], LANE, axis=1), _swa_bias(), o, do, lse)


QB_COL = (SWA_Q + 2 * SWA_KV) // LANE
KB_COL = QB_COL + FOX_W // LANE
VB_COL = KB_COL + FOX_W // LANE
N_PAIR = FOX_HEADS // 2


def _causal(t, keys_first=False):
    r = lax.broadcasted_iota(jnp.int32, (t, t), 0)
    c = lax.broadcasted_iota(jnp.int32, (t, t), 1)
    return c >= r if keys_first else r >= c


N_SPLIT = 3


def _own_half(e):
    hi = lax.broadcasted_iota(jnp.int32, (1, LANE), 1) >= HEAD_DIM
    return hi if e else jnp.logical_not(hi)


def _feature_lane(e, t):
    return HEAD_DIM * (1 - e) + t


def _feature_tables():
    wide = FOX_HEADS * LANE
    place_q, place_k = np.zeros((N_SPLIT * LANE, wide), np.float32), np.zeros((N_SPLIT * LANE, wide), np.float32)
    ones_q, ones_k, ones_v, own = (np.zeros((1, wide), np.float32) for _ in range(4))
    for h in range(FOX_HEADS):
        e = h % 2
        own[0, h * LANE + HEAD_DIM * e:h * LANE + HEAD_DIM * (e + 1)] = 1.0
        ones_v[0, h * LANE + _feature_lane(e, 0)] = 1.0
        for t in range(N_SPLIT):
            place_q[t * LANE + h, h * LANE + _feature_lane(e, t)] = 1.0
            ones_q[0, h * LANE + _feature_lane(e, N_SPLIT + t)] = 1.0
            ones_k[0, h * LANE + _feature_lane(e, t)] = 1.0
            place_k[t * LANE + h, h * LANE + _feature_lane(e, N_SPLIT + t)] = -1.0
    return tuple(jnp.asarray(a) for a in (place_q, place_k, ones_q, ones_k, ones_v, own))


def _fox_prep(zq, c, tm=256):
    s_len = zq.shape[0]
    tm = _div(s_len, tm, 8)
    wide = FOX_HEADS * LANE

    def body(z_ref, c_ref, pq_ref, pk_ref, oq_ref, ok_ref, ov_ref, own_ref, qx_ref, kx_ref, vx_ref):
        rest = c_ref[...]
        parts = []
        for _ in range(N_SPLIT):
            part = rest.astype(BF16).astype(F32)
            rest = rest - part
            parts.append(part)
        parts = jnp.concatenate(parts, axis=1)
        qf = jnp.dot(parts, pq_ref[...], preferred_element_type=F32) + oq_ref[...]
        kf = jnp.dot(parts, pk_ref[...], preferred_element_type=F32) + ok_ref[...]
        own = own_ref[...] > 0.5
        for p in range(N_PAIR):
            cols = slice(2 * p * LANE, (2 * p + 2) * LANE)
            pair = lambda col: jnp.tile(z_ref[:, (col + p) * LANE:(col + p + 1) * LANE].astype(F32), (1, 2))
            qx_ref[:, cols] = jnp.where(own[:, cols], pair(QB_COL) * SCALE, qf[:, cols]).astype(BF16)
            kx_ref[:, cols] = jnp.where(own[:, cols], pair(KB_COL), kf[:, cols]).astype(BF16)
            vx_ref[:, cols] = jnp.where(own[:, cols], pair(VB_COL), ov_ref[:, cols]).astype(BF16)

    out = jax.ShapeDtypeStruct((s_len, wide), BF16)
    blk = pl.BlockSpec((tm, wide), lambda i: (i, 0))
    table = pl.BlockSpec((N_SPLIT * LANE, wide), lambda i: (0, 0))
    vec = pl.BlockSpec((1, wide), lambda i: (0, 0))
    return pl.pallas_call(
        body, name="fox_prep", grid=(s_len // tm,),
        in_specs=[pl.BlockSpec((tm, N_QKV), lambda i: (i, 0)), pl.BlockSpec((tm, LANE), lambda i: (i, 0)),
                  table, table, vec, vec, vec, vec],
        out_specs=[blk, blk, blk], out_shape=[out, out, out],
        compiler_params=_params("parallel"),
    )(zq, c, *_feature_tables())


def _comm_parts(comm):
    return ([], [], [], []) if comm is None else (comm.specs, comm.out_shape, comm.scratch, comm.arrays)


def _fox_fwd(qx, kx, vx, comm=None, t_cap=1024):
    s_len = qx.shape[0]
    t = _div(s_len, t_cap, 2 * LANE)
    half = t // 2
    nq = s_len // t
    c_specs, c_shapes, c_scratch, c_arrays = _comm_parts(comm)

    def compute(q_ref, k_ref, v_ref, o_ref, o32_ref, m_ref, l_ref):
        i = pl.program_id(1)
        qs = [q_ref[:, e * LANE:(e + 1) * LANE] for e in range(2)]

        def attend(e, q, keys, m, acc, mask):
            s2 = lax.dot_general(q, k_ref[keys, e * LANE:(e + 1) * LANE], NT, preferred_element_type=F32) * LOG2E
            if mask is not None:
                s2 = jnp.where(mask, s2, NEG_INF)
            mn = jnp.maximum(m, jnp.ceil(jnp.max(s2, axis=1, keepdims=True)))
            pe = jnp.exp2(s2 - mn).astype(BF16)
            acc = acc * jnp.exp2(m - mn) + jnp.dot(pe, v_ref[keys, e * LANE:(e + 1) * LANE],
                                                   preferred_element_type=F32)
            return mn, acc

        def step(j, carry):
            keys = pl.ds(pl.multiple_of(j * t, t), t)
            return tuple(attend(e, qs[e], keys, *carry[e], None) for e in range(2))

        def diagonal(carry):
            rr = lax.broadcasted_iota(jnp.int32, (t, half), 0)
            cc = lax.broadcasted_iota(jnp.int32, (t, half), 1)
            new = []
            for e in range(2):
                m, acc = attend(e, qs[e], pl.ds(pl.multiple_of(i * t, half), half), *carry[e], rr >= cc)
                m_hi, acc_hi = attend(e, qs[e][half:, :], pl.ds(pl.multiple_of(i * t + half, half), half),
                                      m[half:, :], acc[half:, :], _causal(half))
                new.append((jnp.concatenate([m[:half, :], m_hi], axis=0),
                            jnp.concatenate([acc[:half, :], acc_hi], axis=0)))
            return tuple(new)

        init = (jnp.full((t, 1), NEG_INF, F32), jnp.zeros((t, LANE), F32))
        carry = diagonal(lax.fori_loop(0, i, step, (init, init)))
        outs, ls = [], []
        for e in range(2):
            m, acc = carry[e]
            l = acc[:, _feature_lane(e, 0):_feature_lane(e, 0) + 1]
            outs.append(acc / l)
            ls.append(l)
        out = jnp.where(_own_half(1), outs[1], outs[0])
        o_ref[...] = out.astype(BF16)
        o32_ref[...] = out
        m_ref[...] = jnp.where(_own_half(1), carry[1][0], carry[0][0])
        l_ref[...] = jnp.where(_own_half(1), ls[1], ls[0])

    body = _with_comm(comm, 3, 4, lambda: (pl.program_id(0) == 0) & (pl.program_id(1) == 0),
                      lambda: (pl.program_id(0) == N_PAIR - 1) & (pl.program_id(1) == nq - 1), compute)
    pair = pl.BlockSpec((s_len, 2 * LANE), lambda p, i: (0, p))
    tile = pl.BlockSpec((t, LANE), lambda p, i: (i, p))
    wide = jax.ShapeDtypeStruct((s_len, FOX_W), F32)
    outs = pl.pallas_call(
        body, name="fox_fwd" if comm is None else "fox_fwd_comm%d" % comm.n, grid=(N_PAIR, nq),
        in_specs=[pl.BlockSpec((t, 2 * LANE), lambda p, i: (i, p)), pair, pair] + c_specs,
        out_specs=[tile, tile, tile, tile] + c_specs,
        out_shape=[jax.ShapeDtypeStruct((s_len, FOX_W), BF16), wide, wide, wide] + c_shapes,
        scratch_shapes=c_scratch,
        compiler_params=_params("arbitrary", "arbitrary"),
    )(qx, kx, vx, *c_arrays)
    return outs[0], outs[1], outs[2], outs[3], outs[4:]


def _fox_stats(o, do, m, l, tm=256):
    s_len = o.shape[0]
    tm = _div(s_len, tm, LANE)

    def body(o_ref, do_ref, m_ref, l_ref, dox_ref, st_ref):
        lane = lax.broadcasted_iota(jnp.int32, (1, LANE), 1)
        for p in range(N_PAIR):
            cols = slice(p * LANE, (p + 1) * LANE)
            dout = do_ref[:, cols]
            prod = o_ref[:, cols] * dout.astype(F32)
            shift = m_ref[:, cols]
            inv_l = 1.0 / l_ref[:, cols]
            st = jnp.zeros((tm, LANE), F32)
            for e in range(2):
                h = 2 * p + e
                dox_ref[:, h * LANE:(h + 1) * LANE] = jnp.where(_own_half(e), dout, jnp.zeros_like(dout))
                st = jnp.where(lane == e, shift[:, e * HEAD_DIM:e * HEAD_DIM + 1], st)
                delta = jnp.sum(jnp.where(_own_half(e), prod, 0.0), axis=1, keepdims=True)
                st = jnp.where(lane == 2 + e, delta, st)
                st = jnp.where(lane == 4 + e, inv_l[:, e * HEAD_DIM:e * HEAD_DIM + 1], st)
            st_ref[p] = st.T[:8, :]

    row = pl.BlockSpec((tm, FOX_W), lambda i: (i, 0))
    return pl.pallas_call(
        body, name="fox_stats", grid=(s_len // tm,), in_specs=[row, row, row, row],
        out_specs=[pl.BlockSpec((tm, FOX_HEADS * LANE), lambda i: (i, 0)),
                   pl.BlockSpec((N_PAIR, 8, tm), lambda i: (0, 0, i))],
        out_shape=[jax.ShapeDtypeStruct((s_len, FOX_HEADS * LANE), BF16),
                   jax.ShapeDtypeStruct((N_PAIR, 8, s_len), F32)],
        compiler_params=_params("parallel"),
    )(o, do, m, l)


def _fox_bwd(qx, kx, vx, dox, stats, comm=None, t_cap=1024):
    s_len = qx.shape[0]
    t = _div(s_len, t_cap, 2 * LANE)
    half = t // 2
    n = s_len // t
    c_specs, c_shapes, c_scratch, c_arrays = _comm_parts(comm)

    def compute(q_ref, do_ref, st_ref, k_ref, v_ref, dq_ref, dk_ref, dv_ref, dc_ref):
        j = pl.program_id(1)
        lane = lax.broadcasted_iota(jnp.int32, (1, LANE), 1)

        @pl.when(j == 0)
        def _():
            dq_ref[...] = jnp.zeros_like(dq_ref)

        ks = [k_ref[:, e * LANE:(e + 1) * LANE] for e in range(2)]
        vs = [v_ref[:, e * LANE:(e + 1) * LANE] for e in range(2)]
        ks_t = [k.astype(F32).T.astype(BF16) for k in ks]

        def tile(e, keys, queries, mask):
            q = q_ref[queries, e * LANE:(e + 1) * LANE]
            dout = do_ref[queries, e * LANE:(e + 1) * LANE]
            s_t = lax.dot_general(ks[e][keys, :], q, NT, preferred_element_type=F32) * LOG2E
            if mask is not None:
                s_t = jnp.where(mask, s_t, NEG_INF)
            p_t = jnp.exp2(s_t - st_ref[0, e:e + 1, queries]).astype(BF16).astype(F32) * st_ref[0, 4 + e:5 + e, queries]
            dp_t = lax.dot_general(vs[e][keys, :], dout, NT, preferred_element_type=F32)
            ds_f = p_t * (dp_t - st_ref[0, 2 + e:3 + e, queries])
            ds_t = ds_f.astype(BF16)
            dq_e = jnp.dot(ks_t[e][:, keys], ds_t, preferred_element_type=F32)
            return (jnp.dot(ds_t, q, preferred_element_type=F32), jnp.dot(p_t.astype(BF16), dout, preferred_element_type=F32),
                    jnp.sum(ds_f, axis=1, keepdims=True), jnp.where(_row_halves()[e], dq_e, 0.0))

        def step(i, carry):
            queries = pl.ds(pl.multiple_of(i * t, t), t)
            new, dq = [], 0.0
            for e in range(2):
                dk, dv, dc, dq_e = tile(e, slice(None), queries, None)
                new.append((carry[e][0] + dk, carry[e][1] + dv, carry[e][2] + dc))
                dq = dq + dq_e
            dq_ref[queries, :] += dq.T * SCALE
            return tuple(new)

        def diagonal():
            all_q = pl.ds(pl.multiple_of(j * t, t), t)
            late_q = pl.ds(pl.multiple_of(j * t + half, half), half)
            kk = lax.broadcasted_iota(jnp.int32, (half, t), 0)
            qq = lax.broadcasted_iota(jnp.int32, (half, t), 1)
            new, dq, dq_late = [], 0.0, 0.0
            for e in range(2):
                early = tile(e, slice(0, half), all_q, qq >= kk)
                late = tile(e, slice(half, t), late_q, _causal(half, keys_first=True))
                new.append(tuple(jnp.concatenate([a, b], axis=0) for a, b in zip(early[:3], late[:3])))
                dq, dq_late = dq + early[3], dq_late + late[3]
            dq_ref[all_q, :] += dq.T * SCALE
            dq_ref[late_q, :] += dq_late.T * SCALE
            return tuple(new)

        (dk0, dv0, dc0), (dk1, dv1, dc1) = lax.fori_loop(j + 1, n, step, diagonal())
        dk_ref[...] = jnp.where(_own_half(1), dk1, dk0).astype(BF16)
        dv_ref[...] = jnp.where(_own_half(1), dv1, dv0).astype(BF16)
        dc_ref[...] = jnp.where(lane == 0, -dc0, jnp.where(lane == 1, -dc1, 0.0))

    body = _with_comm(comm, 5, 4, lambda: (pl.program_id(0) == 0) & (pl.program_id(1) == 0),
                      lambda: (pl.program_id(0) == N_PAIR - 1) & (pl.program_id(1) == n - 1), compute)
    pair = pl.BlockSpec((s_len, 2 * LANE), lambda p, j: (0, p))
    blk = pl.BlockSpec((t, 2 * LANE), lambda p, j: (j, p))
    tile = pl.BlockSpec((t, LANE), lambda p, j: (j, p))
    outs = pl.pallas_call(
        body, name="fox_bwd" if comm is None else "fox_bwd_comm", grid=(N_PAIR, n),
        in_specs=[pair, pair, pl.BlockSpec((1, 8, s_len), lambda p, j: (p, 0, 0)), blk, blk] + c_specs,
        out_specs=[pl.BlockSpec((s_len, LANE), lambda p, j: (0, p)), tile, tile, tile] + c_specs,
        out_shape=[jax.ShapeDtypeStruct((s_len, FOX_W), F32), jax.ShapeDtypeStruct((s_len, FOX_W), BF16),
                   jax.ShapeDtypeStruct((s_len, FOX_W), BF16), jax.ShapeDtypeStruct((s_len, FOX_W), F32)] + c_shapes,
        scratch_shapes=c_scratch,
        compiler_params=_params("arbitrary", "arbitrary"),
    )(qx, dox, stats, kx, vx, *c_arrays)
    return outs[0], outs[1], outs[2], outs[3], outs[4:]


def _mixer_out(attn_a, attn_b, zg, h, wpa, wpb, wout, g, b, tm=256):
    m = h.shape[0]
    tm = _div(m, tm, 8)

    def body(a_ref, b_ref, ga_ref, gb_ref, h_ref, wpa_ref, wpb_ref, wout_ref, g_ref, bb_ref, h1_ref, u_ref, mg_ref):
        ya = jnp.dot(a_ref[...], wpa_ref[...], preferred_element_type=F32)
        yb = jnp.dot(b_ref[...], wpb_ref[...], preferred_element_type=F32)
        merged = (jax.nn.sigmoid(ga_ref[...]) * ya + jax.nn.sigmoid(gb_ref[...]) * yb).astype(BF16)
        u = ALPHA * h_ref[...] + jnp.dot(merged, wout_ref[...], preferred_element_type=F32)
        u_ref[...] = u
        h1_ref[...] = _ln(u, g_ref[...], bb_ref[...])
        mg_ref[...] = merged

    row = pl.BlockSpec((tm, D_MODEL), lambda i: (i, 0))
    att = pl.BlockSpec((tm, SWA_Q), lambda i: (i, 0))
    vec = pl.BlockSpec((1, D_MODEL), lambda i: (0, 0))
    wsm = pl.BlockSpec((SWA_Q, D_MODEL), lambda i: (0, 0))
    return pl.pallas_call(
        body, name="mixer_out", grid=(m // tm,),
        in_specs=[att, att, row, pl.BlockSpec((tm, D_MODEL), lambda i: (i, 1)), row, wsm, wsm,
                  pl.BlockSpec((D_MODEL, D_MODEL), lambda i: (0, 0)), vec, vec],
        out_specs=[row, row, row],
        out_shape=[jax.ShapeDtypeStruct((m, D_MODEL), F32), jax.ShapeDtypeStruct((m, D_MODEL), F32),
                   jax.ShapeDtypeStruct((m, D_MODEL), BF16)],
        compiler_params=_params("parallel"),
    )(attn_a, attn_b, zg, zg, h, wpa, wpb, wout, g, b)


def _mixer_bwd(dh1, u1, g, wout, attn_a, attn_b, zg, wpa, wpb, tm=256):
    m = dh1.shape[0]
    tm = _div(m, tm, 8)

    def body(dh_ref, u_ref, g_ref, wout_ref, a_ref, b_ref, ga_ref, gb_ref, wpa_ref, wpb_ref,
             du_ref, dg_ref, db_ref, dya_ref, dyb_ref, dga_ref, dgb_ref, da_ref, dbb_ref):
        @pl.when(pl.program_id(0) == 0)
        def _():
            dg_ref[...] = jnp.zeros_like(dg_ref)
            db_ref[...] = jnp.zeros_like(db_ref)

        du, dg, db = _ln_bwd_block(dh_ref[...], u_ref[...], g_ref[...])
        du_ref[...] = du
        dg_ref[...] += dg
        db_ref[...] += db
        dm = lax.dot_general(du.astype(BF16), wout_ref[...], (((1,), (1,)), ((), ())), preferred_element_type=F32)
        for x_ref, gate_ref, w_ref, dy_ref, dgate_ref, dattn_ref in (
                (a_ref, ga_ref, wpa_ref, dya_ref, dga_ref, da_ref), (b_ref, gb_ref, wpb_ref, dyb_ref, dgb_ref, dbb_ref)):
            sg = jax.nn.sigmoid(gate_ref[...])
            dy = (dm * sg).astype(BF16)
            dy_ref[...] = dy
            y = jnp.dot(x_ref[...], w_ref[...], preferred_element_type=F32)
            dgate_ref[...] = (dm * y * sg * (1.0 - sg)).astype(BF16)
            dattn_ref[...] = lax.dot_general(dy, w_ref[...], (((1,), (1,)), ((), ())),
                                             preferred_element_type=F32).astype(BF16)

    row = pl.BlockSpec((tm, D_MODEL), lambda i: (i, 0))
    att = pl.BlockSpec((tm, SWA_Q), lambda i: (i, 0))
    vec = pl.BlockSpec((1, D_MODEL), lambda i: (0, 0))
    wsm = pl.BlockSpec((SWA_Q, D_MODEL), lambda i: (0, 0))
    wide = jax.ShapeDtypeStruct((m, D_MODEL), BF16)
    narrow = jax.ShapeDtypeStruct((m, SWA_Q), BF16)
    sums = jax.ShapeDtypeStruct((1, D_MODEL), F32)
    return pl.pallas_call(
        body, name="mixer_bwd", grid=(m // tm,),
        in_specs=[row, row, vec, pl.BlockSpec((D_MODEL, D_MODEL), lambda i: (0, 0)), att, att, row,
                  pl.BlockSpec((tm, D_MODEL), lambda i: (i, 1)), wsm, wsm],
        out_specs=[row, vec, vec, row, row, row, row, att, att],
        out_shape=[jax.ShapeDtypeStruct((m, D_MODEL), F32), sums, sums, wide, wide, wide, wide, narrow, narrow],
        compiler_params=_params("arbitrary"),
    )(dh1, u1, g, wout, attn_a, attn_b, zg, zg, wpa, wpb)


def _shift_down(x, k, halo, first):
    rows = lax.broadcasted_iota(jnp.int32, (x.shape[0], 1), 0)
    y = pltpu.roll(x, k, 0)
    for r in range(k):
        fill = jnp.where(first, 0.0, halo[8 - k + r:8 - k + r + 1, :])
        y = jnp.where(rows == r, fill, y)
    return y


def _shift_up(x, k, halo, last):
    n = x.shape[0]
    rows = lax.broadcasted_iota(jnp.int32, (n, 1), 0)
    y = pltpu.roll(x, n - k, 0)
    for r in range(k):
        fill = jnp.where(last, 0.0, halo[r:r + 1, :])
        y = jnp.where(rows == n - k + r, fill, y)
    return y


def _conv_act(gate, gate_m1, gate_m2, cw, cb):
    return cb + cw[0:1, :] * gate_m2 + cw[1:2, :] * gate_m1 + cw[2:3, :] * gate


def _ffn_in_conv(h1, wfi, cw, cb, tm=256):
    s_len = h1.shape[0]
    tm = _div(s_len, tm, 8)
    hb = tm // 8

    def body(a_ref, ap_ref, w_ref, cw_ref, cb_ref, gu_ref, act_ref):
        first = pl.program_id(0) == 0
        a = a_ref[...].astype(BF16)
        before = ap_ref[...].astype(BF16)
        for c in range(N_CHUNK):
            gate = jnp.dot(a, w_ref[c], preferred_element_type=F32)
            up = jnp.dot(a, w_ref[N_CHUNK + c], preferred_element_type=F32)
            halo = jnp.dot(before, w_ref[c], preferred_element_type=F32)
            gu_ref[c, 0] = gate
            gu_ref[c, 1] = up
            conv = _conv_act(gate, _shift_down(gate, 1, halo, first), _shift_down(gate, 2, halo, first),
                             cw_ref[c], cb_ref[c])
            act_ref[c] = (conv * jax.nn.sigmoid(conv) * up).astype(BF16)

    return pl.pallas_call(
        body, name="ffn_in_conv", grid=(s_len // tm,),
        in_specs=[pl.BlockSpec((tm, D_MODEL), lambda i: (i, 0)),
                  pl.BlockSpec((8, D_MODEL), lambda i: (jnp.maximum(i * hb - 1, 0), 0)),
                  pl.BlockSpec((N_DEV, D_MODEL, FF_CHUNK), lambda i: (0, 0, 0)),
                  pl.BlockSpec((N_CHUNK, 8, FF_CHUNK), lambda i: (0, 0, 0)),
                  pl.BlockSpec((N_CHUNK, 1, FF_CHUNK), lambda i: (0, 0, 0))],
        out_specs=[pl.BlockSpec((N_CHUNK, 2, tm, FF_CHUNK), lambda i: (0, 0, i, 0)),
                   pl.BlockSpec((N_CHUNK, tm, FF_CHUNK), lambda i: (0, i, 0))],
        out_shape=[jax.ShapeDtypeStruct((N_CHUNK, 2, s_len, FF_CHUNK), F32),
                   jax.ShapeDtypeStruct((N_CHUNK, s_len, FF_CHUNK), BF16)],
        compiler_params=_params("parallel"),
    )(h1, h1, wfi, cw, cb)


def _ffn_out_ln(act, wfo, res, g, b, target=None, tm=512):
    s_len = res.shape[0]
    tm = _div(s_len, tm, 8)
    last = target is not None

    def body(a_ref, w_ref, res_ref, g_ref, b_ref, *rest):
        u = ALPHA * res_ref[...]
        for c in range(N_CHUNK):
            u = u + jnp.dot(a_ref[c], w_ref[c], preferred_element_type=F32)
        y = _ln(u, g_ref[...], b_ref[...])
        if not last:
            u_ref, y_ref = rest
            u_ref[...] = u
            y_ref[...] = y
            return
        t_ref, u_ref, dy_ref, loss_ref = rest
        u_ref[...] = u

        @pl.when(pl.program_id(0) == 0)
        def _():
            loss_ref[...] = jnp.zeros_like(loss_ref)

        err = y - t_ref[...]
        dy_ref[...] = err / D_MODEL
        loss_ref[...] += 0.5 * jnp.sum(jnp.sum(err * err, axis=1, keepdims=True) / D_MODEL, axis=0, keepdims=True)

    row = pl.BlockSpec((tm, D_MODEL), lambda i: (i, 0))
    vec = pl.BlockSpec((1, D_MODEL), lambda i: (0, 0))
    wide = jax.ShapeDtypeStruct((s_len, D_MODEL), F32)
    return pl.pallas_call(
        body, name="ffn_out_ln_loss" if last else "ffn_out_ln", grid=(s_len // tm,),
        in_specs=[pl.BlockSpec((N_CHUNK, tm, FF_CHUNK), lambda i: (0, i, 0)),
                  pl.BlockSpec((N_CHUNK, FF_CHUNK, D_MODEL), lambda i: (0, 0, 0)), row, vec, vec] + [row] * last,
        out_specs=[row, row] + [pl.BlockSpec((8, LANE), lambda i: (0, 0))] * last,
        out_shape=[wide, wide] + [jax.ShapeDtypeStruct((8, LANE), F32)] * last,
        compiler_params=_params("arbitrary" if last else "parallel"),
    )(act, wfo, res, g, b, *([target] if last else []))


def _ffn_out_bwd(dh2, u2, g, wfo, tm=512):
    s_len = dh2.shape[0]
    tm = _div(s_len, tm, 8)

    def body(dh_ref, u_ref, g_ref, w_ref, du_ref, dg_ref, db_ref, o_ref):
        @pl.when(pl.program_id(0) == 0)
        def _():
            dg_ref[...] = jnp.zeros_like(dg_ref)
            db_ref[...] = jnp.zeros_like(db_ref)

        du, dg, db = _ln_bwd_block(dh_ref[...], u_ref[...], g_ref[...])
        du_ref[...] = du
        dg_ref[...] += dg
        db_ref[...] += db
        du_b = du.astype(BF16)
        for c in range(N_CHUNK):
            o_ref[c] = lax.dot_general(du_b, w_ref[c], (((1,), (1,)), ((), ())), preferred_element_type=F32)

    row = pl.BlockSpec((tm, D_MODEL), lambda i: (i, 0))
    vec = pl.BlockSpec((1, D_MODEL), lambda i: (0, 0))
    sums = jax.ShapeDtypeStruct((1, D_MODEL), F32)
    return pl.pallas_call(
        body, name="ffn_out_bwd", grid=(s_len // tm,),
        in_specs=[row, row, vec, pl.BlockSpec((N_CHUNK, FF_CHUNK, D_MODEL), lambda i: (0, 0, 0))],
        out_specs=[row, vec, vec, pl.BlockSpec((N_CHUNK, tm, FF_CHUNK), lambda i: (0, i, 0))],
        out_shape=[jax.ShapeDtypeStruct((s_len, D_MODEL), F32), sums, sums,
                   jax.ShapeDtypeStruct((N_CHUNK, s_len, FF_CHUNK), F32)],
        compiler_params=_params("arbitrary"),
    )(dh2, u2, g, wfo)


def _g_w_ffn_out(act, du, tm=2048):
    s_len = du.shape[0]
    tm = _div(s_len, tm, 8)
    steps = s_len // tm

    def body(a_ref, g_ref, o_ref, acc_ref):
        s = pl.program_id(1)

        @pl.when(s == 0)
        def _():
            acc_ref[...] = jnp.zeros_like(acc_ref)

        acc_ref[...] += lax.dot_general(a_ref[0], g_ref[...].astype(BF16), (((0,), (0,)), ((), ())),
                                        preferred_element_type=F32)

        @pl.when(s == steps - 1)
        def _():
            o_ref[0] = acc_ref[...].astype(BF16)

    return pl.pallas_call(
        body, name="g_w_ffn_out", grid=(N_CHUNK, steps),
        in_specs=[pl.BlockSpec((1, tm, FF_CHUNK), lambda c, s: (c, s, 0)),
                  pl.BlockSpec((tm, D_MODEL), lambda c, s: (s, 0))],
        out_specs=pl.BlockSpec((1, FF_CHUNK, D_MODEL), lambda c, s: (c, 0, 0)),
        out_shape=jax.ShapeDtypeStruct((N_CHUNK, FF_CHUNK, D_MODEL), BF16),
        scratch_shapes=[pltpu.VMEM((FF_CHUNK, D_MODEL), F32)],
        compiler_params=_params("parallel", "arbitrary"),
    )(act, du)


def _g_w_ffn_in(h1, dgu, tm=2048):
    s_len = h1.shape[0]
    tm = _div(s_len, tm, 8)
    steps = s_len // tm

    def body(a_ref, g_ref, o_ref, acc_ref):
        s = pl.program_id(1)

        @pl.when(s == 0)
        def _():
            acc_ref[...] = jnp.zeros_like(acc_ref)

        acc_ref[...] += lax.dot_general(g_ref[0, 0], a_ref[...].astype(BF16), (((0,), (0,)), ((), ())),
                                        preferred_element_type=F32)

        @pl.when(s == steps - 1)
        def _():
            o_ref[0] = acc_ref[...].astype(BF16)

    return pl.pallas_call(
        body, name="g_w_ffn_in", grid=(N_DEV, steps),
        in_specs=[pl.BlockSpec((tm, D_MODEL), lambda d, s: (s, 0)),
                  pl.BlockSpec((1, 1, tm, FF_CHUNK), lambda d, s: (d % N_CHUNK, d // N_CHUNK, s, 0))],
        out_specs=pl.BlockSpec((1, FF_CHUNK, D_MODEL), lambda d, s: (d, 0, 0)),
        out_shape=jax.ShapeDtypeStruct((N_DEV, FF_CHUNK, D_MODEL), BF16),
        scratch_shapes=[pltpu.VMEM((FF_CHUNK, D_MODEL), F32)],
        compiler_params=_params("parallel", "arbitrary"),
    )(h1, dgu)


def _conv_bwd_dh1(gu, dact, cw, cb, wfi, res, tm=256):
    s_len = gu.shape[2]
    tm = _div(s_len, tm, 8)
    nrow = s_len // tm
    hb = tm // 8

    def dconv_of(conv, sg, up, da):
        return da * up * (sg * (1.0 + conv * (1.0 - sg)))

    def body(gu_ref, gp_ref, gun_ref, da_ref, dan_ref, cw_ref, cb_ref, w_ref, res_ref, dgu_ref, dcw_ref, dh_ref):
        i = pl.program_id(0)
        first = i == 0
        last = i == nrow - 1

        @pl.when(first)
        def _():
            dcw_ref[...] = jnp.zeros_like(dcw_ref)

        row = lax.broadcasted_iota(jnp.int32, (8, 1), 0)
        acc = ALPHA * res_ref[...]
        for c in range(N_CHUNK):
            cw = cw_ref[c]
            cb = cb_ref[c]
            gate = gu_ref[c, 0]
            halo = gp_ref[c, 0]
            g_m1 = _shift_down(gate, 1, halo, first)
            g_m2 = _shift_down(gate, 2, halo, first)
            conv = _conv_act(gate, g_m1, g_m2, cw, cb)
            da = da_ref[c]
            sg = jax.nn.sigmoid(conv)
            dup = (da * conv * sg).astype(BF16)
            dconv = dconv_of(conv, sg, gu_ref[c, 1], da)
            gate_n = gun_ref[c, 0]
            tail = gate[tm - 8:, :]
            conv_n = _conv_act(gate_n, _shift_down(gate_n, 1, tail, False), _shift_down(gate_n, 2, tail, False),
                               cw, cb)
            dconv_n = dconv_of(conv_n, jax.nn.sigmoid(conv_n), gun_ref[c, 1], dan_ref[c])
            dgate = (cw[2:3, :] * dconv + cw[1:2, :] * _shift_up(dconv, 1, dconv_n, last)
                     + cw[0:1, :] * _shift_up(dconv, 2, dconv_n, last)).astype(BF16)
            dgu_ref[c, 0] = dgate
            dgu_ref[c, 1] = dup
            acc = acc + jnp.dot(dgate, w_ref[c], preferred_element_type=F32)
            acc = acc + jnp.dot(dup, w_ref[N_CHUNK + c], preferred_element_type=F32)
            part = jnp.zeros((8, FF_CHUNK), F32)
            for r, term in enumerate((dconv * g_m2, dconv * g_m1, dconv * gate, dconv)):
                part = jnp.where(row == r, jnp.sum(term, axis=0, keepdims=True), part)
            dcw_ref[c] += part
        dh_ref[...] = acc

    nxt = lambda i: jnp.minimum((i + 1) * hb, s_len // 8 - 1)
    main = pl.BlockSpec((N_CHUNK, 2, tm, FF_CHUNK), lambda i: (0, 0, i, 0))
    row_d = pl.BlockSpec((tm, D_MODEL), lambda i: (i, 0))
    return pl.pallas_call(
        body, name="conv_bwd_dh1", grid=(nrow,),
        in_specs=[main,
                  pl.BlockSpec((N_CHUNK, 1, 8, FF_CHUNK), lambda i: (0, 0, jnp.maximum(i * hb - 1, 0), 0)),
                  pl.BlockSpec((N_CHUNK, 2, 8, FF_CHUNK), lambda i: (0, 0, nxt(i), 0)),
                  pl.BlockSpec((N_CHUNK, tm, FF_CHUNK), lambda i: (0, i, 0)),
                  pl.BlockSpec((N_CHUNK, 8, FF_CHUNK), lambda i: (0, nxt(i), 0)),
                  pl.BlockSpec((N_CHUNK, 8, FF_CHUNK), lambda i: (0, 0, 0)),
                  pl.BlockSpec((N_CHUNK, 1, FF_CHUNK), lambda i: (0, 0, 0)),
                  pl.BlockSpec((N_DEV, FF_CHUNK, D_MODEL), lambda i: (0, 0, 0)), row_d],
        out_specs=[main, pl.BlockSpec((N_CHUNK, 8, FF_CHUNK), lambda i: (0, 0, 0)), row_d],
        out_shape=[jax.ShapeDtypeStruct((N_CHUNK, 2, s_len, FF_CHUNK), BF16),
                   jax.ShapeDtypeStruct((N_CHUNK, 8, FF_CHUNK), F32),
                   jax.ShapeDtypeStruct((s_len, D_MODEL), F32)],
        compiler_params=_params("arbitrary"),
    )(gu, gu, gu, dact, dact, cw, cb, wfi, res)


def _sum_devices(r_ref):
    acc = r_ref[0].astype(F32)
    for d in range(1, N_DEV):
        acc = acc + r_ref[d].astype(F32)
    return acc


def _sum8(recv):
    rows = recv.shape[1]
    tr = _div(rows, ROW_BLOCK, 8)

    def body(r_ref, o_ref):
        o_ref[...] = _sum_devices(r_ref)

    return pl.pallas_call(
        body, name="sum8", grid=(rows // tr,),
        in_specs=[pl.BlockSpec((N_DEV, tr, LANE), lambda i: (0, i, 0))],
        out_specs=pl.BlockSpec((tr, LANE), lambda i: (i, 0)),
        out_shape=jax.ShapeDtypeStruct((rows, LANE), F32),
        compiler_params=_params("parallel"),
    )(recv)


def _adamw_math(w, g, m, v):
    m = ADAM_B1 * m + (1.0 - ADAM_B1) * g
    v = ADAM_B2 * v + (1.0 - ADAM_B2) * (g * g)
    m_hat = m / (1.0 - ADAM_B1 ** ADAM_STEP)
    v_hat = v / (1.0 - ADAM_B2 ** ADAM_STEP)
    return -ADAM_LR * (m_hat / (jnp.sqrt(v_hat) + ADAM_EPS) + ADAM_WD * w), m, v


def _adamw_rows(w, g, m, v, name):
    rows = w.shape[0]
    tr = _div(rows, ROW_BLOCK, 8)

    def body(w_ref, g_ref, m_ref, v_ref, d_ref, mo_ref, vo_ref):
        d_ref[...], mo_ref[...], vo_ref[...] = _adamw_math(w_ref[...], g_ref[...], m_ref[...], v_ref[...])

    blk = pl.BlockSpec((tr, LANE), lambda i: (i, 0))
    out = jax.ShapeDtypeStruct((rows, LANE), F32)
    return pl.pallas_call(
        body, name=name, grid=(rows // tr,), in_specs=[blk, blk, blk, blk], out_specs=[blk, blk, blk],
        out_shape=[out, out, out], compiler_params=_params("parallel"),
    )(w, g, m, v)


def _adamw_w_in(recv, w, m, v, tl=128):
    n, depth, d = w.shape

    def body(*refs):
        r_refs, (w_ref, m_ref, v_ref), (g_ref, d_ref, mo_ref, vo_ref) = refs[:depth], refs[depth:depth + 3], refs[-4:]
        for l in range(depth):
            g = _sum_devices(r_refs[l])
            g_ref[:, l, :] = g
            d_ref[:, l, :], mo_ref[:, l, :], vo_ref[:, l, :] = _adamw_math(w_ref[:, l, :], g, m_ref[:, l, :],
                                                                            v_ref[:, l, :])

    blk = pl.BlockSpec((n, depth, tl), lambda j: (0, 0, j))
    out = jax.ShapeDtypeStruct((n, depth, d), F32)
    return pl.pallas_call(
        body, name="adamw_w_in", grid=(d // tl,),
        in_specs=[pl.BlockSpec((N_DEV, n, tl), lambda j: (0, 0, j))] * depth + [blk, blk, blk],
        out_specs=[blk, blk, blk, blk], out_shape=[out, out, out, out],
        compiler_params=_params("parallel"),
    )(*recv, w, m, v)


def _adamw_shard(recv, w, m, v, layer, prev, name):
    _, k, n = recv.shape
    tk = _div(k, 128, 16)

    def body(r_ref, w_ref, m_ref, v_ref, *rest):
        g_ref, d_ref, mo_ref, vo_ref = rest[-4:]
        g = _sum_devices(r_ref)
        g_ref[0] = g
        d_ref[0], mo_ref[0], vo_ref[0] = _adamw_math(w_ref[0], g, m_ref[0], v_ref[0])

    blk = pl.BlockSpec((1, tk, n), lambda i: (layer, i, 0))
    out = jax.ShapeDtypeStruct((DEPTH, k, n), F32)
    carried = [] if prev is None else list(prev)
    return pl.pallas_call(
        body, name=name, grid=(k // tk,),
        in_specs=[pl.BlockSpec((N_DEV, tk, n), lambda i: (0, i, 0)), blk, blk, blk]
        + [pl.BlockSpec(memory_space=pl.ANY)] * len(carried),
        out_specs=[blk, blk, blk, blk], out_shape=[out, out, out, out],
        input_output_aliases={4 + j: j for j in range(len(carried))},
        compiler_params=_params("parallel"),
    )(recv, w, m, v, *carried)


def _to_rows(flat, rows):
    flat = flat.reshape(-1)
    return jnp.pad(flat, (0, rows * LANE - flat.shape[0])).reshape(rows, LANE)


def _pad_z(a, axis):
    f0 = N_QKV
    g0 = N_QKV + FOX_HEADS
    take = lambda lo, hi: lax.slice_in_dim(a, lo, hi, axis=axis)
    shape = list(a.shape)
    shape[axis] = F_PAD - FOX_HEADS
    return jnp.concatenate([take(0, f0), take(g0, N_IN), take(f0, g0), jnp.zeros(shape, a.dtype)], axis=axis)


def _unpad_z(a, axis):
    f0 = N_QKV + N_GATE
    take = lambda lo, hi: lax.slice_in_dim(a, lo, hi, axis=axis)
    return jnp.concatenate([take(0, N_QKV), take(f0, f0 + FOX_HEADS), take(N_QKV, f0)], axis=axis)


def _shards_to_cols(g):
    _, k, n = g.shape
    return g.transpose(1, 0, 2).reshape(k, N_DEV * n)


def _cols_to_shards(full):
    k, n = full.shape
    return full.reshape(k, N_DEV, n // N_DEV).transpose(1, 0, 2)


def _layer_fwd(h, w, p, comm=None, late=None, target=None):
    zq, zg = _z_proj(h, w["w_in_p"], p["b_in_p"])
    qx, kx, vx = _fox_prep(zq, _cumsum_logf(zg))
    attn_a, lse_a = _swa_fwd(zq, p["sinks"])
    attn_b, attn_b32, m_b, l_b, arrived = _fox_fwd(qx, kx, vx, comm)
    if late is not None:
        w, p = late(w, p, arrived)
    h1, u1, merged = _mixer_out(attn_a, attn_b, zg, h, w["w_proj_a"], w["w_proj_b"], w["w_out"],
                                        p["ln_mix_g"], p["ln_mix_b"])
    gu, act = _ffn_in_conv(h1, w["w_ffn_in_fwd"], p["conv_w"], p["conv_b"])
    u2, *h2 = _ffn_out_ln(act, w["w_ffn_out"], h1, p["ln_ffn_g"], p["ln_ffn_b"], target)
    saved = dict(h=h, zq=zq, zg=zg, qx=qx, kx=kx, vx=vx, attn_a=attn_a, lse_a=lse_a, attn_b=attn_b,
                 attn_b32=attn_b32, m_b=m_b, l_b=l_b, h1=h1, u1=u1, merged=merged, gu=gu, act=act, u2=u2)
    return h2, saved, w, p


def _layer_bwd(dh2, sv, w, p, make_comm=None, make_last_comm=None):
    s_len = dh2.shape[0]
    du2, d_ffn_g, d_ffn_b, dact = _ffn_out_bwd(dh2, sv["u2"], p["ln_ffn_g"], w["w_ffn_out"])
    g_ffn_out = _g_w_ffn_out(sv["act"], du2)
    dgu, dcw, dh1 = _conv_bwd_dh1(sv["gu"], dact, p["conv_w"], p["conv_b"], w["w_ffn_in"], du2)
    dcw = dcw.transpose(1, 0, 2).reshape(8, D_FF)
    g_ffn_in = _g_w_ffn_in(sv["h1"], dgu)
    du1, d_mix_g, d_mix_b, dya, dyb, dga, dgb, dattn_a, dattn_b = _mixer_bwd(
        dh1, sv["u1"], p["ln_mix_g"], w["w_out"], sv["attn_a"], sv["attn_b"], sv["zg"], w["w_proj_a"], w["w_proj_b"])
    g_out = _linear_tn(sv["merged"], du1, name="g_w_out", tn=1024)
    g_proj_a = _linear_tn(sv["attn_a"], dya, name="g_w_proj_a", tk=512, tn=1024)
    g_proj_b = _linear_tn(sv["attn_b"], dyb, name="g_w_proj_b", tk=512, tn=1024)
    dq_a, dk_a, dv_a, dsinks = _swa_bwd(sv["zq"], p["sinks"], sv["attn_a"], dattn_a, sv["lse_a"])
    big = dict(w_proj_a=_cols_to_shards(g_proj_a), w_proj_b=_cols_to_shards(g_proj_b),
               w_out=g_out.reshape(N_DEV, D_MODEL // N_DEV, D_MODEL), w_ffn_in=g_ffn_in,
               w_ffn_out=g_ffn_out.reshape(N_DEV, D_FF // N_DEV, D_MODEL))
    dox, stats = _fox_stats(sv["attn_b32"], dattn_b, sv["m_b"], sv["l_b"])
    dq_b, dk_b, dv_b, dcc, arrived = _fox_bwd(sv["qx"], sv["kx"], sv["vx"], dox, stats,
                                              None if make_comm is None else make_comm(big))
    df = _forget_bwd(dcc, sv["zg"])
    dz = jnp.concatenate([dq_a, dk_a.astype(BF16), dv_a.astype(BF16), dq_b.astype(BF16), dk_b, dv_b, dga, dgb, df,
                          jnp.zeros((s_len, F_PAD - LANE), BF16)], axis=1)
    g_in_t, g_b_in = _linear_tn(dz, sv["h"], name="g_w_in", tk=768, tn=1024, colsum=True)
    g_in_t, g_b_in = _unpad_z(g_in_t, 0), _unpad_z(g_b_in, 1)
    big["w_in"] = g_in_t.reshape(N_DEV, N_IN // N_DEV, D_MODEL)
    small = dict(ln_mix_g=d_mix_g, ln_mix_b=d_mix_b, b_in=g_b_in, attn_sinks=dsinks[:, :SWA_HEADS],
                 ln_ffn_g=d_ffn_g, ln_ffn_b=d_ffn_b, conv_w=dcw[:3], conv_b=dcw[3:4])
    dh, arrived_last = _d_h(dz, w["w_in_p"], du1, None if make_last_comm is None else make_last_comm(big, small))
    return dh, big, small, arrived, arrived_last


def _w_in_layouts(w_in):
    return dict(w_in_p=_pad_z(w_in.reshape(N_IN, D_MODEL), 0))


def _other_layouts(w_proj_a, w_proj_b, w_out, w_ffn_in, w_ffn_out):
    return dict(w_proj_a=_shards_to_cols(w_proj_a), w_proj_b=_shards_to_cols(w_proj_b),
                w_out=w_out.reshape(D_MODEL, D_MODEL), w_ffn_in=w_ffn_in,
                w_ffn_in_fwd=w_ffn_in.transpose(0, 2, 1),
                w_ffn_out=w_ffn_out.reshape(N_CHUNK, FF_CHUNK, D_MODEL))


def _layer_params(r):
    return dict(
        b_in_p=_pad_z(r["b_in"].reshape(1, N_IN), 1),
        sinks=jnp.pad(r["attn_sinks"].reshape(1, SWA_HEADS), ((0, 0), (0, LANE - SWA_HEADS))),
        ln_mix_g=r["ln_mix_g"].reshape(1, D_MODEL), ln_mix_b=r["ln_mix_b"].reshape(1, D_MODEL),
        ln_ffn_g=r["ln_ffn_g"].reshape(1, D_MODEL), ln_ffn_b=r["ln_ffn_b"].reshape(1, D_MODEL),
        conv_b=r["conv_b"].reshape(N_CHUNK, 1, FF_CHUNK))


def _conv_w_layout(conv_w):
    return jnp.pad(conv_w, ((0, 5), (0, 0))).reshape(8, N_CHUNK, FF_CHUNK).transpose(1, 0, 2)


def kernel(x, ln_mix_g, ln_mix_b, w_in, b_in, attn_sinks, w_proj_a, w_proj_b, w_out, ln_ffn_g, ln_ffn_b, w_ffn_in, conv_w, conv_b, w_ffn_out, loss_target, m_ln_mix_g, m_ln_mix_b, m_w_in, m_b_in, m_attn_sinks, m_w_proj_a, m_w_proj_b, m_w_out, m_ln_ffn_g, m_ln_ffn_b, m_w_ffn_in, m_conv_w, m_conv_b, m_w_ffn_out, v_ln_mix_g, v_ln_mix_b, v_w_in, v_b_in, v_attn_sinks, v_w_proj_a, v_w_proj_b, v_w_out, v_ln_ffn_g, v_ln_ffn_b, v_w_ffn_in, v_conv_w, v_conv_b, v_w_ffn_out):
    wts = dict(ln_mix_g=ln_mix_g, ln_mix_b=ln_mix_b, w_in=w_in, b_in=b_in, attn_sinks=attn_sinks, w_proj_a=w_proj_a,
               w_proj_b=w_proj_b, w_out=w_out, ln_ffn_g=ln_ffn_g, ln_ffn_b=ln_ffn_b, w_ffn_in=w_ffn_in,
               conv_w=conv_w, conv_b=conv_b, w_ffn_out=w_ffn_out)
    mom = dict(ln_mix_g=m_ln_mix_g, ln_mix_b=m_ln_mix_b, w_in=m_w_in, b_in=m_b_in, attn_sinks=m_attn_sinks,
               w_proj_a=m_w_proj_a, w_proj_b=m_w_proj_b, w_out=m_w_out, ln_ffn_g=m_ln_ffn_g, ln_ffn_b=m_ln_ffn_b,
               w_ffn_in=m_w_ffn_in, conv_w=m_conv_w, conv_b=m_conv_b, w_ffn_out=m_w_ffn_out)
    vel = dict(ln_mix_g=v_ln_mix_g, ln_mix_b=v_ln_mix_b, w_in=v_w_in, b_in=v_b_in, attn_sinks=v_attn_sinks,
               w_proj_a=v_w_proj_a, w_proj_b=v_w_proj_b, w_out=v_w_out, ln_ffn_g=v_ln_ffn_g, ln_ffn_b=v_ln_ffn_b,
               w_ffn_in=v_w_ffn_in, conv_w=v_conv_w, conv_b=v_conv_b, w_ffn_out=v_w_ffn_out)
    names = list(wts)
    big_names = [n for n, _, _ in BIG]
    small_names = [n for n, _ in SMALL]
    me = 4 * lax.axis_index("x") + 2 * lax.axis_index("y") + lax.axis_index("c")
    cw_shard = D_FF // N_DEV

    stored = {"w_in": ((2, 0, 1), (1, 2, 0)), "w_ffn_in": ((0, 2, 1), (0, 2, 1))}
    as_stored = lambda tree: {n: jnp.transpose(tree[n], stored[n][0]) if n in stored else tree[n] for n in big_names}
    w_st, m_st, v_st = as_stored(wts), as_stored(mom), as_stored(vel)
    wb = {n: [(w_st[n][:, l] if n == "w_in" else w_st[n][l]).astype(BF16) for l in range(DEPTH)] for n in big_names}
    ps = [_layer_params(dict(b_in=b_in[l], attn_sinks=attn_sinks[l], ln_mix_g=ln_mix_g[l], ln_mix_b=ln_mix_b[l],
                             ln_ffn_g=ln_ffn_g[l], ln_ffn_b=ln_ffn_b[l], conv_b=conv_b[l])) for l in range(DEPTH)]
    w_in_0 = _gather_two_level(wb["w_in"][0], "gather_w_in_0")
    others = big_names[1:]
    next_layer = {}

    def late_0(w, p, arrived):
        conv_full = arrived[-1].transpose(1, 2, 0, 3).reshape(DEPTH, 3, D_FF)
        next_layer["w"] = _w_in_layouts(arrived[len(others)])
        next_layer["p"] = dict(ps[1], conv_w=_conv_w_layout(conv_full[1]))
        return dict(w, **_other_layouts(*arrived[:len(others)])), dict(p, conv_w=_conv_w_layout(conv_full[0]))

    def late_1(w, p, arrived):
        return dict(w, **_other_layouts(*arrived)), p

    saved, ws = [None] * DEPTH, [None] * DEPTH
    gather_0 = _Comm([(wb[n][0], True) for n in others] + [(wb["w_in"][1], True), (conv_w, True)])
    (h,), saved[0], ws[0], ps[0] = _layer_fwd(x[0], _w_in_layouts(w_in_0), ps[0], gather_0, late_0)
    gather_1 = _Comm([(wb[n][1], True) for n in others])
    (dh, loss_part), saved[1], ws[1], ps[1] = _layer_fwd(h, next_layer["w"], next_layer["p"], gather_1, late_1,
                                                         loss_target[0])

    def small_rows(small):
        vec = jnp.concatenate([small[n].reshape(-1) for n in small_names] + [loss_part[0, 0].reshape(1)])
        return _to_rows(vec, SMALL_LAYER_ROWS)

    dh, big_1, small_1, _, _ = _layer_bwd(dh, saved[1], ws[1], ps[1])

    def exchange_early(big_0):
        return _Comm([(big_1[n].astype(BF16), False) for n in big_names] + [(small_rows(small_1), True)]
                     + [(big_0[n].astype(BF16), False) for n in others])

    def exchange_last(big_0, small_0):
        return _Comm([(big_0["w_in"].astype(BF16), False), (small_rows(small_0), True)])

    grad_x, _, _, arrived, (g_in_0, g_small_0) = _layer_bwd(dh, saved[0], ws[0], ps[0], exchange_early, exchange_last)
    n_big = len(big_names)
    recv = [[g_in_0] + list(arrived[n_big + 1:]) + [g_small_0], list(arrived[:n_big + 1])]

    big_out = {"w_in": _adamw_w_in([recv[l][0] for l in range(DEPTH)], w_st["w_in"], m_st["w_in"], v_st["w_in"])}
    for t, n in enumerate(big_names):
        if n == "w_in":
            continue
        outs = None
        for l in reversed(range(DEPTH)):
            outs = _adamw_shard(recv[l][t], w_st[n], m_st[n], v_st[n], l, outs, "adamw_%s_%d" % (n, l))
        big_out[n] = outs
    for n, (_, back) in stored.items():
        big_out[n] = [jnp.transpose(a, back) for a in big_out[n]]
    small_sum = [_sum8(recv[l][-1]).reshape(-1) for l in range(DEPTH)]
    g_small = {}
    off = 0
    for n, size in SMALL:
        g_small[n] = jnp.stack([small_sum[l][off:off + size] for l in range(DEPTH)])
        off += size
    loss = small_sum[0][off]
    g_small["conv_w"] = lax.dynamic_slice_in_dim(g_small["conv_w"].reshape(DEPTH, 3, D_FF), me * cw_shard, cw_shard,
                                                 axis=2)
    g_small = {n: g_small[n].reshape(wts[n].shape) for n in small_names}

    def pack_small(tree):
        return _to_rows(jnp.concatenate([tree[n].reshape(-1) for n in small_names]), SMALL_ROWS)

    small_out = (pack_small(g_small),) + tuple(_adamw_rows(pack_small(wts), pack_small(g_small), pack_small(mom),
                                                           pack_small(vel), "adamw_small"))

    def result(j):
        out = {n: big_out[n][j] for n in big_names}
        flat = small_out[j].reshape(-1)
        off = 0
        for n in small_names:
            out[n] = flat[off:off + wts[n].size].reshape(wts[n].shape)
            off += wts[n].size
        return [out[n] for n in names]

    return (loss, grad_x[None], *result(0), *result(1), *result(2), *result(3))
```

```python
import jax
import jax.numpy as jnp
import numpy as np
from jax import lax
from jax.experimental import pallas as pl
from jax.experimental.pallas import tpu as pltpu

F32 = jnp.float32
BF16 = jnp.bfloat16
MESH = pl.DeviceIdType.MESH

N_DEV = 8
DEPTH = 2
D_MODEL = 1024
HEAD_DIM = 64
SWA_Q = 512
SWA_KV = 128
FOX_W = 512
FOX_HEADS = 8
SWA_HEADS = 8
D_FF = 2816
N_IN = 4360
N_QKV = SWA_Q + 2 * SWA_KV + 3 * FOX_W
N_GATE = 2 * D_MODEL
F_PAD = 256
N_ZG = N_GATE + F_PAD
N_ZP = N_QKV + N_ZG
LN_EPS = 1e-5
NEG_INF = -1e30
ALPHA = (2 * DEPTH) ** 0.25
SCALE = HEAD_DIM ** -0.5
LOG2E = 1.4426950408889634
SLOPES = tuple(2.0 ** (-8.0 * (h + 1) / SWA_HEADS) for h in range(SWA_HEADS))

ADAM_LR = 0.001
ADAM_B1 = 0.9
ADAM_B2 = 0.999
ADAM_EPS = 1e-08
ADAM_WD = 0.01
ADAM_STEP = 10

LANE = 128
VMEM_LIMIT = 56 * 1024 * 1024

BIG = (("w_in", (D_MODEL, N_IN), 1), ("w_proj_a", (SWA_Q, D_MODEL), 1), ("w_proj_b", (FOX_W, D_MODEL), 1),
       ("w_out", (D_MODEL, D_MODEL), 0), ("w_ffn_in", (D_MODEL, 2 * D_FF), 1), ("w_ffn_out", (D_FF, D_MODEL), 0))
SMALL = (("ln_mix_g", D_MODEL), ("ln_mix_b", D_MODEL), ("b_in", N_IN), ("attn_sinks", SWA_HEADS),
         ("ln_ffn_g", D_MODEL), ("ln_ffn_b", D_MODEL), ("conv_w", 3 * D_FF), ("conv_b", D_FF))
ROW_BLOCK = 512
SMALL_LAYER_ROWS = -(-(sum(n for _, n in SMALL) + 1) // (8 * LANE)) * 8
SMALL_ROWS = ROW_BLOCK
FF_CHUNK = 2 * D_FF // N_DEV
N_CHUNK = D_FF // FF_CHUNK


def _div(n, cap, unit):
    if n <= cap:
        return n
    best = None
    for t in range(unit, cap + 1, unit):
        if n % t == 0:
            best = t
    assert best is not None, (n, cap, unit)
    return best


def _params(*sem):
    return pltpu.CompilerParams(dimension_semantics=sem, vmem_limit_bytes=VMEM_LIMIT)


def _peer(r):
    x, y, c = lax.axis_index("x"), lax.axis_index("y"), lax.axis_index("c")
    px = 1 - x if (r >> 2) & 1 else x
    py = 1 - y if (r >> 1) & 1 else y
    pc = 1 - c if r & 1 else c
    return (px, py, pc), 4 * px + 2 * py + pc


class _Comm:
    def __init__(self, tensors):
        self.arrays = [x for x, _ in tensors]
        self.gathers = [g for _, g in tensors]
        self.n = len(tensors)
        self.out_shape = [jax.ShapeDtypeStruct((N_DEV,) + (x.shape if g else x.shape[1:]), x.dtype)
                          for x, g in tensors]
        self.specs = [pl.BlockSpec(memory_space=pl.ANY)] * self.n
        self.scratch = [pltpu.SemaphoreType.DMA((N_DEV - 1, self.n)), pltpu.SemaphoreType.DMA((N_DEV - 1, self.n)),
                        pltpu.SemaphoreType.DMA((self.n,))]

    def _copies(self, x_refs, out_refs, sems):
        send_sems, recv_sems, local_sems = sems
        _, me = _peer(0)

        def src(t, idx):
            return x_refs[t] if self.gathers[t] else x_refs[t].at[idx]

        def remote(r, t, mine):
            peer, pid = _peer(r)
            return pltpu.make_async_remote_copy(src_ref=src(t, pid), dst_ref=out_refs[t].at[me if mine else pid],
                                                send_sem=send_sems.at[r - 1, t], recv_sem=recv_sems.at[r - 1, t],
                                                device_id=peer, device_id_type=MESH)

        pairs = [(r, t) for r in range(1, N_DEV) for t in range(self.n)]
        local = [pltpu.make_async_copy(src(t, me), out_refs[t].at[me], local_sems.at[t]) for t in range(self.n)]
        return local, [remote(r, t, True) for r, t in pairs], lambda: [remote(r, t, False) for r, t in pairs]

    def start(self, x_refs, out_refs, sems):
        local, sent, _ = self._copies(x_refs, out_refs, sems)
        for cp in local + sent:
            cp.start()

    def wait(self, x_refs, out_refs, sems):
        local, sent, landing = self._copies(x_refs, out_refs, sems)
        for cp in landing():
            cp.wait_recv()
        for cp in sent:
            cp.wait_send()
        for cp in local:
            cp.wait()


def _gather_two_level(x, name):
    def body(x_ref, out_ref, send_sems, recv_sems, local_sem):
        x_, y_, c_ = lax.axis_index("x"), lax.axis_index("y"), lax.axis_index("c")
        me, sibling = (x_, y_, c_), (x_, y_, 1 - c_)
        chips = [(1 - x_, y_), (x_, 1 - y_), (1 - x_, 1 - y_)]

        def slab(px, py, pc):
            return out_ref.at[4 * px + 2 * py + pc]

        def copy(k, block, to, src=None):
            return pltpu.make_async_remote_copy(src_ref=slab(*block) if src is None else src, dst_ref=slab(*block),
                                                send_sem=send_sems.at[k], recv_sem=recv_sems.at[k], device_id=to,
                                                device_id_type=MESH)

        mine = pltpu.make_async_copy(x_ref, slab(*me), local_sem)
        mine.start()
        first = [copy(0, me, sibling, src=x_ref)] + [copy(1 + j, me, (*chip, c_), src=x_ref)
                                                     for j, chip in enumerate(chips)]
        for cp in first:
            cp.start()
        passed = [copy(4 + j, (*chip, c_), sibling) for j, chip in enumerate(chips)]
        for j, chip in enumerate(chips):
            copy(1 + j, (*chip, c_), me).wait_recv()
            passed[j].start()
        copy(0, sibling, me).wait_recv()
        for j, chip in enumerate(chips):
            copy(4 + j, (*chip, 1 - c_), me).wait_recv()
        for cp in first + passed:
            cp.wait_send()
        mine.wait()

    spec = pl.BlockSpec(memory_space=pl.ANY)
    return pl.pallas_call(
        body, name=name, out_shape=jax.ShapeDtypeStruct((N_DEV,) + x.shape, x.dtype), in_specs=[spec], out_specs=spec,
        scratch_shapes=[pltpu.SemaphoreType.DMA((N_DEV - 1,)), pltpu.SemaphoreType.DMA((N_DEV - 1,)),
                        pltpu.SemaphoreType.DMA],
    )(x)


def _with_comm(comm, n_in, n_out, first, last, compute):
    nc = comm.n if comm is not None else 0

    def body(*refs):
        ins, x_refs = refs[:n_in], refs[n_in:n_in + nc]
        outs = refs[n_in + nc:n_in + nc + n_out]
        out_refs = refs[n_in + nc + n_out:n_in + 2 * nc + n_out]
        sems = refs[n_in + 2 * nc + n_out:]
        if nc:
            @pl.when(first())
            def _():
                comm.start(x_refs, out_refs, sems)

        compute(*ins, *outs)
        if nc:
            @pl.when(last())
            def _():
                comm.wait(x_refs, out_refs, sems)

    return body


def _d_h(dz, w_in_t, res, comm=None, tm=512):
    m, k = dz.shape
    d = w_in_t.shape[1]
    tm = _div(m, tm, 8)
    steps = m // tm
    c_specs, c_shapes, c_scratch, c_arrays = _comm_parts(comm)

    def compute(dz_ref, w_ref, res_ref, o_ref):
        o_ref[...] = ALPHA * res_ref[...] + jnp.dot(dz_ref[...], w_ref[...], preferred_element_type=F32)

    body = _with_comm(comm, 3, 1, lambda: pl.program_id(0) == 0, lambda: pl.program_id(0) == steps - 1, compute)
    row = pl.BlockSpec((tm, d), lambda i: (i, 0))
    outs = pl.pallas_call(
        body, name="d_h" if comm is None else "d_h_comm", grid=(steps,),
        in_specs=[pl.BlockSpec((tm, k), lambda i: (i, 0)), pl.BlockSpec((k, d), lambda i: (0, 0)), row] + c_specs,
        out_specs=[row] + c_specs, out_shape=[jax.ShapeDtypeStruct((m, d), F32)] + c_shapes,
        scratch_shapes=c_scratch,
        compiler_params=_params("arbitrary"),
    )(dz, w_in_t, res, *c_arrays)
    return outs[0], outs[1:]


def _z_proj(h, w_in_t, b_p, tm=512):
    m, k = h.shape
    tm = _div(m, tm, 8)
    nt = (((1,), (1,)), ((), ()))

    def body(h_ref, w_ref, b_ref, zq_ref, zg_ref):
        a = h_ref[...].astype(BF16)
        zq = lax.dot_general(a, w_ref[:N_QKV, :], nt, preferred_element_type=F32)
        zq_ref[...] = (zq + b_ref[:, :N_QKV]).astype(BF16)
        zg_ref[...] = lax.dot_general(a, w_ref[N_QKV:, :], nt, preferred_element_type=F32) + b_ref[:, N_QKV:]

    return pl.pallas_call(
        body, name="z_proj", grid=(m // tm,),
        in_specs=[pl.BlockSpec((tm, k), lambda i: (i, 0)), pl.BlockSpec((N_ZP, k), lambda i: (0, 0)),
                  pl.BlockSpec((1, N_ZP), lambda i: (0, 0))],
        out_specs=[pl.BlockSpec((tm, N_QKV), lambda i: (i, 0)), pl.BlockSpec((tm, N_ZG), lambda i: (i, 0))],
        out_shape=[jax.ShapeDtypeStruct((m, N_QKV), BF16), jax.ShapeDtypeStruct((m, N_ZG), F32)],
        compiler_params=_params("parallel"),
    )(h, w_in_t, b_p)


def _linear_tn(a, g, *, name, tk=1024, tn=640, tm=2048, colsum=False):
    m, k = a.shape
    n = g.shape[1]
    tk = _div(k, tk, LANE)
    tn = _div(n, tn, LANE)
    tm = _div(m, tm, 8)
    steps = m // tm
    assert not colsum or tn == n

    def body(a_ref, g_ref, o_ref, *rest):
        acc_ref = rest[-1]
        s = pl.program_id(2)

        @pl.when(s == 0)
        def _():
            acc_ref[...] = jnp.zeros_like(acc_ref)
            if colsum:
                rest[0][...] = jnp.zeros_like(rest[0])

        a_blk = a_ref[...]
        acc_ref[...] += lax.dot_general(a_blk.astype(BF16), g_ref[...].astype(BF16), (((0,), (0,)), ((), ())),
                                        preferred_element_type=F32)
        if colsum:
            rest[0][...] += jnp.sum(a_blk.astype(F32), axis=0, keepdims=True)

        @pl.when(s == steps - 1)
        def _():
            o_ref[...] = acc_ref[...].astype(BF16)

    out_specs = [pl.BlockSpec((tk, tn), lambda i, j, s: (i, j))]
    out_shape = [jax.ShapeDtypeStruct((k, n), BF16)]
    if colsum:
        out_specs.append(pl.BlockSpec((1, tk), lambda i, j, s: (0, i)))
        out_shape.append(jax.ShapeDtypeStruct((1, k), F32))
    outs = pl.pallas_call(
        body, name=name, grid=(k // tk, n // tn, steps),
        in_specs=[pl.BlockSpec((tm, tk), lambda i, j, s: (s, i)), pl.BlockSpec((tm, tn), lambda i, j, s: (s, j))],
        out_specs=out_specs, out_shape=out_shape,
        scratch_shapes=[pltpu.VMEM((tk, tn), F32)],
        compiler_params=_params("parallel", "parallel", "arbitrary"),
    )(a, g)
    return outs if colsum else outs[0]


def _ln(u, g, b):
    mu = jnp.mean(u, axis=-1, keepdims=True)
    d = u - mu
    var = jnp.mean(d * d, axis=-1, keepdims=True)
    return d * lax.rsqrt(var + LN_EPS) * g + b


def _ln_bwd_block(dy, u, g):
    mu = jnp.mean(u, axis=-1, keepdims=True)
    dd = u - mu
    rstd = lax.rsqrt(jnp.mean(dd * dd, axis=-1, keepdims=True) + LN_EPS)
    xhat = dd * rstd
    dxh = dy * g
    m1 = jnp.mean(dxh, axis=-1, keepdims=True)
    m2 = jnp.mean(dxh * xhat, axis=-1, keepdims=True)
    return (rstd * (dxh - m1 - xhat * m2), jnp.sum(dy * xhat, axis=0, keepdims=True),
            jnp.sum(dy, axis=0, keepdims=True))


SCAN_ROWS = 512


def _tri(n, upper):
    r = lax.broadcasted_iota(jnp.int32, (n, n), 0)
    c = lax.broadcasted_iota(jnp.int32, (n, n), 1)
    return jnp.where((c >= r) if upper else (c <= r), 1.0, 0.0).astype(F32)


def _cumsum_logf(zg):
    s = zg.shape[0]
    t = _div(s, SCAN_ROWS, LANE)
    nb = s // t
    fcol = N_GATE // LANE

    def body(f_ref, c_ref, carry_ref):
        @pl.when(pl.program_id(0) == 0)
        def _():
            carry_ref[...] = jnp.zeros_like(carry_ref)

        f = f_ref[...]
        logf = jnp.minimum(f, 0.0) - jnp.log(1.0 + jnp.exp(-jnp.abs(f)))
        c = jnp.dot(_tri(t, False), logf, precision=lax.Precision.HIGHEST, preferred_element_type=F32)
        c = c + carry_ref[0:1, :]
        c_ref[...] = c
        carry_ref[...] = jnp.broadcast_to(c[t - 1:t, :], carry_ref.shape)

    return pl.pallas_call(
        body, name="cumsum_logf", grid=(nb,),
        in_specs=[pl.BlockSpec((t, LANE), lambda i: (i, fcol))],
        out_specs=pl.BlockSpec((t, LANE), lambda i: (i, 0)),
        out_shape=jax.ShapeDtypeStruct((s, LANE), F32),
        scratch_shapes=[pltpu.VMEM((8, LANE), F32)],
        compiler_params=_params("arbitrary"),
    )(zg)


def _forget_bwd(dcc, zg):
    s = zg.shape[0]
    t = _div(s, SCAN_ROWS, LANE)
    nb = s // t
    fcol = N_GATE // LANE

    def body(dc_ref, f_ref, o_ref, carry_ref):
        @pl.when(pl.program_id(0) == 0)
        def _():
            carry_ref[...] = jnp.zeros_like(carry_ref)

        lane = lax.broadcasted_iota(jnp.int32, (1, LANE), 1)
        dc = jnp.zeros((t, LANE), F32)
        for p in range(FOX_HEADS // 2):
            tile = dc_ref[:, p * LANE:(p + 1) * LANE]
            moved = pltpu.roll(tile, 2 * p, 1) if p else tile
            dc = jnp.where((lane == 2 * p) | (lane == 2 * p + 1), moved, dc)
        dlogf = jnp.dot(_tri(t, True), dc, precision=lax.Precision.HIGHEST, preferred_element_type=F32)
        dlogf = dlogf + carry_ref[0:1, :]
        o_ref[...] = (dlogf * jax.nn.sigmoid(-f_ref[...])).astype(BF16)
        carry_ref[...] = jnp.broadcast_to(dlogf[0:1, :], carry_ref.shape)

    return pl.pallas_call(
        body, name="forget_bwd", grid=(nb,),
        in_specs=[pl.BlockSpec((t, FOX_W), lambda i: (nb - 1 - i, 0)),
                  pl.BlockSpec((t, LANE), lambda i: (nb - 1 - i, fcol))],
        out_specs=pl.BlockSpec((t, LANE), lambda i: (nb - 1 - i, 0)),
        out_shape=jax.ShapeDtypeStruct((s, LANE), BF16),
        scratch_shapes=[pltpu.VMEM((8, LANE), F32)],
        compiler_params=_params("arbitrary"),
    )(dcc, zg)


KA_COL = SWA_Q // LANE
VA_COL = KA_COL + 1


def _half_masks():
    lane = lax.broadcasted_iota(jnp.int32, (1, LANE), 1)
    hi = lane >= HEAD_DIM
    return (jnp.logical_not(hi), hi)


def _both_halves(x, sel):
    xs = jnp.where(sel, x, 0.0)
    return xs + pltpu.roll(xs, HEAD_DIM, 1)


SWA_PER_KV = 4
WIDE = SWA_PER_KV * LANE


def _swa_bias():
    k = np.arange(2 * LANE)[:, None]
    q = np.arange(LANE)[None, :]
    dist = (q + LANE - k).astype(np.float32)
    valid = (dist >= 0) & (dist < LANE)
    per_head = [np.where(valid, np.float32(-s) * dist, np.float32(NEG_INF)) for s in SLOPES]
    return jnp.asarray(np.stack([np.concatenate(per_head[SWA_PER_KV * hk:SWA_PER_KV * (hk + 1)], axis=1)
                                 for hk in range(2)]), F32)


def _no_previous_block(i_blk):
    k = lax.broadcasted_iota(jnp.int32, (2 * LANE, WIDE), 0)
    return jnp.where((i_blk == 0) & (k < LANE), NEG_INF, 0.0)


def _stack_heads(ref, blk, hk, halves, scale):
    tiles = []
    for j in range(SWA_PER_KV):
        p = 2 * hk + j // 2
        t = ref[blk, p * LANE:(p + 1) * LANE]
        if scale:
            t = _scaled(t)
        tiles.append(jnp.where(halves[j % 2], t, jnp.zeros_like(t)))
    return jnp.concatenate(tiles, axis=0)


def _pair_tile(wide, pp, row_halves):
    a = wide[:, (2 * pp) * LANE:(2 * pp + 1) * LANE]
    b = wide[:, (2 * pp + 1) * LANE:(2 * pp + 2) * LANE]
    return jnp.where(row_halves[0], a, b).T


def _lane_blocks(rows8, hk):
    return jnp.concatenate([rows8[SWA_PER_KV * hk + j:SWA_PER_KV * hk + j + 1, :] for j in range(SWA_PER_KV)], axis=1)


def _row_halves():
    hi = lax.broadcasted_iota(jnp.int32, (LANE, 1), 0) >= HEAD_DIM
    return (jnp.logical_not(hi), hi)


NT = (((1,), (1,)), ((), ()))


def _scaled(q):
    return (q.astype(F32) * SCALE).astype(BF16)


SWA_GROUP = 4


def _swa_group(s_len):
    return SWA_GROUP if (s_len // LANE) % SWA_GROUP == 0 else 1


def _swa_specs(group):
    rows = group * LANE
    prev = lambda i: jnp.maximum(i * group - 1, 0)
    return [pl.BlockSpec((rows, SWA_Q), lambda i: (i, 0)),
            pl.BlockSpec((rows, LANE), lambda i: (i, KA_COL)), pl.BlockSpec((rows, LANE), lambda i: (i, VA_COL)),
            pl.BlockSpec((LANE, LANE), lambda i: (prev(i), KA_COL)),
            pl.BlockSpec((LANE, LANE), lambda i: (prev(i), VA_COL))]


def _swa_window(g, cur_ref, prev_ref):
    before = prev_ref[...] if g == 0 else cur_ref[(g - 1) * LANE:g * LANE, :]
    return jnp.concatenate([before, cur_ref[g * LANE:(g + 1) * LANE, :]], axis=0).astype(F32)


def _swa_fwd(zq, sinks):
    s_len = zq.shape[0]
    group = _swa_group(s_len)
    rows = group * LANE
    sink_lanes = jnp.repeat(sinks[:, :SWA_HEADS], LANE, axis=1)

    def body(q_ref, kc_ref, vc_ref, kp_ref, vp_ref, sink_ref, bias_ref, o_ref, lse_ref):
        halves = _half_masks()
        row_halves = _row_halves()
        for g in range(group):
            blk = slice(g * LANE, (g + 1) * LANE)
            kcat = _swa_window(g, kc_ref, kp_ref)
            vcat = _swa_window(g, vc_ref, vp_ref)
            lse_rows = []
            for hk in range(2):
                kb = _both_halves(kcat, halves[hk]).astype(BF16)
                v_t = _both_halves(vcat, halves[hk]).T.astype(BF16)
                q4 = _stack_heads(q_ref, blk, hk, halves, True)
                s_t = lax.dot_general(kb, q4, NT, preferred_element_type=F32) + bias_ref[hk]
                if g == 0:
                    s_t = s_t + _no_previous_block(pl.program_id(0))
                sink = sink_ref[:, hk * WIDE:(hk + 1) * WIDE]
                m = jnp.maximum(jnp.max(s_t, axis=0, keepdims=True), sink)
                pe = jnp.exp(s_t - m)
                den = jnp.sum(pe, axis=0, keepdims=True) + jnp.exp(sink - m)
                out_t = jnp.dot(v_t, (pe * (1.0 / den)).astype(BF16), preferred_element_type=F32)
                for pp in range(2):
                    p = 2 * hk + pp
                    o_ref[blk, p * LANE:(p + 1) * LANE] = _pair_tile(out_t, pp, row_halves).astype(BF16)
                lse4 = m + jnp.log(den)
                lse_rows += [lse4[:, j * LANE:(j + 1) * LANE] for j in range(SWA_PER_KV)]
            lse_ref[:, blk] = jnp.concatenate(lse_rows, axis=0)

    return pl.pallas_call(
        body, name="swa_fwd", grid=(s_len // rows,),
        in_specs=_swa_specs(group) + [pl.BlockSpec((1, SWA_HEADS * LANE), lambda i: (0, 0)),
                                      pl.BlockSpec((2, 2 * LANE, WIDE), lambda i: (0, 0, 0))],
        out_specs=[pl.BlockSpec((rows, SWA_Q), lambda i: (i, 0)), pl.BlockSpec((SWA_HEADS, rows), lambda i: (0, i))],
        out_shape=[jax.ShapeDtypeStruct((s_len, SWA_Q), BF16), jax.ShapeDtypeStruct((SWA_HEADS, s_len), F32)],
        compiler_params=_params("parallel"),
    )(zq, zq, zq, zq, zq, sink_lanes, _swa_bias())


def _swa_bwd(zq, sinks, o, do, lse):
    s_len = zq.shape[0]
    group = _swa_group(s_len)
    rows = group * LANE

    def body(q_ref, kc_ref, vc_ref, kp_ref, vp_ref, sink_ref, bias_ref, o_ref, do_ref, lse_ref,
             dq_ref, dk_ref, dv_ref, ds_ref):
        halves = _half_masks()
        row_halves = _row_halves()
        lane = lax.broadcasted_iota(jnp.int32, (1, LANE), 1)
        dsink = jnp.zeros((1, LANE), F32)
        for g in range(group):
            blk = slice(g * LANE, (g + 1) * LANE)
            i_blk = pl.program_id(0) * group + g
            kcat = _swa_window(g, kc_ref, kp_ref)
            vcat = _swa_window(g, vc_ref, vp_ref)
            lse_rows = lse_ref[:, blk]
            prod = do_ref[blk, :].astype(F32) * o_ref[blk, :].astype(F32)
            select = (lax.broadcasted_iota(jnp.int32, (SWA_HEADS, SWA_Q), 1) // HEAD_DIM
                      == lax.broadcasted_iota(jnp.int32, (SWA_HEADS, SWA_Q), 0))
            picks = jnp.where(select, 1.0, 0.0).astype(BF16)
            prod_hi = prod.astype(BF16)
            prod_lo = (prod - prod_hi.astype(F32)).astype(BF16)
            delta_rows = (lax.dot_general(picks, prod_hi, NT, preferred_element_type=F32)
                          + lax.dot_general(picks, prod_lo, NT, preferred_element_type=F32))
            dk_tot = jnp.zeros((2 * LANE, LANE), F32)
            dv_tot = jnp.zeros((2 * LANE, LANE), F32)
            for hk in range(2):
                kb = _both_halves(kcat, halves[hk])
                k_t = kb.T.astype(BF16)
                kb = kb.astype(BF16)
                vb = _both_halves(vcat, halves[hk]).astype(BF16)
                q4 = _stack_heads(q_ref, blk, hk, halves, True)
                do4 = _stack_heads(do_ref, blk, hk, halves, False)
                lse4 = _lane_blocks(lse_rows, hk)
                delta4 = _lane_blocks(delta_rows, hk)
                s_t = lax.dot_general(kb, q4, NT, preferred_element_type=F32) + bias_ref[hk]
                if g == 0:
                    s_t = s_t + _no_previous_block(pl.program_id(0))
                p_t = jnp.exp(s_t - lse4)
                dp_t = lax.dot_general(vb, do4, NT, preferred_element_type=F32)
                ds_t = (p_t * (dp_t - delta4)).astype(BF16)
                sink_part = jnp.exp(sink_ref[:, hk * WIDE:(hk + 1) * WIDE] - lse4) * delta4
                for j in range(SWA_PER_KV):
                    dsink_h = -jnp.sum(sink_part[:, j * LANE:(j + 1) * LANE], axis=1, keepdims=True)
                    dsink = dsink + jnp.where(lane == SWA_PER_KV * hk + j, dsink_h, 0.0)
                dq_t = jnp.dot(k_t, ds_t, preferred_element_type=F32)
                for pp in range(2):
                    p = 2 * hk + pp
                    dq_ref[blk, p * LANE:(p + 1) * LANE] = (_pair_tile(dq_t, pp, row_halves) * SCALE).astype(BF16)
                dk_acc = jnp.dot(ds_t, q4, preferred_element_type=F32)
                dv_acc = jnp.dot(p_t.astype(BF16), do4, preferred_element_type=F32)
                dk_tot = dk_tot + jnp.where(halves[hk], dk_acc + pltpu.roll(dk_acc, HEAD_DIM, 1), 0.0)
                dv_tot = dv_tot + jnp.where(halves[hk], dv_acc + pltpu.roll(dv_acc, HEAD_DIM, 1), 0.0)
            cur = pl.ds(pl.multiple_of(i_blk * LANE, LANE), LANE)
            dk_ref[cur, :] = dk_tot[LANE:, :]
            dv_ref[cur, :] = dv_tot[LANE:, :]

            def add_previous(i_blk=i_blk, dk_tot=dk_tot, dv_tot=dv_tot):
                prv = pl.ds(pl.multiple_of((i_blk - 1) * LANE, LANE), LANE)
                dk_ref[prv, :] += dk_tot[:LANE, :]
                dv_ref[prv, :] += dv_tot[:LANE, :]

            if g == 0:
                pl.when(i_blk > 0)(add_previous)
            else:
                add_previous()

        @pl.when(pl.program_id(0) == 0)
        def _():
            ds_ref[...] = jnp.zeros_like(ds_ref)

        ds_ref[...] += dsink

    blk512 = pl.BlockSpec((rows, SWA_Q), lambda i: (i, 0))
    full = pl.BlockSpec((s_len, LANE), lambda i: (0, 0))
    vec = pl.BlockSpec((1, LANE), lambda i: (0, 0))
    return pl.pallas_call(
        body, name="swa_bwd", grid=(s_len // rows,),
        in_specs=_swa_specs(group) + [pl.BlockSpec((1, SWA_HEADS * LANE), lambda i: (0, 0)),
                                      pl.BlockSpec((2, 2 * LANE, WIDE), lambda i: (0, 0, 0)), blk512, blk512,
                                      pl.BlockSpec((SWA_HEADS, rows), lambda i: (0, i))],
        out_specs=[blk512, full, full, vec],
        out_shape=[jax.ShapeDtypeStruct((s_len, SWA_Q), BF16), jax.ShapeDtypeStruct((s_len, LANE), F32),
                   jax.ShapeDtypeStruct((s_len, LANE), F32), jax.ShapeDtypeStruct((1, LANE), F32)],
        compiler_params=_params("arbitrary"),
    )(zq, zq, zq, zq, zq, jnp.repeat(sinks[:, :SWA_HEADS], LANE, axis=1), _swa_bias(), o, do, lse)


QB_COL = (SWA_Q + 2 * SWA_KV) // LANE
KB_COL = QB_COL + FOX_W // LANE
VB_COL = KB_COL + FOX_W // LANE
N_PAIR = FOX_HEADS // 2


def _causal(t, keys_first=False):
    r = lax.broadcasted_iota(jnp.int32, (t, t), 0)
    c = lax.broadcasted_iota(jnp.int32, (t, t), 1)
    return c >= r if keys_first else r >= c


N_SPLIT = 3


def _own_half(e):
    hi = lax.broadcasted_iota(jnp.int32, (1, LANE), 1) >= HEAD_DIM
    return hi if e else jnp.logical_not(hi)


def _feature_lane(e, t):
    return HEAD_DIM * (1 - e) + t


def _feature_tables():
    wide = FOX_HEADS * LANE
    place_q, place_k = np.zeros((N_SPLIT * LANE, wide), np.float32), np.zeros((N_SPLIT * LANE, wide), np.float32)
    ones_q, ones_k, ones_v, own = (np.zeros((1, wide), np.float32) for _ in range(4))
    for h in range(FOX_HEADS):
        e = h % 2
        own[0, h * LANE + HEAD_DIM * e:h * LANE + HEAD_DIM * (e + 1)] = 1.0
        ones_v[0, h * LANE + _feature_lane(e, 0)] = 1.0
        for t in range(N_SPLIT):
            place_q[t * LANE + h, h * LANE + _feature_lane(e, t)] = 1.0
            ones_q[0, h * LANE + _feature_lane(e, N_SPLIT + t)] = 1.0
            ones_k[0, h * LANE + _feature_lane(e, t)] = 1.0
            place_k[t * LANE + h, h * LANE + _feature_lane(e, N_SPLIT + t)] = -1.0
    return tuple(jnp.asarray(a) for a in (place_q, place_k, ones_q, ones_k, ones_v, own))


def _fox_prep(zq, c, tm=256):
    s_len = zq.shape[0]
    tm = _div(s_len, tm, 8)
    wide = FOX_HEADS * LANE

    def body(z_ref, c_ref, pq_ref, pk_ref, oq_ref, ok_ref, ov_ref, own_ref, qx_ref, kx_ref, vx_ref):
        rest = c_ref[...]
        parts = []
        for _ in range(N_SPLIT):
            part = rest.astype(BF16).astype(F32)
            rest = rest - part
            parts.append(part)
        parts = jnp.concatenate(parts, axis=1)
        qf = jnp.dot(parts, pq_ref[...], preferred_element_type=F32) + oq_ref[...]
        kf = jnp.dot(parts, pk_ref[...], preferred_element_type=F32) + ok_ref[...]
        own = own_ref[...] > 0.5
        for p in range(N_PAIR):
            cols = slice(2 * p * LANE, (2 * p + 2) * LANE)
            pair = lambda col: jnp.tile(z_ref[:, (col + p) * LANE:(col + p + 1) * LANE].astype(F32), (1, 2))
            qx_ref[:, cols] = jnp.where(own[:, cols], pair(QB_COL) * SCALE, qf[:, cols]).astype(BF16)
            kx_ref[:, cols] = jnp.where(own[:, cols], pair(KB_COL), kf[:, cols]).astype(BF16)
            vx_ref[:, cols] = jnp.where(own[:, cols], pair(VB_COL), ov_ref[:, cols]).astype(BF16)

    out = jax.ShapeDtypeStruct((s_len, wide), BF16)
    blk = pl.BlockSpec((tm, wide), lambda i: (i, 0))
    table = pl.BlockSpec((N_SPLIT * LANE, wide), lambda i: (0, 0))
    vec = pl.BlockSpec((1, wide), lambda i: (0, 0))
    return pl.pallas_call(
        body, name="fox_prep", grid=(s_len // tm,),
        in_specs=[pl.BlockSpec((tm, N_QKV), lambda i: (i, 0)), pl.BlockSpec((tm, LANE), lambda i: (i, 0)),
                  table, table, vec, vec, vec, vec],
        out_specs=[blk, blk, blk], out_shape=[out, out, out],
        compiler_params=_params("parallel"),
    )(zq, c, *_feature_tables())


def _comm_parts(comm):
    return ([], [], [], []) if comm is None else (comm.specs, comm.out_shape, comm.scratch, comm.arrays)


def _fox_fwd(qx, kx, vx, comm=None, t_cap=1024):
    s_len = qx.shape[0]
    t = _div(s_len, t_cap, LANE)
    nq = s_len // t
    c_specs, c_shapes, c_scratch, c_arrays = _comm_parts(comm)

    def compute(q_ref, k_ref, v_ref, o_ref, o32_ref, m_ref, l_ref):
        i = pl.program_id(1)
        qs = [q_ref[:, e * LANE:(e + 1) * LANE] for e in range(2)]

        def step(j, carry, diag):
            rows = pl.ds(pl.multiple_of(j * t, t), t)
            new = []
            for e in range(2):
                m, acc = carry[e]
                s2 = lax.dot_general(qs[e], k_ref[rows, e * LANE:(e + 1) * LANE], NT,
                                     preferred_element_type=F32) * LOG2E
                if diag:
                    s2 = jnp.where(_causal(t), s2, NEG_INF)
                mn = jnp.maximum(m, jnp.ceil(jnp.max(s2, axis=1, keepdims=True)))
                pe = jnp.exp2(s2 - mn).astype(BF16)
                acc = acc * jnp.exp2(m - mn) + jnp.dot(pe, v_ref[rows, e * LANE:(e + 1) * LANE],
                                                       preferred_element_type=F32)
                new.append((mn, acc))
            return tuple(new)

        init = (jnp.full((t, 1), NEG_INF, F32), jnp.zeros((t, LANE), F32))
        carry = lax.fori_loop(0, i, lambda j, c: step(j, c, False), (init, init))
        carry = step(i, carry, True)
        outs, ls = [], []
        for e in range(2):
            m, acc = carry[e]
            l = acc[:, _feature_lane(e, 0):_feature_lane(e, 0) + 1]
            outs.append(acc / l)
            ls.append(l)
        out = jnp.where(_own_half(1), outs[1], outs[0])
        o_ref[...] = out.astype(BF16)
        o32_ref[...] = out
        m_ref[...] = jnp.where(_own_half(1), carry[1][0], carry[0][0])
        l_ref[...] = jnp.where(_own_half(1), ls[1], ls[0])

    body = _with_comm(comm, 3, 4, lambda: (pl.program_id(0) == 0) & (pl.program_id(1) == 0),
                      lambda: (pl.program_id(0) == N_PAIR - 1) & (pl.program_id(1) == nq - 1), compute)
    pair = pl.BlockSpec((s_len, 2 * LANE), lambda p, i: (0, p))
    tile = pl.BlockSpec((t, LANE), lambda p, i: (i, p))
    wide = jax.ShapeDtypeStruct((s_len, FOX_W), F32)
    outs = pl.pallas_call(
        body, name="fox_fwd" if comm is None else "fox_fwd_comm%d" % comm.n, grid=(N_PAIR, nq),
        in_specs=[pl.BlockSpec((t, 2 * LANE), lambda p, i: (i, p)), pair, pair] + c_specs,
        out_specs=[tile, tile, tile, tile] + c_specs,
        out_shape=[jax.ShapeDtypeStruct((s_len, FOX_W), BF16), wide, wide, wide] + c_shapes,
        scratch_shapes=c_scratch,
        compiler_params=_params("arbitrary", "arbitrary"),
    )(qx, kx, vx, *c_arrays)
    return outs[0], outs[1], outs[2], outs[3], outs[4:]


def _fox_stats(o, do, m, l, tm=256):
    s_len = o.shape[0]
    tm = _div(s_len, tm, LANE)

    def body(o_ref, do_ref, m_ref, l_ref, dox_ref, st_ref):
        lane = lax.broadcasted_iota(jnp.int32, (1, LANE), 1)
        for p in range(N_PAIR):
            cols = slice(p * LANE, (p + 1) * LANE)
            dout = do_ref[:, cols]
            prod = o_ref[:, cols] * dout.astype(F32)
            shift = m_ref[:, cols]
            inv_l = 1.0 / l_ref[:, cols]
            st = jnp.zeros((tm, LANE), F32)
            for e in range(2):
                h = 2 * p + e
                dox_ref[:, h * LANE:(h + 1) * LANE] = jnp.where(_own_half(e), dout, jnp.zeros_like(dout))
                st = jnp.where(lane == e, shift[:, e * HEAD_DIM:e * HEAD_DIM + 1], st)
                delta = jnp.sum(jnp.where(_own_half(e), prod, 0.0), axis=1, keepdims=True)
                st = jnp.where(lane == 2 + e, delta, st)
                st = jnp.where(lane == 4 + e, inv_l[:, e * HEAD_DIM:e * HEAD_DIM + 1], st)
            st_ref[p] = st.T[:8, :]

    row = pl.BlockSpec((tm, FOX_W), lambda i: (i, 0))
    return pl.pallas_call(
        body, name="fox_stats", grid=(s_len // tm,), in_specs=[row, row, row, row],
        out_specs=[pl.BlockSpec((tm, FOX_HEADS * LANE), lambda i: (i, 0)),
                   pl.BlockSpec((N_PAIR, 8, tm), lambda i: (0, 0, i))],
        out_shape=[jax.ShapeDtypeStruct((s_len, FOX_HEADS * LANE), BF16),
                   jax.ShapeDtypeStruct((N_PAIR, 8, s_len), F32)],
        compiler_params=_params("parallel"),
    )(o, do, m, l)


def _fox_bwd(qx, kx, vx, dox, stats, comm=None, t_cap=1024):
    s_len = qx.shape[0]
    t = _div(s_len, t_cap, LANE)
    n = s_len // t
    c_specs, c_shapes, c_scratch, c_arrays = _comm_parts(comm)

    def compute(q_ref, do_ref, st_ref, k_ref, v_ref, dq_ref, dk_ref, dv_ref, dc_ref):
        j = pl.program_id(1)
        lane = lax.broadcasted_iota(jnp.int32, (1, LANE), 1)

        @pl.when(j == 0)
        def _():
            dq_ref[...] = jnp.zeros_like(dq_ref)

        ks = [k_ref[:, e * LANE:(e + 1) * LANE] for e in range(2)]
        vs = [v_ref[:, e * LANE:(e + 1) * LANE] for e in range(2)]
        ks_t = [k.astype(F32).T.astype(BF16) for k in ks]

        def step(i, carry, diag):
            rows = pl.ds(pl.multiple_of(i * t, t), t)
            new = []
            dq = jnp.zeros((LANE, t), F32)
            for e in range(2):
                dk, dv, dc = carry[e]
                q = q_ref[rows, e * LANE:(e + 1) * LANE]
                dout = do_ref[rows, e * LANE:(e + 1) * LANE]
                s_t = lax.dot_general(ks[e], q, NT, preferred_element_type=F32) * LOG2E
                if diag:
                    s_t = jnp.where(_causal(t, keys_first=True), s_t, NEG_INF)
                p_t = jnp.exp2(s_t - st_ref[0, e:e + 1, rows]).astype(BF16).astype(F32) * st_ref[0, 4 + e:5 + e, rows]
                dp_t = lax.dot_general(vs[e], dout, NT, preferred_element_type=F32)
                ds_f = p_t * (dp_t - st_ref[0, 2 + e:3 + e, rows])
                ds_t = ds_f.astype(BF16)
                dc = dc + jnp.sum(ds_f, axis=1, keepdims=True)
                dv = dv + jnp.dot(p_t.astype(BF16), dout, preferred_element_type=F32)
                dk = dk + jnp.dot(ds_t, q, preferred_element_type=F32)
                dq_e = jnp.dot(ks_t[e], ds_t, preferred_element_type=F32)
                dq = dq + jnp.where(_row_halves()[e], dq_e, 0.0)
                new.append((dk, dv, dc))
            dq_ref[rows, :] += dq.T * SCALE
            return tuple(new)

        zero = jnp.zeros((t, LANE), F32)
        init = (zero, zero, jnp.zeros((t, 1), F32))
        carry = step(j, (init, init), True)
        (dk0, dv0, dc0), (dk1, dv1, dc1) = lax.fori_loop(j + 1, n, lambda i, c: step(i, c, False), carry)
        dk_ref[...] = jnp.where(_own_half(1), dk1, dk0).astype(BF16)
        dv_ref[...] = jnp.where(_own_half(1), dv1, dv0).astype(BF16)
        dc_ref[...] = jnp.where(lane == 0, -dc0, jnp.where(lane == 1, -dc1, 0.0))

    body = _with_comm(comm, 5, 4, lambda: (pl.program_id(0) == 0) & (pl.program_id(1) == 0),
                      lambda: (pl.program_id(0) == N_PAIR - 1) & (pl.program_id(1) == n - 1), compute)
    pair = pl.BlockSpec((s_len, 2 * LANE), lambda p, j: (0, p))
    blk = pl.BlockSpec((t, 2 * LANE), lambda p, j: (j, p))
    tile = pl.BlockSpec((t, LANE), lambda p, j: (j, p))
    outs = pl.pallas_call(
        body, name="fox_bwd" if comm is None else "fox_bwd_comm", grid=(N_PAIR, n),
        in_specs=[pair, pair, pl.BlockSpec((1, 8, s_len), lambda p, j: (p, 0, 0)), blk, blk] + c_specs,
        out_specs=[pl.BlockSpec((s_len, LANE), lambda p, j: (0, p)), tile, tile, tile] + c_specs,
        out_shape=[jax.ShapeDtypeStruct((s_len, FOX_W), F32), jax.ShapeDtypeStruct((s_len, FOX_W), BF16),
                   jax.ShapeDtypeStruct((s_len, FOX_W), BF16), jax.ShapeDtypeStruct((s_len, FOX_W), F32)] + c_shapes,
        scratch_shapes=c_scratch,
        compiler_params=_params("arbitrary", "arbitrary"),
    )(qx, dox, stats, kx, vx, *c_arrays)
    return outs[0], outs[1], outs[2], outs[3], outs[4:]


def _mixer_out(attn_a, attn_b, zg, h, wpa, wpb, wout, g, b, tm=512):
    m = h.shape[0]
    tm = _div(m, tm, 8)

    def body(a_ref, b_ref, ga_ref, gb_ref, h_ref, wpa_ref, wpb_ref, wout_ref, g_ref, bb_ref, h1_ref, u_ref, mg_ref):
        ya = jnp.dot(a_ref[...], wpa_ref[...], preferred_element_type=F32)
        yb = jnp.dot(b_ref[...], wpb_ref[...], preferred_element_type=F32)
        merged = (jax.nn.sigmoid(ga_ref[...]) * ya + jax.nn.sigmoid(gb_ref[...]) * yb).astype(BF16)
        u = ALPHA * h_ref[...] + jnp.dot(merged, wout_ref[...], preferred_element_type=F32)
        u_ref[...] = u
        h1_ref[...] = _ln(u, g_ref[...], bb_ref[...])
        mg_ref[...] = merged

    row = pl.BlockSpec((tm, D_MODEL), lambda i: (i, 0))
    att = pl.BlockSpec((tm, SWA_Q), lambda i: (i, 0))
    vec = pl.BlockSpec((1, D_MODEL), lambda i: (0, 0))
    wsm = pl.BlockSpec((SWA_Q, D_MODEL), lambda i: (0, 0))
    return pl.pallas_call(
        body, name="mixer_out", grid=(m // tm,),
        in_specs=[att, att, row, pl.BlockSpec((tm, D_MODEL), lambda i: (i, 1)), row, wsm, wsm,
                  pl.BlockSpec((D_MODEL, D_MODEL), lambda i: (0, 0)), vec, vec],
        out_specs=[row, row, row],
        out_shape=[jax.ShapeDtypeStruct((m, D_MODEL), F32), jax.ShapeDtypeStruct((m, D_MODEL), F32),
                   jax.ShapeDtypeStruct((m, D_MODEL), BF16)],
        compiler_params=_params("parallel"),
    )(attn_a, attn_b, zg, zg, h, wpa, wpb, wout, g, b)


def _mixer_bwd(dh1, u1, g, wout, attn_a, attn_b, zg, wpa, wpb, tm=512):
    m = dh1.shape[0]
    tm = _div(m, tm, 8)

    def body(dh_ref, u_ref, g_ref, wout_ref, a_ref, b_ref, ga_ref, gb_ref, wpa_ref, wpb_ref,
             du_ref, dg_ref, db_ref, dya_ref, dyb_ref, dga_ref, dgb_ref, da_ref, dbb_ref):
        @pl.when(pl.program_id(0) == 0)
        def _():
            dg_ref[...] = jnp.zeros_like(dg_ref)
            db_ref[...] = jnp.zeros_like(db_ref)

        du, dg, db = _ln_bwd_block(dh_ref[...], u_ref[...], g_ref[...])
        du_ref[...] = du
        dg_ref[...] += dg
        db_ref[...] += db
        dm = lax.dot_general(du.astype(BF16), wout_ref[...], (((1,), (1,)), ((), ())), preferred_element_type=F32)
        for x_ref, gate_ref, w_ref, dy_ref, dgate_ref, dattn_ref in (
                (a_ref, ga_ref, wpa_ref, dya_ref, dga_ref, da_ref), (b_ref, gb_ref, wpb_ref, dyb_ref, dgb_ref, dbb_ref)):
            sg = jax.nn.sigmoid(gate_ref[...])
            dy = (dm * sg).astype(BF16)
            dy_ref[...] = dy
            y = jnp.dot(x_ref[...], w_ref[...], preferred_element_type=F32)
            dgate_ref[...] = (dm * y * sg * (1.0 - sg)).astype(BF16)
            dattn_ref[...] = lax.dot_general(dy, w_ref[...], (((1,), (1,)), ((), ())),
                                             preferred_element_type=F32).astype(BF16)

    row = pl.BlockSpec((tm, D_MODEL), lambda i: (i, 0))
    att = pl.BlockSpec((tm, SWA_Q), lambda i: (i, 0))
    vec = pl.BlockSpec((1, D_MODEL), lambda i: (0, 0))
    wsm = pl.BlockSpec((SWA_Q, D_MODEL), lambda i: (0, 0))
    wide = jax.ShapeDtypeStruct((m, D_MODEL), BF16)
    narrow = jax.ShapeDtypeStruct((m, SWA_Q), BF16)
    sums = jax.ShapeDtypeStruct((1, D_MODEL), F32)
    return pl.pallas_call(
        body, name="mixer_bwd", grid=(m // tm,),
        in_specs=[row, row, vec, pl.BlockSpec((D_MODEL, D_MODEL), lambda i: (0, 0)), att, att, row,
                  pl.BlockSpec((tm, D_MODEL), lambda i: (i, 1)), wsm, wsm],
        out_specs=[row, vec, vec, row, row, row, row, att, att],
        out_shape=[jax.ShapeDtypeStruct((m, D_MODEL), F32), sums, sums, wide, wide, wide, wide, narrow, narrow],
        compiler_params=_params("arbitrary"),
    )(dh1, u1, g, wout, attn_a, attn_b, zg, zg, wpa, wpb)


def _shift_down(x, k, halo, first):
    rows = lax.broadcasted_iota(jnp.int32, (x.shape[0], 1), 0)
    y = pltpu.roll(x, k, 0)
    for r in range(k):
        fill = jnp.where(first, 0.0, halo[8 - k + r:8 - k + r + 1, :])
        y = jnp.where(rows == r, fill, y)
    return y


def _shift_up(x, k, halo, last):
    n = x.shape[0]
    rows = lax.broadcasted_iota(jnp.int32, (n, 1), 0)
    y = pltpu.roll(x, n - k, 0)
    for r in range(k):
        fill = jnp.where(last, 0.0, halo[r:r + 1, :])
        y = jnp.where(rows == n - k + r, fill, y)
    return y


def _conv_act(gate, gate_m1, gate_m2, cw, cb):
    return cb + cw[0:1, :] * gate_m2 + cw[1:2, :] * gate_m1 + cw[2:3, :] * gate


def _ffn_in_conv(h1, wfi, cw, cb, tm=256):
    s_len = h1.shape[0]
    tm = _div(s_len, tm, 8)
    hb = tm // 8

    def body(a_ref, ap_ref, w_ref, cw_ref, cb_ref, gu_ref, act_ref):
        first = pl.program_id(0) == 0
        a = a_ref[...].astype(BF16)
        before = ap_ref[...].astype(BF16)
        for c in range(N_CHUNK):
            gate = jnp.dot(a, w_ref[c], preferred_element_type=F32)
            up = jnp.dot(a, w_ref[N_CHUNK + c], preferred_element_type=F32)
            halo = jnp.dot(before, w_ref[c], preferred_element_type=F32)
            gu_ref[c, 0] = gate
            gu_ref[c, 1] = up
            conv = _conv_act(gate, _shift_down(gate, 1, halo, first), _shift_down(gate, 2, halo, first),
                             cw_ref[c], cb_ref[c])
            act_ref[c] = (conv * jax.nn.sigmoid(conv) * up).astype(BF16)

    return pl.pallas_call(
        body, name="ffn_in_conv", grid=(s_len // tm,),
        in_specs=[pl.BlockSpec((tm, D_MODEL), lambda i: (i, 0)),
                  pl.BlockSpec((8, D_MODEL), lambda i: (jnp.maximum(i * hb - 1, 0), 0)),
                  pl.BlockSpec((N_DEV, D_MODEL, FF_CHUNK), lambda i: (0, 0, 0)),
                  pl.BlockSpec((N_CHUNK, 8, FF_CHUNK), lambda i: (0, 0, 0)),
                  pl.BlockSpec((N_CHUNK, 1, FF_CHUNK), lambda i: (0, 0, 0))],
        out_specs=[pl.BlockSpec((N_CHUNK, 2, tm, FF_CHUNK), lambda i: (0, 0, i, 0)),
                   pl.BlockSpec((N_CHUNK, tm, FF_CHUNK), lambda i: (0, i, 0))],
        out_shape=[jax.ShapeDtypeStruct((N_CHUNK, 2, s_len, FF_CHUNK), F32),
                   jax.ShapeDtypeStruct((N_CHUNK, s_len, FF_CHUNK), BF16)],
        compiler_params=_params("parallel"),
    )(h1, h1, wfi, cw, cb)


def _ffn_out_ln(act, wfo, res, g, b, target=None, tm=512):
    s_len = res.shape[0]
    tm = _div(s_len, tm, 8)
    last = target is not None

    def body(a_ref, w_ref, res_ref, g_ref, b_ref, *rest):
        u = ALPHA * res_ref[...]
        for c in range(N_CHUNK):
            u = u + jnp.dot(a_ref[c], w_ref[c], preferred_element_type=F32)
        y = _ln(u, g_ref[...], b_ref[...])
        if not last:
            u_ref, y_ref = rest
            u_ref[...] = u
            y_ref[...] = y
            return
        t_ref, u_ref, dy_ref, loss_ref = rest
        u_ref[...] = u

        @pl.when(pl.program_id(0) == 0)
        def _():
            loss_ref[...] = jnp.zeros_like(loss_ref)

        err = y - t_ref[...]
        dy_ref[...] = err / D_MODEL
        loss_ref[...] += 0.5 * jnp.sum(jnp.sum(err * err, axis=1, keepdims=True) / D_MODEL, axis=0, keepdims=True)

    row = pl.BlockSpec((tm, D_MODEL), lambda i: (i, 0))
    vec = pl.BlockSpec((1, D_MODEL), lambda i: (0, 0))
    wide = jax.ShapeDtypeStruct((s_len, D_MODEL), F32)
    return pl.pallas_call(
        body, name="ffn_out_ln_loss" if last else "ffn_out_ln", grid=(s_len // tm,),
        in_specs=[pl.BlockSpec((N_CHUNK, tm, FF_CHUNK), lambda i: (0, i, 0)),
                  pl.BlockSpec((N_CHUNK, FF_CHUNK, D_MODEL), lambda i: (0, 0, 0)), row, vec, vec] + [row] * last,
        out_specs=[row, row] + [pl.BlockSpec((8, LANE), lambda i: (0, 0))] * last,
        out_shape=[wide, wide] + [jax.ShapeDtypeStruct((8, LANE), F32)] * last,
        compiler_params=_params("arbitrary" if last else "parallel"),
    )(act, wfo, res, g, b, *([target] if last else []))


def _ffn_out_bwd(dh2, u2, g, wfo, tm=512):
    s_len = dh2.shape[0]
    tm = _div(s_len, tm, 8)

    def body(dh_ref, u_ref, g_ref, w_ref, du_ref, dg_ref, db_ref, o_ref):
        @pl.when(pl.program_id(0) == 0)
        def _():
            dg_ref[...] = jnp.zeros_like(dg_ref)
            db_ref[...] = jnp.zeros_like(db_ref)

        du, dg, db = _ln_bwd_block(dh_ref[...], u_ref[...], g_ref[...])
        du_ref[...] = du
        dg_ref[...] += dg
        db_ref[...] += db
        du_b = du.astype(BF16)
        for c in range(N_CHUNK):
            o_ref[c] = lax.dot_general(du_b, w_ref[c], (((1,), (1,)), ((), ())), preferred_element_type=F32)

    row = pl.BlockSpec((tm, D_MODEL), lambda i: (i, 0))
    vec = pl.BlockSpec((1, D_MODEL), lambda i: (0, 0))
    sums = jax.ShapeDtypeStruct((1, D_MODEL), F32)
    return pl.pallas_call(
        body, name="ffn_out_bwd", grid=(s_len // tm,),
        in_specs=[row, row, vec, pl.BlockSpec((N_CHUNK, FF_CHUNK, D_MODEL), lambda i: (0, 0, 0))],
        out_specs=[row, vec, vec, pl.BlockSpec((N_CHUNK, tm, FF_CHUNK), lambda i: (0, i, 0))],
        out_shape=[jax.ShapeDtypeStruct((s_len, D_MODEL), F32), sums, sums,
                   jax.ShapeDtypeStruct((N_CHUNK, s_len, FF_CHUNK), F32)],
        compiler_params=_params("arbitrary"),
    )(dh2, u2, g, wfo)


def _g_w_ffn_out(act, du, tm=2048):
    s_len = du.shape[0]
    tm = _div(s_len, tm, 8)
    steps = s_len // tm

    def body(a_ref, g_ref, o_ref, acc_ref):
        s = pl.program_id(1)

        @pl.when(s == 0)
        def _():
            acc_ref[...] = jnp.zeros_like(acc_ref)

        acc_ref[...] += lax.dot_general(a_ref[0], g_ref[...].astype(BF16), (((0,), (0,)), ((), ())),
                                        preferred_element_type=F32)

        @pl.when(s == steps - 1)
        def _():
            o_ref[0] = acc_ref[...].astype(BF16)

    return pl.pallas_call(
        body, name="g_w_ffn_out", grid=(N_CHUNK, steps),
        in_specs=[pl.BlockSpec((1, tm, FF_CHUNK), lambda c, s: (c, s, 0)),
                  pl.BlockSpec((tm, D_MODEL), lambda c, s: (s, 0))],
        out_specs=pl.BlockSpec((1, FF_CHUNK, D_MODEL), lambda c, s: (c, 0, 0)),
        out_shape=jax.ShapeDtypeStruct((N_CHUNK, FF_CHUNK, D_MODEL), BF16),
        scratch_shapes=[pltpu.VMEM((FF_CHUNK, D_MODEL), F32)],
        compiler_params=_params("parallel", "arbitrary"),
    )(act, du)


def _g_w_ffn_in(h1, dgu, tm=2048):
    s_len = h1.shape[0]
    tm = _div(s_len, tm, 8)
    steps = s_len // tm

    def body(a_ref, g_ref, o_ref, acc_ref):
        s = pl.program_id(1)

        @pl.when(s == 0)
        def _():
            acc_ref[...] = jnp.zeros_like(acc_ref)

        acc_ref[...] += lax.dot_general(g_ref[0, 0], a_ref[...].astype(BF16), (((0,), (0,)), ((), ())),
                                        preferred_element_type=F32)

        @pl.when(s == steps - 1)
        def _():
            o_ref[0] = acc_ref[...].astype(BF16)

    return pl.pallas_call(
        body, name="g_w_ffn_in", grid=(N_DEV, steps),
        in_specs=[pl.BlockSpec((tm, D_MODEL), lambda d, s: (s, 0)),
                  pl.BlockSpec((1, 1, tm, FF_CHUNK), lambda d, s: (d % N_CHUNK, d // N_CHUNK, s, 0))],
        out_specs=pl.BlockSpec((1, FF_CHUNK, D_MODEL), lambda d, s: (d, 0, 0)),
        out_shape=jax.ShapeDtypeStruct((N_DEV, FF_CHUNK, D_MODEL), BF16),
        scratch_shapes=[pltpu.VMEM((FF_CHUNK, D_MODEL), F32)],
        compiler_params=_params("parallel", "arbitrary"),
    )(h1, dgu)


def _conv_bwd_dh1(gu, dact, cw, cb, wfi, res, tm=256):
    s_len = gu.shape[2]
    tm = _div(s_len, tm, 8)
    nrow = s_len // tm
    hb = tm // 8

    def dconv_of(conv, sg, up, da):
        return da * up * (sg * (1.0 + conv * (1.0 - sg)))

    def body(gu_ref, gp_ref, gun_ref, da_ref, dan_ref, cw_ref, cb_ref, w_ref, res_ref, dgu_ref, dcw_ref, dh_ref):
        i = pl.program_id(0)
        first = i == 0
        last = i == nrow - 1

        @pl.when(first)
        def _():
            dcw_ref[...] = jnp.zeros_like(dcw_ref)

        row = lax.broadcasted_iota(jnp.int32, (8, 1), 0)
        acc = ALPHA * res_ref[...]
        for c in range(N_CHUNK):
            cw = cw_ref[c]
            cb = cb_ref[c]
            gate = gu_ref[c, 0]
            halo = gp_ref[c, 0]
            g_m1 = _shift_down(gate, 1, halo, first)
            g_m2 = _shift_down(gate, 2, halo, first)
            conv = _conv_act(gate, g_m1, g_m2, cw, cb)
            da = da_ref[c]
            sg = jax.nn.sigmoid(conv)
            dup = (da * conv * sg).astype(BF16)
            dconv = dconv_of(conv, sg, gu_ref[c, 1], da)
            gate_n = gun_ref[c, 0]
            tail = gate[tm - 8:, :]
            conv_n = _conv_act(gate_n, _shift_down(gate_n, 1, tail, False), _shift_down(gate_n, 2, tail, False),
                               cw, cb)
            dconv_n = dconv_of(conv_n, jax.nn.sigmoid(conv_n), gun_ref[c, 1], dan_ref[c])
            dgate = (cw[2:3, :] * dconv + cw[1:2, :] * _shift_up(dconv, 1, dconv_n, last)
                     + cw[0:1, :] * _shift_up(dconv, 2, dconv_n, last)).astype(BF16)
            dgu_ref[c, 0] = dgate
            dgu_ref[c, 1] = dup
            acc = acc + jnp.dot(dgate, w_ref[c], preferred_element_type=F32)
            acc = acc + jnp.dot(dup, w_ref[N_CHUNK + c], preferred_element_type=F32)
            part = jnp.zeros((8, FF_CHUNK), F32)
            for r, term in enumerate((dconv * g_m2, dconv * g_m1, dconv * gate, dconv)):
                part = jnp.where(row == r, jnp.sum(term, axis=0, keepdims=True), part)
            dcw_ref[c] += part
        dh_ref[...] = acc

    nxt = lambda i: jnp.minimum((i + 1) * hb, s_len // 8 - 1)
    main = pl.BlockSpec((N_CHUNK, 2, tm, FF_CHUNK), lambda i: (0, 0, i, 0))
    row_d = pl.BlockSpec((tm, D_MODEL), lambda i: (i, 0))
    return pl.pallas_call(
        body, name="conv_bwd_dh1", grid=(nrow,),
        in_specs=[main,
                  pl.BlockSpec((N_CHUNK, 1, 8, FF_CHUNK), lambda i: (0, 0, jnp.maximum(i * hb - 1, 0), 0)),
                  pl.BlockSpec((N_CHUNK, 2, 8, FF_CHUNK), lambda i: (0, 0, nxt(i), 0)),
                  pl.BlockSpec((N_CHUNK, tm, FF_CHUNK), lambda i: (0, i, 0)),
                  pl.BlockSpec((N_CHUNK, 8, FF_CHUNK), lambda i: (0, nxt(i), 0)),
                  pl.BlockSpec((N_CHUNK, 8, FF_CHUNK), lambda i: (0, 0, 0)),
                  pl.BlockSpec((N_CHUNK, 1, FF_CHUNK), lambda i: (0, 0, 0)),
                  pl.BlockSpec((N_DEV, FF_CHUNK, D_MODEL), lambda i: (0, 0, 0)), row_d],
        out_specs=[main, pl.BlockSpec((N_CHUNK, 8, FF_CHUNK), lambda i: (0, 0, 0)), row_d],
        out_shape=[jax.ShapeDtypeStruct((N_CHUNK, 2, s_len, FF_CHUNK), BF16),
                   jax.ShapeDtypeStruct((N_CHUNK, 8, FF_CHUNK), F32),
                   jax.ShapeDtypeStruct((s_len, D_MODEL), F32)],
        compiler_params=_params("arbitrary"),
    )(gu, gu, gu, dact, dact, cw, cb, wfi, res)


def _sum_devices(r_ref):
    acc = r_ref[0].astype(F32)
    for d in range(1, N_DEV):
        acc = acc + r_ref[d].astype(F32)
    return acc


def _sum8(recv):
    rows = recv.shape[1]
    tr = _div(rows, ROW_BLOCK, 8)

    def body(r_ref, o_ref):
        o_ref[...] = _sum_devices(r_ref)

    return pl.pallas_call(
        body, name="sum8", grid=(rows // tr,),
        in_specs=[pl.BlockSpec((N_DEV, tr, LANE), lambda i: (0, i, 0))],
        out_specs=pl.BlockSpec((tr, LANE), lambda i: (i, 0)),
        out_shape=jax.ShapeDtypeStruct((rows, LANE), F32),
        compiler_params=_params("parallel"),
    )(recv)


def _adamw_math(w, g, m, v):
    m = ADAM_B1 * m + (1.0 - ADAM_B1) * g
    v = ADAM_B2 * v + (1.0 - ADAM_B2) * (g * g)
    m_hat = m / (1.0 - ADAM_B1 ** ADAM_STEP)
    v_hat = v / (1.0 - ADAM_B2 ** ADAM_STEP)
    return -ADAM_LR * (m_hat / (jnp.sqrt(v_hat) + ADAM_EPS) + ADAM_WD * w), m, v


def _adamw_rows(w, g, m, v, name):
    rows = w.shape[0]
    tr = _div(rows, ROW_BLOCK, 8)

    def body(w_ref, g_ref, m_ref, v_ref, d_ref, mo_ref, vo_ref):
        d_ref[...], mo_ref[...], vo_ref[...] = _adamw_math(w_ref[...], g_ref[...], m_ref[...], v_ref[...])

    blk = pl.BlockSpec((tr, LANE), lambda i: (i, 0))
    out = jax.ShapeDtypeStruct((rows, LANE), F32)
    return pl.pallas_call(
        body, name=name, grid=(rows // tr,), in_specs=[blk, blk, blk, blk], out_specs=[blk, blk, blk],
        out_shape=[out, out, out], compiler_params=_params("parallel"),
    )(w, g, m, v)


def _adamw_w_in(recv, w, m, v, tl=128):
    n, depth, d = w.shape

    def body(*refs):
        r_refs, (w_ref, m_ref, v_ref), (g_ref, d_ref, mo_ref, vo_ref) = refs[:depth], refs[depth:depth + 3], refs[-4:]
        for l in range(depth):
            g = _sum_devices(r_refs[l])
            g_ref[:, l, :] = g
            d_ref[:, l, :], mo_ref[:, l, :], vo_ref[:, l, :] = _adamw_math(w_ref[:, l, :], g, m_ref[:, l, :],
                                                                            v_ref[:, l, :])

    blk = pl.BlockSpec((n, depth, tl), lambda j: (0, 0, j))
    out = jax.ShapeDtypeStruct((n, depth, d), F32)
    return pl.pallas_call(
        body, name="adamw_w_in", grid=(d // tl,),
        in_specs=[pl.BlockSpec((N_DEV, n, tl), lambda j: (0, 0, j))] * depth + [blk, blk, blk],
        out_specs=[blk, blk, blk, blk], out_shape=[out, out, out, out],
        compiler_params=_params("parallel"),
    )(*recv, w, m, v)


def _adamw_shard(recv, w, m, v, name):
    depth, k, n = w.shape
    tk = _div(k, 128, 16)

    def body(*refs):
        r_refs, (w_ref, m_ref, v_ref), (g_ref, d_ref, mo_ref, vo_ref) = refs[:depth], refs[depth:depth + 3], refs[-4:]
        for l in range(depth):
            g = _sum_devices(r_refs[l])
            g_ref[l] = g
            d_ref[l], mo_ref[l], vo_ref[l] = _adamw_math(w_ref[l], g, m_ref[l], v_ref[l])

    blk = pl.BlockSpec((depth, tk, n), lambda i: (0, i, 0))
    out = jax.ShapeDtypeStruct((depth, k, n), F32)
    return pl.pallas_call(
        body, name=name, grid=(k // tk,),
        in_specs=[pl.BlockSpec((N_DEV, tk, n), lambda i: (0, i, 0))] * depth + [blk, blk, blk],
        out_specs=[blk, blk, blk, blk], out_shape=[out, out, out, out],
        compiler_params=_params("parallel"),
    )(*recv, w, m, v)


def _to_rows(flat, rows):
    flat = flat.reshape(-1)
    return jnp.pad(flat, (0, rows * LANE - flat.shape[0])).reshape(rows, LANE)


def _pad_z(a, axis):
    f0 = N_QKV
    g0 = N_QKV + FOX_HEADS
    take = lambda lo, hi: lax.slice_in_dim(a, lo, hi, axis=axis)
    shape = list(a.shape)
    shape[axis] = F_PAD - FOX_HEADS
    return jnp.concatenate([take(0, f0), take(g0, N_IN), take(f0, g0), jnp.zeros(shape, a.dtype)], axis=axis)


def _unpad_z(a, axis):
    f0 = N_QKV + N_GATE
    take = lambda lo, hi: lax.slice_in_dim(a, lo, hi, axis=axis)
    return jnp.concatenate([take(0, N_QKV), take(f0, f0 + FOX_HEADS), take(N_QKV, f0)], axis=axis)


def _shards_to_cols(g):
    _, k, n = g.shape
    return g.transpose(1, 0, 2).reshape(k, N_DEV * n)


def _cols_to_shards(full):
    k, n = full.shape
    return full.reshape(k, N_DEV, n // N_DEV).transpose(1, 0, 2)


def _layer_fwd(h, w, p, comm=None, late=None, target=None):
    zq, zg = _z_proj(h, w["w_in_p"], p["b_in_p"])
    qx, kx, vx = _fox_prep(zq, _cumsum_logf(zg))
    attn_a, lse_a = _swa_fwd(zq, p["sinks"])
    attn_b, attn_b32, m_b, l_b, arrived = _fox_fwd(qx, kx, vx, comm)
    if late is not None:
        w, p = late(w, p, arrived)
    h1, u1, merged = _mixer_out(attn_a, attn_b, zg, h, w["w_proj_a"], w["w_proj_b"], w["w_out"],
                                        p["ln_mix_g"], p["ln_mix_b"])
    gu, act = _ffn_in_conv(h1, w["w_ffn_in_fwd"], p["conv_w"], p["conv_b"])
    u2, *h2 = _ffn_out_ln(act, w["w_ffn_out"], h1, p["ln_ffn_g"], p["ln_ffn_b"], target)
    saved = dict(h=h, zq=zq, zg=zg, qx=qx, kx=kx, vx=vx, attn_a=attn_a, lse_a=lse_a, attn_b=attn_b,
                 attn_b32=attn_b32, m_b=m_b, l_b=l_b, h1=h1, u1=u1, merged=merged, gu=gu, act=act, u2=u2)
    return h2, saved, w, p


def _layer_bwd(dh2, sv, w, p, make_comm=None, make_last_comm=None):
    s_len = dh2.shape[0]
    du2, d_ffn_g, d_ffn_b, dact = _ffn_out_bwd(dh2, sv["u2"], p["ln_ffn_g"], w["w_ffn_out"])
    g_ffn_out = _g_w_ffn_out(sv["act"], du2)
    dgu, dcw, dh1 = _conv_bwd_dh1(sv["gu"], dact, p["conv_w"], p["conv_b"], w["w_ffn_in"], du2)
    dcw = dcw.transpose(1, 0, 2).reshape(8, D_FF)
    g_ffn_in = _g_w_ffn_in(sv["h1"], dgu)
    du1, d_mix_g, d_mix_b, dya, dyb, dga, dgb, dattn_a, dattn_b = _mixer_bwd(
        dh1, sv["u1"], p["ln_mix_g"], w["w_out"], sv["attn_a"], sv["attn_b"], sv["zg"], w["w_proj_a"], w["w_proj_b"])
    g_out = _linear_tn(sv["merged"], du1, name="g_w_out", tn=1024)
    g_proj_a = _linear_tn(sv["attn_a"], dya, name="g_w_proj_a", tk=512, tn=1024)
    g_proj_b = _linear_tn(sv["attn_b"], dyb, name="g_w_proj_b", tk=512, tn=1024)
    dq_a, dk_a, dv_a, dsinks = _swa_bwd(sv["zq"], p["sinks"], sv["attn_a"], dattn_a, sv["lse_a"])
    big = dict(w_proj_a=_cols_to_shards(g_proj_a), w_proj_b=_cols_to_shards(g_proj_b),
               w_out=g_out.reshape(N_DEV, D_MODEL // N_DEV, D_MODEL), w_ffn_in=g_ffn_in,
               w_ffn_out=g_ffn_out.reshape(N_DEV, D_FF // N_DEV, D_MODEL))
    dox, stats = _fox_stats(sv["attn_b32"], dattn_b, sv["m_b"], sv["l_b"])
    dq_b, dk_b, dv_b, dcc, arrived = _fox_bwd(sv["qx"], sv["kx"], sv["vx"], dox, stats,
                                              None if make_comm is None else make_comm(big))
    df = _forget_bwd(dcc, sv["zg"])
    dz = jnp.concatenate([dq_a, dk_a.astype(BF16), dv_a.astype(BF16), dq_b.astype(BF16), dk_b, dv_b, dga, dgb, df,
                          jnp.zeros((s_len, F_PAD - LANE), BF16)], axis=1)
    g_in_t, g_b_in = _linear_tn(dz, sv["h"], name="g_w_in", tk=768, tn=1024, colsum=True)
    g_in_t, g_b_in = _unpad_z(g_in_t, 0), _unpad_z(g_b_in, 1)
    big["w_in"] = g_in_t.reshape(N_DEV, N_IN // N_DEV, D_MODEL)
    small = dict(ln_mix_g=d_mix_g, ln_mix_b=d_mix_b, b_in=g_b_in, attn_sinks=dsinks[:, :SWA_HEADS],
                 ln_ffn_g=d_ffn_g, ln_ffn_b=d_ffn_b, conv_w=dcw[:3], conv_b=dcw[3:4])
    dh, arrived_last = _d_h(dz, w["w_in_p"], du1, None if make_last_comm is None else make_last_comm(big, small))
    return dh, big, small, arrived, arrived_last


def _w_in_layouts(w_in):
    return dict(w_in_p=_pad_z(w_in.reshape(N_IN, D_MODEL), 0))


def _other_layouts(w_proj_a, w_proj_b, w_out, w_ffn_in, w_ffn_out):
    return dict(w_proj_a=_shards_to_cols(w_proj_a), w_proj_b=_shards_to_cols(w_proj_b),
                w_out=w_out.reshape(D_MODEL, D_MODEL), w_ffn_in=w_ffn_in,
                w_ffn_in_fwd=w_ffn_in.transpose(0, 2, 1),
                w_ffn_out=w_ffn_out.reshape(N_CHUNK, FF_CHUNK, D_MODEL))


def _layer_params(r):
    return dict(
        b_in_p=_pad_z(r["b_in"].reshape(1, N_IN), 1),
        sinks=jnp.pad(r["attn_sinks"].reshape(1, SWA_HEADS), ((0, 0), (0, LANE - SWA_HEADS))),
        ln_mix_g=r["ln_mix_g"].reshape(1, D_MODEL), ln_mix_b=r["ln_mix_b"].reshape(1, D_MODEL),
        ln_ffn_g=r["ln_ffn_g"].reshape(1, D_MODEL), ln_ffn_b=r["ln_ffn_b"].reshape(1, D_MODEL),
        conv_b=r["conv_b"].reshape(N_CHUNK, 1, FF_CHUNK))


def _conv_w_layout(conv_w):
    return jnp.pad(conv_w, ((0, 5), (0, 0))).reshape(8, N_CHUNK, FF_CHUNK).transpose(1, 0, 2)


def kernel(x, ln_mix_g, ln_mix_b, w_in, b_in, attn_sinks, w_proj_a, w_proj_b, w_out, ln_ffn_g, ln_ffn_b, w_ffn_in, conv_w, conv_b, w_ffn_out, loss_target, m_ln_mix_g, m_ln_mix_b, m_w_in, m_b_in, m_attn_sinks, m_w_proj_a, m_w_proj_b, m_w_out, m_ln_ffn_g, m_ln_ffn_b, m_w_ffn_in, m_conv_w, m_conv_b, m_w_ffn_out, v_ln_mix_g, v_ln_mix_b, v_w_in, v_b_in, v_attn_sinks, v_w_proj_a, v_w_proj_b, v_w_out, v_ln_ffn_g, v_ln_ffn_b, v_w_ffn_in, v_conv_w, v_conv_b, v_w_ffn_out):
    wts = dict(ln_mix_g=ln_mix_g, ln_mix_b=ln_mix_b, w_in=w_in, b_in=b_in, attn_sinks=attn_sinks, w_proj_a=w_proj_a,
               w_proj_b=w_proj_b, w_out=w_out, ln_ffn_g=ln_ffn_g, ln_ffn_b=ln_ffn_b, w_ffn_in=w_ffn_in,
               conv_w=conv_w, conv_b=conv_b, w_ffn_out=w_ffn_out)
    mom = dict(ln_mix_g=m_ln_mix_g, ln_mix_b=m_ln_mix_b, w_in=m_w_in, b_in=m_b_in, attn_sinks=m_attn_sinks,
               w_proj_a=m_w_proj_a, w_proj_b=m_w_proj_b, w_out=m_w_out, ln_ffn_g=m_ln_ffn_g, ln_ffn_b=m_ln_ffn_b,
               w_ffn_in=m_w_ffn_in, conv_w=m_conv_w, conv_b=m_conv_b, w_ffn_out=m_w_ffn_out)
    vel = dict(ln_mix_g=v_ln_mix_g, ln_mix_b=v_ln_mix_b, w_in=v_w_in, b_in=v_b_in, attn_sinks=v_attn_sinks,
               w_proj_a=v_w_proj_a, w_proj_b=v_w_proj_b, w_out=v_w_out, ln_ffn_g=v_ln_ffn_g, ln_ffn_b=v_ln_ffn_b,
               w_ffn_in=v_w_ffn_in, conv_w=v_conv_w, conv_b=v_conv_b, w_ffn_out=v_w_ffn_out)
    names = list(wts)
    big_names = [n for n, _, _ in BIG]
    small_names = [n for n, _ in SMALL]
    me = 4 * lax.axis_index("x") + 2 * lax.axis_index("y") + lax.axis_index("c")
    cw_shard = D_FF // N_DEV

    stored = {"w_in": ((2, 0, 1), (1, 2, 0)), "w_ffn_in": ((0, 2, 1), (0, 2, 1))}
    as_stored = lambda tree: {n: jnp.transpose(tree[n], stored[n][0]) if n in stored else tree[n] for n in big_names}
    w_st, m_st, v_st = as_stored(wts), as_stored(mom), as_stored(vel)
    wb = {n: [(w_st[n][:, l] if n == "w_in" else w_st[n][l]).astype(BF16) for l in range(DEPTH)] for n in big_names}
    ps = [_layer_params(dict(b_in=b_in[l], attn_sinks=attn_sinks[l], ln_mix_g=ln_mix_g[l], ln_mix_b=ln_mix_b[l],
                             ln_ffn_g=ln_ffn_g[l], ln_ffn_b=ln_ffn_b[l], conv_b=conv_b[l])) for l in range(DEPTH)]
    w_in_0 = _gather_two_level(wb["w_in"][0], "gather_w_in_0")
    others = big_names[1:]
    next_layer = {}

    def late_0(w, p, arrived):
        conv_full = arrived[-1].transpose(1, 2, 0, 3).reshape(DEPTH, 3, D_FF)
        next_layer["w"] = _w_in_layouts(arrived[len(others)])
        next_layer["p"] = dict(ps[1], conv_w=_conv_w_layout(conv_full[1]))
        return dict(w, **_other_layouts(*arrived[:len(others)])), dict(p, conv_w=_conv_w_layout(conv_full[0]))

    def late_1(w, p, arrived):
        return dict(w, **_other_layouts(*arrived)), p

    saved, ws = [None] * DEPTH, [None] * DEPTH
    gather_0 = _Comm([(wb[n][0], True) for n in others] + [(wb["w_in"][1], True), (conv_w, True)])
    (h,), saved[0], ws[0], ps[0] = _layer_fwd(x[0], _w_in_layouts(w_in_0), ps[0], gather_0, late_0)
    gather_1 = _Comm([(wb[n][1], True) for n in others])
    (dh, loss_part), saved[1], ws[1], ps[1] = _layer_fwd(h, next_layer["w"], next_layer["p"], gather_1, late_1,
                                                         loss_target[0])

    def small_rows(small):
        vec = jnp.concatenate([small[n].reshape(-1) for n in small_names] + [loss_part[0, 0].reshape(1)])
        return _to_rows(vec, SMALL_LAYER_ROWS)

    dh, big_1, small_1, _, _ = _layer_bwd(dh, saved[1], ws[1], ps[1])

    def exchange_early(big_0):
        return _Comm([(big_1[n].astype(BF16), False) for n in big_names] + [(small_rows(small_1), True)]
                     + [(big_0[n].astype(BF16), False) for n in others])

    def exchange_last(big_0, small_0):
        return _Comm([(big_0["w_in"].astype(BF16), False), (small_rows(small_0), True)])

    grad_x, _, _, arrived, (g_in_0, g_small_0) = _layer_bwd(dh, saved[0], ws[0], ps[0], exchange_early, exchange_last)
    n_big = len(big_names)
    recv = [[g_in_0] + list(arrived[n_big + 1:]) + [g_small_0], list(arrived[:n_big + 1])]

    big_out = {"w_in": _adamw_w_in([recv[l][0] for l in range(DEPTH)], w_st["w_in"], m_st["w_in"], v_st["w_in"])}
    for t, n in enumerate(big_names):
        if n == "w_in":
            continue
        big_out[n] = _adamw_shard([recv[l][t] for l in range(DEPTH)], w_st[n], m_st[n], v_st[n], "adamw_%s" % n)
    for n, (_, back) in stored.items():
        big_out[n] = [jnp.transpose(a, back) for a in big_out[n]]
    small_sum = [_sum8(recv[l][-1]).reshape(-1) for l in range(DEPTH)]
    g_small = {}
    off = 0
    for n, size in SMALL:
        g_small[n] = jnp.stack([small_sum[l][off:off + size] for l in range(DEPTH)])
        off += size
    loss = small_sum[0][off]
    g_small["conv_w"] = lax.dynamic_slice_in_dim(g_small["conv_w"].reshape(DEPTH, 3, D_FF), me * cw_shard, cw_shard,
                                                 axis=2)
    g_small = {n: g_small[n].reshape(wts[n].shape) for n in small_names}

    def pack_small(tree):
        return _to_rows(jnp.concatenate([tree[n].reshape(-1) for n in small_names]), SMALL_ROWS)

    small_out = (pack_small(g_small),) + tuple(_adamw_rows(pack_small(wts), pack_small(g_small), pack_small(mom),
                                                           pack_small(vel), "adamw_small"))

    def result(j):
        out = {n: big_out[n][j] for n in big_names}
        flat = small_out[j].reshape(-1)
        off = 0
        for n in small_names:
            out[n] = flat[off:off + wts[n].size].reshape(wts[n].shape)
            off += wts[n].size
        return [out[n] for n in names]

    return (loss, grad_x[None], *result(0), *result(1), *result(2), *result(3))
```

```python
import jax
import jax.numpy as jnp
import numpy as np
from jax import lax
from jax.experimental import pallas as pl
from jax.experimental.pallas import tpu as pltpu

F32 = jnp.float32
BF16 = jnp.bfloat16
MESH = pl.DeviceIdType.MESH

N_DEV = 8
DEPTH = 2
D_MODEL = 1024
HEAD_DIM = 64
SWA_Q = 512
SWA_KV = 128
FOX_W = 512
FOX_HEADS = 8
SWA_HEADS = 8
D_FF = 2816
N_IN = 4360
N_QKV = SWA_Q + 2 * SWA_KV + 3 * FOX_W
N_GATE = 2 * D_MODEL
F_PAD = 256
N_ZG = N_GATE + F_PAD
N_ZP = N_QKV + N_ZG
LN_EPS = 1e-5
NEG_INF = -1e30
ALPHA = (2 * DEPTH) ** 0.25
SCALE = HEAD_DIM ** -0.5
LOG2E = 1.4426950408889634
SLOPES = tuple(2.0 ** (-8.0 * (h + 1) / SWA_HEADS) for h in range(SWA_HEADS))

ADAM_LR = 0.001
ADAM_B1 = 0.9
ADAM_B2 = 0.999
ADAM_EPS = 1e-08
ADAM_WD = 0.01
ADAM_STEP = 10

LANE = 128
VMEM_LIMIT = 56 * 1024 * 1024

BIG = (("w_in", (D_MODEL, N_IN), 1), ("w_proj_a", (SWA_Q, D_MODEL), 1), ("w_proj_b", (FOX_W, D_MODEL), 1),
       ("w_out", (D_MODEL, D_MODEL), 0), ("w_ffn_in", (D_MODEL, 2 * D_FF), 1), ("w_ffn_out", (D_FF, D_MODEL), 0))
SMALL = (("ln_mix_g", D_MODEL), ("ln_mix_b", D_MODEL), ("b_in", N_IN), ("attn_sinks", SWA_HEADS),
         ("ln_ffn_g", D_MODEL), ("ln_ffn_b", D_MODEL), ("conv_w", 3 * D_FF), ("conv_b", D_FF))
ROW_BLOCK = 512
SMALL_LAYER_ROWS = -(-(sum(n for _, n in SMALL) + 1) // (8 * LANE)) * 8
SMALL_ROWS = ROW_BLOCK
FF_CHUNK = 2 * D_FF // N_DEV
N_CHUNK = D_FF // FF_CHUNK


def _div(n, cap, unit):
    if n <= cap:
        return n
    best = None
    for t in range(unit, cap + 1, unit):
        if n % t == 0:
            best = t
    assert best is not None, (n, cap, unit)
    return best


def _params(*sem):
    return pltpu.CompilerParams(dimension_semantics=sem, vmem_limit_bytes=VMEM_LIMIT)


def _peer(r):
    x, y, c = lax.axis_index("x"), lax.axis_index("y"), lax.axis_index("c")
    px = 1 - x if (r >> 2) & 1 else x
    py = 1 - y if (r >> 1) & 1 else y
    pc = 1 - c if r & 1 else c
    return (px, py, pc), 4 * px + 2 * py + pc


class _Comm:
    def __init__(self, tensors):
        self.arrays = [x for x, _ in tensors]
        self.gathers = [g for _, g in tensors]
        self.n = len(tensors)
        self.out_shape = [jax.ShapeDtypeStruct((N_DEV,) + (x.shape if g else x.shape[1:]), x.dtype)
                          for x, g in tensors]
        self.specs = [pl.BlockSpec(memory_space=pl.ANY)] * self.n
        self.scratch = [pltpu.SemaphoreType.DMA((N_DEV - 1, self.n)), pltpu.SemaphoreType.DMA((N_DEV - 1, self.n)),
                        pltpu.SemaphoreType.DMA((self.n,))]

    def _copies(self, x_refs, out_refs, sems):
        send_sems, recv_sems, local_sems = sems
        _, me = _peer(0)

        def src(t, idx):
            return x_refs[t] if self.gathers[t] else x_refs[t].at[idx]

        def remote(r, t, mine):
            peer, pid = _peer(r)
            return pltpu.make_async_remote_copy(src_ref=src(t, pid), dst_ref=out_refs[t].at[me if mine else pid],
                                                send_sem=send_sems.at[r - 1, t], recv_sem=recv_sems.at[r - 1, t],
                                                device_id=peer, device_id_type=MESH)

        pairs = [(r, t) for r in range(1, N_DEV) for t in range(self.n)]
        local = [pltpu.make_async_copy(src(t, me), out_refs[t].at[me], local_sems.at[t]) for t in range(self.n)]
        return local, [remote(r, t, True) for r, t in pairs], lambda: [remote(r, t, False) for r, t in pairs]

    def start(self, x_refs, out_refs, sems):
        local, sent, _ = self._copies(x_refs, out_refs, sems)
        for cp in local + sent:
            cp.start()

    def wait(self, x_refs, out_refs, sems):
        local, sent, landing = self._copies(x_refs, out_refs, sems)
        for cp in landing():
            cp.wait_recv()
        for cp in sent:
            cp.wait_send()
        for cp in local:
            cp.wait()


def _gather_two_level(x, name):
    def body(x_ref, out_ref, send_sems, recv_sems, local_sem):
        x_, y_, c_ = lax.axis_index("x"), lax.axis_index("y"), lax.axis_index("c")
        me, sibling = (x_, y_, c_), (x_, y_, 1 - c_)
        chips = [(1 - x_, y_), (x_, 1 - y_), (1 - x_, 1 - y_)]

        def slab(px, py, pc):
            return out_ref.at[4 * px + 2 * py + pc]

        def copy(k, block, to, src=None):
            return pltpu.make_async_remote_copy(src_ref=slab(*block) if src is None else src, dst_ref=slab(*block),
                                                send_sem=send_sems.at[k], recv_sem=recv_sems.at[k], device_id=to,
                                                device_id_type=MESH)

        mine = pltpu.make_async_copy(x_ref, slab(*me), local_sem)
        mine.start()
        first = [copy(0, me, sibling, src=x_ref)] + [copy(1 + j, me, (*chip, c_), src=x_ref)
                                                     for j, chip in enumerate(chips)]
        for cp in first:
            cp.start()
        passed = [copy(4 + j, (*chip, c_), sibling) for j, chip in enumerate(chips)]
        for j, chip in enumerate(chips):
            copy(1 + j, (*chip, c_), me).wait_recv()
            passed[j].start()
        copy(0, sibling, me).wait_recv()
        for j, chip in enumerate(chips):
            copy(4 + j, (*chip, 1 - c_), me).wait_recv()
        for cp in first + passed:
            cp.wait_send()
        mine.wait()

    spec = pl.BlockSpec(memory_space=pl.ANY)
    return pl.pallas_call(
        body, name=name, out_shape=jax.ShapeDtypeStruct((N_DEV,) + x.shape, x.dtype), in_specs=[spec], out_specs=spec,
        scratch_shapes=[pltpu.SemaphoreType.DMA((N_DEV - 1,)), pltpu.SemaphoreType.DMA((N_DEV - 1,)),
                        pltpu.SemaphoreType.DMA],
    )(x)


def _with_comm(comm, n_in, n_out, first, last, compute):
    nc = comm.n if comm is not None else 0

    def body(*refs):
        ins, x_refs = refs[:n_in], refs[n_in:n_in + nc]
        outs = refs[n_in + nc:n_in + nc + n_out]
        out_refs = refs[n_in + nc + n_out:n_in + 2 * nc + n_out]
        sems = refs[n_in + 2 * nc + n_out:]
        if nc:
            @pl.when(first())
            def _():
                comm.start(x_refs, out_refs, sems)

        compute(*ins, *outs)
        if nc:
            @pl.when(last())
            def _():
                comm.wait(x_refs, out_refs, sems)

    return body


def _d_h(dz, w_in_t, res, comm=None, tm=512):
    m, k = dz.shape
    d = w_in_t.shape[1]
    tm = _div(m, tm, 8)
    steps = m // tm
    c_specs, c_shapes, c_scratch, c_arrays = _comm_parts(comm)

    def compute(dz_ref, w_ref, res_ref, o_ref):
        o_ref[...] = ALPHA * res_ref[...] + jnp.dot(dz_ref[...], w_ref[...], preferred_element_type=F32)

    body = _with_comm(comm, 3, 1, lambda: pl.program_id(0) == 0, lambda: pl.program_id(0) == steps - 1, compute)
    row = pl.BlockSpec((tm, d), lambda i: (i, 0))
    outs = pl.pallas_call(
        body, name="d_h" if comm is None else "d_h_comm", grid=(steps,),
        in_specs=[pl.BlockSpec((tm, k), lambda i: (i, 0)), pl.BlockSpec((k, d), lambda i: (0, 0)), row] + c_specs,
        out_specs=[row] + c_specs, out_shape=[jax.ShapeDtypeStruct((m, d), F32)] + c_shapes,
        scratch_shapes=c_scratch,
        compiler_params=_params("arbitrary"),
    )(dz, w_in_t, res, *c_arrays)
    return outs[0], outs[1:]


def _z_proj(h, w_in_t, b_p, tm=512):
    m, k = h.shape
    tm = _div(m, tm, 8)
    nt = (((1,), (1,)), ((), ()))

    def body(h_ref, w_ref, b_ref, zq_ref, zg_ref):
        a = h_ref[...].astype(BF16)
        zq = lax.dot_general(a, w_ref[:N_QKV, :], nt, preferred_element_type=F32)
        zq_ref[...] = (zq + b_ref[:, :N_QKV]).astype(BF16)
        zg_ref[...] = lax.dot_general(a, w_ref[N_QKV:, :], nt, preferred_element_type=F32) + b_ref[:, N_QKV:]

    return pl.pallas_call(
        body, name="z_proj", grid=(m // tm,),
        in_specs=[pl.BlockSpec((tm, k), lambda i: (i, 0)), pl.BlockSpec((N_ZP, k), lambda i: (0, 0)),
                  pl.BlockSpec((1, N_ZP), lambda i: (0, 0))],
        out_specs=[pl.BlockSpec((tm, N_QKV), lambda i: (i, 0)), pl.BlockSpec((tm, N_ZG), lambda i: (i, 0))],
        out_shape=[jax.ShapeDtypeStruct((m, N_QKV), BF16), jax.ShapeDtypeStruct((m, N_ZG), F32)],
        compiler_params=_params("parallel"),
    )(h, w_in_t, b_p)


def _linear_tn(a, g, *, name, tk=1024, tn=640, tm=2048, colsum=False):
    m, k = a.shape
    n = g.shape[1]
    tk = _div(k, tk, LANE)
    tn = _div(n, tn, LANE)
    tm = _div(m, tm, 8)
    steps = m // tm
    assert not colsum or tn == n

    def body(a_ref, g_ref, o_ref, *rest):
        acc_ref = rest[-1]
        s = pl.program_id(2)

        @pl.when(s == 0)
        def _():
            acc_ref[...] = jnp.zeros_like(acc_ref)
            if colsum:
                rest[0][...] = jnp.zeros_like(rest[0])

        a_blk = a_ref[...]
        acc_ref[...] += lax.dot_general(a_blk.astype(BF16), g_ref[...].astype(BF16), (((0,), (0,)), ((), ())),
                                        preferred_element_type=F32)
        if colsum:
            rest[0][...] += jnp.sum(a_blk.astype(F32), axis=0, keepdims=True)

        @pl.when(s == steps - 1)
        def _():
            o_ref[...] = acc_ref[...].astype(BF16)

    out_specs = [pl.BlockSpec((tk, tn), lambda i, j, s: (i, j))]
    out_shape = [jax.ShapeDtypeStruct((k, n), BF16)]
    if colsum:
        out_specs.append(pl.BlockSpec((1, tk), lambda i, j, s: (0, i)))
        out_shape.append(jax.ShapeDtypeStruct((1, k), F32))
    outs = pl.pallas_call(
        body, name=name, grid=(k // tk, n // tn, steps),
        in_specs=[pl.BlockSpec((tm, tk), lambda i, j, s: (s, i)), pl.BlockSpec((tm, tn), lambda i, j, s: (s, j))],
        out_specs=out_specs, out_shape=out_shape,
        scratch_shapes=[pltpu.VMEM((tk, tn), F32)],
        compiler_params=_params("parallel", "parallel", "arbitrary"),
    )(a, g)
    return outs if colsum else outs[0]


def _ln(u, g, b):
    mu = jnp.mean(u, axis=-1, keepdims=True)
    d = u - mu
    var = jnp.mean(d * d, axis=-1, keepdims=True)
    return d * lax.rsqrt(var + LN_EPS) * g + b


def _ln_bwd_block(dy, u, g):
    mu = jnp.mean(u, axis=-1, keepdims=True)
    dd = u - mu
    rstd = lax.rsqrt(jnp.mean(dd * dd, axis=-1, keepdims=True) + LN_EPS)
    xhat = dd * rstd
    dxh = dy * g
    m1 = jnp.mean(dxh, axis=-1, keepdims=True)
    m2 = jnp.mean(dxh * xhat, axis=-1, keepdims=True)
    return (rstd * (dxh - m1 - xhat * m2), jnp.sum(dy * xhat, axis=0, keepdims=True),
            jnp.sum(dy, axis=0, keepdims=True))


SCAN_ROWS = 512


def _tri(n, upper):
    r = lax.broadcasted_iota(jnp.int32, (n, n), 0)
    c = lax.broadcasted_iota(jnp.int32, (n, n), 1)
    return jnp.where((c >= r) if upper else (c <= r), 1.0, 0.0).astype(F32)


def _cumsum_logf(zg):
    s = zg.shape[0]
    t = _div(s, SCAN_ROWS, LANE)
    nb = s // t
    fcol = N_GATE // LANE

    def body(f_ref, c_ref, carry_ref):
        @pl.when(pl.program_id(0) == 0)
        def _():
            carry_ref[...] = jnp.zeros_like(carry_ref)

        f = f_ref[...]
        logf = jnp.minimum(f, 0.0) - jnp.log(1.0 + jnp.exp(-jnp.abs(f)))
        c = jnp.dot(_tri(t, False), logf, precision=lax.Precision.HIGHEST, preferred_element_type=F32)
        c = c + carry_ref[0:1, :]
        c_ref[...] = c
        carry_ref[...] = jnp.broadcast_to(c[t - 1:t, :], carry_ref.shape)

    return pl.pallas_call(
        body, name="cumsum_logf", grid=(nb,),
        in_specs=[pl.BlockSpec((t, LANE), lambda i: (i, fcol))],
        out_specs=pl.BlockSpec((t, LANE), lambda i: (i, 0)),
        out_shape=jax.ShapeDtypeStruct((s, LANE), F32),
        scratch_shapes=[pltpu.VMEM((8, LANE), F32)],
        compiler_params=_params("arbitrary"),
    )(zg)


def _forget_bwd(dcc, zg):
    s = zg.shape[0]
    t = _div(s, SCAN_ROWS, LANE)
    nb = s // t
    fcol = N_GATE // LANE

    def body(dc_ref, f_ref, o_ref, carry_ref):
        @pl.when(pl.program_id(0) == 0)
        def _():
            carry_ref[...] = jnp.zeros_like(carry_ref)

        lane = lax.broadcasted_iota(jnp.int32, (1, LANE), 1)
        dc = jnp.zeros((t, LANE), F32)
        for p in range(FOX_HEADS // 2):
            tile = dc_ref[:, p * LANE:(p + 1) * LANE]
            moved = pltpu.roll(tile, 2 * p, 1) if p else tile
            dc = jnp.where((lane == 2 * p) | (lane == 2 * p + 1), moved, dc)
        dlogf = jnp.dot(_tri(t, True), dc, precision=lax.Precision.HIGHEST, preferred_element_type=F32)
        dlogf = dlogf + carry_ref[0:1, :]
        o_ref[...] = (dlogf * jax.nn.sigmoid(-f_ref[...])).astype(BF16)
        carry_ref[...] = jnp.broadcast_to(dlogf[0:1, :], carry_ref.shape)

    return pl.pallas_call(
        body, name="forget_bwd", grid=(nb,),
        in_specs=[pl.BlockSpec((t, FOX_W), lambda i: (nb - 1 - i, 0)),
                  pl.BlockSpec((t, LANE), lambda i: (nb - 1 - i, fcol))],
        out_specs=pl.BlockSpec((t, LANE), lambda i: (nb - 1 - i, 0)),
        out_shape=jax.ShapeDtypeStruct((s, LANE), BF16),
        scratch_shapes=[pltpu.VMEM((8, LANE), F32)],
        compiler_params=_params("arbitrary"),
    )(dcc, zg)


KA_COL = SWA_Q // LANE
VA_COL = KA_COL + 1


def _half_masks():
    lane = lax.broadcasted_iota(jnp.int32, (1, LANE), 1)
    hi = lane >= HEAD_DIM
    return (jnp.logical_not(hi), hi)


def _both_halves(x, sel):
    xs = jnp.where(sel, x, 0.0)
    return xs + pltpu.roll(xs, HEAD_DIM, 1)


SWA_PER_KV = 4
WIDE = SWA_PER_KV * LANE


def _swa_bias():
    k = np.arange(2 * LANE)[:, None]
    q = np.arange(LANE)[None, :]
    dist = (q + LANE - k).astype(np.float32)
    valid = (dist >= 0) & (dist < LANE)
    per_head = [np.where(valid, np.float32(-s) * dist, np.float32(NEG_INF)) for s in SLOPES]
    return jnp.asarray(np.stack([np.concatenate(per_head[SWA_PER_KV * hk:SWA_PER_KV * (hk + 1)], axis=1)
                                 for hk in range(2)]), F32)


def _no_previous_block(i_blk):
    k = lax.broadcasted_iota(jnp.int32, (2 * LANE, WIDE), 0)
    return jnp.where((i_blk == 0) & (k < LANE), NEG_INF, 0.0)


def _stack_heads(ref, blk, hk, halves, scale):
    tiles = []
    for j in range(SWA_PER_KV):
        p = 2 * hk + j // 2
        t = ref[blk, p * LANE:(p + 1) * LANE]
        if scale:
            t = _scaled(t)
        tiles.append(jnp.where(halves[j % 2], t, jnp.zeros_like(t)))
    return jnp.concatenate(tiles, axis=0)


def _pair_tile(wide, pp, row_halves):
    a = wide[:, (2 * pp) * LANE:(2 * pp + 1) * LANE]
    b = wide[:, (2 * pp + 1) * LANE:(2 * pp + 2) * LANE]
    return jnp.where(row_halves[0], a, b).T


def _lane_blocks(rows8, hk):
    return jnp.concatenate([rows8[SWA_PER_KV * hk + j:SWA_PER_KV * hk + j + 1, :] for j in range(SWA_PER_KV)], axis=1)


def _row_halves():
    hi = lax.broadcasted_iota(jnp.int32, (LANE, 1), 0) >= HEAD_DIM
    return (jnp.logical_not(hi), hi)


NT = (((1,), (1,)), ((), ()))


def _scaled(q):
    return (q.astype(F32) * SCALE).astype(BF16)


SWA_GROUP = 4


def _swa_group(s_len):
    return SWA_GROUP if (s_len // LANE) % SWA_GROUP == 0 else 1


def _swa_specs(group):
    rows = group * LANE
    prev = lambda i: jnp.maximum(i * group - 1, 0)
    return [pl.BlockSpec((rows, SWA_Q), lambda i: (i, 0)),
            pl.BlockSpec((rows, LANE), lambda i: (i, KA_COL)), pl.BlockSpec((rows, LANE), lambda i: (i, VA_COL)),
            pl.BlockSpec((LANE, LANE), lambda i: (prev(i), KA_COL)),
            pl.BlockSpec((LANE, LANE), lambda i: (prev(i), VA_COL))]


def _swa_window(g, cur_ref, prev_ref):
    before = prev_ref[...] if g == 0 else cur_ref[(g - 1) * LANE:g * LANE, :]
    return jnp.concatenate([before, cur_ref[g * LANE:(g + 1) * LANE, :]], axis=0).astype(F32)


def _swa_fwd(zq, sinks):
    s_len = zq.shape[0]
    group = _swa_group(s_len)
    rows = group * LANE
    sink_lanes = jnp.repeat(sinks[:, :SWA_HEADS], LANE, axis=1)

    def body(q_ref, kc_ref, vc_ref, kp_ref, vp_ref, sink_ref, bias_ref, o_ref, lse_ref):
        halves = _half_masks()
        row_halves = _row_halves()
        for g in range(group):
            blk = slice(g * LANE, (g + 1) * LANE)
            kcat = _swa_window(g, kc_ref, kp_ref)
            vcat = _swa_window(g, vc_ref, vp_ref)
            lse_rows = []
            for hk in range(2):
                kb = _both_halves(kcat, halves[hk]).astype(BF16)
                v_t = _both_halves(vcat, halves[hk]).T.astype(BF16)
                q4 = _stack_heads(q_ref, blk, hk, halves, True)
                s_t = lax.dot_general(kb, q4, NT, preferred_element_type=F32) + bias_ref[hk]
                if g == 0:
                    s_t = s_t + _no_previous_block(pl.program_id(0))
                sink = sink_ref[:, hk * WIDE:(hk + 1) * WIDE]
                m = jnp.maximum(jnp.max(s_t, axis=0, keepdims=True), sink)
                pe = jnp.exp(s_t - m)
                den = jnp.sum(pe, axis=0, keepdims=True) + jnp.exp(sink - m)
                out_t = jnp.dot(v_t, (pe * (1.0 / den)).astype(BF16), preferred_element_type=F32)
                for pp in range(2):
                    p = 2 * hk + pp
                    o_ref[blk, p * LANE:(p + 1) * LANE] = _pair_tile(out_t, pp, row_halves).astype(BF16)
                lse4 = m + jnp.log(den)
                lse_rows += [lse4[:, j * LANE:(j + 1) * LANE] for j in range(SWA_PER_KV)]
            lse_ref[:, blk] = jnp.concatenate(lse_rows, axis=0)

    return pl.pallas_call(
        body, name="swa_fwd", grid=(s_len // rows,),
        in_specs=_swa_specs(group) + [pl.BlockSpec((1, SWA_HEADS * LANE), lambda i: (0, 0)),
                                      pl.BlockSpec((2, 2 * LANE, WIDE), lambda i: (0, 0, 0))],
        out_specs=[pl.BlockSpec((rows, SWA_Q), lambda i: (i, 0)), pl.BlockSpec((SWA_HEADS, rows), lambda i: (0, i))],
        out_shape=[jax.ShapeDtypeStruct((s_len, SWA_Q), BF16), jax.ShapeDtypeStruct((SWA_HEADS, s_len), F32)],
        compiler_params=_params("parallel"),
    )(zq, zq, zq, zq, zq, sink_lanes, _swa_bias())


def _swa_bwd(zq, sinks, o, do, lse):
    s_len = zq.shape[0]
    group = _swa_group(s_len)
    rows = group * LANE

    def body(q_ref, kc_ref, vc_ref, kp_ref, vp_ref, sink_ref, bias_ref, o_ref, do_ref, lse_ref,
             dq_ref, dk_ref, dv_ref, ds_ref):
        halves = _half_masks()
        row_halves = _row_halves()
        lane = lax.broadcasted_iota(jnp.int32, (1, LANE), 1)
        dsink = jnp.zeros((1, LANE), F32)
        for g in range(group):
            blk = slice(g * LANE, (g + 1) * LANE)
            i_blk = pl.program_id(0) * group + g
            kcat = _swa_window(g, kc_ref, kp_ref)
            vcat = _swa_window(g, vc_ref, vp_ref)
            lse_rows = lse_ref[:, blk]
            prod = do_ref[blk, :].astype(F32) * o_ref[blk, :].astype(F32)
            select = (lax.broadcasted_iota(jnp.int32, (SWA_HEADS, SWA_Q), 1) // HEAD_DIM
                      == lax.broadcasted_iota(jnp.int32, (SWA_HEADS, SWA_Q), 0))
            picks = jnp.where(select, 1.0, 0.0).astype(BF16)
            prod_hi = prod.astype(BF16)
            prod_lo = (prod - prod_hi.astype(F32)).astype(BF16)
            delta_rows = (lax.dot_general(picks, prod_hi, NT, preferred_element_type=F32)
                          + lax.dot_general(picks, prod_lo, NT, preferred_element_type=F32))
            dk_tot = jnp.zeros((2 * LANE, LANE), F32)
            dv_tot = jnp.zeros((2 * LANE, LANE), F32)
            for hk in range(2):
                kb = _both_halves(kcat, halves[hk])
                k_t = kb.T.astype(BF16)
                kb = kb.astype(BF16)
                vb = _both_halves(vcat, halves[hk]).astype(BF16)
                q4 = _stack_heads(q_ref, blk, hk, halves, True)
                do4 = _stack_heads(do_ref, blk, hk, halves, False)
                lse4 = _lane_blocks(lse_rows, hk)
                delta4 = _lane_blocks(delta_rows, hk)
                s_t = lax.dot_general(kb, q4, NT, preferred_element_type=F32) + bias_ref[hk]
                if g == 0:
                    s_t = s_t + _no_previous_block(pl.program_id(0))
                p_t = jnp.exp(s_t - lse4)
                dp_t = lax.dot_general(vb, do4, NT, preferred_element_type=F32)
                ds_t = (p_t * (dp_t - delta4)).astype(BF16)
                sink_part = jnp.exp(sink_ref[:, hk * WIDE:(hk + 1) * WIDE] - lse4) * delta4
                for j in range(SWA_PER_KV):
                    dsink_h = -jnp.sum(sink_part[:, j * LANE:(j + 1) * LANE], axis=1, keepdims=True)
                    dsink = dsink + jnp.where(lane == SWA_PER_KV * hk + j, dsink_h, 0.0)
                dq_t = jnp.dot(k_t, ds_t, preferred_element_type=F32)
                for pp in range(2):
                    p = 2 * hk + pp
                    dq_ref[blk, p * LANE:(p + 1) * LANE] = (_pair_tile(dq_t, pp, row_halves) * SCALE).astype(BF16)
                dk_acc = jnp.dot(ds_t, q4, preferred_element_type=F32)
                dv_acc = jnp.dot(p_t.astype(BF16), do4, preferred_element_type=F32)
                dk_tot = dk_tot + jnp.where(halves[hk], dk_acc + pltpu.roll(dk_acc, HEAD_DIM, 1), 0.0)
                dv_tot = dv_tot + jnp.where(halves[hk], dv_acc + pltpu.roll(dv_acc, HEAD_DIM, 1), 0.0)
            cur = pl.ds(pl.multiple_of(i_blk * LANE, LANE), LANE)
            dk_ref[cur, :] = dk_tot[LANE:, :]
            dv_ref[cur, :] = dv_tot[LANE:, :]

            def add_previous(i_blk=i_blk, dk_tot=dk_tot, dv_tot=dv_tot):
                prv = pl.ds(pl.multiple_of((i_blk - 1) * LANE, LANE), LANE)
                dk_ref[prv, :] += dk_tot[:LANE, :]
                dv_ref[prv, :] += dv_tot[:LANE, :]

            if g == 0:
                pl.when(i_blk > 0)(add_previous)
            else:
                add_previous()

        @pl.when(pl.program_id(0) == 0)
        def _():
            ds_ref[...] = jnp.zeros_like(ds_ref)

        ds_ref[...] += dsink

    blk512 = pl.BlockSpec((rows, SWA_Q), lambda i: (i, 0))
    full = pl.BlockSpec((s_len, LANE), lambda i: (0, 0))
    vec = pl.BlockSpec((1, LANE), lambda i: (0, 0))
    return pl.pallas_call(
        body, name="swa_bwd", grid=(s_len // rows,),
        in_specs=_swa_specs(group) + [pl.BlockSpec((1, SWA_HEADS * LANE), lambda i: (0, 0)),
                                      pl.BlockSpec((2, 2 * LANE, WIDE), lambda i: (0, 0, 0)), blk512, blk512,
                                      pl.BlockSpec((SWA_HEADS, rows), lambda i: (0, i))],
        out_specs=[blk512, full, full, vec],
        out_shape=[jax.ShapeDtypeStruct((s_len, SWA_Q), BF16), jax.ShapeDtypeStruct((s_len, LANE), F32),
                   jax.ShapeDtypeStruct((s_len, LANE), F32), jax.ShapeDtypeStruct((1, LANE), F32)],
        compiler_params=_params("arbitrary"),
    )(zq, zq, zq, zq, zq, jnp.repeat(sinks[:, :SWA_HEADS], LANE, axis=1), _swa_bias(), o, do, lse)


QB_COL = (SWA_Q + 2 * SWA_KV) // LANE
KB_COL = QB_COL + FOX_W // LANE
VB_COL = KB_COL + FOX_W // LANE
N_PAIR = FOX_HEADS // 2


def _causal(t, keys_first=False):
    r = lax.broadcasted_iota(jnp.int32, (t, t), 0)
    c = lax.broadcasted_iota(jnp.int32, (t, t), 1)
    return c >= r if keys_first else r >= c


N_SPLIT = 3


def _own_half(e):
    hi = lax.broadcasted_iota(jnp.int32, (1, LANE), 1) >= HEAD_DIM
    return hi if e else jnp.logical_not(hi)


def _feature_lane(e, t):
    return HEAD_DIM * (1 - e) + t


def _feature_tables():
    wide = FOX_HEADS * LANE
    place_q, place_k = np.zeros((N_SPLIT * LANE, wide), np.float32), np.zeros((N_SPLIT * LANE, wide), np.float32)
    ones_q, ones_k, ones_v, own = (np.zeros((1, wide), np.float32) for _ in range(4))
    for h in range(FOX_HEADS):
        e = h % 2
        own[0, h * LANE + HEAD_DIM * e:h * LANE + HEAD_DIM * (e + 1)] = 1.0
        ones_v[0, h * LANE + _feature_lane(e, 0)] = 1.0
        for t in range(N_SPLIT):
            place_q[t * LANE + h, h * LANE + _feature_lane(e, t)] = 1.0
            ones_q[0, h * LANE + _feature_lane(e, N_SPLIT + t)] = 1.0
            ones_k[0, h * LANE + _feature_lane(e, t)] = 1.0
            place_k[t * LANE + h, h * LANE + _feature_lane(e, N_SPLIT + t)] = -1.0
    return tuple(jnp.asarray(a) for a in (place_q, place_k, ones_q, ones_k, ones_v, own))


def _fox_prep(zq, c, tm=512):
    s_len = zq.shape[0]
    tm = _div(s_len, tm, 8)
    wide = FOX_HEADS * LANE

    def body(z_ref, c_ref, pq_ref, pk_ref, oq_ref, ok_ref, ov_ref, own_ref, qx_ref, kx_ref, vx_ref):
        rest = c_ref[...]
        parts = []
        for _ in range(N_SPLIT):
            part = rest.astype(BF16).astype(F32)
            rest = rest - part
            parts.append(part)
        parts = jnp.concatenate(parts, axis=1)
        qf = jnp.dot(parts, pq_ref[...], preferred_element_type=F32) + oq_ref[...]
        kf = jnp.dot(parts, pk_ref[...], preferred_element_type=F32) + ok_ref[...]
        own = own_ref[...] > 0.5
        for p in range(N_PAIR):
            cols = slice(2 * p * LANE, (2 * p + 2) * LANE)
            pair = lambda col: jnp.tile(z_ref[:, (col + p) * LANE:(col + p + 1) * LANE].astype(F32), (1, 2))
            qx_ref[:, cols] = jnp.where(own[:, cols], pair(QB_COL) * SCALE, qf[:, cols]).astype(BF16)
            kx_ref[:, cols] = jnp.where(own[:, cols], pair(KB_COL), kf[:, cols]).astype(BF16)
            vx_ref[:, cols] = jnp.where(own[:, cols], pair(VB_COL), ov_ref[:, cols]).astype(BF16)

    out = jax.ShapeDtypeStruct((s_len, wide), BF16)
    blk = pl.BlockSpec((tm, wide), lambda i: (i, 0))
    table = pl.BlockSpec((N_SPLIT * LANE, wide), lambda i: (0, 0))
    vec = pl.BlockSpec((1, wide), lambda i: (0, 0))
    return pl.pallas_call(
        body, name="fox_prep", grid=(s_len // tm,),
        in_specs=[pl.BlockSpec((tm, N_QKV), lambda i: (i, 0)), pl.BlockSpec((tm, LANE), lambda i: (i, 0)),
                  table, table, vec, vec, vec, vec],
        out_specs=[blk, blk, blk], out_shape=[out, out, out],
        compiler_params=_params("parallel"),
    )(zq, c, *_feature_tables())


def _comm_parts(comm):
    return ([], [], [], []) if comm is None else (comm.specs, comm.out_shape, comm.scratch, comm.arrays)


def _fox_fwd(qx, kx, vx, comm=None, t_cap=1024):
    s_len = qx.shape[0]
    t = _div(s_len, t_cap, LANE)
    nq = s_len // t
    c_specs, c_shapes, c_scratch, c_arrays = _comm_parts(comm)

    def compute(q_ref, k_ref, v_ref, o_ref, o32_ref, m_ref, l_ref):
        i = pl.program_id(1)
        qs = [q_ref[:, e * LANE:(e + 1) * LANE] for e in range(2)]

        def step(j, carry, diag):
            rows = pl.ds(pl.multiple_of(j * t, t), t)
            new = []
            for e in range(2):
                m, acc = carry[e]
                s2 = lax.dot_general(qs[e], k_ref[rows, e * LANE:(e + 1) * LANE], NT,
                                     preferred_element_type=F32) * LOG2E
                if diag:
                    s2 = jnp.where(_causal(t), s2, NEG_INF)
                mn = jnp.maximum(m, jnp.ceil(jnp.max(s2, axis=1, keepdims=True)))
                pe = jnp.exp2(s2 - mn).astype(BF16)
                acc = acc * jnp.exp2(m - mn) + jnp.dot(pe, v_ref[rows, e * LANE:(e + 1) * LANE],
                                                       preferred_element_type=F32)
                new.append((mn, acc))
            return tuple(new)

        init = (jnp.full((t, 1), NEG_INF, F32), jnp.zeros((t, LANE), F32))
        carry = lax.fori_loop(0, i, lambda j, c: step(j, c, False), (init, init))
        carry = step(i, carry, True)
        outs, ls = [], []
        for e in range(2):
            m, acc = carry[e]
            l = acc[:, _feature_lane(e, 0):_feature_lane(e, 0) + 1]
            outs.append(acc / l)
            ls.append(l)
        out = jnp.where(_own_half(1), outs[1], outs[0])
        o_ref[...] = out.astype(BF16)
        o32_ref[...] = out
        m_ref[...] = jnp.where(_own_half(1), carry[1][0], carry[0][0])
        l_ref[...] = jnp.where(_own_half(1), ls[1], ls[0])

    body = _with_comm(comm, 3, 4, lambda: (pl.program_id(0) == 0) & (pl.program_id(1) == 0),
                      lambda: (pl.program_id(0) == N_PAIR - 1) & (pl.program_id(1) == nq - 1), compute)
    pair = pl.BlockSpec((s_len, 2 * LANE), lambda p, i: (0, p))
    tile = pl.BlockSpec((t, LANE), lambda p, i: (i, p))
    wide = jax.ShapeDtypeStruct((s_len, FOX_W), F32)
    outs = pl.pallas_call(
        body, name="fox_fwd" if comm is None else "fox_fwd_comm%d" % comm.n, grid=(N_PAIR, nq),
        in_specs=[pl.BlockSpec((t, 2 * LANE), lambda p, i: (i, p)), pair, pair] + c_specs,
        out_specs=[tile, tile, tile, tile] + c_specs,
        out_shape=[jax.ShapeDtypeStruct((s_len, FOX_W), BF16), wide, wide, wide] + c_shapes,
        scratch_shapes=c_scratch,
        compiler_params=_params("arbitrary", "arbitrary"),
    )(qx, kx, vx, *c_arrays)
    return outs[0], outs[1], outs[2], outs[3], outs[4:]


def _fox_stats(o, do, m, l, tm=512):
    s_len = o.shape[0]
    tm = _div(s_len, tm, LANE)

    def body(o_ref, do_ref, m_ref, l_ref, dox_ref, st_ref):
        lane = lax.broadcasted_iota(jnp.int32, (1, LANE), 1)
        for p in range(N_PAIR):
            cols = slice(p * LANE, (p + 1) * LANE)
            dout = do_ref[:, cols]
            prod = o_ref[:, cols] * dout.astype(F32)
            shift = m_ref[:, cols]
            inv_l = 1.0 / l_ref[:, cols]
            st = jnp.zeros((tm, LANE), F32)
            for e in range(2):
                h = 2 * p + e
                dox_ref[:, h * LANE:(h + 1) * LANE] = jnp.where(_own_half(e), dout, jnp.zeros_like(dout))
                st = jnp.where(lane == e, shift[:, e * HEAD_DIM:e * HEAD_DIM + 1], st)
                delta = jnp.sum(jnp.where(_own_half(e), prod, 0.0), axis=1, keepdims=True)
                st = jnp.where(lane == 2 + e, delta, st)
                st = jnp.where(lane == 4 + e, inv_l[:, e * HEAD_DIM:e * HEAD_DIM + 1], st)
            st_ref[p] = st.T[:8, :]

    row = pl.BlockSpec((tm, FOX_W), lambda i: (i, 0))
    return pl.pallas_call(
        body, name="fox_stats", grid=(s_len // tm,), in_specs=[row, row, row, row],
        out_specs=[pl.BlockSpec((tm, FOX_HEADS * LANE), lambda i: (i, 0)),
                   pl.BlockSpec((N_PAIR, 8, tm), lambda i: (0, 0, i))],
        out_shape=[jax.ShapeDtypeStruct((s_len, FOX_HEADS * LANE), BF16),
                   jax.ShapeDtypeStruct((N_PAIR, 8, s_len), F32)],
        compiler_params=_params("parallel"),
    )(o, do, m, l)


def _fox_bwd(qx, kx, vx, dox, stats, comm=None, t_cap=1024):
    s_len = qx.shape[0]
    t = _div(s_len, t_cap, LANE)
    n = s_len // t
    c_specs, c_shapes, c_scratch, c_arrays = _comm_parts(comm)

    def compute(q_ref, do_ref, st_ref, k_ref, v_ref, dq_ref, dk_ref, dv_ref, dc_ref):
        j = pl.program_id(1)
        lane = lax.broadcasted_iota(jnp.int32, (1, LANE), 1)

        @pl.when(j == 0)
        def _():
            dq_ref[...] = jnp.zeros_like(dq_ref)

        ks = [k_ref[:, e * LANE:(e + 1) * LANE] for e in range(2)]
        vs = [v_ref[:, e * LANE:(e + 1) * LANE] for e in range(2)]
        ks_t = [k.astype(F32).T.astype(BF16) for k in ks]

        def step(i, carry, diag):
            rows = pl.ds(pl.multiple_of(i * t, t), t)
            new = []
            dq = jnp.zeros((LANE, t), F32)
            for e in range(2):
                dk, dv, dc = carry[e]
                q = q_ref[rows, e * LANE:(e + 1) * LANE]
                dout = do_ref[rows, e * LANE:(e + 1) * LANE]
                s_t = lax.dot_general(ks[e], q, NT, preferred_element_type=F32) * LOG2E
                if diag:
                    s_t = jnp.where(_causal(t, keys_first=True), s_t, NEG_INF)
                p_t = jnp.exp2(s_t - st_ref[0, e:e + 1, rows]).astype(BF16).astype(F32) * st_ref[0, 4 + e:5 + e, rows]
                dp_t = lax.dot_general(vs[e], dout, NT, preferred_element_type=F32)
                ds_f = p_t * (dp_t - st_ref[0, 2 + e:3 + e, rows])
                ds_t = ds_f.astype(BF16)
                dc = dc + jnp.sum(ds_f, axis=1, keepdims=True)
                dv = dv + jnp.dot(p_t.astype(BF16), dout, preferred_element_type=F32)
                dk = dk + jnp.dot(ds_t, q, preferred_element_type=F32)
                dq_e = jnp.dot(ks_t[e], ds_t, preferred_element_type=F32)
                dq = dq + jnp.where(_row_halves()[e], dq_e, 0.0)
                new.append((dk, dv, dc))
            dq_ref[rows, :] += dq.T * SCALE
            return tuple(new)

        zero = jnp.zeros((t, LANE), F32)
        init = (zero, zero, jnp.zeros((t, 1), F32))
        carry = step(j, (init, init), True)
        (dk0, dv0, dc0), (dk1, dv1, dc1) = lax.fori_loop(j + 1, n, lambda i, c: step(i, c, False), carry)
        dk_ref[...] = jnp.where(_own_half(1), dk1, dk0).astype(BF16)
        dv_ref[...] = jnp.where(_own_half(1), dv1, dv0).astype(BF16)
        dc_ref[...] = jnp.where(lane == 0, -dc0, jnp.where(lane == 1, -dc1, 0.0))

    body = _with_comm(comm, 5, 4, lambda: (pl.program_id(0) == 0) & (pl.program_id(1) == 0),
                      lambda: (pl.program_id(0) == N_PAIR - 1) & (pl.program_id(1) == n - 1), compute)
    pair = pl.BlockSpec((s_len, 2 * LANE), lambda p, j: (0, p))
    blk = pl.BlockSpec((t, 2 * LANE), lambda p, j: (j, p))
    tile = pl.BlockSpec((t, LANE), lambda p, j: (j, p))
    outs = pl.pallas_call(
        body, name="fox_bwd" if comm is None else "fox_bwd_comm", grid=(N_PAIR, n),
        in_specs=[pair, pair, pl.BlockSpec((1, 8, s_len), lambda p, j: (p, 0, 0)), blk, blk] + c_specs,
        out_specs=[pl.BlockSpec((s_len, LANE), lambda p, j: (0, p)), tile, tile, tile] + c_specs,
        out_shape=[jax.ShapeDtypeStruct((s_len, FOX_W), F32), jax.ShapeDtypeStruct((s_len, FOX_W), BF16),
                   jax.ShapeDtypeStruct((s_len, FOX_W), BF16), jax.ShapeDtypeStruct((s_len, FOX_W), F32)] + c_shapes,
        scratch_shapes=c_scratch,
        compiler_params=_params("arbitrary", "arbitrary"),
    )(qx, dox, stats, kx, vx, *c_arrays)
    return outs[0], outs[1], outs[2], outs[3], outs[4:]


def _mixer_out(attn_a, attn_b, zg, h, wpa, wpb, wout, g, b, tm=512):
    m = h.shape[0]
    tm = _div(m, tm, 8)

    def body(a_ref, b_ref, ga_ref, gb_ref, h_ref, wpa_ref, wpb_ref, wout_ref, g_ref, bb_ref, h1_ref, u_ref, mg_ref):
        ya = jnp.dot(a_ref[...], wpa_ref[...], preferred_element_type=F32)
        yb = jnp.dot(b_ref[...], wpb_ref[...], preferred_element_type=F32)
        merged = (jax.nn.sigmoid(ga_ref[...]) * ya + jax.nn.sigmoid(gb_ref[...]) * yb).astype(BF16)
        u = ALPHA * h_ref[...] + jnp.dot(merged, wout_ref[...], preferred_element_type=F32)
        u_ref[...] = u
        h1_ref[...] = _ln(u, g_ref[...], bb_ref[...])
        mg_ref[...] = merged

    row = pl.BlockSpec((tm, D_MODEL), lambda i: (i, 0))
    att = pl.BlockSpec((tm, SWA_Q), lambda i: (i, 0))
    vec = pl.BlockSpec((1, D_MODEL), lambda i: (0, 0))
    wsm = pl.BlockSpec((SWA_Q, D_MODEL), lambda i: (0, 0))
    return pl.pallas_call(
        body, name="mixer_out", grid=(m // tm,),
        in_specs=[att, att, row, pl.BlockSpec((tm, D_MODEL), lambda i: (i, 1)), row, wsm, wsm,
                  pl.BlockSpec((D_MODEL, D_MODEL), lambda i: (0, 0)), vec, vec],
        out_specs=[row, row, row],
        out_shape=[jax.ShapeDtypeStruct((m, D_MODEL), F32), jax.ShapeDtypeStruct((m, D_MODEL), F32),
                   jax.ShapeDtypeStruct((m, D_MODEL), BF16)],
        compiler_params=_params("parallel"),
    )(attn_a, attn_b, zg, zg, h, wpa, wpb, wout, g, b)


def _mixer_bwd(dh1, u1, g, wout, attn_a, attn_b, zg, wpa, wpb, tm=512):
    m = dh1.shape[0]
    tm = _div(m, tm, 8)

    def body(dh_ref, u_ref, g_ref, wout_ref, a_ref, b_ref, ga_ref, gb_ref, wpa_ref, wpb_ref,
             du_ref, dg_ref, db_ref, dya_ref, dyb_ref, dga_ref, dgb_ref, da_ref, dbb_ref):
        @pl.when(pl.program_id(0) == 0)
        def _():
            dg_ref[...] = jnp.zeros_like(dg_ref)
            db_ref[...] = jnp.zeros_like(db_ref)

        du, dg, db = _ln_bwd_block(dh_ref[...], u_ref[...], g_ref[...])
        du_ref[...] = du
        dg_ref[...] += dg
        db_ref[...] += db
        dm = lax.dot_general(du.astype(BF16), wout_ref[...], (((1,), (1,)), ((), ())), preferred_element_type=F32)
        for x_ref, gate_ref, w_ref, dy_ref, dgate_ref, dattn_ref in (
                (a_ref, ga_ref, wpa_ref, dya_ref, dga_ref, da_ref), (b_ref, gb_ref, wpb_ref, dyb_ref, dgb_ref, dbb_ref)):
            sg = jax.nn.sigmoid(gate_ref[...])
            dy = (dm * sg).astype(BF16)
            dy_ref[...] = dy
            y = jnp.dot(x_ref[...], w_ref[...], preferred_element_type=F32)
            dgate_ref[...] = (dm * y * sg * (1.0 - sg)).astype(BF16)
            dattn_ref[...] = lax.dot_general(dy, w_ref[...], (((1,), (1,)), ((), ())),
                                             preferred_element_type=F32).astype(BF16)

    row = pl.BlockSpec((tm, D_MODEL), lambda i: (i, 0))
    att = pl.BlockSpec((tm, SWA_Q), lambda i: (i, 0))
    vec = pl.BlockSpec((1, D_MODEL), lambda i: (0, 0))
    wsm = pl.BlockSpec((SWA_Q, D_MODEL), lambda i: (0, 0))
    wide = jax.ShapeDtypeStruct((m, D_MODEL), BF16)
    narrow = jax.ShapeDtypeStruct((m, SWA_Q), BF16)
    sums = jax.ShapeDtypeStruct((1, D_MODEL), F32)
    return pl.pallas_call(
        body, name="mixer_bwd", grid=(m // tm,),
        in_specs=[row, row, vec, pl.BlockSpec((D_MODEL, D_MODEL), lambda i: (0, 0)), att, att, row,
                  pl.BlockSpec((tm, D_MODEL), lambda i: (i, 1)), wsm, wsm],
        out_specs=[row, vec, vec, row, row, row, row, att, att],
        out_shape=[jax.ShapeDtypeStruct((m, D_MODEL), F32), sums, sums, wide, wide, wide, wide, narrow, narrow],
        compiler_params=_params("arbitrary"),
    )(dh1, u1, g, wout, attn_a, attn_b, zg, zg, wpa, wpb)


def _shift_down(x, k, halo, first):
    rows = lax.broadcasted_iota(jnp.int32, (x.shape[0], 1), 0)
    y = pltpu.roll(x, k, 0)
    for r in range(k):
        fill = jnp.where(first, 0.0, halo[8 - k + r:8 - k + r + 1, :])
        y = jnp.where(rows == r, fill, y)
    return y


def _shift_up(x, k, halo, last):
    n = x.shape[0]
    rows = lax.broadcasted_iota(jnp.int32, (n, 1), 0)
    y = pltpu.roll(x, n - k, 0)
    for r in range(k):
        fill = jnp.where(last, 0.0, halo[r:r + 1, :])
        y = jnp.where(rows == n - k + r, fill, y)
    return y


def _conv_act(gate, gate_m1, gate_m2, cw, cb):
    return cb + cw[0:1, :] * gate_m2 + cw[1:2, :] * gate_m1 + cw[2:3, :] * gate


def _ffn_in_conv(h1, wfi, cw, cb, tm=256):
    s_len = h1.shape[0]
    tm = _div(s_len, tm, 8)
    hb = tm // 8

    def body(a_ref, ap_ref, w_ref, cw_ref, cb_ref, gu_ref, act_ref):
        first = pl.program_id(0) == 0
        a = a_ref[...].astype(BF16)
        before = ap_ref[...].astype(BF16)
        for c in range(N_CHUNK):
            gate = jnp.dot(a, w_ref[c], preferred_element_type=F32)
            up = jnp.dot(a, w_ref[N_CHUNK + c], preferred_element_type=F32)
            halo = jnp.dot(before, w_ref[c], preferred_element_type=F32)
            gu_ref[c, 0] = gate
            gu_ref[c, 1] = up
            conv = _conv_act(gate, _shift_down(gate, 1, halo, first), _shift_down(gate, 2, halo, first),
                             cw_ref[c], cb_ref[c])
            act_ref[c] = (conv * jax.nn.sigmoid(conv) * up).astype(BF16)

    return pl.pallas_call(
        body, name="ffn_in_conv", grid=(s_len // tm,),
        in_specs=[pl.BlockSpec((tm, D_MODEL), lambda i: (i, 0)),
                  pl.BlockSpec((8, D_MODEL), lambda i: (jnp.maximum(i * hb - 1, 0), 0)),
                  pl.BlockSpec((N_DEV, D_MODEL, FF_CHUNK), lambda i: (0, 0, 0)),
                  pl.BlockSpec((N_CHUNK, 8, FF_CHUNK), lambda i: (0, 0, 0)),
                  pl.BlockSpec((N_CHUNK, 1, FF_CHUNK), lambda i: (0, 0, 0))],
        out_specs=[pl.BlockSpec((N_CHUNK, 2, tm, FF_CHUNK), lambda i: (0, 0, i, 0)),
                   pl.BlockSpec((N_CHUNK, tm, FF_CHUNK), lambda i: (0, i, 0))],
        out_shape=[jax.ShapeDtypeStruct((N_CHUNK, 2, s_len, FF_CHUNK), F32),
                   jax.ShapeDtypeStruct((N_CHUNK, s_len, FF_CHUNK), BF16)],
        compiler_params=_params("parallel"),
    )(h1, h1, wfi, cw, cb)


def _ffn_out_ln(act, wfo, res, g, b, target=None, tm=512):
    s_len = res.shape[0]
    tm = _div(s_len, tm, 8)
    last = target is not None

    def body(a_ref, w_ref, res_ref, g_ref, b_ref, *rest):
        u = ALPHA * res_ref[...]
        for c in range(N_CHUNK):
            u = u + jnp.dot(a_ref[c], w_ref[c], preferred_element_type=F32)
        y = _ln(u, g_ref[...], b_ref[...])
        if not last:
            u_ref, y_ref = rest
            u_ref[...] = u
            y_ref[...] = y
            return
        t_ref, u_ref, dy_ref, loss_ref = rest
        u_ref[...] = u

        @pl.when(pl.program_id(0) == 0)
        def _():
            loss_ref[...] = jnp.zeros_like(loss_ref)

        err = y - t_ref[...]
        dy_ref[...] = err / D_MODEL
        loss_ref[...] += 0.5 * jnp.sum(jnp.sum(err * err, axis=1, keepdims=True) / D_MODEL, axis=0, keepdims=True)

    row = pl.BlockSpec((tm, D_MODEL), lambda i: (i, 0))
    vec = pl.BlockSpec((1, D_MODEL), lambda i: (0, 0))
    wide = jax.ShapeDtypeStruct((s_len, D_MODEL), F32)
    return pl.pallas_call(
        body, name="ffn_out_ln_loss" if last else "ffn_out_ln", grid=(s_len // tm,),
        in_specs=[pl.BlockSpec((N_CHUNK, tm, FF_CHUNK), lambda i: (0, i, 0)),
                  pl.BlockSpec((N_CHUNK, FF_CHUNK, D_MODEL), lambda i: (0, 0, 0)), row, vec, vec] + [row] * last,
        out_specs=[row, row] + [pl.BlockSpec((8, LANE), lambda i: (0, 0))] * last,
        out_shape=[wide, wide] + [jax.ShapeDtypeStruct((8, LANE), F32)] * last,
        compiler_params=_params("arbitrary" if last else "parallel"),
    )(act, wfo, res, g, b, *([target] if last else []))


def _ffn_out_bwd(dh2, u2, g, wfo, tm=512):
    s_len = dh2.shape[0]
    tm = _div(s_len, tm, 8)

    def body(dh_ref, u_ref, g_ref, w_ref, du_ref, dg_ref, db_ref, o_ref):
        @pl.when(pl.program_id(0) == 0)
        def _():
            dg_ref[...] = jnp.zeros_like(dg_ref)
            db_ref[...] = jnp.zeros_like(db_ref)

        du, dg, db = _ln_bwd_block(dh_ref[...], u_ref[...], g_ref[...])
        du_ref[...] = du
        dg_ref[...] += dg
        db_ref[...] += db
        du_b = du.astype(BF16)
        for c in range(N_CHUNK):
            o_ref[c] = lax.dot_general(du_b, w_ref[c], (((1,), (1,)), ((), ())), preferred_element_type=F32)

    row = pl.BlockSpec((tm, D_MODEL), lambda i: (i, 0))
    vec = pl.BlockSpec((1, D_MODEL), lambda i: (0, 0))
    sums = jax.ShapeDtypeStruct((1, D_MODEL), F32)
    return pl.pallas_call(
        body, name="ffn_out_bwd", grid=(s_len // tm,),
        in_specs=[row, row, vec, pl.BlockSpec((N_CHUNK, FF_CHUNK, D_MODEL), lambda i: (0, 0, 0))],
        out_specs=[row, vec, vec, pl.BlockSpec((N_CHUNK, tm, FF_CHUNK), lambda i: (0, i, 0))],
        out_shape=[jax.ShapeDtypeStruct((s_len, D_MODEL), F32), sums, sums,
                   jax.ShapeDtypeStruct((N_CHUNK, s_len, FF_CHUNK), F32)],
        compiler_params=_params("arbitrary"),
    )(dh2, u2, g, wfo)


def _g_w_ffn_out(act, du, tm=2048):
    s_len = du.shape[0]
    tm = _div(s_len, tm, 8)
    steps = s_len // tm

    def body(a_ref, g_ref, o_ref, acc_ref):
        s = pl.program_id(1)

        @pl.when(s == 0)
        def _():
            acc_ref[...] = jnp.zeros_like(acc_ref)

        acc_ref[...] += lax.dot_general(a_ref[0], g_ref[...].astype(BF16), (((0,), (0,)), ((), ())),
                                        preferred_element_type=F32)

        @pl.when(s == steps - 1)
        def _():
            o_ref[0] = acc_ref[...].astype(BF16)

    return pl.pallas_call(
        body, name="g_w_ffn_out", grid=(N_CHUNK, steps),
        in_specs=[pl.BlockSpec((1, tm, FF_CHUNK), lambda c, s: (c, s, 0)),
                  pl.BlockSpec((tm, D_MODEL), lambda c, s: (s, 0))],
        out_specs=pl.BlockSpec((1, FF_CHUNK, D_MODEL), lambda c, s: (c, 0, 0)),
        out_shape=jax.ShapeDtypeStruct((N_CHUNK, FF_CHUNK, D_MODEL), BF16),
        scratch_shapes=[pltpu.VMEM((FF_CHUNK, D_MODEL), F32)],
        compiler_params=_params("parallel", "arbitrary"),
    )(act, du)


def _g_w_ffn_in(h1, dgu, tm=2048):
    s_len = h1.shape[0]
    tm = _div(s_len, tm, 8)
    steps = s_len // tm

    def body(a_ref, g_ref, o_ref, acc_ref):
        s = pl.program_id(1)

        @pl.when(s == 0)
        def _():
            acc_ref[...] = jnp.zeros_like(acc_ref)

        acc_ref[...] += lax.dot_general(g_ref[0, 0], a_ref[...].astype(BF16), (((0,), (0,)), ((), ())),
                                        preferred_element_type=F32)

        @pl.when(s == steps - 1)
        def _():
            o_ref[0] = acc_ref[...].astype(BF16)

    return pl.pallas_call(
        body, name="g_w_ffn_in", grid=(N_DEV, steps),
        in_specs=[pl.BlockSpec((tm, D_MODEL), lambda d, s: (s, 0)),
                  pl.BlockSpec((1, 1, tm, FF_CHUNK), lambda d, s: (d % N_CHUNK, d // N_CHUNK, s, 0))],
        out_specs=pl.BlockSpec((1, FF_CHUNK, D_MODEL), lambda d, s: (d, 0, 0)),
        out_shape=jax.ShapeDtypeStruct((N_DEV, FF_CHUNK, D_MODEL), BF16),
        scratch_shapes=[pltpu.VMEM((FF_CHUNK, D_MODEL), F32)],
        compiler_params=_params("parallel", "arbitrary"),
    )(h1, dgu)


def _conv_bwd_dh1(gu, dact, cw, cb, wfi, res, tm=256):
    s_len = gu.shape[2]
    tm = _div(s_len, tm, 8)
    nrow = s_len // tm
    hb = tm // 8

    def dconv_of(conv, sg, up, da):
        return da * up * (sg * (1.0 + conv * (1.0 - sg)))

    def body(gu_ref, gp_ref, gun_ref, da_ref, dan_ref, cw_ref, cb_ref, w_ref, res_ref, dgu_ref, dcw_ref, dh_ref):
        i = pl.program_id(0)
        first = i == 0
        last = i == nrow - 1

        @pl.when(first)
        def _():
            dcw_ref[...] = jnp.zeros_like(dcw_ref)

        row = lax.broadcasted_iota(jnp.int32, (8, 1), 0)
        acc = ALPHA * res_ref[...]
        for c in range(N_CHUNK):
            cw = cw_ref[c]
            cb = cb_ref[c]
            gate = gu_ref[c, 0]
            halo = gp_ref[c, 0]
            g_m1 = _shift_down(gate, 1, halo, first)
            g_m2 = _shift_down(gate, 2, halo, first)
            conv = _conv_act(gate, g_m1, g_m2, cw, cb)
            da = da_ref[c]
            sg = jax.nn.sigmoid(conv)
            dup = (da * conv * sg).astype(BF16)
            dconv = dconv_of(conv, sg, gu_ref[c, 1], da)
            gate_n = gun_ref[c, 0]
            tail = gate[tm - 8:, :]
            conv_n = _conv_act(gate_n, _shift_down(gate_n, 1, tail, False), _shift_down(gate_n, 2, tail, False),
                               cw, cb)
            dconv_n = dconv_of(conv_n, jax.nn.sigmoid(conv_n), gun_ref[c, 1], dan_ref[c])
            dgate = (cw[2:3, :] * dconv + cw[1:2, :] * _shift_up(dconv, 1, dconv_n, last)
                     + cw[0:1, :] * _shift_up(dconv, 2, dconv_n, last)).astype(BF16)
            dgu_ref[c, 0] = dgate
            dgu_ref[c, 1] = dup
            acc = acc + jnp.dot(dgate, w_ref[c], preferred_element_type=F32)
            acc = acc + jnp.dot(dup, w_ref[N_CHUNK + c], preferred_element_type=F32)
            part = jnp.zeros((8, FF_CHUNK), F32)
            for r, term in enumerate((dconv * g_m2, dconv * g_m1, dconv * gate, dconv)):
                part = jnp.where(row == r, jnp.sum(term, axis=0, keepdims=True), part)
            dcw_ref[c] += part
        dh_ref[...] = acc

    nxt = lambda i: jnp.minimum((i + 1) * hb, s_len // 8 - 1)
    main = pl.BlockSpec((N_CHUNK, 2, tm, FF_CHUNK), lambda i: (0, 0, i, 0))
    row_d = pl.BlockSpec((tm, D_MODEL), lambda i: (i, 0))
    return pl.pallas_call(
        body, name="conv_bwd_dh1", grid=(nrow,),
        in_specs=[main,
                  pl.BlockSpec((N_CHUNK, 1, 8, FF_CHUNK), lambda i: (0, 0, jnp.maximum(i * hb - 1, 0), 0)),
                  pl.BlockSpec((N_CHUNK, 2, 8, FF_CHUNK), lambda i: (0, 0, nxt(i), 0)),
                  pl.BlockSpec((N_CHUNK, tm, FF_CHUNK), lambda i: (0, i, 0)),
                  pl.BlockSpec((N_CHUNK, 8, FF_CHUNK), lambda i: (0, nxt(i), 0)),
                  pl.BlockSpec((N_CHUNK, 8, FF_CHUNK), lambda i: (0, 0, 0)),
                  pl.BlockSpec((N_CHUNK, 1, FF_CHUNK), lambda i: (0, 0, 0)),
                  pl.BlockSpec((N_DEV, FF_CHUNK, D_MODEL), lambda i: (0, 0, 0)), row_d],
        out_specs=[main, pl.BlockSpec((N_CHUNK, 8, FF_CHUNK), lambda i: (0, 0, 0)), row_d],
        out_shape=[jax.ShapeDtypeStruct((N_CHUNK, 2, s_len, FF_CHUNK), BF16),
                   jax.ShapeDtypeStruct((N_CHUNK, 8, FF_CHUNK), F32),
                   jax.ShapeDtypeStruct((s_len, D_MODEL), F32)],
        compiler_params=_params("arbitrary"),
    )(gu, gu, gu, dact, dact, cw, cb, wfi, res)


def _sum_devices(r_ref):
    acc = r_ref[0].astype(F32)
    for d in range(1, N_DEV):
        acc = acc + r_ref[d].astype(F32)
    return acc


def _sum8(recv):
    rows = recv.shape[1]
    tr = _div(rows, ROW_BLOCK, 8)

    def body(r_ref, o_ref):
        o_ref[...] = _sum_devices(r_ref)

    return pl.pallas_call(
        body, name="sum8", grid=(rows // tr,),
        in_specs=[pl.BlockSpec((N_DEV, tr, LANE), lambda i: (0, i, 0))],
        out_specs=pl.BlockSpec((tr, LANE), lambda i: (i, 0)),
        out_shape=jax.ShapeDtypeStruct((rows, LANE), F32),
        compiler_params=_params("parallel"),
    )(recv)


def _adamw_math(w, g, m, v):
    m = ADAM_B1 * m + (1.0 - ADAM_B1) * g
    v = ADAM_B2 * v + (1.0 - ADAM_B2) * (g * g)
    m_hat = m / (1.0 - ADAM_B1 ** ADAM_STEP)
    v_hat = v / (1.0 - ADAM_B2 ** ADAM_STEP)
    return -ADAM_LR * (m_hat / (jnp.sqrt(v_hat) + ADAM_EPS) + ADAM_WD * w), m, v


def _adamw_rows(w, g, m, v, name):
    rows = w.shape[0]
    tr = _div(rows, ROW_BLOCK, 8)

    def body(w_ref, g_ref, m_ref, v_ref, d_ref, mo_ref, vo_ref):
        d_ref[...], mo_ref[...], vo_ref[...] = _adamw_math(w_ref[...], g_ref[...], m_ref[...], v_ref[...])

    blk = pl.BlockSpec((tr, LANE), lambda i: (i, 0))
    out = jax.ShapeDtypeStruct((rows, LANE), F32)
    return pl.pallas_call(
        body, name=name, grid=(rows // tr,), in_specs=[blk, blk, blk, blk], out_specs=[blk, blk, blk],
        out_shape=[out, out, out], compiler_params=_params("parallel"),
    )(w, g, m, v)


def _adamw_w_in(recv, w, m, v, tl=128):
    n, depth, d = w.shape

    def body(*refs):
        r_refs, (w_ref, m_ref, v_ref), (g_ref, d_ref, mo_ref, vo_ref) = refs[:depth], refs[depth:depth + 3], refs[-4:]
        for l in range(depth):
            g = _sum_devices(r_refs[l])
            g_ref[:, l, :] = g
            d_ref[:, l, :], mo_ref[:, l, :], vo_ref[:, l, :] = _adamw_math(w_ref[:, l, :], g, m_ref[:, l, :],
                                                                            v_ref[:, l, :])

    blk = pl.BlockSpec((n, depth, tl), lambda j: (0, 0, j))
    out = jax.ShapeDtypeStruct((n, depth, d), F32)
    return pl.pallas_call(
        body, name="adamw_w_in", grid=(d // tl,),
        in_specs=[pl.BlockSpec((N_DEV, n, tl), lambda j: (0, 0, j))] * depth + [blk, blk, blk],
        out_specs=[blk, blk, blk, blk], out_shape=[out, out, out, out],
        compiler_params=_params("parallel"),
    )(*recv, w, m, v)


def _adamw_shard(recv, w, m, v, name):
    depth, k, n = w.shape
    tk = _div(k, 256, 16)

    def body(*refs):
        r_refs, (w_ref, m_ref, v_ref), (g_ref, d_ref, mo_ref, vo_ref) = refs[:depth], refs[depth:depth + 3], refs[-4:]
        for l in range(depth):
            g = _sum_devices(r_refs[l])
            g_ref[l] = g
            d_ref[l], mo_ref[l], vo_ref[l] = _adamw_math(w_ref[l], g, m_ref[l], v_ref[l])

    blk = pl.BlockSpec((depth, tk, n), lambda i: (0, i, 0))
    out = jax.ShapeDtypeStruct((depth, k, n), F32)
    return pl.pallas_call(
        body, name=name, grid=(k // tk,),
        in_specs=[pl.BlockSpec((N_DEV, tk, n), lambda i: (0, i, 0))] * depth + [blk, blk, blk],
        out_specs=[blk, blk, blk, blk], out_shape=[out, out, out, out],
        compiler_params=_params("parallel"),
    )(*recv, w, m, v)


def _to_rows(flat, rows):
    flat = flat.reshape(-1)
    return jnp.pad(flat, (0, rows * LANE - flat.shape[0])).reshape(rows, LANE)


def _pad_z(a, axis):
    f0 = N_QKV
    g0 = N_QKV + FOX_HEADS
    take = lambda lo, hi: lax.slice_in_dim(a, lo, hi, axis=axis)
    shape = list(a.shape)
    shape[axis] = F_PAD - FOX_HEADS
    return jnp.concatenate([take(0, f0), take(g0, N_IN), take(f0, g0), jnp.zeros(shape, a.dtype)], axis=axis)


def _unpad_z(a, axis):
    f0 = N_QKV + N_GATE
    take = lambda lo, hi: lax.slice_in_dim(a, lo, hi, axis=axis)
    return jnp.concatenate([take(0, N_QKV), take(f0, f0 + FOX_HEADS), take(N_QKV, f0)], axis=axis)


def _shards_to_cols(g):
    _, k, n = g.shape
    return g.transpose(1, 0, 2).reshape(k, N_DEV * n)


def _cols_to_shards(full):
    k, n = full.shape
    return full.reshape(k, N_DEV, n // N_DEV).transpose(1, 0, 2)


def _layer_fwd(h, w, p, comm=None, late=None, target=None):
    zq, zg = _z_proj(h, w["w_in_p"], p["b_in_p"])
    qx, kx, vx = _fox_prep(zq, _cumsum_logf(zg))
    attn_a, lse_a = _swa_fwd(zq, p["sinks"])
    attn_b, attn_b32, m_b, l_b, arrived = _fox_fwd(qx, kx, vx, comm)
    if late is not None:
        w, p = late(w, p, arrived)
    h1, u1, merged = _mixer_out(attn_a, attn_b, zg, h, w["w_proj_a"], w["w_proj_b"], w["w_out"],
                                        p["ln_mix_g"], p["ln_mix_b"])
    gu, act = _ffn_in_conv(h1, w["w_ffn_in_fwd"], p["conv_w"], p["conv_b"])
    u2, *h2 = _ffn_out_ln(act, w["w_ffn_out"], h1, p["ln_ffn_g"], p["ln_ffn_b"], target)
    saved = dict(h=h, zq=zq, zg=zg, qx=qx, kx=kx, vx=vx, attn_a=attn_a, lse_a=lse_a, attn_b=attn_b,
                 attn_b32=attn_b32, m_b=m_b, l_b=l_b, h1=h1, u1=u1, merged=merged, gu=gu, act=act, u2=u2)
    return h2, saved, w, p


def _layer_bwd(dh2, sv, w, p, make_comm=None, make_last_comm=None):
    s_len = dh2.shape[0]
    du2, d_ffn_g, d_ffn_b, dact = _ffn_out_bwd(dh2, sv["u2"], p["ln_ffn_g"], w["w_ffn_out"])
    g_ffn_out = _g_w_ffn_out(sv["act"], du2)
    dgu, dcw, dh1 = _conv_bwd_dh1(sv["gu"], dact, p["conv_w"], p["conv_b"], w["w_ffn_in"], du2)
    dcw = dcw.transpose(1, 0, 2).reshape(8, D_FF)
    g_ffn_in = _g_w_ffn_in(sv["h1"], dgu)
    du1, d_mix_g, d_mix_b, dya, dyb, dga, dgb, dattn_a, dattn_b = _mixer_bwd(
        dh1, sv["u1"], p["ln_mix_g"], w["w_out"], sv["attn_a"], sv["attn_b"], sv["zg"], w["w_proj_a"], w["w_proj_b"])
    g_out = _linear_tn(sv["merged"], du1, name="g_w_out", tn=1024)
    g_proj_a = _linear_tn(sv["attn_a"], dya, name="g_w_proj_a", tk=512, tn=1024)
    g_proj_b = _linear_tn(sv["attn_b"], dyb, name="g_w_proj_b", tk=512, tn=1024)
    dq_a, dk_a, dv_a, dsinks = _swa_bwd(sv["zq"], p["sinks"], sv["attn_a"], dattn_a, sv["lse_a"])
    big = dict(w_proj_a=_cols_to_shards(g_proj_a), w_proj_b=_cols_to_shards(g_proj_b),
               w_out=g_out.reshape(N_DEV, D_MODEL // N_DEV, D_MODEL), w_ffn_in=g_ffn_in,
               w_ffn_out=g_ffn_out.reshape(N_DEV, D_FF // N_DEV, D_MODEL))
    dox, stats = _fox_stats(sv["attn_b32"], dattn_b, sv["m_b"], sv["l_b"])
    dq_b, dk_b, dv_b, dcc, arrived = _fox_bwd(sv["qx"], sv["kx"], sv["vx"], dox, stats,
                                              None if make_comm is None else make_comm(big))
    df = _forget_bwd(dcc, sv["zg"])
    dz = jnp.concatenate([dq_a, dk_a.astype(BF16), dv_a.astype(BF16), dq_b.astype(BF16), dk_b, dv_b, dga, dgb, df,
                          jnp.zeros((s_len, F_PAD - LANE), BF16)], axis=1)
    g_in_t, g_b_in = _linear_tn(dz, sv["h"], name="g_w_in", tk=768, tn=1024, colsum=True)
    g_in_t, g_b_in = _unpad_z(g_in_t, 0), _unpad_z(g_b_in, 1)
    big["w_in"] = g_in_t.reshape(N_DEV, N_IN // N_DEV, D_MODEL)
    small = dict(ln_mix_g=d_mix_g, ln_mix_b=d_mix_b, b_in=g_b_in, attn_sinks=dsinks[:, :SWA_HEADS],
                 ln_ffn_g=d_ffn_g, ln_ffn_b=d_ffn_b, conv_w=dcw[:3], conv_b=dcw[3:4])
    dh, arrived_last = _d_h(dz, w["w_in_p"], du1, None if make_last_comm is None else make_last_comm(big, small))
    return dh, big, small, arrived, arrived_last


def _w_in_layouts(w_in):
    return dict(w_in_p=_pad_z(w_in.reshape(N_IN, D_MODEL), 0))


def _other_layouts(w_proj_a, w_proj_b, w_out, w_ffn_in, w_ffn_out):
    return dict(w_proj_a=_shards_to_cols(w_proj_a), w_proj_b=_shards_to_cols(w_proj_b),
                w_out=w_out.reshape(D_MODEL, D_MODEL), w_ffn_in=w_ffn_in,
                w_ffn_in_fwd=w_ffn_in.transpose(0, 2, 1),
                w_ffn_out=w_ffn_out.reshape(N_CHUNK, FF_CHUNK, D_MODEL))


def _layer_params(r):
    return dict(
        b_in_p=_pad_z(r["b_in"].reshape(1, N_IN), 1),
        sinks=jnp.pad(r["attn_sinks"].reshape(1, SWA_HEADS), ((0, 0), (0, LANE - SWA_HEADS))),
        ln_mix_g=r["ln_mix_g"].reshape(1, D_MODEL), ln_mix_b=r["ln_mix_b"].reshape(1, D_MODEL),
        ln_ffn_g=r["ln_ffn_g"].reshape(1, D_MODEL), ln_ffn_b=r["ln_ffn_b"].reshape(1, D_MODEL),
        conv_b=r["conv_b"].reshape(N_CHUNK, 1, FF_CHUNK))


def _conv_w_layout(conv_w):
    return jnp.pad(conv_w, ((0, 5), (0, 0))).reshape(8, N_CHUNK, FF_CHUNK).transpose(1, 0, 2)


def kernel(x, ln_mix_g, ln_mix_b, w_in, b_in, attn_sinks, w_proj_a, w_proj_b, w_out, ln_ffn_g, ln_ffn_b, w_ffn_in, conv_w, conv_b, w_ffn_out, loss_target, m_ln_mix_g, m_ln_mix_b, m_w_in, m_b_in, m_attn_sinks, m_w_proj_a, m_w_proj_b, m_w_out, m_ln_ffn_g, m_ln_ffn_b, m_w_ffn_in, m_conv_w, m_conv_b, m_w_ffn_out, v_ln_mix_g, v_ln_mix_b, v_w_in, v_b_in, v_attn_sinks, v_w_proj_a, v_w_proj_b, v_w_out, v_ln_ffn_g, v_ln_ffn_b, v_w_ffn_in, v_conv_w, v_conv_b, v_w_ffn_out):
    wts = dict(ln_mix_g=ln_mix_g, ln_mix_b=ln_mix_b, w_in=w_in, b_in=b_in, attn_sinks=attn_sinks, w_proj_a=w_proj_a,
               w_proj_b=w_proj_b, w_out=w_out, ln_ffn_g=ln_ffn_g, ln_ffn_b=ln_ffn_b, w_ffn_in=w_ffn_in,
               conv_w=conv_w, conv_b=conv_b, w_ffn_out=w_ffn_out)
    mom = dict(ln_mix_g=m_ln_mix_g, ln_mix_b=m_ln_mix_b, w_in=m_w_in, b_in=m_b_in, attn_sinks=m_attn_sinks,
               w_proj_a=m_w_proj_a, w_proj_b=m_w_proj_b, w_out=m_w_out, ln_ffn_g=m_ln_ffn_g, ln_ffn_b=m_ln_ffn_b,
               w_ffn_in=m_w_ffn_in, conv_w=m_conv_w, conv_b=m_conv_b, w_ffn_out=m_w_ffn_out)
    vel = dict(ln_mix_g=v_ln_mix_g, ln_mix_b=v_ln_mix_b, w_in=v_w_in, b_in=v_b_in, attn_sinks=v_attn_sinks,
               w_proj_a=v_w_proj_a, w_proj_b=v_w_proj_b, w_out=v_w_out, ln_ffn_g=v_ln_ffn_g, ln_ffn_b=v_ln_ffn_b,
               w_ffn_in=v_w_ffn_in, conv_w=v_conv_w, conv_b=v_conv_b, w_ffn_out=v_w_ffn_out)
    names = list(wts)
    big_names = [n for n, _, _ in BIG]
    small_names = [n for n, _ in SMALL]
    me = 4 * lax.axis_index("x") + 2 * lax.axis_index("y") + lax.axis_index("c")
    cw_shard = D_FF // N_DEV

    stored = {"w_in": ((2, 0, 1), (1, 2, 0)), "w_ffn_in": ((0, 2, 1), (0, 2, 1))}
    as_stored = lambda tree: {n: jnp.transpose(tree[n], stored[n][0]) if n in stored else tree[n] for n in big_names}
    w_st, m_st, v_st = as_stored(wts), as_stored(mom), as_stored(vel)
    wb = {n: [(w_st[n][:, l] if n == "w_in" else w_st[n][l]).astype(BF16) for l in range(DEPTH)] for n in big_names}
    ps = [_layer_params(dict(b_in=b_in[l], attn_sinks=attn_sinks[l], ln_mix_g=ln_mix_g[l], ln_mix_b=ln_mix_b[l],
                             ln_ffn_g=ln_ffn_g[l], ln_ffn_b=ln_ffn_b[l], conv_b=conv_b[l])) for l in range(DEPTH)]
    w_in_0 = _gather_two_level(wb["w_in"][0], "gather_w_in_0")
    others = big_names[1:]
    next_layer = {}

    def late_0(w, p, arrived):
        conv_full = arrived[-1].transpose(1, 2, 0, 3).reshape(DEPTH, 3, D_FF)
        next_layer["w"] = _w_in_layouts(arrived[len(others)])
        next_layer["p"] = dict(ps[1], conv_w=_conv_w_layout(conv_full[1]))
        return dict(w, **_other_layouts(*arrived[:len(others)])), dict(p, conv_w=_conv_w_layout(conv_full[0]))

    def late_1(w, p, arrived):
        return dict(w, **_other_layouts(*arrived)), p

    saved, ws = [None] * DEPTH, [None] * DEPTH
    gather_0 = _Comm([(wb[n][0], True) for n in others] + [(wb["w_in"][1], True), (conv_w, True)])
    (h,), saved[0], ws[0], ps[0] = _layer_fwd(x[0], _w_in_layouts(w_in_0), ps[0], gather_0, late_0)
    gather_1 = _Comm([(wb[n][1], True) for n in others])
    (dh, loss_part), saved[1], ws[1], ps[1] = _layer_fwd(h, next_layer["w"], next_layer["p"], gather_1, late_1,
                                                         loss_target[0])

    def small_rows(small):
        vec = jnp.concatenate([small[n].reshape(-1) for n in small_names] + [loss_part[0, 0].reshape(1)])
        return _to_rows(vec, SMALL_LAYER_ROWS)

    dh, big_1, small_1, _, _ = _layer_bwd(dh, saved[1], ws[1], ps[1])

    def exchange_early(big_0):
        return _Comm([(big_1[n].astype(BF16), False) for n in big_names] + [(small_rows(small_1), True)]
                     + [(big_0[n].astype(BF16), False) for n in others])

    def exchange_last(big_0, small_0):
        return _Comm([(big_0["w_in"].astype(BF16), False), (small_rows(small_0), True)])

    grad_x, _, _, arrived, (g_in_0, g_small_0) = _layer_bwd(dh, saved[0], ws[0], ps[0], exchange_early, exchange_last)
    n_big = len(big_names)
    recv = [[g_in_0] + list(arrived[n_big + 1:]) + [g_small_0], list(arrived[:n_big + 1])]

    big_out = {"w_in": _adamw_w_in([recv[l][0] for l in range(DEPTH)], w_st["w_in"], m_st["w_in"], v_st["w_in"])}
    for t, n in enumerate(big_names):
        if n == "w_in":
            continue
        big_out[n] = _adamw_shard([recv[l][t] for l in range(DEPTH)], w_st[n], m_st[n], v_st[n], "adamw_%s" % n)
    for n, (_, back) in stored.items():
        big_out[n] = [jnp.transpose(a, back) for a in big_out[n]]
    small_sum = [_sum8(recv[l][-1]).reshape(-1) for l in range(DEPTH)]
    g_small = {}
    off = 0
    for n, size in SMALL:
        g_small[n] = jnp.stack([small_sum[l][off:off + size] for l in range(DEPTH)])
        off += size
    loss = small_sum[0][off]
    g_small["conv_w"] = lax.dynamic_slice_in_dim(g_small["conv_w"].reshape(DEPTH, 3, D_FF), me * cw_shard, cw_shard,
                                                 axis=2)
    g_small = {n: g_small[n].reshape(wts[n].shape) for n in small_names}

    def pack_small(tree):
        return _to_rows(jnp.concatenate([tree[n].reshape(-1) for n in small_names]), SMALL_ROWS)

    small_out = (pack_small(g_small),) + tuple(_adamw_rows(pack_small(wts), pack_small(g_small), pack_small(mom),
                                                           pack_small(vel), "adamw_small"))

    def result(j):
        out = {n: big_out[n][j] for n in big_names}
        flat = small_out[j].reshape(-1)
        off = 0
        for n in small_names:
            out[n] = flat[off:off + wts[n].size].reshape(wts[n].shape)
            off += wts[n].size
        return [out[n] for n in names]

    return (loss, grad_x[None], *result(0), *result(1), *result(2), *result(3))
```

```python
import jax
import jax.numpy as jnp
import numpy as np
from jax import lax
from jax.experimental import pallas as pl
from jax.experimental.pallas import tpu as pltpu

F32 = jnp.float32
BF16 = jnp.bfloat16
MESH = pl.DeviceIdType.MESH

N_DEV = 8
DEPTH = 2
D_MODEL = 1024
HEAD_DIM = 64
SWA_Q = 512
SWA_KV = 128
FOX_W = 512
FOX_HEADS = 8
SWA_HEADS = 8
D_FF = 2816
N_IN = 4360
N_QKV = SWA_Q + 2 * SWA_KV + 3 * FOX_W
N_GATE = 2 * D_MODEL
F_PAD = 256
N_ZG = N_GATE + F_PAD
N_ZP = N_QKV + N_ZG
LN_EPS = 1e-5
NEG_INF = -1e30
ALPHA = (2 * DEPTH) ** 0.25
SCALE = HEAD_DIM ** -0.5
LOG2E = 1.4426950408889634
SLOPES = tuple(2.0 ** (-8.0 * (h + 1) / SWA_HEADS) for h in range(SWA_HEADS))

ADAM_LR = 0.001
ADAM_B1 = 0.9
ADAM_B2 = 0.999
ADAM_EPS = 1e-08
ADAM_WD = 0.01
ADAM_STEP = 10

LANE = 128
VMEM_LIMIT = 56 * 1024 * 1024

BIG = (("w_in", (D_MODEL, N_IN), 1), ("w_proj_a", (SWA_Q, D_MODEL), 1), ("w_proj_b", (FOX_W, D_MODEL), 1),
       ("w_out", (D_MODEL, D_MODEL), 0), ("w_ffn_in", (D_MODEL, 2 * D_FF), 1), ("w_ffn_out", (D_FF, D_MODEL), 0))
SMALL = (("ln_mix_g", D_MODEL), ("ln_mix_b", D_MODEL), ("b_in", N_IN), ("attn_sinks", SWA_HEADS),
         ("ln_ffn_g", D_MODEL), ("ln_ffn_b", D_MODEL), ("conv_w", 3 * D_FF), ("conv_b", D_FF))
ROW_BLOCK = 512
SMALL_LAYER_ROWS = -(-(sum(n for _, n in SMALL) + 1) // (8 * LANE)) * 8
SMALL_ROWS = ROW_BLOCK
FF_CHUNK = 2 * D_FF // N_DEV
N_CHUNK = D_FF // FF_CHUNK


def _div(n, cap, unit):
    if n <= cap:
        return n
    best = None
    for t in range(unit, cap + 1, unit):
        if n % t == 0:
            best = t
    assert best is not None, (n, cap, unit)
    return best


def _params(*sem):
    return pltpu.CompilerParams(dimension_semantics=sem, vmem_limit_bytes=VMEM_LIMIT)


def _peer(r):
    x, y, c = lax.axis_index("x"), lax.axis_index("y"), lax.axis_index("c")
    px = 1 - x if (r >> 2) & 1 else x
    py = 1 - y if (r >> 1) & 1 else y
    pc = 1 - c if r & 1 else c
    return (px, py, pc), 4 * px + 2 * py + pc


class _Comm:
    def __init__(self, tensors):
        self.arrays = [x for x, _ in tensors]
        self.gathers = [g for _, g in tensors]
        self.n = len(tensors)
        self.out_shape = [jax.ShapeDtypeStruct((N_DEV,) + (x.shape if g else x.shape[1:]), x.dtype)
                          for x, g in tensors]
        self.specs = [pl.BlockSpec(memory_space=pl.ANY)] * self.n
        self.scratch = [pltpu.SemaphoreType.DMA((N_DEV - 1, self.n)), pltpu.SemaphoreType.DMA((N_DEV - 1, self.n)),
                        pltpu.SemaphoreType.DMA((self.n,))]

    def _copies(self, x_refs, out_refs, sems):
        send_sems, recv_sems, local_sems = sems
        _, me = _peer(0)

        def src(t, idx):
            return x_refs[t] if self.gathers[t] else x_refs[t].at[idx]

        def remote(r, t, mine):
            peer, pid = _peer(r)
            return pltpu.make_async_remote_copy(src_ref=src(t, pid), dst_ref=out_refs[t].at[me if mine else pid],
                                                send_sem=send_sems.at[r - 1, t], recv_sem=recv_sems.at[r - 1, t],
                                                device_id=peer, device_id_type=MESH)

        pairs = [(r, t) for r in range(1, N_DEV) for t in range(self.n)]
        local = [pltpu.make_async_copy(src(t, me), out_refs[t].at[me], local_sems.at[t]) for t in range(self.n)]
        return local, [remote(r, t, True) for r, t in pairs], lambda: [remote(r, t, False) for r, t in pairs]

    def start(self, x_refs, out_refs, sems):
        local, sent, _ = self._copies(x_refs, out_refs, sems)
        for cp in local + sent:
            cp.start()

    def wait(self, x_refs, out_refs, sems):
        local, sent, landing = self._copies(x_refs, out_refs, sems)
        for cp in landing():
            cp.wait_recv()
        for cp in sent:
            cp.wait_send()
        for cp in local:
            cp.wait()


def _gather_two_level(x, name):
    def body(x_ref, out_ref, send_sems, recv_sems, local_sem):
        x_, y_, c_ = lax.axis_index("x"), lax.axis_index("y"), lax.axis_index("c")
        me, sibling = (x_, y_, c_), (x_, y_, 1 - c_)
        chips = [(1 - x_, y_), (x_, 1 - y_), (1 - x_, 1 - y_)]

        def slab(px, py, pc):
            return out_ref.at[4 * px + 2 * py + pc]

        def copy(k, block, to, src=None):
            return pltpu.make_async_remote_copy(src_ref=slab(*block) if src is None else src, dst_ref=slab(*block),
                                                send_sem=send_sems.at[k], recv_sem=recv_sems.at[k], device_id=to,
                                                device_id_type=MESH)

        mine = pltpu.make_async_copy(x_ref, slab(*me), local_sem)
        mine.start()
        first = [copy(0, me, sibling, src=x_ref)] + [copy(1 + j, me, (*chip, c_), src=x_ref)
                                                     for j, chip in enumerate(chips)]
        for cp in first:
            cp.start()
        passed = [copy(4 + j, (*chip, c_), sibling) for j, chip in enumerate(chips)]
        for j, chip in enumerate(chips):
            copy(1 + j, (*chip, c_), me).wait_recv()
            passed[j].start()
        copy(0, sibling, me).wait_recv()
        for j, chip in enumerate(chips):
            copy(4 + j, (*chip, 1 - c_), me).wait_recv()
        for cp in first + passed:
            cp.wait_send()
        mine.wait()

    spec = pl.BlockSpec(memory_space=pl.ANY)
    return pl.pallas_call(
        body, name=name, out_shape=jax.ShapeDtypeStruct((N_DEV,) + x.shape, x.dtype), in_specs=[spec], out_specs=spec,
        scratch_shapes=[pltpu.SemaphoreType.DMA((N_DEV - 1,)), pltpu.SemaphoreType.DMA((N_DEV - 1,)),
                        pltpu.SemaphoreType.DMA],
    )(x)


def _with_comm(comm, n_in, n_out, first, last, compute):
    nc = comm.n if comm is not None else 0

    def body(*refs):
        ins, x_refs = refs[:n_in], refs[n_in:n_in + nc]
        outs = refs[n_in + nc:n_in + nc + n_out]
        out_refs = refs[n_in + nc + n_out:n_in + 2 * nc + n_out]
        sems = refs[n_in + 2 * nc + n_out:]
        if nc:
            @pl.when(first())
            def _():
                comm.start(x_refs, out_refs, sems)

        compute(*ins, *outs)
        if nc:
            @pl.when(last())
            def _():
                comm.wait(x_refs, out_refs, sems)

    return body


def _d_h(dz, w_in_t, res, comm=None, tm=512):
    m, k = dz.shape
    d = w_in_t.shape[1]
    tm = _div(m, tm, 8)
    steps = m // tm
    c_specs, c_shapes, c_scratch, c_arrays = _comm_parts(comm)

    def compute(dz_ref, w_ref, res_ref, o_ref):
        o_ref[...] = ALPHA * res_ref[...] + jnp.dot(dz_ref[...], w_ref[...], preferred_element_type=F32)

    body = _with_comm(comm, 3, 1, lambda: pl.program_id(0) == 0, lambda: pl.program_id(0) == steps - 1, compute)
    row = pl.BlockSpec((tm, d), lambda i: (i, 0))
    outs = pl.pallas_call(
        body, name="d_h" if comm is None else "d_h_comm", grid=(steps,),
        in_specs=[pl.BlockSpec((tm, k), lambda i: (i, 0)), pl.BlockSpec((k, d), lambda i: (0, 0)), row] + c_specs,
        out_specs=[row] + c_specs, out_shape=[jax.ShapeDtypeStruct((m, d), F32)] + c_shapes,
        scratch_shapes=c_scratch,
        compiler_params=_params("arbitrary"),
    )(dz, w_in_t, res, *c_arrays)
    return outs[0], outs[1:]


def _z_proj(h, w_in_t, b_p, tm=512):
    m, k = h.shape
    tm = _div(m, tm, 8)
    nt = (((1,), (1,)), ((), ()))

    def body(h_ref, w_ref, b_ref, zq_ref, zg_ref, hb_ref):
        a = h_ref[...].astype(BF16)
        hb_ref[...] = a
        zq = lax.dot_general(a, w_ref[:N_QKV, :], nt, preferred_element_type=F32)
        zq_ref[...] = (zq + b_ref[:, :N_QKV]).astype(BF16)
        zg_ref[...] = lax.dot_general(a, w_ref[N_QKV:, :], nt, preferred_element_type=F32) + b_ref[:, N_QKV:]

    return pl.pallas_call(
        body, name="z_proj", grid=(m // tm,),
        in_specs=[pl.BlockSpec((tm, k), lambda i: (i, 0)), pl.BlockSpec((N_ZP, k), lambda i: (0, 0)),
                  pl.BlockSpec((1, N_ZP), lambda i: (0, 0))],
        out_specs=[pl.BlockSpec((tm, N_QKV), lambda i: (i, 0)), pl.BlockSpec((tm, N_ZG), lambda i: (i, 0)),
                   pl.BlockSpec((tm, k), lambda i: (i, 0))],
        out_shape=[jax.ShapeDtypeStruct((m, N_QKV), BF16), jax.ShapeDtypeStruct((m, N_ZG), F32),
                   jax.ShapeDtypeStruct((m, k), BF16)],
        compiler_params=_params("parallel"),
    )(h, w_in_t, b_p)


def _linear_tn(a, g, *, name, tk=1024, tn=640, tm=2048, colsum=False):
    m, k = a.shape
    n = g.shape[1]
    tk = _div(k, tk, LANE)
    tn = _div(n, tn, LANE)
    tm = _div(m, tm, 8)
    steps = m // tm
    assert not colsum or tn == n

    def body(a_ref, g_ref, o_ref, *rest):
        acc_ref = rest[-1]
        s = pl.program_id(2)

        @pl.when(s == 0)
        def _():
            acc_ref[...] = jnp.zeros_like(acc_ref)
            if colsum:
                rest[0][...] = jnp.zeros_like(rest[0])

        a_blk = a_ref[...]
        acc_ref[...] += lax.dot_general(a_blk.astype(BF16), g_ref[...].astype(BF16), (((0,), (0,)), ((), ())),
                                        preferred_element_type=F32)
        if colsum:
            rest[0][...] += jnp.sum(a_blk.astype(F32), axis=0, keepdims=True)

        @pl.when(s == steps - 1)
        def _():
            o_ref[...] = acc_ref[...].astype(BF16)

    out_specs = [pl.BlockSpec((tk, tn), lambda i, j, s: (i, j))]
    out_shape = [jax.ShapeDtypeStruct((k, n), BF16)]
    if colsum:
        out_specs.append(pl.BlockSpec((1, tk), lambda i, j, s: (0, i)))
        out_shape.append(jax.ShapeDtypeStruct((1, k), F32))
    outs = pl.pallas_call(
        body, name=name, grid=(k // tk, n // tn, steps),
        in_specs=[pl.BlockSpec((tm, tk), lambda i, j, s: (s, i)), pl.BlockSpec((tm, tn), lambda i, j, s: (s, j))],
        out_specs=out_specs, out_shape=out_shape,
        scratch_shapes=[pltpu.VMEM((tk, tn), F32)],
        compiler_params=_params("parallel", "parallel", "arbitrary"),
    )(a, g)
    return outs if colsum else outs[0]


def _ln(u, g, b):
    mu = jnp.mean(u, axis=-1, keepdims=True)
    d = u - mu
    var = jnp.mean(d * d, axis=-1, keepdims=True)
    return d * lax.rsqrt(var + LN_EPS) * g + b


def _ln_bwd_block(dy, u, g):
    mu = jnp.mean(u, axis=-1, keepdims=True)
    dd = u - mu
    rstd = lax.rsqrt(jnp.mean(dd * dd, axis=-1, keepdims=True) + LN_EPS)
    xhat = dd * rstd
    dxh = dy * g
    m1 = jnp.mean(dxh, axis=-1, keepdims=True)
    m2 = jnp.mean(dxh * xhat, axis=-1, keepdims=True)
    return (rstd * (dxh - m1 - xhat * m2), jnp.sum(dy * xhat, axis=0, keepdims=True),
            jnp.sum(dy, axis=0, keepdims=True))


SCAN_ROWS = 512


def _tri(n, upper):
    r = lax.broadcasted_iota(jnp.int32, (n, n), 0)
    c = lax.broadcasted_iota(jnp.int32, (n, n), 1)
    return jnp.where((c >= r) if upper else (c <= r), 1.0, 0.0).astype(F32)


def _cumsum_logf(zg):
    s = zg.shape[0]
    t = _div(s, SCAN_ROWS, LANE)
    nb = s // t
    fcol = N_GATE // LANE

    def body(f_ref, c_ref, carry_ref):
        @pl.when(pl.program_id(0) == 0)
        def _():
            carry_ref[...] = jnp.zeros_like(carry_ref)

        f = f_ref[...]
        logf = jnp.minimum(f, 0.0) - jnp.log(1.0 + jnp.exp(-jnp.abs(f)))
        c = jnp.dot(_tri(t, False), logf, precision=lax.Precision.HIGHEST, preferred_element_type=F32)
        c = c + carry_ref[0:1, :]
        c_ref[...] = c
        carry_ref[...] = jnp.broadcast_to(c[t - 1:t, :], carry_ref.shape)

    return pl.pallas_call(
        body, name="cumsum_logf", grid=(nb,),
        in_specs=[pl.BlockSpec((t, LANE), lambda i: (i, fcol))],
        out_specs=pl.BlockSpec((t, LANE), lambda i: (i, 0)),
        out_shape=jax.ShapeDtypeStruct((s, LANE), F32),
        scratch_shapes=[pltpu.VMEM((8, LANE), F32)],
        compiler_params=_params("arbitrary"),
    )(zg)


def _forget_bwd(dcc, zg):
    s = zg.shape[0]
    t = _div(s, SCAN_ROWS, LANE)
    nb = s // t
    fcol = N_GATE // LANE

    def body(dc_ref, f_ref, o_ref, carry_ref):
        @pl.when(pl.program_id(0) == 0)
        def _():
            carry_ref[...] = jnp.zeros_like(carry_ref)

        lane = lax.broadcasted_iota(jnp.int32, (1, LANE), 1)
        dc = jnp.zeros((t, LANE), F32)
        for p in range(FOX_HEADS // 2):
            tile = dc_ref[:, p * LANE:(p + 1) * LANE]
            moved = pltpu.roll(tile, 2 * p, 1) if p else tile
            dc = jnp.where((lane == 2 * p) | (lane == 2 * p + 1), moved, dc)
        dlogf = jnp.dot(_tri(t, True), dc, precision=lax.Precision.HIGHEST, preferred_element_type=F32)
        dlogf = dlogf + carry_ref[0:1, :]
        o_ref[...] = (dlogf * jax.nn.sigmoid(-f_ref[...])).astype(BF16)
        carry_ref[...] = jnp.broadcast_to(dlogf[0:1, :], carry_ref.shape)

    return pl.pallas_call(
        body, name="forget_bwd", grid=(nb,),
        in_specs=[pl.BlockSpec((t, FOX_W), lambda i: (nb - 1 - i, 0)),
                  pl.BlockSpec((t, LANE), lambda i: (nb - 1 - i, fcol))],
        out_specs=pl.BlockSpec((t, LANE), lambda i: (nb - 1 - i, 0)),
        out_shape=jax.ShapeDtypeStruct((s, LANE), BF16),
        scratch_shapes=[pltpu.VMEM((8, LANE), F32)],
        compiler_params=_params("arbitrary"),
    )(dcc, zg)


KA_COL = SWA_Q // LANE
VA_COL = KA_COL + 1


def _half_masks():
    lane = lax.broadcasted_iota(jnp.int32, (1, LANE), 1)
    hi = lane >= HEAD_DIM
    return (jnp.logical_not(hi), hi)


def _both_halves(x, sel):
    xs = jnp.where(sel, x, 0.0)
    return xs + pltpu.roll(xs, HEAD_DIM, 1)


SWA_PER_KV = 4
WIDE = SWA_PER_KV * LANE


def _swa_bias():
    k = np.arange(2 * LANE)[:, None]
    q = np.arange(LANE)[None, :]
    dist = (q + LANE - k).astype(np.float32)
    valid = (dist >= 0) & (dist < LANE)
    per_head = [np.where(valid, np.float32(-s) * dist, np.float32(NEG_INF)) for s in SLOPES]
    return jnp.asarray(np.stack([np.concatenate(per_head[SWA_PER_KV * hk:SWA_PER_KV * (hk + 1)], axis=1)
                                 for hk in range(2)]), F32)


def _no_previous_block(i_blk):
    k = lax.broadcasted_iota(jnp.int32, (2 * LANE, WIDE), 0)
    return jnp.where((i_blk == 0) & (k < LANE), NEG_INF, 0.0)


def _stack_heads(ref, blk, hk, halves, scale):
    tiles = []
    for j in range(SWA_PER_KV):
        p = 2 * hk + j // 2
        t = ref[blk, p * LANE:(p + 1) * LANE]
        if scale:
            t = _scaled(t)
        tiles.append(jnp.where(halves[j % 2], t, jnp.zeros_like(t)))
    return jnp.concatenate(tiles, axis=0)


def _pair_tile(wide, pp, row_halves):
    a = wide[:, (2 * pp) * LANE:(2 * pp + 1) * LANE]
    b = wide[:, (2 * pp + 1) * LANE:(2 * pp + 2) * LANE]
    return jnp.where(row_halves[0], a, b).T


def _lane_blocks(rows8, hk):
    return jnp.concatenate([rows8[SWA_PER_KV * hk + j:SWA_PER_KV * hk + j + 1, :] for j in range(SWA_PER_KV)], axis=1)


def _row_halves():
    hi = lax.broadcasted_iota(jnp.int32, (LANE, 1), 0) >= HEAD_DIM
    return (jnp.logical_not(hi), hi)


NT = (((1,), (1,)), ((), ()))


def _scaled(q):
    return (q.astype(F32) * SCALE).astype(BF16)


SWA_GROUP = 4


def _swa_group(s_len):
    return SWA_GROUP if (s_len // LANE) % SWA_GROUP == 0 else 1


def _swa_specs(group):
    rows = group * LANE
    prev = lambda i: jnp.maximum(i * group - 1, 0)
    return [pl.BlockSpec((rows, SWA_Q), lambda i: (i, 0)),
            pl.BlockSpec((rows, LANE), lambda i: (i, KA_COL)), pl.BlockSpec((rows, LANE), lambda i: (i, VA_COL)),
            pl.BlockSpec((LANE, LANE), lambda i: (prev(i), KA_COL)),
            pl.BlockSpec((LANE, LANE), lambda i: (prev(i), VA_COL))]


def _swa_window(g, cur_ref, prev_ref):
    before = prev_ref[...] if g == 0 else cur_ref[(g - 1) * LANE:g * LANE, :]
    return jnp.concatenate([before, cur_ref[g * LANE:(g + 1) * LANE, :]], axis=0).astype(F32)


def _swa_fwd(zq, sinks):
    s_len = zq.shape[0]
    group = _swa_group(s_len)
    rows = group * LANE
    sink_lanes = jnp.repeat(sinks[:, :SWA_HEADS], LANE, axis=1)

    def body(q_ref, kc_ref, vc_ref, kp_ref, vp_ref, sink_ref, bias_ref, o_ref, lse_ref):
        halves = _half_masks()
        row_halves = _row_halves()
        for g in range(group):
            blk = slice(g * LANE, (g + 1) * LANE)
            kcat = _swa_window(g, kc_ref, kp_ref)
            vcat = _swa_window(g, vc_ref, vp_ref)
            lse_rows = []
            for hk in range(2):
                kb = _both_halves(kcat, halves[hk]).astype(BF16)
                v_t = _both_halves(vcat, halves[hk]).T.astype(BF16)
                q4 = _stack_heads(q_ref, blk, hk, halves, True)
                s_t = lax.dot_general(kb, q4, NT, preferred_element_type=F32) + bias_ref[hk]
                if g == 0:
                    s_t = s_t + _no_previous_block(pl.program_id(0))
                sink = sink_ref[:, hk * WIDE:(hk + 1) * WIDE]
                m = jnp.maximum(jnp.max(s_t, axis=0, keepdims=True), sink)
                pe = jnp.exp(s_t - m)
                den = jnp.sum(pe, axis=0, keepdims=True) + jnp.exp(sink - m)
                out_t = jnp.dot(v_t, (pe * (1.0 / den)).astype(BF16), preferred_element_type=F32)
                for pp in range(2):
                    p = 2 * hk + pp
                    o_ref[blk, p * LANE:(p + 1) * LANE] = _pair_tile(out_t, pp, row_halves).astype(BF16)
                lse4 = m + jnp.log(den)
                lse_rows += [lse4[:, j * LANE:(j + 1) * LANE] for j in range(SWA_PER_KV)]
            lse_ref[:, blk] = jnp.concatenate(lse_rows, axis=0)

    return pl.pallas_call(
        body, name="swa_fwd", grid=(s_len // rows,),
        in_specs=_swa_specs(group) + [pl.BlockSpec((1, SWA_HEADS * LANE), lambda i: (0, 0)),
                                      pl.BlockSpec((2, 2 * LANE, WIDE), lambda i: (0, 0, 0))],
        out_specs=[pl.BlockSpec((rows, SWA_Q), lambda i: (i, 0)), pl.BlockSpec((SWA_HEADS, rows), lambda i: (0, i))],
        out_shape=[jax.ShapeDtypeStruct((s_len, SWA_Q), BF16), jax.ShapeDtypeStruct((SWA_HEADS, s_len), F32)],
        compiler_params=_params("parallel"),
    )(zq, zq, zq, zq, zq, sink_lanes, _swa_bias())


def _swa_bwd(zq, sinks, o, do, lse):
    s_len = zq.shape[0]
    group = _swa_group(s_len)
    rows = group * LANE

    def body(q_ref, kc_ref, vc_ref, kp_ref, vp_ref, sink_ref, bias_ref, o_ref, do_ref, lse_ref,
             dq_ref, dk_ref, dv_ref, ds_ref):
        halves = _half_masks()
        row_halves = _row_halves()
        lane = lax.broadcasted_iota(jnp.int32, (1, LANE), 1)
        dsink = jnp.zeros((1, LANE), F32)
        for g in range(group):
            blk = slice(g * LANE, (g + 1) * LANE)
            i_blk = pl.program_id(0) * group + g
            kcat = _swa_window(g, kc_ref, kp_ref)
            vcat = _swa_window(g, vc_ref, vp_ref)
            lse_rows = lse_ref[:, blk]
            prod = do_ref[blk, :].astype(F32) * o_ref[blk, :].astype(F32)
            select = (lax.broadcasted_iota(jnp.int32, (SWA_HEADS, SWA_Q), 1) // HEAD_DIM
                      == lax.broadcasted_iota(jnp.int32, (SWA_HEADS, SWA_Q), 0))
            picks = jnp.where(select, 1.0, 0.0).astype(BF16)
            prod_hi = prod.astype(BF16)
            prod_lo = (prod - prod_hi.astype(F32)).astype(BF16)
            delta_rows = (lax.dot_general(picks, prod_hi, NT, preferred_element_type=F32)
                          + lax.dot_general(picks, prod_lo, NT, preferred_element_type=F32))
            dk_tot = jnp.zeros((2 * LANE, LANE), F32)
            dv_tot = jnp.zeros((2 * LANE, LANE), F32)
            for hk in range(2):
                kb = _both_halves(kcat, halves[hk])
                k_t = kb.T.astype(BF16)
                kb = kb.astype(BF16)
                vb = _both_halves(vcat, halves[hk]).astype(BF16)
                q4 = _stack_heads(q_ref, blk, hk, halves, True)
                do4 = _stack_heads(do_ref, blk, hk, halves, False)
                lse4 = _lane_blocks(lse_rows, hk)
                delta4 = _lane_blocks(delta_rows, hk)
                s_t = lax.dot_general(kb, q4, NT, preferred_element_type=F32) + bias_ref[hk]
                if g == 0:
                    s_t = s_t + _no_previous_block(pl.program_id(0))
                p_t = jnp.exp(s_t - lse4)
                dp_t = lax.dot_general(vb, do4, NT, preferred_element_type=F32)
                ds_t = (p_t * (dp_t - delta4)).astype(BF16)
                sink_part = jnp.exp(sink_ref[:, hk * WIDE:(hk + 1) * WIDE] - lse4) * delta4
                for j in range(SWA_PER_KV):
                    dsink_h = -jnp.sum(sink_part[:, j * LANE:(j + 1) * LANE], axis=1, keepdims=True)
                    dsink = dsink + jnp.where(lane == SWA_PER_KV * hk + j, dsink_h, 0.0)
                dq_t = jnp.dot(k_t, ds_t, preferred_element_type=F32)
                for pp in range(2):
                    p = 2 * hk + pp
                    dq_ref[blk, p * LANE:(p + 1) * LANE] = (_pair_tile(dq_t, pp, row_halves) * SCALE).astype(BF16)
                dk_acc = jnp.dot(ds_t, q4, preferred_element_type=F32)
                dv_acc = jnp.dot(p_t.astype(BF16), do4, preferred_element_type=F32)
                dk_tot = dk_tot + jnp.where(halves[hk], dk_acc + pltpu.roll(dk_acc, HEAD_DIM, 1), 0.0)
                dv_tot = dv_tot + jnp.where(halves[hk], dv_acc + pltpu.roll(dv_acc, HEAD_DIM, 1), 0.0)
            cur = pl.ds(pl.multiple_of(i_blk * LANE, LANE), LANE)
            dk_ref[cur, :] = dk_tot[LANE:, :]
            dv_ref[cur, :] = dv_tot[LANE:, :]

            def add_previous(i_blk=i_blk, dk_tot=dk_tot, dv_tot=dv_tot):
                prv = pl.ds(pl.multiple_of((i_blk - 1) * LANE, LANE), LANE)
                dk_ref[prv, :] += dk_tot[:LANE, :]
                dv_ref[prv, :] += dv_tot[:LANE, :]

            if g == 0:
                pl.when(i_blk > 0)(add_previous)
            else:
                add_previous()

        @pl.when(pl.program_id(0) == 0)
        def _():
            ds_ref[...] = jnp.zeros_like(ds_ref)

        ds_ref[...] += dsink

    blk512 = pl.BlockSpec((rows, SWA_Q), lambda i: (i, 0))
    full = pl.BlockSpec((s_len, LANE), lambda i: (0, 0))
    vec = pl.BlockSpec((1, LANE), lambda i: (0, 0))
    return pl.pallas_call(
        body, name="swa_bwd", grid=(s_len // rows,),
        in_specs=_swa_specs(group) + [pl.BlockSpec((1, SWA_HEADS * LANE), lambda i: (0, 0)),
                                      pl.BlockSpec((2, 2 * LANE, WIDE), lambda i: (0, 0, 0)), blk512, blk512,
                                      pl.BlockSpec((SWA_HEADS, rows), lambda i: (0, i))],
        out_specs=[blk512, full, full, vec],
        out_shape=[jax.ShapeDtypeStruct((s_len, SWA_Q), BF16), jax.ShapeDtypeStruct((s_len, LANE), F32),
                   jax.ShapeDtypeStruct((s_len, LANE), F32), jax.ShapeDtypeStruct((1, LANE), F32)],
        compiler_params=_params("arbitrary"),
    )(zq, zq, zq, zq, zq, jnp.repeat(sinks[:, :SWA_HEADS], LANE, axis=1), _swa_bias(), o, do, lse)


QB_COL = (SWA_Q + 2 * SWA_KV) // LANE
KB_COL = QB_COL + FOX_W // LANE
VB_COL = KB_COL + FOX_W // LANE
N_PAIR = FOX_HEADS // 2


def _causal(t, keys_first=False):
    r = lax.broadcasted_iota(jnp.int32, (t, t), 0)
    c = lax.broadcasted_iota(jnp.int32, (t, t), 1)
    return c >= r if keys_first else r >= c


N_SPLIT = 3


def _own_half(e):
    hi = lax.broadcasted_iota(jnp.int32, (1, LANE), 1) >= HEAD_DIM
    return hi if e else jnp.logical_not(hi)


def _feature_lane(e, t):
    return HEAD_DIM * (1 - e) + t


def _feature_tables():
    wide = FOX_HEADS * LANE
    place_q, place_k = np.zeros((N_SPLIT * LANE, wide), np.float32), np.zeros((N_SPLIT * LANE, wide), np.float32)
    ones_q, ones_k, ones_v, own = (np.zeros((1, wide), np.float32) for _ in range(4))
    for h in range(FOX_HEADS):
        e = h % 2
        own[0, h * LANE + HEAD_DIM * e:h * LANE + HEAD_DIM * (e + 1)] = 1.0
        ones_v[0, h * LANE + _feature_lane(e, 0)] = 1.0
        for t in range(N_SPLIT):
            place_q[t * LANE + h, h * LANE + _feature_lane(e, t)] = 1.0
            ones_q[0, h * LANE + _feature_lane(e, N_SPLIT + t)] = 1.0
            ones_k[0, h * LANE + _feature_lane(e, t)] = 1.0
            place_k[t * LANE + h, h * LANE + _feature_lane(e, N_SPLIT + t)] = -1.0
    return tuple(jnp.asarray(a) for a in (place_q, place_k, ones_q, ones_k, ones_v, own))


def _fox_prep(zq, c, tm=512):
    s_len = zq.shape[0]
    tm = _div(s_len, tm, 8)
    wide = FOX_HEADS * LANE

    def body(z_ref, c_ref, pq_ref, pk_ref, oq_ref, ok_ref, ov_ref, own_ref, qx_ref, kx_ref, vx_ref):
        rest = c_ref[...]
        parts = []
        for _ in range(N_SPLIT):
            part = rest.astype(BF16).astype(F32)
            rest = rest - part
            parts.append(part)
        parts = jnp.concatenate(parts, axis=1)
        qf = jnp.dot(parts, pq_ref[...], preferred_element_type=F32) + oq_ref[...]
        kf = jnp.dot(parts, pk_ref[...], preferred_element_type=F32) + ok_ref[...]
        own = own_ref[...] > 0.5
        for p in range(N_PAIR):
            cols = slice(2 * p * LANE, (2 * p + 2) * LANE)
            pair = lambda col: jnp.tile(z_ref[:, (col + p) * LANE:(col + p + 1) * LANE].astype(F32), (1, 2))
            qx_ref[:, cols] = jnp.where(own[:, cols], pair(QB_COL) * SCALE, qf[:, cols]).astype(BF16)
            kx_ref[:, cols] = jnp.where(own[:, cols], pair(KB_COL), kf[:, cols]).astype(BF16)
            vx_ref[:, cols] = jnp.where(own[:, cols], pair(VB_COL), ov_ref[:, cols]).astype(BF16)

    out = jax.ShapeDtypeStruct((s_len, wide), BF16)
    blk = pl.BlockSpec((tm, wide), lambda i: (i, 0))
    table = pl.BlockSpec((N_SPLIT * LANE, wide), lambda i: (0, 0))
    vec = pl.BlockSpec((1, wide), lambda i: (0, 0))
    return pl.pallas_call(
        body, name="fox_prep", grid=(s_len // tm,),
        in_specs=[pl.BlockSpec((tm, N_QKV), lambda i: (i, 0)), pl.BlockSpec((tm, LANE), lambda i: (i, 0)),
                  table, table, vec, vec, vec, vec],
        out_specs=[blk, blk, blk], out_shape=[out, out, out],
        compiler_params=_params("parallel"),
    )(zq, c, *_feature_tables())


def _comm_parts(comm):
    return ([], [], [], []) if comm is None else (comm.specs, comm.out_shape, comm.scratch, comm.arrays)


def _fox_fwd(qx, kx, vx, comm=None, t_cap=1024):
    s_len = qx.shape[0]
    t = _div(s_len, t_cap, LANE)
    nq = s_len // t
    c_specs, c_shapes, c_scratch, c_arrays = _comm_parts(comm)

    def compute(q_ref, k_ref, v_ref, o_ref, o32_ref, m_ref, l_ref):
        i = pl.program_id(1)
        qs = [q_ref[:, e * LANE:(e + 1) * LANE] for e in range(2)]

        def step(j, carry, diag):
            rows = pl.ds(pl.multiple_of(j * t, t), t)
            new = []
            for e in range(2):
                m, acc = carry[e]
                s2 = lax.dot_general(qs[e], k_ref[rows, e * LANE:(e + 1) * LANE], NT,
                                     preferred_element_type=F32) * LOG2E
                if diag:
                    s2 = jnp.where(_causal(t), s2, NEG_INF)
                mn = jnp.maximum(m, jnp.ceil(jnp.max(s2, axis=1, keepdims=True)))
                pe = jnp.exp2(s2 - mn).astype(BF16)
                acc = acc * jnp.exp2(m - mn) + jnp.dot(pe, v_ref[rows, e * LANE:(e + 1) * LANE],
                                                       preferred_element_type=F32)
                new.append((mn, acc))
            return tuple(new)

        init = (jnp.full((t, 1), NEG_INF, F32), jnp.zeros((t, LANE), F32))
        carry = lax.fori_loop(0, i, lambda j, c: step(j, c, False), (init, init))
        carry = step(i, carry, True)
        outs, ls = [], []
        for e in range(2):
            m, acc = carry[e]
            l = acc[:, _feature_lane(e, 0):_feature_lane(e, 0) + 1]
            outs.append(acc / l)
            ls.append(l)
        out = jnp.where(_own_half(1), outs[1], outs[0])
        o_ref[...] = out.astype(BF16)
        o32_ref[...] = out
        m_ref[...] = jnp.where(_own_half(1), carry[1][0], carry[0][0])
        l_ref[...] = jnp.where(_own_half(1), ls[1], ls[0])

    body = _with_comm(comm, 3, 4, lambda: (pl.program_id(0) == 0) & (pl.program_id(1) == 0),
                      lambda: (pl.program_id(0) == N_PAIR - 1) & (pl.program_id(1) == nq - 1), compute)
    pair = pl.BlockSpec((s_len, 2 * LANE), lambda p, i: (0, p))
    tile = pl.BlockSpec((t, LANE), lambda p, i: (i, p))
    wide = jax.ShapeDtypeStruct((s_len, FOX_W), F32)
    outs = pl.pallas_call(
        body, name="fox_fwd" if comm is None else "fox_fwd_comm%d" % comm.n, grid=(N_PAIR, nq),
        in_specs=[pl.BlockSpec((t, 2 * LANE), lambda p, i: (i, p)), pair, pair] + c_specs,
        out_specs=[tile, tile, tile, tile] + c_specs,
        out_shape=[jax.ShapeDtypeStruct((s_len, FOX_W), BF16), wide, wide, wide] + c_shapes,
        scratch_shapes=c_scratch,
        compiler_params=_params("arbitrary", "arbitrary"),
    )(qx, kx, vx, *c_arrays)
    return outs[0], outs[1], outs[2], outs[3], outs[4:]


def _fox_stats(o, do, m, l, tm=512):
    s_len = o.shape[0]
    tm = _div(s_len, tm, LANE)

    def body(o_ref, do_ref, m_ref, l_ref, dox_ref, st_ref):
        lane = lax.broadcasted_iota(jnp.int32, (1, LANE), 1)
        for p in range(N_PAIR):
            cols = slice(p * LANE, (p + 1) * LANE)
            dout = do_ref[:, cols]
            prod = o_ref[:, cols] * dout.astype(F32)
            shift = m_ref[:, cols]
            inv_l = 1.0 / l_ref[:, cols]
            st = jnp.zeros((tm, LANE), F32)
            for e in range(2):
                h = 2 * p + e
                dox_ref[:, h * LANE:(h + 1) * LANE] = jnp.where(_own_half(e), dout, jnp.zeros_like(dout))
                st = jnp.where(lane == e, shift[:, e * HEAD_DIM:e * HEAD_DIM + 1], st)
                delta = jnp.sum(jnp.where(_own_half(e), prod, 0.0), axis=1, keepdims=True)
                st = jnp.where(lane == 2 + e, delta, st)
                st = jnp.where(lane == 4 + e, inv_l[:, e * HEAD_DIM:e * HEAD_DIM + 1], st)
            st_ref[p] = st.T[:8, :]

    row = pl.BlockSpec((tm, FOX_W), lambda i: (i, 0))
    return pl.pallas_call(
        body, name="fox_stats", grid=(s_len // tm,), in_specs=[row, row, row, row],
        out_specs=[pl.BlockSpec((tm, FOX_HEADS * LANE), lambda i: (i, 0)),
                   pl.BlockSpec((N_PAIR, 8, tm), lambda i: (0, 0, i))],
        out_shape=[jax.ShapeDtypeStruct((s_len, FOX_HEADS * LANE), BF16),
                   jax.ShapeDtypeStruct((N_PAIR, 8, s_len), F32)],
        compiler_params=_params("parallel"),
    )(o, do, m, l)


def _fox_bwd(qx, kx, vx, dox, stats, comm=None, t_cap=1024):
    s_len = qx.shape[0]
    t = _div(s_len, t_cap, LANE)
    n = s_len // t
    c_specs, c_shapes, c_scratch, c_arrays = _comm_parts(comm)

    def compute(q_ref, do_ref, st_ref, k_ref, v_ref, dq_ref, dk_ref, dv_ref, dc_ref):
        j = pl.program_id(1)
        lane = lax.broadcasted_iota(jnp.int32, (1, LANE), 1)

        @pl.when(j == 0)
        def _():
            dq_ref[...] = jnp.zeros_like(dq_ref)

        ks = [k_ref[:, e * LANE:(e + 1) * LANE] for e in range(2)]
        vs = [v_ref[:, e * LANE:(e + 1) * LANE] for e in range(2)]
        ks_t = [k.astype(F32).T.astype(BF16) for k in ks]

        def step(i, carry, diag):
            rows = pl.ds(pl.multiple_of(i * t, t), t)
            new = []
            dq = jnp.zeros((LANE, t), F32)
            for e in range(2):
                dk, dv, dc = carry[e]
                q = q_ref[rows, e * LANE:(e + 1) * LANE]
                dout = do_ref[rows, e * LANE:(e + 1) * LANE]
                s_t = lax.dot_general(ks[e], q, NT, preferred_element_type=F32) * LOG2E
                if diag:
                    s_t = jnp.where(_causal(t, keys_first=True), s_t, NEG_INF)
                p_t = jnp.exp2(s_t - st_ref[0, e:e + 1, rows]).astype(BF16).astype(F32) * st_ref[0, 4 + e:5 + e, rows]
                dp_t = lax.dot_general(vs[e], dout, NT, preferred_element_type=F32)
                ds_f = p_t * (dp_t - st_ref[0, 2 + e:3 + e, rows])
                ds_t = ds_f.astype(BF16)
                dc = dc + jnp.sum(ds_f, axis=1, keepdims=True)
                dv = dv + jnp.dot(p_t.astype(BF16), dout, preferred_element_type=F32)
                dk = dk + jnp.dot(ds_t, q, preferred_element_type=F32)
                dq_e = jnp.dot(ks_t[e], ds_t, preferred_element_type=F32)
                dq = dq + jnp.where(_row_halves()[e], dq_e, 0.0)
                new.append((dk, dv, dc))
            dq_ref[rows, :] += dq.T * SCALE
            return tuple(new)

        zero = jnp.zeros((t, LANE), F32)
        init = (zero, zero, jnp.zeros((t, 1), F32))
        carry = step(j, (init, init), True)
        (dk0, dv0, dc0), (dk1, dv1, dc1) = lax.fori_loop(j + 1, n, lambda i, c: step(i, c, False), carry)
        dk_ref[...] = jnp.where(_own_half(1), dk1, dk0).astype(BF16)
        dv_ref[...] = jnp.where(_own_half(1), dv1, dv0).astype(BF16)
        dc_ref[...] = jnp.where(lane == 0, -dc0, jnp.where(lane == 1, -dc1, 0.0))

    body = _with_comm(comm, 5, 4, lambda: (pl.program_id(0) == 0) & (pl.program_id(1) == 0),
                      lambda: (pl.program_id(0) == N_PAIR - 1) & (pl.program_id(1) == n - 1), compute)
    pair = pl.BlockSpec((s_len, 2 * LANE), lambda p, j: (0, p))
    blk = pl.BlockSpec((t, 2 * LANE), lambda p, j: (j, p))
    tile = pl.BlockSpec((t, LANE), lambda p, j: (j, p))
    outs = pl.pallas_call(
        body, name="fox_bwd" if comm is None else "fox_bwd_comm", grid=(N_PAIR, n),
        in_specs=[pair, pair, pl.BlockSpec((1, 8, s_len), lambda p, j: (p, 0, 0)), blk, blk] + c_specs,
        out_specs=[pl.BlockSpec((s_len, LANE), lambda p, j: (0, p)), tile, tile, tile] + c_specs,
        out_shape=[jax.ShapeDtypeStruct((s_len, FOX_W), F32), jax.ShapeDtypeStruct((s_len, FOX_W), BF16),
                   jax.ShapeDtypeStruct((s_len, FOX_W), BF16), jax.ShapeDtypeStruct((s_len, FOX_W), F32)] + c_shapes,
        scratch_shapes=c_scratch,
        compiler_params=_params("arbitrary", "arbitrary"),
    )(qx, dox, stats, kx, vx, *c_arrays)
    return outs[0], outs[1], outs[2], outs[3], outs[4:]


def _mixer_out(attn_a, attn_b, zg, h, wpa, wpb, wout, g, b, tm=512):
    m = h.shape[0]
    tm = _div(m, tm, 8)

    def body(a_ref, b_ref, ga_ref, gb_ref, h_ref, wpa_ref, wpb_ref, wout_ref, g_ref, bb_ref, h1_ref, u_ref, mg_ref,
             h1b_ref):
        ya = jnp.dot(a_ref[...], wpa_ref[...], preferred_element_type=F32)
        yb = jnp.dot(b_ref[...], wpb_ref[...], preferred_element_type=F32)
        merged = (jax.nn.sigmoid(ga_ref[...]) * ya + jax.nn.sigmoid(gb_ref[...]) * yb).astype(BF16)
        u = ALPHA * h_ref[...] + jnp.dot(merged, wout_ref[...], preferred_element_type=F32)
        u_ref[...] = u
        h1 = _ln(u, g_ref[...], bb_ref[...])
        h1_ref[...] = h1
        h1b_ref[...] = h1.astype(BF16)
        mg_ref[...] = merged

    row = pl.BlockSpec((tm, D_MODEL), lambda i: (i, 0))
    att = pl.BlockSpec((tm, SWA_Q), lambda i: (i, 0))
    vec = pl.BlockSpec((1, D_MODEL), lambda i: (0, 0))
    wsm = pl.BlockSpec((SWA_Q, D_MODEL), lambda i: (0, 0))
    return pl.pallas_call(
        body, name="mixer_out", grid=(m // tm,),
        in_specs=[att, att, row, pl.BlockSpec((tm, D_MODEL), lambda i: (i, 1)), row, wsm, wsm,
                  pl.BlockSpec((D_MODEL, D_MODEL), lambda i: (0, 0)), vec, vec],
        out_specs=[row, row, row, row],
        out_shape=[jax.ShapeDtypeStruct((m, D_MODEL), F32), jax.ShapeDtypeStruct((m, D_MODEL), F32),
                   jax.ShapeDtypeStruct((m, D_MODEL), BF16), jax.ShapeDtypeStruct((m, D_MODEL), BF16)],
        compiler_params=_params("parallel"),
    )(attn_a, attn_b, zg, zg, h, wpa, wpb, wout, g, b)


def _mixer_bwd(dh1, u1, g, wout, attn_a, attn_b, zg, wpa, wpb, tm=512):
    m = dh1.shape[0]
    tm = _div(m, tm, 8)

    def body(dh_ref, u_ref, g_ref, wout_ref, a_ref, b_ref, ga_ref, gb_ref, wpa_ref, wpb_ref,
             du_ref, dg_ref, db_ref, dya_ref, dyb_ref, dga_ref, dgb_ref, da_ref, dbb_ref):
        @pl.when(pl.program_id(0) == 0)
        def _():
            dg_ref[...] = jnp.zeros_like(dg_ref)
            db_ref[...] = jnp.zeros_like(db_ref)

        du, dg, db = _ln_bwd_block(dh_ref[...], u_ref[...], g_ref[...])
        du_ref[...] = du
        dg_ref[...] += dg
        db_ref[...] += db
        dm = lax.dot_general(du.astype(BF16), wout_ref[...], (((1,), (1,)), ((), ())), preferred_element_type=F32)
        for x_ref, gate_ref, w_ref, dy_ref, dgate_ref, dattn_ref in (
                (a_ref, ga_ref, wpa_ref, dya_ref, dga_ref, da_ref), (b_ref, gb_ref, wpb_ref, dyb_ref, dgb_ref, dbb_ref)):
            sg = jax.nn.sigmoid(gate_ref[...])
            dy = (dm * sg).astype(BF16)
            dy_ref[...] = dy
            y = jnp.dot(x_ref[...], w_ref[...], preferred_element_type=F32)
            dgate_ref[...] = (dm * y * sg * (1.0 - sg)).astype(BF16)
            dattn_ref[...] = lax.dot_general(dy, w_ref[...], (((1,), (1,)), ((), ())),
                                             preferred_element_type=F32).astype(BF16)

    row = pl.BlockSpec((tm, D_MODEL), lambda i: (i, 0))
    att = pl.BlockSpec((tm, SWA_Q), lambda i: (i, 0))
    vec = pl.BlockSpec((1, D_MODEL), lambda i: (0, 0))
    wsm = pl.BlockSpec((SWA_Q, D_MODEL), lambda i: (0, 0))
    wide = jax.ShapeDtypeStruct((m, D_MODEL), BF16)
    narrow = jax.ShapeDtypeStruct((m, SWA_Q), BF16)
    sums = jax.ShapeDtypeStruct((1, D_MODEL), F32)
    return pl.pallas_call(
        body, name="mixer_bwd", grid=(m // tm,),
        in_specs=[row, row, vec, pl.BlockSpec((D_MODEL, D_MODEL), lambda i: (0, 0)), att, att, row,
                  pl.BlockSpec((tm, D_MODEL), lambda i: (i, 1)), wsm, wsm],
        out_specs=[row, vec, vec, row, row, row, row, att, att],
        out_shape=[jax.ShapeDtypeStruct((m, D_MODEL), F32), sums, sums, wide, wide, wide, wide, narrow, narrow],
        compiler_params=_params("arbitrary"),
    )(dh1, u1, g, wout, attn_a, attn_b, zg, zg, wpa, wpb)


def _shift_down(x, k, halo, first):
    rows = lax.broadcasted_iota(jnp.int32, (x.shape[0], 1), 0)
    y = pltpu.roll(x, k, 0)
    for r in range(k):
        fill = jnp.where(first, 0.0, halo[8 - k + r:8 - k + r + 1, :])
        y = jnp.where(rows == r, fill, y)
    return y


def _shift_up(x, k, halo, last):
    n = x.shape[0]
    rows = lax.broadcasted_iota(jnp.int32, (n, 1), 0)
    y = pltpu.roll(x, n - k, 0)
    for r in range(k):
        fill = jnp.where(last, 0.0, halo[r:r + 1, :])
        y = jnp.where(rows == n - k + r, fill, y)
    return y


def _conv_act(gate, gate_m1, gate_m2, cw, cb):
    return cb + cw[0:1, :] * gate_m2 + cw[1:2, :] * gate_m1 + cw[2:3, :] * gate


def _ffn_in_conv(h1, wfi, cw, cb, tm=256):
    s_len = h1.shape[0]
    tm = _div(s_len, tm, 8)
    hb = tm // 8

    def body(a_ref, ap_ref, w_ref, cw_ref, cb_ref, gu_ref, act_ref):
        first = pl.program_id(0) == 0
        a = a_ref[...].astype(BF16)
        before = ap_ref[...].astype(BF16)
        for c in range(N_CHUNK):
            gate = jnp.dot(a, w_ref[c], preferred_element_type=F32)
            up = jnp.dot(a, w_ref[N_CHUNK + c], preferred_element_type=F32)
            halo = jnp.dot(before, w_ref[c], preferred_element_type=F32)
            gu_ref[c, 0] = gate
            gu_ref[c, 1] = up
            conv = _conv_act(gate, _shift_down(gate, 1, halo, first), _shift_down(gate, 2, halo, first),
                             cw_ref[c], cb_ref[c])
            act_ref[c] = (conv * jax.nn.sigmoid(conv) * up).astype(BF16)

    return pl.pallas_call(
        body, name="ffn_in_conv", grid=(s_len // tm,),
        in_specs=[pl.BlockSpec((tm, D_MODEL), lambda i: (i, 0)),
                  pl.BlockSpec((8, D_MODEL), lambda i: (jnp.maximum(i * hb - 1, 0), 0)),
                  pl.BlockSpec((N_DEV, D_MODEL, FF_CHUNK), lambda i: (0, 0, 0)),
                  pl.BlockSpec((N_CHUNK, 8, FF_CHUNK), lambda i: (0, 0, 0)),
                  pl.BlockSpec((N_CHUNK, 1, FF_CHUNK), lambda i: (0, 0, 0))],
        out_specs=[pl.BlockSpec((N_CHUNK, 2, tm, FF_CHUNK), lambda i: (0, 0, i, 0)),
                   pl.BlockSpec((N_CHUNK, tm, FF_CHUNK), lambda i: (0, i, 0))],
        out_shape=[jax.ShapeDtypeStruct((N_CHUNK, 2, s_len, FF_CHUNK), F32),
                   jax.ShapeDtypeStruct((N_CHUNK, s_len, FF_CHUNK), BF16)],
        compiler_params=_params("parallel"),
    )(h1, h1, wfi, cw, cb)


def _ffn_out_ln(act, wfo, res, g, b, target=None, tm=512):
    s_len = res.shape[0]
    tm = _div(s_len, tm, 8)
    last = target is not None

    def body(a_ref, w_ref, res_ref, g_ref, b_ref, *rest):
        u = ALPHA * res_ref[...]
        for c in range(N_CHUNK):
            u = u + jnp.dot(a_ref[c], w_ref[c], preferred_element_type=F32)
        y = _ln(u, g_ref[...], b_ref[...])
        if not last:
            u_ref, y_ref = rest
            u_ref[...] = u
            y_ref[...] = y
            return
        t_ref, u_ref, dy_ref, loss_ref = rest
        u_ref[...] = u

        @pl.when(pl.program_id(0) == 0)
        def _():
            loss_ref[...] = jnp.zeros_like(loss_ref)

        err = y - t_ref[...]
        dy_ref[...] = err / D_MODEL
        loss_ref[...] += 0.5 * jnp.sum(jnp.sum(err * err, axis=1, keepdims=True) / D_MODEL, axis=0, keepdims=True)

    row = pl.BlockSpec((tm, D_MODEL), lambda i: (i, 0))
    vec = pl.BlockSpec((1, D_MODEL), lambda i: (0, 0))
    wide = jax.ShapeDtypeStruct((s_len, D_MODEL), F32)
    return pl.pallas_call(
        body, name="ffn_out_ln_loss" if last else "ffn_out_ln", grid=(s_len // tm,),
        in_specs=[pl.BlockSpec((N_CHUNK, tm, FF_CHUNK), lambda i: (0, i, 0)),
                  pl.BlockSpec((N_CHUNK, FF_CHUNK, D_MODEL), lambda i: (0, 0, 0)), row, vec, vec] + [row] * last,
        out_specs=[row, row] + [pl.BlockSpec((8, LANE), lambda i: (0, 0))] * last,
        out_shape=[wide, wide] + [jax.ShapeDtypeStruct((8, LANE), F32)] * last,
        compiler_params=_params("arbitrary" if last else "parallel"),
    )(act, wfo, res, g, b, *([target] if last else []))


def _ffn_out_bwd(dh2, u2, g, wfo, tm=512):
    s_len = dh2.shape[0]
    tm = _div(s_len, tm, 8)

    def body(dh_ref, u_ref, g_ref, w_ref, du_ref, dg_ref, db_ref, o_ref):
        @pl.when(pl.program_id(0) == 0)
        def _():
            dg_ref[...] = jnp.zeros_like(dg_ref)
            db_ref[...] = jnp.zeros_like(db_ref)

        du, dg, db = _ln_bwd_block(dh_ref[...], u_ref[...], g_ref[...])
        du_ref[...] = du
        dg_ref[...] += dg
        db_ref[...] += db
        du_b = du.astype(BF16)
        for c in range(N_CHUNK):
            o_ref[c] = lax.dot_general(du_b, w_ref[c], (((1,), (1,)), ((), ())), preferred_element_type=F32)

    row = pl.BlockSpec((tm, D_MODEL), lambda i: (i, 0))
    vec = pl.BlockSpec((1, D_MODEL), lambda i: (0, 0))
    sums = jax.ShapeDtypeStruct((1, D_MODEL), F32)
    return pl.pallas_call(
        body, name="ffn_out_bwd", grid=(s_len // tm,),
        in_specs=[row, row, vec, pl.BlockSpec((N_CHUNK, FF_CHUNK, D_MODEL), lambda i: (0, 0, 0))],
        out_specs=[row, vec, vec, pl.BlockSpec((N_CHUNK, tm, FF_CHUNK), lambda i: (0, i, 0))],
        out_shape=[jax.ShapeDtypeStruct((s_len, D_MODEL), F32), sums, sums,
                   jax.ShapeDtypeStruct((N_CHUNK, s_len, FF_CHUNK), F32)],
        compiler_params=_params("arbitrary"),
    )(dh2, u2, g, wfo)


def _g_w_ffn_out(act, du, tm=2048):
    s_len = du.shape[0]
    tm = _div(s_len, tm, 8)
    steps = s_len // tm

    def body(a_ref, g_ref, o_ref, acc_ref):
        s = pl.program_id(1)

        @pl.when(s == 0)
        def _():
            acc_ref[...] = jnp.zeros_like(acc_ref)

        acc_ref[...] += lax.dot_general(a_ref[0], g_ref[...].astype(BF16), (((0,), (0,)), ((), ())),
                                        preferred_element_type=F32)

        @pl.when(s == steps - 1)
        def _():
            o_ref[0] = acc_ref[...].astype(BF16)

    return pl.pallas_call(
        body, name="g_w_ffn_out", grid=(N_CHUNK, steps),
        in_specs=[pl.BlockSpec((1, tm, FF_CHUNK), lambda c, s: (c, s, 0)),
                  pl.BlockSpec((tm, D_MODEL), lambda c, s: (s, 0))],
        out_specs=pl.BlockSpec((1, FF_CHUNK, D_MODEL), lambda c, s: (c, 0, 0)),
        out_shape=jax.ShapeDtypeStruct((N_CHUNK, FF_CHUNK, D_MODEL), BF16),
        scratch_shapes=[pltpu.VMEM((FF_CHUNK, D_MODEL), F32)],
        compiler_params=_params("parallel", "arbitrary"),
    )(act, du)


def _g_w_ffn_in(h1, dgu, tm=2048):
    s_len = h1.shape[0]
    tm = _div(s_len, tm, 8)
    steps = s_len // tm

    def body(a_ref, g_ref, o_ref, acc_ref):
        s = pl.program_id(1)

        @pl.when(s == 0)
        def _():
            acc_ref[...] = jnp.zeros_like(acc_ref)

        acc_ref[...] += lax.dot_general(g_ref[0, 0], a_ref[...].astype(BF16), (((0,), (0,)), ((), ())),
                                        preferred_element_type=F32)

        @pl.when(s == steps - 1)
        def _():
            o_ref[0] = acc_ref[...].astype(BF16)

    return pl.pallas_call(
        body, name="g_w_ffn_in", grid=(N_DEV, steps),
        in_specs=[pl.BlockSpec((tm, D_MODEL), lambda d, s: (s, 0)),
                  pl.BlockSpec((1, 1, tm, FF_CHUNK), lambda d, s: (d % N_CHUNK, d // N_CHUNK, s, 0))],
        out_specs=pl.BlockSpec((1, FF_CHUNK, D_MODEL), lambda d, s: (d, 0, 0)),
        out_shape=jax.ShapeDtypeStruct((N_DEV, FF_CHUNK, D_MODEL), BF16),
        scratch_shapes=[pltpu.VMEM((FF_CHUNK, D_MODEL), F32)],
        compiler_params=_params("parallel", "arbitrary"),
    )(h1, dgu)


def _conv_bwd_dh1(gu, dact, cw, cb, wfi, res, tm=256):
    s_len = gu.shape[2]
    tm = _div(s_len, tm, 8)
    nrow = s_len // tm
    hb = tm // 8

    def dconv_of(conv, sg, up, da):
        return da * up * (sg * (1.0 + conv * (1.0 - sg)))

    def body(gu_ref, gp_ref, gun_ref, da_ref, dan_ref, cw_ref, cb_ref, w_ref, res_ref, dgu_ref, dcw_ref, dh_ref):
        i = pl.program_id(0)
        first = i == 0
        last = i == nrow - 1

        @pl.when(first)
        def _():
            dcw_ref[...] = jnp.zeros_like(dcw_ref)

        row = lax.broadcasted_iota(jnp.int32, (8, 1), 0)
        acc = ALPHA * res_ref[...]
        for c in range(N_CHUNK):
            cw = cw_ref[c]
            cb = cb_ref[c]
            gate = gu_ref[c, 0]
            halo = gp_ref[c, 0]
            g_m1 = _shift_down(gate, 1, halo, first)
            g_m2 = _shift_down(gate, 2, halo, first)
            conv = _conv_act(gate, g_m1, g_m2, cw, cb)
            da = da_ref[c]
            sg = jax.nn.sigmoid(conv)
            dup = (da * conv * sg).astype(BF16)
            dconv = dconv_of(conv, sg, gu_ref[c, 1], da)
            gate_n = gun_ref[c, 0]
            tail = gate[tm - 8:, :]
            conv_n = _conv_act(gate_n, _shift_down(gate_n, 1, tail, False), _shift_down(gate_n, 2, tail, False),
                               cw, cb)
            dconv_n = dconv_of(conv_n, jax.nn.sigmoid(conv_n), gun_ref[c, 1], dan_ref[c])
            dgate = (cw[2:3, :] * dconv + cw[1:2, :] * _shift_up(dconv, 1, dconv_n, last)
                     + cw[0:1, :] * _shift_up(dconv, 2, dconv_n, last)).astype(BF16)
            dgu_ref[c, 0] = dgate
            dgu_ref[c, 1] = dup
            acc = acc + jnp.dot(dgate, w_ref[c], preferred_element_type=F32)
            acc = acc + jnp.dot(dup, w_ref[N_CHUNK + c], preferred_element_type=F32)
            part = jnp.zeros((8, FF_CHUNK), F32)
            for r, term in enumerate((dconv * g_m2, dconv * g_m1, dconv * gate, dconv)):
                part = jnp.where(row == r, jnp.sum(term, axis=0, keepdims=True), part)
            dcw_ref[c] += part
        dh_ref[...] = acc

    nxt = lambda i: jnp.minimum((i + 1) * hb, s_len // 8 - 1)
    main = pl.BlockSpec((N_CHUNK, 2, tm, FF_CHUNK), lambda i: (0, 0, i, 0))
    row_d = pl.BlockSpec((tm, D_MODEL), lambda i: (i, 0))
    return pl.pallas_call(
        body, name="conv_bwd_dh1", grid=(nrow,),
        in_specs=[main,
                  pl.BlockSpec((N_CHUNK, 1, 8, FF_CHUNK), lambda i: (0, 0, jnp.maximum(i * hb - 1, 0), 0)),
                  pl.BlockSpec((N_CHUNK, 2, 8, FF_CHUNK), lambda i: (0, 0, nxt(i), 0)),
                  pl.BlockSpec((N_CHUNK, tm, FF_CHUNK), lambda i: (0, i, 0)),
                  pl.BlockSpec((N_CHUNK, 8, FF_CHUNK), lambda i: (0, nxt(i), 0)),
                  pl.BlockSpec((N_CHUNK, 8, FF_CHUNK), lambda i: (0, 0, 0)),
                  pl.BlockSpec((N_CHUNK, 1, FF_CHUNK), lambda i: (0, 0, 0)),
                  pl.BlockSpec((N_DEV, FF_CHUNK, D_MODEL), lambda i: (0, 0, 0)), row_d],
        out_specs=[main, pl.BlockSpec((N_CHUNK, 8, FF_CHUNK), lambda i: (0, 0, 0)), row_d],
        out_shape=[jax.ShapeDtypeStruct((N_CHUNK, 2, s_len, FF_CHUNK), BF16),
                   jax.ShapeDtypeStruct((N_CHUNK, 8, FF_CHUNK), F32),
                   jax.ShapeDtypeStruct((s_len, D_MODEL), F32)],
        compiler_params=_params("arbitrary"),
    )(gu, gu, gu, dact, dact, cw, cb, wfi, res)


def _sum_devices(r_ref):
    acc = r_ref[0].astype(F32)
    for d in range(1, N_DEV):
        acc = acc + r_ref[d].astype(F32)
    return acc


def _sum8(recv):
    rows = recv.shape[1]
    tr = _div(rows, ROW_BLOCK, 8)

    def body(r_ref, o_ref):
        o_ref[...] = _sum_devices(r_ref)

    return pl.pallas_call(
        body, name="sum8", grid=(rows // tr,),
        in_specs=[pl.BlockSpec((N_DEV, tr, LANE), lambda i: (0, i, 0))],
        out_specs=pl.BlockSpec((tr, LANE), lambda i: (i, 0)),
        out_shape=jax.ShapeDtypeStruct((rows, LANE), F32),
        compiler_params=_params("parallel"),
    )(recv)


def _adamw_math(w, g, m, v):
    m = ADAM_B1 * m + (1.0 - ADAM_B1) * g
    v = ADAM_B2 * v + (1.0 - ADAM_B2) * (g * g)
    m_hat = m / (1.0 - ADAM_B1 ** ADAM_STEP)
    v_hat = v / (1.0 - ADAM_B2 ** ADAM_STEP)
    return -ADAM_LR * (m_hat / (jnp.sqrt(v_hat) + ADAM_EPS) + ADAM_WD * w), m, v


def _adamw_rows(w, g, m, v, name):
    rows = w.shape[0]
    tr = _div(rows, ROW_BLOCK, 8)

    def body(w_ref, g_ref, m_ref, v_ref, d_ref, mo_ref, vo_ref):
        d_ref[...], mo_ref[...], vo_ref[...] = _adamw_math(w_ref[...], g_ref[...], m_ref[...], v_ref[...])

    blk = pl.BlockSpec((tr, LANE), lambda i: (i, 0))
    out = jax.ShapeDtypeStruct((rows, LANE), F32)
    return pl.pallas_call(
        body, name=name, grid=(rows // tr,), in_specs=[blk, blk, blk, blk], out_specs=[blk, blk, blk],
        out_shape=[out, out, out], compiler_params=_params("parallel"),
    )(w, g, m, v)


def _adamw_w_in(recv, w, m, v, tl=128):
    n, depth, d = w.shape

    def body(*refs):
        r_refs, (w_ref, m_ref, v_ref), (g_ref, d_ref, mo_ref, vo_ref) = refs[:depth], refs[depth:depth + 3], refs[-4:]
        for l in range(depth):
            g = _sum_devices(r_refs[l])
            g_ref[:, l, :] = g
            d_ref[:, l, :], mo_ref[:, l, :], vo_ref[:, l, :] = _adamw_math(w_ref[:, l, :], g, m_ref[:, l, :],
                                                                            v_ref[:, l, :])

    blk = pl.BlockSpec((n, depth, tl), lambda j: (0, 0, j))
    out = jax.ShapeDtypeStruct((n, depth, d), F32)
    return pl.pallas_call(
        body, name="adamw_w_in", grid=(d // tl,),
        in_specs=[pl.BlockSpec((N_DEV, n, tl), lambda j: (0, 0, j))] * depth + [blk, blk, blk],
        out_specs=[blk, blk, blk, blk], out_shape=[out, out, out, out],
        compiler_params=_params("parallel"),
    )(*recv, w, m, v)


def _adamw_shard(recv, w, m, v, name):
    depth, k, n = w.shape
    tk = _div(k, 256, 16)

    def body(*refs):
        r_refs, (w_ref, m_ref, v_ref), (g_ref, d_ref, mo_ref, vo_ref) = refs[:depth], refs[depth:depth + 3], refs[-4:]
        for l in range(depth):
            g = _sum_devices(r_refs[l])
            g_ref[l] = g
            d_ref[l], mo_ref[l], vo_ref[l] = _adamw_math(w_ref[l], g, m_ref[l], v_ref[l])

    blk = pl.BlockSpec((depth, tk, n), lambda i: (0, i, 0))
    out = jax.ShapeDtypeStruct((depth, k, n), F32)
    return pl.pallas_call(
        body, name=name, grid=(k // tk,),
        in_specs=[pl.BlockSpec((N_DEV, tk, n), lambda i: (0, i, 0))] * depth + [blk, blk, blk],
        out_specs=[blk, blk, blk, blk], out_shape=[out, out, out, out],
        compiler_params=_params("parallel"),
    )(*recv, w, m, v)


def _to_rows(flat, rows):
    flat = flat.reshape(-1)
    return jnp.pad(flat, (0, rows * LANE - flat.shape[0])).reshape(rows, LANE)


def _pad_z(a, axis):
    f0 = N_QKV
    g0 = N_QKV + FOX_HEADS
    take = lambda lo, hi: lax.slice_in_dim(a, lo, hi, axis=axis)
    shape = list(a.shape)
    shape[axis] = F_PAD - FOX_HEADS
    return jnp.concatenate([take(0, f0), take(g0, N_IN), take(f0, g0), jnp.zeros(shape, a.dtype)], axis=axis)


def _unpad_z(a, axis):
    f0 = N_QKV + N_GATE
    take = lambda lo, hi: lax.slice_in_dim(a, lo, hi, axis=axis)
    return jnp.concatenate([take(0, N_QKV), take(f0, f0 + FOX_HEADS), take(N_QKV, f0)], axis=axis)


def _shards_to_cols(g):
    _, k, n = g.shape
    return g.transpose(1, 0, 2).reshape(k, N_DEV * n)


def _cols_to_shards(full):
    k, n = full.shape
    return full.reshape(k, N_DEV, n // N_DEV).transpose(1, 0, 2)


def _layer_fwd(h, w, p, comm=None, late=None, target=None):
    zq, zg, h_b = _z_proj(h, w["w_in_p"], p["b_in_p"])
    qx, kx, vx = _fox_prep(zq, _cumsum_logf(zg))
    attn_a, lse_a = _swa_fwd(zq, p["sinks"])
    attn_b, attn_b32, m_b, l_b, arrived = _fox_fwd(qx, kx, vx, comm)
    if late is not None:
        w, p = late(w, p, arrived)
    h1, u1, merged, h1_b = _mixer_out(attn_a, attn_b, zg, h, w["w_proj_a"], w["w_proj_b"], w["w_out"],
                                        p["ln_mix_g"], p["ln_mix_b"])
    gu, act = _ffn_in_conv(h1, w["w_ffn_in_fwd"], p["conv_w"], p["conv_b"])
    u2, *h2 = _ffn_out_ln(act, w["w_ffn_out"], h1, p["ln_ffn_g"], p["ln_ffn_b"], target)
    saved = dict(h_b=h_b, h1_b=h1_b, zq=zq, zg=zg, qx=qx, kx=kx, vx=vx, attn_a=attn_a, lse_a=lse_a, attn_b=attn_b,
                 attn_b32=attn_b32, m_b=m_b, l_b=l_b, h1=h1, u1=u1, merged=merged, gu=gu, act=act, u2=u2)
    return h2, saved, w, p


def _layer_bwd(dh2, sv, w, p, make_comm=None, make_last_comm=None):
    s_len = dh2.shape[0]
    du2, d_ffn_g, d_ffn_b, dact = _ffn_out_bwd(dh2, sv["u2"], p["ln_ffn_g"], w["w_ffn_out"])
    g_ffn_out = _g_w_ffn_out(sv["act"], du2)
    dgu, dcw, dh1 = _conv_bwd_dh1(sv["gu"], dact, p["conv_w"], p["conv_b"], w["w_ffn_in"], du2)
    dcw = dcw.transpose(1, 0, 2).reshape(8, D_FF)
    g_ffn_in = _g_w_ffn_in(sv["h1_b"], dgu)
    du1, d_mix_g, d_mix_b, dya, dyb, dga, dgb, dattn_a, dattn_b = _mixer_bwd(
        dh1, sv["u1"], p["ln_mix_g"], w["w_out"], sv["attn_a"], sv["attn_b"], sv["zg"], w["w_proj_a"], w["w_proj_b"])
    g_out = _linear_tn(sv["merged"], du1, name="g_w_out", tn=1024)
    g_proj_a = _linear_tn(sv["attn_a"], dya, name="g_w_proj_a", tk=512, tn=1024)
    g_proj_b = _linear_tn(sv["attn_b"], dyb, name="g_w_proj_b", tk=512, tn=1024)
    dq_a, dk_a, dv_a, dsinks = _swa_bwd(sv["zq"], p["sinks"], sv["attn_a"], dattn_a, sv["lse_a"])
    big = dict(w_proj_a=_cols_to_shards(g_proj_a), w_proj_b=_cols_to_shards(g_proj_b),
               w_out=g_out.reshape(N_DEV, D_MODEL // N_DEV, D_MODEL), w_ffn_in=g_ffn_in,
               w_ffn_out=g_ffn_out.reshape(N_DEV, D_FF // N_DEV, D_MODEL))
    dox, stats = _fox_stats(sv["attn_b32"], dattn_b, sv["m_b"], sv["l_b"])
    dq_b, dk_b, dv_b, dcc, arrived = _fox_bwd(sv["qx"], sv["kx"], sv["vx"], dox, stats,
                                              None if make_comm is None else make_comm(big))
    df = _forget_bwd(dcc, sv["zg"])
    dz = jnp.concatenate([dq_a, dk_a.astype(BF16), dv_a.astype(BF16), dq_b.astype(BF16), dk_b, dv_b, dga, dgb, df,
                          jnp.zeros((s_len, F_PAD - LANE), BF16)], axis=1)
    g_in_t, g_b_in = _linear_tn(dz, sv["h_b"], name="g_w_in", tk=768, tn=1024, colsum=True)
    g_in_t, g_b_in = _unpad_z(g_in_t, 0), _unpad_z(g_b_in, 1)
    big["w_in"] = g_in_t.reshape(N_DEV, N_IN // N_DEV, D_MODEL)
    small = dict(ln_mix_g=d_mix_g, ln_mix_b=d_mix_b, b_in=g_b_in, attn_sinks=dsinks[:, :SWA_HEADS],
                 ln_ffn_g=d_ffn_g, ln_ffn_b=d_ffn_b, conv_w=dcw[:3], conv_b=dcw[3:4])
    dh, arrived_last = _d_h(dz, w["w_in_p"], du1, None if make_last_comm is None else make_last_comm(big, small))
    return dh, big, small, arrived, arrived_last


def _w_in_layouts(w_in):
    return dict(w_in_p=_pad_z(w_in.reshape(N_IN, D_MODEL), 0))


def _other_layouts(w_proj_a, w_proj_b, w_out, w_ffn_in, w_ffn_out):
    return dict(w_proj_a=_shards_to_cols(w_proj_a), w_proj_b=_shards_to_cols(w_proj_b),
                w_out=w_out.reshape(D_MODEL, D_MODEL), w_ffn_in=w_ffn_in,
                w_ffn_in_fwd=w_ffn_in.transpose(0, 2, 1),
                w_ffn_out=w_ffn_out.reshape(N_CHUNK, FF_CHUNK, D_MODEL))


def _layer_params(r):
    return dict(
        b_in_p=_pad_z(r["b_in"].reshape(1, N_IN), 1),
        sinks=jnp.pad(r["attn_sinks"].reshape(1, SWA_HEADS), ((0, 0), (0, LANE - SWA_HEADS))),
        ln_mix_g=r["ln_mix_g"].reshape(1, D_MODEL), ln_mix_b=r["ln_mix_b"].reshape(1, D_MODEL),
        ln_ffn_g=r["ln_ffn_g"].reshape(1, D_MODEL), ln_ffn_b=r["ln_ffn_b"].reshape(1, D_MODEL),
        conv_b=r["conv_b"].reshape(N_CHUNK, 1, FF_CHUNK))


def _conv_w_layout(conv_w):
    return jnp.pad(conv_w, ((0, 5), (0, 0))).reshape(8, N_CHUNK, FF_CHUNK).transpose(1, 0, 2)


def kernel(x, ln_mix_g, ln_mix_b, w_in, b_in, attn_sinks, w_proj_a, w_proj_b, w_out, ln_ffn_g, ln_ffn_b, w_ffn_in, conv_w, conv_b, w_ffn_out, loss_target, m_ln_mix_g, m_ln_mix_b, m_w_in, m_b_in, m_attn_sinks, m_w_proj_a, m_w_proj_b, m_w_out, m_ln_ffn_g, m_ln_ffn_b, m_w_ffn_in, m_conv_w, m_conv_b, m_w_ffn_out, v_ln_mix_g, v_ln_mix_b, v_w_in, v_b_in, v_attn_sinks, v_w_proj_a, v_w_proj_b, v_w_out, v_ln_ffn_g, v_ln_ffn_b, v_w_ffn_in, v_conv_w, v_conv_b, v_w_ffn_out):
    wts = dict(ln_mix_g=ln_mix_g, ln_mix_b=ln_mix_b, w_in=w_in, b_in=b_in, attn_sinks=attn_sinks, w_proj_a=w_proj_a,
               w_proj_b=w_proj_b, w_out=w_out, ln_ffn_g=ln_ffn_g, ln_ffn_b=ln_ffn_b, w_ffn_in=w_ffn_in,
               conv_w=conv_w, conv_b=conv_b, w_ffn_out=w_ffn_out)
    mom = dict(ln_mix_g=m_ln_mix_g, ln_mix_b=m_ln_mix_b, w_in=m_w_in, b_in=m_b_in, attn_sinks=m_attn_sinks,
               w_proj_a=m_w_proj_a, w_proj_b=m_w_proj_b, w_out=m_w_out, ln_ffn_g=m_ln_ffn_g, ln_ffn_b=m_ln_ffn_b,
               w_ffn_in=m_w_ffn_in, conv_w=m_conv_w, conv_b=m_conv_b, w_ffn_out=m_w_ffn_out)
    vel = dict(ln_mix_g=v_ln_mix_g, ln_mix_b=v_ln_mix_b, w_in=v_w_in, b_in=v_b_in, attn_sinks=v_attn_sinks,
               w_proj_a=v_w_proj_a, w_proj_b=v_w_proj_b, w_out=v_w_out, ln_ffn_g=v_ln_ffn_g, ln_ffn_b=v_ln_ffn_b,
               w_ffn_in=v_w_ffn_in, conv_w=v_conv_w, conv_b=v_conv_b, w_ffn_out=v_w_ffn_out)
    names = list(wts)
    big_names = [n for n, _, _ in BIG]
    small_names = [n for n, _ in SMALL]
    me = 4 * lax.axis_index("x") + 2 * lax.axis_index("y") + lax.axis_index("c")
    cw_shard = D_FF // N_DEV

    stored = {"w_in": ((2, 0, 1), (1, 2, 0)), "w_ffn_in": ((0, 2, 1), (0, 2, 1))}
    as_stored = lambda tree: {n: jnp.transpose(tree[n], stored[n][0]) if n in stored else tree[n] for n in big_names}
    w_st, m_st, v_st = as_stored(wts), as_stored(mom), as_stored(vel)
    wb = {n: [(w_st[n][:, l] if n == "w_in" else w_st[n][l]).astype(BF16) for l in range(DEPTH)] for n in big_names}
    ps = [_layer_params(dict(b_in=b_in[l], attn_sinks=attn_sinks[l], ln_mix_g=ln_mix_g[l], ln_mix_b=ln_mix_b[l],
                             ln_ffn_g=ln_ffn_g[l], ln_ffn_b=ln_ffn_b[l], conv_b=conv_b[l])) for l in range(DEPTH)]
    w_in_0 = _gather_two_level(wb["w_in"][0], "gather_w_in_0")
    others = big_names[1:]
    next_layer = {}

    def late_0(w, p, arrived):
        conv_full = arrived[-1].transpose(1, 2, 0, 3).reshape(DEPTH, 3, D_FF)
        next_layer["w"] = _w_in_layouts(arrived[len(others)])
        next_layer["p"] = dict(ps[1], conv_w=_conv_w_layout(conv_full[1]))
        return dict(w, **_other_layouts(*arrived[:len(others)])), dict(p, conv_w=_conv_w_layout(conv_full[0]))

    def late_1(w, p, arrived):
        return dict(w, **_other_layouts(*arrived)), p

    saved, ws = [None] * DEPTH, [None] * DEPTH
    gather_0 = _Comm([(wb[n][0], True) for n in others] + [(wb["w_in"][1], True), (conv_w, True)])
    (h,), saved[0], ws[0], ps[0] = _layer_fwd(x[0], _w_in_layouts(w_in_0), ps[0], gather_0, late_0)
    gather_1 = _Comm([(wb[n][1], True) for n in others])
    (dh, loss_part), saved[1], ws[1], ps[1] = _layer_fwd(h, next_layer["w"], next_layer["p"], gather_1, late_1,
                                                         loss_target[0])

    def small_rows(small):
        vec = jnp.concatenate([small[n].reshape(-1) for n in small_names] + [loss_part[0, 0].reshape(1)])
        return _to_rows(vec, SMALL_LAYER_ROWS)

    dh, big_1, small_1, _, _ = _layer_bwd(dh, saved[1], ws[1], ps[1])

    def exchange_early(big_0):
        return _Comm([(big_1[n].astype(BF16), False) for n in big_names] + [(small_rows(small_1), True)]
                     + [(big_0[n].astype(BF16), False) for n in others])

    def exchange_last(big_0, small_0):
        return _Comm([(big_0["w_in"].astype(BF16), False), (small_rows(small_0), True)])

    grad_x, _, _, arrived, (g_in_0, g_small_0) = _layer_bwd(dh, saved[0], ws[0], ps[0], exchange_early, exchange_last)
    n_big = len(big_names)
    recv = [[g_in_0] + list(arrived[n_big + 1:]) + [g_small_0], list(arrived[:n_big + 1])]

    big_out = {"w_in": _adamw_w_in([recv[l][0] for l in range(DEPTH)], w_st["w_in"], m_st["w_in"], v_st["w_in"])}
    for t, n in enumerate(big_names):
        if n == "w_in":
            continue
        big_out[n] = _adamw_shard([recv[l][t] for l in range(DEPTH)], w_st[n], m_st[n], v_st[n], "adamw_%s" % n)
    for n, (_, back) in stored.items():
        big_out[n] = [jnp.transpose(a, back) for a in big_out[n]]
    small_sum = [_sum8(recv[l][-1]).reshape(-1) for l in range(DEPTH)]
    g_small = {}
    off = 0
    for n, size in SMALL:
        g_small[n] = jnp.stack([small_sum[l][off:off + size] for l in range(DEPTH)])
        off += size
    loss = small_sum[0][off]
    g_small["conv_w"] = lax.dynamic_slice_in_dim(g_small["conv_w"].reshape(DEPTH, 3, D_FF), me * cw_shard, cw_shard,
                                                 axis=2)
    g_small = {n: g_small[n].reshape(wts[n].shape) for n in small_names}

    def pack_small(tree):
        return _to_rows(jnp.concatenate([tree[n].reshape(-1) for n in small_names]), SMALL_ROWS)

    small_out = (pack_small(g_small),) + tuple(_adamw_rows(pack_small(wts), pack_small(g_small), pack_small(mom),
                                                           pack_small(vel), "adamw_small"))

    def result(j):
        out = {n: big_out[n][j] for n in big_names}
        flat = small_out[j].reshape(-1)
        off = 0
        for n in small_names:
            out[n] = flat[off:off + wts[n].size].reshape(wts[n].shape)
            off += wts[n].size
        return [out[n] for n in names]

    return (loss, grad_x[None], *result(0), *result(1), *result(2), *result(3))
```

```python
import jax
import jax.numpy as jnp
import numpy as np
from jax import lax
from jax.experimental import pallas as pl
from jax.experimental.pallas import tpu as pltpu

F32 = jnp.float32
BF16 = jnp.bfloat16
MESH = pl.DeviceIdType.MESH

N_DEV = 8
DEPTH = 2
D_MODEL = 1024
HEAD_DIM = 64
SWA_Q = 512
SWA_KV = 128
FOX_W = 512
FOX_HEADS = 8
SWA_HEADS = 8
D_FF = 2816
N_IN = 4360
N_QKV = SWA_Q + 2 * SWA_KV + 3 * FOX_W
N_GATE = 2 * D_MODEL
F_PAD = 256
N_ZG = N_GATE + F_PAD
N_ZP = N_QKV + N_ZG
LN_EPS = 1e-5
NEG_INF = -1e30
ALPHA = (2 * DEPTH) ** 0.25
SCALE = HEAD_DIM ** -0.5
LOG2E = 1.4426950408889634
SLOPES = tuple(2.0 ** (-8.0 * (h + 1) / SWA_HEADS) for h in range(SWA_HEADS))

ADAM_LR = 0.001
ADAM_B1 = 0.9
ADAM_B2 = 0.999
ADAM_EPS = 1e-08
ADAM_WD = 0.01
ADAM_STEP = 10

LANE = 128
VMEM_LIMIT = 56 * 1024 * 1024

BIG = (("w_in", (D_MODEL, N_IN), 1), ("w_proj_a", (SWA_Q, D_MODEL), 1), ("w_proj_b", (FOX_W, D_MODEL), 1),
       ("w_out", (D_MODEL, D_MODEL), 0), ("w_ffn_in", (D_MODEL, 2 * D_FF), 1), ("w_ffn_out", (D_FF, D_MODEL), 0))
SMALL = (("ln_mix_g", D_MODEL), ("ln_mix_b", D_MODEL), ("b_in", N_IN), ("attn_sinks", SWA_HEADS),
         ("ln_ffn_g", D_MODEL), ("ln_ffn_b", D_MODEL), ("conv_w", 3 * D_FF), ("conv_b", D_FF))
ROW_BLOCK = 512
SMALL_LAYER_ROWS = -(-(sum(n for _, n in SMALL) + 1) // (8 * LANE)) * 8
SMALL_ROWS = ROW_BLOCK
FF_CHUNK = 2 * D_FF // N_DEV
N_CHUNK = D_FF // FF_CHUNK


def _div(n, cap, unit):
    if n <= cap:
        return n
    best = None
    for t in range(unit, cap + 1, unit):
        if n % t == 0:
            best = t
    assert best is not None, (n, cap, unit)
    return best


def _params(*sem):
    return pltpu.CompilerParams(dimension_semantics=sem, vmem_limit_bytes=VMEM_LIMIT)


def _peer(r):
    x, y, c = lax.axis_index("x"), lax.axis_index("y"), lax.axis_index("c")
    px = 1 - x if (r >> 2) & 1 else x
    py = 1 - y if (r >> 1) & 1 else y
    pc = 1 - c if r & 1 else c
    return (px, py, pc), 4 * px + 2 * py + pc


class _Comm:
    def __init__(self, tensors):
        self.arrays = [x for x, _ in tensors]
        self.gathers = [g for _, g in tensors]
        self.n = len(tensors)
        self.out_shape = [jax.ShapeDtypeStruct((N_DEV,) + (x.shape if g else x.shape[1:]), x.dtype)
                          for x, g in tensors]
        self.specs = [pl.BlockSpec(memory_space=pl.ANY)] * self.n
        self.scratch = [pltpu.SemaphoreType.DMA((N_DEV - 1, self.n)), pltpu.SemaphoreType.DMA((N_DEV - 1, self.n)),
                        pltpu.SemaphoreType.DMA((self.n,))]

    def _copies(self, x_refs, out_refs, sems):
        send_sems, recv_sems, local_sems = sems
        _, me = _peer(0)

        def src(t, idx):
            return x_refs[t] if self.gathers[t] else x_refs[t].at[idx]

        def remote(r, t, mine):
            peer, pid = _peer(r)
            return pltpu.make_async_remote_copy(src_ref=src(t, pid), dst_ref=out_refs[t].at[me if mine else pid],
                                                send_sem=send_sems.at[r - 1, t], recv_sem=recv_sems.at[r - 1, t],
                                                device_id=peer, device_id_type=MESH)

        pairs = [(r, t) for r in range(1, N_DEV) for t in range(self.n)]
        local = [pltpu.make_async_copy(src(t, me), out_refs[t].at[me], local_sems.at[t]) for t in range(self.n)]
        return local, [remote(r, t, True) for r, t in pairs], lambda: [remote(r, t, False) for r, t in pairs]

    def start(self, x_refs, out_refs, sems):
        local, sent, _ = self._copies(x_refs, out_refs, sems)
        for cp in local + sent:
            cp.start()

    def wait(self, x_refs, out_refs, sems):
        local, sent, landing = self._copies(x_refs, out_refs, sems)
        for cp in landing():
            cp.wait_recv()
        for cp in sent:
            cp.wait_send()
        for cp in local:
            cp.wait()


def _gather_two_level(x, name):
    def body(x_ref, out_ref, send_sems, recv_sems, local_sem):
        x_, y_, c_ = lax.axis_index("x"), lax.axis_index("y"), lax.axis_index("c")
        me, sibling = (x_, y_, c_), (x_, y_, 1 - c_)
        chips = [(1 - x_, y_), (x_, 1 - y_), (1 - x_, 1 - y_)]

        def slab(px, py, pc):
            return out_ref.at[4 * px + 2 * py + pc]

        def copy(k, block, to, src=None):
            return pltpu.make_async_remote_copy(src_ref=slab(*block) if src is None else src, dst_ref=slab(*block),
                                                send_sem=send_sems.at[k], recv_sem=recv_sems.at[k], device_id=to,
                                                device_id_type=MESH)

        mine = pltpu.make_async_copy(x_ref, slab(*me), local_sem)
        mine.start()
        first = [copy(0, me, sibling, src=x_ref)] + [copy(1 + j, me, (*chip, c_), src=x_ref)
                                                     for j, chip in enumerate(chips)]
        for cp in first:
            cp.start()
        passed = [copy(4 + j, (*chip, c_), sibling) for j, chip in enumerate(chips)]
        for j, chip in enumerate(chips):
            copy(1 + j, (*chip, c_), me).wait_recv()
            passed[j].start()
        copy(0, sibling, me).wait_recv()
        for j, chip in enumerate(chips):
            copy(4 + j, (*chip, 1 - c_), me).wait_recv()
        for cp in first + passed:
            cp.wait_send()
        mine.wait()

    spec = pl.BlockSpec(memory_space=pl.ANY)
    return pl.pallas_call(
        body, name=name, out_shape=jax.ShapeDtypeStruct((N_DEV,) + x.shape, x.dtype), in_specs=[spec], out_specs=spec,
        scratch_shapes=[pltpu.SemaphoreType.DMA((N_DEV - 1,)), pltpu.SemaphoreType.DMA((N_DEV - 1,)),
                        pltpu.SemaphoreType.DMA],
    )(x)


def _with_comm(comm, n_in, n_out, first, last, compute):
    nc = comm.n if comm is not None else 0

    def body(*refs):
        ins, x_refs = refs[:n_in], refs[n_in:n_in + nc]
        outs = refs[n_in + nc:n_in + nc + n_out]
        out_refs = refs[n_in + nc + n_out:n_in + 2 * nc + n_out]
        sems = refs[n_in + 2 * nc + n_out:]
        if nc:
            @pl.when(first())
            def _():
                comm.start(x_refs, out_refs, sems)

        compute(*ins, *outs)
        if nc:
            @pl.when(last())
            def _():
                comm.wait(x_refs, out_refs, sems)

    return body


def _d_h(dz, w_in_t, res, comm=None, tm=512):
    m, k = dz.shape
    d = w_in_t.shape[1]
    tm = _div(m, tm, 8)
    steps = m // tm
    c_specs, c_shapes, c_scratch, c_arrays = _comm_parts(comm)

    def compute(dz_ref, w_ref, res_ref, o_ref):
        o_ref[...] = ALPHA * res_ref[...] + jnp.dot(dz_ref[...], w_ref[...], preferred_element_type=F32)

    body = _with_comm(comm, 3, 1, lambda: pl.program_id(0) == 0, lambda: pl.program_id(0) == steps - 1, compute)
    row = pl.BlockSpec((tm, d), lambda i: (i, 0))
    outs = pl.pallas_call(
        body, name="d_h" if comm is None else "d_h_comm", grid=(steps,),
        in_specs=[pl.BlockSpec((tm, k), lambda i: (i, 0)), pl.BlockSpec((k, d), lambda i: (0, 0)), row] + c_specs,
        out_specs=[row] + c_specs, out_shape=[jax.ShapeDtypeStruct((m, d), F32)] + c_shapes,
        scratch_shapes=c_scratch,
        compiler_params=_params("arbitrary"),
    )(dz, w_in_t, res, *c_arrays)
    return outs[0], outs[1:]


def _z_proj(h, w_in_t, b_p, tm=512):
    m, k = h.shape
    tm = _div(m, tm, 8)
    nt = (((1,), (1,)), ((), ()))

    def body(h_ref, w_ref, b_ref, zq_ref, zg_ref, hb_ref):
        a = h_ref[...].astype(BF16)
        hb_ref[...] = a
        zq = lax.dot_general(a, w_ref[:N_QKV, :], nt, preferred_element_type=F32)
        zq_ref[...] = (zq + b_ref[:, :N_QKV]).astype(BF16)
        zg_ref[...] = lax.dot_general(a, w_ref[N_QKV:, :], nt, preferred_element_type=F32) + b_ref[:, N_QKV:]

    return pl.pallas_call(
        body, name="z_proj", grid=(m // tm,),
        in_specs=[pl.BlockSpec((tm, k), lambda i: (i, 0)), pl.BlockSpec((N_ZP, k), lambda i: (0, 0)),
                  pl.BlockSpec((1, N_ZP), lambda i: (0, 0))],
        out_specs=[pl.BlockSpec((tm, N_QKV), lambda i: (i, 0)), pl.BlockSpec((tm, N_ZG), lambda i: (i, 0)),
                   pl.BlockSpec((tm, k), lambda i: (i, 0))],
        out_shape=[jax.ShapeDtypeStruct((m, N_QKV), BF16), jax.ShapeDtypeStruct((m, N_ZG), F32),
                   jax.ShapeDtypeStruct((m, k), BF16)],
        compiler_params=_params("parallel"),
    )(h, w_in_t, b_p)


def _linear_tn(a, g, *, name, tk=1024, tn=640, tm=2048, colsum=False):
    m, k = a.shape
    n = g.shape[1]
    tk = _div(k, tk, LANE)
    tn = _div(n, tn, LANE)
    tm = _div(m, tm, 8)
    steps = m // tm
    assert not colsum or tn == n

    def body(a_ref, g_ref, o_ref, *rest):
        acc_ref = rest[-1]
        s = pl.program_id(2)

        @pl.when(s == 0)
        def _():
            acc_ref[...] = jnp.zeros_like(acc_ref)
            if colsum:
                rest[0][...] = jnp.zeros_like(rest[0])

        a_blk = a_ref[...]
        acc_ref[...] += lax.dot_general(a_blk.astype(BF16), g_ref[...].astype(BF16), (((0,), (0,)), ((), ())),
                                        preferred_element_type=F32)
        if colsum:
            rest[0][...] += jnp.sum(a_blk.astype(F32), axis=0, keepdims=True)

        @pl.when(s == steps - 1)
        def _():
            o_ref[...] = acc_ref[...].astype(BF16)

    out_specs = [pl.BlockSpec((tk, tn), lambda i, j, s: (i, j))]
    out_shape = [jax.ShapeDtypeStruct((k, n), BF16)]
    if colsum:
        out_specs.append(pl.BlockSpec((1, tk), lambda i, j, s: (0, i)))
        out_shape.append(jax.ShapeDtypeStruct((1, k), F32))
    outs = pl.pallas_call(
        body, name=name, grid=(k // tk, n // tn, steps),
        in_specs=[pl.BlockSpec((tm, tk), lambda i, j, s: (s, i)), pl.BlockSpec((tm, tn), lambda i, j, s: (s, j))],
        out_specs=out_specs, out_shape=out_shape,
        scratch_shapes=[pltpu.VMEM((tk, tn), F32)],
        compiler_params=_params("parallel", "parallel", "arbitrary"),
    )(a, g)
    return outs if colsum else outs[0]


def _ln(u, g, b):
    mu = jnp.mean(u, axis=-1, keepdims=True)
    d = u - mu
    var = jnp.mean(d * d, axis=-1, keepdims=True)
    return d * lax.rsqrt(var + LN_EPS) * g + b


def _ln_bwd_block(dy, u, g):
    mu = jnp.mean(u, axis=-1, keepdims=True)
    dd = u - mu
    rstd = lax.rsqrt(jnp.mean(dd * dd, axis=-1, keepdims=True) + LN_EPS)
    xhat = dd * rstd
    dxh = dy * g
    m1 = jnp.mean(dxh, axis=-1, keepdims=True)
    m2 = jnp.mean(dxh * xhat, axis=-1, keepdims=True)
    return (rstd * (dxh - m1 - xhat * m2), jnp.sum(dy * xhat, axis=0, keepdims=True),
            jnp.sum(dy, axis=0, keepdims=True))


SCAN_ROWS = 512


def _tri(n, upper):
    r = lax.broadcasted_iota(jnp.int32, (n, n), 0)
    c = lax.broadcasted_iota(jnp.int32, (n, n), 1)
    return jnp.where((c >= r) if upper else (c <= r), 1.0, 0.0).astype(F32)


def _cumsum_logf(zg):
    s = zg.shape[0]
    t = _div(s, SCAN_ROWS, LANE)
    nb = s // t
    fcol = N_GATE // LANE

    def body(f_ref, c_ref, carry_ref):
        @pl.when(pl.program_id(0) == 0)
        def _():
            carry_ref[...] = jnp.zeros_like(carry_ref)

        f = f_ref[...]
        logf = jnp.minimum(f, 0.0) - jnp.log(1.0 + jnp.exp(-jnp.abs(f)))
        c = jnp.dot(_tri(t, False), logf, precision=lax.Precision.HIGHEST, preferred_element_type=F32)
        c = c + carry_ref[0:1, :]
        c_ref[...] = c
        carry_ref[...] = jnp.broadcast_to(c[t - 1:t, :], carry_ref.shape)

    return pl.pallas_call(
        body, name="cumsum_logf", grid=(nb,),
        in_specs=[pl.BlockSpec((t, LANE), lambda i: (i, fcol))],
        out_specs=pl.BlockSpec((t, LANE), lambda i: (i, 0)),
        out_shape=jax.ShapeDtypeStruct((s, LANE), F32),
        scratch_shapes=[pltpu.VMEM((8, LANE), F32)],
        compiler_params=_params("arbitrary"),
    )(zg)


def _forget_bwd(dcc, zg):
    s = zg.shape[0]
    t = _div(s, SCAN_ROWS, LANE)
    nb = s // t
    fcol = N_GATE // LANE

    def body(dc_ref, f_ref, o_ref, carry_ref):
        @pl.when(pl.program_id(0) == 0)
        def _():
            carry_ref[...] = jnp.zeros_like(carry_ref)

        lane = lax.broadcasted_iota(jnp.int32, (1, LANE), 1)
        dc = jnp.zeros((t, LANE), F32)
        for p in range(FOX_HEADS // 2):
            tile = dc_ref[:, p * LANE:(p + 1) * LANE]
            moved = pltpu.roll(tile, 2 * p, 1) if p else tile
            dc = jnp.where((lane == 2 * p) | (lane == 2 * p + 1), moved, dc)
        dlogf = jnp.dot(_tri(t, True), dc, precision=lax.Precision.HIGHEST, preferred_element_type=F32)
        dlogf = dlogf + carry_ref[0:1, :]
        o_ref[...] = (dlogf * jax.nn.sigmoid(-f_ref[...])).astype(BF16)
        carry_ref[...] = jnp.broadcast_to(dlogf[0:1, :], carry_ref.shape)

    return pl.pallas_call(
        body, name="forget_bwd", grid=(nb,),
        in_specs=[pl.BlockSpec((t, FOX_W), lambda i: (nb - 1 - i, 0)),
                  pl.BlockSpec((t, LANE), lambda i: (nb - 1 - i, fcol))],
        out_specs=pl.BlockSpec((t, LANE), lambda i: (nb - 1 - i, 0)),
        out_shape=jax.ShapeDtypeStruct((s, LANE), BF16),
        scratch_shapes=[pltpu.VMEM((8, LANE), F32)],
        compiler_params=_params("arbitrary"),
    )(dcc, zg)


KA_COL = SWA_Q // LANE
VA_COL = KA_COL + 1


def _half_masks():
    lane = lax.broadcasted_iota(jnp.int32, (1, LANE), 1)
    hi = lane >= HEAD_DIM
    return (jnp.logical_not(hi), hi)


def _both_halves(x, sel):
    xs = jnp.where(sel, x, 0.0)
    return xs + pltpu.roll(xs, HEAD_DIM, 1)


SWA_PER_KV = 4
WIDE = SWA_PER_KV * LANE


def _swa_bias():
    k = np.arange(2 * LANE)[:, None]
    q = np.arange(LANE)[None, :]
    dist = (q + LANE - k).astype(np.float32)
    valid = (dist >= 0) & (dist < LANE)
    per_head = [np.where(valid, np.float32(-s) * dist, np.float32(NEG_INF)) for s in SLOPES]
    return jnp.asarray(np.stack([np.concatenate(per_head[SWA_PER_KV * hk:SWA_PER_KV * (hk + 1)], axis=1)
                                 for hk in range(2)]), F32)


def _no_previous_block(i_blk):
    k = lax.broadcasted_iota(jnp.int32, (2 * LANE, WIDE), 0)
    return jnp.where((i_blk == 0) & (k < LANE), NEG_INF, 0.0)


def _stack_heads(ref, blk, hk, halves, scale):
    tiles = []
    for j in range(SWA_PER_KV):
        p = 2 * hk + j // 2
        t = ref[blk, p * LANE:(p + 1) * LANE]
        if scale:
            t = _scaled(t)
        tiles.append(jnp.where(halves[j % 2], t, jnp.zeros_like(t)))
    return jnp.concatenate(tiles, axis=0)


def _pair_tile(wide, pp, row_halves):
    a = wide[:, (2 * pp) * LANE:(2 * pp + 1) * LANE]
    b = wide[:, (2 * pp + 1) * LANE:(2 * pp + 2) * LANE]
    return jnp.where(row_halves[0], a, b).T


def _lane_blocks(rows8, hk):
    return jnp.concatenate([rows8[SWA_PER_KV * hk + j:SWA_PER_KV * hk + j + 1, :] for j in range(SWA_PER_KV)], axis=1)


def _row_halves():
    hi = lax.broadcasted_iota(jnp.int32, (LANE, 1), 0) >= HEAD_DIM
    return (jnp.logical_not(hi), hi)


NT = (((1,), (1,)), ((), ()))


def _scaled(q):
    return (q.astype(F32) * SCALE).astype(BF16)


SWA_GROUP = 4


def _swa_group(s_len):
    return SWA_GROUP if (s_len // LANE) % SWA_GROUP == 0 else 1


def _swa_specs(group):
    rows = group * LANE
    prev = lambda i: jnp.maximum(i * group - 1, 0)
    return [pl.BlockSpec((rows, SWA_Q), lambda i: (i, 0)),
            pl.BlockSpec((rows, LANE), lambda i: (i, KA_COL)), pl.BlockSpec((rows, LANE), lambda i: (i, VA_COL)),
            pl.BlockSpec((LANE, LANE), lambda i: (prev(i), KA_COL)),
            pl.BlockSpec((LANE, LANE), lambda i: (prev(i), VA_COL))]


def _swa_window(g, cur_ref, prev_ref):
    before = prev_ref[...] if g == 0 else cur_ref[(g - 1) * LANE:g * LANE, :]
    return jnp.concatenate([before, cur_ref[g * LANE:(g + 1) * LANE, :]], axis=0).astype(F32)


def _swa_fwd(zq, sinks):
    s_len = zq.shape[0]
    group = _swa_group(s_len)
    rows = group * LANE
    sink_lanes = jnp.repeat(sinks[:, :SWA_HEADS], LANE, axis=1)

    def body(q_ref, kc_ref, vc_ref, kp_ref, vp_ref, sink_ref, bias_ref, o_ref, lse_ref):
        halves = _half_masks()
        row_halves = _row_halves()
        for g in range(group):
            blk = slice(g * LANE, (g + 1) * LANE)
            kcat = _swa_window(g, kc_ref, kp_ref)
            vcat = _swa_window(g, vc_ref, vp_ref)
            lse_rows = []
            for hk in range(2):
                kb = _both_halves(kcat, halves[hk]).astype(BF16)
                v_t = _both_halves(vcat, halves[hk]).T.astype(BF16)
                q4 = _stack_heads(q_ref, blk, hk, halves, True)
                s_t = lax.dot_general(kb, q4, NT, preferred_element_type=F32) + bias_ref[hk]
                if g == 0:
                    s_t = s_t + _no_previous_block(pl.program_id(0))
                sink = sink_ref[:, hk * WIDE:(hk + 1) * WIDE]
                m = jnp.maximum(jnp.max(s_t, axis=0, keepdims=True), sink)
                pe = jnp.exp(s_t - m)
                den = jnp.sum(pe, axis=0, keepdims=True) + jnp.exp(sink - m)
                out_t = jnp.dot(v_t, (pe * (1.0 / den)).astype(BF16), preferred_element_type=F32)
                for pp in range(2):
                    p = 2 * hk + pp
                    o_ref[blk, p * LANE:(p + 1) * LANE] = _pair_tile(out_t, pp, row_halves).astype(BF16)
                lse4 = m + jnp.log(den)
                lse_rows += [lse4[:, j * LANE:(j + 1) * LANE] for j in range(SWA_PER_KV)]
            lse_ref[:, blk] = jnp.concatenate(lse_rows, axis=0)

    return pl.pallas_call(
        body, name="swa_fwd", grid=(s_len // rows,),
        in_specs=_swa_specs(group) + [pl.BlockSpec((1, SWA_HEADS * LANE), lambda i: (0, 0)),
                                      pl.BlockSpec((2, 2 * LANE, WIDE), lambda i: (0, 0, 0))],
        out_specs=[pl.BlockSpec((rows, SWA_Q), lambda i: (i, 0)), pl.BlockSpec((SWA_HEADS, rows), lambda i: (0, i))],
        out_shape=[jax.ShapeDtypeStruct((s_len, SWA_Q), BF16), jax.ShapeDtypeStruct((SWA_HEADS, s_len), F32)],
        compiler_params=_params("parallel"),
    )(zq, zq, zq, zq, zq, sink_lanes, _swa_bias())


def _swa_bwd(zq, sinks, o, do, lse):
    s_len = zq.shape[0]
    group = _swa_group(s_len)
    rows = group * LANE

    def body(q_ref, kc_ref, vc_ref, kp_ref, vp_ref, sink_ref, bias_ref, o_ref, do_ref, lse_ref,
             dq_ref, dk_ref, dv_ref, ds_ref):
        halves = _half_masks()
        row_halves = _row_halves()
        lane = lax.broadcasted_iota(jnp.int32, (1, LANE), 1)
        dsink = jnp.zeros((1, LANE), F32)
        for g in range(group):
            blk = slice(g * LANE, (g + 1) * LANE)
            i_blk = pl.program_id(0) * group + g
            kcat = _swa_window(g, kc_ref, kp_ref)
            vcat = _swa_window(g, vc_ref, vp_ref)
            lse_rows = lse_ref[:, blk]
            prod = do_ref[blk, :].astype(F32) * o_ref[blk, :].astype(F32)
            select = (lax.broadcasted_iota(jnp.int32, (SWA_HEADS, SWA_Q), 1) // HEAD_DIM
                      == lax.broadcasted_iota(jnp.int32, (SWA_HEADS, SWA_Q), 0))
            picks = jnp.where(select, 1.0, 0.0).astype(BF16)
            prod_hi = prod.astype(BF16)
            prod_lo = (prod - prod_hi.astype(F32)).astype(BF16)
            delta_rows = (lax.dot_general(picks, prod_hi, NT, preferred_element_type=F32)
                          + lax.dot_general(picks, prod_lo, NT, preferred_element_type=F32))
            dk_tot = jnp.zeros((2 * LANE, LANE), F32)
            dv_tot = jnp.zeros((2 * LANE, LANE), F32)
            for hk in range(2):
                kb = _both_halves(kcat, halves[hk])
                k_t = kb.T.astype(BF16)
                kb = kb.astype(BF16)
                vb = _both_halves(vcat, halves[hk]).astype(BF16)
                q4 = _stack_heads(q_ref, blk, hk, halves, True)
                do4 = _stack_heads(do_ref, blk, hk, halves, False)
                lse4 = _lane_blocks(lse_rows, hk)
                delta4 = _lane_blocks(delta_rows, hk)
                s_t = lax.dot_general(kb, q4, NT, preferred_element_type=F32) + bias_ref[hk]
                if g == 0:
                    s_t = s_t + _no_previous_block(pl.program_id(0))
                p_t = jnp.exp(s_t - lse4)
                dp_t = lax.dot_general(vb, do4, NT, preferred_element_type=F32)
                ds_t = (p_t * (dp_t - delta4)).astype(BF16)
                sink_part = jnp.exp(sink_ref[:, hk * WIDE:(hk + 1) * WIDE] - lse4) * delta4
                for j in range(SWA_PER_KV):
                    dsink_h = -jnp.sum(sink_part[:, j * LANE:(j + 1) * LANE], axis=1, keepdims=True)
                    dsink = dsink + jnp.where(lane == SWA_PER_KV * hk + j, dsink_h, 0.0)
                dq_t = jnp.dot(k_t, ds_t, preferred_element_type=F32)
                for pp in range(2):
                    p = 2 * hk + pp
                    dq_ref[blk, p * LANE:(p + 1) * LANE] = (_pair_tile(dq_t, pp, row_halves) * SCALE).astype(BF16)
                dk_acc = jnp.dot(ds_t, q4, preferred_element_type=F32)
                dv_acc = jnp.dot(p_t.astype(BF16), do4, preferred_element_type=F32)
                dk_tot = dk_tot + jnp.where(halves[hk], dk_acc + pltpu.roll(dk_acc, HEAD_DIM, 1), 0.0)
                dv_tot = dv_tot + jnp.where(halves[hk], dv_acc + pltpu.roll(dv_acc, HEAD_DIM, 1), 0.0)
            cur = pl.ds(pl.multiple_of(i_blk * LANE, LANE), LANE)
            dk_ref[cur, :] = dk_tot[LANE:, :]
            dv_ref[cur, :] = dv_tot[LANE:, :]

            def add_previous(i_blk=i_blk, dk_tot=dk_tot, dv_tot=dv_tot):
                prv = pl.ds(pl.multiple_of((i_blk - 1) * LANE, LANE), LANE)
                dk_ref[prv, :] += dk_tot[:LANE, :]
                dv_ref[prv, :] += dv_tot[:LANE, :]

            if g == 0:
                pl.when(i_blk > 0)(add_previous)
            else:
                add_previous()

        @pl.when(pl.program_id(0) == 0)
        def _():
            ds_ref[...] = jnp.zeros_like(ds_ref)

        ds_ref[...] += dsink

    blk512 = pl.BlockSpec((rows, SWA_Q), lambda i: (i, 0))
    full = pl.BlockSpec((s_len, LANE), lambda i: (0, 0))
    vec = pl.BlockSpec((1, LANE), lambda i: (0, 0))
    return pl.pallas_call(
        body, name="swa_bwd", grid=(s_len // rows,),
        in_specs=_swa_specs(group) + [pl.BlockSpec((1, SWA_HEADS * LANE), lambda i: (0, 0)),
                                      pl.BlockSpec((2, 2 * LANE, WIDE), lambda i: (0, 0, 0)), blk512, blk512,
                                      pl.BlockSpec((SWA_HEADS, rows), lambda i: (0, i))],
        out_specs=[blk512, full, full, vec],
        out_shape=[jax.ShapeDtypeStruct((s_len, SWA_Q), BF16), jax.ShapeDtypeStruct((s_len, LANE), F32),
                   jax.ShapeDtypeStruct((s_len, LANE), F32), jax.ShapeDtypeStruct((1, LANE), F32)],
        compiler_params=_params("arbitrary"),
    )(zq, zq, zq, zq, zq, jnp.repeat(sinks[:, :SWA_HEADS], LANE, axis=1), _swa_bias(), o, do, lse)


QB_COL = (SWA_Q + 2 * SWA_KV) // LANE
KB_COL = QB_COL + FOX_W // LANE
VB_COL = KB_COL + FOX_W // LANE
N_PAIR = FOX_HEADS // 2


def _causal(t, keys_first=False):
    r = lax.broadcasted_iota(jnp.int32, (t, t), 0)
    c = lax.broadcasted_iota(jnp.int32, (t, t), 1)
    return c >= r if keys_first else r >= c


N_SPLIT = 3


def _own_half(e):
    hi = lax.broadcasted_iota(jnp.int32, (1, LANE), 1) >= HEAD_DIM
    return hi if e else jnp.logical_not(hi)


def _feature_lane(e, t):
    return HEAD_DIM * (1 - e) + t


def _feature_tables():
    wide = FOX_HEADS * LANE
    place_q, place_k = np.zeros((N_SPLIT * LANE, wide), np.float32), np.zeros((N_SPLIT * LANE, wide), np.float32)
    ones_q, ones_k, ones_v, own = (np.zeros((1, wide), np.float32) for _ in range(4))
    for h in range(FOX_HEADS):
        e = h % 2
        own[0, h * LANE + HEAD_DIM * e:h * LANE + HEAD_DIM * (e + 1)] = 1.0
        ones_v[0, h * LANE + _feature_lane(e, 0)] = 1.0
        for t in range(N_SPLIT):
            place_q[t * LANE + h, h * LANE + _feature_lane(e, t)] = 1.0
            ones_q[0, h * LANE + _feature_lane(e, N_SPLIT + t)] = 1.0
            ones_k[0, h * LANE + _feature_lane(e, t)] = 1.0
            place_k[t * LANE + h, h * LANE + _feature_lane(e, N_SPLIT + t)] = -1.0
    return tuple(jnp.asarray(a) for a in (place_q, place_k, ones_q, ones_k, ones_v, own))


def _fox_prep(zq, c, tm=512):
    s_len = zq.shape[0]
    tm = _div(s_len, tm, 8)
    wide = FOX_HEADS * LANE

    def body(z_ref, c_ref, pq_ref, pk_ref, oq_ref, ok_ref, ov_ref, own_ref, qx_ref, kx_ref, vx_ref):
        rest = c_ref[...]
        parts = []
        for _ in range(N_SPLIT):
            part = rest.astype(BF16).astype(F32)
            rest = rest - part
            parts.append(part)
        parts = jnp.concatenate(parts, axis=1)
        qf = jnp.dot(parts, pq_ref[...], preferred_element_type=F32) + oq_ref[...]
        kf = jnp.dot(parts, pk_ref[...], preferred_element_type=F32) + ok_ref[...]
        own = own_ref[...] > 0.5
        for p in range(N_PAIR):
            cols = slice(2 * p * LANE, (2 * p + 2) * LANE)
            pair = lambda col: jnp.tile(z_ref[:, (col + p) * LANE:(col + p + 1) * LANE].astype(F32), (1, 2))
            qx_ref[:, cols] = jnp.where(own[:, cols], pair(QB_COL) * SCALE, qf[:, cols]).astype(BF16)
            kx_ref[:, cols] = jnp.where(own[:, cols], pair(KB_COL), kf[:, cols]).astype(BF16)
            vx_ref[:, cols] = jnp.where(own[:, cols], pair(VB_COL), ov_ref[:, cols]).astype(BF16)

    out = jax.ShapeDtypeStruct((s_len, wide), BF16)
    blk = pl.BlockSpec((tm, wide), lambda i: (i, 0))
    table = pl.BlockSpec((N_SPLIT * LANE, wide), lambda i: (0, 0))
    vec = pl.BlockSpec((1, wide), lambda i: (0, 0))
    return pl.pallas_call(
        body, name="fox_prep", grid=(s_len // tm,),
        in_specs=[pl.BlockSpec((tm, N_QKV), lambda i: (i, 0)), pl.BlockSpec((tm, LANE), lambda i: (i, 0)),
                  table, table, vec, vec, vec, vec],
        out_specs=[blk, blk, blk], out_shape=[out, out, out],
        compiler_params=_params("parallel"),
    )(zq, c, *_feature_tables())


def _comm_parts(comm):
    return ([], [], [], []) if comm is None else (comm.specs, comm.out_shape, comm.scratch, comm.arrays)


def _fox_fwd(qx, kx, vx, comm=None, t_cap=1024):
    s_len = qx.shape[0]
    t = _div(s_len, t_cap, LANE)
    nq = s_len // t
    c_specs, c_shapes, c_scratch, c_arrays = _comm_parts(comm)

    def compute(q_ref, k_ref, v_ref, o_ref, o32_ref, m_ref, l_ref):
        i = pl.program_id(1)
        qs = [q_ref[:, e * LANE:(e + 1) * LANE] for e in range(2)]

        def step(j, carry, diag):
            rows = pl.ds(pl.multiple_of(j * t, t), t)
            new = []
            for e in range(2):
                m, acc = carry[e]
                s2 = lax.dot_general(qs[e], k_ref[rows, e * LANE:(e + 1) * LANE], NT,
                                     preferred_element_type=F32) * LOG2E
                if diag:
                    s2 = jnp.where(_causal(t), s2, NEG_INF)
                mn = jnp.maximum(m, jnp.ceil(jnp.max(s2, axis=1, keepdims=True)))
                pe = jnp.exp2(s2 - mn).astype(BF16)
                acc = acc * jnp.exp2(m - mn) + jnp.dot(pe, v_ref[rows, e * LANE:(e + 1) * LANE],
                                                       preferred_element_type=F32)
                new.append((mn, acc))
            return tuple(new)

        init = (jnp.full((t, 1), NEG_INF, F32), jnp.zeros((t, LANE), F32))
        carry = lax.fori_loop(0, i, lambda j, c: step(j, c, False), (init, init))
        carry = step(i, carry, True)
        outs, ls = [], []
        for e in range(2):
            m, acc = carry[e]
            l = acc[:, _feature_lane(e, 0):_feature_lane(e, 0) + 1]
            outs.append(acc / l)
            ls.append(l)
        out = jnp.where(_own_half(1), outs[1], outs[0])
        o_ref[...] = out.astype(BF16)
        o32_ref[...] = out
        m_ref[...] = jnp.where(_own_half(1), carry[1][0], carry[0][0])
        l_ref[...] = jnp.where(_own_half(1), ls[1], ls[0])

    body = _with_comm(comm, 3, 4, lambda: (pl.program_id(0) == 0) & (pl.program_id(1) == 0),
                      lambda: (pl.program_id(0) == N_PAIR - 1) & (pl.program_id(1) == nq - 1), compute)
    pair = pl.BlockSpec((s_len, 2 * LANE), lambda p, i: (0, p))
    tile = pl.BlockSpec((t, LANE), lambda p, i: (i, p))
    wide = jax.ShapeDtypeStruct((s_len, FOX_W), F32)
    outs = pl.pallas_call(
        body, name="fox_fwd" if comm is None else "fox_fwd_comm%d" % comm.n, grid=(N_PAIR, nq),
        in_specs=[pl.BlockSpec((t, 2 * LANE), lambda p, i: (i, p)), pair, pair] + c_specs,
        out_specs=[tile, tile, tile, tile] + c_specs,
        out_shape=[jax.ShapeDtypeStruct((s_len, FOX_W), BF16), wide, wide, wide] + c_shapes,
        scratch_shapes=c_scratch,
        compiler_params=_params("arbitrary", "arbitrary"),
    )(qx, kx, vx, *c_arrays)
    return outs[0], outs[1], outs[2], outs[3], outs[4:]


def _fox_stats(o, do, m, l, tm=512):
    s_len = o.shape[0]
    tm = _div(s_len, tm, LANE)

    def body(o_ref, do_ref, m_ref, l_ref, dox_ref, st_ref):
        lane = lax.broadcasted_iota(jnp.int32, (1, LANE), 1)
        for p in range(N_PAIR):
            cols = slice(p * LANE, (p + 1) * LANE)
            dout = do_ref[:, cols]
            prod = o_ref[:, cols] * dout.astype(F32)
            shift = m_ref[:, cols]
            inv_l = 1.0 / l_ref[:, cols]
            st = jnp.zeros((tm, LANE), F32)
            for e in range(2):
                h = 2 * p + e
                dox_ref[:, h * LANE:(h + 1) * LANE] = jnp.where(_own_half(e), dout, jnp.zeros_like(dout))
                st = jnp.where(lane == e, shift[:, e * HEAD_DIM:e * HEAD_DIM + 1], st)
                delta = jnp.sum(jnp.where(_own_half(e), prod, 0.0), axis=1, keepdims=True)
                st = jnp.where(lane == 2 + e, delta, st)
                st = jnp.where(lane == 4 + e, inv_l[:, e * HEAD_DIM:e * HEAD_DIM + 1], st)
            st_ref[p] = st.T[:8, :]

    row = pl.BlockSpec((tm, FOX_W), lambda i: (i, 0))
    return pl.pallas_call(
        body, name="fox_stats", grid=(s_len // tm,), in_specs=[row, row, row, row],
        out_specs=[pl.BlockSpec((tm, FOX_HEADS * LANE), lambda i: (i, 0)),
                   pl.BlockSpec((N_PAIR, 8, tm), lambda i: (0, 0, i))],
        out_shape=[jax.ShapeDtypeStruct((s_len, FOX_HEADS * LANE), BF16),
                   jax.ShapeDtypeStruct((N_PAIR, 8, s_len), F32)],
        compiler_params=_params("parallel"),
    )(o, do, m, l)


def _fox_bwd(qx, kx, vx, dox, stats, comm=None, t_cap=1024):
    s_len = qx.shape[0]
    t = _div(s_len, t_cap, LANE)
    n = s_len // t
    c_specs, c_shapes, c_scratch, c_arrays = _comm_parts(comm)

    def compute(q_ref, do_ref, st_ref, k_ref, v_ref, dq_ref, dk_ref, dv_ref, dc_ref):
        j = pl.program_id(1)
        lane = lax.broadcasted_iota(jnp.int32, (1, LANE), 1)

        @pl.when(j == 0)
        def _():
            dq_ref[...] = jnp.zeros_like(dq_ref)

        ks = [k_ref[:, e * LANE:(e + 1) * LANE] for e in range(2)]
        vs = [v_ref[:, e * LANE:(e + 1) * LANE] for e in range(2)]
        ks_t = [k.astype(F32).T.astype(BF16) for k in ks]

        def step(i, carry, diag):
            rows = pl.ds(pl.multiple_of(i * t, t), t)
            new = []
            dq = jnp.zeros((LANE, t), F32)
            for e in range(2):
                dk, dv, dc = carry[e]
                q = q_ref[rows, e * LANE:(e + 1) * LANE]
                dout = do_ref[rows, e * LANE:(e + 1) * LANE]
                s_t = lax.dot_general(ks[e], q, NT, preferred_element_type=F32) * LOG2E
                if diag:
                    s_t = jnp.where(_causal(t, keys_first=True), s_t, NEG_INF)
                p_t = jnp.exp2(s_t - st_ref[0, e:e + 1, rows]).astype(BF16).astype(F32) * st_ref[0, 4 + e:5 + e, rows]
                dp_t = lax.dot_general(vs[e], dout, NT, preferred_element_type=F32)
                ds_f = p_t * (dp_t - st_ref[0, 2 + e:3 + e, rows])
                ds_t = ds_f.astype(BF16)
                dc = dc + jnp.sum(ds_f, axis=1, keepdims=True)
                dv = dv + jnp.dot(p_t.astype(BF16), dout, preferred_element_type=F32)
                dk = dk + jnp.dot(ds_t, q, preferred_element_type=F32)
                dq_e = jnp.dot(ks_t[e], ds_t, preferred_element_type=F32)
                dq = dq + jnp.where(_row_halves()[e], dq_e, 0.0)
                new.append((dk, dv, dc))
            dq_ref[rows, :] += dq.T * SCALE
            return tuple(new)

        zero = jnp.zeros((t, LANE), F32)
        init = (zero, zero, jnp.zeros((t, 1), F32))
        carry = step(j, (init, init), True)
        (dk0, dv0, dc0), (dk1, dv1, dc1) = lax.fori_loop(j + 1, n, lambda i, c: step(i, c, False), carry)
        dk_ref[...] = jnp.where(_own_half(1), dk1, dk0).astype(BF16)
        dv_ref[...] = jnp.where(_own_half(1), dv1, dv0).astype(BF16)
        dc_ref[...] = jnp.where(lane == 0, -dc0, jnp.where(lane == 1, -dc1, 0.0))

    body = _with_comm(comm, 5, 4, lambda: (pl.program_id(0) == 0) & (pl.program_id(1) == 0),
                      lambda: (pl.program_id(0) == N_PAIR - 1) & (pl.program_id(1) == n - 1), compute)
    pair = pl.BlockSpec((s_len, 2 * LANE), lambda p, j: (0, p))
    blk = pl.BlockSpec((t, 2 * LANE), lambda p, j: (j, p))
    tile = pl.BlockSpec((t, LANE), lambda p, j: (j, p))
    outs = pl.pallas_call(
        body, name="fox_bwd" if comm is None else "fox_bwd_comm", grid=(N_PAIR, n),
        in_specs=[pair, pair, pl.BlockSpec((1, 8, s_len), lambda p, j: (p, 0, 0)), blk, blk] + c_specs,
        out_specs=[pl.BlockSpec((s_len, LANE), lambda p, j: (0, p)), tile, tile, tile] + c_specs,
        out_shape=[jax.ShapeDtypeStruct((s_len, FOX_W), F32), jax.ShapeDtypeStruct((s_len, FOX_W), BF16),
                   jax.ShapeDtypeStruct((s_len, FOX_W), BF16), jax.ShapeDtypeStruct((s_len, FOX_W), F32)] + c_shapes,
        scratch_shapes=c_scratch,
        compiler_params=_params("arbitrary", "arbitrary"),
    )(qx, dox, stats, kx, vx, *c_arrays)
    return outs[0], outs[1], outs[2], outs[3], outs[4:]


def _mixer_out(attn_a, attn_b, zg, h, wpa, wpb, wout, g, b, tm=512):
    m = h.shape[0]
    tm = _div(m, tm, 8)

    def body(a_ref, b_ref, ga_ref, gb_ref, h_ref, wpa_ref, wpb_ref, wout_ref, g_ref, bb_ref, h1_ref, u_ref, mg_ref,
             h1b_ref):
        ya = jnp.dot(a_ref[...], wpa_ref[...], preferred_element_type=F32)
        yb = jnp.dot(b_ref[...], wpb_ref[...], preferred_element_type=F32)
        merged = (jax.nn.sigmoid(ga_ref[...]) * ya + jax.nn.sigmoid(gb_ref[...]) * yb).astype(BF16)
        u = ALPHA * h_ref[...] + jnp.dot(merged, wout_ref[...], preferred_element_type=F32)
        u_ref[...] = u
        h1 = _ln(u, g_ref[...], bb_ref[...])
        h1_ref[...] = h1
        h1b_ref[...] = h1.astype(BF16)
        mg_ref[...] = merged

    row = pl.BlockSpec((tm, D_MODEL), lambda i: (i, 0))
    att = pl.BlockSpec((tm, SWA_Q), lambda i: (i, 0))
    vec = pl.BlockSpec((1, D_MODEL), lambda i: (0, 0))
    wsm = pl.BlockSpec((SWA_Q, D_MODEL), lambda i: (0, 0))
    return pl.pallas_call(
        body, name="mixer_out", grid=(m // tm,),
        in_specs=[att, att, row, pl.BlockSpec((tm, D_MODEL), lambda i: (i, 1)), row, wsm, wsm,
                  pl.BlockSpec((D_MODEL, D_MODEL), lambda i: (0, 0)), vec, vec],
        out_specs=[row, row, row, row],
        out_shape=[jax.ShapeDtypeStruct((m, D_MODEL), F32), jax.ShapeDtypeStruct((m, D_MODEL), F32),
                   jax.ShapeDtypeStruct((m, D_MODEL), BF16), jax.ShapeDtypeStruct((m, D_MODEL), BF16)],
        compiler_params=_params("parallel"),
    )(attn_a, attn_b, zg, zg, h, wpa, wpb, wout, g, b)


def _mixer_bwd(dh1, u1, g, wout, attn_a, attn_b, zg, wpa, wpb, tm=512):
    m = dh1.shape[0]
    tm = _div(m, tm, 8)

    def body(dh_ref, u_ref, g_ref, wout_ref, a_ref, b_ref, ga_ref, gb_ref, wpa_ref, wpb_ref,
             du_ref, dg_ref, db_ref, dya_ref, dyb_ref, dga_ref, dgb_ref, da_ref, dbb_ref):
        @pl.when(pl.program_id(0) == 0)
        def _():
            dg_ref[...] = jnp.zeros_like(dg_ref)
            db_ref[...] = jnp.zeros_like(db_ref)

        du, dg, db = _ln_bwd_block(dh_ref[...], u_ref[...], g_ref[...])
        du_ref[...] = du
        dg_ref[...] += dg
        db_ref[...] += db
        dm = lax.dot_general(du.astype(BF16), wout_ref[...], (((1,), (1,)), ((), ())), preferred_element_type=F32)
        for x_ref, gate_ref, w_ref, dy_ref, dgate_ref, dattn_ref in (
                (a_ref, ga_ref, wpa_ref, dya_ref, dga_ref, da_ref), (b_ref, gb_ref, wpb_ref, dyb_ref, dgb_ref, dbb_ref)):
            sg = jax.nn.sigmoid(gate_ref[...])
            dy = (dm * sg).astype(BF16)
            dy_ref[...] = dy
            y = jnp.dot(x_ref[...], w_ref[...], preferred_element_type=F32)
            dgate_ref[...] = (dm * y * sg * (1.0 - sg)).astype(BF16)
            dattn_ref[...] = lax.dot_general(dy, w_ref[...], (((1,), (1,)), ((), ())),
                                             preferred_element_type=F32).astype(BF16)

    row = pl.BlockSpec((tm, D_MODEL), lambda i: (i, 0))
    att = pl.BlockSpec((tm, SWA_Q), lambda i: (i, 0))
    vec = pl.BlockSpec((1, D_MODEL), lambda i: (0, 0))
    wsm = pl.BlockSpec((SWA_Q, D_MODEL), lambda i: (0, 0))
    wide = jax.ShapeDtypeStruct((m, D_MODEL), BF16)
    narrow = jax.ShapeDtypeStruct((m, SWA_Q), BF16)
    sums = jax.ShapeDtypeStruct((1, D_MODEL), F32)
    return pl.pallas_call(
        body, name="mixer_bwd", grid=(m // tm,),
        in_specs=[row, row, vec, pl.BlockSpec((D_MODEL, D_MODEL), lambda i: (0, 0)), att, att, row,
                  pl.BlockSpec((tm, D_MODEL), lambda i: (i, 1)), wsm, wsm],
        out_specs=[row, vec, vec, row, row, row, row, att, att],
        out_shape=[jax.ShapeDtypeStruct((m, D_MODEL), F32), sums, sums, wide, wide, wide, wide, narrow, narrow],
        compiler_params=_params("arbitrary"),
    )(dh1, u1, g, wout, attn_a, attn_b, zg, zg, wpa, wpb)


def _shift_down(x, k, halo, first):
    rows = lax.broadcasted_iota(jnp.int32, (x.shape[0], 1), 0)
    y = pltpu.roll(x, k, 0)
    for r in range(k):
        fill = jnp.where(first, 0.0, halo[8 - k + r:8 - k + r + 1, :])
        y = jnp.where(rows == r, fill, y)
    return y


def _shift_up(x, k, halo, last):
    n = x.shape[0]
    rows = lax.broadcasted_iota(jnp.int32, (n, 1), 0)
    y = pltpu.roll(x, n - k, 0)
    for r in range(k):
        fill = jnp.where(last, 0.0, halo[r:r + 1, :])
        y = jnp.where(rows == n - k + r, fill, y)
    return y


def _conv_act(gate, gate_m1, gate_m2, cw, cb):
    return cb + cw[0:1, :] * gate_m2 + cw[1:2, :] * gate_m1 + cw[2:3, :] * gate


def _ffn_in_conv(h1, wfi, cw, cb, tm=256):
    s_len = h1.shape[0]
    tm = _div(s_len, tm, 8)
    hb = tm // 8

    def body(a_ref, ap_ref, w_ref, cw_ref, cb_ref, gu_ref, act_ref):
        first = pl.program_id(0) == 0
        a = a_ref[...].astype(BF16)
        before = ap_ref[...].astype(BF16)
        for c in range(N_CHUNK):
            gate = jnp.dot(a, w_ref[c], preferred_element_type=F32)
            up = jnp.dot(a, w_ref[N_CHUNK + c], preferred_element_type=F32)
            halo = jnp.dot(before, w_ref[c], preferred_element_type=F32)
            gu_ref[c, 0] = gate
            gu_ref[c, 1] = up
            conv = _conv_act(gate, _shift_down(gate, 1, halo, first), _shift_down(gate, 2, halo, first),
                             cw_ref[c], cb_ref[c])
            act_ref[c] = (conv * jax.nn.sigmoid(conv) * up).astype(BF16)

    return pl.pallas_call(
        body, name="ffn_in_conv", grid=(s_len // tm,),
        in_specs=[pl.BlockSpec((tm, D_MODEL), lambda i: (i, 0)),
                  pl.BlockSpec((8, D_MODEL), lambda i: (jnp.maximum(i * hb - 1, 0), 0)),
                  pl.BlockSpec((N_DEV, D_MODEL, FF_CHUNK), lambda i: (0, 0, 0)),
                  pl.BlockSpec((N_CHUNK, 8, FF_CHUNK), lambda i: (0, 0, 0)),
                  pl.BlockSpec((N_CHUNK, 1, FF_CHUNK), lambda i: (0, 0, 0))],
        out_specs=[pl.BlockSpec((N_CHUNK, 2, tm, FF_CHUNK), lambda i: (0, 0, i, 0)),
                   pl.BlockSpec((N_CHUNK, tm, FF_CHUNK), lambda i: (0, i, 0))],
        out_shape=[jax.ShapeDtypeStruct((N_CHUNK, 2, s_len, FF_CHUNK), F32),
                   jax.ShapeDtypeStruct((N_CHUNK, s_len, FF_CHUNK), BF16)],
        compiler_params=_params("parallel"),
    )(h1, h1, wfi, cw, cb)


def _ffn_out_ln(act, wfo, res, g, b, target=None, tm=512):
    s_len = res.shape[0]
    tm = _div(s_len, tm, 8)
    last = target is not None

    def body(a_ref, w_ref, res_ref, g_ref, b_ref, *rest):
        u = ALPHA * res_ref[...]
        for c in range(N_CHUNK):
            u = u + jnp.dot(a_ref[c], w_ref[c], preferred_element_type=F32)
        y = _ln(u, g_ref[...], b_ref[...])
        if not last:
            u_ref, y_ref = rest
            u_ref[...] = u
            y_ref[...] = y
            return
        t_ref, u_ref, dy_ref, loss_ref = rest
        u_ref[...] = u

        @pl.when(pl.program_id(0) == 0)
        def _():
            loss_ref[...] = jnp.zeros_like(loss_ref)

        err = y - t_ref[...]
        dy_ref[...] = err / D_MODEL
        loss_ref[...] += 0.5 * jnp.sum(jnp.sum(err * err, axis=1, keepdims=True) / D_MODEL, axis=0, keepdims=True)

    row = pl.BlockSpec((tm, D_MODEL), lambda i: (i, 0))
    vec = pl.BlockSpec((1, D_MODEL), lambda i: (0, 0))
    wide = jax.ShapeDtypeStruct((s_len, D_MODEL), F32)
    return pl.pallas_call(
        body, name="ffn_out_ln_loss" if last else "ffn_out_ln", grid=(s_len // tm,),
        in_specs=[pl.BlockSpec((N_CHUNK, tm, FF_CHUNK), lambda i: (0, i, 0)),
                  pl.BlockSpec((N_CHUNK, FF_CHUNK, D_MODEL), lambda i: (0, 0, 0)), row, vec, vec] + [row] * last,
        out_specs=[row, row] + [pl.BlockSpec((8, LANE), lambda i: (0, 0))] * last,
        out_shape=[wide, wide] + [jax.ShapeDtypeStruct((8, LANE), F32)] * last,
        compiler_params=_params("arbitrary" if last else "parallel"),
    )(act, wfo, res, g, b, *([target] if last else []))


def _ffn_out_bwd(dh2, u2, g, wfo, tm=512):
    s_len = dh2.shape[0]
    tm = _div(s_len, tm, 8)

    def body(dh_ref, u_ref, g_ref, w_ref, du_ref, dg_ref, db_ref, o_ref):
        @pl.when(pl.program_id(0) == 0)
        def _():
            dg_ref[...] = jnp.zeros_like(dg_ref)
            db_ref[...] = jnp.zeros_like(db_ref)

        du, dg, db = _ln_bwd_block(dh_ref[...], u_ref[...], g_ref[...])
        du_ref[...] = du
        dg_ref[...] += dg
        db_ref[...] += db
        du_b = du.astype(BF16)
        for c in range(N_CHUNK):
            o_ref[c] = lax.dot_general(du_b, w_ref[c], (((1,), (1,)), ((), ())), preferred_element_type=F32)

    row = pl.BlockSpec((tm, D_MODEL), lambda i: (i, 0))
    vec = pl.BlockSpec((1, D_MODEL), lambda i: (0, 0))
    sums = jax.ShapeDtypeStruct((1, D_MODEL), F32)
    return pl.pallas_call(
        body, name="ffn_out_bwd", grid=(s_len // tm,),
        in_specs=[row, row, vec, pl.BlockSpec((N_CHUNK, FF_CHUNK, D_MODEL), lambda i: (0, 0, 0))],
        out_specs=[row, vec, vec, pl.BlockSpec((N_CHUNK, tm, FF_CHUNK), lambda i: (0, i, 0))],
        out_shape=[jax.ShapeDtypeStruct((s_len, D_MODEL), F32), sums, sums,
                   jax.ShapeDtypeStruct((N_CHUNK, s_len, FF_CHUNK), F32)],
        compiler_params=_params("arbitrary"),
    )(dh2, u2, g, wfo)


def _g_w_ffn_out(act, du, tm=2048):
    s_len = du.shape[0]
    tm = _div(s_len, tm, 8)
    steps = s_len // tm

    def body(a_ref, g_ref, o_ref, acc_ref):
        s = pl.program_id(1)

        @pl.when(s == 0)
        def _():
            acc_ref[...] = jnp.zeros_like(acc_ref)

        acc_ref[...] += lax.dot_general(a_ref[0], g_ref[...].astype(BF16), (((0,), (0,)), ((), ())),
                                        preferred_element_type=F32)

        @pl.when(s == steps - 1)
        def _():
            o_ref[0] = acc_ref[...].astype(BF16)

    return pl.pallas_call(
        body, name="g_w_ffn_out", grid=(N_CHUNK, steps),
        in_specs=[pl.BlockSpec((1, tm, FF_CHUNK), lambda c, s: (c, s, 0)),
                  pl.BlockSpec((tm, D_MODEL), lambda c, s: (s, 0))],
        out_specs=pl.BlockSpec((1, FF_CHUNK, D_MODEL), lambda c, s: (c, 0, 0)),
        out_shape=jax.ShapeDtypeStruct((N_CHUNK, FF_CHUNK, D_MODEL), BF16),
        scratch_shapes=[pltpu.VMEM((FF_CHUNK, D_MODEL), F32)],
        compiler_params=_params("parallel", "arbitrary"),
    )(act, du)


def _g_w_ffn_in(h1, dgu, tm=4096):
    s_len = h1.shape[0]
    tm = _div(s_len, tm, 8)
    steps = s_len // tm

    def body(a_ref, g_ref, o_ref, acc_ref):
        s = pl.program_id(1)

        @pl.when(s == 0)
        def _():
            acc_ref[...] = jnp.zeros_like(acc_ref)

        acc_ref[...] += lax.dot_general(g_ref[0, 0], a_ref[...].astype(BF16), (((0,), (0,)), ((), ())),
                                        preferred_element_type=F32)

        @pl.when(s == steps - 1)
        def _():
            o_ref[0] = acc_ref[...].astype(BF16)

    return pl.pallas_call(
        body, name="g_w_ffn_in", grid=(N_DEV, steps),
        in_specs=[pl.BlockSpec((tm, D_MODEL), lambda d, s: (s, 0)),
                  pl.BlockSpec((1, 1, tm, FF_CHUNK), lambda d, s: (d % N_CHUNK, d // N_CHUNK, s, 0))],
        out_specs=pl.BlockSpec((1, FF_CHUNK, D_MODEL), lambda d, s: (d, 0, 0)),
        out_shape=jax.ShapeDtypeStruct((N_DEV, FF_CHUNK, D_MODEL), BF16),
        scratch_shapes=[pltpu.VMEM((FF_CHUNK, D_MODEL), F32)],
        compiler_params=_params("parallel", "arbitrary"),
    )(h1, dgu)


def _conv_bwd_dh1(gu, dact, cw, cb, wfi, res, tm=256):
    s_len = gu.shape[2]
    tm = _div(s_len, tm, 8)
    nrow = s_len // tm
    hb = tm // 8

    def dconv_of(conv, sg, up, da):
        return da * up * (sg * (1.0 + conv * (1.0 - sg)))

    def body(gu_ref, gp_ref, gun_ref, da_ref, dan_ref, cw_ref, cb_ref, w_ref, res_ref, dgu_ref, dcw_ref, dh_ref):
        i = pl.program_id(0)
        first = i == 0
        last = i == nrow - 1

        @pl.when(first)
        def _():
            dcw_ref[...] = jnp.zeros_like(dcw_ref)

        row = lax.broadcasted_iota(jnp.int32, (8, 1), 0)
        acc = ALPHA * res_ref[...]
        for c in range(N_CHUNK):
            cw = cw_ref[c]
            cb = cb_ref[c]
            gate = gu_ref[c, 0]
            halo = gp_ref[c, 0]
            g_m1 = _shift_down(gate, 1, halo, first)
            g_m2 = _shift_down(gate, 2, halo, first)
            conv = _conv_act(gate, g_m1, g_m2, cw, cb)
            da = da_ref[c]
            sg = jax.nn.sigmoid(conv)
            dup = (da * conv * sg).astype(BF16)
            dconv = dconv_of(conv, sg, gu_ref[c, 1], da)
            gate_n = gun_ref[c, 0]
            tail = gate[tm - 8:, :]
            conv_n = _conv_act(gate_n, _shift_down(gate_n, 1, tail, False), _shift_down(gate_n, 2, tail, False),
                               cw, cb)
            dconv_n = dconv_of(conv_n, jax.nn.sigmoid(conv_n), gun_ref[c, 1], dan_ref[c])
            dgate = (cw[2:3, :] * dconv + cw[1:2, :] * _shift_up(dconv, 1, dconv_n, last)
                     + cw[0:1, :] * _shift_up(dconv, 2, dconv_n, last)).astype(BF16)
            dgu_ref[c, 0] = dgate
            dgu_ref[c, 1] = dup
            acc = acc + jnp.dot(dgate, w_ref[c], preferred_element_type=F32)
            acc = acc + jnp.dot(dup, w_ref[N_CHUNK + c], preferred_element_type=F32)
            part = jnp.zeros((8, FF_CHUNK), F32)
            for r, term in enumerate((dconv * g_m2, dconv * g_m1, dconv * gate, dconv)):
                part = jnp.where(row == r, jnp.sum(term, axis=0, keepdims=True), part)
            dcw_ref[c] += part
        dh_ref[...] = acc

    nxt = lambda i: jnp.minimum((i + 1) * hb, s_len // 8 - 1)
    main = pl.BlockSpec((N_CHUNK, 2, tm, FF_CHUNK), lambda i: (0, 0, i, 0))
    row_d = pl.BlockSpec((tm, D_MODEL), lambda i: (i, 0))
    return pl.pallas_call(
        body, name="conv_bwd_dh1", grid=(nrow,),
        in_specs=[main,
                  pl.BlockSpec((N_CHUNK, 1, 8, FF_CHUNK), lambda i: (0, 0, jnp.maximum(i * hb - 1, 0), 0)),
                  pl.BlockSpec((N_CHUNK, 2, 8, FF_CHUNK), lambda i: (0, 0, nxt(i), 0)),
                  pl.BlockSpec((N_CHUNK, tm, FF_CHUNK), lambda i: (0, i, 0)),
                  pl.BlockSpec((N_CHUNK, 8, FF_CHUNK), lambda i: (0, nxt(i), 0)),
                  pl.BlockSpec((N_CHUNK, 8, FF_CHUNK), lambda i: (0, 0, 0)),
                  pl.BlockSpec((N_CHUNK, 1, FF_CHUNK), lambda i: (0, 0, 0)),
                  pl.BlockSpec((N_DEV, FF_CHUNK, D_MODEL), lambda i: (0, 0, 0)), row_d],
        out_specs=[main, pl.BlockSpec((N_CHUNK, 8, FF_CHUNK), lambda i: (0, 0, 0)), row_d],
        out_shape=[jax.ShapeDtypeStruct((N_CHUNK, 2, s_len, FF_CHUNK), BF16),
                   jax.ShapeDtypeStruct((N_CHUNK, 8, FF_CHUNK), F32),
                   jax.ShapeDtypeStruct((s_len, D_MODEL), F32)],
        compiler_params=_params("arbitrary"),
    )(gu, gu, gu, dact, dact, cw, cb, wfi, res)


def _sum_devices(r_ref):
    acc = r_ref[0].astype(F32)
    for d in range(1, N_DEV):
        acc = acc + r_ref[d].astype(F32)
    return acc


def _sum8(recv):
    rows = recv.shape[1]
    tr = _div(rows, ROW_BLOCK, 8)

    def body(r_ref, o_ref):
        o_ref[...] = _sum_devices(r_ref)

    return pl.pallas_call(
        body, name="sum8", grid=(rows // tr,),
        in_specs=[pl.BlockSpec((N_DEV, tr, LANE), lambda i: (0, i, 0))],
        out_specs=pl.BlockSpec((tr, LANE), lambda i: (i, 0)),
        out_shape=jax.ShapeDtypeStruct((rows, LANE), F32),
        compiler_params=_params("parallel"),
    )(recv)


def _adamw_math(w, g, m, v):
    m = ADAM_B1 * m + (1.0 - ADAM_B1) * g
    v = ADAM_B2 * v + (1.0 - ADAM_B2) * (g * g)
    m_hat = m / (1.0 - ADAM_B1 ** ADAM_STEP)
    v_hat = v / (1.0 - ADAM_B2 ** ADAM_STEP)
    return -ADAM_LR * (m_hat / (jnp.sqrt(v_hat) + ADAM_EPS) + ADAM_WD * w), m, v


def _adamw_rows(w, g, m, v, name):
    rows = w.shape[0]
    tr = _div(rows, ROW_BLOCK, 8)

    def body(w_ref, g_ref, m_ref, v_ref, d_ref, mo_ref, vo_ref):
        d_ref[...], mo_ref[...], vo_ref[...] = _adamw_math(w_ref[...], g_ref[...], m_ref[...], v_ref[...])

    blk = pl.BlockSpec((tr, LANE), lambda i: (i, 0))
    out = jax.ShapeDtypeStruct((rows, LANE), F32)
    return pl.pallas_call(
        body, name=name, grid=(rows // tr,), in_specs=[blk, blk, blk, blk], out_specs=[blk, blk, blk],
        out_shape=[out, out, out], compiler_params=_params("parallel"),
    )(w, g, m, v)


def _adamw_w_in(recv, w, m, v, tl=128):
    n, depth, d = w.shape

    def body(*refs):
        r_refs, (w_ref, m_ref, v_ref), (g_ref, d_ref, mo_ref, vo_ref) = refs[:depth], refs[depth:depth + 3], refs[-4:]
        for l in range(depth):
            g = _sum_devices(r_refs[l])
            g_ref[:, l, :] = g
            d_ref[:, l, :], mo_ref[:, l, :], vo_ref[:, l, :] = _adamw_math(w_ref[:, l, :], g, m_ref[:, l, :],
                                                                            v_ref[:, l, :])

    blk = pl.BlockSpec((n, depth, tl), lambda j: (0, 0, j))
    out = jax.ShapeDtypeStruct((n, depth, d), F32)
    return pl.pallas_call(
        body, name="adamw_w_in", grid=(d // tl,),
        in_specs=[pl.BlockSpec((N_DEV, n, tl), lambda j: (0, 0, j))] * depth + [blk, blk, blk],
        out_specs=[blk, blk, blk, blk], out_shape=[out, out, out, out],
        compiler_params=_params("parallel"),
    )(*recv, w, m, v)


def _adamw_shard(recv, w, m, v, name):
    depth, k, n = w.shape
    tk = _div(k, 256, 16)

    def body(*refs):
        r_refs, (w_ref, m_ref, v_ref), (g_ref, d_ref, mo_ref, vo_ref) = refs[:depth], refs[depth:depth + 3], refs[-4:]
        for l in range(depth):
            g = _sum_devices(r_refs[l])
            g_ref[l] = g
            d_ref[l], mo_ref[l], vo_ref[l] = _adamw_math(w_ref[l], g, m_ref[l], v_ref[l])

    blk = pl.BlockSpec((depth, tk, n), lambda i: (0, i, 0))
    out = jax.ShapeDtypeStruct((depth, k, n), F32)
    return pl.pallas_call(
        body, name=name, grid=(k // tk,),
        in_specs=[pl.BlockSpec((N_DEV, tk, n), lambda i: (0, i, 0))] * depth + [blk, blk, blk],
        out_specs=[blk, blk, blk, blk], out_shape=[out, out, out, out],
        compiler_params=_params("parallel"),
    )(*recv, w, m, v)


def _to_rows(flat, rows):
    flat = flat.reshape(-1)
    return jnp.pad(flat, (0, rows * LANE - flat.shape[0])).reshape(rows, LANE)


def _pad_z(a, axis):
    f0 = N_QKV
    g0 = N_QKV + FOX_HEADS
    take = lambda lo, hi: lax.slice_in_dim(a, lo, hi, axis=axis)
    shape = list(a.shape)
    shape[axis] = F_PAD - FOX_HEADS
    return jnp.concatenate([take(0, f0), take(g0, N_IN), take(f0, g0), jnp.zeros(shape, a.dtype)], axis=axis)


def _unpad_z(a, axis):
    f0 = N_QKV + N_GATE
    take = lambda lo, hi: lax.slice_in_dim(a, lo, hi, axis=axis)
    return jnp.concatenate([take(0, N_QKV), take(f0, f0 + FOX_HEADS), take(N_QKV, f0)], axis=axis)


def _shards_to_cols(g):
    _, k, n = g.shape
    return g.transpose(1, 0, 2).reshape(k, N_DEV * n)


def _cols_to_shards(full):
    k, n = full.shape
    return full.reshape(k, N_DEV, n // N_DEV).transpose(1, 0, 2)


def _layer_fwd(h, w, p, comm=None, late=None, target=None):
    zq, zg, h_b = _z_proj(h, w["w_in_p"], p["b_in_p"])
    qx, kx, vx = _fox_prep(zq, _cumsum_logf(zg))
    attn_a, lse_a = _swa_fwd(zq, p["sinks"])
    attn_b, attn_b32, m_b, l_b, arrived = _fox_fwd(qx, kx, vx, comm)
    if late is not None:
        w, p = late(w, p, arrived)
    h1, u1, merged, h1_b = _mixer_out(attn_a, attn_b, zg, h, w["w_proj_a"], w["w_proj_b"], w["w_out"],
                                        p["ln_mix_g"], p["ln_mix_b"])
    gu, act = _ffn_in_conv(h1, w["w_ffn_in_fwd"], p["conv_w"], p["conv_b"])
    u2, *h2 = _ffn_out_ln(act, w["w_ffn_out"], h1, p["ln_ffn_g"], p["ln_ffn_b"], target)
    saved = dict(h_b=h_b, h1_b=h1_b, zq=zq, zg=zg, qx=qx, kx=kx, vx=vx, attn_a=attn_a, lse_a=lse_a, attn_b=attn_b,
                 attn_b32=attn_b32, m_b=m_b, l_b=l_b, h1=h1, u1=u1, merged=merged, gu=gu, act=act, u2=u2)
    return h2, saved, w, p


def _layer_bwd(dh2, sv, w, p, make_comm=None, make_last_comm=None):
    s_len = dh2.shape[0]
    du2, d_ffn_g, d_ffn_b, dact = _ffn_out_bwd(dh2, sv["u2"], p["ln_ffn_g"], w["w_ffn_out"])
    g_ffn_out = _g_w_ffn_out(sv["act"], du2)
    dgu, dcw, dh1 = _conv_bwd_dh1(sv["gu"], dact, p["conv_w"], p["conv_b"], w["w_ffn_in"], du2)
    dcw = dcw.transpose(1, 0, 2).reshape(8, D_FF)
    g_ffn_in = _g_w_ffn_in(sv["h1_b"], dgu)
    du1, d_mix_g, d_mix_b, dya, dyb, dga, dgb, dattn_a, dattn_b = _mixer_bwd(
        dh1, sv["u1"], p["ln_mix_g"], w["w_out"], sv["attn_a"], sv["attn_b"], sv["zg"], w["w_proj_a"], w["w_proj_b"])
    g_out = _linear_tn(sv["merged"], du1, name="g_w_out", tn=1024)
    g_proj_a = _linear_tn(sv["attn_a"], dya, name="g_w_proj_a", tk=512, tn=1024)
    g_proj_b = _linear_tn(sv["attn_b"], dyb, name="g_w_proj_b", tk=512, tn=1024)
    dq_a, dk_a, dv_a, dsinks = _swa_bwd(sv["zq"], p["sinks"], sv["attn_a"], dattn_a, sv["lse_a"])
    big = dict(w_proj_a=_cols_to_shards(g_proj_a), w_proj_b=_cols_to_shards(g_proj_b),
               w_out=g_out.reshape(N_DEV, D_MODEL // N_DEV, D_MODEL), w_ffn_in=g_ffn_in,
               w_ffn_out=g_ffn_out.reshape(N_DEV, D_FF // N_DEV, D_MODEL))
    dox, stats = _fox_stats(sv["attn_b32"], dattn_b, sv["m_b"], sv["l_b"])
    dq_b, dk_b, dv_b, dcc, arrived = _fox_bwd(sv["qx"], sv["kx"], sv["vx"], dox, stats,
                                              None if make_comm is None else make_comm(big))
    df = _forget_bwd(dcc, sv["zg"])
    dz = jnp.concatenate([dq_a, dk_a.astype(BF16), dv_a.astype(BF16), dq_b.astype(BF16), dk_b, dv_b, dga, dgb, df,
                          jnp.zeros((s_len, F_PAD - LANE), BF16)], axis=1)
    g_in_t, g_b_in = _linear_tn(dz, sv["h_b"], name="g_w_in", tk=768, tn=1024, tm=4096, colsum=True)
    g_in_t, g_b_in = _unpad_z(g_in_t, 0), _unpad_z(g_b_in, 1)
    big["w_in"] = g_in_t.reshape(N_DEV, N_IN // N_DEV, D_MODEL)
    small = dict(ln_mix_g=d_mix_g, ln_mix_b=d_mix_b, b_in=g_b_in, attn_sinks=dsinks[:, :SWA_HEADS],
                 ln_ffn_g=d_ffn_g, ln_ffn_b=d_ffn_b, conv_w=dcw[:3], conv_b=dcw[3:4])
    dh, arrived_last = _d_h(dz, w["w_in_p"], du1, None if make_last_comm is None else make_last_comm(big, small))
    return dh, big, small, arrived, arrived_last


def _w_in_layouts(w_in):
    return dict(w_in_p=_pad_z(w_in.reshape(N_IN, D_MODEL), 0))


def _other_layouts(w_proj_a, w_proj_b, w_out, w_ffn_in, w_ffn_out):
    return dict(w_proj_a=_shards_to_cols(w_proj_a), w_proj_b=_shards_to_cols(w_proj_b),
                w_out=w_out.reshape(D_MODEL, D_MODEL), w_ffn_in=w_ffn_in,
                w_ffn_in_fwd=w_ffn_in.transpose(0, 2, 1),
                w_ffn_out=w_ffn_out.reshape(N_CHUNK, FF_CHUNK, D_MODEL))


def _layer_params(r):
    return dict(
        b_in_p=_pad_z(r["b_in"].reshape(1, N_IN), 1),
        sinks=jnp.pad(r["attn_sinks"].reshape(1, SWA_HEADS), ((0, 0), (0, LANE - SWA_HEADS))),
        ln_mix_g=r["ln_mix_g"].reshape(1, D_MODEL), ln_mix_b=r["ln_mix_b"].reshape(1, D_MODEL),
        ln_ffn_g=r["ln_ffn_g"].reshape(1, D_MODEL), ln_ffn_b=r["ln_ffn_b"].reshape(1, D_MODEL),
        conv_b=r["conv_b"].reshape(N_CHUNK, 1, FF_CHUNK))


def _conv_w_layout(conv_w):
    return jnp.pad(conv_w, ((0, 5), (0, 0))).reshape(8, N_CHUNK, FF_CHUNK).transpose(1, 0, 2)


def kernel(x, ln_mix_g, ln_mix_b, w_in, b_in, attn_sinks, w_proj_a, w_proj_b, w_out, ln_ffn_g, ln_ffn_b, w_ffn_in, conv_w, conv_b, w_ffn_out, loss_target, m_ln_mix_g, m_ln_mix_b, m_w_in, m_b_in, m_attn_sinks, m_w_proj_a, m_w_proj_b, m_w_out, m_ln_ffn_g, m_ln_ffn_b, m_w_ffn_in, m_conv_w, m_conv_b, m_w_ffn_out, v_ln_mix_g, v_ln_mix_b, v_w_in, v_b_in, v_attn_sinks, v_w_proj_a, v_w_proj_b, v_w_out, v_ln_ffn_g, v_ln_ffn_b, v_w_ffn_in, v_conv_w, v_conv_b, v_w_ffn_out):
    wts = dict(ln_mix_g=ln_mix_g, ln_mix_b=ln_mix_b, w_in=w_in, b_in=b_in, attn_sinks=attn_sinks, w_proj_a=w_proj_a,
               w_proj_b=w_proj_b, w_out=w_out, ln_ffn_g=ln_ffn_g, ln_ffn_b=ln_ffn_b, w_ffn_in=w_ffn_in,
               conv_w=conv_w, conv_b=conv_b, w_ffn_out=w_ffn_out)
    mom = dict(ln_mix_g=m_ln_mix_g, ln_mix_b=m_ln_mix_b, w_in=m_w_in, b_in=m_b_in, attn_sinks=m_attn_sinks,
               w_proj_a=m_w_proj_a, w_proj_b=m_w_proj_b, w_out=m_w_out, ln_ffn_g=m_ln_ffn_g, ln_ffn_b=m_ln_ffn_b,
               w_ffn_in=m_w_ffn_in, conv_w=m_conv_w, conv_b=m_conv_b, w_ffn_out=m_w_ffn_out)
    vel = dict(ln_mix_g=v_ln_mix_g, ln_mix_b=v_ln_mix_b, w_in=v_w_in, b_in=v_b_in, attn_sinks=v_attn_sinks,
               w_proj_a=v_w_proj_a, w_proj_b=v_w_proj_b, w_out=v_w_out, ln_ffn_g=v_ln_ffn_g, ln_ffn_b=v_ln_ffn_b,
               w_ffn_in=v_w_ffn_in, conv_w=v_conv_w, conv_b=v_conv_b, w_ffn_out=v_w_ffn_out)
    names = list(wts)
    big_names = [n for n, _, _ in BIG]
    small_names = [n for n, _ in SMALL]
    me = 4 * lax.axis_index("x") + 2 * lax.axis_index("y") + lax.axis_index("c")
    cw_shard = D_FF // N_DEV

    stored = {"w_in": ((2, 0, 1), (1, 2, 0)), "w_ffn_in": ((0, 2, 1), (0, 2, 1))}
    as_stored = lambda tree: {n: jnp.transpose(tree[n], stored[n][0]) if n in stored else tree[n] for n in big_names}
    w_st, m_st, v_st = as_stored(wts), as_stored(mom), as_stored(vel)
    wb = {n: [(w_st[n][:, l] if n == "w_in" else w_st[n][l]).astype(BF16) for l in range(DEPTH)] for n in big_names}
    ps = [_layer_params(dict(b_in=b_in[l], attn_sinks=attn_sinks[l], ln_mix_g=ln_mix_g[l], ln_mix_b=ln_mix_b[l],
                             ln_ffn_g=ln_ffn_g[l], ln_ffn_b=ln_ffn_b[l], conv_b=conv_b[l])) for l in range(DEPTH)]
    w_in_0 = _gather_two_level(wb["w_in"][0], "gather_w_in_0")
    others = big_names[1:]
    next_layer = {}

    def late_0(w, p, arrived):
        conv_full = arrived[-1].transpose(1, 2, 0, 3).reshape(DEPTH, 3, D_FF)
        next_layer["w"] = _w_in_layouts(arrived[len(others)])
        next_layer["p"] = dict(ps[1], conv_w=_conv_w_layout(conv_full[1]))
        return dict(w, **_other_layouts(*arrived[:len(others)])), dict(p, conv_w=_conv_w_layout(conv_full[0]))

    def late_1(w, p, arrived):
        return dict(w, **_other_layouts(*arrived)), p

    saved, ws = [None] * DEPTH, [None] * DEPTH
    gather_0 = _Comm([(wb[n][0], True) for n in others] + [(wb["w_in"][1], True), (conv_w, True)])
    (h,), saved[0], ws[0], ps[0] = _layer_fwd(x[0], _w_in_layouts(w_in_0), ps[0], gather_0, late_0)
    gather_1 = _Comm([(wb[n][1], True) for n in others])
    (dh, loss_part), saved[1], ws[1], ps[1] = _layer_fwd(h, next_layer["w"], next_layer["p"], gather_1, late_1,
                                                         loss_target[0])

    def small_rows(small):
        vec = jnp.concatenate([small[n].reshape(-1) for n in small_names] + [loss_part[0, 0].reshape(1)])
        return _to_rows(vec, SMALL_LAYER_ROWS)

    dh, big_1, small_1, _, _ = _layer_bwd(dh, saved[1], ws[1], ps[1])

    def exchange_early(big_0):
        return _Comm([(big_1[n].astype(BF16), False) for n in big_names] + [(small_rows(small_1), True)]
                     + [(big_0[n].astype(BF16), False) for n in others])

    def exchange_last(big_0, small_0):
        return _Comm([(big_0["w_in"].astype(BF16), False), (small_rows(small_0), True)])

    grad_x, _, _, arrived, (g_in_0, g_small_0) = _layer_bwd(dh, saved[0], ws[0], ps[0], exchange_early, exchange_last)
    n_big = len(big_names)
    recv = [[g_in_0] + list(arrived[n_big + 1:]) + [g_small_0], list(arrived[:n_big + 1])]

    big_out = {"w_in": _adamw_w_in([recv[l][0] for l in range(DEPTH)], w_st["w_in"], m_st["w_in"], v_st["w_in"])}
    for t, n in enumerate(big_names):
        if n == "w_in":
            continue
        big_out[n] = _adamw_shard([recv[l][t] for l in range(DEPTH)], w_st[n], m_st[n], v_st[n], "adamw_%s" % n)
    for n, (_, back) in stored.items():
        big_out[n] = [jnp.transpose(a, back) for a in big_out[n]]
    small_sum = [_sum8(recv[l][-1]).reshape(-1) for l in range(DEPTH)]
    g_small = {}
    off = 0
    for n, size in SMALL:
        g_small[n] = jnp.stack([small_sum[l][off:off + size] for l in range(DEPTH)])
        off += size
    loss = small_sum[0][off]
    g_small["conv_w"] = lax.dynamic_slice_in_dim(g_small["conv_w"].reshape(DEPTH, 3, D_FF), me * cw_shard, cw_shard,
                                                 axis=2)
    g_small = {n: g_small[n].reshape(wts[n].shape) for n in small_names}

    def pack_small(tree):
        return _to_rows(jnp.concatenate([tree[n].reshape(-1) for n in small_names]), SMALL_ROWS)

    small_out = (pack_small(g_small),) + tuple(_adamw_rows(pack_small(wts), pack_small(g_small), pack_small(mom),
                                                           pack_small(vel), "adamw_small"))

    def result(j):
        out = {n: big_out[n][j] for n in big_names}
        flat = small_out[j].reshape(-1)
        off = 0
        for n in small_names:
            out[n] = flat[off:off + wts[n].size].reshape(wts[n].shape)
            off += wts[n].size
        return [out[n] for n in names]

    return (loss, grad_x[None], *result(0), *result(1), *result(2), *result(3))
```
